```python
import jax, jax.numpy as jnp
from jax import lax
import numpy as np

D_MODEL = 1024
BATCH = 8
SEQ = 2048
DEPTH = 1

D_FF = 2816
POOL_WIDTH = D_MODEL // 2
POOL_WINDOWS = (2, 4, 8, 16)
N_POOL_GROUPS = len(POOL_WINDOWS)
POOL_GROUP = POOL_WIDTH // N_POOL_GROUPS
N_SB_HEADS = 8
SB_HEAD_DIM = 64
SB_WIDTH = N_SB_HEADS * SB_HEAD_DIM
Q_BLOCK = 128
IN_WIDTH = POOL_WIDTH + 3 * SB_WIDTH + 2 * D_MODEL
RMS_EPS = 1e-6

kernel_name = "macaron_pool_stickbreaking_gated_hybrid"


def rmsnorm(x, g):
    xf = x.astype(jnp.float32)
    r = lax.rsqrt(jnp.mean(xf * xf, axis=-1, keepdims=True) + RMS_EPS)
    return (xf * r * g.astype(jnp.float32)).astype(x.dtype)


def swiglu(x, w_gate_up, w_down):
    gu = x @ w_gate_up
    g, u = jnp.split(gu, 2, axis=-1)
    return (jax.nn.silu(g) * u) @ w_down


def causal_pool_mixer(xp, w_group, pool_scale):
    b, s, _ = xp.shape
    xg = xp.reshape(b, s, N_POOL_GROUPS, POOL_GROUP)
    pos = jnp.arange(s, dtype=jnp.int32)
    outs = []
    for gi, w in enumerate(POOL_WINDOWS):
        xi = xg[:, :, gi, :].astype(jnp.float32)
        cs = jnp.cumsum(xi, axis=1)
        lower = jnp.concatenate([jnp.zeros((b, w, POOL_GROUP), cs.dtype), cs[:, : s - w]], axis=1)
        count = jnp.minimum(pos + 1, w).astype(jnp.float32)[None, :, None]
        mean = (cs - lower) / count
        outs.append((mean - xi).astype(xp.dtype))
    y = jnp.stack(outs, axis=2)
    y = jnp.einsum('bsgc,gcd->bsgd', y, w_group)
    return y.reshape(b, s, POOL_WIDTH) * pool_scale


def stick_breaking_attention(q, k, v):
    s_len = q.shape[2]
    scale = 1.0 / np.sqrt(SB_HEAD_DIM)
    outs = []
    for blk in range(s_len // Q_BLOCK):
        i0, i1 = blk * Q_BLOCK, (blk + 1) * Q_BLOCK
        qb = q[:, :, i0:i1]
        kb = k[:, :, :i1]
        vb = v[:, :, :i1]
        z = jnp.einsum('bhqd,bhkd->bhqk', qb, kb).astype(jnp.float32) * scale
        qpos = jnp.arange(i0, i1)[:, None]
        kpos = jnp.arange(i1)[None, :]
        mask = kpos < qpos
        log_beta = jax.nn.log_sigmoid(z)
        log_1m_beta = jnp.where(mask, jax.nn.log_sigmoid(-z), 0.0)
        log_a = log_beta + lax.cumsum(log_1m_beta, axis=3, reverse=True) - log_1m_beta
        a = jnp.where(mask, jnp.exp(log_a), 0.0)
        outs.append(jnp.einsum('bhqk,bhkd->bhqd', a.astype(vb.dtype), vb))
    return jnp.concatenate(outs, axis=2)


def gated_mixer_block(u, w_in, pool_w_group, pool_scale, w_branch_pool, w_branch_attn, w_out):
    b, s, _ = u.shape
    proj = u @ w_in
    o1 = POOL_WIDTH
    o2 = o1 + SB_WIDTH
    o3 = o2 + SB_WIDTH
    o4 = o3 + SB_WIDTH
    xp = proj[..., :o1]
    q = proj[..., o1:o2].reshape(b, s, N_SB_HEADS, SB_HEAD_DIM).transpose(0, 2, 1, 3)
    k = proj[..., o2:o3].reshape(b, s, N_SB_HEADS, SB_HEAD_DIM).transpose(0, 2, 1, 3)
    v = proj[..., o3:o4].reshape(b, s, N_SB_HEADS, SB_HEAD_DIM).transpose(0, 2, 1, 3)
    gate_logits = proj[..., o4:]
    y_pool = causal_pool_mixer(xp, pool_w_group, pool_scale) @ w_branch_pool
    o_sb = stick_breaking_attention(q, k, v).transpose(0, 2, 1, 3).reshape(b, s, SB_WIDTH)
    y_sb = o_sb @ w_branch_attn
    g = jax.nn.sigmoid(gate_logits.astype(jnp.float32)).astype(u.dtype)
    g_pool, g_sb = jnp.split(g, 2, axis=-1)
    return (g_pool * y_pool + g_sb * y_sb) @ w_out


def _fwd_setup_inputs(seed: int = 0) -> dict:
    key = jax.random.key(seed)
    ks = jax.random.split(key, 20)
    f32 = jnp.float32

    def w(k, shape, fan_in):
        return jax.random.normal(k, shape, f32) * (fan_in ** -0.5)

    def gain(k, shape):
        return 1.0 + 0.05 * jax.random.normal(k, shape, f32)

    L = DEPTH
    return {
        "x": jax.random.normal(ks[0], (BATCH, SEQ, D_MODEL), f32),
        "ffn1_norm": gain(ks[1], (L, D_MODEL)),
        "ffn1_w_gate_up": w(ks[2], (L, D_MODEL, 2 * D_FF), D_MODEL),
        "ffn1_w_down": w(ks[3], (L, D_FF, D_MODEL), D_FF),
        "mix_norm": gain(ks[4], (L, D_MODEL)),
        "w_in": w(ks[5], (L, D_MODEL, IN_WIDTH), D_MODEL),
        "pool_w_group": w(ks[6], (L, N_POOL_GROUPS, POOL_GROUP, POOL_GROUP), POOL_GROUP),
        "pool_scale": gain(ks[7], (L, POOL_WIDTH)),
        "w_branch_pool": w(ks[8], (L, POOL_WIDTH, D_MODEL), POOL_WIDTH),
        "w_branch_attn": w(ks[9], (L, SB_WIDTH, D_MODEL), SB_WIDTH),
        "w_out": w(ks[10], (L, D_MODEL, D_MODEL), D_MODEL),
        "ffn2_norm": gain(ks[11], (L, D_MODEL)),
        "ffn2_w_gate_up": w(ks[12], (L, D_MODEL, 2 * D_FF), D_MODEL),
        "ffn2_w_down": w(ks[13], (L, D_FF, D_MODEL), D_FF),
        "final_norm": gain(ks[14], (D_MODEL,)),
    }


def _fwd_reference(x, ffn1_norm, ffn1_w_gate_up, ffn1_w_down, mix_norm, w_in, pool_w_group,
              pool_scale, w_branch_pool, w_branch_attn, w_out, ffn2_norm, ffn2_w_gate_up,
              ffn2_w_down, final_norm):
    h = x
    for l in range(DEPTH):
        h = h + 0.5 * swiglu(rmsnorm(h, ffn1_norm[l]), ffn1_w_gate_up[l], ffn1_w_down[l])
        u = rmsnorm(h, mix_norm[l])
        h = h + gated_mixer_block(u, w_in[l], pool_w_group[l], pool_scale[l],
                                  w_branch_pool[l], w_branch_attn[l], w_out[l])
        h = h + 0.5 * swiglu(rmsnorm(h, ffn2_norm[l]), ffn2_w_gate_up[l], ffn2_w_down[l])
    return rmsnorm(h, final_norm)


import jax as _jax
import jax.numpy as _jnp

TWIN_FORMAT = 'train_step'
FWD_PARAMS = ['x', 'ffn1_norm', 'ffn1_w_gate_up', 'ffn1_w_down', 'mix_norm', 'w_in', 'pool_w_group', 'pool_scale', 'w_branch_pool', 'w_branch_attn', 'w_out', 'ffn2_norm', 'ffn2_w_gate_up', 'ffn2_w_down', 'final_norm']
TWIN_WEIGHTS = ['ffn1_norm', 'ffn1_w_gate_up', 'ffn1_w_down', 'mix_norm', 'w_in', 'pool_w_group', 'pool_scale', 'w_branch_pool', 'w_branch_attn', 'w_out', 'ffn2_norm', 'ffn2_w_gate_up', 'ffn2_w_down', 'final_norm']
TWIN_DIFF_INPUT = 'x'
TWIN_INPUTS = ['x', 'ffn1_norm', 'ffn1_w_gate_up', 'ffn1_w_down', 'mix_norm', 'w_in', 'pool_w_group', 'pool_scale', 'w_branch_pool', 'w_branch_attn', 'w_out', 'ffn2_norm', 'ffn2_w_gate_up', 'ffn2_w_down', 'final_norm', 'loss_target', 'm_ffn1_norm', 'm_ffn1_w_gate_up', 'm_ffn1_w_down', 'm_mix_norm', 'm_w_in', 'm_pool_w_group', 'm_pool_scale', 'm_w_branch_pool', 'm_w_branch_attn', 'm_w_out', 'm_ffn2_norm', 'm_ffn2_w_gate_up', 'm_ffn2_w_down', 'm_final_norm', 'v_ffn1_norm', 'v_ffn1_w_gate_up', 'v_ffn1_w_down', 'v_mix_norm', 'v_w_in', 'v_pool_w_group', 'v_pool_scale', 'v_w_branch_pool', 'v_w_branch_attn', 'v_w_out', 'v_ffn2_norm', 'v_ffn2_w_gate_up', 'v_ffn2_w_down', 'v_final_norm']
TWIN_OUTPUTS = ['loss', 'grad_x', 'grad_ffn1_norm', 'grad_ffn1_w_gate_up', 'grad_ffn1_w_down', 'grad_mix_norm', 'grad_w_in', 'grad_pool_w_group', 'grad_pool_scale', 'grad_w_branch_pool', 'grad_w_branch_attn', 'grad_w_out', 'grad_ffn2_norm', 'grad_ffn2_w_gate_up', 'grad_ffn2_w_down', 'grad_final_norm', 'delta_ffn1_norm', 'delta_ffn1_w_gate_up', 'delta_ffn1_w_down', 'delta_mix_norm', 'delta_w_in', 'delta_pool_w_group', 'delta_pool_scale', 'delta_w_branch_pool', 'delta_w_branch_attn', 'delta_w_out', 'delta_ffn2_norm', 'delta_ffn2_w_gate_up', 'delta_ffn2_w_down', 'delta_final_norm', 'new_m_ffn1_norm', 'new_m_ffn1_w_gate_up', 'new_m_ffn1_w_down', 'new_m_mix_norm', 'new_m_w_in', 'new_m_pool_w_group', 'new_m_pool_scale', 'new_m_w_branch_pool', 'new_m_w_branch_attn', 'new_m_w_out', 'new_m_ffn2_norm', 'new_m_ffn2_w_gate_up', 'new_m_ffn2_w_down', 'new_m_final_norm', 'new_v_ffn1_norm', 'new_v_ffn1_w_gate_up', 'new_v_ffn1_w_down', 'new_v_mix_norm', 'new_v_w_in', 'new_v_pool_w_group', 'new_v_pool_scale', 'new_v_w_branch_pool', 'new_v_w_branch_attn', 'new_v_w_out', 'new_v_ffn2_norm', 'new_v_ffn2_w_gate_up', 'new_v_ffn2_w_down', 'new_v_final_norm']
TWIN_LEAF_KINDS = {'loss': 'loss', 'grad_x': 'grad_x', 'grad_ffn1_norm': 'grad_w', 'grad_ffn1_w_gate_up': 'grad_w', 'grad_ffn1_w_down': 'grad_w', 'grad_mix_norm': 'grad_w', 'grad_w_in': 'grad_w', 'grad_pool_w_group': 'grad_w', 'grad_pool_scale': 'grad_w', 'grad_w_branch_pool': 'grad_w', 'grad_w_branch_attn': 'grad_w', 'grad_w_out': 'grad_w', 'grad_ffn2_norm': 'grad_w', 'grad_ffn2_w_gate_up': 'grad_w', 'grad_ffn2_w_down': 'grad_w', 'grad_final_norm': 'grad_w', 'delta_ffn1_norm': 'delta_w', 'delta_ffn1_w_gate_up': 'delta_w', 'delta_ffn1_w_down': 'delta_w', 'delta_mix_norm': 'delta_w', 'delta_w_in': 'delta_w', 'delta_pool_w_group': 'delta_w', 'delta_pool_scale': 'delta_w', 'delta_w_branch_pool': 'delta_w', 'delta_w_branch_attn': 'delta_w', 'delta_w_out': 'delta_w', 'delta_ffn2_norm': 'delta_w', 'delta_ffn2_w_gate_up': 'delta_w', 'delta_ffn2_w_down': 'delta_w', 'delta_final_norm': 'delta_w', 'new_m_ffn1_norm': 'new_m', 'new_m_ffn1_w_gate_up': 'new_m', 'new_m_ffn1_w_down': 'new_m', 'new_m_mix_norm': 'new_m', 'new_m_w_in': 'new_m', 'new_m_pool_w_group': 'new_m', 'new_m_pool_scale': 'new_m', 'new_m_w_branch_pool': 'new_m', 'new_m_w_branch_attn': 'new_m', 'new_m_w_out': 'new_m', 'new_m_ffn2_norm': 'new_m', 'new_m_ffn2_w_gate_up': 'new_m', 'new_m_ffn2_w_down': 'new_m', 'new_m_final_norm': 'new_m', 'new_v_ffn1_norm': 'new_v', 'new_v_ffn1_w_gate_up': 'new_v', 'new_v_ffn1_w_down': 'new_v', 'new_v_mix_norm': 'new_v', 'new_v_w_in': 'new_v', 'new_v_pool_w_group': 'new_v', 'new_v_pool_scale': 'new_v', 'new_v_w_branch_pool': 'new_v', 'new_v_w_branch_attn': 'new_v', 'new_v_w_out': 'new_v', 'new_v_ffn2_norm': 'new_v', 'new_v_ffn2_w_gate_up': 'new_v', 'new_v_ffn2_w_down': 'new_v', 'new_v_final_norm': 'new_v'}


def _forward(args):
    return _fwd_reference(*[args[k] for k in FWD_PARAMS])


def _output_shape():
    out = _jax.eval_shape(lambda: _forward(_fwd_setup_inputs(0)))
    return out.shape, out.dtype

N_MICROBATCH = 1
ADAM_LR = 0.001
ADAM_B1 = 0.9
ADAM_B2 = 0.999
ADAM_EPS = 1e-08
ADAM_WD = 0.01
ADAM_STEP = 10
PER_EXAMPLE_BATCH_AXIS = {'x': 0, 'loss_target': 0}
SHARED_INPUTS = []
_WEIGHT_DTYPES = {'ffn1_norm': _jnp.float32, 'ffn1_w_gate_up': _jnp.float32, 'ffn1_w_down': _jnp.float32, 'mix_norm': _jnp.float32, 'w_in': _jnp.float32, 'pool_w_group': _jnp.float32, 'pool_scale': _jnp.float32, 'w_branch_pool': _jnp.float32, 'w_branch_attn': _jnp.float32, 'w_out': _jnp.float32, 'ffn2_norm': _jnp.float32, 'ffn2_w_gate_up': _jnp.float32, 'ffn2_w_down': _jnp.float32, 'final_norm': _jnp.float32}
MOMENT_SCALE = {'ffn1_norm': 5.793593e-02, 'ffn1_w_gate_up': 2.359454e-02, 'ffn1_w_down': 3.847654e-02, 'mix_norm': 7.359712e-02, 'w_in': 3.682282e-02, 'pool_w_group': 7.536098e-02, 'pool_scale': 7.597528e-02, 'w_branch_pool': 5.263804e-02, 'w_branch_attn': 3.808030e-02, 'w_out': 6.529205e-02, 'ffn2_norm': 4.397347e-02, 'ffn2_w_gate_up': 1.876771e-02, 'ffn2_w_down': 3.065990e-02, 'final_norm': 1.603285e+01}


def _to_microbatches(a, axis):
    t = _jnp.moveaxis(a, axis, 0)
    t = t.reshape((N_MICROBATCH, t.shape[0] // N_MICROBATCH) + t.shape[1:])
    return _jnp.moveaxis(t, 1, axis + 1)


def setup_inputs(seed: int = 0) -> dict:
    inp = _fwd_setup_inputs(seed)
    key = _jax.random.fold_in(_jax.random.key(seed), 7919)
    shape, _ = _output_shape()
    out = dict(inp)
    out["loss_target"] = _jax.random.normal(_jax.random.fold_in(key, 0), shape, _jnp.float32)
    for i, name in enumerate(TWIN_WEIGHTS):
        w = inp[name].astype(_jnp.float32)
        if MOMENT_SCALE is None:
            s = _jnp.sqrt(_jnp.mean(_jnp.square(w)) + 1e-30)
        else:
            s = MOMENT_SCALE[name]
        km, kv = _jax.random.split(_jax.random.fold_in(key, i + 1))
        out[name] = w
        out["m_" + name] = s * _jax.random.normal(km, w.shape, _jnp.float32)
        out["v_" + name] = (s * s) * _jax.random.uniform(kv, w.shape, _jnp.float32, 0.5, 1.5)
    if N_MICROBATCH > 1:
        for name, axis in PER_EXAMPLE_BATCH_AXIS.items():
            out[name] = _to_microbatches(out[name], axis)
    return {'x': out['x'], 'ffn1_norm': out['ffn1_norm'], 'ffn1_w_gate_up': out['ffn1_w_gate_up'], 'ffn1_w_down': out['ffn1_w_down'], 'mix_norm': out['mix_norm'], 'w_in': out['w_in'], 'pool_w_group': out['pool_w_group'], 'pool_scale': out['pool_scale'], 'w_branch_pool': out['w_branch_pool'], 'w_branch_attn': out['w_branch_attn'], 'w_out': out['w_out'], 'ffn2_norm': out['ffn2_norm'], 'ffn2_w_gate_up': out['ffn2_w_gate_up'], 'ffn2_w_down': out['ffn2_w_down'], 'final_norm': out['final_norm'], 'loss_target': out['loss_target'], 'm_ffn1_norm': out['m_ffn1_norm'], 'm_ffn1_w_gate_up': out['m_ffn1_w_gate_up'], 'm_ffn1_w_down': out['m_ffn1_w_down'], 'm_mix_norm': out['m_mix_norm'], 'm_w_in': out['m_w_in'], 'm_pool_w_group': out['m_pool_w_group'], 'm_pool_scale': out['m_pool_scale'], 'm_w_branch_pool': out['m_w_branch_pool'], 'm_w_branch_attn': out['m_w_branch_attn'], 'm_w_out': out['m_w_out'], 'm_ffn2_norm': out['m_ffn2_norm'], 'm_ffn2_w_gate_up': out['m_ffn2_w_gate_up'], 'm_ffn2_w_down': out['m_ffn2_w_down'], 'm_final_norm': out['m_final_norm'], 'v_ffn1_norm': out['v_ffn1_norm'], 'v_ffn1_w_gate_up': out['v_ffn1_w_gate_up'], 'v_ffn1_w_down': out['v_ffn1_w_down'], 'v_mix_norm': out['v_mix_norm'], 'v_w_in': out['v_w_in'], 'v_pool_w_group': out['v_pool_w_group'], 'v_pool_scale': out['v_pool_scale'], 'v_w_branch_pool': out['v_w_branch_pool'], 'v_w_branch_attn': out['v_w_branch_attn'], 'v_w_out': out['v_w_out'], 'v_ffn2_norm': out['v_ffn2_norm'], 'v_ffn2_w_gate_up': out['v_ffn2_w_gate_up'], 'v_ffn2_w_down': out['v_ffn2_w_down'], 'v_final_norm': out['v_final_norm']}


def _loss(weights, diff, rest, loss_target):
    with _jax.named_scope("forward"):
        args = {**rest, TWIN_DIFF_INPUT: diff, **{k: w.astype(_WEIGHT_DTYPES[k]) for k, w in weights.items()}}
        y = _forward(args)
    with _jax.named_scope("loss_head"):
        err = _jnp.square(y.astype(_jnp.float32) - loss_target)
        return 0.5 * _jnp.sum(_jnp.mean(err, axis=-1)) if err.ndim else 0.5 * err


def _adamw(w, g, m, v):
    m = ADAM_B1 * m + (1.0 - ADAM_B1) * g
    v = ADAM_B2 * v + (1.0 - ADAM_B2) * _jnp.square(g)
    m_hat = m / (1.0 - ADAM_B1 ** ADAM_STEP)
    v_hat = v / (1.0 - ADAM_B2 ** ADAM_STEP)
    delta = -ADAM_LR * (m_hat / (_jnp.sqrt(v_hat) + ADAM_EPS) + ADAM_WD * w)
    return delta, m, v


def reference(x, ffn1_norm, ffn1_w_gate_up, ffn1_w_down, mix_norm, w_in, pool_w_group, pool_scale, w_branch_pool, w_branch_attn, w_out, ffn2_norm, ffn2_w_gate_up, ffn2_w_down, final_norm, loss_target, m_ffn1_norm, m_ffn1_w_gate_up, m_ffn1_w_down, m_mix_norm, m_w_in, m_pool_w_group, m_pool_scale, m_w_branch_pool, m_w_branch_attn, m_w_out, m_ffn2_norm, m_ffn2_w_gate_up, m_ffn2_w_down, m_final_norm, v_ffn1_norm, v_ffn1_w_gate_up, v_ffn1_w_down, v_mix_norm, v_w_in, v_pool_w_group, v_pool_scale, v_w_branch_pool, v_w_branch_attn, v_w_out, v_ffn2_norm, v_ffn2_w_gate_up, v_ffn2_w_down, v_final_norm):
    given = dict(x=x, ffn1_norm=ffn1_norm, ffn1_w_gate_up=ffn1_w_gate_up, ffn1_w_down=ffn1_w_down, mix_norm=mix_norm, w_in=w_in, pool_w_group=pool_w_group, pool_scale=pool_scale, w_branch_pool=w_branch_pool, w_branch_attn=w_branch_attn, w_out=w_out, ffn2_norm=ffn2_norm, ffn2_w_gate_up=ffn2_w_gate_up, ffn2_w_down=ffn2_w_down, final_norm=final_norm, loss_target=loss_target, m_ffn1_norm=m_ffn1_norm, m_ffn1_w_gate_up=m_ffn1_w_gate_up, m_ffn1_w_down=m_ffn1_w_down, m_mix_norm=m_mix_norm, m_w_in=m_w_in, m_pool_w_group=m_pool_w_group, m_pool_scale=m_pool_scale, m_w_branch_pool=m_w_branch_pool, m_w_branch_attn=m_w_branch_attn, m_w_out=m_w_out, m_ffn2_norm=m_ffn2_norm, m_ffn2_w_gate_up=m_ffn2_w_gate_up, m_ffn2_w_down=m_ffn2_w_down, m_final_norm=m_final_norm, v_ffn1_norm=v_ffn1_norm, v_ffn1_w_gate_up=v_ffn1_w_gate_up, v_ffn1_w_down=v_ffn1_w_down, v_mix_norm=v_mix_norm, v_w_in=v_w_in, v_pool_w_group=v_pool_w_group, v_pool_scale=v_pool_scale, v_w_branch_pool=v_w_branch_pool, v_w_branch_attn=v_w_branch_attn, v_w_out=v_w_out, v_ffn2_norm=v_ffn2_norm, v_ffn2_w_gate_up=v_ffn2_w_gate_up, v_ffn2_w_down=v_ffn2_w_down, v_final_norm=v_final_norm)
    weights = {n: given[n] for n in TWIN_WEIGHTS}
    shared = {n: given[n] for n in SHARED_INPUTS}
    per_example = {n: given[n] for n in ['x']}
    grad_fn = _jax.value_and_grad(_loss, argnums=(0, 1))

    def one_microbatch(ex, loss_target):
        ex = dict(ex)
        diff = ex.pop(TWIN_DIFF_INPUT)
        return grad_fn(weights, diff, {**shared, **ex}, loss_target)

    if N_MICROBATCH == 1:
        loss, (grad_w, grad_x) = one_microbatch(per_example, given["loss_target"])
    else:
        def body(carry, xs):
            loss_sum, grad_sum = carry
            l_k, (gw_k, gx_k) = one_microbatch(xs[0], xs[1])
            with _jax.named_scope("update"):
                return (loss_sum + l_k, _jax.tree.map(_jnp.add, grad_sum, gw_k)), gx_k

        init = (_jnp.zeros((), _jnp.float32), _jax.tree.map(_jnp.zeros_like, weights))
        (loss, grad_w), grad_x = _jax.lax.scan(body, init, (per_example, given["loss_target"]))
    with _jax.named_scope("update"):
        delta_w, new_m, new_v = {}, {}, {}
        for n in TWIN_WEIGHTS:
            delta_w[n], new_m[n], new_v[n] = _adamw(weights[n], grad_w[n], given["m_" + n], given["v_" + n])
    return (loss, grad_x, *[grad_w[n] for n in TWIN_WEIGHTS], *[delta_w[n] for n in TWIN_WEIGHTS],
            *[new_m[n] for n in TWIN_WEIGHTS], *[new_v[n] for n in TWIN_WEIGHTS])
```

```python
import functools

import jax
import jax.numpy as jnp
from jax import lax
from jax.experimental import pallas as pl
from jax.experimental.pallas import tpu as pltpu

F32 = jnp.float32
BF16 = jnp.bfloat16

S = 2048
D = 1024
DFF = 2816
FFS = 2 * DFF // 4
NSH = 4
PW = 512
PG = 128
POOL_WINDOWS = (2, 4, 8, 16)
HALO = 16
SBW = 512
DH = 64
EPS = 1e-6
SCALE = 0.125
TA = 256
MIB = 1024 * 1024

LR, B1, B2, AEPS, WD, STEP = 0.001, 0.9, 0.999, 1e-08, 0.01, 10

_VM = pl.BlockSpec(memory_space=pltpu.VMEM)
_ANY = pl.BlockSpec(memory_space=pl.ANY)
MESH = pl.DeviceIdType.MESH


def _nn(a, b):
    return jnp.dot(a, b, preferred_element_type=F32)


def _nt(a, b):
    return lax.dot_general(a, b, (((1,), (1,)), ((), ())), preferred_element_type=F32)


def _tn(a, b):
    return lax.dot_general(a, b, (((0,), (0,)), ((), ())), preferred_element_type=F32)


def _params(sem, vmem_mib):
    return pltpu.CompilerParams(dimension_semantics=sem, vmem_limit_bytes=vmem_mib * MIB)


def _rows(tm, width):
    return pl.BlockSpec((tm, width), lambda i: (i, 0))


def _fixed(shape):
    return pl.BlockSpec(shape, lambda *_: (0,) * len(shape))


def _sds(shape, dtype):
    return jax.ShapeDtypeStruct(shape, dtype)


def _stage(pairs):
    @pl.when(pl.program_id(0) == 0)
    def _():
        for src, dst in pairs:
            pltpu.sync_copy(src, dst)


def _vmem_like(*arrays):
    return [pltpu.VMEM(a.shape, a.dtype) for a in arrays]


def _rms(x):
    r = lax.rsqrt(jnp.mean(x * x, axis=-1, keepdims=True) + EPS)
    return r, x * r


def _rms_bwd(dn, xr, r, gain):
    dng = dn * gain
    dx = r * (dng - xr * jnp.mean(dng * xr, axis=-1, keepdims=True))
    return dx, jnp.sum(dn * xr, axis=0, keepdims=True)


def _ffn_fwd(x, gain, wgu, wd, name):
    tm = 256

    def body(x_ref, g_ref, wgu_hbm, wd_hbm, h_ref, n_ref, gu_ref, wgu_ref, wd_ref):
        _stage([(wgu_hbm, wgu_ref), (wd_hbm, wd_ref)])
        x = x_ref[...]
        _, xr = _rms(x)
        n = (xr * g_ref[...]).astype(BF16)
        n_ref[...] = n
        acc = jnp.zeros((tm, D), F32)
        for j in range(2):
            g = _nn(n, wgu_ref[j])
            u = _nn(n, wgu_ref[2 + j])
            gu_ref[:, j * FFS:(j + 1) * FFS] = g.astype(BF16)
            gu_ref[:, (2 + j) * FFS:(3 + j) * FFS] = u.astype(BF16)
            a = (g * jax.nn.sigmoid(g) * u).astype(BF16)
            acc = acc + _nn(a, wd_ref[j * FFS:(j + 1) * FFS, :])
        h_ref[...] = x + 0.5 * acc

    return pl.pallas_call(
        body, name=name, grid=(S // tm,),
        in_specs=[_rows(tm, D), _fixed((1, D)), _ANY, _ANY],
        out_specs=[_rows(tm, D), _rows(tm, D), _rows(tm, 4 * FFS)],
        out_shape=[_sds((S, D), F32), _sds((S, D), BF16), _sds((S, 4 * FFS), BF16)],
        scratch_shapes=_vmem_like(wgu, wd),
        compiler_params=_params(("arbitrary",), 56),
    )(x, gain, wgu, wd)


def _ffn_bwd(dh, x, gain, gu, wgu, wd, name):
    tm = 256

    def body(dh_ref, x_ref, g_ref, gu_ref, wgu_hbm, wd_hbm, dx_ref, dgu_ref, a_ref, dg_ref, wgu_ref, wd_ref):
        _stage([(wgu_hbm, wgu_ref), (wd_hbm, wd_ref)])
        dh = dh_ref[...]
        dhb = dh.astype(BF16)
        dn = jnp.zeros((tm, D), F32)
        for j in range(2):
            g = gu_ref[:, j * FFS:(j + 1) * FFS].astype(F32)
            u = gu_ref[:, (2 + j) * FFS:(3 + j) * FFS].astype(F32)
            da = 0.5 * _nt(dhb, wd_ref[j * FFS:(j + 1) * FFS, :])
            sg = jax.nn.sigmoid(g)
            si = g * sg
            a_ref[:, j * FFS:(j + 1) * FFS] = (0.5 * si * u).astype(BF16)
            dgb = (da * u * (sg * (1.0 + g * (1.0 - sg)))).astype(BF16)
            dub = (da * si).astype(BF16)
            dgu_ref[:, j * FFS:(j + 1) * FFS] = dgb
            dgu_ref[:, (2 + j) * FFS:(3 + j) * FFS] = dub
            dn = dn + _nt(dgb, wgu_ref[j]) + _nt(dub, wgu_ref[2 + j])
        r, xr = _rms(x_ref[...])
        dx, dgain = _rms_bwd(dn, xr, r, g_ref[...])
        dx_ref[...] = dh + dx

        @pl.when(pl.program_id(0) == 0)
        def _():
            dg_ref[...] = jnp.zeros_like(dg_ref)

        dg_ref[...] += dgain

    return pl.pallas_call(
        body, name=name, grid=(S // tm,),
        in_specs=[_rows(tm, D), _rows(tm, D), _fixed((1, D)), _rows(tm, 4 * FFS), _ANY, _ANY],
        out_specs=[_rows(tm, D), _rows(tm, 4 * FFS), _rows(tm, DFF), _fixed((1, D))],
        out_shape=[_sds((S, D), F32), _sds((S, 4 * FFS), BF16), _sds((S, DFF), BF16), _sds((1, D), F32)],
        scratch_shapes=_vmem_like(wgu, wd),
        compiler_params=_params(("arbitrary",), 56),
    )(dh, x, gain, gu, wgu, wd)


def _head(h, target, gain):
    tm = 512

    def body(h_ref, t_ref, g_ref, dh_ref, loss_ref, dg_ref):
        gain = g_ref[...]
        r, hr = _rms(h_ref[...])
        err = hr * gain - t_ref[...]
        dy = err * (1.0 / D)
        dh, dgain = _rms_bwd(dy, hr, r, gain)
        dh_ref[...] = dh

        @pl.when(pl.program_id(0) == 0)
        def _():
            dg_ref[...] = jnp.zeros_like(dg_ref)
            loss_ref[...] = jnp.zeros_like(loss_ref)

        dg_ref[...] += dgain
        loss_ref[...] += jnp.full((1, 128), (0.5 / D) * jnp.sum(err * err), F32)

    return pl.pallas_call(
        body, name="head", grid=(S // tm,),
        in_specs=[_rows(tm, D), _rows(tm, D), _fixed((1, D))],
        out_specs=[_rows(tm, D), _fixed((1, 128)), _fixed((1, D))],
        out_shape=[_sds((S, D), F32), _sds((1, 128), F32), _sds((1, D), F32)],
        compiler_params=_params(("arbitrary",), 40),
    )(h, target, gain)


def _mix_in(h, gain, w_in):
    tm = 512

    def body(h_ref, g_ref, w_hbm, u_ref, xp_ref, q_ref, k_ref, v_ref, gp_ref, gs_ref, w_ref):
        _stage([(w_hbm, w_ref)])
        _, hr = _rms(h_ref[...])
        u = (hr * g_ref[...]).astype(BF16)
        u_ref[...] = u
        p0 = _nn(u, w_ref[0])
        xp_ref[...] = p0[:, :PW]
        q_ref[...] = p0[:, PW:].astype(BF16)
        p1 = _nn(u, w_ref[1])
        k_ref[...] = p1[:, :SBW].astype(BF16)
        v_ref[...] = p1[:, SBW:].astype(BF16)
        gp_ref[...] = jax.nn.sigmoid(_nn(u, w_ref[2])).astype(BF16)
        gs_ref[...] = jax.nn.sigmoid(_nn(u, w_ref[3])).astype(BF16)

    return pl.pallas_call(
        body, name="mix_in", grid=(S // tm,),
        in_specs=[_rows(tm, D), _fixed((1, D)), _ANY],
        out_specs=[_rows(tm, D), _rows(tm, PW), _rows(tm, SBW), _rows(tm, SBW), _rows(tm, SBW),
                   _rows(tm, D), _rows(tm, D)],
        out_shape=[_sds((S, D), BF16), _sds((S, PW), F32), _sds((S, SBW), BF16), _sds((S, SBW), BF16),
                   _sds((S, SBW), BF16), _sds((S, D), BF16), _sds((S, D), BF16)],
        scratch_shapes=_vmem_like(w_in),
        compiler_params=_params(("arbitrary",), 48),
    )(h, gain, w_in)


def _hilo_dot(x, tri):
    hi = x.astype(BF16)
    lo = (x - hi.astype(F32)).astype(BF16)
    return _nn(hi, tri) + _nn(lo, tri)


def _log_terms(qm, kj):
    z = _nt(qm, kj) * SCALE
    e = jnp.exp(-jnp.abs(z))
    lb = jnp.minimum(z, 0.0) - jnp.log(1.0 + e)
    return z, e, lb, lb - z


def _head_masks():
    lane = lax.broadcasted_iota(jnp.int32, (1, 2 * DH), 1)
    return (lane < DH, lane >= DH)


def _attn_fwd(q, k, v):
    T = TA

    def body(q_ref, k_ref, v_ref, o_ref, c_ref):
        i = pl.program_id(1)
        row = lax.broadcasted_iota(jnp.int32, (T, T), 0)
        col = lax.broadcasted_iota(jnp.int32, (T, T), 1)
        after = (row > col).astype(BF16)
        causal = col < row
        q2 = q_ref[...]
        o_acc = jnp.zeros((T, 2 * DH), F32)
        c_out = jnp.zeros((T, 2 * DH), F32)
        for hm in _head_masks():
            qm = jnp.where(hm, q2, jnp.zeros_like(q2))

            def block(j, carry, o, diag):
                kj = k_ref[pl.ds(pl.multiple_of(j * T, T), T), :]
                vj = v_ref[pl.ds(pl.multiple_of(j * T, T), T), :]
                _, _, lb, l1m = _log_terms(qm, kj)
                if diag:
                    l1m = jnp.where(causal, l1m, 0.0)
                a = jnp.exp(lb + _hilo_dot(l1m, after) + carry)
                if diag:
                    a = jnp.where(causal, a, 0.0)
                o = o + _nn(a.astype(BF16), jnp.where(hm, vj, jnp.zeros_like(vj)))
                return carry + jnp.sum(l1m, axis=1, keepdims=True), o

            carry, o_acc = block(i, jnp.zeros((T, 1), F32), o_acc, True)
            carry, o_acc = lax.fori_loop(
                0, i, lambda jj, c: block(i - 1 - jj, c[0], c[1], False), (carry, o_acc))
            c_out = c_out + jnp.where(hm, carry, 0.0)
        o_ref[...] = o_acc.astype(BF16)
        c_ref[...] = c_out

    blk = pl.BlockSpec((T, 2 * DH), lambda p, i: (i, p))
    full = pl.BlockSpec((S, 2 * DH), lambda p, i: (0, p))
    return pl.pallas_call(
        body, name="attn_fwd", grid=(SBW // (2 * DH), S // T),
        in_specs=[blk, full, full], out_specs=[blk, blk],
        out_shape=[_sds((S, SBW), BF16), _sds((S, SBW), F32)],
        compiler_params=_params(("arbitrary", "arbitrary"), 40),
    )(q, k, v)


def _attn_bwd(q, k, v, do, ctot):
    T = TA
    nq = S // T

    def body(q_ref, k_ref, v_ref, do_ref, c_ref, dq_ref, dk_ref, dv_ref, dk_acc, dv_acc):
        i = pl.program_id(1)

        @pl.when(i == 0)
        def _():
            dk_acc[...] = jnp.zeros_like(dk_acc)
            dv_acc[...] = jnp.zeros_like(dv_acc)

        row = lax.broadcasted_iota(jnp.int32, (T, T), 0)
        col = lax.broadcasted_iota(jnp.int32, (T, T), 1)
        upto = (row <= col).astype(BF16)
        before = (row < col).astype(BF16)
        causal = col < row
        q2 = q_ref[...]
        do2 = do_ref[...]
        dq = jnp.zeros((T, 2 * DH), F32)
        for h, hm in enumerate(_head_masks()):
            qm = jnp.where(hm, q2, jnp.zeros_like(q2))
            dom = jnp.where(hm, do2, jnp.zeros_like(do2))
            ctot = c_ref[:, h * DH:h * DH + 1]

            def block(j, cl, cp, dq, diag):
                rows = pl.ds(pl.multiple_of(j * T, T), T)
                kj = k_ref[rows, :]
                vj = v_ref[rows, :]
                z, e, lb, l1m = _log_terms(qm, kj)
                if diag:
                    l1m = jnp.where(causal, l1m, 0.0)
                a = jnp.exp(lb + (ctot - cl) - _hilo_dot(l1m, upto))
                if diag:
                    a = jnp.where(causal, a, 0.0)
                dl = _nt(dom, vj) * a
                pex = _hilo_dot(dl, before) + cp
                rinv = 1.0 / (1.0 + e)
                pos = z >= 0.0
                beta = jnp.where(pos, rinv, e * rinv)
                omb = jnp.where(pos, e * rinv, rinv)
                dz = (dl * omb - pex * beta) * SCALE
                if diag:
                    dz = jnp.where(causal, dz, 0.0)
                dzb = dz.astype(BF16)
                dq = dq + _nn(dzb, jnp.where(hm, kj, jnp.zeros_like(kj)))
                dk_acc[rows, :] += _tn(dzb, qm)
                dv_acc[rows, :] += _tn(a.astype(BF16), dom)
                return (cl + jnp.sum(l1m, axis=1, keepdims=True),
                        cp + jnp.sum(dl, axis=1, keepdims=True), dq)

            zero = jnp.zeros((T, 1), F32)
            cl, cp, dq = lax.fori_loop(0, i, lambda j, c: block(j, c[0], c[1], c[2], False), (zero, zero, dq))
            _, _, dq = block(i, cl, cp, dq, True)
        dq_ref[...] = dq.astype(BF16)

        @pl.when(i == nq - 1)
        def _():
            dk_ref[...] = dk_acc[...].astype(BF16)
            dv_ref[...] = dv_acc[...].astype(BF16)

    blk = pl.BlockSpec((T, 2 * DH), lambda p, i: (i, p))
    full = pl.BlockSpec((S, 2 * DH), lambda p, i: (0, p))
    return pl.pallas_call(
        body, name="attn_bwd", grid=(SBW // (2 * DH), nq),
        in_specs=[blk, full, full, blk, blk], out_specs=[blk, full, full],
        out_shape=[_sds((S, SBW), BF16), _sds((S, SBW), BF16), _sds((S, SBW), BF16)],
        scratch_shapes=[pltpu.VMEM((S, 2 * DH), F32), pltpu.VMEM((S, 2 * DH), F32)],
        compiler_params=_params(("arbitrary", "arbitrary"), 40),
    )(q, k, v, do, ctot)


def _pool_counts(first_row, tm):
    pos = first_row + lax.broadcasted_iota(jnp.int32, (tm, 1), 0)
    return [jnp.minimum(pos + 1, w).astype(F32) for w in POOL_WINDOWS]


def _mix_out(h, xp, o_sb, gp, gs, w_group, scale, w_bp, w_ba, w_out):
    tm = 512

    def body(h_ref, xp_ref, o_ref, gp_ref, gs_ref, wg_hbm, sc_ref, wbp_hbm, wba_hbm, wo_hbm,
             h2_ref, pm_ref, p_ref, yp_ref, ys_ref, m_ref, halo, wg_ref, wbp_ref, wba_ref, wo_ref):
        _stage([(wg_hbm, wg_ref), (wbp_hbm, wbp_ref), (wba_hbm, wba_ref), (wo_hbm, wo_ref)])
        i = pl.program_id(0)

        @pl.when(i == 0)
        def _():
            halo[...] = jnp.zeros_like(halo)

        xp = xp_ref[...]
        ext = jnp.concatenate([halo[...], xp], axis=0)
        halo[...] = xp[tm - HALO:, :]
        counts = _pool_counts(i * tm, tm)
        for gi in range(len(POOL_WINDOWS)):
            lanes = slice(gi * PG, (gi + 1) * PG)
            win = ext[:, lanes]
            for step in range(gi + 1):
                win = win + pltpu.roll(win, 1 << step, 0)
            pm = (win[HALO:, :] / counts[gi] - xp[:, lanes]).astype(BF16)
            pm_ref[:, lanes] = pm
            p_ref[:, lanes] = (_nn(pm, wg_ref[gi]) * sc_ref[:, lanes]).astype(BF16)
        pb = p_ref[...]
        ob = o_ref[...]
        for j in range(NSH):
            cols = slice(j * (D // NSH), (j + 1) * (D // NSH))
            yp = _nn(pb, wbp_ref[j])
            ys = _nn(ob, wba_ref[j])
            yp_ref[:, cols] = yp.astype(BF16)
            ys_ref[:, cols] = ys.astype(BF16)
            m_ref[:, cols] = (gp_ref[:, cols].astype(F32) * yp + gs_ref[:, cols].astype(F32) * ys).astype(BF16)
        h2_ref[...] = h_ref[...] + _nn(m_ref[...], wo_ref[...])

    return pl.pallas_call(
        body, name="mix_out", grid=(S // tm,),
        in_specs=[_rows(tm, D), _rows(tm, PW), _rows(tm, SBW), _rows(tm, D), _rows(tm, D),
                  _ANY, _fixed((1, PW)), _ANY, _ANY, _ANY],
        out_specs=[_rows(tm, D), _rows(tm, PW), _rows(tm, PW), _rows(tm, D), _rows(tm, D), _rows(tm, D)],
        out_shape=[_sds((S, D), F32), _sds((S, PW), BF16), _sds((S, PW), BF16), _sds((S, D), BF16),
                   _sds((S, D), BF16), _sds((S, D), BF16)],
        scratch_shapes=[pltpu.VMEM((HALO, PW), F32)] + _vmem_like(w_group, w_bp, w_ba, w_out),
        compiler_params=_params(("arbitrary",), 48),
    )(h, xp, o_sb, gp, gs, w_group, scale, w_bp, w_ba, w_out)


def _mix_bwd_out(dh, gp, gs, yp, ys, pm, w_group, scale, w_bp, w_ba, w_out):
    tm = 512
    nt = S // tm

    def body(dh_ref, gp_ref, gs_ref, yp_ref, ys_ref, pm_ref, wg_hbm, sc_ref, wbp_hbm, wba_hbm, wo_hbm,
             dlg_ref, dyp_ref, dys_ref, do_ref, dyg_ref, dxp_ref, dsc_ref, halo, wg_ref, wbp_ref, wba_ref, wo_ref):
        _stage([(wg_hbm, wg_ref), (wbp_hbm, wbp_ref), (wba_hbm, wba_ref), (wo_hbm, wo_ref)])
        step = pl.program_id(0)

        @pl.when(step == 0)
        def _():
            halo[...] = jnp.zeros_like(halo)
            dsc_ref[...] = jnp.zeros_like(dsc_ref)

        dm = _nt(dh_ref[...].astype(BF16), wo_ref[...])
        gp = gp_ref[...].astype(F32)
        gs = gs_ref[...].astype(F32)
        yp = yp_ref[...].astype(F32)
        ys = ys_ref[...].astype(F32)
        dlg_ref[:, :D] = (dm * yp * gp * (1.0 - gp)).astype(BF16)
        dlg_ref[:, D:] = (dm * ys * gs * (1.0 - gs)).astype(BF16)
        dyp_ref[...] = (dm * gp).astype(BF16)
        dys_ref[...] = (dm * gs).astype(BF16)
        dp = jnp.zeros((tm, PW), F32)
        do = jnp.zeros((tm, SBW), F32)
        for j in range(NSH):
            cols = slice(j * (D // NSH), (j + 1) * (D // NSH))
            dp = dp + _nt(dyp_ref[:, cols], wbp_ref[j])
            do = do + _nt(dys_ref[:, cols], wba_ref[j])
        do_ref[...] = do.astype(BF16)
        counts = _pool_counts((nt - 1 - step) * tm, tm)
        dscale = []
        for gi in range(len(POOL_WINDOWS)):
            lanes = slice(gi * PG, (gi + 1) * PG)
            dpg = dp[:, lanes]
            dscale.append(jnp.sum(dpg * _nn(pm_ref[:, lanes], wg_ref[gi]), axis=0, keepdims=True))
            dyg = (dpg * sc_ref[:, lanes]).astype(BF16)
            dyg_ref[:, lanes] = dyg
            dpm = _nt(dyg, wg_ref[gi])
            per = dpm / counts[gi]
            win = jnp.concatenate([per, halo[:, lanes]], axis=0)
            halo[:, lanes] = per[:HALO, :]
            for s in range(gi + 1):
                win = win + pltpu.roll(win, tm + HALO - (1 << s), 0)
            dxp_ref[:, lanes] = (win[:tm, :] - dpm).astype(BF16)
        dsc_ref[...] += jnp.concatenate(dscale, axis=1)

    rev = lambda width: pl.BlockSpec((tm, width), lambda i: (nt - 1 - i, 0))
    return pl.pallas_call(
        body, name="mix_bwd_out", grid=(nt,),
        in_specs=[rev(D), rev(D), rev(D), rev(D), rev(D), rev(PW), _ANY, _fixed((1, PW)), _ANY, _ANY, _ANY],
        out_specs=[rev(2 * D), rev(D), rev(D), rev(SBW), rev(PW), rev(PW), _fixed((1, PW))],
        out_shape=[_sds((S, 2 * D), BF16), _sds((S, D), BF16), _sds((S, D), BF16), _sds((S, SBW), BF16),
                   _sds((S, PW), BF16), _sds((S, PW), BF16), _sds((1, PW), F32)],
        scratch_shapes=[pltpu.VMEM((HALO, PW), F32)] + _vmem_like(w_group, w_bp, w_ba, w_out),
        compiler_params=_params(("arbitrary",), 48),
    )(dh, gp, gs, yp, ys, pm, w_group, scale, w_bp, w_ba, w_out)


def _mix_bwd_in(dh, h, gain, dproj, w_in):
    tm = 512

    def body(dh_ref, h_ref, g_ref, dp_ref, w_hbm, dx_ref, dg_ref, w_ref):
        _stage([(w_hbm, w_ref)])
        du = jnp.zeros((tm, D), F32)
        for j in range(NSH):
            du = du + _nt(dp_ref[:, j * D:(j + 1) * D], w_ref[j])
        r, hr = _rms(h_ref[...])
        dx, dgain = _rms_bwd(du, hr, r, g_ref[...])
        dx_ref[...] = dh_ref[...] + dx

        @pl.when(pl.program_id(0) == 0)
        def _():
            dg_ref[...] = jnp.zeros_like(dg_ref)

        dg_ref[...] += dgain

    return pl.pallas_call(
        body, name="mix_bwd_in", grid=(S // tm,),
        in_specs=[_rows(tm, D), _rows(tm, D), _fixed((1, D)), _rows(tm, 4 * D), _ANY],
        out_specs=[_rows(tm, D), _fixed((1, D))],
        out_shape=[_sds((S, D), F32), _sds((1, D), F32)],
        scratch_shapes=_vmem_like(w_in),
        compiler_params=_params(("arbitrary",), 48),
    )(dh, h, gain, dproj, w_in)


def _wgrad(a, b, nblk, ti, name, out_dtype=BF16):
    ka, n = a.shape[1], b.shape[1]
    ns = n // nblk

    def body(a_ref, b_ref, o_ref):
        o_ref[...] = _tn(a_ref[...].astype(BF16), b_ref[...].astype(BF16)).astype(out_dtype)

    return pl.pallas_call(
        body, name=name, grid=(nblk, ka // ti),
        in_specs=[pl.BlockSpec((S, ti), lambda j, i: (0, i)), pl.BlockSpec((S, ns), lambda j, i: (0, j))],
        out_specs=pl.BlockSpec((None, ti, ns), lambda j, i: (j, i, 0)),
        out_shape=_sds((nblk, ka, ns), out_dtype),
        compiler_params=_params(("arbitrary", "arbitrary"), 56),
    )(a, b)


def _wgrad_groups(pm, dyg):
    def body(a_ref, b_ref, o_ref):
        o_ref[...] = _tn(a_ref[...], b_ref[...])

    col = pl.BlockSpec((S, PG), lambda g: (0, g))
    return pl.pallas_call(
        body, name="wgrad_groups", grid=(PW // PG,),
        in_specs=[col, col], out_specs=pl.BlockSpec((None, PG, PG), lambda g: (g, 0, 0)),
        out_shape=_sds((PW // PG, PG, PG), F32),
        compiler_params=_params(("arbitrary",), 32),
    )(pm, dyg)


def _place():
    x, y, c = lax.axis_index("x"), lax.axis_index("y"), lax.axis_index("c")
    chips = [(1 - x, y), (x, 1 - y), (1 - x, 1 - y)]
    return x, y, c, chips


def _remote(src, dst, ssem, rsem, dev):
    return pltpu.make_async_remote_copy(src_ref=src, dst_ref=dst, send_sem=ssem, recv_sem=rsem,
                                        device_id=dev, device_id_type=MESH)


def _comm_call(body, name, ins, out_shapes, n_sems, n_local):
    return pl.pallas_call(
        body, name=name,
        in_specs=[_ANY] * len(ins), out_specs=[_ANY] * len(out_shapes), out_shape=out_shapes,
        scratch_shapes=[pltpu.SemaphoreType.DMA((n_sems,)), pltpu.SemaphoreType.DMA((n_sems,)),
                        pltpu.SemaphoreType.DMA((max(n_local, 1),))],
    )(*ins)


def _gather_weights(shards):
    n = len(shards)

    def body(*refs):
        ins, outs = refs[:n], refs[n:2 * n]
        ssem, rsem, lsem = refs[2 * n:]
        x, y, c, chips = _place()
        me, sib = 2 * x + y, (x, y, 1 - c)
        started = []
        for w in range(n):
            half = ins[w].shape[0] // 2
            mine = pl.ds(c * half, half)
            lc = pltpu.make_async_copy(ins[w], outs[w].at[me], lsem.at[w])
            lc.start()
            started.append((lc, False))
            for k, (px, py) in enumerate(chips):
                cp = _remote(ins[w].at[mine], outs[w].at[me, mine], ssem.at[6 * w + k], rsem.at[6 * w + k],
                             (px, py, c))
                cp.start()
                started.append((cp, True))
        for w in range(n):
            half = ins[w].shape[0] // 2
            for k, (px, py) in enumerate(chips):
                got = outs[w].at[2 * px + py, pl.ds(c * half, half)]
                _remote(got, got, ssem.at[6 * w + k], rsem.at[6 * w + k], (px, py, c)).wait_recv()
                fw = _remote(got, got, ssem.at[6 * w + 3 + k], rsem.at[6 * w + 3 + k], sib)
                fw.start()
                started.append((fw, True))
        for w in range(n):
            half = ins[w].shape[0] // 2
            for k, (px, py) in enumerate(chips):
                got = outs[w].at[2 * px + py, pl.ds((1 - c) * half, half)]
                _remote(got, got, ssem.at[6 * w + 3 + k], rsem.at[6 * w + 3 + k], sib).wait_recv()
        for cp, remote in started:
            if remote:
                cp.wait_send()
            else:
                cp.wait()

    outs = [_sds((NSH,) + s.shape, s.dtype) for s in shards]
    return _comm_call(body, "gather_weights", shards, outs, 6 * n, n)


def _pair_swap(grads):
    n = len(grads)

    def body(*refs):
        ins, outs = refs[:n], refs[n:2 * n]
        ssem, rsem, _ = refs[2 * n:]
        x, y, c, _ = _place()
        cps = [_remote(ins[w].at[:, 1 - c], outs[w], ssem.at[w], rsem.at[w], (x, y, 1 - c)) for w in range(n)]
        for cp in cps:
            cp.start()
        for cp in cps:
            cp.wait_recv()
        for cp in cps:
            cp.wait_send()

    outs = [_sds((NSH,) + g.shape[2:], g.dtype) for g in grads]
    return _comm_call(body, "pair_swap", grads, outs, n, 0)


def _scatter_to_owner(parts):
    n = len(parts)

    def body(*refs):
        ins, outs = refs[:n], refs[n:2 * n]
        ssem, rsem, lsem = refs[2 * n:]
        x, y, c, chips = _place()
        me = 2 * x + y
        started = []
        for w in range(n):
            lc = pltpu.make_async_copy(ins[w].at[me], outs[w].at[me], lsem.at[w])
            lc.start()
            started.append((lc, False))
            for k, (px, py) in enumerate(chips):
                cp = _remote(ins[w].at[2 * px + py], outs[w].at[me], ssem.at[3 * w + k], rsem.at[3 * w + k],
                             (px, py, c))
                cp.start()
                started.append((cp, True))
        for w in range(n):
            for k, (px, py) in enumerate(chips):
                got = outs[w].at[2 * px + py]
                _remote(got, got, ssem.at[3 * w + k], rsem.at[3 * w + k], (px, py, c)).wait_recv()
        for cp, remote in started:
            if remote:
                cp.wait_send()
            else:
                cp.wait()

    outs = [_sds(p.shape, p.dtype) for p in parts]
    return _comm_call(body, "scatter_to_owner", parts, outs, 3 * n, n)


def _share_halves(halves):
    n = len(halves)

    def body(*refs):
        ins, outs = refs[:n], refs[n:2 * n]
        ssem, rsem, lsem = refs[2 * n:]
        x, y, c, _ = _place()
        lcs = [pltpu.make_async_copy(ins[w], outs[w].at[c], lsem.at[w]) for w in range(n)]
        cps = [_remote(ins[w], outs[w].at[c], ssem.at[w], rsem.at[w], (x, y, 1 - c)) for w in range(n)]
        for cp in lcs + cps:
            cp.start()
        for w in range(n):
            got = outs[w].at[1 - c]
            _remote(got, got, ssem.at[w], rsem.at[w], (x, y, 1 - c)).wait_recv()
        for cp in cps:
            cp.wait_send()
        for cp in lcs:
            cp.wait()

    outs = [_sds((2,) + h.shape, h.dtype) for h in halves]
    return _comm_call(body, "share_halves", halves, outs, n, n)


def _gather_small(block):
    m_per, n = block.shape

    def body(x_ref, out_ref, ssem, rsem, lsem):
        x, y, c, chips = _place()
        me, sib = (x, y, c), (x, y, 1 - c)

        def rows(px, py, pc):
            return out_ref.at[pl.ds((4 * px + 2 * py + pc) * m_per, m_per), :]

        def copy(k, blk, to, src=None):
            return _remote(rows(*blk) if src is None else src, rows(*blk), ssem.at[k], rsem.at[k], to)

        mine = pltpu.make_async_copy(x_ref, rows(*me), lsem)
        mine.start()
        first = [copy(0, me, sib, src=x_ref)]
        first += [copy(1 + j, me, (*chip, c), src=x_ref) for j, chip in enumerate(chips)]
        for cp in first:
            cp.start()
        passed = [copy(4 + j, (*chip, c), sib) for j, chip in enumerate(chips)]
        for j, chip in enumerate(chips):
            copy(1 + j, (*chip, c), me).wait_recv()
            passed[j].start()
        copy(0, sib, me).wait_recv()
        for j, chip in enumerate(chips):
            copy(4 + j, (*chip, 1 - c), me).wait_recv()
        for cp in first + passed:
            cp.wait_send()
        mine.wait()

    return pl.pallas_call(
        body, name="gather_small", out_shape=_sds((8 * m_per, n), block.dtype),
        in_specs=[_VM], out_specs=_VM,
        scratch_shapes=[pltpu.SemaphoreType.DMA((7,)), pltpu.SemaphoreType.DMA((7,)), pltpu.SemaphoreType.DMA],
    )(block)


def _row_block(rows):
    return max(t for t in range(16, 257, 16) if rows % t == 0)


def _pair_sum(grad, got, c_idx, name):
    _, _, half, cols = grad.shape
    tr = _row_block(half)

    def body(c_ref, a_ref, b_ref, o_ref):
        o_ref[...] = (a_ref[...].astype(F32) + b_ref[...].astype(F32)).astype(BF16)

    return pl.pallas_call(
        body, name=name, out_shape=_sds((NSH, half, cols), BF16),
        grid_spec=pltpu.PrefetchScalarGridSpec(
            num_scalar_prefetch=1, grid=(NSH, half // tr),
            in_specs=[pl.BlockSpec((None, None, tr, cols), lambda j, r, c: (j, c[0], r, 0)),
                      pl.BlockSpec((None, tr, cols), lambda j, r, c: (j, r, 0))],
            out_specs=pl.BlockSpec((None, tr, cols), lambda j, r, c: (j, r, 0))),
        compiler_params=_params(("arbitrary", "arbitrary"), 32),
    )(c_idx, grad, got)


def _chip_sum(parts, name):
    _, half, cols = parts.shape
    tr = _row_block(half)

    def body(p_ref, o_ref):
        acc = p_ref[0].astype(F32)
        for j in range(1, NSH):
            acc = acc + p_ref[j].astype(F32)
        o_ref[...] = acc

    return pl.pallas_call(
        body, name=name, grid=(half // tr,), out_shape=_sds((half, cols), F32),
        in_specs=[pl.BlockSpec((NSH, tr, cols), lambda r: (0, r, 0))],
        out_specs=pl.BlockSpec((tr, cols), lambda r: (r, 0)),
        compiler_params=_params(("arbitrary",), 32),
    )(parts)


def _adamw_math(w, g, m, v):
    m = B1 * m + (1.0 - B1) * g
    v = B2 * v + (1.0 - B2) * (g * g)
    m_hat = m / (1.0 - B1 ** STEP)
    v_hat = v / (1.0 - B2 ** STEP)
    return -LR * (m_hat / (jnp.sqrt(v_hat) + AEPS) + WD * w), m, v


def _adamw(w, g, m, v, name):
    rows, cols = w.shape
    tr = _row_block(rows)

    def body(w_ref, g_ref, m_ref, v_ref, d_ref, nm_ref, nv_ref):
        d_ref[...], nm_ref[...], nv_ref[...] = _adamw_math(w_ref[...], g_ref[...], m_ref[...], v_ref[...])

    blk = pl.BlockSpec((tr, cols), lambda r: (r, 0))
    return pl.pallas_call(
        body, name=name, grid=(rows // tr,), out_shape=[_sds(w.shape, F32)] * 3,
        in_specs=[blk] * 4, out_specs=[blk] * 3,
        compiler_params=_params(("arbitrary",), 32),
    )(w, g, m, v)


def _small_update(gathered, w, m, v):
    rows = w.shape[0]

    def body(ga_ref, w_ref, m_ref, v_ref, g_ref, d_ref, nm_ref, nv_ref):
        g = ga_ref[0:rows, :]
        for dev in range(1, 8):
            g = g + ga_ref[dev * rows:(dev + 1) * rows, :]
        g_ref[...] = g
        d_ref[...], nm_ref[...], nv_ref[...] = _adamw_math(w_ref[...], g, m_ref[...], v_ref[...])

    return pl.pallas_call(
        body, name="small_update", out_shape=[_sds(w.shape, F32)] * 4,
        in_specs=[_VM] * 4, out_specs=[_VM] * 4,
    )(gathered, w, m, v)


SMALL = ("ffn1_norm", "mix_norm", "ffn2_norm", "final_norm", "pool_scale", "pool_w_group")
BIG = ("ffn1_w_gate_up", "ffn1_w_down", "w_in", "w_branch_pool", "w_branch_attn", "w_out",
       "ffn2_w_gate_up", "ffn2_w_down")
ORDER = ("ffn1_norm", "ffn1_w_gate_up", "ffn1_w_down", "mix_norm", "w_in", "pool_w_group", "pool_scale",
         "w_branch_pool", "w_branch_attn", "w_out", "ffn2_norm", "ffn2_w_gate_up", "ffn2_w_down", "final_norm")
SMALL_ROWS = 552


def _pack_small(t):
    parts = []
    for k in SMALL:
        rows = t[k].reshape(-1, 128)
        parts.append(jnp.pad(rows, ((0, -rows.shape[0] % 8), (0, 0))))
    packed = jnp.concatenate(parts, axis=0)
    assert packed.shape == (SMALL_ROWS, 128), packed.shape
    return packed


def _unpack_small(packed, like):
    out, at = {}, 0
    for k in SMALL:
        n = like[k].size // 128
        out[k] = packed[at:at + n].reshape(like[k].shape)
        at += n + (-n % 8)
    return out


def kernel(x, ffn1_norm, ffn1_w_gate_up, ffn1_w_down, mix_norm, w_in, pool_w_group, pool_scale, w_branch_pool, w_branch_attn, w_out, ffn2_norm, ffn2_w_gate_up, ffn2_w_down, final_norm, loss_target, m_ffn1_norm, m_ffn1_w_gate_up, m_ffn1_w_down, m_mix_norm, m_w_in, m_pool_w_group, m_pool_scale, m_w_branch_pool, m_w_branch_attn, m_w_out, m_ffn2_norm, m_ffn2_w_gate_up, m_ffn2_w_down, m_final_norm, v_ffn1_norm, v_ffn1_w_gate_up, v_ffn1_w_down, v_mix_norm, v_w_in, v_pool_w_group, v_pool_scale, v_w_branch_pool, v_w_branch_attn, v_w_out, v_ffn2_norm, v_ffn2_w_gate_up, v_ffn2_w_down, v_final_norm):
    wts = dict(ffn1_norm=ffn1_norm, ffn1_w_gate_up=ffn1_w_gate_up, ffn1_w_down=ffn1_w_down, mix_norm=mix_norm,
               w_in=w_in, pool_w_group=pool_w_group, pool_scale=pool_scale, w_branch_pool=w_branch_pool,
               w_branch_attn=w_branch_attn, w_out=w_out, ffn2_norm=ffn2_norm, ffn2_w_gate_up=ffn2_w_gate_up,
               ffn2_w_down=ffn2_w_down, final_norm=final_norm)
    mom = dict(ffn1_norm=m_ffn1_norm, ffn1_w_gate_up=m_ffn1_w_gate_up, ffn1_w_down=m_ffn1_w_down,
               mix_norm=m_mix_norm, w_in=m_w_in, pool_w_group=m_pool_w_group, pool_scale=m_pool_scale,
               w_branch_pool=m_w_branch_pool, w_branch_attn=m_w_branch_attn, w_out=m_w_out,
               ffn2_norm=m_ffn2_norm, ffn2_w_gate_up=m_ffn2_w_gate_up, ffn2_w_down=m_ffn2_w_down,
               final_norm=m_final_norm)
    var = dict(ffn1_norm=v_ffn1_norm, ffn1_w_gate_up=v_ffn1_w_gate_up, ffn1_w_down=v_ffn1_w_down,
               mix_norm=v_mix_norm, w_in=v_w_in, pool_w_group=v_pool_w_group, pool_scale=v_pool_scale,
               w_branch_pool=v_w_branch_pool, w_branch_attn=v_w_branch_attn, w_out=v_w_out,
               ffn2_norm=v_ffn2_norm, ffn2_w_gate_up=v_ffn2_w_gate_up, ffn2_w_down=v_ffn2_w_down,
               final_norm=v_final_norm)

    full = dict(zip(BIG, _gather_weights([wts[k][0].astype(BF16) for k in BIG])))

    loss_row, dx, partial, small_g = _local_step(x[0], loss_target[0], wts, full)

    c_idx = lax.axis_index("c").astype(jnp.int32).reshape(1)
    split = [partial[k].reshape(NSH, 2, partial[k].shape[1] // 2, partial[k].shape[2]) for k in BIG]
    got = _pair_swap(split)
    chip_part = [_pair_sum(split[i], got[i], c_idx, "pair_sum_" + k) for i, k in enumerate(BIG)]
    owned = _scatter_to_owner(chip_part)
    halves = [_chip_sum(owned[i], "chip_sum_" + k) for i, k in enumerate(BIG)]
    both = _share_halves(halves)
    grad = {k: both[i].reshape(wts[k].shape) for i, k in enumerate(BIG)}

    gathered = _gather_small(_pack_small(small_g))
    sg, sd, sm, sv = _small_update(gathered, _pack_small(wts), _pack_small(mom), _pack_small(var))
    grad.update(_unpack_small(sg, wts))
    delta, new_m, new_v = _unpack_small(sd, wts), _unpack_small(sm, wts), _unpack_small(sv, wts)
    for k in BIG:
        shp = wts[k].shape
        d_, m_, v_ = _adamw(wts[k][0], grad[k][0], mom[k][0], var[k][0], "adamw_" + k)
        delta[k], new_m[k], new_v[k] = d_.reshape(shp), m_.reshape(shp), v_.reshape(shp)

    loss = lax.psum(loss_row[0, 0], ("x", "y", "c"))
    return (loss, dx[None], *[grad[k] for k in ORDER], *[delta[k] for k in ORDER],
            *[new_m[k] for k in ORDER], *[new_v[k] for k in ORDER])


def _local_step(x0, tgt, wts, full):
    wgu1, wgu2 = full["ffn1_w_gate_up"], full["ffn2_w_gate_up"]
    wd1, wd2 = full["ffn1_w_down"].reshape(DFF, D), full["ffn2_w_down"].reshape(DFF, D)
    win, wbp, wba = full["w_in"], full["w_branch_pool"], full["w_branch_attn"]
    wout = full["w_out"].reshape(D, D)
    wgrp = wts["pool_w_group"][0].astype(BF16)
    pool_scale = wts["pool_scale"]
    g1, gm, g2, gf = wts["ffn1_norm"], wts["mix_norm"], wts["ffn2_norm"], wts["final_norm"].reshape(1, D)

    h1, n1, gu1 = _ffn_fwd(x0, g1, wgu1, wd1, "ffn1_fwd")
    u, xp, q, k, v, gp, gs = _mix_in(h1, gm, win)
    o_sb, ctot = _attn_fwd(q, k, v)
    h2, pm, p, yp, ys, mm = _mix_out(h1, xp, o_sb, gp, gs, wgrp, pool_scale, wbp, wba, wout)
    h3, n3, gu3 = _ffn_fwd(h2, g2, wgu2, wd2, "ffn2_fwd")
    dh3, loss_row, d_gf = _head(h3, tgt, gf)

    dh2, dgu3, a3, d_g2 = _ffn_bwd(dh3, h2, g2, gu3, wgu2, wd2, "ffn2_bwd")
    dlg, dyp, dys, do_sb, dyg, dxp, d_scale = _mix_bwd_out(dh2, gp, gs, yp, ys, pm, wgrp, pool_scale, wbp, wba, wout)
    dq, dk, dv = _attn_bwd(q, k, v, do_sb, ctot)
    dproj = jnp.concatenate([dxp, dq, dk, dv, dlg], axis=1)
    dh1, d_gm = _mix_bwd_in(dh2, h1, gm, dproj, win)
    dx, dgu1, a1, d_g1 = _ffn_bwd(dh1, x0, g1, gu1, wgu1, wd1, "ffn1_bwd")

    partial = {
        "ffn1_w_gate_up": _wgrad(n1, dgu1, NSH, 512, "wgrad_gu1"),
        "ffn1_w_down": _wgrad(a1, dh1, 1, FFS, "wgrad_d1").reshape(NSH, DFF // NSH, D),
        "w_in": _wgrad(u, dproj, NSH, 512, "wgrad_in"),
        "w_branch_pool": _wgrad(p, dyp, NSH, PW, "wgrad_bp"),
        "w_branch_attn": _wgrad(o_sb, dys, NSH, SBW, "wgrad_ba"),
        "w_out": _wgrad(mm, dh2, 1, 512, "wgrad_out").reshape(NSH, D // NSH, D),
        "ffn2_w_gate_up": _wgrad(n3, dgu3, NSH, 512, "wgrad_gu2"),
        "ffn2_w_down": _wgrad(a3, dh3, 1, FFS, "wgrad_d2").reshape(NSH, DFF // NSH, D),
    }
    small_g = dict(ffn1_norm=d_g1, mix_norm=d_gm, ffn2_norm=d_g2, final_norm=d_gf, pool_scale=d_scale,
                   pool_w_group=_wgrad_groups(pm, dyg))
    return loss_row, dx, partial, small_g
```

```python
import functools

import jax
import jax.numpy as jnp
from jax import lax
from jax.experimental import pallas as pl
from jax.experimental.pallas import tpu as pltpu

F32 = jnp.float32
BF16 = jnp.bfloat16

S = 2048
D = 1024
DFF = 2816
FFS = 2 * DFF // 4
NSH = 4
PW = 512
PG = 128
POOL_WINDOWS = (2, 4, 8, 16)
HALO = 16
SBW = 512
DH = 64
EPS = 1e-6
SCALE = 0.125
TA = 256
MIB = 1024 * 1024

LR, B1, B2, AEPS, WD, STEP = 0.001, 0.9, 0.999, 1e-08, 0.01, 10

_VM = pl.BlockSpec(memory_space=pltpu.VMEM)
_ANY = pl.BlockSpec(memory_space=pl.ANY)
MESH = pl.DeviceIdType.MESH


def _nn(a, b):
    return jnp.dot(a, b, preferred_element_type=F32)


def _nt(a, b):
    return lax.dot_general(a, b, (((1,), (1,)), ((), ())), preferred_element_type=F32)


def _tn(a, b):
    return lax.dot_general(a, b, (((0,), (0,)), ((), ())), preferred_element_type=F32)


def _params(sem, vmem_mib):
    return pltpu.CompilerParams(dimension_semantics=sem, vmem_limit_bytes=vmem_mib * MIB)


def _rows(tm, width):
    return pl.BlockSpec((tm, width), lambda i: (i, 0))


def _fixed(shape):
    return pl.BlockSpec(shape, lambda *_: (0,) * len(shape))


def _sds(shape, dtype):
    return jax.ShapeDtypeStruct(shape, dtype)


def _stage(pairs):
    @pl.when(pl.program_id(0) == 0)
    def _():
        for src, dst in pairs:
            pltpu.sync_copy(src, dst)


def _vmem_like(*arrays):
    return [pltpu.VMEM(a.shape, a.dtype) for a in arrays]


def _rms(x):
    r = lax.rsqrt(jnp.mean(x * x, axis=-1, keepdims=True) + EPS)
    return r, x * r


def _rms_bwd(dn, xr, r, gain):
    dng = dn * gain
    dx = r * (dng - xr * jnp.mean(dng * xr, axis=-1, keepdims=True))
    return dx, jnp.sum(dn * xr, axis=0, keepdims=True)


def _ffn_fwd(x, gain, wgu, wd, name):
    tm = 256

    def body(x_ref, g_ref, wgu_hbm, wd_hbm, h_ref, n_ref, gu_ref, wgu_ref, wd_ref):
        _stage([(wgu_hbm, wgu_ref), (wd_hbm, wd_ref)])
        x = x_ref[...]
        _, xr = _rms(x)
        n = (xr * g_ref[...]).astype(BF16)
        n_ref[...] = n
        acc = jnp.zeros((tm, D), F32)
        for j in range(2):
            g = _nn(n, wgu_ref[j])
            u = _nn(n, wgu_ref[2 + j])
            gu_ref[:, j * FFS:(j + 1) * FFS] = g.astype(BF16)
            gu_ref[:, (2 + j) * FFS:(3 + j) * FFS] = u.astype(BF16)
            a = (g * jax.nn.sigmoid(g) * u).astype(BF16)
            acc = acc + _nn(a, wd_ref[j * FFS:(j + 1) * FFS, :])
        h_ref[...] = x + 0.5 * acc

    return pl.pallas_call(
        body, name=name, grid=(S // tm,),
        in_specs=[_rows(tm, D), _fixed((1, D)), _ANY, _ANY],
        out_specs=[_rows(tm, D), _rows(tm, D), _rows(tm, 4 * FFS)],
        out_shape=[_sds((S, D), F32), _sds((S, D), BF16), _sds((S, 4 * FFS), BF16)],
        scratch_shapes=_vmem_like(wgu, wd),
        compiler_params=_params(("arbitrary",), 56),
    )(x, gain, wgu, wd)


def _ffn_bwd(dh, x, gain, gu, wgu, wd, name):
    tm = 256

    def body(dh_ref, x_ref, g_ref, gu_ref, wgu_hbm, wd_hbm, dx_ref, dgu_ref, a_ref, dg_ref, wgu_ref, wd_ref):
        _stage([(wgu_hbm, wgu_ref), (wd_hbm, wd_ref)])
        dh = dh_ref[...]
        dhb = dh.astype(BF16)
        dn = jnp.zeros((tm, D), F32)
        for j in range(2):
            g = gu_ref[:, j * FFS:(j + 1) * FFS].astype(F32)
            u = gu_ref[:, (2 + j) * FFS:(3 + j) * FFS].astype(F32)
            da = 0.5 * _nt(dhb, wd_ref[j * FFS:(j + 1) * FFS, :])
            sg = jax.nn.sigmoid(g)
            si = g * sg
            a_ref[:, j * FFS:(j + 1) * FFS] = (0.5 * si * u).astype(BF16)
            dgb = (da * u * (sg * (1.0 + g * (1.0 - sg)))).astype(BF16)
            dub = (da * si).astype(BF16)
            dgu_ref[:, j * FFS:(j + 1) * FFS] = dgb
            dgu_ref[:, (2 + j) * FFS:(3 + j) * FFS] = dub
            dn = dn + _nt(dgb, wgu_ref[j]) + _nt(dub, wgu_ref[2 + j])
        r, xr = _rms(x_ref[...])
        dx, dgain = _rms_bwd(dn, xr, r, g_ref[...])
        dx_ref[...] = dh + dx

        @pl.when(pl.program_id(0) == 0)
        def _():
            dg_ref[...] = jnp.zeros_like(dg_ref)

        dg_ref[...] += dgain

    return pl.pallas_call(
        body, name=name, grid=(S // tm,),
        in_specs=[_rows(tm, D), _rows(tm, D), _fixed((1, D)), _rows(tm, 4 * FFS), _ANY, _ANY],
        out_specs=[_rows(tm, D), _rows(tm, 4 * FFS), _rows(tm, DFF), _fixed((1, D))],
        out_shape=[_sds((S, D), F32), _sds((S, 4 * FFS), BF16), _sds((S, DFF), BF16), _sds((1, D), F32)],
        scratch_shapes=_vmem_like(wgu, wd),
        compiler_params=_params(("arbitrary",), 56),
    )(dh, x, gain, gu, wgu, wd)


def _head(h, target, gain):
    tm = 512

    def body(h_ref, t_ref, g_ref, dh_ref, loss_ref, dg_ref):
        gain = g_ref[...]
        r, hr = _rms(h_ref[...])
        err = hr * gain - t_ref[...]
        dy = err * (1.0 / D)
        dh, dgain = _rms_bwd(dy, hr, r, gain)
        dh_ref[...] = dh

        @pl.when(pl.program_id(0) == 0)
        def _():
            dg_ref[...] = jnp.zeros_like(dg_ref)
            loss_ref[...] = jnp.zeros_like(loss_ref)

        dg_ref[...] += dgain
        loss_ref[...] += jnp.full((1, 128), (0.5 / D) * jnp.sum(err * err), F32)

    return pl.pallas_call(
        body, name="head", grid=(S // tm,),
        in_specs=[_rows(tm, D), _rows(tm, D), _fixed((1, D))],
        out_specs=[_rows(tm, D), _fixed((1, 128)), _fixed((1, D))],
        out_shape=[_sds((S, D), F32), _sds((1, 128), F32), _sds((1, D), F32)],
        compiler_params=_params(("arbitrary",), 40),
    )(h, target, gain)


def _mix_in(h, gain, w_in):
    tm = 512

    def body(h_ref, g_ref, w_hbm, u_ref, xp_ref, q_ref, k_ref, v_ref, gp_ref, gs_ref, w_ref):
        _stage([(w_hbm, w_ref)])
        _, hr = _rms(h_ref[...])
        u = (hr * g_ref[...]).astype(BF16)
        u_ref[...] = u
        p0 = _nn(u, w_ref[0])
        xp_ref[...] = p0[:, :PW]
        q_ref[...] = p0[:, PW:].astype(BF16)
        p1 = _nn(u, w_ref[1])
        k_ref[...] = p1[:, :SBW].astype(BF16)
        v_ref[...] = p1[:, SBW:].astype(BF16)
        gp_ref[...] = jax.nn.sigmoid(_nn(u, w_ref[2])).astype(BF16)
        gs_ref[...] = jax.nn.sigmoid(_nn(u, w_ref[3])).astype(BF16)

    return pl.pallas_call(
        body, name="mix_in", grid=(S // tm,),
        in_specs=[_rows(tm, D), _fixed((1, D)), _ANY],
        out_specs=[_rows(tm, D), _rows(tm, PW), _rows(tm, SBW), _rows(tm, SBW), _rows(tm, SBW),
                   _rows(tm, D), _rows(tm, D)],
        out_shape=[_sds((S, D), BF16), _sds((S, PW), F32), _sds((S, SBW), BF16), _sds((S, SBW), BF16),
                   _sds((S, SBW), BF16), _sds((S, D), BF16), _sds((S, D), BF16)],
        scratch_shapes=_vmem_like(w_in),
        compiler_params=_params(("arbitrary",), 48),
    )(h, gain, w_in)


def _hilo_dot(x, tri):
    hi = x.astype(BF16)
    lo = (x - hi.astype(F32)).astype(BF16)
    return _nn(hi, tri) + _nn(lo, tri)


def _log_terms(qm, kj):
    z = _nt(qm, kj) * SCALE
    e = jnp.exp(-jnp.abs(z))
    lb = jnp.minimum(z, 0.0) - jnp.log(1.0 + e)
    return z, e, lb, lb - z


def _head_masks():
    lane = lax.broadcasted_iota(jnp.int32, (1, 2 * DH), 1)
    return (lane < DH, lane >= DH)


def _attn_fwd(q, k, v):
    T = TA

    def body(q_ref, k_ref, v_ref, o_ref, c_ref):
        i = pl.program_id(1)
        row = lax.broadcasted_iota(jnp.int32, (T, T), 0)
        col = lax.broadcasted_iota(jnp.int32, (T, T), 1)
        after = (row > col).astype(BF16)
        causal = col < row
        q2 = q_ref[...]
        o_acc = jnp.zeros((T, 2 * DH), F32)
        c_out = jnp.zeros((T, 2 * DH), F32)
        for hm in _head_masks():
            qm = jnp.where(hm, q2, jnp.zeros_like(q2))

            def block(j, carry, o, diag):
                kj = k_ref[pl.ds(pl.multiple_of(j * T, T), T), :]
                vj = v_ref[pl.ds(pl.multiple_of(j * T, T), T), :]
                _, _, lb, l1m = _log_terms(qm, kj)
                if diag:
                    l1m = jnp.where(causal, l1m, 0.0)
                a = jnp.exp(lb + _hilo_dot(l1m, after) + carry)
                if diag:
                    a = jnp.where(causal, a, 0.0)
                o = o + _nn(a.astype(BF16), jnp.where(hm, vj, jnp.zeros_like(vj)))
                return carry + jnp.sum(l1m, axis=1, keepdims=True), o

            carry, o_acc = block(i, jnp.zeros((T, 1), F32), o_acc, True)
            carry, o_acc = lax.fori_loop(
                0, i, lambda jj, c: block(i - 1 - jj, c[0], c[1], False), (carry, o_acc))
            c_out = c_out + jnp.where(hm, carry, 0.0)
        o_ref[...] = o_acc.astype(BF16)
        c_ref[...] = c_out

    blk = pl.BlockSpec((T, 2 * DH), lambda p, i: (i, p))
    full = pl.BlockSpec((S, 2 * DH), lambda p, i: (0, p))
    return pl.pallas_call(
        body, name="attn_fwd", grid=(SBW // (2 * DH), S // T),
        in_specs=[blk, full, full], out_specs=[blk, blk],
        out_shape=[_sds((S, SBW), BF16), _sds((S, SBW), F32)],
        compiler_params=_params(("arbitrary", "arbitrary"), 40),
    )(q, k, v)


def _attn_bwd(q, k, v, do, ctot):
    T = TA
    nq = S // T

    def body(q_ref, k_ref, v_ref, do_ref, c_ref, dq_ref, dk_ref, dv_ref, dk_acc, dv_acc):
        i = pl.program_id(1)

        @pl.when(i == 0)
        def _():
            dk_acc[...] = jnp.zeros_like(dk_acc)
            dv_acc[...] = jnp.zeros_like(dv_acc)

        row = lax.broadcasted_iota(jnp.int32, (T, T), 0)
        col = lax.broadcasted_iota(jnp.int32, (T, T), 1)
        upto = (row <= col).astype(BF16)
        before = (row < col).astype(BF16)
        causal = col < row
        q2 = q_ref[...]
        do2 = do_ref[...]
        dq = jnp.zeros((T, 2 * DH), F32)
        for h, hm in enumerate(_head_masks()):
            qm = jnp.where(hm, q2, jnp.zeros_like(q2))
            dom = jnp.where(hm, do2, jnp.zeros_like(do2))
            ctot = c_ref[:, h * DH:h * DH + 1]

            def block(j, cl, cp, dq, diag):
                rows = pl.ds(pl.multiple_of(j * T, T), T)
                kj = k_ref[rows, :]
                vj = v_ref[rows, :]
                z, e, lb, l1m = _log_terms(qm, kj)
                if diag:
                    l1m = jnp.where(causal, l1m, 0.0)
                a = jnp.exp(lb + (ctot - cl) - _hilo_dot(l1m, upto))
                if diag:
                    a = jnp.where(causal, a, 0.0)
                dl = _nt(dom, vj) * a
                pex = _hilo_dot(dl, before) + cp
                rinv = 1.0 / (1.0 + e)
                pos = z >= 0.0
                beta = jnp.where(pos, rinv, e * rinv)
                omb = jnp.where(pos, e * rinv, rinv)
                dz = (dl * omb - pex * beta) * SCALE
                if diag:
                    dz = jnp.where(causal, dz, 0.0)
                dzb = dz.astype(BF16)
                dq = dq + _nn(dzb, jnp.where(hm, kj, jnp.zeros_like(kj)))
                dk_acc[rows, :] += _tn(dzb, qm)
                dv_acc[rows, :] += _tn(a.astype(BF16), dom)
                return (cl + jnp.sum(l1m, axis=1, keepdims=True),
                        cp + jnp.sum(dl, axis=1, keepdims=True), dq)

            zero = jnp.zeros((T, 1), F32)
            cl, cp, dq = lax.fori_loop(0, i, lambda j, c: block(j, c[0], c[1], c[2], False), (zero, zero, dq))
            _, _, dq = block(i, cl, cp, dq, True)
        dq_ref[...] = dq.astype(BF16)

        @pl.when(i == nq - 1)
        def _():
            dk_ref[...] = dk_acc[...].astype(BF16)
            dv_ref[...] = dv_acc[...].astype(BF16)

    blk = pl.BlockSpec((T, 2 * DH), lambda p, i: (i, p))
    full = pl.BlockSpec((S, 2 * DH), lambda p, i: (0, p))
    return pl.pallas_call(
        body, name="attn_bwd", grid=(SBW // (2 * DH), nq),
        in_specs=[blk, full, full, blk, blk], out_specs=[blk, full, full],
        out_shape=[_sds((S, SBW), BF16), _sds((S, SBW), BF16), _sds((S, SBW), BF16)],
        scratch_shapes=[pltpu.VMEM((S, 2 * DH), F32), pltpu.VMEM((S, 2 * DH), F32)],
        compiler_params=_params(("arbitrary", "arbitrary"), 40),
    )(q, k, v, do, ctot)


def _pool_counts(first_row, tm):
    pos = first_row + lax.broadcasted_iota(jnp.int32, (tm, 1), 0)
    return [jnp.minimum(pos + 1, w).astype(F32) for w in POOL_WINDOWS]


def _mix_out(h, xp, o_sb, gp, gs, w_group, scale, w_bp, w_ba, w_out):
    tm = 512

    def body(h_ref, xp_ref, o_ref, gp_ref, gs_ref, wg_hbm, sc_ref, wbp_hbm, wba_hbm, wo_hbm,
             h2_ref, pm_ref, p_ref, yp_ref, ys_ref, m_ref, halo, wg_ref, wbp_ref, wba_ref, wo_ref):
        _stage([(wg_hbm, wg_ref), (wbp_hbm, wbp_ref), (wba_hbm, wba_ref), (wo_hbm, wo_ref)])
        i = pl.program_id(0)

        @pl.when(i == 0)
        def _():
            halo[...] = jnp.zeros_like(halo)

        xp = xp_ref[...]
        ext = jnp.concatenate([halo[...], xp], axis=0)
        halo[...] = xp[tm - HALO:, :]
        counts = _pool_counts(i * tm, tm)
        for gi in range(len(POOL_WINDOWS)):
            lanes = slice(gi * PG, (gi + 1) * PG)
            win = ext[:, lanes]
            for step in range(gi + 1):
                win = win + pltpu.roll(win, 1 << step, 0)
            pm = (win[HALO:, :] / counts[gi] - xp[:, lanes]).astype(BF16)
            pm_ref[:, lanes] = pm
            p_ref[:, lanes] = (_nn(pm, wg_ref[gi]) * sc_ref[:, lanes]).astype(BF16)
        pb = p_ref[...]
        ob = o_ref[...]
        for j in range(NSH):
            cols = slice(j * (D // NSH), (j + 1) * (D // NSH))
            yp = _nn(pb, wbp_ref[j])
            ys = _nn(ob, wba_ref[j])
            yp_ref[:, cols] = yp.astype(BF16)
            ys_ref[:, cols] = ys.astype(BF16)
            m_ref[:, cols] = (gp_ref[:, cols].astype(F32) * yp + gs_ref[:, cols].astype(F32) * ys).astype(BF16)
        h2_ref[...] = h_ref[...] + _nn(m_ref[...], wo_ref[...])

    return pl.pallas_call(
        body, name="mix_out", grid=(S // tm,),
        in_specs=[_rows(tm, D), _rows(tm, PW), _rows(tm, SBW), _rows(tm, D), _rows(tm, D),
                  _ANY, _fixed((1, PW)), _ANY, _ANY, _ANY],
        out_specs=[_rows(tm, D), _rows(tm, PW), _rows(tm, PW), _rows(tm, D), _rows(tm, D), _rows(tm, D)],
        out_shape=[_sds((S, D), F32), _sds((S, PW), BF16), _sds((S, PW), BF16), _sds((S, D), BF16),
                   _sds((S, D), BF16), _sds((S, D), BF16)],
        scratch_shapes=[pltpu.VMEM((HALO, PW), F32)] + _vmem_like(w_group, w_bp, w_ba, w_out),
        compiler_params=_params(("arbitrary",), 48),
    )(h, xp, o_sb, gp, gs, w_group, scale, w_bp, w_ba, w_out)


def _mix_bwd_out(dh, gp, gs, yp, ys, pm, w_group, scale, w_bp, w_ba, w_out):
    tm = 512
    nt = S // tm

    def body(dh_ref, gp_ref, gs_ref, yp_ref, ys_ref, pm_ref, wg_hbm, sc_ref, wbp_hbm, wba_hbm, wo_hbm,
             dlg_ref, dyp_ref, dys_ref, do_ref, dyg_ref, dxp_ref, dsc_ref, halo, wg_ref, wbp_ref, wba_ref, wo_ref):
        _stage([(wg_hbm, wg_ref), (wbp_hbm, wbp_ref), (wba_hbm, wba_ref), (wo_hbm, wo_ref)])
        step = pl.program_id(0)

        @pl.when(step == 0)
        def _():
            halo[...] = jnp.zeros_like(halo)
            dsc_ref[...] = jnp.zeros_like(dsc_ref)

        dm = _nt(dh_ref[...].astype(BF16), wo_ref[...])
        gp = gp_ref[...].astype(F32)
        gs = gs_ref[...].astype(F32)
        yp = yp_ref[...].astype(F32)
        ys = ys_ref[...].astype(F32)
        dlg_ref[:, :D] = (dm * yp * gp * (1.0 - gp)).astype(BF16)
        dlg_ref[:, D:] = (dm * ys * gs * (1.0 - gs)).astype(BF16)
        dyp_ref[...] = (dm * gp).astype(BF16)
        dys_ref[...] = (dm * gs).astype(BF16)
        dp = jnp.zeros((tm, PW), F32)
        do = jnp.zeros((tm, SBW), F32)
        for j in range(NSH):
            cols = slice(j * (D // NSH), (j + 1) * (D // NSH))
            dp = dp + _nt(dyp_ref[:, cols], wbp_ref[j])
            do = do + _nt(dys_ref[:, cols], wba_ref[j])
        do_ref[...] = do.astype(BF16)
        counts = _pool_counts((nt - 1 - step) * tm, tm)
        dscale = []
        for gi in range(len(POOL_WINDOWS)):
            lanes = slice(gi * PG, (gi + 1) * PG)
            dpg = dp[:, lanes]
            dscale.append(jnp.sum(dpg * _nn(pm_ref[:, lanes], wg_ref[gi]), axis=0, keepdims=True))
            dyg = (dpg * sc_ref[:, lanes]).astype(BF16)
            dyg_ref[:, lanes] = dyg
            dpm = _nt(dyg, wg_ref[gi])
            per = dpm / counts[gi]
            win = jnp.concatenate([per, halo[:, lanes]], axis=0)
            halo[:, lanes] = per[:HALO, :]
            for s in range(gi + 1):
                win = win + pltpu.roll(win, tm + HALO - (1 << s), 0)
            dxp_ref[:, lanes] = (win[:tm, :] - dpm).astype(BF16)
        dsc_ref[...] += jnp.concatenate(dscale, axis=1)

    rev = lambda width: pl.BlockSpec((tm, width), lambda i: (nt - 1 - i, 0))
    return pl.pallas_call(
        body, name="mix_bwd_out", grid=(nt,),
        in_specs=[rev(D), rev(D), rev(D), rev(D), rev(D), rev(PW), _ANY, _fixed((1, PW)), _ANY, _ANY, _ANY],
        out_specs=[rev(2 * D), rev(D), rev(D), rev(SBW), rev(PW), rev(PW), _fixed((1, PW))],
        out_shape=[_sds((S, 2 * D), BF16), _sds((S, D), BF16), _sds((S, D), BF16), _sds((S, SBW), BF16),
                   _sds((S, PW), BF16), _sds((S, PW), BF16), _sds((1, PW), F32)],
        scratch_shapes=[pltpu.VMEM((HALO, PW), F32)] + _vmem_like(w_group, w_bp, w_ba, w_out),
        compiler_params=_params(("arbitrary",), 48),
    )(dh, gp, gs, yp, ys, pm, w_group, scale, w_bp, w_ba, w_out)


def _mix_bwd_in(dh, h, gain, dproj, w_in):
    tm = 512

    def body(dh_ref, h_ref, g_ref, dp_ref, w_hbm, dx_ref, dg_ref, w_ref):
        _stage([(w_hbm, w_ref)])
        du = jnp.zeros((tm, D), F32)
        for j in range(NSH):
            du = du + _nt(dp_ref[:, j * D:(j + 1) * D], w_ref[j])
        r, hr = _rms(h_ref[...])
        dx, dgain = _rms_bwd(du, hr, r, g_ref[...])
        dx_ref[...] = dh_ref[...] + dx

        @pl.when(pl.program_id(0) == 0)
        def _():
            dg_ref[...] = jnp.zeros_like(dg_ref)

        dg_ref[...] += dgain

    return pl.pallas_call(
        body, name="mix_bwd_in", grid=(S // tm,),
        in_specs=[_rows(tm, D), _rows(tm, D), _fixed((1, D)), _rows(tm, 4 * D), _ANY],
        out_specs=[_rows(tm, D), _fixed((1, D))],
        out_shape=[_sds((S, D), F32), _sds((1, D), F32)],
        scratch_shapes=_vmem_like(w_in),
        compiler_params=_params(("arbitrary",), 48),
    )(dh, h, gain, dproj, w_in)


def _wgrad(a, b, nblk, ti, name, out_dtype=BF16):
    ka, n = a.shape[1], b.shape[1]
    ns = n // nblk

    def body(a_ref, b_ref, o_ref):
        o_ref[...] = _tn(a_ref[...].astype(BF16), b_ref[...].astype(BF16)).astype(out_dtype)

    return pl.pallas_call(
        body, name=name, grid=(nblk, ka // ti),
        in_specs=[pl.BlockSpec((S, ti), lambda j, i: (0, i)), pl.BlockSpec((S, ns), lambda j, i: (0, j))],
        out_specs=pl.BlockSpec((None, ti, ns), lambda j, i: (j, i, 0)),
        out_shape=_sds((nblk, ka, ns), out_dtype),
        compiler_params=_params(("arbitrary", "arbitrary"), 56),
    )(a, b)


def _wgrad_groups(pm, dyg):
    def body(a_ref, b_ref, o_ref):
        o_ref[...] = _tn(a_ref[...], b_ref[...])

    col = pl.BlockSpec((S, PG), lambda g: (0, g))
    return pl.pallas_call(
        body, name="wgrad_groups", grid=(PW // PG,),
        in_specs=[col, col], out_specs=pl.BlockSpec((None, PG, PG), lambda g: (g, 0, 0)),
        out_shape=_sds((PW // PG, PG, PG), F32),
        compiler_params=_params(("arbitrary",), 32),
    )(pm, dyg)


def _place():
    x, y, c = lax.axis_index("x"), lax.axis_index("y"), lax.axis_index("c")
    chips = [(1 - x, y), (x, 1 - y), (1 - x, 1 - y)]
    return x, y, c, chips


def _remote(src, dst, ssem, rsem, dev):
    return pltpu.make_async_remote_copy(src_ref=src, dst_ref=dst, send_sem=ssem, recv_sem=rsem,
                                        device_id=dev, device_id_type=MESH)


def _comm_call(body, name, ins, out_shapes, n_sems, aliases=None):
    return pl.pallas_call(
        body, name=name,
        in_specs=[_ANY] * len(ins), out_specs=[_ANY] * len(out_shapes), out_shape=out_shapes,
        scratch_shapes=[pltpu.SemaphoreType.DMA((n_sems,)), pltpu.SemaphoreType.DMA((n_sems,))],
        input_output_aliases=aliases or {},
    )(*ins)


def _cast_into_block(w, me_idx, name):
    rows, cols = w.shape
    tr = _row_block(rows)

    def body(me_ref, w_ref, o_ref):
        o_ref[...] = w_ref[...].astype(BF16)

    return pl.pallas_call(
        body, name=name, out_shape=_sds((NSH, rows, cols), BF16),
        grid_spec=pltpu.PrefetchScalarGridSpec(
            num_scalar_prefetch=1, grid=(rows // tr,),
            in_specs=[pl.BlockSpec((tr, cols), lambda r, me: (r, 0))],
            out_specs=pl.BlockSpec((None, tr, cols), lambda r, me: (me[0], r, 0))),
        compiler_params=_params(("arbitrary",), 32),
    )(me_idx, w)


def _gather_weights(bufs):
    n = len(bufs)

    def body(*refs):
        outs = refs[n:2 * n]
        ssem, rsem = refs[2 * n:]
        x, y, c, chips = _place()
        me, sib = 2 * x + y, (x, y, 1 - c)
        started = []
        for w in range(n):
            half = outs[w].shape[1] // 2
            mine = outs[w].at[me, pl.ds(c * half, half)]
            for k, (px, py) in enumerate(chips):
                cp = _remote(mine, mine, ssem.at[6 * w + k], rsem.at[6 * w + k], (px, py, c))
                cp.start()
                started.append(cp)
        for w in range(n):
            half = outs[w].shape[1] // 2
            for k, (px, py) in enumerate(chips):
                got = outs[w].at[2 * px + py, pl.ds(c * half, half)]
                _remote(got, got, ssem.at[6 * w + k], rsem.at[6 * w + k], (px, py, c)).wait_recv()
                fw = _remote(got, got, ssem.at[6 * w + 3 + k], rsem.at[6 * w + 3 + k], sib)
                fw.start()
                started.append(fw)
        for w in range(n):
            half = outs[w].shape[1] // 2
            for k, (px, py) in enumerate(chips):
                got = outs[w].at[2 * px + py, pl.ds((1 - c) * half, half)]
                _remote(got, got, ssem.at[6 * w + 3 + k], rsem.at[6 * w + 3 + k], sib).wait_recv()
        for cp in started:
            cp.wait_send()

    outs = [_sds(b.shape, b.dtype) for b in bufs]
    return _comm_call(body, "gather_weights", bufs, outs, 6 * n, {w: w for w in range(n)})


def _pair_swap(grads):
    n = len(grads)

    def body(*refs):
        ins, outs = refs[:n], refs[n:2 * n]
        ssem, rsem = refs[2 * n:]
        x, y, c, _ = _place()
        cps = [_remote(ins[w].at[:, 1 - c], outs[w], ssem.at[w], rsem.at[w], (x, y, 1 - c)) for w in range(n)]
        for cp in cps:
            cp.start()
        for cp in cps:
            cp.wait_recv()
        for cp in cps:
            cp.wait_send()

    outs = [_sds((NSH,) + g.shape[2:], g.dtype) for g in grads]
    return _comm_call(body, "pair_swap", grads, outs, n)


def _scatter_to_owner(parts):
    n = len(parts)

    def body(*refs):
        ins, outs = refs[:n], refs[n:2 * n]
        ssem, rsem = refs[2 * n:]
        x, y, c, chips = _place()
        cps = [_remote(ins[w].at[2 * px + py], outs[w].at[k], ssem.at[3 * w + k], rsem.at[3 * w + k], (px, py, c))
               for w in range(n) for k, (px, py) in enumerate(chips)]
        for cp in cps:
            cp.start()
        for cp in cps:
            cp.wait_recv()
        for cp in cps:
            cp.wait_send()

    outs = [_sds((3,) + p.shape[1:], p.dtype) for p in parts]
    return _comm_call(body, "scatter_to_owner", parts, outs, 3 * n)


def _share_halves(bufs):
    n = len(bufs)

    def body(*refs):
        outs = refs[n:2 * n]
        ssem, rsem = refs[2 * n:]
        x, y, c, _ = _place()
        cps = [_remote(outs[w].at[c], outs[w].at[c], ssem.at[w], rsem.at[w], (x, y, 1 - c)) for w in range(n)]
        for cp in cps:
            cp.start()
        for w in range(n):
            got = outs[w].at[1 - c]
            _remote(got, got, ssem.at[w], rsem.at[w], (x, y, 1 - c)).wait_recv()
        for cp in cps:
            cp.wait_send()

    outs = [_sds(b.shape, b.dtype) for b in bufs]
    return _comm_call(body, "share_halves", bufs, outs, n, {w: w for w in range(n)})


def _gather_small(block):
    m_per, n = block.shape

    def body(x_ref, out_ref, ssem, rsem, lsem):
        x, y, c, chips = _place()
        me, sib = (x, y, c), (x, y, 1 - c)

        def rows(px, py, pc):
            return out_ref.at[pl.ds((4 * px + 2 * py + pc) * m_per, m_per), :]

        def copy(k, blk, to, src=None):
            return _remote(rows(*blk) if src is None else src, rows(*blk), ssem.at[k], rsem.at[k], to)

        mine = pltpu.make_async_copy(x_ref, rows(*me), lsem)
        mine.start()
        first = [copy(0, me, sib, src=x_ref)]
        first += [copy(1 + j, me, (*chip, c), src=x_ref) for j, chip in enumerate(chips)]
        for cp in first:
            cp.start()
        passed = [copy(4 + j, (*chip, c), sib) for j, chip in enumerate(chips)]
        for j, chip in enumerate(chips):
            copy(1 + j, (*chip, c), me).wait_recv()
            passed[j].start()
        copy(0, sib, me).wait_recv()
        for j, chip in enumerate(chips):
            copy(4 + j, (*chip, 1 - c), me).wait_recv()
        for cp in first + passed:
            cp.wait_send()
        mine.wait()

    return pl.pallas_call(
        body, name="gather_small", out_shape=_sds((8 * m_per, n), block.dtype),
        in_specs=[_VM], out_specs=_VM,
        scratch_shapes=[pltpu.SemaphoreType.DMA((7,)), pltpu.SemaphoreType.DMA((7,)), pltpu.SemaphoreType.DMA],
    )(block)


def _row_block(rows):
    return max(t for t in range(16, 257, 16) if rows % t == 0)


def _pair_sum(grad, got, c_idx, name):
    _, _, half, cols = grad.shape
    tr = _row_block(half)

    def body(c_ref, a_ref, b_ref, o_ref):
        o_ref[...] = (a_ref[...].astype(F32) + b_ref[...].astype(F32)).astype(BF16)

    return pl.pallas_call(
        body, name=name, out_shape=_sds((NSH, half, cols), BF16),
        grid_spec=pltpu.PrefetchScalarGridSpec(
            num_scalar_prefetch=1, grid=(NSH, half // tr),
            in_specs=[pl.BlockSpec((None, None, tr, cols), lambda j, r, c: (j, c[0], r, 0)),
                      pl.BlockSpec((None, tr, cols), lambda j, r, c: (j, r, 0))],
            out_specs=pl.BlockSpec((None, tr, cols), lambda j, r, c: (j, r, 0))),
        compiler_params=_params(("arbitrary", "arbitrary"), 32),
    )(c_idx, grad, got)


def _chip_sum(own, got, place, name):
    _, half, cols = own.shape
    tr = _row_block(half)

    def body(place_ref, own_ref, got_ref, o_ref):
        acc = own_ref[...].astype(F32)
        for k in range(3):
            acc = acc + got_ref[k].astype(F32)
        o_ref[...] = acc

    return pl.pallas_call(
        body, name=name, out_shape=_sds((2, half, cols), F32),
        grid_spec=pltpu.PrefetchScalarGridSpec(
            num_scalar_prefetch=1, grid=(half // tr,),
            in_specs=[pl.BlockSpec((None, tr, cols), lambda r, p: (p[0], r, 0)),
                      pl.BlockSpec((3, tr, cols), lambda r, p: (0, r, 0))],
            out_specs=pl.BlockSpec((None, tr, cols), lambda r, p: (p[1], r, 0))),
        compiler_params=_params(("arbitrary",), 32),
    )(place, own, got)


def _adamw_math(w, g, m, v):
    m = B1 * m + (1.0 - B1) * g
    v = B2 * v + (1.0 - B2) * (g * g)
    m_hat = m / (1.0 - B1 ** STEP)
    v_hat = v / (1.0 - B2 ** STEP)
    return -LR * (m_hat / (jnp.sqrt(v_hat) + AEPS) + WD * w), m, v


def _adamw(w, g, m, v, name):
    rows, cols = w.shape
    tr = _row_block(rows)

    def body(w_ref, g_ref, m_ref, v_ref, d_ref, nm_ref, nv_ref):
        d_ref[...], nm_ref[...], nv_ref[...] = _adamw_math(w_ref[...], g_ref[...], m_ref[...], v_ref[...])

    blk = pl.BlockSpec((tr, cols), lambda r: (r, 0))
    return pl.pallas_call(
        body, name=name, grid=(rows // tr,), out_shape=[_sds(w.shape, F32)] * 3,
        in_specs=[blk] * 4, out_specs=[blk] * 3,
        compiler_params=_params(("arbitrary",), 32),
    )(w, g, m, v)


def _small_update(gathered, w, m, v):
    rows = w.shape[0]

    def body(ga_ref, w_ref, m_ref, v_ref, g_ref, d_ref, nm_ref, nv_ref):
        g = ga_ref[0:rows, :]
        for dev in range(1, 8):
            g = g + ga_ref[dev * rows:(dev + 1) * rows, :]
        g_ref[...] = g
        d_ref[...], nm_ref[...], nv_ref[...] = _adamw_math(w_ref[...], g, m_ref[...], v_ref[...])

    return pl.pallas_call(
        body, name="small_update", out_shape=[_sds(w.shape, F32)] * 4,
        in_specs=[_VM] * 4, out_specs=[_VM] * 4,
    )(gathered, w, m, v)


SMALL = ("ffn1_norm", "mix_norm", "ffn2_norm", "final_norm", "pool_scale", "pool_w_group")
BIG = ("ffn1_w_gate_up", "ffn1_w_down", "w_in", "w_branch_pool", "w_branch_attn", "w_out",
       "ffn2_w_gate_up", "ffn2_w_down")
ORDER = ("ffn1_norm", "ffn1_w_gate_up", "ffn1_w_down", "mix_norm", "w_in", "pool_w_group", "pool_scale",
         "w_branch_pool", "w_branch_attn", "w_out", "ffn2_norm", "ffn2_w_gate_up", "ffn2_w_down", "final_norm")
SMALL_ROWS = 552


def _pack_small(t):
    parts = []
    for k in SMALL:
        rows = t[k].reshape(-1, 128)
        parts.append(jnp.pad(rows, ((0, -rows.shape[0] % 8), (0, 0))))
    packed = jnp.concatenate(parts, axis=0)
    assert packed.shape == (SMALL_ROWS, 128), packed.shape
    return packed


def _unpack_small(packed, like):
    out, at = {}, 0
    for k in SMALL:
        n = like[k].size // 128
        out[k] = packed[at:at + n].reshape(like[k].shape)
        at += n + (-n % 8)
    return out


def kernel(x, ffn1_norm, ffn1_w_gate_up, ffn1_w_down, mix_norm, w_in, pool_w_group, pool_scale, w_branch_pool, w_branch_attn, w_out, ffn2_norm, ffn2_w_gate_up, ffn2_w_down, final_norm, loss_target, m_ffn1_norm, m_ffn1_w_gate_up, m_ffn1_w_down, m_mix_norm, m_w_in, m_pool_w_group, m_pool_scale, m_w_branch_pool, m_w_branch_attn, m_w_out, m_ffn2_norm, m_ffn2_w_gate_up, m_ffn2_w_down, m_final_norm, v_ffn1_norm, v_ffn1_w_gate_up, v_ffn1_w_down, v_mix_norm, v_w_in, v_pool_w_group, v_pool_scale, v_w_branch_pool, v_w_branch_attn, v_w_out, v_ffn2_norm, v_ffn2_w_gate_up, v_ffn2_w_down, v_final_norm):
    wts = dict(ffn1_norm=ffn1_norm, ffn1_w_gate_up=ffn1_w_gate_up, ffn1_w_down=ffn1_w_down, mix_norm=mix_norm,
               w_in=w_in, pool_w_group=pool_w_group, pool_scale=pool_scale, w_branch_pool=w_branch_pool,
               w_branch_attn=w_branch_attn, w_out=w_out, ffn2_norm=ffn2_norm, ffn2_w_gate_up=ffn2_w_gate_up,
               ffn2_w_down=ffn2_w_down, final_norm=final_norm)
    mom = dict(ffn1_norm=m_ffn1_norm, ffn1_w_gate_up=m_ffn1_w_gate_up, ffn1_w_down=m_ffn1_w_down,
               mix_norm=m_mix_norm, w_in=m_w_in, pool_w_group=m_pool_w_group, pool_scale=m_pool_scale,
               w_branch_pool=m_w_branch_pool, w_branch_attn=m_w_branch_attn, w_out=m_w_out,
               ffn2_norm=m_ffn2_norm, ffn2_w_gate_up=m_ffn2_w_gate_up, ffn2_w_down=m_ffn2_w_down,
               final_norm=m_final_norm)
    var = dict(ffn1_norm=v_ffn1_norm, ffn1_w_gate_up=v_ffn1_w_gate_up, ffn1_w_down=v_ffn1_w_down,
               mix_norm=v_mix_norm, w_in=v_w_in, pool_w_group=v_pool_w_group, pool_scale=v_pool_scale,
               w_branch_pool=v_w_branch_pool, w_branch_attn=v_w_branch_attn, w_out=v_w_out,
               ffn2_norm=v_ffn2_norm, ffn2_w_gate_up=v_ffn2_w_gate_up, ffn2_w_down=v_ffn2_w_down,
               final_norm=v_final_norm)

    c_idx = lax.axis_index("c").astype(jnp.int32).reshape(1)
    me_idx = (2 * lax.axis_index("x") + lax.axis_index("y")).astype(jnp.int32).reshape(1)
    place = jnp.concatenate([me_idx, c_idx])
    full = dict(zip(BIG, _gather_weights([_cast_into_block(wts[k][0], me_idx, "cast_" + k) for k in BIG])))

    loss_row, dx, partial, small_g = _local_step(x[0], loss_target[0], wts, full)

    split = [partial[k].reshape(NSH, 2, partial[k].shape[1] // 2, partial[k].shape[2]) for k in BIG]
    got = _pair_swap(split)
    chip_part = [_pair_sum(split[i], got[i], c_idx, "pair_sum_" + k) for i, k in enumerate(BIG)]
    owned = _scatter_to_owner(chip_part)
    both = _share_halves([_chip_sum(chip_part[i], owned[i], place, "chip_sum_" + k) for i, k in enumerate(BIG)])
    grad = {k: both[i].reshape(wts[k].shape) for i, k in enumerate(BIG)}

    gathered = _gather_small(_pack_small(small_g))
    sg, sd, sm, sv = _small_update(gathered, _pack_small(wts), _pack_small(mom), _pack_small(var))
    grad.update(_unpack_small(sg, wts))
    delta, new_m, new_v = _unpack_small(sd, wts), _unpack_small(sm, wts), _unpack_small(sv, wts)
    for k in BIG:
        shp = wts[k].shape
        d_, m_, v_ = _adamw(wts[k][0], grad[k][0], mom[k][0], var[k][0], "adamw_" + k)
        delta[k], new_m[k], new_v[k] = d_.reshape(shp), m_.reshape(shp), v_.reshape(shp)

    loss = lax.psum(loss_row[0, 0], ("x", "y", "c"))
    return (loss, dx[None], *[grad[k] for k in ORDER], *[delta[k] for k in ORDER],
            *[new_m[k] for k in ORDER], *[new_v[k] for k in ORDER])


def _local_step(x0, tgt, wts, full):
    wgu1, wgu2 = full["ffn1_w_gate_up"], full["ffn2_w_gate_up"]
    wd1, wd2 = full["ffn1_w_down"].reshape(DFF, D), full["ffn2_w_down"].reshape(DFF, D)
    win, wbp, wba = full["w_in"], full["w_branch_pool"], full["w_branch_attn"]
    wout = full["w_out"].reshape(D, D)
    wgrp = wts["pool_w_group"][0].astype(BF16)
    pool_scale = wts["pool_scale"]
    g1, gm, g2, gf = wts["ffn1_norm"], wts["mix_norm"], wts["ffn2_norm"], wts["final_norm"].reshape(1, D)

    h1, n1, gu1 = _ffn_fwd(x0, g1, wgu1, wd1, "ffn1_fwd")
    u, xp, q, k, v, gp, gs = _mix_in(h1, gm, win)
    o_sb, ctot = _attn_fwd(q, k, v)
    h2, pm, p, yp, ys, mm = _mix_out(h1, xp, o_sb, gp, gs, wgrp, pool_scale, wbp, wba, wout)
    h3, n3, gu3 = _ffn_fwd(h2, g2, wgu2, wd2, "ffn2_fwd")
    dh3, loss_row, d_gf = _head(h3, tgt, gf)

    dh2, dgu3, a3, d_g2 = _ffn_bwd(dh3, h2, g2, gu3, wgu2, wd2, "ffn2_bwd")
    dlg, dyp, dys, do_sb, dyg, dxp, d_scale = _mix_bwd_out(dh2, gp, gs, yp, ys, pm, wgrp, pool_scale, wbp, wba, wout)
    dq, dk, dv = _attn_bwd(q, k, v, do_sb, ctot)
    dproj = jnp.concatenate([dxp, dq, dk, dv, dlg], axis=1)
    dh1, d_gm = _mix_bwd_in(dh2, h1, gm, dproj, win)
    dx, dgu1, a1, d_g1 = _ffn_bwd(dh1, x0, g1, gu1, wgu1, wd1, "ffn1_bwd")

    partial = {
        "ffn1_w_gate_up": _wgrad(n1, dgu1, NSH, 512, "wgrad_gu1"),
        "ffn1_w_down": _wgrad(a1, dh1, 1, FFS, "wgrad_d1").reshape(NSH, DFF // NSH, D),
        "w_in": _wgrad(u, dproj, NSH, 512, "wgrad_in"),
        "w_branch_pool": _wgrad(p, dyp, NSH, PW, "wgrad_bp"),
        "w_branch_attn": _wgrad(o_sb, dys, NSH, SBW, "wgrad_ba"),
        "w_out": _wgrad(mm, dh2, 1, 512, "wgrad_out").reshape(NSH, D // NSH, D),
        "ffn2_w_gate_up": _wgrad(n3, dgu3, NSH, 512, "wgrad_gu2"),
        "ffn2_w_down": _wgrad(a3, dh3, 1, FFS, "wgrad_d2").reshape(NSH, DFF // NSH, D),
    }
    small_g = dict(ffn1_norm=d_g1, mix_norm=d_gm, ffn2_norm=d_g2, final_norm=d_gf, pool_scale=d_scale,
                   pool_w_group=_wgrad_groups(pm, dyg))
    return loss_row, dx, partial, small_g
```

```python
import functools

import jax
import jax.numpy as jnp
from jax import lax
from jax.experimental import pallas as pl
from jax.experimental.pallas import tpu as pltpu

F32 = jnp.float32
BF16 = jnp.bfloat16

S = 2048
D = 1024
DFF = 2816
FFS = 2 * DFF // 4
NSH = 4
PW = 512
PG = 128
POOL_WINDOWS = (2, 4, 8, 16)
HALO = 16
SBW = 512
DH = 64
EPS = 1e-6
SCALE = 0.125
TA = 256
MIB = 1024 * 1024

LR, B1, B2, AEPS, WD, STEP = 0.001, 0.9, 0.999, 1e-08, 0.01, 10

_VM = pl.BlockSpec(memory_space=pltpu.VMEM)
_ANY = pl.BlockSpec(memory_space=pl.ANY)
MESH = pl.DeviceIdType.MESH


def _nn(a, b):
    return jnp.dot(a, b, preferred_element_type=F32)


def _nt(a, b):
    return lax.dot_general(a, b, (((1,), (1,)), ((), ())), preferred_element_type=F32)


def _tn(a, b):
    return lax.dot_general(a, b, (((0,), (0,)), ((), ())), preferred_element_type=F32)


def _params(sem, vmem_mib):
    return pltpu.CompilerParams(dimension_semantics=sem, vmem_limit_bytes=vmem_mib * MIB)


def _rows(tm, width):
    return pl.BlockSpec((tm, width), lambda i: (i, 0))


def _fixed(shape):
    return pl.BlockSpec(shape, lambda *_: (0,) * len(shape))


def _sds(shape, dtype):
    return jax.ShapeDtypeStruct(shape, dtype)


def _stage(pairs):
    @pl.when(pl.program_id(0) == 0)
    def _():
        for src, dst in pairs:
            pltpu.sync_copy(src, dst)


def _vmem_like(*arrays):
    return [pltpu.VMEM(a.shape, a.dtype) for a in arrays]


class Exchange:
    def __init__(self, arrays, landing, aliases, n_sems, start, finish):
        self.arrays, self.landing, self.aliases, self.n_sems = list(arrays), list(landing), dict(aliases), n_sems
        self.start, self.finish = start, finish


def _call(body, args, *, name, grid, in_specs, out_specs, out_shape, scratch_shapes=(), compiler_params=None,
          exchange=None):
    if exchange is None:
        return pl.pallas_call(body, name=name, grid=grid, in_specs=in_specs, out_specs=out_specs,
                              out_shape=out_shape, scratch_shapes=list(scratch_shapes),
                              compiler_params=compiler_params)(*args)
    ex = exchange
    n_in, n_out, n_scr = len(in_specs), len(out_specs), len(scratch_shapes)
    na, nl = len(ex.arrays), len(ex.landing)

    def hosted(*refs):
        at = [0]

        def take(n):
            at[0] += n
            return refs[at[0] - n:at[0]]

        k_in, e_in, k_out, e_out, k_scr = take(n_in), take(na), take(n_out), take(nl), take(n_scr)
        ssem, rsem = take(2)
        ids = [pl.program_id(a) for a in range(len(grid))]
        first = functools.reduce(jnp.logical_and, [i == 0 for i in ids])
        last = functools.reduce(jnp.logical_and, [i == g - 1 for i, g in zip(ids, grid)])

        @pl.when(first)
        def _():
            ex.start(e_in, e_out, ssem, rsem)

        body(*k_in, *k_out, *k_scr)

        @pl.when(last)
        def _():
            ex.finish(e_in, e_out, ssem, rsem)

    outs = pl.pallas_call(
        hosted, name=name, grid=grid,
        in_specs=list(in_specs) + [_ANY] * na, out_specs=list(out_specs) + [_ANY] * nl,
        out_shape=list(out_shape) + ex.landing,
        scratch_shapes=list(scratch_shapes) + [pltpu.SemaphoreType.DMA((ex.n_sems,))] * 2,
        input_output_aliases={n_in + i: n_out + j for i, j in ex.aliases.items()},
        compiler_params=compiler_params,
    )(*args, *ex.arrays)
    return outs[:n_out], outs[n_out:]


def _exchange_alone(ex, name):
    def body(*refs):
        na, nl = len(ex.arrays), len(ex.landing)
        ex.start(refs[:na], refs[na:na + nl], refs[-2], refs[-1])
        ex.finish(refs[:na], refs[na:na + nl], refs[-2], refs[-1])

    return pl.pallas_call(
        body, name=name, in_specs=[_ANY] * len(ex.arrays), out_specs=[_ANY] * len(ex.landing),
        out_shape=ex.landing, scratch_shapes=[pltpu.SemaphoreType.DMA((ex.n_sems,))] * 2,
        input_output_aliases=ex.aliases,
    )(*ex.arrays)


def _rms(x):
    r = lax.rsqrt(jnp.mean(x * x, axis=-1, keepdims=True) + EPS)
    return r, x * r


def _rms_bwd(dn, xr, r, gain):
    dng = dn * gain
    dx = r * (dng - xr * jnp.mean(dng * xr, axis=-1, keepdims=True))
    return dx, jnp.sum(dn * xr, axis=0, keepdims=True)


def _ffn_fwd(x, gain, wgu, wd, name, exchange=None):
    tm = 256

    def body(x_ref, g_ref, wgu_hbm, wd_hbm, h_ref, n_ref, gu_ref, wgu_ref, wd_ref):
        _stage([(wgu_hbm, wgu_ref), (wd_hbm, wd_ref)])
        x = x_ref[...]
        _, xr = _rms(x)
        n = (xr * g_ref[...]).astype(BF16)
        n_ref[...] = n
        acc = jnp.zeros((tm, D), F32)
        for j in range(2):
            g = _nn(n, wgu_ref[j])
            u = _nn(n, wgu_ref[2 + j])
            gu_ref[:, j * FFS:(j + 1) * FFS] = g.astype(BF16)
            gu_ref[:, (2 + j) * FFS:(3 + j) * FFS] = u.astype(BF16)
            a = (g * jax.nn.sigmoid(g) * u).astype(BF16)
            acc = acc + _nn(a, wd_ref[j * FFS:(j + 1) * FFS, :])
        h_ref[...] = x + 0.5 * acc

    return _call(
        body, (x, gain, wgu, wd), name=name, grid=(S // tm,),
        in_specs=[_rows(tm, D), _fixed((1, D)), _ANY, _ANY],
        out_specs=[_rows(tm, D), _rows(tm, D), _rows(tm, 4 * FFS)],
        out_shape=[_sds((S, D), F32), _sds((S, D), BF16), _sds((S, 4 * FFS), BF16)],
        scratch_shapes=_vmem_like(wgu, wd),
        compiler_params=_params(("arbitrary",), 56), exchange=exchange)


def _ffn_bwd(dh, x, gain, gu, wgu, wd, name, exchange=None):
    tm = 256

    def body(dh_ref, x_ref, g_ref, gu_ref, wgu_hbm, wd_hbm, dx_ref, dgu_ref, a_ref, dg_ref, wgu_ref, wd_ref):
        _stage([(wgu_hbm, wgu_ref), (wd_hbm, wd_ref)])
        dh = dh_ref[...]
        dhb = dh.astype(BF16)
        dn = jnp.zeros((tm, D), F32)
        for j in range(2):
            g = gu_ref[:, j * FFS:(j + 1) * FFS].astype(F32)
            u = gu_ref[:, (2 + j) * FFS:(3 + j) * FFS].astype(F32)
            da = 0.5 * _nt(dhb, wd_ref[j * FFS:(j + 1) * FFS, :])
            sg = jax.nn.sigmoid(g)
            si = g * sg
            a_ref[:, j * FFS:(j + 1) * FFS] = (0.5 * si * u).astype(BF16)
            dgb = (da * u * (sg * (1.0 + g * (1.0 - sg)))).astype(BF16)
            dub = (da * si).astype(BF16)
            dgu_ref[:, j * FFS:(j + 1) * FFS] = dgb
            dgu_ref[:, (2 + j) * FFS:(3 + j) * FFS] = dub
            dn = dn + _nt(dgb, wgu_ref[j]) + _nt(dub, wgu_ref[2 + j])
        r, xr = _rms(x_ref[...])
        dx, dgain = _rms_bwd(dn, xr, r, g_ref[...])
        dx_ref[...] = dh + dx

        @pl.when(pl.program_id(0) == 0)
        def _():
            dg_ref[...] = jnp.zeros_like(dg_ref)

        dg_ref[...] += dgain

    return _call(
        body, (dh, x, gain, gu, wgu, wd), name=name, grid=(S // tm,),
        in_specs=[_rows(tm, D), _rows(tm, D), _fixed((1, D)), _rows(tm, 4 * FFS), _ANY, _ANY],
        out_specs=[_rows(tm, D), _rows(tm, 4 * FFS), _rows(tm, DFF), _fixed((1, D))],
        out_shape=[_sds((S, D), F32), _sds((S, 4 * FFS), BF16), _sds((S, DFF), BF16), _sds((1, D), F32)],
        scratch_shapes=_vmem_like(wgu, wd),
        compiler_params=_params(("arbitrary",), 56), exchange=exchange)


def _head(h, target, gain):
    tm = 512

    def body(h_ref, t_ref, g_ref, dh_ref, loss_ref, dg_ref):
        gain = g_ref[...]
        r, hr = _rms(h_ref[...])
        err = hr * gain - t_ref[...]
        dy = err * (1.0 / D)
        dh, dgain = _rms_bwd(dy, hr, r, gain)
        dh_ref[...] = dh

        @pl.when(pl.program_id(0) == 0)
        def _():
            dg_ref[...] = jnp.zeros_like(dg_ref)
            loss_ref[...] = jnp.zeros_like(loss_ref)

        dg_ref[...] += dgain
        loss_ref[...] += jnp.full((1, 128), (0.5 / D) * jnp.sum(err * err), F32)

    return pl.pallas_call(
        body, name="head", grid=(S // tm,),
        in_specs=[_rows(tm, D), _rows(tm, D), _fixed((1, D))],
        out_specs=[_rows(tm, D), _fixed((1, 128)), _fixed((1, D))],
        out_shape=[_sds((S, D), F32), _sds((1, 128), F32), _sds((1, D), F32)],
        compiler_params=_params(("arbitrary",), 40),
    )(h, target, gain)


def _mix_in(h, gain, w_in):
    tm = 512

    def body(h_ref, g_ref, w_hbm, u_ref, xp_ref, q_ref, k_ref, v_ref, gp_ref, gs_ref, w_ref):
        _stage([(w_hbm, w_ref)])
        _, hr = _rms(h_ref[...])
        u = (hr * g_ref[...]).astype(BF16)
        u_ref[...] = u
        p0 = _nn(u, w_ref[0])
        xp_ref[...] = p0[:, :PW]
        q_ref[...] = p0[:, PW:].astype(BF16)
        p1 = _nn(u, w_ref[1])
        k_ref[...] = p1[:, :SBW].astype(BF16)
        v_ref[...] = p1[:, SBW:].astype(BF16)
        gp_ref[...] = jax.nn.sigmoid(_nn(u, w_ref[2])).astype(BF16)
        gs_ref[...] = jax.nn.sigmoid(_nn(u, w_ref[3])).astype(BF16)

    return pl.pallas_call(
        body, name="mix_in", grid=(S // tm,),
        in_specs=[_rows(tm, D), _fixed((1, D)), _ANY],
        out_specs=[_rows(tm, D), _rows(tm, PW), _rows(tm, SBW), _rows(tm, SBW), _rows(tm, SBW),
                   _rows(tm, D), _rows(tm, D)],
        out_shape=[_sds((S, D), BF16), _sds((S, PW), F32), _sds((S, SBW), BF16), _sds((S, SBW), BF16),
                   _sds((S, SBW), BF16), _sds((S, D), BF16), _sds((S, D), BF16)],
        scratch_shapes=_vmem_like(w_in),
        compiler_params=_params(("arbitrary",), 48),
    )(h, gain, w_in)


def _hilo_dot(x, tri):
    hi = x.astype(BF16)
    lo = (x - hi.astype(F32)).astype(BF16)
    return _nn(hi, tri) + _nn(lo, tri)


def _log_terms(qm, kj):
    z = _nt(qm, kj) * SCALE
    e = jnp.exp(-jnp.abs(z))
    lb = jnp.minimum(z, 0.0) - jnp.log(1.0 + e)
    return z, e, lb, lb - z


def _head_masks():
    lane = lax.broadcasted_iota(jnp.int32, (1, 2 * DH), 1)
    return (lane < DH, lane >= DH)


def _attn_fwd(q, k, v, exchange=None):
    T = TA

    def body(q_ref, k_ref, v_ref, o_ref, c_ref):
        i = pl.program_id(1)
        row = lax.broadcasted_iota(jnp.int32, (T, T), 0)
        col = lax.broadcasted_iota(jnp.int32, (T, T), 1)
        after = (row > col).astype(BF16)
        causal = col < row
        q2 = q_ref[...]
        o_acc = jnp.zeros((T, 2 * DH), F32)
        c_out = jnp.zeros((T, 2 * DH), F32)
        for hm in _head_masks():
            qm = jnp.where(hm, q2, jnp.zeros_like(q2))

            def block(j, carry, o, diag):
                kj = k_ref[pl.ds(pl.multiple_of(j * T, T), T), :]
                vj = v_ref[pl.ds(pl.multiple_of(j * T, T), T), :]
                _, _, lb, l1m = _log_terms(qm, kj)
                if diag:
                    l1m = jnp.where(causal, l1m, 0.0)
                a = jnp.exp(lb + _hilo_dot(l1m, after) + carry)
                if diag:
                    a = jnp.where(causal, a, 0.0)
                o = o + _nn(a.astype(BF16), jnp.where(hm, vj, jnp.zeros_like(vj)))
                return carry + jnp.sum(l1m, axis=1, keepdims=True), o

            carry, o_acc = block(i, jnp.zeros((T, 1), F32), o_acc, True)
            carry, o_acc = lax.fori_loop(
                0, i, lambda jj, c: block(i - 1 - jj, c[0], c[1], False), (carry, o_acc))
            c_out = c_out + jnp.where(hm, carry, 0.0)
        o_ref[...] = o_acc.astype(BF16)
        c_ref[...] = c_out

    blk = pl.BlockSpec((T, 2 * DH), lambda p, i: (i, p))
    full = pl.BlockSpec((S, 2 * DH), lambda p, i: (0, p))
    return _call(
        body, (q, k, v), name="attn_fwd", grid=(SBW // (2 * DH), S // T),
        in_specs=[blk, full, full], out_specs=[blk, blk],
        out_shape=[_sds((S, SBW), BF16), _sds((S, SBW), F32)],
        compiler_params=_params(("arbitrary", "arbitrary"), 40), exchange=exchange)


def _attn_bwd(q, k, v, do, ctot, exchange=None):
    T = TA
    nq = S // T

    def body(q_ref, k_ref, v_ref, do_ref, c_ref, dq_ref, dk_ref, dv_ref, dk_acc, dv_acc):
        i = pl.program_id(1)

        @pl.when(i == 0)
        def _():
            dk_acc[...] = jnp.zeros_like(dk_acc)
            dv_acc[...] = jnp.zeros_like(dv_acc)

        row = lax.broadcasted_iota(jnp.int32, (T, T), 0)
        col = lax.broadcasted_iota(jnp.int32, (T, T), 1)
        upto = (row <= col).astype(BF16)
        before = (row < col).astype(BF16)
        causal = col < row
        q2 = q_ref[...]
        do2 = do_ref[...]
        dq = jnp.zeros((T, 2 * DH), F32)
        for h, hm in enumerate(_head_masks()):
            qm = jnp.where(hm, q2, jnp.zeros_like(q2))
            dom = jnp.where(hm, do2, jnp.zeros_like(do2))
            ctot = c_ref[:, h * DH:h * DH + 1]

            def block(j, cl, cp, dq, diag):
                rows = pl.ds(pl.multiple_of(j * T, T), T)
                kj = k_ref[rows, :]
                vj = v_ref[rows, :]
                z, e, lb, l1m = _log_terms(qm, kj)
                if diag:
                    l1m = jnp.where(causal, l1m, 0.0)
                a = jnp.exp(lb + (ctot - cl) - _hilo_dot(l1m, upto))
                if diag:
                    a = jnp.where(causal, a, 0.0)
                dl = _nt(dom, vj) * a
                pex = _hilo_dot(dl, before) + cp
                rinv = 1.0 / (1.0 + e)
                pos = z >= 0.0
                beta = jnp.where(pos, rinv, e * rinv)
                omb = jnp.where(pos, e * rinv, rinv)
                dz = (dl * omb - pex * beta) * SCALE
                if diag:
                    dz = jnp.where(causal, dz, 0.0)
                dzb = dz.astype(BF16)
                dq = dq + _nn(dzb, jnp.where(hm, kj, jnp.zeros_like(kj)))
                dk_acc[rows, :] += _tn(dzb, qm)
                dv_acc[rows, :] += _tn(a.astype(BF16), dom)
                return (cl + jnp.sum(l1m, axis=1, keepdims=True),
                        cp + jnp.sum(dl, axis=1, keepdims=True), dq)

            zero = jnp.zeros((T, 1), F32)
            cl, cp, dq = lax.fori_loop(0, i, lambda j, c: block(j, c[0], c[1], c[2], False), (zero, zero, dq))
            _, _, dq = block(i, cl, cp, dq, True)
        dq_ref[...] = dq.astype(BF16)

        @pl.when(i == nq - 1)
        def _():
            dk_ref[...] = dk_acc[...].astype(BF16)
            dv_ref[...] = dv_acc[...].astype(BF16)

    blk = pl.BlockSpec((T, 2 * DH), lambda p, i: (i, p))
    full = pl.BlockSpec((S, 2 * DH), lambda p, i: (0, p))
    return _call(
        body, (q, k, v, do, ctot), name="attn_bwd", grid=(SBW // (2 * DH), nq),
        in_specs=[blk, full, full, blk, blk], out_specs=[blk, full, full],
        out_shape=[_sds((S, SBW), BF16), _sds((S, SBW), BF16), _sds((S, SBW), BF16)],
        scratch_shapes=[pltpu.VMEM((S, 2 * DH), F32), pltpu.VMEM((S, 2 * DH), F32)],
        compiler_params=_params(("arbitrary", "arbitrary"), 40), exchange=exchange)


def _pool_counts(first_row, tm):
    pos = first_row + lax.broadcasted_iota(jnp.int32, (tm, 1), 0)
    return [jnp.minimum(pos + 1, w).astype(F32) for w in POOL_WINDOWS]


def _mix_out(h, xp, o_sb, gp, gs, w_group, scale, w_bp, w_ba, w_out):
    tm = 512

    def body(h_ref, xp_ref, o_ref, gp_ref, gs_ref, wg_hbm, sc_ref, wbp_hbm, wba_hbm, wo_hbm,
             h2_ref, pm_ref, p_ref, yp_ref, ys_ref, m_ref, halo, wg_ref, wbp_ref, wba_ref, wo_ref):
        _stage([(wg_hbm, wg_ref), (wbp_hbm, wbp_ref), (wba_hbm, wba_ref), (wo_hbm, wo_ref)])
        i = pl.program_id(0)

        @pl.when(i == 0)
        def _():
            halo[...] = jnp.zeros_like(halo)

        xp = xp_ref[...]
        ext = jnp.concatenate([halo[...], xp], axis=0)
        halo[...] = xp[tm - HALO:, :]
        counts = _pool_counts(i * tm, tm)
        for gi in range(len(POOL_WINDOWS)):
            lanes = slice(gi * PG, (gi + 1) * PG)
            win = ext[:, lanes]
            for step in range(gi + 1):
                win = win + pltpu.roll(win, 1 << step, 0)
            pm = (win[HALO:, :] / counts[gi] - xp[:, lanes]).astype(BF16)
            pm_ref[:, lanes] = pm
            p_ref[:, lanes] = (_nn(pm, wg_ref[gi]) * sc_ref[:, lanes]).astype(BF16)
        pb = p_ref[...]
        ob = o_ref[...]
        for j in range(NSH):
            cols = slice(j * (D // NSH), (j + 1) * (D // NSH))
            yp = _nn(pb, wbp_ref[j])
            ys = _nn(ob, wba_ref[j])
            yp_ref[:, cols] = yp.astype(BF16)
            ys_ref[:, cols] = ys.astype(BF16)
            m_ref[:, cols] = (gp_ref[:, cols].astype(F32) * yp + gs_ref[:, cols].astype(F32) * ys).astype(BF16)
        h2_ref[...] = h_ref[...] + _nn(m_ref[...], wo_ref[...])

    return pl.pallas_call(
        body, name="mix_out", grid=(S // tm,),
        in_specs=[_rows(tm, D), _rows(tm, PW), _rows(tm, SBW), _rows(tm, D), _rows(tm, D),
                  _ANY, _fixed((1, PW)), _ANY, _ANY, _ANY],
        out_specs=[_rows(tm, D), _rows(tm, PW), _rows(tm, PW), _rows(tm, D), _rows(tm, D), _rows(tm, D)],
        out_shape=[_sds((S, D), F32), _sds((S, PW), BF16), _sds((S, PW), BF16), _sds((S, D), BF16),
                   _sds((S, D), BF16), _sds((S, D), BF16)],
        scratch_shapes=[pltpu.VMEM((HALO, PW), F32)] + _vmem_like(w_group, w_bp, w_ba, w_out),
        compiler_params=_params(("arbitrary",), 48),
    )(h, xp, o_sb, gp, gs, w_group, scale, w_bp, w_ba, w_out)


def _mix_bwd_out(dh, gp, gs, yp, ys, pm, w_group, scale, w_bp, w_ba, w_out, exchange=None):
    tm = 512
    nt = S // tm

    def body(dh_ref, gp_ref, gs_ref, yp_ref, ys_ref, pm_ref, wg_hbm, sc_ref, wbp_hbm, wba_hbm, wo_hbm,
             dlg_ref, dyp_ref, dys_ref, do_ref, dyg_ref, dxp_ref, dsc_ref, halo, wg_ref, wbp_ref, wba_ref, wo_ref):
        _stage([(wg_hbm, wg_ref), (wbp_hbm, wbp_ref), (wba_hbm, wba_ref), (wo_hbm, wo_ref)])
        step = pl.program_id(0)

        @pl.when(step == 0)
        def _():
            halo[...] = jnp.zeros_like(halo)
            dsc_ref[...] = jnp.zeros_like(dsc_ref)

        dm = _nt(dh_ref[...].astype(BF16), wo_ref[...])
        gp = gp_ref[...].astype(F32)
        gs = gs_ref[...].astype(F32)
        yp = yp_ref[...].astype(F32)
        ys = ys_ref[...].astype(F32)
        dlg_ref[:, :D] = (dm * yp * gp * (1.0 - gp)).astype(BF16)
        dlg_ref[:, D:] = (dm * ys * gs * (1.0 - gs)).astype(BF16)
        dyp_ref[...] = (dm * gp).astype(BF16)
        dys_ref[...] = (dm * gs).astype(BF16)
        dp = jnp.zeros((tm, PW), F32)
        do = jnp.zeros((tm, SBW), F32)
        for j in range(NSH):
            cols = slice(j * (D // NSH), (j + 1) * (D // NSH))
            dp = dp + _nt(dyp_ref[:, cols], wbp_ref[j])
            do = do + _nt(dys_ref[:, cols], wba_ref[j])
        do_ref[...] = do.astype(BF16)
        counts = _pool_counts((nt - 1 - step) * tm, tm)
        dscale = []
        for gi in range(len(POOL_WINDOWS)):
            lanes = slice(gi * PG, (gi + 1) * PG)
            dpg = dp[:, lanes]
            dscale.append(jnp.sum(dpg * _nn(pm_ref[:, lanes], wg_ref[gi]), axis=0, keepdims=True))
            dyg = (dpg * sc_ref[:, lanes]).astype(BF16)
            dyg_ref[:, lanes] = dyg
            dpm = _nt(dyg, wg_ref[gi])
            per = dpm / counts[gi]
            win = jnp.concatenate([per, halo[:, lanes]], axis=0)
            halo[:, lanes] = per[:HALO, :]
            for s in range(gi + 1):
                win = win + pltpu.roll(win, tm + HALO - (1 << s), 0)
            dxp_ref[:, lanes] = (win[:tm, :] - dpm).astype(BF16)
        dsc_ref[...] += jnp.concatenate(dscale, axis=1)

    rev = lambda width: pl.BlockSpec((tm, width), lambda i: (nt - 1 - i, 0))
    return _call(
        body, (dh, gp, gs, yp, ys, pm, w_group, scale, w_bp, w_ba, w_out), name="mix_bwd_out", grid=(nt,),
        in_specs=[rev(D), rev(D), rev(D), rev(D), rev(D), rev(PW), _ANY, _fixed((1, PW)), _ANY, _ANY, _ANY],
        out_specs=[rev(2 * D), rev(D), rev(D), rev(SBW), rev(PW), rev(PW), _fixed((1, PW))],
        out_shape=[_sds((S, 2 * D), BF16), _sds((S, D), BF16), _sds((S, D), BF16), _sds((S, SBW), BF16),
                   _sds((S, PW), BF16), _sds((S, PW), BF16), _sds((1, PW), F32)],
        scratch_shapes=[pltpu.VMEM((HALO, PW), F32)] + _vmem_like(w_group, w_bp, w_ba, w_out),
        compiler_params=_params(("arbitrary",), 48), exchange=exchange)


def _mix_bwd_in(dh, h, gain, dproj, w_in, exchange=None):
    tm = 512

    def body(dh_ref, h_ref, g_ref, dp_ref, w_hbm, dx_ref, dg_ref, w_ref):
        _stage([(w_hbm, w_ref)])
        du = jnp.zeros((tm, D), F32)
        for j in range(NSH):
            du = du + _nt(dp_ref[:, j * D:(j + 1) * D], w_ref[j])
        r, hr = _rms(h_ref[...])
        dx, dgain = _rms_bwd(du, hr, r, g_ref[...])
        dx_ref[...] = dh_ref[...] + dx

        @pl.when(pl.program_id(0) == 0)
        def _():
            dg_ref[...] = jnp.zeros_like(dg_ref)

        dg_ref[...] += dgain

    return _call(
        body, (dh, h, gain, dproj, w_in), name="mix_bwd_in", grid=(S // tm,),
        in_specs=[_rows(tm, D), _rows(tm, D), _fixed((1, D)), _rows(tm, 4 * D), _ANY],
        out_specs=[_rows(tm, D), _fixed((1, D))],
        out_shape=[_sds((S, D), F32), _sds((1, D), F32)],
        scratch_shapes=_vmem_like(w_in),
        compiler_params=_params(("arbitrary",), 48), exchange=exchange)


def _wgrad(a, b, nblk, ti, name, out_dtype=BF16, exchange=None):
    ka, n = a.shape[1], b.shape[1]
    ns = n // nblk

    def body(a_ref, b_ref, o_ref):
        o_ref[...] = _tn(a_ref[...].astype(BF16), b_ref[...].astype(BF16)).astype(out_dtype)

    res = _call(
        body, (a, b), name=name, grid=(nblk, ka // ti),
        in_specs=[pl.BlockSpec((S, ti), lambda j, i: (0, i)), pl.BlockSpec((S, ns), lambda j, i: (0, j))],
        out_specs=[pl.BlockSpec((None, ti, ns), lambda j, i: (j, i, 0))],
        out_shape=[_sds((nblk, ka, ns), out_dtype)],
        compiler_params=_params(("arbitrary", "arbitrary"), 56), exchange=exchange)
    return res[0] if exchange is None else (res[0][0], res[1])


def _wgrad_groups(pm, dyg):
    def body(a_ref, b_ref, o_ref):
        o_ref[...] = _tn(a_ref[...], b_ref[...])

    col = pl.BlockSpec((S, PG), lambda g: (0, g))
    return pl.pallas_call(
        body, name="wgrad_groups", grid=(PW // PG,),
        in_specs=[col, col], out_specs=pl.BlockSpec((None, PG, PG), lambda g: (g, 0, 0)),
        out_shape=_sds((PW // PG, PG, PG), F32),
        compiler_params=_params(("arbitrary",), 32),
    )(pm, dyg)


def _place():
    x, y, c = lax.axis_index("x"), lax.axis_index("y"), lax.axis_index("c")
    chips = [(1 - x, y), (x, 1 - y), (1 - x, 1 - y)]
    return x, y, c, chips


def _remote(src, dst, ssem, rsem, dev):
    return pltpu.make_async_remote_copy(src_ref=src, dst_ref=dst, send_sem=ssem, recv_sem=rsem,
                                        device_id=dev, device_id_type=MESH)


def _cast_into_block(w, me_idx, name):
    rows, cols = w.shape
    tr = _row_block(rows)

    def body(me_ref, w_ref, o_ref):
        o_ref[...] = w_ref[...].astype(BF16)

    return pl.pallas_call(
        body, name=name, out_shape=_sds((NSH, rows, cols), BF16),
        grid_spec=pltpu.PrefetchScalarGridSpec(
            num_scalar_prefetch=1, grid=(rows // tr,),
            in_specs=[pl.BlockSpec((tr, cols), lambda r, me: (r, 0))],
            out_specs=pl.BlockSpec((None, tr, cols), lambda r, me: (me[0], r, 0))),
        compiler_params=_params(("arbitrary",), 32),
    )(me_idx, w)


def _ex_gather(bufs):
    n = len(bufs)

    def copies(outs, ssem, rsem):
        x, y, c, chips = _place()
        me, sib = 2 * x + y, (x, y, 1 - c)
        first, relay, last = [], [], []
        for w in range(n):
            half = outs[w].shape[1] // 2
            mine = outs[w].at[me, pl.ds(c * half, half)]
            for k, (px, py) in enumerate(chips):
                sems = (ssem.at[6 * w + k], rsem.at[6 * w + k])
                sib_sems = (ssem.at[6 * w + 3 + k], rsem.at[6 * w + 3 + k])
                first.append(_remote(mine, mine, *sems, (px, py, c)))
                got = outs[w].at[2 * px + py, pl.ds(c * half, half)]
                relay.append((_remote(got, got, *sems, (px, py, c)), _remote(got, got, *sib_sems, sib)))
                theirs = outs[w].at[2 * px + py, pl.ds((1 - c) * half, half)]
                last.append(_remote(theirs, theirs, *sib_sems, sib))
        return first, relay, last

    def start(ins, outs, ssem, rsem):
        for cp in copies(outs, ssem, rsem)[0]:
            cp.start()

    def finish(ins, outs, ssem, rsem):
        first, relay, last = copies(outs, ssem, rsem)
        for arrived, onward in relay:
            arrived.wait_recv()
            onward.start()
        for cp in last:
            cp.wait_recv()
        for cp in first:
            cp.wait_send()
        for _, onward in relay:
            onward.wait_send()

    return Exchange(bufs, [_sds(b.shape, b.dtype) for b in bufs], {w: w for w in range(n)}, 6 * n, start, finish)


def _simple_exchange(arrays, landing, aliases, make_copies):
    def start(ins, outs, ssem, rsem):
        for cp, _ in make_copies(ins, outs, ssem, rsem):
            cp.start()

    def finish(ins, outs, ssem, rsem):
        cps = make_copies(ins, outs, ssem, rsem)
        for _, landed in cps:
            landed.wait_recv()
        for cp, _ in cps:
            cp.wait_send()

    return Exchange(arrays, landing, aliases, len(arrays) * 3, start, finish)


def _ex_pair_swap(grads):
    def make(ins, outs, ssem, rsem):
        x, y, c, _ = _place()
        cps = [_remote(ins[w].at[:, 1 - c], outs[w], ssem.at[w], rsem.at[w], (x, y, 1 - c))
               for w in range(len(grads))]
        return [(cp, cp) for cp in cps]

    return _simple_exchange(grads, [_sds((NSH,) + g.shape[2:], g.dtype) for g in grads], {}, make)


def _ex_scatter(parts):
    def make(ins, outs, ssem, rsem):
        x, y, c, chips = _place()
        out = []
        for w in range(len(parts)):
            for k, (px, py) in enumerate(chips):
                sems = (ssem.at[3 * w + k], rsem.at[3 * w + k])
                out.append((_remote(ins[w].at[2 * px + py], outs[w].at[k], *sems, (px, py, c)),
                            _remote(outs[w].at[k], outs[w].at[k], *sems, (px, py, c))))
        return out

    return _simple_exchange(parts, [_sds((3,) + p.shape[1:], p.dtype) for p in parts], {}, make)


def _ex_share(bufs):
    def make(ins, outs, ssem, rsem):
        x, y, c, _ = _place()
        sib = (x, y, 1 - c)
        return [(_remote(outs[w].at[c], outs[w].at[c], ssem.at[w], rsem.at[w], sib),
                 _remote(outs[w].at[1 - c], outs[w].at[1 - c], ssem.at[w], rsem.at[w], sib))
                for w in range(len(bufs))]

    return _simple_exchange(bufs, [_sds(b.shape, b.dtype) for b in bufs], {w: w for w in range(len(bufs))}, make)


def _gather_small(block):
    m_per, n = block.shape

    def body(x_ref, out_ref, ssem, rsem, lsem):
        x, y, c, chips = _place()
        me, sib = (x, y, c), (x, y, 1 - c)

        def rows(px, py, pc):
            return out_ref.at[pl.ds((4 * px + 2 * py + pc) * m_per, m_per), :]

        def copy(k, blk, to, src=None):
            return _remote(rows(*blk) if src is None else src, rows(*blk), ssem.at[k], rsem.at[k], to)

        mine = pltpu.make_async_copy(x_ref, rows(*me), lsem)
        mine.start()
        first = [copy(0, me, sib, src=x_ref)]
        first += [copy(1 + j, me, (*chip, c), src=x_ref) for j, chip in enumerate(chips)]
        for cp in first:
            cp.start()
        passed = [copy(4 + j, (*chip, c), sib) for j, chip in enumerate(chips)]
        for j, chip in enumerate(chips):
            copy(1 + j, (*chip, c), me).wait_recv()
            passed[j].start()
        copy(0, sib, me).wait_recv()
        for j, chip in enumerate(chips):
            copy(4 + j, (*chip, 1 - c), me).wait_recv()
        for cp in first + passed:
            cp.wait_send()
        mine.wait()

    return pl.pallas_call(
        body, name="gather_small", out_shape=_sds((8 * m_per, n), block.dtype),
        in_specs=[_VM], out_specs=_VM,
        scratch_shapes=[pltpu.SemaphoreType.DMA((7,)), pltpu.SemaphoreType.DMA((7,)), pltpu.SemaphoreType.DMA],
    )(block)


def _row_block(rows):
    return max(t for t in range(16, 257, 16) if rows % t == 0)


def _pair_sum(grad, got, c_idx, name):
    _, _, half, cols = grad.shape
    tr = _row_block(half)

    def body(c_ref, a_ref, b_ref, o_ref):
        o_ref[...] = (a_ref[...].astype(F32) + b_ref[...].astype(F32)).astype(BF16)

    return pl.pallas_call(
        body, name=name, out_shape=_sds((NSH, half, cols), BF16),
        grid_spec=pltpu.PrefetchScalarGridSpec(
            num_scalar_prefetch=1, grid=(NSH, half // tr),
            in_specs=[pl.BlockSpec((None, None, tr, cols), lambda j, r, c: (j, c[0], r, 0)),
                      pl.BlockSpec((None, tr, cols), lambda j, r, c: (j, r, 0))],
            out_specs=pl.BlockSpec((None, tr, cols), lambda j, r, c: (j, r, 0))),
        compiler_params=_params(("arbitrary", "arbitrary"), 32),
    )(c_idx, grad, got)


def _chip_sum(own, got, place, name):
    _, half, cols = own.shape
    tr = _row_block(half)

    def body(place_ref, own_ref, got_ref, o_ref):
        acc = own_ref[...].astype(F32)
        for k in range(3):
            acc = acc + got_ref[k].astype(F32)
        o_ref[...] = acc

    return pl.pallas_call(
        body, name=name, out_shape=_sds((2, half, cols), F32),
        grid_spec=pltpu.PrefetchScalarGridSpec(
            num_scalar_prefetch=1, grid=(half // tr,),
            in_specs=[pl.BlockSpec((None, tr, cols), lambda r, p: (p[0], r, 0)),
                      pl.BlockSpec((3, tr, cols), lambda r, p: (0, r, 0))],
            out_specs=pl.BlockSpec((None, tr, cols), lambda r, p: (p[1], r, 0))),
        compiler_params=_params(("arbitrary",), 32),
    )(place, own, got)


def _adamw_math(w, g, m, v):
    m = B1 * m + (1.0 - B1) * g
    v = B2 * v + (1.0 - B2) * (g * g)
    m_hat = m / (1.0 - B1 ** STEP)
    v_hat = v / (1.0 - B2 ** STEP)
    return -LR * (m_hat / (jnp.sqrt(v_hat) + AEPS) + WD * w), m, v


def _adamw(w, g, m, v, name, exchange=None):
    rows, cols = w.shape
    tr = _row_block(rows)

    def body(w_ref, g_ref, m_ref, v_ref, d_ref, nm_ref, nv_ref):
        d_ref[...], nm_ref[...], nv_ref[...] = _adamw_math(w_ref[...], g_ref[...], m_ref[...], v_ref[...])

    blk = pl.BlockSpec((tr, cols), lambda r: (r, 0))
    return _call(
        body, (w, g, m, v), name=name, grid=(rows // tr,), out_shape=[_sds(w.shape, F32)] * 3,
        in_specs=[blk] * 4, out_specs=[blk] * 3,
        compiler_params=_params(("arbitrary",), 32), exchange=exchange)


def _small_update(gathered, w, m, v):
    rows = w.shape[0]

    def body(ga_ref, w_ref, m_ref, v_ref, g_ref, d_ref, nm_ref, nv_ref):
        g = ga_ref[0:rows, :]
        for dev in range(1, 8):
            g = g + ga_ref[dev * rows:(dev + 1) * rows, :]
        g_ref[...] = g
        d_ref[...], nm_ref[...], nv_ref[...] = _adamw_math(w_ref[...], g, m_ref[...], v_ref[...])

    return pl.pallas_call(
        body, name="small_update", out_shape=[_sds(w.shape, F32)] * 4,
        in_specs=[_VM] * 4, out_specs=[_VM] * 4,
    )(gathered, w, m, v)


SMALL = ("ffn1_norm", "mix_norm", "ffn2_norm", "final_norm", "pool_scale", "pool_w_group", "loss")
BIG = ("ffn1_w_gate_up", "ffn1_w_down", "w_in", "w_branch_pool", "w_branch_attn", "w_out",
       "ffn2_w_gate_up", "ffn2_w_down")
ORDER = ("ffn1_norm", "ffn1_w_gate_up", "ffn1_w_down", "mix_norm", "w_in", "pool_w_group", "pool_scale",
         "w_branch_pool", "w_branch_attn", "w_out", "ffn2_norm", "ffn2_w_gate_up", "ffn2_w_down", "final_norm")
SMALL_ROWS = 560


def _pack_small(t):
    parts = []
    for k in SMALL:
        rows = t[k].reshape(-1, 128) if k in t else jnp.zeros((1, 128), F32)
        parts.append(jnp.pad(rows, ((0, -rows.shape[0] % 8), (0, 0))))
    packed = jnp.concatenate(parts, axis=0)
    assert packed.shape == (SMALL_ROWS, 128), packed.shape
    return packed


def _unpack_small(packed, like):
    out, at = {}, 0
    for k in SMALL:
        n = like[k].size // 128 if k in like else 1
        out[k] = packed[at:at + n].reshape(like[k].shape) if k in like else packed[at, 0]
        at += n + (-n % 8)
    return out


def _halves(g):
    return g.reshape(NSH, 2, g.shape[1] // 2, g.shape[2])


def kernel(x, ffn1_norm, ffn1_w_gate_up, ffn1_w_down, mix_norm, w_in, pool_w_group, pool_scale, w_branch_pool, w_branch_attn, w_out, ffn2_norm, ffn2_w_gate_up, ffn2_w_down, final_norm, loss_target, m_ffn1_norm, m_ffn1_w_gate_up, m_ffn1_w_down, m_mix_norm, m_w_in, m_pool_w_group, m_pool_scale, m_w_branch_pool, m_w_branch_attn, m_w_out, m_ffn2_norm, m_ffn2_w_gate_up, m_ffn2_w_down, m_final_norm, v_ffn1_norm, v_ffn1_w_gate_up, v_ffn1_w_down, v_mix_norm, v_w_in, v_pool_w_group, v_pool_scale, v_w_branch_pool, v_w_branch_attn, v_w_out, v_ffn2_norm, v_ffn2_w_gate_up, v_ffn2_w_down, v_final_norm):
    wts = dict(ffn1_norm=ffn1_norm, ffn1_w_gate_up=ffn1_w_gate_up, ffn1_w_down=ffn1_w_down, mix_norm=mix_norm,
               w_in=w_in, pool_w_group=pool_w_group, pool_scale=pool_scale, w_branch_pool=w_branch_pool,
               w_branch_attn=w_branch_attn, w_out=w_out, ffn2_norm=ffn2_norm, ffn2_w_gate_up=ffn2_w_gate_up,
               ffn2_w_down=ffn2_w_down, final_norm=final_norm)
    mom = dict(ffn1_norm=m_ffn1_norm, ffn1_w_gate_up=m_ffn1_w_gate_up, ffn1_w_down=m_ffn1_w_down,
               mix_norm=m_mix_norm, w_in=m_w_in, pool_w_group=m_pool_w_group, pool_scale=m_pool_scale,
               w_branch_pool=m_w_branch_pool, w_branch_attn=m_w_branch_attn, w_out=m_w_out,
               ffn2_norm=m_ffn2_norm, ffn2_w_gate_up=m_ffn2_w_gate_up, ffn2_w_down=m_ffn2_w_down,
               final_norm=m_final_norm)
    var = dict(ffn1_norm=v_ffn1_norm, ffn1_w_gate_up=v_ffn1_w_gate_up, ffn1_w_down=v_ffn1_w_down,
               mix_norm=v_mix_norm, w_in=v_w_in, pool_w_group=v_pool_w_group, pool_scale=v_pool_scale,
               w_branch_pool=v_w_branch_pool, w_branch_attn=v_w_branch_attn, w_out=v_w_out,
               ffn2_norm=v_ffn2_norm, ffn2_w_gate_up=v_ffn2_w_gate_up, ffn2_w_down=v_ffn2_w_down,
               final_norm=v_final_norm)

    c_idx = lax.axis_index("c").astype(jnp.int32).reshape(1)
    me_idx = (2 * lax.axis_index("x") + lax.axis_index("y")).astype(jnp.int32).reshape(1)
    place = jnp.concatenate([me_idx, c_idx])
    x0, tgt = x[0], loss_target[0]
    wgrp = pool_w_group[0].astype(BF16)
    g1, gm, g2, gf = ffn1_norm, mix_norm, ffn2_norm, final_norm.reshape(1, D)
    grad, delta, new_m, new_v = {}, {}, {}, {}

    def pair_sums(keys, parts, got):
        return [_pair_sum(parts[i], got[i], c_idx, "pair_sum_" + k) for i, k in enumerate(keys)]

    def chip_sums(keys, chip_parts, owned):
        return [_chip_sum(chip_parts[i], owned[i], place, "chip_sum_" + k) for i, k in enumerate(keys)]

    def adamw(k, exchange=None):
        res = _adamw(wts[k][0], grad[k][0], mom[k][0], var[k][0], "adamw_" + k, exchange=exchange)
        outs, landed = (res, None) if exchange is None else res
        delta[k], new_m[k], new_v[k] = (o.reshape(wts[k].shape) for o in outs)
        return landed

    own = {k: _cast_into_block(wts[k][0], me_idx, "cast_" + k) for k in BIG}
    first, late = ("ffn1_w_gate_up", "ffn1_w_down"), ("w_branch_pool", "w_branch_attn", "w_out",
                                                       "ffn2_w_gate_up", "ffn2_w_down")
    full = dict(zip(first, _exchange_alone(_ex_gather([own[k] for k in first]), "gather_ffn1")))
    wgu1, wd1 = full["ffn1_w_gate_up"], full["ffn1_w_down"].reshape(DFF, D)
    (h1, n1, gu1), (win,) = _ffn_fwd(x0, g1, wgu1, wd1, "ffn1_fwd", exchange=_ex_gather([own["w_in"]]))
    u, xp, q, k, v, gp, gs = _mix_in(h1, gm, win)
    (o_sb, ctot), landed = _attn_fwd(q, k, v, exchange=_ex_gather([own[k_] for k_ in late]))
    full.update(zip(late, landed))
    wbp, wba, wout = full["w_branch_pool"], full["w_branch_attn"], full["w_out"].reshape(D, D)
    wgu2, wd2 = full["ffn2_w_gate_up"], full["ffn2_w_down"].reshape(DFF, D)
    h2, pm, p, yp, ys, mm = _mix_out(h1, xp, o_sb, gp, gs, wgrp, pool_scale, wbp, wba, wout)
    h3, n3, gu3 = _ffn_fwd(h2, g2, wgu2, wd2, "ffn2_fwd")
    dh3, loss_row, d_gf = _head(h3, tgt, gf)

    dh2, dgu3, a3, d_g2 = _ffn_bwd(dh3, h2, g2, gu3, wgu2, wd2, "ffn2_bwd")
    ka = ("ffn2_w_gate_up", "ffn2_w_down")
    pa = [_halves(_wgrad(n3, dgu3, NSH, 512, "wgrad_gu2")),
          _halves(_wgrad(a3, dh3, 1, FFS, "wgrad_d2").reshape(NSH, DFF // NSH, D))]
    (dlg, dyp, dys, do_sb, dyg, dxp, d_scale), got_a = _mix_bwd_out(
        dh2, gp, gs, yp, ys, pm, wgrp, pool_scale, wbp, wba, wout, exchange=_ex_pair_swap(pa))
    chip_a = pair_sums(ka, pa, got_a)
    kb = ("w_out", "w_branch_pool", "w_branch_attn")
    pb = [_halves(_wgrad(mm, dh2, 1, 512, "wgrad_out").reshape(NSH, D // NSH, D)),
          _halves(_wgrad(p, dyp, NSH, PW, "wgrad_bp")), _halves(_wgrad(o_sb, dys, NSH, SBW, "wgrad_ba"))]
    chip_b = pair_sums(kb, pb, _exchange_alone(_ex_pair_swap(pb), "pair_swap_mix"))
    (dq, dk, dv), owned_ab = _attn_bwd(q, k, v, do_sb, ctot, exchange=_ex_scatter(chip_a + chip_b))
    halves_ab = chip_sums(ka + kb, chip_a + chip_b, owned_ab)
    dproj = jnp.concatenate([dxp, dq, dk, dv, dlg], axis=1)
    (dh1, d_gm), both_ab = _mix_bwd_in(dh2, h1, gm, dproj, win, exchange=_ex_share(halves_ab))
    for i, k_ in enumerate(ka + kb):
        grad[k_] = both_ab[i].reshape(wts[k_].shape)
    pc = [_halves(_wgrad(u, dproj, NSH, 512, "wgrad_in"))]
    (dx, dgu1, a1, d_g1), got_c = _ffn_bwd(dh1, x0, g1, gu1, wgu1, wd1, "ffn1_bwd", exchange=_ex_pair_swap(pc))
    chip_c = pair_sums(("w_in",), pc, got_c)
    pd, owned_c = _wgrad(n1, dgu1, NSH, 512, "wgrad_gu1", exchange=_ex_scatter(chip_c))
    pd = [_halves(pd)]
    pe, got_d = _wgrad(a1, dh1, 1, FFS, "wgrad_d1", exchange=_ex_pair_swap(pd))
    pe = [_halves(pe.reshape(NSH, DFF // NSH, D))]
    chip_d = pair_sums(("ffn1_w_gate_up",), pd, got_d)
    got_e = adamw("ffn2_w_gate_up", exchange=_ex_pair_swap(pe))
    chip_e = pair_sums(("ffn1_w_down",), pe, got_e)
    halves_c = chip_sums(("w_in",), chip_c, owned_c)
    owned_de = adamw("ffn2_w_down", exchange=_ex_scatter(chip_d + chip_e))
    kde = ("ffn1_w_gate_up", "ffn1_w_down")
    halves_de = chip_sums(kde, chip_d + chip_e, owned_de)
    both_cde = adamw("w_out", exchange=_ex_share(halves_c + halves_de))
    for i, k_ in enumerate(("w_in",) + kde):
        grad[k_] = both_cde[i].reshape(wts[k_].shape)
    for k_ in ("w_branch_pool", "w_branch_attn", "w_in") + kde:
        adamw(k_)

    small_g = dict(ffn1_norm=d_g1, mix_norm=d_gm, ffn2_norm=d_g2, final_norm=d_gf, pool_scale=d_scale,
                   pool_w_group=_wgrad_groups(pm, dyg), loss=loss_row)
    gathered = _gather_small(_pack_small(small_g))
    sg, sd, sm, sv = _small_update(gathered, _pack_small(wts), _pack_small(mom), _pack_small(var))
    sums = _unpack_small(sg, wts)
    loss = sums.pop("loss")
    grad.update(sums)
    for dst, packed in ((delta, sd), (new_m, sm), (new_v, sv)):
        vals = _unpack_small(packed, wts)
        vals.pop("loss")
        dst.update(vals)
    return (loss, dx[None], *[grad[k_] for k_ in ORDER], *[delta[k_] for k_ in ORDER],
            *[new_m[k_] for k_ in ORDER], *[new_v[k_] for k_ in ORDER])
```

```python
import functools

import jax
import jax.numpy as jnp
from jax import lax
from jax.experimental import pallas as pl
from jax.experimental.pallas import tpu as pltpu

F32 = jnp.float32
BF16 = jnp.bfloat16

S = 2048
D = 1024
DFF = 2816
FFS = 2 * DFF // 4
NSH = 4
PW = 512
PG = 128
POOL_WINDOWS = (2, 4, 8, 16)
HALO = 16
SBW = 512
DH = 64
EPS = 1e-6
SCALE = 0.125
LOG2E = 1.4426950408889634
TA = 256
QB = 2
MIB = 1024 * 1024

LR, B1, B2, AEPS, WD, STEP = 0.001, 0.9, 0.999, 1e-08, 0.01, 10

_VM = pl.BlockSpec(memory_space=pltpu.VMEM)
_ANY = pl.BlockSpec(memory_space=pl.ANY)
MESH = pl.DeviceIdType.MESH


def _nn(a, b):
    return jnp.dot(a, b, preferred_element_type=F32)


def _nt(a, b):
    return lax.dot_general(a, b, (((1,), (1,)), ((), ())), preferred_element_type=F32)


def _tn(a, b):
    return lax.dot_general(a, b, (((0,), (0,)), ((), ())), preferred_element_type=F32)


def _params(sem, vmem_mib):
    return pltpu.CompilerParams(dimension_semantics=sem, vmem_limit_bytes=vmem_mib * MIB)


def _rows(tm, width):
    return pl.BlockSpec((tm, width), lambda i: (i, 0))


def _fixed(shape):
    return pl.BlockSpec(shape, lambda *_: (0,) * len(shape))


def _sds(shape, dtype):
    return jax.ShapeDtypeStruct(shape, dtype)


def _stage(pairs):
    @pl.when(pl.program_id(0) == 0)
    def _():
        for src, dst in pairs:
            pltpu.sync_copy(src, dst)


def _vmem_like(*arrays):
    return [pltpu.VMEM(a.shape, a.dtype) for a in arrays]


class Exchange:
    def __init__(self, arrays, landing, aliases, n_sems, start, finish):
        self.arrays, self.landing, self.aliases, self.n_sems = list(arrays), list(landing), dict(aliases), n_sems
        self.start, self.finish = start, finish


def _call(body, args, *, name, grid, in_specs, out_specs, out_shape, scratch_shapes=(), compiler_params=None,
          exchange=None):
    if exchange is None:
        return pl.pallas_call(body, name=name, grid=grid, in_specs=in_specs, out_specs=out_specs,
                              out_shape=out_shape, scratch_shapes=list(scratch_shapes),
                              compiler_params=compiler_params)(*args)
    ex = exchange
    n_in, n_out, n_scr = len(in_specs), len(out_specs), len(scratch_shapes)
    na, nl = len(ex.arrays), len(ex.landing)

    def hosted(*refs):
        at = [0]

        def take(n):
            at[0] += n
            return refs[at[0] - n:at[0]]

        k_in, e_in, k_out, e_out, k_scr = take(n_in), take(na), take(n_out), take(nl), take(n_scr)
        ssem, rsem = take(2)
        ids = [pl.program_id(a) for a in range(len(grid))]
        first = functools.reduce(jnp.logical_and, [i == 0 for i in ids])
        last = functools.reduce(jnp.logical_and, [i == g - 1 for i, g in zip(ids, grid)])

        @pl.when(first)
        def _():
            ex.start(e_in, e_out, ssem, rsem)

        body(*k_in, *k_out, *k_scr)

        @pl.when(last)
        def _():
            ex.finish(e_in, e_out, ssem, rsem)

    outs = pl.pallas_call(
        hosted, name=name, grid=grid,
        in_specs=list(in_specs) + [_ANY] * na, out_specs=list(out_specs) + [_ANY] * nl,
        out_shape=list(out_shape) + ex.landing,
        scratch_shapes=list(scratch_shapes) + [pltpu.SemaphoreType.DMA((ex.n_sems,))] * 2,
        input_output_aliases={n_in + i: n_out + j for i, j in ex.aliases.items()},
        compiler_params=compiler_params,
    )(*args, *ex.arrays)
    return outs[:n_out], outs[n_out:]


def _exchange_alone(ex, name):
    def body(*refs):
        na, nl = len(ex.arrays), len(ex.landing)
        ex.start(refs[:na], refs[na:na + nl], refs[-2], refs[-1])
        ex.finish(refs[:na], refs[na:na + nl], refs[-2], refs[-1])

    return pl.pallas_call(
        body, name=name, in_specs=[_ANY] * len(ex.arrays), out_specs=[_ANY] * len(ex.landing),
        out_shape=ex.landing, scratch_shapes=[pltpu.SemaphoreType.DMA((ex.n_sems,))] * 2,
        input_output_aliases=ex.aliases,
    )(*ex.arrays)


def _rms(x):
    r = lax.rsqrt(jnp.mean(x * x, axis=-1, keepdims=True) + EPS)
    return r, x * r


def _rms_bwd(dn, xr, r, gain):
    dng = dn * gain
    dx = r * (dng - xr * jnp.mean(dng * xr, axis=-1, keepdims=True))
    return dx, jnp.sum(dn * xr, axis=0, keepdims=True)


def _ffn_fwd(x, gain, wgu, wd, name, exchange=None):
    tm = 256

    def body(x_ref, g_ref, wgu_hbm, wd_hbm, h_ref, n_ref, gu_ref, wgu_ref, wd_ref):
        _stage([(wgu_hbm, wgu_ref), (wd_hbm, wd_ref)])
        x = x_ref[...]
        _, xr = _rms(x)
        n = (xr * g_ref[...]).astype(BF16)
        n_ref[...] = n
        acc = jnp.zeros((tm, D), F32)
        for j in range(2):
            g = _nn(n, wgu_ref[j])
            u = _nn(n, wgu_ref[2 + j])
            gu_ref[:, j * FFS:(j + 1) * FFS] = g.astype(BF16)
            gu_ref[:, (2 + j) * FFS:(3 + j) * FFS] = u.astype(BF16)
            a = (g * jax.nn.sigmoid(g) * u).astype(BF16)
            acc = acc + _nn(a, wd_ref[j * FFS:(j + 1) * FFS, :])
        h_ref[...] = x + 0.5 * acc

    return _call(
        body, (x, gain, wgu, wd), name=name, grid=(S // tm,),
        in_specs=[_rows(tm, D), _fixed((1, D)), _ANY, _ANY],
        out_specs=[_rows(tm, D), _rows(tm, D), _rows(tm, 4 * FFS)],
        out_shape=[_sds((S, D), F32), _sds((S, D), BF16), _sds((S, 4 * FFS), BF16)],
        scratch_shapes=_vmem_like(wgu, wd),
        compiler_params=_params(("arbitrary",), 56), exchange=exchange)


def _ffn_bwd(dh, x, gain, gu, wgu, wd, name, exchange=None):
    tm = 256

    def body(dh_ref, x_ref, g_ref, gu_ref, wgu_hbm, wd_hbm, dx_ref, dgu_ref, a_ref, dg_ref, wgu_ref, wd_ref):
        _stage([(wgu_hbm, wgu_ref), (wd_hbm, wd_ref)])
        dh = dh_ref[...]
        dhb = dh.astype(BF16)
        dn = jnp.zeros((tm, D), F32)
        for j in range(2):
            g = gu_ref[:, j * FFS:(j + 1) * FFS].astype(F32)
            u = gu_ref[:, (2 + j) * FFS:(3 + j) * FFS].astype(F32)
            da = 0.5 * _nt(dhb, wd_ref[j * FFS:(j + 1) * FFS, :])
            sg = jax.nn.sigmoid(g)
            si = g * sg
            a_ref[:, j * FFS:(j + 1) * FFS] = (0.5 * si * u).astype(BF16)
            dgb = (da * u * (sg * (1.0 + g * (1.0 - sg)))).astype(BF16)
            dub = (da * si).astype(BF16)
            dgu_ref[:, j * FFS:(j + 1) * FFS] = dgb
            dgu_ref[:, (2 + j) * FFS:(3 + j) * FFS] = dub
            dn = dn + _nt(dgb, wgu_ref[j]) + _nt(dub, wgu_ref[2 + j])
        r, xr = _rms(x_ref[...])
        dx, dgain = _rms_bwd(dn, xr, r, g_ref[...])
        dx_ref[...] = dh + dx

        @pl.when(pl.program_id(0) == 0)
        def _():
            dg_ref[...] = jnp.zeros_like(dg_ref)

        dg_ref[...] += dgain

    return _call(
        body, (dh, x, gain, gu, wgu, wd), name=name, grid=(S // tm,),
        in_specs=[_rows(tm, D), _rows(tm, D), _fixed((1, D)), _rows(tm, 4 * FFS), _ANY, _ANY],
        out_specs=[_rows(tm, D), _rows(tm, 4 * FFS), _rows(tm, DFF), _fixed((1, D))],
        out_shape=[_sds((S, D), F32), _sds((S, 4 * FFS), BF16), _sds((S, DFF), BF16), _sds((1, D), F32)],
        scratch_shapes=_vmem_like(wgu, wd),
        compiler_params=_params(("arbitrary",), 56), exchange=exchange)


def _head(h, target, gain):
    tm = 512

    def body(h_ref, t_ref, g_ref, dh_ref, loss_ref, dg_ref):
        gain = g_ref[...]
        r, hr = _rms(h_ref[...])
        err = hr * gain - t_ref[...]
        dy = err * (1.0 / D)
        dh, dgain = _rms_bwd(dy, hr, r, gain)
        dh_ref[...] = dh

        @pl.when(pl.program_id(0) == 0)
        def _():
            dg_ref[...] = jnp.zeros_like(dg_ref)
            loss_ref[...] = jnp.zeros_like(loss_ref)

        dg_ref[...] += dgain
        loss_ref[...] += jnp.full((1, 128), (0.5 / D) * jnp.sum(err * err), F32)

    return pl.pallas_call(
        body, name="head", grid=(S // tm,),
        in_specs=[_rows(tm, D), _rows(tm, D), _fixed((1, D))],
        out_specs=[_rows(tm, D), _fixed((1, 128)), _fixed((1, D))],
        out_shape=[_sds((S, D), F32), _sds((1, 128), F32), _sds((1, D), F32)],
        compiler_params=_params(("arbitrary",), 40),
    )(h, target, gain)


def _mix_in(h, gain, w_in):
    tm = 512

    def body(h_ref, g_ref, w_hbm, u_ref, xp_ref, q_ref, k_ref, v_ref, gp_ref, gs_ref, w_ref):
        _stage([(w_hbm, w_ref)])
        _, hr = _rms(h_ref[...])
        u = (hr * g_ref[...]).astype(BF16)
        u_ref[...] = u
        p0 = _nn(u, w_ref[0])
        xp_ref[...] = p0[:, :PW]
        q_ref[...] = p0[:, PW:].astype(BF16)
        p1 = _nn(u, w_ref[1])
        k_ref[...] = p1[:, :SBW].astype(BF16)
        v_ref[...] = p1[:, SBW:].astype(BF16)
        gp_ref[...] = jax.nn.sigmoid(_nn(u, w_ref[2])).astype(BF16)
        gs_ref[...] = jax.nn.sigmoid(_nn(u, w_ref[3])).astype(BF16)

    return pl.pallas_call(
        body, name="mix_in", grid=(S // tm,),
        in_specs=[_rows(tm, D), _fixed((1, D)), _ANY],
        out_specs=[_rows(tm, D), _rows(tm, PW), _rows(tm, SBW), _rows(tm, SBW), _rows(tm, SBW),
                   _rows(tm, D), _rows(tm, D)],
        out_shape=[_sds((S, D), BF16), _sds((S, PW), F32), _sds((S, SBW), BF16), _sds((S, SBW), BF16),
                   _sds((S, SBW), BF16), _sds((S, D), BF16), _sds((S, D), BF16)],
        scratch_shapes=_vmem_like(w_in),
        compiler_params=_params(("arbitrary",), 48),
    )(h, gain, w_in)


def _hilo_dot(x, tri):
    hi = x.astype(BF16)
    lo = (x - hi.astype(F32)).astype(BF16)
    return _nn(hi, tri) + _nn(lo, tri)


def _log_terms(qk):
    z2 = qk * (SCALE * LOG2E)
    e = jnp.exp2(-jnp.abs(z2))
    lb = jnp.minimum(z2, 0.0) - jnp.log2(1.0 + e)
    return e, lb, lb - z2


def _head_masks():
    lane = lax.broadcasted_iota(jnp.int32, (1, 2 * DH), 1)
    return (lane < DH, lane >= DH)


def _attn_fwd(q, k, v, exchange=None):
    T = TA

    def body(q_ref, k_ref, v_ref, o_ref, c_ref):
        i2 = 2 * pl.program_id(1)
        row = lax.broadcasted_iota(jnp.int32, (T, T), 0)
        col = lax.broadcasted_iota(jnp.int32, (T, T), 1)
        after = (row > col).astype(BF16)
        causal = col < row
        masks = _head_masks()
        qms = {}
        for b in range(QB):
            q2 = q_ref[b * T:(b + 1) * T, :]
            for h, hm in enumerate(masks):
                qms[b, h] = jnp.where(hm, q2, jnp.zeros_like(q2))

        def block(j, carries, os, kinds):
            rows = pl.ds(pl.multiple_of(j * T, T), T)
            kj, vj = k_ref[rows, :], v_ref[rows, :]
            vms = [jnp.where(hm, vj, jnp.zeros_like(vj)) for hm in masks]
            chains = [(b, h) for b in range(QB) if kinds[b] for h in range(2)]
            qks = {c: _nt(qms[c], kj) for c in chains}
            lbs, l1ms = {}, {}
            for c in chains:
                _, lbs[c], l1m = _log_terms(qks[c])
                l1ms[c] = jnp.where(causal, l1m, 0.0) if kinds[c[0]] == "diag" else l1m
            cins = {c: _hilo_dot(l1ms[c], after) for c in chains}
            carries, os = dict(carries), list(os)
            for c in chains:
                a = jnp.exp2(lbs[c] + cins[c] + carries[c])
                if kinds[c[0]] == "diag":
                    a = jnp.where(causal, a, 0.0)
                os[c[0]] = os[c[0]] + _nn(a.astype(BF16), vms[c[1]])
            for c in chains:
                carries[c] = carries[c] + jnp.sum(l1ms[c], axis=1, keepdims=True)
            return carries, tuple(os)

        carries = {(b, h): jnp.zeros((T, 1), F32) for b in range(QB) for h in range(2)}
        os = tuple(jnp.zeros((T, 2 * DH), F32) for _ in range(QB))
        carries, os = block(i2 + 1, carries, os, (None, "diag"))
        carries, os = block(i2, carries, os, ("diag", "full"))
        carries, os = lax.fori_loop(
            0, i2, lambda jj, c: block(i2 - 1 - jj, c[0], c[1], ("full", "full")), (carries, os))
        for b in range(QB):
            o_ref[b * T:(b + 1) * T, :] = os[b].astype(BF16)
            c_ref[b * T:(b + 1) * T, :] = jnp.where(masks[0], carries[b, 0], carries[b, 1])

    blk = pl.BlockSpec((QB * T, 2 * DH), lambda p, i: (i, p))
    full = pl.BlockSpec((S, 2 * DH), lambda p, i: (0, p))
    return _call(
        body, (q, k, v), name="attn_fwd", grid=(SBW // (2 * DH), S // (QB * T)),
        in_specs=[blk, full, full], out_specs=[blk, blk],
        out_shape=[_sds((S, SBW), BF16), _sds((S, SBW), F32)],
        compiler_params=_params(("arbitrary", "arbitrary"), 40), exchange=exchange)


def _attn_bwd(q, k, v, do, ctot, exchange=None):
    T = TA
    nq = S // (QB * T)

    def body(q_ref, k_ref, v_ref, do_ref, c_ref, dq_ref, dk_ref, dv_ref, dk_acc, dv_acc):
        step = pl.program_id(1)
        i2 = 2 * step

        @pl.when(step == 0)
        def _():
            dk_acc[...] = jnp.zeros_like(dk_acc)
            dv_acc[...] = jnp.zeros_like(dv_acc)

        row = lax.broadcasted_iota(jnp.int32, (T, T), 0)
        col = lax.broadcasted_iota(jnp.int32, (T, T), 1)
        upto = (row <= col).astype(BF16)
        before = (row < col).astype(BF16)
        causal = col < row
        masks = _head_masks()
        qms, doms, ctots = {}, {}, {}
        for b in range(QB):
            q2, do2 = q_ref[b * T:(b + 1) * T, :], do_ref[b * T:(b + 1) * T, :]
            for h, hm in enumerate(masks):
                qms[b, h] = jnp.where(hm, q2, jnp.zeros_like(q2))
                doms[b, h] = jnp.where(hm, do2, jnp.zeros_like(do2))
                ctots[b, h] = c_ref[b * T:(b + 1) * T, h * DH:h * DH + 1]

        def block(j, sums, dqs, kinds):
            rows = pl.ds(pl.multiple_of(j * T, T), T)
            kj, vj = k_ref[rows, :], v_ref[rows, :]
            kms = [jnp.where(hm, kj, jnp.zeros_like(kj)) for hm in masks]
            chains = [(b, h) for b in range(QB) if kinds[b] for h in range(2)]
            diag = {c: kinds[c[0]] == "diag" for c in chains}
            qks = {c: _nt(qms[c], kj) for c in chains}
            das = {c: _nt(doms[c], vj) for c in chains}
            es, lbs, l1ms = {}, {}, {}
            for c in chains:
                es[c], lbs[c], l1m = _log_terms(qks[c])
                l1ms[c] = jnp.where(causal, l1m, 0.0) if diag[c] else l1m
            pins = {c: _hilo_dot(l1ms[c], upto) for c in chains}
            a_s, dls = {}, {}
            for c in chains:
                a = jnp.exp2(lbs[c] + (ctots[c] - sums[c][0]) - pins[c])
                if diag[c]:
                    a = jnp.where(causal, a, 0.0)
                a_s[c] = a.astype(BF16)
                dls[c] = das[c] * a
            pexs = {c: _hilo_dot(dls[c], before) for c in chains}
            dzbs = {}
            for c in chains:
                rinv = 1.0 / (1.0 + es[c])
                pos = qks[c] >= 0.0
                beta = jnp.where(pos, rinv, es[c] * rinv)
                omb = jnp.where(pos, es[c] * rinv, rinv)
                dz = (dls[c] * omb - (pexs[c] + sums[c][1]) * beta) * SCALE
                if diag[c]:
                    dz = jnp.where(causal, dz, 0.0)
                dzbs[c] = dz.astype(BF16)
            dqs = list(dqs)
            for c in chains:
                dqs[c[0]] = dqs[c[0]] + _nn(dzbs[c], kms[c[1]])
            dk_acc[rows, :] += functools.reduce(jnp.add, [_tn(dzbs[c], qms[c]) for c in chains])
            dv_acc[rows, :] += functools.reduce(jnp.add, [_tn(a_s[c], doms[c]) for c in chains])
            sums = dict(sums)
            for c in chains:
                sums[c] = (sums[c][0] + jnp.sum(l1ms[c], axis=1, keepdims=True),
                           sums[c][1] + jnp.sum(dls[c], axis=1, keepdims=True))
            return sums, tuple(dqs)

        zero = jnp.zeros((T, 1), F32)
        sums = {(b, h): (zero, zero) for b in range(QB) for h in range(2)}
        dqs = tuple(jnp.zeros((T, 2 * DH), F32) for _ in range(QB))
        sums, dqs = lax.fori_loop(0, i2, lambda j, c: block(j, c[0], c[1], ("full", "full")), (sums, dqs))
        sums, dqs = block(i2, sums, dqs, ("diag", "full"))
        _, dqs = block(i2 + 1, sums, dqs, (None, "diag"))
        for b in range(QB):
            dq_ref[b * T:(b + 1) * T, :] = dqs[b].astype(BF16)

        @pl.when(step == nq - 1)
        def _():
            dk_ref[...] = dk_acc[...].astype(BF16)
            dv_ref[...] = dv_acc[...].astype(BF16)

    blk = pl.BlockSpec((QB * T, 2 * DH), lambda p, i: (i, p))
    full = pl.BlockSpec((S, 2 * DH), lambda p, i: (0, p))
    return _call(
        body, (q, k, v, do, ctot), name="attn_bwd", grid=(SBW // (2 * DH), nq),
        in_specs=[blk, full, full, blk, blk], out_specs=[blk, full, full],
        out_shape=[_sds((S, SBW), BF16), _sds((S, SBW), BF16), _sds((S, SBW), BF16)],
        scratch_shapes=[pltpu.VMEM((S, 2 * DH), F32), pltpu.VMEM((S, 2 * DH), F32)],
        compiler_params=_params(("arbitrary", "arbitrary"), 40), exchange=exchange)


def _pool_counts(first_row, tm):
    pos = first_row + lax.broadcasted_iota(jnp.int32, (tm, 1), 0)
    return [jnp.minimum(pos + 1, w).astype(F32) for w in POOL_WINDOWS]


def _mix_out(h, xp, o_sb, gp, gs, w_group, scale, w_bp, w_ba, w_out):
    tm = 512

    def body(h_ref, xp_ref, o_ref, gp_ref, gs_ref, wg_hbm, sc_ref, wbp_hbm, wba_hbm, wo_hbm,
             h2_ref, pm_ref, p_ref, yp_ref, ys_ref, m_ref, halo, wg_ref, wbp_ref, wba_ref, wo_ref):
        _stage([(wg_hbm, wg_ref), (wbp_hbm, wbp_ref), (wba_hbm, wba_ref), (wo_hbm, wo_ref)])
        i = pl.program_id(0)

        @pl.when(i == 0)
        def _():
            halo[...] = jnp.zeros_like(halo)

        xp = xp_ref[...]
        ext = jnp.concatenate([halo[...], xp], axis=0)
        halo[...] = xp[tm - HALO:, :]
        counts = _pool_counts(i * tm, tm)
        for gi in range(len(POOL_WINDOWS)):
            lanes = slice(gi * PG, (gi + 1) * PG)
            win = ext[:, lanes]
            for step in range(gi + 1):
                win = win + pltpu.roll(win, 1 << step, 0)
            pm = (win[HALO:, :] / counts[gi] - xp[:, lanes]).astype(BF16)
            pm_ref[:, lanes] = pm
            p_ref[:, lanes] = (_nn(pm, wg_ref[gi]) * sc_ref[:, lanes]).astype(BF16)
        pb = p_ref[...]
        ob = o_ref[...]
        for j in range(NSH):
            cols = slice(j * (D // NSH), (j + 1) * (D // NSH))
            yp = _nn(pb, wbp_ref[j])
            ys = _nn(ob, wba_ref[j])
            yp_ref[:, cols] = yp.astype(BF16)
            ys_ref[:, cols] = ys.astype(BF16)
            m_ref[:, cols] = (gp_ref[:, cols].astype(F32) * yp + gs_ref[:, cols].astype(F32) * ys).astype(BF16)
        h2_ref[...] = h_ref[...] + _nn(m_ref[...], wo_ref[...])

    return pl.pallas_call(
        body, name="mix_out", grid=(S // tm,),
        in_specs=[_rows(tm, D), _rows(tm, PW), _rows(tm, SBW), _rows(tm, D), _rows(tm, D),
                  _ANY, _fixed((1, PW)), _ANY, _ANY, _ANY],
        out_specs=[_rows(tm, D), _rows(tm, PW), _rows(tm, PW), _rows(tm, D), _rows(tm, D), _rows(tm, D)],
        out_shape=[_sds((S, D), F32), _sds((S, PW), BF16), _sds((S, PW), BF16), _sds((S, D), BF16),
                   _sds((S, D), BF16), _sds((S, D), BF16)],
        scratch_shapes=[pltpu.VMEM((HALO, PW), F32)] + _vmem_like(w_group, w_bp, w_ba, w_out),
        compiler_params=_params(("arbitrary",), 48),
    )(h, xp, o_sb, gp, gs, w_group, scale, w_bp, w_ba, w_out)


def _mix_bwd_out(dh, gp, gs, yp, ys, pm, w_group, scale, w_bp, w_ba, w_out, exchange=None):
    tm = 512
    nt = S // tm

    def body(dh_ref, gp_ref, gs_ref, yp_ref, ys_ref, pm_ref, wg_hbm, sc_ref, wbp_hbm, wba_hbm, wo_hbm,
             dlg_ref, dyp_ref, dys_ref, do_ref, dyg_ref, dxp_ref, dsc_ref, halo, wg_ref, wbp_ref, wba_ref, wo_ref):
        _stage([(wg_hbm, wg_ref), (wbp_hbm, wbp_ref), (wba_hbm, wba_ref), (wo_hbm, wo_ref)])
        step = pl.program_id(0)

        @pl.when(step == 0)
        def _():
            halo[...] = jnp.zeros_like(halo)
            dsc_ref[...] = jnp.zeros_like(dsc_ref)

        dm = _nt(dh_ref[...].astype(BF16), wo_ref[...])
        gp = gp_ref[...].astype(F32)
        gs = gs_ref[...].astype(F32)
        yp = yp_ref[...].astype(F32)
        ys = ys_ref[...].astype(F32)
        dlg_ref[:, :D] = (dm * yp * gp * (1.0 - gp)).astype(BF16)
        dlg_ref[:, D:] = (dm * ys * gs * (1.0 - gs)).astype(BF16)
        dyp_ref[...] = (dm * gp).astype(BF16)
        dys_ref[...] = (dm * gs).astype(BF16)
        dp = jnp.zeros((tm, PW), F32)
        do = jnp.zeros((tm, SBW), F32)
        for j in range(NSH):
            cols = slice(j * (D // NSH), (j + 1) * (D // NSH))
            dp = dp + _nt(dyp_ref[:, cols], wbp_ref[j])
            do = do + _nt(dys_ref[:, cols], wba_ref[j])
        do_ref[...] = do.astype(BF16)
        counts = _pool_counts((nt - 1 - step) * tm, tm)
        dscale = []
        for gi in range(len(POOL_WINDOWS)):
            lanes = slice(gi * PG, (gi + 1) * PG)
            dpg = dp[:, lanes]
            dscale.append(jnp.sum(dpg * _nn(pm_ref[:, lanes], wg_ref[gi]), axis=0, keepdims=True))
            dyg = (dpg * sc_ref[:, lanes]).astype(BF16)
            dyg_ref[:, lanes] = dyg
            dpm = _nt(dyg, wg_ref[gi])
            per = dpm / counts[gi]
            win = jnp.concatenate([per, halo[:, lanes]], axis=0)
            halo[:, lanes] = per[:HALO, :]
            for s in range(gi + 1):
                win = win + pltpu.roll(win, tm + HALO - (1 << s), 0)
            dxp_ref[:, lanes] = (win[:tm, :] - dpm).astype(BF16)
        dsc_ref[...] += jnp.concatenate(dscale, axis=1)

    rev = lambda width: pl.BlockSpec((tm, width), lambda i: (nt - 1 - i, 0))
    return _call(
        body, (dh, gp, gs, yp, ys, pm, w_group, scale, w_bp, w_ba, w_out), name="mix_bwd_out", grid=(nt,),
        in_specs=[rev(D), rev(D), rev(D), rev(D), rev(D), rev(PW), _ANY, _fixed((1, PW)), _ANY, _ANY, _ANY],
        out_specs=[rev(2 * D), rev(D), rev(D), rev(SBW), rev(PW), rev(PW), _fixed((1, PW))],
        out_shape=[_sds((S, 2 * D), BF16), _sds((S, D), BF16), _sds((S, D), BF16), _sds((S, SBW), BF16),
                   _sds((S, PW), BF16), _sds((S, PW), BF16), _sds((1, PW), F32)],
        scratch_shapes=[pltpu.VMEM((HALO, PW), F32)] + _vmem_like(w_group, w_bp, w_ba, w_out),
        compiler_params=_params(("arbitrary",), 48), exchange=exchange)


def _mix_bwd_in(dh, h, gain, dproj, w_in, exchange=None):
    tm = 512

    def body(dh_ref, h_ref, g_ref, dp_ref, w_hbm, dx_ref, dg_ref, w_ref):
        _stage([(w_hbm, w_ref)])
        du = jnp.zeros((tm, D), F32)
        for j in range(NSH):
            du = du + _nt(dp_ref[:, j * D:(j + 1) * D], w_ref[j])
        r, hr = _rms(h_ref[...])
        dx, dgain = _rms_bwd(du, hr, r, g_ref[...])
        dx_ref[...] = dh_ref[...] + dx

        @pl.when(pl.program_id(0) == 0)
        def _():
            dg_ref[...] = jnp.zeros_like(dg_ref)

        dg_ref[...] += dgain

    return _call(
        body, (dh, h, gain, dproj, w_in), name="mix_bwd_in", grid=(S // tm,),
        in_specs=[_rows(tm, D), _rows(tm, D), _fixed((1, D)), _rows(tm, 4 * D), _ANY],
        out_specs=[_rows(tm, D), _fixed((1, D))],
        out_shape=[_sds((S, D), F32), _sds((1, D), F32)],
        scratch_shapes=_vmem_like(w_in),
        compiler_params=_params(("arbitrary",), 48), exchange=exchange)


def _wgrad(a, b, nblk, ti, name, out_dtype=BF16, exchange=None):
    ka, n = a.shape[1], b.shape[1]
    ns = n // nblk

    def body(a_ref, b_ref, o_ref):
        o_ref[...] = _tn(a_ref[...].astype(BF16), b_ref[...].astype(BF16)).astype(out_dtype)

    res = _call(
        body, (a, b), name=name, grid=(nblk, ka // ti),
        in_specs=[pl.BlockSpec((S, ti), lambda j, i: (0, i)), pl.BlockSpec((S, ns), lambda j, i: (0, j))],
        out_specs=[pl.BlockSpec((None, ti, ns), lambda j, i: (j, i, 0))],
        out_shape=[_sds((nblk, ka, ns), out_dtype)],
        compiler_params=_params(("arbitrary", "arbitrary"), 56), exchange=exchange)
    return res[0] if exchange is None else (res[0][0], res[1])


def _wgrad_groups(pm, dyg):
    def body(a_ref, b_ref, o_ref):
        o_ref[...] = _tn(a_ref[...], b_ref[...])

    col = pl.BlockSpec((S, PG), lambda g: (0, g))
    return pl.pallas_call(
        body, name="wgrad_groups", grid=(PW // PG,),
        in_specs=[col, col], out_specs=pl.BlockSpec((None, PG, PG), lambda g: (g, 0, 0)),
        out_shape=_sds((PW // PG, PG, PG), F32),
        compiler_params=_params(("arbitrary",), 32),
    )(pm, dyg)


def _place():
    x, y, c = lax.axis_index("x"), lax.axis_index("y"), lax.axis_index("c")
    chips = [(1 - x, y), (x, 1 - y), (1 - x, 1 - y)]
    return x, y, c, chips


def _remote(src, dst, ssem, rsem, dev):
    return pltpu.make_async_remote_copy(src_ref=src, dst_ref=dst, send_sem=ssem, recv_sem=rsem,
                                        device_id=dev, device_id_type=MESH)


def _cast_into_block(w, me_idx, name):
    rows, cols = w.shape
    tr = _row_block(rows)

    def body(me_ref, w_ref, o_ref):
        o_ref[...] = w_ref[...].astype(BF16)

    return pl.pallas_call(
        body, name=name, out_shape=_sds((NSH, rows, cols), BF16),
        grid_spec=pltpu.PrefetchScalarGridSpec(
            num_scalar_prefetch=1, grid=(rows // tr,),
            in_specs=[pl.BlockSpec((tr, cols), lambda r, me: (r, 0))],
            out_specs=pl.BlockSpec((None, tr, cols), lambda r, me: (me[0], r, 0))),
        compiler_params=_params(("arbitrary",), 32),
    )(me_idx, w)


def _ex_gather(bufs):
    n = len(bufs)

    def copies(outs, ssem, rsem, only_first=False):
        x, y, c, chips = _place()
        me, sib = 2 * x + y, (x, y, 1 - c)
        first, relay, last = [], [], []
        for w in range(n):
            half = outs[w].shape[1] // 2
            mine = outs[w].at[me, pl.ds(c * half, half)]
            for k, (px, py) in enumerate(chips):
                sems = (ssem.at[6 * w + k], rsem.at[6 * w + k])
                sib_sems = (ssem.at[6 * w + 3 + k], rsem.at[6 * w + 3 + k])
                first.append(_remote(mine, mine, *sems, (px, py, c)))
                if only_first:
                    continue
                got = outs[w].at[2 * px + py, pl.ds(c * half, half)]
                relay.append((_remote(got, got, *sems, (px, py, c)), _remote(got, got, *sib_sems, sib)))
                theirs = outs[w].at[2 * px + py, pl.ds((1 - c) * half, half)]
                last.append(_remote(theirs, theirs, *sib_sems, sib))
        return first, relay, last

    def start(ins, outs, ssem, rsem):
        for cp in copies(outs, ssem, rsem, only_first=True)[0]:
            cp.start()

    def finish(ins, outs, ssem, rsem):
        first, relay, last = copies(outs, ssem, rsem)
        for arrived, onward in relay:
            arrived.wait_recv()
            onward.start()
        for cp in last:
            cp.wait_recv()
        for cp in first:
            cp.wait_send()
        for _, onward in relay:
            onward.wait_send()

    return Exchange(bufs, [_sds(b.shape, b.dtype) for b in bufs], {w: w for w in range(n)}, 6 * n, start, finish)


def _simple_exchange(arrays, landing, aliases, make_copies):
    def start(ins, outs, ssem, rsem):
        for cp, _ in make_copies(ins, outs, ssem, rsem, False):
            cp.start()

    def finish(ins, outs, ssem, rsem):
        cps = make_copies(ins, outs, ssem, rsem, True)
        for _, landed in cps:
            landed.wait_recv()
        for cp, _ in cps:
            cp.wait_send()

    return Exchange(arrays, landing, aliases, len(arrays) * 3, start, finish)


def _ex_pair_swap(grads):
    def make(ins, outs, ssem, rsem, landing):
        x, y, c, _ = _place()
        cps = [_remote(ins[w].at[:, 1 - c], outs[w], ssem.at[w], rsem.at[w], (x, y, 1 - c))
               for w in range(len(grads))]
        return [(cp, cp) for cp in cps]

    return _simple_exchange(grads, [_sds((NSH,) + g.shape[2:], g.dtype) for g in grads], {}, make)


def _ex_scatter(parts):
    def make(ins, outs, ssem, rsem, landing):
        x, y, c, chips = _place()
        out = []
        for w in range(len(parts)):
            for k, (px, py) in enumerate(chips):
                sems = (ssem.at[3 * w + k], rsem.at[3 * w + k])
                out.append((_remote(ins[w].at[2 * px + py], outs[w].at[k], *sems, (px, py, c)),
                            _remote(outs[w].at[k], outs[w].at[k], *sems, (px, py, c)) if landing else None))
        return out

    return _simple_exchange(parts, [_sds((3,) + p.shape[1:], p.dtype) for p in parts], {}, make)


def _ex_share(bufs):
    def make(ins, outs, ssem, rsem, landing):
        x, y, c, _ = _place()
        sib = (x, y, 1 - c)
        return [(_remote(outs[w].at[c], outs[w].at[c], ssem.at[w], rsem.at[w], sib),
                 _remote(outs[w].at[1 - c], outs[w].at[1 - c], ssem.at[w], rsem.at[w], sib) if landing else None)
                for w in range(len(bufs))]

    return _simple_exchange(bufs, [_sds(b.shape, b.dtype) for b in bufs], {w: w for w in range(len(bufs))}, make)


def _gather_small(block):
    m_per, n = block.shape

    def body(x_ref, out_ref, ssem, rsem, lsem):
        x, y, c, chips = _place()
        me, sib = (x, y, c), (x, y, 1 - c)

        def rows(px, py, pc):
            return out_ref.at[pl.ds((4 * px + 2 * py + pc) * m_per, m_per), :]

        def copy(k, blk, to, src=None):
            return _remote(rows(*blk) if src is None else src, rows(*blk), ssem.at[k], rsem.at[k], to)

        mine = pltpu.make_async_copy(x_ref, rows(*me), lsem)
        mine.start()
        first = [copy(0, me, sib, src=x_ref)]
        first += [copy(1 + j, me, (*chip, c), src=x_ref) for j, chip in enumerate(chips)]
        for cp in first:
            cp.start()
        passed = [copy(4 + j, (*chip, c), sib) for j, chip in enumerate(chips)]
        for j, chip in enumerate(chips):
            copy(1 + j, (*chip, c), me).wait_recv()
            passed[j].start()
        copy(0, sib, me).wait_recv()
        for j, chip in enumerate(chips):
            copy(4 + j, (*chip, 1 - c), me).wait_recv()
        for cp in first + passed:
            cp.wait_send()
        mine.wait()

    return pl.pallas_call(
        body, name="gather_small", out_shape=_sds((8 * m_per, n), block.dtype),
        in_specs=[_VM], out_specs=_VM,
        scratch_shapes=[pltpu.SemaphoreType.DMA((7,)), pltpu.SemaphoreType.DMA((7,)), pltpu.SemaphoreType.DMA],
    )(block)


def _row_block(rows):
    return max(t for t in range(16, 257, 16) if rows % t == 0)


def _pair_sum(grad, got, c_idx, name):
    _, _, half, cols = grad.shape
    tr = _row_block(half)

    def body(c_ref, a_ref, b_ref, o_ref):
        o_ref[...] = (a_ref[...].astype(F32) + b_ref[...].astype(F32)).astype(BF16)

    return pl.pallas_call(
        body, name=name, out_shape=_sds((NSH, half, cols), BF16),
        grid_spec=pltpu.PrefetchScalarGridSpec(
            num_scalar_prefetch=1, grid=(NSH, half // tr),
            in_specs=[pl.BlockSpec((None, None, tr, cols), lambda j, r, c: (j, c[0], r, 0)),
                      pl.BlockSpec((None, tr, cols), lambda j, r, c: (j, r, 0))],
            out_specs=pl.BlockSpec((None, tr, cols), lambda j, r, c: (j, r, 0))),
        compiler_params=_params(("arbitrary", "arbitrary"), 32),
    )(c_idx, grad, got)


def _chip_sum(own, got, place, name):
    _, half, cols = own.shape
    tr = _row_block(half)

    def body(place_ref, own_ref, got_ref, o_ref):
        acc = own_ref[...].astype(F32)
        for k in range(3):
            acc = acc + got_ref[k].astype(F32)
        o_ref[...] = acc

    return pl.pallas_call(
        body, name=name, out_shape=_sds((2, half, cols), F32),
        grid_spec=pltpu.PrefetchScalarGridSpec(
            num_scalar_prefetch=1, grid=(half // tr,),
            in_specs=[pl.BlockSpec((None, tr, cols), lambda r, p: (p[0], r, 0)),
                      pl.BlockSpec((3, tr, cols), lambda r, p: (0, r, 0))],
            out_specs=pl.BlockSpec((None, tr, cols), lambda r, p: (p[1], r, 0))),
        compiler_params=_params(("arbitrary",), 32),
    )(place, own, got)


def _adamw_math(w, g, m, v):
    m = B1 * m + (1.0 - B1) * g
    v = B2 * v + (1.0 - B2) * (g * g)
    m_hat = m / (1.0 - B1 ** STEP)
    v_hat = v / (1.0 - B2 ** STEP)
    return -LR * (m_hat / (jnp.sqrt(v_hat) + AEPS) + WD * w), m, v


def _adamw(w, g, m, v, name, exchange=None):
    rows, cols = w.shape
    tr = _row_block(rows)

    def body(w_ref, g_ref, m_ref, v_ref, d_ref, nm_ref, nv_ref):
        d_ref[...], nm_ref[...], nv_ref[...] = _adamw_math(w_ref[...], g_ref[...], m_ref[...], v_ref[...])

    blk = pl.BlockSpec((tr, cols), lambda r: (r, 0))
    return _call(
        body, (w, g, m, v), name=name, grid=(rows // tr,), out_shape=[_sds(w.shape, F32)] * 3,
        in_specs=[blk] * 4, out_specs=[blk] * 3,
        compiler_params=_params(("arbitrary",), 32), exchange=exchange)


def _small_update(gathered, w, m, v):
    rows = w.shape[0]

    def body(ga_ref, w_ref, m_ref, v_ref, g_ref, d_ref, nm_ref, nv_ref):
        g = ga_ref[0:rows, :]
        for dev in range(1, 8):
            g = g + ga_ref[dev * rows:(dev + 1) * rows, :]
        g_ref[...] = g
        d_ref[...], nm_ref[...], nv_ref[...] = _adamw_math(w_ref[...], g, m_ref[...], v_ref[...])

    return pl.pallas_call(
        body, name="small_update", out_shape=[_sds(w.shape, F32)] * 4,
        in_specs=[_VM] * 4, out_specs=[_VM] * 4,
    )(gathered, w, m, v)


SMALL = ("ffn1_norm", "mix_norm", "ffn2_norm", "final_norm", "pool_scale", "pool_w_group", "loss")
BIG = ("ffn1_w_gate_up", "ffn1_w_down", "w_in", "w_branch_pool", "w_branch_attn", "w_out",
       "ffn2_w_gate_up", "ffn2_w_down")
ORDER = ("ffn1_norm", "ffn1_w_gate_up", "ffn1_w_down", "mix_norm", "w_in", "pool_w_group", "pool_scale",
         "w_branch_pool", "w_branch_attn", "w_out", "ffn2_norm", "ffn2_w_gate_up", "ffn2_w_down", "final_norm")
SMALL_ROWS = 560


def _pack_small(t):
    parts = []
    for k in SMALL:
        rows = t[k].reshape(-1, 128) if k in t else jnp.zeros((1, 128), F32)
        parts.append(jnp.pad(rows, ((0, -rows.shape[0] % 8), (0, 0))))
    packed = jnp.concatenate(parts, axis=0)
    assert packed.shape == (SMALL_ROWS, 128), packed.shape
    return packed


def _unpack_small(packed, like):
    out, at = {}, 0
    for k in SMALL:
        n = like[k].size // 128 if k in like else 1
        out[k] = packed[at:at + n].reshape(like[k].shape) if k in like else packed[at, 0]
        at += n + (-n % 8)
    return out


def _halves(g):
    return g.reshape(NSH, 2, g.shape[1] // 2, g.shape[2])


def kernel(x, ffn1_norm, ffn1_w_gate_up, ffn1_w_down, mix_norm, w_in, pool_w_group, pool_scale, w_branch_pool, w_branch_attn, w_out, ffn2_norm, ffn2_w_gate_up, ffn2_w_down, final_norm, loss_target, m_ffn1_norm, m_ffn1_w_gate_up, m_ffn1_w_down, m_mix_norm, m_w_in, m_pool_w_group, m_pool_scale, m_w_branch_pool, m_w_branch_attn, m_w_out, m_ffn2_norm, m_ffn2_w_gate_up, m_ffn2_w_down, m_final_norm, v_ffn1_norm, v_ffn1_w_gate_up, v_ffn1_w_down, v_mix_norm, v_w_in, v_pool_w_group, v_pool_scale, v_w_branch_pool, v_w_branch_attn, v_w_out, v_ffn2_norm, v_ffn2_w_gate_up, v_ffn2_w_down, v_final_norm):
    wts = dict(ffn1_norm=ffn1_norm, ffn1_w_gate_up=ffn1_w_gate_up, ffn1_w_down=ffn1_w_down, mix_norm=mix_norm,
               w_in=w_in, pool_w_group=pool_w_group, pool_scale=pool_scale, w_branch_pool=w_branch_pool,
               w_branch_attn=w_branch_attn, w_out=w_out, ffn2_norm=ffn2_norm, ffn2_w_gate_up=ffn2_w_gate_up,
               ffn2_w_down=ffn2_w_down, final_norm=final_norm)
    mom = dict(ffn1_norm=m_ffn1_norm, ffn1_w_gate_up=m_ffn1_w_gate_up, ffn1_w_down=m_ffn1_w_down,
               mix_norm=m_mix_norm, w_in=m_w_in, pool_w_group=m_pool_w_group, pool_scale=m_pool_scale,
               w_branch_pool=m_w_branch_pool, w_branch_attn=m_w_branch_attn, w_out=m_w_out,
               ffn2_norm=m_ffn2_norm, ffn2_w_gate_up=m_ffn2_w_gate_up, ffn2_w_down=m_ffn2_w_down,
               final_norm=m_final_norm)
    var = dict(ffn1_norm=v_ffn1_norm, ffn1_w_gate_up=v_ffn1_w_gate_up, ffn1_w_down=v_ffn1_w_down,
               mix_norm=v_mix_norm, w_in=v_w_in, pool_w_group=v_pool_w_group, pool_scale=v_pool_scale,
               w_branch_pool=v_w_branch_pool, w_branch_attn=v_w_branch_attn, w_out=v_w_out,
               ffn2_norm=v_ffn2_norm, ffn2_w_gate_up=v_ffn2_w_gate_up, ffn2_w_down=v_ffn2_w_down,
               final_norm=v_final_norm)

    c_idx = lax.axis_index("c").astype(jnp.int32).reshape(1)
    me_idx = (2 * lax.axis_index("x") + lax.axis_index("y")).astype(jnp.int32).reshape(1)
    place = jnp.concatenate([me_idx, c_idx])
    x0, tgt = x[0], loss_target[0]
    wgrp = pool_w_group[0].astype(BF16)
    g1, gm, g2, gf = ffn1_norm, mix_norm, ffn2_norm, final_norm.reshape(1, D)
    grad, delta, new_m, new_v = {}, {}, {}, {}

    def pair_sums(keys, parts, got):
        return [_pair_sum(parts[i], got[i], c_idx, "pair_sum_" + k) for i, k in enumerate(keys)]

    def chip_sums(keys, chip_parts, owned):
        return [_chip_sum(chip_parts[i], owned[i], place, "chip_sum_" + k) for i, k in enumerate(keys)]

    def adamw(k, exchange=None):
        res = _adamw(wts[k][0], grad[k][0], mom[k][0], var[k][0], "adamw_" + k, exchange=exchange)
        outs, landed = (res, None) if exchange is None else res
        delta[k], new_m[k], new_v[k] = (o.reshape(wts[k].shape) for o in outs)
        return landed

    own = {k: _cast_into_block(wts[k][0], me_idx, "cast_" + k) for k in BIG}
    first, late = ("ffn1_w_gate_up", "ffn1_w_down"), ("w_branch_pool", "w_branch_attn", "w_out",
                                                       "ffn2_w_gate_up", "ffn2_w_down")
    full = dict(zip(first, _exchange_alone(_ex_gather([own[k] for k in first]), "gather_ffn1")))
    wgu1, wd1 = full["ffn1_w_gate_up"], full["ffn1_w_down"].reshape(DFF, D)
    (h1, n1, gu1), (win,) = _ffn_fwd(x0, g1, wgu1, wd1, "ffn1_fwd", exchange=_ex_gather([own["w_in"]]))
    u, xp, q, k, v, gp, gs = _mix_in(h1, gm, win)
    (o_sb, ctot), landed = _attn_fwd(q, k, v, exchange=_ex_gather([own[k_] for k_ in late]))
    full.update(zip(late, landed))
    wbp, wba, wout = full["w_branch_pool"], full["w_branch_attn"], full["w_out"].reshape(D, D)
    wgu2, wd2 = full["ffn2_w_gate_up"], full["ffn2_w_down"].reshape(DFF, D)
    h2, pm, p, yp, ys, mm = _mix_out(h1, xp, o_sb, gp, gs, wgrp, pool_scale, wbp, wba, wout)
    h3, n3, gu3 = _ffn_fwd(h2, g2, wgu2, wd2, "ffn2_fwd")
    dh3, loss_row, d_gf = _head(h3, tgt, gf)

    dh2, dgu3, a3, d_g2 = _ffn_bwd(dh3, h2, g2, gu3, wgu2, wd2, "ffn2_bwd")
    ka = ("ffn2_w_gate_up", "ffn2_w_down")
    pa = [_halves(_wgrad(n3, dgu3, NSH, 512, "wgrad_gu2")),
          _halves(_wgrad(a3, dh3, 1, FFS, "wgrad_d2").reshape(NSH, DFF // NSH, D))]
    (dlg, dyp, dys, do_sb, dyg, dxp, d_scale), got_a = _mix_bwd_out(
        dh2, gp, gs, yp, ys, pm, wgrp, pool_scale, wbp, wba, wout, exchange=_ex_pair_swap(pa))
    chip_a = pair_sums(ka, pa, got_a)
    kb = ("w_out", "w_branch_pool", "w_branch_attn")
    pb = [_halves(_wgrad(mm, dh2, 1, 512, "wgrad_out").reshape(NSH, D // NSH, D)),
          _halves(_wgrad(p, dyp, NSH, PW, "wgrad_bp")), _halves(_wgrad(o_sb, dys, NSH, SBW, "wgrad_ba"))]
    chip_b = pair_sums(kb, pb, _exchange_alone(_ex_pair_swap(pb), "pair_swap_mix"))
    (dq, dk, dv), owned_ab = _attn_bwd(q, k, v, do_sb, ctot, exchange=_ex_scatter(chip_a + chip_b))
    halves_ab = chip_sums(ka + kb, chip_a + chip_b, owned_ab)
    dproj = jnp.concatenate([dxp, dq, dk, dv, dlg], axis=1)
    (dh1, d_gm), both_ab = _mix_bwd_in(dh2, h1, gm, dproj, win, exchange=_ex_share(halves_ab))
    for i, k_ in enumerate(ka + kb):
        grad[k_] = both_ab[i].reshape(wts[k_].shape)
    pc = [_halves(_wgrad(u, dproj, NSH, 512, "wgrad_in"))]
    (dx, dgu1, a1, d_g1), got_c = _ffn_bwd(dh1, x0, g1, gu1, wgu1, wd1, "ffn1_bwd", exchange=_ex_pair_swap(pc))
    chip_c = pair_sums(("w_in",), pc, got_c)
    pd, owned_c = _wgrad(n1, dgu1, NSH, 512, "wgrad_gu1", exchange=_ex_scatter(chip_c))
    pd = [_halves(pd)]
    pe, got_d = _wgrad(a1, dh1, 1, FFS, "wgrad_d1", exchange=_ex_pair_swap(pd))
    pe = [_halves(pe.reshape(NSH, DFF // NSH, D))]
    chip_d = pair_sums(("ffn1_w_gate_up",), pd, got_d)
    got_e = adamw("ffn2_w_gate_up", exchange=_ex_pair_swap(pe))
    chip_e = pair_sums(("ffn1_w_down",), pe, got_e)
    halves_c = chip_sums(("w_in",), chip_c, owned_c)
    owned_de = adamw("ffn2_w_down", exchange=_ex_scatter(chip_d + chip_e))
    kde = ("ffn1_w_gate_up", "ffn1_w_down")
    halves_de = chip_sums(kde, chip_d + chip_e, owned_de)
    both_cde = adamw("w_out", exchange=_ex_share(halves_c + halves_de))
    for i, k_ in enumerate(("w_in",) + kde):
        grad[k_] = both_cde[i].reshape(wts[k_].shape)
    for k_ in ("w_branch_pool", "w_branch_attn", "w_in") + kde:
        adamw(k_)

    small_g = dict(ffn1_norm=d_g1, mix_norm=d_gm, ffn2_norm=d_g2, final_norm=d_gf, pool_scale=d_scale,
                   pool_w_group=_wgrad_groups(pm, dyg), loss=loss_row)
    gathered = _gather_small(_pack_small(small_g))
    sg, sd, sm, sv = _small_update(gathered, _pack_small(wts), _pack_small(mom), _pack_small(var))
    sums = _unpack_small(sg, wts)
    loss = sums.pop("loss")
    grad.update(sums)
    for dst, packed in ((delta, sd), (new_m, sm), (new_v, sv)):
        vals = _unpack_small(packed, wts)
        vals.pop("loss")
        dst.update(vals)
    return (loss, dx[None], *[grad[k_] for k_ in ORDER], *[delta[k_] for k_ in ORDER],
            *[new_m[k_] for k_ in ORDER], *[new_v[k_] for k_ in ORDER])
```

```python
import functools

import jax
import jax.numpy as jnp
from jax import lax
from jax.experimental import pallas as pl
from jax.experimental.pallas import tpu as pltpu

F32 = jnp.float32
BF16 = jnp.bfloat16

S = 2048
D = 1024
DFF = 2816
FFS = 2 * DFF // 4
NSH = 4
PW = 512
PG = 128
POOL_WINDOWS = (2, 4, 8, 16)
HALO = 16
SBW = 512
DH = 64
EPS = 1e-6
SCALE = 0.125
LOG2E = 1.4426950408889634
TA = 256
QB = 2
MIB = 1024 * 1024

LR, B1, B2, AEPS, WD, STEP = 0.001, 0.9, 0.999, 1e-08, 0.01, 10

_VM = pl.BlockSpec(memory_space=pltpu.VMEM)
_ANY = pl.BlockSpec(memory_space=pl.ANY)
MESH = pl.DeviceIdType.MESH


def _nn(a, b):
    return jnp.dot(a, b, preferred_element_type=F32)


def _nt(a, b):
    return lax.dot_general(a, b, (((1,), (1,)), ((), ())), preferred_element_type=F32)


def _tn(a, b):
    return lax.dot_general(a, b, (((0,), (0,)), ((), ())), preferred_element_type=F32)


def _params(sem, vmem_mib):
    return pltpu.CompilerParams(dimension_semantics=sem, vmem_limit_bytes=vmem_mib * MIB)


def _rows(tm, width):
    return pl.BlockSpec((tm, width), lambda i: (i, 0))


def _fixed(shape):
    return pl.BlockSpec(shape, lambda *_: (0,) * len(shape))


def _sds(shape, dtype):
    return pltpu.HBM(shape, dtype)


def _in_hbm(args):
    return [pltpu.with_memory_space_constraint(a, pltpu.HBM) for a in args]


def _stage(pairs):
    @pl.when(pl.program_id(0) == 0)
    def _():
        for src, dst in pairs:
            pltpu.sync_copy(src, dst)


def _vmem_like(*arrays):
    return [pltpu.VMEM(a.shape, a.dtype) for a in arrays]


class Exchange:
    def __init__(self, arrays, landing, aliases, n_sems, start, finish):
        self.arrays, self.landing, self.aliases, self.n_sems = list(arrays), list(landing), dict(aliases), n_sems
        self.start, self.finish = start, finish


def _call(body, args, *, name, grid, in_specs, out_specs, out_shape, scratch_shapes=(), compiler_params=None,
          exchange=None):
    if exchange is None:
        return pl.pallas_call(body, name=name, grid=grid, in_specs=in_specs, out_specs=out_specs,
                              out_shape=out_shape, scratch_shapes=list(scratch_shapes),
                              compiler_params=compiler_params)(*_in_hbm(args))
    ex = exchange
    n_in, n_out, n_scr = len(in_specs), len(out_specs), len(scratch_shapes)
    na, nl = len(ex.arrays), len(ex.landing)

    def hosted(*refs):
        at = [0]

        def take(n):
            at[0] += n
            return refs[at[0] - n:at[0]]

        k_in, e_in, k_out, e_out, k_scr = take(n_in), take(na), take(n_out), take(nl), take(n_scr)
        ssem, rsem = take(2)
        ids = [pl.program_id(a) for a in range(len(grid))]
        first = functools.reduce(jnp.logical_and, [i == 0 for i in ids])
        last = functools.reduce(jnp.logical_and, [i == g - 1 for i, g in zip(ids, grid)])

        @pl.when(first)
        def _():
            ex.start(e_in, e_out, ssem, rsem)

        body(*k_in, *k_out, *k_scr)

        @pl.when(last)
        def _():
            ex.finish(e_in, e_out, ssem, rsem)

    outs = pl.pallas_call(
        hosted, name=name, grid=grid,
        in_specs=list(in_specs) + [_ANY] * na, out_specs=list(out_specs) + [_ANY] * nl,
        out_shape=list(out_shape) + ex.landing,
        scratch_shapes=list(scratch_shapes) + [pltpu.SemaphoreType.DMA((ex.n_sems,))] * 2,
        input_output_aliases={n_in + i: n_out + j for i, j in ex.aliases.items()},
        compiler_params=compiler_params,
    )(*_in_hbm(args), *_in_hbm(ex.arrays))
    return outs[:n_out], outs[n_out:]


def _exchange_alone(ex, name):
    def body(*refs):
        na, nl = len(ex.arrays), len(ex.landing)
        ex.start(refs[:na], refs[na:na + nl], refs[-2], refs[-1])
        ex.finish(refs[:na], refs[na:na + nl], refs[-2], refs[-1])

    return pl.pallas_call(
        body, name=name, in_specs=[_ANY] * len(ex.arrays), out_specs=[_ANY] * len(ex.landing),
        out_shape=ex.landing, scratch_shapes=[pltpu.SemaphoreType.DMA((ex.n_sems,))] * 2,
        input_output_aliases=ex.aliases,
    )(*_in_hbm(ex.arrays))


def _rms(x):
    r = lax.rsqrt(jnp.mean(x * x, axis=-1, keepdims=True) + EPS)
    return r, x * r


def _rms_bwd(dn, xr, r, gain):
    dng = dn * gain
    dx = r * (dng - xr * jnp.mean(dng * xr, axis=-1, keepdims=True))
    return dx, jnp.sum(dn * xr, axis=0, keepdims=True)


def _ffn_fwd(x, gain, wgu, wd, name, exchange=None):
    tm = 256

    def body(x_ref, g_ref, wgu_hbm, wd_hbm, h_ref, n_ref, gu_ref, wgu_ref, wd_ref):
        _stage([(wgu_hbm, wgu_ref), (wd_hbm, wd_ref)])
        x = x_ref[...]
        _, xr = _rms(x)
        n = (xr * g_ref[...]).astype(BF16)
        n_ref[...] = n
        acc = jnp.zeros((tm, D), F32)
        for j in range(2):
            g = _nn(n, wgu_ref[j])
            u = _nn(n, wgu_ref[2 + j])
            gu_ref[:, j * FFS:(j + 1) * FFS] = g.astype(BF16)
            gu_ref[:, (2 + j) * FFS:(3 + j) * FFS] = u.astype(BF16)
            a = (g * jax.nn.sigmoid(g) * u).astype(BF16)
            acc = acc + _nn(a, wd_ref[j * FFS:(j + 1) * FFS, :])
        h_ref[...] = x + 0.5 * acc

    return _call(
        body, (x, gain, wgu, wd), name=name, grid=(S // tm,),
        in_specs=[_rows(tm, D), _fixed((1, D)), _ANY, _ANY],
        out_specs=[_rows(tm, D), _rows(tm, D), _rows(tm, 4 * FFS)],
        out_shape=[_sds((S, D), F32), _sds((S, D), BF16), _sds((S, 4 * FFS), BF16)],
        scratch_shapes=_vmem_like(wgu, wd),
        compiler_params=_params(("arbitrary",), 56), exchange=exchange)


def _ffn_bwd(dh, x, gain, gu, wgu, wd, name, exchange=None):
    tm = 256

    def body(dh_ref, x_ref, g_ref, gu_ref, wgu_hbm, wd_hbm, dx_ref, dgu_ref, a_ref, dg_ref, wgu_ref, wd_ref):
        _stage([(wgu_hbm, wgu_ref), (wd_hbm, wd_ref)])
        dh = dh_ref[...]
        dhb = dh.astype(BF16)
        dn = jnp.zeros((tm, D), F32)
        for j in range(2):
            g = gu_ref[:, j * FFS:(j + 1) * FFS].astype(F32)
            u = gu_ref[:, (2 + j) * FFS:(3 + j) * FFS].astype(F32)
            da = 0.5 * _nt(dhb, wd_ref[j * FFS:(j + 1) * FFS, :])
            sg = jax.nn.sigmoid(g)
            si = g * sg
            a_ref[:, j * FFS:(j + 1) * FFS] = (0.5 * si * u).astype(BF16)
            dgb = (da * u * (sg * (1.0 + g * (1.0 - sg)))).astype(BF16)
            dub = (da * si).astype(BF16)
            dgu_ref[:, j * FFS:(j + 1) * FFS] = dgb
            dgu_ref[:, (2 + j) * FFS:(3 + j) * FFS] = dub
            dn = dn + _nt(dgb, wgu_ref[j]) + _nt(dub, wgu_ref[2 + j])
        r, xr = _rms(x_ref[...])
        dx, dgain = _rms_bwd(dn, xr, r, g_ref[...])
        dx_ref[...] = dh + dx

        @pl.when(pl.program_id(0) == 0)
        def _():
            dg_ref[...] = jnp.zeros_like(dg_ref)

        dg_ref[...] += dgain

    return _call(
        body, (dh, x, gain, gu, wgu, wd), name=name, grid=(S // tm,),
        in_specs=[_rows(tm, D), _rows(tm, D), _fixed((1, D)), _rows(tm, 4 * FFS), _ANY, _ANY],
        out_specs=[_rows(tm, D), _rows(tm, 4 * FFS), _rows(tm, DFF), _fixed((1, D))],
        out_shape=[_sds((S, D), F32), _sds((S, 4 * FFS), BF16), _sds((S, DFF), BF16), _sds((1, D), F32)],
        scratch_shapes=_vmem_like(wgu, wd),
        compiler_params=_params(("arbitrary",), 56), exchange=exchange)


def _head(h, target, gain):
    tm = 512

    def body(h_ref, t_ref, g_ref, dh_ref, loss_ref, dg_ref):
        gain = g_ref[...]
        r, hr = _rms(h_ref[...])
        err = hr * gain - t_ref[...]
        dy = err * (1.0 / D)
        dh, dgain = _rms_bwd(dy, hr, r, gain)
        dh_ref[...] = dh

        @pl.when(pl.program_id(0) == 0)
        def _():
            dg_ref[...] = jnp.zeros_like(dg_ref)
            loss_ref[...] = jnp.zeros_like(loss_ref)

        dg_ref[...] += dgain
        loss_ref[...] += jnp.full((1, 128), (0.5 / D) * jnp.sum(err * err), F32)

    return pl.pallas_call(
        body, name="head", grid=(S // tm,),
        in_specs=[_rows(tm, D), _rows(tm, D), _fixed((1, D))],
        out_specs=[_rows(tm, D), _fixed((1, 128)), _fixed((1, D))],
        out_shape=[_sds((S, D), F32), _sds((1, 128), F32), _sds((1, D), F32)],
        compiler_params=_params(("arbitrary",), 40),
    )(*_in_hbm((h, target, gain)))


def _mix_in(h, gain, w_in):
    tm = 512

    def body(h_ref, g_ref, w_hbm, u_ref, xp_ref, q_ref, k_ref, v_ref, gp_ref, gs_ref, w_ref):
        _stage([(w_hbm, w_ref)])
        _, hr = _rms(h_ref[...])
        u = (hr * g_ref[...]).astype(BF16)
        u_ref[...] = u
        p0 = _nn(u, w_ref[0])
        xp_ref[...] = p0[:, :PW]
        q_ref[...] = p0[:, PW:].astype(BF16)
        p1 = _nn(u, w_ref[1])
        k_ref[...] = p1[:, :SBW].astype(BF16)
        v_ref[...] = p1[:, SBW:].astype(BF16)
        gp_ref[...] = jax.nn.sigmoid(_nn(u, w_ref[2])).astype(BF16)
        gs_ref[...] = jax.nn.sigmoid(_nn(u, w_ref[3])).astype(BF16)

    return pl.pallas_call(
        body, name="mix_in", grid=(S // tm,),
        in_specs=[_rows(tm, D), _fixed((1, D)), _ANY],
        out_specs=[_rows(tm, D), _rows(tm, PW), _rows(tm, SBW), _rows(tm, SBW), _rows(tm, SBW),
                   _rows(tm, D), _rows(tm, D)],
        out_shape=[_sds((S, D), BF16), _sds((S, PW), F32), _sds((S, SBW), BF16), _sds((S, SBW), BF16),
                   _sds((S, SBW), BF16), _sds((S, D), BF16), _sds((S, D), BF16)],
        scratch_shapes=_vmem_like(w_in),
        compiler_params=_params(("arbitrary",), 48),
    )(*_in_hbm((h, gain, w_in)))


def _hilo_dot(x, tri):
    hi = x.astype(BF16)
    lo = (x - hi.astype(F32)).astype(BF16)
    return _nn(hi, tri) + _nn(lo, tri)


def _log_terms(qk):
    z2 = qk * (SCALE * LOG2E)
    e = jnp.exp2(-jnp.abs(z2))
    lb = jnp.minimum(z2, 0.0) - jnp.log2(1.0 + e)
    return e, lb, lb - z2


def _head_masks():
    lane = lax.broadcasted_iota(jnp.int32, (1, 2 * DH), 1)
    return (lane < DH, lane >= DH)


def _attn_fwd(q, k, v, exchange=None):
    T = TA

    def body(q_ref, k_ref, v_ref, o_ref, c_ref):
        i2 = 2 * pl.program_id(1)
        row = lax.broadcasted_iota(jnp.int32, (T, T), 0)
        col = lax.broadcasted_iota(jnp.int32, (T, T), 1)
        after = (row > col).astype(BF16)
        causal = col < row
        masks = _head_masks()
        qms = {}
        for b in range(QB):
            q2 = q_ref[b * T:(b + 1) * T, :]
            for h, hm in enumerate(masks):
                qms[b, h] = jnp.where(hm, q2, jnp.zeros_like(q2))

        def block(j, carries, os, kinds):
            rows = pl.ds(pl.multiple_of(j * T, T), T)
            kj, vj = k_ref[rows, :], v_ref[rows, :]
            vms = [jnp.where(hm, vj, jnp.zeros_like(vj)) for hm in masks]
            chains = [(b, h) for b in range(QB) if kinds[b] for h in range(2)]
            qks = {c: _nt(qms[c], kj) for c in chains}
            lbs, l1ms = {}, {}
            for c in chains:
                _, lbs[c], l1m = _log_terms(qks[c])
                l1ms[c] = jnp.where(causal, l1m, 0.0) if kinds[c[0]] == "diag" else l1m
            cins = {c: _hilo_dot(l1ms[c], after) for c in chains}
            carries, os = dict(carries), list(os)
            for c in chains:
                a = jnp.exp2(lbs[c] + cins[c] + carries[c])
                if kinds[c[0]] == "diag":
                    a = jnp.where(causal, a, 0.0)
                os[c[0]] = os[c[0]] + _nn(a.astype(BF16), vms[c[1]])
            for c in chains:
                carries[c] = carries[c] + jnp.sum(l1ms[c], axis=1, keepdims=True)
            return carries, tuple(os)

        carries = {(b, h): jnp.zeros((T, 1), F32) for b in range(QB) for h in range(2)}
        os = tuple(jnp.zeros((T, 2 * DH), F32) for _ in range(QB))
        carries, os = block(i2 + 1, carries, os, (None, "diag"))
        carries, os = block(i2, carries, os, ("diag", "full"))
        carries, os = lax.fori_loop(
            0, i2, lambda jj, c: block(i2 - 1 - jj, c[0], c[1], ("full", "full")), (carries, os))
        for b in range(QB):
            o_ref[b * T:(b + 1) * T, :] = os[b].astype(BF16)
            c_ref[b * T:(b + 1) * T, :] = jnp.where(masks[0], carries[b, 0], carries[b, 1])

    blk = pl.BlockSpec((QB * T, 2 * DH), lambda p, i: (i, p))
    full = pl.BlockSpec((S, 2 * DH), lambda p, i: (0, p))
    return _call(
        body, (q, k, v), name="attn_fwd", grid=(SBW // (2 * DH), S // (QB * T)),
        in_specs=[blk, full, full], out_specs=[blk, blk],
        out_shape=[_sds((S, SBW), BF16), _sds((S, SBW), F32)],
        compiler_params=_params(("arbitrary", "arbitrary"), 40), exchange=exchange)


def _attn_bwd(q, k, v, do, ctot, exchange=None):
    T = TA
    nq = S // (QB * T)

    def body(q_ref, k_ref, v_ref, do_ref, c_ref, dq_ref, dk_ref, dv_ref, dk_acc, dv_acc):
        step = pl.program_id(1)
        i2 = 2 * step

        @pl.when(step == 0)
        def _():
            dk_acc[...] = jnp.zeros_like(dk_acc)
            dv_acc[...] = jnp.zeros_like(dv_acc)

        row = lax.broadcasted_iota(jnp.int32, (T, T), 0)
        col = lax.broadcasted_iota(jnp.int32, (T, T), 1)
        upto = (row <= col).astype(BF16)
        before = (row < col).astype(BF16)
        causal = col < row
        masks = _head_masks()
        qms, doms, ctots = {}, {}, {}
        for b in range(QB):
            q2, do2 = q_ref[b * T:(b + 1) * T, :], do_ref[b * T:(b + 1) * T, :]
            for h, hm in enumerate(masks):
                qms[b, h] = jnp.where(hm, q2, jnp.zeros_like(q2))
                doms[b, h] = jnp.where(hm, do2, jnp.zeros_like(do2))
                ctots[b, h] = c_ref[b * T:(b + 1) * T, h * DH:h * DH + 1]

        def block(j, sums, dqs, kinds):
            rows = pl.ds(pl.multiple_of(j * T, T), T)
            kj, vj = k_ref[rows, :], v_ref[rows, :]
            kms = [jnp.where(hm, kj, jnp.zeros_like(kj)) for hm in masks]
            chains = [(b, h) for b in range(QB) if kinds[b] for h in range(2)]
            diag = {c: kinds[c[0]] == "diag" for c in chains}
            qks = {c: _nt(qms[c], kj) for c in chains}
            das = {c: _nt(doms[c], vj) for c in chains}
            es, lbs, l1ms = {}, {}, {}
            for c in chains:
                es[c], lbs[c], l1m = _log_terms(qks[c])
                l1ms[c] = jnp.where(causal, l1m, 0.0) if diag[c] else l1m
            pins = {c: _hilo_dot(l1ms[c], upto) for c in chains}
            a_s, dls = {}, {}
            for c in chains:
                a = jnp.exp2(lbs[c] + (ctots[c] - sums[c][0]) - pins[c])
                if diag[c]:
                    a = jnp.where(causal, a, 0.0)
                a_s[c] = a.astype(BF16)
                dls[c] = das[c] * a
            pexs = {c: _hilo_dot(dls[c], before) for c in chains}
            dzbs = {}
            for c in chains:
                rinv = 1.0 / (1.0 + es[c])
                pos = qks[c] >= 0.0
                beta = jnp.where(pos, rinv, es[c] * rinv)
                omb = jnp.where(pos, es[c] * rinv, rinv)
                dz = (dls[c] * omb - (pexs[c] + sums[c][1]) * beta) * SCALE
                if diag[c]:
                    dz = jnp.where(causal, dz, 0.0)
                dzbs[c] = dz.astype(BF16)
            dqs = list(dqs)
            for c in chains:
                dqs[c[0]] = dqs[c[0]] + _nn(dzbs[c], kms[c[1]])
            dk_acc[rows, :] += functools.reduce(jnp.add, [_tn(dzbs[c], qms[c]) for c in chains])
            dv_acc[rows, :] += functools.reduce(jnp.add, [_tn(a_s[c], doms[c]) for c in chains])
            sums = dict(sums)
            for c in chains:
                sums[c] = (sums[c][0] + jnp.sum(l1ms[c], axis=1, keepdims=True),
                           sums[c][1] + jnp.sum(dls[c], axis=1, keepdims=True))
            return sums, tuple(dqs)

        zero = jnp.zeros((T, 1), F32)
        sums = {(b, h): (zero, zero) for b in range(QB) for h in range(2)}
        dqs = tuple(jnp.zeros((T, 2 * DH), F32) for _ in range(QB))
        sums, dqs = lax.fori_loop(0, i2, lambda j, c: block(j, c[0], c[1], ("full", "full")), (sums, dqs))
        sums, dqs = block(i2, sums, dqs, ("diag", "full"))
        _, dqs = block(i2 + 1, sums, dqs, (None, "diag"))
        for b in range(QB):
            dq_ref[b * T:(b + 1) * T, :] = dqs[b].astype(BF16)

        @pl.when(step == nq - 1)
        def _():
            dk_ref[...] = dk_acc[...].astype(BF16)
            dv_ref[...] = dv_acc[...].astype(BF16)

    blk = pl.BlockSpec((QB * T, 2 * DH), lambda p, i: (i, p))
    full = pl.BlockSpec((S, 2 * DH), lambda p, i: (0, p))
    return _call(
        body, (q, k, v, do, ctot), name="attn_bwd", grid=(SBW // (2 * DH), nq),
        in_specs=[blk, full, full, blk, blk], out_specs=[blk, full, full],
        out_shape=[_sds((S, SBW), BF16), _sds((S, SBW), BF16), _sds((S, SBW), BF16)],
        scratch_shapes=[pltpu.VMEM((S, 2 * DH), F32), pltpu.VMEM((S, 2 * DH), F32)],
        compiler_params=_params(("arbitrary", "arbitrary"), 40), exchange=exchange)


def _pool_counts(first_row, tm):
    pos = first_row + lax.broadcasted_iota(jnp.int32, (tm, 1), 0)
    return [jnp.minimum(pos + 1, w).astype(F32) for w in POOL_WINDOWS]


def _mix_out(h, xp, o_sb, gp, gs, w_group, scale, w_bp, w_ba, w_out):
    tm = 512

    def body(h_ref, xp_ref, o_ref, gp_ref, gs_ref, wg_hbm, sc_ref, wbp_hbm, wba_hbm, wo_hbm,
             h2_ref, pm_ref, p_ref, yp_ref, ys_ref, m_ref, halo, wg_ref, wbp_ref, wba_ref, wo_ref):
        _stage([(wg_hbm, wg_ref), (wbp_hbm, wbp_ref), (wba_hbm, wba_ref), (wo_hbm, wo_ref)])
        i = pl.program_id(0)

        @pl.when(i == 0)
        def _():
            halo[...] = jnp.zeros_like(halo)

        xp = xp_ref[...]
        ext = jnp.concatenate([halo[...], xp], axis=0)
        halo[...] = xp[tm - HALO:, :]
        counts = _pool_counts(i * tm, tm)
        for gi in range(len(POOL_WINDOWS)):
            lanes = slice(gi * PG, (gi + 1) * PG)
            win = ext[:, lanes]
            for step in range(gi + 1):
                win = win + pltpu.roll(win, 1 << step, 0)
            pm = (win[HALO:, :] / counts[gi] - xp[:, lanes]).astype(BF16)
            pm_ref[:, lanes] = pm
            p_ref[:, lanes] = (_nn(pm, wg_ref[gi]) * sc_ref[:, lanes]).astype(BF16)
        pb = p_ref[...]
        ob = o_ref[...]
        for j in range(NSH):
            cols = slice(j * (D // NSH), (j + 1) * (D // NSH))
            yp = _nn(pb, wbp_ref[j])
            ys = _nn(ob, wba_ref[j])
            yp_ref[:, cols] = yp.astype(BF16)
            ys_ref[:, cols] = ys.astype(BF16)
            m_ref[:, cols] = (gp_ref[:, cols].astype(F32) * yp + gs_ref[:, cols].astype(F32) * ys).astype(BF16)
        h2_ref[...] = h_ref[...] + _nn(m_ref[...], wo_ref[...])

    return pl.pallas_call(
        body, name="mix_out", grid=(S // tm,),
        in_specs=[_rows(tm, D), _rows(tm, PW), _rows(tm, SBW), _rows(tm, D), _rows(tm, D),
                  _ANY, _fixed((1, PW)), _ANY, _ANY, _ANY],
        out_specs=[_rows(tm, D), _rows(tm, PW), _rows(tm, PW), _rows(tm, D), _rows(tm, D), _rows(tm, D)],
        out_shape=[_sds((S, D), F32), _sds((S, PW), BF16), _sds((S, PW), BF16), _sds((S, D), BF16),
                   _sds((S, D), BF16), _sds((S, D), BF16)],
        scratch_shapes=[pltpu.VMEM((HALO, PW), F32)] + _vmem_like(w_group, w_bp, w_ba, w_out),
        compiler_params=_params(("arbitrary",), 48),
    )(*_in_hbm((h, xp, o_sb, gp, gs, w_group, scale, w_bp, w_ba, w_out)))


def _mix_bwd_out(dh, gp, gs, yp, ys, pm, w_group, scale, w_bp, w_ba, w_out, exchange=None):
    tm = 512
    nt = S // tm

    def body(dh_ref, gp_ref, gs_ref, yp_ref, ys_ref, pm_ref, wg_hbm, sc_ref, wbp_hbm, wba_hbm, wo_hbm,
             dlg_ref, dyp_ref, dys_ref, do_ref, dyg_ref, dxp_ref, dsc_ref, halo, wg_ref, wbp_ref, wba_ref, wo_ref):
        _stage([(wg_hbm, wg_ref), (wbp_hbm, wbp_ref), (wba_hbm, wba_ref), (wo_hbm, wo_ref)])
        step = pl.program_id(0)

        @pl.when(step == 0)
        def _():
            halo[...] = jnp.zeros_like(halo)
            dsc_ref[...] = jnp.zeros_like(dsc_ref)

        dm = _nt(dh_ref[...].astype(BF16), wo_ref[...])
        gp = gp_ref[...].astype(F32)
        gs = gs_ref[...].astype(F32)
        yp = yp_ref[...].astype(F32)
        ys = ys_ref[...].astype(F32)
        dlg_ref[:, :D] = (dm * yp * gp * (1.0 - gp)).astype(BF16)
        dlg_ref[:, D:] = (dm * ys * gs * (1.0 - gs)).astype(BF16)
        dyp_ref[...] = (dm * gp).astype(BF16)
        dys_ref[...] = (dm * gs).astype(BF16)
        dp = jnp.zeros((tm, PW), F32)
        do = jnp.zeros((tm, SBW), F32)
        for j in range(NSH):
            cols = slice(j * (D // NSH), (j + 1) * (D // NSH))
            dp = dp + _nt(dyp_ref[:, cols], wbp_ref[j])
            do = do + _nt(dys_ref[:, cols], wba_ref[j])
        do_ref[...] = do.astype(BF16)
        counts = _pool_counts((nt - 1 - step) * tm, tm)
        dscale = []
        for gi in range(len(POOL_WINDOWS)):
            lanes = slice(gi * PG, (gi + 1) * PG)
            dpg = dp[:, lanes]
            dscale.append(jnp.sum(dpg * _nn(pm_ref[:, lanes], wg_ref[gi]), axis=0, keepdims=True))
            dyg = (dpg * sc_ref[:, lanes]).astype(BF16)
            dyg_ref[:, lanes] = dyg
            dpm = _nt(dyg, wg_ref[gi])
            per = dpm / counts[gi]
            win = jnp.concatenate([per, halo[:, lanes]], axis=0)
            halo[:, lanes] = per[:HALO, :]
            for s in range(gi + 1):
                win = win + pltpu.roll(win, tm + HALO - (1 << s), 0)
            dxp_ref[:, lanes] = (win[:tm, :] - dpm).astype(BF16)
        dsc_ref[...] += jnp.concatenate(dscale, axis=1)

    rev = lambda width: pl.BlockSpec((tm, width), lambda i: (nt - 1 - i, 0))
    return _call(
        body, (dh, gp, gs, yp, ys, pm, w_group, scale, w_bp, w_ba, w_out), name="mix_bwd_out", grid=(nt,),
        in_specs=[rev(D), rev(D), rev(D), rev(D), rev(D), rev(PW), _ANY, _fixed((1, PW)), _ANY, _ANY, _ANY],
        out_specs=[rev(2 * D), rev(D), rev(D), rev(SBW), rev(PW), rev(PW), _fixed((1, PW))],
        out_shape=[_sds((S, 2 * D), BF16), _sds((S, D), BF16), _sds((S, D), BF16), _sds((S, SBW), BF16),
                   _sds((S, PW), BF16), _sds((S, PW), BF16), _sds((1, PW), F32)],
        scratch_shapes=[pltpu.VMEM((HALO, PW), F32)] + _vmem_like(w_group, w_bp, w_ba, w_out),
        compiler_params=_params(("arbitrary",), 48), exchange=exchange)


def _mix_bwd_in(dh, h, gain, dproj, w_in, exchange=None):
    tm = 512

    def body(dh_ref, h_ref, g_ref, dp_ref, w_hbm, dx_ref, dg_ref, w_ref):
        _stage([(w_hbm, w_ref)])
        du = jnp.zeros((tm, D), F32)
        for j in range(NSH):
            du = du + _nt(dp_ref[:, j * D:(j + 1) * D], w_ref[j])
        r, hr = _rms(h_ref[...])
        dx, dgain = _rms_bwd(du, hr, r, g_ref[...])
        dx_ref[...] = dh_ref[...] + dx

        @pl.when(pl.program_id(0) == 0)
        def _():
            dg_ref[...] = jnp.zeros_like(dg_ref)

        dg_ref[...] += dgain

    return _call(
        body, (dh, h, gain, dproj, w_in), name="mix_bwd_in", grid=(S // tm,),
        in_specs=[_rows(tm, D), _rows(tm, D), _fixed((1, D)), _rows(tm, 4 * D), _ANY],
        out_specs=[_rows(tm, D), _fixed((1, D))],
        out_shape=[_sds((S, D), F32), _sds((1, D), F32)],
        scratch_shapes=_vmem_like(w_in),
        compiler_params=_params(("arbitrary",), 48), exchange=exchange)


def _wgrad(a, b, nblk, ti, name, out_dtype=BF16, exchange=None):
    ka, n = a.shape[1], b.shape[1]
    ns = n // nblk

    def body(a_ref, b_ref, o_ref):
        o_ref[...] = _tn(a_ref[...].astype(BF16), b_ref[...].astype(BF16)).astype(out_dtype)

    res = _call(
        body, (a, b), name=name, grid=(nblk, ka // ti),
        in_specs=[pl.BlockSpec((S, ti), lambda j, i: (0, i)), pl.BlockSpec((S, ns), lambda j, i: (0, j))],
        out_specs=[pl.BlockSpec((None, ti, ns), lambda j, i: (j, i, 0))],
        out_shape=[_sds((nblk, ka, ns), out_dtype)],
        compiler_params=_params(("arbitrary", "arbitrary"), 56), exchange=exchange)
    return res[0] if exchange is None else (res[0][0], res[1])


def _wgrad_groups(pm, dyg):
    def body(a_ref, b_ref, o_ref):
        o_ref[...] = _tn(a_ref[...], b_ref[...])

    col = pl.BlockSpec((S, PG), lambda g: (0, g))
    return pl.pallas_call(
        body, name="wgrad_groups", grid=(PW // PG,),
        in_specs=[col, col], out_specs=pl.BlockSpec((None, PG, PG), lambda g: (g, 0, 0)),
        out_shape=_sds((PW // PG, PG, PG), F32),
        compiler_params=_params(("arbitrary",), 32),
    )(*_in_hbm((pm, dyg)))


def _place():
    x, y, c = lax.axis_index("x"), lax.axis_index("y"), lax.axis_index("c")
    chips = [(1 - x, y), (x, 1 - y), (1 - x, 1 - y)]
    return x, y, c, chips


def _remote(src, dst, ssem, rsem, dev):
    return pltpu.make_async_remote_copy(src_ref=src, dst_ref=dst, send_sem=ssem, recv_sem=rsem,
                                        device_id=dev, device_id_type=MESH)


def _cast_into_block(w, me_idx, name):
    rows, cols = w.shape
    tr = _row_block(rows)

    def body(me_ref, w_ref, o_ref):
        o_ref[...] = w_ref[...].astype(BF16)

    return pl.pallas_call(
        body, name=name, out_shape=_sds((NSH, rows, cols), BF16),
        grid_spec=pltpu.PrefetchScalarGridSpec(
            num_scalar_prefetch=1, grid=(rows // tr,),
            in_specs=[pl.BlockSpec((tr, cols), lambda r, me: (r, 0))],
            out_specs=pl.BlockSpec((None, tr, cols), lambda r, me: (me[0], r, 0))),
        compiler_params=_params(("arbitrary",), 32),
    )(me_idx, *_in_hbm((w,)))


def _ex_gather(bufs):
    n = len(bufs)

    def copies(outs, ssem, rsem, only_first=False):
        x, y, c, chips = _place()
        me, sib = 2 * x + y, (x, y, 1 - c)
        first, relay, last = [], [], []
        for w in range(n):
            half = outs[w].shape[1] // 2
            mine = outs[w].at[me, pl.ds(c * half, half)]
            for k, (px, py) in enumerate(chips):
                sems = (ssem.at[6 * w + k], rsem.at[6 * w + k])
                sib_sems = (ssem.at[6 * w + 3 + k], rsem.at[6 * w + 3 + k])
                first.append(_remote(mine, mine, *sems, (px, py, c)))
                if only_first:
                    continue
                got = outs[w].at[2 * px + py, pl.ds(c * half, half)]
                relay.append((_remote(got, got, *sems, (px, py, c)), _remote(got, got, *sib_sems, sib)))
                theirs = outs[w].at[2 * px + py, pl.ds((1 - c) * half, half)]
                last.append(_remote(theirs, theirs, *sib_sems, sib))
        return first, relay, last

    def start(ins, outs, ssem, rsem):
        for cp in copies(outs, ssem, rsem, only_first=True)[0]:
            cp.start()

    def finish(ins, outs, ssem, rsem):
        first, relay, last = copies(outs, ssem, rsem)
        for arrived, onward in relay:
            arrived.wait_recv()
            onward.start()
        for cp in last:
            cp.wait_recv()
        for cp in first:
            cp.wait_send()
        for _, onward in relay:
            onward.wait_send()

    return Exchange(bufs, [_sds(b.shape, b.dtype) for b in bufs], {w: w for w in range(n)}, 6 * n, start, finish)


def _simple_exchange(arrays, landing, aliases, make_copies):
    def start(ins, outs, ssem, rsem):
        for cp, _ in make_copies(ins, outs, ssem, rsem, False):
            cp.start()

    def finish(ins, outs, ssem, rsem):
        cps = make_copies(ins, outs, ssem, rsem, True)
        for _, landed in cps:
            landed.wait_recv()
        for cp, _ in cps:
            cp.wait_send()

    return Exchange(arrays, landing, aliases, len(arrays) * 3, start, finish)


def _ex_pair_swap(grads):
    def make(ins, outs, ssem, rsem, landing):
        x, y, c, _ = _place()
        cps = [_remote(ins[w].at[:, 1 - c], outs[w], ssem.at[w], rsem.at[w], (x, y, 1 - c))
               for w in range(len(grads))]
        return [(cp, cp) for cp in cps]

    return _simple_exchange(grads, [_sds((NSH,) + g.shape[2:], g.dtype) for g in grads], {}, make)


def _ex_scatter(parts):
    def make(ins, outs, ssem, rsem, landing):
        x, y, c, chips = _place()
        out = []
        for w in range(len(parts)):
            for k, (px, py) in enumerate(chips):
                sems = (ssem.at[3 * w + k], rsem.at[3 * w + k])
                out.append((_remote(ins[w].at[2 * px + py], outs[w].at[k], *sems, (px, py, c)),
                            _remote(outs[w].at[k], outs[w].at[k], *sems, (px, py, c)) if landing else None))
        return out

    return _simple_exchange(parts, [_sds((3,) + p.shape[1:], p.dtype) for p in parts], {}, make)


def _ex_share(bufs):
    def make(ins, outs, ssem, rsem, landing):
        x, y, c, _ = _place()
        sib = (x, y, 1 - c)
        return [(_remote(outs[w].at[c], outs[w].at[c], ssem.at[w], rsem.at[w], sib),
                 _remote(outs[w].at[1 - c], outs[w].at[1 - c], ssem.at[w], rsem.at[w], sib) if landing else None)
                for w in range(len(bufs))]

    return _simple_exchange(bufs, [_sds(b.shape, b.dtype) for b in bufs], {w: w for w in range(len(bufs))}, make)


def _gather_small(block):
    m_per, n = block.shape

    def body(x_ref, out_ref, ssem, rsem, lsem):
        x, y, c, chips = _place()
        me, sib = (x, y, c), (x, y, 1 - c)

        def rows(px, py, pc):
            return out_ref.at[pl.ds((4 * px + 2 * py + pc) * m_per, m_per), :]

        def copy(k, blk, to, src=None):
            return _remote(rows(*blk) if src is None else src, rows(*blk), ssem.at[k], rsem.at[k], to)

        mine = pltpu.make_async_copy(x_ref, rows(*me), lsem)
        mine.start()
        first = [copy(0, me, sib, src=x_ref)]
        first += [copy(1 + j, me, (*chip, c), src=x_ref) for j, chip in enumerate(chips)]
        for cp in first:
            cp.start()
        passed = [copy(4 + j, (*chip, c), sib) for j, chip in enumerate(chips)]
        for j, chip in enumerate(chips):
            copy(1 + j, (*chip, c), me).wait_recv()
            passed[j].start()
        copy(0, sib, me).wait_recv()
        for j, chip in enumerate(chips):
            copy(4 + j, (*chip, 1 - c), me).wait_recv()
        for cp in first + passed:
            cp.wait_send()
        mine.wait()

    return pl.pallas_call(
        body, name="gather_small", out_shape=jax.ShapeDtypeStruct((8 * m_per, n), block.dtype),
        in_specs=[_VM], out_specs=_VM,
        scratch_shapes=[pltpu.SemaphoreType.DMA((7,)), pltpu.SemaphoreType.DMA((7,)), pltpu.SemaphoreType.DMA],
    )(block)


def _row_block(rows):
    return max(t for t in range(16, 257, 16) if rows % t == 0)


def _pair_sum(grad, got, c_idx, name):
    _, _, half, cols = grad.shape
    tr = _row_block(half)

    def body(c_ref, a_ref, b_ref, o_ref):
        o_ref[...] = (a_ref[...].astype(F32) + b_ref[...].astype(F32)).astype(BF16)

    return pl.pallas_call(
        body, name=name, out_shape=_sds((NSH, half, cols), BF16),
        grid_spec=pltpu.PrefetchScalarGridSpec(
            num_scalar_prefetch=1, grid=(NSH, half // tr),
            in_specs=[pl.BlockSpec((None, None, tr, cols), lambda j, r, c: (j, c[0], r, 0)),
                      pl.BlockSpec((None, tr, cols), lambda j, r, c: (j, r, 0))],
            out_specs=pl.BlockSpec((None, tr, cols), lambda j, r, c: (j, r, 0))),
        compiler_params=_params(("arbitrary", "arbitrary"), 32),
    )(c_idx, *_in_hbm((grad, got)))


def _chip_sum(own, got, place, name):
    _, half, cols = own.shape
    tr = _row_block(half)

    def body(place_ref, own_ref, got_ref, o_ref):
        acc = own_ref[...].astype(F32)
        for k in range(3):
            acc = acc + got_ref[k].astype(F32)
        o_ref[...] = acc

    return pl.pallas_call(
        body, name=name, out_shape=_sds((2, half, cols), F32),
        grid_spec=pltpu.PrefetchScalarGridSpec(
            num_scalar_prefetch=1, grid=(half // tr,),
            in_specs=[pl.BlockSpec((None, tr, cols), lambda r, p: (p[0], r, 0)),
                      pl.BlockSpec((3, tr, cols), lambda r, p: (0, r, 0))],
            out_specs=pl.BlockSpec((None, tr, cols), lambda r, p: (p[1], r, 0))),
        compiler_params=_params(("arbitrary",), 32),
    )(place, *_in_hbm((own, got)))


def _adamw_math(w, g, m, v):
    m = B1 * m + (1.0 - B1) * g
    v = B2 * v + (1.0 - B2) * (g * g)
    m_hat = m / (1.0 - B1 ** STEP)
    v_hat = v / (1.0 - B2 ** STEP)
    return -LR * (m_hat / (jnp.sqrt(v_hat) + AEPS) + WD * w), m, v


def _adamw(w, g, m, v, name, exchange=None):
    rows, cols = w.shape
    tr = _row_block(rows)

    def body(w_ref, g_ref, m_ref, v_ref, go_ref, d_ref, nm_ref, nv_ref):
        g = g_ref[...]
        go_ref[...] = g
        d_ref[...], nm_ref[...], nv_ref[...] = _adamw_math(w_ref[...], g, m_ref[...], v_ref[...])

    blk = pl.BlockSpec((tr, cols), lambda r: (r, 0))
    return _call(
        body, (w, g, m, v), name=name, grid=(rows // tr,), out_shape=[_sds(w.shape, F32)] * 4,
        in_specs=[blk] * 4, out_specs=[blk] * 4,
        compiler_params=_params(("arbitrary",), 32), exchange=exchange)


def _small_update(gathered, w, m, v):
    rows = w.shape[0]

    def body(ga_ref, w_ref, m_ref, v_ref, g_ref, d_ref, nm_ref, nv_ref):
        g = ga_ref[0:rows, :]
        for dev in range(1, 8):
            g = g + ga_ref[dev * rows:(dev + 1) * rows, :]
        g_ref[...] = g
        d_ref[...], nm_ref[...], nv_ref[...] = _adamw_math(w_ref[...], g, m_ref[...], v_ref[...])

    return pl.pallas_call(
        body, name="small_update", out_shape=[jax.ShapeDtypeStruct(w.shape, F32)] * 4,
        in_specs=[_VM] * 4, out_specs=[_VM] * 4,
    )(gathered, w, m, v)


SMALL = ("ffn1_norm", "mix_norm", "ffn2_norm", "final_norm", "pool_scale", "pool_w_group", "loss")
BIG = ("ffn1_w_gate_up", "ffn1_w_down", "w_in", "w_branch_pool", "w_branch_attn", "w_out",
       "ffn2_w_gate_up", "ffn2_w_down")
ORDER = ("ffn1_norm", "ffn1_w_gate_up", "ffn1_w_down", "mix_norm", "w_in", "pool_w_group", "pool_scale",
         "w_branch_pool", "w_branch_attn", "w_out", "ffn2_norm", "ffn2_w_gate_up", "ffn2_w_down", "final_norm")
SMALL_ROWS = 560


def _pack_small(t):
    parts = []
    for k in SMALL:
        rows = t[k].reshape(-1, 128) if k in t else jnp.zeros((1, 128), F32)
        parts.append(jnp.pad(rows, ((0, -rows.shape[0] % 8), (0, 0))))
    packed = jnp.concatenate(parts, axis=0)
    assert packed.shape == (SMALL_ROWS, 128), packed.shape
    return packed


def _unpack_small(packed, like):
    out, at = {}, 0
    for k in SMALL:
        n = like[k].size // 128 if k in like else 1
        out[k] = packed[at:at + n].reshape(like[k].shape) if k in like else packed[at, 0]
        at += n + (-n % 8)
    return out


def _halves(g):
    return g.reshape(NSH, 2, g.shape[1] // 2, g.shape[2])


def kernel(x, ffn1_norm, ffn1_w_gate_up, ffn1_w_down, mix_norm, w_in, pool_w_group, pool_scale, w_branch_pool, w_branch_attn, w_out, ffn2_norm, ffn2_w_gate_up, ffn2_w_down, final_norm, loss_target, m_ffn1_norm, m_ffn1_w_gate_up, m_ffn1_w_down, m_mix_norm, m_w_in, m_pool_w_group, m_pool_scale, m_w_branch_pool, m_w_branch_attn, m_w_out, m_ffn2_norm, m_ffn2_w_gate_up, m_ffn2_w_down, m_final_norm, v_ffn1_norm, v_ffn1_w_gate_up, v_ffn1_w_down, v_mix_norm, v_w_in, v_pool_w_group, v_pool_scale, v_w_branch_pool, v_w_branch_attn, v_w_out, v_ffn2_norm, v_ffn2_w_gate_up, v_ffn2_w_down, v_final_norm):
    wts = dict(ffn1_norm=ffn1_norm, ffn1_w_gate_up=ffn1_w_gate_up, ffn1_w_down=ffn1_w_down, mix_norm=mix_norm,
               w_in=w_in, pool_w_group=pool_w_group, pool_scale=pool_scale, w_branch_pool=w_branch_pool,
               w_branch_attn=w_branch_attn, w_out=w_out, ffn2_norm=ffn2_norm, ffn2_w_gate_up=ffn2_w_gate_up,
               ffn2_w_down=ffn2_w_down, final_norm=final_norm)
    mom = dict(ffn1_norm=m_ffn1_norm, ffn1_w_gate_up=m_ffn1_w_gate_up, ffn1_w_down=m_ffn1_w_down,
               mix_norm=m_mix_norm, w_in=m_w_in, pool_w_group=m_pool_w_group, pool_scale=m_pool_scale,
               w_branch_pool=m_w_branch_pool, w_branch_attn=m_w_branch_attn, w_out=m_w_out,
               ffn2_norm=m_ffn2_norm, ffn2_w_gate_up=m_ffn2_w_gate_up, ffn2_w_down=m_ffn2_w_down,
               final_norm=m_final_norm)
    var = dict(ffn1_norm=v_ffn1_norm, ffn1_w_gate_up=v_ffn1_w_gate_up, ffn1_w_down=v_ffn1_w_down,
               mix_norm=v_mix_norm, w_in=v_w_in, pool_w_group=v_pool_w_group, pool_scale=v_pool_scale,
               w_branch_pool=v_w_branch_pool, w_branch_attn=v_w_branch_attn, w_out=v_w_out,
               ffn2_norm=v_ffn2_norm, ffn2_w_gate_up=v_ffn2_w_gate_up, ffn2_w_down=v_ffn2_w_down,
               final_norm=v_final_norm)

    c_idx = lax.axis_index("c").astype(jnp.int32).reshape(1)
    me_idx = (2 * lax.axis_index("x") + lax.axis_index("y")).astype(jnp.int32).reshape(1)
    place = jnp.concatenate([me_idx, c_idx])
    x0, tgt = x[0], loss_target[0]
    wgrp = pool_w_group[0].astype(BF16)
    g1, gm, g2, gf = ffn1_norm, mix_norm, ffn2_norm, final_norm.reshape(1, D)
    grad, delta, new_m, new_v = {}, {}, {}, {}

    def pair_sums(keys, parts, got):
        return [_pair_sum(parts[i], got[i], c_idx, "pair_sum_" + k) for i, k in enumerate(keys)]

    def chip_sums(keys, chip_parts, owned):
        return [_chip_sum(chip_parts[i], owned[i], place, "chip_sum_" + k) for i, k in enumerate(keys)]

    def adamw(k, exchange=None):
        res = _adamw(wts[k][0], grad[k][0], mom[k][0], var[k][0], "adamw_" + k, exchange=exchange)
        outs, landed = (res, None) if exchange is None else res
        grad[k], delta[k], new_m[k], new_v[k] = (o.reshape(wts[k].shape) for o in outs)
        return landed

    own = {k: _cast_into_block(wts[k][0], me_idx, "cast_" + k) for k in BIG}
    first, late = ("ffn1_w_gate_up", "ffn1_w_down"), ("w_branch_pool", "w_branch_attn", "w_out",
                                                       "ffn2_w_gate_up", "ffn2_w_down")
    full = dict(zip(first, _exchange_alone(_ex_gather([own[k] for k in first]), "gather_ffn1")))
    wgu1, wd1 = full["ffn1_w_gate_up"], full["ffn1_w_down"].reshape(DFF, D)
    (h1, n1, gu1), (win,) = _ffn_fwd(x0, g1, wgu1, wd1, "ffn1_fwd", exchange=_ex_gather([own["w_in"]]))
    u, xp, q, k, v, gp, gs = _mix_in(h1, gm, win)
    (o_sb, ctot), landed = _attn_fwd(q, k, v, exchange=_ex_gather([own[k_] for k_ in late]))
    full.update(zip(late, landed))
    wbp, wba, wout = full["w_branch_pool"], full["w_branch_attn"], full["w_out"].reshape(D, D)
    wgu2, wd2 = full["ffn2_w_gate_up"], full["ffn2_w_down"].reshape(DFF, D)
    h2, pm, p, yp, ys, mm = _mix_out(h1, xp, o_sb, gp, gs, wgrp, pool_scale, wbp, wba, wout)
    h3, n3, gu3 = _ffn_fwd(h2, g2, wgu2, wd2, "ffn2_fwd")
    dh3, loss_row, d_gf = _head(h3, tgt, gf)

    dh2, dgu3, a3, d_g2 = _ffn_bwd(dh3, h2, g2, gu3, wgu2, wd2, "ffn2_bwd")
    ka = ("ffn2_w_gate_up", "ffn2_w_down")
    pa = [_halves(_wgrad(n3, dgu3, NSH, 512, "wgrad_gu2")),
          _halves(_wgrad(a3, dh3, 1, FFS, "wgrad_d2").reshape(NSH, DFF // NSH, D))]
    (dlg, dyp, dys, do_sb, dyg, dxp, d_scale), got_a = _mix_bwd_out(
        dh2, gp, gs, yp, ys, pm, wgrp, pool_scale, wbp, wba, wout, exchange=_ex_pair_swap(pa))
    chip_a = pair_sums(ka, pa, got_a)
    kb = ("w_out", "w_branch_pool", "w_branch_attn")
    pb = [_halves(_wgrad(mm, dh2, 1, 512, "wgrad_out").reshape(NSH, D // NSH, D)),
          _halves(_wgrad(p, dyp, NSH, PW, "wgrad_bp")), _halves(_wgrad(o_sb, dys, NSH, SBW, "wgrad_ba"))]
    chip_b = pair_sums(kb, pb, _exchange_alone(_ex_pair_swap(pb), "pair_swap_mix"))
    (dq, dk, dv), owned_ab = _attn_bwd(q, k, v, do_sb, ctot, exchange=_ex_scatter(chip_a + chip_b))
    halves_ab = chip_sums(ka + kb, chip_a + chip_b, owned_ab)
    dproj = jnp.concatenate([dxp, dq, dk, dv, dlg], axis=1)
    (dh1, d_gm), both_ab = _mix_bwd_in(dh2, h1, gm, dproj, win, exchange=_ex_share(halves_ab))
    for i, k_ in enumerate(ka + kb):
        grad[k_] = both_ab[i].reshape(wts[k_].shape)
    pc = [_halves(_wgrad(u, dproj, NSH, 512, "wgrad_in"))]
    (dx, dgu1, a1, d_g1), got_c = _ffn_bwd(dh1, x0, g1, gu1, wgu1, wd1, "ffn1_bwd", exchange=_ex_pair_swap(pc))
    chip_c = pair_sums(("w_in",), pc, got_c)
    pd, owned_c = _wgrad(n1, dgu1, NSH, 512, "wgrad_gu1", exchange=_ex_scatter(chip_c))
    pd = [_halves(pd)]
    pe, got_d = _wgrad(a1, dh1, 1, FFS, "wgrad_d1", exchange=_ex_pair_swap(pd))
    pe = [_halves(pe.reshape(NSH, DFF // NSH, D))]
    chip_d = pair_sums(("ffn1_w_gate_up",), pd, got_d)
    got_e = adamw("ffn2_w_gate_up", exchange=_ex_pair_swap(pe))
    chip_e = pair_sums(("ffn1_w_down",), pe, got_e)
    halves_c = chip_sums(("w_in",), chip_c, owned_c)
    owned_de = adamw("ffn2_w_down", exchange=_ex_scatter(chip_d + chip_e))
    kde = ("ffn1_w_gate_up", "ffn1_w_down")
    halves_de = chip_sums(kde, chip_d + chip_e, owned_de)
    both_cde = adamw("w_out", exchange=_ex_share(halves_c + halves_de))
    for i, k_ in enumerate(("w_in",) + kde):
        grad[k_] = both_cde[i].reshape(wts[k_].shape)
    for k_ in ("w_branch_pool", "w_branch_attn", "w_in") + kde:
        adamw(k_)

    small_g = dict(ffn1_norm=d_g1, mix_norm=d_gm, ffn2_norm=d_g2, final_norm=d_gf, pool_scale=d_scale,
                   pool_w_group=_wgrad_groups(pm, dyg), loss=loss_row)
    gathered = _gather_small(_pack_small(small_g))
    sg, sd, sm, sv = _small_update(gathered, _pack_small(wts), _pack_small(mom), _pack_small(var))
    sums = _unpack_small(sg, wts)
    loss = sums.pop("loss")
    grad.update(sums)
    for dst, packed in ((delta, sd), (new_m, sm), (new_v, sv)):
        vals = _unpack_small(packed, wts)
        vals.pop("loss")
        dst.update(vals)
    return (loss, dx[None], *[grad[k_] for k_ in ORDER], *[delta[k_] for k_ in ORDER],
            *[new_m[k_] for k_ in ORDER], *[new_v[k_] for k_ in ORDER])
```

```python
import functools

import jax
import jax.numpy as jnp
from jax import lax
from jax.experimental import pallas as pl
from jax.experimental.pallas import tpu as pltpu

F32 = jnp.float32
BF16 = jnp.bfloat16

S = 2048
D = 1024
DFF = 2816
FFS = 2 * DFF // 4
NSH = 4
PW = 512
PG = 128
POOL_WINDOWS = (2, 4, 8, 16)
HALO = 16
SBW = 512
DH = 64
EPS = 1e-6
SCALE = 0.125
LOG2E = 1.4426950408889634
TA = 256
QB = 2
MIB = 1024 * 1024

LR, B1, B2, AEPS, WD, STEP = 0.001, 0.9, 0.999, 1e-08, 0.01, 10

_VM = pl.BlockSpec(memory_space=pltpu.VMEM)
_ANY = pl.BlockSpec(memory_space=pl.ANY)
MESH = pl.DeviceIdType.MESH


def _nn(a, b):
    return jnp.dot(a, b, preferred_element_type=F32)


def _nt(a, b):
    return lax.dot_general(a, b, (((1,), (1,)), ((), ())), preferred_element_type=F32)


def _tn(a, b):
    return lax.dot_general(a, b, (((0,), (0,)), ((), ())), preferred_element_type=F32)


def _params(sem, vmem_mib):
    return pltpu.CompilerParams(dimension_semantics=sem, vmem_limit_bytes=vmem_mib * MIB)


def _rows(tm, width):
    return pl.BlockSpec((tm, width), lambda i: (i, 0))


def _fixed(shape):
    return pl.BlockSpec(shape, lambda *_: (0,) * len(shape))


def _sds(shape, dtype):
    return pltpu.HBM(shape, dtype)


def _in_hbm(args):
    return [pltpu.with_memory_space_constraint(a, pltpu.HBM) for a in args]


def _stage(pairs):
    @pl.when(pl.program_id(0) == 0)
    def _():
        for src, dst in pairs:
            pltpu.sync_copy(src, dst)


def _vmem_like(*arrays):
    return [pltpu.VMEM(a.shape, a.dtype) for a in arrays]


class Exchange:
    def __init__(self, arrays, landing, aliases, n_sems, start, finish):
        self.arrays, self.landing, self.aliases, self.n_sems = list(arrays), list(landing), dict(aliases), n_sems
        self.start, self.finish = start, finish


def _call(body, args, *, name, grid, in_specs, out_specs, out_shape, scratch_shapes=(), compiler_params=None,
          exchange=None):
    if exchange is None:
        return pl.pallas_call(body, name=name, grid=grid, in_specs=in_specs, out_specs=out_specs,
                              out_shape=out_shape, scratch_shapes=list(scratch_shapes),
                              compiler_params=compiler_params)(*args)
    ex = exchange
    n_in, n_out, n_scr = len(in_specs), len(out_specs), len(scratch_shapes)
    na, nl = len(ex.arrays), len(ex.landing)

    def hosted(*refs):
        at = [0]

        def take(n):
            at[0] += n
            return refs[at[0] - n:at[0]]

        k_in, e_in, k_out, e_out, k_scr = take(n_in), take(na), take(n_out), take(nl), take(n_scr)
        ssem, rsem = take(2)
        ids = [pl.program_id(a) for a in range(len(grid))]
        first = functools.reduce(jnp.logical_and, [i == 0 for i in ids])
        last = functools.reduce(jnp.logical_and, [i == g - 1 for i, g in zip(ids, grid)])

        @pl.when(first)
        def _():
            ex.start(e_in, e_out, ssem, rsem)

        body(*k_in, *k_out, *k_scr)

        @pl.when(last)
        def _():
            ex.finish(e_in, e_out, ssem, rsem)

    outs = pl.pallas_call(
        hosted, name=name, grid=grid,
        in_specs=list(in_specs) + [_ANY] * na, out_specs=list(out_specs) + [_ANY] * nl,
        out_shape=list(out_shape) + ex.landing,
        scratch_shapes=list(scratch_shapes) + [pltpu.SemaphoreType.DMA((ex.n_sems,))] * 2,
        input_output_aliases={n_in + i: n_out + j for i, j in ex.aliases.items()},
        compiler_params=compiler_params,
    )(*args, *_in_hbm(ex.arrays))
    return outs[:n_out], outs[n_out:]


def _exchange_alone(ex, name):
    def body(*refs):
        na, nl = len(ex.arrays), len(ex.landing)
        ex.start(refs[:na], refs[na:na + nl], refs[-2], refs[-1])
        ex.finish(refs[:na], refs[na:na + nl], refs[-2], refs[-1])

    return pl.pallas_call(
        body, name=name, in_specs=[_ANY] * len(ex.arrays), out_specs=[_ANY] * len(ex.landing),
        out_shape=ex.landing, scratch_shapes=[pltpu.SemaphoreType.DMA((ex.n_sems,))] * 2,
        input_output_aliases=ex.aliases,
    )(*_in_hbm(ex.arrays))


def _rms(x):
    r = lax.rsqrt(jnp.mean(x * x, axis=-1, keepdims=True) + EPS)
    return r, x * r


def _rms_bwd(dn, xr, r, gain):
    dng = dn * gain
    dx = r * (dng - xr * jnp.mean(dng * xr, axis=-1, keepdims=True))
    return dx, jnp.sum(dn * xr, axis=0, keepdims=True)


def _ffn_fwd(x, gain, wgu, wd, name, exchange=None):
    tm = 256

    def body(x_ref, g_ref, wgu_hbm, wd_hbm, h_ref, n_ref, gu_ref, wgu_ref, wd_ref):
        _stage([(wgu_hbm, wgu_ref), (wd_hbm, wd_ref)])
        x = x_ref[...]
        _, xr = _rms(x)
        n = (xr * g_ref[...]).astype(BF16)
        n_ref[...] = n
        acc = jnp.zeros((tm, D), F32)
        for j in range(2):
            g = _nn(n, wgu_ref[j])
            u = _nn(n, wgu_ref[2 + j])
            gu_ref[:, j * FFS:(j + 1) * FFS] = g.astype(BF16)
            gu_ref[:, (2 + j) * FFS:(3 + j) * FFS] = u.astype(BF16)
            a = (g * jax.nn.sigmoid(g) * u).astype(BF16)
            acc = acc + _nn(a, wd_ref[j * FFS:(j + 1) * FFS, :])
        h_ref[...] = x + 0.5 * acc

    return _call(
        body, (x, gain, wgu, wd), name=name, grid=(S // tm,),
        in_specs=[_rows(tm, D), _fixed((1, D)), _ANY, _ANY],
        out_specs=[_rows(tm, D), _rows(tm, D), _rows(tm, 4 * FFS)],
        out_shape=[_sds((S, D), F32), _sds((S, D), BF16), _sds((S, 4 * FFS), BF16)],
        scratch_shapes=_vmem_like(wgu, wd),
        compiler_params=_params(("arbitrary",), 56), exchange=exchange)


def _ffn_bwd(dh, x, gain, gu, wgu, wd, name, exchange=None):
    tm = 256

    def body(dh_ref, x_ref, g_ref, gu_ref, wgu_hbm, wd_hbm, dx_ref, dgu_ref, a_ref, dg_ref, wgu_ref, wd_ref):
        _stage([(wgu_hbm, wgu_ref), (wd_hbm, wd_ref)])
        dh = dh_ref[...]
        dhb = dh.astype(BF16)
        dn = jnp.zeros((tm, D), F32)
        for j in range(2):
            g = gu_ref[:, j * FFS:(j + 1) * FFS].astype(F32)
            u = gu_ref[:, (2 + j) * FFS:(3 + j) * FFS].astype(F32)
            da = 0.5 * _nt(dhb, wd_ref[j * FFS:(j + 1) * FFS, :])
            sg = jax.nn.sigmoid(g)
            si = g * sg
            a_ref[:, j * FFS:(j + 1) * FFS] = (0.5 * si * u).astype(BF16)
            dgb = (da * u * (sg * (1.0 + g * (1.0 - sg)))).astype(BF16)
            dub = (da * si).astype(BF16)
            dgu_ref[:, j * FFS:(j + 1) * FFS] = dgb
            dgu_ref[:, (2 + j) * FFS:(3 + j) * FFS] = dub
            dn = dn + _nt(dgb, wgu_ref[j]) + _nt(dub, wgu_ref[2 + j])
        r, xr = _rms(x_ref[...])
        dx, dgain = _rms_bwd(dn, xr, r, g_ref[...])
        dx_ref[...] = dh + dx

        @pl.when(pl.program_id(0) == 0)
        def _():
            dg_ref[...] = jnp.zeros_like(dg_ref)

        dg_ref[...] += dgain

    return _call(
        body, (dh, x, gain, gu, wgu, wd), name=name, grid=(S // tm,),
        in_specs=[_rows(tm, D), _rows(tm, D), _fixed((1, D)), _rows(tm, 4 * FFS), _ANY, _ANY],
        out_specs=[_rows(tm, D), _rows(tm, 4 * FFS), _rows(tm, DFF), _fixed((1, D))],
        out_shape=[_sds((S, D), F32), _sds((S, 4 * FFS), BF16), _sds((S, DFF), BF16), _sds((1, D), F32)],
        scratch_shapes=_vmem_like(wgu, wd),
        compiler_params=_params(("arbitrary",), 56), exchange=exchange)


def _head(h, target, gain):
    tm = 512

    def body(h_ref, t_ref, g_ref, dh_ref, loss_ref, dg_ref):
        gain = g_ref[...]
        r, hr = _rms(h_ref[...])
        err = hr * gain - t_ref[...]
        dy = err * (1.0 / D)
        dh, dgain = _rms_bwd(dy, hr, r, gain)
        dh_ref[...] = dh

        @pl.when(pl.program_id(0) == 0)
        def _():
            dg_ref[...] = jnp.zeros_like(dg_ref)
            loss_ref[...] = jnp.zeros_like(loss_ref)

        dg_ref[...] += dgain
        loss_ref[...] += jnp.full((1, 128), (0.5 / D) * jnp.sum(err * err), F32)

    return pl.pallas_call(
        body, name="head", grid=(S // tm,),
        in_specs=[_rows(tm, D), _rows(tm, D), _fixed((1, D))],
        out_specs=[_rows(tm, D), _fixed((1, 128)), _fixed((1, D))],
        out_shape=[_sds((S, D), F32), _sds((1, 128), F32), _sds((1, D), F32)],
        compiler_params=_params(("arbitrary",), 40),
    )(h, target, gain)


def _mix_in(h, gain, w_in):
    tm = 512

    def body(h_ref, g_ref, w_hbm, u_ref, xp_ref, q_ref, k_ref, v_ref, gp_ref, gs_ref, w_ref):
        _stage([(w_hbm, w_ref)])
        _, hr = _rms(h_ref[...])
        u = (hr * g_ref[...]).astype(BF16)
        u_ref[...] = u
        p0 = _nn(u, w_ref[0])
        xp_ref[...] = p0[:, :PW]
        q_ref[...] = p0[:, PW:].astype(BF16)
        p1 = _nn(u, w_ref[1])
        k_ref[...] = p1[:, :SBW].astype(BF16)
        v_ref[...] = p1[:, SBW:].astype(BF16)
        gp_ref[...] = jax.nn.sigmoid(_nn(u, w_ref[2])).astype(BF16)
        gs_ref[...] = jax.nn.sigmoid(_nn(u, w_ref[3])).astype(BF16)

    return pl.pallas_call(
        body, name="mix_in", grid=(S // tm,),
        in_specs=[_rows(tm, D), _fixed((1, D)), _ANY],
        out_specs=[_rows(tm, D), _rows(tm, PW), _rows(tm, SBW), _rows(tm, SBW), _rows(tm, SBW),
                   _rows(tm, D), _rows(tm, D)],
        out_shape=[_sds((S, D), BF16), _sds((S, PW), F32), _sds((S, SBW), BF16), _sds((S, SBW), BF16),
                   _sds((S, SBW), BF16), _sds((S, D), BF16), _sds((S, D), BF16)],
        scratch_shapes=_vmem_like(w_in),
        compiler_params=_params(("arbitrary",), 48),
    )(h, gain, w_in)


def _hilo_dot(x, tri):
    hi = x.astype(BF16)
    lo = (x - hi.astype(F32)).astype(BF16)
    return _nn(hi, tri) + _nn(lo, tri)


def _log_terms(qk):
    z2 = qk * (SCALE * LOG2E)
    lb = jnp.minimum(z2, 0.0) - jnp.log2(1.0 + jnp.exp2(-jnp.abs(z2)))
    return lb, lb - z2


def _head_masks():
    lane = lax.broadcasted_iota(jnp.int32, (1, 2 * DH), 1)
    return (lane < DH, lane >= DH)


def _attn_fwd(q, k, v, exchange=None):
    T = TA

    def body(q_ref, k_ref, v_ref, o_ref, c_ref):
        i2 = 2 * pl.program_id(1)
        row = lax.broadcasted_iota(jnp.int32, (T, T), 0)
        col = lax.broadcasted_iota(jnp.int32, (T, T), 1)
        after = (row > col).astype(BF16)
        causal = col < row
        masks = _head_masks()
        qms = {}
        for b in range(QB):
            q2 = q_ref[b * T:(b + 1) * T, :]
            for h, hm in enumerate(masks):
                qms[b, h] = jnp.where(hm, q2, jnp.zeros_like(q2))

        def blocks(keys, pairs, carries, os):
            ks, vms = [], []
            for j in keys:
                rows = pl.ds(pl.multiple_of(j * T, T), T)
                vj = v_ref[rows, :]
                ks.append(k_ref[rows, :])
                vms.append([jnp.where(hm, vj, jnp.zeros_like(vj)) for hm in masks])
            units = [(n, h) for n in range(len(pairs)) for h in range(2)]
            qks = {(n, h): _nt(qms[pairs[n][0], h], ks[pairs[n][1]]) for n, h in units}
            lbs, l1ms = {}, {}
            for u in units:
                lbs[u], l1m = _log_terms(qks[u])
                l1ms[u] = jnp.where(causal, l1m, 0.0) if pairs[u[0]][2] else l1m
            cins = {u: _hilo_dot(l1ms[u], after) for u in units}
            carries, os = dict(carries), list(os)
            for n, h in units:
                b, key, diag = pairs[n]
                a = jnp.exp2(lbs[n, h] + cins[n, h] + carries[b, h])
                if diag:
                    a = jnp.where(causal, a, 0.0)
                os[b] = os[b] + _nn(a.astype(BF16), vms[key][h])
                carries[b, h] = carries[b, h] + jnp.sum(l1ms[n, h], axis=1, keepdims=True)
            return carries, tuple(os)

        carries = {(b, h): jnp.zeros((T, 1), F32) for b in range(QB) for h in range(2)}
        os = tuple(jnp.zeros((T, 2 * DH), F32) for _ in range(QB))
        carries, os = blocks([i2 + 1, i2], [(1, 0, True), (0, 1, True), (1, 1, False)], carries, os)
        carries, os = lax.fori_loop(
            0, i2, lambda jj, c: blocks([i2 - 1 - jj], [(0, 0, False), (1, 0, False)], c[0], c[1]), (carries, os))
        for b in range(QB):
            o_ref[b * T:(b + 1) * T, :] = os[b].astype(BF16)
            c_ref[b * T:(b + 1) * T, :] = jnp.where(masks[0], carries[b, 0], carries[b, 1])

    blk = pl.BlockSpec((QB * T, 2 * DH), lambda p, i: (i, p))
    full = pl.BlockSpec((S, 2 * DH), lambda p, i: (0, p))
    return _call(
        body, (q, k, v), name="attn_fwd", grid=(SBW // (2 * DH), S // (QB * T)),
        in_specs=[blk, full, full], out_specs=[blk, blk],
        out_shape=[_sds((S, SBW), BF16), _sds((S, SBW), F32)],
        compiler_params=_params(("arbitrary", "arbitrary"), 40), exchange=exchange)


def _attn_bwd(q, k, v, do, ctot, exchange=None):
    T = TA
    nq = S // (QB * T)

    def body(q_ref, k_ref, v_ref, do_ref, c_ref, dq_ref, dk_ref, dv_ref, dk_acc, dv_acc):
        step = pl.program_id(1)
        i2 = 2 * step

        @pl.when(step == 0)
        def _():
            dk_acc[...] = jnp.zeros_like(dk_acc)
            dv_acc[...] = jnp.zeros_like(dv_acc)

        row = lax.broadcasted_iota(jnp.int32, (T, T), 0)
        col = lax.broadcasted_iota(jnp.int32, (T, T), 1)
        upto = (row <= col).astype(BF16)
        before = (row < col).astype(BF16)
        causal = col < row
        masks = _head_masks()
        qms, doms, ctots = {}, {}, {}
        for b in range(QB):
            q2, do2 = q_ref[b * T:(b + 1) * T, :], do_ref[b * T:(b + 1) * T, :]
            for h, hm in enumerate(masks):
                qms[b, h] = jnp.where(hm, q2, jnp.zeros_like(q2))
                doms[b, h] = jnp.where(hm, do2, jnp.zeros_like(do2))
                ctots[b, h] = c_ref[b * T:(b + 1) * T, h * DH:h * DH + 1]

        def blocks(keys, pairs, sums, dqs):
            rows = [pl.ds(pl.multiple_of(j * T, T), T) for j in keys]
            ks, vs = [k_ref[r, :] for r in rows], [v_ref[r, :] for r in rows]
            kms = [[jnp.where(hm, kj, jnp.zeros_like(kj)) for hm in masks] for kj in ks]
            units = [(n, h) for n in range(len(pairs)) for h in range(2)]
            qks = {(n, h): _nt(qms[pairs[n][0], h], ks[pairs[n][1]]) for n, h in units}
            das = {(n, h): _nt(doms[pairs[n][0], h], vs[pairs[n][1]]) for n, h in units}
            lbs, l1ms = {}, {}
            for u in units:
                lbs[u], l1m = _log_terms(qks[u])
                l1ms[u] = jnp.where(causal, l1m, 0.0) if pairs[u[0]][2] else l1m
            pins = {u: _hilo_dot(l1ms[u], upto) for u in units}
            sums = dict(sums)
            a_s, dls, cps = {}, {}, {}
            for n, h in units:
                b, _, diag = pairs[n]
                cl, cp = sums[b, h]
                a = jnp.exp2(lbs[n, h] + (ctots[b, h] - cl) - pins[n, h])
                if diag:
                    a = jnp.where(causal, a, 0.0)
                a_s[n, h] = a.astype(BF16)
                dls[n, h] = das[n, h] * a
                cps[n, h] = cp
                sums[b, h] = (cl + jnp.sum(l1ms[n, h], axis=1, keepdims=True),
                              cp + jnp.sum(dls[n, h], axis=1, keepdims=True))
            pexs = {u: _hilo_dot(dls[u], before) for u in units}
            dzbs = {}
            for u in units:
                dz = dls[u] - jnp.exp2(lbs[u]) * (dls[u] + pexs[u] + cps[u])
                if pairs[u[0]][2]:
                    dz = jnp.where(causal, dz, 0.0)
                dzbs[u] = dz.astype(BF16)
            dqs = list(dqs)
            for n, h in units:
                dqs[pairs[n][0]] = dqs[pairs[n][0]] + _nn(dzbs[n, h], kms[pairs[n][1]][h])
            for key, r in enumerate(rows):
                mine = [(n, h) for n, h in units if pairs[n][1] == key]
                dk_acc[r, :] += functools.reduce(jnp.add, [_tn(dzbs[u], qms[pairs[u[0]][0], u[1]]) for u in mine])
                dv_acc[r, :] += functools.reduce(jnp.add, [_tn(a_s[u], doms[pairs[u[0]][0], u[1]]) for u in mine])
            return sums, tuple(dqs)

        zero = jnp.zeros((T, 1), F32)
        sums = {(b, h): (zero, zero) for b in range(QB) for h in range(2)}
        dqs = tuple(jnp.zeros((T, 2 * DH), F32) for _ in range(QB))
        sums, dqs = lax.fori_loop(
            0, i2, lambda j, c: blocks([j], [(0, 0, False), (1, 0, False)], c[0], c[1]), (sums, dqs))
        _, dqs = blocks([i2, i2 + 1], [(0, 0, True), (1, 0, False), (1, 1, True)], sums, dqs)
        for b in range(QB):
            dq_ref[b * T:(b + 1) * T, :] = (dqs[b] * SCALE).astype(BF16)

        @pl.when(step == nq - 1)
        def _():
            dk_ref[...] = (dk_acc[...] * SCALE).astype(BF16)
            dv_ref[...] = dv_acc[...].astype(BF16)

    blk = pl.BlockSpec((QB * T, 2 * DH), lambda p, i: (i, p))
    full = pl.BlockSpec((S, 2 * DH), lambda p, i: (0, p))
    return _call(
        body, (q, k, v, do, ctot), name="attn_bwd", grid=(SBW // (2 * DH), nq),
        in_specs=[blk, full, full, blk, blk], out_specs=[blk, full, full],
        out_shape=[_sds((S, SBW), BF16), _sds((S, SBW), BF16), _sds((S, SBW), BF16)],
        scratch_shapes=[pltpu.VMEM((S, 2 * DH), F32), pltpu.VMEM((S, 2 * DH), F32)],
        compiler_params=_params(("arbitrary", "arbitrary"), 40), exchange=exchange)


def _pool_counts(first_row, tm):
    pos = first_row + lax.broadcasted_iota(jnp.int32, (tm, 1), 0)
    return [jnp.minimum(pos + 1, w).astype(F32) for w in POOL_WINDOWS]


def _mix_out(h, xp, o_sb, gp, gs, w_group, scale, w_bp, w_ba, w_out):
    tm = 512

    def body(h_ref, xp_ref, o_ref, gp_ref, gs_ref, wg_hbm, sc_ref, wbp_hbm, wba_hbm, wo_hbm,
             h2_ref, pm_ref, p_ref, yp_ref, ys_ref, m_ref, halo, wg_ref, wbp_ref, wba_ref, wo_ref):
        _stage([(wg_hbm, wg_ref), (wbp_hbm, wbp_ref), (wba_hbm, wba_ref), (wo_hbm, wo_ref)])
        i = pl.program_id(0)

        @pl.when(i == 0)
        def _():
            halo[...] = jnp.zeros_like(halo)

        xp = xp_ref[...]
        ext = jnp.concatenate([halo[...], xp], axis=0)
        halo[...] = xp[tm - HALO:, :]
        counts = _pool_counts(i * tm, tm)
        for gi in range(len(POOL_WINDOWS)):
            lanes = slice(gi * PG, (gi + 1) * PG)
            win = ext[:, lanes]
            for step in range(gi + 1):
                win = win + pltpu.roll(win, 1 << step, 0)
            pm = (win[HALO:, :] / counts[gi] - xp[:, lanes]).astype(BF16)
            pm_ref[:, lanes] = pm
            p_ref[:, lanes] = (_nn(pm, wg_ref[gi]) * sc_ref[:, lanes]).astype(BF16)
        pb = p_ref[...]
        ob = o_ref[...]
        for j in range(NSH):
            cols = slice(j * (D // NSH), (j + 1) * (D // NSH))
            yp = _nn(pb, wbp_ref[j])
            ys = _nn(ob, wba_ref[j])
            yp_ref[:, cols] = yp.astype(BF16)
            ys_ref[:, cols] = ys.astype(BF16)
            m_ref[:, cols] = (gp_ref[:, cols].astype(F32) * yp + gs_ref[:, cols].astype(F32) * ys).astype(BF16)
        h2_ref[...] = h_ref[...] + _nn(m_ref[...], wo_ref[...])

    return pl.pallas_call(
        body, name="mix_out", grid=(S // tm,),
        in_specs=[_rows(tm, D), _rows(tm, PW), _rows(tm, SBW), _rows(tm, D), _rows(tm, D),
                  _ANY, _fixed((1, PW)), _ANY, _ANY, _ANY],
        out_specs=[_rows(tm, D), _rows(tm, PW), _rows(tm, PW), _rows(tm, D), _rows(tm, D), _rows(tm, D)],
        out_shape=[_sds((S, D), F32), _sds((S, PW), BF16), _sds((S, PW), BF16), _sds((S, D), BF16),
                   _sds((S, D), BF16), _sds((S, D), BF16)],
        scratch_shapes=[pltpu.VMEM((HALO, PW), F32)] + _vmem_like(w_group, w_bp, w_ba, w_out),
        compiler_params=_params(("arbitrary",), 48),
    )(h, xp, o_sb, gp, gs, w_group, scale, w_bp, w_ba, w_out)


def _mix_bwd_out(dh, gp, gs, yp, ys, pm, w_group, scale, w_bp, w_ba, w_out, exchange=None):
    tm = 512
    nt = S // tm

    def body(dh_ref, gp_ref, gs_ref, yp_ref, ys_ref, pm_ref, wg_hbm, sc_ref, wbp_hbm, wba_hbm, wo_hbm,
             dlg_ref, dyp_ref, dys_ref, do_ref, dyg_ref, dxp_ref, dsc_ref, halo, wg_ref, wbp_ref, wba_ref, wo_ref):
        _stage([(wg_hbm, wg_ref), (wbp_hbm, wbp_ref), (wba_hbm, wba_ref), (wo_hbm, wo_ref)])
        step = pl.program_id(0)

        @pl.when(step == 0)
        def _():
            halo[...] = jnp.zeros_like(halo)
            dsc_ref[...] = jnp.zeros_like(dsc_ref)

        dm = _nt(dh_ref[...].astype(BF16), wo_ref[...])
        gp = gp_ref[...].astype(F32)
        gs = gs_ref[...].astype(F32)
        yp = yp_ref[...].astype(F32)
        ys = ys_ref[...].astype(F32)
        dlg_ref[:, :D] = (dm * yp * gp * (1.0 - gp)).astype(BF16)
        dlg_ref[:, D:] = (dm * ys * gs * (1.0 - gs)).astype(BF16)
        dyp_ref[...] = (dm * gp).astype(BF16)
        dys_ref[...] = (dm * gs).astype(BF16)
        dp = jnp.zeros((tm, PW), F32)
        do = jnp.zeros((tm, SBW), F32)
        for j in range(NSH):
            cols = slice(j * (D // NSH), (j + 1) * (D // NSH))
            dp = dp + _nt(dyp_ref[:, cols], wbp_ref[j])
            do = do + _nt(dys_ref[:, cols], wba_ref[j])
        do_ref[...] = do.astype(BF16)
        counts = _pool_counts((nt - 1 - step) * tm, tm)
        dscale = []
        for gi in range(len(POOL_WINDOWS)):
            lanes = slice(gi * PG, (gi + 1) * PG)
            dpg = dp[:, lanes]
            dscale.append(jnp.sum(dpg * _nn(pm_ref[:, lanes], wg_ref[gi]), axis=0, keepdims=True))
            dyg = (dpg * sc_ref[:, lanes]).astype(BF16)
            dyg_ref[:, lanes] = dyg
            dpm = _nt(dyg, wg_ref[gi])
            per = dpm / counts[gi]
            win = jnp.concatenate([per, halo[:, lanes]], axis=0)
            halo[:, lanes] = per[:HALO, :]
            for s in range(gi + 1):
                win = win + pltpu.roll(win, tm + HALO - (1 << s), 0)
            dxp_ref[:, lanes] = (win[:tm, :] - dpm).astype(BF16)
        dsc_ref[...] += jnp.concatenate(dscale, axis=1)

    rev = lambda width: pl.BlockSpec((tm, width), lambda i: (nt - 1 - i, 0))
    return _call(
        body, (dh, gp, gs, yp, ys, pm, w_group, scale, w_bp, w_ba, w_out), name="mix_bwd_out", grid=(nt,),
        in_specs=[rev(D), rev(D), rev(D), rev(D), rev(D), rev(PW), _ANY, _fixed((1, PW)), _ANY, _ANY, _ANY],
        out_specs=[rev(2 * D), rev(D), rev(D), rev(SBW), rev(PW), rev(PW), _fixed((1, PW))],
        out_shape=[_sds((S, 2 * D), BF16), _sds((S, D), BF16), _sds((S, D), BF16), _sds((S, SBW), BF16),
                   _sds((S, PW), BF16), _sds((S, PW), BF16), _sds((1, PW), F32)],
        scratch_shapes=[pltpu.VMEM((HALO, PW), F32)] + _vmem_like(w_group, w_bp, w_ba, w_out),
        compiler_params=_params(("arbitrary",), 48), exchange=exchange)


def _mix_bwd_in(dh, h, gain, dproj, w_in, exchange=None):
    tm = 512

    def body(dh_ref, h_ref, g_ref, dp_ref, w_hbm, dx_ref, dg_ref, w_ref):
        _stage([(w_hbm, w_ref)])
        du = jnp.zeros((tm, D), F32)
        for j in range(NSH):
            du = du + _nt(dp_ref[:, j * D:(j + 1) * D], w_ref[j])
        r, hr = _rms(h_ref[...])
        dx, dgain = _rms_bwd(du, hr, r, g_ref[...])
        dx_ref[...] = dh_ref[...] + dx

        @pl.when(pl.program_id(0) == 0)
        def _():
            dg_ref[...] = jnp.zeros_like(dg_ref)

        dg_ref[...] += dgain

    return _call(
        body, (dh, h, gain, dproj, w_in), name="mix_bwd_in", grid=(S // tm,),
        in_specs=[_rows(tm, D), _rows(tm, D), _fixed((1, D)), _rows(tm, 4 * D), _ANY],
        out_specs=[_rows(tm, D), _fixed((1, D))],
        out_shape=[_sds((S, D), F32), _sds((1, D), F32)],
        scratch_shapes=_vmem_like(w_in),
        compiler_params=_params(("arbitrary",), 48), exchange=exchange)


def _wgrad(a, b, nblk, ti, name, out_dtype=BF16, exchange=None):
    ka, n = a.shape[1], b.shape[1]
    ns = n // nblk

    def body(a_ref, b_ref, o_ref):
        o_ref[...] = _tn(a_ref[...].astype(BF16), b_ref[...].astype(BF16)).astype(out_dtype)

    res = _call(
        body, (a, b), name=name, grid=(nblk, ka // ti),
        in_specs=[pl.BlockSpec((S, ti), lambda j, i: (0, i)), pl.BlockSpec((S, ns), lambda j, i: (0, j))],
        out_specs=[pl.BlockSpec((None, ti, ns), lambda j, i: (j, i, 0))],
        out_shape=[_sds((nblk, ka, ns), out_dtype)],
        compiler_params=_params(("arbitrary", "arbitrary"), 56), exchange=exchange)
    return res[0] if exchange is None else (res[0][0], res[1])


def _wgrad_groups(pm, dyg):
    def body(a_ref, b_ref, o_ref):
        o_ref[...] = _tn(a_ref[...], b_ref[...])

    col = pl.BlockSpec((S, PG), lambda g: (0, g))
    return pl.pallas_call(
        body, name="wgrad_groups", grid=(PW // PG,),
        in_specs=[col, col], out_specs=pl.BlockSpec((None, PG, PG), lambda g: (g, 0, 0)),
        out_shape=_sds((PW // PG, PG, PG), F32),
        compiler_params=_params(("arbitrary",), 32),
    )(pm, dyg)


def _place():
    x, y, c = lax.axis_index("x"), lax.axis_index("y"), lax.axis_index("c")
    chips = [(1 - x, y), (x, 1 - y), (1 - x, 1 - y)]
    return x, y, c, chips


def _remote(src, dst, ssem, rsem, dev):
    return pltpu.make_async_remote_copy(src_ref=src, dst_ref=dst, send_sem=ssem, recv_sem=rsem,
                                        device_id=dev, device_id_type=MESH)


def _cast_into_block(w, me_idx, name):
    rows, cols = w.shape
    tr = _row_block(rows)

    def body(me_ref, w_ref, o_ref):
        o_ref[...] = w_ref[...].astype(BF16)

    return pl.pallas_call(
        body, name=name, out_shape=_sds((NSH, rows, cols), BF16),
        grid_spec=pltpu.PrefetchScalarGridSpec(
            num_scalar_prefetch=1, grid=(rows // tr,),
            in_specs=[pl.BlockSpec((tr, cols), lambda r, me: (r, 0))],
            out_specs=pl.BlockSpec((None, tr, cols), lambda r, me: (me[0], r, 0))),
        compiler_params=_params(("arbitrary",), 32),
    )(me_idx, w)


def _ex_gather(bufs):
    n = len(bufs)

    def copies(outs, ssem, rsem, only_first=False):
        x, y, c, chips = _place()
        me, sib = 2 * x + y, (x, y, 1 - c)
        first, relay, last = [], [], []
        for w in range(n):
            half = outs[w].shape[1] // 2
            mine = outs[w].at[me, pl.ds(c * half, half)]
            for k, (px, py) in enumerate(chips):
                sems = (ssem.at[6 * w + k], rsem.at[6 * w + k])
                sib_sems = (ssem.at[6 * w + 3 + k], rsem.at[6 * w + 3 + k])
                first.append(_remote(mine, mine, *sems, (px, py, c)))
                if only_first:
                    continue
                got = outs[w].at[2 * px + py, pl.ds(c * half, half)]
                relay.append((_remote(got, got, *sems, (px, py, c)), _remote(got, got, *sib_sems, sib)))
                theirs = outs[w].at[2 * px + py, pl.ds((1 - c) * half, half)]
                last.append(_remote(theirs, theirs, *sib_sems, sib))
        return first, relay, last

    def start(ins, outs, ssem, rsem):
        for cp in copies(outs, ssem, rsem, only_first=True)[0]:
            cp.start()

    def finish(ins, outs, ssem, rsem):
        first, relay, last = copies(outs, ssem, rsem)
        for arrived, onward in relay:
            arrived.wait_recv()
            onward.start()
        for cp in last:
            cp.wait_recv()
        for cp in first:
            cp.wait_send()
        for _, onward in relay:
            onward.wait_send()

    return Exchange(bufs, [_sds(b.shape, b.dtype) for b in bufs], {w: w for w in range(n)}, 6 * n, start, finish)


def _simple_exchange(arrays, landing, aliases, make_copies):
    def start(ins, outs, ssem, rsem):
        for cp, _ in make_copies(ins, outs, ssem, rsem, False):
            cp.start()

    def finish(ins, outs, ssem, rsem):
        cps = make_copies(ins, outs, ssem, rsem, True)
        for _, landed in cps:
            landed.wait_recv()
        for cp, _ in cps:
            cp.wait_send()

    return Exchange(arrays, landing, aliases, len(arrays) * 3, start, finish)


def _ex_pair_swap(grads):
    def make(ins, outs, ssem, rsem, landing):
        x, y, c, _ = _place()
        cps = [_remote(ins[w].at[:, 1 - c], outs[w], ssem.at[w], rsem.at[w], (x, y, 1 - c))
               for w in range(len(grads))]
        return [(cp, cp) for cp in cps]

    return _simple_exchange(grads, [_sds((NSH,) + g.shape[2:], g.dtype) for g in grads], {}, make)


def _ex_scatter(parts):
    def make(ins, outs, ssem, rsem, landing):
        x, y, c, chips = _place()
        out = []
        for w in range(len(parts)):
            for k, (px, py) in enumerate(chips):
                sems = (ssem.at[3 * w + k], rsem.at[3 * w + k])
                out.append((_remote(ins[w].at[2 * px + py], outs[w].at[k], *sems, (px, py, c)),
                            _remote(outs[w].at[k], outs[w].at[k], *sems, (px, py, c)) if landing else None))
        return out

    return _simple_exchange(parts, [_sds((3,) + p.shape[1:], p.dtype) for p in parts], {}, make)


def _ex_share(bufs):
    def make(ins, outs, ssem, rsem, landing):
        x, y, c, _ = _place()
        sib = (x, y, 1 - c)
        return [(_remote(outs[w].at[c], outs[w].at[c], ssem.at[w], rsem.at[w], sib),
                 _remote(outs[w].at[1 - c], outs[w].at[1 - c], ssem.at[w], rsem.at[w], sib) if landing else None)
                for w in range(len(bufs))]

    return _simple_exchange(bufs, [_sds(b.shape, b.dtype) for b in bufs], {w: w for w in range(len(bufs))}, make)


def _gather_small(block):
    m_per, n = block.shape

    def body(x_ref, out_ref, ssem, rsem, lsem):
        x, y, c, chips = _place()
        me, sib = (x, y, c), (x, y, 1 - c)

        def rows(px, py, pc):
            return out_ref.at[pl.ds((4 * px + 2 * py + pc) * m_per, m_per), :]

        def copy(k, blk, to, src=None):
            return _remote(rows(*blk) if src is None else src, rows(*blk), ssem.at[k], rsem.at[k], to)

        mine = pltpu.make_async_copy(x_ref, rows(*me), lsem)
        mine.start()
        first = [copy(0, me, sib, src=x_ref)]
        first += [copy(1 + j, me, (*chip, c), src=x_ref) for j, chip in enumerate(chips)]
        for cp in first:
            cp.start()
        passed = [copy(4 + j, (*chip, c), sib) for j, chip in enumerate(chips)]
        for j, chip in enumerate(chips):
            copy(1 + j, (*chip, c), me).wait_recv()
            passed[j].start()
        copy(0, sib, me).wait_recv()
        for j, chip in enumerate(chips):
            copy(4 + j, (*chip, 1 - c), me).wait_recv()
        for cp in first + passed:
            cp.wait_send()
        mine.wait()

    return pl.pallas_call(
        body, name="gather_small", out_shape=jax.ShapeDtypeStruct((8 * m_per, n), block.dtype),
        in_specs=[_VM], out_specs=_VM,
        scratch_shapes=[pltpu.SemaphoreType.DMA((7,)), pltpu.SemaphoreType.DMA((7,)), pltpu.SemaphoreType.DMA],
    )(block)


def _row_block(rows):
    return max(t for t in range(16, 257, 16) if rows % t == 0)


def _pair_sum(grad, got, c_idx, name):
    _, _, half, cols = grad.shape
    tr = _row_block(half)

    def body(c_ref, a_ref, b_ref, o_ref):
        o_ref[...] = (a_ref[...].astype(F32) + b_ref[...].astype(F32)).astype(BF16)

    return pl.pallas_call(
        body, name=name, out_shape=_sds((NSH, half, cols), BF16),
        grid_spec=pltpu.PrefetchScalarGridSpec(
            num_scalar_prefetch=1, grid=(NSH, half // tr),
            in_specs=[pl.BlockSpec((None, None, tr, cols), lambda j, r, c: (j, c[0], r, 0)),
                      pl.BlockSpec((None, tr, cols), lambda j, r, c: (j, r, 0))],
            out_specs=pl.BlockSpec((None, tr, cols), lambda j, r, c: (j, r, 0))),
        compiler_params=_params(("arbitrary", "arbitrary"), 32),
    )(c_idx, grad, got)


def _chip_sum(own, got, place, name):
    _, half, cols = own.shape
    tr = _row_block(half)

    def body(place_ref, own_ref, got_ref, o_ref):
        acc = own_ref[...].astype(F32)
        for k in range(3):
            acc = acc + got_ref[k].astype(F32)
        o_ref[...] = acc

    return pl.pallas_call(
        body, name=name, out_shape=_sds((2, half, cols), F32),
        grid_spec=pltpu.PrefetchScalarGridSpec(
            num_scalar_prefetch=1, grid=(half // tr,),
            in_specs=[pl.BlockSpec((None, tr, cols), lambda r, p: (p[0], r, 0)),
                      pl.BlockSpec((3, tr, cols), lambda r, p: (0, r, 0))],
            out_specs=pl.BlockSpec((None, tr, cols), lambda r, p: (p[1], r, 0))),
        compiler_params=_params(("arbitrary",), 32),
    )(place, own, got)


def _adamw_math(w, g, m, v):
    m = B1 * m + (1.0 - B1) * g
    v = B2 * v + (1.0 - B2) * (g * g)
    m_hat = m / (1.0 - B1 ** STEP)
    v_hat = v / (1.0 - B2 ** STEP)
    return -LR * (m_hat / (jnp.sqrt(v_hat) + AEPS) + WD * w), m, v


def _adamw(w, g, m, v, name, exchange=None):
    rows, cols = w.shape
    tr = _row_block(rows)

    def body(w_ref, g_ref, m_ref, v_ref, go_ref, d_ref, nm_ref, nv_ref):
        g = g_ref[...]
        go_ref[...] = g
        d_ref[...], nm_ref[...], nv_ref[...] = _adamw_math(w_ref[...], g, m_ref[...], v_ref[...])

    blk = pl.BlockSpec((tr, cols), lambda r: (r, 0))
    return _call(
        body, (w, g, m, v), name=name, grid=(rows // tr,), out_shape=[_sds(w.shape, F32)] * 4,
        in_specs=[blk] * 4, out_specs=[blk] * 4,
        compiler_params=_params(("arbitrary",), 32), exchange=exchange)


def _small_update(gathered, w, m, v):
    rows = w.shape[0]

    def body(ga_ref, w_ref, m_ref, v_ref, g_ref, d_ref, nm_ref, nv_ref):
        g = ga_ref[0:rows, :]
        for dev in range(1, 8):
            g = g + ga_ref[dev * rows:(dev + 1) * rows, :]
        g_ref[...] = g
        d_ref[...], nm_ref[...], nv_ref[...] = _adamw_math(w_ref[...], g, m_ref[...], v_ref[...])

    return pl.pallas_call(
        body, name="small_update", out_shape=[jax.ShapeDtypeStruct(w.shape, F32)] * 4,
        in_specs=[_VM] * 4, out_specs=[_VM] * 4,
    )(gathered, w, m, v)


SMALL = ("ffn1_norm", "mix_norm", "ffn2_norm", "final_norm", "pool_scale", "pool_w_group", "loss")
BIG = ("ffn1_w_gate_up", "ffn1_w_down", "w_in", "w_branch_pool", "w_branch_attn", "w_out",
       "ffn2_w_gate_up", "ffn2_w_down")
ORDER = ("ffn1_norm", "ffn1_w_gate_up", "ffn1_w_down", "mix_norm", "w_in", "pool_w_group", "pool_scale",
         "w_branch_pool", "w_branch_attn", "w_out", "ffn2_norm", "ffn2_w_gate_up", "ffn2_w_down", "final_norm")
SMALL_ROWS = 560


def _pack_small(t):
    parts = []
    for k in SMALL:
        rows = t[k].reshape(-1, 128) if k in t else jnp.zeros((1, 128), F32)
        parts.append(jnp.pad(rows, ((0, -rows.shape[0] % 8), (0, 0))))
    packed = jnp.concatenate(parts, axis=0)
    assert packed.shape == (SMALL_ROWS, 128), packed.shape
    return packed


def _unpack_small(packed, like):
    out, at = {}, 0
    for k in SMALL:
        n = like[k].size // 128 if k in like else 1
        out[k] = packed[at:at + n].reshape(like[k].shape) if k in like else packed[at, 0]
        at += n + (-n % 8)
    return out


def _halves(g):
    return g.reshape(NSH, 2, g.shape[1] // 2, g.shape[2])


def kernel(x, ffn1_norm, ffn1_w_gate_up, ffn1_w_down, mix_norm, w_in, pool_w_group, pool_scale, w_branch_pool, w_branch_attn, w_out, ffn2_norm, ffn2_w_gate_up, ffn2_w_down, final_norm, loss_target, m_ffn1_norm, m_ffn1_w_gate_up, m_ffn1_w_down, m_mix_norm, m_w_in, m_pool_w_group, m_pool_scale, m_w_branch_pool, m_w_branch_attn, m_w_out, m_ffn2_norm, m_ffn2_w_gate_up, m_ffn2_w_down, m_final_norm, v_ffn1_norm, v_ffn1_w_gate_up, v_ffn1_w_down, v_mix_norm, v_w_in, v_pool_w_group, v_pool_scale, v_w_branch_pool, v_w_branch_attn, v_w_out, v_ffn2_norm, v_ffn2_w_gate_up, v_ffn2_w_down, v_final_norm):
    wts = dict(ffn1_norm=ffn1_norm, ffn1_w_gate_up=ffn1_w_gate_up, ffn1_w_down=ffn1_w_down, mix_norm=mix_norm,
               w_in=w_in, pool_w_group=pool_w_group, pool_scale=pool_scale, w_branch_pool=w_branch_pool,
               w_branch_attn=w_branch_attn, w_out=w_out, ffn2_norm=ffn2_norm, ffn2_w_gate_up=ffn2_w_gate_up,
               ffn2_w_down=ffn2_w_down, final_norm=final_norm)
    mom = dict(ffn1_norm=m_ffn1_norm, ffn1_w_gate_up=m_ffn1_w_gate_up, ffn1_w_down=m_ffn1_w_down,
               mix_norm=m_mix_norm, w_in=m_w_in, pool_w_group=m_pool_w_group, pool_scale=m_pool_scale,
               w_branch_pool=m_w_branch_pool, w_branch_attn=m_w_branch_attn, w_out=m_w_out,
               ffn2_norm=m_ffn2_norm, ffn2_w_gate_up=m_ffn2_w_gate_up, ffn2_w_down=m_ffn2_w_down,
               final_norm=m_final_norm)
    var = dict(ffn1_norm=v_ffn1_norm, ffn1_w_gate_up=v_ffn1_w_gate_up, ffn1_w_down=v_ffn1_w_down,
               mix_norm=v_mix_norm, w_in=v_w_in, pool_w_group=v_pool_w_group, pool_scale=v_pool_scale,
               w_branch_pool=v_w_branch_pool, w_branch_attn=v_w_branch_attn, w_out=v_w_out,
               ffn2_norm=v_ffn2_norm, ffn2_w_gate_up=v_ffn2_w_gate_up, ffn2_w_down=v_ffn2_w_down,
               final_norm=v_final_norm)

    c_idx = lax.axis_index("c").astype(jnp.int32).reshape(1)
    me_idx = (2 * lax.axis_index("x") + lax.axis_index("y")).astype(jnp.int32).reshape(1)
    place = jnp.concatenate([me_idx, c_idx])
    x0, tgt = x[0], loss_target[0]
    wgrp = pool_w_group[0].astype(BF16)
    g1, gm, g2, gf = ffn1_norm, mix_norm, ffn2_norm, final_norm.reshape(1, D)
    grad, delta, new_m, new_v = {}, {}, {}, {}

    def pair_sums(keys, parts, got):
        return [_pair_sum(parts[i], got[i], c_idx, "pair_sum_" + k) for i, k in enumerate(keys)]

    def chip_sums(keys, chip_parts, owned):
        return [_chip_sum(chip_parts[i], owned[i], place, "chip_sum_" + k) for i, k in enumerate(keys)]

    def adamw(k, exchange=None):
        res = _adamw(wts[k][0], grad[k][0], mom[k][0], var[k][0], "adamw_" + k, exchange=exchange)
        outs, landed = (res, None) if exchange is None else res
        grad[k], delta[k], new_m[k], new_v[k] = (o.reshape(wts[k].shape) for o in outs)
        return landed

    own = {k: _cast_into_block(wts[k][0], me_idx, "cast_" + k) for k in BIG}
    first, late = ("ffn1_w_gate_up", "ffn1_w_down"), ("w_branch_pool", "w_branch_attn", "w_out",
                                                       "ffn2_w_gate_up", "ffn2_w_down")
    full = dict(zip(first, _exchange_alone(_ex_gather([own[k] for k in first]), "gather_ffn1")))
    wgu1, wd1 = full["ffn1_w_gate_up"], full["ffn1_w_down"].reshape(DFF, D)
    (h1, n1, gu1), (win,) = _ffn_fwd(x0, g1, wgu1, wd1, "ffn1_fwd", exchange=_ex_gather([own["w_in"]]))
    u, xp, q, k, v, gp, gs = _mix_in(h1, gm, win)
    (o_sb, ctot), landed = _attn_fwd(q, k, v, exchange=_ex_gather([own[k_] for k_ in late]))
    full.update(zip(late, landed))
    wbp, wba, wout = full["w_branch_pool"], full["w_branch_attn"], full["w_out"].reshape(D, D)
    wgu2, wd2 = full["ffn2_w_gate_up"], full["ffn2_w_down"].reshape(DFF, D)
    h2, pm, p, yp, ys, mm = _mix_out(h1, xp, o_sb, gp, gs, wgrp, pool_scale, wbp, wba, wout)
    h3, n3, gu3 = _ffn_fwd(h2, g2, wgu2, wd2, "ffn2_fwd")
    dh3, loss_row, d_gf = _head(h3, tgt, gf)

    dh2, dgu3, a3, d_g2 = _ffn_bwd(dh3, h2, g2, gu3, wgu2, wd2, "ffn2_bwd")
    ka = ("ffn2_w_gate_up", "ffn2_w_down")
    pa = [_halves(_wgrad(n3, dgu3, NSH, 512, "wgrad_gu2")),
          _halves(_wgrad(a3, dh3, 1, FFS, "wgrad_d2").reshape(NSH, DFF // NSH, D))]
    (dlg, dyp, dys, do_sb, dyg, dxp, d_scale), got_a = _mix_bwd_out(
        dh2, gp, gs, yp, ys, pm, wgrp, pool_scale, wbp, wba, wout, exchange=_ex_pair_swap(pa))
    chip_a = pair_sums(ka, pa, got_a)
    kb = ("w_out", "w_branch_pool", "w_branch_attn")
    pb = [_halves(_wgrad(mm, dh2, 1, 512, "wgrad_out").reshape(NSH, D // NSH, D)),
          _halves(_wgrad(p, dyp, NSH, PW, "wgrad_bp")), _halves(_wgrad(o_sb, dys, NSH, SBW, "wgrad_ba"))]
    chip_b = pair_sums(kb, pb, _exchange_alone(_ex_pair_swap(pb), "pair_swap_mix"))
    (dq, dk, dv), owned_ab = _attn_bwd(q, k, v, do_sb, ctot, exchange=_ex_scatter(chip_a + chip_b))
    halves_ab = chip_sums(ka + kb, chip_a + chip_b, owned_ab)
    dproj = jnp.concatenate([dxp, dq, dk, dv, dlg], axis=1)
    (dh1, d_gm), both_ab = _mix_bwd_in(dh2, h1, gm, dproj, win, exchange=_ex_share(halves_ab))
    for i, k_ in enumerate(ka + kb):
        grad[k_] = both_ab[i].reshape(wts[k_].shape)
    pc = [_halves(_wgrad(u, dproj, NSH, 512, "wgrad_in"))]
    (dx, dgu1, a1, d_g1), got_c = _ffn_bwd(dh1, x0, g1, gu1, wgu1, wd1, "ffn1_bwd", exchange=_ex_pair_swap(pc))
    chip_c = pair_sums(("w_in",), pc, got_c)
    pd, owned_c = _wgrad(n1, dgu1, NSH, 512, "wgrad_gu1", exchange=_ex_scatter(chip_c))
    pd = [_halves(pd)]
    pe, got_d = _wgrad(a1, dh1, 1, FFS, "wgrad_d1", exchange=_ex_pair_swap(pd))
    pe = [_halves(pe.reshape(NSH, DFF // NSH, D))]
    chip_d = pair_sums(("ffn1_w_gate_up",), pd, got_d)
    got_e = adamw("ffn2_w_gate_up", exchange=_ex_pair_swap(pe))
    chip_e = pair_sums(("ffn1_w_down",), pe, got_e)
    halves_c = chip_sums(("w_in",), chip_c, owned_c)
    owned_de = adamw("ffn2_w_down", exchange=_ex_scatter(chip_d + chip_e))
    kde = ("ffn1_w_gate_up", "ffn1_w_down")
    halves_de = chip_sums(kde, chip_d + chip_e, owned_de)
    both_cde = adamw("w_out", exchange=_ex_share(halves_c + halves_de))
    for i, k_ in enumerate(("w_in",) + kde):
        grad[k_] = both_cde[i].reshape(wts[k_].shape)
    for k_ in ("w_branch_pool", "w_branch_attn", "w_in") + kde:
        adamw(k_)

    small_g = dict(ffn1_norm=d_g1, mix_norm=d_gm, ffn2_norm=d_g2, final_norm=d_gf, pool_scale=d_scale,
                   pool_w_group=_wgrad_groups(pm, dyg), loss=loss_row)
    gathered = _gather_small(_pack_small(small_g))
    sg, sd, sm, sv = _small_update(gathered, _pack_small(wts), _pack_small(mom), _pack_small(var))
    sums = _unpack_small(sg, wts)
    loss = sums.pop("loss")
    grad.update(sums)
    for dst, packed in ((delta, sd), (new_m, sm), (new_v, sv)):
        vals = _unpack_small(packed, wts)
        vals.pop("loss")
        dst.update(vals)
    return (loss, dx[None], *[grad[k_] for k_ in ORDER], *[delta[k_] for k_ in ORDER],
            *[new_m[k_] for k_ in ORDER], *[new_v[k_] for k_ in ORDER])
```

```python
import functools

import jax
import jax.numpy as jnp
from jax import lax
from jax.experimental import pallas as pl
from jax.experimental.pallas import tpu as pltpu

F32 = jnp.float32
BF16 = jnp.bfloat16

S = 2048
D = 1024
DFF = 2816
FFS = 2 * DFF // 4
NSH = 4
PW = 512
PG = 128
POOL_WINDOWS = (2, 4, 8, 16)
HALO = 16
SBW = 512
DH = 64
EPS = 1e-6
SCALE = 0.125
LOG2E = 1.4426950408889634
TA = 256
QB = 2
MIB = 1024 * 1024

LR, B1, B2, AEPS, WD, STEP = 0.001, 0.9, 0.999, 1e-08, 0.01, 10

_VM = pl.BlockSpec(memory_space=pltpu.VMEM)
_ANY = pl.BlockSpec(memory_space=pl.ANY)
MESH = pl.DeviceIdType.MESH


def _nn(a, b):
    return jnp.dot(a, b, preferred_element_type=F32)


def _nt(a, b):
    return lax.dot_general(a, b, (((1,), (1,)), ((), ())), preferred_element_type=F32)


def _tn(a, b):
    return lax.dot_general(a, b, (((0,), (0,)), ((), ())), preferred_element_type=F32)


def _params(sem, vmem_mib):
    return pltpu.CompilerParams(dimension_semantics=sem, vmem_limit_bytes=vmem_mib * MIB)


def _rows(tm, width):
    return pl.BlockSpec((tm, width), lambda i: (i, 0))


def _fixed(shape):
    return pl.BlockSpec(shape, lambda *_: (0,) * len(shape))


def _sds(shape, dtype):
    return pltpu.HBM(shape, dtype)


def _in_hbm(args):
    return [pltpu.with_memory_space_constraint(a, pltpu.HBM) for a in args]


def _stage(pairs):
    @pl.when(pl.program_id(0) == 0)
    def _():
        for src, dst in pairs:
            pltpu.sync_copy(src, dst)


def _vmem_like(*arrays):
    return [pltpu.VMEM(a.shape, a.dtype) for a in arrays]


class Exchange:
    def __init__(self, arrays, landing, aliases, n_sems, start, finish):
        self.arrays, self.landing, self.aliases, self.n_sems = list(arrays), list(landing), dict(aliases), n_sems
        self.start, self.finish = start, finish


def _join(a, b):
    na, la = len(a.arrays), len(a.landing)

    def both(fa, fb):
        def run(ins, outs, ssem, rsem):
            fa(ins[:na], outs[:la], ssem.at[pl.ds(0, a.n_sems)], rsem.at[pl.ds(0, a.n_sems)])
            fb(ins[na:], outs[la:], ssem.at[pl.ds(a.n_sems, b.n_sems)], rsem.at[pl.ds(a.n_sems, b.n_sems)])
        return run

    aliases = {**a.aliases, **{na + i: la + j for i, j in b.aliases.items()}}
    return Exchange(a.arrays + b.arrays, a.landing + b.landing, aliases, a.n_sems + b.n_sems,
                    both(a.start, b.start), both(a.finish, b.finish))


def _call(body, args, *, name, grid, in_specs, out_specs, out_shape, scratch_shapes=(), compiler_params=None,
          exchange=None):
    if exchange is None:
        return pl.pallas_call(body, name=name, grid=grid, in_specs=in_specs, out_specs=out_specs,
                              out_shape=out_shape, scratch_shapes=list(scratch_shapes),
                              compiler_params=compiler_params)(*args)
    ex = exchange
    n_in, n_out, n_scr = len(in_specs), len(out_specs), len(scratch_shapes)
    na, nl = len(ex.arrays), len(ex.landing)

    def hosted(*refs):
        at = [0]

        def take(n):
            at[0] += n
            return refs[at[0] - n:at[0]]

        k_in, e_in, k_out, e_out, k_scr = take(n_in), take(na), take(n_out), take(nl), take(n_scr)
        ssem, rsem = take(2)
        ids = [pl.program_id(a) for a in range(len(grid))]
        first = functools.reduce(jnp.logical_and, [i == 0 for i in ids])
        last = functools.reduce(jnp.logical_and, [i == g - 1 for i, g in zip(ids, grid)])

        @pl.when(first)
        def _():
            ex.start(e_in, e_out, ssem, rsem)

        body(*k_in, *k_out, *k_scr)

        @pl.when(last)
        def _():
            ex.finish(e_in, e_out, ssem, rsem)

    outs = pl.pallas_call(
        hosted, name=name, grid=grid,
        in_specs=list(in_specs) + [_ANY] * na, out_specs=list(out_specs) + [_ANY] * nl,
        out_shape=list(out_shape) + ex.landing,
        scratch_shapes=list(scratch_shapes) + [pltpu.SemaphoreType.DMA((ex.n_sems,))] * 2,
        input_output_aliases={n_in + i: n_out + j for i, j in ex.aliases.items()},
        compiler_params=compiler_params,
    )(*args, *_in_hbm(ex.arrays))
    return outs[:n_out], outs[n_out:]


def _exchange_alone(ex, name):
    def body(*refs):
        na, nl = len(ex.arrays), len(ex.landing)
        ex.start(refs[:na], refs[na:na + nl], refs[-2], refs[-1])
        ex.finish(refs[:na], refs[na:na + nl], refs[-2], refs[-1])

    return pl.pallas_call(
        body, name=name, in_specs=[_ANY] * len(ex.arrays), out_specs=[_ANY] * len(ex.landing),
        out_shape=ex.landing, scratch_shapes=[pltpu.SemaphoreType.DMA((ex.n_sems,))] * 2,
        input_output_aliases=ex.aliases,
    )(*_in_hbm(ex.arrays))


def _rms(x):
    r = lax.rsqrt(jnp.mean(x * x, axis=-1, keepdims=True) + EPS)
    return r, x * r


def _rms_bwd(dn, xr, r, gain):
    dng = dn * gain
    dx = r * (dng - xr * jnp.mean(dng * xr, axis=-1, keepdims=True))
    return dx, jnp.sum(dn * xr, axis=0, keepdims=True)


def _ffn_fwd(x, gain, wgu, wd, name, exchange=None):
    tm = 256

    def body(x_ref, g_ref, wgu_hbm, wd_hbm, h_ref, n_ref, gu_ref, a_ref, wgu_ref, wd_ref):
        _stage([(wgu_hbm, wgu_ref), (wd_hbm, wd_ref)])
        x = x_ref[...]
        _, xr = _rms(x)
        n = (xr * g_ref[...]).astype(BF16)
        n_ref[...] = n
        acc = jnp.zeros((tm, D), F32)
        for j in range(2):
            g = _nn(n, wgu_ref[j])
            u = _nn(n, wgu_ref[2 + j])
            gu_ref[:, j * FFS:(j + 1) * FFS] = g.astype(BF16)
            gu_ref[:, (2 + j) * FFS:(3 + j) * FFS] = u.astype(BF16)
            half_act = (0.5 * (g * jax.nn.sigmoid(g) * u)).astype(BF16)
            a_ref[:, j * FFS:(j + 1) * FFS] = half_act
            acc = acc + _nn(half_act, wd_ref[j * FFS:(j + 1) * FFS, :])
        h_ref[...] = x + acc

    return _call(
        body, (x, gain, wgu, wd), name=name, grid=(S // tm,),
        in_specs=[_rows(tm, D), _fixed((1, D)), _ANY, _ANY],
        out_specs=[_rows(tm, D), _rows(tm, D), _rows(tm, 4 * FFS), _rows(tm, DFF)],
        out_shape=[_sds((S, D), F32), _sds((S, D), BF16), _sds((S, 4 * FFS), BF16), _sds((S, DFF), BF16)],
        scratch_shapes=_vmem_like(wgu, wd),
        compiler_params=_params(("arbitrary",), 56), exchange=exchange)


def _ffn_bwd_act(dh, gu, wd, name, exchange=None):
    tm = 512

    def body(dh_ref, gu_ref, wd_hbm, dgu_ref, wd_ref):
        _stage([(wd_hbm, wd_ref)])
        dhb = dh_ref[...].astype(BF16)
        for j in range(2):
            g = gu_ref[:, j * FFS:(j + 1) * FFS].astype(F32)
            u = gu_ref[:, (2 + j) * FFS:(3 + j) * FFS].astype(F32)
            da = 0.5 * _nt(dhb, wd_ref[j * FFS:(j + 1) * FFS, :])
            sg = jax.nn.sigmoid(g)
            dgu_ref[:, j * FFS:(j + 1) * FFS] = (da * u * (sg * (1.0 + g * (1.0 - sg)))).astype(BF16)
            dgu_ref[:, (2 + j) * FFS:(3 + j) * FFS] = (da * (g * sg)).astype(BF16)

    res = _call(
        body, (dh, gu, wd), name=name, grid=(S // tm,),
        in_specs=[_rows(tm, D), _rows(tm, 4 * FFS), _ANY], out_specs=[_rows(tm, 4 * FFS)],
        out_shape=[_sds((S, 4 * FFS), BF16)], scratch_shapes=_vmem_like(wd),
        compiler_params=_params(("arbitrary",), 56), exchange=exchange)
    return res[0] if exchange is None else (res[0][0], res[1])


def _ffn_bwd_in(dh, x, gain, dgu, wgu, name, exchange=None):
    tm = 512

    def body(dh_ref, x_ref, g_ref, dgu_ref, wgu_hbm, dx_ref, dg_ref, wgu_ref):
        _stage([(wgu_hbm, wgu_ref)])
        dn = jnp.zeros((tm, D), F32)
        for j in range(NSH):
            dn = dn + _nt(dgu_ref[:, j * FFS:(j + 1) * FFS], wgu_ref[j])
        r, xr = _rms(x_ref[...])
        dx, dgain = _rms_bwd(dn, xr, r, g_ref[...])
        dx_ref[...] = dh_ref[...] + dx

        @pl.when(pl.program_id(0) == 0)
        def _():
            dg_ref[...] = jnp.zeros_like(dg_ref)

        dg_ref[...] += dgain

    return _call(
        body, (dh, x, gain, dgu, wgu), name=name, grid=(S // tm,),
        in_specs=[_rows(tm, D), _rows(tm, D), _fixed((1, D)), _rows(tm, 4 * FFS), _ANY],
        out_specs=[_rows(tm, D), _fixed((1, D))],
        out_shape=[_sds((S, D), F32), _sds((1, D), F32)], scratch_shapes=_vmem_like(wgu),
        compiler_params=_params(("arbitrary",), 56), exchange=exchange)


def _head(h, target, gain):
    tm = 512

    def body(h_ref, t_ref, g_ref, dh_ref, loss_ref, dg_ref):
        gain = g_ref[...]
        r, hr = _rms(h_ref[...])
        err = hr * gain - t_ref[...]
        dy = err * (1.0 / D)
        dh, dgain = _rms_bwd(dy, hr, r, gain)
        dh_ref[...] = dh

        @pl.when(pl.program_id(0) == 0)
        def _():
            dg_ref[...] = jnp.zeros_like(dg_ref)
            loss_ref[...] = jnp.zeros_like(loss_ref)

        dg_ref[...] += dgain
        loss_ref[...] += jnp.full((1, 128), (0.5 / D) * jnp.sum(err * err), F32)

    return pl.pallas_call(
        body, name="head", grid=(S // tm,),
        in_specs=[_rows(tm, D), _rows(tm, D), _fixed((1, D))],
        out_specs=[_rows(tm, D), _fixed((1, 128)), _fixed((1, D))],
        out_shape=[_sds((S, D), F32), _sds((1, 128), F32), _sds((1, D), F32)],
        compiler_params=_params(("arbitrary",), 40),
    )(h, target, gain)


def _mix_in(h, gain, w_in):
    tm = 512

    def body(h_ref, g_ref, w_hbm, u_ref, xp_ref, q_ref, k_ref, v_ref, gp_ref, gs_ref, w_ref):
        _stage([(w_hbm, w_ref)])
        _, hr = _rms(h_ref[...])
        u = (hr * g_ref[...]).astype(BF16)
        u_ref[...] = u
        p0 = _nn(u, w_ref[0])
        xp_ref[...] = p0[:, :PW]
        q_ref[...] = p0[:, PW:].astype(BF16)
        p1 = _nn(u, w_ref[1])
        k_ref[...] = p1[:, :SBW].astype(BF16)
        v_ref[...] = p1[:, SBW:].astype(BF16)
        gp_ref[...] = jax.nn.sigmoid(_nn(u, w_ref[2])).astype(BF16)
        gs_ref[...] = jax.nn.sigmoid(_nn(u, w_ref[3])).astype(BF16)

    return pl.pallas_call(
        body, name="mix_in", grid=(S // tm,),
        in_specs=[_rows(tm, D), _fixed((1, D)), _ANY],
        out_specs=[_rows(tm, D), _rows(tm, PW), _rows(tm, SBW), _rows(tm, SBW), _rows(tm, SBW),
                   _rows(tm, D), _rows(tm, D)],
        out_shape=[_sds((S, D), BF16), _sds((S, PW), F32), _sds((S, SBW), BF16), _sds((S, SBW), BF16),
                   _sds((S, SBW), BF16), _sds((S, D), BF16), _sds((S, D), BF16)],
        scratch_shapes=_vmem_like(w_in),
        compiler_params=_params(("arbitrary",), 48),
    )(h, gain, w_in)


def _hilo_dot(x, tri):
    hi = x.astype(BF16)
    lo = (x - hi.astype(F32)).astype(BF16)
    return _nn(hi, tri) + _nn(lo, tri)


def _log_terms(qk):
    z2 = qk * (SCALE * LOG2E)
    lb = jnp.minimum(z2, 0.0) - jnp.log2(1.0 + jnp.exp2(-jnp.abs(z2)))
    return lb, lb - z2


def _head_masks():
    lane = lax.broadcasted_iota(jnp.int32, (1, 2 * DH), 1)
    return (lane < DH, lane >= DH)


def _attn_fwd(q, k, v, exchange=None):
    T = TA

    def body(q_ref, k_ref, v_ref, o_ref, c_ref):
        i2 = 2 * pl.program_id(1)
        row = lax.broadcasted_iota(jnp.int32, (T, T), 0)
        col = lax.broadcasted_iota(jnp.int32, (T, T), 1)
        after = (row > col).astype(BF16)
        causal = col < row
        masks = _head_masks()
        qms = {}
        for b in range(QB):
            q2 = q_ref[b * T:(b + 1) * T, :]
            for h, hm in enumerate(masks):
                qms[b, h] = jnp.where(hm, q2, jnp.zeros_like(q2))

        def blocks(keys, pairs, carries, os):
            ks, vms = [], []
            for j in keys:
                rows = pl.ds(pl.multiple_of(j * T, T), T)
                vj = v_ref[rows, :]
                ks.append(k_ref[rows, :])
                vms.append([jnp.where(hm, vj, jnp.zeros_like(vj)) for hm in masks])
            units = [(n, h) for n in range(len(pairs)) for h in range(2)]
            qks = {(n, h): _nt(qms[pairs[n][0], h], ks[pairs[n][1]]) for n, h in units}
            lbs, l1ms = {}, {}
            for u in units:
                lbs[u], l1m = _log_terms(qks[u])
                l1ms[u] = jnp.where(causal, l1m, 0.0) if pairs[u[0]][2] else l1m
            cins = {u: _hilo_dot(l1ms[u], after) for u in units}
            carries, os = dict(carries), list(os)
            for n, h in units:
                b, key, diag = pairs[n]
                a = jnp.exp2(lbs[n, h] + cins[n, h] + carries[b, h])
                if diag:
                    a = jnp.where(causal, a, 0.0)
                os[b] = os[b] + _nn(a.astype(BF16), vms[key][h])
                carries[b, h] = carries[b, h] + jnp.sum(l1ms[n, h], axis=1, keepdims=True)
            return carries, tuple(os)

        carries = {(b, h): jnp.zeros((T, 1), F32) for b in range(QB) for h in range(2)}
        os = tuple(jnp.zeros((T, 2 * DH), F32) for _ in range(QB))
        carries, os = blocks([i2 + 1, i2], [(1, 0, True), (0, 1, True), (1, 1, False)], carries, os)
        carries, os = lax.fori_loop(
            0, i2, lambda jj, c: blocks([i2 - 1 - jj], [(0, 0, False), (1, 0, False)], c[0], c[1]), (carries, os))
        for b in range(QB):
            o_ref[b * T:(b + 1) * T, :] = os[b].astype(BF16)
            c_ref[b * T:(b + 1) * T, :] = jnp.where(masks[0], carries[b, 0], carries[b, 1])

    blk = pl.BlockSpec((QB * T, 2 * DH), lambda p, i: (i, p))
    full = pl.BlockSpec((S, 2 * DH), lambda p, i: (0, p))
    return _call(
        body, (q, k, v), name="attn_fwd", grid=(SBW // (2 * DH), S // (QB * T)),
        in_specs=[blk, full, full], out_specs=[blk, blk],
        out_shape=[_sds((S, SBW), BF16), _sds((S, SBW), F32)],
        compiler_params=_params(("arbitrary", "arbitrary"), 40), exchange=exchange)


def _attn_bwd(q, k, v, do, ctot, exchange=None):
    T = TA
    nq = S // (QB * T)

    def body(q_ref, k_ref, v_ref, do_ref, c_ref, dq_ref, dk_ref, dv_ref, dk_acc, dv_acc):
        step = pl.program_id(1)
        i2 = 2 * step

        @pl.when(step == 0)
        def _():
            dk_acc[...] = jnp.zeros_like(dk_acc)
            dv_acc[...] = jnp.zeros_like(dv_acc)

        row = lax.broadcasted_iota(jnp.int32, (T, T), 0)
        col = lax.broadcasted_iota(jnp.int32, (T, T), 1)
        upto = (row <= col).astype(BF16)
        before = (row < col).astype(BF16)
        causal = col < row
        masks = _head_masks()
        qms, doms, ctots = {}, {}, {}
        for b in range(QB):
            q2, do2 = q_ref[b * T:(b + 1) * T, :], do_ref[b * T:(b + 1) * T, :]
            for h, hm in enumerate(masks):
                qms[b, h] = jnp.where(hm, q2, jnp.zeros_like(q2))
                doms[b, h] = jnp.where(hm, do2, jnp.zeros_like(do2))
                ctots[b, h] = c_ref[b * T:(b + 1) * T, h * DH:h * DH + 1]

        def blocks(keys, pairs, sums, dqs):
            rows = [pl.ds(pl.multiple_of(j * T, T), T) for j in keys]
            ks, vs = [k_ref[r, :] for r in rows], [v_ref[r, :] for r in rows]
            kms = [[jnp.where(hm, kj, jnp.zeros_like(kj)) for hm in masks] for kj in ks]
            units = [(n, h) for n in range(len(pairs)) for h in range(2)]
            qks = {(n, h): _nt(qms[pairs[n][0], h], ks[pairs[n][1]]) for n, h in units}
            das = {(n, h): _nt(doms[pairs[n][0], h], vs[pairs[n][1]]) for n, h in units}
            lbs, l1ms = {}, {}
            for u in units:
                lbs[u], l1m = _log_terms(qks[u])
                l1ms[u] = jnp.where(causal, l1m, 0.0) if pairs[u[0]][2] else l1m
            pins = {u: _hilo_dot(l1ms[u], upto) for u in units}
            sums = dict(sums)
            a_s, dls, cps = {}, {}, {}
            for n, h in units:
                b, _, diag = pairs[n]
                cl, cp = sums[b, h]
                a = jnp.exp2(lbs[n, h] + (ctots[b, h] - cl) - pins[n, h])
                if diag:
                    a = jnp.where(causal, a, 0.0)
                a_s[n, h] = a.astype(BF16)
                dls[n, h] = das[n, h] * a
                cps[n, h] = cp
                sums[b, h] = (cl + jnp.sum(l1ms[n, h], axis=1, keepdims=True),
                              cp + jnp.sum(dls[n, h], axis=1, keepdims=True))
            pexs = {u: _hilo_dot(dls[u], before) for u in units}
            dzbs = {}
            for u in units:
                dz = dls[u] - jnp.exp2(lbs[u]) * (dls[u] + pexs[u] + cps[u])
                if pairs[u[0]][2]:
                    dz = jnp.where(causal, dz, 0.0)
                dzbs[u] = dz.astype(BF16)
            dqs = list(dqs)
            for n, h in units:
                dqs[pairs[n][0]] = dqs[pairs[n][0]] + _nn(dzbs[n, h], kms[pairs[n][1]][h])
            for key, r in enumerate(rows):
                mine = [(n, h) for n, h in units if pairs[n][1] == key]
                dk_acc[r, :] += functools.reduce(jnp.add, [_tn(dzbs[u], qms[pairs[u[0]][0], u[1]]) for u in mine])
                dv_acc[r, :] += functools.reduce(jnp.add, [_tn(a_s[u], doms[pairs[u[0]][0], u[1]]) for u in mine])
            return sums, tuple(dqs)

        zero = jnp.zeros((T, 1), F32)
        sums = {(b, h): (zero, zero) for b in range(QB) for h in range(2)}
        dqs = tuple(jnp.zeros((T, 2 * DH), F32) for _ in range(QB))
        sums, dqs = lax.fori_loop(
            0, i2, lambda j, c: blocks([j], [(0, 0, False), (1, 0, False)], c[0], c[1]), (sums, dqs))
        _, dqs = blocks([i2, i2 + 1], [(0, 0, True), (1, 0, False), (1, 1, True)], sums, dqs)
        for b in range(QB):
            dq_ref[b * T:(b + 1) * T, :] = (dqs[b] * SCALE).astype(BF16)

        @pl.when(step == nq - 1)
        def _():
            dk_ref[...] = (dk_acc[...] * SCALE).astype(BF16)
            dv_ref[...] = dv_acc[...].astype(BF16)

    blk = pl.BlockSpec((QB * T, 2 * DH), lambda p, i: (i, p))
    full = pl.BlockSpec((S, 2 * DH), lambda p, i: (0, p))
    return _call(
        body, (q, k, v, do, ctot), name="attn_bwd", grid=(SBW // (2 * DH), nq),
        in_specs=[blk, full, full, blk, blk], out_specs=[blk, full, full],
        out_shape=[_sds((S, SBW), BF16), _sds((S, SBW), BF16), _sds((S, SBW), BF16)],
        scratch_shapes=[pltpu.VMEM((S, 2 * DH), F32), pltpu.VMEM((S, 2 * DH), F32)],
        compiler_params=_params(("arbitrary", "arbitrary"), 40), exchange=exchange)


def _pool_counts(first_row, tm):
    pos = first_row + lax.broadcasted_iota(jnp.int32, (tm, 1), 0)
    return [jnp.minimum(pos + 1, w).astype(F32) for w in POOL_WINDOWS]


def _mix_out(h, xp, o_sb, gp, gs, w_group, scale, w_bp, w_ba, w_out):
    tm = 512

    def body(h_ref, xp_ref, o_ref, gp_ref, gs_ref, wg_hbm, sc_ref, wbp_hbm, wba_hbm, wo_hbm,
             h2_ref, pm_ref, p_ref, yp_ref, ys_ref, m_ref, halo, wg_ref, wbp_ref, wba_ref, wo_ref):
        _stage([(wg_hbm, wg_ref), (wbp_hbm, wbp_ref), (wba_hbm, wba_ref), (wo_hbm, wo_ref)])
        i = pl.program_id(0)

        @pl.when(i == 0)
        def _():
            halo[...] = jnp.zeros_like(halo)

        xp = xp_ref[...]
        ext = jnp.concatenate([halo[...], xp], axis=0)
        halo[...] = xp[tm - HALO:, :]
        counts = _pool_counts(i * tm, tm)
        for gi in range(len(POOL_WINDOWS)):
            lanes = slice(gi * PG, (gi + 1) * PG)
            win = ext[:, lanes]
            for step in range(gi + 1):
                win = win + pltpu.roll(win, 1 << step, 0)
            pm = (win[HALO:, :] / counts[gi] - xp[:, lanes]).astype(BF16)
            pm_ref[:, lanes] = pm
            p_ref[:, lanes] = (_nn(pm, wg_ref[gi]) * sc_ref[:, lanes]).astype(BF16)
        pb = p_ref[...]
        ob = o_ref[...]
        for j in range(NSH):
            cols = slice(j * (D // NSH), (j + 1) * (D // NSH))
            yp = _nn(pb, wbp_ref[j])
            ys = _nn(ob, wba_ref[j])
            yp_ref[:, cols] = yp.astype(BF16)
            ys_ref[:, cols] = ys.astype(BF16)
            m_ref[:, cols] = (gp_ref[:, cols].astype(F32) * yp + gs_ref[:, cols].astype(F32) * ys).astype(BF16)
        h2_ref[...] = h_ref[...] + _nn(m_ref[...], wo_ref[...])

    return pl.pallas_call(
        body, name="mix_out", grid=(S // tm,),
        in_specs=[_rows(tm, D), _rows(tm, PW), _rows(tm, SBW), _rows(tm, D), _rows(tm, D),
                  _ANY, _fixed((1, PW)), _ANY, _ANY, _ANY],
        out_specs=[_rows(tm, D), _rows(tm, PW), _rows(tm, PW), _rows(tm, D), _rows(tm, D), _rows(tm, D)],
        out_shape=[_sds((S, D), F32), _sds((S, PW), BF16), _sds((S, PW), BF16), _sds((S, D), BF16),
                   _sds((S, D), BF16), _sds((S, D), BF16)],
        scratch_shapes=[pltpu.VMEM((HALO, PW), F32)] + _vmem_like(w_group, w_bp, w_ba, w_out),
        compiler_params=_params(("arbitrary",), 48),
    )(h, xp, o_sb, gp, gs, w_group, scale, w_bp, w_ba, w_out)


def _mix_bwd_out(dh, gp, gs, yp, ys, pm, w_group, scale, w_bp, w_ba, w_out, exchange=None):
    tm = 512
    nt = S // tm

    def body(dh_ref, gp_ref, gs_ref, yp_ref, ys_ref, pm_ref, wg_hbm, sc_ref, wbp_hbm, wba_hbm, wo_hbm,
             dlg_ref, dyp_ref, dys_ref, do_ref, dyg_ref, dxp_ref, dsc_ref, halo, wg_ref, wbp_ref, wba_ref, wo_ref):
        _stage([(wg_hbm, wg_ref), (wbp_hbm, wbp_ref), (wba_hbm, wba_ref), (wo_hbm, wo_ref)])
        step = pl.program_id(0)

        @pl.when(step == 0)
        def _():
            halo[...] = jnp.zeros_like(halo)
            dsc_ref[...] = jnp.zeros_like(dsc_ref)

        dm = _nt(dh_ref[...].astype(BF16), wo_ref[...])
        gp = gp_ref[...].astype(F32)
        gs = gs_ref[...].astype(F32)
        yp = yp_ref[...].astype(F32)
        ys = ys_ref[...].astype(F32)
        dlg_ref[:, :D] = (dm * yp * gp * (1.0 - gp)).astype(BF16)
        dlg_ref[:, D:] = (dm * ys * gs * (1.0 - gs)).astype(BF16)
        dyp_ref[...] = (dm * gp).astype(BF16)
        dys_ref[...] = (dm * gs).astype(BF16)
        dp = jnp.zeros((tm, PW), F32)
        do = jnp.zeros((tm, SBW), F32)
        for j in range(NSH):
            cols = slice(j * (D // NSH), (j + 1) * (D // NSH))
            dp = dp + _nt(dyp_ref[:, cols], wbp_ref[j])
            do = do + _nt(dys_ref[:, cols], wba_ref[j])
        do_ref[...] = do.astype(BF16)
        counts = _pool_counts((nt - 1 - step) * tm, tm)
        dscale = []
        for gi in range(len(POOL_WINDOWS)):
            lanes = slice(gi * PG, (gi + 1) * PG)
            dpg = dp[:, lanes]
            dscale.append(jnp.sum(dpg * _nn(pm_ref[:, lanes], wg_ref[gi]), axis=0, keepdims=True))
            dyg = (dpg * sc_ref[:, lanes]).astype(BF16)
            dyg_ref[:, lanes] = dyg
            dpm = _nt(dyg, wg_ref[gi])
            per = dpm / counts[gi]
            win = jnp.concatenate([per, halo[:, lanes]], axis=0)
            halo[:, lanes] = per[:HALO, :]
            for s in range(gi + 1):
                win = win + pltpu.roll(win, tm + HALO - (1 << s), 0)
            dxp_ref[:, lanes] = (win[:tm, :] - dpm).astype(BF16)
        dsc_ref[...] += jnp.concatenate(dscale, axis=1)

    rev = lambda width: pl.BlockSpec((tm, width), lambda i: (nt - 1 - i, 0))
    return _call(
        body, (dh, gp, gs, yp, ys, pm, w_group, scale, w_bp, w_ba, w_out), name="mix_bwd_out", grid=(nt,),
        in_specs=[rev(D), rev(D), rev(D), rev(D), rev(D), rev(PW), _ANY, _fixed((1, PW)), _ANY, _ANY, _ANY],
        out_specs=[rev(2 * D), rev(D), rev(D), rev(SBW), rev(PW), rev(PW), _fixed((1, PW))],
        out_shape=[_sds((S, 2 * D), BF16), _sds((S, D), BF16), _sds((S, D), BF16), _sds((S, SBW), BF16),
                   _sds((S, PW), BF16), _sds((S, PW), BF16), _sds((1, PW), F32)],
        scratch_shapes=[pltpu.VMEM((HALO, PW), F32)] + _vmem_like(w_group, w_bp, w_ba, w_out),
        compiler_params=_params(("arbitrary",), 48), exchange=exchange)


def _mix_bwd_in(dh, h, gain, pieces, w_in, exchange=None):
    tm = 512
    widths = [p.shape[1] for p in pieces]

    def body(dh_ref, h_ref, g_ref, *rest):
        piece_refs, (w_hbm, dx_ref, dg_ref, dp_ref, w_ref) = rest[:len(pieces)], rest[len(pieces):]
        _stage([(w_hbm, w_ref)])
        at = 0
        for ref, width in zip(piece_refs, widths):
            dp_ref[:, at:at + width] = ref[...]
            at += width
        du = jnp.zeros((tm, D), F32)
        for j in range(NSH):
            du = du + _nt(dp_ref[:, j * D:(j + 1) * D], w_ref[j])
        r, hr = _rms(h_ref[...])
        dx, dgain = _rms_bwd(du, hr, r, g_ref[...])
        dx_ref[...] = dh_ref[...] + dx

        @pl.when(pl.program_id(0) == 0)
        def _():
            dg_ref[...] = jnp.zeros_like(dg_ref)

        dg_ref[...] += dgain

    return _call(
        body, (dh, h, gain, *pieces, w_in), name="mix_bwd_in", grid=(S // tm,),
        in_specs=[_rows(tm, D), _rows(tm, D), _fixed((1, D))] + [_rows(tm, w) for w in widths] + [_ANY],
        out_specs=[_rows(tm, D), _fixed((1, D)), _rows(tm, 4 * D)],
        out_shape=[_sds((S, D), F32), _sds((1, D), F32), _sds((S, 4 * D), BF16)],
        scratch_shapes=_vmem_like(w_in),
        compiler_params=_params(("arbitrary",), 48), exchange=exchange)


def _wgrad(a, b, nblk, ti, name, out_dtype=BF16, exchange=None):
    ka, n = a.shape[1], b.shape[1]
    ns = n // nblk

    def body(a_ref, b_ref, o_ref):
        o_ref[...] = _tn(a_ref[...].astype(BF16), b_ref[...].astype(BF16)).astype(out_dtype)

    res = _call(
        body, (a, b), name=name, grid=(nblk, ka // ti),
        in_specs=[pl.BlockSpec((S, ti), lambda j, i: (0, i)), pl.BlockSpec((S, ns), lambda j, i: (0, j))],
        out_specs=[pl.BlockSpec((None, ti, ns), lambda j, i: (j, i, 0))],
        out_shape=[_sds((nblk, ka, ns), out_dtype)],
        compiler_params=_params(("arbitrary", "arbitrary"), 56), exchange=exchange)
    return res[0] if exchange is None else (res[0][0], res[1])


def _wgrad_groups(pm, dyg):
    def body(a_ref, b_ref, o_ref):
        o_ref[...] = _tn(a_ref[...], b_ref[...])

    col = pl.BlockSpec((S, PG), lambda g: (0, g))
    return pl.pallas_call(
        body, name="wgrad_groups", grid=(PW // PG,),
        in_specs=[col, col], out_specs=pl.BlockSpec((None, PG, PG), lambda g: (g, 0, 0)),
        out_shape=_sds((PW // PG, PG, PG), F32),
        compiler_params=_params(("arbitrary",), 32),
    )(pm, dyg)


def _place():
    x, y, c = lax.axis_index("x"), lax.axis_index("y"), lax.axis_index("c")
    chips = [(1 - x, y), (x, 1 - y), (1 - x, 1 - y)]
    return x, y, c, chips


def _remote(src, dst, ssem, rsem, dev):
    return pltpu.make_async_remote_copy(src_ref=src, dst_ref=dst, send_sem=ssem, recv_sem=rsem,
                                        device_id=dev, device_id_type=MESH)


def _cast_into_block(w, me_idx, name):
    rows, cols = w.shape
    tr = _row_block(rows)

    def body(me_ref, w_ref, o_ref):
        o_ref[...] = w_ref[...].astype(BF16)

    return pl.pallas_call(
        body, name=name, out_shape=_sds((NSH, rows, cols), BF16),
        grid_spec=pltpu.PrefetchScalarGridSpec(
            num_scalar_prefetch=1, grid=(rows // tr,),
            in_specs=[pl.BlockSpec((tr, cols), lambda r, me: (r, 0))],
            out_specs=pl.BlockSpec((None, tr, cols), lambda r, me: (me[0], r, 0))),
        compiler_params=_params(("arbitrary",), 32),
    )(me_idx, w)


def _ex_gather(bufs):
    n = len(bufs)

    def copies(outs, ssem, rsem, only_first=False):
        x, y, c, chips = _place()
        me, sib = 2 * x + y, (x, y, 1 - c)
        first, relay, last = [], [], []
        for w in range(n):
            half = outs[w].shape[1] // 2
            mine = outs[w].at[me, pl.ds(c * half, half)]
            for k, (px, py) in enumerate(chips):
                sems = (ssem.at[6 * w + k], rsem.at[6 * w + k])
                sib_sems = (ssem.at[6 * w + 3 + k], rsem.at[6 * w + 3 + k])
                first.append(_remote(mine, mine, *sems, (px, py, c)))
                if only_first:
                    continue
                got = outs[w].at[2 * px + py, pl.ds(c * half, half)]
                relay.append((_remote(got, got, *sems, (px, py, c)), _remote(got, got, *sib_sems, sib)))
                theirs = outs[w].at[2 * px + py, pl.ds((1 - c) * half, half)]
                last.append(_remote(theirs, theirs, *sib_sems, sib))
        return first, relay, last

    def start(ins, outs, ssem, rsem):
        for cp in copies(outs, ssem, rsem, only_first=True)[0]:
            cp.start()

    def finish(ins, outs, ssem, rsem):
        first, relay, last = copies(outs, ssem, rsem)
        for arrived, onward in relay:
            arrived.wait_recv()
            onward.start()
        for cp in last:
            cp.wait_recv()
        for cp in first:
            cp.wait_send()
        for _, onward in relay:
            onward.wait_send()

    return Exchange(bufs, [_sds(b.shape, b.dtype) for b in bufs], {w: w for w in range(n)}, 6 * n, start, finish)


def _simple_exchange(arrays, landing, aliases, make_copies):
    def start(ins, outs, ssem, rsem):
        for cp, _ in make_copies(ins, outs, ssem, rsem, False):
            cp.start()

    def finish(ins, outs, ssem, rsem):
        cps = make_copies(ins, outs, ssem, rsem, True)
        for _, landed in cps:
            landed.wait_recv()
        for cp, _ in cps:
            cp.wait_send()

    return Exchange(arrays, landing, aliases, len(arrays) * 3, start, finish)


def _ex_pair_swap(grads):
    def make(ins, outs, ssem, rsem, landing):
        x, y, c, _ = _place()
        cps = [_remote(ins[w].at[:, 1 - c], outs[w], ssem.at[w], rsem.at[w], (x, y, 1 - c))
               for w in range(len(grads))]
        return [(cp, cp) for cp in cps]

    return _simple_exchange(grads, [_sds((NSH,) + g.shape[2:], g.dtype) for g in grads], {}, make)


def _ex_scatter(parts):
    def make(ins, outs, ssem, rsem, landing):
        x, y, c, chips = _place()
        out = []
        for w in range(len(parts)):
            for k, (px, py) in enumerate(chips):
                sems = (ssem.at[3 * w + k], rsem.at[3 * w + k])
                out.append((_remote(ins[w].at[2 * px + py], outs[w].at[k], *sems, (px, py, c)),
                            _remote(outs[w].at[k], outs[w].at[k], *sems, (px, py, c)) if landing else None))
        return out

    return _simple_exchange(parts, [_sds((3,) + p.shape[1:], p.dtype) for p in parts], {}, make)


def _ex_share(bufs):
    def make(ins, outs, ssem, rsem, landing):
        x, y, c, _ = _place()
        sib = (x, y, 1 - c)
        return [(_remote(outs[w].at[c], outs[w].at[c], ssem.at[w], rsem.at[w], sib),
                 _remote(outs[w].at[1 - c], outs[w].at[1 - c], ssem.at[w], rsem.at[w], sib) if landing else None)
                for w in range(len(bufs))]

    return _simple_exchange(bufs, [_sds(b.shape, b.dtype) for b in bufs], {w: w for w in range(len(bufs))}, make)


def _gather_small(block):
    m_per, n = block.shape

    def body(x_ref, out_ref, ssem, rsem, lsem):
        x, y, c, chips = _place()
        me, sib = (x, y, c), (x, y, 1 - c)

        def rows(px, py, pc):
            return out_ref.at[pl.ds((4 * px + 2 * py + pc) * m_per, m_per), :]

        def copy(k, blk, to, src=None):
            return _remote(rows(*blk) if src is None else src, rows(*blk), ssem.at[k], rsem.at[k], to)

        mine = pltpu.make_async_copy(x_ref, rows(*me), lsem)
        mine.start()
        first = [copy(0, me, sib, src=x_ref)]
        first += [copy(1 + j, me, (*chip, c), src=x_ref) for j, chip in enumerate(chips)]
        for cp in first:
            cp.start()
        passed = [copy(4 + j, (*chip, c), sib) for j, chip in enumerate(chips)]
        for j, chip in enumerate(chips):
            copy(1 + j, (*chip, c), me).wait_recv()
            passed[j].start()
        copy(0, sib, me).wait_recv()
        for j, chip in enumerate(chips):
            copy(4 + j, (*chip, 1 - c), me).wait_recv()
        for cp in first + passed:
            cp.wait_send()
        mine.wait()

    return pl.pallas_call(
        body, name="gather_small", out_shape=jax.ShapeDtypeStruct((8 * m_per, n), block.dtype),
        in_specs=[_VM], out_specs=_VM,
        scratch_shapes=[pltpu.SemaphoreType.DMA((7,)), pltpu.SemaphoreType.DMA((7,)), pltpu.SemaphoreType.DMA],
    )(block)


def _row_block(rows):
    return max(t for t in range(16, 257, 16) if rows % t == 0)


def _pair_sum(grad, got, c_idx, name):
    _, _, half, cols = grad.shape
    tr = _row_block(half)

    def body(c_ref, a_ref, b_ref, o_ref):
        o_ref[...] = (a_ref[...].astype(F32) + b_ref[...].astype(F32)).astype(BF16)

    return pl.pallas_call(
        body, name=name, out_shape=_sds((NSH, half, cols), BF16),
        grid_spec=pltpu.PrefetchScalarGridSpec(
            num_scalar_prefetch=1, grid=(NSH, half // tr),
            in_specs=[pl.BlockSpec((None, None, tr, cols), lambda j, r, c: (j, c[0], r, 0)),
                      pl.BlockSpec((None, tr, cols), lambda j, r, c: (j, r, 0))],
            out_specs=pl.BlockSpec((None, tr, cols), lambda j, r, c: (j, r, 0))),
        compiler_params=_params(("arbitrary", "arbitrary"), 32),
    )(c_idx, grad, got)


def _chip_sum(own, got, place, name):
    _, half, cols = own.shape
    tr = _row_block(half)

    def body(place_ref, own_ref, got_ref, o_ref):
        acc = own_ref[...].astype(F32)
        for k in range(3):
            acc = acc + got_ref[k].astype(F32)
        o_ref[...] = acc

    return pl.pallas_call(
        body, name=name, out_shape=_sds((2, half, cols), F32),
        grid_spec=pltpu.PrefetchScalarGridSpec(
            num_scalar_prefetch=1, grid=(half // tr,),
            in_specs=[pl.BlockSpec((None, tr, cols), lambda r, p: (p[0], r, 0)),
                      pl.BlockSpec((3, tr, cols), lambda r, p: (0, r, 0))],
            out_specs=pl.BlockSpec((None, tr, cols), lambda r, p: (p[1], r, 0))),
        compiler_params=_params(("arbitrary",), 32),
    )(place, own, got)


def _adamw_math(w, g, m, v):
    m = B1 * m + (1.0 - B1) * g
    v = B2 * v + (1.0 - B2) * (g * g)
    m_hat = m / (1.0 - B1 ** STEP)
    v_hat = v / (1.0 - B2 ** STEP)
    return -LR * (m_hat / (jnp.sqrt(v_hat) + AEPS) + WD * w), m, v


def _adamw(w, g, m, v, name, exchange=None):
    rows, cols = w.shape
    tr = _row_block(rows)

    def body(w_ref, g_ref, m_ref, v_ref, go_ref, d_ref, nm_ref, nv_ref):
        g = g_ref[...]
        go_ref[...] = g
        d_ref[...], nm_ref[...], nv_ref[...] = _adamw_math(w_ref[...], g, m_ref[...], v_ref[...])

    blk = pl.BlockSpec((tr, cols), lambda r: (r, 0))
    return _call(
        body, (w, g, m, v), name=name, grid=(rows // tr,), out_shape=[_sds(w.shape, F32)] * 4,
        in_specs=[blk] * 4, out_specs=[blk] * 4,
        compiler_params=_params(("arbitrary",), 32), exchange=exchange)


def _small_update(gathered, w, m, v):
    rows = w.shape[0]

    def body(ga_ref, w_ref, m_ref, v_ref, g_ref, d_ref, nm_ref, nv_ref):
        g = ga_ref[0:rows, :]
        for dev in range(1, 8):
            g = g + ga_ref[dev * rows:(dev + 1) * rows, :]
        g_ref[...] = g
        d_ref[...], nm_ref[...], nv_ref[...] = _adamw_math(w_ref[...], g, m_ref[...], v_ref[...])

    return pl.pallas_call(
        body, name="small_update", out_shape=[jax.ShapeDtypeStruct(w.shape, F32)] * 4,
        in_specs=[_VM] * 4, out_specs=[_VM] * 4,
    )(gathered, w, m, v)


SMALL = ("ffn1_norm", "mix_norm", "ffn2_norm", "final_norm", "pool_scale", "pool_w_group", "loss")
BIG = ("ffn1_w_gate_up", "ffn1_w_down", "w_in", "w_branch_pool", "w_branch_attn", "w_out",
       "ffn2_w_gate_up", "ffn2_w_down")
ORDER = ("ffn1_norm", "ffn1_w_gate_up", "ffn1_w_down", "mix_norm", "w_in", "pool_w_group", "pool_scale",
         "w_branch_pool", "w_branch_attn", "w_out", "ffn2_norm", "ffn2_w_gate_up", "ffn2_w_down", "final_norm")
SMALL_ROWS = 560


def _pack_small(t):
    parts = []
    for k in SMALL:
        rows = t[k].reshape(-1, 128) if k in t else jnp.zeros((1, 128), F32)
        parts.append(jnp.pad(rows, ((0, -rows.shape[0] % 8), (0, 0))))
    packed = jnp.concatenate(parts, axis=0)
    assert packed.shape == (SMALL_ROWS, 128), packed.shape
    return packed


def _unpack_small(packed, like):
    out, at = {}, 0
    for k in SMALL:
        n = like[k].size // 128 if k in like else 1
        out[k] = packed[at:at + n].reshape(like[k].shape) if k in like else packed[at, 0]
        at += n + (-n % 8)
    return out


def _halves(g):
    return g.reshape(NSH, 2, g.shape[1] // 2, g.shape[2])


def kernel(x, ffn1_norm, ffn1_w_gate_up, ffn1_w_down, mix_norm, w_in, pool_w_group, pool_scale, w_branch_pool, w_branch_attn, w_out, ffn2_norm, ffn2_w_gate_up, ffn2_w_down, final_norm, loss_target, m_ffn1_norm, m_ffn1_w_gate_up, m_ffn1_w_down, m_mix_norm, m_w_in, m_pool_w_group, m_pool_scale, m_w_branch_pool, m_w_branch_attn, m_w_out, m_ffn2_norm, m_ffn2_w_gate_up, m_ffn2_w_down, m_final_norm, v_ffn1_norm, v_ffn1_w_gate_up, v_ffn1_w_down, v_mix_norm, v_w_in, v_pool_w_group, v_pool_scale, v_w_branch_pool, v_w_branch_attn, v_w_out, v_ffn2_norm, v_ffn2_w_gate_up, v_ffn2_w_down, v_final_norm):
    wts = dict(ffn1_norm=ffn1_norm, ffn1_w_gate_up=ffn1_w_gate_up, ffn1_w_down=ffn1_w_down, mix_norm=mix_norm,
               w_in=w_in, pool_w_group=pool_w_group, pool_scale=pool_scale, w_branch_pool=w_branch_pool,
               w_branch_attn=w_branch_attn, w_out=w_out, ffn2_norm=ffn2_norm, ffn2_w_gate_up=ffn2_w_gate_up,
               ffn2_w_down=ffn2_w_down, final_norm=final_norm)
    mom = dict(ffn1_norm=m_ffn1_norm, ffn1_w_gate_up=m_ffn1_w_gate_up, ffn1_w_down=m_ffn1_w_down,
               mix_norm=m_mix_norm, w_in=m_w_in, pool_w_group=m_pool_w_group, pool_scale=m_pool_scale,
               w_branch_pool=m_w_branch_pool, w_branch_attn=m_w_branch_attn, w_out=m_w_out,
               ffn2_norm=m_ffn2_norm, ffn2_w_gate_up=m_ffn2_w_gate_up, ffn2_w_down=m_ffn2_w_down,
               final_norm=m_final_norm)
    var = dict(ffn1_norm=v_ffn1_norm, ffn1_w_gate_up=v_ffn1_w_gate_up, ffn1_w_down=v_ffn1_w_down,
               mix_norm=v_mix_norm, w_in=v_w_in, pool_w_group=v_pool_w_group, pool_scale=v_pool_scale,
               w_branch_pool=v_w_branch_pool, w_branch_attn=v_w_branch_attn, w_out=v_w_out,
               ffn2_norm=v_ffn2_norm, ffn2_w_gate_up=v_ffn2_w_gate_up, ffn2_w_down=v_ffn2_w_down,
               final_norm=v_final_norm)

    c_idx = lax.axis_index("c").astype(jnp.int32).reshape(1)
    me_idx = (2 * lax.axis_index("x") + lax.axis_index("y")).astype(jnp.int32).reshape(1)
    place = jnp.concatenate([me_idx, c_idx])
    x0, tgt = x[0], loss_target[0]
    wgrp = pool_w_group[0].astype(BF16)
    g1, gm, g2, gf = ffn1_norm, mix_norm, ffn2_norm, final_norm.reshape(1, D)
    grad, delta, new_m, new_v = {}, {}, {}, {}

    def pair_sums(keys, parts, got):
        return [_pair_sum(parts[i], got[i], c_idx, "pair_sum_" + k) for i, k in enumerate(keys)]

    def chip_sums(keys, chip_parts, owned):
        return [_chip_sum(chip_parts[i], owned[i], place, "chip_sum_" + k) for i, k in enumerate(keys)]

    def adamw(k, exchange=None):
        res = _adamw(wts[k][0], grad[k][0], mom[k][0], var[k][0], "adamw_" + k, exchange=exchange)
        outs, landed = (res, None) if exchange is None else res
        grad[k], delta[k], new_m[k], new_v[k] = (o.reshape(wts[k].shape) for o in outs)
        return landed

    own = {k: _cast_into_block(wts[k][0], me_idx, "cast_" + k) for k in BIG}
    first, late = ("ffn1_w_gate_up", "ffn1_w_down"), ("w_branch_pool", "w_branch_attn", "w_out",
                                                       "ffn2_w_gate_up", "ffn2_w_down")
    full = dict(zip(first, _exchange_alone(_ex_gather([own[k] for k in first]), "gather_ffn1")))
    wgu1, wd1 = full["ffn1_w_gate_up"], full["ffn1_w_down"].reshape(DFF, D)
    (h1, n1, gu1, a1), (win,) = _ffn_fwd(x0, g1, wgu1, wd1, "ffn1_fwd", exchange=_ex_gather([own["w_in"]]))
    u, xp, q, k, v, gp, gs = _mix_in(h1, gm, win)
    (o_sb, ctot), landed = _attn_fwd(q, k, v, exchange=_ex_gather([own[k_] for k_ in late]))
    full.update(zip(late, landed))
    wbp, wba, wout = full["w_branch_pool"], full["w_branch_attn"], full["w_out"].reshape(D, D)
    wgu2, wd2 = full["ffn2_w_gate_up"], full["ffn2_w_down"].reshape(DFF, D)
    h2, pm, p, yp, ys, mm = _mix_out(h1, xp, o_sb, gp, gs, wgrp, pool_scale, wbp, wba, wout)
    h3, n3, gu3, a3 = _ffn_fwd(h2, g2, wgu2, wd2, "ffn2_fwd")
    dh3, loss_row, d_gf = _head(h3, tgt, gf)

    def grad_gate_up(n, dgu, name, exchange=None):
        res = _wgrad(n, dgu, NSH, 512, name, exchange=exchange)
        return [_halves(res)] if exchange is None else ([_halves(res[0])], res[1])

    def grad_down(a, dh, name, exchange=None):
        res = _wgrad(a, dh, 1, FFS, name, exchange=exchange)
        halves = lambda g: [_halves(g.reshape(NSH, DFF // NSH, D))]
        return halves(res) if exchange is None else (halves(res[0]), res[1])

    k_gu2, k_d2, k_gu1, k_d1, k_in = (("ffn2_w_gate_up",), ("ffn2_w_down",), ("ffn1_w_gate_up",),
                                      ("ffn1_w_down",), ("w_in",))
    dgu3 = _ffn_bwd_act(dh3, gu3, wd2, "ffn2_bwd_act")
    pa = grad_gate_up(n3, dgu3, "wgrad_gu2") + grad_down(a3, dh3, "wgrad_d2")
    (dh2, d_g2), got_a = _ffn_bwd_in(dh3, h2, g2, dgu3, wgu2, "ffn2_bwd_in", exchange=_ex_pair_swap(pa))
    chip_a = pair_sums(k_gu2 + k_d2, pa, got_a)
    dlg, dyp, dys, do_sb, dyg, dxp, d_scale = _mix_bwd_out(dh2, gp, gs, yp, ys, pm, wgrp, pool_scale, wbp, wba, wout)
    kb = ("w_out", "w_branch_pool", "w_branch_attn")
    pb = [_halves(_wgrad(mm, dh2, 1, 512, "wgrad_out").reshape(NSH, D // NSH, D)),
          _halves(_wgrad(p, dyp, NSH, PW, "wgrad_bp")), _halves(_wgrad(o_sb, dys, NSH, SBW, "wgrad_ba"))]
    chip_b = pair_sums(kb, pb, _exchange_alone(_ex_pair_swap(pb), "pair_swap_mix"))
    k_ab = k_gu2 + k_d2 + kb
    (dq, dk, dv), owned_ab = _attn_bwd(q, k, v, do_sb, ctot, exchange=_ex_scatter(chip_a + chip_b))
    halves_ab = chip_sums(k_ab, chip_a + chip_b, owned_ab)
    (dh1, d_gm, dproj), both_ab = _mix_bwd_in(dh2, h1, gm, (dxp, dq, dk, dv, dlg), win, exchange=_ex_share(halves_ab))
    for i, k_ in enumerate(k_ab):
        grad[k_] = both_ab[i].reshape(wts[k_].shape)

    p_in = [_halves(_wgrad(u, dproj, NSH, 512, "wgrad_in"))]
    p_d1, got_in = grad_down(a1, dh1, "wgrad_d1", exchange=_ex_pair_swap(p_in))
    chip_in = pair_sums(k_in, p_in, got_in)
    dgu1, landed = _ffn_bwd_act(dh1, gu1, wd1, "ffn1_bwd_act",
                                exchange=_join(_ex_scatter(chip_in), _ex_pair_swap(p_d1)))
    owned_in, got_d1 = landed[:1], landed[1:]
    chip_d1 = pair_sums(k_d1, p_d1, got_d1)
    halves_in = chip_sums(k_in, chip_in, owned_in)
    p_gu1, landed = grad_gate_up(n1, dgu1, "wgrad_gu1", exchange=_join(_ex_scatter(chip_d1), _ex_share(halves_in)))
    owned_d1, both_in = landed[:1], landed[1:]
    grad["w_in"] = both_in[0].reshape(w_in.shape)
    halves_d1 = chip_sums(k_d1, chip_d1, owned_d1)
    chip_gu1 = pair_sums(k_gu1, p_gu1, _exchange_alone(_ex_pair_swap(p_gu1), "pair_swap_gu1"))
    (dx, d_g1), landed = _ffn_bwd_in(dh1, x0, g1, dgu1, wgu1, "ffn1_bwd_in",
                                     exchange=_join(_ex_scatter(chip_gu1), _ex_share(halves_d1)))
    owned_gu1, both_d1 = landed[:1], landed[1:]
    grad["ffn1_w_down"] = both_d1[0].reshape(ffn1_w_down.shape)
    both_gu1 = _exchange_alone(_ex_share(chip_sums(k_gu1, chip_gu1, owned_gu1)), "share_gu1")
    grad["ffn1_w_gate_up"] = both_gu1[0].reshape(ffn1_w_gate_up.shape)
    for k_ in BIG:
        adamw(k_)

    small_g = dict(ffn1_norm=d_g1, mix_norm=d_gm, ffn2_norm=d_g2, final_norm=d_gf, pool_scale=d_scale,
                   pool_w_group=_wgrad_groups(pm, dyg), loss=loss_row)
    gathered = _gather_small(_pack_small(small_g))
    sg, sd, sm, sv = _small_update(gathered, _pack_small(wts), _pack_small(mom), _pack_small(var))
    sums = _unpack_small(sg, wts)
    loss = sums.pop("loss")
    grad.update(sums)
    for dst, packed in ((delta, sd), (new_m, sm), (new_v, sv)):
        vals = _unpack_small(packed, wts)
        vals.pop("loss")
        dst.update(vals)
    return (loss, dx[None], *[grad[k_] for k_ in ORDER], *[delta[k_] for k_ in ORDER],
            *[new_m[k_] for k_ in ORDER], *[new_v[k_] for k_ in ORDER])
```

```python
import functools

import jax
import jax.numpy as jnp
from jax import lax
from jax.experimental import pallas as pl
from jax.experimental.pallas import tpu as pltpu

F32 = jnp.float32
BF16 = jnp.bfloat16

S = 2048
D = 1024
DFF = 2816
FFS = 2 * DFF // 4
NSH = 4
PW = 512
PG = 128
POOL_WINDOWS = (2, 4, 8, 16)
HALO = 16
SBW = 512
DH = 64
EPS = 1e-6
SCALE = 0.125
LOG2E = 1.4426950408889634
TA = 256
QB = 2
MIB = 1024 * 1024

LR, B1, B2, AEPS, WD, STEP = 0.001, 0.9, 0.999, 1e-08, 0.01, 10

_VM = pl.BlockSpec(memory_space=pltpu.VMEM)
_ANY = pl.BlockSpec(memory_space=pl.ANY)
MESH = pl.DeviceIdType.MESH


def _nn(a, b):
    return jnp.dot(a, b, preferred_element_type=F32)


def _nt(a, b):
    return lax.dot_general(a, b, (((1,), (1,)), ((), ())), preferred_element_type=F32)


def _tn(a, b):
    return lax.dot_general(a, b, (((0,), (0,)), ((), ())), preferred_element_type=F32)


def _params(sem, vmem_mib):
    return pltpu.CompilerParams(dimension_semantics=sem, vmem_limit_bytes=vmem_mib * MIB)


def _rows(tm, width):
    return pl.BlockSpec((tm, width), lambda i: (i, 0))


def _fixed(shape):
    return pl.BlockSpec(shape, lambda *_: (0,) * len(shape))


def _sds(shape, dtype):
    return pltpu.HBM(shape, dtype)


def _in_hbm(args):
    return [pltpu.with_memory_space_constraint(a, pltpu.HBM) for a in args]


def _stage(pairs):
    @pl.when(pl.program_id(0) == 0)
    def _():
        for src, dst in pairs:
            pltpu.sync_copy(src, dst)


def _vmem_like(*arrays):
    return [pltpu.VMEM(a.shape, a.dtype) for a in arrays]


class Exchange:
    def __init__(self, arrays, landing, aliases, n_sems, start, finish):
        self.arrays, self.landing, self.aliases, self.n_sems = list(arrays), list(landing), dict(aliases), n_sems
        self.start, self.finish = start, finish


def _join(a, b):
    na, la = len(a.arrays), len(a.landing)

    def both(fa, fb):
        def run(ins, outs, ssem, rsem):
            fa(ins[:na], outs[:la], ssem.at[pl.ds(0, a.n_sems)], rsem.at[pl.ds(0, a.n_sems)])
            fb(ins[na:], outs[la:], ssem.at[pl.ds(a.n_sems, b.n_sems)], rsem.at[pl.ds(a.n_sems, b.n_sems)])
        return run

    aliases = {**a.aliases, **{na + i: la + j for i, j in b.aliases.items()}}
    return Exchange(a.arrays + b.arrays, a.landing + b.landing, aliases, a.n_sems + b.n_sems,
                    both(a.start, b.start), both(a.finish, b.finish))


def _call(body, args, *, name, grid, in_specs, out_specs, out_shape, scratch_shapes=(), compiler_params=None,
          exchange=None):
    if exchange is None:
        return pl.pallas_call(body, name=name, grid=grid, in_specs=in_specs, out_specs=out_specs,
                              out_shape=out_shape, scratch_shapes=list(scratch_shapes),
                              compiler_params=compiler_params)(*args)
    ex = exchange
    n_in, n_out, n_scr = len(in_specs), len(out_specs), len(scratch_shapes)
    na, nl = len(ex.arrays), len(ex.landing)

    def hosted(*refs):
        at = [0]

        def take(n):
            at[0] += n
            return refs[at[0] - n:at[0]]

        k_in, e_in, k_out, e_out, k_scr = take(n_in), take(na), take(n_out), take(nl), take(n_scr)
        ssem, rsem = take(2)
        ids = [pl.program_id(a) for a in range(len(grid))]
        first = functools.reduce(jnp.logical_and, [i == 0 for i in ids])
        last = functools.reduce(jnp.logical_and, [i == g - 1 for i, g in zip(ids, grid)])

        @pl.when(first)
        def _():
            ex.start(e_in, e_out, ssem, rsem)

        body(*k_in, *k_out, *k_scr)

        @pl.when(last)
        def _():
            ex.finish(e_in, e_out, ssem, rsem)

    outs = pl.pallas_call(
        hosted, name=name, grid=grid,
        in_specs=list(in_specs) + [_ANY] * na, out_specs=list(out_specs) + [_ANY] * nl,
        out_shape=list(out_shape) + ex.landing,
        scratch_shapes=list(scratch_shapes) + [pltpu.SemaphoreType.DMA((ex.n_sems,))] * 2,
        input_output_aliases={n_in + i: n_out + j for i, j in ex.aliases.items()},
        compiler_params=compiler_params,
    )(*args, *_in_hbm(ex.arrays))
    return outs[:n_out], outs[n_out:]


def _exchange_alone(ex, name):
    def body(*refs):
        na, nl = len(ex.arrays), len(ex.landing)
        ex.start(refs[:na], refs[na:na + nl], refs[-2], refs[-1])
        ex.finish(refs[:na], refs[na:na + nl], refs[-2], refs[-1])

    return pl.pallas_call(
        body, name=name, in_specs=[_ANY] * len(ex.arrays), out_specs=[_ANY] * len(ex.landing),
        out_shape=ex.landing, scratch_shapes=[pltpu.SemaphoreType.DMA((ex.n_sems,))] * 2,
        input_output_aliases=ex.aliases,
    )(*_in_hbm(ex.arrays))


def _rms(x):
    r = lax.rsqrt(jnp.mean(x * x, axis=-1, keepdims=True) + EPS)
    return r, x * r


def _rms_bwd(dn, xr, r, gain):
    dng = dn * gain
    dx = r * (dng - xr * jnp.mean(dng * xr, axis=-1, keepdims=True))
    return dx, jnp.sum(dn * xr, axis=0, keepdims=True)


def _ffn_fwd(x, gain, wgu, wd, name, exchange=None):
    tm = 256

    def body(x_ref, g_ref, wgu_hbm, wd_hbm, h_ref, n_ref, gu_ref, a_ref, wgu_ref, wd_ref):
        _stage([(wgu_hbm, wgu_ref), (wd_hbm, wd_ref)])
        x = x_ref[...]
        _, xr = _rms(x)
        n = (xr * g_ref[...]).astype(BF16)
        n_ref[...] = n
        acc = jnp.zeros((tm, D), F32)
        for j in range(2):
            g = _nn(n, wgu_ref[j])
            u = _nn(n, wgu_ref[2 + j])
            gu_ref[:, j * FFS:(j + 1) * FFS] = g.astype(BF16)
            gu_ref[:, (2 + j) * FFS:(3 + j) * FFS] = u.astype(BF16)
            half_act = (0.5 * (g * jax.nn.sigmoid(g) * u)).astype(BF16)
            a_ref[:, j * FFS:(j + 1) * FFS] = half_act
            acc = acc + _nn(half_act, wd_ref[j * FFS:(j + 1) * FFS, :])
        h_ref[...] = x + acc

    return _call(
        body, (x, gain, wgu, wd), name=name, grid=(S // tm,),
        in_specs=[_rows(tm, D), _fixed((1, D)), _ANY, _ANY],
        out_specs=[_rows(tm, D), _rows(tm, D), _rows(tm, 4 * FFS), _rows(tm, DFF)],
        out_shape=[_sds((S, D), F32), _sds((S, D), BF16), _sds((S, 4 * FFS), BF16), _sds((S, DFF), BF16)],
        scratch_shapes=_vmem_like(wgu, wd),
        compiler_params=_params(("arbitrary",), 56), exchange=exchange)


def _ffn_bwd_act(dh, gu, wd, name, exchange=None):
    tm = 512

    def body(dh_ref, gu_ref, wd_hbm, dgu_ref, wd_ref):
        _stage([(wd_hbm, wd_ref)])
        dhb = dh_ref[...].astype(BF16)
        for j in range(2):
            g = gu_ref[:, j * FFS:(j + 1) * FFS].astype(F32)
            u = gu_ref[:, (2 + j) * FFS:(3 + j) * FFS].astype(F32)
            da = 0.5 * _nt(dhb, wd_ref[j * FFS:(j + 1) * FFS, :])
            sg = jax.nn.sigmoid(g)
            dgu_ref[:, j * FFS:(j + 1) * FFS] = (da * u * (sg * (1.0 + g * (1.0 - sg)))).astype(BF16)
            dgu_ref[:, (2 + j) * FFS:(3 + j) * FFS] = (da * (g * sg)).astype(BF16)

    res = _call(
        body, (dh, gu, wd), name=name, grid=(S // tm,),
        in_specs=[_rows(tm, D), _rows(tm, 4 * FFS), _ANY], out_specs=[_rows(tm, 4 * FFS)],
        out_shape=[_sds((S, 4 * FFS), BF16)], scratch_shapes=_vmem_like(wd),
        compiler_params=_params(("arbitrary",), 56), exchange=exchange)
    return res[0] if exchange is None else (res[0][0], res[1])


def _ffn_bwd_in(dh, x, gain, dgu, wgu, name, exchange=None):
    tm = 512

    def body(dh_ref, x_ref, g_ref, dgu_ref, wgu_hbm, dx_ref, dg_ref, wgu_ref):
        _stage([(wgu_hbm, wgu_ref)])
        dn = jnp.zeros((tm, D), F32)
        for j in range(NSH):
            dn = dn + _nt(dgu_ref[:, j * FFS:(j + 1) * FFS], wgu_ref[j])
        r, xr = _rms(x_ref[...])
        dx, dgain = _rms_bwd(dn, xr, r, g_ref[...])
        dx_ref[...] = dh_ref[...] + dx

        @pl.when(pl.program_id(0) == 0)
        def _():
            dg_ref[...] = jnp.zeros_like(dg_ref)

        dg_ref[...] += dgain

    return _call(
        body, (dh, x, gain, dgu, wgu), name=name, grid=(S // tm,),
        in_specs=[_rows(tm, D), _rows(tm, D), _fixed((1, D)), _rows(tm, 4 * FFS), _ANY],
        out_specs=[_rows(tm, D), _fixed((1, D))],
        out_shape=[_sds((S, D), F32), _sds((1, D), F32)], scratch_shapes=_vmem_like(wgu),
        compiler_params=_params(("arbitrary",), 56), exchange=exchange)


def _head(h, target, gain):
    tm = 512

    def body(h_ref, t_ref, g_ref, dh_ref, loss_ref, dg_ref):
        gain = g_ref[...]
        r, hr = _rms(h_ref[...])
        err = hr * gain - t_ref[...]
        dy = err * (1.0 / D)
        dh, dgain = _rms_bwd(dy, hr, r, gain)
        dh_ref[...] = dh

        @pl.when(pl.program_id(0) == 0)
        def _():
            dg_ref[...] = jnp.zeros_like(dg_ref)
            loss_ref[...] = jnp.zeros_like(loss_ref)

        dg_ref[...] += dgain
        loss_ref[...] += jnp.full((1, 128), (0.5 / D) * jnp.sum(err * err), F32)

    return pl.pallas_call(
        body, name="head", grid=(S // tm,),
        in_specs=[_rows(tm, D), _rows(tm, D), _fixed((1, D))],
        out_specs=[_rows(tm, D), _fixed((1, 128)), _fixed((1, D))],
        out_shape=[_sds((S, D), F32), _sds((1, 128), F32), _sds((1, D), F32)],
        compiler_params=_params(("arbitrary",), 40),
    )(h, target, gain)


def _mix_in(h, gain, w_in):
    tm = 512

    def body(h_ref, g_ref, w_hbm, u_ref, xp_ref, q_ref, k_ref, v_ref, gp_ref, gs_ref, w_ref):
        _stage([(w_hbm, w_ref)])
        _, hr = _rms(h_ref[...])
        u = (hr * g_ref[...]).astype(BF16)
        u_ref[...] = u
        p0 = _nn(u, w_ref[0])
        xp_ref[...] = p0[:, :PW]
        q_ref[...] = p0[:, PW:].astype(BF16)
        p1 = _nn(u, w_ref[1])
        k_ref[...] = p1[:, :SBW].astype(BF16)
        v_ref[...] = p1[:, SBW:].astype(BF16)
        gp_ref[...] = jax.nn.sigmoid(_nn(u, w_ref[2])).astype(BF16)
        gs_ref[...] = jax.nn.sigmoid(_nn(u, w_ref[3])).astype(BF16)

    return pl.pallas_call(
        body, name="mix_in", grid=(S // tm,),
        in_specs=[_rows(tm, D), _fixed((1, D)), _ANY],
        out_specs=[_rows(tm, D), _rows(tm, PW), _rows(tm, SBW), _rows(tm, SBW), _rows(tm, SBW),
                   _rows(tm, D), _rows(tm, D)],
        out_shape=[_sds((S, D), BF16), _sds((S, PW), F32), _sds((S, SBW), BF16), _sds((S, SBW), BF16),
                   _sds((S, SBW), BF16), _sds((S, D), BF16), _sds((S, D), BF16)],
        scratch_shapes=_vmem_like(w_in),
        compiler_params=_params(("arbitrary",), 48),
    )(h, gain, w_in)


def _hilo_dot(x, tri):
    hi = x.astype(BF16)
    lo = (x - hi.astype(F32)).astype(BF16)
    return _nn(hi, tri) + _nn(lo, tri)


def _log_terms(qk):
    z2 = qk * (SCALE * LOG2E)
    lb = jnp.minimum(z2, 0.0) - jnp.log2(1.0 + jnp.exp2(-jnp.abs(z2)))
    return lb, lb - z2


def _head_masks():
    lane = lax.broadcasted_iota(jnp.int32, (1, 2 * DH), 1)
    return (lane < DH, lane >= DH)


def _attn_fwd(q, k, v, exchange=None):
    T = TA

    def body(q_ref, k_ref, v_ref, o_ref, c_ref):
        i2 = 2 * pl.program_id(1)
        row = lax.broadcasted_iota(jnp.int32, (T, T), 0)
        col = lax.broadcasted_iota(jnp.int32, (T, T), 1)
        after = (row > col).astype(BF16)
        causal = col < row
        masks = _head_masks()
        qms = {}
        for b in range(QB):
            q2 = q_ref[b * T:(b + 1) * T, :]
            for h, hm in enumerate(masks):
                qms[b, h] = jnp.where(hm, q2, jnp.zeros_like(q2))

        def blocks(keys, pairs, carries, os):
            ks, vms = [], []
            for j in keys:
                rows = pl.ds(pl.multiple_of(j * T, T), T)
                vj = v_ref[rows, :]
                ks.append(k_ref[rows, :])
                vms.append([jnp.where(hm, vj, jnp.zeros_like(vj)) for hm in masks])
            units = [(n, h) for n in range(len(pairs)) for h in range(2)]
            qks = {(n, h): _nt(qms[pairs[n][0], h], ks[pairs[n][1]]) for n, h in units}
            lbs, l1ms = {}, {}
            for u in units:
                lbs[u], l1m = _log_terms(qks[u])
                l1ms[u] = jnp.where(causal, l1m, 0.0) if pairs[u[0]][2] else l1m
            cins = {u: _hilo_dot(l1ms[u], after) for u in units}
            carries, os = dict(carries), list(os)
            for n, h in units:
                b, key, diag = pairs[n]
                a = jnp.exp2(lbs[n, h] + cins[n, h] + carries[b, h])
                if diag:
                    a = jnp.where(causal, a, 0.0)
                os[b] = os[b] + _nn(a.astype(BF16), vms[key][h])
                carries[b, h] = carries[b, h] + jnp.sum(l1ms[n, h], axis=1, keepdims=True)
            return carries, tuple(os)

        carries = {(b, h): jnp.zeros((T, 1), F32) for b in range(QB) for h in range(2)}
        os = tuple(jnp.zeros((T, 2 * DH), F32) for _ in range(QB))
        carries, os = blocks([i2 + 1, i2], [(1, 0, True), (0, 1, True), (1, 1, False)], carries, os)
        carries, os = lax.fori_loop(
            0, i2, lambda jj, c: blocks([i2 - 1 - jj], [(0, 0, False), (1, 0, False)], c[0], c[1]), (carries, os))
        for b in range(QB):
            o_ref[b * T:(b + 1) * T, :] = os[b].astype(BF16)
            c_ref[b * T:(b + 1) * T, :] = jnp.where(masks[0], carries[b, 0], carries[b, 1])

    blk = pl.BlockSpec((QB * T, 2 * DH), lambda p, i: (i, p))
    full = pl.BlockSpec((S, 2 * DH), lambda p, i: (0, p))
    return _call(
        body, (q, k, v), name="attn_fwd", grid=(SBW // (2 * DH), S // (QB * T)),
        in_specs=[blk, full, full], out_specs=[blk, blk],
        out_shape=[_sds((S, SBW), BF16), _sds((S, SBW), F32)],
        compiler_params=_params(("arbitrary", "arbitrary"), 40), exchange=exchange)


def _attn_bwd(q, k, v, do, ctot, exchange=None):
    T = TA
    nq = S // (QB * T)

    def body(q_ref, k_ref, v_ref, do_ref, c_ref, dq_ref, dk_ref, dv_ref, dk_acc, dv_acc):
        step = pl.program_id(1)
        i2 = 2 * step

        @pl.when(step == 0)
        def _():
            dk_acc[...] = jnp.zeros_like(dk_acc)
            dv_acc[...] = jnp.zeros_like(dv_acc)

        row = lax.broadcasted_iota(jnp.int32, (T, T), 0)
        col = lax.broadcasted_iota(jnp.int32, (T, T), 1)
        upto = (row <= col).astype(BF16)
        before = (row < col).astype(BF16)
        causal = col < row
        masks = _head_masks()
        qms, doms, ctots = {}, {}, {}
        for b in range(QB):
            q2, do2 = q_ref[b * T:(b + 1) * T, :], do_ref[b * T:(b + 1) * T, :]
            for h, hm in enumerate(masks):
                qms[b, h] = jnp.where(hm, q2, jnp.zeros_like(q2))
                doms[b, h] = jnp.where(hm, do2, jnp.zeros_like(do2))
                ctots[b, h] = c_ref[b * T:(b + 1) * T, h * DH:h * DH + 1]

        def blocks(keys, pairs, sums, dqs):
            rows = [pl.ds(pl.multiple_of(j * T, T), T) for j in keys]
            ks, vs = [k_ref[r, :] for r in rows], [v_ref[r, :] for r in rows]
            kms = [[jnp.where(hm, kj, jnp.zeros_like(kj)) for hm in masks] for kj in ks]
            units = [(n, h) for n in range(len(pairs)) for h in range(2)]
            qks = {(n, h): _nt(qms[pairs[n][0], h], ks[pairs[n][1]]) for n, h in units}
            das = {(n, h): _nt(doms[pairs[n][0], h], vs[pairs[n][1]]) for n, h in units}
            lbs, l1ms = {}, {}
            for u in units:
                lbs[u], l1m = _log_terms(qks[u])
                l1ms[u] = jnp.where(causal, l1m, 0.0) if pairs[u[0]][2] else l1m
            pins = {u: _hilo_dot(l1ms[u], upto) for u in units}
            sums = dict(sums)
            a_s, dls, cps = {}, {}, {}
            for n, h in units:
                b, _, diag = pairs[n]
                cl, cp = sums[b, h]
                a = jnp.exp2(lbs[n, h] + (ctots[b, h] - cl) - pins[n, h])
                if diag:
                    a = jnp.where(causal, a, 0.0)
                a_s[n, h] = a.astype(BF16)
                dls[n, h] = das[n, h] * a
                cps[n, h] = cp
                sums[b, h] = (cl + jnp.sum(l1ms[n, h], axis=1, keepdims=True),
                              cp + jnp.sum(dls[n, h], axis=1, keepdims=True))
            pexs = {u: _hilo_dot(dls[u], before) for u in units}
            dzbs = {}
            for u in units:
                dz = dls[u] - jnp.exp2(lbs[u]) * (dls[u] + pexs[u] + cps[u])
                if pairs[u[0]][2]:
                    dz = jnp.where(causal, dz, 0.0)
                dzbs[u] = dz.astype(BF16)
            dqs = list(dqs)
            for n, h in units:
                dqs[pairs[n][0]] = dqs[pairs[n][0]] + _nn(dzbs[n, h], kms[pairs[n][1]][h])
            for key, r in enumerate(rows):
                mine = [(n, h) for n, h in units if pairs[n][1] == key]
                dk_acc[r, :] += functools.reduce(jnp.add, [_tn(dzbs[u], qms[pairs[u[0]][0], u[1]]) for u in mine])
                dv_acc[r, :] += functools.reduce(jnp.add, [_tn(a_s[u], doms[pairs[u[0]][0], u[1]]) for u in mine])
            return sums, tuple(dqs)

        zero = jnp.zeros((T, 1), F32)
        sums = {(b, h): (zero, zero) for b in range(QB) for h in range(2)}
        dqs = tuple(jnp.zeros((T, 2 * DH), F32) for _ in range(QB))
        sums, dqs = lax.fori_loop(
            0, i2, lambda j, c: blocks([j], [(0, 0, False), (1, 0, False)], c[0], c[1]), (sums, dqs))
        _, dqs = blocks([i2, i2 + 1], [(0, 0, True), (1, 0, False), (1, 1, True)], sums, dqs)
        for b in range(QB):
            dq_ref[b * T:(b + 1) * T, :] = (dqs[b] * SCALE).astype(BF16)

        @pl.when(step == nq - 1)
        def _():
            dk_ref[...] = (dk_acc[...] * SCALE).astype(BF16)
            dv_ref[...] = dv_acc[...].astype(BF16)

    blk = pl.BlockSpec((QB * T, 2 * DH), lambda p, i: (i, p))
    full = pl.BlockSpec((S, 2 * DH), lambda p, i: (0, p))
    return _call(
        body, (q, k, v, do, ctot), name="attn_bwd", grid=(SBW // (2 * DH), nq),
        in_specs=[blk, full, full, blk, blk], out_specs=[blk, full, full],
        out_shape=[_sds((S, SBW), BF16), _sds((S, SBW), BF16), _sds((S, SBW), BF16)],
        scratch_shapes=[pltpu.VMEM((S, 2 * DH), F32), pltpu.VMEM((S, 2 * DH), F32)],
        compiler_params=_params(("arbitrary", "arbitrary"), 40), exchange=exchange)


def _pool_counts(first_row, tm):
    pos = first_row + lax.broadcasted_iota(jnp.int32, (tm, 1), 0)
    return [jnp.minimum(pos + 1, w).astype(F32) for w in POOL_WINDOWS]


def _mix_out(h, xp, o_sb, gp, gs, w_group, scale, w_bp, w_ba, w_out):
    tm = 512

    def body(h_ref, xp_ref, o_ref, gp_ref, gs_ref, wg_hbm, sc_ref, wbp_hbm, wba_hbm, wo_hbm,
             h2_ref, pm_ref, p_ref, yp_ref, ys_ref, m_ref, halo, wg_ref, wbp_ref, wba_ref, wo_ref):
        _stage([(wg_hbm, wg_ref), (wbp_hbm, wbp_ref), (wba_hbm, wba_ref), (wo_hbm, wo_ref)])
        i = pl.program_id(0)

        @pl.when(i == 0)
        def _():
            halo[...] = jnp.zeros_like(halo)

        xp = xp_ref[...]
        ext = jnp.concatenate([halo[...], xp], axis=0)
        halo[...] = xp[tm - HALO:, :]
        counts = _pool_counts(i * tm, tm)
        for gi in range(len(POOL_WINDOWS)):
            lanes = slice(gi * PG, (gi + 1) * PG)
            win = ext[:, lanes]
            for step in range(gi + 1):
                win = win + pltpu.roll(win, 1 << step, 0)
            pm = (win[HALO:, :] / counts[gi] - xp[:, lanes]).astype(BF16)
            pm_ref[:, lanes] = pm
            p_ref[:, lanes] = (_nn(pm, wg_ref[gi]) * sc_ref[:, lanes]).astype(BF16)
        pb = p_ref[...]
        ob = o_ref[...]
        for j in range(NSH):
            cols = slice(j * (D // NSH), (j + 1) * (D // NSH))
            yp = _nn(pb, wbp_ref[j])
            ys = _nn(ob, wba_ref[j])
            yp_ref[:, cols] = yp.astype(BF16)
            ys_ref[:, cols] = ys.astype(BF16)
            m_ref[:, cols] = (gp_ref[:, cols].astype(F32) * yp + gs_ref[:, cols].astype(F32) * ys).astype(BF16)
        h2_ref[...] = h_ref[...] + _nn(m_ref[...], wo_ref[...])

    return pl.pallas_call(
        body, name="mix_out", grid=(S // tm,),
        in_specs=[_rows(tm, D), _rows(tm, PW), _rows(tm, SBW), _rows(tm, D), _rows(tm, D),
                  _ANY, _fixed((1, PW)), _ANY, _ANY, _ANY],
        out_specs=[_rows(tm, D), _rows(tm, PW), _rows(tm, PW), _rows(tm, D), _rows(tm, D), _rows(tm, D)],
        out_shape=[_sds((S, D), F32), _sds((S, PW), BF16), _sds((S, PW), BF16), _sds((S, D), BF16),
                   _sds((S, D), BF16), _sds((S, D), BF16)],
        scratch_shapes=[pltpu.VMEM((HALO, PW), F32)] + _vmem_like(w_group, w_bp, w_ba, w_out),
        compiler_params=_params(("arbitrary",), 48),
    )(h, xp, o_sb, gp, gs, w_group, scale, w_bp, w_ba, w_out)


def _mix_bwd_out(dh, gp, gs, yp, ys, pm, w_group, scale, w_bp, w_ba, w_out, exchange=None):
    tm = 512
    nt = S // tm

    def body(dh_ref, gp_ref, gs_ref, yp_ref, ys_ref, pm_ref, wg_hbm, sc_ref, wbp_hbm, wba_hbm, wo_hbm,
             dlg_ref, dyp_ref, dys_ref, do_ref, dyg_ref, dxp_ref, dsc_ref, halo, wg_ref, wbp_ref, wba_ref, wo_ref):
        _stage([(wg_hbm, wg_ref), (wbp_hbm, wbp_ref), (wba_hbm, wba_ref), (wo_hbm, wo_ref)])
        step = pl.program_id(0)

        @pl.when(step == 0)
        def _():
            halo[...] = jnp.zeros_like(halo)
            dsc_ref[...] = jnp.zeros_like(dsc_ref)

        dm = _nt(dh_ref[...].astype(BF16), wo_ref[...])
        gp = gp_ref[...].astype(F32)
        gs = gs_ref[...].astype(F32)
        yp = yp_ref[...].astype(F32)
        ys = ys_ref[...].astype(F32)
        dlg_ref[:, :D] = (dm * yp * gp * (1.0 - gp)).astype(BF16)
        dlg_ref[:, D:] = (dm * ys * gs * (1.0 - gs)).astype(BF16)
        dyp_ref[...] = (dm * gp).astype(BF16)
        dys_ref[...] = (dm * gs).astype(BF16)
        dp = jnp.zeros((tm, PW), F32)
        do = jnp.zeros((tm, SBW), F32)
        for j in range(NSH):
            cols = slice(j * (D // NSH), (j + 1) * (D // NSH))
            dp = dp + _nt(dyp_ref[:, cols], wbp_ref[j])
            do = do + _nt(dys_ref[:, cols], wba_ref[j])
        do_ref[...] = do.astype(BF16)
        counts = _pool_counts((nt - 1 - step) * tm, tm)
        dscale = []
        for gi in range(len(POOL_WINDOWS)):
            lanes = slice(gi * PG, (gi + 1) * PG)
            dpg = dp[:, lanes]
            dscale.append(jnp.sum(dpg * _nn(pm_ref[:, lanes], wg_ref[gi]), axis=0, keepdims=True))
            dyg = (dpg * sc_ref[:, lanes]).astype(BF16)
            dyg_ref[:, lanes] = dyg
            dpm = _nt(dyg, wg_ref[gi])
            per = dpm / counts[gi]
            win = jnp.concatenate([per, halo[:, lanes]], axis=0)
            halo[:, lanes] = per[:HALO, :]
            for s in range(gi + 1):
                win = win + pltpu.roll(win, tm + HALO - (1 << s), 0)
            dxp_ref[:, lanes] = (win[:tm, :] - dpm).astype(BF16)
        dsc_ref[...] += jnp.concatenate(dscale, axis=1)

    rev = lambda width: pl.BlockSpec((tm, width), lambda i: (nt - 1 - i, 0))
    return _call(
        body, (dh, gp, gs, yp, ys, pm, w_group, scale, w_bp, w_ba, w_out), name="mix_bwd_out", grid=(nt,),
        in_specs=[rev(D), rev(D), rev(D), rev(D), rev(D), rev(PW), _ANY, _fixed((1, PW)), _ANY, _ANY, _ANY],
        out_specs=[rev(2 * D), rev(D), rev(D), rev(SBW), rev(PW), rev(PW), _fixed((1, PW))],
        out_shape=[_sds((S, 2 * D), BF16), _sds((S, D), BF16), _sds((S, D), BF16), _sds((S, SBW), BF16),
                   _sds((S, PW), BF16), _sds((S, PW), BF16), _sds((1, PW), F32)],
        scratch_shapes=[pltpu.VMEM((HALO, PW), F32)] + _vmem_like(w_group, w_bp, w_ba, w_out),
        compiler_params=_params(("arbitrary",), 48), exchange=exchange)


def _mix_bwd_in(dh, h, gain, pieces, w_in, exchange=None):
    tm = 512
    widths = [p.shape[1] for p in pieces]

    def body(dh_ref, h_ref, g_ref, *rest):
        piece_refs, (w_hbm, dx_ref, dg_ref, dp_ref, w_ref) = rest[:len(pieces)], rest[len(pieces):]
        _stage([(w_hbm, w_ref)])
        at = 0
        for ref, width in zip(piece_refs, widths):
            dp_ref[:, at:at + width] = ref[...]
            at += width
        du = jnp.zeros((tm, D), F32)
        for j in range(NSH):
            du = du + _nt(dp_ref[:, j * D:(j + 1) * D], w_ref[j])
        r, hr = _rms(h_ref[...])
        dx, dgain = _rms_bwd(du, hr, r, g_ref[...])
        dx_ref[...] = dh_ref[...] + dx

        @pl.when(pl.program_id(0) == 0)
        def _():
            dg_ref[...] = jnp.zeros_like(dg_ref)

        dg_ref[...] += dgain

    return _call(
        body, (dh, h, gain, *pieces, w_in), name="mix_bwd_in", grid=(S // tm,),
        in_specs=[_rows(tm, D), _rows(tm, D), _fixed((1, D))] + [_rows(tm, w) for w in widths] + [_ANY],
        out_specs=[_rows(tm, D), _fixed((1, D)), _rows(tm, 4 * D)],
        out_shape=[_sds((S, D), F32), _sds((1, D), F32), _sds((S, 4 * D), BF16)],
        scratch_shapes=_vmem_like(w_in),
        compiler_params=_params(("arbitrary",), 48), exchange=exchange)


def _wgrad(a, b, nblk, ti, name, out_dtype=BF16, exchange=None):
    ka, n = a.shape[1], b.shape[1]
    ns = n // nblk

    def body(a_ref, b_ref, o_ref):
        o_ref[...] = _tn(a_ref[...].astype(BF16), b_ref[...].astype(BF16)).astype(out_dtype)

    res = _call(
        body, (a, b), name=name, grid=(nblk, ka // ti),
        in_specs=[pl.BlockSpec((S, ti), lambda j, i: (0, i)), pl.BlockSpec((S, ns), lambda j, i: (0, j))],
        out_specs=[pl.BlockSpec((None, ti, ns), lambda j, i: (j, i, 0))],
        out_shape=[_sds((nblk, ka, ns), out_dtype)],
        compiler_params=_params(("arbitrary", "arbitrary"), 56), exchange=exchange)
    return res[0] if exchange is None else (res[0][0], res[1])


def _wgrad_groups(pm, dyg):
    def body(a_ref, b_ref, o_ref):
        o_ref[...] = _tn(a_ref[...], b_ref[...])

    col = pl.BlockSpec((S, PG), lambda g: (0, g))
    return pl.pallas_call(
        body, name="wgrad_groups", grid=(PW // PG,),
        in_specs=[col, col], out_specs=pl.BlockSpec((None, PG, PG), lambda g: (g, 0, 0)),
        out_shape=_sds((PW // PG, PG, PG), F32),
        compiler_params=_params(("arbitrary",), 32),
    )(pm, dyg)


def _place():
    x, y, c = lax.axis_index("x"), lax.axis_index("y"), lax.axis_index("c")
    chips = [(1 - x, y), (x, 1 - y), (1 - x, 1 - y)]
    return x, y, c, chips


def _remote(src, dst, ssem, rsem, dev):
    return pltpu.make_async_remote_copy(src_ref=src, dst_ref=dst, send_sem=ssem, recv_sem=rsem,
                                        device_id=dev, device_id_type=MESH)


def _cast_into_block(w, me_idx, name):
    rows, cols = w.shape
    tr = _row_block(rows)

    def body(me_ref, w_ref, o_ref):
        o_ref[...] = w_ref[...].astype(BF16)

    return pl.pallas_call(
        body, name=name, out_shape=_sds((NSH, rows, cols), BF16),
        grid_spec=pltpu.PrefetchScalarGridSpec(
            num_scalar_prefetch=1, grid=(rows // tr,),
            in_specs=[pl.BlockSpec((tr, cols), lambda r, me: (r, 0))],
            out_specs=pl.BlockSpec((None, tr, cols), lambda r, me: (me[0], r, 0))),
        compiler_params=_params(("arbitrary",), 32),
    )(me_idx, w)


def _ex_gather(bufs):
    n = len(bufs)
    per = 8

    def plan(outs, ssem, rsem, w):
        x, y, c, _ = _place()
        sib, nbr_x, nbr_y = (x, y, 1 - c), (1 - x, y, c), (x, 1 - y, c)
        half = outs[w].shape[1] // 2
        quarter = half // 2
        sem = lambda k: (ssem.at[per * w + k], rsem.at[per * w + k])
        rows = lambda blk, start, size: outs[w].at[blk, pl.ds(start, size)]
        mine = rows(2 * x + y, c * half, half)
        from_x = rows(2 * (1 - x) + y, c * half, half)
        from_y = rows(2 * x + (1 - y), c * half, half)
        diag = 2 * (1 - x) + (1 - y)
        pass_y = rows(2 * (1 - x) + y, c * half, quarter)
        pass_x = rows(2 * x + (1 - y), c * half + quarter, quarter)
        diag_0, diag_1 = rows(diag, c * half, quarter), rows(diag, c * half + quarter, quarter)
        first = [_remote(mine, mine, *sem(0), nbr_x), _remote(mine, mine, *sem(1), nbr_y)]
        arrivals = [
            (_remote(from_x, from_x, *sem(0), nbr_x),
             [_remote(pass_y, pass_y, *sem(2), nbr_y), _remote(from_x, from_x, *sem(4), sib)]),
            (_remote(from_y, from_y, *sem(1), nbr_y),
             [_remote(pass_x, pass_x, *sem(3), nbr_x), _remote(from_y, from_y, *sem(5), sib)]),
            (_remote(diag_0, diag_0, *sem(2), nbr_y), [_remote(diag_0, diag_0, *sem(6), sib)]),
            (_remote(diag_1, diag_1, *sem(3), nbr_x), [_remote(diag_1, diag_1, *sem(7), sib)]),
        ]
        other = (1 - c) * half
        from_sibling = [
            _remote(rows(2 * (1 - x) + y, other, half), rows(2 * (1 - x) + y, other, half), *sem(4), sib),
            _remote(rows(2 * x + (1 - y), other, half), rows(2 * x + (1 - y), other, half), *sem(5), sib),
            _remote(rows(diag, other, quarter), rows(diag, other, quarter), *sem(6), sib),
            _remote(rows(diag, other + quarter, quarter), rows(diag, other + quarter, quarter), *sem(7), sib),
        ]
        return first, arrivals, from_sibling

    def start(ins, outs, ssem, rsem):
        x, y, c, _ = _place()
        for w in range(n):
            half = outs[w].shape[1] // 2
            mine = outs[w].at[2 * x + y, pl.ds(c * half, half)]
            _remote(mine, mine, ssem.at[per * w], rsem.at[per * w], (1 - x, y, c)).start()
            _remote(mine, mine, ssem.at[per * w + 1], rsem.at[per * w + 1], (x, 1 - y, c)).start()

    def finish(ins, outs, ssem, rsem):
        plans = [plan(outs, ssem, rsem, w) for w in range(n)]
        started = []
        for direct in (True, False):
            for first, arrivals, _ in plans:
                for arrived, onward in (arrivals[:2] if direct else arrivals[2:]):
                    arrived.wait_recv()
                    for cp in onward:
                        cp.start()
                    started += onward
        for first, _, from_sibling in plans:
            for cp in from_sibling:
                cp.wait_recv()
            started += first
        for cp in started:
            cp.wait_send()

    return Exchange(bufs, [_sds(b.shape, b.dtype) for b in bufs], {w: w for w in range(n)}, per * n, start, finish)


def _simple_exchange(arrays, landing, aliases, make_copies):
    def start(ins, outs, ssem, rsem):
        for cp, _ in make_copies(ins, outs, ssem, rsem, False):
            cp.start()

    def finish(ins, outs, ssem, rsem):
        cps = make_copies(ins, outs, ssem, rsem, True)
        for _, landed in cps:
            landed.wait_recv()
        for cp, _ in cps:
            cp.wait_send()

    return Exchange(arrays, landing, aliases, len(arrays) * 3, start, finish)


def _ex_pair_swap(grads):
    def make(ins, outs, ssem, rsem, landing):
        x, y, c, _ = _place()
        cps = [_remote(ins[w].at[:, 1 - c], outs[w], ssem.at[w], rsem.at[w], (x, y, 1 - c))
               for w in range(len(grads))]
        return [(cp, cp) for cp in cps]

    return _simple_exchange(grads, [_sds((NSH,) + g.shape[2:], g.dtype) for g in grads], {}, make)


def _ex_scatter(parts):
    def make(ins, outs, ssem, rsem, landing):
        x, y, c, chips = _place()
        out = []
        for w in range(len(parts)):
            for k, (px, py) in enumerate(chips):
                sems = (ssem.at[3 * w + k], rsem.at[3 * w + k])
                out.append((_remote(ins[w].at[2 * px + py], outs[w].at[k], *sems, (px, py, c)),
                            _remote(outs[w].at[k], outs[w].at[k], *sems, (px, py, c)) if landing else None))
        return out

    return _simple_exchange(parts, [_sds((3,) + p.shape[1:], p.dtype) for p in parts], {}, make)


def _ex_share(bufs):
    def make(ins, outs, ssem, rsem, landing):
        x, y, c, _ = _place()
        sib = (x, y, 1 - c)
        return [(_remote(outs[w].at[c], outs[w].at[c], ssem.at[w], rsem.at[w], sib),
                 _remote(outs[w].at[1 - c], outs[w].at[1 - c], ssem.at[w], rsem.at[w], sib) if landing else None)
                for w in range(len(bufs))]

    return _simple_exchange(bufs, [_sds(b.shape, b.dtype) for b in bufs], {w: w for w in range(len(bufs))}, make)


def _gather_small(block):
    m_per, n = block.shape

    def body(x_ref, out_ref, ssem, rsem, lsem):
        x, y, c, chips = _place()
        me, sib = (x, y, c), (x, y, 1 - c)

        def rows(px, py, pc):
            return out_ref.at[pl.ds((4 * px + 2 * py + pc) * m_per, m_per), :]

        def copy(k, blk, to, src=None):
            return _remote(rows(*blk) if src is None else src, rows(*blk), ssem.at[k], rsem.at[k], to)

        mine = pltpu.make_async_copy(x_ref, rows(*me), lsem)
        mine.start()
        first = [copy(0, me, sib, src=x_ref)]
        first += [copy(1 + j, me, (*chip, c), src=x_ref) for j, chip in enumerate(chips)]
        for cp in first:
            cp.start()
        passed = [copy(4 + j, (*chip, c), sib) for j, chip in enumerate(chips)]
        for j, chip in enumerate(chips):
            copy(1 + j, (*chip, c), me).wait_recv()
            passed[j].start()
        copy(0, sib, me).wait_recv()
        for j, chip in enumerate(chips):
            copy(4 + j, (*chip, 1 - c), me).wait_recv()
        for cp in first + passed:
            cp.wait_send()
        mine.wait()

    return pl.pallas_call(
        body, name="gather_small", out_shape=jax.ShapeDtypeStruct((8 * m_per, n), block.dtype),
        in_specs=[_VM], out_specs=_VM,
        scratch_shapes=[pltpu.SemaphoreType.DMA((7,)), pltpu.SemaphoreType.DMA((7,)), pltpu.SemaphoreType.DMA],
    )(block)


def _row_block(rows):
    return max(t for t in range(16, 257, 16) if rows % t == 0)


def _pair_sum(grad, got, c_idx, name):
    _, _, half, cols = grad.shape
    tr = _row_block(half)

    def body(c_ref, a_ref, b_ref, o_ref):
        o_ref[...] = (a_ref[...].astype(F32) + b_ref[...].astype(F32)).astype(BF16)

    return pl.pallas_call(
        body, name=name, out_shape=_sds((NSH, half, cols), BF16),
        grid_spec=pltpu.PrefetchScalarGridSpec(
            num_scalar_prefetch=1, grid=(NSH, half // tr),
            in_specs=[pl.BlockSpec((None, None, tr, cols), lambda j, r, c: (j, c[0], r, 0)),
                      pl.BlockSpec((None, tr, cols), lambda j, r, c: (j, r, 0))],
            out_specs=pl.BlockSpec((None, tr, cols), lambda j, r, c: (j, r, 0))),
        compiler_params=_params(("arbitrary", "arbitrary"), 32),
    )(c_idx, grad, got)


def _chip_sum(own, got, place, name):
    _, half, cols = own.shape
    tr = _row_block(half)

    def body(place_ref, own_ref, got_ref, o_ref):
        acc = own_ref[...].astype(F32)
        for k in range(3):
            acc = acc + got_ref[k].astype(F32)
        o_ref[...] = acc

    return pl.pallas_call(
        body, name=name, out_shape=_sds((2, half, cols), F32),
        grid_spec=pltpu.PrefetchScalarGridSpec(
            num_scalar_prefetch=1, grid=(half // tr,),
            in_specs=[pl.BlockSpec((None, tr, cols), lambda r, p: (p[0], r, 0)),
                      pl.BlockSpec((3, tr, cols), lambda r, p: (0, r, 0))],
            out_specs=pl.BlockSpec((None, tr, cols), lambda r, p: (p[1], r, 0))),
        compiler_params=_params(("arbitrary",), 32),
    )(place, own, got)


def _adamw_math(w, g, m, v):
    m = B1 * m + (1.0 - B1) * g
    v = B2 * v + (1.0 - B2) * (g * g)
    m_hat = m / (1.0 - B1 ** STEP)
    v_hat = v / (1.0 - B2 ** STEP)
    return -LR * (m_hat / (jnp.sqrt(v_hat) + AEPS) + WD * w), m, v


def _adamw(w, g, m, v, name, exchange=None):
    rows, cols = w.shape
    tr = _row_block(rows)

    def body(w_ref, g_ref, m_ref, v_ref, go_ref, d_ref, nm_ref, nv_ref):
        g = g_ref[...]
        go_ref[...] = g
        d_ref[...], nm_ref[...], nv_ref[...] = _adamw_math(w_ref[...], g, m_ref[...], v_ref[...])

    blk = pl.BlockSpec((tr, cols), lambda r: (r, 0))
    return _call(
        body, (w, g, m, v), name=name, grid=(rows // tr,), out_shape=[_sds(w.shape, F32)] * 4,
        in_specs=[blk] * 4, out_specs=[blk] * 4,
        compiler_params=_params(("arbitrary",), 32), exchange=exchange)


def _small_update(gathered, w, m, v):
    rows = w.shape[0]

    def body(ga_ref, w_ref, m_ref, v_ref, g_ref, d_ref, nm_ref, nv_ref):
        g = ga_ref[0:rows, :]
        for dev in range(1, 8):
            g = g + ga_ref[dev * rows:(dev + 1) * rows, :]
        g_ref[...] = g
        d_ref[...], nm_ref[...], nv_ref[...] = _adamw_math(w_ref[...], g, m_ref[...], v_ref[...])

    return pl.pallas_call(
        body, name="small_update", out_shape=[jax.ShapeDtypeStruct(w.shape, F32)] * 4,
        in_specs=[_VM] * 4, out_specs=[_VM] * 4,
    )(gathered, w, m, v)


SMALL = ("ffn1_norm", "mix_norm", "ffn2_norm", "final_norm", "pool_scale", "pool_w_group", "loss")
BIG = ("ffn1_w_gate_up", "ffn1_w_down", "w_in", "w_branch_pool", "w_branch_attn", "w_out",
       "ffn2_w_gate_up", "ffn2_w_down")
ORDER = ("ffn1_norm", "ffn1_w_gate_up", "ffn1_w_down", "mix_norm", "w_in", "pool_w_group", "pool_scale",
         "w_branch_pool", "w_branch_attn", "w_out", "ffn2_norm", "ffn2_w_gate_up", "ffn2_w_down", "final_norm")
SMALL_ROWS = 560


def _pack_small(t):
    parts = []
    for k in SMALL:
        rows = t[k].reshape(-1, 128) if k in t else jnp.zeros((1, 128), F32)
        parts.append(jnp.pad(rows, ((0, -rows.shape[0] % 8), (0, 0))))
    packed = jnp.concatenate(parts, axis=0)
    assert packed.shape == (SMALL_ROWS, 128), packed.shape
    return packed


def _unpack_small(packed, like):
    out, at = {}, 0
    for k in SMALL:
        n = like[k].size // 128 if k in like else 1
        out[k] = packed[at:at + n].reshape(like[k].shape) if k in like else packed[at, 0]
        at += n + (-n % 8)
    return out


def _halves(g):
    return g.reshape(NSH, 2, g.shape[1] // 2, g.shape[2])


def kernel(x, ffn1_norm, ffn1_w_gate_up, ffn1_w_down, mix_norm, w_in, pool_w_group, pool_scale, w_branch_pool, w_branch_attn, w_out, ffn2_norm, ffn2_w_gate_up, ffn2_w_down, final_norm, loss_target, m_ffn1_norm, m_ffn1_w_gate_up, m_ffn1_w_down, m_mix_norm, m_w_in, m_pool_w_group, m_pool_scale, m_w_branch_pool, m_w_branch_attn, m_w_out, m_ffn2_norm, m_ffn2_w_gate_up, m_ffn2_w_down, m_final_norm, v_ffn1_norm, v_ffn1_w_gate_up, v_ffn1_w_down, v_mix_norm, v_w_in, v_pool_w_group, v_pool_scale, v_w_branch_pool, v_w_branch_attn, v_w_out, v_ffn2_norm, v_ffn2_w_gate_up, v_ffn2_w_down, v_final_norm):
    wts = dict(ffn1_norm=ffn1_norm, ffn1_w_gate_up=ffn1_w_gate_up, ffn1_w_down=ffn1_w_down, mix_norm=mix_norm,
               w_in=w_in, pool_w_group=pool_w_group, pool_scale=pool_scale, w_branch_pool=w_branch_pool,
               w_branch_attn=w_branch_attn, w_out=w_out, ffn2_norm=ffn2_norm, ffn2_w_gate_up=ffn2_w_gate_up,
               ffn2_w_down=ffn2_w_down, final_norm=final_norm)
    mom = dict(ffn1_norm=m_ffn1_norm, ffn1_w_gate_up=m_ffn1_w_gate_up, ffn1_w_down=m_ffn1_w_down,
               mix_norm=m_mix_norm, w_in=m_w_in, pool_w_group=m_pool_w_group, pool_scale=m_pool_scale,
               w_branch_pool=m_w_branch_pool, w_branch_attn=m_w_branch_attn, w_out=m_w_out,
               ffn2_norm=m_ffn2_norm, ffn2_w_gate_up=m_ffn2_w_gate_up, ffn2_w_down=m_ffn2_w_down,
               final_norm=m_final_norm)
    var = dict(ffn1_norm=v_ffn1_norm, ffn1_w_gate_up=v_ffn1_w_gate_up, ffn1_w_down=v_ffn1_w_down,
               mix_norm=v_mix_norm, w_in=v_w_in, pool_w_group=v_pool_w_group, pool_scale=v_pool_scale,
               w_branch_pool=v_w_branch_pool, w_branch_attn=v_w_branch_attn, w_out=v_w_out,
               ffn2_norm=v_ffn2_norm, ffn2_w_gate_up=v_ffn2_w_gate_up, ffn2_w_down=v_ffn2_w_down,
               final_norm=v_final_norm)

    c_idx = lax.axis_index("c").astype(jnp.int32).reshape(1)
    me_idx = (2 * lax.axis_index("x") + lax.axis_index("y")).astype(jnp.int32).reshape(1)
    place = jnp.concatenate([me_idx, c_idx])
    x0, tgt = x[0], loss_target[0]
    wgrp = pool_w_group[0].astype(BF16)
    g1, gm, g2, gf = ffn1_norm, mix_norm, ffn2_norm, final_norm.reshape(1, D)
    grad, delta, new_m, new_v = {}, {}, {}, {}

    def pair_sums(keys, parts, got):
        return [_pair_sum(parts[i], got[i], c_idx, "pair_sum_" + k) for i, k in enumerate(keys)]

    def chip_sums(keys, chip_parts, owned):
        return [_chip_sum(chip_parts[i], owned[i], place, "chip_sum_" + k) for i, k in enumerate(keys)]

    def adamw(k, exchange=None):
        res = _adamw(wts[k][0], grad[k][0], mom[k][0], var[k][0], "adamw_" + k, exchange=exchange)
        outs, landed = (res, None) if exchange is None else res
        grad[k], delta[k], new_m[k], new_v[k] = (o.reshape(wts[k].shape) for o in outs)
        return landed

    own = {k: _cast_into_block(wts[k][0], me_idx, "cast_" + k) for k in BIG}
    first, late = ("ffn1_w_gate_up", "ffn1_w_down"), ("w_branch_pool", "w_branch_attn", "w_out",
                                                       "ffn2_w_gate_up", "ffn2_w_down")
    full = dict(zip(first, _exchange_alone(_ex_gather([own[k] for k in first]), "gather_ffn1")))
    wgu1, wd1 = full["ffn1_w_gate_up"], full["ffn1_w_down"].reshape(DFF, D)
    (h1, n1, gu1, a1), (win,) = _ffn_fwd(x0, g1, wgu1, wd1, "ffn1_fwd", exchange=_ex_gather([own["w_in"]]))
    u, xp, q, k, v, gp, gs = _mix_in(h1, gm, win)
    (o_sb, ctot), landed = _attn_fwd(q, k, v, exchange=_ex_gather([own[k_] for k_ in late]))
    full.update(zip(late, landed))
    wbp, wba, wout = full["w_branch_pool"], full["w_branch_attn"], full["w_out"].reshape(D, D)
    wgu2, wd2 = full["ffn2_w_gate_up"], full["ffn2_w_down"].reshape(DFF, D)
    h2, pm, p, yp, ys, mm = _mix_out(h1, xp, o_sb, gp, gs, wgrp, pool_scale, wbp, wba, wout)
    h3, n3, gu3, a3 = _ffn_fwd(h2, g2, wgu2, wd2, "ffn2_fwd")
    dh3, loss_row, d_gf = _head(h3, tgt, gf)

    def grad_gate_up(n, dgu, name, exchange=None):
        res = _wgrad(n, dgu, NSH, 512, name, exchange=exchange)
        return [_halves(res)] if exchange is None else ([_halves(res[0])], res[1])

    def grad_down(a, dh, name, exchange=None):
        res = _wgrad(a, dh, 1, FFS, name, exchange=exchange)
        halves = lambda g: [_halves(g.reshape(NSH, DFF // NSH, D))]
        return halves(res) if exchange is None else (halves(res[0]), res[1])

    k_gu2, k_d2, k_gu1, k_d1, k_in = (("ffn2_w_gate_up",), ("ffn2_w_down",), ("ffn1_w_gate_up",),
                                      ("ffn1_w_down",), ("w_in",))
    dgu3 = _ffn_bwd_act(dh3, gu3, wd2, "ffn2_bwd_act")
    pa = grad_gate_up(n3, dgu3, "wgrad_gu2") + grad_down(a3, dh3, "wgrad_d2")
    (dh2, d_g2), got_a = _ffn_bwd_in(dh3, h2, g2, dgu3, wgu2, "ffn2_bwd_in", exchange=_ex_pair_swap(pa))
    chip_a = pair_sums(k_gu2 + k_d2, pa, got_a)
    dlg, dyp, dys, do_sb, dyg, dxp, d_scale = _mix_bwd_out(dh2, gp, gs, yp, ys, pm, wgrp, pool_scale, wbp, wba, wout)
    kb = ("w_out", "w_branch_pool", "w_branch_attn")
    pb = [_halves(_wgrad(mm, dh2, 1, 512, "wgrad_out").reshape(NSH, D // NSH, D)),
          _halves(_wgrad(p, dyp, NSH, PW, "wgrad_bp")), _halves(_wgrad(o_sb, dys, NSH, SBW, "wgrad_ba"))]
    chip_b = pair_sums(kb, pb, _exchange_alone(_ex_pair_swap(pb), "pair_swap_mix"))
    k_ab = k_gu2 + k_d2 + kb
    (dq, dk, dv), owned_ab = _attn_bwd(q, k, v, do_sb, ctot, exchange=_ex_scatter(chip_a + chip_b))
    halves_ab = chip_sums(k_ab, chip_a + chip_b, owned_ab)
    (dh1, d_gm, dproj), both_ab = _mix_bwd_in(dh2, h1, gm, (dxp, dq, dk, dv, dlg), win, exchange=_ex_share(halves_ab))
    for i, k_ in enumerate(k_ab):
        grad[k_] = both_ab[i].reshape(wts[k_].shape)

    p_in = [_halves(_wgrad(u, dproj, NSH, 512, "wgrad_in"))]
    p_d1, got_in = grad_down(a1, dh1, "wgrad_d1", exchange=_ex_pair_swap(p_in))
    chip_in = pair_sums(k_in, p_in, got_in)
    dgu1, landed = _ffn_bwd_act(dh1, gu1, wd1, "ffn1_bwd_act",
                                exchange=_join(_ex_scatter(chip_in), _ex_pair_swap(p_d1)))
    owned_in, got_d1 = landed[:1], landed[1:]
    chip_d1 = pair_sums(k_d1, p_d1, got_d1)
    halves_in = chip_sums(k_in, chip_in, owned_in)
    p_gu1, landed = grad_gate_up(n1, dgu1, "wgrad_gu1", exchange=_join(_ex_scatter(chip_d1), _ex_share(halves_in)))
    owned_d1, both_in = landed[:1], landed[1:]
    grad["w_in"] = both_in[0].reshape(w_in.shape)
    halves_d1 = chip_sums(k_d1, chip_d1, owned_d1)
    chip_gu1 = pair_sums(k_gu1, p_gu1, _exchange_alone(_ex_pair_swap(p_gu1), "pair_swap_gu1"))
    (dx, d_g1), landed = _ffn_bwd_in(dh1, x0, g1, dgu1, wgu1, "ffn1_bwd_in",
                                     exchange=_join(_ex_scatter(chip_gu1), _ex_share(halves_d1)))
    owned_gu1, both_d1 = landed[:1], landed[1:]
    grad["ffn1_w_down"] = both_d1[0].reshape(ffn1_w_down.shape)
    both_gu1 = _exchange_alone(_ex_share(chip_sums(k_gu1, chip_gu1, owned_gu1)), "share_gu1")
    grad["ffn1_w_gate_up"] = both_gu1[0].reshape(ffn1_w_gate_up.shape)
    for k_ in BIG:
        adamw(k_)

    small_g = dict(ffn1_norm=d_g1, mix_norm=d_gm, ffn2_norm=d_g2, final_norm=d_gf, pool_scale=d_scale,
                   pool_w_group=_wgrad_groups(pm, dyg), loss=loss_row)
    gathered = _gather_small(_pack_small(small_g))
    sg, sd, sm, sv = _small_update(gathered, _pack_small(wts), _pack_small(mom), _pack_small(var))
    sums = _unpack_small(sg, wts)
    loss = sums.pop("loss")
    grad.update(sums)
    for dst, packed in ((delta, sd), (new_m, sm), (new_v, sv)):
        vals = _unpack_small(packed, wts)
        vals.pop("loss")
        dst.update(vals)
    return (loss, dx[None], *[grad[k_] for k_ in ORDER], *[delta[k_] for k_ in ORDER],
            *[new_m[k_] for k_ in ORDER], *[new_v[k_] for k_ in ORDER])
```

```python
import functools

import jax
import jax.numpy as jnp
from jax import lax
from jax.experimental import pallas as pl
from jax.experimental.pallas import tpu as pltpu

F32 = jnp.float32
BF16 = jnp.bfloat16

S = 2048
D = 1024
DFF = 2816
FFS = 2 * DFF // 4
NSH = 4
PW = 512
PG = 128
POOL_WINDOWS = (2, 4, 8, 16)
HALO = 16
SBW = 512
DH = 64
EPS = 1e-6
SCALE = 0.125
LOG2E = 1.4426950408889634
TA = 256
QB = 2
MIB = 1024 * 1024

LR, B1, B2, AEPS, WD, STEP = 0.001, 0.9, 0.999, 1e-08, 0.01, 10

_VM = pl.BlockSpec(memory_space=pltpu.VMEM)
_ANY = pl.BlockSpec(memory_space=pl.ANY)
MESH = pl.DeviceIdType.MESH


def _nn(a, b):
    return jnp.dot(a, b, preferred_element_type=F32)


def _nt(a, b):
    return lax.dot_general(a, b, (((1,), (1,)), ((), ())), preferred_element_type=F32)


def _tn(a, b):
    return lax.dot_general(a, b, (((0,), (0,)), ((), ())), preferred_element_type=F32)


def _params(sem, vmem_mib):
    return pltpu.CompilerParams(dimension_semantics=sem, vmem_limit_bytes=vmem_mib * MIB)


def _rows(tm, width):
    return pl.BlockSpec((tm, width), lambda i: (i, 0))


def _fixed(shape):
    return pl.BlockSpec(shape, lambda *_: (0,) * len(shape))


def _sds(shape, dtype):
    return pltpu.HBM(shape, dtype)


def _in_hbm(args):
    return [pltpu.with_memory_space_constraint(a, pltpu.HBM) for a in args]


def _stage(pairs):
    @pl.when(pl.program_id(0) == 0)
    def _():
        for src, dst in pairs:
            pltpu.sync_copy(src, dst)


def _vmem_like(*arrays):
    return [pltpu.VMEM(a.shape, a.dtype) for a in arrays]


class Exchange:
    def __init__(self, arrays, landing, aliases, n_sems, start, finish):
        self.arrays, self.landing, self.aliases, self.n_sems = list(arrays), list(landing), dict(aliases), n_sems
        self.start, self.finish = start, finish


def _join(a, b):
    na, la = len(a.arrays), len(a.landing)

    def both(fa, fb):
        def run(ins, outs, ssem, rsem):
            fa(ins[:na], outs[:la], ssem.at[pl.ds(0, a.n_sems)], rsem.at[pl.ds(0, a.n_sems)])
            fb(ins[na:], outs[la:], ssem.at[pl.ds(a.n_sems, b.n_sems)], rsem.at[pl.ds(a.n_sems, b.n_sems)])
        return run

    aliases = {**a.aliases, **{na + i: la + j for i, j in b.aliases.items()}}
    return Exchange(a.arrays + b.arrays, a.landing + b.landing, aliases, a.n_sems + b.n_sems,
                    both(a.start, b.start), both(a.finish, b.finish))


def _call(body, args, *, name, grid, in_specs, out_specs, out_shape, scratch_shapes=(), compiler_params=None,
          exchange=None, free=()):
    args = [a if i in free else pltpu.with_memory_space_constraint(a, pltpu.HBM) for i, a in enumerate(args)]
    if exchange is None:
        return pl.pallas_call(body, name=name, grid=grid, in_specs=in_specs, out_specs=out_specs,
                              out_shape=out_shape, scratch_shapes=list(scratch_shapes),
                              compiler_params=compiler_params)(*args)
    ex = exchange
    n_in, n_out, n_scr = len(in_specs), len(out_specs), len(scratch_shapes)
    na, nl = len(ex.arrays), len(ex.landing)

    def hosted(*refs):
        at = [0]

        def take(n):
            at[0] += n
            return refs[at[0] - n:at[0]]

        k_in, e_in, k_out, e_out, k_scr = take(n_in), take(na), take(n_out), take(nl), take(n_scr)
        ssem, rsem = take(2)
        ids = [pl.program_id(a) for a in range(len(grid))]
        first = functools.reduce(jnp.logical_and, [i == 0 for i in ids])
        last = functools.reduce(jnp.logical_and, [i == g - 1 for i, g in zip(ids, grid)])

        @pl.when(first)
        def _():
            ex.start(e_in, e_out, ssem, rsem)

        body(*k_in, *k_out, *k_scr)

        @pl.when(last)
        def _():
            ex.finish(e_in, e_out, ssem, rsem)

    outs = pl.pallas_call(
        hosted, name=name, grid=grid,
        in_specs=list(in_specs) + [_ANY] * na, out_specs=list(out_specs) + [_ANY] * nl,
        out_shape=list(out_shape) + ex.landing,
        scratch_shapes=list(scratch_shapes) + [pltpu.SemaphoreType.DMA((ex.n_sems,))] * 2,
        input_output_aliases={n_in + i: n_out + j for i, j in ex.aliases.items()},
        compiler_params=compiler_params,
    )(*args, *_in_hbm(ex.arrays))
    return outs[:n_out], outs[n_out:]


def _exchange_alone(ex, name):
    def body(*refs):
        na, nl = len(ex.arrays), len(ex.landing)
        ex.start(refs[:na], refs[na:na + nl], refs[-2], refs[-1])
        ex.finish(refs[:na], refs[na:na + nl], refs[-2], refs[-1])

    return pl.pallas_call(
        body, name=name, in_specs=[_ANY] * len(ex.arrays), out_specs=[_ANY] * len(ex.landing),
        out_shape=ex.landing, scratch_shapes=[pltpu.SemaphoreType.DMA((ex.n_sems,))] * 2,
        input_output_aliases=ex.aliases,
    )(*_in_hbm(ex.arrays))


def _rms(x):
    r = lax.rsqrt(jnp.mean(x * x, axis=-1, keepdims=True) + EPS)
    return r, x * r


def _rms_bwd(dn, xr, r, gain):
    dng = dn * gain
    dx = r * (dng - xr * jnp.mean(dng * xr, axis=-1, keepdims=True))
    return dx, jnp.sum(dn * xr, axis=0, keepdims=True)


def _ffn_fwd(x, gain, wgu, wd, name, exchange=None):
    tm = 256

    def body(x_ref, g_ref, wgu_hbm, wd_hbm, h_ref, n_ref, gu_ref, a_ref, wgu_ref, wd_ref):
        _stage([(wgu_hbm, wgu_ref), (wd_hbm, wd_ref)])
        x = x_ref[...]
        _, xr = _rms(x)
        n = (xr * g_ref[...]).astype(BF16)
        n_ref[...] = n
        acc = jnp.zeros((tm, D), F32)
        for j in range(2):
            g = _nn(n, wgu_ref[j])
            u = _nn(n, wgu_ref[2 + j])
            gu_ref[:, j * FFS:(j + 1) * FFS] = g.astype(BF16)
            gu_ref[:, (2 + j) * FFS:(3 + j) * FFS] = u.astype(BF16)
            half_act = (0.5 * (g * jax.nn.sigmoid(g) * u)).astype(BF16)
            a_ref[:, j * FFS:(j + 1) * FFS] = half_act
            acc = acc + _nn(half_act, wd_ref[j * FFS:(j + 1) * FFS, :])
        h_ref[...] = x + acc

    return _call(
        body, (x, gain, wgu, wd), name=name, grid=(S // tm,),
        in_specs=[_rows(tm, D), _fixed((1, D)), _ANY, _ANY],
        out_specs=[_rows(tm, D), _rows(tm, D), _rows(tm, 4 * FFS), _rows(tm, DFF)],
        out_shape=[_sds((S, D), F32), _sds((S, D), BF16), _sds((S, 4 * FFS), BF16), _sds((S, DFF), BF16)],
        scratch_shapes=_vmem_like(wgu, wd),
        compiler_params=_params(("arbitrary",), 56), exchange=exchange)


def _ffn_bwd_act(dh, gu, wd, name, exchange=None):
    tm = 512

    def body(dh_ref, gu_ref, wd_hbm, dgu_ref, wd_ref):
        _stage([(wd_hbm, wd_ref)])
        dhb = dh_ref[...].astype(BF16)
        for j in range(2):
            g = gu_ref[:, j * FFS:(j + 1) * FFS].astype(F32)
            u = gu_ref[:, (2 + j) * FFS:(3 + j) * FFS].astype(F32)
            da = 0.5 * _nt(dhb, wd_ref[j * FFS:(j + 1) * FFS, :])
            sg = jax.nn.sigmoid(g)
            dgu_ref[:, j * FFS:(j + 1) * FFS] = (da * u * (sg * (1.0 + g * (1.0 - sg)))).astype(BF16)
            dgu_ref[:, (2 + j) * FFS:(3 + j) * FFS] = (da * (g * sg)).astype(BF16)

    res = _call(
        body, (dh, gu, wd), name=name, grid=(S // tm,),
        in_specs=[_rows(tm, D), _rows(tm, 4 * FFS), _ANY], out_specs=[_rows(tm, 4 * FFS)],
        out_shape=[_sds((S, 4 * FFS), BF16)], scratch_shapes=_vmem_like(wd),
        compiler_params=_params(("arbitrary",), 56), exchange=exchange)
    return res[0] if exchange is None else (res[0][0], res[1])


def _ffn_bwd_in(dh, x, gain, dgu, wgu, name, exchange=None):
    tm = 512

    def body(dh_ref, x_ref, g_ref, dgu_ref, wgu_hbm, dx_ref, dg_ref, wgu_ref):
        _stage([(wgu_hbm, wgu_ref)])
        dn = jnp.zeros((tm, D), F32)
        for j in range(NSH):
            dn = dn + _nt(dgu_ref[:, j * FFS:(j + 1) * FFS], wgu_ref[j])
        r, xr = _rms(x_ref[...])
        dx, dgain = _rms_bwd(dn, xr, r, g_ref[...])
        dx_ref[...] = dh_ref[...] + dx

        @pl.when(pl.program_id(0) == 0)
        def _():
            dg_ref[...] = jnp.zeros_like(dg_ref)

        dg_ref[...] += dgain

    return _call(
        body, (dh, x, gain, dgu, wgu), name=name, grid=(S // tm,),
        in_specs=[_rows(tm, D), _rows(tm, D), _fixed((1, D)), _rows(tm, 4 * FFS), _ANY],
        out_specs=[_rows(tm, D), _fixed((1, D))],
        out_shape=[_sds((S, D), F32), _sds((1, D), F32)], scratch_shapes=_vmem_like(wgu),
        compiler_params=_params(("arbitrary",), 56), exchange=exchange)


def _head(h, target, gain):
    tm = 512

    def body(h_ref, t_ref, g_ref, dh_ref, loss_ref, dg_ref):
        gain = g_ref[...]
        r, hr = _rms(h_ref[...])
        err = hr * gain - t_ref[...]
        dy = err * (1.0 / D)
        dh, dgain = _rms_bwd(dy, hr, r, gain)
        dh_ref[...] = dh

        @pl.when(pl.program_id(0) == 0)
        def _():
            dg_ref[...] = jnp.zeros_like(dg_ref)
            loss_ref[...] = jnp.zeros_like(loss_ref)

        dg_ref[...] += dgain
        loss_ref[...] += jnp.full((1, 128), (0.5 / D) * jnp.sum(err * err), F32)

    return pl.pallas_call(
        body, name="head", grid=(S // tm,),
        in_specs=[_rows(tm, D), _rows(tm, D), _fixed((1, D))],
        out_specs=[_rows(tm, D), _fixed((1, 128)), _fixed((1, D))],
        out_shape=[_sds((S, D), F32), _sds((1, 128), F32), _sds((1, D), F32)],
        compiler_params=_params(("arbitrary",), 40),
    )(*_in_hbm([h]), target, gain)


def _mix_in(h, gain, w_in):
    tm = 512

    def body(h_ref, g_ref, w_hbm, u_ref, xp_ref, q_ref, k_ref, v_ref, gp_ref, gs_ref, w_ref):
        _stage([(w_hbm, w_ref)])
        _, hr = _rms(h_ref[...])
        u = (hr * g_ref[...]).astype(BF16)
        u_ref[...] = u
        p0 = _nn(u, w_ref[0])
        xp_ref[...] = p0[:, :PW]
        q_ref[...] = p0[:, PW:].astype(BF16)
        p1 = _nn(u, w_ref[1])
        k_ref[...] = p1[:, :SBW].astype(BF16)
        v_ref[...] = p1[:, SBW:].astype(BF16)
        gp_ref[...] = jax.nn.sigmoid(_nn(u, w_ref[2])).astype(BF16)
        gs_ref[...] = jax.nn.sigmoid(_nn(u, w_ref[3])).astype(BF16)

    return pl.pallas_call(
        body, name="mix_in", grid=(S // tm,),
        in_specs=[_rows(tm, D), _fixed((1, D)), _ANY],
        out_specs=[_rows(tm, D), _rows(tm, PW), _rows(tm, SBW), _rows(tm, SBW), _rows(tm, SBW),
                   _rows(tm, D), _rows(tm, D)],
        out_shape=[_sds((S, D), BF16), _sds((S, PW), F32), _sds((S, SBW), BF16), _sds((S, SBW), BF16),
                   _sds((S, SBW), BF16), _sds((S, D), BF16), _sds((S, D), BF16)],
        scratch_shapes=_vmem_like(w_in),
        compiler_params=_params(("arbitrary",), 48),
    )(*_in_hbm([h]), gain, *_in_hbm([w_in]))


def _hilo_dot(x, tri):
    hi = x.astype(BF16)
    lo = (x - hi.astype(F32)).astype(BF16)
    return _nn(hi, tri) + _nn(lo, tri)


def _log_terms(qk):
    z2 = qk * (SCALE * LOG2E)
    lb = jnp.minimum(z2, 0.0) - jnp.log2(1.0 + jnp.exp2(-jnp.abs(z2)))
    return lb, lb - z2


def _head_masks():
    lane = lax.broadcasted_iota(jnp.int32, (1, 2 * DH), 1)
    return (lane < DH, lane >= DH)


def _attn_fwd(q, k, v, exchange=None):
    T = TA

    def body(q_ref, k_ref, v_ref, o_ref, c_ref):
        i2 = 2 * pl.program_id(1)
        row = lax.broadcasted_iota(jnp.int32, (T, T), 0)
        col = lax.broadcasted_iota(jnp.int32, (T, T), 1)
        after = (row > col).astype(BF16)
        causal = col < row
        masks = _head_masks()
        qms = {}
        for b in range(QB):
            q2 = q_ref[b * T:(b + 1) * T, :]
            for h, hm in enumerate(masks):
                qms[b, h] = jnp.where(hm, q2, jnp.zeros_like(q2))

        def blocks(keys, pairs, carries, os):
            ks, vms = [], []
            for j in keys:
                rows = pl.ds(pl.multiple_of(j * T, T), T)
                vj = v_ref[rows, :]
                ks.append(k_ref[rows, :])
                vms.append([jnp.where(hm, vj, jnp.zeros_like(vj)) for hm in masks])
            units = [(n, h) for n in range(len(pairs)) for h in range(2)]
            qks = {(n, h): _nt(qms[pairs[n][0], h], ks[pairs[n][1]]) for n, h in units}
            lbs, l1ms = {}, {}
            for u in units:
                lbs[u], l1m = _log_terms(qks[u])
                l1ms[u] = jnp.where(causal, l1m, 0.0) if pairs[u[0]][2] else l1m
            cins = {u: _hilo_dot(l1ms[u], after) for u in units}
            carries, os = dict(carries), list(os)
            for n, h in units:
                b, key, diag = pairs[n]
                a = jnp.exp2(lbs[n, h] + cins[n, h] + carries[b, h])
                if diag:
                    a = jnp.where(causal, a, 0.0)
                os[b] = os[b] + _nn(a.astype(BF16), vms[key][h])
                carries[b, h] = carries[b, h] + jnp.sum(l1ms[n, h], axis=1, keepdims=True)
            return carries, tuple(os)

        carries = {(b, h): jnp.zeros((T, 1), F32) for b in range(QB) for h in range(2)}
        os = tuple(jnp.zeros((T, 2 * DH), F32) for _ in range(QB))
        carries, os = blocks([i2 + 1, i2], [(1, 0, True), (0, 1, True), (1, 1, False)], carries, os)
        carries, os = lax.fori_loop(
            0, i2, lambda jj, c: blocks([i2 - 1 - jj], [(0, 0, False), (1, 0, False)], c[0], c[1]), (carries, os))
        for b in range(QB):
            o_ref[b * T:(b + 1) * T, :] = os[b].astype(BF16)
            c_ref[b * T:(b + 1) * T, :] = jnp.where(masks[0], carries[b, 0], carries[b, 1])

    blk = pl.BlockSpec((QB * T, 2 * DH), lambda p, i: (i, p))
    full = pl.BlockSpec((S, 2 * DH), lambda p, i: (0, p))
    return _call(
        body, (q, k, v), name="attn_fwd", grid=(SBW // (2 * DH), S // (QB * T)),
        in_specs=[blk, full, full], out_specs=[blk, blk],
        out_shape=[_sds((S, SBW), BF16), _sds((S, SBW), F32)],
        compiler_params=_params(("arbitrary", "arbitrary"), 40), exchange=exchange)


def _attn_bwd(q, k, v, do, ctot, exchange=None):
    T = TA
    nq = S // (QB * T)

    def body(q_ref, k_ref, v_ref, do_ref, c_ref, dq_ref, dk_ref, dv_ref, dk_acc, dv_acc):
        step = pl.program_id(1)
        i2 = 2 * step

        @pl.when(step == 0)
        def _():
            dk_acc[...] = jnp.zeros_like(dk_acc)
            dv_acc[...] = jnp.zeros_like(dv_acc)

        row = lax.broadcasted_iota(jnp.int32, (T, T), 0)
        col = lax.broadcasted_iota(jnp.int32, (T, T), 1)
        upto = (row <= col).astype(BF16)
        before = (row < col).astype(BF16)
        causal = col < row
        masks = _head_masks()
        qms, doms, ctots = {}, {}, {}
        for b in range(QB):
            q2, do2 = q_ref[b * T:(b + 1) * T, :], do_ref[b * T:(b + 1) * T, :]
            for h, hm in enumerate(masks):
                qms[b, h] = jnp.where(hm, q2, jnp.zeros_like(q2))
                doms[b, h] = jnp.where(hm, do2, jnp.zeros_like(do2))
                ctots[b, h] = c_ref[b * T:(b + 1) * T, h * DH:h * DH + 1]

        def blocks(keys, pairs, sums, dqs):
            rows = [pl.ds(pl.multiple_of(j * T, T), T) for j in keys]
            ks, vs = [k_ref[r, :] for r in rows], [v_ref[r, :] for r in rows]
            kms = [[jnp.where(hm, kj, jnp.zeros_like(kj)) for hm in masks] for kj in ks]
            units = [(n, h) for n in range(len(pairs)) for h in range(2)]
            qks = {(n, h): _nt(qms[pairs[n][0], h], ks[pairs[n][1]]) for n, h in units}
            das = {(n, h): _nt(doms[pairs[n][0], h], vs[pairs[n][1]]) for n, h in units}
            lbs, l1ms = {}, {}
            for u in units:
                lbs[u], l1m = _log_terms(qks[u])
                l1ms[u] = jnp.where(causal, l1m, 0.0) if pairs[u[0]][2] else l1m
            pins = {u: _hilo_dot(l1ms[u], upto) for u in units}
            sums = dict(sums)
            a_s, dls, cps = {}, {}, {}
            for n, h in units:
                b, _, diag = pairs[n]
                cl, cp = sums[b, h]
                a = jnp.exp2(lbs[n, h] + (ctots[b, h] - cl) - pins[n, h])
                if diag:
                    a = jnp.where(causal, a, 0.0)
                a_s[n, h] = a.astype(BF16)
                dls[n, h] = das[n, h] * a
                cps[n, h] = cp
                sums[b, h] = (cl + jnp.sum(l1ms[n, h], axis=1, keepdims=True),
                              cp + jnp.sum(dls[n, h], axis=1, keepdims=True))
            pexs = {u: _hilo_dot(dls[u], before) for u in units}
            dzbs = {}
            for u in units:
                dz = dls[u] - jnp.exp2(lbs[u]) * (dls[u] + pexs[u] + cps[u])
                if pairs[u[0]][2]:
                    dz = jnp.where(causal, dz, 0.0)
                dzbs[u] = dz.astype(BF16)
            dqs = list(dqs)
            for n, h in units:
                dqs[pairs[n][0]] = dqs[pairs[n][0]] + _nn(dzbs[n, h], kms[pairs[n][1]][h])
            for key, r in enumerate(rows):
                mine = [(n, h) for n, h in units if pairs[n][1] == key]
                dk_acc[r, :] += functools.reduce(jnp.add, [_tn(dzbs[u], qms[pairs[u[0]][0], u[1]]) for u in mine])
                dv_acc[r, :] += functools.reduce(jnp.add, [_tn(a_s[u], doms[pairs[u[0]][0], u[1]]) for u in mine])
            return sums, tuple(dqs)

        zero = jnp.zeros((T, 1), F32)
        sums = {(b, h): (zero, zero) for b in range(QB) for h in range(2)}
        dqs = tuple(jnp.zeros((T, 2 * DH), F32) for _ in range(QB))
        sums, dqs = lax.fori_loop(
            0, i2, lambda j, c: blocks([j], [(0, 0, False), (1, 0, False)], c[0], c[1]), (sums, dqs))
        _, dqs = blocks([i2, i2 + 1], [(0, 0, True), (1, 0, False), (1, 1, True)], sums, dqs)
        for b in range(QB):
            dq_ref[b * T:(b + 1) * T, :] = (dqs[b] * SCALE).astype(BF16)

        @pl.when(step == nq - 1)
        def _():
            dk_ref[...] = (dk_acc[...] * SCALE).astype(BF16)
            dv_ref[...] = dv_acc[...].astype(BF16)

    blk = pl.BlockSpec((QB * T, 2 * DH), lambda p, i: (i, p))
    full = pl.BlockSpec((S, 2 * DH), lambda p, i: (0, p))
    return _call(
        body, (q, k, v, do, ctot), name="attn_bwd", grid=(SBW // (2 * DH), nq),
        in_specs=[blk, full, full, blk, blk], out_specs=[blk, full, full],
        out_shape=[_sds((S, SBW), BF16), _sds((S, SBW), BF16), _sds((S, SBW), BF16)],
        scratch_shapes=[pltpu.VMEM((S, 2 * DH), F32), pltpu.VMEM((S, 2 * DH), F32)],
        compiler_params=_params(("arbitrary", "arbitrary"), 40), exchange=exchange)


def _pool_counts(first_row, tm):
    pos = first_row + lax.broadcasted_iota(jnp.int32, (tm, 1), 0)
    return [jnp.minimum(pos + 1, w).astype(F32) for w in POOL_WINDOWS]


def _mix_out(h, xp, o_sb, gp, gs, w_group, scale, w_bp, w_ba, w_out):
    tm = 512

    def body(h_ref, xp_ref, o_ref, gp_ref, gs_ref, wg_hbm, sc_ref, wbp_hbm, wba_hbm, wo_hbm,
             h2_ref, pm_ref, p_ref, yp_ref, ys_ref, m_ref, halo, wg_ref, wbp_ref, wba_ref, wo_ref):
        _stage([(wg_hbm, wg_ref), (wbp_hbm, wbp_ref), (wba_hbm, wba_ref), (wo_hbm, wo_ref)])
        i = pl.program_id(0)

        @pl.when(i == 0)
        def _():
            halo[...] = jnp.zeros_like(halo)

        xp = xp_ref[...]
        ext = jnp.concatenate([halo[...], xp], axis=0)
        halo[...] = xp[tm - HALO:, :]
        counts = _pool_counts(i * tm, tm)
        for gi in range(len(POOL_WINDOWS)):
            lanes = slice(gi * PG, (gi + 1) * PG)
            win = ext[:, lanes]
            for step in range(gi + 1):
                win = win + pltpu.roll(win, 1 << step, 0)
            pm = (win[HALO:, :] / counts[gi] - xp[:, lanes]).astype(BF16)
            pm_ref[:, lanes] = pm
            p_ref[:, lanes] = (_nn(pm, wg_ref[gi]) * sc_ref[:, lanes]).astype(BF16)
        pb = p_ref[...]
        ob = o_ref[...]
        for j in range(NSH):
            cols = slice(j * (D // NSH), (j + 1) * (D // NSH))
            yp = _nn(pb, wbp_ref[j])
            ys = _nn(ob, wba_ref[j])
            yp_ref[:, cols] = yp.astype(BF16)
            ys_ref[:, cols] = ys.astype(BF16)
            m_ref[:, cols] = (gp_ref[:, cols].astype(F32) * yp + gs_ref[:, cols].astype(F32) * ys).astype(BF16)
        h2_ref[...] = h_ref[...] + _nn(m_ref[...], wo_ref[...])

    return pl.pallas_call(
        body, name="mix_out", grid=(S // tm,),
        in_specs=[_rows(tm, D), _rows(tm, PW), _rows(tm, SBW), _rows(tm, D), _rows(tm, D),
                  _ANY, _fixed((1, PW)), _ANY, _ANY, _ANY],
        out_specs=[_rows(tm, D), _rows(tm, PW), _rows(tm, PW), _rows(tm, D), _rows(tm, D), _rows(tm, D)],
        out_shape=[_sds((S, D), F32), _sds((S, PW), BF16), _sds((S, PW), BF16), _sds((S, D), BF16),
                   _sds((S, D), BF16), _sds((S, D), BF16)],
        scratch_shapes=[pltpu.VMEM((HALO, PW), F32)] + _vmem_like(w_group, w_bp, w_ba, w_out),
        compiler_params=_params(("arbitrary",), 48),
    )(*_in_hbm([h, xp, o_sb, gp, gs]), w_group, scale, *_in_hbm([w_bp, w_ba, w_out]))


def _mix_bwd_out(dh, gp, gs, yp, ys, pm, w_group, scale, w_bp, w_ba, w_out, exchange=None):
    tm = 512
    nt = S // tm

    def body(dh_ref, gp_ref, gs_ref, yp_ref, ys_ref, pm_ref, wg_hbm, sc_ref, wbp_hbm, wba_hbm, wo_hbm,
             dlg_ref, dyp_ref, dys_ref, do_ref, dyg_ref, dxp_ref, dsc_ref, halo, wg_ref, wbp_ref, wba_ref, wo_ref):
        _stage([(wg_hbm, wg_ref), (wbp_hbm, wbp_ref), (wba_hbm, wba_ref), (wo_hbm, wo_ref)])
        step = pl.program_id(0)

        @pl.when(step == 0)
        def _():
            halo[...] = jnp.zeros_like(halo)
            dsc_ref[...] = jnp.zeros_like(dsc_ref)

        dm = _nt(dh_ref[...].astype(BF16), wo_ref[...])
        gp = gp_ref[...].astype(F32)
        gs = gs_ref[...].astype(F32)
        yp = yp_ref[...].astype(F32)
        ys = ys_ref[...].astype(F32)
        dlg_ref[:, :D] = (dm * yp * gp * (1.0 - gp)).astype(BF16)
        dlg_ref[:, D:] = (dm * ys * gs * (1.0 - gs)).astype(BF16)
        dyp_ref[...] = (dm * gp).astype(BF16)
        dys_ref[...] = (dm * gs).astype(BF16)
        dp = jnp.zeros((tm, PW), F32)
        do = jnp.zeros((tm, SBW), F32)
        for j in range(NSH):
            cols = slice(j * (D // NSH), (j + 1) * (D // NSH))
            dp = dp + _nt(dyp_ref[:, cols], wbp_ref[j])
            do = do + _nt(dys_ref[:, cols], wba_ref[j])
        do_ref[...] = do.astype(BF16)
        counts = _pool_counts((nt - 1 - step) * tm, tm)
        dscale = []
        for gi in range(len(POOL_WINDOWS)):
            lanes = slice(gi * PG, (gi + 1) * PG)
            dpg = dp[:, lanes]
            dscale.append(jnp.sum(dpg * _nn(pm_ref[:, lanes], wg_ref[gi]), axis=0, keepdims=True))
            dyg = (dpg * sc_ref[:, lanes]).astype(BF16)
            dyg_ref[:, lanes] = dyg
            dpm = _nt(dyg, wg_ref[gi])
            per = dpm / counts[gi]
            win = jnp.concatenate([per, halo[:, lanes]], axis=0)
            halo[:, lanes] = per[:HALO, :]
            for s in range(gi + 1):
                win = win + pltpu.roll(win, tm + HALO - (1 << s), 0)
            dxp_ref[:, lanes] = (win[:tm, :] - dpm).astype(BF16)
        dsc_ref[...] += jnp.concatenate(dscale, axis=1)

    rev = lambda width: pl.BlockSpec((tm, width), lambda i: (nt - 1 - i, 0))
    return _call(
        body, (dh, gp, gs, yp, ys, pm, w_group, scale, w_bp, w_ba, w_out), name="mix_bwd_out", grid=(nt,),
        in_specs=[rev(D), rev(D), rev(D), rev(D), rev(D), rev(PW), _ANY, _fixed((1, PW)), _ANY, _ANY, _ANY],
        out_specs=[rev(2 * D), rev(D), rev(D), rev(SBW), rev(PW), rev(PW), _fixed((1, PW))],
        out_shape=[_sds((S, 2 * D), BF16), _sds((S, D), BF16), _sds((S, D), BF16), _sds((S, SBW), BF16),
                   _sds((S, PW), BF16), _sds((S, PW), BF16), _sds((1, PW), F32)],
        scratch_shapes=[pltpu.VMEM((HALO, PW), F32)] + _vmem_like(w_group, w_bp, w_ba, w_out),
        compiler_params=_params(("arbitrary",), 48), exchange=exchange)


def _mix_bwd_in(dh, h, gain, pieces, w_in, exchange=None):
    tm = 512
    widths = [p.shape[1] for p in pieces]

    def body(dh_ref, h_ref, g_ref, *rest):
        piece_refs, (w_hbm, dx_ref, dg_ref, dp_ref, w_ref) = rest[:len(pieces)], rest[len(pieces):]
        _stage([(w_hbm, w_ref)])
        at = 0
        for ref, width in zip(piece_refs, widths):
            dp_ref[:, at:at + width] = ref[...]
            at += width
        du = jnp.zeros((tm, D), F32)
        for j in range(NSH):
            du = du + _nt(dp_ref[:, j * D:(j + 1) * D], w_ref[j])
        r, hr = _rms(h_ref[...])
        dx, dgain = _rms_bwd(du, hr, r, g_ref[...])
        dx_ref[...] = dh_ref[...] + dx

        @pl.when(pl.program_id(0) == 0)
        def _():
            dg_ref[...] = jnp.zeros_like(dg_ref)

        dg_ref[...] += dgain

    return _call(
        body, (dh, h, gain, *pieces, w_in), name="mix_bwd_in", grid=(S // tm,),
        in_specs=[_rows(tm, D), _rows(tm, D), _fixed((1, D))] + [_rows(tm, w) for w in widths] + [_ANY],
        out_specs=[_rows(tm, D), _fixed((1, D)), _rows(tm, 4 * D)],
        out_shape=[_sds((S, D), F32), _sds((1, D), F32), _sds((S, 4 * D), BF16)],
        scratch_shapes=_vmem_like(w_in),
        compiler_params=_params(("arbitrary",), 48), exchange=exchange)


def _wgrad(a, b, nblk, ti, name, out_dtype=BF16, exchange=None):
    ka, n = a.shape[1], b.shape[1]
    ns = n // nblk

    def body(a_ref, b_ref, o_ref):
        o_ref[...] = _tn(a_ref[...].astype(BF16), b_ref[...].astype(BF16)).astype(out_dtype)

    res = _call(
        body, (a, b), name=name, grid=(nblk, ka // ti),
        in_specs=[pl.BlockSpec((S, ti), lambda j, i: (0, i)), pl.BlockSpec((S, ns), lambda j, i: (0, j))],
        out_specs=[pl.BlockSpec((None, ti, ns), lambda j, i: (j, i, 0))],
        out_shape=[_sds((nblk, ka, ns), out_dtype)],
        compiler_params=_params(("arbitrary", "arbitrary"), 56), exchange=exchange)
    return res[0] if exchange is None else (res[0][0], res[1])


def _wgrad_groups(pm, dyg):
    def body(a_ref, b_ref, o_ref):
        o_ref[...] = _tn(a_ref[...], b_ref[...])

    col = pl.BlockSpec((S, PG), lambda g: (0, g))
    return pl.pallas_call(
        body, name="wgrad_groups", grid=(PW // PG,),
        in_specs=[col, col], out_specs=pl.BlockSpec((None, PG, PG), lambda g: (g, 0, 0)),
        out_shape=_sds((PW // PG, PG, PG), F32),
        compiler_params=_params(("arbitrary",), 32),
    )(*_in_hbm([pm, dyg]))


def _place():
    x, y, c = lax.axis_index("x"), lax.axis_index("y"), lax.axis_index("c")
    chips = [(1 - x, y), (x, 1 - y), (1 - x, 1 - y)]
    return x, y, c, chips


def _remote(src, dst, ssem, rsem, dev):
    return pltpu.make_async_remote_copy(src_ref=src, dst_ref=dst, send_sem=ssem, recv_sem=rsem,
                                        device_id=dev, device_id_type=MESH)


def _cast_into_block(w, me_idx, name):
    rows, cols = w.shape
    tr = _row_block(rows)

    def body(me_ref, w_ref, o_ref):
        o_ref[...] = w_ref[...].astype(BF16)

    return pl.pallas_call(
        body, name=name, out_shape=_sds((NSH, rows, cols), BF16),
        grid_spec=pltpu.PrefetchScalarGridSpec(
            num_scalar_prefetch=1, grid=(rows // tr,),
            in_specs=[pl.BlockSpec((tr, cols), lambda r, me: (r, 0))],
            out_specs=pl.BlockSpec((None, tr, cols), lambda r, me: (me[0], r, 0))),
        compiler_params=_params(("arbitrary",), 32),
    )(me_idx, w)


def _ex_gather(bufs):
    n = len(bufs)
    per = 8

    def plan(outs, ssem, rsem, w):
        x, y, c, _ = _place()
        sib, nbr_x, nbr_y = (x, y, 1 - c), (1 - x, y, c), (x, 1 - y, c)
        half = outs[w].shape[1] // 2
        quarter = half // 2
        sem = lambda k: (ssem.at[per * w + k], rsem.at[per * w + k])
        rows = lambda blk, start, size: outs[w].at[blk, pl.ds(start, size)]
        mine = rows(2 * x + y, c * half, half)
        from_x = rows(2 * (1 - x) + y, c * half, half)
        from_y = rows(2 * x + (1 - y), c * half, half)
        diag = 2 * (1 - x) + (1 - y)
        pass_y = rows(2 * (1 - x) + y, c * half, quarter)
        pass_x = rows(2 * x + (1 - y), c * half + quarter, quarter)
        diag_0, diag_1 = rows(diag, c * half, quarter), rows(diag, c * half + quarter, quarter)
        first = [_remote(mine, mine, *sem(0), nbr_x), _remote(mine, mine, *sem(1), nbr_y)]
        arrivals = [
            (_remote(from_x, from_x, *sem(0), nbr_x),
             [_remote(pass_y, pass_y, *sem(2), nbr_y), _remote(from_x, from_x, *sem(4), sib)]),
            (_remote(from_y, from_y, *sem(1), nbr_y),
             [_remote(pass_x, pass_x, *sem(3), nbr_x), _remote(from_y, from_y, *sem(5), sib)]),
            (_remote(diag_0, diag_0, *sem(2), nbr_y), [_remote(diag_0, diag_0, *sem(6), sib)]),
            (_remote(diag_1, diag_1, *sem(3), nbr_x), [_remote(diag_1, diag_1, *sem(7), sib)]),
        ]
        other = (1 - c) * half
        from_sibling = [
            _remote(rows(2 * (1 - x) + y, other, half), rows(2 * (1 - x) + y, other, half), *sem(4), sib),
            _remote(rows(2 * x + (1 - y), other, half), rows(2 * x + (1 - y), other, half), *sem(5), sib),
            _remote(rows(diag, other, quarter), rows(diag, other, quarter), *sem(6), sib),
            _remote(rows(diag, other + quarter, quarter), rows(diag, other + quarter, quarter), *sem(7), sib),
        ]
        return first, arrivals, from_sibling

    def start(ins, outs, ssem, rsem):
        x, y, c, _ = _place()
        for w in range(n):
            half = outs[w].shape[1] // 2
            mine = outs[w].at[2 * x + y, pl.ds(c * half, half)]
            _remote(mine, mine, ssem.at[per * w], rsem.at[per * w], (1 - x, y, c)).start()
            _remote(mine, mine, ssem.at[per * w + 1], rsem.at[per * w + 1], (x, 1 - y, c)).start()

    def finish(ins, outs, ssem, rsem):
        plans = [plan(outs, ssem, rsem, w) for w in range(n)]
        started = []
        for direct in (True, False):
            for first, arrivals, _ in plans:
                for arrived, onward in (arrivals[:2] if direct else arrivals[2:]):
                    arrived.wait_recv()
                    for cp in onward:
                        cp.start()
                    started += onward
        for first, _, from_sibling in plans:
            for cp in from_sibling:
                cp.wait_recv()
            started += first
        for cp in started:
            cp.wait_send()

    return Exchange(bufs, [_sds(b.shape, b.dtype) for b in bufs], {w: w for w in range(n)}, per * n, start, finish)


def _ex_gather_direct(bufs):
    n = len(bufs)

    def copies(outs, ssem, rsem, only_first=False):
        x, y, c, chips = _place()
        me, sib = 2 * x + y, (x, y, 1 - c)
        first, relay, last = [], [], []
        for w in range(n):
            half = outs[w].shape[1] // 2
            mine = outs[w].at[me, pl.ds(c * half, half)]
            for k, (px, py) in enumerate(chips):
                sems = (ssem.at[6 * w + k], rsem.at[6 * w + k])
                sib_sems = (ssem.at[6 * w + 3 + k], rsem.at[6 * w + 3 + k])
                first.append(_remote(mine, mine, *sems, (px, py, c)))
                if only_first:
                    continue
                got = outs[w].at[2 * px + py, pl.ds(c * half, half)]
                relay.append((_remote(got, got, *sems, (px, py, c)), _remote(got, got, *sib_sems, sib)))
                theirs = outs[w].at[2 * px + py, pl.ds((1 - c) * half, half)]
                last.append(_remote(theirs, theirs, *sib_sems, sib))
        return first, relay, last

    def start(ins, outs, ssem, rsem):
        for cp in copies(outs, ssem, rsem, only_first=True)[0]:
            cp.start()

    def finish(ins, outs, ssem, rsem):
        first, relay, last = copies(outs, ssem, rsem)
        for arrived, onward in relay:
            arrived.wait_recv()
            onward.start()
        for cp in last:
            cp.wait_recv()
        for cp in first:
            cp.wait_send()
        for _, onward in relay:
            onward.wait_send()

    return Exchange(bufs, [_sds(b.shape, b.dtype) for b in bufs], {w: w for w in range(n)}, 6 * n, start, finish)


def _simple_exchange(arrays, landing, aliases, make_copies):
    def start(ins, outs, ssem, rsem):
        for cp, _ in make_copies(ins, outs, ssem, rsem, False):
            cp.start()

    def finish(ins, outs, ssem, rsem):
        cps = make_copies(ins, outs, ssem, rsem, True)
        for _, landed in cps:
            landed.wait_recv()
        for cp, _ in cps:
            cp.wait_send()

    return Exchange(arrays, landing, aliases, len(arrays) * 3, start, finish)


def _ex_pair_swap(grads):
    def make(ins, outs, ssem, rsem, landing):
        x, y, c, _ = _place()
        cps = [_remote(ins[w].at[:, 1 - c], outs[w], ssem.at[w], rsem.at[w], (x, y, 1 - c))
               for w in range(len(grads))]
        return [(cp, cp) for cp in cps]

    return _simple_exchange(grads, [_sds((NSH,) + g.shape[2:], g.dtype) for g in grads], {}, make)


def _ex_scatter(parts):
    def make(ins, outs, ssem, rsem, landing):
        x, y, c, chips = _place()
        out = []
        for w in range(len(parts)):
            for k, (px, py) in enumerate(chips):
                sems = (ssem.at[3 * w + k], rsem.at[3 * w + k])
                out.append((_remote(ins[w].at[2 * px + py], outs[w].at[k], *sems, (px, py, c)),
                            _remote(outs[w].at[k], outs[w].at[k], *sems, (px, py, c)) if landing else None))
        return out

    return _simple_exchange(parts, [_sds((3,) + p.shape[1:], p.dtype) for p in parts], {}, make)


def _ex_share(bufs):
    def make(ins, outs, ssem, rsem, landing):
        x, y, c, _ = _place()
        sib = (x, y, 1 - c)
        return [(_remote(outs[w].at[c], outs[w].at[c], ssem.at[w], rsem.at[w], sib),
                 _remote(outs[w].at[1 - c], outs[w].at[1 - c], ssem.at[w], rsem.at[w], sib) if landing else None)
                for w in range(len(bufs))]

    return _simple_exchange(bufs, [_sds(b.shape, b.dtype) for b in bufs], {w: w for w in range(len(bufs))}, make)


def _gather_small(block):
    m_per, n = block.shape

    def body(x_ref, out_ref, ssem, rsem, lsem):
        x, y, c, chips = _place()
        me, sib = (x, y, c), (x, y, 1 - c)

        def rows(px, py, pc):
            return out_ref.at[pl.ds((4 * px + 2 * py + pc) * m_per, m_per), :]

        def copy(k, blk, to, src=None):
            return _remote(rows(*blk) if src is None else src, rows(*blk), ssem.at[k], rsem.at[k], to)

        mine = pltpu.make_async_copy(x_ref, rows(*me), lsem)
        mine.start()
        first = [copy(0, me, sib, src=x_ref)]
        first += [copy(1 + j, me, (*chip, c), src=x_ref) for j, chip in enumerate(chips)]
        for cp in first:
            cp.start()
        passed = [copy(4 + j, (*chip, c), sib) for j, chip in enumerate(chips)]
        for j, chip in enumerate(chips):
            copy(1 + j, (*chip, c), me).wait_recv()
            passed[j].start()
        copy(0, sib, me).wait_recv()
        for j, chip in enumerate(chips):
            copy(4 + j, (*chip, 1 - c), me).wait_recv()
        for cp in first + passed:
            cp.wait_send()
        mine.wait()

    return pl.pallas_call(
        body, name="gather_small", out_shape=jax.ShapeDtypeStruct((8 * m_per, n), block.dtype),
        in_specs=[_VM], out_specs=_VM,
        scratch_shapes=[pltpu.SemaphoreType.DMA((7,)), pltpu.SemaphoreType.DMA((7,)), pltpu.SemaphoreType.DMA],
    )(block)


def _row_block(rows):
    return max(t for t in range(16, 257, 16) if rows % t == 0)


def _pair_sum(grad, got, c_idx, name):
    _, _, half, cols = grad.shape
    tr = _row_block(half)

    def body(c_ref, a_ref, b_ref, o_ref):
        o_ref[...] = (a_ref[...].astype(F32) + b_ref[...].astype(F32)).astype(BF16)

    return pl.pallas_call(
        body, name=name, out_shape=_sds((NSH, half, cols), BF16),
        grid_spec=pltpu.PrefetchScalarGridSpec(
            num_scalar_prefetch=1, grid=(NSH, half // tr),
            in_specs=[pl.BlockSpec((None, None, tr, cols), lambda j, r, c: (j, c[0], r, 0)),
                      pl.BlockSpec((None, tr, cols), lambda j, r, c: (j, r, 0))],
            out_specs=pl.BlockSpec((None, tr, cols), lambda j, r, c: (j, r, 0))),
        compiler_params=_params(("arbitrary", "arbitrary"), 32),
    )(c_idx, *_in_hbm([grad, got]))


def _chip_sum(own, got, place, name):
    _, half, cols = own.shape
    tr = _row_block(half)

    def body(place_ref, own_ref, got_ref, o_ref):
        acc = own_ref[...].astype(F32)
        for k in range(3):
            acc = acc + got_ref[k].astype(F32)
        o_ref[...] = acc

    return pl.pallas_call(
        body, name=name, out_shape=_sds((2, half, cols), F32),
        grid_spec=pltpu.PrefetchScalarGridSpec(
            num_scalar_prefetch=1, grid=(half // tr,),
            in_specs=[pl.BlockSpec((None, tr, cols), lambda r, p: (p[0], r, 0)),
                      pl.BlockSpec((3, tr, cols), lambda r, p: (0, r, 0))],
            out_specs=pl.BlockSpec((None, tr, cols), lambda r, p: (p[1], r, 0))),
        compiler_params=_params(("arbitrary",), 32),
    )(place, *_in_hbm([own, got]))


def _adamw_math(w, g, m, v):
    m = B1 * m + (1.0 - B1) * g
    v = B2 * v + (1.0 - B2) * (g * g)
    m_hat = m / (1.0 - B1 ** STEP)
    v_hat = v / (1.0 - B2 ** STEP)
    return -LR * (m_hat / (jnp.sqrt(v_hat) + AEPS) + WD * w), m, v


def _adamw(w, g, m, v, name, exchange=None):
    rows, cols = w.shape
    tr = _row_block(rows)

    def body(w_ref, g_ref, m_ref, v_ref, go_ref, d_ref, nm_ref, nv_ref):
        g = g_ref[...]
        go_ref[...] = g
        d_ref[...], nm_ref[...], nv_ref[...] = _adamw_math(w_ref[...], g, m_ref[...], v_ref[...])

    blk = pl.BlockSpec((tr, cols), lambda r: (r, 0))
    return _call(
        body, (w, g, m, v), name=name, grid=(rows // tr,), out_shape=[_sds(w.shape, F32)] * 4,
        in_specs=[blk] * 4, out_specs=[blk] * 4,
        compiler_params=_params(("arbitrary",), 32), exchange=exchange, free=(0, 2, 3))


def _small_update(gathered, w, m, v):
    rows = w.shape[0]

    def body(ga_ref, w_ref, m_ref, v_ref, g_ref, d_ref, nm_ref, nv_ref):
        g = ga_ref[0:rows, :]
        for dev in range(1, 8):
            g = g + ga_ref[dev * rows:(dev + 1) * rows, :]
        g_ref[...] = g
        d_ref[...], nm_ref[...], nv_ref[...] = _adamw_math(w_ref[...], g, m_ref[...], v_ref[...])

    return pl.pallas_call(
        body, name="small_update", out_shape=[jax.ShapeDtypeStruct(w.shape, F32)] * 4,
        in_specs=[_VM] * 4, out_specs=[_VM] * 4,
    )(gathered, w, m, v)


SMALL = ("ffn1_norm", "mix_norm", "ffn2_norm", "final_norm", "pool_scale", "pool_w_group", "loss")
BIG = ("ffn1_w_gate_up", "ffn1_w_down", "w_in", "w_branch_pool", "w_branch_attn", "w_out",
       "ffn2_w_gate_up", "ffn2_w_down")
ORDER = ("ffn1_norm", "ffn1_w_gate_up", "ffn1_w_down", "mix_norm", "w_in", "pool_w_group", "pool_scale",
         "w_branch_pool", "w_branch_attn", "w_out", "ffn2_norm", "ffn2_w_gate_up", "ffn2_w_down", "final_norm")
SMALL_ROWS = 560


def _pack_small(t):
    parts = []
    for k in SMALL:
        rows = t[k].reshape(-1, 128) if k in t else jnp.zeros((1, 128), F32)
        parts.append(jnp.pad(rows, ((0, -rows.shape[0] % 8), (0, 0))))
    packed = jnp.concatenate(parts, axis=0)
    assert packed.shape == (SMALL_ROWS, 128), packed.shape
    return packed


def _unpack_small(packed, like):
    out, at = {}, 0
    for k in SMALL:
        n = like[k].size // 128 if k in like else 1
        out[k] = packed[at:at + n].reshape(like[k].shape) if k in like else packed[at, 0]
        at += n + (-n % 8)
    return out


def _halves(g):
    return g.reshape(NSH, 2, g.shape[1] // 2, g.shape[2])


def kernel(x, ffn1_norm, ffn1_w_gate_up, ffn1_w_down, mix_norm, w_in, pool_w_group, pool_scale, w_branch_pool, w_branch_attn, w_out, ffn2_norm, ffn2_w_gate_up, ffn2_w_down, final_norm, loss_target, m_ffn1_norm, m_ffn1_w_gate_up, m_ffn1_w_down, m_mix_norm, m_w_in, m_pool_w_group, m_pool_scale, m_w_branch_pool, m_w_branch_attn, m_w_out, m_ffn2_norm, m_ffn2_w_gate_up, m_ffn2_w_down, m_final_norm, v_ffn1_norm, v_ffn1_w_gate_up, v_ffn1_w_down, v_mix_norm, v_w_in, v_pool_w_group, v_pool_scale, v_w_branch_pool, v_w_branch_attn, v_w_out, v_ffn2_norm, v_ffn2_w_gate_up, v_ffn2_w_down, v_final_norm):
    wts = dict(ffn1_norm=ffn1_norm, ffn1_w_gate_up=ffn1_w_gate_up, ffn1_w_down=ffn1_w_down, mix_norm=mix_norm,
               w_in=w_in, pool_w_group=pool_w_group, pool_scale=pool_scale, w_branch_pool=w_branch_pool,
               w_branch_attn=w_branch_attn, w_out=w_out, ffn2_norm=ffn2_norm, ffn2_w_gate_up=ffn2_w_gate_up,
               ffn2_w_down=ffn2_w_down, final_norm=final_norm)
    mom = dict(ffn1_norm=m_ffn1_norm, ffn1_w_gate_up=m_ffn1_w_gate_up, ffn1_w_down=m_ffn1_w_down,
               mix_norm=m_mix_norm, w_in=m_w_in, pool_w_group=m_pool_w_group, pool_scale=m_pool_scale,
               w_branch_pool=m_w_branch_pool, w_branch_attn=m_w_branch_attn, w_out=m_w_out,
               ffn2_norm=m_ffn2_norm, ffn2_w_gate_up=m_ffn2_w_gate_up, ffn2_w_down=m_ffn2_w_down,
               final_norm=m_final_norm)
    var = dict(ffn1_norm=v_ffn1_norm, ffn1_w_gate_up=v_ffn1_w_gate_up, ffn1_w_down=v_ffn1_w_down,
               mix_norm=v_mix_norm, w_in=v_w_in, pool_w_group=v_pool_w_group, pool_scale=v_pool_scale,
               w_branch_pool=v_w_branch_pool, w_branch_attn=v_w_branch_attn, w_out=v_w_out,
               ffn2_norm=v_ffn2_norm, ffn2_w_gate_up=v_ffn2_w_gate_up, ffn2_w_down=v_ffn2_w_down,
               final_norm=v_final_norm)

    c_idx = lax.axis_index("c").astype(jnp.int32).reshape(1)
    me_idx = (2 * lax.axis_index("x") + lax.axis_index("y")).astype(jnp.int32).reshape(1)
    place = jnp.concatenate([me_idx, c_idx])
    x0, tgt = x[0], loss_target[0]
    wgrp = pool_w_group[0].astype(BF16)
    g1, gm, g2, gf = ffn1_norm, mix_norm, ffn2_norm, final_norm.reshape(1, D)
    grad, delta, new_m, new_v = {}, {}, {}, {}

    def pair_sums(keys, parts, got):
        return [_pair_sum(parts[i], got[i], c_idx, "pair_sum_" + k) for i, k in enumerate(keys)]

    def chip_sums(keys, chip_parts, owned):
        return [_chip_sum(chip_parts[i], owned[i], place, "chip_sum_" + k) for i, k in enumerate(keys)]

    def adamw(k, exchange=None):
        res = _adamw(wts[k][0], grad[k][0], mom[k][0], var[k][0], "adamw_" + k, exchange=exchange)
        outs, landed = (res, None) if exchange is None else res
        grad[k], delta[k], new_m[k], new_v[k] = (o.reshape(wts[k].shape) for o in outs)
        return landed

    own = {k: _cast_into_block(wts[k][0], me_idx, "cast_" + k) for k in BIG}
    first, late = ("ffn1_w_gate_up", "ffn1_w_down"), ("w_branch_pool", "w_branch_attn", "w_out",
                                                       "ffn2_w_gate_up", "ffn2_w_down")
    full = dict(zip(first, _exchange_alone(_ex_gather([own[k] for k in first]), "gather_ffn1")))
    wgu1, wd1 = full["ffn1_w_gate_up"], full["ffn1_w_down"].reshape(DFF, D)
    (h1, n1, gu1, a1), (win,) = _ffn_fwd(x0, g1, wgu1, wd1, "ffn1_fwd", exchange=_ex_gather_direct([own["w_in"]]))
    u, xp, q, k, v, gp, gs = _mix_in(h1, gm, win)
    (o_sb, ctot), landed = _attn_fwd(q, k, v, exchange=_ex_gather_direct([own[k_] for k_ in late]))
    full.update(zip(late, landed))
    wbp, wba, wout = full["w_branch_pool"], full["w_branch_attn"], full["w_out"].reshape(D, D)
    wgu2, wd2 = full["ffn2_w_gate_up"], full["ffn2_w_down"].reshape(DFF, D)
    h2, pm, p, yp, ys, mm = _mix_out(h1, xp, o_sb, gp, gs, wgrp, pool_scale, wbp, wba, wout)
    h3, n3, gu3, a3 = _ffn_fwd(h2, g2, wgu2, wd2, "ffn2_fwd")
    dh3, loss_row, d_gf = _head(h3, tgt, gf)

    def grad_gate_up(n, dgu, name, exchange=None):
        res = _wgrad(n, dgu, NSH, 512, name, exchange=exchange)
        return [_halves(res)] if exchange is None else ([_halves(res[0])], res[1])

    def grad_down(a, dh, name, exchange=None):
        res = _wgrad(a, dh, 1, FFS, name, exchange=exchange)
        halves = lambda g: [_halves(g.reshape(NSH, DFF // NSH, D))]
        return halves(res) if exchange is None else (halves(res[0]), res[1])

    k_gu2, k_d2, k_gu1, k_d1, k_in = (("ffn2_w_gate_up",), ("ffn2_w_down",), ("ffn1_w_gate_up",),
                                      ("ffn1_w_down",), ("w_in",))
    dgu3 = _ffn_bwd_act(dh3, gu3, wd2, "ffn2_bwd_act")
    pa = grad_gate_up(n3, dgu3, "wgrad_gu2") + grad_down(a3, dh3, "wgrad_d2")
    (dh2, d_g2), got_a = _ffn_bwd_in(dh3, h2, g2, dgu3, wgu2, "ffn2_bwd_in", exchange=_ex_pair_swap(pa))
    chip_a = pair_sums(k_gu2 + k_d2, pa, got_a)
    dlg, dyp, dys, do_sb, dyg, dxp, d_scale = _mix_bwd_out(dh2, gp, gs, yp, ys, pm, wgrp, pool_scale, wbp, wba, wout)
    kb = ("w_out", "w_branch_pool", "w_branch_attn")
    pb = [_halves(_wgrad(mm, dh2, 1, 512, "wgrad_out").reshape(NSH, D // NSH, D)),
          _halves(_wgrad(p, dyp, NSH, PW, "wgrad_bp")), _halves(_wgrad(o_sb, dys, NSH, SBW, "wgrad_ba"))]
    chip_b = pair_sums(kb, pb, _exchange_alone(_ex_pair_swap(pb), "pair_swap_mix"))
    k_ab = k_gu2 + k_d2 + kb
    (dq, dk, dv), owned_ab = _attn_bwd(q, k, v, do_sb, ctot, exchange=_ex_scatter(chip_a + chip_b))
    halves_ab = chip_sums(k_ab, chip_a + chip_b, owned_ab)
    (dh1, d_gm, dproj), both_ab = _mix_bwd_in(dh2, h1, gm, (dxp, dq, dk, dv, dlg), win, exchange=_ex_share(halves_ab))
    for i, k_ in enumerate(k_ab):
        grad[k_] = both_ab[i].reshape(wts[k_].shape)

    p_in = [_halves(_wgrad(u, dproj, NSH, 512, "wgrad_in"))]
    p_d1, got_in = grad_down(a1, dh1, "wgrad_d1", exchange=_ex_pair_swap(p_in))
    chip_in = pair_sums(k_in, p_in, got_in)
    dgu1, landed = _ffn_bwd_act(dh1, gu1, wd1, "ffn1_bwd_act",
                                exchange=_join(_ex_scatter(chip_in), _ex_pair_swap(p_d1)))
    owned_in, got_d1 = landed[:1], landed[1:]
    chip_d1 = pair_sums(k_d1, p_d1, got_d1)
    halves_in = chip_sums(k_in, chip_in, owned_in)
    p_gu1, landed = grad_gate_up(n1, dgu1, "wgrad_gu1", exchange=_join(_ex_scatter(chip_d1), _ex_share(halves_in)))
    owned_d1, both_in = landed[:1], landed[1:]
    grad["w_in"] = both_in[0].reshape(w_in.shape)
    halves_d1 = chip_sums(k_d1, chip_d1, owned_d1)
    chip_gu1 = pair_sums(k_gu1, p_gu1, _exchange_alone(_ex_pair_swap(p_gu1), "pair_swap_gu1"))
    (dx, d_g1), landed = _ffn_bwd_in(dh1, x0, g1, dgu1, wgu1, "ffn1_bwd_in",
                                     exchange=_join(_ex_scatter(chip_gu1), _ex_share(halves_d1)))
    owned_gu1, both_d1 = landed[:1], landed[1:]
    grad["ffn1_w_down"] = both_d1[0].reshape(ffn1_w_down.shape)
    both_gu1 = _exchange_alone(_ex_share(chip_sums(k_gu1, chip_gu1, owned_gu1)), "share_gu1")
    grad["ffn1_w_gate_up"] = both_gu1[0].reshape(ffn1_w_gate_up.shape)
    for k_ in BIG:
        adamw(k_)

    small_g = dict(ffn1_norm=d_g1, mix_norm=d_gm, ffn2_norm=d_g2, final_norm=d_gf, pool_scale=d_scale,
                   pool_w_group=_wgrad_groups(pm, dyg), loss=loss_row)
    gathered = _gather_small(_pack_small(small_g))
    sg, sd, sm, sv = _small_update(gathered, _pack_small(wts), _pack_small(mom), _pack_small(var))
    sums = _unpack_small(sg, wts)
    loss = sums.pop("loss")
    grad.update(sums)
    for dst, packed in ((delta, sd), (new_m, sm), (new_v, sv)):
        vals = _unpack_small(packed, wts)
        vals.pop("loss")
        dst.update(vals)
    return (loss, dx[None], *[grad[k_] for k_ in ORDER], *[delta[k_] for k_ in ORDER],
            *[new_m[k_] for k_ in ORDER], *[new_v[k_] for k_ in ORDER])
```

```python
import functools

import jax
import jax.numpy as jnp
from jax import lax
from jax.experimental import pallas as pl
from jax.experimental.pallas import tpu as pltpu

F32 = jnp.float32
BF16 = jnp.bfloat16

S = 2048
D = 1024
DFF = 2816
FFS = 2 * DFF // 4
NSH = 4
PW = 512
PG = 128
POOL_WINDOWS = (2, 4, 8, 16)
HALO = 16
SBW = 512
DH = 64
EPS = 1e-6
SCALE = 0.125
LOG2E = 1.4426950408889634
TA = 256
QB = 2
MIB = 1024 * 1024

LR, B1, B2, AEPS, WD, STEP = 0.001, 0.9, 0.999, 1e-08, 0.01, 10

_VM = pl.BlockSpec(memory_space=pltpu.VMEM)
_ANY = pl.BlockSpec(memory_space=pl.ANY)
MESH = pl.DeviceIdType.MESH


def _nn(a, b):
    return jnp.dot(a, b, preferred_element_type=F32)


def _nt(a, b):
    return lax.dot_general(a, b, (((1,), (1,)), ((), ())), preferred_element_type=F32)


def _tn(a, b):
    return lax.dot_general(a, b, (((0,), (0,)), ((), ())), preferred_element_type=F32)


def _params(sem, vmem_mib):
    return pltpu.CompilerParams(dimension_semantics=sem, vmem_limit_bytes=vmem_mib * MIB)


def _rows(tm, width):
    return pl.BlockSpec((tm, width), lambda i: (i, 0))


def _fixed(shape):
    return pl.BlockSpec(shape, lambda *_: (0,) * len(shape))


def _sds(shape, dtype):
    return pltpu.HBM(shape, dtype)


def _in_hbm(args):
    return [pltpu.with_memory_space_constraint(a, pltpu.HBM) for a in args]


def _stage(pairs):
    @pl.when(pl.program_id(0) == 0)
    def _():
        for src, dst in pairs:
            pltpu.sync_copy(src, dst)


def _vmem_like(*arrays):
    return [pltpu.VMEM(a.shape, a.dtype) for a in arrays]


class Exchange:
    def __init__(self, arrays, landing, aliases, n_sems, start, finish):
        self.arrays, self.landing, self.aliases, self.n_sems = list(arrays), list(landing), dict(aliases), n_sems
        self.start, self.finish = start, finish


def _join(a, b):
    na, la = len(a.arrays), len(a.landing)

    def both(fa, fb):
        def run(ins, outs, ssem, rsem):
            fa(ins[:na], outs[:la], ssem.at[pl.ds(0, a.n_sems)], rsem.at[pl.ds(0, a.n_sems)])
            fb(ins[na:], outs[la:], ssem.at[pl.ds(a.n_sems, b.n_sems)], rsem.at[pl.ds(a.n_sems, b.n_sems)])
        return run

    aliases = {**a.aliases, **{na + i: la + j for i, j in b.aliases.items()}}
    return Exchange(a.arrays + b.arrays, a.landing + b.landing, aliases, a.n_sems + b.n_sems,
                    both(a.start, b.start), both(a.finish, b.finish))


def _call(body, args, *, name, grid, in_specs, out_specs, out_shape, scratch_shapes=(), compiler_params=None,
          exchange=None, free=(), after=()):
    args = [a if i in free else pltpu.with_memory_space_constraint(a, pltpu.HBM) for i, a in enumerate(args)]
    if exchange is None:
        n_in = len(in_specs)

        def plain(*refs):
            body(*refs[:n_in], *refs[n_in + len(after):])

        return pl.pallas_call(plain, name=name, grid=grid, in_specs=list(in_specs) + [_ANY] * len(after),
                              out_specs=out_specs, out_shape=out_shape, scratch_shapes=list(scratch_shapes),
                              compiler_params=compiler_params)(*args, *after)
    assert not after
    ex = exchange
    n_in, n_out, n_scr = len(in_specs), len(out_specs), len(scratch_shapes)
    na, nl = len(ex.arrays), len(ex.landing)

    def hosted(*refs):
        at = [0]

        def take(n):
            at[0] += n
            return refs[at[0] - n:at[0]]

        k_in, e_in, k_out, e_out, k_scr = take(n_in), take(na), take(n_out), take(nl), take(n_scr)
        ssem, rsem = take(2)
        ids = [pl.program_id(a) for a in range(len(grid))]
        first = functools.reduce(jnp.logical_and, [i == 0 for i in ids])
        last = functools.reduce(jnp.logical_and, [i == g - 1 for i, g in zip(ids, grid)])

        @pl.when(first)
        def _():
            ex.start(e_in, e_out, ssem, rsem)

        body(*k_in, *k_out, *k_scr)

        @pl.when(last)
        def _():
            ex.finish(e_in, e_out, ssem, rsem)

    outs = pl.pallas_call(
        hosted, name=name, grid=grid,
        in_specs=list(in_specs) + [_ANY] * na, out_specs=list(out_specs) + [_ANY] * nl,
        out_shape=list(out_shape) + ex.landing,
        scratch_shapes=list(scratch_shapes) + [pltpu.SemaphoreType.DMA((ex.n_sems,))] * 2,
        input_output_aliases={n_in + i: n_out + j for i, j in ex.aliases.items()},
        compiler_params=compiler_params,
    )(*args, *_in_hbm(ex.arrays))
    return outs[:n_out], outs[n_out:]


def _exchange_alone(ex, name):
    def body(*refs):
        na, nl = len(ex.arrays), len(ex.landing)
        ex.start(refs[:na], refs[na:na + nl], refs[-2], refs[-1])
        ex.finish(refs[:na], refs[na:na + nl], refs[-2], refs[-1])

    return pl.pallas_call(
        body, name=name, in_specs=[_ANY] * len(ex.arrays), out_specs=[_ANY] * len(ex.landing),
        out_shape=ex.landing, scratch_shapes=[pltpu.SemaphoreType.DMA((ex.n_sems,))] * 2,
        input_output_aliases=ex.aliases,
    )(*_in_hbm(ex.arrays))


_HBM = pl.BlockSpec(memory_space=pltpu.HBM)
_SEM = pl.BlockSpec(memory_space=pltpu.SEMAPHORE)
_EFFECT = pltpu.SideEffectType.DATAFLOW_SIDE_EFFECTING


def _scatter_copies(srcs, lands, ssems, rsems):
    x, y, c, chips = _place()
    return [_remote(srcs[w].at[2 * px + py], lands[w].at[k], ssems[3 * w + k], rsems[3 * w + k], (px, py, c))
            for w in range(len(srcs)) for k, (px, py) in enumerate(chips)]


def _scatter_start(parts, name):
    n, ncp = len(parts), 3 * len(parts)
    lands = [lax.empty((3,) + p.shape[1:], p.dtype) for p in parts]

    def body(*refs):
        srcs, land_refs = refs[:n], refs[n:2 * n]
        ssems, rsems = refs[2 * n:2 * n + ncp], refs[2 * n + ncp:2 * n + 2 * ncp]
        for cp in _scatter_copies(srcs, land_refs, ssems, rsems):
            cp.start()
        token = refs[-1]
        token[...] = jnp.zeros_like(token)

    outs = pl.pallas_call(
        body, name=name,
        out_shape=([pltpu.SemaphoreType.DMA(())] * (2 * ncp) + [pltpu.HBM(a.shape, a.dtype) for a in parts + lands]
                   + [jax.ShapeDtypeStruct((8, 128), F32)]),
        in_specs=[_HBM] * (2 * n), out_specs=[_SEM] * (2 * ncp) + [_HBM] * (2 * n) + [_VM],
        input_output_aliases={i: 2 * ncp + i for i in range(2 * n)},
        compiler_params=pltpu.CompilerParams(has_side_effects=_EFFECT),
    )(*_in_hbm(parts), *_in_hbm(lands))
    sems, thru, token = outs[:2 * ncp], outs[2 * ncp:2 * ncp + 2 * n], outs[-1]
    return sems, thru, token


def _scatter_wait(sems, thru, after, name):
    n = len(thru) // 2
    ncp = 3 * n

    def body(*refs):
        srcs, land_refs = refs[:n], refs[n:2 * n]
        ssems, rsems = refs[2 * n:2 * n + ncp], refs[2 * n + ncp:2 * n + 2 * ncp]
        for cp in _scatter_copies(srcs, land_refs, ssems, rsems):
            cp.wait_send()
            cp.wait_recv()

    outs = pl.pallas_call(
        body, name=name, out_shape=[pltpu.HBM(a.shape, a.dtype) for a in thru],
        in_specs=[_HBM] * (2 * n) + [_SEM] * (2 * ncp) + [_ANY] * len(after), out_specs=[_HBM] * (2 * n),
        input_output_aliases={i: i for i in range(2 * n)},
        compiler_params=pltpu.CompilerParams(has_side_effects=_EFFECT),
    )(*thru, *sems, *after)
    return outs[n:]


def _rms(x):
    r = lax.rsqrt(jnp.mean(x * x, axis=-1, keepdims=True) + EPS)
    return r, x * r


def _rms_bwd(dn, xr, r, gain):
    dng = dn * gain
    dx = r * (dng - xr * jnp.mean(dng * xr, axis=-1, keepdims=True))
    return dx, jnp.sum(dn * xr, axis=0, keepdims=True)


def _ffn_fwd(x, gain, wgu, wd, name, exchange=None):
    tm = 256

    def body(x_ref, g_ref, wgu_hbm, wd_hbm, h_ref, n_ref, gu_ref, a_ref, wgu_ref, wd_ref):
        _stage([(wgu_hbm, wgu_ref), (wd_hbm, wd_ref)])
        x = x_ref[...]
        _, xr = _rms(x)
        n = (xr * g_ref[...]).astype(BF16)
        n_ref[...] = n
        acc = jnp.zeros((tm, D), F32)
        for j in range(2):
            g = _nn(n, wgu_ref[j])
            u = _nn(n, wgu_ref[2 + j])
            gu_ref[:, j * FFS:(j + 1) * FFS] = g.astype(BF16)
            gu_ref[:, (2 + j) * FFS:(3 + j) * FFS] = u.astype(BF16)
            half_act = (0.5 * (g * jax.nn.sigmoid(g) * u)).astype(BF16)
            a_ref[:, j * FFS:(j + 1) * FFS] = half_act
            acc = acc + _nn(half_act, wd_ref[j * FFS:(j + 1) * FFS, :])
        h_ref[...] = x + acc

    return _call(
        body, (x, gain, wgu, wd), name=name, grid=(S // tm,),
        in_specs=[_rows(tm, D), _fixed((1, D)), _ANY, _ANY],
        out_specs=[_rows(tm, D), _rows(tm, D), _rows(tm, 4 * FFS), _rows(tm, DFF)],
        out_shape=[_sds((S, D), F32), _sds((S, D), BF16), _sds((S, 4 * FFS), BF16), _sds((S, DFF), BF16)],
        scratch_shapes=_vmem_like(wgu, wd),
        compiler_params=_params(("arbitrary",), 56), exchange=exchange)


def _ffn_bwd_act(dh, gu, wd, name, exchange=None):
    tm = 512

    def body(dh_ref, gu_ref, wd_hbm, dgu_ref, wd_ref):
        _stage([(wd_hbm, wd_ref)])
        dhb = dh_ref[...].astype(BF16)
        for j in range(2):
            g = gu_ref[:, j * FFS:(j + 1) * FFS].astype(F32)
            u = gu_ref[:, (2 + j) * FFS:(3 + j) * FFS].astype(F32)
            da = 0.5 * _nt(dhb, wd_ref[j * FFS:(j + 1) * FFS, :])
            sg = jax.nn.sigmoid(g)
            dgu_ref[:, j * FFS:(j + 1) * FFS] = (da * u * (sg * (1.0 + g * (1.0 - sg)))).astype(BF16)
            dgu_ref[:, (2 + j) * FFS:(3 + j) * FFS] = (da * (g * sg)).astype(BF16)

    res = _call(
        body, (dh, gu, wd), name=name, grid=(S // tm,),
        in_specs=[_rows(tm, D), _rows(tm, 4 * FFS), _ANY], out_specs=[_rows(tm, 4 * FFS)],
        out_shape=[_sds((S, 4 * FFS), BF16)], scratch_shapes=_vmem_like(wd),
        compiler_params=_params(("arbitrary",), 56), exchange=exchange)
    return res[0] if exchange is None else (res[0][0], res[1])


def _ffn_bwd_in(dh, x, gain, dgu, wgu, name, exchange=None, after=()):
    tm = 512

    def body(dh_ref, x_ref, g_ref, dgu_ref, wgu_hbm, dx_ref, dg_ref, wgu_ref):
        _stage([(wgu_hbm, wgu_ref)])
        dn = jnp.zeros((tm, D), F32)
        for j in range(NSH):
            dn = dn + _nt(dgu_ref[:, j * FFS:(j + 1) * FFS], wgu_ref[j])
        r, xr = _rms(x_ref[...])
        dx, dgain = _rms_bwd(dn, xr, r, g_ref[...])
        dx_ref[...] = dh_ref[...] + dx

        @pl.when(pl.program_id(0) == 0)
        def _():
            dg_ref[...] = jnp.zeros_like(dg_ref)

        dg_ref[...] += dgain

    return _call(
        body, (dh, x, gain, dgu, wgu), name=name, grid=(S // tm,),
        in_specs=[_rows(tm, D), _rows(tm, D), _fixed((1, D)), _rows(tm, 4 * FFS), _ANY],
        out_specs=[_rows(tm, D), _fixed((1, D))],
        out_shape=[_sds((S, D), F32), _sds((1, D), F32)], scratch_shapes=_vmem_like(wgu),
        compiler_params=_params(("arbitrary",), 56), exchange=exchange, after=after)


def _head(h, target, gain):
    tm = 512

    def body(h_ref, t_ref, g_ref, dh_ref, loss_ref, dg_ref):
        gain = g_ref[...]
        r, hr = _rms(h_ref[...])
        err = hr * gain - t_ref[...]
        dy = err * (1.0 / D)
        dh, dgain = _rms_bwd(dy, hr, r, gain)
        dh_ref[...] = dh

        @pl.when(pl.program_id(0) == 0)
        def _():
            dg_ref[...] = jnp.zeros_like(dg_ref)
            loss_ref[...] = jnp.zeros_like(loss_ref)

        dg_ref[...] += dgain
        loss_ref[...] += jnp.full((1, 128), (0.5 / D) * jnp.sum(err * err), F32)

    return pl.pallas_call(
        body, name="head", grid=(S // tm,),
        in_specs=[_rows(tm, D), _rows(tm, D), _fixed((1, D))],
        out_specs=[_rows(tm, D), _fixed((1, 128)), _fixed((1, D))],
        out_shape=[_sds((S, D), F32), _sds((1, 128), F32), _sds((1, D), F32)],
        compiler_params=_params(("arbitrary",), 40),
    )(*_in_hbm([h]), target, gain)


def _mix_in(h, gain, w_in):
    tm = 512

    def body(h_ref, g_ref, w_hbm, u_ref, xp_ref, q_ref, k_ref, v_ref, gp_ref, gs_ref, w_ref):
        _stage([(w_hbm, w_ref)])
        _, hr = _rms(h_ref[...])
        u = (hr * g_ref[...]).astype(BF16)
        u_ref[...] = u
        p0 = _nn(u, w_ref[0])
        xp_ref[...] = p0[:, :PW]
        q_ref[...] = p0[:, PW:].astype(BF16)
        p1 = _nn(u, w_ref[1])
        k_ref[...] = p1[:, :SBW].astype(BF16)
        v_ref[...] = p1[:, SBW:].astype(BF16)
        gp_ref[...] = jax.nn.sigmoid(_nn(u, w_ref[2])).astype(BF16)
        gs_ref[...] = jax.nn.sigmoid(_nn(u, w_ref[3])).astype(BF16)

    return pl.pallas_call(
        body, name="mix_in", grid=(S // tm,),
        in_specs=[_rows(tm, D), _fixed((1, D)), _ANY],
        out_specs=[_rows(tm, D), _rows(tm, PW), _rows(tm, SBW), _rows(tm, SBW), _rows(tm, SBW),
                   _rows(tm, D), _rows(tm, D)],
        out_shape=[_sds((S, D), BF16), _sds((S, PW), F32), _sds((S, SBW), BF16), _sds((S, SBW), BF16),
                   _sds((S, SBW), BF16), _sds((S, D), BF16), _sds((S, D), BF16)],
        scratch_shapes=_vmem_like(w_in),
        compiler_params=_params(("arbitrary",), 48),
    )(*_in_hbm([h]), gain, *_in_hbm([w_in]))


def _hilo_dot(x, tri):
    hi = x.astype(BF16)
    lo = (x - hi.astype(F32)).astype(BF16)
    return _nn(hi, tri) + _nn(lo, tri)


def _log_terms(qk):
    z2 = qk * (SCALE * LOG2E)
    lb = jnp.minimum(z2, 0.0) - jnp.log2(1.0 + jnp.exp2(-jnp.abs(z2)))
    return lb, lb - z2


def _head_masks():
    lane = lax.broadcasted_iota(jnp.int32, (1, 2 * DH), 1)
    return (lane < DH, lane >= DH)


def _attn_fwd(q, k, v, exchange=None):
    T = TA

    def body(q_ref, k_ref, v_ref, o_ref, c_ref):
        i2 = 2 * pl.program_id(1)
        row = lax.broadcasted_iota(jnp.int32, (T, T), 0)
        col = lax.broadcasted_iota(jnp.int32, (T, T), 1)
        after = (row > col).astype(BF16)
        causal = col < row
        masks = _head_masks()
        qms = {}
        for b in range(QB):
            q2 = q_ref[b * T:(b + 1) * T, :]
            for h, hm in enumerate(masks):
                qms[b, h] = jnp.where(hm, q2, jnp.zeros_like(q2))

        def blocks(keys, pairs, carries, os):
            ks, vms = [], []
            for j in keys:
                rows = pl.ds(pl.multiple_of(j * T, T), T)
                vj = v_ref[rows, :]
                ks.append(k_ref[rows, :])
                vms.append([jnp.where(hm, vj, jnp.zeros_like(vj)) for hm in masks])
            units = [(n, h) for n in range(len(pairs)) for h in range(2)]
            qks = {(n, h): _nt(qms[pairs[n][0], h], ks[pairs[n][1]]) for n, h in units}
            lbs, l1ms = {}, {}
            for u in units:
                lbs[u], l1m = _log_terms(qks[u])
                l1ms[u] = jnp.where(causal, l1m, 0.0) if pairs[u[0]][2] else l1m
            cins = {u: _hilo_dot(l1ms[u], after) for u in units}
            carries, os = dict(carries), list(os)
            for n, h in units:
                b, key, diag = pairs[n]
                a = jnp.exp2(lbs[n, h] + cins[n, h] + carries[b, h])
                if diag:
                    a = jnp.where(causal, a, 0.0)
                os[b] = os[b] + _nn(a.astype(BF16), vms[key][h])
                carries[b, h] = carries[b, h] + jnp.sum(l1ms[n, h], axis=1, keepdims=True)
            return carries, tuple(os)

        carries = {(b, h): jnp.zeros((T, 1), F32) for b in range(QB) for h in range(2)}
        os = tuple(jnp.zeros((T, 2 * DH), F32) for _ in range(QB))
        carries, os = blocks([i2 + 1, i2], [(1, 0, True), (0, 1, True), (1, 1, False)], carries, os)
        carries, os = lax.fori_loop(
            0, i2, lambda jj, c: blocks([i2 - 1 - jj], [(0, 0, False), (1, 0, False)], c[0], c[1]), (carries, os))
        for b in range(QB):
            o_ref[b * T:(b + 1) * T, :] = os[b].astype(BF16)
            c_ref[b * T:(b + 1) * T, :] = jnp.where(masks[0], carries[b, 0], carries[b, 1])

    blk = pl.BlockSpec((QB * T, 2 * DH), lambda p, i: (i, p))
    full = pl.BlockSpec((S, 2 * DH), lambda p, i: (0, p))
    return _call(
        body, (q, k, v), name="attn_fwd", grid=(SBW // (2 * DH), S // (QB * T)),
        in_specs=[blk, full, full], out_specs=[blk, blk],
        out_shape=[_sds((S, SBW), BF16), _sds((S, SBW), F32)],
        compiler_params=_params(("arbitrary", "arbitrary"), 40), exchange=exchange)


def _attn_bwd(q, k, v, do, ctot, exchange=None):
    T = TA
    nq = S // (QB * T)

    def body(q_ref, k_ref, v_ref, do_ref, c_ref, dq_ref, dk_ref, dv_ref, dk_acc, dv_acc):
        step = pl.program_id(1)
        i2 = 2 * step

        @pl.when(step == 0)
        def _():
            dk_acc[...] = jnp.zeros_like(dk_acc)
            dv_acc[...] = jnp.zeros_like(dv_acc)

        row = lax.broadcasted_iota(jnp.int32, (T, T), 0)
        col = lax.broadcasted_iota(jnp.int32, (T, T), 1)
        upto = (row <= col).astype(BF16)
        before = (row < col).astype(BF16)
        causal = col < row
        masks = _head_masks()
        qms, doms, ctots = {}, {}, {}
        for b in range(QB):
            q2, do2 = q_ref[b * T:(b + 1) * T, :], do_ref[b * T:(b + 1) * T, :]
            for h, hm in enumerate(masks):
                qms[b, h] = jnp.where(hm, q2, jnp.zeros_like(q2))
                doms[b, h] = jnp.where(hm, do2, jnp.zeros_like(do2))
                ctots[b, h] = c_ref[b * T:(b + 1) * T, h * DH:h * DH + 1]

        def blocks(keys, pairs, sums, dqs):
            rows = [pl.ds(pl.multiple_of(j * T, T), T) for j in keys]
            ks, vs = [k_ref[r, :] for r in rows], [v_ref[r, :] for r in rows]
            kms = [[jnp.where(hm, kj, jnp.zeros_like(kj)) for hm in masks] for kj in ks]
            units = [(n, h) for n in range(len(pairs)) for h in range(2)]
            qks = {(n, h): _nt(qms[pairs[n][0], h], ks[pairs[n][1]]) for n, h in units}
            das = {(n, h): _nt(doms[pairs[n][0], h], vs[pairs[n][1]]) for n, h in units}
            lbs, l1ms = {}, {}
            for u in units:
                lbs[u], l1m = _log_terms(qks[u])
                l1ms[u] = jnp.where(causal, l1m, 0.0) if pairs[u[0]][2] else l1m
            pins = {u: _hilo_dot(l1ms[u], upto) for u in units}
            sums = dict(sums)
            a_s, dls, cps = {}, {}, {}
            for n, h in units:
                b, _, diag = pairs[n]
                cl, cp = sums[b, h]
                a = jnp.exp2(lbs[n, h] + (ctots[b, h] - cl) - pins[n, h])
                if diag:
                    a = jnp.where(causal, a, 0.0)
                a_s[n, h] = a.astype(BF16)
                dls[n, h] = das[n, h] * a
                cps[n, h] = cp
                sums[b, h] = (cl + jnp.sum(l1ms[n, h], axis=1, keepdims=True),
                              cp + jnp.sum(dls[n, h], axis=1, keepdims=True))
            pexs = {u: _hilo_dot(dls[u], before) for u in units}
            dzbs = {}
            for u in units:
                dz = dls[u] - jnp.exp2(lbs[u]) * (dls[u] + pexs[u] + cps[u])
                if pairs[u[0]][2]:
                    dz = jnp.where(causal, dz, 0.0)
                dzbs[u] = dz.astype(BF16)
            dqs = list(dqs)
            for n, h in units:
                dqs[pairs[n][0]] = dqs[pairs[n][0]] + _nn(dzbs[n, h], kms[pairs[n][1]][h])
            for key, r in enumerate(rows):
                mine = [(n, h) for n, h in units if pairs[n][1] == key]
                dk_acc[r, :] += functools.reduce(jnp.add, [_tn(dzbs[u], qms[pairs[u[0]][0], u[1]]) for u in mine])
                dv_acc[r, :] += functools.reduce(jnp.add, [_tn(a_s[u], doms[pairs[u[0]][0], u[1]]) for u in mine])
            return sums, tuple(dqs)

        zero = jnp.zeros((T, 1), F32)
        sums = {(b, h): (zero, zero) for b in range(QB) for h in range(2)}
        dqs = tuple(jnp.zeros((T, 2 * DH), F32) for _ in range(QB))
        sums, dqs = lax.fori_loop(
            0, i2, lambda j, c: blocks([j], [(0, 0, False), (1, 0, False)], c[0], c[1]), (sums, dqs))
        _, dqs = blocks([i2, i2 + 1], [(0, 0, True), (1, 0, False), (1, 1, True)], sums, dqs)
        for b in range(QB):
            dq_ref[b * T:(b + 1) * T, :] = (dqs[b] * SCALE).astype(BF16)

        @pl.when(step == nq - 1)
        def _():
            dk_ref[...] = (dk_acc[...] * SCALE).astype(BF16)
            dv_ref[...] = dv_acc[...].astype(BF16)

    blk = pl.BlockSpec((QB * T, 2 * DH), lambda p, i: (i, p))
    full = pl.BlockSpec((S, 2 * DH), lambda p, i: (0, p))
    return _call(
        body, (q, k, v, do, ctot), name="attn_bwd", grid=(SBW // (2 * DH), nq),
        in_specs=[blk, full, full, blk, blk], out_specs=[blk, full, full],
        out_shape=[_sds((S, SBW), BF16), _sds((S, SBW), BF16), _sds((S, SBW), BF16)],
        scratch_shapes=[pltpu.VMEM((S, 2 * DH), F32), pltpu.VMEM((S, 2 * DH), F32)],
        compiler_params=_params(("arbitrary", "arbitrary"), 40), exchange=exchange)


def _pool_counts(first_row, tm):
    pos = first_row + lax.broadcasted_iota(jnp.int32, (tm, 1), 0)
    return [jnp.minimum(pos + 1, w).astype(F32) for w in POOL_WINDOWS]


def _mix_out(h, xp, o_sb, gp, gs, w_group, scale, w_bp, w_ba, w_out):
    tm = 512

    def body(h_ref, xp_ref, o_ref, gp_ref, gs_ref, wg_hbm, sc_ref, wbp_hbm, wba_hbm, wo_hbm,
             h2_ref, pm_ref, p_ref, yp_ref, ys_ref, m_ref, halo, wg_ref, wbp_ref, wba_ref, wo_ref):
        _stage([(wg_hbm, wg_ref), (wbp_hbm, wbp_ref), (wba_hbm, wba_ref), (wo_hbm, wo_ref)])
        i = pl.program_id(0)

        @pl.when(i == 0)
        def _():
            halo[...] = jnp.zeros_like(halo)

        xp = xp_ref[...]
        ext = jnp.concatenate([halo[...], xp], axis=0)
        halo[...] = xp[tm - HALO:, :]
        counts = _pool_counts(i * tm, tm)
        for gi in range(len(POOL_WINDOWS)):
            lanes = slice(gi * PG, (gi + 1) * PG)
            win = ext[:, lanes]
            for step in range(gi + 1):
                win = win + pltpu.roll(win, 1 << step, 0)
            pm = (win[HALO:, :] / counts[gi] - xp[:, lanes]).astype(BF16)
            pm_ref[:, lanes] = pm
            p_ref[:, lanes] = (_nn(pm, wg_ref[gi]) * sc_ref[:, lanes]).astype(BF16)
        pb = p_ref[...]
        ob = o_ref[...]
        for j in range(NSH):
            cols = slice(j * (D // NSH), (j + 1) * (D // NSH))
            yp = _nn(pb, wbp_ref[j])
            ys = _nn(ob, wba_ref[j])
            yp_ref[:, cols] = yp.astype(BF16)
            ys_ref[:, cols] = ys.astype(BF16)
            m_ref[:, cols] = (gp_ref[:, cols].astype(F32) * yp + gs_ref[:, cols].astype(F32) * ys).astype(BF16)
        h2_ref[...] = h_ref[...] + _nn(m_ref[...], wo_ref[...])

    return pl.pallas_call(
        body, name="mix_out", grid=(S // tm,),
        in_specs=[_rows(tm, D), _rows(tm, PW), _rows(tm, SBW), _rows(tm, D), _rows(tm, D),
                  _ANY, _fixed((1, PW)), _ANY, _ANY, _ANY],
        out_specs=[_rows(tm, D), _rows(tm, PW), _rows(tm, PW), _rows(tm, D), _rows(tm, D), _rows(tm, D)],
        out_shape=[_sds((S, D), F32), _sds((S, PW), BF16), _sds((S, PW), BF16), _sds((S, D), BF16),
                   _sds((S, D), BF16), _sds((S, D), BF16)],
        scratch_shapes=[pltpu.VMEM((HALO, PW), F32)] + _vmem_like(w_group, w_bp, w_ba, w_out),
        compiler_params=_params(("arbitrary",), 48),
    )(*_in_hbm([h, xp, o_sb, gp, gs]), w_group, scale, *_in_hbm([w_bp, w_ba, w_out]))


def _mix_bwd_out(dh, gp, gs, yp, ys, pm, w_group, scale, w_bp, w_ba, w_out, exchange=None):
    tm = 512
    nt = S // tm

    def body(dh_ref, gp_ref, gs_ref, yp_ref, ys_ref, pm_ref, wg_hbm, sc_ref, wbp_hbm, wba_hbm, wo_hbm,
             dlg_ref, dyp_ref, dys_ref, do_ref, dyg_ref, dxp_ref, dsc_ref, halo, wg_ref, wbp_ref, wba_ref, wo_ref):
        _stage([(wg_hbm, wg_ref), (wbp_hbm, wbp_ref), (wba_hbm, wba_ref), (wo_hbm, wo_ref)])
        step = pl.program_id(0)

        @pl.when(step == 0)
        def _():
            halo[...] = jnp.zeros_like(halo)
            dsc_ref[...] = jnp.zeros_like(dsc_ref)

        dm = _nt(dh_ref[...].astype(BF16), wo_ref[...])
        gp = gp_ref[...].astype(F32)
        gs = gs_ref[...].astype(F32)
        yp = yp_ref[...].astype(F32)
        ys = ys_ref[...].astype(F32)
        dlg_ref[:, :D] = (dm * yp * gp * (1.0 - gp)).astype(BF16)
        dlg_ref[:, D:] = (dm * ys * gs * (1.0 - gs)).astype(BF16)
        dyp_ref[...] = (dm * gp).astype(BF16)
        dys_ref[...] = (dm * gs).astype(BF16)
        dp = jnp.zeros((tm, PW), F32)
        do = jnp.zeros((tm, SBW), F32)
        for j in range(NSH):
            cols = slice(j * (D // NSH), (j + 1) * (D // NSH))
            dp = dp + _nt(dyp_ref[:, cols], wbp_ref[j])
            do = do + _nt(dys_ref[:, cols], wba_ref[j])
        do_ref[...] = do.astype(BF16)
        counts = _pool_counts((nt - 1 - step) * tm, tm)
        dscale = []
        for gi in range(len(POOL_WINDOWS)):
            lanes = slice(gi * PG, (gi + 1) * PG)
            dpg = dp[:, lanes]
            dscale.append(jnp.sum(dpg * _nn(pm_ref[:, lanes], wg_ref[gi]), axis=0, keepdims=True))
            dyg = (dpg * sc_ref[:, lanes]).astype(BF16)
            dyg_ref[:, lanes] = dyg
            dpm = _nt(dyg, wg_ref[gi])
            per = dpm / counts[gi]
            win = jnp.concatenate([per, halo[:, lanes]], axis=0)
            halo[:, lanes] = per[:HALO, :]
            for s in range(gi + 1):
                win = win + pltpu.roll(win, tm + HALO - (1 << s), 0)
            dxp_ref[:, lanes] = (win[:tm, :] - dpm).astype(BF16)
        dsc_ref[...] += jnp.concatenate(dscale, axis=1)

    rev = lambda width: pl.BlockSpec((tm, width), lambda i: (nt - 1 - i, 0))
    return _call(
        body, (dh, gp, gs, yp, ys, pm, w_group, scale, w_bp, w_ba, w_out), name="mix_bwd_out", grid=(nt,),
        in_specs=[rev(D), rev(D), rev(D), rev(D), rev(D), rev(PW), _ANY, _fixed((1, PW)), _ANY, _ANY, _ANY],
        out_specs=[rev(2 * D), rev(D), rev(D), rev(SBW), rev(PW), rev(PW), _fixed((1, PW))],
        out_shape=[_sds((S, 2 * D), BF16), _sds((S, D), BF16), _sds((S, D), BF16), _sds((S, SBW), BF16),
                   _sds((S, PW), BF16), _sds((S, PW), BF16), _sds((1, PW), F32)],
        scratch_shapes=[pltpu.VMEM((HALO, PW), F32)] + _vmem_like(w_group, w_bp, w_ba, w_out),
        compiler_params=_params(("arbitrary",), 48), exchange=exchange)


def _mix_bwd_in(dh, h, gain, pieces, w_in, exchange=None):
    tm = 512
    widths = [p.shape[1] for p in pieces]

    def body(dh_ref, h_ref, g_ref, *rest):
        piece_refs, (w_hbm, dx_ref, dg_ref, dp_ref, w_ref) = rest[:len(pieces)], rest[len(pieces):]
        _stage([(w_hbm, w_ref)])
        at = 0
        for ref, width in zip(piece_refs, widths):
            dp_ref[:, at:at + width] = ref[...]
            at += width
        du = jnp.zeros((tm, D), F32)
        for j in range(NSH):
            du = du + _nt(dp_ref[:, j * D:(j + 1) * D], w_ref[j])
        r, hr = _rms(h_ref[...])
        dx, dgain = _rms_bwd(du, hr, r, g_ref[...])
        dx_ref[...] = dh_ref[...] + dx

        @pl.when(pl.program_id(0) == 0)
        def _():
            dg_ref[...] = jnp.zeros_like(dg_ref)

        dg_ref[...] += dgain

    return _call(
        body, (dh, h, gain, *pieces, w_in), name="mix_bwd_in", grid=(S // tm,),
        in_specs=[_rows(tm, D), _rows(tm, D), _fixed((1, D))] + [_rows(tm, w) for w in widths] + [_ANY],
        out_specs=[_rows(tm, D), _fixed((1, D)), _rows(tm, 4 * D)],
        out_shape=[_sds((S, D), F32), _sds((1, D), F32), _sds((S, 4 * D), BF16)],
        scratch_shapes=_vmem_like(w_in),
        compiler_params=_params(("arbitrary",), 48), exchange=exchange)


def _wgrad(a, b, nblk, ti, name, out_dtype=BF16, exchange=None):
    ka, n = a.shape[1], b.shape[1]
    ns = n // nblk

    def body(a_ref, b_ref, o_ref):
        o_ref[...] = _tn(a_ref[...].astype(BF16), b_ref[...].astype(BF16)).astype(out_dtype)

    res = _call(
        body, (a, b), name=name, grid=(nblk, ka // ti),
        in_specs=[pl.BlockSpec((S, ti), lambda j, i: (0, i)), pl.BlockSpec((S, ns), lambda j, i: (0, j))],
        out_specs=[pl.BlockSpec((None, ti, ns), lambda j, i: (j, i, 0))],
        out_shape=[_sds((nblk, ka, ns), out_dtype)],
        compiler_params=_params(("arbitrary", "arbitrary"), 56), exchange=exchange)
    return res[0] if exchange is None else (res[0][0], res[1])


def _wgrad_groups(pm, dyg):
    def body(a_ref, b_ref, o_ref):
        o_ref[...] = _tn(a_ref[...], b_ref[...])

    col = pl.BlockSpec((S, PG), lambda g: (0, g))
    return pl.pallas_call(
        body, name="wgrad_groups", grid=(PW // PG,),
        in_specs=[col, col], out_specs=pl.BlockSpec((None, PG, PG), lambda g: (g, 0, 0)),
        out_shape=_sds((PW // PG, PG, PG), F32),
        compiler_params=_params(("arbitrary",), 32),
    )(*_in_hbm([pm, dyg]))


def _place():
    x, y, c = lax.axis_index("x"), lax.axis_index("y"), lax.axis_index("c")
    chips = [(1 - x, y), (x, 1 - y), (1 - x, 1 - y)]
    return x, y, c, chips


def _remote(src, dst, ssem, rsem, dev):
    return pltpu.make_async_remote_copy(src_ref=src, dst_ref=dst, send_sem=ssem, recv_sem=rsem,
                                        device_id=dev, device_id_type=MESH)


def _cast_into_block(w, me_idx, name):
    rows, cols = w.shape
    tr = _row_block(rows)

    def body(me_ref, w_ref, o_ref):
        o_ref[...] = w_ref[...].astype(BF16)

    return pl.pallas_call(
        body, name=name, out_shape=_sds((NSH, rows, cols), BF16),
        grid_spec=pltpu.PrefetchScalarGridSpec(
            num_scalar_prefetch=1, grid=(rows // tr,),
            in_specs=[pl.BlockSpec((tr, cols), lambda r, me: (r, 0))],
            out_specs=pl.BlockSpec((None, tr, cols), lambda r, me: (me[0], r, 0))),
        compiler_params=_params(("arbitrary",), 32),
    )(me_idx, w)


def _ex_gather(bufs):
    n = len(bufs)
    per = 8

    def plan(outs, ssem, rsem, w):
        x, y, c, _ = _place()
        sib, nbr_x, nbr_y = (x, y, 1 - c), (1 - x, y, c), (x, 1 - y, c)
        half = outs[w].shape[1] // 2
        quarter = half // 2
        sem = lambda k: (ssem.at[per * w + k], rsem.at[per * w + k])
        rows = lambda blk, start, size: outs[w].at[blk, pl.ds(start, size)]
        mine = rows(2 * x + y, c * half, half)
        from_x = rows(2 * (1 - x) + y, c * half, half)
        from_y = rows(2 * x + (1 - y), c * half, half)
        diag = 2 * (1 - x) + (1 - y)
        pass_y = rows(2 * (1 - x) + y, c * half, quarter)
        pass_x = rows(2 * x + (1 - y), c * half + quarter, quarter)
        diag_0, diag_1 = rows(diag, c * half, quarter), rows(diag, c * half + quarter, quarter)
        first = [_remote(mine, mine, *sem(0), nbr_x), _remote(mine, mine, *sem(1), nbr_y)]
        arrivals = [
            (_remote(from_x, from_x, *sem(0), nbr_x),
             [_remote(pass_y, pass_y, *sem(2), nbr_y), _remote(from_x, from_x, *sem(4), sib)]),
            (_remote(from_y, from_y, *sem(1), nbr_y),
             [_remote(pass_x, pass_x, *sem(3), nbr_x), _remote(from_y, from_y, *sem(5), sib)]),
            (_remote(diag_0, diag_0, *sem(2), nbr_y), [_remote(diag_0, diag_0, *sem(6), sib)]),
            (_remote(diag_1, diag_1, *sem(3), nbr_x), [_remote(diag_1, diag_1, *sem(7), sib)]),
        ]
        other = (1 - c) * half
        from_sibling = [
            _remote(rows(2 * (1 - x) + y, other, half), rows(2 * (1 - x) + y, other, half), *sem(4), sib),
            _remote(rows(2 * x + (1 - y), other, half), rows(2 * x + (1 - y), other, half), *sem(5), sib),
            _remote(rows(diag, other, quarter), rows(diag, other, quarter), *sem(6), sib),
            _remote(rows(diag, other + quarter, quarter), rows(diag, other + quarter, quarter), *sem(7), sib),
        ]
        return first, arrivals, from_sibling

    def start(ins, outs, ssem, rsem):
        x, y, c, _ = _place()
        for w in range(n):
            half = outs[w].shape[1] // 2
            mine = outs[w].at[2 * x + y, pl.ds(c * half, half)]
            _remote(mine, mine, ssem.at[per * w], rsem.at[per * w], (1 - x, y, c)).start()
            _remote(mine, mine, ssem.at[per * w + 1], rsem.at[per * w + 1], (x, 1 - y, c)).start()

    def finish(ins, outs, ssem, rsem):
        plans = [plan(outs, ssem, rsem, w) for w in range(n)]
        started = []
        for direct in (True, False):
            for first, arrivals, _ in plans:
                for arrived, onward in (arrivals[:2] if direct else arrivals[2:]):
                    arrived.wait_recv()
                    for cp in onward:
                        cp.start()
                    started += onward
        for first, _, from_sibling in plans:
            for cp in from_sibling:
                cp.wait_recv()
            started += first
        for cp in started:
            cp.wait_send()

    return Exchange(bufs, [_sds(b.shape, b.dtype) for b in bufs], {w: w for w in range(n)}, per * n, start, finish)


def _ex_gather_direct(bufs):
    n = len(bufs)

    def copies(outs, ssem, rsem, only_first=False):
        x, y, c, chips = _place()
        me, sib = 2 * x + y, (x, y, 1 - c)
        first, relay, last = [], [], []
        for w in range(n):
            half = outs[w].shape[1] // 2
            mine = outs[w].at[me, pl.ds(c * half, half)]
            for k, (px, py) in enumerate(chips):
                sems = (ssem.at[6 * w + k], rsem.at[6 * w + k])
                sib_sems = (ssem.at[6 * w + 3 + k], rsem.at[6 * w + 3 + k])
                first.append(_remote(mine, mine, *sems, (px, py, c)))
                if only_first:
                    continue
                got = outs[w].at[2 * px + py, pl.ds(c * half, half)]
                relay.append((_remote(got, got, *sems, (px, py, c)), _remote(got, got, *sib_sems, sib)))
                theirs = outs[w].at[2 * px + py, pl.ds((1 - c) * half, half)]
                last.append(_remote(theirs, theirs, *sib_sems, sib))
        return first, relay, last

    def start(ins, outs, ssem, rsem):
        for cp in copies(outs, ssem, rsem, only_first=True)[0]:
            cp.start()

    def finish(ins, outs, ssem, rsem):
        first, relay, last = copies(outs, ssem, rsem)
        for arrived, onward in relay:
            arrived.wait_recv()
            onward.start()
        for cp in last:
            cp.wait_recv()
        for cp in first:
            cp.wait_send()
        for _, onward in relay:
            onward.wait_send()

    return Exchange(bufs, [_sds(b.shape, b.dtype) for b in bufs], {w: w for w in range(n)}, 6 * n, start, finish)


def _simple_exchange(arrays, landing, aliases, make_copies):
    def start(ins, outs, ssem, rsem):
        for cp, _ in make_copies(ins, outs, ssem, rsem, False):
            cp.start()

    def finish(ins, outs, ssem, rsem):
        cps = make_copies(ins, outs, ssem, rsem, True)
        for _, landed in cps:
            landed.wait_recv()
        for cp, _ in cps:
            cp.wait_send()

    return Exchange(arrays, landing, aliases, len(arrays) * 3, start, finish)


def _ex_pair_swap(grads):
    def make(ins, outs, ssem, rsem, landing):
        x, y, c, _ = _place()
        cps = [_remote(ins[w].at[:, 1 - c], outs[w], ssem.at[w], rsem.at[w], (x, y, 1 - c))
               for w in range(len(grads))]
        return [(cp, cp) for cp in cps]

    return _simple_exchange(grads, [_sds((NSH,) + g.shape[2:], g.dtype) for g in grads], {}, make)


def _ex_scatter(parts):
    def make(ins, outs, ssem, rsem, landing):
        x, y, c, chips = _place()
        out = []
        for w in range(len(parts)):
            for k, (px, py) in enumerate(chips):
                sems = (ssem.at[3 * w + k], rsem.at[3 * w + k])
                out.append((_remote(ins[w].at[2 * px + py], outs[w].at[k], *sems, (px, py, c)),
                            _remote(outs[w].at[k], outs[w].at[k], *sems, (px, py, c)) if landing else None))
        return out

    return _simple_exchange(parts, [_sds((3,) + p.shape[1:], p.dtype) for p in parts], {}, make)


def _ex_share(bufs):
    def make(ins, outs, ssem, rsem, landing):
        x, y, c, _ = _place()
        sib = (x, y, 1 - c)
        return [(_remote(outs[w].at[c], outs[w].at[c], ssem.at[w], rsem.at[w], sib),
                 _remote(outs[w].at[1 - c], outs[w].at[1 - c], ssem.at[w], rsem.at[w], sib) if landing else None)
                for w in range(len(bufs))]

    return _simple_exchange(bufs, [_sds(b.shape, b.dtype) for b in bufs], {w: w for w in range(len(bufs))}, make)


def _gather_small(block, ex):
    m_per, n = block.shape
    na, nl = len(ex.arrays), len(ex.landing)

    def body(x_ref, *refs):
        e_in, out_ref, e_out = refs[:na], refs[na], refs[na + 1:na + 1 + nl]
        ssem, rsem, lsem, e_ssem, e_rsem = refs[na + 1 + nl:]
        ex.start(e_in, e_out, e_ssem, e_rsem)
        x, y, c, chips = _place()
        me, sib = (x, y, c), (x, y, 1 - c)

        def rows(px, py, pc):
            return out_ref.at[pl.ds((4 * px + 2 * py + pc) * m_per, m_per), :]

        def copy(k, blk, to, src=None):
            return _remote(rows(*blk) if src is None else src, rows(*blk), ssem.at[k], rsem.at[k], to)

        mine = pltpu.make_async_copy(x_ref, rows(*me), lsem)
        mine.start()
        first = [copy(0, me, sib, src=x_ref)]
        first += [copy(1 + j, me, (*chip, c), src=x_ref) for j, chip in enumerate(chips)]
        for cp in first:
            cp.start()
        passed = [copy(4 + j, (*chip, c), sib) for j, chip in enumerate(chips)]
        for j, chip in enumerate(chips):
            copy(1 + j, (*chip, c), me).wait_recv()
            passed[j].start()
        copy(0, sib, me).wait_recv()
        for j, chip in enumerate(chips):
            copy(4 + j, (*chip, 1 - c), me).wait_recv()
        for cp in first + passed:
            cp.wait_send()
        mine.wait()
        ex.finish(e_in, e_out, e_ssem, e_rsem)

    outs = pl.pallas_call(
        body, name="gather_small", out_shape=[jax.ShapeDtypeStruct((8 * m_per, n), block.dtype)] + ex.landing,
        in_specs=[_VM] + [_ANY] * na, out_specs=[_VM] + [_ANY] * nl,
        scratch_shapes=[pltpu.SemaphoreType.DMA((7,)), pltpu.SemaphoreType.DMA((7,)), pltpu.SemaphoreType.DMA]
        + [pltpu.SemaphoreType.DMA((ex.n_sems,))] * 2,
        input_output_aliases={1 + i: 1 + j for i, j in ex.aliases.items()},
    )(block, *_in_hbm(ex.arrays))
    return outs[0], outs[1:]


def _row_block(rows):
    return max(t for t in range(16, 257, 16) if rows % t == 0)


def _pair_sum(grad, got, c_idx, name):
    _, _, half, cols = grad.shape
    tr = _row_block(half)

    def body(c_ref, a_ref, b_ref, o_ref):
        o_ref[...] = (a_ref[...].astype(F32) + b_ref[...].astype(F32)).astype(BF16)

    return pl.pallas_call(
        body, name=name, out_shape=_sds((NSH, half, cols), BF16),
        grid_spec=pltpu.PrefetchScalarGridSpec(
            num_scalar_prefetch=1, grid=(NSH, half // tr),
            in_specs=[pl.BlockSpec((None, None, tr, cols), lambda j, r, c: (j, c[0], r, 0)),
                      pl.BlockSpec((None, tr, cols), lambda j, r, c: (j, r, 0))],
            out_specs=pl.BlockSpec((None, tr, cols), lambda j, r, c: (j, r, 0))),
        compiler_params=_params(("arbitrary", "arbitrary"), 32),
    )(c_idx, *_in_hbm([grad, got]))


def _chip_sum(own, got, place, name):
    _, half, cols = own.shape
    tr = _row_block(half)

    def body(place_ref, own_ref, got_ref, o_ref):
        acc = own_ref[...].astype(F32)
        for k in range(3):
            acc = acc + got_ref[k].astype(F32)
        o_ref[...] = acc

    return pl.pallas_call(
        body, name=name, out_shape=_sds((2, half, cols), F32),
        grid_spec=pltpu.PrefetchScalarGridSpec(
            num_scalar_prefetch=1, grid=(half // tr,),
            in_specs=[pl.BlockSpec((None, tr, cols), lambda r, p: (p[0], r, 0)),
                      pl.BlockSpec((3, tr, cols), lambda r, p: (0, r, 0))],
            out_specs=pl.BlockSpec((None, tr, cols), lambda r, p: (p[1], r, 0))),
        compiler_params=_params(("arbitrary",), 32),
    )(place, *_in_hbm([own, got]))


def _adamw_math(w, g, m, v):
    m = B1 * m + (1.0 - B1) * g
    v = B2 * v + (1.0 - B2) * (g * g)
    m_hat = m / (1.0 - B1 ** STEP)
    v_hat = v / (1.0 - B2 ** STEP)
    return -LR * (m_hat / (jnp.sqrt(v_hat) + AEPS) + WD * w), m, v


def _adamw(w, g, m, v, name, after=()):
    rows, cols = w.shape
    tr = _row_block(rows)

    def body(w_ref, g_ref, m_ref, v_ref, go_ref, d_ref, nm_ref, nv_ref):
        g = g_ref[...]
        go_ref[...] = g
        d_ref[...], nm_ref[...], nv_ref[...] = _adamw_math(w_ref[...], g, m_ref[...], v_ref[...])

    blk = pl.BlockSpec((tr, cols), lambda r: (r, 0))
    return _call(
        body, (w, g, m, v), name=name, grid=(rows // tr,), out_shape=[_sds(w.shape, F32)] * 4,
        in_specs=[blk] * 4, out_specs=[blk] * 4,
        compiler_params=_params(("arbitrary",), 32), free=(0, 2, 3), after=after)


def _small_update(gathered, w, m, v):
    rows = w.shape[0]

    def body(ga_ref, w_ref, m_ref, v_ref, g_ref, d_ref, nm_ref, nv_ref):
        g = ga_ref[0:rows, :]
        for dev in range(1, 8):
            g = g + ga_ref[dev * rows:(dev + 1) * rows, :]
        g_ref[...] = g
        d_ref[...], nm_ref[...], nv_ref[...] = _adamw_math(w_ref[...], g, m_ref[...], v_ref[...])

    return pl.pallas_call(
        body, name="small_update", out_shape=[jax.ShapeDtypeStruct(w.shape, F32)] * 4,
        in_specs=[_VM] * 4, out_specs=[_VM] * 4,
    )(gathered, w, m, v)


SMALL = ("ffn1_norm", "mix_norm", "ffn2_norm", "final_norm", "pool_scale", "pool_w_group", "loss")
BIG = ("ffn1_w_gate_up", "ffn1_w_down", "w_in", "w_branch_pool", "w_branch_attn", "w_out",
       "ffn2_w_gate_up", "ffn2_w_down")
ORDER = ("ffn1_norm", "ffn1_w_gate_up", "ffn1_w_down", "mix_norm", "w_in", "pool_w_group", "pool_scale",
         "w_branch_pool", "w_branch_attn", "w_out", "ffn2_norm", "ffn2_w_gate_up", "ffn2_w_down", "final_norm")
SMALL_ROWS = 560


def _pack_small(t):
    parts = []
    for k in SMALL:
        rows = t[k].reshape(-1, 128) if k in t else jnp.zeros((1, 128), F32)
        parts.append(jnp.pad(rows, ((0, -rows.shape[0] % 8), (0, 0))))
    packed = jnp.concatenate(parts, axis=0)
    assert packed.shape == (SMALL_ROWS, 128), packed.shape
    return packed


def _unpack_small(packed, like):
    out, at = {}, 0
    for k in SMALL:
        n = like[k].size // 128 if k in like else 1
        out[k] = packed[at:at + n].reshape(like[k].shape) if k in like else packed[at, 0]
        at += n + (-n % 8)
    return out


def _halves(g):
    return g.reshape(NSH, 2, g.shape[1] // 2, g.shape[2])


def kernel(x, ffn1_norm, ffn1_w_gate_up, ffn1_w_down, mix_norm, w_in, pool_w_group, pool_scale, w_branch_pool, w_branch_attn, w_out, ffn2_norm, ffn2_w_gate_up, ffn2_w_down, final_norm, loss_target, m_ffn1_norm, m_ffn1_w_gate_up, m_ffn1_w_down, m_mix_norm, m_w_in, m_pool_w_group, m_pool_scale, m_w_branch_pool, m_w_branch_attn, m_w_out, m_ffn2_norm, m_ffn2_w_gate_up, m_ffn2_w_down, m_final_norm, v_ffn1_norm, v_ffn1_w_gate_up, v_ffn1_w_down, v_mix_norm, v_w_in, v_pool_w_group, v_pool_scale, v_w_branch_pool, v_w_branch_attn, v_w_out, v_ffn2_norm, v_ffn2_w_gate_up, v_ffn2_w_down, v_final_norm):
    wts = dict(ffn1_norm=ffn1_norm, ffn1_w_gate_up=ffn1_w_gate_up, ffn1_w_down=ffn1_w_down, mix_norm=mix_norm,
               w_in=w_in, pool_w_group=pool_w_group, pool_scale=pool_scale, w_branch_pool=w_branch_pool,
               w_branch_attn=w_branch_attn, w_out=w_out, ffn2_norm=ffn2_norm, ffn2_w_gate_up=ffn2_w_gate_up,
               ffn2_w_down=ffn2_w_down, final_norm=final_norm)
    mom = dict(ffn1_norm=m_ffn1_norm, ffn1_w_gate_up=m_ffn1_w_gate_up, ffn1_w_down=m_ffn1_w_down,
               mix_norm=m_mix_norm, w_in=m_w_in, pool_w_group=m_pool_w_group, pool_scale=m_pool_scale,
               w_branch_pool=m_w_branch_pool, w_branch_attn=m_w_branch_attn, w_out=m_w_out,
               ffn2_norm=m_ffn2_norm, ffn2_w_gate_up=m_ffn2_w_gate_up, ffn2_w_down=m_ffn2_w_down,
               final_norm=m_final_norm)
    var = dict(ffn1_norm=v_ffn1_norm, ffn1_w_gate_up=v_ffn1_w_gate_up, ffn1_w_down=v_ffn1_w_down,
               mix_norm=v_mix_norm, w_in=v_w_in, pool_w_group=v_pool_w_group, pool_scale=v_pool_scale,
               w_branch_pool=v_w_branch_pool, w_branch_attn=v_w_branch_attn, w_out=v_w_out,
               ffn2_norm=v_ffn2_norm, ffn2_w_gate_up=v_ffn2_w_gate_up, ffn2_w_down=v_ffn2_w_down,
               final_norm=v_final_norm)

    c_idx = lax.axis_index("c").astype(jnp.int32).reshape(1)
    me_idx = (2 * lax.axis_index("x") + lax.axis_index("y")).astype(jnp.int32).reshape(1)
    place = jnp.concatenate([me_idx, c_idx])
    x0, tgt = x[0], loss_target[0]
    wgrp = pool_w_group[0].astype(BF16)
    g1, gm, g2, gf = ffn1_norm, mix_norm, ffn2_norm, final_norm.reshape(1, D)
    grad, delta, new_m, new_v = {}, {}, {}, {}

    def pair_sums(keys, parts, got):
        return [_pair_sum(parts[i], got[i], c_idx, "pair_sum_" + k) for i, k in enumerate(keys)]

    def chip_sums(keys, chip_parts, owned):
        return [_chip_sum(chip_parts[i], owned[i], place, "chip_sum_" + k) for i, k in enumerate(keys)]

    def adamw(k, after=()):
        outs = _adamw(wts[k][0], grad[k][0], mom[k][0], var[k][0], "adamw_" + k, after=after)
        grad[k], delta[k], new_m[k], new_v[k] = (o.reshape(wts[k].shape) for o in outs)

    own = {k: _cast_into_block(wts[k][0], me_idx, "cast_" + k) for k in BIG}
    first, late = ("ffn1_w_gate_up", "ffn1_w_down"), ("w_branch_pool", "w_branch_attn", "w_out",
                                                       "ffn2_w_gate_up", "ffn2_w_down")
    full = dict(zip(first, _exchange_alone(_ex_gather([own[k] for k in first]), "gather_ffn1")))
    wgu1, wd1 = full["ffn1_w_gate_up"], full["ffn1_w_down"].reshape(DFF, D)
    (h1, n1, gu1, a1), (win,) = _ffn_fwd(x0, g1, wgu1, wd1, "ffn1_fwd", exchange=_ex_gather_direct([own["w_in"]]))
    u, xp, q, k, v, gp, gs = _mix_in(h1, gm, win)
    (o_sb, ctot), landed = _attn_fwd(q, k, v, exchange=_ex_gather_direct([own[k_] for k_ in late]))
    full.update(zip(late, landed))
    wbp, wba, wout = full["w_branch_pool"], full["w_branch_attn"], full["w_out"].reshape(D, D)
    wgu2, wd2 = full["ffn2_w_gate_up"], full["ffn2_w_down"].reshape(DFF, D)
    h2, pm, p, yp, ys, mm = _mix_out(h1, xp, o_sb, gp, gs, wgrp, pool_scale, wbp, wba, wout)
    h3, n3, gu3, a3 = _ffn_fwd(h2, g2, wgu2, wd2, "ffn2_fwd")
    dh3, loss_row, d_gf = _head(h3, tgt, gf)

    def grad_gate_up(n, dgu, name, exchange=None):
        res = _wgrad(n, dgu, NSH, 512, name, exchange=exchange)
        return [_halves(res)] if exchange is None else ([_halves(res[0])], res[1])

    def grad_down(a, dh, name, exchange=None):
        res = _wgrad(a, dh, 1, FFS, name, exchange=exchange)
        halves = lambda g: [_halves(g.reshape(NSH, DFF // NSH, D))]
        return halves(res) if exchange is None else (halves(res[0]), res[1])

    k_gu2, k_d2, k_gu1, k_d1, k_in = (("ffn2_w_gate_up",), ("ffn2_w_down",), ("ffn1_w_gate_up",),
                                      ("ffn1_w_down",), ("w_in",))
    dgu3 = _ffn_bwd_act(dh3, gu3, wd2, "ffn2_bwd_act")
    pa = grad_gate_up(n3, dgu3, "wgrad_gu2") + grad_down(a3, dh3, "wgrad_d2")
    (dh2, d_g2), got_a = _ffn_bwd_in(dh3, h2, g2, dgu3, wgu2, "ffn2_bwd_in", exchange=_ex_pair_swap(pa))
    chip_a = pair_sums(k_gu2 + k_d2, pa, got_a)
    dlg, dyp, dys, do_sb, dyg, dxp, d_scale = _mix_bwd_out(dh2, gp, gs, yp, ys, pm, wgrp, pool_scale, wbp, wba, wout)
    kb = ("w_out", "w_branch_pool", "w_branch_attn")
    pb = [_halves(_wgrad(mm, dh2, 1, 512, "wgrad_out").reshape(NSH, D // NSH, D)),
          _halves(_wgrad(p, dyp, NSH, PW, "wgrad_bp")), _halves(_wgrad(o_sb, dys, NSH, SBW, "wgrad_ba"))]
    chip_b = pair_sums(kb, pb, _exchange_alone(_ex_pair_swap(pb), "pair_swap_mix"))
    k_ab = k_gu2 + k_d2 + kb
    (dq, dk, dv), owned_ab = _attn_bwd(q, k, v, do_sb, ctot, exchange=_ex_scatter(chip_a + chip_b))
    halves_ab = chip_sums(k_ab, chip_a + chip_b, owned_ab)
    (dh1, d_gm, dproj), both_ab = _mix_bwd_in(dh2, h1, gm, (dxp, dq, dk, dv, dlg), win, exchange=_ex_share(halves_ab))
    for i, k_ in enumerate(k_ab):
        grad[k_] = both_ab[i].reshape(wts[k_].shape)

    p_in = [_halves(_wgrad(u, dproj, NSH, 512, "wgrad_in"))]
    p_d1, got_in = grad_down(a1, dh1, "wgrad_d1", exchange=_ex_pair_swap(p_in))
    chip_in = pair_sums(k_in, p_in, got_in)
    dgu1, landed = _ffn_bwd_act(dh1, gu1, wd1, "ffn1_bwd_act",
                                exchange=_join(_ex_scatter(chip_in), _ex_pair_swap(p_d1)))
    owned_in, got_d1 = landed[:1], landed[1:]
    chip_d1 = pair_sums(k_d1, p_d1, got_d1)
    halves_in = chip_sums(k_in, chip_in, owned_in)
    p_gu1, landed = grad_gate_up(n1, dgu1, "wgrad_gu1", exchange=_join(_ex_scatter(chip_d1), _ex_share(halves_in)))
    owned_d1, both_in = landed[:1], landed[1:]
    grad["w_in"] = both_in[0].reshape(w_in.shape)
    halves_d1 = chip_sums(k_d1, chip_d1, owned_d1)
    chip_gu1 = pair_sums(k_gu1, p_gu1, _exchange_alone(_ex_pair_swap(p_gu1), "pair_swap_gu1"))
    sems, thru, token = _scatter_start(chip_gu1, "scatter_gu1_start")
    for k_ in k_ab + k_in:
        adamw(k_, after=(token,))
    dx, d_g1 = _ffn_bwd_in(dh1, x0, g1, dgu1, wgu1, "ffn1_bwd_in", after=(token,))
    owned_gu1 = _scatter_wait(sems, thru, [dx] + [delta[k_] for k_ in k_ab + k_in], "scatter_gu1_wait")

    small_g = dict(ffn1_norm=d_g1, mix_norm=d_gm, ffn2_norm=d_g2, final_norm=d_gf, pool_scale=d_scale,
                   pool_w_group=_wgrad_groups(pm, dyg), loss=loss_row)
    gathered, both = _gather_small(_pack_small(small_g), _ex_share(halves_d1 + chip_sums(k_gu1, chip_gu1, owned_gu1)))
    grad["ffn1_w_down"] = both[0].reshape(ffn1_w_down.shape)
    grad["ffn1_w_gate_up"] = both[1].reshape(ffn1_w_gate_up.shape)
    for k_ in k_d1 + k_gu1:
        adamw(k_)
    sg, sd, sm, sv = _small_update(gathered, _pack_small(wts), _pack_small(mom), _pack_small(var))
    sums = _unpack_small(sg, wts)
    loss = sums.pop("loss")
    grad.update(sums)
    for dst, packed in ((delta, sd), (new_m, sm), (new_v, sv)):
        vals = _unpack_small(packed, wts)
        vals.pop("loss")
        dst.update(vals)
    return (loss, dx[None], *[grad[k_] for k_ in ORDER], *[delta[k_] for k_ in ORDER],
            *[new_m[k_] for k_ in ORDER], *[new_v[k_] for k_ in ORDER])
```

```python
import functools

import jax
import jax.numpy as jnp
from jax import lax
from jax.experimental import pallas as pl
from jax.experimental.pallas import tpu as pltpu

F32 = jnp.float32
BF16 = jnp.bfloat16

S = 2048
D = 1024
DFF = 2816
FFS = 2 * DFF // 4
NSH = 4
PW = 512
PG = 128
POOL_WINDOWS = (2, 4, 8, 16)
HALO = 16
SBW = 512
DH = 64
EPS = 1e-6
SCALE = 0.125
LOG2E = 1.4426950408889634
TA = 256
QB = 2
MIB = 1024 * 1024

LR, B1, B2, AEPS, WD, STEP = 0.001, 0.9, 0.999, 1e-08, 0.01, 10

_VM = pl.BlockSpec(memory_space=pltpu.VMEM)
_ANY = pl.BlockSpec(memory_space=pl.ANY)
MESH = pl.DeviceIdType.MESH


def _nn(a, b):
    return jnp.dot(a, b, preferred_element_type=F32)


def _nt(a, b):
    return lax.dot_general(a, b, (((1,), (1,)), ((), ())), preferred_element_type=F32)


def _tn(a, b):
    return lax.dot_general(a, b, (((0,), (0,)), ((), ())), preferred_element_type=F32)


def _params(sem, vmem_mib):
    return pltpu.CompilerParams(dimension_semantics=sem, vmem_limit_bytes=vmem_mib * MIB)


def _rows(tm, width):
    return pl.BlockSpec((tm, width), lambda i: (i, 0))


def _fixed(shape):
    return pl.BlockSpec(shape, lambda *_: (0,) * len(shape))


def _sds(shape, dtype):
    return pltpu.HBM(shape, dtype)


def _in_hbm(args):
    return [pltpu.with_memory_space_constraint(a, pltpu.HBM) for a in args]


def _stage(pairs):
    @pl.when(pl.program_id(0) == 0)
    def _():
        for src, dst in pairs:
            pltpu.sync_copy(src, dst)


def _vmem_like(*arrays):
    return [pltpu.VMEM(a.shape, a.dtype) for a in arrays]


class Exchange:
    def __init__(self, arrays, landing, aliases, n_sems, start, finish):
        self.arrays, self.landing, self.aliases, self.n_sems = list(arrays), list(landing), dict(aliases), n_sems
        self.start, self.finish = start, finish


def _join(a, b):
    na, la = len(a.arrays), len(a.landing)

    def both(fa, fb):
        def run(ins, outs, ssem, rsem):
            fa(ins[:na], outs[:la], ssem.at[pl.ds(0, a.n_sems)], rsem.at[pl.ds(0, a.n_sems)])
            fb(ins[na:], outs[la:], ssem.at[pl.ds(a.n_sems, b.n_sems)], rsem.at[pl.ds(a.n_sems, b.n_sems)])
        return run

    aliases = {**a.aliases, **{na + i: la + j for i, j in b.aliases.items()}}
    return Exchange(a.arrays + b.arrays, a.landing + b.landing, aliases, a.n_sems + b.n_sems,
                    both(a.start, b.start), both(a.finish, b.finish))


def _call(body, args, *, name, grid, in_specs, out_specs, out_shape, scratch_shapes=(), compiler_params=None,
          exchange=None, free=(), after=()):
    args = [a if i in free else pltpu.with_memory_space_constraint(a, pltpu.HBM) for i, a in enumerate(args)]
    if exchange is None:
        n_in = len(in_specs)

        def plain(*refs):
            body(*refs[:n_in], *refs[n_in + len(after):])

        return pl.pallas_call(plain, name=name, grid=grid, in_specs=list(in_specs) + [_ANY] * len(after),
                              out_specs=out_specs, out_shape=out_shape, scratch_shapes=list(scratch_shapes),
                              compiler_params=compiler_params)(*args, *after)
    ex = exchange
    n_in, n_out, n_scr = len(in_specs), len(out_specs), len(scratch_shapes)
    na, nl = len(ex.arrays), len(ex.landing)

    def hosted(*refs):
        at = [0]

        def take(n):
            at[0] += n
            return refs[at[0] - n:at[0]]

        k_in, _, e_in, k_out, e_out, k_scr = take(n_in), take(len(after)), take(na), take(n_out), take(nl), take(n_scr)
        ssem, rsem = take(2)
        ids = [pl.program_id(a) for a in range(len(grid))]
        first = functools.reduce(jnp.logical_and, [i == 0 for i in ids])
        last = functools.reduce(jnp.logical_and, [i == g - 1 for i, g in zip(ids, grid)])

        @pl.when(first)
        def _():
            ex.start(e_in, e_out, ssem, rsem)

        body(*k_in, *k_out, *k_scr)

        @pl.when(last)
        def _():
            ex.finish(e_in, e_out, ssem, rsem)

    outs = pl.pallas_call(
        hosted, name=name, grid=grid,
        in_specs=list(in_specs) + [_ANY] * (len(after) + na), out_specs=list(out_specs) + [_ANY] * nl,
        out_shape=list(out_shape) + ex.landing,
        scratch_shapes=list(scratch_shapes) + [pltpu.SemaphoreType.DMA((ex.n_sems,))] * 2,
        input_output_aliases={n_in + len(after) + i: n_out + j for i, j in ex.aliases.items()},
        compiler_params=compiler_params,
    )(*args, *after, *_in_hbm(ex.arrays))
    return outs[:n_out], outs[n_out:]


def _exchange_alone(ex, name):
    def body(*refs):
        na, nl = len(ex.arrays), len(ex.landing)
        ex.start(refs[:na], refs[na:na + nl], refs[-2], refs[-1])
        ex.finish(refs[:na], refs[na:na + nl], refs[-2], refs[-1])

    return pl.pallas_call(
        body, name=name, in_specs=[_ANY] * len(ex.arrays), out_specs=[_ANY] * len(ex.landing),
        out_shape=ex.landing, scratch_shapes=[pltpu.SemaphoreType.DMA((ex.n_sems,))] * 2,
        input_output_aliases=ex.aliases,
    )(*_in_hbm(ex.arrays))


_HBM = pl.BlockSpec(memory_space=pltpu.HBM)
_SEM = pl.BlockSpec(memory_space=pltpu.SEMAPHORE)
_EFFECT = pltpu.SideEffectType.DATAFLOW_SIDE_EFFECTING


def _scatter_copies(srcs, lands, ssems, rsems):
    x, y, c, chips = _place()
    return [_remote(srcs[w].at[2 * px + py], lands[w].at[k], ssems[3 * w + k], rsems[3 * w + k], (px, py, c))
            for w in range(len(srcs)) for k, (px, py) in enumerate(chips)]


def _scatter_start(parts, name):
    n, ncp = len(parts), 3 * len(parts)
    lands = [lax.empty((3,) + p.shape[1:], p.dtype) for p in parts]

    def body(*refs):
        srcs, land_refs = refs[:n], refs[n:2 * n]
        ssems, rsems = refs[2 * n:2 * n + ncp], refs[2 * n + ncp:2 * n + 2 * ncp]
        for cp in _scatter_copies(srcs, land_refs, ssems, rsems):
            cp.start()
        token = refs[-1]
        token[...] = jnp.zeros_like(token)

    outs = pl.pallas_call(
        body, name=name,
        out_shape=([pltpu.SemaphoreType.DMA(())] * (2 * ncp) + [pltpu.HBM(a.shape, a.dtype) for a in parts + lands]
                   + [jax.ShapeDtypeStruct((8, 128), F32)]),
        in_specs=[_HBM] * (2 * n), out_specs=[_SEM] * (2 * ncp) + [_HBM] * (2 * n) + [_VM],
        input_output_aliases={i: 2 * ncp + i for i in range(2 * n)},
        compiler_params=pltpu.CompilerParams(has_side_effects=_EFFECT),
    )(*_in_hbm(parts), *_in_hbm(lands))
    sems, thru, token = outs[:2 * ncp], outs[2 * ncp:2 * ncp + 2 * n], outs[-1]
    return sems, thru, token


def _scatter_wait(sems, thru, after, name):
    n = len(thru) // 2
    ncp = 3 * n

    def body(*refs):
        srcs, land_refs = refs[:n], refs[n:2 * n]
        ssems, rsems = refs[2 * n:2 * n + ncp], refs[2 * n + ncp:2 * n + 2 * ncp]
        for cp in _scatter_copies(srcs, land_refs, ssems, rsems):
            cp.wait_send()
            cp.wait_recv()

    outs = pl.pallas_call(
        body, name=name, out_shape=[pltpu.HBM(a.shape, a.dtype) for a in thru],
        in_specs=[_HBM] * (2 * n) + [_SEM] * (2 * ncp) + [_ANY] * len(after), out_specs=[_HBM] * (2 * n),
        input_output_aliases={i: i for i in range(2 * n)},
        compiler_params=pltpu.CompilerParams(has_side_effects=_EFFECT),
    )(*thru, *sems, *after)
    return outs[:n], outs[n:]


def _rms(x):
    r = lax.rsqrt(jnp.mean(x * x, axis=-1, keepdims=True) + EPS)
    return r, x * r


def _rms_bwd(dn, xr, r, gain):
    dng = dn * gain
    dx = r * (dng - xr * jnp.mean(dng * xr, axis=-1, keepdims=True))
    return dx, jnp.sum(dn * xr, axis=0, keepdims=True)


def _ffn_fwd(x, gain, wgu, wd, name, exchange=None):
    tm = 256

    def body(x_ref, g_ref, wgu_hbm, wd_hbm, h_ref, n_ref, gu_ref, a_ref, wgu_ref, wd_ref):
        _stage([(wgu_hbm, wgu_ref), (wd_hbm, wd_ref)])
        x = x_ref[...]
        _, xr = _rms(x)
        n = (xr * g_ref[...]).astype(BF16)
        n_ref[...] = n
        acc = jnp.zeros((tm, D), F32)
        for j in range(2):
            g = _nn(n, wgu_ref[j])
            u = _nn(n, wgu_ref[2 + j])
            gu_ref[:, j * FFS:(j + 1) * FFS] = g.astype(BF16)
            gu_ref[:, (2 + j) * FFS:(3 + j) * FFS] = u.astype(BF16)
            half_act = (0.5 * (g * jax.nn.sigmoid(g) * u)).astype(BF16)
            a_ref[:, j * FFS:(j + 1) * FFS] = half_act
            acc = acc + _nn(half_act, wd_ref[j * FFS:(j + 1) * FFS, :])
        h_ref[...] = x + acc

    return _call(
        body, (x, gain, wgu, wd), name=name, grid=(S // tm,),
        in_specs=[_rows(tm, D), _fixed((1, D)), _ANY, _ANY],
        out_specs=[_rows(tm, D), _rows(tm, D), _rows(tm, 4 * FFS), _rows(tm, DFF)],
        out_shape=[_sds((S, D), F32), _sds((S, D), BF16), _sds((S, 4 * FFS), BF16), _sds((S, DFF), BF16)],
        scratch_shapes=_vmem_like(wgu, wd),
        compiler_params=_params(("arbitrary",), 56), exchange=exchange)


def _ffn_bwd_act(dh, gu, wd, name, exchange=None, after=()):
    tm = 512

    def body(dh_ref, gu_ref, wd_hbm, dgu_ref, wd_ref):
        _stage([(wd_hbm, wd_ref)])
        dhb = dh_ref[...].astype(BF16)
        for j in range(2):
            g = gu_ref[:, j * FFS:(j + 1) * FFS].astype(F32)
            u = gu_ref[:, (2 + j) * FFS:(3 + j) * FFS].astype(F32)
            da = 0.5 * _nt(dhb, wd_ref[j * FFS:(j + 1) * FFS, :])
            sg = jax.nn.sigmoid(g)
            dgu_ref[:, j * FFS:(j + 1) * FFS] = (da * u * (sg * (1.0 + g * (1.0 - sg)))).astype(BF16)
            dgu_ref[:, (2 + j) * FFS:(3 + j) * FFS] = (da * (g * sg)).astype(BF16)

    res = _call(
        body, (dh, gu, wd), name=name, grid=(S // tm,),
        in_specs=[_rows(tm, D), _rows(tm, 4 * FFS), _ANY], out_specs=[_rows(tm, 4 * FFS)],
        out_shape=[_sds((S, 4 * FFS), BF16)], scratch_shapes=_vmem_like(wd),
        compiler_params=_params(("arbitrary",), 56), exchange=exchange, after=after)
    return res[0] if exchange is None else (res[0][0], res[1])


def _ffn_bwd_in(dh, x, gain, dgu, wgu, name, exchange=None, after=()):
    tm = 512

    def body(dh_ref, x_ref, g_ref, dgu_ref, wgu_hbm, dx_ref, dg_ref, wgu_ref):
        _stage([(wgu_hbm, wgu_ref)])
        dn = jnp.zeros((tm, D), F32)
        for j in range(NSH):
            dn = dn + _nt(dgu_ref[:, j * FFS:(j + 1) * FFS], wgu_ref[j])
        r, xr = _rms(x_ref[...])
        dx, dgain = _rms_bwd(dn, xr, r, g_ref[...])
        dx_ref[...] = dh_ref[...] + dx

        @pl.when(pl.program_id(0) == 0)
        def _():
            dg_ref[...] = jnp.zeros_like(dg_ref)

        dg_ref[...] += dgain

    return _call(
        body, (dh, x, gain, dgu, wgu), name=name, grid=(S // tm,),
        in_specs=[_rows(tm, D), _rows(tm, D), _fixed((1, D)), _rows(tm, 4 * FFS), _ANY],
        out_specs=[_rows(tm, D), _fixed((1, D))],
        out_shape=[_sds((S, D), F32), _sds((1, D), F32)], scratch_shapes=_vmem_like(wgu),
        compiler_params=_params(("arbitrary",), 56), exchange=exchange, after=after)


def _head(h, target, gain):
    tm = 512

    def body(h_ref, t_ref, g_ref, dh_ref, loss_ref, dg_ref):
        gain = g_ref[...]
        r, hr = _rms(h_ref[...])
        err = hr * gain - t_ref[...]
        dy = err * (1.0 / D)
        dh, dgain = _rms_bwd(dy, hr, r, gain)
        dh_ref[...] = dh

        @pl.when(pl.program_id(0) == 0)
        def _():
            dg_ref[...] = jnp.zeros_like(dg_ref)
            loss_ref[...] = jnp.zeros_like(loss_ref)

        dg_ref[...] += dgain
        loss_ref[...] += jnp.full((1, 128), (0.5 / D) * jnp.sum(err * err), F32)

    return pl.pallas_call(
        body, name="head", grid=(S // tm,),
        in_specs=[_rows(tm, D), _rows(tm, D), _fixed((1, D))],
        out_specs=[_rows(tm, D), _fixed((1, 128)), _fixed((1, D))],
        out_shape=[_sds((S, D), F32), _sds((1, 128), F32), _sds((1, D), F32)],
        compiler_params=_params(("arbitrary",), 40),
    )(*_in_hbm([h]), target, gain)


def _mix_in(h, gain, w_in):
    tm = 512

    def body(h_ref, g_ref, w_hbm, u_ref, xp_ref, q_ref, k_ref, v_ref, gp_ref, gs_ref, w_ref):
        _stage([(w_hbm, w_ref)])
        _, hr = _rms(h_ref[...])
        u = (hr * g_ref[...]).astype(BF16)
        u_ref[...] = u
        p0 = _nn(u, w_ref[0])
        xp_ref[...] = p0[:, :PW]
        q_ref[...] = p0[:, PW:].astype(BF16)
        p1 = _nn(u, w_ref[1])
        k_ref[...] = p1[:, :SBW].astype(BF16)
        v_ref[...] = p1[:, SBW:].astype(BF16)
        gp_ref[...] = jax.nn.sigmoid(_nn(u, w_ref[2])).astype(BF16)
        gs_ref[...] = jax.nn.sigmoid(_nn(u, w_ref[3])).astype(BF16)

    return pl.pallas_call(
        body, name="mix_in", grid=(S // tm,),
        in_specs=[_rows(tm, D), _fixed((1, D)), _ANY],
        out_specs=[_rows(tm, D), _rows(tm, PW), _rows(tm, SBW), _rows(tm, SBW), _rows(tm, SBW),
                   _rows(tm, D), _rows(tm, D)],
        out_shape=[_sds((S, D), BF16), _sds((S, PW), F32), _sds((S, SBW), BF16), _sds((S, SBW), BF16),
                   _sds((S, SBW), BF16), _sds((S, D), BF16), _sds((S, D), BF16)],
        scratch_shapes=_vmem_like(w_in),
        compiler_params=_params(("arbitrary",), 48),
    )(*_in_hbm([h]), gain, *_in_hbm([w_in]))


def _hilo_dot(x, tri):
    hi = x.astype(BF16)
    lo = (x - hi.astype(F32)).astype(BF16)
    return _nn(hi, tri) + _nn(lo, tri)


def _log_terms(qk):
    z2 = qk * (SCALE * LOG2E)
    lb = jnp.minimum(z2, 0.0) - jnp.log2(1.0 + jnp.exp2(-jnp.abs(z2)))
    return lb, lb - z2


def _head_masks():
    lane = lax.broadcasted_iota(jnp.int32, (1, 2 * DH), 1)
    return (lane < DH, lane >= DH)


def _attn_fwd(q, k, v, exchange=None):
    T = TA

    def body(q_ref, k_ref, v_ref, o_ref, c_ref):
        i2 = 2 * pl.program_id(1)
        row = lax.broadcasted_iota(jnp.int32, (T, T), 0)
        col = lax.broadcasted_iota(jnp.int32, (T, T), 1)
        after = (row > col).astype(BF16)
        causal = col < row
        masks = _head_masks()
        qms = {}
        for b in range(QB):
            q2 = q_ref[b * T:(b + 1) * T, :]
            for h, hm in enumerate(masks):
                qms[b, h] = jnp.where(hm, q2, jnp.zeros_like(q2))

        def blocks(keys, pairs, carries, os):
            ks, vms = [], []
            for j in keys:
                rows = pl.ds(pl.multiple_of(j * T, T), T)
                vj = v_ref[rows, :]
                ks.append(k_ref[rows, :])
                vms.append([jnp.where(hm, vj, jnp.zeros_like(vj)) for hm in masks])
            units = [(n, h) for n in range(len(pairs)) for h in range(2)]
            qks = {(n, h): _nt(qms[pairs[n][0], h], ks[pairs[n][1]]) for n, h in units}
            lbs, l1ms = {}, {}
            for u in units:
                lbs[u], l1m = _log_terms(qks[u])
                l1ms[u] = jnp.where(causal, l1m, 0.0) if pairs[u[0]][2] else l1m
            cins = {u: _hilo_dot(l1ms[u], after) for u in units}
            carries, os = dict(carries), list(os)
            for n, h in units:
                b, key, diag = pairs[n]
                a = jnp.exp2(lbs[n, h] + cins[n, h] + carries[b, h])
                if diag:
                    a = jnp.where(causal, a, 0.0)
                os[b] = os[b] + _nn(a.astype(BF16), vms[key][h])
                carries[b, h] = carries[b, h] + jnp.sum(l1ms[n, h], axis=1, keepdims=True)
            return carries, tuple(os)

        carries = {(b, h): jnp.zeros((T, 1), F32) for b in range(QB) for h in range(2)}
        os = tuple(jnp.zeros((T, 2 * DH), F32) for _ in range(QB))
        carries, os = blocks([i2 + 1, i2], [(1, 0, True), (0, 1, True), (1, 1, False)], carries, os)
        carries, os = lax.fori_loop(
            0, i2, lambda jj, c: blocks([i2 - 1 - jj], [(0, 0, False), (1, 0, False)], c[0], c[1]), (carries, os))
        for b in range(QB):
            o_ref[b * T:(b + 1) * T, :] = os[b].astype(BF16)
            c_ref[b * T:(b + 1) * T, :] = jnp.where(masks[0], carries[b, 0], carries[b, 1])

    blk = pl.BlockSpec((QB * T, 2 * DH), lambda p, i: (i, p))
    full = pl.BlockSpec((S, 2 * DH), lambda p, i: (0, p))
    return _call(
        body, (q, k, v), name="attn_fwd", grid=(SBW // (2 * DH), S // (QB * T)),
        in_specs=[blk, full, full], out_specs=[blk, blk],
        out_shape=[_sds((S, SBW), BF16), _sds((S, SBW), F32)],
        compiler_params=_params(("arbitrary", "arbitrary"), 40), exchange=exchange)


def _attn_bwd(q, k, v, do, ctot, exchange=None):
    T = TA
    nq = S // (QB * T)

    def body(q_ref, k_ref, v_ref, do_ref, c_ref, dq_ref, dk_ref, dv_ref, dk_acc, dv_acc):
        step = pl.program_id(1)
        i2 = 2 * step

        @pl.when(step == 0)
        def _():
            dk_acc[...] = jnp.zeros_like(dk_acc)
            dv_acc[...] = jnp.zeros_like(dv_acc)

        row = lax.broadcasted_iota(jnp.int32, (T, T), 0)
        col = lax.broadcasted_iota(jnp.int32, (T, T), 1)
        upto = (row <= col).astype(BF16)
        before = (row < col).astype(BF16)
        causal = col < row
        masks = _head_masks()
        qms, doms, ctots = {}, {}, {}
        for b in range(QB):
            q2, do2 = q_ref[b * T:(b + 1) * T, :], do_ref[b * T:(b + 1) * T, :]
            for h, hm in enumerate(masks):
                qms[b, h] = jnp.where(hm, q2, jnp.zeros_like(q2))
                doms[b, h] = jnp.where(hm, do2, jnp.zeros_like(do2))
                ctots[b, h] = c_ref[b * T:(b + 1) * T, h * DH:h * DH + 1]

        def blocks(keys, pairs, sums, dqs):
            rows = [pl.ds(pl.multiple_of(j * T, T), T) for j in keys]
            ks, vs = [k_ref[r, :] for r in rows], [v_ref[r, :] for r in rows]
            kms = [[jnp.where(hm, kj, jnp.zeros_like(kj)) for hm in masks] for kj in ks]
            units = [(n, h) for n in range(len(pairs)) for h in range(2)]
            qks = {(n, h): _nt(qms[pairs[n][0], h], ks[pairs[n][1]]) for n, h in units}
            das = {(n, h): _nt(doms[pairs[n][0], h], vs[pairs[n][1]]) for n, h in units}
            lbs, l1ms = {}, {}
            for u in units:
                lbs[u], l1m = _log_terms(qks[u])
                l1ms[u] = jnp.where(causal, l1m, 0.0) if pairs[u[0]][2] else l1m
            pins = {u: _hilo_dot(l1ms[u], upto) for u in units}
            sums = dict(sums)
            a_s, dls, cps = {}, {}, {}
            for n, h in units:
                b, _, diag = pairs[n]
                cl, cp = sums[b, h]
                a = jnp.exp2(lbs[n, h] + (ctots[b, h] - cl) - pins[n, h])
                if diag:
                    a = jnp.where(causal, a, 0.0)
                a_s[n, h] = a.astype(BF16)
                dls[n, h] = das[n, h] * a
                cps[n, h] = cp
                sums[b, h] = (cl + jnp.sum(l1ms[n, h], axis=1, keepdims=True),
                              cp + jnp.sum(dls[n, h], axis=1, keepdims=True))
            pexs = {u: _hilo_dot(dls[u], before) for u in units}
            dzbs = {}
            for u in units:
                dz = dls[u] - jnp.exp2(lbs[u]) * (dls[u] + pexs[u] + cps[u])
                if pairs[u[0]][2]:
                    dz = jnp.where(causal, dz, 0.0)
                dzbs[u] = dz.astype(BF16)
            dqs = list(dqs)
            for n, h in units:
                dqs[pairs[n][0]] = dqs[pairs[n][0]] + _nn(dzbs[n, h], kms[pairs[n][1]][h])
            for key, r in enumerate(rows):
                mine = [(n, h) for n, h in units if pairs[n][1] == key]
                dk_acc[r, :] += functools.reduce(jnp.add, [_tn(dzbs[u], qms[pairs[u[0]][0], u[1]]) for u in mine])
                dv_acc[r, :] += functools.reduce(jnp.add, [_tn(a_s[u], doms[pairs[u[0]][0], u[1]]) for u in mine])
            return sums, tuple(dqs)

        zero = jnp.zeros((T, 1), F32)
        sums = {(b, h): (zero, zero) for b in range(QB) for h in range(2)}
        dqs = tuple(jnp.zeros((T, 2 * DH), F32) for _ in range(QB))
        sums, dqs = lax.fori_loop(
            0, i2, lambda j, c: blocks([j], [(0, 0, False), (1, 0, False)], c[0], c[1]), (sums, dqs))
        _, dqs = blocks([i2, i2 + 1], [(0, 0, True), (1, 0, False), (1, 1, True)], sums, dqs)
        for b in range(QB):
            dq_ref[b * T:(b + 1) * T, :] = (dqs[b] * SCALE).astype(BF16)

        @pl.when(step == nq - 1)
        def _():
            dk_ref[...] = (dk_acc[...] * SCALE).astype(BF16)
            dv_ref[...] = dv_acc[...].astype(BF16)

    blk = pl.BlockSpec((QB * T, 2 * DH), lambda p, i: (i, p))
    full = pl.BlockSpec((S, 2 * DH), lambda p, i: (0, p))
    return _call(
        body, (q, k, v, do, ctot), name="attn_bwd", grid=(SBW // (2 * DH), nq),
        in_specs=[blk, full, full, blk, blk], out_specs=[blk, full, full],
        out_shape=[_sds((S, SBW), BF16), _sds((S, SBW), BF16), _sds((S, SBW), BF16)],
        scratch_shapes=[pltpu.VMEM((S, 2 * DH), F32), pltpu.VMEM((S, 2 * DH), F32)],
        compiler_params=_params(("arbitrary", "arbitrary"), 40), exchange=exchange)


def _pool_counts(first_row, tm):
    pos = first_row + lax.broadcasted_iota(jnp.int32, (tm, 1), 0)
    return [jnp.minimum(pos + 1, w).astype(F32) for w in POOL_WINDOWS]


def _mix_out(h, xp, o_sb, gp, gs, w_group, scale, w_bp, w_ba, w_out):
    tm = 512

    def body(h_ref, xp_ref, o_ref, gp_ref, gs_ref, wg_hbm, sc_ref, wbp_hbm, wba_hbm, wo_hbm,
             h2_ref, pm_ref, p_ref, yp_ref, ys_ref, m_ref, halo, wg_ref, wbp_ref, wba_ref, wo_ref):
        _stage([(wg_hbm, wg_ref), (wbp_hbm, wbp_ref), (wba_hbm, wba_ref), (wo_hbm, wo_ref)])
        i = pl.program_id(0)

        @pl.when(i == 0)
        def _():
            halo[...] = jnp.zeros_like(halo)

        xp = xp_ref[...]
        ext = jnp.concatenate([halo[...], xp], axis=0)
        halo[...] = xp[tm - HALO:, :]
        counts = _pool_counts(i * tm, tm)
        for gi in range(len(POOL_WINDOWS)):
            lanes = slice(gi * PG, (gi + 1) * PG)
            win = ext[:, lanes]
            for step in range(gi + 1):
                win = win + pltpu.roll(win, 1 << step, 0)
            pm = (win[HALO:, :] / counts[gi] - xp[:, lanes]).astype(BF16)
            pm_ref[:, lanes] = pm
            p_ref[:, lanes] = (_nn(pm, wg_ref[gi]) * sc_ref[:, lanes]).astype(BF16)
        pb = p_ref[...]
        ob = o_ref[...]
        for j in range(NSH):
            cols = slice(j * (D // NSH), (j + 1) * (D // NSH))
            yp = _nn(pb, wbp_ref[j])
            ys = _nn(ob, wba_ref[j])
            yp_ref[:, cols] = yp.astype(BF16)
            ys_ref[:, cols] = ys.astype(BF16)
            m_ref[:, cols] = (gp_ref[:, cols].astype(F32) * yp + gs_ref[:, cols].astype(F32) * ys).astype(BF16)
        h2_ref[...] = h_ref[...] + _nn(m_ref[...], wo_ref[...])

    return pl.pallas_call(
        body, name="mix_out", grid=(S // tm,),
        in_specs=[_rows(tm, D), _rows(tm, PW), _rows(tm, SBW), _rows(tm, D), _rows(tm, D),
                  _ANY, _fixed((1, PW)), _ANY, _ANY, _ANY],
        out_specs=[_rows(tm, D), _rows(tm, PW), _rows(tm, PW), _rows(tm, D), _rows(tm, D), _rows(tm, D)],
        out_shape=[_sds((S, D), F32), _sds((S, PW), BF16), _sds((S, PW), BF16), _sds((S, D), BF16),
                   _sds((S, D), BF16), _sds((S, D), BF16)],
        scratch_shapes=[pltpu.VMEM((HALO, PW), F32)] + _vmem_like(w_group, w_bp, w_ba, w_out),
        compiler_params=_params(("arbitrary",), 48),
    )(*_in_hbm([h, xp, o_sb, gp, gs]), w_group, scale, *_in_hbm([w_bp, w_ba, w_out]))


def _mix_bwd_out(dh, gp, gs, yp, ys, pm, w_group, scale, w_bp, w_ba, w_out, exchange=None):
    tm = 512
    nt = S // tm

    def body(dh_ref, gp_ref, gs_ref, yp_ref, ys_ref, pm_ref, wg_hbm, sc_ref, wbp_hbm, wba_hbm, wo_hbm,
             dlg_ref, dyp_ref, dys_ref, do_ref, dyg_ref, dxp_ref, dsc_ref, halo, wg_ref, wbp_ref, wba_ref, wo_ref):
        _stage([(wg_hbm, wg_ref), (wbp_hbm, wbp_ref), (wba_hbm, wba_ref), (wo_hbm, wo_ref)])
        step = pl.program_id(0)

        @pl.when(step == 0)
        def _():
            halo[...] = jnp.zeros_like(halo)
            dsc_ref[...] = jnp.zeros_like(dsc_ref)

        dm = _nt(dh_ref[...].astype(BF16), wo_ref[...])
        gp = gp_ref[...].astype(F32)
        gs = gs_ref[...].astype(F32)
        yp = yp_ref[...].astype(F32)
        ys = ys_ref[...].astype(F32)
        dlg_ref[:, :D] = (dm * yp * gp * (1.0 - gp)).astype(BF16)
        dlg_ref[:, D:] = (dm * ys * gs * (1.0 - gs)).astype(BF16)
        dyp_ref[...] = (dm * gp).astype(BF16)
        dys_ref[...] = (dm * gs).astype(BF16)
        dp = jnp.zeros((tm, PW), F32)
        do = jnp.zeros((tm, SBW), F32)
        for j in range(NSH):
            cols = slice(j * (D // NSH), (j + 1) * (D // NSH))
            dp = dp + _nt(dyp_ref[:, cols], wbp_ref[j])
            do = do + _nt(dys_ref[:, cols], wba_ref[j])
        do_ref[...] = do.astype(BF16)
        counts = _pool_counts((nt - 1 - step) * tm, tm)
        dscale = []
        for gi in range(len(POOL_WINDOWS)):
            lanes = slice(gi * PG, (gi + 1) * PG)
            dpg = dp[:, lanes]
            dscale.append(jnp.sum(dpg * _nn(pm_ref[:, lanes], wg_ref[gi]), axis=0, keepdims=True))
            dyg = (dpg * sc_ref[:, lanes]).astype(BF16)
            dyg_ref[:, lanes] = dyg
            dpm = _nt(dyg, wg_ref[gi])
            per = dpm / counts[gi]
            win = jnp.concatenate([per, halo[:, lanes]], axis=0)
            halo[:, lanes] = per[:HALO, :]
            for s in range(gi + 1):
                win = win + pltpu.roll(win, tm + HALO - (1 << s), 0)
            dxp_ref[:, lanes] = (win[:tm, :] - dpm).astype(BF16)
        dsc_ref[...] += jnp.concatenate(dscale, axis=1)

    rev = lambda width: pl.BlockSpec((tm, width), lambda i: (nt - 1 - i, 0))
    return _call(
        body, (dh, gp, gs, yp, ys, pm, w_group, scale, w_bp, w_ba, w_out), name="mix_bwd_out", grid=(nt,),
        in_specs=[rev(D), rev(D), rev(D), rev(D), rev(D), rev(PW), _ANY, _fixed((1, PW)), _ANY, _ANY, _ANY],
        out_specs=[rev(2 * D), rev(D), rev(D), rev(SBW), rev(PW), rev(PW), _fixed((1, PW))],
        out_shape=[_sds((S, 2 * D), BF16), _sds((S, D), BF16), _sds((S, D), BF16), _sds((S, SBW), BF16),
                   _sds((S, PW), BF16), _sds((S, PW), BF16), _sds((1, PW), F32)],
        scratch_shapes=[pltpu.VMEM((HALO, PW), F32)] + _vmem_like(w_group, w_bp, w_ba, w_out),
        compiler_params=_params(("arbitrary",), 48), exchange=exchange)


def _mix_bwd_in(dh, h, gain, pieces, w_in, exchange=None):
    tm = 512
    widths = [p.shape[1] for p in pieces]

    def body(dh_ref, h_ref, g_ref, *rest):
        piece_refs, (w_hbm, dx_ref, dg_ref, dp_ref, w_ref) = rest[:len(pieces)], rest[len(pieces):]
        _stage([(w_hbm, w_ref)])
        at = 0
        for ref, width in zip(piece_refs, widths):
            dp_ref[:, at:at + width] = ref[...]
            at += width
        du = jnp.zeros((tm, D), F32)
        for j in range(NSH):
            du = du + _nt(dp_ref[:, j * D:(j + 1) * D], w_ref[j])
        r, hr = _rms(h_ref[...])
        dx, dgain = _rms_bwd(du, hr, r, g_ref[...])
        dx_ref[...] = dh_ref[...] + dx

        @pl.when(pl.program_id(0) == 0)
        def _():
            dg_ref[...] = jnp.zeros_like(dg_ref)

        dg_ref[...] += dgain

    return _call(
        body, (dh, h, gain, *pieces, w_in), name="mix_bwd_in", grid=(S // tm,),
        in_specs=[_rows(tm, D), _rows(tm, D), _fixed((1, D))] + [_rows(tm, w) for w in widths] + [_ANY],
        out_specs=[_rows(tm, D), _fixed((1, D)), _rows(tm, 4 * D)],
        out_shape=[_sds((S, D), F32), _sds((1, D), F32), _sds((S, 4 * D), BF16)],
        scratch_shapes=_vmem_like(w_in),
        compiler_params=_params(("arbitrary",), 48), exchange=exchange)


def _wgrad(a, b, nblk, ti, name, out_dtype=BF16, exchange=None, after=()):
    ka, n = a.shape[1], b.shape[1]
    ns = n // nblk

    def body(a_ref, b_ref, o_ref):
        o_ref[...] = _tn(a_ref[...].astype(BF16), b_ref[...].astype(BF16)).astype(out_dtype)

    res = _call(
        body, (a, b), name=name, grid=(nblk, ka // ti),
        in_specs=[pl.BlockSpec((S, ti), lambda j, i: (0, i)), pl.BlockSpec((S, ns), lambda j, i: (0, j))],
        out_specs=[pl.BlockSpec((None, ti, ns), lambda j, i: (j, i, 0))],
        out_shape=[_sds((nblk, ka, ns), out_dtype)],
        compiler_params=_params(("arbitrary", "arbitrary"), 56), exchange=exchange, after=after)
    return res[0] if exchange is None else (res[0][0], res[1])


def _wgrad_groups(pm, dyg):
    def body(a_ref, b_ref, o_ref):
        o_ref[...] = _tn(a_ref[...], b_ref[...])

    col = pl.BlockSpec((S, PG), lambda g: (0, g))
    return pl.pallas_call(
        body, name="wgrad_groups", grid=(PW // PG,),
        in_specs=[col, col], out_specs=pl.BlockSpec((None, PG, PG), lambda g: (g, 0, 0)),
        out_shape=_sds((PW // PG, PG, PG), F32),
        compiler_params=_params(("arbitrary",), 32),
    )(*_in_hbm([pm, dyg]))


def _place():
    x, y, c = lax.axis_index("x"), lax.axis_index("y"), lax.axis_index("c")
    chips = [(1 - x, y), (x, 1 - y), (1 - x, 1 - y)]
    return x, y, c, chips


def _remote(src, dst, ssem, rsem, dev):
    return pltpu.make_async_remote_copy(src_ref=src, dst_ref=dst, send_sem=ssem, recv_sem=rsem,
                                        device_id=dev, device_id_type=MESH)


def _cast_into_block(w, me_idx, name):
    rows, cols = w.shape
    tr = _row_block(rows)

    def body(me_ref, w_ref, o_ref):
        o_ref[...] = w_ref[...].astype(BF16)

    return pl.pallas_call(
        body, name=name, out_shape=_sds((NSH, rows, cols), BF16),
        grid_spec=pltpu.PrefetchScalarGridSpec(
            num_scalar_prefetch=1, grid=(rows // tr,),
            in_specs=[pl.BlockSpec((tr, cols), lambda r, me: (r, 0))],
            out_specs=pl.BlockSpec((None, tr, cols), lambda r, me: (me[0], r, 0))),
        compiler_params=_params(("arbitrary",), 32),
    )(me_idx, w)


def _ex_gather(bufs):
    n = len(bufs)
    per = 8

    def plan(outs, ssem, rsem, w):
        x, y, c, _ = _place()
        sib, nbr_x, nbr_y = (x, y, 1 - c), (1 - x, y, c), (x, 1 - y, c)
        half = outs[w].shape[1] // 2
        quarter = half // 2
        sem = lambda k: (ssem.at[per * w + k], rsem.at[per * w + k])
        rows = lambda blk, start, size: outs[w].at[blk, pl.ds(start, size)]
        mine = rows(2 * x + y, c * half, half)
        from_x = rows(2 * (1 - x) + y, c * half, half)
        from_y = rows(2 * x + (1 - y), c * half, half)
        diag = 2 * (1 - x) + (1 - y)
        pass_y = rows(2 * (1 - x) + y, c * half, quarter)
        pass_x = rows(2 * x + (1 - y), c * half + quarter, quarter)
        diag_0, diag_1 = rows(diag, c * half, quarter), rows(diag, c * half + quarter, quarter)
        first = [_remote(mine, mine, *sem(0), nbr_x), _remote(mine, mine, *sem(1), nbr_y)]
        arrivals = [
            (_remote(from_x, from_x, *sem(0), nbr_x),
             [_remote(pass_y, pass_y, *sem(2), nbr_y), _remote(from_x, from_x, *sem(4), sib)]),
            (_remote(from_y, from_y, *sem(1), nbr_y),
             [_remote(pass_x, pass_x, *sem(3), nbr_x), _remote(from_y, from_y, *sem(5), sib)]),
            (_remote(diag_0, diag_0, *sem(2), nbr_y), [_remote(diag_0, diag_0, *sem(6), sib)]),
            (_remote(diag_1, diag_1, *sem(3), nbr_x), [_remote(diag_1, diag_1, *sem(7), sib)]),
        ]
        other = (1 - c) * half
        from_sibling = [
            _remote(rows(2 * (1 - x) + y, other, half), rows(2 * (1 - x) + y, other, half), *sem(4), sib),
            _remote(rows(2 * x + (1 - y), other, half), rows(2 * x + (1 - y), other, half), *sem(5), sib),
            _remote(rows(diag, other, quarter), rows(diag, other, quarter), *sem(6), sib),
            _remote(rows(diag, other + quarter, quarter), rows(diag, other + quarter, quarter), *sem(7), sib),
        ]
        return first, arrivals, from_sibling

    def start(ins, outs, ssem, rsem):
        x, y, c, _ = _place()
        for w in range(n):
            half = outs[w].shape[1] // 2
            mine = outs[w].at[2 * x + y, pl.ds(c * half, half)]
            _remote(mine, mine, ssem.at[per * w], rsem.at[per * w], (1 - x, y, c)).start()
            _remote(mine, mine, ssem.at[per * w + 1], rsem.at[per * w + 1], (x, 1 - y, c)).start()

    def finish(ins, outs, ssem, rsem):
        plans = [plan(outs, ssem, rsem, w) for w in range(n)]
        started = []
        for direct in (True, False):
            for first, arrivals, _ in plans:
                for arrived, onward in (arrivals[:2] if direct else arrivals[2:]):
                    arrived.wait_recv()
                    for cp in onward:
                        cp.start()
                    started += onward
        for first, _, from_sibling in plans:
            for cp in from_sibling:
                cp.wait_recv()
            started += first
        for cp in started:
            cp.wait_send()

    return Exchange(bufs, [_sds(b.shape, b.dtype) for b in bufs], {w: w for w in range(n)}, per * n, start, finish)


def _ex_gather_direct(bufs):
    n = len(bufs)

    def copies(outs, ssem, rsem, only_first=False):
        x, y, c, chips = _place()
        me, sib = 2 * x + y, (x, y, 1 - c)
        first, relay, last = [], [], []
        for w in range(n):
            half = outs[w].shape[1] // 2
            mine = outs[w].at[me, pl.ds(c * half, half)]
            for k, (px, py) in enumerate(chips):
                sems = (ssem.at[6 * w + k], rsem.at[6 * w + k])
                sib_sems = (ssem.at[6 * w + 3 + k], rsem.at[6 * w + 3 + k])
                first.append(_remote(mine, mine, *sems, (px, py, c)))
                if only_first:
                    continue
                got = outs[w].at[2 * px + py, pl.ds(c * half, half)]
                relay.append((_remote(got, got, *sems, (px, py, c)), _remote(got, got, *sib_sems, sib)))
                theirs = outs[w].at[2 * px + py, pl.ds((1 - c) * half, half)]
                last.append(_remote(theirs, theirs, *sib_sems, sib))
        return first, relay, last

    def start(ins, outs, ssem, rsem):
        for cp in copies(outs, ssem, rsem, only_first=True)[0]:
            cp.start()

    def finish(ins, outs, ssem, rsem):
        first, relay, last = copies(outs, ssem, rsem)
        for arrived, onward in relay:
            arrived.wait_recv()
            onward.start()
        for cp in last:
            cp.wait_recv()
        for cp in first:
            cp.wait_send()
        for _, onward in relay:
            onward.wait_send()

    return Exchange(bufs, [_sds(b.shape, b.dtype) for b in bufs], {w: w for w in range(n)}, 6 * n, start, finish)


def _simple_exchange(arrays, landing, aliases, make_copies):
    def start(ins, outs, ssem, rsem):
        for cp, _ in make_copies(ins, outs, ssem, rsem, False):
            cp.start()

    def finish(ins, outs, ssem, rsem):
        cps = make_copies(ins, outs, ssem, rsem, True)
        for _, landed in cps:
            landed.wait_recv()
        for cp, _ in cps:
            cp.wait_send()

    return Exchange(arrays, landing, aliases, len(arrays) * 3, start, finish)


def _ex_pair_swap(grads):
    def make(ins, outs, ssem, rsem, landing):
        x, y, c, _ = _place()
        cps = [_remote(ins[w].at[:, 1 - c], outs[w], ssem.at[w], rsem.at[w], (x, y, 1 - c))
               for w in range(len(grads))]
        return [(cp, cp) for cp in cps]

    return _simple_exchange(grads, [_sds((NSH,) + g.shape[2:], g.dtype) for g in grads], {}, make)


def _ex_scatter(parts):
    def make(ins, outs, ssem, rsem, landing):
        x, y, c, chips = _place()
        out = []
        for w in range(len(parts)):
            for k, (px, py) in enumerate(chips):
                sems = (ssem.at[3 * w + k], rsem.at[3 * w + k])
                out.append((_remote(ins[w].at[2 * px + py], outs[w].at[k], *sems, (px, py, c)),
                            _remote(outs[w].at[k], outs[w].at[k], *sems, (px, py, c)) if landing else None))
        return out

    return _simple_exchange(parts, [_sds((3,) + p.shape[1:], p.dtype) for p in parts], {}, make)


def _ex_share(bufs):
    def make(ins, outs, ssem, rsem, landing):
        x, y, c, _ = _place()
        sib = (x, y, 1 - c)
        return [(_remote(outs[w].at[c], outs[w].at[c], ssem.at[w], rsem.at[w], sib),
                 _remote(outs[w].at[1 - c], outs[w].at[1 - c], ssem.at[w], rsem.at[w], sib) if landing else None)
                for w in range(len(bufs))]

    return _simple_exchange(bufs, [_sds(b.shape, b.dtype) for b in bufs], {w: w for w in range(len(bufs))}, make)


def _gather_small(block, ex):
    m_per, n = block.shape
    na, nl = len(ex.arrays), len(ex.landing)

    def body(x_ref, *refs):
        e_in, out_ref, e_out = refs[:na], refs[na], refs[na + 1:na + 1 + nl]
        ssem, rsem, lsem, e_ssem, e_rsem = refs[na + 1 + nl:]
        ex.start(e_in, e_out, e_ssem, e_rsem)
        x, y, c, chips = _place()
        me, sib = (x, y, c), (x, y, 1 - c)

        def rows(px, py, pc):
            return out_ref.at[pl.ds((4 * px + 2 * py + pc) * m_per, m_per), :]

        def copy(k, blk, to, src=None):
            return _remote(rows(*blk) if src is None else src, rows(*blk), ssem.at[k], rsem.at[k], to)

        mine = pltpu.make_async_copy(x_ref, rows(*me), lsem)
        mine.start()
        first = [copy(0, me, sib, src=x_ref)]
        first += [copy(1 + j, me, (*chip, c), src=x_ref) for j, chip in enumerate(chips)]
        for cp in first:
            cp.start()
        passed = [copy(4 + j, (*chip, c), sib) for j, chip in enumerate(chips)]
        for j, chip in enumerate(chips):
            copy(1 + j, (*chip, c), me).wait_recv()
            passed[j].start()
        copy(0, sib, me).wait_recv()
        for j, chip in enumerate(chips):
            copy(4 + j, (*chip, 1 - c), me).wait_recv()
        for cp in first + passed:
            cp.wait_send()
        mine.wait()
        ex.finish(e_in, e_out, e_ssem, e_rsem)

    outs = pl.pallas_call(
        body, name="gather_small", out_shape=[jax.ShapeDtypeStruct((8 * m_per, n), block.dtype)] + ex.landing,
        in_specs=[_VM] + [_ANY] * na, out_specs=[_VM] + [_ANY] * nl,
        scratch_shapes=[pltpu.SemaphoreType.DMA((7,)), pltpu.SemaphoreType.DMA((7,)), pltpu.SemaphoreType.DMA]
        + [pltpu.SemaphoreType.DMA((ex.n_sems,))] * 2,
        input_output_aliases={1 + i: 1 + j for i, j in ex.aliases.items()},
    )(block, *_in_hbm(ex.arrays))
    return outs[0], outs[1:]


def _row_block(rows):
    return max(t for t in range(16, 257, 16) if rows % t == 0)


def _pair_sum(grad, got, c_idx, name):
    _, _, half, cols = grad.shape
    tr = _row_block(half)

    def body(c_ref, a_ref, b_ref, o_ref):
        o_ref[...] = (a_ref[...].astype(F32) + b_ref[...].astype(F32)).astype(BF16)

    return pl.pallas_call(
        body, name=name, out_shape=_sds((NSH, half, cols), BF16),
        grid_spec=pltpu.PrefetchScalarGridSpec(
            num_scalar_prefetch=1, grid=(NSH, half // tr),
            in_specs=[pl.BlockSpec((None, None, tr, cols), lambda j, r, c: (j, c[0], r, 0)),
                      pl.BlockSpec((None, tr, cols), lambda j, r, c: (j, r, 0))],
            out_specs=pl.BlockSpec((None, tr, cols), lambda j, r, c: (j, r, 0))),
        compiler_params=_params(("arbitrary", "arbitrary"), 32),
    )(c_idx, *_in_hbm([grad, got]))


def _chip_sum(own, got, place, name):
    _, half, cols = own.shape
    tr = _row_block(half)

    def body(place_ref, own_ref, got_ref, o_ref):
        acc = own_ref[...].astype(F32)
        for k in range(3):
            acc = acc + got_ref[k].astype(F32)
        o_ref[...] = acc

    return pl.pallas_call(
        body, name=name, out_shape=_sds((2, half, cols), F32),
        grid_spec=pltpu.PrefetchScalarGridSpec(
            num_scalar_prefetch=1, grid=(half // tr,),
            in_specs=[pl.BlockSpec((None, tr, cols), lambda r, p: (p[0], r, 0)),
                      pl.BlockSpec((3, tr, cols), lambda r, p: (0, r, 0))],
            out_specs=pl.BlockSpec((None, tr, cols), lambda r, p: (p[1], r, 0))),
        compiler_params=_params(("arbitrary",), 32),
    )(place, *_in_hbm([own, got]))


def _adamw_math(w, g, m, v):
    m = B1 * m + (1.0 - B1) * g
    v = B2 * v + (1.0 - B2) * (g * g)
    m_hat = m / (1.0 - B1 ** STEP)
    v_hat = v / (1.0 - B2 ** STEP)
    return -LR * (m_hat / (jnp.sqrt(v_hat) + AEPS) + WD * w), m, v


def _adamw(w, g, m, v, name, after=()):
    rows, cols = w.shape
    tr = _row_block(rows)

    def body(w_ref, g_ref, m_ref, v_ref, go_ref, d_ref, nm_ref, nv_ref):
        g = g_ref[...]
        go_ref[...] = g
        d_ref[...], nm_ref[...], nv_ref[...] = _adamw_math(w_ref[...], g, m_ref[...], v_ref[...])

    blk = pl.BlockSpec((tr, cols), lambda r: (r, 0))
    return _call(
        body, (w, g, m, v), name=name, grid=(rows // tr,), out_shape=[_sds(w.shape, F32)] * 4,
        in_specs=[blk] * 4, out_specs=[blk] * 4,
        compiler_params=_params(("arbitrary",), 32), free=(0, 2, 3), after=after)


def _small_update(gathered, w, m, v):
    rows = w.shape[0]

    def body(ga_ref, w_ref, m_ref, v_ref, g_ref, d_ref, nm_ref, nv_ref):
        g = ga_ref[0:rows, :]
        for dev in range(1, 8):
            g = g + ga_ref[dev * rows:(dev + 1) * rows, :]
        g_ref[...] = g
        d_ref[...], nm_ref[...], nv_ref[...] = _adamw_math(w_ref[...], g, m_ref[...], v_ref[...])

    return pl.pallas_call(
        body, name="small_update", out_shape=[jax.ShapeDtypeStruct(w.shape, F32)] * 4,
        in_specs=[_VM] * 4, out_specs=[_VM] * 4,
    )(gathered, w, m, v)


SMALL = ("ffn1_norm", "mix_norm", "ffn2_norm", "final_norm", "pool_scale", "pool_w_group", "loss")
BIG = ("ffn1_w_gate_up", "ffn1_w_down", "w_in", "w_branch_pool", "w_branch_attn", "w_out",
       "ffn2_w_gate_up", "ffn2_w_down")
ORDER = ("ffn1_norm", "ffn1_w_gate_up", "ffn1_w_down", "mix_norm", "w_in", "pool_w_group", "pool_scale",
         "w_branch_pool", "w_branch_attn", "w_out", "ffn2_norm", "ffn2_w_gate_up", "ffn2_w_down", "final_norm")
SMALL_ROWS = 560


def _pack_small(t):
    parts = []
    for k in SMALL:
        rows = t[k].reshape(-1, 128) if k in t else jnp.zeros((1, 128), F32)
        parts.append(jnp.pad(rows, ((0, -rows.shape[0] % 8), (0, 0))))
    packed = jnp.concatenate(parts, axis=0)
    assert packed.shape == (SMALL_ROWS, 128), packed.shape
    return packed


def _unpack_small(packed, like):
    out, at = {}, 0
    for k in SMALL:
        n = like[k].size // 128 if k in like else 1
        out[k] = packed[at:at + n].reshape(like[k].shape) if k in like else packed[at, 0]
        at += n + (-n % 8)
    return out


def _halves(g):
    return g.reshape(NSH, 2, g.shape[1] // 2, g.shape[2])


def kernel(x, ffn1_norm, ffn1_w_gate_up, ffn1_w_down, mix_norm, w_in, pool_w_group, pool_scale, w_branch_pool, w_branch_attn, w_out, ffn2_norm, ffn2_w_gate_up, ffn2_w_down, final_norm, loss_target, m_ffn1_norm, m_ffn1_w_gate_up, m_ffn1_w_down, m_mix_norm, m_w_in, m_pool_w_group, m_pool_scale, m_w_branch_pool, m_w_branch_attn, m_w_out, m_ffn2_norm, m_ffn2_w_gate_up, m_ffn2_w_down, m_final_norm, v_ffn1_norm, v_ffn1_w_gate_up, v_ffn1_w_down, v_mix_norm, v_w_in, v_pool_w_group, v_pool_scale, v_w_branch_pool, v_w_branch_attn, v_w_out, v_ffn2_norm, v_ffn2_w_gate_up, v_ffn2_w_down, v_final_norm):
    wts = dict(ffn1_norm=ffn1_norm, ffn1_w_gate_up=ffn1_w_gate_up, ffn1_w_down=ffn1_w_down, mix_norm=mix_norm,
               w_in=w_in, pool_w_group=pool_w_group, pool_scale=pool_scale, w_branch_pool=w_branch_pool,
               w_branch_attn=w_branch_attn, w_out=w_out, ffn2_norm=ffn2_norm, ffn2_w_gate_up=ffn2_w_gate_up,
               ffn2_w_down=ffn2_w_down, final_norm=final_norm)
    mom = dict(ffn1_norm=m_ffn1_norm, ffn1_w_gate_up=m_ffn1_w_gate_up, ffn1_w_down=m_ffn1_w_down,
               mix_norm=m_mix_norm, w_in=m_w_in, pool_w_group=m_pool_w_group, pool_scale=m_pool_scale,
               w_branch_pool=m_w_branch_pool, w_branch_attn=m_w_branch_attn, w_out=m_w_out,
               ffn2_norm=m_ffn2_norm, ffn2_w_gate_up=m_ffn2_w_gate_up, ffn2_w_down=m_ffn2_w_down,
               final_norm=m_final_norm)
    var = dict(ffn1_norm=v_ffn1_norm, ffn1_w_gate_up=v_ffn1_w_gate_up, ffn1_w_down=v_ffn1_w_down,
               mix_norm=v_mix_norm, w_in=v_w_in, pool_w_group=v_pool_w_group, pool_scale=v_pool_scale,
               w_branch_pool=v_w_branch_pool, w_branch_attn=v_w_branch_attn, w_out=v_w_out,
               ffn2_norm=v_ffn2_norm, ffn2_w_gate_up=v_ffn2_w_gate_up, ffn2_w_down=v_ffn2_w_down,
               final_norm=v_final_norm)

    c_idx = lax.axis_index("c").astype(jnp.int32).reshape(1)
    me_idx = (2 * lax.axis_index("x") + lax.axis_index("y")).astype(jnp.int32).reshape(1)
    place = jnp.concatenate([me_idx, c_idx])
    x0, tgt = x[0], loss_target[0]
    wgrp = pool_w_group[0].astype(BF16)
    g1, gm, g2, gf = ffn1_norm, mix_norm, ffn2_norm, final_norm.reshape(1, D)
    grad, delta, new_m, new_v = {}, {}, {}, {}

    def pair_sums(keys, parts, got):
        return [_pair_sum(parts[i], got[i], c_idx, "pair_sum_" + k) for i, k in enumerate(keys)]

    def chip_sums(keys, chip_parts, owned):
        return [_chip_sum(chip_parts[i], owned[i], place, "chip_sum_" + k) for i, k in enumerate(keys)]

    def adamw(k, after=()):
        outs = _adamw(wts[k][0], grad[k][0], mom[k][0], var[k][0], "adamw_" + k, after=after)
        grad[k], delta[k], new_m[k], new_v[k] = (o.reshape(wts[k].shape) for o in outs)

    own = {k: _cast_into_block(wts[k][0], me_idx, "cast_" + k) for k in BIG}
    first, late = ("ffn1_w_gate_up", "ffn1_w_down"), ("w_branch_pool", "w_branch_attn", "w_out",
                                                       "ffn2_w_gate_up", "ffn2_w_down")
    full = dict(zip(first, _exchange_alone(_ex_gather([own[k] for k in first]), "gather_ffn1")))
    wgu1, wd1 = full["ffn1_w_gate_up"], full["ffn1_w_down"].reshape(DFF, D)
    (h1, n1, gu1, a1), (win,) = _ffn_fwd(x0, g1, wgu1, wd1, "ffn1_fwd", exchange=_ex_gather_direct([own["w_in"]]))
    u, xp, q, k, v, gp, gs = _mix_in(h1, gm, win)
    (o_sb, ctot), landed = _attn_fwd(q, k, v, exchange=_ex_gather_direct([own[k_] for k_ in late]))
    full.update(zip(late, landed))
    wbp, wba, wout = full["w_branch_pool"], full["w_branch_attn"], full["w_out"].reshape(D, D)
    wgu2, wd2 = full["ffn2_w_gate_up"], full["ffn2_w_down"].reshape(DFF, D)
    h2, pm, p, yp, ys, mm = _mix_out(h1, xp, o_sb, gp, gs, wgrp, pool_scale, wbp, wba, wout)
    h3, n3, gu3, a3 = _ffn_fwd(h2, g2, wgu2, wd2, "ffn2_fwd")
    dh3, loss_row, d_gf = _head(h3, tgt, gf)

    def grad_gate_up(n, dgu, name, exchange=None):
        res = _wgrad(n, dgu, NSH, 512, name, exchange=exchange)
        return [_halves(res)] if exchange is None else ([_halves(res[0])], res[1])

    def grad_down(a, dh, name, exchange=None):
        res = _wgrad(a, dh, 1, FFS, name, exchange=exchange)
        halves = lambda g: [_halves(g.reshape(NSH, DFF // NSH, D))]
        return halves(res) if exchange is None else (halves(res[0]), res[1])

    k_gu2, k_d2, k_gu1, k_d1, k_in = (("ffn2_w_gate_up",), ("ffn2_w_down",), ("ffn1_w_gate_up",),
                                      ("ffn1_w_down",), ("w_in",))
    dgu3 = _ffn_bwd_act(dh3, gu3, wd2, "ffn2_bwd_act")
    pa = grad_gate_up(n3, dgu3, "wgrad_gu2") + grad_down(a3, dh3, "wgrad_d2")
    (dh2, d_g2), got_a = _ffn_bwd_in(dh3, h2, g2, dgu3, wgu2, "ffn2_bwd_in", exchange=_ex_pair_swap(pa))
    chip_a = pair_sums(k_gu2 + k_d2, pa, got_a)
    dlg, dyp, dys, do_sb, dyg, dxp, d_scale = _mix_bwd_out(dh2, gp, gs, yp, ys, pm, wgrp, pool_scale, wbp, wba, wout)
    kb = ("w_out", "w_branch_pool", "w_branch_attn")
    pb = [_halves(_wgrad(mm, dh2, 1, 512, "wgrad_out").reshape(NSH, D // NSH, D)),
          _halves(_wgrad(p, dyp, NSH, PW, "wgrad_bp")), _halves(_wgrad(o_sb, dys, NSH, SBW, "wgrad_ba"))]
    chip_b = pair_sums(kb, pb, _exchange_alone(_ex_pair_swap(pb), "pair_swap_mix"))
    k_ab = k_gu2 + k_d2 + kb
    (dq, dk, dv), owned_ab = _attn_bwd(q, k, v, do_sb, ctot, exchange=_ex_scatter(chip_a + chip_b))
    halves_ab = chip_sums(k_ab, chip_a + chip_b, owned_ab)
    (dh1, d_gm, dproj), both_ab = _mix_bwd_in(dh2, h1, gm, (dxp, dq, dk, dv, dlg), win, exchange=_ex_share(halves_ab))
    for i, k_ in enumerate(k_ab):
        grad[k_] = both_ab[i].reshape(wts[k_].shape)

    p_in = [_halves(_wgrad(u, dproj, NSH, 512, "wgrad_in"))]
    p_d1, got_in = grad_down(a1, dh1, "wgrad_d1", exchange=_ex_pair_swap(p_in))
    sems_in, thru_in, token_in = _scatter_start(pair_sums(k_in, p_in, got_in), "scatter_in_start")
    dgu1, got_d1 = _ffn_bwd_act(dh1, gu1, wd1, "ffn1_bwd_act", exchange=_ex_pair_swap(p_d1), after=(token_in,))
    sems_d1, thru_d1, token_d1 = _scatter_start(pair_sums(k_d1, p_d1, got_d1), "scatter_d1_start")
    p_gu1 = [_halves(_wgrad(n1, dgu1, NSH, 512, "wgrad_gu1", after=(token_in, token_d1)))]
    chip_in, owned_in = _scatter_wait(sems_in, thru_in, p_gu1, "scatter_in_wait")
    chip_d1, owned_d1 = _scatter_wait(sems_d1, thru_d1, p_gu1, "scatter_d1_wait")
    halves_in, halves_d1 = chip_sums(k_in, chip_in, owned_in), chip_sums(k_d1, chip_d1, owned_d1)
    landed = _exchange_alone(_join(_ex_pair_swap(p_gu1), _ex_share(halves_in)), "pair_swap_gu1")
    grad["w_in"] = landed[1].reshape(w_in.shape)
    sems, thru, token = _scatter_start(pair_sums(k_gu1, p_gu1, landed[:1]), "scatter_gu1_start")
    for k_ in k_ab + k_in:
        adamw(k_, after=(token,))
    dx, d_g1 = _ffn_bwd_in(dh1, x0, g1, dgu1, wgu1, "ffn1_bwd_in", after=(token,))
    chip_gu1, owned_gu1 = _scatter_wait(sems, thru, [dx] + [delta[k_] for k_ in k_ab + k_in], "scatter_gu1_wait")

    small_g = dict(ffn1_norm=d_g1, mix_norm=d_gm, ffn2_norm=d_g2, final_norm=d_gf, pool_scale=d_scale,
                   pool_w_group=_wgrad_groups(pm, dyg), loss=loss_row)
    gathered, both = _gather_small(_pack_small(small_g), _ex_share(halves_d1 + chip_sums(k_gu1, chip_gu1, owned_gu1)))
    grad["ffn1_w_down"] = both[0].reshape(ffn1_w_down.shape)
    grad["ffn1_w_gate_up"] = both[1].reshape(ffn1_w_gate_up.shape)
    for k_ in k_d1 + k_gu1:
        adamw(k_)
    sg, sd, sm, sv = _small_update(gathered, _pack_small(wts), _pack_small(mom), _pack_small(var))
    sums = _unpack_small(sg, wts)
    loss = sums.pop("loss")
    grad.update(sums)
    for dst, packed in ((delta, sd), (new_m, sm), (new_v, sv)):
        vals = _unpack_small(packed, wts)
        vals.pop("loss")
        dst.update(vals)
    return (loss, dx[None], *[grad[k_] for k_ in ORDER], *[delta[k_] for k_ in ORDER],
            *[new_m[k_] for k_ in ORDER], *[new_v[k_] for k_ in ORDER])
```

```python
import functools

import jax
import jax.numpy as jnp
from jax import lax
from jax.experimental import pallas as pl
from jax.experimental.pallas import tpu as pltpu

F32 = jnp.float32
BF16 = jnp.bfloat16

S = 2048
D = 1024
DFF = 2816
FFS = 2 * DFF // 4
NSH = 4
PW = 512
PG = 128
POOL_WINDOWS = (2, 4, 8, 16)
HALO = 16
SBW = 512
DH = 64
EPS = 1e-6
SCALE = 0.125
LOG2E = 1.4426950408889634
TA = 256
QB = 2
MIB = 1024 * 1024

LR, B1, B2, AEPS, WD, STEP = 0.001, 0.9, 0.999, 1e-08, 0.01, 10

_VM = pl.BlockSpec(memory_space=pltpu.VMEM)
_ANY = pl.BlockSpec(memory_space=pl.ANY)
MESH = pl.DeviceIdType.MESH


def _nn(a, b):
    return jnp.dot(a, b, preferred_element_type=F32)


def _nt(a, b):
    return lax.dot_general(a, b, (((1,), (1,)), ((), ())), preferred_element_type=F32)


def _tn(a, b):
    return lax.dot_general(a, b, (((0,), (0,)), ((), ())), preferred_element_type=F32)


def _params(sem, vmem_mib):
    return pltpu.CompilerParams(dimension_semantics=sem, vmem_limit_bytes=vmem_mib * MIB)


def _rows(tm, width):
    return pl.BlockSpec((tm, width), lambda i: (i, 0))


def _fixed(shape):
    return pl.BlockSpec(shape, lambda *_: (0,) * len(shape))


def _sds(shape, dtype):
    return pltpu.HBM(shape, dtype)


def _in_hbm(args):
    return [pltpu.with_memory_space_constraint(a, pltpu.HBM) for a in args]


def _stage(pairs):
    @pl.when(pl.program_id(0) == 0)
    def _():
        for src, dst in pairs:
            pltpu.sync_copy(src, dst)


class _Staging:
    def __init__(self, pieces, sem):
        self.first = pl.program_id(0) == 0
        self.copies = [pltpu.make_async_copy(src, dst, sem.at[k]) for k, (src, dst) in enumerate(pieces)]

        @pl.when(self.first)
        def _():
            for cp in self.copies:
                cp.start()

    def need(self, k):
        @pl.when(self.first)
        def _():
            self.copies[k].wait()


def _vmem_like(*arrays):
    return [pltpu.VMEM(a.shape, a.dtype) for a in arrays]


class Exchange:
    def __init__(self, arrays, landing, aliases, n_sems, start, finish):
        self.arrays, self.landing, self.aliases, self.n_sems = list(arrays), list(landing), dict(aliases), n_sems
        self.start, self.finish = start, finish


def _join(a, b):
    na, la = len(a.arrays), len(a.landing)

    def both(fa, fb):
        def run(ins, outs, ssem, rsem):
            fa(ins[:na], outs[:la], ssem.at[pl.ds(0, a.n_sems)], rsem.at[pl.ds(0, a.n_sems)])
            fb(ins[na:], outs[la:], ssem.at[pl.ds(a.n_sems, b.n_sems)], rsem.at[pl.ds(a.n_sems, b.n_sems)])
        return run

    aliases = {**a.aliases, **{na + i: la + j for i, j in b.aliases.items()}}
    return Exchange(a.arrays + b.arrays, a.landing + b.landing, aliases, a.n_sems + b.n_sems,
                    both(a.start, b.start), both(a.finish, b.finish))


def _call(body, args, *, name, grid, in_specs, out_specs, out_shape, scratch_shapes=(), compiler_params=None,
          exchange=None, free=(), after=()):
    args = [a if i in free else pltpu.with_memory_space_constraint(a, pltpu.HBM) for i, a in enumerate(args)]
    if exchange is None:
        n_in = len(in_specs)

        def plain(*refs):
            body(*refs[:n_in], *refs[n_in + len(after):])

        return pl.pallas_call(plain, name=name, grid=grid, in_specs=list(in_specs) + [_ANY] * len(after),
                              out_specs=out_specs, out_shape=out_shape, scratch_shapes=list(scratch_shapes),
                              compiler_params=compiler_params)(*args, *after)
    ex = exchange
    n_in, n_out, n_scr = len(in_specs), len(out_specs), len(scratch_shapes)
    na, nl = len(ex.arrays), len(ex.landing)

    def hosted(*refs):
        at = [0]

        def take(n):
            at[0] += n
            return refs[at[0] - n:at[0]]

        k_in, _, e_in, k_out, e_out, k_scr = take(n_in), take(len(after)), take(na), take(n_out), take(nl), take(n_scr)
        ssem, rsem = take(2)
        ids = [pl.program_id(a) for a in range(len(grid))]
        first = functools.reduce(jnp.logical_and, [i == 0 for i in ids])
        last = functools.reduce(jnp.logical_and, [i == g - 1 for i, g in zip(ids, grid)])

        @pl.when(first)
        def _():
            ex.start(e_in, e_out, ssem, rsem)

        body(*k_in, *k_out, *k_scr)

        @pl.when(last)
        def _():
            ex.finish(e_in, e_out, ssem, rsem)

    outs = pl.pallas_call(
        hosted, name=name, grid=grid,
        in_specs=list(in_specs) + [_ANY] * (len(after) + na), out_specs=list(out_specs) + [_ANY] * nl,
        out_shape=list(out_shape) + ex.landing,
        scratch_shapes=list(scratch_shapes) + [pltpu.SemaphoreType.DMA((ex.n_sems,))] * 2,
        input_output_aliases={n_in + len(after) + i: n_out + j for i, j in ex.aliases.items()},
        compiler_params=compiler_params,
    )(*args, *after, *_in_hbm(ex.arrays))
    return outs[:n_out], outs[n_out:]


def _exchange_alone(ex, name):
    def body(*refs):
        na, nl = len(ex.arrays), len(ex.landing)
        ex.start(refs[:na], refs[na:na + nl], refs[-2], refs[-1])
        ex.finish(refs[:na], refs[na:na + nl], refs[-2], refs[-1])

    return pl.pallas_call(
        body, name=name, in_specs=[_ANY] * len(ex.arrays), out_specs=[_ANY] * len(ex.landing),
        out_shape=ex.landing, scratch_shapes=[pltpu.SemaphoreType.DMA((ex.n_sems,))] * 2,
        input_output_aliases=ex.aliases,
    )(*_in_hbm(ex.arrays))


_HBM = pl.BlockSpec(memory_space=pltpu.HBM)
_SEM = pl.BlockSpec(memory_space=pltpu.SEMAPHORE)
_EFFECT = pltpu.SideEffectType.DATAFLOW_SIDE_EFFECTING


def _scatter_copies(srcs, lands, ssems, rsems):
    x, y, c, chips = _place()
    return [_remote(srcs[w].at[2 * px + py], lands[w].at[k], ssems[3 * w + k], rsems[3 * w + k], (px, py, c))
            for w in range(len(srcs)) for k, (px, py) in enumerate(chips)]


def _scatter_start(parts, name):
    n, ncp = len(parts), 3 * len(parts)
    lands = [lax.empty((3,) + p.shape[1:], p.dtype) for p in parts]

    def body(*refs):
        srcs, land_refs = refs[:n], refs[n:2 * n]
        ssems, rsems = refs[2 * n:2 * n + ncp], refs[2 * n + ncp:2 * n + 2 * ncp]
        for cp in _scatter_copies(srcs, land_refs, ssems, rsems):
            cp.start()
        token = refs[-1]
        token[...] = jnp.zeros_like(token)

    outs = pl.pallas_call(
        body, name=name,
        out_shape=([pltpu.SemaphoreType.DMA(())] * (2 * ncp) + [pltpu.HBM(a.shape, a.dtype) for a in parts + lands]
                   + [jax.ShapeDtypeStruct((8, 128), F32)]),
        in_specs=[_HBM] * (2 * n), out_specs=[_SEM] * (2 * ncp) + [_HBM] * (2 * n) + [_VM],
        input_output_aliases={i: 2 * ncp + i for i in range(2 * n)},
        compiler_params=pltpu.CompilerParams(has_side_effects=_EFFECT),
    )(*_in_hbm(parts), *_in_hbm(lands))
    sems, thru, token = outs[:2 * ncp], outs[2 * ncp:2 * ncp + 2 * n], outs[-1]
    return sems, thru, token


def _scatter_wait(sems, thru, after, name):
    n = len(thru) // 2
    ncp = 3 * n

    def body(*refs):
        srcs, land_refs = refs[:n], refs[n:2 * n]
        ssems, rsems = refs[2 * n:2 * n + ncp], refs[2 * n + ncp:2 * n + 2 * ncp]
        for cp in _scatter_copies(srcs, land_refs, ssems, rsems):
            cp.wait_send()
            cp.wait_recv()

    outs = pl.pallas_call(
        body, name=name, out_shape=[pltpu.HBM(a.shape, a.dtype) for a in thru],
        in_specs=[_HBM] * (2 * n) + [_SEM] * (2 * ncp) + [_ANY] * len(after), out_specs=[_HBM] * (2 * n),
        input_output_aliases={i: i for i in range(2 * n)},
        compiler_params=pltpu.CompilerParams(has_side_effects=_EFFECT),
    )(*thru, *sems, *after)
    return outs[:n], outs[n:]


def _gather_copies(bufs, ssems, rsems, sending):
    x, y, c, chips = _place()
    out = []
    for w, ref in enumerate(bufs):
        half = ref.shape[1] // 2
        for k, (px, py) in enumerate(chips):
            rows = ref.at[2 * x + y if sending else 2 * px + py, pl.ds(c * half, half)]
            out.append(_remote(rows, rows, ssems[3 * w + k], rsems[3 * w + k], (px, py, c)))
    return out


def _gather_start(bufs, after, name):
    n, ncp = len(bufs), 3 * len(bufs)

    def body(*refs):
        ssems, rsems = refs[n + len(after):n + len(after) + ncp], refs[n + len(after) + ncp:n + len(after) + 2 * ncp]
        for cp in _gather_copies(refs[:n], ssems, rsems, True):
            cp.start()
        token = refs[-1]
        token[...] = jnp.zeros_like(token)

    outs = pl.pallas_call(
        body, name=name,
        out_shape=([pltpu.SemaphoreType.DMA(())] * (2 * ncp) + [pltpu.HBM(a.shape, a.dtype) for a in bufs]
                   + [jax.ShapeDtypeStruct((8, 128), F32)]),
        in_specs=[_HBM] * n + [_ANY] * len(after), out_specs=[_SEM] * (2 * ncp) + [_HBM] * n + [_VM],
        input_output_aliases={i: 2 * ncp + i for i in range(n)},
        compiler_params=pltpu.CompilerParams(has_side_effects=_EFFECT),
    )(*_in_hbm(bufs), *after)
    return outs[:2 * ncp], outs[2 * ncp:2 * ncp + n], outs[-1]


def _gather_wait(sems, thru, after, name):
    n = len(thru)
    ncp = 3 * n

    def body(*refs):
        ssems, rsems = refs[n:n + ncp], refs[n + ncp:n + 2 * ncp]
        for cp in _gather_copies(refs[:n], ssems, rsems, True):
            cp.wait_send()
        for cp in _gather_copies(refs[:n], ssems, rsems, False):
            cp.wait_recv()

    return pl.pallas_call(
        body, name=name, out_shape=[pltpu.HBM(a.shape, a.dtype) for a in thru],
        in_specs=[_HBM] * n + [_SEM] * (2 * ncp) + [_ANY] * len(after), out_specs=[_HBM] * n,
        input_output_aliases={i: i for i in range(n)},
        compiler_params=pltpu.CompilerParams(has_side_effects=_EFFECT),
    )(*thru, *sems, *after)


def _rms(x):
    r = lax.rsqrt(jnp.mean(x * x, axis=-1, keepdims=True) + EPS)
    return r, x * r


def _rms_bwd(dn, xr, r, gain):
    dng = dn * gain
    dx = r * (dng - xr * jnp.mean(dng * xr, axis=-1, keepdims=True))
    return dx, jnp.sum(dn * xr, axis=0, keepdims=True)


def _ffn_fwd(x, gain, wgu, wd, name, exchange=None):
    tm = 256

    def body(x_ref, g_ref, wgu_hbm, wd_hbm, h_ref, n_ref, gu_ref, a_ref, wgu_ref, wd_ref, wsem):
        pieces = []
        for j in range(2):
            down = pl.ds(j * FFS, FFS)
            pieces += [(wgu_hbm.at[j], wgu_ref.at[j]), (wgu_hbm.at[2 + j], wgu_ref.at[2 + j]),
                       (wd_hbm.at[down], wd_ref.at[down])]
        staged = _Staging(pieces, wsem)
        x = x_ref[...]
        _, xr = _rms(x)
        n = (xr * g_ref[...]).astype(BF16)
        n_ref[...] = n
        acc = jnp.zeros((tm, D), F32)
        for j in range(2):
            staged.need(3 * j)
            g = _nn(n, wgu_ref[j])
            staged.need(3 * j + 1)
            u = _nn(n, wgu_ref[2 + j])
            gu_ref[:, j * FFS:(j + 1) * FFS] = g.astype(BF16)
            gu_ref[:, (2 + j) * FFS:(3 + j) * FFS] = u.astype(BF16)
            half_act = (0.5 * (g * jax.nn.sigmoid(g) * u)).astype(BF16)
            a_ref[:, j * FFS:(j + 1) * FFS] = half_act
            staged.need(3 * j + 2)
            acc = acc + _nn(half_act, wd_ref[j * FFS:(j + 1) * FFS, :])
        h_ref[...] = x + acc

    return _call(
        body, (x, gain, wgu, wd), name=name, grid=(S // tm,),
        in_specs=[_rows(tm, D), _fixed((1, D)), _ANY, _ANY],
        out_specs=[_rows(tm, D), _rows(tm, D), _rows(tm, 4 * FFS), _rows(tm, DFF)],
        out_shape=[_sds((S, D), F32), _sds((S, D), BF16), _sds((S, 4 * FFS), BF16), _sds((S, DFF), BF16)],
        scratch_shapes=_vmem_like(wgu, wd) + [pltpu.SemaphoreType.DMA((6,))],
        compiler_params=_params(("arbitrary",), 56), exchange=exchange)


def _ffn_bwd_act(dh, gu, wd, name, exchange=None, after=()):
    tm = 512

    def body(dh_ref, gu_ref, wd_hbm, dgu_ref, wd_ref, wsem):
        staged = _Staging([(wd_hbm.at[pl.ds(j * FFS, FFS)], wd_ref.at[pl.ds(j * FFS, FFS)]) for j in range(2)], wsem)
        dhb = dh_ref[...].astype(BF16)
        for j in range(2):
            g = gu_ref[:, j * FFS:(j + 1) * FFS].astype(F32)
            u = gu_ref[:, (2 + j) * FFS:(3 + j) * FFS].astype(F32)
            staged.need(j)
            da = 0.5 * _nt(dhb, wd_ref[j * FFS:(j + 1) * FFS, :])
            sg = jax.nn.sigmoid(g)
            dgu_ref[:, j * FFS:(j + 1) * FFS] = (da * u * (sg * (1.0 + g * (1.0 - sg)))).astype(BF16)
            dgu_ref[:, (2 + j) * FFS:(3 + j) * FFS] = (da * (g * sg)).astype(BF16)

    res = _call(
        body, (dh, gu, wd), name=name, grid=(S // tm,),
        in_specs=[_rows(tm, D), _rows(tm, 4 * FFS), _ANY], out_specs=[_rows(tm, 4 * FFS)],
        out_shape=[_sds((S, 4 * FFS), BF16)], scratch_shapes=_vmem_like(wd) + [pltpu.SemaphoreType.DMA((2,))],
        compiler_params=_params(("arbitrary",), 56), exchange=exchange, after=after)
    return res[0] if exchange is None else (res[0][0], res[1])


def _ffn_bwd_in(dh, x, gain, dgu, wgu, name, exchange=None, after=()):
    tm = 512

    def body(dh_ref, x_ref, g_ref, dgu_ref, wgu_hbm, dx_ref, dg_ref, wgu_ref, wsem):
        staged = _Staging([(wgu_hbm.at[j], wgu_ref.at[j]) for j in range(NSH)], wsem)
        dn = jnp.zeros((tm, D), F32)
        for j in range(NSH):
            staged.need(j)
            dn = dn + _nt(dgu_ref[:, j * FFS:(j + 1) * FFS], wgu_ref[j])
        r, xr = _rms(x_ref[...])
        dx, dgain = _rms_bwd(dn, xr, r, g_ref[...])
        dx_ref[...] = dh_ref[...] + dx

        @pl.when(pl.program_id(0) == 0)
        def _():
            dg_ref[...] = jnp.zeros_like(dg_ref)

        dg_ref[...] += dgain

    return _call(
        body, (dh, x, gain, dgu, wgu), name=name, grid=(S // tm,),
        in_specs=[_rows(tm, D), _rows(tm, D), _fixed((1, D)), _rows(tm, 4 * FFS), _ANY],
        out_specs=[_rows(tm, D), _fixed((1, D))],
        out_shape=[_sds((S, D), F32), _sds((1, D), F32)],
        scratch_shapes=_vmem_like(wgu) + [pltpu.SemaphoreType.DMA((NSH,))],
        compiler_params=_params(("arbitrary",), 56), exchange=exchange, after=after)


def _head(h, target, gain):
    tm = 512

    def body(h_ref, t_ref, g_ref, dh_ref, loss_ref, dg_ref):
        gain = g_ref[...]
        r, hr = _rms(h_ref[...])
        err = hr * gain - t_ref[...]
        dy = err * (1.0 / D)
        dh, dgain = _rms_bwd(dy, hr, r, gain)
        dh_ref[...] = dh

        @pl.when(pl.program_id(0) == 0)
        def _():
            dg_ref[...] = jnp.zeros_like(dg_ref)
            loss_ref[...] = jnp.zeros_like(loss_ref)

        dg_ref[...] += dgain
        loss_ref[...] += jnp.full((1, 128), (0.5 / D) * jnp.sum(err * err), F32)

    return pl.pallas_call(
        body, name="head", grid=(S // tm,),
        in_specs=[_rows(tm, D), _rows(tm, D), _fixed((1, D))],
        out_specs=[_rows(tm, D), _fixed((1, 128)), _fixed((1, D))],
        out_shape=[_sds((S, D), F32), _sds((1, 128), F32), _sds((1, D), F32)],
        compiler_params=_params(("arbitrary",), 40),
    )(*_in_hbm([h]), target, gain)


def _mix_in(h, gain, w_in, after=()):
    tm = 512

    def body(h_ref, g_ref, w_hbm, u_ref, xp_ref, q_ref, k_ref, v_ref, gp_ref, gs_ref, w_ref, wsem):
        staged = _Staging([(w_hbm.at[j], w_ref.at[j]) for j in range(NSH)], wsem)
        _, hr = _rms(h_ref[...])
        u = (hr * g_ref[...]).astype(BF16)
        u_ref[...] = u
        staged.need(0)
        p0 = _nn(u, w_ref[0])
        xp_ref[...] = p0[:, :PW]
        q_ref[...] = p0[:, PW:].astype(BF16)
        staged.need(1)
        p1 = _nn(u, w_ref[1])
        k_ref[...] = p1[:, :SBW].astype(BF16)
        v_ref[...] = p1[:, SBW:].astype(BF16)
        staged.need(2)
        gp_ref[...] = jax.nn.sigmoid(_nn(u, w_ref[2])).astype(BF16)
        staged.need(3)
        gs_ref[...] = jax.nn.sigmoid(_nn(u, w_ref[3])).astype(BF16)

    return _call(
        body, (h, gain, w_in), name="mix_in", grid=(S // tm,),
        in_specs=[_rows(tm, D), _fixed((1, D)), _ANY],
        out_specs=[_rows(tm, D), _rows(tm, PW), _rows(tm, SBW), _rows(tm, SBW), _rows(tm, SBW),
                   _rows(tm, D), _rows(tm, D)],
        out_shape=[_sds((S, D), BF16), _sds((S, PW), F32), _sds((S, SBW), BF16), _sds((S, SBW), BF16),
                   _sds((S, SBW), BF16), _sds((S, D), BF16), _sds((S, D), BF16)],
        scratch_shapes=_vmem_like(w_in) + [pltpu.SemaphoreType.DMA((NSH,))],
        compiler_params=_params(("arbitrary",), 48), free=(1,), after=after)


def _hilo_dot(x, tri):
    hi = x.astype(BF16)
    lo = (x - hi.astype(F32)).astype(BF16)
    return _nn(hi, tri) + _nn(lo, tri)


def _log_terms(qk):
    z2 = qk * (SCALE * LOG2E)
    lb = jnp.minimum(z2, 0.0) - jnp.log2(1.0 + jnp.exp2(-jnp.abs(z2)))
    return lb, lb - z2


def _head_masks():
    lane = lax.broadcasted_iota(jnp.int32, (1, 2 * DH), 1)
    return (lane < DH, lane >= DH)


def _attn_fwd(q, k, v, exchange=None):
    T = TA

    def body(q_ref, k_ref, v_ref, o_ref, c_ref):
        i2 = 2 * pl.program_id(1)
        row = lax.broadcasted_iota(jnp.int32, (T, T), 0)
        col = lax.broadcasted_iota(jnp.int32, (T, T), 1)
        after = (row > col).astype(BF16)
        causal = col < row
        masks = _head_masks()
        qms = {}
        for b in range(QB):
            q2 = q_ref[b * T:(b + 1) * T, :]
            for h, hm in enumerate(masks):
                qms[b, h] = jnp.where(hm, q2, jnp.zeros_like(q2))

        def blocks(keys, pairs, carries, os):
            ks, vms = [], []
            for j in keys:
                rows = pl.ds(pl.multiple_of(j * T, T), T)
                vj = v_ref[rows, :]
                ks.append(k_ref[rows, :])
                vms.append([jnp.where(hm, vj, jnp.zeros_like(vj)) for hm in masks])
            units = [(n, h) for n in range(len(pairs)) for h in range(2)]
            qks = {(n, h): _nt(qms[pairs[n][0], h], ks[pairs[n][1]]) for n, h in units}
            lbs, l1ms = {}, {}
            for u in units:
                lbs[u], l1m = _log_terms(qks[u])
                l1ms[u] = jnp.where(causal, l1m, 0.0) if pairs[u[0]][2] else l1m
            cins = {u: _hilo_dot(l1ms[u], after) for u in units}
            carries, os = dict(carries), list(os)
            for n, h in units:
                b, key, diag = pairs[n]
                a = jnp.exp2(lbs[n, h] + cins[n, h] + carries[b, h])
                if diag:
                    a = jnp.where(causal, a, 0.0)
                os[b] = os[b] + _nn(a.astype(BF16), vms[key][h])
                carries[b, h] = carries[b, h] + jnp.sum(l1ms[n, h], axis=1, keepdims=True)
            return carries, tuple(os)

        carries = {(b, h): jnp.zeros((T, 1), F32) for b in range(QB) for h in range(2)}
        os = tuple(jnp.zeros((T, 2 * DH), F32) for _ in range(QB))
        carries, os = blocks([i2 + 1, i2], [(1, 0, True), (0, 1, True), (1, 1, False)], carries, os)
        carries, os = lax.fori_loop(
            0, i2, lambda jj, c: blocks([i2 - 1 - jj], [(0, 0, False), (1, 0, False)], c[0], c[1]), (carries, os))
        for b in range(QB):
            o_ref[b * T:(b + 1) * T, :] = os[b].astype(BF16)
            c_ref[b * T:(b + 1) * T, :] = jnp.where(masks[0], carries[b, 0], carries[b, 1])

    blk = pl.BlockSpec((QB * T, 2 * DH), lambda p, i: (i, p))
    full = pl.BlockSpec((S, 2 * DH), lambda p, i: (0, p))
    return _call(
        body, (q, k, v), name="attn_fwd", grid=(SBW // (2 * DH), S // (QB * T)),
        in_specs=[blk, full, full], out_specs=[blk, blk],
        out_shape=[_sds((S, SBW), BF16), _sds((S, SBW), F32)],
        compiler_params=_params(("arbitrary", "arbitrary"), 40), exchange=exchange)


def _attn_bwd(q, k, v, do, ctot, exchange=None):
    T = TA
    nq = S // (QB * T)

    def body(q_ref, k_ref, v_ref, do_ref, c_ref, dq_ref, dk_ref, dv_ref, dk_acc, dv_acc):
        step = pl.program_id(1)
        i2 = 2 * step

        @pl.when(step == 0)
        def _():
            dk_acc[...] = jnp.zeros_like(dk_acc)
            dv_acc[...] = jnp.zeros_like(dv_acc)

        row = lax.broadcasted_iota(jnp.int32, (T, T), 0)
        col = lax.broadcasted_iota(jnp.int32, (T, T), 1)
        upto = (row <= col).astype(BF16)
        before = (row < col).astype(BF16)
        causal = col < row
        masks = _head_masks()
        qms, doms, ctots = {}, {}, {}
        for b in range(QB):
            q2, do2 = q_ref[b * T:(b + 1) * T, :], do_ref[b * T:(b + 1) * T, :]
            for h, hm in enumerate(masks):
                qms[b, h] = jnp.where(hm, q2, jnp.zeros_like(q2))
                doms[b, h] = jnp.where(hm, do2, jnp.zeros_like(do2))
                ctots[b, h] = c_ref[b * T:(b + 1) * T, h * DH:h * DH + 1]

        def blocks(keys, pairs, sums, dqs):
            rows = [pl.ds(pl.multiple_of(j * T, T), T) for j in keys]
            ks, vs = [k_ref[r, :] for r in rows], [v_ref[r, :] for r in rows]
            kms = [[jnp.where(hm, kj, jnp.zeros_like(kj)) for hm in masks] for kj in ks]
            units = [(n, h) for n in range(len(pairs)) for h in range(2)]
            qks = {(n, h): _nt(qms[pairs[n][0], h], ks[pairs[n][1]]) for n, h in units}
            das = {(n, h): _nt(doms[pairs[n][0], h], vs[pairs[n][1]]) for n, h in units}
            lbs, l1ms = {}, {}
            for u in units:
                lbs[u], l1m = _log_terms(qks[u])
                l1ms[u] = jnp.where(causal, l1m, 0.0) if pairs[u[0]][2] else l1m
            pins = {u: _hilo_dot(l1ms[u], upto) for u in units}
            sums = dict(sums)
            a_s, dls, cps = {}, {}, {}
            for n, h in units:
                b, _, diag = pairs[n]
                cl, cp = sums[b, h]
                a = jnp.exp2(lbs[n, h] + (ctots[b, h] - cl) - pins[n, h])
                if diag:
                    a = jnp.where(causal, a, 0.0)
                a_s[n, h] = a.astype(BF16)
                dls[n, h] = das[n, h] * a
                cps[n, h] = cp
                sums[b, h] = (cl + jnp.sum(l1ms[n, h], axis=1, keepdims=True),
                              cp + jnp.sum(dls[n, h], axis=1, keepdims=True))
            pexs = {u: _hilo_dot(dls[u], before) for u in units}
            dzbs = {}
            for u in units:
                dz = dls[u] - jnp.exp2(lbs[u]) * (dls[u] + pexs[u] + cps[u])
                if pairs[u[0]][2]:
                    dz = jnp.where(causal, dz, 0.0)
                dzbs[u] = dz.astype(BF16)
            dqs = list(dqs)
            for n, h in units:
                dqs[pairs[n][0]] = dqs[pairs[n][0]] + _nn(dzbs[n, h], kms[pairs[n][1]][h])
            for key, r in enumerate(rows):
                mine = [(n, h) for n, h in units if pairs[n][1] == key]
                dk_acc[r, :] += functools.reduce(jnp.add, [_tn(dzbs[u], qms[pairs[u[0]][0], u[1]]) for u in mine])
                dv_acc[r, :] += functools.reduce(jnp.add, [_tn(a_s[u], doms[pairs[u[0]][0], u[1]]) for u in mine])
            return sums, tuple(dqs)

        zero = jnp.zeros((T, 1), F32)
        sums = {(b, h): (zero, zero) for b in range(QB) for h in range(2)}
        dqs = tuple(jnp.zeros((T, 2 * DH), F32) for _ in range(QB))
        sums, dqs = lax.fori_loop(
            0, i2, lambda j, c: blocks([j], [(0, 0, False), (1, 0, False)], c[0], c[1]), (sums, dqs))
        _, dqs = blocks([i2, i2 + 1], [(0, 0, True), (1, 0, False), (1, 1, True)], sums, dqs)
        for b in range(QB):
            dq_ref[b * T:(b + 1) * T, :] = (dqs[b] * SCALE).astype(BF16)

        @pl.when(step == nq - 1)
        def _():
            dk_ref[...] = (dk_acc[...] * SCALE).astype(BF16)
            dv_ref[...] = dv_acc[...].astype(BF16)

    blk = pl.BlockSpec((QB * T, 2 * DH), lambda p, i: (i, p))
    full = pl.BlockSpec((S, 2 * DH), lambda p, i: (0, p))
    return _call(
        body, (q, k, v, do, ctot), name="attn_bwd", grid=(SBW // (2 * DH), nq),
        in_specs=[blk, full, full, blk, blk], out_specs=[blk, full, full],
        out_shape=[_sds((S, SBW), BF16), _sds((S, SBW), BF16), _sds((S, SBW), BF16)],
        scratch_shapes=[pltpu.VMEM((S, 2 * DH), F32), pltpu.VMEM((S, 2 * DH), F32)],
        compiler_params=_params(("arbitrary", "arbitrary"), 40), exchange=exchange)


def _pool_counts(first_row, tm):
    pos = first_row + lax.broadcasted_iota(jnp.int32, (tm, 1), 0)
    return [jnp.minimum(pos + 1, w).astype(F32) for w in POOL_WINDOWS]


def _mix_out(h, xp, o_sb, gp, gs, w_group, scale, w_bp, w_ba, w_out, exchange=None):
    tm = 512

    def body(h_ref, xp_ref, o_ref, gp_ref, gs_ref, wg_hbm, sc_ref, wbp_hbm, wba_hbm, wo_hbm,
             h2_ref, pm_ref, p_ref, yp_ref, ys_ref, m_ref, halo, wg_ref, wbp_ref, wba_ref, wo_ref):
        _stage([(wg_hbm, wg_ref), (wbp_hbm, wbp_ref), (wba_hbm, wba_ref), (wo_hbm, wo_ref)])
        i = pl.program_id(0)

        @pl.when(i == 0)
        def _():
            halo[...] = jnp.zeros_like(halo)

        xp = xp_ref[...]
        ext = jnp.concatenate([halo[...], xp], axis=0)
        halo[...] = xp[tm - HALO:, :]
        counts = _pool_counts(i * tm, tm)
        for gi in range(len(POOL_WINDOWS)):
            lanes = slice(gi * PG, (gi + 1) * PG)
            win = ext[:, lanes]
            for step in range(gi + 1):
                win = win + pltpu.roll(win, 1 << step, 0)
            pm = (win[HALO:, :] / counts[gi] - xp[:, lanes]).astype(BF16)
            pm_ref[:, lanes] = pm
            p_ref[:, lanes] = (_nn(pm, wg_ref[gi]) * sc_ref[:, lanes]).astype(BF16)
        pb = p_ref[...]
        ob = o_ref[...]
        for j in range(NSH):
            cols = slice(j * (D // NSH), (j + 1) * (D // NSH))
            yp = _nn(pb, wbp_ref[j])
            ys = _nn(ob, wba_ref[j])
            yp_ref[:, cols] = yp.astype(BF16)
            ys_ref[:, cols] = ys.astype(BF16)
            m_ref[:, cols] = (gp_ref[:, cols].astype(F32) * yp + gs_ref[:, cols].astype(F32) * ys).astype(BF16)
        h2_ref[...] = h_ref[...] + _nn(m_ref[...], wo_ref[...])

    return _call(
        body, (h, xp, o_sb, gp, gs, w_group, scale, w_bp, w_ba, w_out), name="mix_out", grid=(S // tm,),
        in_specs=[_rows(tm, D), _rows(tm, PW), _rows(tm, SBW), _rows(tm, D), _rows(tm, D),
                  _ANY, _fixed((1, PW)), _ANY, _ANY, _ANY],
        out_specs=[_rows(tm, D), _rows(tm, PW), _rows(tm, PW), _rows(tm, D), _rows(tm, D), _rows(tm, D)],
        out_shape=[_sds((S, D), F32), _sds((S, PW), BF16), _sds((S, PW), BF16), _sds((S, D), BF16),
                   _sds((S, D), BF16), _sds((S, D), BF16)],
        scratch_shapes=[pltpu.VMEM((HALO, PW), F32)] + _vmem_like(w_group, w_bp, w_ba, w_out),
        compiler_params=_params(("arbitrary",), 48), free=(5, 6), exchange=exchange)


def _mix_bwd_out(dh, gp, gs, yp, ys, pm, w_group, scale, w_bp, w_ba, w_out, exchange=None):
    tm = 512
    nt = S // tm

    def body(dh_ref, gp_ref, gs_ref, yp_ref, ys_ref, pm_ref, wg_hbm, sc_ref, wbp_hbm, wba_hbm, wo_hbm,
             dlg_ref, dyp_ref, dys_ref, do_ref, dyg_ref, dxp_ref, dsc_ref, halo, wg_ref, wbp_ref, wba_ref, wo_ref):
        _stage([(wg_hbm, wg_ref), (wbp_hbm, wbp_ref), (wba_hbm, wba_ref), (wo_hbm, wo_ref)])
        step = pl.program_id(0)

        @pl.when(step == 0)
        def _():
            halo[...] = jnp.zeros_like(halo)
            dsc_ref[...] = jnp.zeros_like(dsc_ref)

        dm = _nt(dh_ref[...].astype(BF16), wo_ref[...])
        gp = gp_ref[...].astype(F32)
        gs = gs_ref[...].astype(F32)
        yp = yp_ref[...].astype(F32)
        ys = ys_ref[...].astype(F32)
        dlg_ref[:, :D] = (dm * yp * gp * (1.0 - gp)).astype(BF16)
        dlg_ref[:, D:] = (dm * ys * gs * (1.0 - gs)).astype(BF16)
        dyp_ref[...] = (dm * gp).astype(BF16)
        dys_ref[...] = (dm * gs).astype(BF16)
        dp = jnp.zeros((tm, PW), F32)
        do = jnp.zeros((tm, SBW), F32)
        for j in range(NSH):
            cols = slice(j * (D // NSH), (j + 1) * (D // NSH))
            dp = dp + _nt(dyp_ref[:, cols], wbp_ref[j])
            do = do + _nt(dys_ref[:, cols], wba_ref[j])
        do_ref[...] = do.astype(BF16)
        counts = _pool_counts((nt - 1 - step) * tm, tm)
        dscale = []
        for gi in range(len(POOL_WINDOWS)):
            lanes = slice(gi * PG, (gi + 1) * PG)
            dpg = dp[:, lanes]
            dscale.append(jnp.sum(dpg * _nn(pm_ref[:, lanes], wg_ref[gi]), axis=0, keepdims=True))
            dyg = (dpg * sc_ref[:, lanes]).astype(BF16)
            dyg_ref[:, lanes] = dyg
            dpm = _nt(dyg, wg_ref[gi])
            per = dpm / counts[gi]
            win = jnp.concatenate([per, halo[:, lanes]], axis=0)
            halo[:, lanes] = per[:HALO, :]
            for s in range(gi + 1):
                win = win + pltpu.roll(win, tm + HALO - (1 << s), 0)
            dxp_ref[:, lanes] = (win[:tm, :] - dpm).astype(BF16)
        dsc_ref[...] += jnp.concatenate(dscale, axis=1)

    rev = lambda width: pl.BlockSpec((tm, width), lambda i: (nt - 1 - i, 0))
    return _call(
        body, (dh, gp, gs, yp, ys, pm, w_group, scale, w_bp, w_ba, w_out), name="mix_bwd_out", grid=(nt,),
        in_specs=[rev(D), rev(D), rev(D), rev(D), rev(D), rev(PW), _ANY, _fixed((1, PW)), _ANY, _ANY, _ANY],
        out_specs=[rev(2 * D), rev(D), rev(D), rev(SBW), rev(PW), rev(PW), _fixed((1, PW))],
        out_shape=[_sds((S, 2 * D), BF16), _sds((S, D), BF16), _sds((S, D), BF16), _sds((S, SBW), BF16),
                   _sds((S, PW), BF16), _sds((S, PW), BF16), _sds((1, PW), F32)],
        scratch_shapes=[pltpu.VMEM((HALO, PW), F32)] + _vmem_like(w_group, w_bp, w_ba, w_out),
        compiler_params=_params(("arbitrary",), 48), exchange=exchange)


def _mix_bwd_in(dh, h, gain, pieces, w_in, exchange=None):
    tm = 512
    widths = [p.shape[1] for p in pieces]

    def body(dh_ref, h_ref, g_ref, *rest):
        piece_refs, (w_hbm, dx_ref, dg_ref, dp_ref, w_ref, wsem) = rest[:len(pieces)], rest[len(pieces):]
        staged = _Staging([(w_hbm.at[j], w_ref.at[j]) for j in range(NSH)], wsem)
        at = 0
        for ref, width in zip(piece_refs, widths):
            dp_ref[:, at:at + width] = ref[...]
            at += width
        du = jnp.zeros((tm, D), F32)
        for j in range(NSH):
            staged.need(j)
            du = du + _nt(dp_ref[:, j * D:(j + 1) * D], w_ref[j])
        r, hr = _rms(h_ref[...])
        dx, dgain = _rms_bwd(du, hr, r, g_ref[...])
        dx_ref[...] = dh_ref[...] + dx

        @pl.when(pl.program_id(0) == 0)
        def _():
            dg_ref[...] = jnp.zeros_like(dg_ref)

        dg_ref[...] += dgain

    return _call(
        body, (dh, h, gain, *pieces, w_in), name="mix_bwd_in", grid=(S // tm,),
        in_specs=[_rows(tm, D), _rows(tm, D), _fixed((1, D))] + [_rows(tm, w) for w in widths] + [_ANY],
        out_specs=[_rows(tm, D), _fixed((1, D)), _rows(tm, 4 * D)],
        out_shape=[_sds((S, D), F32), _sds((1, D), F32), _sds((S, 4 * D), BF16)],
        scratch_shapes=_vmem_like(w_in) + [pltpu.SemaphoreType.DMA((NSH,))],
        compiler_params=_params(("arbitrary",), 48), exchange=exchange)


def _wgrad(a, b, nblk, ti, name, out_dtype=BF16, exchange=None, after=()):
    ka, n = a.shape[1], b.shape[1]
    ns = n // nblk

    def body(a_ref, b_ref, o_ref):
        o_ref[...] = _tn(a_ref[...].astype(BF16), b_ref[...].astype(BF16)).astype(out_dtype)

    res = _call(
        body, (a, b), name=name, grid=(nblk, ka // ti),
        in_specs=[pl.BlockSpec((S, ti), lambda j, i: (0, i)), pl.BlockSpec((S, ns), lambda j, i: (0, j))],
        out_specs=[pl.BlockSpec((None, ti, ns), lambda j, i: (j, i, 0))],
        out_shape=[_sds((nblk, ka, ns), out_dtype)],
        compiler_params=_params(("arbitrary", "arbitrary"), 56), exchange=exchange, after=after)
    return res[0] if exchange is None else (res[0][0], res[1])


def _wgrad_groups(pm, dyg):
    def body(a_ref, b_ref, o_ref):
        o_ref[...] = _tn(a_ref[...], b_ref[...])

    col = pl.BlockSpec((S, PG), lambda g: (0, g))
    return pl.pallas_call(
        body, name="wgrad_groups", grid=(PW // PG,),
        in_specs=[col, col], out_specs=pl.BlockSpec((None, PG, PG), lambda g: (g, 0, 0)),
        out_shape=_sds((PW // PG, PG, PG), F32),
        compiler_params=_params(("arbitrary",), 32),
    )(*_in_hbm([pm, dyg]))


def _place():
    x, y, c = lax.axis_index("x"), lax.axis_index("y"), lax.axis_index("c")
    chips = [(1 - x, y), (x, 1 - y), (1 - x, 1 - y)]
    return x, y, c, chips


def _remote(src, dst, ssem, rsem, dev):
    return pltpu.make_async_remote_copy(src_ref=src, dst_ref=dst, send_sem=ssem, recv_sem=rsem,
                                        device_id=dev, device_id_type=MESH)


def _cast_into_block(w, me_idx, name):
    rows, cols = w.shape
    tr = _row_block(rows)

    def body(me_ref, w_ref, o_ref):
        o_ref[...] = w_ref[...].astype(BF16)

    return pl.pallas_call(
        body, name=name, out_shape=_sds((NSH, rows, cols), BF16),
        grid_spec=pltpu.PrefetchScalarGridSpec(
            num_scalar_prefetch=1, grid=(rows // tr,),
            in_specs=[pl.BlockSpec((tr, cols), lambda r, me: (r, 0))],
            out_specs=pl.BlockSpec((None, tr, cols), lambda r, me: (me[0], r, 0))),
        compiler_params=_params(("arbitrary",), 32),
    )(me_idx, w)


def _ex_gather(bufs):
    n = len(bufs)
    per = 8

    def plan(outs, ssem, rsem, w):
        x, y, c, _ = _place()
        sib, nbr_x, nbr_y = (x, y, 1 - c), (1 - x, y, c), (x, 1 - y, c)
        half = outs[w].shape[1] // 2
        quarter = half // 2
        sem = lambda k: (ssem.at[per * w + k], rsem.at[per * w + k])
        rows = lambda blk, start, size: outs[w].at[blk, pl.ds(start, size)]
        mine = rows(2 * x + y, c * half, half)
        from_x = rows(2 * (1 - x) + y, c * half, half)
        from_y = rows(2 * x + (1 - y), c * half, half)
        diag = 2 * (1 - x) + (1 - y)
        pass_y = rows(2 * (1 - x) + y, c * half, quarter)
        pass_x = rows(2 * x + (1 - y), c * half + quarter, quarter)
        diag_0, diag_1 = rows(diag, c * half, quarter), rows(diag, c * half + quarter, quarter)
        first = [_remote(mine, mine, *sem(0), nbr_x), _remote(mine, mine, *sem(1), nbr_y)]
        arrivals = [
            (_remote(from_x, from_x, *sem(0), nbr_x),
             [_remote(pass_y, pass_y, *sem(2), nbr_y), _remote(from_x, from_x, *sem(4), sib)]),
            (_remote(from_y, from_y, *sem(1), nbr_y),
             [_remote(pass_x, pass_x, *sem(3), nbr_x), _remote(from_y, from_y, *sem(5), sib)]),
            (_remote(diag_0, diag_0, *sem(2), nbr_y), [_remote(diag_0, diag_0, *sem(6), sib)]),
            (_remote(diag_1, diag_1, *sem(3), nbr_x), [_remote(diag_1, diag_1, *sem(7), sib)]),
        ]
        other = (1 - c) * half
        from_sibling = [
            _remote(rows(2 * (1 - x) + y, other, half), rows(2 * (1 - x) + y, other, half), *sem(4), sib),
            _remote(rows(2 * x + (1 - y), other, half), rows(2 * x + (1 - y), other, half), *sem(5), sib),
            _remote(rows(diag, other, quarter), rows(diag, other, quarter), *sem(6), sib),
            _remote(rows(diag, other + quarter, quarter), rows(diag, other + quarter, quarter), *sem(7), sib),
        ]
        return first, arrivals, from_sibling

    def start(ins, outs, ssem, rsem):
        x, y, c, _ = _place()
        for w in range(n):
            half = outs[w].shape[1] // 2
            mine = outs[w].at[2 * x + y, pl.ds(c * half, half)]
            _remote(mine, mine, ssem.at[per * w], rsem.at[per * w], (1 - x, y, c)).start()
            _remote(mine, mine, ssem.at[per * w + 1], rsem.at[per * w + 1], (x, 1 - y, c)).start()

    def finish(ins, outs, ssem, rsem):
        plans = [plan(outs, ssem, rsem, w) for w in range(n)]
        started = []
        for direct in (True, False):
            for first, arrivals, _ in plans:
                for arrived, onward in (arrivals[:2] if direct else arrivals[2:]):
                    arrived.wait_recv()
                    for cp in onward:
                        cp.start()
                    started += onward
        for first, _, from_sibling in plans:
            for cp in from_sibling:
                cp.wait_recv()
            started += first
        for cp in started:
            cp.wait_send()

    return Exchange(bufs, [_sds(b.shape, b.dtype) for b in bufs], {w: w for w in range(n)}, per * n, start, finish)


def _ex_gather_direct(bufs):
    n = len(bufs)

    def copies(outs, ssem, rsem, only_first=False):
        x, y, c, chips = _place()
        me, sib = 2 * x + y, (x, y, 1 - c)
        first, relay, last = [], [], []
        for w in range(n):
            half = outs[w].shape[1] // 2
            mine = outs[w].at[me, pl.ds(c * half, half)]
            for k, (px, py) in enumerate(chips):
                sems = (ssem.at[6 * w + k], rsem.at[6 * w + k])
                sib_sems = (ssem.at[6 * w + 3 + k], rsem.at[6 * w + 3 + k])
                first.append(_remote(mine, mine, *sems, (px, py, c)))
                if only_first:
                    continue
                got = outs[w].at[2 * px + py, pl.ds(c * half, half)]
                relay.append((_remote(got, got, *sems, (px, py, c)), _remote(got, got, *sib_sems, sib)))
                theirs = outs[w].at[2 * px + py, pl.ds((1 - c) * half, half)]
                last.append(_remote(theirs, theirs, *sib_sems, sib))
        return first, relay, last

    def start(ins, outs, ssem, rsem):
        for cp in copies(outs, ssem, rsem, only_first=True)[0]:
            cp.start()

    def finish(ins, outs, ssem, rsem):
        first, relay, last = copies(outs, ssem, rsem)
        for arrived, onward in relay:
            arrived.wait_recv()
            onward.start()
        for cp in last:
            cp.wait_recv()
        for cp in first:
            cp.wait_send()
        for _, onward in relay:
            onward.wait_send()

    return Exchange(bufs, [_sds(b.shape, b.dtype) for b in bufs], {w: w for w in range(n)}, 6 * n, start, finish)


def _simple_exchange(arrays, landing, aliases, make_copies):
    def start(ins, outs, ssem, rsem):
        for cp, _ in make_copies(ins, outs, ssem, rsem, False):
            cp.start()

    def finish(ins, outs, ssem, rsem):
        cps = make_copies(ins, outs, ssem, rsem, True)
        for _, landed in cps:
            landed.wait_recv()
        for cp, _ in cps:
            cp.wait_send()

    return Exchange(arrays, landing, aliases, len(arrays) * 3, start, finish)


def _ex_pair_swap(grads):
    def make(ins, outs, ssem, rsem, landing):
        x, y, c, _ = _place()
        cps = [_remote(ins[w].at[:, 1 - c], outs[w], ssem.at[w], rsem.at[w], (x, y, 1 - c))
               for w in range(len(grads))]
        return [(cp, cp) for cp in cps]

    return _simple_exchange(grads, [_sds((NSH,) + g.shape[2:], g.dtype) for g in grads], {}, make)


def _ex_scatter(parts):
    def make(ins, outs, ssem, rsem, landing):
        x, y, c, chips = _place()
        out = []
        for w in range(len(parts)):
            for k, (px, py) in enumerate(chips):
                sems = (ssem.at[3 * w + k], rsem.at[3 * w + k])
                out.append((_remote(ins[w].at[2 * px + py], outs[w].at[k], *sems, (px, py, c)),
                            _remote(outs[w].at[k], outs[w].at[k], *sems, (px, py, c)) if landing else None))
        return out

    return _simple_exchange(parts, [_sds((3,) + p.shape[1:], p.dtype) for p in parts], {}, make)


def _ex_relay(bufs):
    def make(ins, outs, ssem, rsem, landing):
        x, y, c, chips = _place()
        sib = (x, y, 1 - c)
        out = []
        for w in range(len(bufs)):
            half = outs[w].shape[1] // 2
            for k, (px, py) in enumerate(chips):
                sems = (ssem.at[3 * w + k], rsem.at[3 * w + k])
                have = outs[w].at[2 * px + py, pl.ds(c * half, half)]
                miss = outs[w].at[2 * px + py, pl.ds((1 - c) * half, half)]
                out.append((_remote(have, have, *sems, sib), _remote(miss, miss, *sems, sib) if landing else None))
        return out

    return _simple_exchange(bufs, [_sds(b.shape, b.dtype) for b in bufs], {w: w for w in range(len(bufs))}, make)


def _ex_share(bufs):
    def make(ins, outs, ssem, rsem, landing):
        x, y, c, _ = _place()
        sib = (x, y, 1 - c)
        return [(_remote(outs[w].at[c], outs[w].at[c], ssem.at[w], rsem.at[w], sib),
                 _remote(outs[w].at[1 - c], outs[w].at[1 - c], ssem.at[w], rsem.at[w], sib) if landing else None)
                for w in range(len(bufs))]

    return _simple_exchange(bufs, [_sds(b.shape, b.dtype) for b in bufs], {w: w for w in range(len(bufs))}, make)


def _gather_small(block, ex):
    m_per, n = block.shape
    na, nl = len(ex.arrays), len(ex.landing)

    def body(x_ref, *refs):
        e_in, out_ref, e_out = refs[:na], refs[na], refs[na + 1:na + 1 + nl]
        ssem, rsem, lsem, e_ssem, e_rsem = refs[na + 1 + nl:]
        ex.start(e_in, e_out, e_ssem, e_rsem)
        x, y, c, chips = _place()
        me, sib = (x, y, c), (x, y, 1 - c)

        def rows(px, py, pc):
            return out_ref.at[pl.ds((4 * px + 2 * py + pc) * m_per, m_per), :]

        def copy(k, blk, to, src=None):
            return _remote(rows(*blk) if src is None else src, rows(*blk), ssem.at[k], rsem.at[k], to)

        mine = pltpu.make_async_copy(x_ref, rows(*me), lsem)
        mine.start()
        first = [copy(0, me, sib, src=x_ref)]
        first += [copy(1 + j, me, (*chip, c), src=x_ref) for j, chip in enumerate(chips)]
        for cp in first:
            cp.start()
        passed = [copy(4 + j, (*chip, c), sib) for j, chip in enumerate(chips)]
        for j, chip in enumerate(chips):
            copy(1 + j, (*chip, c), me).wait_recv()
            passed[j].start()
        copy(0, sib, me).wait_recv()
        for j, chip in enumerate(chips):
            copy(4 + j, (*chip, 1 - c), me).wait_recv()
        for cp in first + passed:
            cp.wait_send()
        mine.wait()
        ex.finish(e_in, e_out, e_ssem, e_rsem)

    outs = pl.pallas_call(
        body, name="gather_small", out_shape=[jax.ShapeDtypeStruct((8 * m_per, n), block.dtype)] + ex.landing,
        in_specs=[_VM] + [_ANY] * na, out_specs=[_VM] + [_ANY] * nl,
        scratch_shapes=[pltpu.SemaphoreType.DMA((7,)), pltpu.SemaphoreType.DMA((7,)), pltpu.SemaphoreType.DMA]
        + [pltpu.SemaphoreType.DMA((ex.n_sems,))] * 2,
        input_output_aliases={1 + i: 1 + j for i, j in ex.aliases.items()},
    )(block, *_in_hbm(ex.arrays))
    return outs[0], outs[1:]


def _row_block(rows):
    return max(t for t in range(16, 257, 16) if rows % t == 0)


def _pair_sum(grad, got, c_idx, name):
    _, _, half, cols = grad.shape
    tr = _row_block(half)

    def body(c_ref, a_ref, b_ref, o_ref):
        o_ref[...] = (a_ref[...].astype(F32) + b_ref[...].astype(F32)).astype(BF16)

    return pl.pallas_call(
        body, name=name, out_shape=_sds((NSH, half, cols), BF16),
        grid_spec=pltpu.PrefetchScalarGridSpec(
            num_scalar_prefetch=1, grid=(NSH, half // tr),
            in_specs=[pl.BlockSpec((None, None, tr, cols), lambda j, r, c: (j, c[0], r, 0)),
                      pl.BlockSpec((None, tr, cols), lambda j, r, c: (j, r, 0))],
            out_specs=pl.BlockSpec((None, tr, cols), lambda j, r, c: (j, r, 0))),
        compiler_params=_params(("arbitrary", "arbitrary"), 32),
    )(c_idx, *_in_hbm([grad, got]))


def _chip_sum(own, got, place, name):
    _, half, cols = own.shape
    tr = _row_block(half)

    def body(place_ref, own_ref, got_ref, o_ref):
        acc = own_ref[...].astype(F32)
        for k in range(3):
            acc = acc + got_ref[k].astype(F32)
        o_ref[...] = acc

    return pl.pallas_call(
        body, name=name, out_shape=_sds((2, half, cols), F32),
        grid_spec=pltpu.PrefetchScalarGridSpec(
            num_scalar_prefetch=1, grid=(half // tr,),
            in_specs=[pl.BlockSpec((None, tr, cols), lambda r, p: (p[0], r, 0)),
                      pl.BlockSpec((3, tr, cols), lambda r, p: (0, r, 0))],
            out_specs=pl.BlockSpec((None, tr, cols), lambda r, p: (p[1], r, 0))),
        compiler_params=_params(("arbitrary",), 32),
    )(place, *_in_hbm([own, got]))


def _adamw_math(w, g, m, v):
    m = B1 * m + (1.0 - B1) * g
    v = B2 * v + (1.0 - B2) * (g * g)
    m_hat = m / (1.0 - B1 ** STEP)
    v_hat = v / (1.0 - B2 ** STEP)
    return -LR * (m_hat / (jnp.sqrt(v_hat) + AEPS) + WD * w), m, v


def _adamw(w, g, m, v, name, after=()):
    rows, cols = w.shape
    tr = _row_block(rows)

    def body(w_ref, g_ref, m_ref, v_ref, go_ref, d_ref, nm_ref, nv_ref):
        g = g_ref[...]
        go_ref[...] = g
        d_ref[...], nm_ref[...], nv_ref[...] = _adamw_math(w_ref[...], g, m_ref[...], v_ref[...])

    blk = pl.BlockSpec((tr, cols), lambda r: (r, 0))
    return _call(
        body, (w, g, m, v), name=name, grid=(rows // tr,), out_shape=[_sds(w.shape, F32)] * 4,
        in_specs=[blk] * 4, out_specs=[blk] * 4,
        compiler_params=_params(("arbitrary",), 32), free=(0, 2, 3), after=after)


def _small_update(gathered, w, m, v):
    rows = w.shape[0]

    def body(ga_ref, w_ref, m_ref, v_ref, g_ref, d_ref, nm_ref, nv_ref):
        g = ga_ref[0:rows, :]
        for dev in range(1, 8):
            g = g + ga_ref[dev * rows:(dev + 1) * rows, :]
        g_ref[...] = g
        d_ref[...], nm_ref[...], nv_ref[...] = _adamw_math(w_ref[...], g, m_ref[...], v_ref[...])

    return pl.pallas_call(
        body, name="small_update", out_shape=[jax.ShapeDtypeStruct(w.shape, F32)] * 4,
        in_specs=[_VM] * 4, out_specs=[_VM] * 4,
    )(gathered, w, m, v)


SMALL = ("ffn1_norm", "mix_norm", "ffn2_norm", "final_norm", "pool_scale", "pool_w_group", "loss")
BIG = ("ffn1_w_gate_up", "ffn1_w_down", "w_in", "w_branch_pool", "w_branch_attn", "w_out",
       "ffn2_w_gate_up", "ffn2_w_down")
ORDER = ("ffn1_norm", "ffn1_w_gate_up", "ffn1_w_down", "mix_norm", "w_in", "pool_w_group", "pool_scale",
         "w_branch_pool", "w_branch_attn", "w_out", "ffn2_norm", "ffn2_w_gate_up", "ffn2_w_down", "final_norm")
SMALL_ROWS = 560


def _pack_small(t):
    parts = []
    for k in SMALL:
        rows = t[k].reshape(-1, 128) if k in t else jnp.zeros((1, 128), F32)
        parts.append(jnp.pad(rows, ((0, -rows.shape[0] % 8), (0, 0))))
    packed = jnp.concatenate(parts, axis=0)
    assert packed.shape == (SMALL_ROWS, 128), packed.shape
    return packed


def _unpack_small(packed, like):
    out, at = {}, 0
    for k in SMALL:
        n = like[k].size // 128 if k in like else 1
        out[k] = packed[at:at + n].reshape(like[k].shape) if k in like else packed[at, 0]
        at += n + (-n % 8)
    return out


def _halves(g):
    return g.reshape(NSH, 2, g.shape[1] // 2, g.shape[2])


def kernel(x, ffn1_norm, ffn1_w_gate_up, ffn1_w_down, mix_norm, w_in, pool_w_group, pool_scale, w_branch_pool, w_branch_attn, w_out, ffn2_norm, ffn2_w_gate_up, ffn2_w_down, final_norm, loss_target, m_ffn1_norm, m_ffn1_w_gate_up, m_ffn1_w_down, m_mix_norm, m_w_in, m_pool_w_group, m_pool_scale, m_w_branch_pool, m_w_branch_attn, m_w_out, m_ffn2_norm, m_ffn2_w_gate_up, m_ffn2_w_down, m_final_norm, v_ffn1_norm, v_ffn1_w_gate_up, v_ffn1_w_down, v_mix_norm, v_w_in, v_pool_w_group, v_pool_scale, v_w_branch_pool, v_w_branch_attn, v_w_out, v_ffn2_norm, v_ffn2_w_gate_up, v_ffn2_w_down, v_final_norm):
    wts = dict(ffn1_norm=ffn1_norm, ffn1_w_gate_up=ffn1_w_gate_up, ffn1_w_down=ffn1_w_down, mix_norm=mix_norm,
               w_in=w_in, pool_w_group=pool_w_group, pool_scale=pool_scale, w_branch_pool=w_branch_pool,
               w_branch_attn=w_branch_attn, w_out=w_out, ffn2_norm=ffn2_norm, ffn2_w_gate_up=ffn2_w_gate_up,
               ffn2_w_down=ffn2_w_down, final_norm=final_norm)
    mom = dict(ffn1_norm=m_ffn1_norm, ffn1_w_gate_up=m_ffn1_w_gate_up, ffn1_w_down=m_ffn1_w_down,
               mix_norm=m_mix_norm, w_in=m_w_in, pool_w_group=m_pool_w_group, pool_scale=m_pool_scale,
               w_branch_pool=m_w_branch_pool, w_branch_attn=m_w_branch_attn, w_out=m_w_out,
               ffn2_norm=m_ffn2_norm, ffn2_w_gate_up=m_ffn2_w_gate_up, ffn2_w_down=m_ffn2_w_down,
               final_norm=m_final_norm)
    var = dict(ffn1_norm=v_ffn1_norm, ffn1_w_gate_up=v_ffn1_w_gate_up, ffn1_w_down=v_ffn1_w_down,
               mix_norm=v_mix_norm, w_in=v_w_in, pool_w_group=v_pool_w_group, pool_scale=v_pool_scale,
               w_branch_pool=v_w_branch_pool, w_branch_attn=v_w_branch_attn, w_out=v_w_out,
               ffn2_norm=v_ffn2_norm, ffn2_w_gate_up=v_ffn2_w_gate_up, ffn2_w_down=v_ffn2_w_down,
               final_norm=v_final_norm)

    c_idx = lax.axis_index("c").astype(jnp.int32).reshape(1)
    me_idx = (2 * lax.axis_index("x") + lax.axis_index("y")).astype(jnp.int32).reshape(1)
    place = jnp.concatenate([me_idx, c_idx])
    x0, tgt = x[0], loss_target[0]
    wgrp = pool_w_group[0].astype(BF16)
    g1, gm, g2, gf = ffn1_norm, mix_norm, ffn2_norm, final_norm.reshape(1, D)
    grad, delta, new_m, new_v = {}, {}, {}, {}

    def pair_sums(keys, parts, got):
        return [_pair_sum(parts[i], got[i], c_idx, "pair_sum_" + k) for i, k in enumerate(keys)]

    def chip_sums(keys, chip_parts, owned):
        return [_chip_sum(chip_parts[i], owned[i], place, "chip_sum_" + k) for i, k in enumerate(keys)]

    def adamw(k, after=()):
        outs = _adamw(wts[k][0], grad[k][0], mom[k][0], var[k][0], "adamw_" + k, after=after)
        grad[k], delta[k], new_m[k], new_v[k] = (o.reshape(wts[k].shape) for o in outs)

    own = {k: _cast_into_block(wts[k][0], me_idx, "cast_" + k) for k in BIG}
    first, late = ("ffn1_w_gate_up", "ffn1_w_down"), ("w_branch_pool", "w_branch_attn", "w_out",
                                                       "ffn2_w_gate_up", "ffn2_w_down")
    full = dict(zip(first, _exchange_alone(_ex_gather([own[k] for k in first]), "gather_ffn1")))
    wgu1, wd1 = full["ffn1_w_gate_up"], full["ffn1_w_down"].reshape(DFF, D)
    (h1, n1, gu1, a1), (win,) = _ffn_fwd(x0, g1, wgu1, wd1, "ffn1_fwd", exchange=_ex_gather_direct([own["w_in"]]))
    sems_l, thru_l, token_l = _gather_start([own[k_] for k_ in late], [h1], "gather_late_start")
    u, xp, q, k, v, gp, gs = _mix_in(h1, gm, win, after=(token_l,))
    o_sb, ctot = _attn_fwd(q, k, v)
    arrived = _gather_wait(sems_l, thru_l, [o_sb], "gather_late_wait")
    wbp, wba, wout = _exchange_alone(_ex_relay(arrived[:3]), "relay_mix")
    wout = wout.reshape(D, D)
    (h2, pm, p, yp, ys, mm), (wgu2, wd2) = _mix_out(h1, xp, o_sb, gp, gs, wgrp, pool_scale, wbp, wba, wout,
                                                    exchange=_ex_relay(arrived[3:]))
    wd2 = wd2.reshape(DFF, D)
    h3, n3, gu3, a3 = _ffn_fwd(h2, g2, wgu2, wd2, "ffn2_fwd")
    dh3, loss_row, d_gf = _head(h3, tgt, gf)

    def grad_gate_up(n, dgu, name, exchange=None):
        res = _wgrad(n, dgu, NSH, 512, name, exchange=exchange)
        return [_halves(res)] if exchange is None else ([_halves(res[0])], res[1])

    def grad_down(a, dh, name, exchange=None):
        res = _wgrad(a, dh, 1, FFS, name, exchange=exchange)
        halves = lambda g: [_halves(g.reshape(NSH, DFF // NSH, D))]
        return halves(res) if exchange is None else (halves(res[0]), res[1])

    k_gu2, k_d2, k_gu1, k_d1, k_in = (("ffn2_w_gate_up",), ("ffn2_w_down",), ("ffn1_w_gate_up",),
                                      ("ffn1_w_down",), ("w_in",))
    dgu3 = _ffn_bwd_act(dh3, gu3, wd2, "ffn2_bwd_act")
    pa = grad_gate_up(n3, dgu3, "wgrad_gu2") + grad_down(a3, dh3, "wgrad_d2")
    (dh2, d_g2), got_a = _ffn_bwd_in(dh3, h2, g2, dgu3, wgu2, "ffn2_bwd_in", exchange=_ex_pair_swap(pa))
    chip_a = pair_sums(k_gu2 + k_d2, pa, got_a)
    dlg, dyp, dys, do_sb, dyg, dxp, d_scale = _mix_bwd_out(dh2, gp, gs, yp, ys, pm, wgrp, pool_scale, wbp, wba, wout)
    kb = ("w_out", "w_branch_pool", "w_branch_attn")
    pb = [_halves(_wgrad(mm, dh2, 1, 512, "wgrad_out").reshape(NSH, D // NSH, D)),
          _halves(_wgrad(p, dyp, NSH, PW, "wgrad_bp")), _halves(_wgrad(o_sb, dys, NSH, SBW, "wgrad_ba"))]
    chip_b = pair_sums(kb, pb, _exchange_alone(_ex_pair_swap(pb), "pair_swap_mix"))
    k_ab = k_gu2 + k_d2 + kb
    (dq, dk, dv), owned_ab = _attn_bwd(q, k, v, do_sb, ctot, exchange=_ex_scatter(chip_a + chip_b))
    halves_ab = chip_sums(k_ab, chip_a + chip_b, owned_ab)
    (dh1, d_gm, dproj), both_ab = _mix_bwd_in(dh2, h1, gm, (dxp, dq, dk, dv, dlg), win, exchange=_ex_share(halves_ab))
    for i, k_ in enumerate(k_ab):
        grad[k_] = both_ab[i].reshape(wts[k_].shape)

    p_in = [_halves(_wgrad(u, dproj, NSH, 512, "wgrad_in"))]
    p_d1, got_in = grad_down(a1, dh1, "wgrad_d1", exchange=_ex_pair_swap(p_in))
    sems_in, thru_in, token_in = _scatter_start(pair_sums(k_in, p_in, got_in), "scatter_in_start")
    dgu1, got_d1 = _ffn_bwd_act(dh1, gu1, wd1, "ffn1_bwd_act", exchange=_ex_pair_swap(p_d1), after=(token_in,))
    sems_d1, thru_d1, token_d1 = _scatter_start(pair_sums(k_d1, p_d1, got_d1), "scatter_d1_start")
    p_gu1 = [_halves(_wgrad(n1, dgu1, NSH, 512, "wgrad_gu1", after=(token_in, token_d1)))]
    chip_in, owned_in = _scatter_wait(sems_in, thru_in, p_gu1, "scatter_in_wait")
    chip_d1, owned_d1 = _scatter_wait(sems_d1, thru_d1, p_gu1, "scatter_d1_wait")
    halves_in, halves_d1 = chip_sums(k_in, chip_in, owned_in), chip_sums(k_d1, chip_d1, owned_d1)
    landed = _exchange_alone(_join(_ex_pair_swap(p_gu1), _ex_share(halves_in)), "pair_swap_gu1")
    grad["w_in"] = landed[1].reshape(w_in.shape)
    sems, thru, token = _scatter_start(pair_sums(k_gu1, p_gu1, landed[:1]), "scatter_gu1_start")
    for k_ in k_ab + k_in:
        adamw(k_, after=(token,))
    dx, d_g1 = _ffn_bwd_in(dh1, x0, g1, dgu1, wgu1, "ffn1_bwd_in", after=(token,))
    chip_gu1, owned_gu1 = _scatter_wait(sems, thru, [dx] + [delta[k_] for k_ in k_ab + k_in], "scatter_gu1_wait")

    small_g = dict(ffn1_norm=d_g1, mix_norm=d_gm, ffn2_norm=d_g2, final_norm=d_gf, pool_scale=d_scale,
                   pool_w_group=_wgrad_groups(pm, dyg), loss=loss_row)
    gathered, both = _gather_small(_pack_small(small_g), _ex_share(halves_d1 + chip_sums(k_gu1, chip_gu1, owned_gu1)))
    grad["ffn1_w_down"] = both[0].reshape(ffn1_w_down.shape)
    grad["ffn1_w_gate_up"] = both[1].reshape(ffn1_w_gate_up.shape)
    for k_ in k_d1 + k_gu1:
        adamw(k_)
    sg, sd, sm, sv = _small_update(gathered, _pack_small(wts), _pack_small(mom), _pack_small(var))
    sums = _unpack_small(sg, wts)
    loss = sums.pop("loss")
    grad.update(sums)
    for dst, packed in ((delta, sd), (new_m, sm), (new_v, sv)):
        vals = _unpack_small(packed, wts)
        vals.pop("loss")
        dst.update(vals)
    return (loss, dx[None], *[grad[k_] for k_ in ORDER], *[delta[k_] for k_ in ORDER],
            *[new_m[k_] for k_ in ORDER], *[new_v[k_] for k_ in ORDER])
```

```python
import functools

import jax
import jax.numpy as jnp
from jax import lax
from jax.experimental import pallas as pl
from jax.experimental.pallas import tpu as pltpu

F32 = jnp.float32
BF16 = jnp.bfloat16

S = 2048
D = 1024
DFF = 2816
FFS = 2 * DFF // 4
NSH = 4
PW = 512
PG = 128
POOL_WINDOWS = (2, 4, 8, 16)
HALO = 16
SBW = 512
DH = 64
EPS = 1e-6
SCALE = 0.125
LOG2E = 1.4426950408889634
TA = 256
QB = 2
MIB = 1024 * 1024

LR, B1, B2, AEPS, WD, STEP = 0.001, 0.9, 0.999, 1e-08, 0.01, 10

_VM = pl.BlockSpec(memory_space=pltpu.VMEM)
_ANY = pl.BlockSpec(memory_space=pl.ANY)
MESH = pl.DeviceIdType.MESH


def _nn(a, b):
    return jnp.dot(a, b, preferred_element_type=F32)


def _nt(a, b):
    return lax.dot_general(a, b, (((1,), (1,)), ((), ())), preferred_element_type=F32)


def _tn(a, b):
    return lax.dot_general(a, b, (((0,), (0,)), ((), ())), preferred_element_type=F32)


def _params(sem, vmem_mib):
    return pltpu.CompilerParams(dimension_semantics=sem, vmem_limit_bytes=vmem_mib * MIB)


def _rows(tm, width):
    return pl.BlockSpec((tm, width), lambda i: (i, 0))


def _fixed(shape):
    return pl.BlockSpec(shape, lambda *_: (0,) * len(shape))


def _sds(shape, dtype):
    return pltpu.HBM(shape, dtype)


def _in_hbm(args):
    return [pltpu.with_memory_space_constraint(a, pltpu.HBM) for a in args]


def _stage(pairs):
    @pl.when(pl.program_id(0) == 0)
    def _():
        for src, dst in pairs:
            pltpu.sync_copy(src, dst)


def _vmem_like(*arrays):
    return [pltpu.VMEM(a.shape, a.dtype) for a in arrays]


class Exchange:
    def __init__(self, arrays, landing, aliases, n_sems, start, finish):
        self.arrays, self.landing, self.aliases, self.n_sems = list(arrays), list(landing), dict(aliases), n_sems
        self.start, self.finish = start, finish


def _join(a, b):
    na, la = len(a.arrays), len(a.landing)

    def both(fa, fb):
        def run(ins, outs, ssem, rsem):
            fa(ins[:na], outs[:la], ssem.at[pl.ds(0, a.n_sems)], rsem.at[pl.ds(0, a.n_sems)])
            fb(ins[na:], outs[la:], ssem.at[pl.ds(a.n_sems, b.n_sems)], rsem.at[pl.ds(a.n_sems, b.n_sems)])
        return run

    aliases = {**a.aliases, **{na + i: la + j for i, j in b.aliases.items()}}
    return Exchange(a.arrays + b.arrays, a.landing + b.landing, aliases, a.n_sems + b.n_sems,
                    both(a.start, b.start), both(a.finish, b.finish))


def _call(body, args, *, name, grid, in_specs, out_specs, out_shape, scratch_shapes=(), compiler_params=None,
          exchange=None, free=(), after=()):
    args = [a if i in free else pltpu.with_memory_space_constraint(a, pltpu.HBM) for i, a in enumerate(args)]
    if exchange is None:
        n_in = len(in_specs)

        def plain(*refs):
            body(*refs[:n_in], *refs[n_in + len(after):])

        return pl.pallas_call(plain, name=name, grid=grid, in_specs=list(in_specs) + [_ANY] * len(after),
                              out_specs=out_specs, out_shape=out_shape, scratch_shapes=list(scratch_shapes),
                              compiler_params=compiler_params)(*args, *after)
    ex = exchange
    n_in, n_out, n_scr = len(in_specs), len(out_specs), len(scratch_shapes)
    na, nl = len(ex.arrays), len(ex.landing)

    def hosted(*refs):
        at = [0]

        def take(n):
            at[0] += n
            return refs[at[0] - n:at[0]]

        k_in, _, e_in, k_out, e_out, k_scr = take(n_in), take(len(after)), take(na), take(n_out), take(nl), take(n_scr)
        ssem, rsem = take(2)
        ids = [pl.program_id(a) for a in range(len(grid))]
        first = functools.reduce(jnp.logical_and, [i == 0 for i in ids])
        last = functools.reduce(jnp.logical_and, [i == g - 1 for i, g in zip(ids, grid)])

        @pl.when(first)
        def _():
            ex.start(e_in, e_out, ssem, rsem)

        body(*k_in, *k_out, *k_scr)

        @pl.when(last)
        def _():
            ex.finish(e_in, e_out, ssem, rsem)

    outs = pl.pallas_call(
        hosted, name=name, grid=grid,
        in_specs=list(in_specs) + [_ANY] * (len(after) + na), out_specs=list(out_specs) + [_ANY] * nl,
        out_shape=list(out_shape) + ex.landing,
        scratch_shapes=list(scratch_shapes) + [pltpu.SemaphoreType.DMA((ex.n_sems,))] * 2,
        input_output_aliases={n_in + len(after) + i: n_out + j for i, j in ex.aliases.items()},
        compiler_params=compiler_params,
    )(*args, *after, *_in_hbm(ex.arrays))
    return outs[:n_out], outs[n_out:]


def _exchange_alone(ex, name):
    def body(*refs):
        na, nl = len(ex.arrays), len(ex.landing)
        ex.start(refs[:na], refs[na:na + nl], refs[-2], refs[-1])
        ex.finish(refs[:na], refs[na:na + nl], refs[-2], refs[-1])

    return pl.pallas_call(
        body, name=name, in_specs=[_ANY] * len(ex.arrays), out_specs=[_ANY] * len(ex.landing),
        out_shape=ex.landing, scratch_shapes=[pltpu.SemaphoreType.DMA((ex.n_sems,))] * 2,
        input_output_aliases=ex.aliases,
    )(*_in_hbm(ex.arrays))


_HBM = pl.BlockSpec(memory_space=pltpu.HBM)
_SEM = pl.BlockSpec(memory_space=pltpu.SEMAPHORE)
_EFFECT = pltpu.SideEffectType.DATAFLOW_SIDE_EFFECTING


def _scatter_copies(srcs, lands, ssems, rsems):
    x, y, c, chips = _place()
    return [_remote(srcs[w].at[2 * px + py], lands[w].at[k], ssems[3 * w + k], rsems[3 * w + k], (px, py, c))
            for w in range(len(srcs)) for k, (px, py) in enumerate(chips)]


def _scatter_start(parts, name):
    n, ncp = len(parts), 3 * len(parts)
    lands = [lax.empty((3,) + p.shape[1:], p.dtype) for p in parts]

    def body(*refs):
        srcs, land_refs = refs[:n], refs[n:2 * n]
        ssems, rsems = refs[2 * n:2 * n + ncp], refs[2 * n + ncp:2 * n + 2 * ncp]
        for cp in _scatter_copies(srcs, land_refs, ssems, rsems):
            cp.start()
        token = refs[-1]
        token[...] = jnp.zeros_like(token)

    outs = pl.pallas_call(
        body, name=name,
        out_shape=([pltpu.SemaphoreType.DMA(())] * (2 * ncp) + [pltpu.HBM(a.shape, a.dtype) for a in parts + lands]
                   + [jax.ShapeDtypeStruct((8, 128), F32)]),
        in_specs=[_HBM] * (2 * n), out_specs=[_SEM] * (2 * ncp) + [_HBM] * (2 * n) + [_VM],
        input_output_aliases={i: 2 * ncp + i for i in range(2 * n)},
        compiler_params=pltpu.CompilerParams(has_side_effects=_EFFECT),
    )(*_in_hbm(parts), *_in_hbm(lands))
    sems, thru, token = outs[:2 * ncp], outs[2 * ncp:2 * ncp + 2 * n], outs[-1]
    return sems, thru, token


def _scatter_wait(sems, thru, after, name):
    n = len(thru) // 2
    ncp = 3 * n

    def body(*refs):
        srcs, land_refs = refs[:n], refs[n:2 * n]
        ssems, rsems = refs[2 * n:2 * n + ncp], refs[2 * n + ncp:2 * n + 2 * ncp]
        for cp in _scatter_copies(srcs, land_refs, ssems, rsems):
            cp.wait_send()
            cp.wait_recv()

    outs = pl.pallas_call(
        body, name=name, out_shape=[pltpu.HBM(a.shape, a.dtype) for a in thru],
        in_specs=[_HBM] * (2 * n) + [_SEM] * (2 * ncp) + [_ANY] * len(after), out_specs=[_HBM] * (2 * n),
        input_output_aliases={i: i for i in range(2 * n)},
        compiler_params=pltpu.CompilerParams(has_side_effects=_EFFECT),
    )(*thru, *sems, *after)
    return outs[:n], outs[n:]


def _gather_copies(bufs, ssems, rsems, sending):
    x, y, c, chips = _place()
    out = []
    for w, ref in enumerate(bufs):
        half = ref.shape[1] // 2
        for k, (px, py) in enumerate(chips):
            rows = ref.at[2 * x + y if sending else 2 * px + py, pl.ds(c * half, half)]
            out.append(_remote(rows, rows, ssems[3 * w + k], rsems[3 * w + k], (px, py, c)))
    return out


def _gather_start(bufs, after, name):
    n, ncp = len(bufs), 3 * len(bufs)

    def body(*refs):
        ssems, rsems = refs[n + len(after):n + len(after) + ncp], refs[n + len(after) + ncp:n + len(after) + 2 * ncp]
        for cp in _gather_copies(refs[:n], ssems, rsems, True):
            cp.start()
        token = refs[-1]
        token[...] = jnp.zeros_like(token)

    outs = pl.pallas_call(
        body, name=name,
        out_shape=([pltpu.SemaphoreType.DMA(())] * (2 * ncp) + [pltpu.HBM(a.shape, a.dtype) for a in bufs]
                   + [jax.ShapeDtypeStruct((8, 128), F32)]),
        in_specs=[_HBM] * n + [_ANY] * len(after), out_specs=[_SEM] * (2 * ncp) + [_HBM] * n + [_VM],
        input_output_aliases={i: 2 * ncp + i for i in range(n)},
        compiler_params=pltpu.CompilerParams(has_side_effects=_EFFECT),
    )(*_in_hbm(bufs), *after)
    return outs[:2 * ncp], outs[2 * ncp:2 * ncp + n], outs[-1]


def _gather_wait(sems, thru, after, name):
    n = len(thru)
    ncp = 3 * n

    def body(*refs):
        ssems, rsems = refs[n:n + ncp], refs[n + ncp:n + 2 * ncp]
        for cp in _gather_copies(refs[:n], ssems, rsems, True):
            cp.wait_send()
        for cp in _gather_copies(refs[:n], ssems, rsems, False):
            cp.wait_recv()

    return pl.pallas_call(
        body, name=name, out_shape=[pltpu.HBM(a.shape, a.dtype) for a in thru],
        in_specs=[_HBM] * n + [_SEM] * (2 * ncp) + [_ANY] * len(after), out_specs=[_HBM] * n,
        input_output_aliases={i: i for i in range(n)},
        compiler_params=pltpu.CompilerParams(has_side_effects=_EFFECT),
    )(*thru, *sems, *after)


def _rms(x):
    r = lax.rsqrt(jnp.mean(x * x, axis=-1, keepdims=True) + EPS)
    return r, x * r


def _rms_bwd(dn, xr, r, gain):
    dng = dn * gain
    dx = r * (dng - xr * jnp.mean(dng * xr, axis=-1, keepdims=True))
    return dx, jnp.sum(dn * xr, axis=0, keepdims=True)


def _ffn_fwd(x, gain, wgu, wd, name, exchange=None):
    tm = 256

    def body(x_ref, g_ref, wgu_hbm, wd_hbm, h_ref, n_ref, gu_ref, a_ref, wgu_ref, wd_ref):
        _stage([(wgu_hbm, wgu_ref), (wd_hbm, wd_ref)])
        x = x_ref[...]
        _, xr = _rms(x)
        n = (xr * g_ref[...]).astype(BF16)
        n_ref[...] = n
        acc = jnp.zeros((tm, D), F32)
        for j in range(2):
            g = _nn(n, wgu_ref[j])
            u = _nn(n, wgu_ref[2 + j])
            gu_ref[:, j * FFS:(j + 1) * FFS] = g.astype(BF16)
            gu_ref[:, (2 + j) * FFS:(3 + j) * FFS] = u.astype(BF16)
            half_act = (0.5 * (g * jax.nn.sigmoid(g) * u)).astype(BF16)
            a_ref[:, j * FFS:(j + 1) * FFS] = half_act
            acc = acc + _nn(half_act, wd_ref[j * FFS:(j + 1) * FFS, :])
        h_ref[...] = x + acc

    return _call(
        body, (x, gain, wgu, wd), name=name, grid=(S // tm,),
        in_specs=[_rows(tm, D), _fixed((1, D)), _ANY, _ANY],
        out_specs=[_rows(tm, D), _rows(tm, D), _rows(tm, 4 * FFS), _rows(tm, DFF)],
        out_shape=[_sds((S, D), F32), _sds((S, D), BF16), _sds((S, 4 * FFS), BF16), _sds((S, DFF), BF16)],
        scratch_shapes=_vmem_like(wgu, wd),
        compiler_params=_params(("arbitrary",), 56), exchange=exchange)


def _ffn_bwd_act(dh, gu, wd, name, exchange=None, after=()):
    tm = 512

    def body(dh_ref, gu_ref, wd_hbm, dgu_ref, wd_ref):
        _stage([(wd_hbm, wd_ref)])
        dhb = dh_ref[...].astype(BF16)
        for j in range(2):
            g = gu_ref[:, j * FFS:(j + 1) * FFS].astype(F32)
            u = gu_ref[:, (2 + j) * FFS:(3 + j) * FFS].astype(F32)
            da = 0.5 * _nt(dhb, wd_ref[j * FFS:(j + 1) * FFS, :])
            sg = jax.nn.sigmoid(g)
            dgu_ref[:, j * FFS:(j + 1) * FFS] = (da * u * (sg * (1.0 + g * (1.0 - sg)))).astype(BF16)
            dgu_ref[:, (2 + j) * FFS:(3 + j) * FFS] = (da * (g * sg)).astype(BF16)

    res = _call(
        body, (dh, gu, wd), name=name, grid=(S // tm,),
        in_specs=[_rows(tm, D), _rows(tm, 4 * FFS), _ANY], out_specs=[_rows(tm, 4 * FFS)],
        out_shape=[_sds((S, 4 * FFS), BF16)], scratch_shapes=_vmem_like(wd),
        compiler_params=_params(("arbitrary",), 56), exchange=exchange, after=after)
    return res[0] if exchange is None else (res[0][0], res[1])


def _ffn_bwd_in(dh, x, gain, dgu, wgu, name, exchange=None, after=()):
    tm = 512

    def body(dh_ref, x_ref, g_ref, dgu_ref, wgu_hbm, dx_ref, dg_ref, wgu_ref):
        _stage([(wgu_hbm, wgu_ref)])
        dn = jnp.zeros((tm, D), F32)
        for j in range(NSH):
            dn = dn + _nt(dgu_ref[:, j * FFS:(j + 1) * FFS], wgu_ref[j])
        r, xr = _rms(x_ref[...])
        dx, dgain = _rms_bwd(dn, xr, r, g_ref[...])
        dx_ref[...] = dh_ref[...] + dx

        @pl.when(pl.program_id(0) == 0)
        def _():
            dg_ref[...] = jnp.zeros_like(dg_ref)

        dg_ref[...] += dgain

    return _call(
        body, (dh, x, gain, dgu, wgu), name=name, grid=(S // tm,),
        in_specs=[_rows(tm, D), _rows(tm, D), _fixed((1, D)), _rows(tm, 4 * FFS), _ANY],
        out_specs=[_rows(tm, D), _fixed((1, D))],
        out_shape=[_sds((S, D), F32), _sds((1, D), F32)],
        scratch_shapes=_vmem_like(wgu),
        compiler_params=_params(("arbitrary",), 56), exchange=exchange, after=after)


def _head(h, target, gain):
    tm = 512

    def body(h_ref, t_ref, g_ref, dh_ref, loss_ref, dg_ref):
        gain = g_ref[...]
        r, hr = _rms(h_ref[...])
        err = hr * gain - t_ref[...]
        dy = err * (1.0 / D)
        dh, dgain = _rms_bwd(dy, hr, r, gain)
        dh_ref[...] = dh

        @pl.when(pl.program_id(0) == 0)
        def _():
            dg_ref[...] = jnp.zeros_like(dg_ref)
            loss_ref[...] = jnp.zeros_like(loss_ref)

        dg_ref[...] += dgain
        loss_ref[...] += jnp.full((1, 128), (0.5 / D) * jnp.sum(err * err), F32)

    return pl.pallas_call(
        body, name="head", grid=(S // tm,),
        in_specs=[_rows(tm, D), _rows(tm, D), _fixed((1, D))],
        out_specs=[_rows(tm, D), _fixed((1, 128)), _fixed((1, D))],
        out_shape=[_sds((S, D), F32), _sds((1, 128), F32), _sds((1, D), F32)],
        compiler_params=_params(("arbitrary",), 40),
    )(*_in_hbm([h]), target, gain)


def _mix_in(h, gain, w_in, after=()):
    tm = 512

    def body(h_ref, g_ref, w_hbm, u_ref, xp_ref, q_ref, k_ref, v_ref, gp_ref, gs_ref, w_ref):
        _stage([(w_hbm, w_ref)])
        _, hr = _rms(h_ref[...])
        u = (hr * g_ref[...]).astype(BF16)
        u_ref[...] = u
        p0 = _nn(u, w_ref[0])
        xp_ref[...] = p0[:, :PW]
        q_ref[...] = p0[:, PW:].astype(BF16)
        p1 = _nn(u, w_ref[1])
        k_ref[...] = p1[:, :SBW].astype(BF16)
        v_ref[...] = p1[:, SBW:].astype(BF16)
        gp_ref[...] = jax.nn.sigmoid(_nn(u, w_ref[2])).astype(BF16)
        gs_ref[...] = jax.nn.sigmoid(_nn(u, w_ref[3])).astype(BF16)

    return _call(
        body, (h, gain, w_in), name="mix_in", grid=(S // tm,),
        in_specs=[_rows(tm, D), _fixed((1, D)), _ANY],
        out_specs=[_rows(tm, D), _rows(tm, PW), _rows(tm, SBW), _rows(tm, SBW), _rows(tm, SBW),
                   _rows(tm, D), _rows(tm, D)],
        out_shape=[_sds((S, D), BF16), _sds((S, PW), F32), _sds((S, SBW), BF16), _sds((S, SBW), BF16),
                   _sds((S, SBW), BF16), _sds((S, D), BF16), _sds((S, D), BF16)],
        scratch_shapes=_vmem_like(w_in),
        compiler_params=_params(("arbitrary",), 48), free=(1,), after=after)


def _hilo_dot(x, tri):
    hi = x.astype(BF16)
    lo = (x - hi.astype(F32)).astype(BF16)
    return _nn(hi, tri) + _nn(lo, tri)


def _log_terms(qk):
    z2 = qk * (SCALE * LOG2E)
    lb = jnp.minimum(z2, 0.0) - jnp.log2(1.0 + jnp.exp2(-jnp.abs(z2)))
    return lb, lb - z2


def _head_masks():
    lane = lax.broadcasted_iota(jnp.int32, (1, 2 * DH), 1)
    return (lane < DH, lane >= DH)


def _attn_fwd(q, k, v, exchange=None):
    T = TA

    def body(q_ref, k_ref, v_ref, o_ref, c_ref):
        i2 = 2 * pl.program_id(1)
        row = lax.broadcasted_iota(jnp.int32, (T, T), 0)
        col = lax.broadcasted_iota(jnp.int32, (T, T), 1)
        after = (row > col).astype(BF16)
        causal = col < row
        masks = _head_masks()
        qms = {}
        for b in range(QB):
            q2 = q_ref[b * T:(b + 1) * T, :]
            for h, hm in enumerate(masks):
                qms[b, h] = jnp.where(hm, q2, jnp.zeros_like(q2))

        def blocks(keys, pairs, carries, os):
            ks, vms = [], []
            for j in keys:
                rows = pl.ds(pl.multiple_of(j * T, T), T)
                vj = v_ref[rows, :]
                ks.append(k_ref[rows, :])
                vms.append([jnp.where(hm, vj, jnp.zeros_like(vj)) for hm in masks])
            units = [(n, h) for n in range(len(pairs)) for h in range(2)]
            qks = {(n, h): _nt(qms[pairs[n][0], h], ks[pairs[n][1]]) for n, h in units}
            lbs, l1ms = {}, {}
            for u in units:
                lbs[u], l1m = _log_terms(qks[u])
                l1ms[u] = jnp.where(causal, l1m, 0.0) if pairs[u[0]][2] else l1m
            cins = {u: _hilo_dot(l1ms[u], after) for u in units}
            carries, os = dict(carries), list(os)
            for n, h in units:
                b, key, diag = pairs[n]
                a = jnp.exp2(lbs[n, h] + cins[n, h] + carries[b, h])
                if diag:
                    a = jnp.where(causal, a, 0.0)
                os[b] = os[b] + _nn(a.astype(BF16), vms[key][h])
                carries[b, h] = carries[b, h] + jnp.sum(l1ms[n, h], axis=1, keepdims=True)
            return carries, tuple(os)

        carries = {(b, h): jnp.zeros((T, 1), F32) for b in range(QB) for h in range(2)}
        os = tuple(jnp.zeros((T, 2 * DH), F32) for _ in range(QB))
        carries, os = blocks([i2 + 1, i2], [(1, 0, True), (0, 1, True), (1, 1, False)], carries, os)
        carries, os = lax.fori_loop(
            0, i2, lambda jj, c: blocks([i2 - 1 - jj], [(0, 0, False), (1, 0, False)], c[0], c[1]), (carries, os))
        for b in range(QB):
            o_ref[b * T:(b + 1) * T, :] = os[b].astype(BF16)
            c_ref[b * T:(b + 1) * T, :] = jnp.where(masks[0], carries[b, 0], carries[b, 1])

    blk = pl.BlockSpec((QB * T, 2 * DH), lambda p, i: (i, p))
    full = pl.BlockSpec((S, 2 * DH), lambda p, i: (0, p))
    return _call(
        body, (q, k, v), name="attn_fwd", grid=(SBW // (2 * DH), S // (QB * T)),
        in_specs=[blk, full, full], out_specs=[blk, blk],
        out_shape=[_sds((S, SBW), BF16), _sds((S, SBW), F32)],
        compiler_params=_params(("arbitrary", "arbitrary"), 40), exchange=exchange)


def _attn_bwd(q, k, v, do, ctot, exchange=None):
    T = TA
    nq = S // (QB * T)

    def body(q_ref, k_ref, v_ref, do_ref, c_ref, dq_ref, dk_ref, dv_ref, dk_acc, dv_acc):
        step = pl.program_id(1)
        i2 = 2 * step

        @pl.when(step == 0)
        def _():
            dk_acc[...] = jnp.zeros_like(dk_acc)
            dv_acc[...] = jnp.zeros_like(dv_acc)

        row = lax.broadcasted_iota(jnp.int32, (T, T), 0)
        col = lax.broadcasted_iota(jnp.int32, (T, T), 1)
        upto = (row <= col).astype(BF16)
        before = (row < col).astype(BF16)
        causal = col < row
        masks = _head_masks()
        qms, doms, ctots = {}, {}, {}
        for b in range(QB):
            q2, do2 = q_ref[b * T:(b + 1) * T, :], do_ref[b * T:(b + 1) * T, :]
            for h, hm in enumerate(masks):
                qms[b, h] = jnp.where(hm, q2, jnp.zeros_like(q2))
                doms[b, h] = jnp.where(hm, do2, jnp.zeros_like(do2))
                ctots[b, h] = c_ref[b * T:(b + 1) * T, h * DH:h * DH + 1]

        def blocks(keys, pairs, sums, dqs):
            rows = [pl.ds(pl.multiple_of(j * T, T), T) for j in keys]
            ks, vs = [k_ref[r, :] for r in rows], [v_ref[r, :] for r in rows]
            kms = [[jnp.where(hm, kj, jnp.zeros_like(kj)) for hm in masks] for kj in ks]
            units = [(n, h) for n in range(len(pairs)) for h in range(2)]
            qks = {(n, h): _nt(qms[pairs[n][0], h], ks[pairs[n][1]]) for n, h in units}
            das = {(n, h): _nt(doms[pairs[n][0], h], vs[pairs[n][1]]) for n, h in units}
            lbs, l1ms = {}, {}
            for u in units:
                lbs[u], l1m = _log_terms(qks[u])
                l1ms[u] = jnp.where(causal, l1m, 0.0) if pairs[u[0]][2] else l1m
            pins = {u: _hilo_dot(l1ms[u], upto) for u in units}
            sums = dict(sums)
            a_s, dls, cps = {}, {}, {}
            for n, h in units:
                b, _, diag = pairs[n]
                cl, cp = sums[b, h]
                a = jnp.exp2(lbs[n, h] + (ctots[b, h] - cl) - pins[n, h])
                if diag:
                    a = jnp.where(causal, a, 0.0)
                a_s[n, h] = a.astype(BF16)
                dls[n, h] = das[n, h] * a
                cps[n, h] = cp
                sums[b, h] = (cl + jnp.sum(l1ms[n, h], axis=1, keepdims=True),
                              cp + jnp.sum(dls[n, h], axis=1, keepdims=True))
            pexs = {u: _hilo_dot(dls[u], before) for u in units}
            dzbs = {}
            for u in units:
                dz = dls[u] - jnp.exp2(lbs[u]) * (dls[u] + pexs[u] + cps[u])
                if pairs[u[0]][2]:
                    dz = jnp.where(causal, dz, 0.0)
                dzbs[u] = dz.astype(BF16)
            dqs = list(dqs)
            for n, h in units:
                dqs[pairs[n][0]] = dqs[pairs[n][0]] + _nn(dzbs[n, h], kms[pairs[n][1]][h])
            for key, r in enumerate(rows):
                mine = [(n, h) for n, h in units if pairs[n][1] == key]
                dk_acc[r, :] += functools.reduce(jnp.add, [_tn(dzbs[u], qms[pairs[u[0]][0], u[1]]) for u in mine])
                dv_acc[r, :] += functools.reduce(jnp.add, [_tn(a_s[u], doms[pairs[u[0]][0], u[1]]) for u in mine])
            return sums, tuple(dqs)

        zero = jnp.zeros((T, 1), F32)
        sums = {(b, h): (zero, zero) for b in range(QB) for h in range(2)}
        dqs = tuple(jnp.zeros((T, 2 * DH), F32) for _ in range(QB))
        sums, dqs = lax.fori_loop(
            0, i2, lambda j, c: blocks([j], [(0, 0, False), (1, 0, False)], c[0], c[1]), (sums, dqs))
        _, dqs = blocks([i2, i2 + 1], [(0, 0, True), (1, 0, False), (1, 1, True)], sums, dqs)
        for b in range(QB):
            dq_ref[b * T:(b + 1) * T, :] = (dqs[b] * SCALE).astype(BF16)

        @pl.when(step == nq - 1)
        def _():
            dk_ref[...] = (dk_acc[...] * SCALE).astype(BF16)
            dv_ref[...] = dv_acc[...].astype(BF16)

    blk = pl.BlockSpec((QB * T, 2 * DH), lambda p, i: (i, p))
    full = pl.BlockSpec((S, 2 * DH), lambda p, i: (0, p))
    return _call(
        body, (q, k, v, do, ctot), name="attn_bwd", grid=(SBW // (2 * DH), nq),
        in_specs=[blk, full, full, blk, blk], out_specs=[blk, full, full],
        out_shape=[_sds((S, SBW), BF16), _sds((S, SBW), BF16), _sds((S, SBW), BF16)],
        scratch_shapes=[pltpu.VMEM((S, 2 * DH), F32), pltpu.VMEM((S, 2 * DH), F32)],
        compiler_params=_params(("arbitrary", "arbitrary"), 40), exchange=exchange)


def _pool_counts(first_row, tm):
    pos = first_row + lax.broadcasted_iota(jnp.int32, (tm, 1), 0)
    return [jnp.minimum(pos + 1, w).astype(F32) for w in POOL_WINDOWS]


def _mix_out(h, xp, o_sb, gp, gs, w_group, scale, w_bp, w_ba, w_out, exchange=None):
    tm = 512

    def body(h_ref, xp_ref, o_ref, gp_ref, gs_ref, wg_hbm, sc_ref, wbp_hbm, wba_hbm, wo_hbm,
             h2_ref, pm_ref, p_ref, yp_ref, ys_ref, m_ref, halo, wg_ref, wbp_ref, wba_ref, wo_ref):
        _stage([(wg_hbm, wg_ref), (wbp_hbm, wbp_ref), (wba_hbm, wba_ref), (wo_hbm, wo_ref)])
        i = pl.program_id(0)

        @pl.when(i == 0)
        def _():
            halo[...] = jnp.zeros_like(halo)

        xp = xp_ref[...]
        ext = jnp.concatenate([halo[...], xp], axis=0)
        halo[...] = xp[tm - HALO:, :]
        counts = _pool_counts(i * tm, tm)
        for gi in range(len(POOL_WINDOWS)):
            lanes = slice(gi * PG, (gi + 1) * PG)
            win = ext[:, lanes]
            for step in range(gi + 1):
                win = win + pltpu.roll(win, 1 << step, 0)
            pm = (win[HALO:, :] / counts[gi] - xp[:, lanes]).astype(BF16)
            pm_ref[:, lanes] = pm
            p_ref[:, lanes] = (_nn(pm, wg_ref[gi]) * sc_ref[:, lanes]).astype(BF16)
        pb = p_ref[...]
        ob = o_ref[...]
        for j in range(NSH):
            cols = slice(j * (D // NSH), (j + 1) * (D // NSH))
            yp = _nn(pb, wbp_ref[j])
            ys = _nn(ob, wba_ref[j])
            yp_ref[:, cols] = yp.astype(BF16)
            ys_ref[:, cols] = ys.astype(BF16)
            m_ref[:, cols] = (gp_ref[:, cols].astype(F32) * yp + gs_ref[:, cols].astype(F32) * ys).astype(BF16)
        h2_ref[...] = h_ref[...] + _nn(m_ref[...], wo_ref[...])

    return _call(
        body, (h, xp, o_sb, gp, gs, w_group, scale, w_bp, w_ba, w_out), name="mix_out", grid=(S // tm,),
        in_specs=[_rows(tm, D), _rows(tm, PW), _rows(tm, SBW), _rows(tm, D), _rows(tm, D),
                  _ANY, _fixed((1, PW)), _ANY, _ANY, _ANY],
        out_specs=[_rows(tm, D), _rows(tm, PW), _rows(tm, PW), _rows(tm, D), _rows(tm, D), _rows(tm, D)],
        out_shape=[_sds((S, D), F32), _sds((S, PW), BF16), _sds((S, PW), BF16), _sds((S, D), BF16),
                   _sds((S, D), BF16), _sds((S, D), BF16)],
        scratch_shapes=[pltpu.VMEM((HALO, PW), F32)] + _vmem_like(w_group, w_bp, w_ba, w_out),
        compiler_params=_params(("arbitrary",), 48), free=(5, 6), exchange=exchange)


def _mix_bwd_out(dh, gp, gs, yp, ys, pm, w_group, scale, w_bp, w_ba, w_out, exchange=None):
    tm = 512
    nt = S // tm

    def body(dh_ref, gp_ref, gs_ref, yp_ref, ys_ref, pm_ref, wg_hbm, sc_ref, wbp_hbm, wba_hbm, wo_hbm,
             dlg_ref, dyp_ref, dys_ref, do_ref, dyg_ref, dxp_ref, dsc_ref, halo, wg_ref, wbp_ref, wba_ref, wo_ref):
        _stage([(wg_hbm, wg_ref), (wbp_hbm, wbp_ref), (wba_hbm, wba_ref), (wo_hbm, wo_ref)])
        step = pl.program_id(0)

        @pl.when(step == 0)
        def _():
            halo[...] = jnp.zeros_like(halo)
            dsc_ref[...] = jnp.zeros_like(dsc_ref)

        dm = _nt(dh_ref[...].astype(BF16), wo_ref[...])
        gp = gp_ref[...].astype(F32)
        gs = gs_ref[...].astype(F32)
        yp = yp_ref[...].astype(F32)
        ys = ys_ref[...].astype(F32)
        dlg_ref[:, :D] = (dm * yp * gp * (1.0 - gp)).astype(BF16)
        dlg_ref[:, D:] = (dm * ys * gs * (1.0 - gs)).astype(BF16)
        dyp_ref[...] = (dm * gp).astype(BF16)
        dys_ref[...] = (dm * gs).astype(BF16)
        dp = jnp.zeros((tm, PW), F32)
        do = jnp.zeros((tm, SBW), F32)
        for j in range(NSH):
            cols = slice(j * (D // NSH), (j + 1) * (D // NSH))
            dp = dp + _nt(dyp_ref[:, cols], wbp_ref[j])
            do = do + _nt(dys_ref[:, cols], wba_ref[j])
        do_ref[...] = do.astype(BF16)
        counts = _pool_counts((nt - 1 - step) * tm, tm)
        dscale = []
        for gi in range(len(POOL_WINDOWS)):
            lanes = slice(gi * PG, (gi + 1) * PG)
            dpg = dp[:, lanes]
            dscale.append(jnp.sum(dpg * _nn(pm_ref[:, lanes], wg_ref[gi]), axis=0, keepdims=True))
            dyg = (dpg * sc_ref[:, lanes]).astype(BF16)
            dyg_ref[:, lanes] = dyg
            dpm = _nt(dyg, wg_ref[gi])
            per = dpm / counts[gi]
            win = jnp.concatenate([per, halo[:, lanes]], axis=0)
            halo[:, lanes] = per[:HALO, :]
            for s in range(gi + 1):
                win = win + pltpu.roll(win, tm + HALO - (1 << s), 0)
            dxp_ref[:, lanes] = (win[:tm, :] - dpm).astype(BF16)
        dsc_ref[...] += jnp.concatenate(dscale, axis=1)

    rev = lambda width: pl.BlockSpec((tm, width), lambda i: (nt - 1 - i, 0))
    return _call(
        body, (dh, gp, gs, yp, ys, pm, w_group, scale, w_bp, w_ba, w_out), name="mix_bwd_out", grid=(nt,),
        in_specs=[rev(D), rev(D), rev(D), rev(D), rev(D), rev(PW), _ANY, _fixed((1, PW)), _ANY, _ANY, _ANY],
        out_specs=[rev(2 * D), rev(D), rev(D), rev(SBW), rev(PW), rev(PW), _fixed((1, PW))],
        out_shape=[_sds((S, 2 * D), BF16), _sds((S, D), BF16), _sds((S, D), BF16), _sds((S, SBW), BF16),
                   _sds((S, PW), BF16), _sds((S, PW), BF16), _sds((1, PW), F32)],
        scratch_shapes=[pltpu.VMEM((HALO, PW), F32)] + _vmem_like(w_group, w_bp, w_ba, w_out),
        compiler_params=_params(("arbitrary",), 48), exchange=exchange)


def _mix_bwd_in(dh, h, gain, pieces, w_in, exchange=None):
    tm = 512
    widths = [p.shape[1] for p in pieces]

    def body(dh_ref, h_ref, g_ref, *rest):
        piece_refs, (w_hbm, dx_ref, dg_ref, dp_ref, w_ref) = rest[:len(pieces)], rest[len(pieces):]
        _stage([(w_hbm, w_ref)])
        at = 0
        for ref, width in zip(piece_refs, widths):
            dp_ref[:, at:at + width] = ref[...]
            at += width
        du = jnp.zeros((tm, D), F32)
        for j in range(NSH):
            du = du + _nt(dp_ref[:, j * D:(j + 1) * D], w_ref[j])
        r, hr = _rms(h_ref[...])
        dx, dgain = _rms_bwd(du, hr, r, g_ref[...])
        dx_ref[...] = dh_ref[...] + dx

        @pl.when(pl.program_id(0) == 0)
        def _():
            dg_ref[...] = jnp.zeros_like(dg_ref)

        dg_ref[...] += dgain

    return _call(
        body, (dh, h, gain, *pieces, w_in), name="mix_bwd_in", grid=(S // tm,),
        in_specs=[_rows(tm, D), _rows(tm, D), _fixed((1, D))] + [_rows(tm, w) for w in widths] + [_ANY],
        out_specs=[_rows(tm, D), _fixed((1, D)), _rows(tm, 4 * D)],
        out_shape=[_sds((S, D), F32), _sds((1, D), F32), _sds((S, 4 * D), BF16)],
        scratch_shapes=_vmem_like(w_in),
        compiler_params=_params(("arbitrary",), 48), exchange=exchange)


def _wgrad(a, b, nblk, ti, name, out_dtype=BF16, exchange=None, after=()):
    ka, n = a.shape[1], b.shape[1]
    ns = n // nblk

    def body(a_ref, b_ref, o_ref):
        o_ref[...] = _tn(a_ref[...].astype(BF16), b_ref[...].astype(BF16)).astype(out_dtype)

    res = _call(
        body, (a, b), name=name, grid=(nblk, ka // ti),
        in_specs=[pl.BlockSpec((S, ti), lambda j, i: (0, i)), pl.BlockSpec((S, ns), lambda j, i: (0, j))],
        out_specs=[pl.BlockSpec((None, ti, ns), lambda j, i: (j, i, 0))],
        out_shape=[_sds((nblk, ka, ns), out_dtype)],
        compiler_params=_params(("arbitrary", "arbitrary"), 56), exchange=exchange, after=after)
    return res[0] if exchange is None else (res[0][0], res[1])


def _wgrad_groups(pm, dyg):
    def body(a_ref, b_ref, o_ref):
        o_ref[...] = _tn(a_ref[...], b_ref[...])

    col = pl.BlockSpec((S, PG), lambda g: (0, g))
    return pl.pallas_call(
        body, name="wgrad_groups", grid=(PW // PG,),
        in_specs=[col, col], out_specs=pl.BlockSpec((None, PG, PG), lambda g: (g, 0, 0)),
        out_shape=_sds((PW // PG, PG, PG), F32),
        compiler_params=_params(("arbitrary",), 32),
    )(*_in_hbm([pm, dyg]))


def _place():
    x, y, c = lax.axis_index("x"), lax.axis_index("y"), lax.axis_index("c")
    chips = [(1 - x, y), (x, 1 - y), (1 - x, 1 - y)]
    return x, y, c, chips


def _remote(src, dst, ssem, rsem, dev):
    return pltpu.make_async_remote_copy(src_ref=src, dst_ref=dst, send_sem=ssem, recv_sem=rsem,
                                        device_id=dev, device_id_type=MESH)


def _cast_into_block(w, me_idx, name):
    rows, cols = w.shape
    tr = _row_block(rows)

    def body(me_ref, w_ref, o_ref):
        o_ref[...] = w_ref[...].astype(BF16)

    return pl.pallas_call(
        body, name=name, out_shape=_sds((NSH, rows, cols), BF16),
        grid_spec=pltpu.PrefetchScalarGridSpec(
            num_scalar_prefetch=1, grid=(rows // tr,),
            in_specs=[pl.BlockSpec((tr, cols), lambda r, me: (r, 0))],
            out_specs=pl.BlockSpec((None, tr, cols), lambda r, me: (me[0], r, 0))),
        compiler_params=_params(("arbitrary",), 32),
    )(me_idx, w)


def _ex_gather(bufs):
    n = len(bufs)
    per = 8

    def plan(outs, ssem, rsem, w):
        x, y, c, _ = _place()
        sib, nbr_x, nbr_y = (x, y, 1 - c), (1 - x, y, c), (x, 1 - y, c)
        half = outs[w].shape[1] // 2
        quarter = half // 2
        sem = lambda k: (ssem.at[per * w + k], rsem.at[per * w + k])
        rows = lambda blk, start, size: outs[w].at[blk, pl.ds(start, size)]
        mine = rows(2 * x + y, c * half, half)
        from_x = rows(2 * (1 - x) + y, c * half, half)
        from_y = rows(2 * x + (1 - y), c * half, half)
        diag = 2 * (1 - x) + (1 - y)
        pass_y = rows(2 * (1 - x) + y, c * half, quarter)
        pass_x = rows(2 * x + (1 - y), c * half + quarter, quarter)
        diag_0, diag_1 = rows(diag, c * half, quarter), rows(diag, c * half + quarter, quarter)
        first = [_remote(mine, mine, *sem(0), nbr_x), _remote(mine, mine, *sem(1), nbr_y)]
        arrivals = [
            (_remote(from_x, from_x, *sem(0), nbr_x),
             [_remote(pass_y, pass_y, *sem(2), nbr_y), _remote(from_x, from_x, *sem(4), sib)]),
            (_remote(from_y, from_y, *sem(1), nbr_y),
             [_remote(pass_x, pass_x, *sem(3), nbr_x), _remote(from_y, from_y, *sem(5), sib)]),
            (_remote(diag_0, diag_0, *sem(2), nbr_y), [_remote(diag_0, diag_0, *sem(6), sib)]),
            (_remote(diag_1, diag_1, *sem(3), nbr_x), [_remote(diag_1, diag_1, *sem(7), sib)]),
        ]
        other = (1 - c) * half
        from_sibling = [
            _remote(rows(2 * (1 - x) + y, other, half), rows(2 * (1 - x) + y, other, half), *sem(4), sib),
            _remote(rows(2 * x + (1 - y), other, half), rows(2 * x + (1 - y), other, half), *sem(5), sib),
            _remote(rows(diag, other, quarter), rows(diag, other, quarter), *sem(6), sib),
            _remote(rows(diag, other + quarter, quarter), rows(diag, other + quarter, quarter), *sem(7), sib),
        ]
        return first, arrivals, from_sibling

    def start(ins, outs, ssem, rsem):
        x, y, c, _ = _place()
        for w in range(n):
            half = outs[w].shape[1] // 2
            mine = outs[w].at[2 * x + y, pl.ds(c * half, half)]
            _remote(mine, mine, ssem.at[per * w], rsem.at[per * w], (1 - x, y, c)).start()
            _remote(mine, mine, ssem.at[per * w + 1], rsem.at[per * w + 1], (x, 1 - y, c)).start()

    def finish(ins, outs, ssem, rsem):
        plans = [plan(outs, ssem, rsem, w) for w in range(n)]
        started = []
        for direct in (True, False):
            for first, arrivals, _ in plans:
                for arrived, onward in (arrivals[:2] if direct else arrivals[2:]):
                    arrived.wait_recv()
                    for cp in onward:
                        cp.start()
                    started += onward
        for first, _, from_sibling in plans:
            for cp in from_sibling:
                cp.wait_recv()
            started += first
        for cp in started:
            cp.wait_send()

    return Exchange(bufs, [_sds(b.shape, b.dtype) for b in bufs], {w: w for w in range(n)}, per * n, start, finish)


def _ex_gather_direct(bufs):
    n = len(bufs)

    def copies(outs, ssem, rsem, only_first=False):
        x, y, c, chips = _place()
        me, sib = 2 * x + y, (x, y, 1 - c)
        first, relay, last = [], [], []
        for w in range(n):
            half = outs[w].shape[1] // 2
            mine = outs[w].at[me, pl.ds(c * half, half)]
            for k, (px, py) in enumerate(chips):
                sems = (ssem.at[6 * w + k], rsem.at[6 * w + k])
                sib_sems = (ssem.at[6 * w + 3 + k], rsem.at[6 * w + 3 + k])
                first.append(_remote(mine, mine, *sems, (px, py, c)))
                if only_first:
                    continue
                got = outs[w].at[2 * px + py, pl.ds(c * half, half)]
                relay.append((_remote(got, got, *sems, (px, py, c)), _remote(got, got, *sib_sems, sib)))
                theirs = outs[w].at[2 * px + py, pl.ds((1 - c) * half, half)]
                last.append(_remote(theirs, theirs, *sib_sems, sib))
        return first, relay, last

    def start(ins, outs, ssem, rsem):
        for cp in copies(outs, ssem, rsem, only_first=True)[0]:
            cp.start()

    def finish(ins, outs, ssem, rsem):
        first, relay, last = copies(outs, ssem, rsem)
        for arrived, onward in relay:
            arrived.wait_recv()
            onward.start()
        for cp in last:
            cp.wait_recv()
        for cp in first:
            cp.wait_send()
        for _, onward in relay:
            onward.wait_send()

    return Exchange(bufs, [_sds(b.shape, b.dtype) for b in bufs], {w: w for w in range(n)}, 6 * n, start, finish)


def _simple_exchange(arrays, landing, aliases, make_copies):
    def start(ins, outs, ssem, rsem):
        for cp, _ in make_copies(ins, outs, ssem, rsem, False):
            cp.start()

    def finish(ins, outs, ssem, rsem):
        cps = make_copies(ins, outs, ssem, rsem, True)
        for _, landed in cps:
            landed.wait_recv()
        for cp, _ in cps:
            cp.wait_send()

    return Exchange(arrays, landing, aliases, len(arrays) * 3, start, finish)


def _ex_pair_swap(grads):
    def make(ins, outs, ssem, rsem, landing):
        x, y, c, _ = _place()
        cps = [_remote(ins[w].at[:, 1 - c], outs[w], ssem.at[w], rsem.at[w], (x, y, 1 - c))
               for w in range(len(grads))]
        return [(cp, cp) for cp in cps]

    return _simple_exchange(grads, [_sds((NSH,) + g.shape[2:], g.dtype) for g in grads], {}, make)


def _ex_scatter(parts):
    def make(ins, outs, ssem, rsem, landing):
        x, y, c, chips = _place()
        out = []
        for w in range(len(parts)):
            for k, (px, py) in enumerate(chips):
                sems = (ssem.at[3 * w + k], rsem.at[3 * w + k])
                out.append((_remote(ins[w].at[2 * px + py], outs[w].at[k], *sems, (px, py, c)),
                            _remote(outs[w].at[k], outs[w].at[k], *sems, (px, py, c)) if landing else None))
        return out

    return _simple_exchange(parts, [_sds((3,) + p.shape[1:], p.dtype) for p in parts], {}, make)


def _ex_relay(bufs):
    def make(ins, outs, ssem, rsem, landing):
        x, y, c, chips = _place()
        sib = (x, y, 1 - c)
        out = []
        for w in range(len(bufs)):
            half = outs[w].shape[1] // 2
            for k, (px, py) in enumerate(chips):
                sems = (ssem.at[3 * w + k], rsem.at[3 * w + k])
                have = outs[w].at[2 * px + py, pl.ds(c * half, half)]
                miss = outs[w].at[2 * px + py, pl.ds((1 - c) * half, half)]
                out.append((_remote(have, have, *sems, sib), _remote(miss, miss, *sems, sib) if landing else None))
        return out

    return _simple_exchange(bufs, [_sds(b.shape, b.dtype) for b in bufs], {w: w for w in range(len(bufs))}, make)


def _ex_share(bufs):
    def make(ins, outs, ssem, rsem, landing):
        x, y, c, _ = _place()
        sib = (x, y, 1 - c)
        return [(_remote(outs[w].at[c], outs[w].at[c], ssem.at[w], rsem.at[w], sib),
                 _remote(outs[w].at[1 - c], outs[w].at[1 - c], ssem.at[w], rsem.at[w], sib) if landing else None)
                for w in range(len(bufs))]

    return _simple_exchange(bufs, [_sds(b.shape, b.dtype) for b in bufs], {w: w for w in range(len(bufs))}, make)


def _gather_small(block, ex):
    m_per, n = block.shape
    na, nl = len(ex.arrays), len(ex.landing)

    def body(x_ref, *refs):
        e_in, out_ref, e_out = refs[:na], refs[na], refs[na + 1:na + 1 + nl]
        ssem, rsem, lsem, e_ssem, e_rsem = refs[na + 1 + nl:]
        ex.start(e_in, e_out, e_ssem, e_rsem)
        x, y, c, chips = _place()
        me, sib = (x, y, c), (x, y, 1 - c)

        def rows(px, py, pc):
            return out_ref.at[pl.ds((4 * px + 2 * py + pc) * m_per, m_per), :]

        def copy(k, blk, to, src=None):
            return _remote(rows(*blk) if src is None else src, rows(*blk), ssem.at[k], rsem.at[k], to)

        mine = pltpu.make_async_copy(x_ref, rows(*me), lsem)
        mine.start()
        first = [copy(0, me, sib, src=x_ref)]
        first += [copy(1 + j, me, (*chip, c), src=x_ref) for j, chip in enumerate(chips)]
        for cp in first:
            cp.start()
        passed = [copy(4 + j, (*chip, c), sib) for j, chip in enumerate(chips)]
        for j, chip in enumerate(chips):
            copy(1 + j, (*chip, c), me).wait_recv()
            passed[j].start()
        copy(0, sib, me).wait_recv()
        for j, chip in enumerate(chips):
            copy(4 + j, (*chip, 1 - c), me).wait_recv()
        for cp in first + passed:
            cp.wait_send()
        mine.wait()
        ex.finish(e_in, e_out, e_ssem, e_rsem)

    outs = pl.pallas_call(
        body, name="gather_small", out_shape=[jax.ShapeDtypeStruct((8 * m_per, n), block.dtype)] + ex.landing,
        in_specs=[_VM] + [_ANY] * na, out_specs=[_VM] + [_ANY] * nl,
        scratch_shapes=[pltpu.SemaphoreType.DMA((7,)), pltpu.SemaphoreType.DMA((7,)), pltpu.SemaphoreType.DMA]
        + [pltpu.SemaphoreType.DMA((ex.n_sems,))] * 2,
        input_output_aliases={1 + i: 1 + j for i, j in ex.aliases.items()},
    )(block, *_in_hbm(ex.arrays))
    return outs[0], outs[1:]


def _row_block(rows):
    return max(t for t in range(16, 257, 16) if rows % t == 0)


def _pair_sum(grad, got, c_idx, name):
    _, _, half, cols = grad.shape
    tr = _row_block(half)

    def body(c_ref, a_ref, b_ref, o_ref):
        o_ref[...] = (a_ref[...].astype(F32) + b_ref[...].astype(F32)).astype(BF16)

    return pl.pallas_call(
        body, name=name, out_shape=_sds((NSH, half, cols), BF16),
        grid_spec=pltpu.PrefetchScalarGridSpec(
            num_scalar_prefetch=1, grid=(NSH, half // tr),
            in_specs=[pl.BlockSpec((None, None, tr, cols), lambda j, r, c: (j, c[0], r, 0)),
                      pl.BlockSpec((None, tr, cols), lambda j, r, c: (j, r, 0))],
            out_specs=pl.BlockSpec((None, tr, cols), lambda j, r, c: (j, r, 0))),
        compiler_params=_params(("arbitrary", "arbitrary"), 32),
    )(c_idx, *_in_hbm([grad, got]))


def _chip_sum(own, got, place, name):
    _, half, cols = own.shape
    tr = _row_block(half)

    def body(place_ref, own_ref, got_ref, o_ref):
        acc = own_ref[...].astype(F32)
        for k in range(3):
            acc = acc + got_ref[k].astype(F32)
        o_ref[...] = acc

    return pl.pallas_call(
        body, name=name, out_shape=_sds((2, half, cols), F32),
        grid_spec=pltpu.PrefetchScalarGridSpec(
            num_scalar_prefetch=1, grid=(half // tr,),
            in_specs=[pl.BlockSpec((None, tr, cols), lambda r, p: (p[0], r, 0)),
                      pl.BlockSpec((3, tr, cols), lambda r, p: (0, r, 0))],
            out_specs=pl.BlockSpec((None, tr, cols), lambda r, p: (p[1], r, 0))),
        compiler_params=_params(("arbitrary",), 32),
    )(place, *_in_hbm([own, got]))


def _adamw_math(w, g, m, v):
    m = B1 * m + (1.0 - B1) * g
    v = B2 * v + (1.0 - B2) * (g * g)
    m_hat = m / (1.0 - B1 ** STEP)
    v_hat = v / (1.0 - B2 ** STEP)
    return -LR * (m_hat / (jnp.sqrt(v_hat) + AEPS) + WD * w), m, v


def _adamw(w, g, m, v, name, after=()):
    rows, cols = w.shape
    tr = _row_block(rows)

    def body(w_ref, g_ref, m_ref, v_ref, go_ref, d_ref, nm_ref, nv_ref):
        g = g_ref[...]
        go_ref[...] = g
        d_ref[...], nm_ref[...], nv_ref[...] = _adamw_math(w_ref[...], g, m_ref[...], v_ref[...])

    blk = pl.BlockSpec((tr, cols), lambda r: (r, 0))
    return _call(
        body, (w, g, m, v), name=name, grid=(rows // tr,), out_shape=[_sds(w.shape, F32)] * 4,
        in_specs=[blk] * 4, out_specs=[blk] * 4,
        compiler_params=_params(("arbitrary",), 32), free=(0, 2, 3), after=after)


def _small_update(gathered, w, m, v):
    rows = w.shape[0]

    def body(ga_ref, w_ref, m_ref, v_ref, g_ref, d_ref, nm_ref, nv_ref):
        g = ga_ref[0:rows, :]
        for dev in range(1, 8):
            g = g + ga_ref[dev * rows:(dev + 1) * rows, :]
        g_ref[...] = g
        d_ref[...], nm_ref[...], nv_ref[...] = _adamw_math(w_ref[...], g, m_ref[...], v_ref[...])

    return pl.pallas_call(
        body, name="small_update", out_shape=[jax.ShapeDtypeStruct(w.shape, F32)] * 4,
        in_specs=[_VM] * 4, out_specs=[_VM] * 4,
    )(gathered, w, m, v)


SMALL = ("ffn1_norm", "mix_norm", "ffn2_norm", "final_norm", "pool_scale", "pool_w_group", "loss")
BIG = ("ffn1_w_gate_up", "ffn1_w_down", "w_in", "w_branch_pool", "w_branch_attn", "w_out",
       "ffn2_w_gate_up", "ffn2_w_down")
ORDER = ("ffn1_norm", "ffn1_w_gate_up", "ffn1_w_down", "mix_norm", "w_in", "pool_w_group", "pool_scale",
         "w_branch_pool", "w_branch_attn", "w_out", "ffn2_norm", "ffn2_w_gate_up", "ffn2_w_down", "final_norm")
SMALL_ROWS = 560


def _pack_small(t):
    parts = []
    for k in SMALL:
        rows = t[k].reshape(-1, 128) if k in t else jnp.zeros((1, 128), F32)
        parts.append(jnp.pad(rows, ((0, -rows.shape[0] % 8), (0, 0))))
    packed = jnp.concatenate(parts, axis=0)
    assert packed.shape == (SMALL_ROWS, 128), packed.shape
    return packed


def _unpack_small(packed, like):
    out, at = {}, 0
    for k in SMALL:
        n = like[k].size // 128 if k in like else 1
        out[k] = packed[at:at + n].reshape(like[k].shape) if k in like else packed[at, 0]
        at += n + (-n % 8)
    return out


def _halves(g):
    return g.reshape(NSH, 2, g.shape[1] // 2, g.shape[2])


def kernel(x, ffn1_norm, ffn1_w_gate_up, ffn1_w_down, mix_norm, w_in, pool_w_group, pool_scale, w_branch_pool, w_branch_attn, w_out, ffn2_norm, ffn2_w_gate_up, ffn2_w_down, final_norm, loss_target, m_ffn1_norm, m_ffn1_w_gate_up, m_ffn1_w_down, m_mix_norm, m_w_in, m_pool_w_group, m_pool_scale, m_w_branch_pool, m_w_branch_attn, m_w_out, m_ffn2_norm, m_ffn2_w_gate_up, m_ffn2_w_down, m_final_norm, v_ffn1_norm, v_ffn1_w_gate_up, v_ffn1_w_down, v_mix_norm, v_w_in, v_pool_w_group, v_pool_scale, v_w_branch_pool, v_w_branch_attn, v_w_out, v_ffn2_norm, v_ffn2_w_gate_up, v_ffn2_w_down, v_final_norm):
    wts = dict(ffn1_norm=ffn1_norm, ffn1_w_gate_up=ffn1_w_gate_up, ffn1_w_down=ffn1_w_down, mix_norm=mix_norm,
               w_in=w_in, pool_w_group=pool_w_group, pool_scale=pool_scale, w_branch_pool=w_branch_pool,
               w_branch_attn=w_branch_attn, w_out=w_out, ffn2_norm=ffn2_norm, ffn2_w_gate_up=ffn2_w_gate_up,
               ffn2_w_down=ffn2_w_down, final_norm=final_norm)
    mom = dict(ffn1_norm=m_ffn1_norm, ffn1_w_gate_up=m_ffn1_w_gate_up, ffn1_w_down=m_ffn1_w_down,
               mix_norm=m_mix_norm, w_in=m_w_in, pool_w_group=m_pool_w_group, pool_scale=m_pool_scale,
               w_branch_pool=m_w_branch_pool, w_branch_attn=m_w_branch_attn, w_out=m_w_out,
               ffn2_norm=m_ffn2_norm, ffn2_w_gate_up=m_ffn2_w_gate_up, ffn2_w_down=m_ffn2_w_down,
               final_norm=m_final_norm)
    var = dict(ffn1_norm=v_ffn1_norm, ffn1_w_gate_up=v_ffn1_w_gate_up, ffn1_w_down=v_ffn1_w_down,
               mix_norm=v_mix_norm, w_in=v_w_in, pool_w_group=v_pool_w_group, pool_scale=v_pool_scale,
               w_branch_pool=v_w_branch_pool, w_branch_attn=v_w_branch_attn, w_out=v_w_out,
               ffn2_norm=v_ffn2_norm, ffn2_w_gate_up=v_ffn2_w_gate_up, ffn2_w_down=v_ffn2_w_down,
               final_norm=v_final_norm)

    c_idx = lax.axis_index("c").astype(jnp.int32).reshape(1)
    me_idx = (2 * lax.axis_index("x") + lax.axis_index("y")).astype(jnp.int32).reshape(1)
    place = jnp.concatenate([me_idx, c_idx])
    x0, tgt = x[0], loss_target[0]
    wgrp = pool_w_group[0].astype(BF16)
    g1, gm, g2, gf = ffn1_norm, mix_norm, ffn2_norm, final_norm.reshape(1, D)
    grad, delta, new_m, new_v = {}, {}, {}, {}

    def pair_sums(keys, parts, got):
        return [_pair_sum(parts[i], got[i], c_idx, "pair_sum_" + k) for i, k in enumerate(keys)]

    def chip_sums(keys, chip_parts, owned):
        return [_chip_sum(chip_parts[i], owned[i], place, "chip_sum_" + k) for i, k in enumerate(keys)]

    def adamw(k, after=()):
        outs = _adamw(wts[k][0], grad[k][0], mom[k][0], var[k][0], "adamw_" + k, after=after)
        grad[k], delta[k], new_m[k], new_v[k] = (o.reshape(wts[k].shape) for o in outs)

    own = {k: _cast_into_block(wts[k][0], me_idx, "cast_" + k) for k in BIG}
    first, late = ("ffn1_w_gate_up", "ffn1_w_down"), ("w_branch_pool", "w_branch_attn", "w_out",
                                                       "ffn2_w_gate_up", "ffn2_w_down")
    full = dict(zip(first, _exchange_alone(_ex_gather([own[k] for k in first]), "gather_ffn1")))
    wgu1, wd1 = full["ffn1_w_gate_up"], full["ffn1_w_down"].reshape(DFF, D)
    (h1, n1, gu1, a1), (win,) = _ffn_fwd(x0, g1, wgu1, wd1, "ffn1_fwd", exchange=_ex_gather_direct([own["w_in"]]))
    sems_l, thru_l, token_l = _gather_start([own[k_] for k_ in late], [h1], "gather_late_start")
    u, xp, q, k, v, gp, gs = _mix_in(h1, gm, win, after=(token_l,))
    o_sb, ctot = _attn_fwd(q, k, v)
    arrived = _gather_wait(sems_l, thru_l, [o_sb], "gather_late_wait")
    wbp, wba, wout = _exchange_alone(_ex_relay(arrived[:3]), "relay_mix")
    wout = wout.reshape(D, D)
    (h2, pm, p, yp, ys, mm), (wgu2, wd2) = _mix_out(h1, xp, o_sb, gp, gs, wgrp, pool_scale, wbp, wba, wout,
                                                    exchange=_ex_relay(arrived[3:]))
    wd2 = wd2.reshape(DFF, D)
    h3, n3, gu3, a3 = _ffn_fwd(h2, g2, wgu2, wd2, "ffn2_fwd")
    dh3, loss_row, d_gf = _head(h3, tgt, gf)

    def grad_gate_up(n, dgu, name, exchange=None):
        res = _wgrad(n, dgu, NSH, 512, name, exchange=exchange)
        return [_halves(res)] if exchange is None else ([_halves(res[0])], res[1])

    def grad_down(a, dh, name, exchange=None):
        res = _wgrad(a, dh, 1, FFS, name, exchange=exchange)
        halves = lambda g: [_halves(g.reshape(NSH, DFF // NSH, D))]
        return halves(res) if exchange is None else (halves(res[0]), res[1])

    k_gu2, k_d2, k_gu1, k_d1, k_in = (("ffn2_w_gate_up",), ("ffn2_w_down",), ("ffn1_w_gate_up",),
                                      ("ffn1_w_down",), ("w_in",))
    dgu3 = _ffn_bwd_act(dh3, gu3, wd2, "ffn2_bwd_act")
    pa = grad_gate_up(n3, dgu3, "wgrad_gu2") + grad_down(a3, dh3, "wgrad_d2")
    (dh2, d_g2), got_a = _ffn_bwd_in(dh3, h2, g2, dgu3, wgu2, "ffn2_bwd_in", exchange=_ex_pair_swap(pa))
    chip_a = pair_sums(k_gu2 + k_d2, pa, got_a)
    dlg, dyp, dys, do_sb, dyg, dxp, d_scale = _mix_bwd_out(dh2, gp, gs, yp, ys, pm, wgrp, pool_scale, wbp, wba, wout)
    kb = ("w_out", "w_branch_pool", "w_branch_attn")
    pb = [_halves(_wgrad(mm, dh2, 1, 512, "wgrad_out").reshape(NSH, D // NSH, D)),
          _halves(_wgrad(p, dyp, NSH, PW, "wgrad_bp")), _halves(_wgrad(o_sb, dys, NSH, SBW, "wgrad_ba"))]
    chip_b = pair_sums(kb, pb, _exchange_alone(_ex_pair_swap(pb), "pair_swap_mix"))
    k_ab = k_gu2 + k_d2 + kb
    (dq, dk, dv), owned_ab = _attn_bwd(q, k, v, do_sb, ctot, exchange=_ex_scatter(chip_a + chip_b))
    halves_ab = chip_sums(k_ab, chip_a + chip_b, owned_ab)
    (dh1, d_gm, dproj), both_ab = _mix_bwd_in(dh2, h1, gm, (dxp, dq, dk, dv, dlg), win, exchange=_ex_share(halves_ab))
    for i, k_ in enumerate(k_ab):
        grad[k_] = both_ab[i].reshape(wts[k_].shape)

    p_in = [_halves(_wgrad(u, dproj, NSH, 512, "wgrad_in"))]
    p_d1, got_in = grad_down(a1, dh1, "wgrad_d1", exchange=_ex_pair_swap(p_in))
    sems_in, thru_in, token_in = _scatter_start(pair_sums(k_in, p_in, got_in), "scatter_in_start")
    dgu1, got_d1 = _ffn_bwd_act(dh1, gu1, wd1, "ffn1_bwd_act", exchange=_ex_pair_swap(p_d1), after=(token_in,))
    sems_d1, thru_d1, token_d1 = _scatter_start(pair_sums(k_d1, p_d1, got_d1), "scatter_d1_start")
    p_gu1 = [_halves(_wgrad(n1, dgu1, NSH, 512, "wgrad_gu1", after=(token_in, token_d1)))]
    chip_in, owned_in = _scatter_wait(sems_in, thru_in, p_gu1, "scatter_in_wait")
    chip_d1, owned_d1 = _scatter_wait(sems_d1, thru_d1, p_gu1, "scatter_d1_wait")
    halves_in, halves_d1 = chip_sums(k_in, chip_in, owned_in), chip_sums(k_d1, chip_d1, owned_d1)
    landed = _exchange_alone(_join(_ex_pair_swap(p_gu1), _ex_share(halves_in)), "pair_swap_gu1")
    grad["w_in"] = landed[1].reshape(w_in.shape)
    sems, thru, token = _scatter_start(pair_sums(k_gu1, p_gu1, landed[:1]), "scatter_gu1_start")
    for k_ in k_ab + k_in:
        adamw(k_, after=(token,))
    dx, d_g1 = _ffn_bwd_in(dh1, x0, g1, dgu1, wgu1, "ffn1_bwd_in", after=(token,))
    chip_gu1, owned_gu1 = _scatter_wait(sems, thru, [dx] + [delta[k_] for k_ in k_ab + k_in], "scatter_gu1_wait")

    small_g = dict(ffn1_norm=d_g1, mix_norm=d_gm, ffn2_norm=d_g2, final_norm=d_gf, pool_scale=d_scale,
                   pool_w_group=_wgrad_groups(pm, dyg), loss=loss_row)
    gathered, both = _gather_small(_pack_small(small_g), _ex_share(halves_d1 + chip_sums(k_gu1, chip_gu1, owned_gu1)))
    grad["ffn1_w_down"] = both[0].reshape(ffn1_w_down.shape)
    grad["ffn1_w_gate_up"] = both[1].reshape(ffn1_w_gate_up.shape)
    for k_ in k_d1 + k_gu1:
        adamw(k_)
    sg, sd, sm, sv = _small_update(gathered, _pack_small(wts), _pack_small(mom), _pack_small(var))
    sums = _unpack_small(sg, wts)
    loss = sums.pop("loss")
    grad.update(sums)
    for dst, packed in ((delta, sd), (new_m, sm), (new_v, sv)):
        vals = _unpack_small(packed, wts)
        vals.pop("loss")
        dst.update(vals)
    return (loss, dx[None], *[grad[k_] for k_ in ORDER], *[delta[k_] for k_ in ORDER],
            *[new_m[k_] for k_ in ORDER], *[new_v[k_] for k_ in ORDER])
```

```python
import functools

import jax
import jax.numpy as jnp
from jax import lax
from jax.experimental import pallas as pl
from jax.experimental.pallas import tpu as pltpu

F32 = jnp.float32
BF16 = jnp.bfloat16

S = 2048
D = 1024
DFF = 2816
FFS = 2 * DFF // 4
NSH = 4
PW = 512
PG = 128
POOL_WINDOWS = (2, 4, 8, 16)
HALO = 16
SBW = 512
DH = 64
EPS = 1e-6
SCALE = 0.125
LOG2E = 1.4426950408889634
TA = 256
QB = 2
MIB = 1024 * 1024

LR, B1, B2, AEPS, WD, STEP = 0.001, 0.9, 0.999, 1e-08, 0.01, 10

_VM = pl.BlockSpec(memory_space=pltpu.VMEM)
_ANY = pl.BlockSpec(memory_space=pl.ANY)
MESH = pl.DeviceIdType.MESH


def _nn(a, b):
    return jnp.dot(a, b, preferred_element_type=F32)


def _nt(a, b):
    return lax.dot_general(a, b, (((1,), (1,)), ((), ())), preferred_element_type=F32)


def _tn(a, b):
    return lax.dot_general(a, b, (((0,), (0,)), ((), ())), preferred_element_type=F32)


def _params(sem, vmem_mib):
    return pltpu.CompilerParams(dimension_semantics=sem, vmem_limit_bytes=vmem_mib * MIB)


def _rows(tm, width):
    return pl.BlockSpec((tm, width), lambda i: (i, 0))


def _fixed(shape):
    return pl.BlockSpec(shape, lambda *_: (0,) * len(shape))


def _sds(shape, dtype):
    return pltpu.HBM(shape, dtype)


def _in_hbm(args):
    return [pltpu.with_memory_space_constraint(a, pltpu.HBM) for a in args]


def _stage(pairs):
    @pl.when(pl.program_id(0) == 0)
    def _():
        for src, dst in pairs:
            pltpu.sync_copy(src, dst)


def _vmem_like(*arrays):
    return [pltpu.VMEM(a.shape, a.dtype) for a in arrays]


class Exchange:
    def __init__(self, arrays, landing, aliases, n_sems, start, finish):
        self.arrays, self.landing, self.aliases, self.n_sems = list(arrays), list(landing), dict(aliases), n_sems
        self.start, self.finish = start, finish


def _join(a, b):
    na, la = len(a.arrays), len(a.landing)

    def both(fa, fb):
        def run(ins, outs, ssem, rsem):
            fa(ins[:na], outs[:la], ssem.at[pl.ds(0, a.n_sems)], rsem.at[pl.ds(0, a.n_sems)])
            fb(ins[na:], outs[la:], ssem.at[pl.ds(a.n_sems, b.n_sems)], rsem.at[pl.ds(a.n_sems, b.n_sems)])
        return run

    aliases = {**a.aliases, **{na + i: la + j for i, j in b.aliases.items()}}
    return Exchange(a.arrays + b.arrays, a.landing + b.landing, aliases, a.n_sems + b.n_sems,
                    both(a.start, b.start), both(a.finish, b.finish))


def _call(body, args, *, name, grid, in_specs, out_specs, out_shape, scratch_shapes=(), compiler_params=None,
          exchange=None, free=(), after=()):
    args = [a if i in free else pltpu.with_memory_space_constraint(a, pltpu.HBM) for i, a in enumerate(args)]
    if exchange is None:
        n_in = len(in_specs)

        def plain(*refs):
            body(*refs[:n_in], *refs[n_in + len(after):])

        return pl.pallas_call(plain, name=name, grid=grid, in_specs=list(in_specs) + [_ANY] * len(after),
                              out_specs=out_specs, out_shape=out_shape, scratch_shapes=list(scratch_shapes),
                              compiler_params=compiler_params)(*args, *after)
    ex = exchange
    n_in, n_out, n_scr = len(in_specs), len(out_specs), len(scratch_shapes)
    na, nl = len(ex.arrays), len(ex.landing)

    def hosted(*refs):
        at = [0]

        def take(n):
            at[0] += n
            return refs[at[0] - n:at[0]]

        k_in, _, e_in, k_out, e_out, k_scr = take(n_in), take(len(after)), take(na), take(n_out), take(nl), take(n_scr)
        ssem, rsem = take(2)
        ids = [pl.program_id(a) for a in range(len(grid))]
        first = functools.reduce(jnp.logical_and, [i == 0 for i in ids])
        last = functools.reduce(jnp.logical_and, [i == g - 1 for i, g in zip(ids, grid)])

        @pl.when(first)
        def _():
            ex.start(e_in, e_out, ssem, rsem)

        body(*k_in, *k_out, *k_scr)

        @pl.when(last)
        def _():
            ex.finish(e_in, e_out, ssem, rsem)

    outs = pl.pallas_call(
        hosted, name=name, grid=grid,
        in_specs=list(in_specs) + [_ANY] * (len(after) + na), out_specs=list(out_specs) + [_ANY] * nl,
        out_shape=list(out_shape) + ex.landing,
        scratch_shapes=list(scratch_shapes) + [pltpu.SemaphoreType.DMA((ex.n_sems,))] * 2,
        input_output_aliases={n_in + len(after) + i: n_out + j for i, j in ex.aliases.items()},
        compiler_params=compiler_params,
    )(*args, *after, *_in_hbm(ex.arrays))
    return outs[:n_out], outs[n_out:]


def _exchange_alone(ex, name):
    def body(*refs):
        na, nl = len(ex.arrays), len(ex.landing)
        ex.start(refs[:na], refs[na:na + nl], refs[-2], refs[-1])
        ex.finish(refs[:na], refs[na:na + nl], refs[-2], refs[-1])

    return pl.pallas_call(
        body, name=name, in_specs=[_ANY] * len(ex.arrays), out_specs=[_ANY] * len(ex.landing),
        out_shape=ex.landing, scratch_shapes=[pltpu.SemaphoreType.DMA((ex.n_sems,))] * 2,
        input_output_aliases=ex.aliases,
    )(*_in_hbm(ex.arrays))


_HBM = pl.BlockSpec(memory_space=pltpu.HBM)
_SEM = pl.BlockSpec(memory_space=pltpu.SEMAPHORE)
_EFFECT = pltpu.SideEffectType.DATAFLOW_SIDE_EFFECTING


def _scatter_copies(srcs, lands, ssems, rsems):
    x, y, c, chips = _place()
    return [_remote(srcs[w].at[2 * px + py], lands[w].at[k], ssems[3 * w + k], rsems[3 * w + k], (px, py, c))
            for w in range(len(srcs)) for k, (px, py) in enumerate(chips)]


def _scatter_start(parts, name):
    n, ncp = len(parts), 3 * len(parts)
    lands = [lax.empty((3,) + p.shape[1:], p.dtype) for p in parts]

    def body(*refs):
        srcs, land_refs = refs[:n], refs[n:2 * n]
        ssems, rsems = refs[2 * n:2 * n + ncp], refs[2 * n + ncp:2 * n + 2 * ncp]
        for cp in _scatter_copies(srcs, land_refs, ssems, rsems):
            cp.start()
        token = refs[-1]
        token[...] = jnp.zeros_like(token)

    outs = pl.pallas_call(
        body, name=name,
        out_shape=([pltpu.SemaphoreType.DMA(())] * (2 * ncp) + [pltpu.HBM(a.shape, a.dtype) for a in parts + lands]
                   + [jax.ShapeDtypeStruct((8, 128), F32)]),
        in_specs=[_HBM] * (2 * n), out_specs=[_SEM] * (2 * ncp) + [_HBM] * (2 * n) + [_VM],
        input_output_aliases={i: 2 * ncp + i for i in range(2 * n)},
        compiler_params=pltpu.CompilerParams(has_side_effects=_EFFECT),
    )(*_in_hbm(parts), *_in_hbm(lands))
    sems, thru, token = outs[:2 * ncp], outs[2 * ncp:2 * ncp + 2 * n], outs[-1]
    return sems, thru, token


def _scatter_wait(sems, thru, after, name):
    n = len(thru) // 2
    ncp = 3 * n

    def body(*refs):
        srcs, land_refs = refs[:n], refs[n:2 * n]
        ssems, rsems = refs[2 * n:2 * n + ncp], refs[2 * n + ncp:2 * n + 2 * ncp]
        for cp in _scatter_copies(srcs, land_refs, ssems, rsems):
            cp.wait_send()
            cp.wait_recv()

    outs = pl.pallas_call(
        body, name=name, out_shape=[pltpu.HBM(a.shape, a.dtype) for a in thru],
        in_specs=[_HBM] * (2 * n) + [_SEM] * (2 * ncp) + [_ANY] * len(after), out_specs=[_HBM] * (2 * n),
        input_output_aliases={i: i for i in range(2 * n)},
        compiler_params=pltpu.CompilerParams(has_side_effects=_EFFECT),
    )(*thru, *sems, *after)
    return outs[:n], outs[n:]


def _gather_copies(bufs, ssems, rsems, sending):
    x, y, c, chips = _place()
    out = []
    for w, ref in enumerate(bufs):
        half = ref.shape[1] // 2
        for k, (px, py) in enumerate(chips):
            rows = ref.at[2 * x + y if sending else 2 * px + py, pl.ds(c * half, half)]
            out.append(_remote(rows, rows, ssems[3 * w + k], rsems[3 * w + k], (px, py, c)))
    return out


def _gather_start(bufs, after, name):
    n, ncp = len(bufs), 3 * len(bufs)

    def body(*refs):
        ssems, rsems = refs[n + len(after):n + len(after) + ncp], refs[n + len(after) + ncp:n + len(after) + 2 * ncp]
        for cp in _gather_copies(refs[:n], ssems, rsems, True):
            cp.start()
        token = refs[-1]
        token[...] = jnp.zeros_like(token)

    outs = pl.pallas_call(
        body, name=name,
        out_shape=([pltpu.SemaphoreType.DMA(())] * (2 * ncp) + [pltpu.HBM(a.shape, a.dtype) for a in bufs]
                   + [jax.ShapeDtypeStruct((8, 128), F32)]),
        in_specs=[_HBM] * n + [_ANY] * len(after), out_specs=[_SEM] * (2 * ncp) + [_HBM] * n + [_VM],
        input_output_aliases={i: 2 * ncp + i for i in range(n)},
        compiler_params=pltpu.CompilerParams(has_side_effects=_EFFECT),
    )(*_in_hbm(bufs), *after)
    return outs[:2 * ncp], outs[2 * ncp:2 * ncp + n], outs[-1]


def _gather_wait(sems, thru, after, name):
    n = len(thru)
    ncp = 3 * n

    def body(*refs):
        ssems, rsems = refs[n:n + ncp], refs[n + ncp:n + 2 * ncp]
        for cp in _gather_copies(refs[:n], ssems, rsems, True):
            cp.wait_send()
        for cp in _gather_copies(refs[:n], ssems, rsems, False):
            cp.wait_recv()

    return pl.pallas_call(
        body, name=name, out_shape=[pltpu.HBM(a.shape, a.dtype) for a in thru],
        in_specs=[_HBM] * n + [_SEM] * (2 * ncp) + [_ANY] * len(after), out_specs=[_HBM] * n,
        input_output_aliases={i: i for i in range(n)},
        compiler_params=pltpu.CompilerParams(has_side_effects=_EFFECT),
    )(*thru, *sems, *after)


def _rms(x):
    r = lax.rsqrt(jnp.mean(x * x, axis=-1, keepdims=True) + EPS)
    return r, x * r


def _rms_bwd(dn, xr, r, gain):
    dng = dn * gain
    dx = r * (dng - xr * jnp.mean(dng * xr, axis=-1, keepdims=True))
    return dx, jnp.sum(dn * xr, axis=0, keepdims=True)


def _ffn_fwd(x, gain, wgu, wd, name, exchange=None):
    tm = 256

    def body(x_ref, g_ref, wgu_hbm, wd_hbm, h_ref, n_ref, gu_ref, a_ref, wgu_ref, wd_ref):
        _stage([(wgu_hbm, wgu_ref), (wd_hbm, wd_ref)])
        x = x_ref[...]
        _, xr = _rms(x)
        n = (xr * g_ref[...]).astype(BF16)
        n_ref[...] = n
        acc = jnp.zeros((tm, D), F32)
        for j in range(2):
            g = _nn(n, wgu_ref[j])
            u = _nn(n, wgu_ref[2 + j])
            gu_ref[:, j * FFS:(j + 1) * FFS] = g.astype(BF16)
            gu_ref[:, (2 + j) * FFS:(3 + j) * FFS] = u.astype(BF16)
            half_act = (0.5 * (g * jax.nn.sigmoid(g) * u)).astype(BF16)
            a_ref[:, j * FFS:(j + 1) * FFS] = half_act
            acc = acc + _nn(half_act, wd_ref[j * FFS:(j + 1) * FFS, :])
        h_ref[...] = x + acc

    return _call(
        body, (x, gain, wgu, wd), name=name, grid=(S // tm,),
        in_specs=[_rows(tm, D), _fixed((1, D)), _ANY, _ANY],
        out_specs=[_rows(tm, D), _rows(tm, D), _rows(tm, 4 * FFS), _rows(tm, DFF)],
        out_shape=[_sds((S, D), F32), _sds((S, D), BF16), _sds((S, 4 * FFS), BF16), _sds((S, DFF), BF16)],
        scratch_shapes=_vmem_like(wgu, wd),
        compiler_params=_params(("arbitrary",), 56), exchange=exchange)


def _ffn_bwd_act(dh, gu, wd, name, exchange=None, after=()):
    tm = 512

    def body(dh_ref, gu_ref, wd_hbm, dgu_ref, wd_ref):
        _stage([(wd_hbm, wd_ref)])
        dhb = dh_ref[...].astype(BF16)
        for j in range(2):
            g = gu_ref[:, j * FFS:(j + 1) * FFS].astype(F32)
            u = gu_ref[:, (2 + j) * FFS:(3 + j) * FFS].astype(F32)
            da = 0.5 * _nt(dhb, wd_ref[j * FFS:(j + 1) * FFS, :])
            sg = jax.nn.sigmoid(g)
            dgu_ref[:, j * FFS:(j + 1) * FFS] = (da * u * (sg * (1.0 + g * (1.0 - sg)))).astype(BF16)
            dgu_ref[:, (2 + j) * FFS:(3 + j) * FFS] = (da * (g * sg)).astype(BF16)

    res = _call(
        body, (dh, gu, wd), name=name, grid=(S // tm,),
        in_specs=[_rows(tm, D), _rows(tm, 4 * FFS), _ANY], out_specs=[_rows(tm, 4 * FFS)],
        out_shape=[_sds((S, 4 * FFS), BF16)], scratch_shapes=_vmem_like(wd),
        compiler_params=_params(("arbitrary",), 56), exchange=exchange, after=after)
    return res[0] if exchange is None else (res[0][0], res[1])


def _ffn_bwd_in(dh, x, gain, dgu, wgu, name, exchange=None, after=()):
    tm = 512

    def body(dh_ref, x_ref, g_ref, dgu_ref, wgu_hbm, dx_ref, dg_ref, wgu_ref):
        _stage([(wgu_hbm, wgu_ref)])
        dn = jnp.zeros((tm, D), F32)
        for j in range(NSH):
            dn = dn + _nt(dgu_ref[:, j * FFS:(j + 1) * FFS], wgu_ref[j])
        r, xr = _rms(x_ref[...])
        dx, dgain = _rms_bwd(dn, xr, r, g_ref[...])
        dx_ref[...] = dh_ref[...] + dx

        @pl.when(pl.program_id(0) == 0)
        def _():
            dg_ref[...] = jnp.zeros_like(dg_ref)

        dg_ref[...] += dgain

    return _call(
        body, (dh, x, gain, dgu, wgu), name=name, grid=(S // tm,),
        in_specs=[_rows(tm, D), _rows(tm, D), _fixed((1, D)), _rows(tm, 4 * FFS), _ANY],
        out_specs=[_rows(tm, D), _fixed((1, D))],
        out_shape=[_sds((S, D), F32), _sds((1, D), F32)],
        scratch_shapes=_vmem_like(wgu),
        compiler_params=_params(("arbitrary",), 56), exchange=exchange, after=after)


def _head(h, target, gain):
    tm = 512

    def body(h_ref, t_ref, g_ref, dh_ref, loss_ref, dg_ref):
        gain = g_ref[...]
        r, hr = _rms(h_ref[...])
        err = hr * gain - t_ref[...]
        dy = err * (1.0 / D)
        dh, dgain = _rms_bwd(dy, hr, r, gain)
        dh_ref[...] = dh

        @pl.when(pl.program_id(0) == 0)
        def _():
            dg_ref[...] = jnp.zeros_like(dg_ref)
            loss_ref[...] = jnp.zeros_like(loss_ref)

        dg_ref[...] += dgain
        loss_ref[...] += jnp.full((1, 128), (0.5 / D) * jnp.sum(err * err), F32)

    return pl.pallas_call(
        body, name="head", grid=(S // tm,),
        in_specs=[_rows(tm, D), _rows(tm, D), _fixed((1, D))],
        out_specs=[_rows(tm, D), _fixed((1, 128)), _fixed((1, D))],
        out_shape=[_sds((S, D), F32), _sds((1, 128), F32), _sds((1, D), F32)],
        compiler_params=_params(("arbitrary",), 40),
    )(*_in_hbm([h]), target, gain)


def _mix_in(h, gain, w_in, after=()):
    tm = 512

    def body(h_ref, g_ref, w_hbm, u_ref, xp_ref, q_ref, k_ref, v_ref, gp_ref, gs_ref, w_ref):
        _stage([(w_hbm, w_ref)])
        _, hr = _rms(h_ref[...])
        u = (hr * g_ref[...]).astype(BF16)
        u_ref[...] = u
        p0 = _nn(u, w_ref[0])
        xp_ref[...] = p0[:, :PW]
        q_ref[...] = p0[:, PW:].astype(BF16)
        p1 = _nn(u, w_ref[1])
        k_ref[...] = p1[:, :SBW].astype(BF16)
        v_ref[...] = p1[:, SBW:].astype(BF16)
        gp_ref[...] = jax.nn.sigmoid(_nn(u, w_ref[2])).astype(BF16)
        gs_ref[...] = jax.nn.sigmoid(_nn(u, w_ref[3])).astype(BF16)

    return _call(
        body, (h, gain, w_in), name="mix_in", grid=(S // tm,),
        in_specs=[_rows(tm, D), _fixed((1, D)), _ANY],
        out_specs=[_rows(tm, D), _rows(tm, PW), _rows(tm, SBW), _rows(tm, SBW), _rows(tm, SBW),
                   _rows(tm, D), _rows(tm, D)],
        out_shape=[_sds((S, D), BF16), _sds((S, PW), F32), _sds((S, SBW), BF16), _sds((S, SBW), BF16),
                   _sds((S, SBW), BF16), _sds((S, D), BF16), _sds((S, D), BF16)],
        scratch_shapes=_vmem_like(w_in),
        compiler_params=_params(("arbitrary",), 48), free=(1,), after=after)


def _hilo_dot(x, tri):
    hi = x.astype(BF16)
    lo = (x - hi.astype(F32)).astype(BF16)
    return _nn(hi, tri) + _nn(lo, tri)


def _log_terms(qk):
    z2 = qk * (SCALE * LOG2E)
    lb = jnp.minimum(z2, 0.0) - jnp.log2(1.0 + jnp.exp2(-jnp.abs(z2)))
    return lb, lb - z2


def _head_masks():
    lane = lax.broadcasted_iota(jnp.int32, (1, 2 * DH), 1)
    return (lane < DH, lane >= DH)


def _attn_fwd(q, k, v, exchange=None):
    T = TA

    def body(q_ref, k_ref, v_ref, o_ref, c_ref):
        i2 = 2 * pl.program_id(1)
        row = lax.broadcasted_iota(jnp.int32, (T, T), 0)
        col = lax.broadcasted_iota(jnp.int32, (T, T), 1)
        after = (row > col).astype(BF16)
        causal = col < row
        masks = _head_masks()
        qms = {}
        for b in range(QB):
            q2 = q_ref[b * T:(b + 1) * T, :]
            for h, hm in enumerate(masks):
                qms[b, h] = jnp.where(hm, q2, jnp.zeros_like(q2))

        def blocks(keys, pairs, carries, os):
            ks, vms = [], []
            for j in keys:
                rows = pl.ds(pl.multiple_of(j * T, T), T)
                vj = v_ref[rows, :]
                ks.append(k_ref[rows, :])
                vms.append([jnp.where(hm, vj, jnp.zeros_like(vj)) for hm in masks])
            units = [(n, h) for n in range(len(pairs)) for h in range(2)]
            qks = {(n, h): _nt(qms[pairs[n][0], h], ks[pairs[n][1]]) for n, h in units}
            lbs, l1ms = {}, {}
            for u in units:
                lbs[u], l1m = _log_terms(qks[u])
                l1ms[u] = jnp.where(causal, l1m, 0.0) if pairs[u[0]][2] else l1m
            cins = {u: _hilo_dot(l1ms[u], after) for u in units}
            carries, os = dict(carries), list(os)
            for n, h in units:
                b, key, diag = pairs[n]
                a = jnp.exp2(lbs[n, h] + cins[n, h] + carries[b, h])
                if diag:
                    a = jnp.where(causal, a, 0.0)
                os[b] = os[b] + _nn(a.astype(BF16), vms[key][h])
                carries[b, h] = carries[b, h] + jnp.sum(l1ms[n, h], axis=1, keepdims=True)
            return carries, tuple(os)

        carries = {(b, h): jnp.zeros((T, 1), F32) for b in range(QB) for h in range(2)}
        os = tuple(jnp.zeros((T, 2 * DH), F32) for _ in range(QB))
        carries, os = blocks([i2 + 1, i2], [(1, 0, True), (0, 1, True), (1, 1, False)], carries, os)
        carries, os = lax.fori_loop(
            0, i2, lambda jj, c: blocks([i2 - 1 - jj], [(0, 0, False), (1, 0, False)], c[0], c[1]), (carries, os))
        for b in range(QB):
            o_ref[b * T:(b + 1) * T, :] = os[b].astype(BF16)
            c_ref[b * T:(b + 1) * T, :] = jnp.where(masks[0], carries[b, 0], carries[b, 1])

    blk = pl.BlockSpec((QB * T, 2 * DH), lambda p, i: (i, p))
    full = pl.BlockSpec((S, 2 * DH), lambda p, i: (0, p))
    return _call(
        body, (q, k, v), name="attn_fwd", grid=(SBW // (2 * DH), S // (QB * T)),
        in_specs=[blk, full, full], out_specs=[blk, blk],
        out_shape=[_sds((S, SBW), BF16), _sds((S, SBW), F32)],
        compiler_params=_params(("arbitrary", "arbitrary"), 40), exchange=exchange)


def _attn_bwd(q, k, v, do, ctot, after=()):
    T = TA
    nq = S // (QB * T)

    def body(q_ref, k_ref, v_ref, do_ref, c_ref, dq_ref, dk_ref, dv_ref, dk_acc, dv_acc):
        step = pl.program_id(1)
        i2 = 2 * step

        @pl.when(step == 0)
        def _():
            dk_acc[...] = jnp.zeros_like(dk_acc)
            dv_acc[...] = jnp.zeros_like(dv_acc)

        row = lax.broadcasted_iota(jnp.int32, (T, T), 0)
        col = lax.broadcasted_iota(jnp.int32, (T, T), 1)
        upto = (row <= col).astype(BF16)
        before = (row < col).astype(BF16)
        causal = col < row
        masks = _head_masks()
        qms, doms, ctots = {}, {}, {}
        for b in range(QB):
            q2, do2 = q_ref[b * T:(b + 1) * T, :], do_ref[b * T:(b + 1) * T, :]
            for h, hm in enumerate(masks):
                qms[b, h] = jnp.where(hm, q2, jnp.zeros_like(q2))
                doms[b, h] = jnp.where(hm, do2, jnp.zeros_like(do2))
                ctots[b, h] = c_ref[b * T:(b + 1) * T, h * DH:h * DH + 1]

        def blocks(keys, pairs, sums, dqs):
            rows = [pl.ds(pl.multiple_of(j * T, T), T) for j in keys]
            ks, vs = [k_ref[r, :] for r in rows], [v_ref[r, :] for r in rows]
            kms = [[jnp.where(hm, kj, jnp.zeros_like(kj)) for hm in masks] for kj in ks]
            units = [(n, h) for n in range(len(pairs)) for h in range(2)]
            qks = {(n, h): _nt(qms[pairs[n][0], h], ks[pairs[n][1]]) for n, h in units}
            das = {(n, h): _nt(doms[pairs[n][0], h], vs[pairs[n][1]]) for n, h in units}
            lbs, l1ms = {}, {}
            for u in units:
                lbs[u], l1m = _log_terms(qks[u])
                l1ms[u] = jnp.where(causal, l1m, 0.0) if pairs[u[0]][2] else l1m
            pins = {u: _hilo_dot(l1ms[u], upto) for u in units}
            sums = dict(sums)
            a_s, dls, cps = {}, {}, {}
            for n, h in units:
                b, _, diag = pairs[n]
                cl, cp = sums[b, h]
                a = jnp.exp2(lbs[n, h] + (ctots[b, h] - cl) - pins[n, h])
                if diag:
                    a = jnp.where(causal, a, 0.0)
                a_s[n, h] = a.astype(BF16)
                dls[n, h] = das[n, h] * a
                cps[n, h] = cp
                sums[b, h] = (cl + jnp.sum(l1ms[n, h], axis=1, keepdims=True),
                              cp + jnp.sum(dls[n, h], axis=1, keepdims=True))
            pexs = {u: _hilo_dot(dls[u], before) for u in units}
            dzbs = {}
            for u in units:
                dz = dls[u] - jnp.exp2(lbs[u]) * (dls[u] + pexs[u] + cps[u])
                if pairs[u[0]][2]:
                    dz = jnp.where(causal, dz, 0.0)
                dzbs[u] = dz.astype(BF16)
            dqs = list(dqs)
            for n, h in units:
                dqs[pairs[n][0]] = dqs[pairs[n][0]] + _nn(dzbs[n, h], kms[pairs[n][1]][h])
            for key, r in enumerate(rows):
                mine = [(n, h) for n, h in units if pairs[n][1] == key]
                dk_acc[r, :] += functools.reduce(jnp.add, [_tn(dzbs[u], qms[pairs[u[0]][0], u[1]]) for u in mine])
                dv_acc[r, :] += functools.reduce(jnp.add, [_tn(a_s[u], doms[pairs[u[0]][0], u[1]]) for u in mine])
            return sums, tuple(dqs)

        zero = jnp.zeros((T, 1), F32)
        sums = {(b, h): (zero, zero) for b in range(QB) for h in range(2)}
        dqs = tuple(jnp.zeros((T, 2 * DH), F32) for _ in range(QB))
        sums, dqs = lax.fori_loop(
            0, i2, lambda j, c: blocks([j], [(0, 0, False), (1, 0, False)], c[0], c[1]), (sums, dqs))
        _, dqs = blocks([i2, i2 + 1], [(0, 0, True), (1, 0, False), (1, 1, True)], sums, dqs)
        for b in range(QB):
            dq_ref[b * T:(b + 1) * T, :] = (dqs[b] * SCALE).astype(BF16)

        @pl.when(step == nq - 1)
        def _():
            dk_ref[...] = (dk_acc[...] * SCALE).astype(BF16)
            dv_ref[...] = dv_acc[...].astype(BF16)

    blk = pl.BlockSpec((QB * T, 2 * DH), lambda p, i: (i, p))
    full = pl.BlockSpec((S, 2 * DH), lambda p, i: (0, p))
    return _call(
        body, (q, k, v, do, ctot), name="attn_bwd", grid=(SBW // (2 * DH), nq),
        in_specs=[blk, full, full, blk, blk], out_specs=[blk, full, full],
        out_shape=[_sds((S, SBW), BF16), _sds((S, SBW), BF16), _sds((S, SBW), BF16)],
        scratch_shapes=[pltpu.VMEM((S, 2 * DH), F32), pltpu.VMEM((S, 2 * DH), F32)],
        compiler_params=_params(("arbitrary", "arbitrary"), 40), after=after)


def _pool_counts(first_row, tm):
    pos = first_row + lax.broadcasted_iota(jnp.int32, (tm, 1), 0)
    return [jnp.minimum(pos + 1, w).astype(F32) for w in POOL_WINDOWS]


def _mix_out(h, xp, o_sb, gp, gs, w_group, scale, w_bp, w_ba, w_out, exchange=None):
    tm = 512

    def body(h_ref, xp_ref, o_ref, gp_ref, gs_ref, wg_hbm, sc_ref, wbp_hbm, wba_hbm, wo_hbm,
             h2_ref, pm_ref, p_ref, yp_ref, ys_ref, m_ref, halo, wg_ref, wbp_ref, wba_ref, wo_ref):
        _stage([(wg_hbm, wg_ref), (wbp_hbm, wbp_ref), (wba_hbm, wba_ref), (wo_hbm, wo_ref)])
        i = pl.program_id(0)

        @pl.when(i == 0)
        def _():
            halo[...] = jnp.zeros_like(halo)

        xp = xp_ref[...]
        ext = jnp.concatenate([halo[...], xp], axis=0)
        halo[...] = xp[tm - HALO:, :]
        counts = _pool_counts(i * tm, tm)
        for gi in range(len(POOL_WINDOWS)):
            lanes = slice(gi * PG, (gi + 1) * PG)
            win = ext[:, lanes]
            for step in range(gi + 1):
                win = win + pltpu.roll(win, 1 << step, 0)
            pm = (win[HALO:, :] / counts[gi] - xp[:, lanes]).astype(BF16)
            pm_ref[:, lanes] = pm
            p_ref[:, lanes] = (_nn(pm, wg_ref[gi]) * sc_ref[:, lanes]).astype(BF16)
        pb = p_ref[...]
        ob = o_ref[...]
        for j in range(NSH):
            cols = slice(j * (D // NSH), (j + 1) * (D // NSH))
            yp = _nn(pb, wbp_ref[j])
            ys = _nn(ob, wba_ref[j])
            yp_ref[:, cols] = yp.astype(BF16)
            ys_ref[:, cols] = ys.astype(BF16)
            m_ref[:, cols] = (gp_ref[:, cols].astype(F32) * yp + gs_ref[:, cols].astype(F32) * ys).astype(BF16)
        h2_ref[...] = h_ref[...] + _nn(m_ref[...], wo_ref[...])

    return _call(
        body, (h, xp, o_sb, gp, gs, w_group, scale, w_bp, w_ba, w_out), name="mix_out", grid=(S // tm,),
        in_specs=[_rows(tm, D), _rows(tm, PW), _rows(tm, SBW), _rows(tm, D), _rows(tm, D),
                  _ANY, _fixed((1, PW)), _ANY, _ANY, _ANY],
        out_specs=[_rows(tm, D), _rows(tm, PW), _rows(tm, PW), _rows(tm, D), _rows(tm, D), _rows(tm, D)],
        out_shape=[_sds((S, D), F32), _sds((S, PW), BF16), _sds((S, PW), BF16), _sds((S, D), BF16),
                   _sds((S, D), BF16), _sds((S, D), BF16)],
        scratch_shapes=[pltpu.VMEM((HALO, PW), F32)] + _vmem_like(w_group, w_bp, w_ba, w_out),
        compiler_params=_params(("arbitrary",), 48), free=(5, 6), exchange=exchange)


def _mix_bwd_out(dh, gp, gs, yp, ys, pm, w_group, scale, w_bp, w_ba, w_out, exchange=None):
    tm = 512
    nt = S // tm

    def body(dh_ref, gp_ref, gs_ref, yp_ref, ys_ref, pm_ref, wg_hbm, sc_ref, wbp_hbm, wba_hbm, wo_hbm,
             dlg_ref, dyp_ref, dys_ref, do_ref, dyg_ref, dxp_ref, dsc_ref, halo, wg_ref, wbp_ref, wba_ref, wo_ref):
        _stage([(wg_hbm, wg_ref), (wbp_hbm, wbp_ref), (wba_hbm, wba_ref), (wo_hbm, wo_ref)])
        step = pl.program_id(0)

        @pl.when(step == 0)
        def _():
            halo[...] = jnp.zeros_like(halo)
            dsc_ref[...] = jnp.zeros_like(dsc_ref)

        dm = _nt(dh_ref[...].astype(BF16), wo_ref[...])
        gp = gp_ref[...].astype(F32)
        gs = gs_ref[...].astype(F32)
        yp = yp_ref[...].astype(F32)
        ys = ys_ref[...].astype(F32)
        dlg_ref[:, :D] = (dm * yp * gp * (1.0 - gp)).astype(BF16)
        dlg_ref[:, D:] = (dm * ys * gs * (1.0 - gs)).astype(BF16)
        dyp_ref[...] = (dm * gp).astype(BF16)
        dys_ref[...] = (dm * gs).astype(BF16)
        dp = jnp.zeros((tm, PW), F32)
        do = jnp.zeros((tm, SBW), F32)
        for j in range(NSH):
            cols = slice(j * (D // NSH), (j + 1) * (D // NSH))
            dp = dp + _nt(dyp_ref[:, cols], wbp_ref[j])
            do = do + _nt(dys_ref[:, cols], wba_ref[j])
        do_ref[...] = do.astype(BF16)
        counts = _pool_counts((nt - 1 - step) * tm, tm)
        dscale = []
        for gi in range(len(POOL_WINDOWS)):
            lanes = slice(gi * PG, (gi + 1) * PG)
            dpg = dp[:, lanes]
            dscale.append(jnp.sum(dpg * _nn(pm_ref[:, lanes], wg_ref[gi]), axis=0, keepdims=True))
            dyg = (dpg * sc_ref[:, lanes]).astype(BF16)
            dyg_ref[:, lanes] = dyg
            dpm = _nt(dyg, wg_ref[gi])
            per = dpm / counts[gi]
            win = jnp.concatenate([per, halo[:, lanes]], axis=0)
            halo[:, lanes] = per[:HALO, :]
            for s in range(gi + 1):
                win = win + pltpu.roll(win, tm + HALO - (1 << s), 0)
            dxp_ref[:, lanes] = (win[:tm, :] - dpm).astype(BF16)
        dsc_ref[...] += jnp.concatenate(dscale, axis=1)

    rev = lambda width: pl.BlockSpec((tm, width), lambda i: (nt - 1 - i, 0))
    return _call(
        body, (dh, gp, gs, yp, ys, pm, w_group, scale, w_bp, w_ba, w_out), name="mix_bwd_out", grid=(nt,),
        in_specs=[rev(D), rev(D), rev(D), rev(D), rev(D), rev(PW), _ANY, _fixed((1, PW)), _ANY, _ANY, _ANY],
        out_specs=[rev(2 * D), rev(D), rev(D), rev(SBW), rev(PW), rev(PW), _fixed((1, PW))],
        out_shape=[_sds((S, 2 * D), BF16), _sds((S, D), BF16), _sds((S, D), BF16), _sds((S, SBW), BF16),
                   _sds((S, PW), BF16), _sds((S, PW), BF16), _sds((1, PW), F32)],
        scratch_shapes=[pltpu.VMEM((HALO, PW), F32)] + _vmem_like(w_group, w_bp, w_ba, w_out),
        compiler_params=_params(("arbitrary",), 48), exchange=exchange)


def _mix_bwd_in(dh, h, gain, pieces, w_in, exchange=None):
    tm = 512
    widths = [p.shape[1] for p in pieces]

    def body(dh_ref, h_ref, g_ref, *rest):
        piece_refs, (w_hbm, dx_ref, dg_ref, dp_ref, w_ref) = rest[:len(pieces)], rest[len(pieces):]
        _stage([(w_hbm, w_ref)])
        at = 0
        for ref, width in zip(piece_refs, widths):
            dp_ref[:, at:at + width] = ref[...]
            at += width
        du = jnp.zeros((tm, D), F32)
        for j in range(NSH):
            du = du + _nt(dp_ref[:, j * D:(j + 1) * D], w_ref[j])
        r, hr = _rms(h_ref[...])
        dx, dgain = _rms_bwd(du, hr, r, g_ref[...])
        dx_ref[...] = dh_ref[...] + dx

        @pl.when(pl.program_id(0) == 0)
        def _():
            dg_ref[...] = jnp.zeros_like(dg_ref)

        dg_ref[...] += dgain

    return _call(
        body, (dh, h, gain, *pieces, w_in), name="mix_bwd_in", grid=(S // tm,),
        in_specs=[_rows(tm, D), _rows(tm, D), _fixed((1, D))] + [_rows(tm, w) for w in widths] + [_ANY],
        out_specs=[_rows(tm, D), _fixed((1, D)), _rows(tm, 4 * D)],
        out_shape=[_sds((S, D), F32), _sds((1, D), F32), _sds((S, 4 * D), BF16)],
        scratch_shapes=_vmem_like(w_in),
        compiler_params=_params(("arbitrary",), 48), exchange=exchange)


def _wgrad(a, b, nblk, ti, name, out_dtype=BF16, exchange=None, after=()):
    ka, n = a.shape[1], b.shape[1]
    ns = n // nblk

    def body(a_ref, b_ref, o_ref):
        o_ref[...] = _tn(a_ref[...].astype(BF16), b_ref[...].astype(BF16)).astype(out_dtype)

    res = _call(
        body, (a, b), name=name, grid=(nblk, ka // ti),
        in_specs=[pl.BlockSpec((S, ti), lambda j, i: (0, i)), pl.BlockSpec((S, ns), lambda j, i: (0, j))],
        out_specs=[pl.BlockSpec((None, ti, ns), lambda j, i: (j, i, 0))],
        out_shape=[_sds((nblk, ka, ns), out_dtype)],
        compiler_params=_params(("arbitrary", "arbitrary"), 56), exchange=exchange, after=after)
    return res[0] if exchange is None else (res[0][0], res[1])


def _wgrad_groups(pm, dyg):
    def body(a_ref, b_ref, o_ref):
        o_ref[...] = _tn(a_ref[...], b_ref[...])

    col = pl.BlockSpec((S, PG), lambda g: (0, g))
    return pl.pallas_call(
        body, name="wgrad_groups", grid=(PW // PG,),
        in_specs=[col, col], out_specs=pl.BlockSpec((None, PG, PG), lambda g: (g, 0, 0)),
        out_shape=_sds((PW // PG, PG, PG), F32),
        compiler_params=_params(("arbitrary",), 32),
    )(*_in_hbm([pm, dyg]))


def _place():
    x, y, c = lax.axis_index("x"), lax.axis_index("y"), lax.axis_index("c")
    chips = [(1 - x, y), (x, 1 - y), (1 - x, 1 - y)]
    return x, y, c, chips


def _remote(src, dst, ssem, rsem, dev):
    return pltpu.make_async_remote_copy(src_ref=src, dst_ref=dst, send_sem=ssem, recv_sem=rsem,
                                        device_id=dev, device_id_type=MESH)


def _cast_into_block(w, me_idx, name):
    rows, cols = w.shape
    tr = _row_block(rows)

    def body(me_ref, w_ref, o_ref):
        o_ref[...] = w_ref[...].astype(BF16)

    return pl.pallas_call(
        body, name=name, out_shape=_sds((NSH, rows, cols), BF16),
        grid_spec=pltpu.PrefetchScalarGridSpec(
            num_scalar_prefetch=1, grid=(rows // tr,),
            in_specs=[pl.BlockSpec((tr, cols), lambda r, me: (r, 0))],
            out_specs=pl.BlockSpec((None, tr, cols), lambda r, me: (me[0], r, 0))),
        compiler_params=_params(("arbitrary",), 32),
    )(me_idx, w)


def _ex_gather(bufs):
    n = len(bufs)
    per = 8

    def plan(outs, ssem, rsem, w):
        x, y, c, _ = _place()
        sib, nbr_x, nbr_y = (x, y, 1 - c), (1 - x, y, c), (x, 1 - y, c)
        half = outs[w].shape[1] // 2
        quarter = half // 2
        sem = lambda k: (ssem.at[per * w + k], rsem.at[per * w + k])
        rows = lambda blk, start, size: outs[w].at[blk, pl.ds(start, size)]
        mine = rows(2 * x + y, c * half, half)
        from_x = rows(2 * (1 - x) + y, c * half, half)
        from_y = rows(2 * x + (1 - y), c * half, half)
        diag = 2 * (1 - x) + (1 - y)
        pass_y = rows(2 * (1 - x) + y, c * half, quarter)
        pass_x = rows(2 * x + (1 - y), c * half + quarter, quarter)
        diag_0, diag_1 = rows(diag, c * half, quarter), rows(diag, c * half + quarter, quarter)
        first = [_remote(mine, mine, *sem(0), nbr_x), _remote(mine, mine, *sem(1), nbr_y)]
        arrivals = [
            (_remote(from_x, from_x, *sem(0), nbr_x),
             [_remote(pass_y, pass_y, *sem(2), nbr_y), _remote(from_x, from_x, *sem(4), sib)]),
            (_remote(from_y, from_y, *sem(1), nbr_y),
             [_remote(pass_x, pass_x, *sem(3), nbr_x), _remote(from_y, from_y, *sem(5), sib)]),
            (_remote(diag_0, diag_0, *sem(2), nbr_y), [_remote(diag_0, diag_0, *sem(6), sib)]),
            (_remote(diag_1, diag_1, *sem(3), nbr_x), [_remote(diag_1, diag_1, *sem(7), sib)]),
        ]
        other = (1 - c) * half
        from_sibling = [
            _remote(rows(2 * (1 - x) + y, other, half), rows(2 * (1 - x) + y, other, half), *sem(4), sib),
            _remote(rows(2 * x + (1 - y), other, half), rows(2 * x + (1 - y), other, half), *sem(5), sib),
            _remote(rows(diag, other, quarter), rows(diag, other, quarter), *sem(6), sib),
            _remote(rows(diag, other + quarter, quarter), rows(diag, other + quarter, quarter), *sem(7), sib),
        ]
        return first, arrivals, from_sibling

    def start(ins, outs, ssem, rsem):
        x, y, c, _ = _place()
        for w in range(n):
            half = outs[w].shape[1] // 2
            mine = outs[w].at[2 * x + y, pl.ds(c * half, half)]
            _remote(mine, mine, ssem.at[per * w], rsem.at[per * w], (1 - x, y, c)).start()
            _remote(mine, mine, ssem.at[per * w + 1], rsem.at[per * w + 1], (x, 1 - y, c)).start()

    def finish(ins, outs, ssem, rsem):
        plans = [plan(outs, ssem, rsem, w) for w in range(n)]
        started = []
        for direct in (True, False):
            for first, arrivals, _ in plans:
                for arrived, onward in (arrivals[:2] if direct else arrivals[2:]):
                    arrived.wait_recv()
                    for cp in onward:
                        cp.start()
                    started += onward
        for first, _, from_sibling in plans:
            for cp in from_sibling:
                cp.wait_recv()
            started += first
        for cp in started:
            cp.wait_send()

    return Exchange(bufs, [_sds(b.shape, b.dtype) for b in bufs], {w: w for w in range(n)}, per * n, start, finish)


def _ex_gather_direct(bufs):
    n = len(bufs)

    def copies(outs, ssem, rsem, only_first=False):
        x, y, c, chips = _place()
        me, sib = 2 * x + y, (x, y, 1 - c)
        first, relay, last = [], [], []
        for w in range(n):
            half = outs[w].shape[1] // 2
            mine = outs[w].at[me, pl.ds(c * half, half)]
            for k, (px, py) in enumerate(chips):
                sems = (ssem.at[6 * w + k], rsem.at[6 * w + k])
                sib_sems = (ssem.at[6 * w + 3 + k], rsem.at[6 * w + 3 + k])
                first.append(_remote(mine, mine, *sems, (px, py, c)))
                if only_first:
                    continue
                got = outs[w].at[2 * px + py, pl.ds(c * half, half)]
                relay.append((_remote(got, got, *sems, (px, py, c)), _remote(got, got, *sib_sems, sib)))
                theirs = outs[w].at[2 * px + py, pl.ds((1 - c) * half, half)]
                last.append(_remote(theirs, theirs, *sib_sems, sib))
        return first, relay, last

    def start(ins, outs, ssem, rsem):
        for cp in copies(outs, ssem, rsem, only_first=True)[0]:
            cp.start()

    def finish(ins, outs, ssem, rsem):
        first, relay, last = copies(outs, ssem, rsem)
        for arrived, onward in relay:
            arrived.wait_recv()
            onward.start()
        for cp in last:
            cp.wait_recv()
        for cp in first:
            cp.wait_send()
        for _, onward in relay:
            onward.wait_send()

    return Exchange(bufs, [_sds(b.shape, b.dtype) for b in bufs], {w: w for w in range(n)}, 6 * n, start, finish)


def _simple_exchange(arrays, landing, aliases, make_copies):
    def start(ins, outs, ssem, rsem):
        for cp, _ in make_copies(ins, outs, ssem, rsem, False):
            cp.start()

    def finish(ins, outs, ssem, rsem):
        cps = make_copies(ins, outs, ssem, rsem, True)
        for _, landed in cps:
            landed.wait_recv()
        for cp, _ in cps:
            cp.wait_send()

    return Exchange(arrays, landing, aliases, len(arrays) * 3, start, finish)


def _ex_pair_swap(grads):
    def make(ins, outs, ssem, rsem, landing):
        x, y, c, _ = _place()
        cps = [_remote(ins[w].at[:, 1 - c], outs[w], ssem.at[w], rsem.at[w], (x, y, 1 - c))
               for w in range(len(grads))]
        return [(cp, cp) for cp in cps]

    return _simple_exchange(grads, [_sds((NSH,) + g.shape[2:], g.dtype) for g in grads], {}, make)


def _ex_scatter(parts):
    def make(ins, outs, ssem, rsem, landing):
        x, y, c, chips = _place()
        out = []
        for w in range(len(parts)):
            for k, (px, py) in enumerate(chips):
                sems = (ssem.at[3 * w + k], rsem.at[3 * w + k])
                out.append((_remote(ins[w].at[2 * px + py], outs[w].at[k], *sems, (px, py, c)),
                            _remote(outs[w].at[k], outs[w].at[k], *sems, (px, py, c)) if landing else None))
        return out

    return _simple_exchange(parts, [_sds((3,) + p.shape[1:], p.dtype) for p in parts], {}, make)


def _ex_relay(bufs):
    def make(ins, outs, ssem, rsem, landing):
        x, y, c, chips = _place()
        sib = (x, y, 1 - c)
        out = []
        for w in range(len(bufs)):
            half = outs[w].shape[1] // 2
            for k, (px, py) in enumerate(chips):
                sems = (ssem.at[3 * w + k], rsem.at[3 * w + k])
                have = outs[w].at[2 * px + py, pl.ds(c * half, half)]
                miss = outs[w].at[2 * px + py, pl.ds((1 - c) * half, half)]
                out.append((_remote(have, have, *sems, sib), _remote(miss, miss, *sems, sib) if landing else None))
        return out

    return _simple_exchange(bufs, [_sds(b.shape, b.dtype) for b in bufs], {w: w for w in range(len(bufs))}, make)


def _ex_share(bufs):
    def make(ins, outs, ssem, rsem, landing):
        x, y, c, _ = _place()
        sib = (x, y, 1 - c)
        return [(_remote(outs[w].at[c], outs[w].at[c], ssem.at[w], rsem.at[w], sib),
                 _remote(outs[w].at[1 - c], outs[w].at[1 - c], ssem.at[w], rsem.at[w], sib) if landing else None)
                for w in range(len(bufs))]

    return _simple_exchange(bufs, [_sds(b.shape, b.dtype) for b in bufs], {w: w for w in range(len(bufs))}, make)


def _gather_small(block, ex):
    m_per, n = block.shape
    na, nl = len(ex.arrays), len(ex.landing)

    def body(x_ref, *refs):
        e_in, out_ref, e_out = refs[:na], refs[na], refs[na + 1:na + 1 + nl]
        ssem, rsem, lsem, e_ssem, e_rsem = refs[na + 1 + nl:]
        ex.start(e_in, e_out, e_ssem, e_rsem)
        x, y, c, chips = _place()
        me, sib = (x, y, c), (x, y, 1 - c)

        def rows(px, py, pc):
            return out_ref.at[pl.ds((4 * px + 2 * py + pc) * m_per, m_per), :]

        def copy(k, blk, to, src=None):
            return _remote(rows(*blk) if src is None else src, rows(*blk), ssem.at[k], rsem.at[k], to)

        mine = pltpu.make_async_copy(x_ref, rows(*me), lsem)
        mine.start()
        first = [copy(0, me, sib, src=x_ref)]
        first += [copy(1 + j, me, (*chip, c), src=x_ref) for j, chip in enumerate(chips)]
        for cp in first:
            cp.start()
        passed = [copy(4 + j, (*chip, c), sib) for j, chip in enumerate(chips)]
        for j, chip in enumerate(chips):
            copy(1 + j, (*chip, c), me).wait_recv()
            passed[j].start()
        copy(0, sib, me).wait_recv()
        for j, chip in enumerate(chips):
            copy(4 + j, (*chip, 1 - c), me).wait_recv()
        for cp in first + passed:
            cp.wait_send()
        mine.wait()
        ex.finish(e_in, e_out, e_ssem, e_rsem)

    outs = pl.pallas_call(
        body, name="gather_small", out_shape=[jax.ShapeDtypeStruct((8 * m_per, n), block.dtype)] + ex.landing,
        in_specs=[_VM] + [_ANY] * na, out_specs=[_VM] + [_ANY] * nl,
        scratch_shapes=[pltpu.SemaphoreType.DMA((7,)), pltpu.SemaphoreType.DMA((7,)), pltpu.SemaphoreType.DMA]
        + [pltpu.SemaphoreType.DMA((ex.n_sems,))] * 2,
        input_output_aliases={1 + i: 1 + j for i, j in ex.aliases.items()},
    )(block, *_in_hbm(ex.arrays))
    return outs[0], outs[1:]


def _row_block(rows):
    return max(t for t in range(16, 257, 16) if rows % t == 0)


def _pair_sum(grad, got, c_idx, name):
    _, _, half, cols = grad.shape
    tr = _row_block(half)

    def body(c_ref, a_ref, b_ref, o_ref):
        o_ref[...] = (a_ref[...].astype(F32) + b_ref[...].astype(F32)).astype(BF16)

    return pl.pallas_call(
        body, name=name, out_shape=_sds((NSH, half, cols), BF16),
        grid_spec=pltpu.PrefetchScalarGridSpec(
            num_scalar_prefetch=1, grid=(NSH, half // tr),
            in_specs=[pl.BlockSpec((None, None, tr, cols), lambda j, r, c: (j, c[0], r, 0)),
                      pl.BlockSpec((None, tr, cols), lambda j, r, c: (j, r, 0))],
            out_specs=pl.BlockSpec((None, tr, cols), lambda j, r, c: (j, r, 0))),
        compiler_params=_params(("arbitrary", "arbitrary"), 32),
    )(c_idx, *_in_hbm([grad, got]))


def _chip_sum(own, got, place, name):
    _, half, cols = own.shape
    tr = _row_block(half)

    def body(place_ref, own_ref, got_ref, o_ref):
        acc = own_ref[...].astype(F32)
        for k in range(3):
            acc = acc + got_ref[k].astype(F32)
        o_ref[...] = acc

    return pl.pallas_call(
        body, name=name, out_shape=_sds((2, half, cols), F32),
        grid_spec=pltpu.PrefetchScalarGridSpec(
            num_scalar_prefetch=1, grid=(half // tr,),
            in_specs=[pl.BlockSpec((None, tr, cols), lambda r, p: (p[0], r, 0)),
                      pl.BlockSpec((3, tr, cols), lambda r, p: (0, r, 0))],
            out_specs=pl.BlockSpec((None, tr, cols), lambda r, p: (p[1], r, 0))),
        compiler_params=_params(("arbitrary",), 32),
    )(place, *_in_hbm([own, got]))


def _adamw_math(w, g, m, v):
    m = B1 * m + (1.0 - B1) * g
    v = B2 * v + (1.0 - B2) * (g * g)
    m_hat = m / (1.0 - B1 ** STEP)
    v_hat = v / (1.0 - B2 ** STEP)
    return -LR * (m_hat / (jnp.sqrt(v_hat) + AEPS) + WD * w), m, v


def _adamw(w, g, m, v, name, after=()):
    rows, cols = w.shape
    tr = _row_block(rows)

    def body(w_ref, g_ref, m_ref, v_ref, go_ref, d_ref, nm_ref, nv_ref):
        g = g_ref[...]
        go_ref[...] = g
        d_ref[...], nm_ref[...], nv_ref[...] = _adamw_math(w_ref[...], g, m_ref[...], v_ref[...])

    blk = pl.BlockSpec((tr, cols), lambda r: (r, 0))
    return _call(
        body, (w, g, m, v), name=name, grid=(rows // tr,), out_shape=[_sds(w.shape, F32)] * 4,
        in_specs=[blk] * 4, out_specs=[blk] * 4,
        compiler_params=_params(("arbitrary",), 32), free=(0, 2, 3), after=after)


def _small_update(gathered, w, m, v):
    rows = w.shape[0]

    def body(ga_ref, w_ref, m_ref, v_ref, g_ref, d_ref, nm_ref, nv_ref):
        g = ga_ref[0:rows, :]
        for dev in range(1, 8):
            g = g + ga_ref[dev * rows:(dev + 1) * rows, :]
        g_ref[...] = g
        d_ref[...], nm_ref[...], nv_ref[...] = _adamw_math(w_ref[...], g, m_ref[...], v_ref[...])

    return pl.pallas_call(
        body, name="small_update", out_shape=[jax.ShapeDtypeStruct(w.shape, F32)] * 4,
        in_specs=[_VM] * 4, out_specs=[_VM] * 4,
    )(gathered, w, m, v)


SMALL = ("ffn1_norm", "mix_norm", "ffn2_norm", "final_norm", "pool_scale", "pool_w_group", "loss")
BIG = ("ffn1_w_gate_up", "ffn1_w_down", "w_in", "w_branch_pool", "w_branch_attn", "w_out",
       "ffn2_w_gate_up", "ffn2_w_down")
ORDER = ("ffn1_norm", "ffn1_w_gate_up", "ffn1_w_down", "mix_norm", "w_in", "pool_w_group", "pool_scale",
         "w_branch_pool", "w_branch_attn", "w_out", "ffn2_norm", "ffn2_w_gate_up", "ffn2_w_down", "final_norm")
SMALL_ROWS = 560


def _pack_small(t):
    parts = []
    for k in SMALL:
        rows = t[k].reshape(-1, 128) if k in t else jnp.zeros((1, 128), F32)
        parts.append(jnp.pad(rows, ((0, -rows.shape[0] % 8), (0, 0))))
    packed = jnp.concatenate(parts, axis=0)
    assert packed.shape == (SMALL_ROWS, 128), packed.shape
    return packed


def _unpack_small(packed, like):
    out, at = {}, 0
    for k in SMALL:
        n = like[k].size // 128 if k in like else 1
        out[k] = packed[at:at + n].reshape(like[k].shape) if k in like else packed[at, 0]
        at += n + (-n % 8)
    return out


def _halves(g):
    return g.reshape(NSH, 2, g.shape[1] // 2, g.shape[2])


def kernel(x, ffn1_norm, ffn1_w_gate_up, ffn1_w_down, mix_norm, w_in, pool_w_group, pool_scale, w_branch_pool, w_branch_attn, w_out, ffn2_norm, ffn2_w_gate_up, ffn2_w_down, final_norm, loss_target, m_ffn1_norm, m_ffn1_w_gate_up, m_ffn1_w_down, m_mix_norm, m_w_in, m_pool_w_group, m_pool_scale, m_w_branch_pool, m_w_branch_attn, m_w_out, m_ffn2_norm, m_ffn2_w_gate_up, m_ffn2_w_down, m_final_norm, v_ffn1_norm, v_ffn1_w_gate_up, v_ffn1_w_down, v_mix_norm, v_w_in, v_pool_w_group, v_pool_scale, v_w_branch_pool, v_w_branch_attn, v_w_out, v_ffn2_norm, v_ffn2_w_gate_up, v_ffn2_w_down, v_final_norm):
    wts = dict(ffn1_norm=ffn1_norm, ffn1_w_gate_up=ffn1_w_gate_up, ffn1_w_down=ffn1_w_down, mix_norm=mix_norm,
               w_in=w_in, pool_w_group=pool_w_group, pool_scale=pool_scale, w_branch_pool=w_branch_pool,
               w_branch_attn=w_branch_attn, w_out=w_out, ffn2_norm=ffn2_norm, ffn2_w_gate_up=ffn2_w_gate_up,
               ffn2_w_down=ffn2_w_down, final_norm=final_norm)
    mom = dict(ffn1_norm=m_ffn1_norm, ffn1_w_gate_up=m_ffn1_w_gate_up, ffn1_w_down=m_ffn1_w_down,
               mix_norm=m_mix_norm, w_in=m_w_in, pool_w_group=m_pool_w_group, pool_scale=m_pool_scale,
               w_branch_pool=m_w_branch_pool, w_branch_attn=m_w_branch_attn, w_out=m_w_out,
               ffn2_norm=m_ffn2_norm, ffn2_w_gate_up=m_ffn2_w_gate_up, ffn2_w_down=m_ffn2_w_down,
               final_norm=m_final_norm)
    var = dict(ffn1_norm=v_ffn1_norm, ffn1_w_gate_up=v_ffn1_w_gate_up, ffn1_w_down=v_ffn1_w_down,
               mix_norm=v_mix_norm, w_in=v_w_in, pool_w_group=v_pool_w_group, pool_scale=v_pool_scale,
               w_branch_pool=v_w_branch_pool, w_branch_attn=v_w_branch_attn, w_out=v_w_out,
               ffn2_norm=v_ffn2_norm, ffn2_w_gate_up=v_ffn2_w_gate_up, ffn2_w_down=v_ffn2_w_down,
               final_norm=v_final_norm)

    c_idx = lax.axis_index("c").astype(jnp.int32).reshape(1)
    me_idx = (2 * lax.axis_index("x") + lax.axis_index("y")).astype(jnp.int32).reshape(1)
    place = jnp.concatenate([me_idx, c_idx])
    x0, tgt = x[0], loss_target[0]
    wgrp = pool_w_group[0].astype(BF16)
    g1, gm, g2, gf = ffn1_norm, mix_norm, ffn2_norm, final_norm.reshape(1, D)
    grad, delta, new_m, new_v = {}, {}, {}, {}

    def pair_sums(keys, parts, got):
        return [_pair_sum(parts[i], got[i], c_idx, "pair_sum_" + k) for i, k in enumerate(keys)]

    def chip_sums(keys, chip_parts, owned):
        return [_chip_sum(chip_parts[i], owned[i], place, "chip_sum_" + k) for i, k in enumerate(keys)]

    def adamw(k, after=()):
        outs = _adamw(wts[k][0], grad[k][0], mom[k][0], var[k][0], "adamw_" + k, after=after)
        grad[k], delta[k], new_m[k], new_v[k] = (o.reshape(wts[k].shape) for o in outs)

    own = {k: _cast_into_block(wts[k][0], me_idx, "cast_" + k) for k in BIG}
    first, late = ("ffn1_w_gate_up", "ffn1_w_down"), ("w_branch_pool", "w_branch_attn", "w_out",
                                                       "ffn2_w_gate_up", "ffn2_w_down")
    full = dict(zip(first, _exchange_alone(_ex_gather([own[k] for k in first]), "gather_ffn1")))
    wgu1, wd1 = full["ffn1_w_gate_up"], full["ffn1_w_down"].reshape(DFF, D)
    (h1, n1, gu1, a1), (win,) = _ffn_fwd(x0, g1, wgu1, wd1, "ffn1_fwd", exchange=_ex_gather_direct([own["w_in"]]))
    sems_l, thru_l, token_l = _gather_start([own[k_] for k_ in late], [h1], "gather_late_start")
    u, xp, q, k, v, gp, gs = _mix_in(h1, gm, win, after=(token_l,))
    o_sb, ctot = _attn_fwd(q, k, v)
    arrived = _gather_wait(sems_l, thru_l, [o_sb], "gather_late_wait")
    wbp, wba, wout = _exchange_alone(_ex_relay(arrived[:3]), "relay_mix")
    wout = wout.reshape(D, D)
    (h2, pm, p, yp, ys, mm), (wgu2, wd2) = _mix_out(h1, xp, o_sb, gp, gs, wgrp, pool_scale, wbp, wba, wout,
                                                    exchange=_ex_relay(arrived[3:]))
    wd2 = wd2.reshape(DFF, D)
    h3, n3, gu3, a3 = _ffn_fwd(h2, g2, wgu2, wd2, "ffn2_fwd")
    dh3, loss_row, d_gf = _head(h3, tgt, gf)

    def grad_gate_up(n, dgu, name, exchange=None):
        res = _wgrad(n, dgu, NSH, 512, name, exchange=exchange)
        return [_halves(res)] if exchange is None else ([_halves(res[0])], res[1])

    def grad_down(a, dh, name, exchange=None):
        res = _wgrad(a, dh, 1, FFS, name, exchange=exchange)
        halves = lambda g: [_halves(g.reshape(NSH, DFF // NSH, D))]
        return halves(res) if exchange is None else (halves(res[0]), res[1])

    k_gu2, k_d2, k_gu1, k_d1, k_in = (("ffn2_w_gate_up",), ("ffn2_w_down",), ("ffn1_w_gate_up",),
                                      ("ffn1_w_down",), ("w_in",))
    dgu3 = _ffn_bwd_act(dh3, gu3, wd2, "ffn2_bwd_act")
    pa = grad_gate_up(n3, dgu3, "wgrad_gu2") + grad_down(a3, dh3, "wgrad_d2")
    (dh2, d_g2), got_a = _ffn_bwd_in(dh3, h2, g2, dgu3, wgu2, "ffn2_bwd_in", exchange=_ex_pair_swap(pa))
    chip_a = pair_sums(k_gu2 + k_d2, pa, got_a)
    dlg, dyp, dys, do_sb, dyg, dxp, d_scale = _mix_bwd_out(dh2, gp, gs, yp, ys, pm, wgrp, pool_scale, wbp, wba, wout)
    kb = ("w_out", "w_branch_pool", "w_branch_attn")
    pb = [_halves(_wgrad(mm, dh2, 1, 512, "wgrad_out").reshape(NSH, D // NSH, D)),
          _halves(_wgrad(p, dyp, NSH, PW, "wgrad_bp")), _halves(_wgrad(o_sb, dys, NSH, SBW, "wgrad_ba"))]
    k_a, k_in = k_gu2 + k_d2, k_in + kb
    sems_a, thru_a, token_a = _scatter_start(chip_a, "scatter_a_start")
    dq, dk, dv = _attn_bwd(q, k, v, do_sb, ctot, after=(token_a,))
    chip_a, owned_a = _scatter_wait(sems_a, thru_a, [dq], "scatter_a_wait")
    halves_a = chip_sums(k_a, chip_a, owned_a)
    (dh1, d_gm, dproj), both_a = _mix_bwd_in(dh2, h1, gm, (dxp, dq, dk, dv, dlg), win, exchange=_ex_share(halves_a))
    for i, k_ in enumerate(k_a):
        grad[k_] = both_a[i].reshape(wts[k_].shape)

    p_in = [_halves(_wgrad(u, dproj, NSH, 512, "wgrad_in"))] + pb
    p_d1, got_in = grad_down(a1, dh1, "wgrad_d1", exchange=_ex_pair_swap(p_in))
    sems_in, thru_in, token_in = _scatter_start(pair_sums(k_in, p_in, got_in), "scatter_in_start")
    dgu1, got_d1 = _ffn_bwd_act(dh1, gu1, wd1, "ffn1_bwd_act", exchange=_ex_pair_swap(p_d1), after=(token_in,))
    sems_d1, thru_d1, token_d1 = _scatter_start(pair_sums(k_d1, p_d1, got_d1), "scatter_d1_start")
    p_gu1 = [_halves(_wgrad(n1, dgu1, NSH, 512, "wgrad_gu1", after=(token_in, token_d1)))]
    chip_in, owned_in = _scatter_wait(sems_in, thru_in, p_gu1, "scatter_in_wait")
    chip_d1, owned_d1 = _scatter_wait(sems_d1, thru_d1, p_gu1, "scatter_d1_wait")
    halves_in, halves_d1 = chip_sums(k_in, chip_in, owned_in), chip_sums(k_d1, chip_d1, owned_d1)
    landed = _exchange_alone(_join(_ex_pair_swap(p_gu1), _ex_share(halves_in)), "pair_swap_gu1")
    for i, k_ in enumerate(k_in):
        grad[k_] = landed[1 + i].reshape(wts[k_].shape)
    sems, thru, token = _scatter_start(pair_sums(k_gu1, p_gu1, landed[:1]), "scatter_gu1_start")
    for k_ in k_a + k_in:
        adamw(k_, after=(token,))
    dx, d_g1 = _ffn_bwd_in(dh1, x0, g1, dgu1, wgu1, "ffn1_bwd_in", after=(token,))
    chip_gu1, owned_gu1 = _scatter_wait(sems, thru, [dx] + [delta[k_] for k_ in k_a + k_in], "scatter_gu1_wait")

    small_g = dict(ffn1_norm=d_g1, mix_norm=d_gm, ffn2_norm=d_g2, final_norm=d_gf, pool_scale=d_scale,
                   pool_w_group=_wgrad_groups(pm, dyg), loss=loss_row)
    gathered, both = _gather_small(_pack_small(small_g), _ex_share(halves_d1 + chip_sums(k_gu1, chip_gu1, owned_gu1)))
    grad["ffn1_w_down"] = both[0].reshape(ffn1_w_down.shape)
    grad["ffn1_w_gate_up"] = both[1].reshape(ffn1_w_gate_up.shape)
    for k_ in k_d1 + k_gu1:
        adamw(k_)
    sg, sd, sm, sv = _small_update(gathered, _pack_small(wts), _pack_small(mom), _pack_small(var))
    sums = _unpack_small(sg, wts)
    loss = sums.pop("loss")
    grad.update(sums)
    for dst, packed in ((delta, sd), (new_m, sm), (new_v, sv)):
        vals = _unpack_small(packed, wts)
        vals.pop("loss")
        dst.update(vals)
    return (loss, dx[None], *[grad[k_] for k_ in ORDER], *[delta[k_] for k_ in ORDER],
            *[new_m[k_] for k_ in ORDER], *[new_v[k_] for k_ in ORDER])
```

```python
import functools

import jax
import jax.numpy as jnp
from jax import lax
from jax.experimental import pallas as pl
from jax.experimental.pallas import tpu as pltpu

F32 = jnp.float32
BF16 = jnp.bfloat16

S = 2048
D = 1024
DFF = 2816
FFS = 2 * DFF // 4
NSH = 4
PW = 512
PG = 128
POOL_WINDOWS = (2, 4, 8, 16)
HALO = 16
SBW = 512
DH = 64
EPS = 1e-6
SCALE = 0.125
LOG2E = 1.4426950408889634
TA = 256
QB = 2
MIB = 1024 * 1024

LR, B1, B2, AEPS, WD, STEP = 0.001, 0.9, 0.999, 1e-08, 0.01, 10

_VM = pl.BlockSpec(memory_space=pltpu.VMEM)
_ANY = pl.BlockSpec(memory_space=pl.ANY)
MESH = pl.DeviceIdType.MESH


def _nn(a, b):
    return jnp.dot(a, b, preferred_element_type=F32)


def _nt(a, b):
    return lax.dot_general(a, b, (((1,), (1,)), ((), ())), preferred_element_type=F32)


def _tn(a, b):
    return lax.dot_general(a, b, (((0,), (0,)), ((), ())), preferred_element_type=F32)


def _params(sem, vmem_mib):
    return pltpu.CompilerParams(dimension_semantics=sem, vmem_limit_bytes=vmem_mib * MIB)


def _rows(tm, width):
    return pl.BlockSpec((tm, width), lambda i: (i, 0))


def _fixed(shape):
    return pl.BlockSpec(shape, lambda *_: (0,) * len(shape))


def _sds(shape, dtype):
    return pltpu.HBM(shape, dtype)


def _in_hbm(args):
    return [pltpu.with_memory_space_constraint(a, pltpu.HBM) for a in args]


def _stage(pairs):
    @pl.when(pl.program_id(0) == 0)
    def _():
        for src, dst in pairs:
            pltpu.sync_copy(src, dst)


def _vmem_like(*arrays):
    return [pltpu.VMEM(a.shape, a.dtype) for a in arrays]


class Exchange:
    def __init__(self, arrays, landing, aliases, n_sems, start, finish):
        self.arrays, self.landing, self.aliases, self.n_sems = list(arrays), list(landing), dict(aliases), n_sems
        self.start, self.finish = start, finish


def _join(a, b):
    na, la = len(a.arrays), len(a.landing)

    def both(fa, fb):
        def run(ins, outs, ssem, rsem):
            fa(ins[:na], outs[:la], ssem.at[pl.ds(0, a.n_sems)], rsem.at[pl.ds(0, a.n_sems)])
            fb(ins[na:], outs[la:], ssem.at[pl.ds(a.n_sems, b.n_sems)], rsem.at[pl.ds(a.n_sems, b.n_sems)])
        return run

    aliases = {**a.aliases, **{na + i: la + j for i, j in b.aliases.items()}}
    return Exchange(a.arrays + b.arrays, a.landing + b.landing, aliases, a.n_sems + b.n_sems,
                    both(a.start, b.start), both(a.finish, b.finish))


def _call(body, args, *, name, grid, in_specs, out_specs, out_shape, scratch_shapes=(), compiler_params=None,
          exchange=None, free=(), after=()):
    args = [a if i in free else pltpu.with_memory_space_constraint(a, pltpu.HBM) for i, a in enumerate(args)]
    if exchange is None:
        n_in = len(in_specs)

        def plain(*refs):
            body(*refs[:n_in], *refs[n_in + len(after):])

        return pl.pallas_call(plain, name=name, grid=grid, in_specs=list(in_specs) + [_ANY] * len(after),
                              out_specs=out_specs, out_shape=out_shape, scratch_shapes=list(scratch_shapes),
                              compiler_params=compiler_params)(*args, *after)
    ex = exchange
    n_in, n_out, n_scr = len(in_specs), len(out_specs), len(scratch_shapes)
    na, nl = len(ex.arrays), len(ex.landing)

    def hosted(*refs):
        at = [0]

        def take(n):
            at[0] += n
            return refs[at[0] - n:at[0]]

        k_in, _, e_in, k_out, e_out, k_scr = take(n_in), take(len(after)), take(na), take(n_out), take(nl), take(n_scr)
        ssem, rsem = take(2)
        ids = [pl.program_id(a) for a in range(len(grid))]
        first = functools.reduce(jnp.logical_and, [i == 0 for i in ids])
        last = functools.reduce(jnp.logical_and, [i == g - 1 for i, g in zip(ids, grid)])

        @pl.when(first)
        def _():
            ex.start(e_in, e_out, ssem, rsem)

        body(*k_in, *k_out, *k_scr)

        @pl.when(last)
        def _():
            ex.finish(e_in, e_out, ssem, rsem)

    outs = pl.pallas_call(
        hosted, name=name, grid=grid,
        in_specs=list(in_specs) + [_ANY] * (len(after) + na), out_specs=list(out_specs) + [_ANY] * nl,
        out_shape=list(out_shape) + ex.landing,
        scratch_shapes=list(scratch_shapes) + [pltpu.SemaphoreType.DMA((ex.n_sems,))] * 2,
        input_output_aliases={n_in + len(after) + i: n_out + j for i, j in ex.aliases.items()},
        compiler_params=compiler_params,
    )(*args, *after, *_in_hbm(ex.arrays))
    return outs[:n_out], outs[n_out:]


def _exchange_alone(ex, name):
    def body(*refs):
        na, nl = len(ex.arrays), len(ex.landing)
        ex.start(refs[:na], refs[na:na + nl], refs[-2], refs[-1])
        ex.finish(refs[:na], refs[na:na + nl], refs[-2], refs[-1])

    return pl.pallas_call(
        body, name=name, in_specs=[_ANY] * len(ex.arrays), out_specs=[_ANY] * len(ex.landing),
        out_shape=ex.landing, scratch_shapes=[pltpu.SemaphoreType.DMA((ex.n_sems,))] * 2,
        input_output_aliases=ex.aliases,
    )(*_in_hbm(ex.arrays))


_HBM = pl.BlockSpec(memory_space=pltpu.HBM)
_SEM = pl.BlockSpec(memory_space=pltpu.SEMAPHORE)
_EFFECT = pltpu.SideEffectType.DATAFLOW_SIDE_EFFECTING


def _scatter_copies(srcs, lands, ssems, rsems):
    x, y, c, chips = _place()
    return [_remote(srcs[w].at[2 * px + py], lands[w].at[k], ssems[3 * w + k], rsems[3 * w + k], (px, py, c))
            for w in range(len(srcs)) for k, (px, py) in enumerate(chips)]


def _scatter_start(parts, name):
    n, ncp = len(parts), 3 * len(parts)
    lands = [lax.empty((3,) + p.shape[1:], p.dtype) for p in parts]

    def body(*refs):
        srcs, land_refs = refs[:n], refs[n:2 * n]
        ssems, rsems = refs[2 * n:2 * n + ncp], refs[2 * n + ncp:2 * n + 2 * ncp]
        for cp in _scatter_copies(srcs, land_refs, ssems, rsems):
            cp.start()
        token = refs[-1]
        token[...] = jnp.zeros_like(token)

    outs = pl.pallas_call(
        body, name=name,
        out_shape=([pltpu.SemaphoreType.DMA(())] * (2 * ncp) + [pltpu.HBM(a.shape, a.dtype) for a in parts + lands]
                   + [jax.ShapeDtypeStruct((8, 128), F32)]),
        in_specs=[_HBM] * (2 * n), out_specs=[_SEM] * (2 * ncp) + [_HBM] * (2 * n) + [_VM],
        input_output_aliases={i: 2 * ncp + i for i in range(2 * n)},
        compiler_params=pltpu.CompilerParams(has_side_effects=_EFFECT),
    )(*_in_hbm(parts), *_in_hbm(lands))
    sems, thru, token = outs[:2 * ncp], outs[2 * ncp:2 * ncp + 2 * n], outs[-1]
    return sems, thru, token


def _scatter_wait(sems, thru, after, name):
    n = len(thru) // 2
    ncp = 3 * n

    def body(*refs):
        srcs, land_refs = refs[:n], refs[n:2 * n]
        ssems, rsems = refs[2 * n:2 * n + ncp], refs[2 * n + ncp:2 * n + 2 * ncp]
        for cp in _scatter_copies(srcs, land_refs, ssems, rsems):
            cp.wait_send()
            cp.wait_recv()

    outs = pl.pallas_call(
        body, name=name, out_shape=[pltpu.HBM(a.shape, a.dtype) for a in thru],
        in_specs=[_HBM] * (2 * n) + [_SEM] * (2 * ncp) + [_ANY] * len(after), out_specs=[_HBM] * (2 * n),
        input_output_aliases={i: i for i in range(2 * n)},
        compiler_params=pltpu.CompilerParams(has_side_effects=_EFFECT),
    )(*thru, *sems, *after)
    return outs[:n], outs[n:]


def _gather_copies(bufs, ssems, rsems, sending):
    x, y, c, chips = _place()
    out = []
    for w, ref in enumerate(bufs):
        half = ref.shape[1] // 2
        for k, (px, py) in enumerate(chips):
            rows = ref.at[2 * x + y if sending else 2 * px + py, pl.ds(c * half, half)]
            out.append(_remote(rows, rows, ssems[3 * w + k], rsems[3 * w + k], (px, py, c)))
    return out


def _gather_start(bufs, after, name):
    n, ncp = len(bufs), 3 * len(bufs)

    def body(*refs):
        ssems, rsems = refs[n + len(after):n + len(after) + ncp], refs[n + len(after) + ncp:n + len(after) + 2 * ncp]
        for cp in _gather_copies(refs[:n], ssems, rsems, True):
            cp.start()
        token = refs[-1]
        token[...] = jnp.zeros_like(token)

    outs = pl.pallas_call(
        body, name=name,
        out_shape=([pltpu.SemaphoreType.DMA(())] * (2 * ncp) + [pltpu.HBM(a.shape, a.dtype) for a in bufs]
                   + [jax.ShapeDtypeStruct((8, 128), F32)]),
        in_specs=[_HBM] * n + [_ANY] * len(after), out_specs=[_SEM] * (2 * ncp) + [_HBM] * n + [_VM],
        input_output_aliases={i: 2 * ncp + i for i in range(n)},
        compiler_params=pltpu.CompilerParams(has_side_effects=_EFFECT),
    )(*_in_hbm(bufs), *after)
    return outs[:2 * ncp], outs[2 * ncp:2 * ncp + n], outs[-1]


def _gather_wait(sems, thru, after, name):
    n = len(thru)
    ncp = 3 * n

    def body(*refs):
        ssems, rsems = refs[n:n + ncp], refs[n + ncp:n + 2 * ncp]
        for cp in _gather_copies(refs[:n], ssems, rsems, True):
            cp.wait_send()
        for cp in _gather_copies(refs[:n], ssems, rsems, False):
            cp.wait_recv()

    return pl.pallas_call(
        body, name=name, out_shape=[pltpu.HBM(a.shape, a.dtype) for a in thru],
        in_specs=[_HBM] * n + [_SEM] * (2 * ncp) + [_ANY] * len(after), out_specs=[_HBM] * n,
        input_output_aliases={i: i for i in range(n)},
        compiler_params=pltpu.CompilerParams(has_side_effects=_EFFECT),
    )(*thru, *sems, *after)


def _rms(x):
    r = lax.rsqrt(jnp.mean(x * x, axis=-1, keepdims=True) + EPS)
    return r, x * r


def _rms_bwd(dn, xr, r, gain):
    dng = dn * gain
    dx = r * (dng - xr * jnp.mean(dng * xr, axis=-1, keepdims=True))
    return dx, jnp.sum(dn * xr, axis=0, keepdims=True)


def _ffn_fwd(x, gain, wgu, wd, name, exchange=None, head=None):
    tm = 256

    def body(x_ref, g_ref, wgu_hbm, wd_hbm, *rest):
        if head is None:
            h_ref, n_ref, gu_ref, a_ref, wgu_ref, wd_ref = rest
        else:
            t_ref, gf_ref, h_ref, loss_ref, dgf_ref, n_ref, gu_ref, a_ref, wgu_ref, wd_ref = rest
        _stage([(wgu_hbm, wgu_ref), (wd_hbm, wd_ref)])
        x = x_ref[...]
        _, xr = _rms(x)
        n = (xr * g_ref[...]).astype(BF16)
        n_ref[...] = n
        acc = jnp.zeros((tm, D), F32)
        for j in range(2):
            g = _nn(n, wgu_ref[j])
            u = _nn(n, wgu_ref[2 + j])
            gu_ref[:, j * FFS:(j + 1) * FFS] = g.astype(BF16)
            gu_ref[:, (2 + j) * FFS:(3 + j) * FFS] = u.astype(BF16)
            half_act = (0.5 * (g * jax.nn.sigmoid(g) * u)).astype(BF16)
            a_ref[:, j * FFS:(j + 1) * FFS] = half_act
            acc = acc + _nn(half_act, wd_ref[j * FFS:(j + 1) * FFS, :])
        h = x + acc
        if head is None:
            h_ref[...] = h
            return
        gf = gf_ref[...]
        r, hr = _rms(h)
        err = hr * gf - t_ref[...]
        dh, dgain = _rms_bwd(err * (1.0 / D), hr, r, gf)
        h_ref[...] = dh

        @pl.when(pl.program_id(0) == 0)
        def _():
            dgf_ref[...] = jnp.zeros_like(dgf_ref)
            loss_ref[...] = jnp.zeros_like(loss_ref)

        dgf_ref[...] += dgain
        loss_ref[...] += jnp.full((1, 128), (0.5 / D) * jnp.sum(err * err), F32)

    saved_specs = [_rows(tm, D), _rows(tm, 4 * FFS), _rows(tm, DFF)]
    saved_shapes = [_sds((S, D), BF16), _sds((S, 4 * FFS), BF16), _sds((S, DFF), BF16)]
    if head is None:
        return _call(
            body, (x, gain, wgu, wd), name=name, grid=(S // tm,),
            in_specs=[_rows(tm, D), _fixed((1, D)), _ANY, _ANY],
            out_specs=[_rows(tm, D)] + saved_specs, out_shape=[_sds((S, D), F32)] + saved_shapes,
            scratch_shapes=_vmem_like(wgu, wd),
            compiler_params=_params(("arbitrary",), 56), exchange=exchange)
    return _call(
        body, (x, gain, wgu, wd, *head), name=name, grid=(S // tm,),
        in_specs=[_rows(tm, D), _fixed((1, D)), _ANY, _ANY, _rows(tm, D), _fixed((1, D))],
        out_specs=[_rows(tm, D), _fixed((1, 128)), _fixed((1, D))] + saved_specs,
        out_shape=[_sds((S, D), F32), _sds((1, 128), F32), _sds((1, D), F32)] + saved_shapes,
        scratch_shapes=_vmem_like(wgu, wd),
        compiler_params=_params(("arbitrary",), 56), exchange=exchange, free=(4, 5))


def _ffn_bwd(dh, x, gain, gu, wgu, wd, name):
    tm = 256

    def body(dh_ref, x_ref, g_ref, gu_ref, wgu_hbm, wd_hbm, dx_ref, dgu_ref, dg_ref, wgu_ref, wd_ref):
        _stage([(wgu_hbm, wgu_ref), (wd_hbm, wd_ref)])
        dh = dh_ref[...]
        dhb = dh.astype(BF16)
        dn = jnp.zeros((tm, D), F32)
        for j in range(2):
            g = gu_ref[:, j * FFS:(j + 1) * FFS].astype(F32)
            u = gu_ref[:, (2 + j) * FFS:(3 + j) * FFS].astype(F32)
            da = 0.5 * _nt(dhb, wd_ref[j * FFS:(j + 1) * FFS, :])
            sg = jax.nn.sigmoid(g)
            dgb = (da * u * (sg * (1.0 + g * (1.0 - sg)))).astype(BF16)
            dub = (da * (g * sg)).astype(BF16)
            dgu_ref[:, j * FFS:(j + 1) * FFS] = dgb
            dgu_ref[:, (2 + j) * FFS:(3 + j) * FFS] = dub
            dn = dn + _nt(dgb, wgu_ref[j]) + _nt(dub, wgu_ref[2 + j])
        r, xr = _rms(x_ref[...])
        dx, dgain = _rms_bwd(dn, xr, r, g_ref[...])
        dx_ref[...] = dh + dx

        @pl.when(pl.program_id(0) == 0)
        def _():
            dg_ref[...] = jnp.zeros_like(dg_ref)

        dg_ref[...] += dgain

    return _call(
        body, (dh, x, gain, gu, wgu, wd), name=name, grid=(S // tm,),
        in_specs=[_rows(tm, D), _rows(tm, D), _fixed((1, D)), _rows(tm, 4 * FFS), _ANY, _ANY],
        out_specs=[_rows(tm, D), _rows(tm, 4 * FFS), _fixed((1, D))],
        out_shape=[_sds((S, D), F32), _sds((S, 4 * FFS), BF16), _sds((1, D), F32)],
        scratch_shapes=_vmem_like(wgu, wd), compiler_params=_params(("arbitrary",), 56))


def _ffn_bwd_act(dh, gu, wd, name, exchange=None, after=()):
    tm = 512

    def body(dh_ref, gu_ref, wd_hbm, dgu_ref, wd_ref):
        _stage([(wd_hbm, wd_ref)])
        dhb = dh_ref[...].astype(BF16)
        for j in range(2):
            g = gu_ref[:, j * FFS:(j + 1) * FFS].astype(F32)
            u = gu_ref[:, (2 + j) * FFS:(3 + j) * FFS].astype(F32)
            da = 0.5 * _nt(dhb, wd_ref[j * FFS:(j + 1) * FFS, :])
            sg = jax.nn.sigmoid(g)
            dgu_ref[:, j * FFS:(j + 1) * FFS] = (da * u * (sg * (1.0 + g * (1.0 - sg)))).astype(BF16)
            dgu_ref[:, (2 + j) * FFS:(3 + j) * FFS] = (da * (g * sg)).astype(BF16)

    res = _call(
        body, (dh, gu, wd), name=name, grid=(S // tm,),
        in_specs=[_rows(tm, D), _rows(tm, 4 * FFS), _ANY], out_specs=[_rows(tm, 4 * FFS)],
        out_shape=[_sds((S, 4 * FFS), BF16)], scratch_shapes=_vmem_like(wd),
        compiler_params=_params(("arbitrary",), 56), exchange=exchange, after=after)
    return res[0] if exchange is None else (res[0][0], res[1])


def _ffn_bwd_in(dh, x, gain, dgu, wgu, name, exchange=None, after=()):
    tm = 512

    def body(dh_ref, x_ref, g_ref, dgu_ref, wgu_hbm, dx_ref, dg_ref, wgu_ref):
        _stage([(wgu_hbm, wgu_ref)])
        dn = jnp.zeros((tm, D), F32)
        for j in range(NSH):
            dn = dn + _nt(dgu_ref[:, j * FFS:(j + 1) * FFS], wgu_ref[j])
        r, xr = _rms(x_ref[...])
        dx, dgain = _rms_bwd(dn, xr, r, g_ref[...])
        dx_ref[...] = dh_ref[...] + dx

        @pl.when(pl.program_id(0) == 0)
        def _():
            dg_ref[...] = jnp.zeros_like(dg_ref)

        dg_ref[...] += dgain

    return _call(
        body, (dh, x, gain, dgu, wgu), name=name, grid=(S // tm,),
        in_specs=[_rows(tm, D), _rows(tm, D), _fixed((1, D)), _rows(tm, 4 * FFS), _ANY],
        out_specs=[_rows(tm, D), _fixed((1, D))],
        out_shape=[_sds((S, D), F32), _sds((1, D), F32)],
        scratch_shapes=_vmem_like(wgu),
        compiler_params=_params(("arbitrary",), 56), exchange=exchange, after=after)


def _mix_in(h, gain, w_in, after=()):
    tm = 512

    def body(h_ref, g_ref, w_hbm, u_ref, xp_ref, q_ref, k_ref, v_ref, gp_ref, gs_ref, w_ref):
        _stage([(w_hbm, w_ref)])
        _, hr = _rms(h_ref[...])
        u = (hr * g_ref[...]).astype(BF16)
        u_ref[...] = u
        p0 = _nn(u, w_ref[0])
        xp_ref[...] = p0[:, :PW]
        q_ref[...] = p0[:, PW:].astype(BF16)
        p1 = _nn(u, w_ref[1])
        k_ref[...] = p1[:, :SBW].astype(BF16)
        v_ref[...] = p1[:, SBW:].astype(BF16)
        gp_ref[...] = jax.nn.sigmoid(_nn(u, w_ref[2])).astype(BF16)
        gs_ref[...] = jax.nn.sigmoid(_nn(u, w_ref[3])).astype(BF16)

    return _call(
        body, (h, gain, w_in), name="mix_in", grid=(S // tm,),
        in_specs=[_rows(tm, D), _fixed((1, D)), _ANY],
        out_specs=[_rows(tm, D), _rows(tm, PW), _rows(tm, SBW), _rows(tm, SBW), _rows(tm, SBW),
                   _rows(tm, D), _rows(tm, D)],
        out_shape=[_sds((S, D), BF16), _sds((S, PW), F32), _sds((S, SBW), BF16), _sds((S, SBW), BF16),
                   _sds((S, SBW), BF16), _sds((S, D), BF16), _sds((S, D), BF16)],
        scratch_shapes=_vmem_like(w_in),
        compiler_params=_params(("arbitrary",), 48), free=(1,), after=after)


def _hilo_dot(x, tri):
    hi = x.astype(BF16)
    lo = (x - hi.astype(F32)).astype(BF16)
    return _nn(hi, tri) + _nn(lo, tri)


def _log_terms(qk):
    z2 = qk * (SCALE * LOG2E)
    lb = jnp.minimum(z2, 0.0) - jnp.log2(1.0 + jnp.exp2(-jnp.abs(z2)))
    return lb, lb - z2


def _head_masks():
    lane = lax.broadcasted_iota(jnp.int32, (1, 2 * DH), 1)
    return (lane < DH, lane >= DH)


def _attn_fwd(q, k, v, exchange=None):
    T = TA

    def body(q_ref, k_ref, v_ref, o_ref, c_ref):
        i2 = 2 * pl.program_id(1)
        row = lax.broadcasted_iota(jnp.int32, (T, T), 0)
        col = lax.broadcasted_iota(jnp.int32, (T, T), 1)
        after = (row > col).astype(BF16)
        causal = col < row
        masks = _head_masks()
        qms = {}
        for b in range(QB):
            q2 = q_ref[b * T:(b + 1) * T, :]
            for h, hm in enumerate(masks):
                qms[b, h] = jnp.where(hm, q2, jnp.zeros_like(q2))

        def blocks(keys, pairs, carries, os):
            ks, vms = [], []
            for j in keys:
                rows = pl.ds(pl.multiple_of(j * T, T), T)
                vj = v_ref[rows, :]
                ks.append(k_ref[rows, :])
                vms.append([jnp.where(hm, vj, jnp.zeros_like(vj)) for hm in masks])
            units = [(n, h) for n in range(len(pairs)) for h in range(2)]
            qks = {(n, h): _nt(qms[pairs[n][0], h], ks[pairs[n][1]]) for n, h in units}
            lbs, l1ms = {}, {}
            for u in units:
                lbs[u], l1m = _log_terms(qks[u])
                l1ms[u] = jnp.where(causal, l1m, 0.0) if pairs[u[0]][2] else l1m
            cins = {u: _hilo_dot(l1ms[u], after) for u in units}
            carries, os = dict(carries), list(os)
            for n, h in units:
                b, key, diag = pairs[n]
                a = jnp.exp2(lbs[n, h] + cins[n, h] + carries[b, h])
                if diag:
                    a = jnp.where(causal, a, 0.0)
                os[b] = os[b] + _nn(a.astype(BF16), vms[key][h])
                carries[b, h] = carries[b, h] + jnp.sum(l1ms[n, h], axis=1, keepdims=True)
            return carries, tuple(os)

        carries = {(b, h): jnp.zeros((T, 1), F32) for b in range(QB) for h in range(2)}
        os = tuple(jnp.zeros((T, 2 * DH), F32) for _ in range(QB))
        carries, os = blocks([i2 + 1, i2], [(1, 0, True), (0, 1, True), (1, 1, False)], carries, os)
        carries, os = lax.fori_loop(
            0, i2, lambda jj, c: blocks([i2 - 1 - jj], [(0, 0, False), (1, 0, False)], c[0], c[1]), (carries, os))
        for b in range(QB):
            o_ref[b * T:(b + 1) * T, :] = os[b].astype(BF16)
            c_ref[b * T:(b + 1) * T, :] = jnp.where(masks[0], carries[b, 0], carries[b, 1])

    blk = pl.BlockSpec((QB * T, 2 * DH), lambda p, i: (i, p))
    full = pl.BlockSpec((S, 2 * DH), lambda p, i: (0, p))
    return _call(
        body, (q, k, v), name="attn_fwd", grid=(SBW // (2 * DH), S // (QB * T)),
        in_specs=[blk, full, full], out_specs=[blk, blk],
        out_shape=[_sds((S, SBW), BF16), _sds((S, SBW), F32)],
        compiler_params=_params(("arbitrary", "arbitrary"), 40), exchange=exchange)


def _attn_bwd(q, k, v, do, ctot, after=()):
    T = TA
    nq = S // (QB * T)

    def body(q_ref, k_ref, v_ref, do_ref, c_ref, dq_ref, dk_ref, dv_ref, dk_acc, dv_acc):
        step = pl.program_id(1)
        i2 = 2 * step

        @pl.when(step == 0)
        def _():
            dk_acc[...] = jnp.zeros_like(dk_acc)
            dv_acc[...] = jnp.zeros_like(dv_acc)

        row = lax.broadcasted_iota(jnp.int32, (T, T), 0)
        col = lax.broadcasted_iota(jnp.int32, (T, T), 1)
        upto = (row <= col).astype(BF16)
        before = (row < col).astype(BF16)
        causal = col < row
        masks = _head_masks()
        qms, doms, ctots = {}, {}, {}
        for b in range(QB):
            q2, do2 = q_ref[b * T:(b + 1) * T, :], do_ref[b * T:(b + 1) * T, :]
            for h, hm in enumerate(masks):
                qms[b, h] = jnp.where(hm, q2, jnp.zeros_like(q2))
                doms[b, h] = jnp.where(hm, do2, jnp.zeros_like(do2))
                ctots[b, h] = c_ref[b * T:(b + 1) * T, h * DH:h * DH + 1]

        def blocks(keys, pairs, sums, dqs):
            rows = [pl.ds(pl.multiple_of(j * T, T), T) for j in keys]
            ks, vs = [k_ref[r, :] for r in rows], [v_ref[r, :] for r in rows]
            kms = [[jnp.where(hm, kj, jnp.zeros_like(kj)) for hm in masks] for kj in ks]
            units = [(n, h) for n in range(len(pairs)) for h in range(2)]
            qks = {(n, h): _nt(qms[pairs[n][0], h], ks[pairs[n][1]]) for n, h in units}
            das = {(n, h): _nt(doms[pairs[n][0], h], vs[pairs[n][1]]) for n, h in units}
            lbs, l1ms = {}, {}
            for u in units:
                lbs[u], l1m = _log_terms(qks[u])
                l1ms[u] = jnp.where(causal, l1m, 0.0) if pairs[u[0]][2] else l1m
            pins = {u: _hilo_dot(l1ms[u], upto) for u in units}
            sums = dict(sums)
            a_s, dls, cps = {}, {}, {}
            for n, h in units:
                b, _, diag = pairs[n]
                cl, cp = sums[b, h]
                a = jnp.exp2(lbs[n, h] + (ctots[b, h] - cl) - pins[n, h])
                if diag:
                    a = jnp.where(causal, a, 0.0)
                a_s[n, h] = a.astype(BF16)
                dls[n, h] = das[n, h] * a
                cps[n, h] = cp
                sums[b, h] = (cl + jnp.sum(l1ms[n, h], axis=1, keepdims=True),
                              cp + jnp.sum(dls[n, h], axis=1, keepdims=True))
            pexs = {u: _hilo_dot(dls[u], before) for u in units}
            dzbs = {}
            for u in units:
                dz = dls[u] - jnp.exp2(lbs[u]) * (dls[u] + pexs[u] + cps[u])
                if pairs[u[0]][2]:
                    dz = jnp.where(causal, dz, 0.0)
                dzbs[u] = dz.astype(BF16)
            dqs = list(dqs)
            for n, h in units:
                dqs[pairs[n][0]] = dqs[pairs[n][0]] + _nn(dzbs[n, h], kms[pairs[n][1]][h])
            for key, r in enumerate(rows):
                mine = [(n, h) for n, h in units if pairs[n][1] == key]
                dk_acc[r, :] += functools.reduce(jnp.add, [_tn(dzbs[u], qms[pairs[u[0]][0], u[1]]) for u in mine])
                dv_acc[r, :] += functools.reduce(jnp.add, [_tn(a_s[u], doms[pairs[u[0]][0], u[1]]) for u in mine])
            return sums, tuple(dqs)

        zero = jnp.zeros((T, 1), F32)
        sums = {(b, h): (zero, zero) for b in range(QB) for h in range(2)}
        dqs = tuple(jnp.zeros((T, 2 * DH), F32) for _ in range(QB))
        sums, dqs = lax.fori_loop(
            0, i2, lambda j, c: blocks([j], [(0, 0, False), (1, 0, False)], c[0], c[1]), (sums, dqs))
        _, dqs = blocks([i2, i2 + 1], [(0, 0, True), (1, 0, False), (1, 1, True)], sums, dqs)
        for b in range(QB):
            dq_ref[b * T:(b + 1) * T, :] = (dqs[b] * SCALE).astype(BF16)

        @pl.when(step == nq - 1)
        def _():
            dk_ref[...] = (dk_acc[...] * SCALE).astype(BF16)
            dv_ref[...] = dv_acc[...].astype(BF16)

    blk = pl.BlockSpec((QB * T, 2 * DH), lambda p, i: (i, p))
    full = pl.BlockSpec((S, 2 * DH), lambda p, i: (0, p))
    return _call(
        body, (q, k, v, do, ctot), name="attn_bwd", grid=(SBW // (2 * DH), nq),
        in_specs=[blk, full, full, blk, blk], out_specs=[blk, full, full],
        out_shape=[_sds((S, SBW), BF16), _sds((S, SBW), BF16), _sds((S, SBW), BF16)],
        scratch_shapes=[pltpu.VMEM((S, 2 * DH), F32), pltpu.VMEM((S, 2 * DH), F32)],
        compiler_params=_params(("arbitrary", "arbitrary"), 40), after=after)


def _pool_counts(first_row, tm):
    pos = first_row + lax.broadcasted_iota(jnp.int32, (tm, 1), 0)
    return [jnp.minimum(pos + 1, w).astype(F32) for w in POOL_WINDOWS]


def _mix_out(h, xp, o_sb, gp, gs, w_group, scale, w_bp, w_ba, w_out, exchange=None):
    tm = 512

    def body(h_ref, xp_ref, o_ref, gp_ref, gs_ref, wg_hbm, sc_ref, wbp_hbm, wba_hbm, wo_hbm,
             h2_ref, pm_ref, p_ref, yp_ref, ys_ref, m_ref, halo, wg_ref, wbp_ref, wba_ref, wo_ref):
        _stage([(wg_hbm, wg_ref), (wbp_hbm, wbp_ref), (wba_hbm, wba_ref), (wo_hbm, wo_ref)])
        i = pl.program_id(0)

        @pl.when(i == 0)
        def _():
            halo[...] = jnp.zeros_like(halo)

        xp = xp_ref[...]
        ext = jnp.concatenate([halo[...], xp], axis=0)
        halo[...] = xp[tm - HALO:, :]
        counts = _pool_counts(i * tm, tm)
        for gi in range(len(POOL_WINDOWS)):
            lanes = slice(gi * PG, (gi + 1) * PG)
            win = ext[:, lanes]
            for step in range(gi + 1):
                win = win + pltpu.roll(win, 1 << step, 0)
            pm = (win[HALO:, :] / counts[gi] - xp[:, lanes]).astype(BF16)
            pm_ref[:, lanes] = pm
            p_ref[:, lanes] = (_nn(pm, wg_ref[gi]) * sc_ref[:, lanes]).astype(BF16)
        pb = p_ref[...]
        ob = o_ref[...]
        for j in range(NSH):
            cols = slice(j * (D // NSH), (j + 1) * (D // NSH))
            yp = _nn(pb, wbp_ref[j])
            ys = _nn(ob, wba_ref[j])
            yp_ref[:, cols] = yp.astype(BF16)
            ys_ref[:, cols] = ys.astype(BF16)
            m_ref[:, cols] = (gp_ref[:, cols].astype(F32) * yp + gs_ref[:, cols].astype(F32) * ys).astype(BF16)
        h2_ref[...] = h_ref[...] + _nn(m_ref[...], wo_ref[...])

    return _call(
        body, (h, xp, o_sb, gp, gs, w_group, scale, w_bp, w_ba, w_out), name="mix_out", grid=(S // tm,),
        in_specs=[_rows(tm, D), _rows(tm, PW), _rows(tm, SBW), _rows(tm, D), _rows(tm, D),
                  _ANY, _fixed((1, PW)), _ANY, _ANY, _ANY],
        out_specs=[_rows(tm, D), _rows(tm, PW), _rows(tm, PW), _rows(tm, D), _rows(tm, D), _rows(tm, D)],
        out_shape=[_sds((S, D), F32), _sds((S, PW), BF16), _sds((S, PW), BF16), _sds((S, D), BF16),
                   _sds((S, D), BF16), _sds((S, D), BF16)],
        scratch_shapes=[pltpu.VMEM((HALO, PW), F32)] + _vmem_like(w_group, w_bp, w_ba, w_out),
        compiler_params=_params(("arbitrary",), 48), free=(5, 6), exchange=exchange)


def _mix_bwd_out(dh, gp, gs, yp, ys, pm, w_group, scale, w_bp, w_ba, w_out, exchange=None):
    tm = 512
    nt = S // tm

    def body(dh_ref, gp_ref, gs_ref, yp_ref, ys_ref, pm_ref, wg_hbm, sc_ref, wbp_hbm, wba_hbm, wo_hbm,
             dlg_ref, dyp_ref, dys_ref, do_ref, dyg_ref, dxp_ref, dsc_ref, halo, wg_ref, wbp_ref, wba_ref, wo_ref):
        _stage([(wg_hbm, wg_ref), (wbp_hbm, wbp_ref), (wba_hbm, wba_ref), (wo_hbm, wo_ref)])
        step = pl.program_id(0)

        @pl.when(step == 0)
        def _():
            halo[...] = jnp.zeros_like(halo)
            dsc_ref[...] = jnp.zeros_like(dsc_ref)

        dm = _nt(dh_ref[...].astype(BF16), wo_ref[...])
        gp = gp_ref[...].astype(F32)
        gs = gs_ref[...].astype(F32)
        yp = yp_ref[...].astype(F32)
        ys = ys_ref[...].astype(F32)
        dlg_ref[:, :D] = (dm * yp * gp * (1.0 - gp)).astype(BF16)
        dlg_ref[:, D:] = (dm * ys * gs * (1.0 - gs)).astype(BF16)
        dyp_ref[...] = (dm * gp).astype(BF16)
        dys_ref[...] = (dm * gs).astype(BF16)
        dp = jnp.zeros((tm, PW), F32)
        do = jnp.zeros((tm, SBW), F32)
        for j in range(NSH):
            cols = slice(j * (D // NSH), (j + 1) * (D // NSH))
            dp = dp + _nt(dyp_ref[:, cols], wbp_ref[j])
            do = do + _nt(dys_ref[:, cols], wba_ref[j])
        do_ref[...] = do.astype(BF16)
        counts = _pool_counts((nt - 1 - step) * tm, tm)
        dscale = []
        for gi in range(len(POOL_WINDOWS)):
            lanes = slice(gi * PG, (gi + 1) * PG)
            dpg = dp[:, lanes]
            dscale.append(jnp.sum(dpg * _nn(pm_ref[:, lanes], wg_ref[gi]), axis=0, keepdims=True))
            dyg = (dpg * sc_ref[:, lanes]).astype(BF16)
            dyg_ref[:, lanes] = dyg
            dpm = _nt(dyg, wg_ref[gi])
            per = dpm / counts[gi]
            win = jnp.concatenate([per, halo[:, lanes]], axis=0)
            halo[:, lanes] = per[:HALO, :]
            for s in range(gi + 1):
                win = win + pltpu.roll(win, tm + HALO - (1 << s), 0)
            dxp_ref[:, lanes] = (win[:tm, :] - dpm).astype(BF16)
        dsc_ref[...] += jnp.concatenate(dscale, axis=1)

    rev = lambda width: pl.BlockSpec((tm, width), lambda i: (nt - 1 - i, 0))
    return _call(
        body, (dh, gp, gs, yp, ys, pm, w_group, scale, w_bp, w_ba, w_out), name="mix_bwd_out", grid=(nt,),
        in_specs=[rev(D), rev(D), rev(D), rev(D), rev(D), rev(PW), _ANY, _fixed((1, PW)), _ANY, _ANY, _ANY],
        out_specs=[rev(2 * D), rev(D), rev(D), rev(SBW), rev(PW), rev(PW), _fixed((1, PW))],
        out_shape=[_sds((S, 2 * D), BF16), _sds((S, D), BF16), _sds((S, D), BF16), _sds((S, SBW), BF16),
                   _sds((S, PW), BF16), _sds((S, PW), BF16), _sds((1, PW), F32)],
        scratch_shapes=[pltpu.VMEM((HALO, PW), F32)] + _vmem_like(w_group, w_bp, w_ba, w_out),
        compiler_params=_params(("arbitrary",), 48), exchange=exchange)


def _mix_bwd_in(dh, h, gain, pieces, w_in, exchange=None):
    tm = 512
    widths = [p.shape[1] for p in pieces]

    def body(dh_ref, h_ref, g_ref, *rest):
        piece_refs, (w_hbm, dx_ref, dg_ref, dp_ref, w_ref) = rest[:len(pieces)], rest[len(pieces):]
        _stage([(w_hbm, w_ref)])
        at = 0
        for ref, width in zip(piece_refs, widths):
            dp_ref[:, at:at + width] = ref[...]
            at += width
        du = jnp.zeros((tm, D), F32)
        for j in range(NSH):
            du = du + _nt(dp_ref[:, j * D:(j + 1) * D], w_ref[j])
        r, hr = _rms(h_ref[...])
        dx, dgain = _rms_bwd(du, hr, r, g_ref[...])
        dx_ref[...] = dh_ref[...] + dx

        @pl.when(pl.program_id(0) == 0)
        def _():
            dg_ref[...] = jnp.zeros_like(dg_ref)

        dg_ref[...] += dgain

    return _call(
        body, (dh, h, gain, *pieces, w_in), name="mix_bwd_in", grid=(S // tm,),
        in_specs=[_rows(tm, D), _rows(tm, D), _fixed((1, D))] + [_rows(tm, w) for w in widths] + [_ANY],
        out_specs=[_rows(tm, D), _fixed((1, D)), _rows(tm, 4 * D)],
        out_shape=[_sds((S, D), F32), _sds((1, D), F32), _sds((S, 4 * D), BF16)],
        scratch_shapes=_vmem_like(w_in),
        compiler_params=_params(("arbitrary",), 48), exchange=exchange)


def _wgrad(a, b, nblk, ti, name, out_dtype=BF16, exchange=None, after=()):
    ka, n = a.shape[1], b.shape[1]
    ns = n // nblk

    def body(a_ref, b_ref, o_ref):
        o_ref[...] = _tn(a_ref[...].astype(BF16), b_ref[...].astype(BF16)).astype(out_dtype)

    res = _call(
        body, (a, b), name=name, grid=(nblk, ka // ti),
        in_specs=[pl.BlockSpec((S, ti), lambda j, i: (0, i)), pl.BlockSpec((S, ns), lambda j, i: (0, j))],
        out_specs=[pl.BlockSpec((None, ti, ns), lambda j, i: (j, i, 0))],
        out_shape=[_sds((nblk, ka, ns), out_dtype)],
        compiler_params=_params(("arbitrary", "arbitrary"), 56), exchange=exchange, after=after)
    return res[0] if exchange is None else (res[0][0], res[1])


def _wgrad_groups(pm, dyg):
    def body(a_ref, b_ref, o_ref):
        o_ref[...] = _tn(a_ref[...], b_ref[...])

    col = pl.BlockSpec((S, PG), lambda g: (0, g))
    return pl.pallas_call(
        body, name="wgrad_groups", grid=(PW // PG,),
        in_specs=[col, col], out_specs=pl.BlockSpec((None, PG, PG), lambda g: (g, 0, 0)),
        out_shape=_sds((PW // PG, PG, PG), F32),
        compiler_params=_params(("arbitrary",), 32),
    )(*_in_hbm([pm, dyg]))


def _place():
    x, y, c = lax.axis_index("x"), lax.axis_index("y"), lax.axis_index("c")
    chips = [(1 - x, y), (x, 1 - y), (1 - x, 1 - y)]
    return x, y, c, chips


def _remote(src, dst, ssem, rsem, dev):
    return pltpu.make_async_remote_copy(src_ref=src, dst_ref=dst, send_sem=ssem, recv_sem=rsem,
                                        device_id=dev, device_id_type=MESH)


def _cast_into_block(w, me_idx, name):
    rows, cols = w.shape
    tr = _row_block(rows)

    def body(me_ref, w_ref, o_ref):
        o_ref[...] = w_ref[...].astype(BF16)

    return pl.pallas_call(
        body, name=name, out_shape=_sds((NSH, rows, cols), BF16),
        grid_spec=pltpu.PrefetchScalarGridSpec(
            num_scalar_prefetch=1, grid=(rows // tr,),
            in_specs=[pl.BlockSpec((tr, cols), lambda r, me: (r, 0))],
            out_specs=pl.BlockSpec((None, tr, cols), lambda r, me: (me[0], r, 0))),
        compiler_params=_params(("arbitrary",), 32),
    )(me_idx, w)


def _ex_gather(bufs):
    n = len(bufs)
    per = 8

    def plan(outs, ssem, rsem, w):
        x, y, c, _ = _place()
        sib, nbr_x, nbr_y = (x, y, 1 - c), (1 - x, y, c), (x, 1 - y, c)
        half = outs[w].shape[1] // 2
        quarter = half // 2
        sem = lambda k: (ssem.at[per * w + k], rsem.at[per * w + k])
        rows = lambda blk, start, size: outs[w].at[blk, pl.ds(start, size)]
        mine = rows(2 * x + y, c * half, half)
        from_x = rows(2 * (1 - x) + y, c * half, half)
        from_y = rows(2 * x + (1 - y), c * half, half)
        diag = 2 * (1 - x) + (1 - y)
        pass_y = rows(2 * (1 - x) + y, c * half, quarter)
        pass_x = rows(2 * x + (1 - y), c * half + quarter, quarter)
        diag_0, diag_1 = rows(diag, c * half, quarter), rows(diag, c * half + quarter, quarter)
        first = [_remote(mine, mine, *sem(0), nbr_x), _remote(mine, mine, *sem(1), nbr_y)]
        arrivals = [
            (_remote(from_x, from_x, *sem(0), nbr_x),
             [_remote(pass_y, pass_y, *sem(2), nbr_y), _remote(from_x, from_x, *sem(4), sib)]),
            (_remote(from_y, from_y, *sem(1), nbr_y),
             [_remote(pass_x, pass_x, *sem(3), nbr_x), _remote(from_y, from_y, *sem(5), sib)]),
            (_remote(diag_0, diag_0, *sem(2), nbr_y), [_remote(diag_0, diag_0, *sem(6), sib)]),
            (_remote(diag_1, diag_1, *sem(3), nbr_x), [_remote(diag_1, diag_1, *sem(7), sib)]),
        ]
        other = (1 - c) * half
        from_sibling = [
            _remote(rows(2 * (1 - x) + y, other, half), rows(2 * (1 - x) + y, other, half), *sem(4), sib),
            _remote(rows(2 * x + (1 - y), other, half), rows(2 * x + (1 - y), other, half), *sem(5), sib),
            _remote(rows(diag, other, quarter), rows(diag, other, quarter), *sem(6), sib),
            _remote(rows(diag, other + quarter, quarter), rows(diag, other + quarter, quarter), *sem(7), sib),
        ]
        return first, arrivals, from_sibling

    def start(ins, outs, ssem, rsem):
        x, y, c, _ = _place()
        for w in range(n):
            half = outs[w].shape[1] // 2
            mine = outs[w].at[2 * x + y, pl.ds(c * half, half)]
            _remote(mine, mine, ssem.at[per * w], rsem.at[per * w], (1 - x, y, c)).start()
            _remote(mine, mine, ssem.at[per * w + 1], rsem.at[per * w + 1], (x, 1 - y, c)).start()

    def finish(ins, outs, ssem, rsem):
        plans = [plan(outs, ssem, rsem, w) for w in range(n)]
        started = []
        for direct in (True, False):
            for first, arrivals, _ in plans:
                for arrived, onward in (arrivals[:2] if direct else arrivals[2:]):
                    arrived.wait_recv()
                    for cp in onward:
                        cp.start()
                    started += onward
        for first, _, from_sibling in plans:
            for cp in from_sibling:
                cp.wait_recv()
            started += first
        for cp in started:
            cp.wait_send()

    return Exchange(bufs, [_sds(b.shape, b.dtype) for b in bufs], {w: w for w in range(n)}, per * n, start, finish)


def _ex_gather_direct(bufs):
    n = len(bufs)

    def copies(outs, ssem, rsem, only_first=False):
        x, y, c, chips = _place()
        me, sib = 2 * x + y, (x, y, 1 - c)
        first, relay, last = [], [], []
        for w in range(n):
            half = outs[w].shape[1] // 2
            mine = outs[w].at[me, pl.ds(c * half, half)]
            for k, (px, py) in enumerate(chips):
                sems = (ssem.at[6 * w + k], rsem.at[6 * w + k])
                sib_sems = (ssem.at[6 * w + 3 + k], rsem.at[6 * w + 3 + k])
                first.append(_remote(mine, mine, *sems, (px, py, c)))
                if only_first:
                    continue
                got = outs[w].at[2 * px + py, pl.ds(c * half, half)]
                relay.append((_remote(got, got, *sems, (px, py, c)), _remote(got, got, *sib_sems, sib)))
                theirs = outs[w].at[2 * px + py, pl.ds((1 - c) * half, half)]
                last.append(_remote(theirs, theirs, *sib_sems, sib))
        return first, relay, last

    def start(ins, outs, ssem, rsem):
        for cp in copies(outs, ssem, rsem, only_first=True)[0]:
            cp.start()

    def finish(ins, outs, ssem, rsem):
        first, relay, last = copies(outs, ssem, rsem)
        for arrived, onward in relay:
            arrived.wait_recv()
            onward.start()
        for cp in last:
            cp.wait_recv()
        for cp in first:
            cp.wait_send()
        for _, onward in relay:
            onward.wait_send()

    return Exchange(bufs, [_sds(b.shape, b.dtype) for b in bufs], {w: w for w in range(n)}, 6 * n, start, finish)


def _simple_exchange(arrays, landing, aliases, make_copies):
    def start(ins, outs, ssem, rsem):
        for cp, _ in make_copies(ins, outs, ssem, rsem, False):
            cp.start()

    def finish(ins, outs, ssem, rsem):
        cps = make_copies(ins, outs, ssem, rsem, True)
        for _, landed in cps:
            landed.wait_recv()
        for cp, _ in cps:
            cp.wait_send()

    return Exchange(arrays, landing, aliases, len(arrays) * 3, start, finish)


def _ex_pair_swap(grads):
    def make(ins, outs, ssem, rsem, landing):
        x, y, c, _ = _place()
        cps = [_remote(ins[w].at[:, 1 - c], outs[w], ssem.at[w], rsem.at[w], (x, y, 1 - c))
               for w in range(len(grads))]
        return [(cp, cp) for cp in cps]

    return _simple_exchange(grads, [_sds((NSH,) + g.shape[2:], g.dtype) for g in grads], {}, make)


def _ex_scatter(parts):
    def make(ins, outs, ssem, rsem, landing):
        x, y, c, chips = _place()
        out = []
        for w in range(len(parts)):
            for k, (px, py) in enumerate(chips):
                sems = (ssem.at[3 * w + k], rsem.at[3 * w + k])
                out.append((_remote(ins[w].at[2 * px + py], outs[w].at[k], *sems, (px, py, c)),
                            _remote(outs[w].at[k], outs[w].at[k], *sems, (px, py, c)) if landing else None))
        return out

    return _simple_exchange(parts, [_sds((3,) + p.shape[1:], p.dtype) for p in parts], {}, make)


def _ex_relay(bufs):
    def make(ins, outs, ssem, rsem, landing):
        x, y, c, chips = _place()
        sib = (x, y, 1 - c)
        out = []
        for w in range(len(bufs)):
            half = outs[w].shape[1] // 2
            for k, (px, py) in enumerate(chips):
                sems = (ssem.at[3 * w + k], rsem.at[3 * w + k])
                have = outs[w].at[2 * px + py, pl.ds(c * half, half)]
                miss = outs[w].at[2 * px + py, pl.ds((1 - c) * half, half)]
                out.append((_remote(have, have, *sems, sib), _remote(miss, miss, *sems, sib) if landing else None))
        return out

    return _simple_exchange(bufs, [_sds(b.shape, b.dtype) for b in bufs], {w: w for w in range(len(bufs))}, make)


def _ex_share(bufs):
    def make(ins, outs, ssem, rsem, landing):
        x, y, c, _ = _place()
        sib = (x, y, 1 - c)
        return [(_remote(outs[w].at[c], outs[w].at[c], ssem.at[w], rsem.at[w], sib),
                 _remote(outs[w].at[1 - c], outs[w].at[1 - c], ssem.at[w], rsem.at[w], sib) if landing else None)
                for w in range(len(bufs))]

    return _simple_exchange(bufs, [_sds(b.shape, b.dtype) for b in bufs], {w: w for w in range(len(bufs))}, make)


def _gather_small(block, ex):
    m_per, n = block.shape
    na, nl = len(ex.arrays), len(ex.landing)

    def body(x_ref, *refs):
        e_in, out_ref, e_out = refs[:na], refs[na], refs[na + 1:na + 1 + nl]
        ssem, rsem, lsem, e_ssem, e_rsem = refs[na + 1 + nl:]
        ex.start(e_in, e_out, e_ssem, e_rsem)
        x, y, c, chips = _place()
        me, sib = (x, y, c), (x, y, 1 - c)

        def rows(px, py, pc):
            return out_ref.at[pl.ds((4 * px + 2 * py + pc) * m_per, m_per), :]

        def copy(k, blk, to, src=None):
            return _remote(rows(*blk) if src is None else src, rows(*blk), ssem.at[k], rsem.at[k], to)

        mine = pltpu.make_async_copy(x_ref, rows(*me), lsem)
        mine.start()
        first = [copy(0, me, sib, src=x_ref)]
        first += [copy(1 + j, me, (*chip, c), src=x_ref) for j, chip in enumerate(chips)]
        for cp in first:
            cp.start()
        passed = [copy(4 + j, (*chip, c), sib) for j, chip in enumerate(chips)]
        for j, chip in enumerate(chips):
            copy(1 + j, (*chip, c), me).wait_recv()
            passed[j].start()
        copy(0, sib, me).wait_recv()
        for j, chip in enumerate(chips):
            copy(4 + j, (*chip, 1 - c), me).wait_recv()
        for cp in first + passed:
            cp.wait_send()
        mine.wait()
        ex.finish(e_in, e_out, e_ssem, e_rsem)

    outs = pl.pallas_call(
        body, name="gather_small", out_shape=[jax.ShapeDtypeStruct((8 * m_per, n), block.dtype)] + ex.landing,
        in_specs=[_VM] + [_ANY] * na, out_specs=[_VM] + [_ANY] * nl,
        scratch_shapes=[pltpu.SemaphoreType.DMA((7,)), pltpu.SemaphoreType.DMA((7,)), pltpu.SemaphoreType.DMA]
        + [pltpu.SemaphoreType.DMA((ex.n_sems,))] * 2,
        input_output_aliases={1 + i: 1 + j for i, j in ex.aliases.items()},
    )(block, *_in_hbm(ex.arrays))
    return outs[0], outs[1:]


def _row_block(rows):
    return max(t for t in range(16, 257, 16) if rows % t == 0)


def _pair_sum(grad, got, c_idx, name):
    _, _, half, cols = grad.shape
    tr = _row_block(half)

    def body(c_ref, a_ref, b_ref, o_ref):
        o_ref[...] = (a_ref[...].astype(F32) + b_ref[...].astype(F32)).astype(BF16)

    return pl.pallas_call(
        body, name=name, out_shape=_sds((NSH, half, cols), BF16),
        grid_spec=pltpu.PrefetchScalarGridSpec(
            num_scalar_prefetch=1, grid=(NSH, half // tr),
            in_specs=[pl.BlockSpec((None, None, tr, cols), lambda j, r, c: (j, c[0], r, 0)),
                      pl.BlockSpec((None, tr, cols), lambda j, r, c: (j, r, 0))],
            out_specs=pl.BlockSpec((None, tr, cols), lambda j, r, c: (j, r, 0))),
        compiler_params=_params(("arbitrary", "arbitrary"), 32),
    )(c_idx, *_in_hbm([grad, got]))


def _chip_sum(own, got, place, name):
    _, half, cols = own.shape
    tr = _row_block(half)

    def body(place_ref, own_ref, got_ref, o_ref):
        acc = own_ref[...].astype(F32)
        for k in range(3):
            acc = acc + got_ref[k].astype(F32)
        o_ref[...] = acc

    return pl.pallas_call(
        body, name=name, out_shape=_sds((2, half, cols), F32),
        grid_spec=pltpu.PrefetchScalarGridSpec(
            num_scalar_prefetch=1, grid=(half // tr,),
            in_specs=[pl.BlockSpec((None, tr, cols), lambda r, p: (p[0], r, 0)),
                      pl.BlockSpec((3, tr, cols), lambda r, p: (0, r, 0))],
            out_specs=pl.BlockSpec((None, tr, cols), lambda r, p: (p[1], r, 0))),
        compiler_params=_params(("arbitrary",), 32),
    )(place, *_in_hbm([own, got]))


def _adamw_math(w, g, m, v):
    m = B1 * m + (1.0 - B1) * g
    v = B2 * v + (1.0 - B2) * (g * g)
    m_hat = m / (1.0 - B1 ** STEP)
    v_hat = v / (1.0 - B2 ** STEP)
    return -LR * (m_hat / (jnp.sqrt(v_hat) + AEPS) + WD * w), m, v


def _adamw(w, g, m, v, name, after=()):
    rows, cols = w.shape
    tr = _row_block(rows)

    def body(w_ref, g_ref, m_ref, v_ref, go_ref, d_ref, nm_ref, nv_ref):
        g = g_ref[...]
        go_ref[...] = g
        d_ref[...], nm_ref[...], nv_ref[...] = _adamw_math(w_ref[...], g, m_ref[...], v_ref[...])

    blk = pl.BlockSpec((tr, cols), lambda r: (r, 0))
    return _call(
        body, (w, g, m, v), name=name, grid=(rows // tr,), out_shape=[_sds(w.shape, F32)] * 4,
        in_specs=[blk] * 4, out_specs=[blk] * 4,
        compiler_params=_params(("arbitrary",), 32), free=(0, 2, 3), after=after)


def _small_update(gathered, w, m, v):
    rows = w.shape[0]

    def body(ga_ref, w_ref, m_ref, v_ref, g_ref, d_ref, nm_ref, nv_ref):
        g = ga_ref[0:rows, :]
        for dev in range(1, 8):
            g = g + ga_ref[dev * rows:(dev + 1) * rows, :]
        g_ref[...] = g
        d_ref[...], nm_ref[...], nv_ref[...] = _adamw_math(w_ref[...], g, m_ref[...], v_ref[...])

    return pl.pallas_call(
        body, name="small_update", out_shape=[jax.ShapeDtypeStruct(w.shape, F32)] * 4,
        in_specs=[_VM] * 4, out_specs=[_VM] * 4,
    )(gathered, w, m, v)


SMALL = ("ffn1_norm", "mix_norm", "ffn2_norm", "final_norm", "pool_scale", "pool_w_group", "loss")
BIG = ("ffn1_w_gate_up", "ffn1_w_down", "w_in", "w_branch_pool", "w_branch_attn", "w_out",
       "ffn2_w_gate_up", "ffn2_w_down")
ORDER = ("ffn1_norm", "ffn1_w_gate_up", "ffn1_w_down", "mix_norm", "w_in", "pool_w_group", "pool_scale",
         "w_branch_pool", "w_branch_attn", "w_out", "ffn2_norm", "ffn2_w_gate_up", "ffn2_w_down", "final_norm")
SMALL_ROWS = 560


def _pack_small(t):
    parts = []
    for k in SMALL:
        rows = t[k].reshape(-1, 128) if k in t else jnp.zeros((1, 128), F32)
        parts.append(jnp.pad(rows, ((0, -rows.shape[0] % 8), (0, 0))))
    packed = jnp.concatenate(parts, axis=0)
    assert packed.shape == (SMALL_ROWS, 128), packed.shape
    return packed


def _unpack_small(packed, like):
    out, at = {}, 0
    for k in SMALL:
        n = like[k].size // 128 if k in like else 1
        out[k] = packed[at:at + n].reshape(like[k].shape) if k in like else packed[at, 0]
        at += n + (-n % 8)
    return out


def _halves(g):
    return g.reshape(NSH, 2, g.shape[1] // 2, g.shape[2])


def kernel(x, ffn1_norm, ffn1_w_gate_up, ffn1_w_down, mix_norm, w_in, pool_w_group, pool_scale, w_branch_pool, w_branch_attn, w_out, ffn2_norm, ffn2_w_gate_up, ffn2_w_down, final_norm, loss_target, m_ffn1_norm, m_ffn1_w_gate_up, m_ffn1_w_down, m_mix_norm, m_w_in, m_pool_w_group, m_pool_scale, m_w_branch_pool, m_w_branch_attn, m_w_out, m_ffn2_norm, m_ffn2_w_gate_up, m_ffn2_w_down, m_final_norm, v_ffn1_norm, v_ffn1_w_gate_up, v_ffn1_w_down, v_mix_norm, v_w_in, v_pool_w_group, v_pool_scale, v_w_branch_pool, v_w_branch_attn, v_w_out, v_ffn2_norm, v_ffn2_w_gate_up, v_ffn2_w_down, v_final_norm):
    wts = dict(ffn1_norm=ffn1_norm, ffn1_w_gate_up=ffn1_w_gate_up, ffn1_w_down=ffn1_w_down, mix_norm=mix_norm,
               w_in=w_in, pool_w_group=pool_w_group, pool_scale=pool_scale, w_branch_pool=w_branch_pool,
               w_branch_attn=w_branch_attn, w_out=w_out, ffn2_norm=ffn2_norm, ffn2_w_gate_up=ffn2_w_gate_up,
               ffn2_w_down=ffn2_w_down, final_norm=final_norm)
    mom = dict(ffn1_norm=m_ffn1_norm, ffn1_w_gate_up=m_ffn1_w_gate_up, ffn1_w_down=m_ffn1_w_down,
               mix_norm=m_mix_norm, w_in=m_w_in, pool_w_group=m_pool_w_group, pool_scale=m_pool_scale,
               w_branch_pool=m_w_branch_pool, w_branch_attn=m_w_branch_attn, w_out=m_w_out,
               ffn2_norm=m_ffn2_norm, ffn2_w_gate_up=m_ffn2_w_gate_up, ffn2_w_down=m_ffn2_w_down,
               final_norm=m_final_norm)
    var = dict(ffn1_norm=v_ffn1_norm, ffn1_w_gate_up=v_ffn1_w_gate_up, ffn1_w_down=v_ffn1_w_down,
               mix_norm=v_mix_norm, w_in=v_w_in, pool_w_group=v_pool_w_group, pool_scale=v_pool_scale,
               w_branch_pool=v_w_branch_pool, w_branch_attn=v_w_branch_attn, w_out=v_w_out,
               ffn2_norm=v_ffn2_norm, ffn2_w_gate_up=v_ffn2_w_gate_up, ffn2_w_down=v_ffn2_w_down,
               final_norm=v_final_norm)

    c_idx = lax.axis_index("c").astype(jnp.int32).reshape(1)
    me_idx = (2 * lax.axis_index("x") + lax.axis_index("y")).astype(jnp.int32).reshape(1)
    place = jnp.concatenate([me_idx, c_idx])
    x0, tgt = x[0], loss_target[0]
    wgrp = pool_w_group[0].astype(BF16)
    g1, gm, g2, gf = ffn1_norm, mix_norm, ffn2_norm, final_norm.reshape(1, D)
    grad, delta, new_m, new_v = {}, {}, {}, {}

    def pair_sums(keys, parts, got):
        return [_pair_sum(parts[i], got[i], c_idx, "pair_sum_" + k) for i, k in enumerate(keys)]

    def chip_sums(keys, chip_parts, owned):
        return [_chip_sum(chip_parts[i], owned[i], place, "chip_sum_" + k) for i, k in enumerate(keys)]

    def adamw(k, after=()):
        outs = _adamw(wts[k][0], grad[k][0], mom[k][0], var[k][0], "adamw_" + k, after=after)
        grad[k], delta[k], new_m[k], new_v[k] = (o.reshape(wts[k].shape) for o in outs)

    own = {k: _cast_into_block(wts[k][0], me_idx, "cast_" + k) for k in BIG}
    first, late = ("ffn1_w_gate_up", "ffn1_w_down"), ("w_branch_pool", "w_branch_attn", "w_out",
                                                       "ffn2_w_gate_up", "ffn2_w_down")
    full = dict(zip(first, _exchange_alone(_ex_gather([own[k] for k in first]), "gather_ffn1")))
    wgu1, wd1 = full["ffn1_w_gate_up"], full["ffn1_w_down"].reshape(DFF, D)
    (h1, n1, gu1, a1), (win,) = _ffn_fwd(x0, g1, wgu1, wd1, "ffn1_fwd", exchange=_ex_gather_direct([own["w_in"]]))
    sems_l, thru_l, token_l = _gather_start([own[k_] for k_ in late], [h1], "gather_late_start")
    u, xp, q, k, v, gp, gs = _mix_in(h1, gm, win, after=(token_l,))
    o_sb, ctot = _attn_fwd(q, k, v)
    arrived = _gather_wait(sems_l, thru_l, [o_sb], "gather_late_wait")
    wbp, wba, wout = _exchange_alone(_ex_relay(arrived[:3]), "relay_mix")
    wout = wout.reshape(D, D)
    (h2, pm, p, yp, ys, mm), (wgu2, wd2) = _mix_out(h1, xp, o_sb, gp, gs, wgrp, pool_scale, wbp, wba, wout,
                                                    exchange=_ex_relay(arrived[3:]))
    wd2 = wd2.reshape(DFF, D)
    dh3, loss_row, d_gf, n3, gu3, a3 = _ffn_fwd(h2, g2, wgu2, wd2, "ffn2_fwd", head=(tgt, gf))

    def grad_gate_up(n, dgu, name, exchange=None):
        res = _wgrad(n, dgu, NSH, 512, name, exchange=exchange)
        return [_halves(res)] if exchange is None else ([_halves(res[0])], res[1])

    def grad_down(a, dh, name, exchange=None):
        res = _wgrad(a, dh, 1, FFS, name, exchange=exchange)
        halves = lambda g: [_halves(g.reshape(NSH, DFF // NSH, D))]
        return halves(res) if exchange is None else (halves(res[0]), res[1])

    k_gu2, k_d2, k_gu1, k_d1, k_in = (("ffn2_w_gate_up",), ("ffn2_w_down",), ("ffn1_w_gate_up",),
                                      ("ffn1_w_down",), ("w_in",))
    dh2, dgu3, d_g2 = _ffn_bwd(dh3, h2, g2, gu3, wgu2, wd2, "ffn2_bwd")
    pa = grad_gate_up(n3, dgu3, "wgrad_gu2") + grad_down(a3, dh3, "wgrad_d2")
    (dlg, dyp, dys, do_sb, dyg, dxp, d_scale), got_a = _mix_bwd_out(
        dh2, gp, gs, yp, ys, pm, wgrp, pool_scale, wbp, wba, wout, exchange=_ex_pair_swap(pa))
    chip_a = pair_sums(k_gu2 + k_d2, pa, got_a)
    kb = ("w_out", "w_branch_pool", "w_branch_attn")
    pb = [_halves(_wgrad(mm, dh2, 1, 512, "wgrad_out").reshape(NSH, D // NSH, D)),
          _halves(_wgrad(p, dyp, NSH, PW, "wgrad_bp")), _halves(_wgrad(o_sb, dys, NSH, SBW, "wgrad_ba"))]
    k_a, k_in = k_gu2 + k_d2, k_in + kb
    sems_a, thru_a, token_a = _scatter_start(chip_a, "scatter_a_start")
    dq, dk, dv = _attn_bwd(q, k, v, do_sb, ctot, after=(token_a,))
    chip_a, owned_a = _scatter_wait(sems_a, thru_a, [dq], "scatter_a_wait")
    halves_a = chip_sums(k_a, chip_a, owned_a)
    (dh1, d_gm, dproj), both_a = _mix_bwd_in(dh2, h1, gm, (dxp, dq, dk, dv, dlg), win, exchange=_ex_share(halves_a))
    for i, k_ in enumerate(k_a):
        grad[k_] = both_a[i].reshape(wts[k_].shape)

    p_in = [_halves(_wgrad(u, dproj, NSH, 512, "wgrad_in"))] + pb
    p_d1, got_in = grad_down(a1, dh1, "wgrad_d1", exchange=_ex_pair_swap(p_in))
    sems_in, thru_in, token_in = _scatter_start(pair_sums(k_in, p_in, got_in), "scatter_in_start")
    dgu1, got_d1 = _ffn_bwd_act(dh1, gu1, wd1, "ffn1_bwd_act", exchange=_ex_pair_swap(p_d1), after=(token_in,))
    sems_d1, thru_d1, token_d1 = _scatter_start(pair_sums(k_d1, p_d1, got_d1), "scatter_d1_start")
    p_gu1 = [_halves(_wgrad(n1, dgu1, NSH, 512, "wgrad_gu1", after=(token_in, token_d1)))]
    chip_in, owned_in = _scatter_wait(sems_in, thru_in, p_gu1, "scatter_in_wait")
    chip_d1, owned_d1 = _scatter_wait(sems_d1, thru_d1, p_gu1, "scatter_d1_wait")
    halves_in, halves_d1 = chip_sums(k_in, chip_in, owned_in), chip_sums(k_d1, chip_d1, owned_d1)
    landed = _exchange_alone(_join(_ex_pair_swap(p_gu1), _ex_share(halves_in)), "pair_swap_gu1")
    for i, k_ in enumerate(k_in):
        grad[k_] = landed[1 + i].reshape(wts[k_].shape)
    sems, thru, token = _scatter_start(pair_sums(k_gu1, p_gu1, landed[:1]), "scatter_gu1_start")
    for k_ in k_a + k_in:
        adamw(k_, after=(token,))
    dx, d_g1 = _ffn_bwd_in(dh1, x0, g1, dgu1, wgu1, "ffn1_bwd_in", after=(token,))
    chip_gu1, owned_gu1 = _scatter_wait(sems, thru, [dx] + [delta[k_] for k_ in k_a + k_in], "scatter_gu1_wait")

    small_g = dict(ffn1_norm=d_g1, mix_norm=d_gm, ffn2_norm=d_g2, final_norm=d_gf, pool_scale=d_scale,
                   pool_w_group=_wgrad_groups(pm, dyg), loss=loss_row)
    gathered, both = _gather_small(_pack_small(small_g), _ex_share(halves_d1 + chip_sums(k_gu1, chip_gu1, owned_gu1)))
    grad["ffn1_w_down"] = both[0].reshape(ffn1_w_down.shape)
    grad["ffn1_w_gate_up"] = both[1].reshape(ffn1_w_gate_up.shape)
    for k_ in k_d1 + k_gu1:
        adamw(k_)
    sg, sd, sm, sv = _small_update(gathered, _pack_small(wts), _pack_small(mom), _pack_small(var))
    sums = _unpack_small(sg, wts)
    loss = sums.pop("loss")
    grad.update(sums)
    for dst, packed in ((delta, sd), (new_m, sm), (new_v, sv)):
        vals = _unpack_small(packed, wts)
        vals.pop("loss")
        dst.update(vals)
    return (loss, dx[None], *[grad[k_] for k_ in ORDER], *[delta[k_] for k_ in ORDER],
            *[new_m[k_] for k_ in ORDER], *[new_v[k_] for k_ in ORDER])
```

```python
import functools

import jax
import jax.numpy as jnp
from jax import lax
from jax.experimental import pallas as pl
from jax.experimental.pallas import tpu as pltpu

F32 = jnp.float32
BF16 = jnp.bfloat16

S = 2048
D = 1024
DFF = 2816
FFS = 2 * DFF // 4
NSH = 4
PW = 512
PG = 128
POOL_WINDOWS = (2, 4, 8, 16)
HALO = 16
SBW = 512
DH = 64
EPS = 1e-6
SCALE = 0.125
LOG2E = 1.4426950408889634
TA = 256
QB = 2
MIB = 1024 * 1024

LR, B1, B2, AEPS, WD, STEP = 0.001, 0.9, 0.999, 1e-08, 0.01, 10

_VM = pl.BlockSpec(memory_space=pltpu.VMEM)
_ANY = pl.BlockSpec(memory_space=pl.ANY)
MESH = pl.DeviceIdType.MESH


def _nn(a, b):
    return jnp.dot(a, b, preferred_element_type=F32)


def _nt(a, b):
    return lax.dot_general(a, b, (((1,), (1,)), ((), ())), preferred_element_type=F32)


def _tn(a, b):
    return lax.dot_general(a, b, (((0,), (0,)), ((), ())), preferred_element_type=F32)


def _params(sem, vmem_mib):
    return pltpu.CompilerParams(dimension_semantics=sem, vmem_limit_bytes=vmem_mib * MIB)


def _rows(tm, width):
    return pl.BlockSpec((tm, width), lambda i: (i, 0))


def _fixed(shape):
    return pl.BlockSpec(shape, lambda *_: (0,) * len(shape))


def _sds(shape, dtype):
    return pltpu.HBM(shape, dtype)


def _in_hbm(args):
    return [pltpu.with_memory_space_constraint(a, pltpu.HBM) for a in args]


def _stage(pairs):
    @pl.when(pl.program_id(0) == 0)
    def _():
        for src, dst in pairs:
            pltpu.sync_copy(src, dst)


def _vmem_like(*arrays):
    return [pltpu.VMEM(a.shape, a.dtype) for a in arrays]


class Exchange:
    def __init__(self, arrays, landing, aliases, n_sems, start, finish):
        self.arrays, self.landing, self.aliases, self.n_sems = list(arrays), list(landing), dict(aliases), n_sems
        self.start, self.finish = start, finish


def _join(a, b):
    na, la = len(a.arrays), len(a.landing)

    def both(fa, fb):
        def run(ins, outs, ssem, rsem):
            fa(ins[:na], outs[:la], ssem.at[pl.ds(0, a.n_sems)], rsem.at[pl.ds(0, a.n_sems)])
            fb(ins[na:], outs[la:], ssem.at[pl.ds(a.n_sems, b.n_sems)], rsem.at[pl.ds(a.n_sems, b.n_sems)])
        return run

    aliases = {**a.aliases, **{na + i: la + j for i, j in b.aliases.items()}}
    return Exchange(a.arrays + b.arrays, a.landing + b.landing, aliases, a.n_sems + b.n_sems,
                    both(a.start, b.start), both(a.finish, b.finish))


def _call(body, args, *, name, grid, in_specs, out_specs, out_shape, scratch_shapes=(), compiler_params=None,
          exchange=None, free=(), after=()):
    args = [a if i in free else pltpu.with_memory_space_constraint(a, pltpu.HBM) for i, a in enumerate(args)]
    if exchange is None:
        n_in = len(in_specs)

        def plain(*refs):
            body(*refs[:n_in], *refs[n_in + len(after):])

        return pl.pallas_call(plain, name=name, grid=grid, in_specs=list(in_specs) + [_ANY] * len(after),
                              out_specs=out_specs, out_shape=out_shape, scratch_shapes=list(scratch_shapes),
                              compiler_params=compiler_params)(*args, *after)
    ex = exchange
    n_in, n_out, n_scr = len(in_specs), len(out_specs), len(scratch_shapes)
    na, nl = len(ex.arrays), len(ex.landing)

    def hosted(*refs):
        at = [0]

        def take(n):
            at[0] += n
            return refs[at[0] - n:at[0]]

        k_in, _, e_in, k_out, e_out, k_scr = take(n_in), take(len(after)), take(na), take(n_out), take(nl), take(n_scr)
        ssem, rsem = take(2)
        ids = [pl.program_id(a) for a in range(len(grid))]
        first = functools.reduce(jnp.logical_and, [i == 0 for i in ids])
        last = functools.reduce(jnp.logical_and, [i == g - 1 for i, g in zip(ids, grid)])

        @pl.when(first)
        def _():
            ex.start(e_in, e_out, ssem, rsem)

        body(*k_in, *k_out, *k_scr)

        @pl.when(last)
        def _():
            ex.finish(e_in, e_out, ssem, rsem)

    outs = pl.pallas_call(
        hosted, name=name, grid=grid,
        in_specs=list(in_specs) + [_ANY] * (len(after) + na), out_specs=list(out_specs) + [_ANY] * nl,
        out_shape=list(out_shape) + ex.landing,
        scratch_shapes=list(scratch_shapes) + [pltpu.SemaphoreType.DMA((ex.n_sems,))] * 2,
        input_output_aliases={n_in + len(after) + i: n_out + j for i, j in ex.aliases.items()},
        compiler_params=compiler_params,
    )(*args, *after, *_in_hbm(ex.arrays))
    return outs[:n_out], outs[n_out:]


def _exchange_alone(ex, name):
    def body(*refs):
        na, nl = len(ex.arrays), len(ex.landing)
        ex.start(refs[:na], refs[na:na + nl], refs[-2], refs[-1])
        ex.finish(refs[:na], refs[na:na + nl], refs[-2], refs[-1])

    return pl.pallas_call(
        body, name=name, in_specs=[_ANY] * len(ex.arrays), out_specs=[_ANY] * len(ex.landing),
        out_shape=ex.landing, scratch_shapes=[pltpu.SemaphoreType.DMA((ex.n_sems,))] * 2,
        input_output_aliases=ex.aliases,
    )(*_in_hbm(ex.arrays))


_HBM = pl.BlockSpec(memory_space=pltpu.HBM)
_SEM = pl.BlockSpec(memory_space=pltpu.SEMAPHORE)
_EFFECT = pltpu.SideEffectType.DATAFLOW_SIDE_EFFECTING


def _scatter_copies(srcs, lands, ssems, rsems):
    x, y, c, chips = _place()
    return [_remote(srcs[w].at[2 * px + py], lands[w].at[k], ssems[3 * w + k], rsems[3 * w + k], (px, py, c))
            for w in range(len(srcs)) for k, (px, py) in enumerate(chips)]


def _scatter_start(parts, name):
    n, ncp = len(parts), 3 * len(parts)
    lands = [lax.empty((3,) + p.shape[1:], p.dtype) for p in parts]

    def body(*refs):
        srcs, land_refs = refs[:n], refs[n:2 * n]
        ssems, rsems = refs[2 * n:2 * n + ncp], refs[2 * n + ncp:2 * n + 2 * ncp]
        for cp in _scatter_copies(srcs, land_refs, ssems, rsems):
            cp.start()
        token = refs[-1]
        token[...] = jnp.zeros_like(token)

    outs = pl.pallas_call(
        body, name=name,
        out_shape=([pltpu.SemaphoreType.DMA(())] * (2 * ncp) + [pltpu.HBM(a.shape, a.dtype) for a in parts + lands]
                   + [jax.ShapeDtypeStruct((8, 128), F32)]),
        in_specs=[_HBM] * (2 * n), out_specs=[_SEM] * (2 * ncp) + [_HBM] * (2 * n) + [_VM],
        input_output_aliases={i: 2 * ncp + i for i in range(2 * n)},
        compiler_params=pltpu.CompilerParams(has_side_effects=_EFFECT),
    )(*_in_hbm(parts), *_in_hbm(lands))
    sems, thru, token = outs[:2 * ncp], outs[2 * ncp:2 * ncp + 2 * n], outs[-1]
    return sems, thru, token


def _scatter_wait(sems, thru, after, name):
    n = len(thru) // 2
    ncp = 3 * n

    def body(*refs):
        srcs, land_refs = refs[:n], refs[n:2 * n]
        ssems, rsems = refs[2 * n:2 * n + ncp], refs[2 * n + ncp:2 * n + 2 * ncp]
        for cp in _scatter_copies(srcs, land_refs, ssems, rsems):
            cp.wait_send()
            cp.wait_recv()

    outs = pl.pallas_call(
        body, name=name, out_shape=[pltpu.HBM(a.shape, a.dtype) for a in thru],
        in_specs=[_HBM] * (2 * n) + [_SEM] * (2 * ncp) + [_ANY] * len(after), out_specs=[_HBM] * (2 * n),
        input_output_aliases={i: i for i in range(2 * n)},
        compiler_params=pltpu.CompilerParams(has_side_effects=_EFFECT),
    )(*thru, *sems, *after)
    return outs[:n], outs[n:]


def _gather_copies(bufs, ssems, rsems, sending):
    x, y, c, chips = _place()
    out = []
    for w, ref in enumerate(bufs):
        half = ref.shape[1] // 2
        for k, (px, py) in enumerate(chips):
            rows = ref.at[2 * x + y if sending else 2 * px + py, pl.ds(c * half, half)]
            out.append(_remote(rows, rows, ssems[3 * w + k], rsems[3 * w + k], (px, py, c)))
    return out


def _gather_start(bufs, after, name):
    n, ncp = len(bufs), 3 * len(bufs)

    def body(*refs):
        ssems, rsems = refs[n + len(after):n + len(after) + ncp], refs[n + len(after) + ncp:n + len(after) + 2 * ncp]
        for cp in _gather_copies(refs[:n], ssems, rsems, True):
            cp.start()
        token = refs[-1]
        token[...] = jnp.zeros_like(token)

    outs = pl.pallas_call(
        body, name=name,
        out_shape=([pltpu.SemaphoreType.DMA(())] * (2 * ncp) + [pltpu.HBM(a.shape, a.dtype) for a in bufs]
                   + [jax.ShapeDtypeStruct((8, 128), F32)]),
        in_specs=[_HBM] * n + [_ANY] * len(after), out_specs=[_SEM] * (2 * ncp) + [_HBM] * n + [_VM],
        input_output_aliases={i: 2 * ncp + i for i in range(n)},
        compiler_params=pltpu.CompilerParams(has_side_effects=_EFFECT),
    )(*_in_hbm(bufs), *after)
    return outs[:2 * ncp], outs[2 * ncp:2 * ncp + n], outs[-1]


def _gather_wait(sems, thru, after, name):
    n = len(thru)
    ncp = 3 * n

    def body(*refs):
        ssems, rsems = refs[n:n + ncp], refs[n + ncp:n + 2 * ncp]
        for cp in _gather_copies(refs[:n], ssems, rsems, True):
            cp.wait_send()
        for cp in _gather_copies(refs[:n], ssems, rsems, False):
            cp.wait_recv()

    return pl.pallas_call(
        body, name=name, out_shape=[pltpu.HBM(a.shape, a.dtype) for a in thru],
        in_specs=[_HBM] * n + [_SEM] * (2 * ncp) + [_ANY] * len(after), out_specs=[_HBM] * n,
        input_output_aliases={i: i for i in range(n)},
        compiler_params=pltpu.CompilerParams(has_side_effects=_EFFECT),
    )(*thru, *sems, *after)


def _rms(x):
    r = lax.rsqrt(jnp.mean(x * x, axis=-1, keepdims=True) + EPS)
    return r, x * r


def _rms_bwd(dn, xr, r, gain):
    dng = dn * gain
    dx = r * (dng - xr * jnp.mean(dng * xr, axis=-1, keepdims=True))
    return dx, jnp.sum(dn * xr, axis=0, keepdims=True)


def _ffn_fwd(x, gain, wgu, wd, name, exchange=None, head=None):
    tm = 256

    def body(x_ref, g_ref, wgu_hbm, wd_hbm, *rest):
        if head is None:
            h_ref, n_ref, gu_ref, a_ref, wgu_ref, wd_ref = rest
        else:
            t_ref, gf_ref, h_ref, loss_ref, dgf_ref, n_ref, gu_ref, a_ref, wgu_ref, wd_ref = rest
        _stage([(wgu_hbm, wgu_ref), (wd_hbm, wd_ref)])
        x = x_ref[...]
        _, xr = _rms(x)
        n = (xr * g_ref[...]).astype(BF16)
        n_ref[...] = n
        acc = jnp.zeros((tm, D), F32)
        for j in range(2):
            g = _nn(n, wgu_ref[j])
            u = _nn(n, wgu_ref[2 + j])
            gu_ref[:, j * FFS:(j + 1) * FFS] = g.astype(BF16)
            gu_ref[:, (2 + j) * FFS:(3 + j) * FFS] = u.astype(BF16)
            half_act = (0.5 * (g * jax.nn.sigmoid(g) * u)).astype(BF16)
            a_ref[:, j * FFS:(j + 1) * FFS] = half_act
            acc = acc + _nn(half_act, wd_ref[j * FFS:(j + 1) * FFS, :])
        h = x + acc
        if head is None:
            h_ref[...] = h
            return
        gf = gf_ref[...]
        r, hr = _rms(h)
        err = hr * gf - t_ref[...]
        dh, dgain = _rms_bwd(err * (1.0 / D), hr, r, gf)
        h_ref[...] = dh

        @pl.when(pl.program_id(0) == 0)
        def _():
            dgf_ref[...] = jnp.zeros_like(dgf_ref)
            loss_ref[...] = jnp.zeros_like(loss_ref)

        dgf_ref[...] += dgain
        loss_ref[...] += jnp.full((1, 128), (0.5 / D) * jnp.sum(err * err), F32)

    saved_specs = [_rows(tm, D), _rows(tm, 4 * FFS), _rows(tm, DFF)]
    saved_shapes = [_sds((S, D), BF16), _sds((S, 4 * FFS), BF16), _sds((S, DFF), BF16)]
    if head is None:
        return _call(
            body, (x, gain, wgu, wd), name=name, grid=(S // tm,),
            in_specs=[_rows(tm, D), _fixed((1, D)), _ANY, _ANY],
            out_specs=[_rows(tm, D)] + saved_specs, out_shape=[_sds((S, D), F32)] + saved_shapes,
            scratch_shapes=_vmem_like(wgu, wd),
            compiler_params=_params(("arbitrary",), 56), exchange=exchange)
    return _call(
        body, (x, gain, wgu, wd, *head), name=name, grid=(S // tm,),
        in_specs=[_rows(tm, D), _fixed((1, D)), _ANY, _ANY, _rows(tm, D), _fixed((1, D))],
        out_specs=[_rows(tm, D), _fixed((1, 128)), _fixed((1, D))] + saved_specs,
        out_shape=[_sds((S, D), F32), _sds((1, 128), F32), _sds((1, D), F32)] + saved_shapes,
        scratch_shapes=_vmem_like(wgu, wd),
        compiler_params=_params(("arbitrary",), 56), exchange=exchange, free=(4, 5))


def _ffn_bwd(dh, x, gain, gu, wgu, wd, name):
    tm = 256

    def body(dh_ref, x_ref, g_ref, gu_ref, wgu_hbm, wd_hbm, dx_ref, dgu_ref, dg_ref, wgu_ref, wd_ref):
        _stage([(wgu_hbm, wgu_ref), (wd_hbm, wd_ref)])
        dh = dh_ref[...]
        dhb = dh.astype(BF16)
        dn = jnp.zeros((tm, D), F32)
        for j in range(2):
            g = gu_ref[:, j * FFS:(j + 1) * FFS].astype(F32)
            u = gu_ref[:, (2 + j) * FFS:(3 + j) * FFS].astype(F32)
            da = 0.5 * _nt(dhb, wd_ref[j * FFS:(j + 1) * FFS, :])
            sg = jax.nn.sigmoid(g)
            dgb = (da * u * (sg * (1.0 + g * (1.0 - sg)))).astype(BF16)
            dub = (da * (g * sg)).astype(BF16)
            dgu_ref[:, j * FFS:(j + 1) * FFS] = dgb
            dgu_ref[:, (2 + j) * FFS:(3 + j) * FFS] = dub
            dn = dn + _nt(dgb, wgu_ref[j]) + _nt(dub, wgu_ref[2 + j])
        r, xr = _rms(x_ref[...])
        dx, dgain = _rms_bwd(dn, xr, r, g_ref[...])
        dx_ref[...] = dh + dx

        @pl.when(pl.program_id(0) == 0)
        def _():
            dg_ref[...] = jnp.zeros_like(dg_ref)

        dg_ref[...] += dgain

    return _call(
        body, (dh, x, gain, gu, wgu, wd), name=name, grid=(S // tm,),
        in_specs=[_rows(tm, D), _rows(tm, D), _fixed((1, D)), _rows(tm, 4 * FFS), _ANY, _ANY],
        out_specs=[_rows(tm, D), _rows(tm, 4 * FFS), _fixed((1, D))],
        out_shape=[_sds((S, D), F32), _sds((S, 4 * FFS), BF16), _sds((1, D), F32)],
        scratch_shapes=_vmem_like(wgu, wd), compiler_params=_params(("arbitrary",), 56))


def _ffn_bwd_act(dh, gu, wd, name, exchange=None, after=()):
    tm = 512

    def body(dh_ref, gu_ref, wd_hbm, dgu_ref, wd_ref):
        _stage([(wd_hbm, wd_ref)])
        dhb = dh_ref[...].astype(BF16)
        for j in range(2):
            g = gu_ref[:, j * FFS:(j + 1) * FFS].astype(F32)
            u = gu_ref[:, (2 + j) * FFS:(3 + j) * FFS].astype(F32)
            da = 0.5 * _nt(dhb, wd_ref[j * FFS:(j + 1) * FFS, :])
            sg = jax.nn.sigmoid(g)
            dgu_ref[:, j * FFS:(j + 1) * FFS] = (da * u * (sg * (1.0 + g * (1.0 - sg)))).astype(BF16)
            dgu_ref[:, (2 + j) * FFS:(3 + j) * FFS] = (da * (g * sg)).astype(BF16)

    res = _call(
        body, (dh, gu, wd), name=name, grid=(S // tm,),
        in_specs=[_rows(tm, D), _rows(tm, 4 * FFS), _ANY], out_specs=[_rows(tm, 4 * FFS)],
        out_shape=[_sds((S, 4 * FFS), BF16)], scratch_shapes=_vmem_like(wd),
        compiler_params=_params(("arbitrary",), 56), exchange=exchange, after=after)
    return res[0] if exchange is None else (res[0][0], res[1])


def _ffn_bwd_in(dh, x, gain, dgu, wgu, name, exchange=None, after=()):
    tm = 512

    def body(dh_ref, x_ref, g_ref, dgu_ref, wgu_hbm, dx_ref, dg_ref, wgu_ref):
        _stage([(wgu_hbm, wgu_ref)])
        dn = jnp.zeros((tm, D), F32)
        for j in range(NSH):
            dn = dn + _nt(dgu_ref[:, j * FFS:(j + 1) * FFS], wgu_ref[j])
        r, xr = _rms(x_ref[...])
        dx, dgain = _rms_bwd(dn, xr, r, g_ref[...])
        dx_ref[...] = dh_ref[...] + dx

        @pl.when(pl.program_id(0) == 0)
        def _():
            dg_ref[...] = jnp.zeros_like(dg_ref)

        dg_ref[...] += dgain

    return _call(
        body, (dh, x, gain, dgu, wgu), name=name, grid=(S // tm,),
        in_specs=[_rows(tm, D), _rows(tm, D), _fixed((1, D)), _rows(tm, 4 * FFS), _ANY],
        out_specs=[_rows(tm, D), _fixed((1, D))],
        out_shape=[_sds((S, D), F32), _sds((1, D), F32)],
        scratch_shapes=_vmem_like(wgu),
        compiler_params=_params(("arbitrary",), 56), exchange=exchange, after=after)


def _mix_in(h, gain, w_in, after=()):
    tm = 512

    def body(h_ref, g_ref, w_hbm, u_ref, xp_ref, q_ref, k_ref, v_ref, gp_ref, gs_ref, w_ref):
        _stage([(w_hbm, w_ref)])
        _, hr = _rms(h_ref[...])
        u = (hr * g_ref[...]).astype(BF16)
        u_ref[...] = u
        p0 = _nn(u, w_ref[0])
        xp_ref[...] = p0[:, :PW]
        q_ref[...] = p0[:, PW:].astype(BF16)
        p1 = _nn(u, w_ref[1])
        k_ref[...] = p1[:, :SBW].astype(BF16)
        v_ref[...] = p1[:, SBW:].astype(BF16)
        gp_ref[...] = jax.nn.sigmoid(_nn(u, w_ref[2])).astype(BF16)
        gs_ref[...] = jax.nn.sigmoid(_nn(u, w_ref[3])).astype(BF16)

    return _call(
        body, (h, gain, w_in), name="mix_in", grid=(S // tm,),
        in_specs=[_rows(tm, D), _fixed((1, D)), _ANY],
        out_specs=[_rows(tm, D), _rows(tm, PW), _rows(tm, SBW), _rows(tm, SBW), _rows(tm, SBW),
                   _rows(tm, D), _rows(tm, D)],
        out_shape=[_sds((S, D), BF16), _sds((S, PW), F32), _sds((S, SBW), BF16), _sds((S, SBW), BF16),
                   _sds((S, SBW), BF16), _sds((S, D), BF16), _sds((S, D), BF16)],
        scratch_shapes=_vmem_like(w_in),
        compiler_params=_params(("arbitrary",), 48), free=(1,), after=after)


def _hilo_dot(x, tri):
    hi = x.astype(BF16)
    lo = (x - hi.astype(F32)).astype(BF16)
    return _nn(hi, tri) + _nn(lo, tri)


def _log_terms(qk):
    z2 = qk * (SCALE * LOG2E)
    lb = jnp.minimum(z2, 0.0) - jnp.log2(1.0 + jnp.exp2(-jnp.abs(z2)))
    return lb, lb - z2


def _head_masks():
    lane = lax.broadcasted_iota(jnp.int32, (1, 2 * DH), 1)
    return (lane < DH, lane >= DH)


def _attn_fwd(q, k, v, exchange=None):
    T = TA

    def body(q_ref, k_ref, v_ref, o_ref, c_ref):
        i2 = 2 * pl.program_id(1)
        row = lax.broadcasted_iota(jnp.int32, (T, T), 0)
        col = lax.broadcasted_iota(jnp.int32, (T, T), 1)
        after = (row > col).astype(BF16)
        causal = col < row
        masks = _head_masks()
        qms = {}
        for b in range(QB):
            q2 = q_ref[b * T:(b + 1) * T, :]
            for h, hm in enumerate(masks):
                qms[b, h] = jnp.where(hm, q2, jnp.zeros_like(q2))

        def blocks(keys, pairs, carries, os):
            ks, vms = [], []
            for j in keys:
                rows = pl.ds(pl.multiple_of(j * T, T), T)
                vj = v_ref[rows, :]
                ks.append(k_ref[rows, :])
                vms.append([jnp.where(hm, vj, jnp.zeros_like(vj)) for hm in masks])
            units = [(n, h) for n in range(len(pairs)) for h in range(2)]
            qks = {(n, h): _nt(qms[pairs[n][0], h], ks[pairs[n][1]]) for n, h in units}
            lbs, l1ms = {}, {}
            for u in units:
                lbs[u], l1m = _log_terms(qks[u])
                l1ms[u] = jnp.where(causal, l1m, 0.0) if pairs[u[0]][2] else l1m
            cins = {u: _hilo_dot(l1ms[u], after) for u in units}
            carries, os = dict(carries), list(os)
            for n, h in units:
                b, key, diag = pairs[n]
                a = jnp.exp2(lbs[n, h] + cins[n, h] + carries[b, h])
                if diag:
                    a = jnp.where(causal, a, 0.0)
                os[b] = os[b] + _nn(a.astype(BF16), vms[key][h])
                carries[b, h] = carries[b, h] + jnp.sum(l1ms[n, h], axis=1, keepdims=True)
            return carries, tuple(os)

        carries = {(b, h): jnp.zeros((T, 1), F32) for b in range(QB) for h in range(2)}
        os = tuple(jnp.zeros((T, 2 * DH), F32) for _ in range(QB))
        carries, os = blocks([i2 + 1, i2], [(1, 0, True), (0, 1, True), (1, 1, False)], carries, os)
        carries, os = lax.fori_loop(
            0, i2, lambda jj, c: blocks([i2 - 1 - jj], [(0, 0, False), (1, 0, False)], c[0], c[1]), (carries, os))
        for b in range(QB):
            o_ref[b * T:(b + 1) * T, :] = os[b].astype(BF16)
            c_ref[b * T:(b + 1) * T, :] = jnp.where(masks[0], carries[b, 0], carries[b, 1])

    blk = pl.BlockSpec((QB * T, 2 * DH), lambda p, i: (i, p))
    full = pl.BlockSpec((S, 2 * DH), lambda p, i: (0, p))
    return _call(
        body, (q, k, v), name="attn_fwd", grid=(SBW // (2 * DH), S // (QB * T)),
        in_specs=[blk, full, full], out_specs=[blk, blk],
        out_shape=[_sds((S, SBW), BF16), _sds((S, SBW), F32)],
        compiler_params=_params(("arbitrary", "arbitrary"), 40), exchange=exchange)


def _attn_bwd(q, k, v, do, ctot, after=()):
    T = TA
    nq = S // (QB * T)

    def body(q_ref, k_ref, v_ref, do_ref, c_ref, dq_ref, dk_ref, dv_ref, dk_acc, dv_acc):
        step = pl.program_id(1)
        i2 = 2 * step

        @pl.when(step == 0)
        def _():
            dk_acc[...] = jnp.zeros_like(dk_acc)
            dv_acc[...] = jnp.zeros_like(dv_acc)

        row = lax.broadcasted_iota(jnp.int32, (T, T), 0)
        col = lax.broadcasted_iota(jnp.int32, (T, T), 1)
        upto = (row <= col).astype(BF16)
        before = (row < col).astype(BF16)
        causal = col < row
        masks = _head_masks()
        qms, doms, ctots = {}, {}, {}
        for b in range(QB):
            q2, do2 = q_ref[b * T:(b + 1) * T, :], do_ref[b * T:(b + 1) * T, :]
            for h, hm in enumerate(masks):
                qms[b, h] = jnp.where(hm, q2, jnp.zeros_like(q2))
                doms[b, h] = jnp.where(hm, do2, jnp.zeros_like(do2))
                ctots[b, h] = c_ref[b * T:(b + 1) * T, h * DH:h * DH + 1]

        def blocks(keys, pairs, sums, dqs):
            rows = [pl.ds(pl.multiple_of(j * T, T), T) for j in keys]
            ks, vs = [k_ref[r, :] for r in rows], [v_ref[r, :] for r in rows]
            kms = [[jnp.where(hm, kj, jnp.zeros_like(kj)) for hm in masks] for kj in ks]
            units = [(n, h) for n in range(len(pairs)) for h in range(2)]
            qks = {(n, h): _nt(qms[pairs[n][0], h], ks[pairs[n][1]]) for n, h in units}
            das = {(n, h): _nt(doms[pairs[n][0], h], vs[pairs[n][1]]) for n, h in units}
            lbs, l1ms = {}, {}
            for u in units:
                lbs[u], l1m = _log_terms(qks[u])
                l1ms[u] = jnp.where(causal, l1m, 0.0) if pairs[u[0]][2] else l1m
            pins = {u: _hilo_dot(l1ms[u], upto) for u in units}
            sums = dict(sums)
            a_s, dls, cps = {}, {}, {}
            for n, h in units:
                b, _, diag = pairs[n]
                cl, cp = sums[b, h]
                a = jnp.exp2(lbs[n, h] + (ctots[b, h] - cl) - pins[n, h])
                if diag:
                    a = jnp.where(causal, a, 0.0)
                a_s[n, h] = a.astype(BF16)
                dls[n, h] = das[n, h] * a
                cps[n, h] = cp
                sums[b, h] = (cl + jnp.sum(l1ms[n, h], axis=1, keepdims=True),
                              cp + jnp.sum(dls[n, h], axis=1, keepdims=True))
            pexs = {u: _hilo_dot(dls[u], before) for u in units}
            dzbs = {}
            for u in units:
                dz = dls[u] - jnp.exp2(lbs[u]) * (dls[u] + pexs[u] + cps[u])
                if pairs[u[0]][2]:
                    dz = jnp.where(causal, dz, 0.0)
                dzbs[u] = dz.astype(BF16)
            dqs = list(dqs)
            for n, h in units:
                dqs[pairs[n][0]] = dqs[pairs[n][0]] + _nn(dzbs[n, h], kms[pairs[n][1]][h])
            for key, r in enumerate(rows):
                mine = [(n, h) for n, h in units if pairs[n][1] == key]
                dk_acc[r, :] += functools.reduce(jnp.add, [_tn(dzbs[u], qms[pairs[u[0]][0], u[1]]) for u in mine])
                dv_acc[r, :] += functools.reduce(jnp.add, [_tn(a_s[u], doms[pairs[u[0]][0], u[1]]) for u in mine])
            return sums, tuple(dqs)

        zero = jnp.zeros((T, 1), F32)
        sums = {(b, h): (zero, zero) for b in range(QB) for h in range(2)}
        dqs = tuple(jnp.zeros((T, 2 * DH), F32) for _ in range(QB))
        sums, dqs = lax.fori_loop(
            0, i2, lambda j, c: blocks([j], [(0, 0, False), (1, 0, False)], c[0], c[1]), (sums, dqs))
        _, dqs = blocks([i2, i2 + 1], [(0, 0, True), (1, 0, False), (1, 1, True)], sums, dqs)
        for b in range(QB):
            dq_ref[b * T:(b + 1) * T, :] = (dqs[b] * SCALE).astype(BF16)

        @pl.when(step == nq - 1)
        def _():
            dk_ref[...] = (dk_acc[...] * SCALE).astype(BF16)
            dv_ref[...] = dv_acc[...].astype(BF16)

    blk = pl.BlockSpec((QB * T, 2 * DH), lambda p, i: (i, p))
    full = pl.BlockSpec((S, 2 * DH), lambda p, i: (0, p))
    return _call(
        body, (q, k, v, do, ctot), name="attn_bwd", grid=(SBW // (2 * DH), nq),
        in_specs=[blk, full, full, blk, blk], out_specs=[blk, full, full],
        out_shape=[_sds((S, SBW), BF16), _sds((S, SBW), BF16), _sds((S, SBW), BF16)],
        scratch_shapes=[pltpu.VMEM((S, 2 * DH), F32), pltpu.VMEM((S, 2 * DH), F32)],
        compiler_params=_params(("arbitrary", "arbitrary"), 40), after=after)


def _pool_counts(first_row, tm):
    pos = first_row + lax.broadcasted_iota(jnp.int32, (tm, 1), 0)
    return [jnp.minimum(pos + 1, w).astype(F32) for w in POOL_WINDOWS]


def _mix_out(h, xp, o_sb, gp, gs, w_group, scale, w_bp, w_ba, w_out, exchange=None):
    tm = 512

    def body(h_ref, xp_ref, o_ref, gp_ref, gs_ref, wg_hbm, sc_ref, wbp_hbm, wba_hbm, wo_hbm,
             h2_ref, pm_ref, p_ref, yp_ref, ys_ref, m_ref, halo, wg_ref, wbp_ref, wba_ref, wo_ref):
        _stage([(wg_hbm, wg_ref), (wbp_hbm, wbp_ref), (wba_hbm, wba_ref), (wo_hbm, wo_ref)])
        i = pl.program_id(0)

        @pl.when(i == 0)
        def _():
            halo[...] = jnp.zeros_like(halo)

        xp = xp_ref[...]
        ext = jnp.concatenate([halo[...], xp], axis=0)
        halo[...] = xp[tm - HALO:, :]
        counts = _pool_counts(i * tm, tm)
        for gi in range(len(POOL_WINDOWS)):
            lanes = slice(gi * PG, (gi + 1) * PG)
            win = ext[:, lanes]
            for step in range(gi + 1):
                win = win + pltpu.roll(win, 1 << step, 0)
            pm = (win[HALO:, :] / counts[gi] - xp[:, lanes]).astype(BF16)
            pm_ref[:, lanes] = pm
            p_ref[:, lanes] = (_nn(pm, wg_ref[gi]) * sc_ref[:, lanes]).astype(BF16)
        pb = p_ref[...]
        ob = o_ref[...]
        for j in range(NSH):
            cols = slice(j * (D // NSH), (j + 1) * (D // NSH))
            yp = _nn(pb, wbp_ref[j])
            ys = _nn(ob, wba_ref[j])
            yp_ref[:, cols] = yp.astype(BF16)
            ys_ref[:, cols] = ys.astype(BF16)
            m_ref[:, cols] = (gp_ref[:, cols].astype(F32) * yp + gs_ref[:, cols].astype(F32) * ys).astype(BF16)
        h2_ref[...] = h_ref[...] + _nn(m_ref[...], wo_ref[...])

    return _call(
        body, (h, xp, o_sb, gp, gs, w_group, scale, w_bp, w_ba, w_out), name="mix_out", grid=(S // tm,),
        in_specs=[_rows(tm, D), _rows(tm, PW), _rows(tm, SBW), _rows(tm, D), _rows(tm, D),
                  _ANY, _fixed((1, PW)), _ANY, _ANY, _ANY],
        out_specs=[_rows(tm, D), _rows(tm, PW), _rows(tm, PW), _rows(tm, D), _rows(tm, D), _rows(tm, D)],
        out_shape=[_sds((S, D), F32), _sds((S, PW), BF16), _sds((S, PW), BF16), _sds((S, D), BF16),
                   _sds((S, D), BF16), _sds((S, D), BF16)],
        scratch_shapes=[pltpu.VMEM((HALO, PW), F32)] + _vmem_like(w_group, w_bp, w_ba, w_out),
        compiler_params=_params(("arbitrary",), 48), free=(5, 6), exchange=exchange)


def _mix_bwd_out(dh, gp, gs, yp, ys, pm, w_group, scale, w_bp, w_ba, w_out, exchange=None):
    tm = 512
    nt = S // tm

    def body(dh_ref, gp_ref, gs_ref, yp_ref, ys_ref, pm_ref, wg_hbm, sc_ref, wbp_hbm, wba_hbm, wo_hbm,
             dlg_ref, dyp_ref, dys_ref, do_ref, dyg_ref, dxp_ref, dsc_ref, halo, wg_ref, wbp_ref, wba_ref, wo_ref):
        _stage([(wg_hbm, wg_ref), (wbp_hbm, wbp_ref), (wba_hbm, wba_ref), (wo_hbm, wo_ref)])
        step = pl.program_id(0)

        @pl.when(step == 0)
        def _():
            halo[...] = jnp.zeros_like(halo)
            dsc_ref[...] = jnp.zeros_like(dsc_ref)

        dm = _nt(dh_ref[...].astype(BF16), wo_ref[...])
        gp = gp_ref[...].astype(F32)
        gs = gs_ref[...].astype(F32)
        yp = yp_ref[...].astype(F32)
        ys = ys_ref[...].astype(F32)
        dlg_ref[:, :D] = (dm * yp * gp * (1.0 - gp)).astype(BF16)
        dlg_ref[:, D:] = (dm * ys * gs * (1.0 - gs)).astype(BF16)
        dyp_ref[...] = (dm * gp).astype(BF16)
        dys_ref[...] = (dm * gs).astype(BF16)
        dp = jnp.zeros((tm, PW), F32)
        do = jnp.zeros((tm, SBW), F32)
        for j in range(NSH):
            cols = slice(j * (D // NSH), (j + 1) * (D // NSH))
            dp = dp + _nt(dyp_ref[:, cols], wbp_ref[j])
            do = do + _nt(dys_ref[:, cols], wba_ref[j])
        do_ref[...] = do.astype(BF16)
        counts = _pool_counts((nt - 1 - step) * tm, tm)
        dscale = []
        for gi in range(len(POOL_WINDOWS)):
            lanes = slice(gi * PG, (gi + 1) * PG)
            dpg = dp[:, lanes]
            dscale.append(jnp.sum(dpg * _nn(pm_ref[:, lanes], wg_ref[gi]), axis=0, keepdims=True))
            dyg = (dpg * sc_ref[:, lanes]).astype(BF16)
            dyg_ref[:, lanes] = dyg
            dpm = _nt(dyg, wg_ref[gi])
            per = dpm / counts[gi]
            win = jnp.concatenate([per, halo[:, lanes]], axis=0)
            halo[:, lanes] = per[:HALO, :]
            for s in range(gi + 1):
                win = win + pltpu.roll(win, tm + HALO - (1 << s), 0)
            dxp_ref[:, lanes] = (win[:tm, :] - dpm).astype(BF16)
        dsc_ref[...] += jnp.concatenate(dscale, axis=1)

    rev = lambda width: pl.BlockSpec((tm, width), lambda i: (nt - 1 - i, 0))
    return _call(
        body, (dh, gp, gs, yp, ys, pm, w_group, scale, w_bp, w_ba, w_out), name="mix_bwd_out", grid=(nt,),
        in_specs=[rev(D), rev(D), rev(D), rev(D), rev(D), rev(PW), _ANY, _fixed((1, PW)), _ANY, _ANY, _ANY],
        out_specs=[rev(2 * D), rev(D), rev(D), rev(SBW), rev(PW), rev(PW), _fixed((1, PW))],
        out_shape=[_sds((S, 2 * D), BF16), _sds((S, D), BF16), _sds((S, D), BF16), _sds((S, SBW), BF16),
                   _sds((S, PW), BF16), _sds((S, PW), BF16), _sds((1, PW), F32)],
        scratch_shapes=[pltpu.VMEM((HALO, PW), F32)] + _vmem_like(w_group, w_bp, w_ba, w_out),
        compiler_params=_params(("arbitrary",), 48), exchange=exchange)


def _mix_bwd_in(dh, h, gain, pieces, w_in, exchange=None):
    tm = 512
    widths = [p.shape[1] for p in pieces]

    def body(dh_ref, h_ref, g_ref, *rest):
        piece_refs, (w_hbm, dx_ref, dg_ref, dp_ref, w_ref) = rest[:len(pieces)], rest[len(pieces):]
        _stage([(w_hbm, w_ref)])
        at = 0
        for ref, width in zip(piece_refs, widths):
            dp_ref[:, at:at + width] = ref[...]
            at += width
        du = jnp.zeros((tm, D), F32)
        for j in range(NSH):
            du = du + _nt(dp_ref[:, j * D:(j + 1) * D], w_ref[j])
        r, hr = _rms(h_ref[...])
        dx, dgain = _rms_bwd(du, hr, r, g_ref[...])
        dx_ref[...] = dh_ref[...] + dx

        @pl.when(pl.program_id(0) == 0)
        def _():
            dg_ref[...] = jnp.zeros_like(dg_ref)

        dg_ref[...] += dgain

    return _call(
        body, (dh, h, gain, *pieces, w_in), name="mix_bwd_in", grid=(S // tm,),
        in_specs=[_rows(tm, D), _rows(tm, D), _fixed((1, D))] + [_rows(tm, w) for w in widths] + [_ANY],
        out_specs=[_rows(tm, D), _fixed((1, D)), _rows(tm, 4 * D)],
        out_shape=[_sds((S, D), F32), _sds((1, D), F32), _sds((S, 4 * D), BF16)],
        scratch_shapes=_vmem_like(w_in),
        compiler_params=_params(("arbitrary",), 48), exchange=exchange)


def _wgrad(a, b, nblk, ti, name, out_dtype=BF16, exchange=None, after=()):
    ka, n = a.shape[1], b.shape[1]
    ns = n // nblk

    def body(a_ref, b_ref, o_ref):
        o_ref[...] = _tn(a_ref[...].astype(BF16), b_ref[...].astype(BF16)).astype(out_dtype)

    res = _call(
        body, (a, b), name=name, grid=(nblk, ka // ti),
        in_specs=[pl.BlockSpec((S, ti), lambda j, i: (0, i)), pl.BlockSpec((S, ns), lambda j, i: (0, j))],
        out_specs=[pl.BlockSpec((None, ti, ns), lambda j, i: (j, i, 0))],
        out_shape=[_sds((nblk, ka, ns), out_dtype)],
        compiler_params=_params(("arbitrary", "arbitrary"), 56), exchange=exchange, after=after)
    return res[0] if exchange is None else (res[0][0], res[1])


def _wgrad_groups(pm, dyg):
    def body(a_ref, b_ref, o_ref):
        o_ref[...] = _tn(a_ref[...], b_ref[...])

    col = pl.BlockSpec((S, PG), lambda g: (0, g))
    return pl.pallas_call(
        body, name="wgrad_groups", grid=(PW // PG,),
        in_specs=[col, col], out_specs=pl.BlockSpec((None, PG, PG), lambda g: (g, 0, 0)),
        out_shape=_sds((PW // PG, PG, PG), F32),
        compiler_params=_params(("arbitrary",), 32),
    )(*_in_hbm([pm, dyg]))


def _place():
    x, y, c = lax.axis_index("x"), lax.axis_index("y"), lax.axis_index("c")
    chips = [(1 - x, y), (x, 1 - y), (1 - x, 1 - y)]
    return x, y, c, chips


def _remote(src, dst, ssem, rsem, dev):
    return pltpu.make_async_remote_copy(src_ref=src, dst_ref=dst, send_sem=ssem, recv_sem=rsem,
                                        device_id=dev, device_id_type=MESH)


def _cast_into_block(w, me_idx, name):
    rows, cols = w.shape
    tr = _row_block(rows)

    def body(me_ref, w_ref, o_ref):
        o_ref[...] = w_ref[...].astype(BF16)

    return pl.pallas_call(
        body, name=name, out_shape=_sds((NSH, rows, cols), BF16),
        grid_spec=pltpu.PrefetchScalarGridSpec(
            num_scalar_prefetch=1, grid=(rows // tr,),
            in_specs=[pl.BlockSpec((tr, cols), lambda r, me: (r, 0))],
            out_specs=pl.BlockSpec((None, tr, cols), lambda r, me: (me[0], r, 0))),
        compiler_params=_params(("arbitrary",), 32),
    )(me_idx, w)


def _ex_gather(bufs):
    n = len(bufs)
    per = 8

    def plan(outs, ssem, rsem, w):
        x, y, c, _ = _place()
        sib, nbr_x, nbr_y = (x, y, 1 - c), (1 - x, y, c), (x, 1 - y, c)
        half = outs[w].shape[1] // 2
        quarter = half // 2
        sem = lambda k: (ssem.at[per * w + k], rsem.at[per * w + k])
        rows = lambda blk, start, size: outs[w].at[blk, pl.ds(start, size)]
        mine = rows(2 * x + y, c * half, half)
        from_x = rows(2 * (1 - x) + y, c * half, half)
        from_y = rows(2 * x + (1 - y), c * half, half)
        diag = 2 * (1 - x) + (1 - y)
        pass_y = rows(2 * (1 - x) + y, c * half, quarter)
        pass_x = rows(2 * x + (1 - y), c * half + quarter, quarter)
        diag_0, diag_1 = rows(diag, c * half, quarter), rows(diag, c * half + quarter, quarter)
        first = [_remote(mine, mine, *sem(0), nbr_x), _remote(mine, mine, *sem(1), nbr_y)]
        arrivals = [
            (_remote(from_x, from_x, *sem(0), nbr_x),
             [_remote(pass_y, pass_y, *sem(2), nbr_y), _remote(from_x, from_x, *sem(4), sib)]),
            (_remote(from_y, from_y, *sem(1), nbr_y),
             [_remote(pass_x, pass_x, *sem(3), nbr_x), _remote(from_y, from_y, *sem(5), sib)]),
            (_remote(diag_0, diag_0, *sem(2), nbr_y), [_remote(diag_0, diag_0, *sem(6), sib)]),
            (_remote(diag_1, diag_1, *sem(3), nbr_x), [_remote(diag_1, diag_1, *sem(7), sib)]),
        ]
        other = (1 - c) * half
        from_sibling = [
            _remote(rows(2 * (1 - x) + y, other, half), rows(2 * (1 - x) + y, other, half), *sem(4), sib),
            _remote(rows(2 * x + (1 - y), other, half), rows(2 * x + (1 - y), other, half), *sem(5), sib),
            _remote(rows(diag, other, quarter), rows(diag, other, quarter), *sem(6), sib),
            _remote(rows(diag, other + quarter, quarter), rows(diag, other + quarter, quarter), *sem(7), sib),
        ]
        return first, arrivals, from_sibling

    def start(ins, outs, ssem, rsem):
        x, y, c, _ = _place()
        for w in range(n):
            half = outs[w].shape[1] // 2
            mine = outs[w].at[2 * x + y, pl.ds(c * half, half)]
            _remote(mine, mine, ssem.at[per * w], rsem.at[per * w], (1 - x, y, c)).start()
            _remote(mine, mine, ssem.at[per * w + 1], rsem.at[per * w + 1], (x, 1 - y, c)).start()

    def finish(ins, outs, ssem, rsem):
        plans = [plan(outs, ssem, rsem, w) for w in range(n)]
        started = []
        for direct in (True, False):
            for first, arrivals, _ in plans:
                for arrived, onward in (arrivals[:2] if direct else arrivals[2:]):
                    arrived.wait_recv()
                    for cp in onward:
                        cp.start()
                    started += onward
        for first, _, from_sibling in plans:
            for cp in from_sibling:
                cp.wait_recv()
            started += first
        for cp in started:
            cp.wait_send()

    return Exchange(bufs, [_sds(b.shape, b.dtype) for b in bufs], {w: w for w in range(n)}, per * n, start, finish)


def _ex_gather_direct(bufs):
    n = len(bufs)

    def copies(outs, ssem, rsem, only_first=False):
        x, y, c, chips = _place()
        me, sib = 2 * x + y, (x, y, 1 - c)
        first, relay, last = [], [], []
        for w in range(n):
            half = outs[w].shape[1] // 2
            mine = outs[w].at[me, pl.ds(c * half, half)]
            for k, (px, py) in enumerate(chips):
                sems = (ssem.at[6 * w + k], rsem.at[6 * w + k])
                sib_sems = (ssem.at[6 * w + 3 + k], rsem.at[6 * w + 3 + k])
                first.append(_remote(mine, mine, *sems, (px, py, c)))
                if only_first:
                    continue
                got = outs[w].at[2 * px + py, pl.ds(c * half, half)]
                relay.append((_remote(got, got, *sems, (px, py, c)), _remote(got, got, *sib_sems, sib)))
                theirs = outs[w].at[2 * px + py, pl.ds((1 - c) * half, half)]
                last.append(_remote(theirs, theirs, *sib_sems, sib))
        return first, relay, last

    def start(ins, outs, ssem, rsem):
        for cp in copies(outs, ssem, rsem, only_first=True)[0]:
            cp.start()

    def finish(ins, outs, ssem, rsem):
        first, relay, last = copies(outs, ssem, rsem)
        for arrived, onward in relay:
            arrived.wait_recv()
            onward.start()
        for cp in last:
            cp.wait_recv()
        for cp in first:
            cp.wait_send()
        for _, onward in relay:
            onward.wait_send()

    return Exchange(bufs, [_sds(b.shape, b.dtype) for b in bufs], {w: w for w in range(n)}, 6 * n, start, finish)


def _simple_exchange(arrays, landing, aliases, make_copies):
    def start(ins, outs, ssem, rsem):
        for cp, _ in make_copies(ins, outs, ssem, rsem, False):
            cp.start()

    def finish(ins, outs, ssem, rsem):
        cps = make_copies(ins, outs, ssem, rsem, True)
        for _, landed in cps:
            landed.wait_recv()
        for cp, _ in cps:
            cp.wait_send()

    return Exchange(arrays, landing, aliases, len(arrays) * 3, start, finish)


def _ex_pair_swap(grads):
    def make(ins, outs, ssem, rsem, landing):
        x, y, c, _ = _place()
        cps = [_remote(ins[w].at[:, 1 - c], outs[w], ssem.at[w], rsem.at[w], (x, y, 1 - c))
               for w in range(len(grads))]
        return [(cp, cp) for cp in cps]

    return _simple_exchange(grads, [_sds((NSH,) + g.shape[2:], g.dtype) for g in grads], {}, make)


def _ex_scatter(parts):
    def make(ins, outs, ssem, rsem, landing):
        x, y, c, chips = _place()
        out = []
        for w in range(len(parts)):
            for k, (px, py) in enumerate(chips):
                sems = (ssem.at[3 * w + k], rsem.at[3 * w + k])
                out.append((_remote(ins[w].at[2 * px + py], outs[w].at[k], *sems, (px, py, c)),
                            _remote(outs[w].at[k], outs[w].at[k], *sems, (px, py, c)) if landing else None))
        return out

    return _simple_exchange(parts, [_sds((3,) + p.shape[1:], p.dtype) for p in parts], {}, make)


def _ex_relay(bufs):
    def make(ins, outs, ssem, rsem, landing):
        x, y, c, chips = _place()
        sib = (x, y, 1 - c)
        out = []
        for w in range(len(bufs)):
            half = outs[w].shape[1] // 2
            for k, (px, py) in enumerate(chips):
                sems = (ssem.at[3 * w + k], rsem.at[3 * w + k])
                have = outs[w].at[2 * px + py, pl.ds(c * half, half)]
                miss = outs[w].at[2 * px + py, pl.ds((1 - c) * half, half)]
                out.append((_remote(have, have, *sems, sib), _remote(miss, miss, *sems, sib) if landing else None))
        return out

    return _simple_exchange(bufs, [_sds(b.shape, b.dtype) for b in bufs], {w: w for w in range(len(bufs))}, make)


def _ex_share(bufs):
    def make(ins, outs, ssem, rsem, landing):
        x, y, c, _ = _place()
        sib = (x, y, 1 - c)
        return [(_remote(outs[w].at[c], outs[w].at[c], ssem.at[w], rsem.at[w], sib),
                 _remote(outs[w].at[1 - c], outs[w].at[1 - c], ssem.at[w], rsem.at[w], sib) if landing else None)
                for w in range(len(bufs))]

    return _simple_exchange(bufs, [_sds(b.shape, b.dtype) for b in bufs], {w: w for w in range(len(bufs))}, make)


def _gather_small(block, ex):
    m_per, n = block.shape
    na, nl = len(ex.arrays), len(ex.landing)

    def body(x_ref, *refs):
        e_in, out_ref, e_out = refs[:na], refs[na], refs[na + 1:na + 1 + nl]
        ssem, rsem, lsem, e_ssem, e_rsem = refs[na + 1 + nl:]
        ex.start(e_in, e_out, e_ssem, e_rsem)
        x, y, c, chips = _place()
        me, sib = (x, y, c), (x, y, 1 - c)

        def rows(px, py, pc):
            return out_ref.at[pl.ds((4 * px + 2 * py + pc) * m_per, m_per), :]

        def copy(k, blk, to, src=None):
            return _remote(rows(*blk) if src is None else src, rows(*blk), ssem.at[k], rsem.at[k], to)

        mine = pltpu.make_async_copy(x_ref, rows(*me), lsem)
        mine.start()
        first = [copy(0, me, sib, src=x_ref)]
        first += [copy(1 + j, me, (*chip, c), src=x_ref) for j, chip in enumerate(chips)]
        for cp in first:
            cp.start()
        passed = [copy(4 + j, (*chip, c), sib) for j, chip in enumerate(chips)]
        for j, chip in enumerate(chips):
            copy(1 + j, (*chip, c), me).wait_recv()
            passed[j].start()
        copy(0, sib, me).wait_recv()
        for j, chip in enumerate(chips):
            copy(4 + j, (*chip, 1 - c), me).wait_recv()
        for cp in first + passed:
            cp.wait_send()
        mine.wait()
        ex.finish(e_in, e_out, e_ssem, e_rsem)

    outs = pl.pallas_call(
        body, name="gather_small", out_shape=[jax.ShapeDtypeStruct((8 * m_per, n), block.dtype)] + ex.landing,
        in_specs=[_VM] + [_ANY] * na, out_specs=[_VM] + [_ANY] * nl,
        scratch_shapes=[pltpu.SemaphoreType.DMA((7,)), pltpu.SemaphoreType.DMA((7,)), pltpu.SemaphoreType.DMA]
        + [pltpu.SemaphoreType.DMA((ex.n_sems,))] * 2,
        input_output_aliases={1 + i: 1 + j for i, j in ex.aliases.items()},
    )(block, *_in_hbm(ex.arrays))
    return outs[0], outs[1:]


def _row_block(rows, cap=256):
    return max(t for t in range(16, cap + 1, 16) if rows % t == 0)


def _pair_sum(grad, got, c_idx, name):
    _, _, half, cols = grad.shape
    tr = _row_block(half, 512)

    def body(c_ref, a_ref, b_ref, o_ref):
        o_ref[...] = (a_ref[...].astype(F32) + b_ref[...].astype(F32)).astype(BF16)

    return pl.pallas_call(
        body, name=name, out_shape=_sds((NSH, half, cols), BF16),
        grid_spec=pltpu.PrefetchScalarGridSpec(
            num_scalar_prefetch=1, grid=(NSH, half // tr),
            in_specs=[pl.BlockSpec((None, None, tr, cols), lambda j, r, c: (j, c[0], r, 0)),
                      pl.BlockSpec((None, tr, cols), lambda j, r, c: (j, r, 0))],
            out_specs=pl.BlockSpec((None, tr, cols), lambda j, r, c: (j, r, 0))),
        compiler_params=_params(("arbitrary", "arbitrary"), 32),
    )(c_idx, *_in_hbm([grad, got]))


def _chip_sum(own, got, place, name):
    _, half, cols = own.shape
    tr = _row_block(half, 512)

    def body(place_ref, own_ref, got_ref, o_ref):
        acc = own_ref[...].astype(F32)
        for k in range(3):
            acc = acc + got_ref[k].astype(F32)
        o_ref[...] = acc

    return pl.pallas_call(
        body, name=name, out_shape=_sds((2, half, cols), F32),
        grid_spec=pltpu.PrefetchScalarGridSpec(
            num_scalar_prefetch=1, grid=(half // tr,),
            in_specs=[pl.BlockSpec((None, tr, cols), lambda r, p: (p[0], r, 0)),
                      pl.BlockSpec((3, tr, cols), lambda r, p: (0, r, 0))],
            out_specs=pl.BlockSpec((None, tr, cols), lambda r, p: (p[1], r, 0))),
        compiler_params=_params(("arbitrary",), 32),
    )(place, *_in_hbm([own, got]))


def _adamw_math(w, g, m, v):
    m = B1 * m + (1.0 - B1) * g
    v = B2 * v + (1.0 - B2) * (g * g)
    m_hat = m / (1.0 - B1 ** STEP)
    v_hat = v / (1.0 - B2 ** STEP)
    return -LR * (m_hat / (jnp.sqrt(v_hat) + AEPS) + WD * w), m, v


def _adamw(w, g, m, v, name, after=()):
    rows, cols = w.shape
    tr = _row_block(rows)

    def body(w_ref, g_ref, m_ref, v_ref, go_ref, d_ref, nm_ref, nv_ref):
        g = g_ref[...]
        go_ref[...] = g
        d_ref[...], nm_ref[...], nv_ref[...] = _adamw_math(w_ref[...], g, m_ref[...], v_ref[...])

    blk = pl.BlockSpec((tr, cols), lambda r: (r, 0))
    return _call(
        body, (w, g, m, v), name=name, grid=(rows // tr,), out_shape=[_sds(w.shape, F32)] * 4,
        in_specs=[blk] * 4, out_specs=[blk] * 4,
        compiler_params=_params(("arbitrary",), 32), free=(0, 2, 3), after=after)


def _small_update(gathered, w, m, v):
    rows = w.shape[0]

    def body(ga_ref, w_ref, m_ref, v_ref, g_ref, d_ref, nm_ref, nv_ref):
        g = ga_ref[0:rows, :]
        for dev in range(1, 8):
            g = g + ga_ref[dev * rows:(dev + 1) * rows, :]
        g_ref[...] = g
        d_ref[...], nm_ref[...], nv_ref[...] = _adamw_math(w_ref[...], g, m_ref[...], v_ref[...])

    return pl.pallas_call(
        body, name="small_update", out_shape=[jax.ShapeDtypeStruct(w.shape, F32)] * 4,
        in_specs=[_VM] * 4, out_specs=[_VM] * 4,
    )(gathered, w, m, v)


SMALL = ("ffn1_norm", "mix_norm", "ffn2_norm", "final_norm", "pool_scale", "pool_w_group", "loss")
BIG = ("ffn1_w_gate_up", "ffn1_w_down", "w_in", "w_branch_pool", "w_branch_attn", "w_out",
       "ffn2_w_gate_up", "ffn2_w_down")
ORDER = ("ffn1_norm", "ffn1_w_gate_up", "ffn1_w_down", "mix_norm", "w_in", "pool_w_group", "pool_scale",
         "w_branch_pool", "w_branch_attn", "w_out", "ffn2_norm", "ffn2_w_gate_up", "ffn2_w_down", "final_norm")
SMALL_ROWS = 560


def _pack_small(t):
    parts = []
    for k in SMALL:
        rows = t[k].reshape(-1, 128) if k in t else jnp.zeros((1, 128), F32)
        parts.append(jnp.pad(rows, ((0, -rows.shape[0] % 8), (0, 0))))
    packed = jnp.concatenate(parts, axis=0)
    assert packed.shape == (SMALL_ROWS, 128), packed.shape
    return packed


def _unpack_small(packed, like):
    out, at = {}, 0
    for k in SMALL:
        n = like[k].size // 128 if k in like else 1
        out[k] = packed[at:at + n].reshape(like[k].shape) if k in like else packed[at, 0]
        at += n + (-n % 8)
    return out


def _halves(g):
    return g.reshape(NSH, 2, g.shape[1] // 2, g.shape[2])


def kernel(x, ffn1_norm, ffn1_w_gate_up, ffn1_w_down, mix_norm, w_in, pool_w_group, pool_scale, w_branch_pool, w_branch_attn, w_out, ffn2_norm, ffn2_w_gate_up, ffn2_w_down, final_norm, loss_target, m_ffn1_norm, m_ffn1_w_gate_up, m_ffn1_w_down, m_mix_norm, m_w_in, m_pool_w_group, m_pool_scale, m_w_branch_pool, m_w_branch_attn, m_w_out, m_ffn2_norm, m_ffn2_w_gate_up, m_ffn2_w_down, m_final_norm, v_ffn1_norm, v_ffn1_w_gate_up, v_ffn1_w_down, v_mix_norm, v_w_in, v_pool_w_group, v_pool_scale, v_w_branch_pool, v_w_branch_attn, v_w_out, v_ffn2_norm, v_ffn2_w_gate_up, v_ffn2_w_down, v_final_norm):
    wts = dict(ffn1_norm=ffn1_norm, ffn1_w_gate_up=ffn1_w_gate_up, ffn1_w_down=ffn1_w_down, mix_norm=mix_norm,
               w_in=w_in, pool_w_group=pool_w_group, pool_scale=pool_scale, w_branch_pool=w_branch_pool,
               w_branch_attn=w_branch_attn, w_out=w_out, ffn2_norm=ffn2_norm, ffn2_w_gate_up=ffn2_w_gate_up,
               ffn2_w_down=ffn2_w_down, final_norm=final_norm)
    mom = dict(ffn1_norm=m_ffn1_norm, ffn1_w_gate_up=m_ffn1_w_gate_up, ffn1_w_down=m_ffn1_w_down,
               mix_norm=m_mix_norm, w_in=m_w_in, pool_w_group=m_pool_w_group, pool_scale=m_pool_scale,
               w_branch_pool=m_w_branch_pool, w_branch_attn=m_w_branch_attn, w_out=m_w_out,
               ffn2_norm=m_ffn2_norm, ffn2_w_gate_up=m_ffn2_w_gate_up, ffn2_w_down=m_ffn2_w_down,
               final_norm=m_final_norm)
    var = dict(ffn1_norm=v_ffn1_norm, ffn1_w_gate_up=v_ffn1_w_gate_up, ffn1_w_down=v_ffn1_w_down,
               mix_norm=v_mix_norm, w_in=v_w_in, pool_w_group=v_pool_w_group, pool_scale=v_pool_scale,
               w_branch_pool=v_w_branch_pool, w_branch_attn=v_w_branch_attn, w_out=v_w_out,
               ffn2_norm=v_ffn2_norm, ffn2_w_gate_up=v_ffn2_w_gate_up, ffn2_w_down=v_ffn2_w_down,
               final_norm=v_final_norm)

    c_idx = lax.axis_index("c").astype(jnp.int32).reshape(1)
    me_idx = (2 * lax.axis_index("x") + lax.axis_index("y")).astype(jnp.int32).reshape(1)
    place = jnp.concatenate([me_idx, c_idx])
    x0, tgt = x[0], loss_target[0]
    wgrp = pool_w_group[0].astype(BF16)
    g1, gm, g2, gf = ffn1_norm, mix_norm, ffn2_norm, final_norm.reshape(1, D)
    grad, delta, new_m, new_v = {}, {}, {}, {}

    def pair_sums(keys, parts, got):
        return [_pair_sum(parts[i], got[i], c_idx, "pair_sum_" + k) for i, k in enumerate(keys)]

    def chip_sums(keys, chip_parts, owned):
        return [_chip_sum(chip_parts[i], owned[i], place, "chip_sum_" + k) for i, k in enumerate(keys)]

    def adamw(k, after=()):
        outs = _adamw(wts[k][0], grad[k][0], mom[k][0], var[k][0], "adamw_" + k, after=after)
        grad[k], delta[k], new_m[k], new_v[k] = (o.reshape(wts[k].shape) for o in outs)

    own = {k: _cast_into_block(wts[k][0], me_idx, "cast_" + k) for k in BIG}
    first, late = ("ffn1_w_gate_up", "ffn1_w_down"), ("w_branch_pool", "w_branch_attn", "w_out",
                                                       "ffn2_w_gate_up", "ffn2_w_down")
    full = dict(zip(first, _exchange_alone(_ex_gather([own[k] for k in first]), "gather_ffn1")))
    wgu1, wd1 = full["ffn1_w_gate_up"], full["ffn1_w_down"].reshape(DFF, D)
    (h1, n1, gu1, a1), (win,) = _ffn_fwd(x0, g1, wgu1, wd1, "ffn1_fwd", exchange=_ex_gather_direct([own["w_in"]]))
    sems_l, thru_l, token_l = _gather_start([own[k_] for k_ in late], [h1], "gather_late_start")
    u, xp, q, k, v, gp, gs = _mix_in(h1, gm, win, after=(token_l,))
    o_sb, ctot = _attn_fwd(q, k, v)
    arrived = _gather_wait(sems_l, thru_l, [o_sb], "gather_late_wait")
    wbp, wba, wout = _exchange_alone(_ex_relay(arrived[:3]), "relay_mix")
    wout = wout.reshape(D, D)
    (h2, pm, p, yp, ys, mm), (wgu2, wd2) = _mix_out(h1, xp, o_sb, gp, gs, wgrp, pool_scale, wbp, wba, wout,
                                                    exchange=_ex_relay(arrived[3:]))
    wd2 = wd2.reshape(DFF, D)
    dh3, loss_row, d_gf, n3, gu3, a3 = _ffn_fwd(h2, g2, wgu2, wd2, "ffn2_fwd", head=(tgt, gf))

    def grad_gate_up(n, dgu, name, exchange=None):
        res = _wgrad(n, dgu, NSH, D, name, exchange=exchange)
        return [_halves(res)] if exchange is None else ([_halves(res[0])], res[1])

    def grad_down(a, dh, name, exchange=None):
        res = _wgrad(a, dh, 1, FFS, name, exchange=exchange)
        halves = lambda g: [_halves(g.reshape(NSH, DFF // NSH, D))]
        return halves(res) if exchange is None else (halves(res[0]), res[1])

    k_gu2, k_d2, k_gu1, k_d1, k_in = (("ffn2_w_gate_up",), ("ffn2_w_down",), ("ffn1_w_gate_up",),
                                      ("ffn1_w_down",), ("w_in",))
    dh2, dgu3, d_g2 = _ffn_bwd(dh3, h2, g2, gu3, wgu2, wd2, "ffn2_bwd")
    pa = grad_gate_up(n3, dgu3, "wgrad_gu2") + grad_down(a3, dh3, "wgrad_d2")
    (dlg, dyp, dys, do_sb, dyg, dxp, d_scale), got_a = _mix_bwd_out(
        dh2, gp, gs, yp, ys, pm, wgrp, pool_scale, wbp, wba, wout, exchange=_ex_pair_swap(pa))
    chip_a = pair_sums(k_gu2 + k_d2, pa, got_a)
    kb = ("w_out", "w_branch_pool", "w_branch_attn")
    pb = [_halves(_wgrad(mm, dh2, 1, D, "wgrad_out").reshape(NSH, D // NSH, D)),
          _halves(_wgrad(p, dyp, NSH, PW, "wgrad_bp")), _halves(_wgrad(o_sb, dys, NSH, SBW, "wgrad_ba"))]
    k_a, k_in = k_gu2 + k_d2, k_in + kb
    sems_a, thru_a, token_a = _scatter_start(chip_a, "scatter_a_start")
    dq, dk, dv = _attn_bwd(q, k, v, do_sb, ctot, after=(token_a,))
    chip_a, owned_a = _scatter_wait(sems_a, thru_a, [dq], "scatter_a_wait")
    halves_a = chip_sums(k_a, chip_a, owned_a)
    (dh1, d_gm, dproj), both_a = _mix_bwd_in(dh2, h1, gm, (dxp, dq, dk, dv, dlg), win, exchange=_ex_share(halves_a))
    for i, k_ in enumerate(k_a):
        grad[k_] = both_a[i].reshape(wts[k_].shape)

    p_in = [_halves(_wgrad(u, dproj, NSH, D, "wgrad_in"))] + pb
    p_d1, got_in = grad_down(a1, dh1, "wgrad_d1", exchange=_ex_pair_swap(p_in))
    sems_in, thru_in, token_in = _scatter_start(pair_sums(k_in, p_in, got_in), "scatter_in_start")
    dgu1, got_d1 = _ffn_bwd_act(dh1, gu1, wd1, "ffn1_bwd_act", exchange=_ex_pair_swap(p_d1), after=(token_in,))
    sems_d1, thru_d1, token_d1 = _scatter_start(pair_sums(k_d1, p_d1, got_d1), "scatter_d1_start")
    p_gu1 = [_halves(_wgrad(n1, dgu1, NSH, D, "wgrad_gu1", after=(token_in, token_d1)))]
    chip_in, owned_in = _scatter_wait(sems_in, thru_in, p_gu1, "scatter_in_wait")
    chip_d1, owned_d1 = _scatter_wait(sems_d1, thru_d1, p_gu1, "scatter_d1_wait")
    halves_in, halves_d1 = chip_sums(k_in, chip_in, owned_in), chip_sums(k_d1, chip_d1, owned_d1)
    landed = _exchange_alone(_join(_ex_pair_swap(p_gu1), _ex_share(halves_in)), "pair_swap_gu1")
    for i, k_ in enumerate(k_in):
        grad[k_] = landed[1 + i].reshape(wts[k_].shape)
    sems, thru, token = _scatter_start(pair_sums(k_gu1, p_gu1, landed[:1]), "scatter_gu1_start")
    for k_ in k_a + k_in:
        adamw(k_, after=(token,))
    dx, d_g1 = _ffn_bwd_in(dh1, x0, g1, dgu1, wgu1, "ffn1_bwd_in", after=(token,))
    chip_gu1, owned_gu1 = _scatter_wait(sems, thru, [dx] + [delta[k_] for k_ in k_a + k_in], "scatter_gu1_wait")

    small_g = dict(ffn1_norm=d_g1, mix_norm=d_gm, ffn2_norm=d_g2, final_norm=d_gf, pool_scale=d_scale,
                   pool_w_group=_wgrad_groups(pm, dyg), loss=loss_row)
    gathered, both = _gather_small(_pack_small(small_g), _ex_share(halves_d1 + chip_sums(k_gu1, chip_gu1, owned_gu1)))
    grad["ffn1_w_down"] = both[0].reshape(ffn1_w_down.shape)
    grad["ffn1_w_gate_up"] = both[1].reshape(ffn1_w_gate_up.shape)
    for k_ in k_d1 + k_gu1:
        adamw(k_)
    sg, sd, sm, sv = _small_update(gathered, _pack_small(wts), _pack_small(mom), _pack_small(var))
    sums = _unpack_small(sg, wts)
    loss = sums.pop("loss")
    grad.update(sums)
    for dst, packed in ((delta, sd), (new_m, sm), (new_v, sv)):
        vals = _unpack_small(packed, wts)
        vals.pop("loss")
        dst.update(vals)
    return (loss, dx[None], *[grad[k_] for k_ in ORDER], *[delta[k_] for k_ in ORDER],
            *[new_m[k_] for k_ in ORDER], *[new_v[k_] for k_ in ORDER])
```

```python
import functools

import jax
import jax.numpy as jnp
from jax import lax
from jax.experimental import pallas as pl
from jax.experimental.pallas import tpu as pltpu

F32 = jnp.float32
BF16 = jnp.bfloat16

S = 2048
D = 1024
DFF = 2816
FFS = 2 * DFF // 4
NSH = 4
PW = 512
PG = 128
POOL_WINDOWS = (2, 4, 8, 16)
HALO = 16
SBW = 512
DH = 64
EPS = 1e-6
SCALE = 0.125
LOG2E = 1.4426950408889634
TA = 256
QB = 2
MIB = 1024 * 1024

LR, B1, B2, AEPS, WD, STEP = 0.001, 0.9, 0.999, 1e-08, 0.01, 10

_VM = pl.BlockSpec(memory_space=pltpu.VMEM)
_ANY = pl.BlockSpec(memory_space=pl.ANY)
MESH = pl.DeviceIdType.MESH


def _nn(a, b):
    return jnp.dot(a, b, preferred_element_type=F32)


def _nt(a, b):
    return lax.dot_general(a, b, (((1,), (1,)), ((), ())), preferred_element_type=F32)


def _tn(a, b):
    return lax.dot_general(a, b, (((0,), (0,)), ((), ())), preferred_element_type=F32)


def _params(sem, vmem_mib):
    return pltpu.CompilerParams(dimension_semantics=sem, vmem_limit_bytes=vmem_mib * MIB)


def _rows(tm, width):
    return pl.BlockSpec((tm, width), lambda i: (i, 0))


def _fixed(shape):
    return pl.BlockSpec(shape, lambda *_: (0,) * len(shape))


def _sds(shape, dtype):
    return pltpu.HBM(shape, dtype)


def _in_hbm(args):
    return [pltpu.with_memory_space_constraint(a, pltpu.HBM) for a in args]


def _stage(pairs):
    @pl.when(pl.program_id(0) == 0)
    def _():
        for src, dst in pairs:
            pltpu.sync_copy(src, dst)


def _vmem_like(*arrays):
    return [pltpu.VMEM(a.shape, a.dtype) for a in arrays]


class Exchange:
    def __init__(self, arrays, landing, aliases, n_sems, start, finish):
        self.arrays, self.landing, self.aliases, self.n_sems = list(arrays), list(landing), dict(aliases), n_sems
        self.start, self.finish = start, finish


def _join(a, b):
    na, la = len(a.arrays), len(a.landing)

    def both(fa, fb):
        def run(ins, outs, ssem, rsem):
            fa(ins[:na], outs[:la], ssem.at[pl.ds(0, a.n_sems)], rsem.at[pl.ds(0, a.n_sems)])
            fb(ins[na:], outs[la:], ssem.at[pl.ds(a.n_sems, b.n_sems)], rsem.at[pl.ds(a.n_sems, b.n_sems)])
        return run

    aliases = {**a.aliases, **{na + i: la + j for i, j in b.aliases.items()}}
    return Exchange(a.arrays + b.arrays, a.landing + b.landing, aliases, a.n_sems + b.n_sems,
                    both(a.start, b.start), both(a.finish, b.finish))


def _call(body, args, *, name, grid, in_specs, out_specs, out_shape, scratch_shapes=(), compiler_params=None,
          exchange=None, free=(), after=()):
    args = [a if i in free else pltpu.with_memory_space_constraint(a, pltpu.HBM) for i, a in enumerate(args)]
    if exchange is None:
        n_in = len(in_specs)

        def plain(*refs):
            body(*refs[:n_in], *refs[n_in + len(after):])

        return pl.pallas_call(plain, name=name, grid=grid, in_specs=list(in_specs) + [_ANY] * len(after),
                              out_specs=out_specs, out_shape=out_shape, scratch_shapes=list(scratch_shapes),
                              compiler_params=compiler_params)(*args, *after)
    ex = exchange
    n_in, n_out, n_scr = len(in_specs), len(out_specs), len(scratch_shapes)
    na, nl = len(ex.arrays), len(ex.landing)

    def hosted(*refs):
        at = [0]

        def take(n):
            at[0] += n
            return refs[at[0] - n:at[0]]

        k_in, _, e_in, k_out, e_out, k_scr = take(n_in), take(len(after)), take(na), take(n_out), take(nl), take(n_scr)
        ssem, rsem = take(2)
        ids = [pl.program_id(a) for a in range(len(grid))]
        first = functools.reduce(jnp.logical_and, [i == 0 for i in ids])
        last = functools.reduce(jnp.logical_and, [i == g - 1 for i, g in zip(ids, grid)])

        @pl.when(first)
        def _():
            ex.start(e_in, e_out, ssem, rsem)

        body(*k_in, *k_out, *k_scr)

        @pl.when(last)
        def _():
            ex.finish(e_in, e_out, ssem, rsem)

    outs = pl.pallas_call(
        hosted, name=name, grid=grid,
        in_specs=list(in_specs) + [_ANY] * (len(after) + na), out_specs=list(out_specs) + [_ANY] * nl,
        out_shape=list(out_shape) + ex.landing,
        scratch_shapes=list(scratch_shapes) + [pltpu.SemaphoreType.DMA((ex.n_sems,))] * 2,
        input_output_aliases={n_in + len(after) + i: n_out + j for i, j in ex.aliases.items()},
        compiler_params=compiler_params,
    )(*args, *after, *_in_hbm(ex.arrays))
    return outs[:n_out], outs[n_out:]


def _exchange_alone(ex, name):
    def body(*refs):
        na, nl = len(ex.arrays), len(ex.landing)
        ex.start(refs[:na], refs[na:na + nl], refs[-2], refs[-1])
        ex.finish(refs[:na], refs[na:na + nl], refs[-2], refs[-1])

    return pl.pallas_call(
        body, name=name, in_specs=[_ANY] * len(ex.arrays), out_specs=[_ANY] * len(ex.landing),
        out_shape=ex.landing, scratch_shapes=[pltpu.SemaphoreType.DMA((ex.n_sems,))] * 2,
        input_output_aliases=ex.aliases,
    )(*_in_hbm(ex.arrays))


_HBM = pl.BlockSpec(memory_space=pltpu.HBM)
_SEM = pl.BlockSpec(memory_space=pltpu.SEMAPHORE)
_EFFECT = pltpu.SideEffectType.DATAFLOW_SIDE_EFFECTING


def _scatter_copies(srcs, lands, ssems, rsems):
    x, y, c, chips = _place()
    return [_remote(srcs[w].at[2 * px + py], lands[w].at[k], ssems[3 * w + k], rsems[3 * w + k], (px, py, c))
            for w in range(len(srcs)) for k, (px, py) in enumerate(chips)]


def _scatter_start(parts, name):
    n, ncp = len(parts), 3 * len(parts)
    lands = [lax.empty((3,) + p.shape[1:], p.dtype) for p in parts]

    def body(*refs):
        srcs, land_refs = refs[:n], refs[n:2 * n]
        ssems, rsems = refs[2 * n:2 * n + ncp], refs[2 * n + ncp:2 * n + 2 * ncp]
        for cp in _scatter_copies(srcs, land_refs, ssems, rsems):
            cp.start()
        token = refs[-1]
        token[...] = jnp.zeros_like(token)

    outs = pl.pallas_call(
        body, name=name,
        out_shape=([pltpu.SemaphoreType.DMA(())] * (2 * ncp) + [pltpu.HBM(a.shape, a.dtype) for a in parts + lands]
                   + [jax.ShapeDtypeStruct((8, 128), F32)]),
        in_specs=[_HBM] * (2 * n), out_specs=[_SEM] * (2 * ncp) + [_HBM] * (2 * n) + [_VM],
        input_output_aliases={i: 2 * ncp + i for i in range(2 * n)},
        compiler_params=pltpu.CompilerParams(has_side_effects=_EFFECT),
    )(*_in_hbm(parts), *_in_hbm(lands))
    sems, thru, token = outs[:2 * ncp], outs[2 * ncp:2 * ncp + 2 * n], outs[-1]
    return sems, thru, token


def _scatter_wait(sems, thru, after, name):
    n = len(thru) // 2
    ncp = 3 * n

    def body(*refs):
        srcs, land_refs = refs[:n], refs[n:2 * n]
        ssems, rsems = refs[2 * n:2 * n + ncp], refs[2 * n + ncp:2 * n + 2 * ncp]
        for cp in _scatter_copies(srcs, land_refs, ssems, rsems):
            cp.wait_send()
            cp.wait_recv()

    outs = pl.pallas_call(
        body, name=name, out_shape=[pltpu.HBM(a.shape, a.dtype) for a in thru],
        in_specs=[_HBM] * (2 * n) + [_SEM] * (2 * ncp) + [_ANY] * len(after), out_specs=[_HBM] * (2 * n),
        input_output_aliases={i: i for i in range(2 * n)},
        compiler_params=pltpu.CompilerParams(has_side_effects=_EFFECT),
    )(*thru, *sems, *after)
    return outs[:n], outs[n:]


def _gather_copies(bufs, ssems, rsems, sending):
    x, y, c, chips = _place()
    out = []
    for w, ref in enumerate(bufs):
        half = ref.shape[1] // 2
        for k, (px, py) in enumerate(chips):
            rows = ref.at[2 * x + y if sending else 2 * px + py, pl.ds(c * half, half)]
            out.append(_remote(rows, rows, ssems[3 * w + k], rsems[3 * w + k], (px, py, c)))
    return out


def _gather_start(bufs, after, name):
    n, ncp = len(bufs), 3 * len(bufs)

    def body(*refs):
        ssems, rsems = refs[n + len(after):n + len(after) + ncp], refs[n + len(after) + ncp:n + len(after) + 2 * ncp]
        for cp in _gather_copies(refs[:n], ssems, rsems, True):
            cp.start()
        token = refs[-1]
        token[...] = jnp.zeros_like(token)

    outs = pl.pallas_call(
        body, name=name,
        out_shape=([pltpu.SemaphoreType.DMA(())] * (2 * ncp) + [pltpu.HBM(a.shape, a.dtype) for a in bufs]
                   + [jax.ShapeDtypeStruct((8, 128), F32)]),
        in_specs=[_HBM] * n + [_ANY] * len(after), out_specs=[_SEM] * (2 * ncp) + [_HBM] * n + [_VM],
        input_output_aliases={i: 2 * ncp + i for i in range(n)},
        compiler_params=pltpu.CompilerParams(has_side_effects=_EFFECT),
    )(*_in_hbm(bufs), *after)
    return outs[:2 * ncp], outs[2 * ncp:2 * ncp + n], outs[-1]


def _gather_wait(sems, thru, after, name):
    n = len(thru)
    ncp = 3 * n

    def body(*refs):
        ssems, rsems = refs[n:n + ncp], refs[n + ncp:n + 2 * ncp]
        for cp in _gather_copies(refs[:n], ssems, rsems, True):
            cp.wait_send()
        for cp in _gather_copies(refs[:n], ssems, rsems, False):
            cp.wait_recv()

    return pl.pallas_call(
        body, name=name, out_shape=[pltpu.HBM(a.shape, a.dtype) for a in thru],
        in_specs=[_HBM] * n + [_SEM] * (2 * ncp) + [_ANY] * len(after), out_specs=[_HBM] * n,
        input_output_aliases={i: i for i in range(n)},
        compiler_params=pltpu.CompilerParams(has_side_effects=_EFFECT),
    )(*thru, *sems, *after)


def _rms(x):
    r = lax.rsqrt(jnp.mean(x * x, axis=-1, keepdims=True) + EPS)
    return r, x * r


def _rms_bwd(dn, xr, r, gain):
    dng = dn * gain
    dx = r * (dng - xr * jnp.mean(dng * xr, axis=-1, keepdims=True))
    return dx, jnp.sum(dn * xr, axis=0, keepdims=True)


def _ffn_fwd(x, gain, wgu, wd, name, exchange=None, head=None):
    tm = 256

    def body(x_ref, g_ref, wgu_hbm, wd_hbm, *rest):
        if head is None:
            h_ref, n_ref, gu_ref, a_ref, wgu_ref, wd_ref = rest
        else:
            t_ref, gf_ref, h_ref, loss_ref, dgf_ref, n_ref, gu_ref, a_ref, wgu_ref, wd_ref = rest
        _stage([(wgu_hbm, wgu_ref), (wd_hbm, wd_ref)])
        x = x_ref[...]
        _, xr = _rms(x)
        n = (xr * g_ref[...]).astype(BF16)
        n_ref[...] = n
        acc = jnp.zeros((tm, D), F32)
        for j in range(2):
            g = _nn(n, wgu_ref[j])
            u = _nn(n, wgu_ref[2 + j])
            gu_ref[:, j * FFS:(j + 1) * FFS] = g.astype(BF16)
            gu_ref[:, (2 + j) * FFS:(3 + j) * FFS] = u.astype(BF16)
            half_act = (0.5 * (g * jax.nn.sigmoid(g) * u)).astype(BF16)
            a_ref[:, j * FFS:(j + 1) * FFS] = half_act
            acc = acc + _nn(half_act, wd_ref[j * FFS:(j + 1) * FFS, :])
        h = x + acc
        if head is None:
            h_ref[...] = h
            return
        gf = gf_ref[...]
        r, hr = _rms(h)
        err = hr * gf - t_ref[...]
        dh, dgain = _rms_bwd(err * (1.0 / D), hr, r, gf)
        h_ref[...] = dh

        @pl.when(pl.program_id(0) == 0)
        def _():
            dgf_ref[...] = jnp.zeros_like(dgf_ref)
            loss_ref[...] = jnp.zeros_like(loss_ref)

        dgf_ref[...] += dgain
        loss_ref[...] += jnp.full((1, 128), (0.5 / D) * jnp.sum(err * err), F32)

    saved_specs = [_rows(tm, D), _rows(tm, 4 * FFS), _rows(tm, DFF)]
    saved_shapes = [_sds((S, D), BF16), _sds((S, 4 * FFS), BF16), _sds((S, DFF), BF16)]
    if head is None:
        return _call(
            body, (x, gain, wgu, wd), name=name, grid=(S // tm,),
            in_specs=[_rows(tm, D), _fixed((1, D)), _ANY, _ANY],
            out_specs=[_rows(tm, D)] + saved_specs, out_shape=[_sds((S, D), F32)] + saved_shapes,
            scratch_shapes=_vmem_like(wgu, wd),
            compiler_params=_params(("arbitrary",), 56), exchange=exchange)
    return _call(
        body, (x, gain, wgu, wd, *head), name=name, grid=(S // tm,),
        in_specs=[_rows(tm, D), _fixed((1, D)), _ANY, _ANY, _rows(tm, D), _fixed((1, D))],
        out_specs=[_rows(tm, D), _fixed((1, 128)), _fixed((1, D))] + saved_specs,
        out_shape=[_sds((S, D), F32), _sds((1, 128), F32), _sds((1, D), F32)] + saved_shapes,
        scratch_shapes=_vmem_like(wgu, wd),
        compiler_params=_params(("arbitrary",), 56), exchange=exchange, free=(4, 5))


def _ffn_bwd(dh, x, gain, gu, wgu, wd, name):
    tm = 256

    def body(dh_ref, x_ref, g_ref, gu_ref, wgu_hbm, wd_hbm, dx_ref, dgu_ref, dg_ref, wgu_ref, wd_ref):
        _stage([(wgu_hbm, wgu_ref), (wd_hbm, wd_ref)])
        dh = dh_ref[...]
        dhb = dh.astype(BF16)
        dn = jnp.zeros((tm, D), F32)
        for j in range(2):
            g = gu_ref[:, j * FFS:(j + 1) * FFS].astype(F32)
            u = gu_ref[:, (2 + j) * FFS:(3 + j) * FFS].astype(F32)
            da = 0.5 * _nt(dhb, wd_ref[j * FFS:(j + 1) * FFS, :])
            sg = jax.nn.sigmoid(g)
            dgb = (da * u * (sg * (1.0 + g * (1.0 - sg)))).astype(BF16)
            dub = (da * (g * sg)).astype(BF16)
            dgu_ref[:, j * FFS:(j + 1) * FFS] = dgb
            dgu_ref[:, (2 + j) * FFS:(3 + j) * FFS] = dub
            dn = dn + _nt(dgb, wgu_ref[j]) + _nt(dub, wgu_ref[2 + j])
        r, xr = _rms(x_ref[...])
        dx, dgain = _rms_bwd(dn, xr, r, g_ref[...])
        dx_ref[...] = dh + dx

        @pl.when(pl.program_id(0) == 0)
        def _():
            dg_ref[...] = jnp.zeros_like(dg_ref)

        dg_ref[...] += dgain

    return _call(
        body, (dh, x, gain, gu, wgu, wd), name=name, grid=(S // tm,),
        in_specs=[_rows(tm, D), _rows(tm, D), _fixed((1, D)), _rows(tm, 4 * FFS), _ANY, _ANY],
        out_specs=[_rows(tm, D), _rows(tm, 4 * FFS), _fixed((1, D))],
        out_shape=[_sds((S, D), F32), _sds((S, 4 * FFS), BF16), _sds((1, D), F32)],
        scratch_shapes=_vmem_like(wgu, wd), compiler_params=_params(("arbitrary",), 56))


def _ffn_bwd_act(dh, gu, wd, name, exchange=None, after=()):
    tm = 512

    def body(dh_ref, gu_ref, wd_hbm, dgu_ref, wd_ref):
        _stage([(wd_hbm, wd_ref)])
        dhb = dh_ref[...].astype(BF16)
        for j in range(2):
            g = gu_ref[:, j * FFS:(j + 1) * FFS].astype(F32)
            u = gu_ref[:, (2 + j) * FFS:(3 + j) * FFS].astype(F32)
            da = 0.5 * _nt(dhb, wd_ref[j * FFS:(j + 1) * FFS, :])
            sg = jax.nn.sigmoid(g)
            dgu_ref[:, j * FFS:(j + 1) * FFS] = (da * u * (sg * (1.0 + g * (1.0 - sg)))).astype(BF16)
            dgu_ref[:, (2 + j) * FFS:(3 + j) * FFS] = (da * (g * sg)).astype(BF16)

    res = _call(
        body, (dh, gu, wd), name=name, grid=(S // tm,),
        in_specs=[_rows(tm, D), _rows(tm, 4 * FFS), _ANY], out_specs=[_rows(tm, 4 * FFS)],
        out_shape=[_sds((S, 4 * FFS), BF16)], scratch_shapes=_vmem_like(wd),
        compiler_params=_params(("arbitrary",), 56), exchange=exchange, after=after)
    return res[0] if exchange is None else (res[0][0], res[1])


def _ffn_bwd_in(dh, x, gain, dgu, wgu, name, exchange=None, after=()):
    tm = 512

    def body(dh_ref, x_ref, g_ref, dgu_ref, wgu_hbm, dx_ref, dg_ref, wgu_ref):
        _stage([(wgu_hbm, wgu_ref)])
        dn = jnp.zeros((tm, D), F32)
        for j in range(NSH):
            dn = dn + _nt(dgu_ref[:, j * FFS:(j + 1) * FFS], wgu_ref[j])
        r, xr = _rms(x_ref[...])
        dx, dgain = _rms_bwd(dn, xr, r, g_ref[...])
        dx_ref[...] = dh_ref[...] + dx

        @pl.when(pl.program_id(0) == 0)
        def _():
            dg_ref[...] = jnp.zeros_like(dg_ref)

        dg_ref[...] += dgain

    return _call(
        body, (dh, x, gain, dgu, wgu), name=name, grid=(S // tm,),
        in_specs=[_rows(tm, D), _rows(tm, D), _fixed((1, D)), _rows(tm, 4 * FFS), _ANY],
        out_specs=[_rows(tm, D), _fixed((1, D))],
        out_shape=[_sds((S, D), F32), _sds((1, D), F32)],
        scratch_shapes=_vmem_like(wgu),
        compiler_params=_params(("arbitrary",), 56), exchange=exchange, after=after)


def _mix_in(h, gain, w_in, after=()):
    tm = 512

    def body(h_ref, g_ref, w_hbm, u_ref, xp_ref, q_ref, k_ref, v_ref, gp_ref, gs_ref, w_ref):
        _stage([(w_hbm, w_ref)])
        _, hr = _rms(h_ref[...])
        u = (hr * g_ref[...]).astype(BF16)
        u_ref[...] = u
        p0 = _nn(u, w_ref[0])
        xp_ref[...] = p0[:, :PW]
        q_ref[...] = p0[:, PW:].astype(BF16)
        p1 = _nn(u, w_ref[1])
        k_ref[...] = p1[:, :SBW].astype(BF16)
        v_ref[...] = p1[:, SBW:].astype(BF16)
        gp_ref[...] = jax.nn.sigmoid(_nn(u, w_ref[2])).astype(BF16)
        gs_ref[...] = jax.nn.sigmoid(_nn(u, w_ref[3])).astype(BF16)

    return _call(
        body, (h, gain, w_in), name="mix_in", grid=(S // tm,),
        in_specs=[_rows(tm, D), _fixed((1, D)), _ANY],
        out_specs=[_rows(tm, D), _rows(tm, PW), _rows(tm, SBW), _rows(tm, SBW), _rows(tm, SBW),
                   _rows(tm, D), _rows(tm, D)],
        out_shape=[_sds((S, D), BF16), _sds((S, PW), F32), _sds((S, SBW), BF16), _sds((S, SBW), BF16),
                   _sds((S, SBW), BF16), _sds((S, D), BF16), _sds((S, D), BF16)],
        scratch_shapes=_vmem_like(w_in),
        compiler_params=_params(("arbitrary",), 48), free=(1,), after=after)


def _hilo_dot(x, tri):
    hi = x.astype(BF16)
    lo = (x - hi.astype(F32)).astype(BF16)
    return _nn(hi, tri) + _nn(lo, tri)


def _log_terms(qk):
    z2 = qk * (SCALE * LOG2E)
    lb = jnp.minimum(z2, 0.0) - jnp.log2(1.0 + jnp.exp2(-jnp.abs(z2)))
    return lb, lb - z2


def _head_masks():
    lane = lax.broadcasted_iota(jnp.int32, (1, 2 * DH), 1)
    return (lane < DH, lane >= DH)


def _attn_fwd(q, k, v, exchange=None):
    T = TA

    def body(q_ref, k_ref, v_ref, o_ref, c_ref):
        i2 = 2 * pl.program_id(1)
        row = lax.broadcasted_iota(jnp.int32, (T, T), 0)
        col = lax.broadcasted_iota(jnp.int32, (T, T), 1)
        after = (row > col).astype(BF16)
        causal = col < row
        masks = _head_masks()
        qms = {}
        for b in range(QB):
            q2 = q_ref[b * T:(b + 1) * T, :]
            for h, hm in enumerate(masks):
                qms[b, h] = jnp.where(hm, q2, jnp.zeros_like(q2))

        def blocks(keys, pairs, carries, os):
            ks, vms = [], []
            for j in keys:
                rows = pl.ds(pl.multiple_of(j * T, T), T)
                vj = v_ref[rows, :]
                ks.append(k_ref[rows, :])
                vms.append([jnp.where(hm, vj, jnp.zeros_like(vj)) for hm in masks])
            units = [(n, h) for n in range(len(pairs)) for h in range(2)]
            qks = {(n, h): _nt(qms[pairs[n][0], h], ks[pairs[n][1]]) for n, h in units}
            lbs, l1ms = {}, {}
            for u in units:
                lbs[u], l1m = _log_terms(qks[u])
                l1ms[u] = jnp.where(causal, l1m, 0.0) if pairs[u[0]][2] else l1m
            cins = {u: _hilo_dot(l1ms[u], after) for u in units}
            carries, os = dict(carries), list(os)
            for n, h in units:
                b, key, diag = pairs[n]
                a = jnp.exp2(lbs[n, h] + cins[n, h] + carries[b, h])
                if diag:
                    a = jnp.where(causal, a, 0.0)
                os[b] = os[b] + _nn(a.astype(BF16), vms[key][h])
                carries[b, h] = carries[b, h] + jnp.sum(l1ms[n, h], axis=1, keepdims=True)
            return carries, tuple(os)

        carries = {(b, h): jnp.zeros((T, 1), F32) for b in range(QB) for h in range(2)}
        os = tuple(jnp.zeros((T, 2 * DH), F32) for _ in range(QB))
        carries, os = blocks([i2 + 1, i2], [(1, 0, True), (0, 1, True), (1, 1, False)], carries, os)
        carries, os = lax.fori_loop(
            0, i2, lambda jj, c: blocks([i2 - 1 - jj], [(0, 0, False), (1, 0, False)], c[0], c[1]), (carries, os))
        for b in range(QB):
            o_ref[b * T:(b + 1) * T, :] = os[b].astype(BF16)
            c_ref[b * T:(b + 1) * T, :] = jnp.where(masks[0], carries[b, 0], carries[b, 1])

    blk = pl.BlockSpec((QB * T, 2 * DH), lambda p, i: (i, p))
    full = pl.BlockSpec((S, 2 * DH), lambda p, i: (0, p))
    return _call(
        body, (q, k, v), name="attn_fwd", grid=(SBW // (2 * DH), S // (QB * T)),
        in_specs=[blk, full, full], out_specs=[blk, blk],
        out_shape=[_sds((S, SBW), BF16), _sds((S, SBW), F32)],
        compiler_params=_params(("arbitrary", "arbitrary"), 40), exchange=exchange)


def _attn_bwd(q, k, v, do, ctot, after=()):
    T = TA
    nq = S // (QB * T)

    def body(q_ref, k_ref, v_ref, do_ref, c_ref, dq_ref, dk_ref, dv_ref, dk_acc, dv_acc):
        step = pl.program_id(1)
        i2 = 2 * step

        @pl.when(step == 0)
        def _():
            dk_acc[...] = jnp.zeros_like(dk_acc)
            dv_acc[...] = jnp.zeros_like(dv_acc)

        row = lax.broadcasted_iota(jnp.int32, (T, T), 0)
        col = lax.broadcasted_iota(jnp.int32, (T, T), 1)
        upto = (row <= col).astype(BF16)
        before = (row < col).astype(BF16)
        causal = col < row
        masks = _head_masks()
        qms, doms, ctots = {}, {}, {}
        for b in range(QB):
            q2, do2 = q_ref[b * T:(b + 1) * T, :], do_ref[b * T:(b + 1) * T, :]
            for h, hm in enumerate(masks):
                qms[b, h] = jnp.where(hm, q2, jnp.zeros_like(q2))
                doms[b, h] = jnp.where(hm, do2, jnp.zeros_like(do2))
                ctots[b, h] = c_ref[b * T:(b + 1) * T, h * DH:h * DH + 1]

        def blocks(keys, pairs, sums, dqs):
            rows = [pl.ds(pl.multiple_of(j * T, T), T) for j in keys]
            ks, vs = [k_ref[r, :] for r in rows], [v_ref[r, :] for r in rows]
            kms = [[jnp.where(hm, kj, jnp.zeros_like(kj)) for hm in masks] for kj in ks]
            units = [(n, h) for n in range(len(pairs)) for h in range(2)]
            qks = {(n, h): _nt(qms[pairs[n][0], h], ks[pairs[n][1]]) for n, h in units}
            das = {(n, h): _nt(doms[pairs[n][0], h], vs[pairs[n][1]]) for n, h in units}
            lbs, l1ms = {}, {}
            for u in units:
                lbs[u], l1m = _log_terms(qks[u])
                l1ms[u] = jnp.where(causal, l1m, 0.0) if pairs[u[0]][2] else l1m
            pins = {u: _hilo_dot(l1ms[u], upto) for u in units}
            sums = dict(sums)
            a_s, dls, cps = {}, {}, {}
            for n, h in units:
                b, _, diag = pairs[n]
                cl, cp = sums[b, h]
                a = jnp.exp2(lbs[n, h] + (ctots[b, h] - cl) - pins[n, h])
                if diag:
                    a = jnp.where(causal, a, 0.0)
                a_s[n, h] = a.astype(BF16)
                dls[n, h] = das[n, h] * a
                cps[n, h] = cp
                sums[b, h] = (cl + jnp.sum(l1ms[n, h], axis=1, keepdims=True),
                              cp + jnp.sum(dls[n, h], axis=1, keepdims=True))
            pexs = {u: _hilo_dot(dls[u], before) for u in units}
            dzbs = {}
            for u in units:
                dz = dls[u] - jnp.exp2(lbs[u]) * (dls[u] + pexs[u] + cps[u])
                if pairs[u[0]][2]:
                    dz = jnp.where(causal, dz, 0.0)
                dzbs[u] = dz.astype(BF16)
            dqs = list(dqs)
            for n, h in units:
                dqs[pairs[n][0]] = dqs[pairs[n][0]] + _nn(dzbs[n, h], kms[pairs[n][1]][h])
            for key, r in enumerate(rows):
                mine = [(n, h) for n, h in units if pairs[n][1] == key]
                dk_acc[r, :] += functools.reduce(jnp.add, [_tn(dzbs[u], qms[pairs[u[0]][0], u[1]]) for u in mine])
                dv_acc[r, :] += functools.reduce(jnp.add, [_tn(a_s[u], doms[pairs[u[0]][0], u[1]]) for u in mine])
            return sums, tuple(dqs)

        zero = jnp.zeros((T, 1), F32)
        sums = {(b, h): (zero, zero) for b in range(QB) for h in range(2)}
        dqs = tuple(jnp.zeros((T, 2 * DH), F32) for _ in range(QB))
        sums, dqs = lax.fori_loop(
            0, i2, lambda j, c: blocks([j], [(0, 0, False), (1, 0, False)], c[0], c[1]), (sums, dqs))
        _, dqs = blocks([i2, i2 + 1], [(0, 0, True), (1, 0, False), (1, 1, True)], sums, dqs)
        for b in range(QB):
            dq_ref[b * T:(b + 1) * T, :] = (dqs[b] * SCALE).astype(BF16)

        @pl.when(step == nq - 1)
        def _():
            dk_ref[...] = (dk_acc[...] * SCALE).astype(BF16)
            dv_ref[...] = dv_acc[...].astype(BF16)

    blk = pl.BlockSpec((QB * T, 2 * DH), lambda p, i: (i, p))
    full = pl.BlockSpec((S, 2 * DH), lambda p, i: (0, p))
    return _call(
        body, (q, k, v, do, ctot), name="attn_bwd", grid=(SBW // (2 * DH), nq),
        in_specs=[blk, full, full, blk, blk], out_specs=[blk, full, full],
        out_shape=[_sds((S, SBW), BF16), _sds((S, SBW), BF16), _sds((S, SBW), BF16)],
        scratch_shapes=[pltpu.VMEM((S, 2 * DH), F32), pltpu.VMEM((S, 2 * DH), F32)],
        compiler_params=_params(("arbitrary", "arbitrary"), 40), after=after)


def _pool_counts(first_row, tm):
    pos = first_row + lax.broadcasted_iota(jnp.int32, (tm, 1), 0)
    return [jnp.minimum(pos + 1, w).astype(F32) for w in POOL_WINDOWS]


def _mix_out(h, xp, o_sb, gp, gs, w_group, scale, w_bp, w_ba, w_out, exchange=None):
    tm = 512

    def body(h_ref, xp_ref, o_ref, gp_ref, gs_ref, wg_hbm, sc_ref, wbp_hbm, wba_hbm, wo_hbm,
             h2_ref, pm_ref, p_ref, yp_ref, ys_ref, m_ref, halo, wg_ref, wbp_ref, wba_ref, wo_ref):
        _stage([(wg_hbm, wg_ref), (wbp_hbm, wbp_ref), (wba_hbm, wba_ref), (wo_hbm, wo_ref)])
        i = pl.program_id(0)

        @pl.when(i == 0)
        def _():
            halo[...] = jnp.zeros_like(halo)

        xp = xp_ref[...]
        ext = jnp.concatenate([halo[...], xp], axis=0)
        halo[...] = xp[tm - HALO:, :]
        counts = _pool_counts(i * tm, tm)
        for gi in range(len(POOL_WINDOWS)):
            lanes = slice(gi * PG, (gi + 1) * PG)
            win = ext[:, lanes]
            for step in range(gi + 1):
                win = win + pltpu.roll(win, 1 << step, 0)
            pm = (win[HALO:, :] / counts[gi] - xp[:, lanes]).astype(BF16)
            pm_ref[:, lanes] = pm
            p_ref[:, lanes] = (_nn(pm, wg_ref[gi]) * sc_ref[:, lanes]).astype(BF16)
        pb = p_ref[...]
        ob = o_ref[...]
        for j in range(NSH):
            cols = slice(j * (D // NSH), (j + 1) * (D // NSH))
            yp = _nn(pb, wbp_ref[j])
            ys = _nn(ob, wba_ref[j])
            yp_ref[:, cols] = yp.astype(BF16)
            ys_ref[:, cols] = ys.astype(BF16)
            m_ref[:, cols] = (gp_ref[:, cols].astype(F32) * yp + gs_ref[:, cols].astype(F32) * ys).astype(BF16)
        h2_ref[...] = h_ref[...] + _nn(m_ref[...], wo_ref[...])

    return _call(
        body, (h, xp, o_sb, gp, gs, w_group, scale, w_bp, w_ba, w_out), name="mix_out", grid=(S // tm,),
        in_specs=[_rows(tm, D), _rows(tm, PW), _rows(tm, SBW), _rows(tm, D), _rows(tm, D),
                  _ANY, _fixed((1, PW)), _ANY, _ANY, _ANY],
        out_specs=[_rows(tm, D), _rows(tm, PW), _rows(tm, PW), _rows(tm, D), _rows(tm, D), _rows(tm, D)],
        out_shape=[_sds((S, D), F32), _sds((S, PW), BF16), _sds((S, PW), BF16), _sds((S, D), BF16),
                   _sds((S, D), BF16), _sds((S, D), BF16)],
        scratch_shapes=[pltpu.VMEM((HALO, PW), F32)] + _vmem_like(w_group, w_bp, w_ba, w_out),
        compiler_params=_params(("arbitrary",), 48), free=(5, 6), exchange=exchange)


def _mix_bwd_out(dh, gp, gs, yp, ys, pm, w_group, scale, w_bp, w_ba, w_out, exchange=None):
    tm = 512
    nt = S // tm

    def body(dh_ref, gp_ref, gs_ref, yp_ref, ys_ref, pm_ref, wg_hbm, sc_ref, wbp_hbm, wba_hbm, wo_hbm,
             dlg_ref, dyp_ref, dys_ref, do_ref, dyg_ref, dxp_ref, dsc_ref, halo, wg_ref, wbp_ref, wba_ref, wo_ref):
        _stage([(wg_hbm, wg_ref), (wbp_hbm, wbp_ref), (wba_hbm, wba_ref), (wo_hbm, wo_ref)])
        step = pl.program_id(0)

        @pl.when(step == 0)
        def _():
            halo[...] = jnp.zeros_like(halo)
            dsc_ref[...] = jnp.zeros_like(dsc_ref)

        dm = _nt(dh_ref[...].astype(BF16), wo_ref[...])
        gp = gp_ref[...].astype(F32)
        gs = gs_ref[...].astype(F32)
        yp = yp_ref[...].astype(F32)
        ys = ys_ref[...].astype(F32)
        dlg_ref[:, :D] = (dm * yp * gp * (1.0 - gp)).astype(BF16)
        dlg_ref[:, D:] = (dm * ys * gs * (1.0 - gs)).astype(BF16)
        dyp_ref[...] = (dm * gp).astype(BF16)
        dys_ref[...] = (dm * gs).astype(BF16)
        dp = jnp.zeros((tm, PW), F32)
        do = jnp.zeros((tm, SBW), F32)
        for j in range(NSH):
            cols = slice(j * (D // NSH), (j + 1) * (D // NSH))
            dp = dp + _nt(dyp_ref[:, cols], wbp_ref[j])
            do = do + _nt(dys_ref[:, cols], wba_ref[j])
        do_ref[...] = do.astype(BF16)
        counts = _pool_counts((nt - 1 - step) * tm, tm)
        dscale = []
        for gi in range(len(POOL_WINDOWS)):
            lanes = slice(gi * PG, (gi + 1) * PG)
            dpg = dp[:, lanes]
            dscale.append(jnp.sum(dpg * _nn(pm_ref[:, lanes], wg_ref[gi]), axis=0, keepdims=True))
            dyg = (dpg * sc_ref[:, lanes]).astype(BF16)
            dyg_ref[:, lanes] = dyg
            dpm = _nt(dyg, wg_ref[gi])
            per = dpm / counts[gi]
            win = jnp.concatenate([per, halo[:, lanes]], axis=0)
            halo[:, lanes] = per[:HALO, :]
            for s in range(gi + 1):
                win = win + pltpu.roll(win, tm + HALO - (1 << s), 0)
            dxp_ref[:, lanes] = (win[:tm, :] - dpm).astype(BF16)
        dsc_ref[...] += jnp.concatenate(dscale, axis=1)

    rev = lambda width: pl.BlockSpec((tm, width), lambda i: (nt - 1 - i, 0))
    return _call(
        body, (dh, gp, gs, yp, ys, pm, w_group, scale, w_bp, w_ba, w_out), name="mix_bwd_out", grid=(nt,),
        in_specs=[rev(D), rev(D), rev(D), rev(D), rev(D), rev(PW), _ANY, _fixed((1, PW)), _ANY, _ANY, _ANY],
        out_specs=[rev(2 * D), rev(D), rev(D), rev(SBW), rev(PW), rev(PW), _fixed((1, PW))],
        out_shape=[_sds((S, 2 * D), BF16), _sds((S, D), BF16), _sds((S, D), BF16), _sds((S, SBW), BF16),
                   _sds((S, PW), BF16), _sds((S, PW), BF16), _sds((1, PW), F32)],
        scratch_shapes=[pltpu.VMEM((HALO, PW), F32)] + _vmem_like(w_group, w_bp, w_ba, w_out),
        compiler_params=_params(("arbitrary",), 48), exchange=exchange)


def _mix_bwd_in(dh, h, gain, pieces, w_in, exchange=None):
    tm = 512
    widths = [p.shape[1] for p in pieces]

    def body(dh_ref, h_ref, g_ref, *rest):
        piece_refs, (w_hbm, dx_ref, dg_ref, dp_ref, w_ref) = rest[:len(pieces)], rest[len(pieces):]
        _stage([(w_hbm, w_ref)])
        at = 0
        for ref, width in zip(piece_refs, widths):
            dp_ref[:, at:at + width] = ref[...]
            at += width
        du = jnp.zeros((tm, D), F32)
        for j in range(NSH):
            du = du + _nt(dp_ref[:, j * D:(j + 1) * D], w_ref[j])
        r, hr = _rms(h_ref[...])
        dx, dgain = _rms_bwd(du, hr, r, g_ref[...])
        dx_ref[...] = dh_ref[...] + dx

        @pl.when(pl.program_id(0) == 0)
        def _():
            dg_ref[...] = jnp.zeros_like(dg_ref)

        dg_ref[...] += dgain

    return _call(
        body, (dh, h, gain, *pieces, w_in), name="mix_bwd_in", grid=(S // tm,),
        in_specs=[_rows(tm, D), _rows(tm, D), _fixed((1, D))] + [_rows(tm, w) for w in widths] + [_ANY],
        out_specs=[_rows(tm, D), _fixed((1, D)), _rows(tm, 4 * D)],
        out_shape=[_sds((S, D), F32), _sds((1, D), F32), _sds((S, 4 * D), BF16)],
        scratch_shapes=_vmem_like(w_in),
        compiler_params=_params(("arbitrary",), 48), exchange=exchange)


def _wgrad(a, b, nblk, ti, name, out_dtype=BF16, exchange=None, after=()):
    ka, n = a.shape[1], b.shape[1]
    ns = n // nblk

    def body(a_ref, b_ref, o_ref):
        o_ref[...] = _tn(a_ref[...].astype(BF16), b_ref[...].astype(BF16)).astype(out_dtype)

    res = _call(
        body, (a, b), name=name, grid=(nblk, ka // ti),
        in_specs=[pl.BlockSpec((S, ti), lambda j, i: (0, i)), pl.BlockSpec((S, ns), lambda j, i: (0, j))],
        out_specs=[pl.BlockSpec((None, ti, ns), lambda j, i: (j, i, 0))],
        out_shape=[_sds((nblk, ka, ns), out_dtype)],
        compiler_params=_params(("arbitrary", "arbitrary"), 56), exchange=exchange, after=after)
    return res[0] if exchange is None else (res[0][0], res[1])


def _wgrad_groups(pm, dyg):
    def body(a_ref, b_ref, o_ref):
        o_ref[...] = _tn(a_ref[...], b_ref[...])

    col = pl.BlockSpec((S, PG), lambda g: (0, g))
    return pl.pallas_call(
        body, name="wgrad_groups", grid=(PW // PG,),
        in_specs=[col, col], out_specs=pl.BlockSpec((None, PG, PG), lambda g: (g, 0, 0)),
        out_shape=_sds((PW // PG, PG, PG), F32),
        compiler_params=_params(("arbitrary",), 32),
    )(*_in_hbm([pm, dyg]))


def _place():
    x, y, c = lax.axis_index("x"), lax.axis_index("y"), lax.axis_index("c")
    chips = [(1 - x, y), (x, 1 - y), (1 - x, 1 - y)]
    return x, y, c, chips


def _remote(src, dst, ssem, rsem, dev):
    return pltpu.make_async_remote_copy(src_ref=src, dst_ref=dst, send_sem=ssem, recv_sem=rsem,
                                        device_id=dev, device_id_type=MESH)


def _cast_into_block(w, me_idx, name):
    rows, cols = w.shape
    tr = _row_block(rows)

    def body(me_ref, w_ref, o_ref):
        o_ref[...] = w_ref[...].astype(BF16)

    return pl.pallas_call(
        body, name=name, out_shape=_sds((NSH, rows, cols), BF16),
        grid_spec=pltpu.PrefetchScalarGridSpec(
            num_scalar_prefetch=1, grid=(rows // tr,),
            in_specs=[pl.BlockSpec((tr, cols), lambda r, me: (r, 0))],
            out_specs=pl.BlockSpec((None, tr, cols), lambda r, me: (me[0], r, 0))),
        compiler_params=_params(("arbitrary",), 32),
    )(me_idx, w)


def _ex_gather(bufs):
    n = len(bufs)
    per = 8

    def plan(outs, ssem, rsem, w):
        x, y, c, _ = _place()
        sib, nbr_x, nbr_y = (x, y, 1 - c), (1 - x, y, c), (x, 1 - y, c)
        half = outs[w].shape[1] // 2
        quarter = half // 2
        sem = lambda k: (ssem.at[per * w + k], rsem.at[per * w + k])
        rows = lambda blk, start, size: outs[w].at[blk, pl.ds(start, size)]
        mine = rows(2 * x + y, c * half, half)
        from_x = rows(2 * (1 - x) + y, c * half, half)
        from_y = rows(2 * x + (1 - y), c * half, half)
        diag = 2 * (1 - x) + (1 - y)
        pass_y = rows(2 * (1 - x) + y, c * half, quarter)
        pass_x = rows(2 * x + (1 - y), c * half + quarter, quarter)
        diag_0, diag_1 = rows(diag, c * half, quarter), rows(diag, c * half + quarter, quarter)
        first = [_remote(mine, mine, *sem(0), nbr_x), _remote(mine, mine, *sem(1), nbr_y)]
        arrivals = [
            (_remote(from_x, from_x, *sem(0), nbr_x),
             [_remote(pass_y, pass_y, *sem(2), nbr_y), _remote(from_x, from_x, *sem(4), sib)]),
            (_remote(from_y, from_y, *sem(1), nbr_y),
             [_remote(pass_x, pass_x, *sem(3), nbr_x), _remote(from_y, from_y, *sem(5), sib)]),
            (_remote(diag_0, diag_0, *sem(2), nbr_y), [_remote(diag_0, diag_0, *sem(6), sib)]),
            (_remote(diag_1, diag_1, *sem(3), nbr_x), [_remote(diag_1, diag_1, *sem(7), sib)]),
        ]
        other = (1 - c) * half
        from_sibling = [
            _remote(rows(2 * (1 - x) + y, other, half), rows(2 * (1 - x) + y, other, half), *sem(4), sib),
            _remote(rows(2 * x + (1 - y), other, half), rows(2 * x + (1 - y), other, half), *sem(5), sib),
            _remote(rows(diag, other, quarter), rows(diag, other, quarter), *sem(6), sib),
            _remote(rows(diag, other + quarter, quarter), rows(diag, other + quarter, quarter), *sem(7), sib),
        ]
        return first, arrivals, from_sibling

    def start(ins, outs, ssem, rsem):
        x, y, c, _ = _place()
        for w in range(n):
            half = outs[w].shape[1] // 2
            mine = outs[w].at[2 * x + y, pl.ds(c * half, half)]
            _remote(mine, mine, ssem.at[per * w], rsem.at[per * w], (1 - x, y, c)).start()
            _remote(mine, mine, ssem.at[per * w + 1], rsem.at[per * w + 1], (x, 1 - y, c)).start()

    def finish(ins, outs, ssem, rsem):
        plans = [plan(outs, ssem, rsem, w) for w in range(n)]
        started = []
        for direct in (True, False):
            for first, arrivals, _ in plans:
                for arrived, onward in (arrivals[:2] if direct else arrivals[2:]):
                    arrived.wait_recv()
                    for cp in onward:
                        cp.start()
                    started += onward
        for first, _, from_sibling in plans:
            for cp in from_sibling:
                cp.wait_recv()
            started += first
        for cp in started:
            cp.wait_send()

    return Exchange(bufs, [_sds(b.shape, b.dtype) for b in bufs], {w: w for w in range(n)}, per * n, start, finish)


def _ex_gather_direct(bufs):
    n = len(bufs)

    def copies(outs, ssem, rsem, only_first=False):
        x, y, c, chips = _place()
        me, sib = 2 * x + y, (x, y, 1 - c)
        first, relay, last = [], [], []
        for w in range(n):
            half = outs[w].shape[1] // 2
            mine = outs[w].at[me, pl.ds(c * half, half)]
            for k, (px, py) in enumerate(chips):
                sems = (ssem.at[6 * w + k], rsem.at[6 * w + k])
                sib_sems = (ssem.at[6 * w + 3 + k], rsem.at[6 * w + 3 + k])
                first.append(_remote(mine, mine, *sems, (px, py, c)))
                if only_first:
                    continue
                got = outs[w].at[2 * px + py, pl.ds(c * half, half)]
                relay.append((_remote(got, got, *sems, (px, py, c)), _remote(got, got, *sib_sems, sib)))
                theirs = outs[w].at[2 * px + py, pl.ds((1 - c) * half, half)]
                last.append(_remote(theirs, theirs, *sib_sems, sib))
        return first, relay, last

    def start(ins, outs, ssem, rsem):
        for cp in copies(outs, ssem, rsem, only_first=True)[0]:
            cp.start()

    def finish(ins, outs, ssem, rsem):
        first, relay, last = copies(outs, ssem, rsem)
        for arrived, onward in relay:
            arrived.wait_recv()
            onward.start()
        for cp in last:
            cp.wait_recv()
        for cp in first:
            cp.wait_send()
        for _, onward in relay:
            onward.wait_send()

    return Exchange(bufs, [_sds(b.shape, b.dtype) for b in bufs], {w: w for w in range(n)}, 6 * n, start, finish)


def _simple_exchange(arrays, landing, aliases, make_copies):
    def start(ins, outs, ssem, rsem):
        for cp, _ in make_copies(ins, outs, ssem, rsem, False):
            cp.start()

    def finish(ins, outs, ssem, rsem):
        cps = make_copies(ins, outs, ssem, rsem, True)
        for _, landed in cps:
            landed.wait_recv()
        for cp, _ in cps:
            cp.wait_send()

    return Exchange(arrays, landing, aliases, len(arrays) * 3, start, finish)


def _ex_pair_swap(grads):
    def make(ins, outs, ssem, rsem, landing):
        x, y, c, _ = _place()
        cps = [_remote(ins[w].at[:, 1 - c], outs[w], ssem.at[w], rsem.at[w], (x, y, 1 - c))
               for w in range(len(grads))]
        return [(cp, cp) for cp in cps]

    return _simple_exchange(grads, [_sds((NSH,) + g.shape[2:], g.dtype) for g in grads], {}, make)


def _ex_scatter(parts):
    def make(ins, outs, ssem, rsem, landing):
        x, y, c, chips = _place()
        out = []
        for w in range(len(parts)):
            for k, (px, py) in enumerate(chips):
                sems = (ssem.at[3 * w + k], rsem.at[3 * w + k])
                out.append((_remote(ins[w].at[2 * px + py], outs[w].at[k], *sems, (px, py, c)),
                            _remote(outs[w].at[k], outs[w].at[k], *sems, (px, py, c)) if landing else None))
        return out

    return _simple_exchange(parts, [_sds((3,) + p.shape[1:], p.dtype) for p in parts], {}, make)


def _ex_relay(bufs):
    def make(ins, outs, ssem, rsem, landing):
        x, y, c, chips = _place()
        sib = (x, y, 1 - c)
        out = []
        for w in range(len(bufs)):
            half = outs[w].shape[1] // 2
            for k, (px, py) in enumerate(chips):
                sems = (ssem.at[3 * w + k], rsem.at[3 * w + k])
                have = outs[w].at[2 * px + py, pl.ds(c * half, half)]
                miss = outs[w].at[2 * px + py, pl.ds((1 - c) * half, half)]
                out.append((_remote(have, have, *sems, sib), _remote(miss, miss, *sems, sib) if landing else None))
        return out

    return _simple_exchange(bufs, [_sds(b.shape, b.dtype) for b in bufs], {w: w for w in range(len(bufs))}, make)


def _ex_share(bufs):
    def make(ins, outs, ssem, rsem, landing):
        x, y, c, _ = _place()
        sib = (x, y, 1 - c)
        return [(_remote(outs[w].at[c], outs[w].at[c], ssem.at[w], rsem.at[w], sib),
                 _remote(outs[w].at[1 - c], outs[w].at[1 - c], ssem.at[w], rsem.at[w], sib) if landing else None)
                for w in range(len(bufs))]

    return _simple_exchange(bufs, [_sds(b.shape, b.dtype) for b in bufs], {w: w for w in range(len(bufs))}, make)


def _small_copies(slots, ssems, rsems, sending):
    x, y, c, _ = _place()
    out = []
    for m in range(1, 8):
        px, py, pc = x ^ (m >> 2), y ^ ((m >> 1) & 1), c ^ (m & 1)
        slot = slots.at[4 * x + 2 * y + c if sending else 4 * px + 2 * py + pc]
        out.append(_remote(slot, slot, ssems[m - 1], rsems[m - 1], (px, py, pc)))
    return out


def _small_gather_start(slots, name):
    def body(*refs):
        for cp in _small_copies(refs[0], refs[1:8], refs[8:15], True):
            cp.start()

    outs = pl.pallas_call(
        body, name=name,
        out_shape=[pltpu.SemaphoreType.DMA(())] * 14 + [pltpu.HBM(slots.shape, slots.dtype)],
        in_specs=[_HBM], out_specs=[_SEM] * 14 + [_HBM], input_output_aliases={0: 14},
        compiler_params=pltpu.CompilerParams(has_side_effects=_EFFECT),
    )(*_in_hbm([slots]))
    return outs[:14], outs[14]


def _small_gather_wait(sems, slots, after, name):
    def body(*refs):
        for cp in _small_copies(refs[0], refs[1:8], refs[8:15], True):
            cp.wait_send()
        for cp in _small_copies(refs[0], refs[1:8], refs[8:15], False):
            cp.wait_recv()

    return pl.pallas_call(
        body, name=name, out_shape=pltpu.HBM(slots.shape, slots.dtype),
        in_specs=[_HBM] + [_SEM] * 14 + [_ANY] * len(after), out_specs=_HBM, input_output_aliases={0: 0},
        compiler_params=pltpu.CompilerParams(has_side_effects=_EFFECT),
    )(slots, *sems, *after)


def _row_block(rows, cap=256):
    return max(t for t in range(16, cap + 1, 16) if rows % t == 0)


def _pair_sum(grad, got, c_idx, name):
    _, _, half, cols = grad.shape
    tr = _row_block(half, 512)

    def body(c_ref, a_ref, b_ref, o_ref):
        o_ref[...] = (a_ref[...].astype(F32) + b_ref[...].astype(F32)).astype(BF16)

    return pl.pallas_call(
        body, name=name, out_shape=_sds((NSH, half, cols), BF16),
        grid_spec=pltpu.PrefetchScalarGridSpec(
            num_scalar_prefetch=1, grid=(NSH, half // tr),
            in_specs=[pl.BlockSpec((None, None, tr, cols), lambda j, r, c: (j, c[0], r, 0)),
                      pl.BlockSpec((None, tr, cols), lambda j, r, c: (j, r, 0))],
            out_specs=pl.BlockSpec((None, tr, cols), lambda j, r, c: (j, r, 0))),
        compiler_params=_params(("arbitrary", "arbitrary"), 32),
    )(c_idx, *_in_hbm([grad, got]))


def _chip_sum(own, got, place, name):
    _, half, cols = own.shape
    tr = _row_block(half, 512)

    def body(place_ref, own_ref, got_ref, o_ref):
        acc = own_ref[...].astype(F32)
        for k in range(3):
            acc = acc + got_ref[k].astype(F32)
        o_ref[...] = acc

    return pl.pallas_call(
        body, name=name, out_shape=_sds((2, half, cols), F32),
        grid_spec=pltpu.PrefetchScalarGridSpec(
            num_scalar_prefetch=1, grid=(half // tr,),
            in_specs=[pl.BlockSpec((None, tr, cols), lambda r, p: (p[0], r, 0)),
                      pl.BlockSpec((3, tr, cols), lambda r, p: (0, r, 0))],
            out_specs=pl.BlockSpec((None, tr, cols), lambda r, p: (p[1], r, 0))),
        compiler_params=_params(("arbitrary",), 32),
    )(place, *_in_hbm([own, got]))


def _adamw_math(w, g, m, v):
    m = B1 * m + (1.0 - B1) * g
    v = B2 * v + (1.0 - B2) * (g * g)
    m_hat = m / (1.0 - B1 ** STEP)
    v_hat = v / (1.0 - B2 ** STEP)
    return -LR * (m_hat / (jnp.sqrt(v_hat) + AEPS) + WD * w), m, v


def _adamw(w, g, m, v, name, after=()):
    rows, cols = w.shape
    tr = _row_block(rows)

    def body(w_ref, g_ref, m_ref, v_ref, go_ref, d_ref, nm_ref, nv_ref):
        g = g_ref[...]
        go_ref[...] = g
        d_ref[...], nm_ref[...], nv_ref[...] = _adamw_math(w_ref[...], g, m_ref[...], v_ref[...])

    blk = pl.BlockSpec((tr, cols), lambda r: (r, 0))
    return _call(
        body, (w, g, m, v), name=name, grid=(rows // tr,), out_shape=[_sds(w.shape, F32)] * 4,
        in_specs=[blk] * 4, out_specs=[blk] * 4,
        compiler_params=_params(("arbitrary",), 32), free=(0, 2, 3), after=after)


def _small_update(gathered, w, m, v):
    rows = w.shape[0]

    def body(ga_ref, w_ref, m_ref, v_ref, g_ref, d_ref, nm_ref, nv_ref):
        g = ga_ref[0:rows, :]
        for dev in range(1, 8):
            g = g + ga_ref[dev * rows:(dev + 1) * rows, :]
        g_ref[...] = g
        d_ref[...], nm_ref[...], nv_ref[...] = _adamw_math(w_ref[...], g, m_ref[...], v_ref[...])

    return pl.pallas_call(
        body, name="small_update", out_shape=[jax.ShapeDtypeStruct(w.shape, F32)] * 4,
        in_specs=[_VM] * 4, out_specs=[_VM] * 4,
    )(gathered, w, m, v)


SMALL = ("ffn1_norm", "mix_norm", "ffn2_norm", "final_norm", "pool_scale", "pool_w_group", "loss")
BIG = ("ffn1_w_gate_up", "ffn1_w_down", "w_in", "w_branch_pool", "w_branch_attn", "w_out",
       "ffn2_w_gate_up", "ffn2_w_down")
ORDER = ("ffn1_norm", "ffn1_w_gate_up", "ffn1_w_down", "mix_norm", "w_in", "pool_w_group", "pool_scale",
         "w_branch_pool", "w_branch_attn", "w_out", "ffn2_norm", "ffn2_w_gate_up", "ffn2_w_down", "final_norm")
SMALL_ROWS = 560


def _pack_small(t):
    parts = []
    for k in SMALL:
        rows = t[k].reshape(-1, 128) if k in t else jnp.zeros((1, 128), F32)
        parts.append(jnp.pad(rows, ((0, -rows.shape[0] % 8), (0, 0))))
    packed = jnp.concatenate(parts, axis=0)
    assert packed.shape == (SMALL_ROWS, 128), packed.shape
    return packed


def _unpack_small(packed, like):
    out, at = {}, 0
    for k in SMALL:
        n = like[k].size // 128 if k in like else 1
        out[k] = packed[at:at + n].reshape(like[k].shape) if k in like else packed[at, 0]
        at += n + (-n % 8)
    return out


def _halves(g):
    return g.reshape(NSH, 2, g.shape[1] // 2, g.shape[2])


def kernel(x, ffn1_norm, ffn1_w_gate_up, ffn1_w_down, mix_norm, w_in, pool_w_group, pool_scale, w_branch_pool, w_branch_attn, w_out, ffn2_norm, ffn2_w_gate_up, ffn2_w_down, final_norm, loss_target, m_ffn1_norm, m_ffn1_w_gate_up, m_ffn1_w_down, m_mix_norm, m_w_in, m_pool_w_group, m_pool_scale, m_w_branch_pool, m_w_branch_attn, m_w_out, m_ffn2_norm, m_ffn2_w_gate_up, m_ffn2_w_down, m_final_norm, v_ffn1_norm, v_ffn1_w_gate_up, v_ffn1_w_down, v_mix_norm, v_w_in, v_pool_w_group, v_pool_scale, v_w_branch_pool, v_w_branch_attn, v_w_out, v_ffn2_norm, v_ffn2_w_gate_up, v_ffn2_w_down, v_final_norm):
    wts = dict(ffn1_norm=ffn1_norm, ffn1_w_gate_up=ffn1_w_gate_up, ffn1_w_down=ffn1_w_down, mix_norm=mix_norm,
               w_in=w_in, pool_w_group=pool_w_group, pool_scale=pool_scale, w_branch_pool=w_branch_pool,
               w_branch_attn=w_branch_attn, w_out=w_out, ffn2_norm=ffn2_norm, ffn2_w_gate_up=ffn2_w_gate_up,
               ffn2_w_down=ffn2_w_down, final_norm=final_norm)
    mom = dict(ffn1_norm=m_ffn1_norm, ffn1_w_gate_up=m_ffn1_w_gate_up, ffn1_w_down=m_ffn1_w_down,
               mix_norm=m_mix_norm, w_in=m_w_in, pool_w_group=m_pool_w_group, pool_scale=m_pool_scale,
               w_branch_pool=m_w_branch_pool, w_branch_attn=m_w_branch_attn, w_out=m_w_out,
               ffn2_norm=m_ffn2_norm, ffn2_w_gate_up=m_ffn2_w_gate_up, ffn2_w_down=m_ffn2_w_down,
               final_norm=m_final_norm)
    var = dict(ffn1_norm=v_ffn1_norm, ffn1_w_gate_up=v_ffn1_w_gate_up, ffn1_w_down=v_ffn1_w_down,
               mix_norm=v_mix_norm, w_in=v_w_in, pool_w_group=v_pool_w_group, pool_scale=v_pool_scale,
               w_branch_pool=v_w_branch_pool, w_branch_attn=v_w_branch_attn, w_out=v_w_out,
               ffn2_norm=v_ffn2_norm, ffn2_w_gate_up=v_ffn2_w_gate_up, ffn2_w_down=v_ffn2_w_down,
               final_norm=v_final_norm)

    c_idx = lax.axis_index("c").astype(jnp.int32).reshape(1)
    me_idx = (2 * lax.axis_index("x") + lax.axis_index("y")).astype(jnp.int32).reshape(1)
    place = jnp.concatenate([me_idx, c_idx])
    x0, tgt = x[0], loss_target[0]
    wgrp = pool_w_group[0].astype(BF16)
    g1, gm, g2, gf = ffn1_norm, mix_norm, ffn2_norm, final_norm.reshape(1, D)
    grad, delta, new_m, new_v = {}, {}, {}, {}

    def pair_sums(keys, parts, got):
        return [_pair_sum(parts[i], got[i], c_idx, "pair_sum_" + k) for i, k in enumerate(keys)]

    def chip_sums(keys, chip_parts, owned):
        return [_chip_sum(chip_parts[i], owned[i], place, "chip_sum_" + k) for i, k in enumerate(keys)]

    def adamw(k, after=()):
        outs = _adamw(wts[k][0], grad[k][0], mom[k][0], var[k][0], "adamw_" + k, after=after)
        grad[k], delta[k], new_m[k], new_v[k] = (o.reshape(wts[k].shape) for o in outs)

    own = {k: _cast_into_block(wts[k][0], me_idx, "cast_" + k) for k in BIG}
    first, late = ("ffn1_w_gate_up", "ffn1_w_down"), ("w_branch_pool", "w_branch_attn", "w_out",
                                                       "ffn2_w_gate_up", "ffn2_w_down")
    full = dict(zip(first, _exchange_alone(_ex_gather([own[k] for k in first]), "gather_ffn1")))
    wgu1, wd1 = full["ffn1_w_gate_up"], full["ffn1_w_down"].reshape(DFF, D)
    (h1, n1, gu1, a1), (win,) = _ffn_fwd(x0, g1, wgu1, wd1, "ffn1_fwd", exchange=_ex_gather_direct([own["w_in"]]))
    sems_l, thru_l, token_l = _gather_start([own[k_] for k_ in late], [h1], "gather_late_start")
    u, xp, q, k, v, gp, gs = _mix_in(h1, gm, win, after=(token_l,))
    o_sb, ctot = _attn_fwd(q, k, v)
    arrived = _gather_wait(sems_l, thru_l, [o_sb], "gather_late_wait")
    wbp, wba, wout = _exchange_alone(_ex_relay(arrived[:3]), "relay_mix")
    wout = wout.reshape(D, D)
    (h2, pm, p, yp, ys, mm), (wgu2, wd2) = _mix_out(h1, xp, o_sb, gp, gs, wgrp, pool_scale, wbp, wba, wout,
                                                    exchange=_ex_relay(arrived[3:]))
    wd2 = wd2.reshape(DFF, D)
    dh3, loss_row, d_gf, n3, gu3, a3 = _ffn_fwd(h2, g2, wgu2, wd2, "ffn2_fwd", head=(tgt, gf))

    def grad_gate_up(n, dgu, name, exchange=None):
        res = _wgrad(n, dgu, NSH, D, name, exchange=exchange)
        return [_halves(res)] if exchange is None else ([_halves(res[0])], res[1])

    def grad_down(a, dh, name, exchange=None):
        res = _wgrad(a, dh, 1, FFS, name, exchange=exchange)
        halves = lambda g: [_halves(g.reshape(NSH, DFF // NSH, D))]
        return halves(res) if exchange is None else (halves(res[0]), res[1])

    k_gu2, k_d2, k_gu1, k_d1, k_in = (("ffn2_w_gate_up",), ("ffn2_w_down",), ("ffn1_w_gate_up",),
                                      ("ffn1_w_down",), ("w_in",))
    dh2, dgu3, d_g2 = _ffn_bwd(dh3, h2, g2, gu3, wgu2, wd2, "ffn2_bwd")
    pa = grad_gate_up(n3, dgu3, "wgrad_gu2") + grad_down(a3, dh3, "wgrad_d2")
    (dlg, dyp, dys, do_sb, dyg, dxp, d_scale), got_a = _mix_bwd_out(
        dh2, gp, gs, yp, ys, pm, wgrp, pool_scale, wbp, wba, wout, exchange=_ex_pair_swap(pa))
    chip_a = pair_sums(k_gu2 + k_d2, pa, got_a)
    kb = ("w_out", "w_branch_pool", "w_branch_attn")
    pb = [_halves(_wgrad(mm, dh2, 1, D, "wgrad_out").reshape(NSH, D // NSH, D)),
          _halves(_wgrad(p, dyp, NSH, PW, "wgrad_bp")), _halves(_wgrad(o_sb, dys, NSH, SBW, "wgrad_ba"))]
    k_a, k_in = k_gu2 + k_d2, k_in + kb
    sems_a, thru_a, token_a = _scatter_start(chip_a, "scatter_a_start")
    dq, dk, dv = _attn_bwd(q, k, v, do_sb, ctot, after=(token_a,))
    chip_a, owned_a = _scatter_wait(sems_a, thru_a, [dq], "scatter_a_wait")
    halves_a = chip_sums(k_a, chip_a, owned_a)
    (dh1, d_gm, dproj), both_a = _mix_bwd_in(dh2, h1, gm, (dxp, dq, dk, dv, dlg), win, exchange=_ex_share(halves_a))
    for i, k_ in enumerate(k_a):
        grad[k_] = both_a[i].reshape(wts[k_].shape)

    p_in = [_halves(_wgrad(u, dproj, NSH, D, "wgrad_in"))] + pb
    p_d1, got_in = grad_down(a1, dh1, "wgrad_d1", exchange=_ex_pair_swap(p_in))
    sems_in, thru_in, token_in = _scatter_start(pair_sums(k_in, p_in, got_in), "scatter_in_start")
    dgu1, got_d1 = _ffn_bwd_act(dh1, gu1, wd1, "ffn1_bwd_act", exchange=_ex_pair_swap(p_d1), after=(token_in,))
    sems_d1, thru_d1, token_d1 = _scatter_start(pair_sums(k_d1, p_d1, got_d1), "scatter_d1_start")
    p_gu1 = [_halves(_wgrad(n1, dgu1, NSH, D, "wgrad_gu1", after=(token_in, token_d1)))]
    chip_in, owned_in = _scatter_wait(sems_in, thru_in, p_gu1, "scatter_in_wait")
    chip_d1, owned_d1 = _scatter_wait(sems_d1, thru_d1, p_gu1, "scatter_d1_wait")
    halves_in, halves_d1 = chip_sums(k_in, chip_in, owned_in), chip_sums(k_d1, chip_d1, owned_d1)
    landed = _exchange_alone(_join(_ex_pair_swap(p_gu1), _ex_share(halves_in)), "pair_swap_gu1")
    for i, k_ in enumerate(k_in):
        grad[k_] = landed[1 + i].reshape(wts[k_].shape)
    sems, thru, token = _scatter_start(pair_sums(k_gu1, p_gu1, landed[:1]), "scatter_gu1_start")
    for k_ in k_a + k_in:
        adamw(k_, after=(token,))
    dx, d_g1 = _ffn_bwd_in(dh1, x0, g1, dgu1, wgu1, "ffn1_bwd_in", after=(token,))
    small_g = dict(ffn1_norm=d_g1, mix_norm=d_gm, ffn2_norm=d_g2, final_norm=d_gf, pool_scale=d_scale,
                   pool_w_group=_wgrad_groups(pm, dyg), loss=loss_row)
    dev = 4 * lax.axis_index("x") + 2 * lax.axis_index("y") + lax.axis_index("c")
    slots = lax.dynamic_update_slice(jnp.zeros((8, SMALL_ROWS, 128), F32), _pack_small(small_g)[None], (dev, 0, 0))
    sems_s, slots = _small_gather_start(slots, "small_gather_start")

    chip_gu1, owned_gu1 = _scatter_wait(sems, thru, [dx] + [delta[k_] for k_ in k_a + k_in], "scatter_gu1_wait")
    both = _exchange_alone(_ex_share(halves_d1 + chip_sums(k_gu1, chip_gu1, owned_gu1)), "share_last")
    grad["ffn1_w_down"] = both[0].reshape(ffn1_w_down.shape)
    grad["ffn1_w_gate_up"] = both[1].reshape(ffn1_w_gate_up.shape)
    for k_ in k_d1 + k_gu1:
        adamw(k_)
    gathered = _small_gather_wait(sems_s, slots, [delta[k_] for k_ in k_d1 + k_gu1], "small_gather_wait")
    gathered = gathered.reshape(8 * SMALL_ROWS, 128)
    sg, sd, sm, sv = _small_update(gathered, _pack_small(wts), _pack_small(mom), _pack_small(var))
    sums = _unpack_small(sg, wts)
    loss = sums.pop("loss")
    grad.update(sums)
    for dst, packed in ((delta, sd), (new_m, sm), (new_v, sv)):
        vals = _unpack_small(packed, wts)
        vals.pop("loss")
        dst.update(vals)
    return (loss, dx[None], *[grad[k_] for k_ in ORDER], *[delta[k_] for k_ in ORDER],
            *[new_m[k_] for k_ in ORDER], *[new_v[k_] for k_ in ORDER])
```

```python
import functools

import jax
import jax.numpy as jnp
from jax import lax
from jax.experimental import pallas as pl
from jax.experimental.pallas import tpu as pltpu

F32 = jnp.float32
BF16 = jnp.bfloat16

S = 2048
D = 1024
DFF = 2816
FFS = 2 * DFF // 4
NSH = 4
PW = 512
PG = 128
POOL_WINDOWS = (2, 4, 8, 16)
HALO = 16
SBW = 512
DH = 64
EPS = 1e-6
SCALE = 0.125
LOG2E = 1.4426950408889634
TA = 256
QB = 2
MIB = 1024 * 1024

LR, B1, B2, AEPS, WD, STEP = 0.001, 0.9, 0.999, 1e-08, 0.01, 10

_VM = pl.BlockSpec(memory_space=pltpu.VMEM)
_ANY = pl.BlockSpec(memory_space=pl.ANY)
MESH = pl.DeviceIdType.MESH


def _nn(a, b):
    return jnp.dot(a, b, preferred_element_type=F32)


def _nt(a, b):
    return lax.dot_general(a, b, (((1,), (1,)), ((), ())), preferred_element_type=F32)


def _tn(a, b):
    return lax.dot_general(a, b, (((0,), (0,)), ((), ())), preferred_element_type=F32)


def _params(sem, vmem_mib):
    return pltpu.CompilerParams(dimension_semantics=sem, vmem_limit_bytes=vmem_mib * MIB)


def _rows(tm, width):
    return pl.BlockSpec((tm, width), lambda i: (i, 0))


def _fixed(shape):
    return pl.BlockSpec(shape, lambda *_: (0,) * len(shape))


def _sds(shape, dtype):
    return pltpu.HBM(shape, dtype)


def _in_hbm(args):
    return [pltpu.with_memory_space_constraint(a, pltpu.HBM) for a in args]


def _stage(pairs):
    @pl.when(pl.program_id(0) == 0)
    def _():
        for src, dst in pairs:
            pltpu.sync_copy(src, dst)


def _vmem_like(*arrays):
    return [pltpu.VMEM(a.shape, a.dtype) for a in arrays]


class Exchange:
    def __init__(self, arrays, landing, aliases, n_sems, start, finish):
        self.arrays, self.landing, self.aliases, self.n_sems = list(arrays), list(landing), dict(aliases), n_sems
        self.start, self.finish = start, finish


def _join(a, b):
    na, la = len(a.arrays), len(a.landing)

    def both(fa, fb):
        def run(ins, outs, ssem, rsem):
            fa(ins[:na], outs[:la], ssem.at[pl.ds(0, a.n_sems)], rsem.at[pl.ds(0, a.n_sems)])
            fb(ins[na:], outs[la:], ssem.at[pl.ds(a.n_sems, b.n_sems)], rsem.at[pl.ds(a.n_sems, b.n_sems)])
        return run

    aliases = {**a.aliases, **{na + i: la + j for i, j in b.aliases.items()}}
    return Exchange(a.arrays + b.arrays, a.landing + b.landing, aliases, a.n_sems + b.n_sems,
                    both(a.start, b.start), both(a.finish, b.finish))


def _call(body, args, *, name, grid, in_specs, out_specs, out_shape, scratch_shapes=(), compiler_params=None,
          exchange=None, free=(), after=()):
    args = [a if i in free else pltpu.with_memory_space_constraint(a, pltpu.HBM) for i, a in enumerate(args)]
    if exchange is None:
        n_in = len(in_specs)

        def plain(*refs):
            body(*refs[:n_in], *refs[n_in + len(after):])

        return pl.pallas_call(plain, name=name, grid=grid, in_specs=list(in_specs) + [_ANY] * len(after),
                              out_specs=out_specs, out_shape=out_shape, scratch_shapes=list(scratch_shapes),
                              compiler_params=compiler_params)(*args, *after)
    ex = exchange
    n_in, n_out, n_scr = len(in_specs), len(out_specs), len(scratch_shapes)
    na, nl = len(ex.arrays), len(ex.landing)

    def hosted(*refs):
        at = [0]

        def take(n):
            at[0] += n
            return refs[at[0] - n:at[0]]

        k_in, _, e_in, k_out, e_out, k_scr = take(n_in), take(len(after)), take(na), take(n_out), take(nl), take(n_scr)
        ssem, rsem = take(2)
        ids = [pl.program_id(a) for a in range(len(grid))]
        first = functools.reduce(jnp.logical_and, [i == 0 for i in ids])
        last = functools.reduce(jnp.logical_and, [i == g - 1 for i, g in zip(ids, grid)])

        @pl.when(first)
        def _():
            ex.start(e_in, e_out, ssem, rsem)

        body(*k_in, *k_out, *k_scr)

        @pl.when(last)
        def _():
            ex.finish(e_in, e_out, ssem, rsem)

    outs = pl.pallas_call(
        hosted, name=name, grid=grid,
        in_specs=list(in_specs) + [_ANY] * (len(after) + na), out_specs=list(out_specs) + [_ANY] * nl,
        out_shape=list(out_shape) + ex.landing,
        scratch_shapes=list(scratch_shapes) + [pltpu.SemaphoreType.DMA((ex.n_sems,))] * 2,
        input_output_aliases={n_in + len(after) + i: n_out + j for i, j in ex.aliases.items()},
        compiler_params=compiler_params,
    )(*args, *after, *_in_hbm(ex.arrays))
    return outs[:n_out], outs[n_out:]


def _exchange_alone(ex, name, after=()):
    na, nl = len(ex.arrays), len(ex.landing)

    def body(*refs):
        outs = refs[na + len(after):na + len(after) + nl]
        ex.start(refs[:na], outs, refs[-2], refs[-1])
        ex.finish(refs[:na], outs, refs[-2], refs[-1])

    return pl.pallas_call(
        body, name=name, in_specs=[_ANY] * (na + len(after)), out_specs=[_ANY] * nl,
        out_shape=ex.landing, scratch_shapes=[pltpu.SemaphoreType.DMA((ex.n_sems,))] * 2,
        input_output_aliases=ex.aliases,
    )(*_in_hbm(ex.arrays), *after)


_HBM = pl.BlockSpec(memory_space=pltpu.HBM)
_SEM = pl.BlockSpec(memory_space=pltpu.SEMAPHORE)
_EFFECT = pltpu.SideEffectType.DATAFLOW_SIDE_EFFECTING


def _scatter_copies(srcs, lands, ssems, rsems):
    x, y, c, chips = _place()
    return [_remote(srcs[w].at[2 * px + py], lands[w].at[k], ssems[3 * w + k], rsems[3 * w + k], (px, py, c))
            for w in range(len(srcs)) for k, (px, py) in enumerate(chips)]


def _scatter_start(parts, name):
    n, ncp = len(parts), 3 * len(parts)
    lands = [lax.empty((3,) + p.shape[1:], p.dtype) for p in parts]

    def body(*refs):
        srcs, land_refs = refs[:n], refs[n:2 * n]
        ssems, rsems = refs[2 * n:2 * n + ncp], refs[2 * n + ncp:2 * n + 2 * ncp]
        for cp in _scatter_copies(srcs, land_refs, ssems, rsems):
            cp.start()
        token = refs[-1]
        token[...] = jnp.zeros_like(token)

    outs = pl.pallas_call(
        body, name=name,
        out_shape=([pltpu.SemaphoreType.DMA(())] * (2 * ncp) + [pltpu.HBM(a.shape, a.dtype) for a in parts + lands]
                   + [jax.ShapeDtypeStruct((8, 128), F32)]),
        in_specs=[_HBM] * (2 * n), out_specs=[_SEM] * (2 * ncp) + [_HBM] * (2 * n) + [_VM],
        input_output_aliases={i: 2 * ncp + i for i in range(2 * n)},
        compiler_params=pltpu.CompilerParams(has_side_effects=_EFFECT),
    )(*_in_hbm(parts), *_in_hbm(lands))
    sems, thru, token = outs[:2 * ncp], outs[2 * ncp:2 * ncp + 2 * n], outs[-1]
    return sems, thru, token


def _scatter_wait(sems, thru, after, name):
    n = len(thru) // 2
    ncp = 3 * n

    def body(*refs):
        srcs, land_refs = refs[:n], refs[n:2 * n]
        ssems, rsems = refs[2 * n:2 * n + ncp], refs[2 * n + ncp:2 * n + 2 * ncp]
        for cp in _scatter_copies(srcs, land_refs, ssems, rsems):
            cp.wait_send()
            cp.wait_recv()

    outs = pl.pallas_call(
        body, name=name, out_shape=[pltpu.HBM(a.shape, a.dtype) for a in thru],
        in_specs=[_HBM] * (2 * n) + [_SEM] * (2 * ncp) + [_ANY] * len(after), out_specs=[_HBM] * (2 * n),
        input_output_aliases={i: i for i in range(2 * n)},
        compiler_params=pltpu.CompilerParams(has_side_effects=_EFFECT),
    )(*thru, *sems, *after)
    return outs[:n], outs[n:]


def _gather_copies(bufs, ssems, rsems, sending):
    x, y, c, chips = _place()
    out = []
    for w, ref in enumerate(bufs):
        half = ref.shape[1] // 2
        for k, (px, py) in enumerate(chips):
            rows = ref.at[2 * x + y if sending else 2 * px + py, pl.ds(c * half, half)]
            out.append(_remote(rows, rows, ssems[3 * w + k], rsems[3 * w + k], (px, py, c)))
    return out


def _gather_start(bufs, after, name):
    n, ncp = len(bufs), 3 * len(bufs)

    def body(*refs):
        ssems, rsems = refs[n + len(after):n + len(after) + ncp], refs[n + len(after) + ncp:n + len(after) + 2 * ncp]
        for cp in _gather_copies(refs[:n], ssems, rsems, True):
            cp.start()
        token = refs[-1]
        token[...] = jnp.zeros_like(token)

    outs = pl.pallas_call(
        body, name=name,
        out_shape=([pltpu.SemaphoreType.DMA(())] * (2 * ncp) + [pltpu.HBM(a.shape, a.dtype) for a in bufs]
                   + [jax.ShapeDtypeStruct((8, 128), F32)]),
        in_specs=[_HBM] * n + [_ANY] * len(after), out_specs=[_SEM] * (2 * ncp) + [_HBM] * n + [_VM],
        input_output_aliases={i: 2 * ncp + i for i in range(n)},
        compiler_params=pltpu.CompilerParams(has_side_effects=_EFFECT),
    )(*_in_hbm(bufs), *after)
    return outs[:2 * ncp], outs[2 * ncp:2 * ncp + n], outs[-1]


def _gather_wait(sems, thru, after, name):
    n = len(thru)
    ncp = 3 * n

    def body(*refs):
        ssems, rsems = refs[n:n + ncp], refs[n + ncp:n + 2 * ncp]
        for cp in _gather_copies(refs[:n], ssems, rsems, True):
            cp.wait_send()
        for cp in _gather_copies(refs[:n], ssems, rsems, False):
            cp.wait_recv()

    return pl.pallas_call(
        body, name=name, out_shape=[pltpu.HBM(a.shape, a.dtype) for a in thru],
        in_specs=[_HBM] * n + [_SEM] * (2 * ncp) + [_ANY] * len(after), out_specs=[_HBM] * n,
        input_output_aliases={i: i for i in range(n)},
        compiler_params=pltpu.CompilerParams(has_side_effects=_EFFECT),
    )(*thru, *sems, *after)


def _rms(x):
    r = lax.rsqrt(jnp.mean(x * x, axis=-1, keepdims=True) + EPS)
    return r, x * r


def _rms_bwd(dn, xr, r, gain):
    dng = dn * gain
    dx = r * (dng - xr * jnp.mean(dng * xr, axis=-1, keepdims=True))
    return dx, jnp.sum(dn * xr, axis=0, keepdims=True)


def _ffn_fwd(x, gain, wgu, wd, name, exchange=None, head=None):
    tm = 256

    def body(x_ref, g_ref, wgu_hbm, wd_hbm, *rest):
        if head is None:
            h_ref, n_ref, gu_ref, a_ref, wgu_ref, wd_ref = rest
        else:
            t_ref, gf_ref, h_ref, loss_ref, dgf_ref, n_ref, gu_ref, a_ref, wgu_ref, wd_ref = rest
        _stage([(wgu_hbm, wgu_ref), (wd_hbm, wd_ref)])
        x = x_ref[...]
        _, xr = _rms(x)
        n = (xr * g_ref[...]).astype(BF16)
        n_ref[...] = n
        acc = jnp.zeros((tm, D), F32)
        for j in range(2):
            g = _nn(n, wgu_ref[j])
            u = _nn(n, wgu_ref[2 + j])
            gu_ref[:, j * FFS:(j + 1) * FFS] = g.astype(BF16)
            gu_ref[:, (2 + j) * FFS:(3 + j) * FFS] = u.astype(BF16)
            half_act = (0.5 * (g * jax.nn.sigmoid(g) * u)).astype(BF16)
            a_ref[:, j * FFS:(j + 1) * FFS] = half_act
            acc = acc + _nn(half_act, wd_ref[j * FFS:(j + 1) * FFS, :])
        h = x + acc
        if head is None:
            h_ref[...] = h
            return
        gf = gf_ref[...]
        r, hr = _rms(h)
        err = hr * gf - t_ref[...]
        dh, dgain = _rms_bwd(err * (1.0 / D), hr, r, gf)
        h_ref[...] = dh

        @pl.when(pl.program_id(0) == 0)
        def _():
            dgf_ref[...] = jnp.zeros_like(dgf_ref)
            loss_ref[...] = jnp.zeros_like(loss_ref)

        dgf_ref[...] += dgain
        loss_ref[...] += jnp.full((1, 128), (0.5 / D) * jnp.sum(err * err), F32)

    saved_specs = [_rows(tm, D), _rows(tm, 4 * FFS), _rows(tm, DFF)]
    saved_shapes = [_sds((S, D), BF16), _sds((S, 4 * FFS), BF16), _sds((S, DFF), BF16)]
    if head is None:
        return _call(
            body, (x, gain, wgu, wd), name=name, grid=(S // tm,),
            in_specs=[_rows(tm, D), _fixed((1, D)), _ANY, _ANY],
            out_specs=[_rows(tm, D)] + saved_specs, out_shape=[_sds((S, D), F32)] + saved_shapes,
            scratch_shapes=_vmem_like(wgu, wd),
            compiler_params=_params(("arbitrary",), 56), exchange=exchange)
    return _call(
        body, (x, gain, wgu, wd, *head), name=name, grid=(S // tm,),
        in_specs=[_rows(tm, D), _fixed((1, D)), _ANY, _ANY, _rows(tm, D), _fixed((1, D))],
        out_specs=[_rows(tm, D), _fixed((1, 128)), _fixed((1, D))] + saved_specs,
        out_shape=[_sds((S, D), F32), _sds((1, 128), F32), _sds((1, D), F32)] + saved_shapes,
        scratch_shapes=_vmem_like(wgu, wd),
        compiler_params=_params(("arbitrary",), 56), exchange=exchange, free=(4, 5))


def _ffn_bwd(dh, x, gain, gu, wgu, wd, name):
    tm = 256

    def body(dh_ref, x_ref, g_ref, gu_ref, wgu_hbm, wd_hbm, dx_ref, dgu_ref, dg_ref, wgu_ref, wd_ref):
        _stage([(wgu_hbm, wgu_ref), (wd_hbm, wd_ref)])
        dh = dh_ref[...]
        dhb = dh.astype(BF16)
        dn = jnp.zeros((tm, D), F32)
        for j in range(2):
            g = gu_ref[:, j * FFS:(j + 1) * FFS].astype(F32)
            u = gu_ref[:, (2 + j) * FFS:(3 + j) * FFS].astype(F32)
            da = 0.5 * _nt(dhb, wd_ref[j * FFS:(j + 1) * FFS, :])
            sg = jax.nn.sigmoid(g)
            dgb = (da * u * (sg * (1.0 + g * (1.0 - sg)))).astype(BF16)
            dub = (da * (g * sg)).astype(BF16)
            dgu_ref[:, j * FFS:(j + 1) * FFS] = dgb
            dgu_ref[:, (2 + j) * FFS:(3 + j) * FFS] = dub
            dn = dn + _nt(dgb, wgu_ref[j]) + _nt(dub, wgu_ref[2 + j])
        r, xr = _rms(x_ref[...])
        dx, dgain = _rms_bwd(dn, xr, r, g_ref[...])
        dx_ref[...] = dh + dx

        @pl.when(pl.program_id(0) == 0)
        def _():
            dg_ref[...] = jnp.zeros_like(dg_ref)

        dg_ref[...] += dgain

    return _call(
        body, (dh, x, gain, gu, wgu, wd), name=name, grid=(S // tm,),
        in_specs=[_rows(tm, D), _rows(tm, D), _fixed((1, D)), _rows(tm, 4 * FFS), _ANY, _ANY],
        out_specs=[_rows(tm, D), _rows(tm, 4 * FFS), _fixed((1, D))],
        out_shape=[_sds((S, D), F32), _sds((S, 4 * FFS), BF16), _sds((1, D), F32)],
        scratch_shapes=_vmem_like(wgu, wd), compiler_params=_params(("arbitrary",), 56))


def _ffn_bwd_act(dh, gu, wd, name, exchange=None, after=()):
    tm = 512

    def body(dh_ref, gu_ref, wd_hbm, dgu_ref, wd_ref):
        _stage([(wd_hbm, wd_ref)])
        dhb = dh_ref[...].astype(BF16)
        for j in range(2):
            g = gu_ref[:, j * FFS:(j + 1) * FFS].astype(F32)
            u = gu_ref[:, (2 + j) * FFS:(3 + j) * FFS].astype(F32)
            da = 0.5 * _nt(dhb, wd_ref[j * FFS:(j + 1) * FFS, :])
            sg = jax.nn.sigmoid(g)
            dgu_ref[:, j * FFS:(j + 1) * FFS] = (da * u * (sg * (1.0 + g * (1.0 - sg)))).astype(BF16)
            dgu_ref[:, (2 + j) * FFS:(3 + j) * FFS] = (da * (g * sg)).astype(BF16)

    res = _call(
        body, (dh, gu, wd), name=name, grid=(S // tm,),
        in_specs=[_rows(tm, D), _rows(tm, 4 * FFS), _ANY], out_specs=[_rows(tm, 4 * FFS)],
        out_shape=[_sds((S, 4 * FFS), BF16)], scratch_shapes=_vmem_like(wd),
        compiler_params=_params(("arbitrary",), 56), exchange=exchange, after=after)
    return res[0] if exchange is None else (res[0][0], res[1])


def _ffn_bwd_in(dh, x, gain, dgu, wgu, name, exchange=None, after=()):
    tm = 512

    def body(dh_ref, x_ref, g_ref, dgu_ref, wgu_hbm, dx_ref, dg_ref, wgu_ref):
        _stage([(wgu_hbm, wgu_ref)])
        dn = jnp.zeros((tm, D), F32)
        for j in range(NSH):
            dn = dn + _nt(dgu_ref[:, j * FFS:(j + 1) * FFS], wgu_ref[j])
        r, xr = _rms(x_ref[...])
        dx, dgain = _rms_bwd(dn, xr, r, g_ref[...])
        dx_ref[...] = dh_ref[...] + dx

        @pl.when(pl.program_id(0) == 0)
        def _():
            dg_ref[...] = jnp.zeros_like(dg_ref)

        dg_ref[...] += dgain

    return _call(
        body, (dh, x, gain, dgu, wgu), name=name, grid=(S // tm,),
        in_specs=[_rows(tm, D), _rows(tm, D), _fixed((1, D)), _rows(tm, 4 * FFS), _ANY],
        out_specs=[_rows(tm, D), _fixed((1, D))],
        out_shape=[_sds((S, D), F32), _sds((1, D), F32)],
        scratch_shapes=_vmem_like(wgu),
        compiler_params=_params(("arbitrary",), 56), exchange=exchange, after=after)


def _mix_in(h, gain, w_in, after=()):
    tm = 512

    def body(h_ref, g_ref, w_hbm, u_ref, xp_ref, q_ref, k_ref, v_ref, gp_ref, gs_ref, w_ref):
        _stage([(w_hbm, w_ref)])
        _, hr = _rms(h_ref[...])
        u = (hr * g_ref[...]).astype(BF16)
        u_ref[...] = u
        p0 = _nn(u, w_ref[0])
        xp_ref[...] = p0[:, :PW]
        q_ref[...] = p0[:, PW:].astype(BF16)
        p1 = _nn(u, w_ref[1])
        k_ref[...] = p1[:, :SBW].astype(BF16)
        v_ref[...] = p1[:, SBW:].astype(BF16)
        gp_ref[...] = jax.nn.sigmoid(_nn(u, w_ref[2])).astype(BF16)
        gs_ref[...] = jax.nn.sigmoid(_nn(u, w_ref[3])).astype(BF16)

    return _call(
        body, (h, gain, w_in), name="mix_in", grid=(S // tm,),
        in_specs=[_rows(tm, D), _fixed((1, D)), _ANY],
        out_specs=[_rows(tm, D), _rows(tm, PW), _rows(tm, SBW), _rows(tm, SBW), _rows(tm, SBW),
                   _rows(tm, D), _rows(tm, D)],
        out_shape=[_sds((S, D), BF16), _sds((S, PW), F32), _sds((S, SBW), BF16), _sds((S, SBW), BF16),
                   _sds((S, SBW), BF16), _sds((S, D), BF16), _sds((S, D), BF16)],
        scratch_shapes=_vmem_like(w_in),
        compiler_params=_params(("arbitrary",), 48), free=(1,), after=after)


def _hilo_dot(x, tri):
    hi = x.astype(BF16)
    lo = (x - hi.astype(F32)).astype(BF16)
    return _nn(hi, tri) + _nn(lo, tri)


def _log_terms(qk):
    z2 = qk * (SCALE * LOG2E)
    lb = jnp.minimum(z2, 0.0) - jnp.log2(1.0 + jnp.exp2(-jnp.abs(z2)))
    return lb, lb - z2


def _head_masks():
    lane = lax.broadcasted_iota(jnp.int32, (1, 2 * DH), 1)
    return (lane < DH, lane >= DH)


def _attn_fwd(q, k, v, exchange=None):
    T = TA

    def body(q_ref, k_ref, v_ref, o_ref, c_ref):
        i2 = 2 * pl.program_id(1)
        row = lax.broadcasted_iota(jnp.int32, (T, T), 0)
        col = lax.broadcasted_iota(jnp.int32, (T, T), 1)
        after = (row > col).astype(BF16)
        causal = col < row
        masks = _head_masks()
        qms = {}
        for b in range(QB):
            q2 = q_ref[b * T:(b + 1) * T, :]
            for h, hm in enumerate(masks):
                qms[b, h] = jnp.where(hm, q2, jnp.zeros_like(q2))

        def blocks(keys, pairs, carries, os):
            ks, vms = [], []
            for j in keys:
                rows = pl.ds(pl.multiple_of(j * T, T), T)
                vj = v_ref[rows, :]
                ks.append(k_ref[rows, :])
                vms.append([jnp.where(hm, vj, jnp.zeros_like(vj)) for hm in masks])
            units = [(n, h) for n in range(len(pairs)) for h in range(2)]
            qks = {(n, h): _nt(qms[pairs[n][0], h], ks[pairs[n][1]]) for n, h in units}
            lbs, l1ms = {}, {}
            for u in units:
                lbs[u], l1m = _log_terms(qks[u])
                l1ms[u] = jnp.where(causal, l1m, 0.0) if pairs[u[0]][2] else l1m
            cins = {u: _hilo_dot(l1ms[u], after) for u in units}
            carries, os = dict(carries), list(os)
            for n, h in units:
                b, key, diag = pairs[n]
                a = jnp.exp2(lbs[n, h] + cins[n, h] + carries[b, h])
                if diag:
                    a = jnp.where(causal, a, 0.0)
                os[b] = os[b] + _nn(a.astype(BF16), vms[key][h])
                carries[b, h] = carries[b, h] + jnp.sum(l1ms[n, h], axis=1, keepdims=True)
            return carries, tuple(os)

        carries = {(b, h): jnp.zeros((T, 1), F32) for b in range(QB) for h in range(2)}
        os = tuple(jnp.zeros((T, 2 * DH), F32) for _ in range(QB))
        carries, os = blocks([i2 + 1, i2], [(1, 0, True), (0, 1, True), (1, 1, False)], carries, os)
        carries, os = lax.fori_loop(
            0, i2, lambda jj, c: blocks([i2 - 1 - jj], [(0, 0, False), (1, 0, False)], c[0], c[1]), (carries, os))
        for b in range(QB):
            o_ref[b * T:(b + 1) * T, :] = os[b].astype(BF16)
            c_ref[b * T:(b + 1) * T, :] = jnp.where(masks[0], carries[b, 0], carries[b, 1])

    blk = pl.BlockSpec((QB * T, 2 * DH), lambda p, i: (i, p))
    full = pl.BlockSpec((S, 2 * DH), lambda p, i: (0, p))
    return _call(
        body, (q, k, v), name="attn_fwd", grid=(SBW // (2 * DH), S // (QB * T)),
        in_specs=[blk, full, full], out_specs=[blk, blk],
        out_shape=[_sds((S, SBW), BF16), _sds((S, SBW), F32)],
        compiler_params=_params(("arbitrary", "arbitrary"), 40), exchange=exchange)


def _attn_bwd(q, k, v, do, ctot, after=()):
    T = TA
    nq = S // (QB * T)

    def body(q_ref, k_ref, v_ref, do_ref, c_ref, dq_ref, dk_ref, dv_ref, dk_acc, dv_acc):
        step = pl.program_id(1)
        i2 = 2 * step

        @pl.when(step == 0)
        def _():
            dk_acc[...] = jnp.zeros_like(dk_acc)
            dv_acc[...] = jnp.zeros_like(dv_acc)

        row = lax.broadcasted_iota(jnp.int32, (T, T), 0)
        col = lax.broadcasted_iota(jnp.int32, (T, T), 1)
        upto = (row <= col).astype(BF16)
        before = (row < col).astype(BF16)
        causal = col < row
        masks = _head_masks()
        qms, doms, ctots = {}, {}, {}
        for b in range(QB):
            q2, do2 = q_ref[b * T:(b + 1) * T, :], do_ref[b * T:(b + 1) * T, :]
            for h, hm in enumerate(masks):
                qms[b, h] = jnp.where(hm, q2, jnp.zeros_like(q2))
                doms[b, h] = jnp.where(hm, do2, jnp.zeros_like(do2))
                ctots[b, h] = c_ref[b * T:(b + 1) * T, h * DH:h * DH + 1]

        def blocks(keys, pairs, sums, dqs):
            rows = [pl.ds(pl.multiple_of(j * T, T), T) for j in keys]
            ks, vs = [k_ref[r, :] for r in rows], [v_ref[r, :] for r in rows]
            kms = [[jnp.where(hm, kj, jnp.zeros_like(kj)) for hm in masks] for kj in ks]
            units = [(n, h) for n in range(len(pairs)) for h in range(2)]
            qks = {(n, h): _nt(qms[pairs[n][0], h], ks[pairs[n][1]]) for n, h in units}
            das = {(n, h): _nt(doms[pairs[n][0], h], vs[pairs[n][1]]) for n, h in units}
            lbs, l1ms = {}, {}
            for u in units:
                lbs[u], l1m = _log_terms(qks[u])
                l1ms[u] = jnp.where(causal, l1m, 0.0) if pairs[u[0]][2] else l1m
            pins = {u: _hilo_dot(l1ms[u], upto) for u in units}
            sums = dict(sums)
            a_s, dls, cps = {}, {}, {}
            for n, h in units:
                b, _, diag = pairs[n]
                cl, cp = sums[b, h]
                a = jnp.exp2(lbs[n, h] + (ctots[b, h] - cl) - pins[n, h])
                if diag:
                    a = jnp.where(causal, a, 0.0)
                a_s[n, h] = a.astype(BF16)
                dls[n, h] = das[n, h] * a
                cps[n, h] = cp
                sums[b, h] = (cl + jnp.sum(l1ms[n, h], axis=1, keepdims=True),
                              cp + jnp.sum(dls[n, h], axis=1, keepdims=True))
            pexs = {u: _hilo_dot(dls[u], before) for u in units}
            dzbs = {}
            for u in units:
                dz = dls[u] - jnp.exp2(lbs[u]) * (dls[u] + pexs[u] + cps[u])
                if pairs[u[0]][2]:
                    dz = jnp.where(causal, dz, 0.0)
                dzbs[u] = dz.astype(BF16)
            dqs = list(dqs)
            for n, h in units:
                dqs[pairs[n][0]] = dqs[pairs[n][0]] + _nn(dzbs[n, h], kms[pairs[n][1]][h])
            for key, r in enumerate(rows):
                mine = [(n, h) for n, h in units if pairs[n][1] == key]
                dk_acc[r, :] += functools.reduce(jnp.add, [_tn(dzbs[u], qms[pairs[u[0]][0], u[1]]) for u in mine])
                dv_acc[r, :] += functools.reduce(jnp.add, [_tn(a_s[u], doms[pairs[u[0]][0], u[1]]) for u in mine])
            return sums, tuple(dqs)

        zero = jnp.zeros((T, 1), F32)
        sums = {(b, h): (zero, zero) for b in range(QB) for h in range(2)}
        dqs = tuple(jnp.zeros((T, 2 * DH), F32) for _ in range(QB))
        sums, dqs = lax.fori_loop(
            0, i2, lambda j, c: blocks([j], [(0, 0, False), (1, 0, False)], c[0], c[1]), (sums, dqs))
        _, dqs = blocks([i2, i2 + 1], [(0, 0, True), (1, 0, False), (1, 1, True)], sums, dqs)
        for b in range(QB):
            dq_ref[b * T:(b + 1) * T, :] = (dqs[b] * SCALE).astype(BF16)

        @pl.when(step == nq - 1)
        def _():
            dk_ref[...] = (dk_acc[...] * SCALE).astype(BF16)
            dv_ref[...] = dv_acc[...].astype(BF16)

    blk = pl.BlockSpec((QB * T, 2 * DH), lambda p, i: (i, p))
    full = pl.BlockSpec((S, 2 * DH), lambda p, i: (0, p))
    return _call(
        body, (q, k, v, do, ctot), name="attn_bwd", grid=(SBW // (2 * DH), nq),
        in_specs=[blk, full, full, blk, blk], out_specs=[blk, full, full],
        out_shape=[_sds((S, SBW), BF16), _sds((S, SBW), BF16), _sds((S, SBW), BF16)],
        scratch_shapes=[pltpu.VMEM((S, 2 * DH), F32), pltpu.VMEM((S, 2 * DH), F32)],
        compiler_params=_params(("arbitrary", "arbitrary"), 40), after=after)


def _pool_counts(first_row, tm):
    pos = first_row + lax.broadcasted_iota(jnp.int32, (tm, 1), 0)
    return [jnp.minimum(pos + 1, w).astype(F32) for w in POOL_WINDOWS]


def _mix_out(h, xp, o_sb, gp, gs, w_group, scale, w_bp, w_ba, w_out, exchange=None):
    tm = 512

    def body(h_ref, xp_ref, o_ref, gp_ref, gs_ref, wg_hbm, sc_ref, wbp_hbm, wba_hbm, wo_hbm,
             h2_ref, pm_ref, p_ref, yp_ref, ys_ref, m_ref, halo, wg_ref, wbp_ref, wba_ref, wo_ref):
        _stage([(wg_hbm, wg_ref), (wbp_hbm, wbp_ref), (wba_hbm, wba_ref), (wo_hbm, wo_ref)])
        i = pl.program_id(0)

        @pl.when(i == 0)
        def _():
            halo[...] = jnp.zeros_like(halo)

        xp = xp_ref[...]
        ext = jnp.concatenate([halo[...], xp], axis=0)
        halo[...] = xp[tm - HALO:, :]
        counts = _pool_counts(i * tm, tm)
        for gi in range(len(POOL_WINDOWS)):
            lanes = slice(gi * PG, (gi + 1) * PG)
            win = ext[:, lanes]
            for step in range(gi + 1):
                win = win + pltpu.roll(win, 1 << step, 0)
            pm = (win[HALO:, :] / counts[gi] - xp[:, lanes]).astype(BF16)
            pm_ref[:, lanes] = pm
            p_ref[:, lanes] = (_nn(pm, wg_ref[gi]) * sc_ref[:, lanes]).astype(BF16)
        pb = p_ref[...]
        ob = o_ref[...]
        for j in range(NSH):
            cols = slice(j * (D // NSH), (j + 1) * (D // NSH))
            yp = _nn(pb, wbp_ref[j])
            ys = _nn(ob, wba_ref[j])
            yp_ref[:, cols] = yp.astype(BF16)
            ys_ref[:, cols] = ys.astype(BF16)
            m_ref[:, cols] = (gp_ref[:, cols].astype(F32) * yp + gs_ref[:, cols].astype(F32) * ys).astype(BF16)
        h2_ref[...] = h_ref[...] + _nn(m_ref[...], wo_ref[...])

    return _call(
        body, (h, xp, o_sb, gp, gs, w_group, scale, w_bp, w_ba, w_out), name="mix_out", grid=(S // tm,),
        in_specs=[_rows(tm, D), _rows(tm, PW), _rows(tm, SBW), _rows(tm, D), _rows(tm, D),
                  _ANY, _fixed((1, PW)), _ANY, _ANY, _ANY],
        out_specs=[_rows(tm, D), _rows(tm, PW), _rows(tm, PW), _rows(tm, D), _rows(tm, D), _rows(tm, D)],
        out_shape=[_sds((S, D), F32), _sds((S, PW), BF16), _sds((S, PW), BF16), _sds((S, D), BF16),
                   _sds((S, D), BF16), _sds((S, D), BF16)],
        scratch_shapes=[pltpu.VMEM((HALO, PW), F32)] + _vmem_like(w_group, w_bp, w_ba, w_out),
        compiler_params=_params(("arbitrary",), 48), free=(5, 6), exchange=exchange)


def _mix_bwd_out(dh, gp, gs, yp, ys, pm, w_group, scale, w_bp, w_ba, w_out, exchange=None):
    tm = 512
    nt = S // tm

    def body(dh_ref, gp_ref, gs_ref, yp_ref, ys_ref, pm_ref, wg_hbm, sc_ref, wbp_hbm, wba_hbm, wo_hbm,
             dlg_ref, dyp_ref, dys_ref, do_ref, dyg_ref, dxp_ref, dsc_ref, halo, wg_ref, wbp_ref, wba_ref, wo_ref):
        _stage([(wg_hbm, wg_ref), (wbp_hbm, wbp_ref), (wba_hbm, wba_ref), (wo_hbm, wo_ref)])
        step = pl.program_id(0)

        @pl.when(step == 0)
        def _():
            halo[...] = jnp.zeros_like(halo)
            dsc_ref[...] = jnp.zeros_like(dsc_ref)

        dm = _nt(dh_ref[...].astype(BF16), wo_ref[...])
        gp = gp_ref[...].astype(F32)
        gs = gs_ref[...].astype(F32)
        yp = yp_ref[...].astype(F32)
        ys = ys_ref[...].astype(F32)
        dlg_ref[:, :D] = (dm * yp * gp * (1.0 - gp)).astype(BF16)
        dlg_ref[:, D:] = (dm * ys * gs * (1.0 - gs)).astype(BF16)
        dyp_ref[...] = (dm * gp).astype(BF16)
        dys_ref[...] = (dm * gs).astype(BF16)
        dp = jnp.zeros((tm, PW), F32)
        do = jnp.zeros((tm, SBW), F32)
        for j in range(NSH):
            cols = slice(j * (D // NSH), (j + 1) * (D // NSH))
            dp = dp + _nt(dyp_ref[:, cols], wbp_ref[j])
            do = do + _nt(dys_ref[:, cols], wba_ref[j])
        do_ref[...] = do.astype(BF16)
        counts = _pool_counts((nt - 1 - step) * tm, tm)
        dscale = []
        for gi in range(len(POOL_WINDOWS)):
            lanes = slice(gi * PG, (gi + 1) * PG)
            dpg = dp[:, lanes]
            dscale.append(jnp.sum(dpg * _nn(pm_ref[:, lanes], wg_ref[gi]), axis=0, keepdims=True))
            dyg = (dpg * sc_ref[:, lanes]).astype(BF16)
            dyg_ref[:, lanes] = dyg
            dpm = _nt(dyg, wg_ref[gi])
            per = dpm / counts[gi]
            win = jnp.concatenate([per, halo[:, lanes]], axis=0)
            halo[:, lanes] = per[:HALO, :]
            for s in range(gi + 1):
                win = win + pltpu.roll(win, tm + HALO - (1 << s), 0)
            dxp_ref[:, lanes] = (win[:tm, :] - dpm).astype(BF16)
        dsc_ref[...] += jnp.concatenate(dscale, axis=1)

    rev = lambda width: pl.BlockSpec((tm, width), lambda i: (nt - 1 - i, 0))
    return _call(
        body, (dh, gp, gs, yp, ys, pm, w_group, scale, w_bp, w_ba, w_out), name="mix_bwd_out", grid=(nt,),
        in_specs=[rev(D), rev(D), rev(D), rev(D), rev(D), rev(PW), _ANY, _fixed((1, PW)), _ANY, _ANY, _ANY],
        out_specs=[rev(2 * D), rev(D), rev(D), rev(SBW), rev(PW), rev(PW), _fixed((1, PW))],
        out_shape=[_sds((S, 2 * D), BF16), _sds((S, D), BF16), _sds((S, D), BF16), _sds((S, SBW), BF16),
                   _sds((S, PW), BF16), _sds((S, PW), BF16), _sds((1, PW), F32)],
        scratch_shapes=[pltpu.VMEM((HALO, PW), F32)] + _vmem_like(w_group, w_bp, w_ba, w_out),
        compiler_params=_params(("arbitrary",), 48), exchange=exchange)


def _mix_bwd_in(dh, h, gain, pieces, w_in, exchange=None):
    tm = 512
    widths = [p.shape[1] for p in pieces]

    def body(dh_ref, h_ref, g_ref, *rest):
        piece_refs, (w_hbm, dx_ref, dg_ref, dp_ref, w_ref) = rest[:len(pieces)], rest[len(pieces):]
        _stage([(w_hbm, w_ref)])
        at = 0
        for ref, width in zip(piece_refs, widths):
            dp_ref[:, at:at + width] = ref[...]
            at += width
        du = jnp.zeros((tm, D), F32)
        for j in range(NSH):
            du = du + _nt(dp_ref[:, j * D:(j + 1) * D], w_ref[j])
        r, hr = _rms(h_ref[...])
        dx, dgain = _rms_bwd(du, hr, r, g_ref[...])
        dx_ref[...] = dh_ref[...] + dx

        @pl.when(pl.program_id(0) == 0)
        def _():
            dg_ref[...] = jnp.zeros_like(dg_ref)

        dg_ref[...] += dgain

    return _call(
        body, (dh, h, gain, *pieces, w_in), name="mix_bwd_in", grid=(S // tm,),
        in_specs=[_rows(tm, D), _rows(tm, D), _fixed((1, D))] + [_rows(tm, w) for w in widths] + [_ANY],
        out_specs=[_rows(tm, D), _fixed((1, D)), _rows(tm, 4 * D)],
        out_shape=[_sds((S, D), F32), _sds((1, D), F32), _sds((S, 4 * D), BF16)],
        scratch_shapes=_vmem_like(w_in),
        compiler_params=_params(("arbitrary",), 48), exchange=exchange)


def _wgrad(a, b, nblk, ti, name, out_dtype=BF16, exchange=None, after=()):
    ka, n = a.shape[1], b.shape[1]
    ns = n // nblk

    def body(a_ref, b_ref, o_ref):
        o_ref[...] = _tn(a_ref[...].astype(BF16), b_ref[...].astype(BF16)).astype(out_dtype)

    res = _call(
        body, (a, b), name=name, grid=(nblk, ka // ti),
        in_specs=[pl.BlockSpec((S, ti), lambda j, i: (0, i)), pl.BlockSpec((S, ns), lambda j, i: (0, j))],
        out_specs=[pl.BlockSpec((None, ti, ns), lambda j, i: (j, i, 0))],
        out_shape=[_sds((nblk, ka, ns), out_dtype)],
        compiler_params=_params(("arbitrary", "arbitrary"), 56), exchange=exchange, after=after)
    return res[0] if exchange is None else (res[0][0], res[1])


def _wgrad_groups(pm, dyg):
    def body(a_ref, b_ref, o_ref):
        o_ref[...] = _tn(a_ref[...], b_ref[...])

    col = pl.BlockSpec((S, PG), lambda g: (0, g))
    return pl.pallas_call(
        body, name="wgrad_groups", grid=(PW // PG,),
        in_specs=[col, col], out_specs=pl.BlockSpec((None, PG, PG), lambda g: (g, 0, 0)),
        out_shape=_sds((PW // PG, PG, PG), F32),
        compiler_params=_params(("arbitrary",), 32),
    )(*_in_hbm([pm, dyg]))


def _place():
    x, y, c = lax.axis_index("x"), lax.axis_index("y"), lax.axis_index("c")
    chips = [(1 - x, y), (x, 1 - y), (1 - x, 1 - y)]
    return x, y, c, chips


def _remote(src, dst, ssem, rsem, dev):
    return pltpu.make_async_remote_copy(src_ref=src, dst_ref=dst, send_sem=ssem, recv_sem=rsem,
                                        device_id=dev, device_id_type=MESH)


def _cast_into_block(w, me_idx, name):
    rows, cols = w.shape
    tr = _row_block(rows)

    def body(me_ref, w_ref, o_ref):
        o_ref[...] = w_ref[...].astype(BF16)

    return pl.pallas_call(
        body, name=name, out_shape=_sds((NSH, rows, cols), BF16),
        grid_spec=pltpu.PrefetchScalarGridSpec(
            num_scalar_prefetch=1, grid=(rows // tr,),
            in_specs=[pl.BlockSpec((tr, cols), lambda r, me: (r, 0))],
            out_specs=pl.BlockSpec((None, tr, cols), lambda r, me: (me[0], r, 0))),
        compiler_params=_params(("arbitrary",), 32),
    )(me_idx, w)


def _ex_gather(bufs):
    n = len(bufs)
    per = 8

    def plan(outs, ssem, rsem, w):
        x, y, c, _ = _place()
        sib, nbr_x, nbr_y = (x, y, 1 - c), (1 - x, y, c), (x, 1 - y, c)
        half = outs[w].shape[1] // 2
        quarter = half // 2
        sem = lambda k: (ssem.at[per * w + k], rsem.at[per * w + k])
        rows = lambda blk, start, size: outs[w].at[blk, pl.ds(start, size)]
        mine = rows(2 * x + y, c * half, half)
        from_x = rows(2 * (1 - x) + y, c * half, half)
        from_y = rows(2 * x + (1 - y), c * half, half)
        diag = 2 * (1 - x) + (1 - y)
        pass_y = rows(2 * (1 - x) + y, c * half, quarter)
        pass_x = rows(2 * x + (1 - y), c * half + quarter, quarter)
        diag_0, diag_1 = rows(diag, c * half, quarter), rows(diag, c * half + quarter, quarter)
        first = [_remote(mine, mine, *sem(0), nbr_x), _remote(mine, mine, *sem(1), nbr_y)]
        arrivals = [
            (_remote(from_x, from_x, *sem(0), nbr_x),
             [_remote(pass_y, pass_y, *sem(2), nbr_y), _remote(from_x, from_x, *sem(4), sib)]),
            (_remote(from_y, from_y, *sem(1), nbr_y),
             [_remote(pass_x, pass_x, *sem(3), nbr_x), _remote(from_y, from_y, *sem(5), sib)]),
            (_remote(diag_0, diag_0, *sem(2), nbr_y), [_remote(diag_0, diag_0, *sem(6), sib)]),
            (_remote(diag_1, diag_1, *sem(3), nbr_x), [_remote(diag_1, diag_1, *sem(7), sib)]),
        ]
        other = (1 - c) * half
        from_sibling = [
            _remote(rows(2 * (1 - x) + y, other, half), rows(2 * (1 - x) + y, other, half), *sem(4), sib),
            _remote(rows(2 * x + (1 - y), other, half), rows(2 * x + (1 - y), other, half), *sem(5), sib),
            _remote(rows(diag, other, quarter), rows(diag, other, quarter), *sem(6), sib),
            _remote(rows(diag, other + quarter, quarter), rows(diag, other + quarter, quarter), *sem(7), sib),
        ]
        return first, arrivals, from_sibling

    def start(ins, outs, ssem, rsem):
        x, y, c, _ = _place()
        for w in range(n):
            half = outs[w].shape[1] // 2
            mine = outs[w].at[2 * x + y, pl.ds(c * half, half)]
            _remote(mine, mine, ssem.at[per * w], rsem.at[per * w], (1 - x, y, c)).start()
            _remote(mine, mine, ssem.at[per * w + 1], rsem.at[per * w + 1], (x, 1 - y, c)).start()

    def finish(ins, outs, ssem, rsem):
        plans = [plan(outs, ssem, rsem, w) for w in range(n)]
        started = []
        for direct in (True, False):
            for first, arrivals, _ in plans:
                for arrived, onward in (arrivals[:2] if direct else arrivals[2:]):
                    arrived.wait_recv()
                    for cp in onward:
                        cp.start()
                    started += onward
        for first, _, from_sibling in plans:
            for cp in from_sibling:
                cp.wait_recv()
            started += first
        for cp in started:
            cp.wait_send()

    return Exchange(bufs, [_sds(b.shape, b.dtype) for b in bufs], {w: w for w in range(n)}, per * n, start, finish)


def _ex_gather_direct(bufs):
    n = len(bufs)

    def copies(outs, ssem, rsem, only_first=False):
        x, y, c, chips = _place()
        me, sib = 2 * x + y, (x, y, 1 - c)
        first, relay, last = [], [], []
        for w in range(n):
            half = outs[w].shape[1] // 2
            mine = outs[w].at[me, pl.ds(c * half, half)]
            for k, (px, py) in enumerate(chips):
                sems = (ssem.at[6 * w + k], rsem.at[6 * w + k])
                sib_sems = (ssem.at[6 * w + 3 + k], rsem.at[6 * w + 3 + k])
                first.append(_remote(mine, mine, *sems, (px, py, c)))
                if only_first:
                    continue
                got = outs[w].at[2 * px + py, pl.ds(c * half, half)]
                relay.append((_remote(got, got, *sems, (px, py, c)), _remote(got, got, *sib_sems, sib)))
                theirs = outs[w].at[2 * px + py, pl.ds((1 - c) * half, half)]
                last.append(_remote(theirs, theirs, *sib_sems, sib))
        return first, relay, last

    def start(ins, outs, ssem, rsem):
        for cp in copies(outs, ssem, rsem, only_first=True)[0]:
            cp.start()

    def finish(ins, outs, ssem, rsem):
        first, relay, last = copies(outs, ssem, rsem)
        for arrived, onward in relay:
            arrived.wait_recv()
            onward.start()
        for cp in last:
            cp.wait_recv()
        for cp in first:
            cp.wait_send()
        for _, onward in relay:
            onward.wait_send()

    return Exchange(bufs, [_sds(b.shape, b.dtype) for b in bufs], {w: w for w in range(n)}, 6 * n, start, finish)


def _simple_exchange(arrays, landing, aliases, make_copies):
    def start(ins, outs, ssem, rsem):
        for cp, _ in make_copies(ins, outs, ssem, rsem, False):
            cp.start()

    def finish(ins, outs, ssem, rsem):
        cps = make_copies(ins, outs, ssem, rsem, True)
        for _, landed in cps:
            landed.wait_recv()
        for cp, _ in cps:
            cp.wait_send()

    return Exchange(arrays, landing, aliases, len(arrays) * 3, start, finish)


def _ex_pair_swap(grads):
    def make(ins, outs, ssem, rsem, landing):
        x, y, c, _ = _place()
        cps = [_remote(ins[w].at[:, 1 - c], outs[w], ssem.at[w], rsem.at[w], (x, y, 1 - c))
               for w in range(len(grads))]
        return [(cp, cp) for cp in cps]

    return _simple_exchange(grads, [_sds((NSH,) + g.shape[2:], g.dtype) for g in grads], {}, make)


def _ex_scatter(parts):
    def make(ins, outs, ssem, rsem, landing):
        x, y, c, chips = _place()
        out = []
        for w in range(len(parts)):
            for k, (px, py) in enumerate(chips):
                sems = (ssem.at[3 * w + k], rsem.at[3 * w + k])
                out.append((_remote(ins[w].at[2 * px + py], outs[w].at[k], *sems, (px, py, c)),
                            _remote(outs[w].at[k], outs[w].at[k], *sems, (px, py, c)) if landing else None))
        return out

    return _simple_exchange(parts, [_sds((3,) + p.shape[1:], p.dtype) for p in parts], {}, make)


def _ex_relay(bufs):
    def make(ins, outs, ssem, rsem, landing):
        x, y, c, chips = _place()
        sib = (x, y, 1 - c)
        out = []
        for w in range(len(bufs)):
            half = outs[w].shape[1] // 2
            for k, (px, py) in enumerate(chips):
                sems = (ssem.at[3 * w + k], rsem.at[3 * w + k])
                have = outs[w].at[2 * px + py, pl.ds(c * half, half)]
                miss = outs[w].at[2 * px + py, pl.ds((1 - c) * half, half)]
                out.append((_remote(have, have, *sems, sib), _remote(miss, miss, *sems, sib) if landing else None))
        return out

    return _simple_exchange(bufs, [_sds(b.shape, b.dtype) for b in bufs], {w: w for w in range(len(bufs))}, make)


def _ex_share(bufs):
    def make(ins, outs, ssem, rsem, landing):
        x, y, c, _ = _place()
        sib = (x, y, 1 - c)
        return [(_remote(outs[w].at[c], outs[w].at[c], ssem.at[w], rsem.at[w], sib),
                 _remote(outs[w].at[1 - c], outs[w].at[1 - c], ssem.at[w], rsem.at[w], sib) if landing else None)
                for w in range(len(bufs))]

    return _simple_exchange(bufs, [_sds(b.shape, b.dtype) for b in bufs], {w: w for w in range(len(bufs))}, make)


def _small_copies(slots, ssems, rsems, sending):
    x, y, c, _ = _place()
    out = []
    for m in range(1, 8):
        px, py, pc = x ^ (m >> 2), y ^ ((m >> 1) & 1), c ^ (m & 1)
        slot = slots.at[4 * x + 2 * y + c if sending else 4 * px + 2 * py + pc]
        out.append(_remote(slot, slot, ssems[m - 1], rsems[m - 1], (px, py, pc)))
    return out


def _small_gather_start(slots, name):
    def body(*refs):
        for cp in _small_copies(refs[0], refs[1:8], refs[8:15], True):
            cp.start()
        refs[-1][...] = jnp.zeros_like(refs[-1])

    outs = pl.pallas_call(
        body, name=name,
        out_shape=([pltpu.SemaphoreType.DMA(())] * 14 + [pltpu.HBM(slots.shape, slots.dtype)]
                   + [jax.ShapeDtypeStruct((8, 128), F32)]),
        in_specs=[_HBM], out_specs=[_SEM] * 14 + [_HBM, _VM], input_output_aliases={0: 14},
        compiler_params=pltpu.CompilerParams(has_side_effects=_EFFECT),
    )(*_in_hbm([slots]))
    return outs[:14], outs[14], outs[15]


def _small_gather_wait(sems, slots, after, name):
    def body(*refs):
        for cp in _small_copies(refs[0], refs[1:8], refs[8:15], True):
            cp.wait_send()
        for cp in _small_copies(refs[0], refs[1:8], refs[8:15], False):
            cp.wait_recv()

    return pl.pallas_call(
        body, name=name, out_shape=pltpu.HBM(slots.shape, slots.dtype),
        in_specs=[_HBM] + [_SEM] * 14 + [_ANY] * len(after), out_specs=_HBM, input_output_aliases={0: 0},
        compiler_params=pltpu.CompilerParams(has_side_effects=_EFFECT),
    )(slots, *sems, *after)


def _row_block(rows, cap=256):
    return max(t for t in range(16, cap + 1, 16) if rows % t == 0)


def _pair_sum(grad, got, c_idx, name):
    _, _, half, cols = grad.shape
    tr = _row_block(half, 512)

    def body(c_ref, a_ref, b_ref, o_ref):
        o_ref[...] = (a_ref[...].astype(F32) + b_ref[...].astype(F32)).astype(BF16)

    return pl.pallas_call(
        body, name=name, out_shape=_sds((NSH, half, cols), BF16),
        grid_spec=pltpu.PrefetchScalarGridSpec(
            num_scalar_prefetch=1, grid=(NSH, half // tr),
            in_specs=[pl.BlockSpec((None, None, tr, cols), lambda j, r, c: (j, c[0], r, 0)),
                      pl.BlockSpec((None, tr, cols), lambda j, r, c: (j, r, 0))],
            out_specs=pl.BlockSpec((None, tr, cols), lambda j, r, c: (j, r, 0))),
        compiler_params=_params(("arbitrary", "arbitrary"), 32),
    )(c_idx, *_in_hbm([grad, got]))


def _chip_sum(own, got, place, name):
    _, half, cols = own.shape
    tr = _row_block(half, 512)

    def body(place_ref, own_ref, got_ref, o_ref):
        acc = own_ref[...].astype(F32)
        for k in range(3):
            acc = acc + got_ref[k].astype(F32)
        o_ref[...] = acc

    return pl.pallas_call(
        body, name=name, out_shape=_sds((2, half, cols), F32),
        grid_spec=pltpu.PrefetchScalarGridSpec(
            num_scalar_prefetch=1, grid=(half // tr,),
            in_specs=[pl.BlockSpec((None, tr, cols), lambda r, p: (p[0], r, 0)),
                      pl.BlockSpec((3, tr, cols), lambda r, p: (0, r, 0))],
            out_specs=pl.BlockSpec((None, tr, cols), lambda r, p: (p[1], r, 0))),
        compiler_params=_params(("arbitrary",), 32),
    )(place, *_in_hbm([own, got]))


def _adamw_math(w, g, m, v):
    m = B1 * m + (1.0 - B1) * g
    v = B2 * v + (1.0 - B2) * (g * g)
    m_hat = m / (1.0 - B1 ** STEP)
    v_hat = v / (1.0 - B2 ** STEP)
    return -LR * (m_hat / (jnp.sqrt(v_hat) + AEPS) + WD * w), m, v


def _adamw(w, g, m, v, name, after=()):
    rows, cols = w.shape
    tr = _row_block(rows)

    def body(w_ref, g_ref, m_ref, v_ref, go_ref, d_ref, nm_ref, nv_ref):
        g = g_ref[...]
        go_ref[...] = g
        d_ref[...], nm_ref[...], nv_ref[...] = _adamw_math(w_ref[...], g, m_ref[...], v_ref[...])

    blk = pl.BlockSpec((tr, cols), lambda r: (r, 0))
    return _call(
        body, (w, g, m, v), name=name, grid=(rows // tr,), out_shape=[_sds(w.shape, F32)] * 4,
        in_specs=[blk] * 4, out_specs=[blk] * 4,
        compiler_params=_params(("arbitrary",), 32), free=(0, 2, 3), after=after)


def _small_update(gathered, w, m, v):
    rows = w.shape[0]

    def body(ga_ref, w_ref, m_ref, v_ref, g_ref, d_ref, nm_ref, nv_ref):
        g = ga_ref[0:rows, :]
        for dev in range(1, 8):
            g = g + ga_ref[dev * rows:(dev + 1) * rows, :]
        g_ref[...] = g
        d_ref[...], nm_ref[...], nv_ref[...] = _adamw_math(w_ref[...], g, m_ref[...], v_ref[...])

    return pl.pallas_call(
        body, name="small_update", out_shape=[jax.ShapeDtypeStruct(w.shape, F32)] * 4,
        in_specs=[_VM] * 4, out_specs=[_VM] * 4,
    )(gathered, w, m, v)


SMALL = ("ffn1_norm", "mix_norm", "ffn2_norm", "final_norm", "pool_scale", "pool_w_group", "loss")
BIG = ("ffn1_w_gate_up", "ffn1_w_down", "w_in", "w_branch_pool", "w_branch_attn", "w_out",
       "ffn2_w_gate_up", "ffn2_w_down")
ORDER = ("ffn1_norm", "ffn1_w_gate_up", "ffn1_w_down", "mix_norm", "w_in", "pool_w_group", "pool_scale",
         "w_branch_pool", "w_branch_attn", "w_out", "ffn2_norm", "ffn2_w_gate_up", "ffn2_w_down", "final_norm")
SMALL_ROWS = 560


def _pack_small(t):
    parts = []
    for k in SMALL:
        rows = t[k].reshape(-1, 128) if k in t else jnp.zeros((1, 128), F32)
        parts.append(jnp.pad(rows, ((0, -rows.shape[0] % 8), (0, 0))))
    packed = jnp.concatenate(parts, axis=0)
    assert packed.shape == (SMALL_ROWS, 128), packed.shape
    return packed


def _unpack_small(packed, like):
    out, at = {}, 0
    for k in SMALL:
        n = like[k].size // 128 if k in like else 1
        out[k] = packed[at:at + n].reshape(like[k].shape) if k in like else packed[at, 0]
        at += n + (-n % 8)
    return out


def _halves(g):
    return g.reshape(NSH, 2, g.shape[1] // 2, g.shape[2])


def kernel(x, ffn1_norm, ffn1_w_gate_up, ffn1_w_down, mix_norm, w_in, pool_w_group, pool_scale, w_branch_pool, w_branch_attn, w_out, ffn2_norm, ffn2_w_gate_up, ffn2_w_down, final_norm, loss_target, m_ffn1_norm, m_ffn1_w_gate_up, m_ffn1_w_down, m_mix_norm, m_w_in, m_pool_w_group, m_pool_scale, m_w_branch_pool, m_w_branch_attn, m_w_out, m_ffn2_norm, m_ffn2_w_gate_up, m_ffn2_w_down, m_final_norm, v_ffn1_norm, v_ffn1_w_gate_up, v_ffn1_w_down, v_mix_norm, v_w_in, v_pool_w_group, v_pool_scale, v_w_branch_pool, v_w_branch_attn, v_w_out, v_ffn2_norm, v_ffn2_w_gate_up, v_ffn2_w_down, v_final_norm):
    wts = dict(ffn1_norm=ffn1_norm, ffn1_w_gate_up=ffn1_w_gate_up, ffn1_w_down=ffn1_w_down, mix_norm=mix_norm,
               w_in=w_in, pool_w_group=pool_w_group, pool_scale=pool_scale, w_branch_pool=w_branch_pool,
               w_branch_attn=w_branch_attn, w_out=w_out, ffn2_norm=ffn2_norm, ffn2_w_gate_up=ffn2_w_gate_up,
               ffn2_w_down=ffn2_w_down, final_norm=final_norm)
    mom = dict(ffn1_norm=m_ffn1_norm, ffn1_w_gate_up=m_ffn1_w_gate_up, ffn1_w_down=m_ffn1_w_down,
               mix_norm=m_mix_norm, w_in=m_w_in, pool_w_group=m_pool_w_group, pool_scale=m_pool_scale,
               w_branch_pool=m_w_branch_pool, w_branch_attn=m_w_branch_attn, w_out=m_w_out,
               ffn2_norm=m_ffn2_norm, ffn2_w_gate_up=m_ffn2_w_gate_up, ffn2_w_down=m_ffn2_w_down,
               final_norm=m_final_norm)
    var = dict(ffn1_norm=v_ffn1_norm, ffn1_w_gate_up=v_ffn1_w_gate_up, ffn1_w_down=v_ffn1_w_down,
               mix_norm=v_mix_norm, w_in=v_w_in, pool_w_group=v_pool_w_group, pool_scale=v_pool_scale,
               w_branch_pool=v_w_branch_pool, w_branch_attn=v_w_branch_attn, w_out=v_w_out,
               ffn2_norm=v_ffn2_norm, ffn2_w_gate_up=v_ffn2_w_gate_up, ffn2_w_down=v_ffn2_w_down,
               final_norm=v_final_norm)

    c_idx = lax.axis_index("c").astype(jnp.int32).reshape(1)
    me_idx = (2 * lax.axis_index("x") + lax.axis_index("y")).astype(jnp.int32).reshape(1)
    place = jnp.concatenate([me_idx, c_idx])
    x0, tgt = x[0], loss_target[0]
    wgrp = pool_w_group[0].astype(BF16)
    g1, gm, g2, gf = ffn1_norm, mix_norm, ffn2_norm, final_norm.reshape(1, D)
    grad, delta, new_m, new_v = {}, {}, {}, {}

    def pair_sums(keys, parts, got):
        return [_pair_sum(parts[i], got[i], c_idx, "pair_sum_" + k) for i, k in enumerate(keys)]

    def chip_sums(keys, chip_parts, owned):
        return [_chip_sum(chip_parts[i], owned[i], place, "chip_sum_" + k) for i, k in enumerate(keys)]

    def adamw(k, after=()):
        outs = _adamw(wts[k][0], grad[k][0], mom[k][0], var[k][0], "adamw_" + k, after=after)
        grad[k], delta[k], new_m[k], new_v[k] = (o.reshape(wts[k].shape) for o in outs)

    own = {k: _cast_into_block(wts[k][0], me_idx, "cast_" + k) for k in BIG}
    first, late = ("ffn1_w_gate_up", "ffn1_w_down"), ("w_branch_pool", "w_branch_attn", "w_out",
                                                       "ffn2_w_gate_up", "ffn2_w_down")
    full = dict(zip(first, _exchange_alone(_ex_gather([own[k] for k in first]), "gather_ffn1")))
    wgu1, wd1 = full["ffn1_w_gate_up"], full["ffn1_w_down"].reshape(DFF, D)
    (h1, n1, gu1, a1), (win,) = _ffn_fwd(x0, g1, wgu1, wd1, "ffn1_fwd", exchange=_ex_gather_direct([own["w_in"]]))
    sems_l, thru_l, token_l = _gather_start([own[k_] for k_ in late], [h1], "gather_late_start")
    u, xp, q, k, v, gp, gs = _mix_in(h1, gm, win, after=(token_l,))
    o_sb, ctot = _attn_fwd(q, k, v)
    arrived = _gather_wait(sems_l, thru_l, [o_sb], "gather_late_wait")
    wbp, wba, wout = _exchange_alone(_ex_relay(arrived[:3]), "relay_mix")
    wout = wout.reshape(D, D)
    (h2, pm, p, yp, ys, mm), (wgu2, wd2) = _mix_out(h1, xp, o_sb, gp, gs, wgrp, pool_scale, wbp, wba, wout,
                                                    exchange=_ex_relay(arrived[3:]))
    wd2 = wd2.reshape(DFF, D)
    dh3, loss_row, d_gf, n3, gu3, a3 = _ffn_fwd(h2, g2, wgu2, wd2, "ffn2_fwd", head=(tgt, gf))

    def grad_gate_up(n, dgu, name, exchange=None):
        res = _wgrad(n, dgu, NSH, D, name, exchange=exchange)
        return [_halves(res)] if exchange is None else ([_halves(res[0])], res[1])

    def grad_down(a, dh, name, exchange=None):
        res = _wgrad(a, dh, 1, FFS, name, exchange=exchange)
        halves = lambda g: [_halves(g.reshape(NSH, DFF // NSH, D))]
        return halves(res) if exchange is None else (halves(res[0]), res[1])

    k_gu2, k_d2, k_gu1, k_d1, k_in = (("ffn2_w_gate_up",), ("ffn2_w_down",), ("ffn1_w_gate_up",),
                                      ("ffn1_w_down",), ("w_in",))
    dh2, dgu3, d_g2 = _ffn_bwd(dh3, h2, g2, gu3, wgu2, wd2, "ffn2_bwd")
    pa = grad_gate_up(n3, dgu3, "wgrad_gu2") + grad_down(a3, dh3, "wgrad_d2")
    (dlg, dyp, dys, do_sb, dyg, dxp, d_scale), got_a = _mix_bwd_out(
        dh2, gp, gs, yp, ys, pm, wgrp, pool_scale, wbp, wba, wout, exchange=_ex_pair_swap(pa))
    chip_a = pair_sums(k_gu2 + k_d2, pa, got_a)
    kb = ("w_out", "w_branch_pool", "w_branch_attn")
    pb = [_halves(_wgrad(mm, dh2, 1, D, "wgrad_out").reshape(NSH, D // NSH, D)),
          _halves(_wgrad(p, dyp, NSH, PW, "wgrad_bp")), _halves(_wgrad(o_sb, dys, NSH, SBW, "wgrad_ba"))]
    k_a, k_in = k_gu2 + k_d2, k_in + kb
    sems_a, thru_a, token_a = _scatter_start(chip_a, "scatter_a_start")
    dq, dk, dv = _attn_bwd(q, k, v, do_sb, ctot, after=(token_a,))
    chip_a, owned_a = _scatter_wait(sems_a, thru_a, [dq], "scatter_a_wait")
    halves_a = chip_sums(k_a, chip_a, owned_a)
    (dh1, d_gm, dproj), both_a = _mix_bwd_in(dh2, h1, gm, (dxp, dq, dk, dv, dlg), win, exchange=_ex_share(halves_a))
    for i, k_ in enumerate(k_a):
        grad[k_] = both_a[i].reshape(wts[k_].shape)

    p_in = [_halves(_wgrad(u, dproj, NSH, D, "wgrad_in"))] + pb
    p_d1, got_in = grad_down(a1, dh1, "wgrad_d1", exchange=_ex_pair_swap(p_in))
    sems_in, thru_in, token_in = _scatter_start(pair_sums(k_in, p_in, got_in), "scatter_in_start")
    dgu1, got_d1 = _ffn_bwd_act(dh1, gu1, wd1, "ffn1_bwd_act", exchange=_ex_pair_swap(p_d1), after=(token_in,))
    sems_d1, thru_d1, token_d1 = _scatter_start(pair_sums(k_d1, p_d1, got_d1), "scatter_d1_start")
    p_gu1 = [_halves(_wgrad(n1, dgu1, NSH, D, "wgrad_gu1", after=(token_in, token_d1)))]
    chip_in, owned_in = _scatter_wait(sems_in, thru_in, p_gu1, "scatter_in_wait")
    chip_d1, owned_d1 = _scatter_wait(sems_d1, thru_d1, p_gu1, "scatter_d1_wait")
    halves_in, halves_d1 = chip_sums(k_in, chip_in, owned_in), chip_sums(k_d1, chip_d1, owned_d1)
    landed = _exchange_alone(_join(_ex_pair_swap(p_gu1), _ex_share(halves_in)), "pair_swap_gu1")
    for i, k_ in enumerate(k_in):
        grad[k_] = landed[1 + i].reshape(wts[k_].shape)
    sems, thru, token = _scatter_start(pair_sums(k_gu1, p_gu1, landed[:1]), "scatter_gu1_start")
    for k_ in k_a + k_in:
        adamw(k_, after=(token,))
    dx, d_g1 = _ffn_bwd_in(dh1, x0, g1, dgu1, wgu1, "ffn1_bwd_in", after=(token,))
    small_g = dict(ffn1_norm=d_g1, mix_norm=d_gm, ffn2_norm=d_g2, final_norm=d_gf, pool_scale=d_scale,
                   pool_w_group=_wgrad_groups(pm, dyg), loss=loss_row)
    dev = 4 * lax.axis_index("x") + 2 * lax.axis_index("y") + lax.axis_index("c")
    slots = lax.dynamic_update_slice(jnp.zeros((8, SMALL_ROWS, 128), F32), _pack_small(small_g)[None], (dev, 0, 0))
    sems_s, slots, token_s = _small_gather_start(slots, "small_gather_start")

    chip_gu1, owned_gu1 = _scatter_wait(sems, thru, [dx] + [delta[k_] for k_ in k_a + k_in], "scatter_gu1_wait")
    both = _exchange_alone(_ex_share(halves_d1 + chip_sums(k_gu1, chip_gu1, owned_gu1)), "share_last",
                           after=(token_s,))
    grad["ffn1_w_down"] = both[0].reshape(ffn1_w_down.shape)
    grad["ffn1_w_gate_up"] = both[1].reshape(ffn1_w_gate_up.shape)
    for k_ in k_d1 + k_gu1:
        adamw(k_, after=(token_s,))
    gathered = _small_gather_wait(sems_s, slots, [delta[k_] for k_ in k_d1 + k_gu1], "small_gather_wait")
    gathered = gathered.reshape(8 * SMALL_ROWS, 128)
    sg, sd, sm, sv = _small_update(gathered, _pack_small(wts), _pack_small(mom), _pack_small(var))
    sums = _unpack_small(sg, wts)
    loss = sums.pop("loss")
    grad.update(sums)
    for dst, packed in ((delta, sd), (new_m, sm), (new_v, sv)):
        vals = _unpack_small(packed, wts)
        vals.pop("loss")
        dst.update(vals)
    return (loss, dx[None], *[grad[k_] for k_ in ORDER], *[delta[k_] for k_ in ORDER],
            *[new_m[k_] for k_ in ORDER], *[new_v[k_] for k_ in ORDER])
```

```python
import functools

import jax
import jax.numpy as jnp
from jax import lax
from jax.experimental import pallas as pl
from jax.experimental.pallas import tpu as pltpu

F32 = jnp.float32
BF16 = jnp.bfloat16

S = 2048
D = 1024
DFF = 2816
FFS = 2 * DFF // 4
NSH = 4
PW = 512
PG = 128
POOL_WINDOWS = (2, 4, 8, 16)
HALO = 16
SBW = 512
DH = 64
EPS = 1e-6
SCALE = 0.125
LOG2E = 1.4426950408889634
TA = 256
QB = 2
MIB = 1024 * 1024

LR, B1, B2, AEPS, WD, STEP = 0.001, 0.9, 0.999, 1e-08, 0.01, 10

_VM = pl.BlockSpec(memory_space=pltpu.VMEM)
_ANY = pl.BlockSpec(memory_space=pl.ANY)
MESH = pl.DeviceIdType.MESH


def _nn(a, b):
    return jnp.dot(a, b, preferred_element_type=F32)


def _nt(a, b):
    return lax.dot_general(a, b, (((1,), (1,)), ((), ())), preferred_element_type=F32)


def _tn(a, b):
    return lax.dot_general(a, b, (((0,), (0,)), ((), ())), preferred_element_type=F32)


def _params(sem, vmem_mib):
    return pltpu.CompilerParams(dimension_semantics=sem, vmem_limit_bytes=vmem_mib * MIB)


def _rows(tm, width):
    return pl.BlockSpec((tm, width), lambda i: (i, 0))


def _fixed(shape):
    return pl.BlockSpec(shape, lambda *_: (0,) * len(shape))


def _sds(shape, dtype):
    return pltpu.HBM(shape, dtype)


def _in_hbm(args):
    return [pltpu.with_memory_space_constraint(a, pltpu.HBM) for a in args]


def _stage(pairs):
    @pl.when(pl.program_id(0) == 0)
    def _():
        for src, dst in pairs:
            pltpu.sync_copy(src, dst)


def _vmem_like(*arrays):
    return [pltpu.VMEM(a.shape, a.dtype) for a in arrays]


class Exchange:
    def __init__(self, arrays, landing, aliases, n_sems, start, finish):
        self.arrays, self.landing, self.aliases, self.n_sems = list(arrays), list(landing), dict(aliases), n_sems
        self.start, self.finish = start, finish


def _join(a, b):
    na, la = len(a.arrays), len(a.landing)

    def both(fa, fb):
        def run(ins, outs, ssem, rsem):
            fa(ins[:na], outs[:la], ssem.at[pl.ds(0, a.n_sems)], rsem.at[pl.ds(0, a.n_sems)])
            fb(ins[na:], outs[la:], ssem.at[pl.ds(a.n_sems, b.n_sems)], rsem.at[pl.ds(a.n_sems, b.n_sems)])
        return run

    aliases = {**a.aliases, **{na + i: la + j for i, j in b.aliases.items()}}
    return Exchange(a.arrays + b.arrays, a.landing + b.landing, aliases, a.n_sems + b.n_sems,
                    both(a.start, b.start), both(a.finish, b.finish))


def _call(body, args, *, name, grid, in_specs, out_specs, out_shape, scratch_shapes=(), compiler_params=None,
          exchange=None, free=(), after=()):
    args = [a if i in free else pltpu.with_memory_space_constraint(a, pltpu.HBM) for i, a in enumerate(args)]
    if exchange is None:
        n_in = len(in_specs)

        def plain(*refs):
            body(*refs[:n_in], *refs[n_in + len(after):])

        return pl.pallas_call(plain, name=name, grid=grid, in_specs=list(in_specs) + [_ANY] * len(after),
                              out_specs=out_specs, out_shape=out_shape, scratch_shapes=list(scratch_shapes),
                              compiler_params=compiler_params)(*args, *after)
    ex = exchange
    n_in, n_out, n_scr = len(in_specs), len(out_specs), len(scratch_shapes)
    na, nl = len(ex.arrays), len(ex.landing)

    def hosted(*refs):
        at = [0]

        def take(n):
            at[0] += n
            return refs[at[0] - n:at[0]]

        k_in, _, e_in, k_out, e_out, k_scr = take(n_in), take(len(after)), take(na), take(n_out), take(nl), take(n_scr)
        ssem, rsem = take(2)
        ids = [pl.program_id(a) for a in range(len(grid))]
        first = functools.reduce(jnp.logical_and, [i == 0 for i in ids])
        last = functools.reduce(jnp.logical_and, [i == g - 1 for i, g in zip(ids, grid)])

        @pl.when(first)
        def _():
            ex.start(e_in, e_out, ssem, rsem)

        body(*k_in, *k_out, *k_scr)

        @pl.when(last)
        def _():
            ex.finish(e_in, e_out, ssem, rsem)

    outs = pl.pallas_call(
        hosted, name=name, grid=grid,
        in_specs=list(in_specs) + [_ANY] * (len(after) + na), out_specs=list(out_specs) + [_ANY] * nl,
        out_shape=list(out_shape) + ex.landing,
        scratch_shapes=list(scratch_shapes) + [pltpu.SemaphoreType.DMA((ex.n_sems,))] * 2,
        input_output_aliases={n_in + len(after) + i: n_out + j for i, j in ex.aliases.items()},
        compiler_params=compiler_params,
    )(*args, *after, *_in_hbm(ex.arrays))
    return outs[:n_out], outs[n_out:]


def _exchange_alone(ex, name, after=()):
    na, nl = len(ex.arrays), len(ex.landing)

    def body(*refs):
        outs = refs[na + len(after):na + len(after) + nl]
        ex.start(refs[:na], outs, refs[-2], refs[-1])
        ex.finish(refs[:na], outs, refs[-2], refs[-1])

    return pl.pallas_call(
        body, name=name, in_specs=[_ANY] * (na + len(after)), out_specs=[_ANY] * nl,
        out_shape=ex.landing, scratch_shapes=[pltpu.SemaphoreType.DMA((ex.n_sems,))] * 2,
        input_output_aliases=ex.aliases,
    )(*_in_hbm(ex.arrays), *after)


_HBM = pl.BlockSpec(memory_space=pltpu.HBM)
_SEM = pl.BlockSpec(memory_space=pltpu.SEMAPHORE)
_EFFECT = pltpu.SideEffectType.DATAFLOW_SIDE_EFFECTING


def _scatter_copies(srcs, lands, ssems, rsems):
    x, y, c, chips = _place()
    return [_remote(srcs[w].at[2 * px + py], lands[w].at[k], ssems[3 * w + k], rsems[3 * w + k], (px, py, c))
            for w in range(len(srcs)) for k, (px, py) in enumerate(chips)]


def _scatter_start(parts, name):
    n, ncp = len(parts), 3 * len(parts)
    lands = [lax.empty((3,) + p.shape[1:], p.dtype) for p in parts]

    def body(*refs):
        srcs, land_refs = refs[:n], refs[n:2 * n]
        ssems, rsems = refs[2 * n:2 * n + ncp], refs[2 * n + ncp:2 * n + 2 * ncp]
        for cp in _scatter_copies(srcs, land_refs, ssems, rsems):
            cp.start()
        token = refs[-1]
        token[...] = jnp.zeros_like(token)

    outs = pl.pallas_call(
        body, name=name,
        out_shape=([pltpu.SemaphoreType.DMA(())] * (2 * ncp) + [pltpu.HBM(a.shape, a.dtype) for a in parts + lands]
                   + [jax.ShapeDtypeStruct((8, 128), F32)]),
        in_specs=[_HBM] * (2 * n), out_specs=[_SEM] * (2 * ncp) + [_HBM] * (2 * n) + [_VM],
        input_output_aliases={i: 2 * ncp + i for i in range(2 * n)},
        compiler_params=pltpu.CompilerParams(has_side_effects=_EFFECT),
    )(*_in_hbm(parts), *_in_hbm(lands))
    sems, thru, token = outs[:2 * ncp], outs[2 * ncp:2 * ncp + 2 * n], outs[-1]
    return sems, thru, token


def _scatter_wait(sems, thru, after, name):
    n = len(thru) // 2
    ncp = 3 * n

    def body(*refs):
        srcs, land_refs = refs[:n], refs[n:2 * n]
        ssems, rsems = refs[2 * n:2 * n + ncp], refs[2 * n + ncp:2 * n + 2 * ncp]
        for cp in _scatter_copies(srcs, land_refs, ssems, rsems):
            cp.wait_send()
            cp.wait_recv()

    outs = pl.pallas_call(
        body, name=name, out_shape=[pltpu.HBM(a.shape, a.dtype) for a in thru],
        in_specs=[_HBM] * (2 * n) + [_SEM] * (2 * ncp) + [_ANY] * len(after), out_specs=[_HBM] * (2 * n),
        input_output_aliases={i: i for i in range(2 * n)},
        compiler_params=pltpu.CompilerParams(has_side_effects=_EFFECT),
    )(*thru, *sems, *after)
    return outs[:n], outs[n:]


def _gather_copies(bufs, ssems, rsems, sending):
    x, y, c, chips = _place()
    out = []
    for w, ref in enumerate(bufs):
        half = ref.shape[1] // 2
        for k, (px, py) in enumerate(chips):
            rows = ref.at[2 * x + y if sending else 2 * px + py, pl.ds(c * half, half)]
            out.append(_remote(rows, rows, ssems[3 * w + k], rsems[3 * w + k], (px, py, c)))
    return out


def _gather_start(bufs, after, name):
    n, ncp = len(bufs), 3 * len(bufs)

    def body(*refs):
        ssems, rsems = refs[n + len(after):n + len(after) + ncp], refs[n + len(after) + ncp:n + len(after) + 2 * ncp]
        for cp in _gather_copies(refs[:n], ssems, rsems, True):
            cp.start()
        token = refs[-1]
        token[...] = jnp.zeros_like(token)

    outs = pl.pallas_call(
        body, name=name,
        out_shape=([pltpu.SemaphoreType.DMA(())] * (2 * ncp) + [pltpu.HBM(a.shape, a.dtype) for a in bufs]
                   + [jax.ShapeDtypeStruct((8, 128), F32)]),
        in_specs=[_HBM] * n + [_ANY] * len(after), out_specs=[_SEM] * (2 * ncp) + [_HBM] * n + [_VM],
        input_output_aliases={i: 2 * ncp + i for i in range(n)},
        compiler_params=pltpu.CompilerParams(has_side_effects=_EFFECT),
    )(*_in_hbm(bufs), *after)
    return outs[:2 * ncp], outs[2 * ncp:2 * ncp + n], outs[-1]


def _gather_wait(sems, thru, after, name):
    n = len(thru)
    ncp = 3 * n

    def body(*refs):
        ssems, rsems = refs[n:n + ncp], refs[n + ncp:n + 2 * ncp]
        for cp in _gather_copies(refs[:n], ssems, rsems, True):
            cp.wait_send()
        for cp in _gather_copies(refs[:n], ssems, rsems, False):
            cp.wait_recv()

    return pl.pallas_call(
        body, name=name, out_shape=[pltpu.HBM(a.shape, a.dtype) for a in thru],
        in_specs=[_HBM] * n + [_SEM] * (2 * ncp) + [_ANY] * len(after), out_specs=[_HBM] * n,
        input_output_aliases={i: i for i in range(n)},
        compiler_params=pltpu.CompilerParams(has_side_effects=_EFFECT),
    )(*thru, *sems, *after)


def _rms(x):
    r = lax.rsqrt(jnp.mean(x * x, axis=-1, keepdims=True) + EPS)
    return r, x * r


def _rms_bwd(dn, xr, r, gain):
    dng = dn * gain
    dx = r * (dng - xr * jnp.mean(dng * xr, axis=-1, keepdims=True))
    return dx, jnp.sum(dn * xr, axis=0, keepdims=True)


def _ffn_fwd(x, gain, wgu, wd, name, exchange=None, head=None):
    tm = 256

    def body(x_ref, g_ref, wgu_hbm, wd_hbm, *rest):
        if head is None:
            h_ref, n_ref, gu_ref, a_ref, wgu_ref, wd_ref = rest
        else:
            t_ref, gf_ref, h_ref, loss_ref, dgf_ref, n_ref, gu_ref, a_ref, wgu_ref, wd_ref = rest
        _stage([(wgu_hbm, wgu_ref), (wd_hbm, wd_ref)])
        x = x_ref[...]
        _, xr = _rms(x)
        n = (xr * g_ref[...]).astype(BF16)
        n_ref[...] = n
        acc = jnp.zeros((tm, D), F32)
        for j in range(2):
            g = _nn(n, wgu_ref[j])
            u = _nn(n, wgu_ref[2 + j])
            gu_ref[:, j * FFS:(j + 1) * FFS] = g.astype(BF16)
            gu_ref[:, (2 + j) * FFS:(3 + j) * FFS] = u.astype(BF16)
            half_act = (0.5 * (g * jax.nn.sigmoid(g) * u)).astype(BF16)
            a_ref[:, j * FFS:(j + 1) * FFS] = half_act
            acc = acc + _nn(half_act, wd_ref[j * FFS:(j + 1) * FFS, :])
        h = x + acc
        if head is None:
            h_ref[...] = h
            return
        gf = gf_ref[...]
        r, hr = _rms(h)
        err = hr * gf - t_ref[...]
        dh, dgain = _rms_bwd(err * (1.0 / D), hr, r, gf)
        h_ref[...] = dh

        @pl.when(pl.program_id(0) == 0)
        def _():
            dgf_ref[...] = jnp.zeros_like(dgf_ref)
            loss_ref[...] = jnp.zeros_like(loss_ref)

        dgf_ref[...] += dgain
        loss_ref[...] += jnp.full((1, 128), (0.5 / D) * jnp.sum(err * err), F32)

    saved_specs = [_rows(tm, D), _rows(tm, 4 * FFS), _rows(tm, DFF)]
    saved_shapes = [_sds((S, D), BF16), _sds((S, 4 * FFS), BF16), _sds((S, DFF), BF16)]
    if head is None:
        return _call(
            body, (x, gain, wgu, wd), name=name, grid=(S // tm,),
            in_specs=[_rows(tm, D), _fixed((1, D)), _ANY, _ANY],
            out_specs=[_rows(tm, D)] + saved_specs, out_shape=[_sds((S, D), F32)] + saved_shapes,
            scratch_shapes=_vmem_like(wgu, wd),
            compiler_params=_params(("arbitrary",), 56), exchange=exchange)
    return _call(
        body, (x, gain, wgu, wd, *head), name=name, grid=(S // tm,),
        in_specs=[_rows(tm, D), _fixed((1, D)), _ANY, _ANY, _rows(tm, D), _fixed((1, D))],
        out_specs=[_rows(tm, D), _fixed((1, 128)), _fixed((1, D))] + saved_specs,
        out_shape=[_sds((S, D), F32), _sds((1, 128), F32), _sds((1, D), F32)] + saved_shapes,
        scratch_shapes=_vmem_like(wgu, wd),
        compiler_params=_params(("arbitrary",), 56), exchange=exchange, free=(4, 5))


def _ffn_bwd(dh, x, gain, gu, wgu, wd, name):
    tm = 256

    def body(dh_ref, x_ref, g_ref, gu_ref, wgu_hbm, wd_hbm, dx_ref, dgu_ref, dg_ref, wgu_ref, wd_ref):
        _stage([(wgu_hbm, wgu_ref), (wd_hbm, wd_ref)])
        dh = dh_ref[...]
        dhb = dh.astype(BF16)
        dn = jnp.zeros((tm, D), F32)
        for j in range(2):
            g = gu_ref[:, j * FFS:(j + 1) * FFS].astype(F32)
            u = gu_ref[:, (2 + j) * FFS:(3 + j) * FFS].astype(F32)
            da = 0.5 * _nt(dhb, wd_ref[j * FFS:(j + 1) * FFS, :])
            sg = jax.nn.sigmoid(g)
            dgb = (da * u * (sg * (1.0 + g * (1.0 - sg)))).astype(BF16)
            dub = (da * (g * sg)).astype(BF16)
            dgu_ref[:, j * FFS:(j + 1) * FFS] = dgb
            dgu_ref[:, (2 + j) * FFS:(3 + j) * FFS] = dub
            dn = dn + _nt(dgb, wgu_ref[j]) + _nt(dub, wgu_ref[2 + j])
        r, xr = _rms(x_ref[...])
        dx, dgain = _rms_bwd(dn, xr, r, g_ref[...])
        dx_ref[...] = dh + dx

        @pl.when(pl.program_id(0) == 0)
        def _():
            dg_ref[...] = jnp.zeros_like(dg_ref)

        dg_ref[...] += dgain

    return _call(
        body, (dh, x, gain, gu, wgu, wd), name=name, grid=(S // tm,),
        in_specs=[_rows(tm, D), _rows(tm, D), _fixed((1, D)), _rows(tm, 4 * FFS), _ANY, _ANY],
        out_specs=[_rows(tm, D), _rows(tm, 4 * FFS), _fixed((1, D))],
        out_shape=[_sds((S, D), F32), _sds((S, 4 * FFS), BF16), _sds((1, D), F32)],
        scratch_shapes=_vmem_like(wgu, wd), compiler_params=_params(("arbitrary",), 56))


def _ffn_bwd_act(dh, gu, wd, name, exchange=None, after=()):
    tm = 512

    def body(dh_ref, gu_ref, wd_hbm, dgu_ref, wd_ref):
        _stage([(wd_hbm, wd_ref)])
        dhb = dh_ref[...].astype(BF16)
        for j in range(2):
            g = gu_ref[:, j * FFS:(j + 1) * FFS].astype(F32)
            u = gu_ref[:, (2 + j) * FFS:(3 + j) * FFS].astype(F32)
            da = 0.5 * _nt(dhb, wd_ref[j * FFS:(j + 1) * FFS, :])
            sg = jax.nn.sigmoid(g)
            dgu_ref[:, j * FFS:(j + 1) * FFS] = (da * u * (sg * (1.0 + g * (1.0 - sg)))).astype(BF16)
            dgu_ref[:, (2 + j) * FFS:(3 + j) * FFS] = (da * (g * sg)).astype(BF16)

    res = _call(
        body, (dh, gu, wd), name=name, grid=(S // tm,),
        in_specs=[_rows(tm, D), _rows(tm, 4 * FFS), _ANY], out_specs=[_rows(tm, 4 * FFS)],
        out_shape=[_sds((S, 4 * FFS), BF16)], scratch_shapes=_vmem_like(wd),
        compiler_params=_params(("arbitrary",), 56), exchange=exchange, after=after)
    return res[0] if exchange is None else (res[0][0], res[1])


def _ffn_bwd_in(dh, x, gain, dgu, wgu, name, exchange=None, after=()):
    tm = 512

    def body(dh_ref, x_ref, g_ref, dgu_ref, wgu_hbm, dx_ref, dg_ref, wgu_ref):
        _stage([(wgu_hbm, wgu_ref)])
        dn = jnp.zeros((tm, D), F32)
        for j in range(NSH):
            dn = dn + _nt(dgu_ref[:, j * FFS:(j + 1) * FFS], wgu_ref[j])
        r, xr = _rms(x_ref[...])
        dx, dgain = _rms_bwd(dn, xr, r, g_ref[...])
        dx_ref[...] = dh_ref[...] + dx

        @pl.when(pl.program_id(0) == 0)
        def _():
            dg_ref[...] = jnp.zeros_like(dg_ref)

        dg_ref[...] += dgain

    return _call(
        body, (dh, x, gain, dgu, wgu), name=name, grid=(S // tm,),
        in_specs=[_rows(tm, D), _rows(tm, D), _fixed((1, D)), _rows(tm, 4 * FFS), _ANY],
        out_specs=[_rows(tm, D), _fixed((1, D))],
        out_shape=[_sds((S, D), F32), _sds((1, D), F32)],
        scratch_shapes=_vmem_like(wgu),
        compiler_params=_params(("arbitrary",), 56), exchange=exchange, after=after)


def _mix_in(h, gain, w_in, after=()):
    tm = 512

    def body(h_ref, g_ref, w_hbm, u_ref, xp_ref, q_ref, k_ref, v_ref, gp_ref, gs_ref, w_ref):
        _stage([(w_hbm, w_ref)])
        _, hr = _rms(h_ref[...])
        u = (hr * g_ref[...]).astype(BF16)
        u_ref[...] = u
        p0 = _nn(u, w_ref[0])
        xp_ref[...] = p0[:, :PW]
        q_ref[...] = p0[:, PW:].astype(BF16)
        p1 = _nn(u, w_ref[1])
        k_ref[...] = p1[:, :SBW].astype(BF16)
        v_ref[...] = p1[:, SBW:].astype(BF16)
        gp_ref[...] = jax.nn.sigmoid(_nn(u, w_ref[2])).astype(BF16)
        gs_ref[...] = jax.nn.sigmoid(_nn(u, w_ref[3])).astype(BF16)

    return _call(
        body, (h, gain, w_in), name="mix_in", grid=(S // tm,),
        in_specs=[_rows(tm, D), _fixed((1, D)), _ANY],
        out_specs=[_rows(tm, D), _rows(tm, PW), _rows(tm, SBW), _rows(tm, SBW), _rows(tm, SBW),
                   _rows(tm, D), _rows(tm, D)],
        out_shape=[_sds((S, D), BF16), _sds((S, PW), F32), _sds((S, SBW), BF16), _sds((S, SBW), BF16),
                   _sds((S, SBW), BF16), _sds((S, D), BF16), _sds((S, D), BF16)],
        scratch_shapes=_vmem_like(w_in),
        compiler_params=_params(("arbitrary",), 48), free=(1,), after=after)


def _hilo_dot(x, tri):
    hi = x.astype(BF16)
    lo = (x - hi.astype(F32)).astype(BF16)
    return _nn(hi, tri) + _nn(lo, tri)


def _log_terms(qk):
    z2 = qk * (SCALE * LOG2E)
    lb = jnp.minimum(z2, 0.0) - jnp.log2(1.0 + jnp.exp2(-jnp.abs(z2)))
    return lb, lb - z2


def _head_masks():
    lane = lax.broadcasted_iota(jnp.int32, (1, 2 * DH), 1)
    return (lane < DH, lane >= DH)


def _attn_fwd(q, k, v, exchange=None):
    T = TA

    def body(q_ref, k_ref, v_ref, o_ref, c_ref):
        i2 = 2 * pl.program_id(1)
        row = lax.broadcasted_iota(jnp.int32, (T, T), 0)
        col = lax.broadcasted_iota(jnp.int32, (T, T), 1)
        after = (row > col).astype(BF16)
        causal = col < row
        masks = _head_masks()
        qms = {}
        for b in range(QB):
            q2 = q_ref[b * T:(b + 1) * T, :]
            for h, hm in enumerate(masks):
                qms[b, h] = jnp.where(hm, q2, jnp.zeros_like(q2))

        def blocks(keys, pairs, carries, os):
            ks, vms = [], []
            for j in keys:
                rows = pl.ds(pl.multiple_of(j * T, T), T)
                vj = v_ref[rows, :]
                ks.append(k_ref[rows, :])
                vms.append([jnp.where(hm, vj, jnp.zeros_like(vj)) for hm in masks])
            units = [(n, h) for n in range(len(pairs)) for h in range(2)]
            qks = {(n, h): _nt(qms[pairs[n][0], h], ks[pairs[n][1]]) for n, h in units}
            lbs, l1ms = {}, {}
            for u in units:
                lbs[u], l1m = _log_terms(qks[u])
                l1ms[u] = jnp.where(causal, l1m, 0.0) if pairs[u[0]][2] else l1m
            cins = {u: _hilo_dot(l1ms[u], after) for u in units}
            carries, os = dict(carries), list(os)
            for n, h in units:
                b, key, diag = pairs[n]
                a = jnp.exp2(lbs[n, h] + cins[n, h] + carries[b, h])
                if diag:
                    a = jnp.where(causal, a, 0.0)
                os[b] = os[b] + _nn(a.astype(BF16), vms[key][h])
                carries[b, h] = carries[b, h] + jnp.sum(l1ms[n, h], axis=1, keepdims=True)
            return carries, tuple(os)

        carries = {(b, h): jnp.zeros((T, 1), F32) for b in range(QB) for h in range(2)}
        os = tuple(jnp.zeros((T, 2 * DH), F32) for _ in range(QB))
        carries, os = blocks([i2 + 1, i2], [(1, 0, True), (0, 1, True), (1, 1, False)], carries, os)
        carries, os = lax.fori_loop(
            0, i2, lambda jj, c: blocks([i2 - 1 - jj], [(0, 0, False), (1, 0, False)], c[0], c[1]), (carries, os))
        for b in range(QB):
            o_ref[b * T:(b + 1) * T, :] = os[b].astype(BF16)
            c_ref[b * T:(b + 1) * T, :] = jnp.where(masks[0], carries[b, 0], carries[b, 1])

    blk = pl.BlockSpec((QB * T, 2 * DH), lambda p, i: (i, p))
    full = pl.BlockSpec((S, 2 * DH), lambda p, i: (0, p))
    return _call(
        body, (q, k, v), name="attn_fwd", grid=(SBW // (2 * DH), S // (QB * T)),
        in_specs=[blk, full, full], out_specs=[blk, blk],
        out_shape=[_sds((S, SBW), BF16), _sds((S, SBW), F32)],
        compiler_params=_params(("arbitrary", "arbitrary"), 40), exchange=exchange)


def _attn_bwd(q, k, v, do, ctot, after=()):
    T = TA
    nq = S // (QB * T)

    def body(q_ref, k_ref, v_ref, do_ref, c_ref, dq_ref, dk_ref, dv_ref, dk_acc, dv_acc):
        step = pl.program_id(1)
        i2 = 2 * step

        @pl.when(step == 0)
        def _():
            dk_acc[...] = jnp.zeros_like(dk_acc)
            dv_acc[...] = jnp.zeros_like(dv_acc)

        row = lax.broadcasted_iota(jnp.int32, (T, T), 0)
        col = lax.broadcasted_iota(jnp.int32, (T, T), 1)
        upto = (row <= col).astype(BF16)
        before = (row < col).astype(BF16)
        causal = col < row
        masks = _head_masks()
        qms, doms, ctots = {}, {}, {}
        for b in range(QB):
            q2, do2 = q_ref[b * T:(b + 1) * T, :], do_ref[b * T:(b + 1) * T, :]
            for h, hm in enumerate(masks):
                qms[b, h] = jnp.where(hm, q2, jnp.zeros_like(q2))
                doms[b, h] = jnp.where(hm, do2, jnp.zeros_like(do2))
                ctots[b, h] = c_ref[b * T:(b + 1) * T, h * DH:h * DH + 1]

        def blocks(keys, pairs, sums, dqs):
            rows = [pl.ds(pl.multiple_of(j * T, T), T) for j in keys]
            ks, vs = [k_ref[r, :] for r in rows], [v_ref[r, :] for r in rows]
            kms = [[jnp.where(hm, kj, jnp.zeros_like(kj)) for hm in masks] for kj in ks]
            units = [(n, h) for n in range(len(pairs)) for h in range(2)]
            qks = {(n, h): _nt(qms[pairs[n][0], h], ks[pairs[n][1]]) for n, h in units}
            das = {(n, h): _nt(doms[pairs[n][0], h], vs[pairs[n][1]]) for n, h in units}
            lbs, l1ms = {}, {}
            for u in units:
                lbs[u], l1m = _log_terms(qks[u])
                l1ms[u] = jnp.where(causal, l1m, 0.0) if pairs[u[0]][2] else l1m
            pins = {u: _hilo_dot(l1ms[u], upto) for u in units}
            sums = dict(sums)
            a_s, dls, cps = {}, {}, {}
            for n, h in units:
                b, _, diag = pairs[n]
                cl, cp = sums[b, h]
                a = jnp.exp2(lbs[n, h] + (ctots[b, h] - cl) - pins[n, h])
                if diag:
                    a = jnp.where(causal, a, 0.0)
                a_s[n, h] = a.astype(BF16)
                dls[n, h] = das[n, h] * a
                cps[n, h] = cp
                sums[b, h] = (cl + jnp.sum(l1ms[n, h], axis=1, keepdims=True),
                              cp + jnp.sum(dls[n, h], axis=1, keepdims=True))
            pexs = {u: _hilo_dot(dls[u], before) for u in units}
            dzbs = {}
            for u in units:
                dz = dls[u] - jnp.exp2(lbs[u]) * (dls[u] + pexs[u] + cps[u])
                if pairs[u[0]][2]:
                    dz = jnp.where(causal, dz, 0.0)
                dzbs[u] = dz.astype(BF16)
            dqs = list(dqs)
            for n, h in units:
                dqs[pairs[n][0]] = dqs[pairs[n][0]] + _nn(dzbs[n, h], kms[pairs[n][1]][h])
            for key, r in enumerate(rows):
                mine = [(n, h) for n, h in units if pairs[n][1] == key]
                dk_acc[r, :] += functools.reduce(jnp.add, [_tn(dzbs[u], qms[pairs[u[0]][0], u[1]]) for u in mine])
                dv_acc[r, :] += functools.reduce(jnp.add, [_tn(a_s[u], doms[pairs[u[0]][0], u[1]]) for u in mine])
            return sums, tuple(dqs)

        zero = jnp.zeros((T, 1), F32)
        sums = {(b, h): (zero, zero) for b in range(QB) for h in range(2)}
        dqs = tuple(jnp.zeros((T, 2 * DH), F32) for _ in range(QB))
        sums, dqs = lax.fori_loop(
            0, i2, lambda j, c: blocks([j], [(0, 0, False), (1, 0, False)], c[0], c[1]), (sums, dqs))
        _, dqs = blocks([i2, i2 + 1], [(0, 0, True), (1, 0, False), (1, 1, True)], sums, dqs)
        for b in range(QB):
            dq_ref[b * T:(b + 1) * T, :] = (dqs[b] * SCALE).astype(BF16)

        @pl.when(step == nq - 1)
        def _():
            dk_ref[...] = (dk_acc[...] * SCALE).astype(BF16)
            dv_ref[...] = dv_acc[...].astype(BF16)

    blk = pl.BlockSpec((QB * T, 2 * DH), lambda p, i: (i, p))
    full = pl.BlockSpec((S, 2 * DH), lambda p, i: (0, p))
    return _call(
        body, (q, k, v, do, ctot), name="attn_bwd", grid=(SBW // (2 * DH), nq),
        in_specs=[blk, full, full, blk, blk], out_specs=[blk, full, full],
        out_shape=[_sds((S, SBW), BF16), _sds((S, SBW), BF16), _sds((S, SBW), BF16)],
        scratch_shapes=[pltpu.VMEM((S, 2 * DH), F32), pltpu.VMEM((S, 2 * DH), F32)],
        compiler_params=_params(("arbitrary", "arbitrary"), 40), after=after)


def _pool_counts(first_row, tm):
    pos = first_row + lax.broadcasted_iota(jnp.int32, (tm, 1), 0)
    return [jnp.minimum(pos + 1, w).astype(F32) for w in POOL_WINDOWS]


def _mix_out(h, xp, o_sb, gp, gs, w_group, scale, w_bp, w_ba, w_out, exchange=None):
    tm = 512

    def body(h_ref, xp_ref, o_ref, gp_ref, gs_ref, wg_hbm, sc_ref, wbp_hbm, wba_hbm, wo_hbm,
             h2_ref, pm_ref, p_ref, yp_ref, ys_ref, m_ref, halo, wg_ref, wbp_ref, wba_ref, wo_ref):
        _stage([(wg_hbm, wg_ref), (wbp_hbm, wbp_ref), (wba_hbm, wba_ref), (wo_hbm, wo_ref)])
        i = pl.program_id(0)

        @pl.when(i == 0)
        def _():
            halo[...] = jnp.zeros_like(halo)

        xp = xp_ref[...]
        ext = jnp.concatenate([halo[...], xp], axis=0)
        halo[...] = xp[tm - HALO:, :]
        counts = _pool_counts(i * tm, tm)
        for gi in range(len(POOL_WINDOWS)):
            lanes = slice(gi * PG, (gi + 1) * PG)
            win = ext[:, lanes]
            for step in range(gi + 1):
                win = win + pltpu.roll(win, 1 << step, 0)
            pm = (win[HALO:, :] / counts[gi] - xp[:, lanes]).astype(BF16)
            pm_ref[:, lanes] = pm
            p_ref[:, lanes] = (_nn(pm, wg_ref[gi]) * sc_ref[:, lanes]).astype(BF16)
        pb = p_ref[...]
        ob = o_ref[...]
        for j in range(NSH):
            cols = slice(j * (D // NSH), (j + 1) * (D // NSH))
            yp = _nn(pb, wbp_ref[j])
            ys = _nn(ob, wba_ref[j])
            yp_ref[:, cols] = yp.astype(BF16)
            ys_ref[:, cols] = ys.astype(BF16)
            m_ref[:, cols] = (gp_ref[:, cols].astype(F32) * yp + gs_ref[:, cols].astype(F32) * ys).astype(BF16)
        h2_ref[...] = h_ref[...] + _nn(m_ref[...], wo_ref[...])

    return _call(
        body, (h, xp, o_sb, gp, gs, w_group, scale, w_bp, w_ba, w_out), name="mix_out", grid=(S // tm,),
        in_specs=[_rows(tm, D), _rows(tm, PW), _rows(tm, SBW), _rows(tm, D), _rows(tm, D),
                  _ANY, _fixed((1, PW)), _ANY, _ANY, _ANY],
        out_specs=[_rows(tm, D), _rows(tm, PW), _rows(tm, PW), _rows(tm, D), _rows(tm, D), _rows(tm, D)],
        out_shape=[_sds((S, D), F32), _sds((S, PW), BF16), _sds((S, PW), BF16), _sds((S, D), BF16),
                   _sds((S, D), BF16), _sds((S, D), BF16)],
        scratch_shapes=[pltpu.VMEM((HALO, PW), F32)] + _vmem_like(w_group, w_bp, w_ba, w_out),
        compiler_params=_params(("arbitrary",), 48), free=(5, 6), exchange=exchange)


def _mix_bwd_out(dh, gp, gs, yp, ys, pm, w_group, scale, w_bp, w_ba, w_out, exchange=None):
    tm = 512
    nt = S // tm

    def body(dh_ref, gp_ref, gs_ref, yp_ref, ys_ref, pm_ref, wg_hbm, sc_ref, wbp_hbm, wba_hbm, wo_hbm,
             dlg_ref, dyp_ref, dys_ref, do_ref, dyg_ref, dxp_ref, dsc_ref, halo, wg_ref, wbp_ref, wba_ref, wo_ref):
        _stage([(wg_hbm, wg_ref), (wbp_hbm, wbp_ref), (wba_hbm, wba_ref), (wo_hbm, wo_ref)])
        step = pl.program_id(0)

        @pl.when(step == 0)
        def _():
            halo[...] = jnp.zeros_like(halo)
            dsc_ref[...] = jnp.zeros_like(dsc_ref)

        dm = _nt(dh_ref[...].astype(BF16), wo_ref[...])
        gp = gp_ref[...].astype(F32)
        gs = gs_ref[...].astype(F32)
        yp = yp_ref[...].astype(F32)
        ys = ys_ref[...].astype(F32)
        dlg_ref[:, :D] = (dm * yp * gp * (1.0 - gp)).astype(BF16)
        dlg_ref[:, D:] = (dm * ys * gs * (1.0 - gs)).astype(BF16)
        dyp_ref[...] = (dm * gp).astype(BF16)
        dys_ref[...] = (dm * gs).astype(BF16)
        dp = jnp.zeros((tm, PW), F32)
        do = jnp.zeros((tm, SBW), F32)
        for j in range(NSH):
            cols = slice(j * (D // NSH), (j + 1) * (D // NSH))
            dp = dp + _nt(dyp_ref[:, cols], wbp_ref[j])
            do = do + _nt(dys_ref[:, cols], wba_ref[j])
        do_ref[...] = do.astype(BF16)
        counts = _pool_counts((nt - 1 - step) * tm, tm)
        dscale = []
        for gi in range(len(POOL_WINDOWS)):
            lanes = slice(gi * PG, (gi + 1) * PG)
            dpg = dp[:, lanes]
            dscale.append(jnp.sum(dpg * _nn(pm_ref[:, lanes], wg_ref[gi]), axis=0, keepdims=True))
            dyg = (dpg * sc_ref[:, lanes]).astype(BF16)
            dyg_ref[:, lanes] = dyg
            dpm = _nt(dyg, wg_ref[gi])
            per = dpm / counts[gi]
            win = jnp.concatenate([per, halo[:, lanes]], axis=0)
            halo[:, lanes] = per[:HALO, :]
            for s in range(gi + 1):
                win = win + pltpu.roll(win, tm + HALO - (1 << s), 0)
            dxp_ref[:, lanes] = (win[:tm, :] - dpm).astype(BF16)
        dsc_ref[...] += jnp.concatenate(dscale, axis=1)

    rev = lambda width: pl.BlockSpec((tm, width), lambda i: (nt - 1 - i, 0))
    return _call(
        body, (dh, gp, gs, yp, ys, pm, w_group, scale, w_bp, w_ba, w_out), name="mix_bwd_out", grid=(nt,),
        in_specs=[rev(D), rev(D), rev(D), rev(D), rev(D), rev(PW), _ANY, _fixed((1, PW)), _ANY, _ANY, _ANY],
        out_specs=[rev(2 * D), rev(D), rev(D), rev(SBW), rev(PW), rev(PW), _fixed((1, PW))],
        out_shape=[_sds((S, 2 * D), BF16), _sds((S, D), BF16), _sds((S, D), BF16), _sds((S, SBW), BF16),
                   _sds((S, PW), BF16), _sds((S, PW), BF16), _sds((1, PW), F32)],
        scratch_shapes=[pltpu.VMEM((HALO, PW), F32)] + _vmem_like(w_group, w_bp, w_ba, w_out),
        compiler_params=_params(("arbitrary",), 48), exchange=exchange)


def _mix_bwd_in(dh, h, gain, pieces, w_in, exchange=None):
    tm = 512
    widths = [p.shape[1] for p in pieces]

    def body(dh_ref, h_ref, g_ref, *rest):
        piece_refs, (w_hbm, dx_ref, dg_ref, dp_ref, w_ref) = rest[:len(pieces)], rest[len(pieces):]
        _stage([(w_hbm, w_ref)])
        at = 0
        for ref, width in zip(piece_refs, widths):
            dp_ref[:, at:at + width] = ref[...]
            at += width
        du = jnp.zeros((tm, D), F32)
        for j in range(NSH):
            du = du + _nt(dp_ref[:, j * D:(j + 1) * D], w_ref[j])
        r, hr = _rms(h_ref[...])
        dx, dgain = _rms_bwd(du, hr, r, g_ref[...])
        dx_ref[...] = dh_ref[...] + dx

        @pl.when(pl.program_id(0) == 0)
        def _():
            dg_ref[...] = jnp.zeros_like(dg_ref)

        dg_ref[...] += dgain

    return _call(
        body, (dh, h, gain, *pieces, w_in), name="mix_bwd_in", grid=(S // tm,),
        in_specs=[_rows(tm, D), _rows(tm, D), _fixed((1, D))] + [_rows(tm, w) for w in widths] + [_ANY],
        out_specs=[_rows(tm, D), _fixed((1, D)), _rows(tm, 4 * D)],
        out_shape=[_sds((S, D), F32), _sds((1, D), F32), _sds((S, 4 * D), BF16)],
        scratch_shapes=_vmem_like(w_in),
        compiler_params=_params(("arbitrary",), 48), exchange=exchange)


def _wgrad(a, b, nblk, ti, name, out_dtype=BF16, exchange=None, after=()):
    ka, n = a.shape[1], b.shape[1]
    ns = n // nblk

    def body(a_ref, b_ref, o_ref):
        o_ref[...] = _tn(a_ref[...].astype(BF16), b_ref[...].astype(BF16)).astype(out_dtype)

    res = _call(
        body, (a, b), name=name, grid=(nblk, ka // ti),
        in_specs=[pl.BlockSpec((S, ti), lambda j, i: (0, i)), pl.BlockSpec((S, ns), lambda j, i: (0, j))],
        out_specs=[pl.BlockSpec((None, ti, ns), lambda j, i: (j, i, 0))],
        out_shape=[_sds((nblk, ka, ns), out_dtype)],
        compiler_params=_params(("arbitrary", "arbitrary"), 56), exchange=exchange, after=after)
    return res[0] if exchange is None else (res[0][0], res[1])


def _wgrad_groups(pm, dyg):
    def body(a_ref, b_ref, o_ref):
        o_ref[...] = _tn(a_ref[...], b_ref[...])

    col = pl.BlockSpec((S, PG), lambda g: (0, g))
    return pl.pallas_call(
        body, name="wgrad_groups", grid=(PW // PG,),
        in_specs=[col, col], out_specs=pl.BlockSpec((None, PG, PG), lambda g: (g, 0, 0)),
        out_shape=_sds((PW // PG, PG, PG), F32),
        compiler_params=_params(("arbitrary",), 32),
    )(*_in_hbm([pm, dyg]))


def _place():
    x, y, c = lax.axis_index("x"), lax.axis_index("y"), lax.axis_index("c")
    chips = [(1 - x, y), (x, 1 - y), (1 - x, 1 - y)]
    return x, y, c, chips


def _remote(src, dst, ssem, rsem, dev):
    return pltpu.make_async_remote_copy(src_ref=src, dst_ref=dst, send_sem=ssem, recv_sem=rsem,
                                        device_id=dev, device_id_type=MESH)


def _cast_into_block(w, me_idx, name):
    rows, cols = w.shape
    tr = _row_block(rows)

    def body(me_ref, w_ref, o_ref):
        o_ref[...] = w_ref[...].astype(BF16)

    return pl.pallas_call(
        body, name=name, out_shape=_sds((NSH, rows, cols), BF16),
        grid_spec=pltpu.PrefetchScalarGridSpec(
            num_scalar_prefetch=1, grid=(rows // tr,),
            in_specs=[pl.BlockSpec((tr, cols), lambda r, me: (r, 0))],
            out_specs=pl.BlockSpec((None, tr, cols), lambda r, me: (me[0], r, 0))),
        compiler_params=_params(("arbitrary",), 32),
    )(me_idx, w)


def _ex_gather(bufs):
    n = len(bufs)
    per = 8

    def plan(outs, ssem, rsem, w):
        x, y, c, _ = _place()
        sib, nbr_x, nbr_y = (x, y, 1 - c), (1 - x, y, c), (x, 1 - y, c)
        half = outs[w].shape[1] // 2
        quarter = half // 2
        sem = lambda k: (ssem.at[per * w + k], rsem.at[per * w + k])
        rows = lambda blk, start, size: outs[w].at[blk, pl.ds(start, size)]
        mine = rows(2 * x + y, c * half, half)
        from_x = rows(2 * (1 - x) + y, c * half, half)
        from_y = rows(2 * x + (1 - y), c * half, half)
        diag = 2 * (1 - x) + (1 - y)
        pass_y = rows(2 * (1 - x) + y, c * half, quarter)
        pass_x = rows(2 * x + (1 - y), c * half + quarter, quarter)
        diag_0, diag_1 = rows(diag, c * half, quarter), rows(diag, c * half + quarter, quarter)
        first = [_remote(mine, mine, *sem(0), nbr_x), _remote(mine, mine, *sem(1), nbr_y)]
        arrivals = [
            (_remote(from_x, from_x, *sem(0), nbr_x),
             [_remote(pass_y, pass_y, *sem(2), nbr_y), _remote(from_x, from_x, *sem(4), sib)]),
            (_remote(from_y, from_y, *sem(1), nbr_y),
             [_remote(pass_x, pass_x, *sem(3), nbr_x), _remote(from_y, from_y, *sem(5), sib)]),
            (_remote(diag_0, diag_0, *sem(2), nbr_y), [_remote(diag_0, diag_0, *sem(6), sib)]),
            (_remote(diag_1, diag_1, *sem(3), nbr_x), [_remote(diag_1, diag_1, *sem(7), sib)]),
        ]
        other = (1 - c) * half
        from_sibling = [
            _remote(rows(2 * (1 - x) + y, other, half), rows(2 * (1 - x) + y, other, half), *sem(4), sib),
            _remote(rows(2 * x + (1 - y), other, half), rows(2 * x + (1 - y), other, half), *sem(5), sib),
            _remote(rows(diag, other, quarter), rows(diag, other, quarter), *sem(6), sib),
            _remote(rows(diag, other + quarter, quarter), rows(diag, other + quarter, quarter), *sem(7), sib),
        ]
        return first, arrivals, from_sibling

    def start(ins, outs, ssem, rsem):
        x, y, c, _ = _place()
        for w in range(n):
            half = outs[w].shape[1] // 2
            mine = outs[w].at[2 * x + y, pl.ds(c * half, half)]
            _remote(mine, mine, ssem.at[per * w], rsem.at[per * w], (1 - x, y, c)).start()
            _remote(mine, mine, ssem.at[per * w + 1], rsem.at[per * w + 1], (x, 1 - y, c)).start()

    def finish(ins, outs, ssem, rsem):
        plans = [plan(outs, ssem, rsem, w) for w in range(n)]
        started = []
        for direct in (True, False):
            for first, arrivals, _ in plans:
                for arrived, onward in (arrivals[:2] if direct else arrivals[2:]):
                    arrived.wait_recv()
                    for cp in onward:
                        cp.start()
                    started += onward
        for first, _, from_sibling in plans:
            for cp in from_sibling:
                cp.wait_recv()
            started += first
        for cp in started:
            cp.wait_send()

    return Exchange(bufs, [_sds(b.shape, b.dtype) for b in bufs], {w: w for w in range(n)}, per * n, start, finish)


def _ex_gather_direct(bufs):
    n = len(bufs)

    def copies(outs, ssem, rsem, only_first=False):
        x, y, c, chips = _place()
        me, sib = 2 * x + y, (x, y, 1 - c)
        first, relay, last = [], [], []
        for w in range(n):
            half = outs[w].shape[1] // 2
            mine = outs[w].at[me, pl.ds(c * half, half)]
            for k, (px, py) in enumerate(chips):
                sems = (ssem.at[6 * w + k], rsem.at[6 * w + k])
                sib_sems = (ssem.at[6 * w + 3 + k], rsem.at[6 * w + 3 + k])
                first.append(_remote(mine, mine, *sems, (px, py, c)))
                if only_first:
                    continue
                got = outs[w].at[2 * px + py, pl.ds(c * half, half)]
                relay.append((_remote(got, got, *sems, (px, py, c)), _remote(got, got, *sib_sems, sib)))
                theirs = outs[w].at[2 * px + py, pl.ds((1 - c) * half, half)]
                last.append(_remote(theirs, theirs, *sib_sems, sib))
        return first, relay, last

    def start(ins, outs, ssem, rsem):
        for cp in copies(outs, ssem, rsem, only_first=True)[0]:
            cp.start()

    def finish(ins, outs, ssem, rsem):
        first, relay, last = copies(outs, ssem, rsem)
        for arrived, onward in relay:
            arrived.wait_recv()
            onward.start()
        for cp in last:
            cp.wait_recv()
        for cp in first:
            cp.wait_send()
        for _, onward in relay:
            onward.wait_send()

    return Exchange(bufs, [_sds(b.shape, b.dtype) for b in bufs], {w: w for w in range(n)}, 6 * n, start, finish)


def _simple_exchange(arrays, landing, aliases, make_copies):
    def start(ins, outs, ssem, rsem):
        for cp, _ in make_copies(ins, outs, ssem, rsem, False):
            cp.start()

    def finish(ins, outs, ssem, rsem):
        cps = make_copies(ins, outs, ssem, rsem, True)
        for _, landed in cps:
            landed.wait_recv()
        for cp, _ in cps:
            cp.wait_send()

    return Exchange(arrays, landing, aliases, len(arrays) * 3, start, finish)


def _ex_pair_swap(grads):
    def make(ins, outs, ssem, rsem, landing):
        x, y, c, _ = _place()
        cps = [_remote(ins[w].at[:, 1 - c], outs[w], ssem.at[w], rsem.at[w], (x, y, 1 - c))
               for w in range(len(grads))]
        return [(cp, cp) for cp in cps]

    return _simple_exchange(grads, [_sds((NSH,) + g.shape[2:], g.dtype) for g in grads], {}, make)


def _ex_scatter(parts):
    def make(ins, outs, ssem, rsem, landing):
        x, y, c, chips = _place()
        out = []
        for w in range(len(parts)):
            for k, (px, py) in enumerate(chips):
                sems = (ssem.at[3 * w + k], rsem.at[3 * w + k])
                out.append((_remote(ins[w].at[2 * px + py], outs[w].at[k], *sems, (px, py, c)),
                            _remote(outs[w].at[k], outs[w].at[k], *sems, (px, py, c)) if landing else None))
        return out

    return _simple_exchange(parts, [_sds((3,) + p.shape[1:], p.dtype) for p in parts], {}, make)


def _ex_relay(bufs):
    def make(ins, outs, ssem, rsem, landing):
        x, y, c, chips = _place()
        sib = (x, y, 1 - c)
        out = []
        for w in range(len(bufs)):
            half = outs[w].shape[1] // 2
            for k, (px, py) in enumerate(chips):
                sems = (ssem.at[3 * w + k], rsem.at[3 * w + k])
                have = outs[w].at[2 * px + py, pl.ds(c * half, half)]
                miss = outs[w].at[2 * px + py, pl.ds((1 - c) * half, half)]
                out.append((_remote(have, have, *sems, sib), _remote(miss, miss, *sems, sib) if landing else None))
        return out

    return _simple_exchange(bufs, [_sds(b.shape, b.dtype) for b in bufs], {w: w for w in range(len(bufs))}, make)


def _ex_share(bufs):
    def make(ins, outs, ssem, rsem, landing):
        x, y, c, _ = _place()
        sib = (x, y, 1 - c)
        return [(_remote(outs[w].at[c], outs[w].at[c], ssem.at[w], rsem.at[w], sib),
                 _remote(outs[w].at[1 - c], outs[w].at[1 - c], ssem.at[w], rsem.at[w], sib) if landing else None)
                for w in range(len(bufs))]

    return _simple_exchange(bufs, [_sds(b.shape, b.dtype) for b in bufs], {w: w for w in range(len(bufs))}, make)


def _small_copies(slots, ssems, rsems, sending):
    x, y, c, _ = _place()
    out = []
    for m in range(1, 8):
        px, py, pc = x ^ (m >> 2), y ^ ((m >> 1) & 1), c ^ (m & 1)
        slot = slots.at[4 * x + 2 * y + c if sending else 4 * px + 2 * py + pc]
        out.append(_remote(slot, slot, ssems[m - 1], rsems[m - 1], (px, py, pc)))
    return out


def _small_gather_start(slots, name):
    def body(*refs):
        for cp in _small_copies(refs[0], refs[1:8], refs[8:15], True):
            cp.start()
        refs[-1][...] = jnp.zeros_like(refs[-1])

    outs = pl.pallas_call(
        body, name=name,
        out_shape=([pltpu.SemaphoreType.DMA(())] * 14 + [pltpu.HBM(slots.shape, slots.dtype)]
                   + [jax.ShapeDtypeStruct((8, 128), F32)]),
        in_specs=[_HBM], out_specs=[_SEM] * 14 + [_HBM, _VM], input_output_aliases={0: 14},
        compiler_params=pltpu.CompilerParams(has_side_effects=_EFFECT),
    )(*_in_hbm([slots]))
    return outs[:14], outs[14], outs[15]


def _small_gather_wait(sems, slots, after, name):
    def body(*refs):
        for cp in _small_copies(refs[0], refs[1:8], refs[8:15], True):
            cp.wait_send()
        for cp in _small_copies(refs[0], refs[1:8], refs[8:15], False):
            cp.wait_recv()

    return pl.pallas_call(
        body, name=name, out_shape=pltpu.HBM(slots.shape, slots.dtype),
        in_specs=[_HBM] + [_SEM] * 14 + [_ANY] * len(after), out_specs=_HBM, input_output_aliases={0: 0},
        compiler_params=pltpu.CompilerParams(has_side_effects=_EFFECT),
    )(slots, *sems, *after)


def _row_block(rows, cap=256):
    return max(t for t in range(16, cap + 1, 16) if rows % t == 0)


def _pair_sum(grad, got, c_idx, name):
    _, _, half, cols = grad.shape
    tr = _row_block(half, 512)

    def body(c_ref, a_ref, b_ref, o_ref):
        o_ref[...] = (a_ref[...].astype(F32) + b_ref[...].astype(F32)).astype(BF16)

    return pl.pallas_call(
        body, name=name, out_shape=_sds((NSH, half, cols), BF16),
        grid_spec=pltpu.PrefetchScalarGridSpec(
            num_scalar_prefetch=1, grid=(NSH, half // tr),
            in_specs=[pl.BlockSpec((None, None, tr, cols), lambda j, r, c: (j, c[0], r, 0)),
                      pl.BlockSpec((None, tr, cols), lambda j, r, c: (j, r, 0))],
            out_specs=pl.BlockSpec((None, tr, cols), lambda j, r, c: (j, r, 0))),
        compiler_params=_params(("arbitrary", "arbitrary"), 32),
    )(c_idx, *_in_hbm([grad, got]))


def _chip_sum(own, got, place, name):
    _, half, cols = own.shape
    tr = _row_block(half, 512)

    def body(place_ref, own_ref, got_ref, o_ref):
        acc = own_ref[...].astype(F32)
        for k in range(3):
            acc = acc + got_ref[k].astype(F32)
        o_ref[...] = acc

    return pl.pallas_call(
        body, name=name, out_shape=_sds((2, half, cols), F32),
        grid_spec=pltpu.PrefetchScalarGridSpec(
            num_scalar_prefetch=1, grid=(half // tr,),
            in_specs=[pl.BlockSpec((None, tr, cols), lambda r, p: (p[0], r, 0)),
                      pl.BlockSpec((3, tr, cols), lambda r, p: (0, r, 0))],
            out_specs=pl.BlockSpec((None, tr, cols), lambda r, p: (p[1], r, 0))),
        compiler_params=_params(("arbitrary",), 32),
    )(place, *_in_hbm([own, got]))


def _adamw_math(w, g, m, v):
    m = B1 * m + (1.0 - B1) * g
    v = B2 * v + (1.0 - B2) * (g * g)
    m_hat = m / (1.0 - B1 ** STEP)
    v_hat = v / (1.0 - B2 ** STEP)
    return -LR * (m_hat / (jnp.sqrt(v_hat) + AEPS) + WD * w), m, v


def _adamw(w, g, m, v, name, after=()):
    rows, cols = w.shape
    tr = _row_block(rows)

    def body(w_ref, g_ref, m_ref, v_ref, go_ref, d_ref, nm_ref, nv_ref):
        g = g_ref[...]
        go_ref[...] = g
        d_ref[...], nm_ref[...], nv_ref[...] = _adamw_math(w_ref[...], g, m_ref[...], v_ref[...])

    blk = pl.BlockSpec((tr, cols), lambda r: (r, 0))
    return _call(
        body, (w, g, m, v), name=name, grid=(rows // tr,), out_shape=[_sds(w.shape, F32)] * 4,
        in_specs=[blk] * 4, out_specs=[blk] * 4,
        compiler_params=_params(("arbitrary",), 32), free=(0, 2, 3), after=after)


def _small_update(gathered, w, m, v):
    rows = w.shape[0]

    def body(ga_ref, w_ref, m_ref, v_ref, *out_refs):
        g = ga_ref[0:rows, :]
        for dev in range(1, 8):
            g = g + ga_ref[dev * rows:(dev + 1) * rows, :]
        results = (g,) + _adamw_math(w_ref[...], g, m_ref[...], v_ref[...])
        for i, res in enumerate(results):
            out_refs[i][...] = res[:SMALL_HEAD, :]
            out_refs[4 + i][...] = res[SMALL_HEAD:, :]

    outs = pl.pallas_call(
        body, name="small_update",
        out_shape=[jax.ShapeDtypeStruct((SMALL_HEAD, 128), F32)] * 4
        + [jax.ShapeDtypeStruct((rows - SMALL_HEAD, 128), F32)] * 4,
        in_specs=[_VM] * 4, out_specs=[_VM] * 8,
    )(gathered, w, m, v)
    return outs[:4], outs[4:]


SMALL = ("ffn1_norm", "mix_norm", "ffn2_norm", "final_norm", "pool_scale", "loss", "pool_w_group")
SMALL_HEAD = 48
BIG = ("ffn1_w_gate_up", "ffn1_w_down", "w_in", "w_branch_pool", "w_branch_attn", "w_out",
       "ffn2_w_gate_up", "ffn2_w_down")
ORDER = ("ffn1_norm", "ffn1_w_gate_up", "ffn1_w_down", "mix_norm", "w_in", "pool_w_group", "pool_scale",
         "w_branch_pool", "w_branch_attn", "w_out", "ffn2_norm", "ffn2_w_gate_up", "ffn2_w_down", "final_norm")
SMALL_ROWS = 560


def _pack_small(t):
    parts = []
    for k in SMALL:
        rows = t[k].reshape(-1, 128) if k in t else jnp.zeros((1, 128), F32)
        parts.append(jnp.pad(rows, ((0, -rows.shape[0] % 8), (0, 0))))
    packed = jnp.concatenate(parts, axis=0)
    assert packed.shape == (SMALL_ROWS, 128), packed.shape
    return packed


def _unpack_small(head, group, like):
    out, at = {"pool_w_group": group.reshape(like["pool_w_group"].shape)}, 0
    for k in SMALL[:-1]:
        n = like[k].size // 128 if k in like else 1
        out[k] = head[at:at + n].reshape(like[k].shape) if k in like else head[at, 0]
        at += n + (-n % 8)
    return out


def _halves(g):
    return g.reshape(NSH, 2, g.shape[1] // 2, g.shape[2])


def kernel(x, ffn1_norm, ffn1_w_gate_up, ffn1_w_down, mix_norm, w_in, pool_w_group, pool_scale, w_branch_pool, w_branch_attn, w_out, ffn2_norm, ffn2_w_gate_up, ffn2_w_down, final_norm, loss_target, m_ffn1_norm, m_ffn1_w_gate_up, m_ffn1_w_down, m_mix_norm, m_w_in, m_pool_w_group, m_pool_scale, m_w_branch_pool, m_w_branch_attn, m_w_out, m_ffn2_norm, m_ffn2_w_gate_up, m_ffn2_w_down, m_final_norm, v_ffn1_norm, v_ffn1_w_gate_up, v_ffn1_w_down, v_mix_norm, v_w_in, v_pool_w_group, v_pool_scale, v_w_branch_pool, v_w_branch_attn, v_w_out, v_ffn2_norm, v_ffn2_w_gate_up, v_ffn2_w_down, v_final_norm):
    wts = dict(ffn1_norm=ffn1_norm, ffn1_w_gate_up=ffn1_w_gate_up, ffn1_w_down=ffn1_w_down, mix_norm=mix_norm,
               w_in=w_in, pool_w_group=pool_w_group, pool_scale=pool_scale, w_branch_pool=w_branch_pool,
               w_branch_attn=w_branch_attn, w_out=w_out, ffn2_norm=ffn2_norm, ffn2_w_gate_up=ffn2_w_gate_up,
               ffn2_w_down=ffn2_w_down, final_norm=final_norm)
    mom = dict(ffn1_norm=m_ffn1_norm, ffn1_w_gate_up=m_ffn1_w_gate_up, ffn1_w_down=m_ffn1_w_down,
               mix_norm=m_mix_norm, w_in=m_w_in, pool_w_group=m_pool_w_group, pool_scale=m_pool_scale,
               w_branch_pool=m_w_branch_pool, w_branch_attn=m_w_branch_attn, w_out=m_w_out,
               ffn2_norm=m_ffn2_norm, ffn2_w_gate_up=m_ffn2_w_gate_up, ffn2_w_down=m_ffn2_w_down,
               final_norm=m_final_norm)
    var = dict(ffn1_norm=v_ffn1_norm, ffn1_w_gate_up=v_ffn1_w_gate_up, ffn1_w_down=v_ffn1_w_down,
               mix_norm=v_mix_norm, w_in=v_w_in, pool_w_group=v_pool_w_group, pool_scale=v_pool_scale,
               w_branch_pool=v_w_branch_pool, w_branch_attn=v_w_branch_attn, w_out=v_w_out,
               ffn2_norm=v_ffn2_norm, ffn2_w_gate_up=v_ffn2_w_gate_up, ffn2_w_down=v_ffn2_w_down,
               final_norm=v_final_norm)

    c_idx = lax.axis_index("c").astype(jnp.int32).reshape(1)
    me_idx = (2 * lax.axis_index("x") + lax.axis_index("y")).astype(jnp.int32).reshape(1)
    place = jnp.concatenate([me_idx, c_idx])
    x0, tgt = x[0], loss_target[0]
    wgrp = pool_w_group[0].astype(BF16)
    g1, gm, g2, gf = ffn1_norm, mix_norm, ffn2_norm, final_norm.reshape(1, D)
    grad, delta, new_m, new_v = {}, {}, {}, {}

    def pair_sums(keys, parts, got):
        return [_pair_sum(parts[i], got[i], c_idx, "pair_sum_" + k) for i, k in enumerate(keys)]

    def chip_sums(keys, chip_parts, owned):
        return [_chip_sum(chip_parts[i], owned[i], place, "chip_sum_" + k) for i, k in enumerate(keys)]

    def adamw(k, after=()):
        outs = _adamw(wts[k][0], grad[k][0], mom[k][0], var[k][0], "adamw_" + k, after=after)
        grad[k], delta[k], new_m[k], new_v[k] = (o.reshape(wts[k].shape) for o in outs)

    own = {k: _cast_into_block(wts[k][0], me_idx, "cast_" + k) for k in BIG}
    first, late = ("ffn1_w_gate_up", "ffn1_w_down"), ("w_branch_pool", "w_branch_attn", "w_out",
                                                       "ffn2_w_gate_up", "ffn2_w_down")
    full = dict(zip(first, _exchange_alone(_ex_gather([own[k] for k in first]), "gather_ffn1")))
    wgu1, wd1 = full["ffn1_w_gate_up"], full["ffn1_w_down"].reshape(DFF, D)
    (h1, n1, gu1, a1), (win,) = _ffn_fwd(x0, g1, wgu1, wd1, "ffn1_fwd", exchange=_ex_gather_direct([own["w_in"]]))
    sems_l, thru_l, token_l = _gather_start([own[k_] for k_ in late], [h1], "gather_late_start")
    u, xp, q, k, v, gp, gs = _mix_in(h1, gm, win, after=(token_l,))
    o_sb, ctot = _attn_fwd(q, k, v)
    arrived = _gather_wait(sems_l, thru_l, [o_sb], "gather_late_wait")
    wbp, wba, wout = _exchange_alone(_ex_relay(arrived[:3]), "relay_mix")
    wout = wout.reshape(D, D)
    (h2, pm, p, yp, ys, mm), (wgu2, wd2) = _mix_out(h1, xp, o_sb, gp, gs, wgrp, pool_scale, wbp, wba, wout,
                                                    exchange=_ex_relay(arrived[3:]))
    wd2 = wd2.reshape(DFF, D)
    dh3, loss_row, d_gf, n3, gu3, a3 = _ffn_fwd(h2, g2, wgu2, wd2, "ffn2_fwd", head=(tgt, gf))

    def grad_gate_up(n, dgu, name, exchange=None):
        res = _wgrad(n, dgu, NSH, D, name, exchange=exchange)
        return [_halves(res)] if exchange is None else ([_halves(res[0])], res[1])

    def grad_down(a, dh, name, exchange=None):
        res = _wgrad(a, dh, 1, FFS, name, exchange=exchange)
        halves = lambda g: [_halves(g.reshape(NSH, DFF // NSH, D))]
        return halves(res) if exchange is None else (halves(res[0]), res[1])

    k_gu2, k_d2, k_gu1, k_d1, k_in = (("ffn2_w_gate_up",), ("ffn2_w_down",), ("ffn1_w_gate_up",),
                                      ("ffn1_w_down",), ("w_in",))
    dh2, dgu3, d_g2 = _ffn_bwd(dh3, h2, g2, gu3, wgu2, wd2, "ffn2_bwd")
    pa = grad_gate_up(n3, dgu3, "wgrad_gu2") + grad_down(a3, dh3, "wgrad_d2")
    (dlg, dyp, dys, do_sb, dyg, dxp, d_scale), got_a = _mix_bwd_out(
        dh2, gp, gs, yp, ys, pm, wgrp, pool_scale, wbp, wba, wout, exchange=_ex_pair_swap(pa))
    chip_a = pair_sums(k_gu2 + k_d2, pa, got_a)
    kb = ("w_out", "w_branch_pool", "w_branch_attn")
    pb = [_halves(_wgrad(mm, dh2, 1, D, "wgrad_out").reshape(NSH, D // NSH, D)),
          _halves(_wgrad(p, dyp, NSH, PW, "wgrad_bp")), _halves(_wgrad(o_sb, dys, NSH, SBW, "wgrad_ba"))]
    k_a, k_in = k_gu2 + k_d2, k_in + kb
    sems_a, thru_a, token_a = _scatter_start(chip_a, "scatter_a_start")
    dq, dk, dv = _attn_bwd(q, k, v, do_sb, ctot, after=(token_a,))
    chip_a, owned_a = _scatter_wait(sems_a, thru_a, [dq], "scatter_a_wait")
    halves_a = chip_sums(k_a, chip_a, owned_a)
    (dh1, d_gm, dproj), both_a = _mix_bwd_in(dh2, h1, gm, (dxp, dq, dk, dv, dlg), win, exchange=_ex_share(halves_a))
    for i, k_ in enumerate(k_a):
        grad[k_] = both_a[i].reshape(wts[k_].shape)

    p_in = [_halves(_wgrad(u, dproj, NSH, D, "wgrad_in"))] + pb
    p_d1, got_in = grad_down(a1, dh1, "wgrad_d1", exchange=_ex_pair_swap(p_in))
    sems_in, thru_in, token_in = _scatter_start(pair_sums(k_in, p_in, got_in), "scatter_in_start")
    dgu1, got_d1 = _ffn_bwd_act(dh1, gu1, wd1, "ffn1_bwd_act", exchange=_ex_pair_swap(p_d1), after=(token_in,))
    sems_d1, thru_d1, token_d1 = _scatter_start(pair_sums(k_d1, p_d1, got_d1), "scatter_d1_start")
    p_gu1 = [_halves(_wgrad(n1, dgu1, NSH, D, "wgrad_gu1", after=(token_in, token_d1)))]
    chip_in, owned_in = _scatter_wait(sems_in, thru_in, p_gu1, "scatter_in_wait")
    chip_d1, owned_d1 = _scatter_wait(sems_d1, thru_d1, p_gu1, "scatter_d1_wait")
    halves_in, halves_d1 = chip_sums(k_in, chip_in, owned_in), chip_sums(k_d1, chip_d1, owned_d1)
    landed = _exchange_alone(_join(_ex_pair_swap(p_gu1), _ex_share(halves_in)), "pair_swap_gu1")
    for i, k_ in enumerate(k_in):
        grad[k_] = landed[1 + i].reshape(wts[k_].shape)
    sems, thru, token = _scatter_start(pair_sums(k_gu1, p_gu1, landed[:1]), "scatter_gu1_start")
    for k_ in k_a + k_in:
        adamw(k_, after=(token,))
    dx, d_g1 = _ffn_bwd_in(dh1, x0, g1, dgu1, wgu1, "ffn1_bwd_in", after=(token,))
    small_g = dict(ffn1_norm=d_g1, mix_norm=d_gm, ffn2_norm=d_g2, final_norm=d_gf, pool_scale=d_scale,
                   pool_w_group=_wgrad_groups(pm, dyg), loss=loss_row)
    dev = 4 * lax.axis_index("x") + 2 * lax.axis_index("y") + lax.axis_index("c")
    slots = lax.dynamic_update_slice(jnp.zeros((8, SMALL_ROWS, 128), F32), _pack_small(small_g)[None], (dev, 0, 0))
    sems_s, slots, token_s = _small_gather_start(slots, "small_gather_start")

    chip_gu1, owned_gu1 = _scatter_wait(sems, thru, [dx] + [delta[k_] for k_ in k_a + k_in], "scatter_gu1_wait")
    both = _exchange_alone(_ex_share(halves_d1 + chip_sums(k_gu1, chip_gu1, owned_gu1)), "share_last",
                           after=(token_s,))
    grad["ffn1_w_down"] = both[0].reshape(ffn1_w_down.shape)
    grad["ffn1_w_gate_up"] = both[1].reshape(ffn1_w_gate_up.shape)
    for k_ in k_d1 + k_gu1:
        adamw(k_, after=(token_s,))
    gathered = _small_gather_wait(sems_s, slots, [delta[k_] for k_ in k_d1 + k_gu1], "small_gather_wait")
    gathered = gathered.reshape(8 * SMALL_ROWS, 128)
    heads, groups = _small_update(gathered, _pack_small(wts), _pack_small(mom), _pack_small(var))
    for dst, head, group in zip((grad, delta, new_m, new_v), heads, groups):
        vals = _unpack_small(head, group, wts)
        if dst is grad:
            loss = vals["loss"]
        vals.pop("loss")
        dst.update(vals)
    return (loss, dx[None], *[grad[k_] for k_ in ORDER], *[delta[k_] for k_ in ORDER],
            *[new_m[k_] for k_ in ORDER], *[new_v[k_] for k_ in ORDER])
```

```python
import functools

import jax
import jax.numpy as jnp
from jax import lax
from jax.experimental import pallas as pl
from jax.experimental.pallas import tpu as pltpu

F32 = jnp.float32
BF16 = jnp.bfloat16

S = 2048
D = 1024
DFF = 2816
FFS = 2 * DFF // 4
NSH = 4
PW = 512
PG = 128
POOL_WINDOWS = (2, 4, 8, 16)
HALO = 16
SBW = 512
DH = 64
EPS = 1e-6
SCALE = 0.125
LOG2E = 1.4426950408889634
TA = 256
QB = 2
MIB = 1024 * 1024

LR, B1, B2, AEPS, WD, STEP = 0.001, 0.9, 0.999, 1e-08, 0.01, 10

_VM = pl.BlockSpec(memory_space=pltpu.VMEM)
_ANY = pl.BlockSpec(memory_space=pl.ANY)
MESH = pl.DeviceIdType.MESH


def _nn(a, b):
    return jnp.dot(a, b, preferred_element_type=F32)


def _nt(a, b):
    return lax.dot_general(a, b, (((1,), (1,)), ((), ())), preferred_element_type=F32)


def _tn(a, b):
    return lax.dot_general(a, b, (((0,), (0,)), ((), ())), preferred_element_type=F32)


def _params(sem, vmem_mib):
    return pltpu.CompilerParams(dimension_semantics=sem, vmem_limit_bytes=vmem_mib * MIB)


def _rows(tm, width):
    return pl.BlockSpec((tm, width), lambda i: (i, 0))


def _fixed(shape):
    return pl.BlockSpec(shape, lambda *_: (0,) * len(shape))


def _sds(shape, dtype):
    return pltpu.HBM(shape, dtype)


def _in_hbm(args):
    return [pltpu.with_memory_space_constraint(a, pltpu.HBM) for a in args]


def _stage(pairs):
    @pl.when(pl.program_id(0) == 0)
    def _():
        for src, dst in pairs:
            pltpu.sync_copy(src, dst)


def _vmem_like(*arrays):
    return [pltpu.VMEM(a.shape, a.dtype) for a in arrays]


class Exchange:
    def __init__(self, arrays, landing, aliases, n_sems, start, finish):
        self.arrays, self.landing, self.aliases, self.n_sems = list(arrays), list(landing), dict(aliases), n_sems
        self.start, self.finish = start, finish


def _join(a, b):
    na, la = len(a.arrays), len(a.landing)

    def both(fa, fb):
        def run(ins, outs, ssem, rsem):
            fa(ins[:na], outs[:la], ssem.at[pl.ds(0, a.n_sems)], rsem.at[pl.ds(0, a.n_sems)])
            fb(ins[na:], outs[la:], ssem.at[pl.ds(a.n_sems, b.n_sems)], rsem.at[pl.ds(a.n_sems, b.n_sems)])
        return run

    aliases = {**a.aliases, **{na + i: la + j for i, j in b.aliases.items()}}
    return Exchange(a.arrays + b.arrays, a.landing + b.landing, aliases, a.n_sems + b.n_sems,
                    both(a.start, b.start), both(a.finish, b.finish))


def _call(body, args, *, name, grid, in_specs, out_specs, out_shape, scratch_shapes=(), compiler_params=None,
          exchange=None, free=(), after=()):
    args = [a if i in free else pltpu.with_memory_space_constraint(a, pltpu.HBM) for i, a in enumerate(args)]
    if exchange is None:
        n_in = len(in_specs)

        def plain(*refs):
            body(*refs[:n_in], *refs[n_in + len(after):])

        return pl.pallas_call(plain, name=name, grid=grid, in_specs=list(in_specs) + [_ANY] * len(after),
                              out_specs=out_specs, out_shape=out_shape, scratch_shapes=list(scratch_shapes),
                              compiler_params=compiler_params)(*args, *after)
    ex = exchange
    n_in, n_out, n_scr = len(in_specs), len(out_specs), len(scratch_shapes)
    na, nl = len(ex.arrays), len(ex.landing)

    def hosted(*refs):
        at = [0]

        def take(n):
            at[0] += n
            return refs[at[0] - n:at[0]]

        k_in, _, e_in, k_out, e_out, k_scr = take(n_in), take(len(after)), take(na), take(n_out), take(nl), take(n_scr)
        ssem, rsem = take(2)
        ids = [pl.program_id(a) for a in range(len(grid))]
        first = functools.reduce(jnp.logical_and, [i == 0 for i in ids])
        last = functools.reduce(jnp.logical_and, [i == g - 1 for i, g in zip(ids, grid)])

        @pl.when(first)
        def _():
            ex.start(e_in, e_out, ssem, rsem)

        body(*k_in, *k_out, *k_scr)

        @pl.when(last)
        def _():
            ex.finish(e_in, e_out, ssem, rsem)

    outs = pl.pallas_call(
        hosted, name=name, grid=grid,
        in_specs=list(in_specs) + [_ANY] * (len(after) + na), out_specs=list(out_specs) + [_ANY] * nl,
        out_shape=list(out_shape) + ex.landing,
        scratch_shapes=list(scratch_shapes) + [pltpu.SemaphoreType.DMA((ex.n_sems,))] * 2,
        input_output_aliases={n_in + len(after) + i: n_out + j for i, j in ex.aliases.items()},
        compiler_params=compiler_params,
    )(*args, *after, *_in_hbm(ex.arrays))
    return outs[:n_out], outs[n_out:]


def _exchange_alone(ex, name, after=()):
    na, nl = len(ex.arrays), len(ex.landing)

    def body(*refs):
        outs = refs[na + len(after):na + len(after) + nl]
        ex.start(refs[:na], outs, refs[-2], refs[-1])
        ex.finish(refs[:na], outs, refs[-2], refs[-1])

    return pl.pallas_call(
        body, name=name, in_specs=[_ANY] * (na + len(after)), out_specs=[_ANY] * nl,
        out_shape=ex.landing, scratch_shapes=[pltpu.SemaphoreType.DMA((ex.n_sems,))] * 2,
        input_output_aliases=ex.aliases,
    )(*_in_hbm(ex.arrays), *after)


_HBM = pl.BlockSpec(memory_space=pltpu.HBM)
_SEM = pl.BlockSpec(memory_space=pltpu.SEMAPHORE)
_EFFECT = pltpu.SideEffectType.DATAFLOW_SIDE_EFFECTING


def _scatter_copies(srcs, lands, ssems, rsems):
    x, y, c, chips = _place()
    return [_remote(srcs[w].at[2 * px + py], lands[w].at[k], ssems[3 * w + k], rsems[3 * w + k], (px, py, c))
            for w in range(len(srcs)) for k, (px, py) in enumerate(chips)]


def _scatter_start(parts, name):
    n, ncp = len(parts), 3 * len(parts)
    lands = [lax.empty((3,) + p.shape[1:], p.dtype) for p in parts]

    def body(*refs):
        srcs, land_refs = refs[:n], refs[n:2 * n]
        ssems, rsems = refs[2 * n:2 * n + ncp], refs[2 * n + ncp:2 * n + 2 * ncp]
        for cp in _scatter_copies(srcs, land_refs, ssems, rsems):
            cp.start()
        token = refs[-1]
        token[...] = jnp.zeros_like(token)

    outs = pl.pallas_call(
        body, name=name,
        out_shape=([pltpu.SemaphoreType.DMA(())] * (2 * ncp) + [pltpu.HBM(a.shape, a.dtype) for a in parts + lands]
                   + [jax.ShapeDtypeStruct((8, 128), F32)]),
        in_specs=[_HBM] * (2 * n), out_specs=[_SEM] * (2 * ncp) + [_HBM] * (2 * n) + [_VM],
        input_output_aliases={i: 2 * ncp + i for i in range(2 * n)},
        compiler_params=pltpu.CompilerParams(has_side_effects=_EFFECT),
    )(*_in_hbm(parts), *_in_hbm(lands))
    sems, thru, token = outs[:2 * ncp], outs[2 * ncp:2 * ncp + 2 * n], outs[-1]
    return sems, thru, token


def _scatter_wait(sems, thru, after, name):
    n = len(thru) // 2
    ncp = 3 * n

    def body(*refs):
        srcs, land_refs = refs[:n], refs[n:2 * n]
        ssems, rsems = refs[2 * n:2 * n + ncp], refs[2 * n + ncp:2 * n + 2 * ncp]
        for cp in _scatter_copies(srcs, land_refs, ssems, rsems):
            cp.wait_send()
            cp.wait_recv()

    outs = pl.pallas_call(
        body, name=name, out_shape=[pltpu.HBM(a.shape, a.dtype) for a in thru],
        in_specs=[_HBM] * (2 * n) + [_SEM] * (2 * ncp) + [_ANY] * len(after), out_specs=[_HBM] * (2 * n),
        input_output_aliases={i: i for i in range(2 * n)},
        compiler_params=pltpu.CompilerParams(has_side_effects=_EFFECT),
    )(*thru, *sems, *after)
    return outs[:n], outs[n:]


def _gather_copies(bufs, ssems, rsems, sending):
    x, y, c, chips = _place()
    out = []
    for w, ref in enumerate(bufs):
        half = ref.shape[1] // 2
        for k, (px, py) in enumerate(chips):
            rows = ref.at[2 * x + y if sending else 2 * px + py, pl.ds(c * half, half)]
            out.append(_remote(rows, rows, ssems[3 * w + k], rsems[3 * w + k], (px, py, c)))
    return out


def _gather_start(bufs, after, name):
    n, ncp = len(bufs), 3 * len(bufs)

    def body(*refs):
        ssems, rsems = refs[n + len(after):n + len(after) + ncp], refs[n + len(after) + ncp:n + len(after) + 2 * ncp]
        for cp in _gather_copies(refs[:n], ssems, rsems, True):
            cp.start()
        token = refs[-1]
        token[...] = jnp.zeros_like(token)

    outs = pl.pallas_call(
        body, name=name,
        out_shape=([pltpu.SemaphoreType.DMA(())] * (2 * ncp) + [pltpu.HBM(a.shape, a.dtype) for a in bufs]
                   + [jax.ShapeDtypeStruct((8, 128), F32)]),
        in_specs=[_HBM] * n + [_ANY] * len(after), out_specs=[_SEM] * (2 * ncp) + [_HBM] * n + [_VM],
        input_output_aliases={i: 2 * ncp + i for i in range(n)},
        compiler_params=pltpu.CompilerParams(has_side_effects=_EFFECT),
    )(*_in_hbm(bufs), *after)
    return outs[:2 * ncp], outs[2 * ncp:2 * ncp + n], outs[-1]


def _gather_wait(sems, thru, after, name):
    n = len(thru)
    ncp = 3 * n

    def body(*refs):
        ssems, rsems = refs[n:n + ncp], refs[n + ncp:n + 2 * ncp]
        for cp in _gather_copies(refs[:n], ssems, rsems, True):
            cp.wait_send()
        for cp in _gather_copies(refs[:n], ssems, rsems, False):
            cp.wait_recv()

    return pl.pallas_call(
        body, name=name, out_shape=[pltpu.HBM(a.shape, a.dtype) for a in thru],
        in_specs=[_HBM] * n + [_SEM] * (2 * ncp) + [_ANY] * len(after), out_specs=[_HBM] * n,
        input_output_aliases={i: i for i in range(n)},
        compiler_params=pltpu.CompilerParams(has_side_effects=_EFFECT),
    )(*thru, *sems, *after)


def _rms(x):
    r = lax.rsqrt(jnp.mean(x * x, axis=-1, keepdims=True) + EPS)
    return r, x * r


def _rms_bwd(dn, xr, r, gain):
    dng = dn * gain
    dx = r * (dng - xr * jnp.mean(dng * xr, axis=-1, keepdims=True))
    return dx, jnp.sum(dn * xr, axis=0, keepdims=True)


def _ffn_fwd(x, gain, wgu, wd, name, exchange=None, head=None):
    tm = 256

    def body(x_ref, g_ref, wgu_hbm, wd_hbm, *rest):
        if head is None:
            h_ref, n_ref, gu_ref, a_ref, wgu_ref, wd_ref = rest
        else:
            t_ref, gf_ref, h_ref, loss_ref, dgf_ref, n_ref, gu_ref, a_ref, wgu_ref, wd_ref = rest
        _stage([(wgu_hbm, wgu_ref), (wd_hbm, wd_ref)])
        x = x_ref[...]
        _, xr = _rms(x)
        n = (xr * g_ref[...]).astype(BF16)
        n_ref[...] = n
        acc = jnp.zeros((tm, D), F32)
        for j in range(2):
            g = _nn(n, wgu_ref[j])
            u = _nn(n, wgu_ref[2 + j])
            gu_ref[:, j * FFS:(j + 1) * FFS] = g.astype(BF16)
            gu_ref[:, (2 + j) * FFS:(3 + j) * FFS] = u.astype(BF16)
            half_act = (0.5 * (g * jax.nn.sigmoid(g) * u)).astype(BF16)
            a_ref[:, j * FFS:(j + 1) * FFS] = half_act
            acc = acc + _nn(half_act, wd_ref[j * FFS:(j + 1) * FFS, :])
        h = x + acc
        if head is None:
            h_ref[...] = h
            return
        gf = gf_ref[...]
        r, hr = _rms(h)
        err = hr * gf - t_ref[...]
        dh, dgain = _rms_bwd(err * (1.0 / D), hr, r, gf)
        h_ref[...] = dh

        @pl.when(pl.program_id(0) == 0)
        def _():
            dgf_ref[...] = jnp.zeros_like(dgf_ref)
            loss_ref[...] = jnp.zeros_like(loss_ref)

        dgf_ref[...] += dgain
        loss_ref[...] += jnp.full((1, 128), (0.5 / D) * jnp.sum(err * err), F32)

    saved_specs = [_rows(tm, D), _rows(tm, 4 * FFS), _rows(tm, DFF)]
    saved_shapes = [_sds((S, D), BF16), _sds((S, 4 * FFS), BF16), _sds((S, DFF), BF16)]
    if head is None:
        return _call(
            body, (x, gain, wgu, wd), name=name, grid=(S // tm,),
            in_specs=[_rows(tm, D), _fixed((1, D)), _ANY, _ANY],
            out_specs=[_rows(tm, D)] + saved_specs, out_shape=[_sds((S, D), F32)] + saved_shapes,
            scratch_shapes=_vmem_like(wgu, wd),
            compiler_params=_params(("arbitrary",), 56), exchange=exchange)
    return _call(
        body, (x, gain, wgu, wd, *head), name=name, grid=(S // tm,),
        in_specs=[_rows(tm, D), _fixed((1, D)), _ANY, _ANY, _rows(tm, D), _fixed((1, D))],
        out_specs=[_rows(tm, D), _fixed((1, 128)), _fixed((1, D))] + saved_specs,
        out_shape=[_sds((S, D), F32), _sds((1, 128), F32), _sds((1, D), F32)] + saved_shapes,
        scratch_shapes=_vmem_like(wgu, wd),
        compiler_params=_params(("arbitrary",), 56), exchange=exchange, free=(4, 5))


def _ffn_bwd(dh, x, gain, gu, wgu, wd, name):
    tm = 256

    def body(dh_ref, x_ref, g_ref, gu_ref, wgu_hbm, wd_hbm, dx_ref, dgu_ref, dg_ref, wgu_ref, wd_ref):
        _stage([(wgu_hbm, wgu_ref), (wd_hbm, wd_ref)])
        dh = dh_ref[...]
        dhb = dh.astype(BF16)
        dn = jnp.zeros((tm, D), F32)
        for j in range(2):
            g = gu_ref[:, j * FFS:(j + 1) * FFS].astype(F32)
            u = gu_ref[:, (2 + j) * FFS:(3 + j) * FFS].astype(F32)
            da = 0.5 * _nt(dhb, wd_ref[j * FFS:(j + 1) * FFS, :])
            sg = jax.nn.sigmoid(g)
            dgb = (da * u * (sg * (1.0 + g * (1.0 - sg)))).astype(BF16)
            dub = (da * (g * sg)).astype(BF16)
            dgu_ref[:, j * FFS:(j + 1) * FFS] = dgb
            dgu_ref[:, (2 + j) * FFS:(3 + j) * FFS] = dub
            dn = dn + _nt(dgb, wgu_ref[j]) + _nt(dub, wgu_ref[2 + j])
        r, xr = _rms(x_ref[...])
        dx, dgain = _rms_bwd(dn, xr, r, g_ref[...])
        dx_ref[...] = dh + dx

        @pl.when(pl.program_id(0) == 0)
        def _():
            dg_ref[...] = jnp.zeros_like(dg_ref)

        dg_ref[...] += dgain

    return _call(
        body, (dh, x, gain, gu, wgu, wd), name=name, grid=(S // tm,),
        in_specs=[_rows(tm, D), _rows(tm, D), _fixed((1, D)), _rows(tm, 4 * FFS), _ANY, _ANY],
        out_specs=[_rows(tm, D), _rows(tm, 4 * FFS), _fixed((1, D))],
        out_shape=[_sds((S, D), F32), _sds((S, 4 * FFS), BF16), _sds((1, D), F32)],
        scratch_shapes=_vmem_like(wgu, wd), compiler_params=_params(("arbitrary",), 56))


def _ffn_bwd_act(dh, gu, wd, name, exchange=None, after=()):
    tm = 512

    def body(dh_ref, gu_ref, wd_hbm, dgu_ref, wd_ref):
        _stage([(wd_hbm, wd_ref)])
        dhb = dh_ref[...].astype(BF16)
        for j in range(2):
            g = gu_ref[:, j * FFS:(j + 1) * FFS].astype(F32)
            u = gu_ref[:, (2 + j) * FFS:(3 + j) * FFS].astype(F32)
            da = 0.5 * _nt(dhb, wd_ref[j * FFS:(j + 1) * FFS, :])
            sg = jax.nn.sigmoid(g)
            dgu_ref[:, j * FFS:(j + 1) * FFS] = (da * u * (sg * (1.0 + g * (1.0 - sg)))).astype(BF16)
            dgu_ref[:, (2 + j) * FFS:(3 + j) * FFS] = (da * (g * sg)).astype(BF16)

    res = _call(
        body, (dh, gu, wd), name=name, grid=(S // tm,),
        in_specs=[_rows(tm, D), _rows(tm, 4 * FFS), _ANY], out_specs=[_rows(tm, 4 * FFS)],
        out_shape=[_sds((S, 4 * FFS), BF16)], scratch_shapes=_vmem_like(wd),
        compiler_params=_params(("arbitrary",), 56), exchange=exchange, after=after)
    return res[0] if exchange is None else (res[0][0], res[1])


def _ffn_bwd_in(dh, x, gain, dgu, wgu, name, exchange=None, after=()):
    tm = 512

    def body(dh_ref, x_ref, g_ref, dgu_ref, wgu_hbm, dx_ref, dg_ref, wgu_ref):
        _stage([(wgu_hbm, wgu_ref)])
        dn = jnp.zeros((tm, D), F32)
        for j in range(NSH):
            dn = dn + _nt(dgu_ref[:, j * FFS:(j + 1) * FFS], wgu_ref[j])
        r, xr = _rms(x_ref[...])
        dx, dgain = _rms_bwd(dn, xr, r, g_ref[...])
        dx_ref[...] = dh_ref[...] + dx

        @pl.when(pl.program_id(0) == 0)
        def _():
            dg_ref[...] = jnp.zeros_like(dg_ref)

        dg_ref[...] += dgain

    return _call(
        body, (dh, x, gain, dgu, wgu), name=name, grid=(S // tm,),
        in_specs=[_rows(tm, D), _rows(tm, D), _fixed((1, D)), _rows(tm, 4 * FFS), _ANY],
        out_specs=[_rows(tm, D), _fixed((1, D))],
        out_shape=[_sds((S, D), F32), _sds((1, D), F32)],
        scratch_shapes=_vmem_like(wgu),
        compiler_params=_params(("arbitrary",), 56), exchange=exchange, after=after)


def _mix_in(h, gain, w_in, after=()):
    tm = 512

    def body(h_ref, g_ref, w_hbm, u_ref, xp_ref, q_ref, k_ref, v_ref, gp_ref, gs_ref, w_ref):
        _stage([(w_hbm, w_ref)])
        _, hr = _rms(h_ref[...])
        u = (hr * g_ref[...]).astype(BF16)
        u_ref[...] = u
        p0 = _nn(u, w_ref[0])
        xp_ref[...] = p0[:, :PW]
        q_ref[...] = p0[:, PW:].astype(BF16)
        p1 = _nn(u, w_ref[1])
        k_ref[...] = p1[:, :SBW].astype(BF16)
        v_ref[...] = p1[:, SBW:].astype(BF16)
        gp_ref[...] = jax.nn.sigmoid(_nn(u, w_ref[2])).astype(BF16)
        gs_ref[...] = jax.nn.sigmoid(_nn(u, w_ref[3])).astype(BF16)

    return _call(
        body, (h, gain, w_in), name="mix_in", grid=(S // tm,),
        in_specs=[_rows(tm, D), _fixed((1, D)), _ANY],
        out_specs=[_rows(tm, D), _rows(tm, PW), _rows(tm, SBW), _rows(tm, SBW), _rows(tm, SBW),
                   _rows(tm, D), _rows(tm, D)],
        out_shape=[_sds((S, D), BF16), _sds((S, PW), F32), _sds((S, SBW), BF16), _sds((S, SBW), BF16),
                   _sds((S, SBW), BF16), _sds((S, D), BF16), _sds((S, D), BF16)],
        scratch_shapes=_vmem_like(w_in),
        compiler_params=_params(("arbitrary",), 48), free=(1,), after=after)


def _hilo_dot(x, tri):
    hi = x.astype(BF16)
    lo = (x - hi.astype(F32)).astype(BF16)
    return _nn(hi, tri) + _nn(lo, tri)


def _log_terms(qk):
    z2 = qk * (SCALE * LOG2E)
    lb = jnp.minimum(z2, 0.0) - jnp.log2(1.0 + jnp.exp2(-jnp.abs(z2)))
    return lb, lb - z2


def _head_masks():
    lane = lax.broadcasted_iota(jnp.int32, (1, 2 * DH), 1)
    return (lane < DH, lane >= DH)


def _attn_fwd(q, k, v, exchange=None):
    T = TA

    def body(q_ref, k_ref, v_ref, o_ref, c_ref):
        i2 = 2 * pl.program_id(1)
        row = lax.broadcasted_iota(jnp.int32, (T, T), 0)
        col = lax.broadcasted_iota(jnp.int32, (T, T), 1)
        after = (row > col).astype(BF16)
        causal = col < row
        masks = _head_masks()
        qms = {}
        for b in range(QB):
            q2 = q_ref[b * T:(b + 1) * T, :]
            for h, hm in enumerate(masks):
                qms[b, h] = jnp.where(hm, q2, jnp.zeros_like(q2))

        def blocks(keys, pairs, carries, os):
            ks, vms = [], []
            for j in keys:
                rows = pl.ds(pl.multiple_of(j * T, T), T)
                vj = v_ref[rows, :]
                ks.append(k_ref[rows, :])
                vms.append([jnp.where(hm, vj, jnp.zeros_like(vj)) for hm in masks])
            units = [(n, h) for n in range(len(pairs)) for h in range(2)]
            qks = {(n, h): _nt(qms[pairs[n][0], h], ks[pairs[n][1]]) for n, h in units}
            lbs, l1ms = {}, {}
            for u in units:
                lbs[u], l1m = _log_terms(qks[u])
                l1ms[u] = jnp.where(causal, l1m, 0.0) if pairs[u[0]][2] else l1m
            cins = {u: _hilo_dot(l1ms[u], after) for u in units}
            carries, os = dict(carries), list(os)
            for n, h in units:
                b, key, diag = pairs[n]
                a = jnp.exp2(lbs[n, h] + cins[n, h] + carries[b, h])
                if diag:
                    a = jnp.where(causal, a, 0.0)
                os[b] = os[b] + _nn(a.astype(BF16), vms[key][h])
                carries[b, h] = carries[b, h] + jnp.sum(l1ms[n, h], axis=1, keepdims=True)
            return carries, tuple(os)

        carries = {(b, h): jnp.zeros((T, 1), F32) for b in range(QB) for h in range(2)}
        os = tuple(jnp.zeros((T, 2 * DH), F32) for _ in range(QB))
        carries, os = blocks([i2 + 1, i2], [(1, 0, True), (0, 1, True), (1, 1, False)], carries, os)
        carries, os = lax.fori_loop(
            0, i2, lambda jj, c: blocks([i2 - 1 - jj], [(0, 0, False), (1, 0, False)], c[0], c[1]), (carries, os))
        for b in range(QB):
            o_ref[b * T:(b + 1) * T, :] = os[b].astype(BF16)
            c_ref[b * T:(b + 1) * T, :] = jnp.where(masks[0], carries[b, 0], carries[b, 1])

    blk = pl.BlockSpec((QB * T, 2 * DH), lambda p, i: (i, p))
    full = pl.BlockSpec((S, 2 * DH), lambda p, i: (0, p))
    return _call(
        body, (q, k, v), name="attn_fwd", grid=(SBW // (2 * DH), S // (QB * T)),
        in_specs=[blk, full, full], out_specs=[blk, blk],
        out_shape=[_sds((S, SBW), BF16), _sds((S, SBW), F32)],
        compiler_params=_params(("arbitrary", "arbitrary"), 40), exchange=exchange)


def _attn_bwd(q, k, v, do, ctot, after=()):
    T = TA
    nq = S // (QB * T)

    def body(q_ref, k_ref, v_ref, do_ref, c_ref, dq_ref, dk_ref, dv_ref, dk_acc, dv_acc):
        step = pl.program_id(1)
        i2 = 2 * step

        @pl.when(step == 0)
        def _():
            dk_acc[...] = jnp.zeros_like(dk_acc)
            dv_acc[...] = jnp.zeros_like(dv_acc)

        row = lax.broadcasted_iota(jnp.int32, (T, T), 0)
        col = lax.broadcasted_iota(jnp.int32, (T, T), 1)
        upto = (row <= col).astype(BF16)
        before = (row < col).astype(BF16)
        causal = col < row
        masks = _head_masks()
        qms, doms, ctots = {}, {}, {}
        for b in range(QB):
            q2, do2 = q_ref[b * T:(b + 1) * T, :], do_ref[b * T:(b + 1) * T, :]
            for h, hm in enumerate(masks):
                qms[b, h] = jnp.where(hm, q2, jnp.zeros_like(q2))
                doms[b, h] = jnp.where(hm, do2, jnp.zeros_like(do2))
                ctots[b, h] = c_ref[b * T:(b + 1) * T, h * DH:h * DH + 1]

        def blocks(keys, pairs, sums, dqs):
            rows = [pl.ds(pl.multiple_of(j * T, T), T) for j in keys]
            ks, vs = [k_ref[r, :] for r in rows], [v_ref[r, :] for r in rows]
            kms = [[jnp.where(hm, kj, jnp.zeros_like(kj)) for hm in masks] for kj in ks]
            units = [(n, h) for n in range(len(pairs)) for h in range(2)]
            qks = {(n, h): _nt(qms[pairs[n][0], h], ks[pairs[n][1]]) for n, h in units}
            das = {(n, h): _nt(doms[pairs[n][0], h], vs[pairs[n][1]]) for n, h in units}
            lbs, l1ms = {}, {}
            for u in units:
                lbs[u], l1m = _log_terms(qks[u])
                l1ms[u] = jnp.where(causal, l1m, 0.0) if pairs[u[0]][2] else l1m
            pins = {u: _hilo_dot(l1ms[u], upto) for u in units}
            sums = dict(sums)
            a_s, dls, cps = {}, {}, {}
            for n, h in units:
                b, _, diag = pairs[n]
                cl, cp = sums[b, h]
                a = jnp.exp2(lbs[n, h] + (ctots[b, h] - cl) - pins[n, h])
                if diag:
                    a = jnp.where(causal, a, 0.0)
                a_s[n, h] = a.astype(BF16)
                dls[n, h] = das[n, h] * a
                cps[n, h] = cp
                sums[b, h] = (cl + jnp.sum(l1ms[n, h], axis=1, keepdims=True),
                              cp + jnp.sum(dls[n, h], axis=1, keepdims=True))
            pexs = {u: _hilo_dot(dls[u], before) for u in units}
            dzbs = {}
            for u in units:
                dz = dls[u] - jnp.exp2(lbs[u]) * (dls[u] + pexs[u] + cps[u])
                if pairs[u[0]][2]:
                    dz = jnp.where(causal, dz, 0.0)
                dzbs[u] = dz.astype(BF16)
            dqs = list(dqs)
            for n, h in units:
                dqs[pairs[n][0]] = dqs[pairs[n][0]] + _nn(dzbs[n, h], kms[pairs[n][1]][h])
            for key, r in enumerate(rows):
                mine = [(n, h) for n, h in units if pairs[n][1] == key]
                dk_acc[r, :] += functools.reduce(jnp.add, [_tn(dzbs[u], qms[pairs[u[0]][0], u[1]]) for u in mine])
                dv_acc[r, :] += functools.reduce(jnp.add, [_tn(a_s[u], doms[pairs[u[0]][0], u[1]]) for u in mine])
            return sums, tuple(dqs)

        zero = jnp.zeros((T, 1), F32)
        sums = {(b, h): (zero, zero) for b in range(QB) for h in range(2)}
        dqs = tuple(jnp.zeros((T, 2 * DH), F32) for _ in range(QB))
        sums, dqs = lax.fori_loop(
            0, i2, lambda j, c: blocks([j], [(0, 0, False), (1, 0, False)], c[0], c[1]), (sums, dqs))
        _, dqs = blocks([i2, i2 + 1], [(0, 0, True), (1, 0, False), (1, 1, True)], sums, dqs)
        for b in range(QB):
            dq_ref[b * T:(b + 1) * T, :] = (dqs[b] * SCALE).astype(BF16)

        @pl.when(step == nq - 1)
        def _():
            dk_ref[...] = (dk_acc[...] * SCALE).astype(BF16)
            dv_ref[...] = dv_acc[...].astype(BF16)

    blk = pl.BlockSpec((QB * T, 2 * DH), lambda p, i: (i, p))
    full = pl.BlockSpec((S, 2 * DH), lambda p, i: (0, p))
    return _call(
        body, (q, k, v, do, ctot), name="attn_bwd", grid=(SBW // (2 * DH), nq),
        in_specs=[blk, full, full, blk, blk], out_specs=[blk, full, full],
        out_shape=[_sds((S, SBW), BF16), _sds((S, SBW), BF16), _sds((S, SBW), BF16)],
        scratch_shapes=[pltpu.VMEM((S, 2 * DH), F32), pltpu.VMEM((S, 2 * DH), F32)],
        compiler_params=_params(("arbitrary", "arbitrary"), 40), after=after)


def _pool_counts(first_row, tm):
    pos = first_row + lax.broadcasted_iota(jnp.int32, (tm, 1), 0)
    return [jnp.minimum(pos + 1, w).astype(F32) for w in POOL_WINDOWS]


def _mix_out(h, xp, o_sb, gp, gs, w_group, scale, w_bp, w_ba, w_out, exchange=None):
    tm = 512

    def body(h_ref, xp_ref, o_ref, gp_ref, gs_ref, wg_hbm, sc_ref, wbp_hbm, wba_hbm, wo_hbm,
             h2_ref, pm_ref, p_ref, yp_ref, ys_ref, m_ref, halo, wg_ref, wbp_ref, wba_ref, wo_ref):
        _stage([(wg_hbm, wg_ref), (wbp_hbm, wbp_ref), (wba_hbm, wba_ref), (wo_hbm, wo_ref)])
        i = pl.program_id(0)

        @pl.when(i == 0)
        def _():
            halo[...] = jnp.zeros_like(halo)

        xp = xp_ref[...]
        ext = jnp.concatenate([halo[...], xp], axis=0)
        halo[...] = xp[tm - HALO:, :]
        counts = _pool_counts(i * tm, tm)
        for gi in range(len(POOL_WINDOWS)):
            lanes = slice(gi * PG, (gi + 1) * PG)
            win = ext[:, lanes]
            for step in range(gi + 1):
                win = win + pltpu.roll(win, 1 << step, 0)
            pm = (win[HALO:, :] / counts[gi] - xp[:, lanes]).astype(BF16)
            pm_ref[:, lanes] = pm
            p_ref[:, lanes] = (_nn(pm, wg_ref[gi]) * sc_ref[:, lanes]).astype(BF16)
        pb = p_ref[...]
        ob = o_ref[...]
        for j in range(NSH):
            cols = slice(j * (D // NSH), (j + 1) * (D // NSH))
            yp = _nn(pb, wbp_ref[j])
            ys = _nn(ob, wba_ref[j])
            yp_ref[:, cols] = yp.astype(BF16)
            ys_ref[:, cols] = ys.astype(BF16)
            m_ref[:, cols] = (gp_ref[:, cols].astype(F32) * yp + gs_ref[:, cols].astype(F32) * ys).astype(BF16)
        h2_ref[...] = h_ref[...] + _nn(m_ref[...], wo_ref[...])

    return _call(
        body, (h, xp, o_sb, gp, gs, w_group, scale, w_bp, w_ba, w_out), name="mix_out", grid=(S // tm,),
        in_specs=[_rows(tm, D), _rows(tm, PW), _rows(tm, SBW), _rows(tm, D), _rows(tm, D),
                  _ANY, _fixed((1, PW)), _ANY, _ANY, _ANY],
        out_specs=[_rows(tm, D), _rows(tm, PW), _rows(tm, PW), _rows(tm, D), _rows(tm, D), _rows(tm, D)],
        out_shape=[_sds((S, D), F32), _sds((S, PW), BF16), _sds((S, PW), BF16), _sds((S, D), BF16),
                   _sds((S, D), BF16), _sds((S, D), BF16)],
        scratch_shapes=[pltpu.VMEM((HALO, PW), F32)] + _vmem_like(w_group, w_bp, w_ba, w_out),
        compiler_params=_params(("arbitrary",), 48), free=(5, 6), exchange=exchange)


def _mix_bwd_out(dh, gp, gs, yp, ys, pm, w_group, scale, w_bp, w_ba, w_out, exchange=None):
    tm = 512
    nt = S // tm

    def body(dh_ref, gp_ref, gs_ref, yp_ref, ys_ref, pm_ref, wg_hbm, sc_ref, wbp_hbm, wba_hbm, wo_hbm,
             dlg_ref, dyp_ref, dys_ref, do_ref, dyg_ref, dxp_ref, dsc_ref, halo, wg_ref, wbp_ref, wba_ref, wo_ref):
        _stage([(wg_hbm, wg_ref), (wbp_hbm, wbp_ref), (wba_hbm, wba_ref), (wo_hbm, wo_ref)])
        step = pl.program_id(0)

        @pl.when(step == 0)
        def _():
            halo[...] = jnp.zeros_like(halo)
            dsc_ref[...] = jnp.zeros_like(dsc_ref)

        dm = _nt(dh_ref[...].astype(BF16), wo_ref[...])
        gp = gp_ref[...].astype(F32)
        gs = gs_ref[...].astype(F32)
        yp = yp_ref[...].astype(F32)
        ys = ys_ref[...].astype(F32)
        dlg_ref[:, :D] = (dm * yp * gp * (1.0 - gp)).astype(BF16)
        dlg_ref[:, D:] = (dm * ys * gs * (1.0 - gs)).astype(BF16)
        dyp_ref[...] = (dm * gp).astype(BF16)
        dys_ref[...] = (dm * gs).astype(BF16)
        dp = jnp.zeros((tm, PW), F32)
        do = jnp.zeros((tm, SBW), F32)
        for j in range(NSH):
            cols = slice(j * (D // NSH), (j + 1) * (D // NSH))
            dp = dp + _nt(dyp_ref[:, cols], wbp_ref[j])
            do = do + _nt(dys_ref[:, cols], wba_ref[j])
        do_ref[...] = do.astype(BF16)
        counts = _pool_counts((nt - 1 - step) * tm, tm)
        dscale = []
        for gi in range(len(POOL_WINDOWS)):
            lanes = slice(gi * PG, (gi + 1) * PG)
            dpg = dp[:, lanes]
            dscale.append(jnp.sum(dpg * _nn(pm_ref[:, lanes], wg_ref[gi]), axis=0, keepdims=True))
            dyg = (dpg * sc_ref[:, lanes]).astype(BF16)
            dyg_ref[:, lanes] = dyg
            dpm = _nt(dyg, wg_ref[gi])
            per = dpm / counts[gi]
            win = jnp.concatenate([per, halo[:, lanes]], axis=0)
            halo[:, lanes] = per[:HALO, :]
            for s in range(gi + 1):
                win = win + pltpu.roll(win, tm + HALO - (1 << s), 0)
            dxp_ref[:, lanes] = (win[:tm, :] - dpm).astype(BF16)
        dsc_ref[...] += jnp.concatenate(dscale, axis=1)

    rev = lambda width: pl.BlockSpec((tm, width), lambda i: (nt - 1 - i, 0))
    return _call(
        body, (dh, gp, gs, yp, ys, pm, w_group, scale, w_bp, w_ba, w_out), name="mix_bwd_out", grid=(nt,),
        in_specs=[rev(D), rev(D), rev(D), rev(D), rev(D), rev(PW), _ANY, _fixed((1, PW)), _ANY, _ANY, _ANY],
        out_specs=[rev(2 * D), rev(D), rev(D), rev(SBW), rev(PW), rev(PW), _fixed((1, PW))],
        out_shape=[_sds((S, 2 * D), BF16), _sds((S, D), BF16), _sds((S, D), BF16), _sds((S, SBW), BF16),
                   _sds((S, PW), BF16), _sds((S, PW), BF16), _sds((1, PW), F32)],
        scratch_shapes=[pltpu.VMEM((HALO, PW), F32)] + _vmem_like(w_group, w_bp, w_ba, w_out),
        compiler_params=_params(("arbitrary",), 48), exchange=exchange)


def _mix_bwd_in(dh, h, gain, pieces, w_in, exchange=None):
    tm = 512
    widths = [p.shape[1] for p in pieces]

    def body(dh_ref, h_ref, g_ref, *rest):
        piece_refs, (w_hbm, dx_ref, dg_ref, dp_ref, w_ref) = rest[:len(pieces)], rest[len(pieces):]
        _stage([(w_hbm, w_ref)])
        at = 0
        for ref, width in zip(piece_refs, widths):
            dp_ref[:, at:at + width] = ref[...]
            at += width
        du = jnp.zeros((tm, D), F32)
        for j in range(NSH):
            du = du + _nt(dp_ref[:, j * D:(j + 1) * D], w_ref[j])
        r, hr = _rms(h_ref[...])
        dx, dgain = _rms_bwd(du, hr, r, g_ref[...])
        dx_ref[...] = dh_ref[...] + dx

        @pl.when(pl.program_id(0) == 0)
        def _():
            dg_ref[...] = jnp.zeros_like(dg_ref)

        dg_ref[...] += dgain

    return _call(
        body, (dh, h, gain, *pieces, w_in), name="mix_bwd_in", grid=(S // tm,),
        in_specs=[_rows(tm, D), _rows(tm, D), _fixed((1, D))] + [_rows(tm, w) for w in widths] + [_ANY],
        out_specs=[_rows(tm, D), _fixed((1, D)), _rows(tm, 4 * D)],
        out_shape=[_sds((S, D), F32), _sds((1, D), F32), _sds((S, 4 * D), BF16)],
        scratch_shapes=_vmem_like(w_in),
        compiler_params=_params(("arbitrary",), 48), exchange=exchange)


def _wgrad(a, b, nblk, ti, name, out_dtype=BF16, exchange=None, after=()):
    ka, n = a.shape[1], b.shape[1]
    ns = n // nblk

    def body(a_ref, b_ref, o_ref):
        o_ref[...] = _tn(a_ref[...].astype(BF16), b_ref[...].astype(BF16)).astype(out_dtype)

    res = _call(
        body, (a, b), name=name, grid=(nblk, ka // ti),
        in_specs=[pl.BlockSpec((S, ti), lambda j, i: (0, i)), pl.BlockSpec((S, ns), lambda j, i: (0, j))],
        out_specs=[pl.BlockSpec((None, ti, ns), lambda j, i: (j, i, 0))],
        out_shape=[_sds((nblk, ka, ns), out_dtype)],
        compiler_params=_params(("arbitrary", "arbitrary"), 56), exchange=exchange, after=after)
    return res[0] if exchange is None else (res[0][0], res[1])


def _wgrad_groups(pm, dyg):
    def body(a_ref, b_ref, o_ref):
        o_ref[...] = _tn(a_ref[...], b_ref[...])

    col = pl.BlockSpec((S, PG), lambda g: (0, g))
    return pl.pallas_call(
        body, name="wgrad_groups", grid=(PW // PG,),
        in_specs=[col, col], out_specs=pl.BlockSpec((None, PG, PG), lambda g: (g, 0, 0)),
        out_shape=_sds((PW // PG, PG, PG), F32),
        compiler_params=_params(("arbitrary",), 32),
    )(*_in_hbm([pm, dyg]))


def _place():
    x, y, c = lax.axis_index("x"), lax.axis_index("y"), lax.axis_index("c")
    chips = [(1 - x, y), (x, 1 - y), (1 - x, 1 - y)]
    return x, y, c, chips


def _remote(src, dst, ssem, rsem, dev):
    return pltpu.make_async_remote_copy(src_ref=src, dst_ref=dst, send_sem=ssem, recv_sem=rsem,
                                        device_id=dev, device_id_type=MESH)


def _cast_into_block(w, me_idx, name):
    rows, cols = w.shape
    tr = _row_block(rows)

    def body(me_ref, w_ref, o_ref):
        o_ref[...] = w_ref[...].astype(BF16)

    return pl.pallas_call(
        body, name=name, out_shape=_sds((NSH, rows, cols), BF16),
        grid_spec=pltpu.PrefetchScalarGridSpec(
            num_scalar_prefetch=1, grid=(rows // tr,),
            in_specs=[pl.BlockSpec((tr, cols), lambda r, me: (r, 0))],
            out_specs=pl.BlockSpec((None, tr, cols), lambda r, me: (me[0], r, 0))),
        compiler_params=_params(("arbitrary",), 32),
    )(me_idx, w)


def _ex_gather(bufs):
    n = len(bufs)
    per = 8

    def plan(outs, ssem, rsem, w):
        x, y, c, _ = _place()
        sib, nbr_x, nbr_y = (x, y, 1 - c), (1 - x, y, c), (x, 1 - y, c)
        half = outs[w].shape[1] // 2
        quarter = half // 2
        sem = lambda k: (ssem.at[per * w + k], rsem.at[per * w + k])
        rows = lambda blk, start, size: outs[w].at[blk, pl.ds(start, size)]
        mine = rows(2 * x + y, c * half, half)
        from_x = rows(2 * (1 - x) + y, c * half, half)
        from_y = rows(2 * x + (1 - y), c * half, half)
        diag = 2 * (1 - x) + (1 - y)
        pass_y = rows(2 * (1 - x) + y, c * half, quarter)
        pass_x = rows(2 * x + (1 - y), c * half + quarter, quarter)
        diag_0, diag_1 = rows(diag, c * half, quarter), rows(diag, c * half + quarter, quarter)
        first = [_remote(mine, mine, *sem(0), nbr_x), _remote(mine, mine, *sem(1), nbr_y)]
        arrivals = [
            (_remote(from_x, from_x, *sem(0), nbr_x),
             [_remote(pass_y, pass_y, *sem(2), nbr_y), _remote(from_x, from_x, *sem(4), sib)]),
            (_remote(from_y, from_y, *sem(1), nbr_y),
             [_remote(pass_x, pass_x, *sem(3), nbr_x), _remote(from_y, from_y, *sem(5), sib)]),
            (_remote(diag_0, diag_0, *sem(2), nbr_y), [_remote(diag_0, diag_0, *sem(6), sib)]),
            (_remote(diag_1, diag_1, *sem(3), nbr_x), [_remote(diag_1, diag_1, *sem(7), sib)]),
        ]
        other = (1 - c) * half
        from_sibling = [
            _remote(rows(2 * (1 - x) + y, other, half), rows(2 * (1 - x) + y, other, half), *sem(4), sib),
            _remote(rows(2 * x + (1 - y), other, half), rows(2 * x + (1 - y), other, half), *sem(5), sib),
            _remote(rows(diag, other, quarter), rows(diag, other, quarter), *sem(6), sib),
            _remote(rows(diag, other + quarter, quarter), rows(diag, other + quarter, quarter), *sem(7), sib),
        ]
        return first, arrivals, from_sibling

    def start(ins, outs, ssem, rsem):
        x, y, c, _ = _place()
        for w in range(n):
            half = outs[w].shape[1] // 2
            mine = outs[w].at[2 * x + y, pl.ds(c * half, half)]
            _remote(mine, mine, ssem.at[per * w], rsem.at[per * w], (1 - x, y, c)).start()
            _remote(mine, mine, ssem.at[per * w + 1], rsem.at[per * w + 1], (x, 1 - y, c)).start()

    def finish(ins, outs, ssem, rsem):
        plans = [plan(outs, ssem, rsem, w) for w in range(n)]
        started = []
        for direct in (True, False):
            for first, arrivals, _ in plans:
                for arrived, onward in (arrivals[:2] if direct else arrivals[2:]):
                    arrived.wait_recv()
                    for cp in onward:
                        cp.start()
                    started += onward
        for first, _, from_sibling in plans:
            for cp in from_sibling:
                cp.wait_recv()
            started += first
        for cp in started:
            cp.wait_send()

    return Exchange(bufs, [_sds(b.shape, b.dtype) for b in bufs], {w: w for w in range(n)}, per * n, start, finish)


def _ex_gather_direct(bufs):
    n = len(bufs)

    def copies(outs, ssem, rsem, only_first=False):
        x, y, c, chips = _place()
        me, sib = 2 * x + y, (x, y, 1 - c)
        first, relay, last = [], [], []
        for w in range(n):
            half = outs[w].shape[1] // 2
            mine = outs[w].at[me, pl.ds(c * half, half)]
            for k, (px, py) in enumerate(chips):
                sems = (ssem.at[6 * w + k], rsem.at[6 * w + k])
                sib_sems = (ssem.at[6 * w + 3 + k], rsem.at[6 * w + 3 + k])
                first.append(_remote(mine, mine, *sems, (px, py, c)))
                if only_first:
                    continue
                got = outs[w].at[2 * px + py, pl.ds(c * half, half)]
                relay.append((_remote(got, got, *sems, (px, py, c)), _remote(got, got, *sib_sems, sib)))
                theirs = outs[w].at[2 * px + py, pl.ds((1 - c) * half, half)]
                last.append(_remote(theirs, theirs, *sib_sems, sib))
        return first, relay, last

    def start(ins, outs, ssem, rsem):
        for cp in copies(outs, ssem, rsem, only_first=True)[0]:
            cp.start()

    def finish(ins, outs, ssem, rsem):
        first, relay, last = copies(outs, ssem, rsem)
        for arrived, onward in relay:
            arrived.wait_recv()
            onward.start()
        for cp in last:
            cp.wait_recv()
        for cp in first:
            cp.wait_send()
        for _, onward in relay:
            onward.wait_send()

    return Exchange(bufs, [_sds(b.shape, b.dtype) for b in bufs], {w: w for w in range(n)}, 6 * n, start, finish)


def _simple_exchange(arrays, landing, aliases, make_copies):
    def start(ins, outs, ssem, rsem):
        for cp, _ in make_copies(ins, outs, ssem, rsem, False):
            cp.start()

    def finish(ins, outs, ssem, rsem):
        cps = make_copies(ins, outs, ssem, rsem, True)
        for _, landed in cps:
            landed.wait_recv()
        for cp, _ in cps:
            cp.wait_send()

    return Exchange(arrays, landing, aliases, len(arrays) * 3, start, finish)


def _ex_pair_swap(grads):
    def make(ins, outs, ssem, rsem, landing):
        x, y, c, _ = _place()
        cps = [_remote(ins[w].at[:, 1 - c], outs[w], ssem.at[w], rsem.at[w], (x, y, 1 - c))
               for w in range(len(grads))]
        return [(cp, cp) for cp in cps]

    return _simple_exchange(grads, [_sds((NSH,) + g.shape[2:], g.dtype) for g in grads], {}, make)


def _ex_scatter(parts):
    def make(ins, outs, ssem, rsem, landing):
        x, y, c, chips = _place()
        out = []
        for w in range(len(parts)):
            for k, (px, py) in enumerate(chips):
                sems = (ssem.at[3 * w + k], rsem.at[3 * w + k])
                out.append((_remote(ins[w].at[2 * px + py], outs[w].at[k], *sems, (px, py, c)),
                            _remote(outs[w].at[k], outs[w].at[k], *sems, (px, py, c)) if landing else None))
        return out

    return _simple_exchange(parts, [_sds((3,) + p.shape[1:], p.dtype) for p in parts], {}, make)


def _ex_relay(bufs):
    def make(ins, outs, ssem, rsem, landing):
        x, y, c, chips = _place()
        sib = (x, y, 1 - c)
        out = []
        for w in range(len(bufs)):
            half = outs[w].shape[1] // 2
            for k, (px, py) in enumerate(chips):
                sems = (ssem.at[3 * w + k], rsem.at[3 * w + k])
                have = outs[w].at[2 * px + py, pl.ds(c * half, half)]
                miss = outs[w].at[2 * px + py, pl.ds((1 - c) * half, half)]
                out.append((_remote(have, have, *sems, sib), _remote(miss, miss, *sems, sib) if landing else None))
        return out

    return _simple_exchange(bufs, [_sds(b.shape, b.dtype) for b in bufs], {w: w for w in range(len(bufs))}, make)


def _ex_share(bufs):
    def make(ins, outs, ssem, rsem, landing):
        x, y, c, _ = _place()
        sib = (x, y, 1 - c)
        return [(_remote(outs[w].at[c], outs[w].at[c], ssem.at[w], rsem.at[w], sib),
                 _remote(outs[w].at[1 - c], outs[w].at[1 - c], ssem.at[w], rsem.at[w], sib) if landing else None)
                for w in range(len(bufs))]

    return _simple_exchange(bufs, [_sds(b.shape, b.dtype) for b in bufs], {w: w for w in range(len(bufs))}, make)


def _small_copies(slots, ssems, rsems, sending):
    x, y, c, _ = _place()
    out = []
    for m in range(1, 8):
        px, py, pc = x ^ (m >> 2), y ^ ((m >> 1) & 1), c ^ (m & 1)
        slot = slots.at[4 * x + 2 * y + c if sending else 4 * px + 2 * py + pc]
        out.append(_remote(slot, slot, ssems[m - 1], rsems[m - 1], (px, py, pc)))
    return out


def _small_gather_start(slots, name):
    def body(*refs):
        for cp in _small_copies(refs[0], refs[1:8], refs[8:15], True):
            cp.start()
        refs[-1][...] = jnp.zeros_like(refs[-1])

    outs = pl.pallas_call(
        body, name=name,
        out_shape=([pltpu.SemaphoreType.DMA(())] * 14 + [pltpu.HBM(slots.shape, slots.dtype)]
                   + [jax.ShapeDtypeStruct((8, 128), F32)]),
        in_specs=[_HBM], out_specs=[_SEM] * 14 + [_HBM, _VM], input_output_aliases={0: 14},
        compiler_params=pltpu.CompilerParams(has_side_effects=_EFFECT),
    )(*_in_hbm([slots]))
    return outs[:14], outs[14], outs[15]


def _small_gather_wait(sems, slots, after, name):
    def body(*refs):
        for cp in _small_copies(refs[0], refs[1:8], refs[8:15], True):
            cp.wait_send()
        for cp in _small_copies(refs[0], refs[1:8], refs[8:15], False):
            cp.wait_recv()

    return pl.pallas_call(
        body, name=name, out_shape=pltpu.HBM(slots.shape, slots.dtype),
        in_specs=[_HBM] + [_SEM] * 14 + [_ANY] * len(after), out_specs=_HBM, input_output_aliases={0: 0},
        compiler_params=pltpu.CompilerParams(has_side_effects=_EFFECT),
    )(slots, *sems, *after)


def _row_block(rows, cap=256):
    return max(t for t in range(16, cap + 1, 16) if rows % t == 0)


def _pair_sum(grad, got, c_idx, name):
    _, _, half, cols = grad.shape
    tr = _row_block(half, 512)

    def body(c_ref, a_ref, b_ref, o_ref):
        o_ref[...] = (a_ref[...].astype(F32) + b_ref[...].astype(F32)).astype(BF16)

    return pl.pallas_call(
        body, name=name, out_shape=_sds((NSH, half, cols), BF16),
        grid_spec=pltpu.PrefetchScalarGridSpec(
            num_scalar_prefetch=1, grid=(NSH, half // tr),
            in_specs=[pl.BlockSpec((None, None, tr, cols), lambda j, r, c: (j, c[0], r, 0)),
                      pl.BlockSpec((None, tr, cols), lambda j, r, c: (j, r, 0))],
            out_specs=pl.BlockSpec((None, tr, cols), lambda j, r, c: (j, r, 0))),
        compiler_params=_params(("arbitrary", "arbitrary"), 32),
    )(c_idx, *_in_hbm([grad, got]))


def _chip_sum(own, got, place, name):
    _, half, cols = own.shape
    tr = _row_block(half, 512)

    def body(place_ref, own_ref, got_ref, o_ref):
        acc = own_ref[...].astype(F32)
        for k in range(3):
            acc = acc + got_ref[k].astype(F32)
        o_ref[...] = acc

    return pl.pallas_call(
        body, name=name, out_shape=_sds((2, half, cols), F32),
        grid_spec=pltpu.PrefetchScalarGridSpec(
            num_scalar_prefetch=1, grid=(half // tr,),
            in_specs=[pl.BlockSpec((None, tr, cols), lambda r, p: (p[0], r, 0)),
                      pl.BlockSpec((3, tr, cols), lambda r, p: (0, r, 0))],
            out_specs=pl.BlockSpec((None, tr, cols), lambda r, p: (p[1], r, 0))),
        compiler_params=_params(("arbitrary",), 32),
    )(place, *_in_hbm([own, got]))


def _adamw_math(w, g, m, v):
    m = B1 * m + (1.0 - B1) * g
    v = B2 * v + (1.0 - B2) * (g * g)
    m_hat = m / (1.0 - B1 ** STEP)
    v_hat = v / (1.0 - B2 ** STEP)
    return -LR * (m_hat / (jnp.sqrt(v_hat) + AEPS) + WD * w), m, v


def _adamw(w, g, m, v, name, after=()):
    rows, cols = w.shape
    tr = _row_block(rows)

    def body(w_ref, g_ref, m_ref, v_ref, go_ref, d_ref, nm_ref, nv_ref):
        g = g_ref[...]
        go_ref[...] = g
        d_ref[...], nm_ref[...], nv_ref[...] = _adamw_math(w_ref[...], g, m_ref[...], v_ref[...])

    blk = pl.BlockSpec((tr, cols), lambda r: (r, 0))
    return _call(
        body, (w, g, m, v), name=name, grid=(rows // tr,), out_shape=[_sds(w.shape, F32)] * 4,
        in_specs=[blk] * 4, out_specs=[blk] * 4,
        compiler_params=_params(("arbitrary",), 32), free=(0, 2, 3), after=after)


def _small_update(gathered, w, m, v):
    rows = w.shape[0]

    def body(ga_ref, w_ref, m_ref, v_ref, *out_refs):
        g = ga_ref[0:rows, :]
        for dev in range(1, 8):
            g = g + ga_ref[dev * rows:(dev + 1) * rows, :]
        results = (g,) + _adamw_math(w_ref[...], g, m_ref[...], v_ref[...])
        for i, res in enumerate(results):
            out_refs[i][...] = res[:SMALL_HEAD, :]
            out_refs[4 + i][...] = res[SMALL_HEAD:, :]

    outs = pl.pallas_call(
        body, name="small_update",
        out_shape=[jax.ShapeDtypeStruct((SMALL_HEAD, 128), F32)] * 4
        + [jax.ShapeDtypeStruct((rows - SMALL_HEAD, 128), F32)] * 4,
        in_specs=[_VM] * 4, out_specs=[_VM] * 8,
    )(gathered, w, m, v)
    return outs[:4], outs[4:]


SMALL = ("ffn1_norm", "mix_norm", "ffn2_norm", "final_norm", "pool_scale", "loss", "pool_w_group")
SMALL_HEAD = 48
BIG = ("ffn1_w_gate_up", "ffn1_w_down", "w_in", "w_branch_pool", "w_branch_attn", "w_out",
       "ffn2_w_gate_up", "ffn2_w_down")
ORDER = ("ffn1_norm", "ffn1_w_gate_up", "ffn1_w_down", "mix_norm", "w_in", "pool_w_group", "pool_scale",
         "w_branch_pool", "w_branch_attn", "w_out", "ffn2_norm", "ffn2_w_gate_up", "ffn2_w_down", "final_norm")
SMALL_ROWS = 560


def _pack_small(t):
    parts = []
    for k in SMALL:
        rows = t[k].reshape(-1, 128) if k in t else jnp.zeros((1, 128), F32)
        parts.append(jnp.pad(rows, ((0, -rows.shape[0] % 8), (0, 0))))
    packed = jnp.concatenate(parts, axis=0)
    assert packed.shape == (SMALL_ROWS, 128), packed.shape
    return packed


def _unpack_small(head, group, like):
    out, at = {"pool_w_group": group.reshape(like["pool_w_group"].shape)}, 0
    for k in SMALL[:-1]:
        n = like[k].size // 128 if k in like else 1
        out[k] = head[at:at + n].reshape(like[k].shape) if k in like else head[at, 0]
        at += n + (-n % 8)
    return out


def _halves(g):
    return g.reshape(NSH, 2, g.shape[1] // 2, g.shape[2])


def kernel(x, ffn1_norm, ffn1_w_gate_up, ffn1_w_down, mix_norm, w_in, pool_w_group, pool_scale, w_branch_pool, w_branch_attn, w_out, ffn2_norm, ffn2_w_gate_up, ffn2_w_down, final_norm, loss_target, m_ffn1_norm, m_ffn1_w_gate_up, m_ffn1_w_down, m_mix_norm, m_w_in, m_pool_w_group, m_pool_scale, m_w_branch_pool, m_w_branch_attn, m_w_out, m_ffn2_norm, m_ffn2_w_gate_up, m_ffn2_w_down, m_final_norm, v_ffn1_norm, v_ffn1_w_gate_up, v_ffn1_w_down, v_mix_norm, v_w_in, v_pool_w_group, v_pool_scale, v_w_branch_pool, v_w_branch_attn, v_w_out, v_ffn2_norm, v_ffn2_w_gate_up, v_ffn2_w_down, v_final_norm):
    wts = dict(ffn1_norm=ffn1_norm, ffn1_w_gate_up=ffn1_w_gate_up, ffn1_w_down=ffn1_w_down, mix_norm=mix_norm,
               w_in=w_in, pool_w_group=pool_w_group, pool_scale=pool_scale, w_branch_pool=w_branch_pool,
               w_branch_attn=w_branch_attn, w_out=w_out, ffn2_norm=ffn2_norm, ffn2_w_gate_up=ffn2_w_gate_up,
               ffn2_w_down=ffn2_w_down, final_norm=final_norm)
    mom = dict(ffn1_norm=m_ffn1_norm, ffn1_w_gate_up=m_ffn1_w_gate_up, ffn1_w_down=m_ffn1_w_down,
               mix_norm=m_mix_norm, w_in=m_w_in, pool_w_group=m_pool_w_group, pool_scale=m_pool_scale,
               w_branch_pool=m_w_branch_pool, w_branch_attn=m_w_branch_attn, w_out=m_w_out,
               ffn2_norm=m_ffn2_norm, ffn2_w_gate_up=m_ffn2_w_gate_up, ffn2_w_down=m_ffn2_w_down,
               final_norm=m_final_norm)
    var = dict(ffn1_norm=v_ffn1_norm, ffn1_w_gate_up=v_ffn1_w_gate_up, ffn1_w_down=v_ffn1_w_down,
               mix_norm=v_mix_norm, w_in=v_w_in, pool_w_group=v_pool_w_group, pool_scale=v_pool_scale,
               w_branch_pool=v_w_branch_pool, w_branch_attn=v_w_branch_attn, w_out=v_w_out,
               ffn2_norm=v_ffn2_norm, ffn2_w_gate_up=v_ffn2_w_gate_up, ffn2_w_down=v_ffn2_w_down,
               final_norm=v_final_norm)

    c_idx = lax.axis_index("c").astype(jnp.int32).reshape(1)
    me_idx = (2 * lax.axis_index("x") + lax.axis_index("y")).astype(jnp.int32).reshape(1)
    place = jnp.concatenate([me_idx, c_idx])
    x0, tgt = x[0], loss_target[0]
    wgrp = pool_w_group[0].astype(BF16)
    g1, gm, g2, gf = ffn1_norm, mix_norm, ffn2_norm, final_norm.reshape(1, D)
    grad, delta, new_m, new_v = {}, {}, {}, {}

    def pair_sums(keys, parts, got):
        return [_pair_sum(parts[i], got[i], c_idx, "pair_sum_" + k) for i, k in enumerate(keys)]

    def chip_sums(keys, chip_parts, owned):
        return [_chip_sum(chip_parts[i], owned[i], place, "chip_sum_" + k) for i, k in enumerate(keys)]

    def adamw(k, after=()):
        outs = _adamw(wts[k][0], grad[k][0], mom[k][0], var[k][0], "adamw_" + k, after=after)
        grad[k], delta[k], new_m[k], new_v[k] = (o.reshape(wts[k].shape) for o in outs)

    own = {k: _cast_into_block(wts[k][0], me_idx, "cast_" + k) for k in BIG}
    first, late = ("ffn1_w_gate_up", "ffn1_w_down"), ("w_branch_pool", "w_branch_attn", "w_out",
                                                       "ffn2_w_gate_up", "ffn2_w_down")
    full = dict(zip(first, _exchange_alone(_ex_gather([own[k] for k in first]), "gather_ffn1")))
    wgu1, wd1 = full["ffn1_w_gate_up"], full["ffn1_w_down"].reshape(DFF, D)
    (h1, n1, gu1, a1), (win,) = _ffn_fwd(x0, g1, wgu1, wd1, "ffn1_fwd", exchange=_ex_gather_direct([own["w_in"]]))
    sems_l, thru_l, token_l = _gather_start([own[k_] for k_ in late], [h1], "gather_late_start")
    u, xp, q, k, v, gp, gs = _mix_in(h1, gm, win, after=(token_l,))
    o_sb, ctot = _attn_fwd(q, k, v)
    arrived = _gather_wait(sems_l, thru_l, [o_sb], "gather_late_wait")
    wbp, wba, wout = _exchange_alone(_ex_relay(arrived[:3]), "relay_mix")
    wout = wout.reshape(D, D)
    (h2, pm, p, yp, ys, mm), (wgu2, wd2) = _mix_out(h1, xp, o_sb, gp, gs, wgrp, pool_scale, wbp, wba, wout,
                                                    exchange=_ex_relay(arrived[3:]))
    wd2 = wd2.reshape(DFF, D)
    dh3, loss_row, d_gf, n3, gu3, a3 = _ffn_fwd(h2, g2, wgu2, wd2, "ffn2_fwd", head=(tgt, gf))

    def grad_gate_up(n, dgu, name, exchange=None):
        res = _wgrad(n, dgu, NSH, D, name, exchange=exchange)
        return [_halves(res)] if exchange is None else ([_halves(res[0])], res[1])

    def grad_down(a, dh, name, exchange=None):
        res = _wgrad(a, dh, 1, FFS, name, exchange=exchange)
        halves = lambda g: [_halves(g.reshape(NSH, DFF // NSH, D))]
        return halves(res) if exchange is None else (halves(res[0]), res[1])

    k_gu2, k_d2, k_gu1, k_d1, k_in = (("ffn2_w_gate_up",), ("ffn2_w_down",), ("ffn1_w_gate_up",),
                                      ("ffn1_w_down",), ("w_in",))
    dh2, dgu3, d_g2 = _ffn_bwd(dh3, h2, g2, gu3, wgu2, wd2, "ffn2_bwd")
    pa = grad_gate_up(n3, dgu3, "wgrad_gu2") + grad_down(a3, dh3, "wgrad_d2")
    (dlg, dyp, dys, do_sb, dyg, dxp, d_scale), got_a = _mix_bwd_out(
        dh2, gp, gs, yp, ys, pm, wgrp, pool_scale, wbp, wba, wout, exchange=_ex_pair_swap(pa))
    chip_a = pair_sums(k_gu2 + k_d2, pa, got_a)
    kb = ("w_out", "w_branch_pool", "w_branch_attn")
    pb = [_halves(_wgrad(mm, dh2, 1, D, "wgrad_out").reshape(NSH, D // NSH, D)),
          _halves(_wgrad(p, dyp, NSH, PW, "wgrad_bp")), _halves(_wgrad(o_sb, dys, NSH, SBW, "wgrad_ba"))]
    k_a, k_in = k_gu2 + k_d2, k_in + kb
    sems_a, thru_a, token_a = _scatter_start(chip_a, "scatter_a_start")
    dq, dk, dv = _attn_bwd(q, k, v, do_sb, ctot, after=(token_a,))
    chip_a, owned_a = _scatter_wait(sems_a, thru_a, [dq], "scatter_a_wait")
    halves_a = chip_sums(k_a, chip_a, owned_a)
    (dh1, d_gm, dproj), both_a = _mix_bwd_in(dh2, h1, gm, (dxp, dq, dk, dv, dlg), win, exchange=_ex_share(halves_a))
    for i, k_ in enumerate(k_a):
        grad[k_] = both_a[i].reshape(wts[k_].shape)

    p_in = [_halves(_wgrad(u, dproj, NSH, D, "wgrad_in"))] + pb
    p_d1, got_in = grad_down(a1, dh1, "wgrad_d1", exchange=_ex_pair_swap(p_in))
    sems_in, thru_in, token_in = _scatter_start(pair_sums(k_in, p_in, got_in), "scatter_in_start")
    dgu1, got_d1 = _ffn_bwd_act(dh1, gu1, wd1, "ffn1_bwd_act", exchange=_ex_pair_swap(p_d1), after=(token_in,))
    sems_d1, thru_d1, token_d1 = _scatter_start(pair_sums(k_d1, p_d1, got_d1), "scatter_d1_start")
    p_gu1 = [_halves(_wgrad(n1, dgu1, NSH, D, "wgrad_gu1", after=(token_in, token_d1)))]
    chip_in, owned_in = _scatter_wait(sems_in, thru_in, p_gu1, "scatter_in_wait")
    chip_d1, owned_d1 = _scatter_wait(sems_d1, thru_d1, p_gu1, "scatter_d1_wait")
    halves_in, halves_d1 = chip_sums(k_in, chip_in, owned_in), chip_sums(k_d1, chip_d1, owned_d1)
    landed = _exchange_alone(_join(_ex_pair_swap(p_gu1), _ex_share(halves_in + halves_d1)), "pair_swap_gu1")
    for i, k_ in enumerate(k_in + k_d1):
        grad[k_] = landed[1 + i].reshape(wts[k_].shape)
    sems, thru, token = _scatter_start(pair_sums(k_gu1, p_gu1, landed[:1]), "scatter_gu1_start")
    for k_ in k_a + k_in + k_d1:
        adamw(k_, after=(token,))
    dx, d_g1 = _ffn_bwd_in(dh1, x0, g1, dgu1, wgu1, "ffn1_bwd_in", after=(token,))
    small_g = dict(ffn1_norm=d_g1, mix_norm=d_gm, ffn2_norm=d_g2, final_norm=d_gf, pool_scale=d_scale,
                   pool_w_group=_wgrad_groups(pm, dyg), loss=loss_row)
    dev = 4 * lax.axis_index("x") + 2 * lax.axis_index("y") + lax.axis_index("c")
    slots = lax.dynamic_update_slice(jnp.zeros((8, SMALL_ROWS, 128), F32), _pack_small(small_g)[None], (dev, 0, 0))
    sems_s, slots, token_s = _small_gather_start(slots, "small_gather_start")

    chip_gu1, owned_gu1 = _scatter_wait(sems, thru, [dx] + [delta[k_] for k_ in k_a + k_in + k_d1], "scatter_gu1_wait")
    both = _exchange_alone(_ex_share(chip_sums(k_gu1, chip_gu1, owned_gu1)), "share_last", after=(token_s,))
    grad["ffn1_w_gate_up"] = both[0].reshape(ffn1_w_gate_up.shape)
    adamw("ffn1_w_gate_up", after=(token_s,))
    gathered = _small_gather_wait(sems_s, slots, [delta["ffn1_w_gate_up"]], "small_gather_wait")
    gathered = gathered.reshape(8 * SMALL_ROWS, 128)
    heads, groups = _small_update(gathered, _pack_small(wts), _pack_small(mom), _pack_small(var))
    for dst, head, group in zip((grad, delta, new_m, new_v), heads, groups):
        vals = _unpack_small(head, group, wts)
        if dst is grad:
            loss = vals["loss"]
        vals.pop("loss")
        dst.update(vals)
    return (loss, dx[None], *[grad[k_] for k_ in ORDER], *[delta[k_] for k_ in ORDER],
            *[new_m[k_] for k_ in ORDER], *[new_v[k_] for k_ in ORDER])
```

```python
import dataclasses
import functools

import jax
import jax.numpy as jnp
from jax import lax
from jax.experimental import pallas as pl
from jax.experimental.pallas import tpu as pltpu

F32 = jnp.float32
BF16 = jnp.bfloat16

S = 2048
D = 1024
DFF = 2816
FFS = 2 * DFF // 4
NSH = 4
PW = 512
PG = 128
POOL_WINDOWS = (2, 4, 8, 16)
HALO = 16
SBW = 512
DH = 64
EPS = 1e-6
SCALE = 0.125
LOG2E = 1.4426950408889634
TA = 256
QB = 2
MIB = 1024 * 1024

LR, B1, B2, AEPS, WD, STEP = 0.001, 0.9, 0.999, 1e-08, 0.01, 10

_VM = pl.BlockSpec(memory_space=pltpu.VMEM)
_ANY = pl.BlockSpec(memory_space=pl.ANY)
MESH = pl.DeviceIdType.MESH
SIBLING_PAIR_ID = 1


def _nn(a, b):
    return jnp.dot(a, b, preferred_element_type=F32)


def _nt(a, b):
    return lax.dot_general(a, b, (((1,), (1,)), ((), ())), preferred_element_type=F32)


def _tn(a, b):
    return lax.dot_general(a, b, (((0,), (0,)), ((), ())), preferred_element_type=F32)


def _params(sem, vmem_mib):
    return pltpu.CompilerParams(dimension_semantics=sem, vmem_limit_bytes=vmem_mib * MIB)


def _rows(tm, width):
    return pl.BlockSpec((tm, width), lambda i: (i, 0))


def _fixed(shape):
    return pl.BlockSpec(shape, lambda *_: (0,) * len(shape))


def _sds(shape, dtype):
    return pltpu.HBM(shape, dtype)


def _in_hbm(args):
    return [pltpu.with_memory_space_constraint(a, pltpu.HBM) for a in args]


def _stage(pairs):
    @pl.when(pl.program_id(0) == 0)
    def _():
        for src, dst in pairs:
            pltpu.sync_copy(src, dst)


def _vmem_like(*arrays):
    return [pltpu.VMEM(a.shape, a.dtype) for a in arrays]


class Exchange:
    def __init__(self, arrays, landing, aliases, n_sems, start, finish, sibling_only=False):
        self.arrays, self.landing, self.aliases, self.n_sems = list(arrays), list(landing), dict(aliases), n_sems
        self.start, self.finish = start, finish
        self.sibling_only = sibling_only

    def enter(self):
        if self.sibling_only:
            barrier = pltpu.get_barrier_semaphore()
            sibling = (lax.axis_index("x"), lax.axis_index("y"), 1 - lax.axis_index("c"))
            pl.semaphore_signal(barrier, inc=1, device_id=sibling, device_id_type=MESH)
            pl.semaphore_wait(barrier, 1)

    def params(self, compiler_params=None):
        kw = dict(collective_id=SIBLING_PAIR_ID) if self.sibling_only else {}
        if compiler_params is None:
            return pltpu.CompilerParams(**kw)
        return dataclasses.replace(compiler_params, **kw)


def _join(a, b):
    na, la = len(a.arrays), len(a.landing)

    def both(fa, fb):
        def run(ins, outs, ssem, rsem):
            fa(ins[:na], outs[:la], ssem.at[pl.ds(0, a.n_sems)], rsem.at[pl.ds(0, a.n_sems)])
            fb(ins[na:], outs[la:], ssem.at[pl.ds(a.n_sems, b.n_sems)], rsem.at[pl.ds(a.n_sems, b.n_sems)])
        return run

    aliases = {**a.aliases, **{na + i: la + j for i, j in b.aliases.items()}}
    return Exchange(a.arrays + b.arrays, a.landing + b.landing, aliases, a.n_sems + b.n_sems,
                    both(a.start, b.start), both(a.finish, b.finish), a.sibling_only and b.sibling_only)


def _call(body, args, *, name, grid, in_specs, out_specs, out_shape, scratch_shapes=(), compiler_params=None,
          exchange=None, free=(), after=()):
    args = [a if i in free else pltpu.with_memory_space_constraint(a, pltpu.HBM) for i, a in enumerate(args)]
    if exchange is None:
        n_in = len(in_specs)

        def plain(*refs):
            body(*refs[:n_in], *refs[n_in + len(after):])

        return pl.pallas_call(plain, name=name, grid=grid, in_specs=list(in_specs) + [_ANY] * len(after),
                              out_specs=out_specs, out_shape=out_shape, scratch_shapes=list(scratch_shapes),
                              compiler_params=compiler_params)(*args, *after)
    ex = exchange
    n_in, n_out, n_scr = len(in_specs), len(out_specs), len(scratch_shapes)
    na, nl = len(ex.arrays), len(ex.landing)

    def hosted(*refs):
        at = [0]

        def take(n):
            at[0] += n
            return refs[at[0] - n:at[0]]

        k_in, _, e_in, k_out, e_out, k_scr = take(n_in), take(len(after)), take(na), take(n_out), take(nl), take(n_scr)
        ssem, rsem = take(2)
        ids = [pl.program_id(a) for a in range(len(grid))]
        first = functools.reduce(jnp.logical_and, [i == 0 for i in ids])
        last = functools.reduce(jnp.logical_and, [i == g - 1 for i, g in zip(ids, grid)])

        @pl.when(first)
        def _():
            ex.enter()
            ex.start(e_in, e_out, ssem, rsem)

        body(*k_in, *k_out, *k_scr)

        @pl.when(last)
        def _():
            ex.finish(e_in, e_out, ssem, rsem)

    outs = pl.pallas_call(
        hosted, name=name, grid=grid,
        in_specs=list(in_specs) + [_ANY] * (len(after) + na), out_specs=list(out_specs) + [_ANY] * nl,
        out_shape=list(out_shape) + ex.landing,
        scratch_shapes=list(scratch_shapes) + [pltpu.SemaphoreType.DMA((ex.n_sems,))] * 2,
        input_output_aliases={n_in + len(after) + i: n_out + j for i, j in ex.aliases.items()},
        compiler_params=ex.params(compiler_params),
    )(*args, *after, *_in_hbm(ex.arrays))
    return outs[:n_out], outs[n_out:]


def _exchange_alone(ex, name, after=()):
    na, nl = len(ex.arrays), len(ex.landing)

    def body(*refs):
        outs = refs[na + len(after):na + len(after) + nl]
        ex.enter()
        ex.start(refs[:na], outs, refs[-2], refs[-1])
        ex.finish(refs[:na], outs, refs[-2], refs[-1])

    return pl.pallas_call(
        body, name=name, in_specs=[_ANY] * (na + len(after)), out_specs=[_ANY] * nl,
        out_shape=ex.landing, scratch_shapes=[pltpu.SemaphoreType.DMA((ex.n_sems,))] * 2,
        input_output_aliases=ex.aliases, compiler_params=ex.params(),
    )(*_in_hbm(ex.arrays), *after)


_HBM = pl.BlockSpec(memory_space=pltpu.HBM)
_SEM = pl.BlockSpec(memory_space=pltpu.SEMAPHORE)
_EFFECT = pltpu.SideEffectType.DATAFLOW_SIDE_EFFECTING


def _scatter_copies(srcs, lands, ssems, rsems):
    x, y, c, chips = _place()
    return [_remote(srcs[w].at[2 * px + py], lands[w].at[k], ssems[3 * w + k], rsems[3 * w + k], (px, py, c))
            for w in range(len(srcs)) for k, (px, py) in enumerate(chips)]


def _scatter_start(parts, name):
    n, ncp = len(parts), 3 * len(parts)
    lands = [lax.empty((3,) + p.shape[1:], p.dtype) for p in parts]

    def body(*refs):
        srcs, land_refs = refs[:n], refs[n:2 * n]
        ssems, rsems = refs[2 * n:2 * n + ncp], refs[2 * n + ncp:2 * n + 2 * ncp]
        for cp in _scatter_copies(srcs, land_refs, ssems, rsems):
            cp.start()
        token = refs[-1]
        token[...] = jnp.zeros_like(token)

    outs = pl.pallas_call(
        body, name=name,
        out_shape=([pltpu.SemaphoreType.DMA(())] * (2 * ncp) + [pltpu.HBM(a.shape, a.dtype) for a in parts + lands]
                   + [jax.ShapeDtypeStruct((8, 128), F32)]),
        in_specs=[_HBM] * (2 * n), out_specs=[_SEM] * (2 * ncp) + [_HBM] * (2 * n) + [_VM],
        input_output_aliases={i: 2 * ncp + i for i in range(2 * n)},
        compiler_params=pltpu.CompilerParams(has_side_effects=_EFFECT),
    )(*_in_hbm(parts), *_in_hbm(lands))
    sems, thru, token = outs[:2 * ncp], outs[2 * ncp:2 * ncp + 2 * n], outs[-1]
    return sems, thru, token


def _scatter_wait(sems, thru, after, name):
    n = len(thru) // 2
    ncp = 3 * n

    def body(*refs):
        srcs, land_refs = refs[:n], refs[n:2 * n]
        ssems, rsems = refs[2 * n:2 * n + ncp], refs[2 * n + ncp:2 * n + 2 * ncp]
        for cp in _scatter_copies(srcs, land_refs, ssems, rsems):
            cp.wait_send()
            cp.wait_recv()

    outs = pl.pallas_call(
        body, name=name, out_shape=[pltpu.HBM(a.shape, a.dtype) for a in thru],
        in_specs=[_HBM] * (2 * n) + [_SEM] * (2 * ncp) + [_ANY] * len(after), out_specs=[_HBM] * (2 * n),
        input_output_aliases={i: i for i in range(2 * n)},
        compiler_params=pltpu.CompilerParams(has_side_effects=_EFFECT),
    )(*thru, *sems, *after)
    return outs[:n], outs[n:]


def _gather_copies(bufs, ssems, rsems, sending):
    x, y, c, chips = _place()
    out = []
    for w, ref in enumerate(bufs):
        half = ref.shape[1] // 2
        for k, (px, py) in enumerate(chips):
            rows = ref.at[2 * x + y if sending else 2 * px + py, pl.ds(c * half, half)]
            out.append(_remote(rows, rows, ssems[3 * w + k], rsems[3 * w + k], (px, py, c)))
    return out


def _gather_start(bufs, after, name):
    n, ncp = len(bufs), 3 * len(bufs)

    def body(*refs):
        ssems, rsems = refs[n + len(after):n + len(after) + ncp], refs[n + len(after) + ncp:n + len(after) + 2 * ncp]
        for cp in _gather_copies(refs[:n], ssems, rsems, True):
            cp.start()
        token = refs[-1]
        token[...] = jnp.zeros_like(token)

    outs = pl.pallas_call(
        body, name=name,
        out_shape=([pltpu.SemaphoreType.DMA(())] * (2 * ncp) + [pltpu.HBM(a.shape, a.dtype) for a in bufs]
                   + [jax.ShapeDtypeStruct((8, 128), F32)]),
        in_specs=[_HBM] * n + [_ANY] * len(after), out_specs=[_SEM] * (2 * ncp) + [_HBM] * n + [_VM],
        input_output_aliases={i: 2 * ncp + i for i in range(n)},
        compiler_params=pltpu.CompilerParams(has_side_effects=_EFFECT),
    )(*_in_hbm(bufs), *after)
    return outs[:2 * ncp], outs[2 * ncp:2 * ncp + n], outs[-1]


def _gather_wait(sems, thru, after, name):
    n = len(thru)
    ncp = 3 * n

    def body(*refs):
        ssems, rsems = refs[n:n + ncp], refs[n + ncp:n + 2 * ncp]
        for cp in _gather_copies(refs[:n], ssems, rsems, True):
            cp.wait_send()
        for cp in _gather_copies(refs[:n], ssems, rsems, False):
            cp.wait_recv()

    return pl.pallas_call(
        body, name=name, out_shape=[pltpu.HBM(a.shape, a.dtype) for a in thru],
        in_specs=[_HBM] * n + [_SEM] * (2 * ncp) + [_ANY] * len(after), out_specs=[_HBM] * n,
        input_output_aliases={i: i for i in range(n)},
        compiler_params=pltpu.CompilerParams(has_side_effects=_EFFECT),
    )(*thru, *sems, *after)


def _rms(x):
    r = lax.rsqrt(jnp.mean(x * x, axis=-1, keepdims=True) + EPS)
    return r, x * r


def _rms_bwd(dn, xr, r, gain):
    dng = dn * gain
    dx = r * (dng - xr * jnp.mean(dng * xr, axis=-1, keepdims=True))
    return dx, jnp.sum(dn * xr, axis=0, keepdims=True)


def _ffn_fwd(x, gain, wgu, wd, name, exchange=None, head=None):
    tm = 256

    def body(x_ref, g_ref, wgu_hbm, wd_hbm, *rest):
        if head is None:
            h_ref, n_ref, gu_ref, a_ref, wgu_ref, wd_ref = rest
        else:
            t_ref, gf_ref, h_ref, loss_ref, dgf_ref, n_ref, gu_ref, a_ref, wgu_ref, wd_ref = rest
        _stage([(wgu_hbm, wgu_ref), (wd_hbm, wd_ref)])
        x = x_ref[...]
        _, xr = _rms(x)
        n = (xr * g_ref[...]).astype(BF16)
        n_ref[...] = n
        acc = jnp.zeros((tm, D), F32)
        for j in range(2):
            g = _nn(n, wgu_ref[j])
            u = _nn(n, wgu_ref[2 + j])
            gu_ref[:, j * FFS:(j + 1) * FFS] = g.astype(BF16)
            gu_ref[:, (2 + j) * FFS:(3 + j) * FFS] = u.astype(BF16)
            half_act = (0.5 * (g * jax.nn.sigmoid(g) * u)).astype(BF16)
            a_ref[:, j * FFS:(j + 1) * FFS] = half_act
            acc = acc + _nn(half_act, wd_ref[j * FFS:(j + 1) * FFS, :])
        h = x + acc
        if head is None:
            h_ref[...] = h
            return
        gf = gf_ref[...]
        r, hr = _rms(h)
        err = hr * gf - t_ref[...]
        dh, dgain = _rms_bwd(err * (1.0 / D), hr, r, gf)
        h_ref[...] = dh

        @pl.when(pl.program_id(0) == 0)
        def _():
            dgf_ref[...] = jnp.zeros_like(dgf_ref)
            loss_ref[...] = jnp.zeros_like(loss_ref)

        dgf_ref[...] += dgain
        loss_ref[...] += jnp.full((1, 128), (0.5 / D) * jnp.sum(err * err), F32)

    saved_specs = [_rows(tm, D), _rows(tm, 4 * FFS), _rows(tm, DFF)]
    saved_shapes = [_sds((S, D), BF16), _sds((S, 4 * FFS), BF16), _sds((S, DFF), BF16)]
    if head is None:
        return _call(
            body, (x, gain, wgu, wd), name=name, grid=(S // tm,),
            in_specs=[_rows(tm, D), _fixed((1, D)), _ANY, _ANY],
            out_specs=[_rows(tm, D)] + saved_specs, out_shape=[_sds((S, D), F32)] + saved_shapes,
            scratch_shapes=_vmem_like(wgu, wd),
            compiler_params=_params(("arbitrary",), 56), exchange=exchange)
    return _call(
        body, (x, gain, wgu, wd, *head), name=name, grid=(S // tm,),
        in_specs=[_rows(tm, D), _fixed((1, D)), _ANY, _ANY, _rows(tm, D), _fixed((1, D))],
        out_specs=[_rows(tm, D), _fixed((1, 128)), _fixed((1, D))] + saved_specs,
        out_shape=[_sds((S, D), F32), _sds((1, 128), F32), _sds((1, D), F32)] + saved_shapes,
        scratch_shapes=_vmem_like(wgu, wd),
        compiler_params=_params(("arbitrary",), 56), exchange=exchange, free=(4, 5))


def _ffn_bwd(dh, x, gain, gu, wgu, wd, name):
    tm = 256

    def body(dh_ref, x_ref, g_ref, gu_ref, wgu_hbm, wd_hbm, dx_ref, dgu_ref, dg_ref, wgu_ref, wd_ref):
        _stage([(wgu_hbm, wgu_ref), (wd_hbm, wd_ref)])
        dh = dh_ref[...]
        dhb = dh.astype(BF16)
        dn = jnp.zeros((tm, D), F32)
        for j in range(2):
            g = gu_ref[:, j * FFS:(j + 1) * FFS].astype(F32)
            u = gu_ref[:, (2 + j) * FFS:(3 + j) * FFS].astype(F32)
            da = 0.5 * _nt(dhb, wd_ref[j * FFS:(j + 1) * FFS, :])
            sg = jax.nn.sigmoid(g)
            dgb = (da * u * (sg * (1.0 + g * (1.0 - sg)))).astype(BF16)
            dub = (da * (g * sg)).astype(BF16)
            dgu_ref[:, j * FFS:(j + 1) * FFS] = dgb
            dgu_ref[:, (2 + j) * FFS:(3 + j) * FFS] = dub
            dn = dn + _nt(dgb, wgu_ref[j]) + _nt(dub, wgu_ref[2 + j])
        r, xr = _rms(x_ref[...])
        dx, dgain = _rms_bwd(dn, xr, r, g_ref[...])
        dx_ref[...] = dh + dx

        @pl.when(pl.program_id(0) == 0)
        def _():
            dg_ref[...] = jnp.zeros_like(dg_ref)

        dg_ref[...] += dgain

    return _call(
        body, (dh, x, gain, gu, wgu, wd), name=name, grid=(S // tm,),
        in_specs=[_rows(tm, D), _rows(tm, D), _fixed((1, D)), _rows(tm, 4 * FFS), _ANY, _ANY],
        out_specs=[_rows(tm, D), _rows(tm, 4 * FFS), _fixed((1, D))],
        out_shape=[_sds((S, D), F32), _sds((S, 4 * FFS), BF16), _sds((1, D), F32)],
        scratch_shapes=_vmem_like(wgu, wd), compiler_params=_params(("arbitrary",), 56))


def _ffn_bwd_act(dh, gu, wd, name, exchange=None, after=()):
    tm = 512

    def body(dh_ref, gu_ref, wd_hbm, dgu_ref, wd_ref):
        _stage([(wd_hbm, wd_ref)])
        dhb = dh_ref[...].astype(BF16)
        for j in range(2):
            g = gu_ref[:, j * FFS:(j + 1) * FFS].astype(F32)
            u = gu_ref[:, (2 + j) * FFS:(3 + j) * FFS].astype(F32)
            da = 0.5 * _nt(dhb, wd_ref[j * FFS:(j + 1) * FFS, :])
            sg = jax.nn.sigmoid(g)
            dgu_ref[:, j * FFS:(j + 1) * FFS] = (da * u * (sg * (1.0 + g * (1.0 - sg)))).astype(BF16)
            dgu_ref[:, (2 + j) * FFS:(3 + j) * FFS] = (da * (g * sg)).astype(BF16)

    res = _call(
        body, (dh, gu, wd), name=name, grid=(S // tm,),
        in_specs=[_rows(tm, D), _rows(tm, 4 * FFS), _ANY], out_specs=[_rows(tm, 4 * FFS)],
        out_shape=[_sds((S, 4 * FFS), BF16)], scratch_shapes=_vmem_like(wd),
        compiler_params=_params(("arbitrary",), 56), exchange=exchange, after=after)
    return res[0] if exchange is None else (res[0][0], res[1])


def _ffn_bwd_in(dh, x, gain, dgu, wgu, name, exchange=None, after=()):
    tm = 512

    def body(dh_ref, x_ref, g_ref, dgu_ref, wgu_hbm, dx_ref, dg_ref, wgu_ref):
        _stage([(wgu_hbm, wgu_ref)])
        dn = jnp.zeros((tm, D), F32)
        for j in range(NSH):
            dn = dn + _nt(dgu_ref[:, j * FFS:(j + 1) * FFS], wgu_ref[j])
        r, xr = _rms(x_ref[...])
        dx, dgain = _rms_bwd(dn, xr, r, g_ref[...])
        dx_ref[...] = dh_ref[...] + dx

        @pl.when(pl.program_id(0) == 0)
        def _():
            dg_ref[...] = jnp.zeros_like(dg_ref)

        dg_ref[...] += dgain

    return _call(
        body, (dh, x, gain, dgu, wgu), name=name, grid=(S // tm,),
        in_specs=[_rows(tm, D), _rows(tm, D), _fixed((1, D)), _rows(tm, 4 * FFS), _ANY],
        out_specs=[_rows(tm, D), _fixed((1, D))],
        out_shape=[_sds((S, D), F32), _sds((1, D), F32)],
        scratch_shapes=_vmem_like(wgu),
        compiler_params=_params(("arbitrary",), 56), exchange=exchange, after=after)


def _mix_in(h, gain, w_in, after=()):
    tm = 512

    def body(h_ref, g_ref, w_hbm, u_ref, xp_ref, q_ref, k_ref, v_ref, gp_ref, gs_ref, w_ref):
        _stage([(w_hbm, w_ref)])
        _, hr = _rms(h_ref[...])
        u = (hr * g_ref[...]).astype(BF16)
        u_ref[...] = u
        p0 = _nn(u, w_ref[0])
        xp_ref[...] = p0[:, :PW]
        q_ref[...] = p0[:, PW:].astype(BF16)
        p1 = _nn(u, w_ref[1])
        k_ref[...] = p1[:, :SBW].astype(BF16)
        v_ref[...] = p1[:, SBW:].astype(BF16)
        gp_ref[...] = jax.nn.sigmoid(_nn(u, w_ref[2])).astype(BF16)
        gs_ref[...] = jax.nn.sigmoid(_nn(u, w_ref[3])).astype(BF16)

    return _call(
        body, (h, gain, w_in), name="mix_in", grid=(S // tm,),
        in_specs=[_rows(tm, D), _fixed((1, D)), _ANY],
        out_specs=[_rows(tm, D), _rows(tm, PW), _rows(tm, SBW), _rows(tm, SBW), _rows(tm, SBW),
                   _rows(tm, D), _rows(tm, D)],
        out_shape=[_sds((S, D), BF16), _sds((S, PW), F32), _sds((S, SBW), BF16), _sds((S, SBW), BF16),
                   _sds((S, SBW), BF16), _sds((S, D), BF16), _sds((S, D), BF16)],
        scratch_shapes=_vmem_like(w_in),
        compiler_params=_params(("arbitrary",), 48), free=(1,), after=after)


def _hilo_dot(x, tri):
    hi = x.astype(BF16)
    lo = (x - hi.astype(F32)).astype(BF16)
    return _nn(hi, tri) + _nn(lo, tri)


def _log_terms(qk):
    z2 = qk * (SCALE * LOG2E)
    lb = jnp.minimum(z2, 0.0) - jnp.log2(1.0 + jnp.exp2(-jnp.abs(z2)))
    return lb, lb - z2


def _head_masks():
    lane = lax.broadcasted_iota(jnp.int32, (1, 2 * DH), 1)
    return (lane < DH, lane >= DH)


def _attn_fwd(q, k, v, exchange=None):
    T = TA

    def body(q_ref, k_ref, v_ref, o_ref, c_ref):
        i2 = 2 * pl.program_id(1)
        row = lax.broadcasted_iota(jnp.int32, (T, T), 0)
        col = lax.broadcasted_iota(jnp.int32, (T, T), 1)
        after = (row > col).astype(BF16)
        causal = col < row
        masks = _head_masks()
        qms = {}
        for b in range(QB):
            q2 = q_ref[b * T:(b + 1) * T, :]
            for h, hm in enumerate(masks):
                qms[b, h] = jnp.where(hm, q2, jnp.zeros_like(q2))

        def blocks(keys, pairs, carries, os):
            ks, vms = [], []
            for j in keys:
                rows = pl.ds(pl.multiple_of(j * T, T), T)
                vj = v_ref[rows, :]
                ks.append(k_ref[rows, :])
                vms.append([jnp.where(hm, vj, jnp.zeros_like(vj)) for hm in masks])
            units = [(n, h) for n in range(len(pairs)) for h in range(2)]
            qks = {(n, h): _nt(qms[pairs[n][0], h], ks[pairs[n][1]]) for n, h in units}
            lbs, l1ms = {}, {}
            for u in units:
                lbs[u], l1m = _log_terms(qks[u])
                l1ms[u] = jnp.where(causal, l1m, 0.0) if pairs[u[0]][2] else l1m
            cins = {u: _hilo_dot(l1ms[u], after) for u in units}
            carries, os = dict(carries), list(os)
            for n, h in units:
                b, key, diag = pairs[n]
                a = jnp.exp2(lbs[n, h] + cins[n, h] + carries[b, h])
                if diag:
                    a = jnp.where(causal, a, 0.0)
                os[b] = os[b] + _nn(a.astype(BF16), vms[key][h])
                carries[b, h] = carries[b, h] + jnp.sum(l1ms[n, h], axis=1, keepdims=True)
            return carries, tuple(os)

        carries = {(b, h): jnp.zeros((T, 1), F32) for b in range(QB) for h in range(2)}
        os = tuple(jnp.zeros((T, 2 * DH), F32) for _ in range(QB))
        carries, os = blocks([i2 + 1, i2], [(1, 0, True), (0, 1, True), (1, 1, False)], carries, os)
        carries, os = lax.fori_loop(
            0, i2, lambda jj, c: blocks([i2 - 1 - jj], [(0, 0, False), (1, 0, False)], c[0], c[1]), (carries, os))
        for b in range(QB):
            o_ref[b * T:(b + 1) * T, :] = os[b].astype(BF16)
            c_ref[b * T:(b + 1) * T, :] = jnp.where(masks[0], carries[b, 0], carries[b, 1])

    blk = pl.BlockSpec((QB * T, 2 * DH), lambda p, i: (i, p))
    full = pl.BlockSpec((S, 2 * DH), lambda p, i: (0, p))
    return _call(
        body, (q, k, v), name="attn_fwd", grid=(SBW // (2 * DH), S // (QB * T)),
        in_specs=[blk, full, full], out_specs=[blk, blk],
        out_shape=[_sds((S, SBW), BF16), _sds((S, SBW), F32)],
        compiler_params=_params(("arbitrary", "arbitrary"), 40), exchange=exchange)


def _attn_bwd(q, k, v, do, ctot, after=()):
    T = TA
    nq = S // (QB * T)

    def body(q_ref, k_ref, v_ref, do_ref, c_ref, dq_ref, dk_ref, dv_ref, dk_acc, dv_acc):
        step = pl.program_id(1)
        i2 = 2 * step

        @pl.when(step == 0)
        def _():
            dk_acc[...] = jnp.zeros_like(dk_acc)
            dv_acc[...] = jnp.zeros_like(dv_acc)

        row = lax.broadcasted_iota(jnp.int32, (T, T), 0)
        col = lax.broadcasted_iota(jnp.int32, (T, T), 1)
        upto = (row <= col).astype(BF16)
        before = (row < col).astype(BF16)
        causal = col < row
        masks = _head_masks()
        qms, doms, ctots = {}, {}, {}
        for b in range(QB):
            q2, do2 = q_ref[b * T:(b + 1) * T, :], do_ref[b * T:(b + 1) * T, :]
            for h, hm in enumerate(masks):
                qms[b, h] = jnp.where(hm, q2, jnp.zeros_like(q2))
                doms[b, h] = jnp.where(hm, do2, jnp.zeros_like(do2))
                ctots[b, h] = c_ref[b * T:(b + 1) * T, h * DH:h * DH + 1]

        def blocks(keys, pairs, sums, dqs):
            rows = [pl.ds(pl.multiple_of(j * T, T), T) for j in keys]
            ks, vs = [k_ref[r, :] for r in rows], [v_ref[r, :] for r in rows]
            kms = [[jnp.where(hm, kj, jnp.zeros_like(kj)) for hm in masks] for kj in ks]
            units = [(n, h) for n in range(len(pairs)) for h in range(2)]
            qks = {(n, h): _nt(qms[pairs[n][0], h], ks[pairs[n][1]]) for n, h in units}
            das = {(n, h): _nt(doms[pairs[n][0], h], vs[pairs[n][1]]) for n, h in units}
            lbs, l1ms = {}, {}
            for u in units:
                lbs[u], l1m = _log_terms(qks[u])
                l1ms[u] = jnp.where(causal, l1m, 0.0) if pairs[u[0]][2] else l1m
            pins = {u: _hilo_dot(l1ms[u], upto) for u in units}
            sums = dict(sums)
            a_s, dls, cps = {}, {}, {}
            for n, h in units:
                b, _, diag = pairs[n]
                cl, cp = sums[b, h]
                a = jnp.exp2(lbs[n, h] + (ctots[b, h] - cl) - pins[n, h])
                if diag:
                    a = jnp.where(causal, a, 0.0)
                a_s[n, h] = a.astype(BF16)
                dls[n, h] = das[n, h] * a
                cps[n, h] = cp
                sums[b, h] = (cl + jnp.sum(l1ms[n, h], axis=1, keepdims=True),
                              cp + jnp.sum(dls[n, h], axis=1, keepdims=True))
            pexs = {u: _hilo_dot(dls[u], before) for u in units}
            dzbs = {}
            for u in units:
                dz = dls[u] - jnp.exp2(lbs[u]) * (dls[u] + pexs[u] + cps[u])
                if pairs[u[0]][2]:
                    dz = jnp.where(causal, dz, 0.0)
                dzbs[u] = dz.astype(BF16)
            dqs = list(dqs)
            for n, h in units:
                dqs[pairs[n][0]] = dqs[pairs[n][0]] + _nn(dzbs[n, h], kms[pairs[n][1]][h])
            for key, r in enumerate(rows):
                mine = [(n, h) for n, h in units if pairs[n][1] == key]
                dk_acc[r, :] += functools.reduce(jnp.add, [_tn(dzbs[u], qms[pairs[u[0]][0], u[1]]) for u in mine])
                dv_acc[r, :] += functools.reduce(jnp.add, [_tn(a_s[u], doms[pairs[u[0]][0], u[1]]) for u in mine])
            return sums, tuple(dqs)

        zero = jnp.zeros((T, 1), F32)
        sums = {(b, h): (zero, zero) for b in range(QB) for h in range(2)}
        dqs = tuple(jnp.zeros((T, 2 * DH), F32) for _ in range(QB))
        sums, dqs = lax.fori_loop(
            0, i2, lambda j, c: blocks([j], [(0, 0, False), (1, 0, False)], c[0], c[1]), (sums, dqs))
        _, dqs = blocks([i2, i2 + 1], [(0, 0, True), (1, 0, False), (1, 1, True)], sums, dqs)
        for b in range(QB):
            dq_ref[b * T:(b + 1) * T, :] = (dqs[b] * SCALE).astype(BF16)

        @pl.when(step == nq - 1)
        def _():
            dk_ref[...] = (dk_acc[...] * SCALE).astype(BF16)
            dv_ref[...] = dv_acc[...].astype(BF16)

    blk = pl.BlockSpec((QB * T, 2 * DH), lambda p, i: (i, p))
    full = pl.BlockSpec((S, 2 * DH), lambda p, i: (0, p))
    return _call(
        body, (q, k, v, do, ctot), name="attn_bwd", grid=(SBW // (2 * DH), nq),
        in_specs=[blk, full, full, blk, blk], out_specs=[blk, full, full],
        out_shape=[_sds((S, SBW), BF16), _sds((S, SBW), BF16), _sds((S, SBW), BF16)],
        scratch_shapes=[pltpu.VMEM((S, 2 * DH), F32), pltpu.VMEM((S, 2 * DH), F32)],
        compiler_params=_params(("arbitrary", "arbitrary"), 40), after=after)


def _pool_counts(first_row, tm):
    pos = first_row + lax.broadcasted_iota(jnp.int32, (tm, 1), 0)
    return [jnp.minimum(pos + 1, w).astype(F32) for w in POOL_WINDOWS]


def _mix_out(h, xp, o_sb, gp, gs, w_group, scale, w_bp, w_ba, w_out, exchange=None):
    tm = 512

    def body(h_ref, xp_ref, o_ref, gp_ref, gs_ref, wg_hbm, sc_ref, wbp_hbm, wba_hbm, wo_hbm,
             h2_ref, pm_ref, p_ref, yp_ref, ys_ref, m_ref, halo, wg_ref, wbp_ref, wba_ref, wo_ref):
        _stage([(wg_hbm, wg_ref), (wbp_hbm, wbp_ref), (wba_hbm, wba_ref), (wo_hbm, wo_ref)])
        i = pl.program_id(0)

        @pl.when(i == 0)
        def _():
            halo[...] = jnp.zeros_like(halo)

        xp = xp_ref[...]
        ext = jnp.concatenate([halo[...], xp], axis=0)
        halo[...] = xp[tm - HALO:, :]
        counts = _pool_counts(i * tm, tm)
        for gi in range(len(POOL_WINDOWS)):
            lanes = slice(gi * PG, (gi + 1) * PG)
            win = ext[:, lanes]
            for step in range(gi + 1):
                win = win + pltpu.roll(win, 1 << step, 0)
            pm = (win[HALO:, :] / counts[gi] - xp[:, lanes]).astype(BF16)
            pm_ref[:, lanes] = pm
            p_ref[:, lanes] = (_nn(pm, wg_ref[gi]) * sc_ref[:, lanes]).astype(BF16)
        pb = p_ref[...]
        ob = o_ref[...]
        for j in range(NSH):
            cols = slice(j * (D // NSH), (j + 1) * (D // NSH))
            yp = _nn(pb, wbp_ref[j])
            ys = _nn(ob, wba_ref[j])
            yp_ref[:, cols] = yp.astype(BF16)
            ys_ref[:, cols] = ys.astype(BF16)
            m_ref[:, cols] = (gp_ref[:, cols].astype(F32) * yp + gs_ref[:, cols].astype(F32) * ys).astype(BF16)
        h2_ref[...] = h_ref[...] + _nn(m_ref[...], wo_ref[...])

    return _call(
        body, (h, xp, o_sb, gp, gs, w_group, scale, w_bp, w_ba, w_out), name="mix_out", grid=(S // tm,),
        in_specs=[_rows(tm, D), _rows(tm, PW), _rows(tm, SBW), _rows(tm, D), _rows(tm, D),
                  _ANY, _fixed((1, PW)), _ANY, _ANY, _ANY],
        out_specs=[_rows(tm, D), _rows(tm, PW), _rows(tm, PW), _rows(tm, D), _rows(tm, D), _rows(tm, D)],
        out_shape=[_sds((S, D), F32), _sds((S, PW), BF16), _sds((S, PW), BF16), _sds((S, D), BF16),
                   _sds((S, D), BF16), _sds((S, D), BF16)],
        scratch_shapes=[pltpu.VMEM((HALO, PW), F32)] + _vmem_like(w_group, w_bp, w_ba, w_out),
        compiler_params=_params(("arbitrary",), 48), free=(5, 6), exchange=exchange)


def _mix_bwd_out(dh, gp, gs, yp, ys, pm, w_group, scale, w_bp, w_ba, w_out, exchange=None):
    tm = 512
    nt = S // tm

    def body(dh_ref, gp_ref, gs_ref, yp_ref, ys_ref, pm_ref, wg_hbm, sc_ref, wbp_hbm, wba_hbm, wo_hbm,
             dlg_ref, dyp_ref, dys_ref, do_ref, dyg_ref, dxp_ref, dsc_ref, halo, wg_ref, wbp_ref, wba_ref, wo_ref):
        _stage([(wg_hbm, wg_ref), (wbp_hbm, wbp_ref), (wba_hbm, wba_ref), (wo_hbm, wo_ref)])
        step = pl.program_id(0)

        @pl.when(step == 0)
        def _():
            halo[...] = jnp.zeros_like(halo)
            dsc_ref[...] = jnp.zeros_like(dsc_ref)

        dm = _nt(dh_ref[...].astype(BF16), wo_ref[...])
        gp = gp_ref[...].astype(F32)
        gs = gs_ref[...].astype(F32)
        yp = yp_ref[...].astype(F32)
        ys = ys_ref[...].astype(F32)
        dlg_ref[:, :D] = (dm * yp * gp * (1.0 - gp)).astype(BF16)
        dlg_ref[:, D:] = (dm * ys * gs * (1.0 - gs)).astype(BF16)
        dyp_ref[...] = (dm * gp).astype(BF16)
        dys_ref[...] = (dm * gs).astype(BF16)
        dp = jnp.zeros((tm, PW), F32)
        do = jnp.zeros((tm, SBW), F32)
        for j in range(NSH):
            cols = slice(j * (D // NSH), (j + 1) * (D // NSH))
            dp = dp + _nt(dyp_ref[:, cols], wbp_ref[j])
            do = do + _nt(dys_ref[:, cols], wba_ref[j])
        do_ref[...] = do.astype(BF16)
        counts = _pool_counts((nt - 1 - step) * tm, tm)
        dscale = []
        for gi in range(len(POOL_WINDOWS)):
            lanes = slice(gi * PG, (gi + 1) * PG)
            dpg = dp[:, lanes]
            dscale.append(jnp.sum(dpg * _nn(pm_ref[:, lanes], wg_ref[gi]), axis=0, keepdims=True))
            dyg = (dpg * sc_ref[:, lanes]).astype(BF16)
            dyg_ref[:, lanes] = dyg
            dpm = _nt(dyg, wg_ref[gi])
            per = dpm / counts[gi]
            win = jnp.concatenate([per, halo[:, lanes]], axis=0)
            halo[:, lanes] = per[:HALO, :]
            for s in range(gi + 1):
                win = win + pltpu.roll(win, tm + HALO - (1 << s), 0)
            dxp_ref[:, lanes] = (win[:tm, :] - dpm).astype(BF16)
        dsc_ref[...] += jnp.concatenate(dscale, axis=1)

    rev = lambda width: pl.BlockSpec((tm, width), lambda i: (nt - 1 - i, 0))
    return _call(
        body, (dh, gp, gs, yp, ys, pm, w_group, scale, w_bp, w_ba, w_out), name="mix_bwd_out", grid=(nt,),
        in_specs=[rev(D), rev(D), rev(D), rev(D), rev(D), rev(PW), _ANY, _fixed((1, PW)), _ANY, _ANY, _ANY],
        out_specs=[rev(2 * D), rev(D), rev(D), rev(SBW), rev(PW), rev(PW), _fixed((1, PW))],
        out_shape=[_sds((S, 2 * D), BF16), _sds((S, D), BF16), _sds((S, D), BF16), _sds((S, SBW), BF16),
                   _sds((S, PW), BF16), _sds((S, PW), BF16), _sds((1, PW), F32)],
        scratch_shapes=[pltpu.VMEM((HALO, PW), F32)] + _vmem_like(w_group, w_bp, w_ba, w_out),
        compiler_params=_params(("arbitrary",), 48), exchange=exchange)


def _mix_bwd_in(dh, h, gain, pieces, w_in, exchange=None):
    tm = 512
    widths = [p.shape[1] for p in pieces]

    def body(dh_ref, h_ref, g_ref, *rest):
        piece_refs, (w_hbm, dx_ref, dg_ref, dp_ref, w_ref) = rest[:len(pieces)], rest[len(pieces):]
        _stage([(w_hbm, w_ref)])
        at = 0
        for ref, width in zip(piece_refs, widths):
            dp_ref[:, at:at + width] = ref[...]
            at += width
        du = jnp.zeros((tm, D), F32)
        for j in range(NSH):
            du = du + _nt(dp_ref[:, j * D:(j + 1) * D], w_ref[j])
        r, hr = _rms(h_ref[...])
        dx, dgain = _rms_bwd(du, hr, r, g_ref[...])
        dx_ref[...] = dh_ref[...] + dx

        @pl.when(pl.program_id(0) == 0)
        def _():
            dg_ref[...] = jnp.zeros_like(dg_ref)

        dg_ref[...] += dgain

    return _call(
        body, (dh, h, gain, *pieces, w_in), name="mix_bwd_in", grid=(S // tm,),
        in_specs=[_rows(tm, D), _rows(tm, D), _fixed((1, D))] + [_rows(tm, w) for w in widths] + [_ANY],
        out_specs=[_rows(tm, D), _fixed((1, D)), _rows(tm, 4 * D)],
        out_shape=[_sds((S, D), F32), _sds((1, D), F32), _sds((S, 4 * D), BF16)],
        scratch_shapes=_vmem_like(w_in),
        compiler_params=_params(("arbitrary",), 48), exchange=exchange)


def _wgrad(a, b, nblk, ti, name, out_dtype=BF16, exchange=None, after=()):
    ka, n = a.shape[1], b.shape[1]
    ns = n // nblk

    def body(a_ref, b_ref, o_ref):
        o_ref[...] = _tn(a_ref[...].astype(BF16), b_ref[...].astype(BF16)).astype(out_dtype)

    res = _call(
        body, (a, b), name=name, grid=(nblk, ka // ti),
        in_specs=[pl.BlockSpec((S, ti), lambda j, i: (0, i)), pl.BlockSpec((S, ns), lambda j, i: (0, j))],
        out_specs=[pl.BlockSpec((None, ti, ns), lambda j, i: (j, i, 0))],
        out_shape=[_sds((nblk, ka, ns), out_dtype)],
        compiler_params=_params(("arbitrary", "arbitrary"), 56), exchange=exchange, after=after)
    return res[0] if exchange is None else (res[0][0], res[1])


def _wgrad_groups(pm, dyg):
    def body(a_ref, b_ref, o_ref):
        o_ref[...] = _tn(a_ref[...], b_ref[...])

    col = pl.BlockSpec((S, PG), lambda g: (0, g))
    return pl.pallas_call(
        body, name="wgrad_groups", grid=(PW // PG,),
        in_specs=[col, col], out_specs=pl.BlockSpec((None, PG, PG), lambda g: (g, 0, 0)),
        out_shape=_sds((PW // PG, PG, PG), F32),
        compiler_params=_params(("arbitrary",), 32),
    )(*_in_hbm([pm, dyg]))


def _place():
    x, y, c = lax.axis_index("x"), lax.axis_index("y"), lax.axis_index("c")
    chips = [(1 - x, y), (x, 1 - y), (1 - x, 1 - y)]
    return x, y, c, chips


def _remote(src, dst, ssem, rsem, dev):
    return pltpu.make_async_remote_copy(src_ref=src, dst_ref=dst, send_sem=ssem, recv_sem=rsem,
                                        device_id=dev, device_id_type=MESH)


def _cast_into_block(w, me_idx, name):
    rows, cols = w.shape
    tr = _row_block(rows)

    def body(me_ref, w_ref, o_ref):
        o_ref[...] = w_ref[...].astype(BF16)

    return pl.pallas_call(
        body, name=name, out_shape=_sds((NSH, rows, cols), BF16),
        grid_spec=pltpu.PrefetchScalarGridSpec(
            num_scalar_prefetch=1, grid=(rows // tr,),
            in_specs=[pl.BlockSpec((tr, cols), lambda r, me: (r, 0))],
            out_specs=pl.BlockSpec((None, tr, cols), lambda r, me: (me[0], r, 0))),
        compiler_params=_params(("arbitrary",), 32),
    )(me_idx, w)


def _ex_gather(bufs):
    n = len(bufs)
    per = 8

    def plan(outs, ssem, rsem, w):
        x, y, c, _ = _place()
        sib, nbr_x, nbr_y = (x, y, 1 - c), (1 - x, y, c), (x, 1 - y, c)
        half = outs[w].shape[1] // 2
        quarter = half // 2
        sem = lambda k: (ssem.at[per * w + k], rsem.at[per * w + k])
        rows = lambda blk, start, size: outs[w].at[blk, pl.ds(start, size)]
        mine = rows(2 * x + y, c * half, half)
        from_x = rows(2 * (1 - x) + y, c * half, half)
        from_y = rows(2 * x + (1 - y), c * half, half)
        diag = 2 * (1 - x) + (1 - y)
        pass_y = rows(2 * (1 - x) + y, c * half, quarter)
        pass_x = rows(2 * x + (1 - y), c * half + quarter, quarter)
        diag_0, diag_1 = rows(diag, c * half, quarter), rows(diag, c * half + quarter, quarter)
        first = [_remote(mine, mine, *sem(0), nbr_x), _remote(mine, mine, *sem(1), nbr_y)]
        arrivals = [
            (_remote(from_x, from_x, *sem(0), nbr_x),
             [_remote(pass_y, pass_y, *sem(2), nbr_y), _remote(from_x, from_x, *sem(4), sib)]),
            (_remote(from_y, from_y, *sem(1), nbr_y),
             [_remote(pass_x, pass_x, *sem(3), nbr_x), _remote(from_y, from_y, *sem(5), sib)]),
            (_remote(diag_0, diag_0, *sem(2), nbr_y), [_remote(diag_0, diag_0, *sem(6), sib)]),
            (_remote(diag_1, diag_1, *sem(3), nbr_x), [_remote(diag_1, diag_1, *sem(7), sib)]),
        ]
        other = (1 - c) * half
        from_sibling = [
            _remote(rows(2 * (1 - x) + y, other, half), rows(2 * (1 - x) + y, other, half), *sem(4), sib),
            _remote(rows(2 * x + (1 - y), other, half), rows(2 * x + (1 - y), other, half), *sem(5), sib),
            _remote(rows(diag, other, quarter), rows(diag, other, quarter), *sem(6), sib),
            _remote(rows(diag, other + quarter, quarter), rows(diag, other + quarter, quarter), *sem(7), sib),
        ]
        return first, arrivals, from_sibling

    def start(ins, outs, ssem, rsem):
        x, y, c, _ = _place()
        for w in range(n):
            half = outs[w].shape[1] // 2
            mine = outs[w].at[2 * x + y, pl.ds(c * half, half)]
            _remote(mine, mine, ssem.at[per * w], rsem.at[per * w], (1 - x, y, c)).start()
            _remote(mine, mine, ssem.at[per * w + 1], rsem.at[per * w + 1], (x, 1 - y, c)).start()

    def finish(ins, outs, ssem, rsem):
        plans = [plan(outs, ssem, rsem, w) for w in range(n)]
        started = []
        for direct in (True, False):
            for first, arrivals, _ in plans:
                for arrived, onward in (arrivals[:2] if direct else arrivals[2:]):
                    arrived.wait_recv()
                    for cp in onward:
                        cp.start()
                    started += onward
        for first, _, from_sibling in plans:
            for cp in from_sibling:
                cp.wait_recv()
            started += first
        for cp in started:
            cp.wait_send()

    return Exchange(bufs, [_sds(b.shape, b.dtype) for b in bufs], {w: w for w in range(n)}, per * n, start, finish)


def _ex_gather_direct(bufs):
    n = len(bufs)

    def copies(outs, ssem, rsem, only_first=False):
        x, y, c, chips = _place()
        me, sib = 2 * x + y, (x, y, 1 - c)
        first, relay, last = [], [], []
        for w in range(n):
            half = outs[w].shape[1] // 2
            mine = outs[w].at[me, pl.ds(c * half, half)]
            for k, (px, py) in enumerate(chips):
                sems = (ssem.at[6 * w + k], rsem.at[6 * w + k])
                sib_sems = (ssem.at[6 * w + 3 + k], rsem.at[6 * w + 3 + k])
                first.append(_remote(mine, mine, *sems, (px, py, c)))
                if only_first:
                    continue
                got = outs[w].at[2 * px + py, pl.ds(c * half, half)]
                relay.append((_remote(got, got, *sems, (px, py, c)), _remote(got, got, *sib_sems, sib)))
                theirs = outs[w].at[2 * px + py, pl.ds((1 - c) * half, half)]
                last.append(_remote(theirs, theirs, *sib_sems, sib))
        return first, relay, last

    def start(ins, outs, ssem, rsem):
        for cp in copies(outs, ssem, rsem, only_first=True)[0]:
            cp.start()

    def finish(ins, outs, ssem, rsem):
        first, relay, last = copies(outs, ssem, rsem)
        for arrived, onward in relay:
            arrived.wait_recv()
            onward.start()
        for cp in last:
            cp.wait_recv()
        for cp in first:
            cp.wait_send()
        for _, onward in relay:
            onward.wait_send()

    return Exchange(bufs, [_sds(b.shape, b.dtype) for b in bufs], {w: w for w in range(n)}, 6 * n, start, finish)


def _simple_exchange(arrays, landing, aliases, make_copies, sibling_only=False):
    def start(ins, outs, ssem, rsem):
        for cp, _ in make_copies(ins, outs, ssem, rsem, False):
            cp.start()

    def finish(ins, outs, ssem, rsem):
        cps = make_copies(ins, outs, ssem, rsem, True)
        for _, landed in cps:
            landed.wait_recv()
        for cp, _ in cps:
            cp.wait_send()

    return Exchange(arrays, landing, aliases, len(arrays) * 3, start, finish, sibling_only)


def _ex_pair_swap(grads):
    def make(ins, outs, ssem, rsem, landing):
        x, y, c, _ = _place()
        cps = [_remote(ins[w].at[:, 1 - c], outs[w], ssem.at[w], rsem.at[w], (x, y, 1 - c))
               for w in range(len(grads))]
        return [(cp, cp) for cp in cps]

    return _simple_exchange(grads, [_sds((NSH,) + g.shape[2:], g.dtype) for g in grads], {}, make, True)


def _ex_scatter(parts):
    def make(ins, outs, ssem, rsem, landing):
        x, y, c, chips = _place()
        out = []
        for w in range(len(parts)):
            for k, (px, py) in enumerate(chips):
                sems = (ssem.at[3 * w + k], rsem.at[3 * w + k])
                out.append((_remote(ins[w].at[2 * px + py], outs[w].at[k], *sems, (px, py, c)),
                            _remote(outs[w].at[k], outs[w].at[k], *sems, (px, py, c)) if landing else None))
        return out

    return _simple_exchange(parts, [_sds((3,) + p.shape[1:], p.dtype) for p in parts], {}, make)


def _ex_relay(bufs):
    def make(ins, outs, ssem, rsem, landing):
        x, y, c, chips = _place()
        sib = (x, y, 1 - c)
        out = []
        for w in range(len(bufs)):
            half = outs[w].shape[1] // 2
            for k, (px, py) in enumerate(chips):
                sems = (ssem.at[3 * w + k], rsem.at[3 * w + k])
                have = outs[w].at[2 * px + py, pl.ds(c * half, half)]
                miss = outs[w].at[2 * px + py, pl.ds((1 - c) * half, half)]
                out.append((_remote(have, have, *sems, sib), _remote(miss, miss, *sems, sib) if landing else None))
        return out

    return _simple_exchange(bufs, [_sds(b.shape, b.dtype) for b in bufs], {w: w for w in range(len(bufs))}, make, True)


def _ex_share(bufs):
    def make(ins, outs, ssem, rsem, landing):
        x, y, c, _ = _place()
        sib = (x, y, 1 - c)
        return [(_remote(outs[w].at[c], outs[w].at[c], ssem.at[w], rsem.at[w], sib),
                 _remote(outs[w].at[1 - c], outs[w].at[1 - c], ssem.at[w], rsem.at[w], sib) if landing else None)
                for w in range(len(bufs))]

    return _simple_exchange(bufs, [_sds(b.shape, b.dtype) for b in bufs], {w: w for w in range(len(bufs))}, make, True)


def _small_copies(slots, ssems, rsems, sending):
    x, y, c, _ = _place()
    out = []
    for m in range(1, 8):
        px, py, pc = x ^ (m >> 2), y ^ ((m >> 1) & 1), c ^ (m & 1)
        slot = slots.at[4 * x + 2 * y + c if sending else 4 * px + 2 * py + pc]
        out.append(_remote(slot, slot, ssems[m - 1], rsems[m - 1], (px, py, pc)))
    return out


def _small_gather_start(slots, name):
    def body(*refs):
        for cp in _small_copies(refs[0], refs[1:8], refs[8:15], True):
            cp.start()
        refs[-1][...] = jnp.zeros_like(refs[-1])

    outs = pl.pallas_call(
        body, name=name,
        out_shape=([pltpu.SemaphoreType.DMA(())] * 14 + [pltpu.HBM(slots.shape, slots.dtype)]
                   + [jax.ShapeDtypeStruct((8, 128), F32)]),
        in_specs=[_HBM], out_specs=[_SEM] * 14 + [_HBM, _VM], input_output_aliases={0: 14},
        compiler_params=pltpu.CompilerParams(has_side_effects=_EFFECT),
    )(*_in_hbm([slots]))
    return outs[:14], outs[14], outs[15]


def _small_gather_wait(sems, slots, after, name):
    def body(*refs):
        for cp in _small_copies(refs[0], refs[1:8], refs[8:15], True):
            cp.wait_send()
        for cp in _small_copies(refs[0], refs[1:8], refs[8:15], False):
            cp.wait_recv()

    return pl.pallas_call(
        body, name=name, out_shape=pltpu.HBM(slots.shape, slots.dtype),
        in_specs=[_HBM] + [_SEM] * 14 + [_ANY] * len(after), out_specs=_HBM, input_output_aliases={0: 0},
        compiler_params=pltpu.CompilerParams(has_side_effects=_EFFECT),
    )(slots, *sems, *after)


def _row_block(rows, cap=256):
    return max(t for t in range(16, cap + 1, 16) if rows % t == 0)


def _pair_sum(grad, got, c_idx, name):
    _, _, half, cols = grad.shape
    tr = _row_block(half, 512)

    def body(c_ref, a_ref, b_ref, o_ref):
        o_ref[...] = (a_ref[...].astype(F32) + b_ref[...].astype(F32)).astype(BF16)

    return pl.pallas_call(
        body, name=name, out_shape=_sds((NSH, half, cols), BF16),
        grid_spec=pltpu.PrefetchScalarGridSpec(
            num_scalar_prefetch=1, grid=(NSH, half // tr),
            in_specs=[pl.BlockSpec((None, None, tr, cols), lambda j, r, c: (j, c[0], r, 0)),
                      pl.BlockSpec((None, tr, cols), lambda j, r, c: (j, r, 0))],
            out_specs=pl.BlockSpec((None, tr, cols), lambda j, r, c: (j, r, 0))),
        compiler_params=_params(("arbitrary", "arbitrary"), 32),
    )(c_idx, *_in_hbm([grad, got]))


def _chip_sum(own, got, place, name):
    _, half, cols = own.shape
    tr = _row_block(half, 512)

    def body(place_ref, own_ref, got_ref, o_ref):
        acc = own_ref[...].astype(F32)
        for k in range(3):
            acc = acc + got_ref[k].astype(F32)
        o_ref[...] = acc

    return pl.pallas_call(
        body, name=name, out_shape=_sds((2, half, cols), F32),
        grid_spec=pltpu.PrefetchScalarGridSpec(
            num_scalar_prefetch=1, grid=(half // tr,),
            in_specs=[pl.BlockSpec((None, tr, cols), lambda r, p: (p[0], r, 0)),
                      pl.BlockSpec((3, tr, cols), lambda r, p: (0, r, 0))],
            out_specs=pl.BlockSpec((None, tr, cols), lambda r, p: (p[1], r, 0))),
        compiler_params=_params(("arbitrary",), 32),
    )(place, *_in_hbm([own, got]))


def _adamw_math(w, g, m, v):
    m = B1 * m + (1.0 - B1) * g
    v = B2 * v + (1.0 - B2) * (g * g)
    m_hat = m / (1.0 - B1 ** STEP)
    v_hat = v / (1.0 - B2 ** STEP)
    return -LR * (m_hat / (jnp.sqrt(v_hat) + AEPS) + WD * w), m, v


def _adamw(w, g, m, v, name, after=()):
    rows, cols = w.shape
    tr = _row_block(rows)

    def body(w_ref, g_ref, m_ref, v_ref, go_ref, d_ref, nm_ref, nv_ref):
        g = g_ref[...]
        go_ref[...] = g
        d_ref[...], nm_ref[...], nv_ref[...] = _adamw_math(w_ref[...], g, m_ref[...], v_ref[...])

    blk = pl.BlockSpec((tr, cols), lambda r: (r, 0))
    return _call(
        body, (w, g, m, v), name=name, grid=(rows // tr,), out_shape=[_sds(w.shape, F32)] * 4,
        in_specs=[blk] * 4, out_specs=[blk] * 4,
        compiler_params=_params(("arbitrary",), 32), free=(0, 2, 3), after=after)


def _small_update(gathered, w, m, v):
    rows = w.shape[0]

    def body(ga_ref, w_ref, m_ref, v_ref, *out_refs):
        g = ga_ref[0:rows, :]
        for dev in range(1, 8):
            g = g + ga_ref[dev * rows:(dev + 1) * rows, :]
        results = (g,) + _adamw_math(w_ref[...], g, m_ref[...], v_ref[...])
        for i, res in enumerate(results):
            out_refs[i][...] = res[:SMALL_HEAD, :]
            out_refs[4 + i][...] = res[SMALL_HEAD:, :]

    outs = pl.pallas_call(
        body, name="small_update",
        out_shape=[jax.ShapeDtypeStruct((SMALL_HEAD, 128), F32)] * 4
        + [jax.ShapeDtypeStruct((rows - SMALL_HEAD, 128), F32)] * 4,
        in_specs=[_VM] * 4, out_specs=[_VM] * 8,
    )(gathered, w, m, v)
    return outs[:4], outs[4:]


SMALL = ("ffn1_norm", "mix_norm", "ffn2_norm", "final_norm", "pool_scale", "loss", "pool_w_group")
SMALL_HEAD = 48
BIG = ("ffn1_w_gate_up", "ffn1_w_down", "w_in", "w_branch_pool", "w_branch_attn", "w_out",
       "ffn2_w_gate_up", "ffn2_w_down")
ORDER = ("ffn1_norm", "ffn1_w_gate_up", "ffn1_w_down", "mix_norm", "w_in", "pool_w_group", "pool_scale",
         "w_branch_pool", "w_branch_attn", "w_out", "ffn2_norm", "ffn2_w_gate_up", "ffn2_w_down", "final_norm")
SMALL_ROWS = 560


def _pack_small(t):
    parts = []
    for k in SMALL:
        rows = t[k].reshape(-1, 128) if k in t else jnp.zeros((1, 128), F32)
        parts.append(jnp.pad(rows, ((0, -rows.shape[0] % 8), (0, 0))))
    packed = jnp.concatenate(parts, axis=0)
    assert packed.shape == (SMALL_ROWS, 128), packed.shape
    return packed


def _unpack_small(head, group, like):
    out, at = {"pool_w_group": group.reshape(like["pool_w_group"].shape)}, 0
    for k in SMALL[:-1]:
        n = like[k].size // 128 if k in like else 1
        out[k] = head[at:at + n].reshape(like[k].shape) if k in like else head[at, 0]
        at += n + (-n % 8)
    return out


def _halves(g):
    return g.reshape(NSH, 2, g.shape[1] // 2, g.shape[2])


def kernel(x, ffn1_norm, ffn1_w_gate_up, ffn1_w_down, mix_norm, w_in, pool_w_group, pool_scale, w_branch_pool, w_branch_attn, w_out, ffn2_norm, ffn2_w_gate_up, ffn2_w_down, final_norm, loss_target, m_ffn1_norm, m_ffn1_w_gate_up, m_ffn1_w_down, m_mix_norm, m_w_in, m_pool_w_group, m_pool_scale, m_w_branch_pool, m_w_branch_attn, m_w_out, m_ffn2_norm, m_ffn2_w_gate_up, m_ffn2_w_down, m_final_norm, v_ffn1_norm, v_ffn1_w_gate_up, v_ffn1_w_down, v_mix_norm, v_w_in, v_pool_w_group, v_pool_scale, v_w_branch_pool, v_w_branch_attn, v_w_out, v_ffn2_norm, v_ffn2_w_gate_up, v_ffn2_w_down, v_final_norm):
    wts = dict(ffn1_norm=ffn1_norm, ffn1_w_gate_up=ffn1_w_gate_up, ffn1_w_down=ffn1_w_down, mix_norm=mix_norm,
               w_in=w_in, pool_w_group=pool_w_group, pool_scale=pool_scale, w_branch_pool=w_branch_pool,
               w_branch_attn=w_branch_attn, w_out=w_out, ffn2_norm=ffn2_norm, ffn2_w_gate_up=ffn2_w_gate_up,
               ffn2_w_down=ffn2_w_down, final_norm=final_norm)
    mom = dict(ffn1_norm=m_ffn1_norm, ffn1_w_gate_up=m_ffn1_w_gate_up, ffn1_w_down=m_ffn1_w_down,
               mix_norm=m_mix_norm, w_in=m_w_in, pool_w_group=m_pool_w_group, pool_scale=m_pool_scale,
               w_branch_pool=m_w_branch_pool, w_branch_attn=m_w_branch_attn, w_out=m_w_out,
               ffn2_norm=m_ffn2_norm, ffn2_w_gate_up=m_ffn2_w_gate_up, ffn2_w_down=m_ffn2_w_down,
               final_norm=m_final_norm)
    var = dict(ffn1_norm=v_ffn1_norm, ffn1_w_gate_up=v_ffn1_w_gate_up, ffn1_w_down=v_ffn1_w_down,
               mix_norm=v_mix_norm, w_in=v_w_in, pool_w_group=v_pool_w_group, pool_scale=v_pool_scale,
               w_branch_pool=v_w_branch_pool, w_branch_attn=v_w_branch_attn, w_out=v_w_out,
               ffn2_norm=v_ffn2_norm, ffn2_w_gate_up=v_ffn2_w_gate_up, ffn2_w_down=v_ffn2_w_down,
               final_norm=v_final_norm)

    c_idx = lax.axis_index("c").astype(jnp.int32).reshape(1)
    me_idx = (2 * lax.axis_index("x") + lax.axis_index("y")).astype(jnp.int32).reshape(1)
    place = jnp.concatenate([me_idx, c_idx])
    x0, tgt = x[0], loss_target[0]
    wgrp = pool_w_group[0].astype(BF16)
    g1, gm, g2, gf = ffn1_norm, mix_norm, ffn2_norm, final_norm.reshape(1, D)
    grad, delta, new_m, new_v = {}, {}, {}, {}

    def pair_sums(keys, parts, got):
        return [_pair_sum(parts[i], got[i], c_idx, "pair_sum_" + k) for i, k in enumerate(keys)]

    def chip_sums(keys, chip_parts, owned):
        return [_chip_sum(chip_parts[i], owned[i], place, "chip_sum_" + k) for i, k in enumerate(keys)]

    def adamw(k, after=()):
        outs = _adamw(wts[k][0], grad[k][0], mom[k][0], var[k][0], "adamw_" + k, after=after)
        grad[k], delta[k], new_m[k], new_v[k] = (o.reshape(wts[k].shape) for o in outs)

    own = {k: _cast_into_block(wts[k][0], me_idx, "cast_" + k) for k in BIG}
    first, late = ("ffn1_w_gate_up", "ffn1_w_down"), ("w_branch_pool", "w_branch_attn", "w_out",
                                                       "ffn2_w_gate_up", "ffn2_w_down")
    full = dict(zip(first, _exchange_alone(_ex_gather([own[k] for k in first]), "gather_ffn1")))
    wgu1, wd1 = full["ffn1_w_gate_up"], full["ffn1_w_down"].reshape(DFF, D)
    (h1, n1, gu1, a1), (win,) = _ffn_fwd(x0, g1, wgu1, wd1, "ffn1_fwd", exchange=_ex_gather_direct([own["w_in"]]))
    sems_l, thru_l, token_l = _gather_start([own[k_] for k_ in late], [h1], "gather_late_start")
    u, xp, q, k, v, gp, gs = _mix_in(h1, gm, win, after=(token_l,))
    o_sb, ctot = _attn_fwd(q, k, v)
    arrived = _gather_wait(sems_l, thru_l, [o_sb], "gather_late_wait")
    wbp, wba, wout = _exchange_alone(_ex_relay(arrived[:3]), "relay_mix")
    wout = wout.reshape(D, D)
    (h2, pm, p, yp, ys, mm), (wgu2, wd2) = _mix_out(h1, xp, o_sb, gp, gs, wgrp, pool_scale, wbp, wba, wout,
                                                    exchange=_ex_relay(arrived[3:]))
    wd2 = wd2.reshape(DFF, D)
    dh3, loss_row, d_gf, n3, gu3, a3 = _ffn_fwd(h2, g2, wgu2, wd2, "ffn2_fwd", head=(tgt, gf))

    def grad_gate_up(n, dgu, name, exchange=None):
        res = _wgrad(n, dgu, NSH, D, name, exchange=exchange)
        return [_halves(res)] if exchange is None else ([_halves(res[0])], res[1])

    def grad_down(a, dh, name, exchange=None):
        res = _wgrad(a, dh, 1, FFS, name, exchange=exchange)
        halves = lambda g: [_halves(g.reshape(NSH, DFF // NSH, D))]
        return halves(res) if exchange is None else (halves(res[0]), res[1])

    k_gu2, k_d2, k_gu1, k_d1, k_in = (("ffn2_w_gate_up",), ("ffn2_w_down",), ("ffn1_w_gate_up",),
                                      ("ffn1_w_down",), ("w_in",))
    dh2, dgu3, d_g2 = _ffn_bwd(dh3, h2, g2, gu3, wgu2, wd2, "ffn2_bwd")
    pa = grad_gate_up(n3, dgu3, "wgrad_gu2") + grad_down(a3, dh3, "wgrad_d2")
    (dlg, dyp, dys, do_sb, dyg, dxp, d_scale), got_a = _mix_bwd_out(
        dh2, gp, gs, yp, ys, pm, wgrp, pool_scale, wbp, wba, wout, exchange=_ex_pair_swap(pa))
    chip_a = pair_sums(k_gu2 + k_d2, pa, got_a)
    kb = ("w_out", "w_branch_pool", "w_branch_attn")
    pb = [_halves(_wgrad(mm, dh2, 1, D, "wgrad_out").reshape(NSH, D // NSH, D)),
          _halves(_wgrad(p, dyp, NSH, PW, "wgrad_bp")), _halves(_wgrad(o_sb, dys, NSH, SBW, "wgrad_ba"))]
    k_a, k_in = k_gu2 + k_d2, k_in + kb
    sems_a, thru_a, token_a = _scatter_start(chip_a, "scatter_a_start")
    dq, dk, dv = _attn_bwd(q, k, v, do_sb, ctot, after=(token_a,))
    chip_a, owned_a = _scatter_wait(sems_a, thru_a, [dq], "scatter_a_wait")
    halves_a = chip_sums(k_a, chip_a, owned_a)
    (dh1, d_gm, dproj), both_a = _mix_bwd_in(dh2, h1, gm, (dxp, dq, dk, dv, dlg), win, exchange=_ex_share(halves_a))
    for i, k_ in enumerate(k_a):
        grad[k_] = both_a[i].reshape(wts[k_].shape)

    p_in = [_halves(_wgrad(u, dproj, NSH, D, "wgrad_in"))] + pb
    p_d1, got_in = grad_down(a1, dh1, "wgrad_d1", exchange=_ex_pair_swap(p_in))
    sems_in, thru_in, token_in = _scatter_start(pair_sums(k_in, p_in, got_in), "scatter_in_start")
    dgu1, got_d1 = _ffn_bwd_act(dh1, gu1, wd1, "ffn1_bwd_act", exchange=_ex_pair_swap(p_d1), after=(token_in,))
    sems_d1, thru_d1, token_d1 = _scatter_start(pair_sums(k_d1, p_d1, got_d1), "scatter_d1_start")
    p_gu1 = [_halves(_wgrad(n1, dgu1, NSH, D, "wgrad_gu1", after=(token_in, token_d1)))]
    chip_in, owned_in = _scatter_wait(sems_in, thru_in, p_gu1, "scatter_in_wait")
    chip_d1, owned_d1 = _scatter_wait(sems_d1, thru_d1, p_gu1, "scatter_d1_wait")
    halves_in, halves_d1 = chip_sums(k_in, chip_in, owned_in), chip_sums(k_d1, chip_d1, owned_d1)
    landed = _exchange_alone(_join(_ex_pair_swap(p_gu1), _ex_share(halves_in)), "pair_swap_gu1")
    for i, k_ in enumerate(k_in):
        grad[k_] = landed[1 + i].reshape(wts[k_].shape)
    sems, thru, token = _scatter_start(pair_sums(k_gu1, p_gu1, landed[:1]), "scatter_gu1_start")
    for k_ in k_a + k_in:
        adamw(k_, after=(token,))
    dx, d_g1 = _ffn_bwd_in(dh1, x0, g1, dgu1, wgu1, "ffn1_bwd_in", after=(token,))
    small_g = dict(ffn1_norm=d_g1, mix_norm=d_gm, ffn2_norm=d_g2, final_norm=d_gf, pool_scale=d_scale,
                   pool_w_group=_wgrad_groups(pm, dyg), loss=loss_row)
    dev = 4 * lax.axis_index("x") + 2 * lax.axis_index("y") + lax.axis_index("c")
    slots = lax.dynamic_update_slice(jnp.zeros((8, SMALL_ROWS, 128), F32), _pack_small(small_g)[None], (dev, 0, 0))
    sems_s, slots, token_s = _small_gather_start(slots, "small_gather_start")

    chip_gu1, owned_gu1 = _scatter_wait(sems, thru, [dx] + [delta[k_] for k_ in k_a + k_in], "scatter_gu1_wait")
    both = _exchange_alone(_ex_share(halves_d1 + chip_sums(k_gu1, chip_gu1, owned_gu1)), "share_last",
                           after=(token_s,))
    grad["ffn1_w_down"] = both[0].reshape(ffn1_w_down.shape)
    grad["ffn1_w_gate_up"] = both[1].reshape(ffn1_w_gate_up.shape)
    for k_ in k_d1 + k_gu1:
        adamw(k_, after=(token_s,))
    gathered = _small_gather_wait(sems_s, slots, [delta[k_] for k_ in k_d1 + k_gu1], "small_gather_wait")
    gathered = gathered.reshape(8 * SMALL_ROWS, 128)
    heads, groups = _small_update(gathered, _pack_small(wts), _pack_small(mom), _pack_small(var))
    for dst, head, group in zip((grad, delta, new_m, new_v), heads, groups):
        vals = _unpack_small(head, group, wts)
        if dst is grad:
            loss = vals["loss"]
        vals.pop("loss")
        dst.update(vals)
    return (loss, dx[None], *[grad[k_] for k_ in ORDER], *[delta[k_] for k_ in ORDER],
            *[new_m[k_] for k_ in ORDER], *[new_v[k_] for k_ in ORDER])
```

```python
import dataclasses
import functools

import jax
import jax.numpy as jnp
from jax import lax
from jax.experimental import pallas as pl
from jax.experimental.pallas import tpu as pltpu

F32 = jnp.float32
BF16 = jnp.bfloat16

S = 2048
D = 1024
DFF = 2816
FFS = 2 * DFF // 4
NSH = 4
PW = 512
PG = 128
POOL_WINDOWS = (2, 4, 8, 16)
HALO = 16
SBW = 512
DH = 64
EPS = 1e-6
SCALE = 0.125
LOG2E = 1.4426950408889634
TA = 256
QB = 2
MIB = 1024 * 1024

LR, B1, B2, AEPS, WD, STEP = 0.001, 0.9, 0.999, 1e-08, 0.01, 10

_VM = pl.BlockSpec(memory_space=pltpu.VMEM)
_ANY = pl.BlockSpec(memory_space=pl.ANY)
MESH = pl.DeviceIdType.MESH
SIBLING_PAIR_ID = 1


def _nn(a, b):
    return jnp.dot(a, b, preferred_element_type=F32)


def _nt(a, b):
    return lax.dot_general(a, b, (((1,), (1,)), ((), ())), preferred_element_type=F32)


def _tn(a, b):
    return lax.dot_general(a, b, (((0,), (0,)), ((), ())), preferred_element_type=F32)


def _params(sem, vmem_mib):
    return pltpu.CompilerParams(dimension_semantics=sem, vmem_limit_bytes=vmem_mib * MIB)


def _rows(tm, width):
    return pl.BlockSpec((tm, width), lambda i: (i, 0))


def _fixed(shape):
    return pl.BlockSpec(shape, lambda *_: (0,) * len(shape))


def _sds(shape, dtype):
    return pltpu.HBM(shape, dtype)


def _in_hbm(args):
    return [pltpu.with_memory_space_constraint(a, pltpu.HBM) for a in args]


def _stage(pairs):
    @pl.when(pl.program_id(0) == 0)
    def _():
        for src, dst in pairs:
            pltpu.sync_copy(src, dst)


def _vmem_like(*arrays):
    return [pltpu.VMEM(a.shape, a.dtype) for a in arrays]


class Exchange:
    def __init__(self, arrays, landing, aliases, n_sems, start, finish, sibling_only=False):
        self.arrays, self.landing, self.aliases, self.n_sems = list(arrays), list(landing), dict(aliases), n_sems
        self.start, self.finish = start, finish
        self.sibling_only = sibling_only

    def enter(self):
        if self.sibling_only:
            barrier = pltpu.get_barrier_semaphore()
            sibling = (lax.axis_index("x"), lax.axis_index("y"), 1 - lax.axis_index("c"))
            pl.semaphore_signal(barrier, inc=1, device_id=sibling, device_id_type=MESH)
            pl.semaphore_wait(barrier, 1)

    def params(self, compiler_params=None):
        kw = dict(collective_id=SIBLING_PAIR_ID) if self.sibling_only else {}
        if compiler_params is None:
            return pltpu.CompilerParams(**kw)
        return dataclasses.replace(compiler_params, **kw)


def _join(a, b):
    na, la = len(a.arrays), len(a.landing)

    def both(fa, fb):
        def run(ins, outs, ssem, rsem):
            fa(ins[:na], outs[:la], ssem.at[pl.ds(0, a.n_sems)], rsem.at[pl.ds(0, a.n_sems)])
            fb(ins[na:], outs[la:], ssem.at[pl.ds(a.n_sems, b.n_sems)], rsem.at[pl.ds(a.n_sems, b.n_sems)])
        return run

    aliases = {**a.aliases, **{na + i: la + j for i, j in b.aliases.items()}}
    return Exchange(a.arrays + b.arrays, a.landing + b.landing, aliases, a.n_sems + b.n_sems,
                    both(a.start, b.start), both(a.finish, b.finish), a.sibling_only and b.sibling_only)


def _call(body, args, *, name, grid, in_specs, out_specs, out_shape, scratch_shapes=(), compiler_params=None,
          exchange=None, free=(), after=()):
    args = [a if i in free else pltpu.with_memory_space_constraint(a, pltpu.HBM) for i, a in enumerate(args)]
    if exchange is None:
        n_in = len(in_specs)

        def plain(*refs):
            body(*refs[:n_in], *refs[n_in + len(after):])

        return pl.pallas_call(plain, name=name, grid=grid, in_specs=list(in_specs) + [_ANY] * len(after),
                              out_specs=out_specs, out_shape=out_shape, scratch_shapes=list(scratch_shapes),
                              compiler_params=compiler_params)(*args, *after)
    ex = exchange
    n_in, n_out, n_scr = len(in_specs), len(out_specs), len(scratch_shapes)
    na, nl = len(ex.arrays), len(ex.landing)

    def hosted(*refs):
        at = [0]

        def take(n):
            at[0] += n
            return refs[at[0] - n:at[0]]

        k_in, _, e_in, k_out, e_out, k_scr = take(n_in), take(len(after)), take(na), take(n_out), take(nl), take(n_scr)
        ssem, rsem = take(2)
        ids = [pl.program_id(a) for a in range(len(grid))]
        first = functools.reduce(jnp.logical_and, [i == 0 for i in ids])
        last = functools.reduce(jnp.logical_and, [i == g - 1 for i, g in zip(ids, grid)])

        @pl.when(first)
        def _():
            ex.enter()
            ex.start(e_in, e_out, ssem, rsem)

        body(*k_in, *k_out, *k_scr)

        @pl.when(last)
        def _():
            ex.finish(e_in, e_out, ssem, rsem)

    outs = pl.pallas_call(
        hosted, name=name, grid=grid,
        in_specs=list(in_specs) + [_ANY] * (len(after) + na), out_specs=list(out_specs) + [_ANY] * nl,
        out_shape=list(out_shape) + ex.landing,
        scratch_shapes=list(scratch_shapes) + [pltpu.SemaphoreType.DMA((ex.n_sems,))] * 2,
        input_output_aliases={n_in + len(after) + i: n_out + j for i, j in ex.aliases.items()},
        compiler_params=ex.params(compiler_params),
    )(*args, *after, *_in_hbm(ex.arrays))
    return outs[:n_out], outs[n_out:]


def _exchange_alone(ex, name, after=()):
    na, nl = len(ex.arrays), len(ex.landing)

    def body(*refs):
        outs = refs[na + len(after):na + len(after) + nl]
        ex.enter()
        ex.start(refs[:na], outs, refs[-2], refs[-1])
        ex.finish(refs[:na], outs, refs[-2], refs[-1])

    return pl.pallas_call(
        body, name=name, in_specs=[_ANY] * (na + len(after)), out_specs=[_ANY] * nl,
        out_shape=ex.landing, scratch_shapes=[pltpu.SemaphoreType.DMA((ex.n_sems,))] * 2,
        input_output_aliases=ex.aliases, compiler_params=ex.params(),
    )(*_in_hbm(ex.arrays), *after)


_HBM = pl.BlockSpec(memory_space=pltpu.HBM)
_SEM = pl.BlockSpec(memory_space=pltpu.SEMAPHORE)
_EFFECT = pltpu.SideEffectType.DATAFLOW_SIDE_EFFECTING


def _scatter_copies(srcs, lands, ssems, rsems):
    x, y, c, chips = _place()
    return [_remote(srcs[w].at[2 * px + py], lands[w].at[k], ssems[3 * w + k], rsems[3 * w + k], (px, py, c))
            for w in range(len(srcs)) for k, (px, py) in enumerate(chips)]


def _scatter_start(parts, name):
    n, ncp = len(parts), 3 * len(parts)
    lands = [lax.empty((3,) + p.shape[1:], p.dtype) for p in parts]

    def body(*refs):
        srcs, land_refs = refs[:n], refs[n:2 * n]
        ssems, rsems = refs[2 * n:2 * n + ncp], refs[2 * n + ncp:2 * n + 2 * ncp]
        for cp in _scatter_copies(srcs, land_refs, ssems, rsems):
            cp.start()
        token = refs[-1]
        token[...] = jnp.zeros_like(token)

    outs = pl.pallas_call(
        body, name=name,
        out_shape=([pltpu.SemaphoreType.DMA(())] * (2 * ncp) + [pltpu.HBM(a.shape, a.dtype) for a in parts + lands]
                   + [jax.ShapeDtypeStruct((8, 128), F32)]),
        in_specs=[_HBM] * (2 * n), out_specs=[_SEM] * (2 * ncp) + [_HBM] * (2 * n) + [_VM],
        input_output_aliases={i: 2 * ncp + i for i in range(2 * n)},
        compiler_params=pltpu.CompilerParams(has_side_effects=_EFFECT),
    )(*_in_hbm(parts), *_in_hbm(lands))
    sems, thru, token = outs[:2 * ncp], outs[2 * ncp:2 * ncp + 2 * n], outs[-1]
    return sems, thru, token


def _scatter_wait(sems, thru, after, name):
    n = len(thru) // 2
    ncp = 3 * n

    def body(*refs):
        srcs, land_refs = refs[:n], refs[n:2 * n]
        ssems, rsems = refs[2 * n:2 * n + ncp], refs[2 * n + ncp:2 * n + 2 * ncp]
        for cp in _scatter_copies(srcs, land_refs, ssems, rsems):
            cp.wait_send()
            cp.wait_recv()

    outs = pl.pallas_call(
        body, name=name, out_shape=[pltpu.HBM(a.shape, a.dtype) for a in thru],
        in_specs=[_HBM] * (2 * n) + [_SEM] * (2 * ncp) + [_ANY] * len(after), out_specs=[_HBM] * (2 * n),
        input_output_aliases={i: i for i in range(2 * n)},
        compiler_params=pltpu.CompilerParams(has_side_effects=_EFFECT),
    )(*thru, *sems, *after)
    return outs[:n], outs[n:]


def _gather_copies(bufs, ssems, rsems, sending):
    x, y, c, chips = _place()
    out = []
    for w, ref in enumerate(bufs):
        half = ref.shape[1] // 2
        for k, (px, py) in enumerate(chips):
            rows = ref.at[2 * x + y if sending else 2 * px + py, pl.ds(c * half, half)]
            out.append(_remote(rows, rows, ssems[3 * w + k], rsems[3 * w + k], (px, py, c)))
    return out


def _gather_start(bufs, after, name):
    n, ncp = len(bufs), 3 * len(bufs)

    def body(*refs):
        ssems, rsems = refs[n + len(after):n + len(after) + ncp], refs[n + len(after) + ncp:n + len(after) + 2 * ncp]
        for cp in _gather_copies(refs[:n], ssems, rsems, True):
            cp.start()
        token = refs[-1]
        token[...] = jnp.zeros_like(token)

    outs = pl.pallas_call(
        body, name=name,
        out_shape=([pltpu.SemaphoreType.DMA(())] * (2 * ncp) + [pltpu.HBM(a.shape, a.dtype) for a in bufs]
                   + [jax.ShapeDtypeStruct((8, 128), F32)]),
        in_specs=[_HBM] * n + [_ANY] * len(after), out_specs=[_SEM] * (2 * ncp) + [_HBM] * n + [_VM],
        input_output_aliases={i: 2 * ncp + i for i in range(n)},
        compiler_params=pltpu.CompilerParams(has_side_effects=_EFFECT),
    )(*_in_hbm(bufs), *after)
    return outs[:2 * ncp], outs[2 * ncp:2 * ncp + n], outs[-1]


def _gather_wait(sems, thru, after, name):
    n = len(thru)
    ncp = 3 * n

    def body(*refs):
        ssems, rsems = refs[n:n + ncp], refs[n + ncp:n + 2 * ncp]
        for cp in _gather_copies(refs[:n], ssems, rsems, True):
            cp.wait_send()
        for cp in _gather_copies(refs[:n], ssems, rsems, False):
            cp.wait_recv()

    return pl.pallas_call(
        body, name=name, out_shape=[pltpu.HBM(a.shape, a.dtype) for a in thru],
        in_specs=[_HBM] * n + [_SEM] * (2 * ncp) + [_ANY] * len(after), out_specs=[_HBM] * n,
        input_output_aliases={i: i for i in range(n)},
        compiler_params=pltpu.CompilerParams(has_side_effects=_EFFECT),
    )(*thru, *sems, *after)


def _rms(x):
    r = lax.rsqrt(jnp.mean(x * x, axis=-1, keepdims=True) + EPS)
    return r, x * r


def _rms_bwd(dn, xr, r, gain):
    dng = dn * gain
    dx = r * (dng - xr * jnp.mean(dng * xr, axis=-1, keepdims=True))
    return dx, jnp.sum(dn * xr, axis=0, keepdims=True)


def _ffn_fwd(x, gain, wgu, wd, name, exchange=None, head=None):
    tm = 256

    def body(x_ref, g_ref, wgu_hbm, wd_hbm, *rest):
        if head is None:
            h_ref, n_ref, gu_ref, a_ref, wgu_ref, wd_ref = rest
        else:
            t_ref, gf_ref, h_ref, loss_ref, dgf_ref, n_ref, gu_ref, a_ref, wgu_ref, wd_ref = rest
        _stage([(wgu_hbm, wgu_ref), (wd_hbm, wd_ref)])
        x = x_ref[...]
        _, xr = _rms(x)
        n = (xr * g_ref[...]).astype(BF16)
        n_ref[...] = n
        acc = jnp.zeros((tm, D), F32)
        for j in range(2):
            g = _nn(n, wgu_ref[j])
            u = _nn(n, wgu_ref[2 + j])
            gu_ref[:, j * FFS:(j + 1) * FFS] = g.astype(BF16)
            gu_ref[:, (2 + j) * FFS:(3 + j) * FFS] = u.astype(BF16)
            half_act = (0.5 * (g * jax.nn.sigmoid(g) * u)).astype(BF16)
            a_ref[:, j * FFS:(j + 1) * FFS] = half_act
            acc = acc + _nn(half_act, wd_ref[j * FFS:(j + 1) * FFS, :])
        h = x + acc
        if head is None:
            h_ref[...] = h
            return
        gf = gf_ref[...]
        r, hr = _rms(h)
        err = hr * gf - t_ref[...]
        dh, dgain = _rms_bwd(err * (1.0 / D), hr, r, gf)
        h_ref[...] = dh

        @pl.when(pl.program_id(0) == 0)
        def _():
            dgf_ref[...] = jnp.zeros_like(dgf_ref)
            loss_ref[...] = jnp.zeros_like(loss_ref)

        dgf_ref[...] += dgain
        loss_ref[...] += jnp.full((1, 128), (0.5 / D) * jnp.sum(err * err), F32)

    saved_specs = [_rows(tm, D), _rows(tm, 4 * FFS), _rows(tm, DFF)]
    saved_shapes = [_sds((S, D), BF16), _sds((S, 4 * FFS), BF16), _sds((S, DFF), BF16)]
    if head is None:
        return _call(
            body, (x, gain, wgu, wd), name=name, grid=(S // tm,),
            in_specs=[_rows(tm, D), _fixed((1, D)), _ANY, _ANY],
            out_specs=[_rows(tm, D)] + saved_specs, out_shape=[_sds((S, D), F32)] + saved_shapes,
            scratch_shapes=_vmem_like(wgu, wd),
            compiler_params=_params(("arbitrary",), 56), exchange=exchange)
    return _call(
        body, (x, gain, wgu, wd, *head), name=name, grid=(S // tm,),
        in_specs=[_rows(tm, D), _fixed((1, D)), _ANY, _ANY, _rows(tm, D), _fixed((1, D))],
        out_specs=[_rows(tm, D), _fixed((1, 128)), _fixed((1, D))] + saved_specs,
        out_shape=[_sds((S, D), F32), _sds((1, 128), F32), _sds((1, D), F32)] + saved_shapes,
        scratch_shapes=_vmem_like(wgu, wd),
        compiler_params=_params(("arbitrary",), 56), exchange=exchange, free=(4, 5))


def _ffn_bwd(dh, x, gain, gu, wgu, wd, name):
    tm = 256

    def body(dh_ref, x_ref, g_ref, gu_ref, wgu_hbm, wd_hbm, dx_ref, dgu_ref, dg_ref, wgu_ref, wd_ref):
        _stage([(wgu_hbm, wgu_ref), (wd_hbm, wd_ref)])
        dh = dh_ref[...]
        dhb = dh.astype(BF16)
        dn = jnp.zeros((tm, D), F32)
        for j in range(2):
            g = gu_ref[:, j * FFS:(j + 1) * FFS].astype(F32)
            u = gu_ref[:, (2 + j) * FFS:(3 + j) * FFS].astype(F32)
            da = 0.5 * _nt(dhb, wd_ref[j * FFS:(j + 1) * FFS, :])
            sg = jax.nn.sigmoid(g)
            dgb = (da * u * (sg * (1.0 + g * (1.0 - sg)))).astype(BF16)
            dub = (da * (g * sg)).astype(BF16)
            dgu_ref[:, j * FFS:(j + 1) * FFS] = dgb
            dgu_ref[:, (2 + j) * FFS:(3 + j) * FFS] = dub
            dn = dn + _nt(dgb, wgu_ref[j]) + _nt(dub, wgu_ref[2 + j])
        r, xr = _rms(x_ref[...])
        dx, dgain = _rms_bwd(dn, xr, r, g_ref[...])
        dx_ref[...] = dh + dx

        @pl.when(pl.program_id(0) == 0)
        def _():
            dg_ref[...] = jnp.zeros_like(dg_ref)

        dg_ref[...] += dgain

    return _call(
        body, (dh, x, gain, gu, wgu, wd), name=name, grid=(S // tm,),
        in_specs=[_rows(tm, D), _rows(tm, D), _fixed((1, D)), _rows(tm, 4 * FFS), _ANY, _ANY],
        out_specs=[_rows(tm, D), _rows(tm, 4 * FFS), _fixed((1, D))],
        out_shape=[_sds((S, D), F32), _sds((S, 4 * FFS), BF16), _sds((1, D), F32)],
        scratch_shapes=_vmem_like(wgu, wd), compiler_params=_params(("arbitrary",), 56))


def _ffn_bwd_act(dh, gu, wd, name, exchange=None, after=()):
    tm = 512

    def body(dh_ref, gu_ref, wd_hbm, dgu_ref, wd_ref):
        _stage([(wd_hbm, wd_ref)])
        dhb = dh_ref[...].astype(BF16)
        for j in range(2):
            g = gu_ref[:, j * FFS:(j + 1) * FFS].astype(F32)
            u = gu_ref[:, (2 + j) * FFS:(3 + j) * FFS].astype(F32)
            da = 0.5 * _nt(dhb, wd_ref[j * FFS:(j + 1) * FFS, :])
            sg = jax.nn.sigmoid(g)
            dgu_ref[:, j * FFS:(j + 1) * FFS] = (da * u * (sg * (1.0 + g * (1.0 - sg)))).astype(BF16)
            dgu_ref[:, (2 + j) * FFS:(3 + j) * FFS] = (da * (g * sg)).astype(BF16)

    res = _call(
        body, (dh, gu, wd), name=name, grid=(S // tm,),
        in_specs=[_rows(tm, D), _rows(tm, 4 * FFS), _ANY], out_specs=[_rows(tm, 4 * FFS)],
        out_shape=[_sds((S, 4 * FFS), BF16)], scratch_shapes=_vmem_like(wd),
        compiler_params=_params(("arbitrary",), 56), exchange=exchange, after=after)
    return res[0] if exchange is None else (res[0][0], res[1])


def _ffn_bwd_in(dh, x, gain, dgu, wgu, name, exchange=None, after=()):
    tm = 512

    def body(dh_ref, x_ref, g_ref, dgu_ref, wgu_hbm, dx_ref, dg_ref, wgu_ref):
        _stage([(wgu_hbm, wgu_ref)])
        dn = jnp.zeros((tm, D), F32)
        for j in range(NSH):
            dn = dn + _nt(dgu_ref[:, j * FFS:(j + 1) * FFS], wgu_ref[j])
        r, xr = _rms(x_ref[...])
        dx, dgain = _rms_bwd(dn, xr, r, g_ref[...])
        dx_ref[...] = dh_ref[...] + dx

        @pl.when(pl.program_id(0) == 0)
        def _():
            dg_ref[...] = jnp.zeros_like(dg_ref)

        dg_ref[...] += dgain

    return _call(
        body, (dh, x, gain, dgu, wgu), name=name, grid=(S // tm,),
        in_specs=[_rows(tm, D), _rows(tm, D), _fixed((1, D)), _rows(tm, 4 * FFS), _ANY],
        out_specs=[_rows(tm, D), _fixed((1, D))],
        out_shape=[_sds((S, D), F32), _sds((1, D), F32)],
        scratch_shapes=_vmem_like(wgu),
        compiler_params=_params(("arbitrary",), 56), exchange=exchange, after=after)


def _mix_in(h, gain, w_in, after=()):
    tm = 512

    def body(h_ref, g_ref, w_hbm, u_ref, xp_ref, q_ref, k_ref, v_ref, gp_ref, gs_ref, w_ref):
        _stage([(w_hbm, w_ref)])
        _, hr = _rms(h_ref[...])
        u = (hr * g_ref[...]).astype(BF16)
        u_ref[...] = u
        p0 = _nn(u, w_ref[0])
        xp_ref[...] = p0[:, :PW]
        q_ref[...] = p0[:, PW:].astype(BF16)
        p1 = _nn(u, w_ref[1])
        k_ref[...] = p1[:, :SBW].astype(BF16)
        v_ref[...] = p1[:, SBW:].astype(BF16)
        gp_ref[...] = jax.nn.sigmoid(_nn(u, w_ref[2])).astype(BF16)
        gs_ref[...] = jax.nn.sigmoid(_nn(u, w_ref[3])).astype(BF16)

    return _call(
        body, (h, gain, w_in), name="mix_in", grid=(S // tm,),
        in_specs=[_rows(tm, D), _fixed((1, D)), _ANY],
        out_specs=[_rows(tm, D), _rows(tm, PW), _rows(tm, SBW), _rows(tm, SBW), _rows(tm, SBW),
                   _rows(tm, D), _rows(tm, D)],
        out_shape=[_sds((S, D), BF16), _sds((S, PW), F32), _sds((S, SBW), BF16), _sds((S, SBW), BF16),
                   _sds((S, SBW), BF16), _sds((S, D), BF16), _sds((S, D), BF16)],
        scratch_shapes=_vmem_like(w_in),
        compiler_params=_params(("arbitrary",), 48), free=(1,), after=after)


def _hilo_dot(x, tri):
    hi = x.astype(BF16)
    lo = (x - hi.astype(F32)).astype(BF16)
    return _nn(hi, tri) + _nn(lo, tri)


def _log_terms(qk):
    z2 = qk * (SCALE * LOG2E)
    lb = jnp.minimum(z2, 0.0) - jnp.log2(1.0 + jnp.exp2(-jnp.abs(z2)))
    return lb, lb - z2


def _head_masks():
    lane = lax.broadcasted_iota(jnp.int32, (1, 2 * DH), 1)
    return (lane < DH, lane >= DH)


def _attn_fwd(q, k, v, exchange=None):
    T = TA

    def body(q_ref, k_ref, v_ref, o_ref, c_ref):
        i2 = 2 * pl.program_id(1)
        row = lax.broadcasted_iota(jnp.int32, (T, T), 0)
        col = lax.broadcasted_iota(jnp.int32, (T, T), 1)
        after = (row > col).astype(BF16)
        causal = col < row
        masks = _head_masks()
        qms = {}
        for b in range(QB):
            q2 = q_ref[b * T:(b + 1) * T, :]
            for h, hm in enumerate(masks):
                qms[b, h] = jnp.where(hm, q2, jnp.zeros_like(q2))

        def blocks(keys, pairs, carries, os):
            ks, vms = [], []
            for j in keys:
                rows = pl.ds(pl.multiple_of(j * T, T), T)
                vj = v_ref[rows, :]
                ks.append(k_ref[rows, :])
                vms.append([jnp.where(hm, vj, jnp.zeros_like(vj)) for hm in masks])
            units = [(n, h) for n in range(len(pairs)) for h in range(2)]
            qks = {(n, h): _nt(qms[pairs[n][0], h], ks[pairs[n][1]]) for n, h in units}
            lbs, l1ms = {}, {}
            for u in units:
                lbs[u], l1m = _log_terms(qks[u])
                l1ms[u] = jnp.where(causal, l1m, 0.0) if pairs[u[0]][2] else l1m
            cins = {u: _hilo_dot(l1ms[u], after) for u in units}
            carries, os = dict(carries), list(os)
            for n, h in units:
                b, key, diag = pairs[n]
                a = jnp.exp2(lbs[n, h] + cins[n, h] + carries[b, h])
                if diag:
                    a = jnp.where(causal, a, 0.0)
                os[b] = os[b] + _nn(a.astype(BF16), vms[key][h])
                carries[b, h] = carries[b, h] + jnp.sum(l1ms[n, h], axis=1, keepdims=True)
            return carries, tuple(os)

        carries = {(b, h): jnp.zeros((T, 1), F32) for b in range(QB) for h in range(2)}
        os = tuple(jnp.zeros((T, 2 * DH), F32) for _ in range(QB))
        carries, os = blocks([i2 + 1, i2], [(1, 0, True), (0, 1, True), (1, 1, False)], carries, os)
        carries, os = lax.fori_loop(
            0, i2 // 2,
            lambda t, c: blocks([i2 - 1 - 2 * t, i2 - 2 - 2 * t],
                                [(0, 0, False), (1, 0, False), (0, 1, False), (1, 1, False)], c[0], c[1]),
            (carries, os))
        for b in range(QB):
            o_ref[b * T:(b + 1) * T, :] = os[b].astype(BF16)
            c_ref[b * T:(b + 1) * T, :] = jnp.where(masks[0], carries[b, 0], carries[b, 1])

    blk = pl.BlockSpec((QB * T, 2 * DH), lambda p, i: (i, p))
    full = pl.BlockSpec((S, 2 * DH), lambda p, i: (0, p))
    return _call(
        body, (q, k, v), name="attn_fwd", grid=(SBW // (2 * DH), S // (QB * T)),
        in_specs=[blk, full, full], out_specs=[blk, blk],
        out_shape=[_sds((S, SBW), BF16), _sds((S, SBW), F32)],
        compiler_params=_params(("arbitrary", "arbitrary"), 40), exchange=exchange)


def _attn_bwd(q, k, v, do, ctot, after=()):
    T = TA
    nq = S // (QB * T)

    def body(q_ref, k_ref, v_ref, do_ref, c_ref, dq_ref, dk_ref, dv_ref, dk_acc, dv_acc):
        step = pl.program_id(1)
        i2 = 2 * step

        @pl.when(step == 0)
        def _():
            dk_acc[...] = jnp.zeros_like(dk_acc)
            dv_acc[...] = jnp.zeros_like(dv_acc)

        row = lax.broadcasted_iota(jnp.int32, (T, T), 0)
        col = lax.broadcasted_iota(jnp.int32, (T, T), 1)
        upto = (row <= col).astype(BF16)
        before = (row < col).astype(BF16)
        causal = col < row
        masks = _head_masks()
        qms, doms, ctots = {}, {}, {}
        for b in range(QB):
            q2, do2 = q_ref[b * T:(b + 1) * T, :], do_ref[b * T:(b + 1) * T, :]
            for h, hm in enumerate(masks):
                qms[b, h] = jnp.where(hm, q2, jnp.zeros_like(q2))
                doms[b, h] = jnp.where(hm, do2, jnp.zeros_like(do2))
                ctots[b, h] = c_ref[b * T:(b + 1) * T, h * DH:h * DH + 1]

        def blocks(keys, pairs, sums, dqs):
            rows = [pl.ds(pl.multiple_of(j * T, T), T) for j in keys]
            ks, vs = [k_ref[r, :] for r in rows], [v_ref[r, :] for r in rows]
            kms = [[jnp.where(hm, kj, jnp.zeros_like(kj)) for hm in masks] for kj in ks]
            units = [(n, h) for n in range(len(pairs)) for h in range(2)]
            qks = {(n, h): _nt(qms[pairs[n][0], h], ks[pairs[n][1]]) for n, h in units}
            das = {(n, h): _nt(doms[pairs[n][0], h], vs[pairs[n][1]]) for n, h in units}
            lbs, l1ms = {}, {}
            for u in units:
                lbs[u], l1m = _log_terms(qks[u])
                l1ms[u] = jnp.where(causal, l1m, 0.0) if pairs[u[0]][2] else l1m
            pins = {u: _hilo_dot(l1ms[u], upto) for u in units}
            sums = dict(sums)
            a_s, dls, cps = {}, {}, {}
            for n, h in units:
                b, _, diag = pairs[n]
                cl, cp = sums[b, h]
                a = jnp.exp2(lbs[n, h] + (ctots[b, h] - cl) - pins[n, h])
                if diag:
                    a = jnp.where(causal, a, 0.0)
                a_s[n, h] = a.astype(BF16)
                dls[n, h] = das[n, h] * a
                cps[n, h] = cp
                sums[b, h] = (cl + jnp.sum(l1ms[n, h], axis=1, keepdims=True),
                              cp + jnp.sum(dls[n, h], axis=1, keepdims=True))
            pexs = {u: _hilo_dot(dls[u], before) for u in units}
            dzbs = {}
            for u in units:
                dz = dls[u] - jnp.exp2(lbs[u]) * (dls[u] + pexs[u] + cps[u])
                if pairs[u[0]][2]:
                    dz = jnp.where(causal, dz, 0.0)
                dzbs[u] = dz.astype(BF16)
            dqs = list(dqs)
            for n, h in units:
                dqs[pairs[n][0]] = dqs[pairs[n][0]] + _nn(dzbs[n, h], kms[pairs[n][1]][h])
            for key, r in enumerate(rows):
                mine = [(n, h) for n, h in units if pairs[n][1] == key]
                dk_acc[r, :] += functools.reduce(jnp.add, [_tn(dzbs[u], qms[pairs[u[0]][0], u[1]]) for u in mine])
                dv_acc[r, :] += functools.reduce(jnp.add, [_tn(a_s[u], doms[pairs[u[0]][0], u[1]]) for u in mine])
            return sums, tuple(dqs)

        zero = jnp.zeros((T, 1), F32)
        sums = {(b, h): (zero, zero) for b in range(QB) for h in range(2)}
        dqs = tuple(jnp.zeros((T, 2 * DH), F32) for _ in range(QB))
        sums, dqs = lax.fori_loop(
            0, i2 // 2,
            lambda t, c: blocks([2 * t, 2 * t + 1],
                                [(0, 0, False), (1, 0, False), (0, 1, False), (1, 1, False)], c[0], c[1]),
            (sums, dqs))
        _, dqs = blocks([i2, i2 + 1], [(0, 0, True), (1, 0, False), (1, 1, True)], sums, dqs)
        for b in range(QB):
            dq_ref[b * T:(b + 1) * T, :] = (dqs[b] * SCALE).astype(BF16)

        @pl.when(step == nq - 1)
        def _():
            dk_ref[...] = (dk_acc[...] * SCALE).astype(BF16)
            dv_ref[...] = dv_acc[...].astype(BF16)

    blk = pl.BlockSpec((QB * T, 2 * DH), lambda p, i: (i, p))
    full = pl.BlockSpec((S, 2 * DH), lambda p, i: (0, p))
    return _call(
        body, (q, k, v, do, ctot), name="attn_bwd", grid=(SBW // (2 * DH), nq),
        in_specs=[blk, full, full, blk, blk], out_specs=[blk, full, full],
        out_shape=[_sds((S, SBW), BF16), _sds((S, SBW), BF16), _sds((S, SBW), BF16)],
        scratch_shapes=[pltpu.VMEM((S, 2 * DH), F32), pltpu.VMEM((S, 2 * DH), F32)],
        compiler_params=_params(("arbitrary", "arbitrary"), 40), after=after)


def _pool_counts(first_row, tm):
    pos = first_row + lax.broadcasted_iota(jnp.int32, (tm, 1), 0)
    return [jnp.minimum(pos + 1, w).astype(F32) for w in POOL_WINDOWS]


def _mix_out(h, xp, o_sb, gp, gs, w_group, scale, w_bp, w_ba, w_out, exchange=None):
    tm = 512

    def body(h_ref, xp_ref, o_ref, gp_ref, gs_ref, wg_hbm, sc_ref, wbp_hbm, wba_hbm, wo_hbm,
             h2_ref, pm_ref, p_ref, yp_ref, ys_ref, m_ref, halo, wg_ref, wbp_ref, wba_ref, wo_ref):
        _stage([(wg_hbm, wg_ref), (wbp_hbm, wbp_ref), (wba_hbm, wba_ref), (wo_hbm, wo_ref)])
        i = pl.program_id(0)

        @pl.when(i == 0)
        def _():
            halo[...] = jnp.zeros_like(halo)

        xp = xp_ref[...]
        ext = jnp.concatenate([halo[...], xp], axis=0)
        halo[...] = xp[tm - HALO:, :]
        counts = _pool_counts(i * tm, tm)
        for gi in range(len(POOL_WINDOWS)):
            lanes = slice(gi * PG, (gi + 1) * PG)
            win = ext[:, lanes]
            for step in range(gi + 1):
                win = win + pltpu.roll(win, 1 << step, 0)
            pm = (win[HALO:, :] / counts[gi] - xp[:, lanes]).astype(BF16)
            pm_ref[:, lanes] = pm
            p_ref[:, lanes] = (_nn(pm, wg_ref[gi]) * sc_ref[:, lanes]).astype(BF16)
        pb = p_ref[...]
        ob = o_ref[...]
        for j in range(NSH):
            cols = slice(j * (D // NSH), (j + 1) * (D // NSH))
            yp = _nn(pb, wbp_ref[j])
            ys = _nn(ob, wba_ref[j])
            yp_ref[:, cols] = yp.astype(BF16)
            ys_ref[:, cols] = ys.astype(BF16)
            m_ref[:, cols] = (gp_ref[:, cols].astype(F32) * yp + gs_ref[:, cols].astype(F32) * ys).astype(BF16)
        h2_ref[...] = h_ref[...] + _nn(m_ref[...], wo_ref[...])

    return _call(
        body, (h, xp, o_sb, gp, gs, w_group, scale, w_bp, w_ba, w_out), name="mix_out", grid=(S // tm,),
        in_specs=[_rows(tm, D), _rows(tm, PW), _rows(tm, SBW), _rows(tm, D), _rows(tm, D),
                  _ANY, _fixed((1, PW)), _ANY, _ANY, _ANY],
        out_specs=[_rows(tm, D), _rows(tm, PW), _rows(tm, PW), _rows(tm, D), _rows(tm, D), _rows(tm, D)],
        out_shape=[_sds((S, D), F32), _sds((S, PW), BF16), _sds((S, PW), BF16), _sds((S, D), BF16),
                   _sds((S, D), BF16), _sds((S, D), BF16)],
        scratch_shapes=[pltpu.VMEM((HALO, PW), F32)] + _vmem_like(w_group, w_bp, w_ba, w_out),
        compiler_params=_params(("arbitrary",), 48), free=(5, 6), exchange=exchange)


def _mix_bwd_out(dh, gp, gs, yp, ys, pm, w_group, scale, w_bp, w_ba, w_out, exchange=None):
    tm = 512
    nt = S // tm

    def body(dh_ref, gp_ref, gs_ref, yp_ref, ys_ref, pm_ref, wg_hbm, sc_ref, wbp_hbm, wba_hbm, wo_hbm,
             dlg_ref, dyp_ref, dys_ref, do_ref, dyg_ref, dxp_ref, dsc_ref, halo, wg_ref, wbp_ref, wba_ref, wo_ref):
        _stage([(wg_hbm, wg_ref), (wbp_hbm, wbp_ref), (wba_hbm, wba_ref), (wo_hbm, wo_ref)])
        step = pl.program_id(0)

        @pl.when(step == 0)
        def _():
            halo[...] = jnp.zeros_like(halo)
            dsc_ref[...] = jnp.zeros_like(dsc_ref)

        dm = _nt(dh_ref[...].astype(BF16), wo_ref[...])
        gp = gp_ref[...].astype(F32)
        gs = gs_ref[...].astype(F32)
        yp = yp_ref[...].astype(F32)
        ys = ys_ref[...].astype(F32)
        dlg_ref[:, :D] = (dm * yp * gp * (1.0 - gp)).astype(BF16)
        dlg_ref[:, D:] = (dm * ys * gs * (1.0 - gs)).astype(BF16)
        dyp_ref[...] = (dm * gp).astype(BF16)
        dys_ref[...] = (dm * gs).astype(BF16)
        dp = jnp.zeros((tm, PW), F32)
        do = jnp.zeros((tm, SBW), F32)
        for j in range(NSH):
            cols = slice(j * (D // NSH), (j + 1) * (D // NSH))
            dp = dp + _nt(dyp_ref[:, cols], wbp_ref[j])
            do = do + _nt(dys_ref[:, cols], wba_ref[j])
        do_ref[...] = do.astype(BF16)
        counts = _pool_counts((nt - 1 - step) * tm, tm)
        dscale = []
        for gi in range(len(POOL_WINDOWS)):
            lanes = slice(gi * PG, (gi + 1) * PG)
            dpg = dp[:, lanes]
            dscale.append(jnp.sum(dpg * _nn(pm_ref[:, lanes], wg_ref[gi]), axis=0, keepdims=True))
            dyg = (dpg * sc_ref[:, lanes]).astype(BF16)
            dyg_ref[:, lanes] = dyg
            dpm = _nt(dyg, wg_ref[gi])
            per = dpm / counts[gi]
            win = jnp.concatenate([per, halo[:, lanes]], axis=0)
            halo[:, lanes] = per[:HALO, :]
            for s in range(gi + 1):
                win = win + pltpu.roll(win, tm + HALO - (1 << s), 0)
            dxp_ref[:, lanes] = (win[:tm, :] - dpm).astype(BF16)
        dsc_ref[...] += jnp.concatenate(dscale, axis=1)

    rev = lambda width: pl.BlockSpec((tm, width), lambda i: (nt - 1 - i, 0))
    return _call(
        body, (dh, gp, gs, yp, ys, pm, w_group, scale, w_bp, w_ba, w_out), name="mix_bwd_out", grid=(nt,),
        in_specs=[rev(D), rev(D), rev(D), rev(D), rev(D), rev(PW), _ANY, _fixed((1, PW)), _ANY, _ANY, _ANY],
        out_specs=[rev(2 * D), rev(D), rev(D), rev(SBW), rev(PW), rev(PW), _fixed((1, PW))],
        out_shape=[_sds((S, 2 * D), BF16), _sds((S, D), BF16), _sds((S, D), BF16), _sds((S, SBW), BF16),
                   _sds((S, PW), BF16), _sds((S, PW), BF16), _sds((1, PW), F32)],
        scratch_shapes=[pltpu.VMEM((HALO, PW), F32)] + _vmem_like(w_group, w_bp, w_ba, w_out),
        compiler_params=_params(("arbitrary",), 48), exchange=exchange)


def _mix_bwd_in(dh, h, gain, pieces, w_in, exchange=None):
    tm = 512
    widths = [p.shape[1] for p in pieces]

    def body(dh_ref, h_ref, g_ref, *rest):
        piece_refs, (w_hbm, dx_ref, dg_ref, dp_ref, w_ref) = rest[:len(pieces)], rest[len(pieces):]
        _stage([(w_hbm, w_ref)])
        at = 0
        for ref, width in zip(piece_refs, widths):
            dp_ref[:, at:at + width] = ref[...]
            at += width
        du = jnp.zeros((tm, D), F32)
        for j in range(NSH):
            du = du + _nt(dp_ref[:, j * D:(j + 1) * D], w_ref[j])
        r, hr = _rms(h_ref[...])
        dx, dgain = _rms_bwd(du, hr, r, g_ref[...])
        dx_ref[...] = dh_ref[...] + dx

        @pl.when(pl.program_id(0) == 0)
        def _():
            dg_ref[...] = jnp.zeros_like(dg_ref)

        dg_ref[...] += dgain

    return _call(
        body, (dh, h, gain, *pieces, w_in), name="mix_bwd_in", grid=(S // tm,),
        in_specs=[_rows(tm, D), _rows(tm, D), _fixed((1, D))] + [_rows(tm, w) for w in widths] + [_ANY],
        out_specs=[_rows(tm, D), _fixed((1, D)), _rows(tm, 4 * D)],
        out_shape=[_sds((S, D), F32), _sds((1, D), F32), _sds((S, 4 * D), BF16)],
        scratch_shapes=_vmem_like(w_in),
        compiler_params=_params(("arbitrary",), 48), exchange=exchange)


def _wgrad(a, b, nblk, ti, name, out_dtype=BF16, exchange=None, after=()):
    ka, n = a.shape[1], b.shape[1]
    ns = n // nblk

    def body(a_ref, b_ref, o_ref):
        o_ref[...] = _tn(a_ref[...].astype(BF16), b_ref[...].astype(BF16)).astype(out_dtype)

    res = _call(
        body, (a, b), name=name, grid=(nblk, ka // ti),
        in_specs=[pl.BlockSpec((S, ti), lambda j, i: (0, i)), pl.BlockSpec((S, ns), lambda j, i: (0, j))],
        out_specs=[pl.BlockSpec((None, ti, ns), lambda j, i: (j, i, 0))],
        out_shape=[_sds((nblk, ka, ns), out_dtype)],
        compiler_params=_params(("arbitrary", "arbitrary"), 56), exchange=exchange, after=after)
    return res[0] if exchange is None else (res[0][0], res[1])


def _wgrad_groups(pm, dyg):
    def body(a_ref, b_ref, o_ref):
        o_ref[...] = _tn(a_ref[...], b_ref[...])

    col = pl.BlockSpec((S, PG), lambda g: (0, g))
    return pl.pallas_call(
        body, name="wgrad_groups", grid=(PW // PG,),
        in_specs=[col, col], out_specs=pl.BlockSpec((None, PG, PG), lambda g: (g, 0, 0)),
        out_shape=_sds((PW // PG, PG, PG), F32),
        compiler_params=_params(("arbitrary",), 32),
    )(*_in_hbm([pm, dyg]))


def _place():
    x, y, c = lax.axis_index("x"), lax.axis_index("y"), lax.axis_index("c")
    chips = [(1 - x, y), (x, 1 - y), (1 - x, 1 - y)]
    return x, y, c, chips


def _remote(src, dst, ssem, rsem, dev):
    return pltpu.make_async_remote_copy(src_ref=src, dst_ref=dst, send_sem=ssem, recv_sem=rsem,
                                        device_id=dev, device_id_type=MESH)


def _cast_into_block(w, me_idx, name):
    rows, cols = w.shape
    tr = _row_block(rows)

    def body(me_ref, w_ref, o_ref):
        o_ref[...] = w_ref[...].astype(BF16)

    return pl.pallas_call(
        body, name=name, out_shape=_sds((NSH, rows, cols), BF16),
        grid_spec=pltpu.PrefetchScalarGridSpec(
            num_scalar_prefetch=1, grid=(rows // tr,),
            in_specs=[pl.BlockSpec((tr, cols), lambda r, me: (r, 0))],
            out_specs=pl.BlockSpec((None, tr, cols), lambda r, me: (me[0], r, 0))),
        compiler_params=_params(("arbitrary",), 32),
    )(me_idx, w)


def _ex_gather(bufs):
    n = len(bufs)
    per = 8

    def plan(outs, ssem, rsem, w):
        x, y, c, _ = _place()
        sib, nbr_x, nbr_y = (x, y, 1 - c), (1 - x, y, c), (x, 1 - y, c)
        half = outs[w].shape[1] // 2
        quarter = half // 2
        sem = lambda k: (ssem.at[per * w + k], rsem.at[per * w + k])
        rows = lambda blk, start, size: outs[w].at[blk, pl.ds(start, size)]
        mine = rows(2 * x + y, c * half, half)
        from_x = rows(2 * (1 - x) + y, c * half, half)
        from_y = rows(2 * x + (1 - y), c * half, half)
        diag = 2 * (1 - x) + (1 - y)
        pass_y = rows(2 * (1 - x) + y, c * half, quarter)
        pass_x = rows(2 * x + (1 - y), c * half + quarter, quarter)
        diag_0, diag_1 = rows(diag, c * half, quarter), rows(diag, c * half + quarter, quarter)
        first = [_remote(mine, mine, *sem(0), nbr_x), _remote(mine, mine, *sem(1), nbr_y)]
        arrivals = [
            (_remote(from_x, from_x, *sem(0), nbr_x),
             [_remote(pass_y, pass_y, *sem(2), nbr_y), _remote(from_x, from_x, *sem(4), sib)]),
            (_remote(from_y, from_y, *sem(1), nbr_y),
             [_remote(pass_x, pass_x, *sem(3), nbr_x), _remote(from_y, from_y, *sem(5), sib)]),
            (_remote(diag_0, diag_0, *sem(2), nbr_y), [_remote(diag_0, diag_0, *sem(6), sib)]),
            (_remote(diag_1, diag_1, *sem(3), nbr_x), [_remote(diag_1, diag_1, *sem(7), sib)]),
        ]
        other = (1 - c) * half
        from_sibling = [
            _remote(rows(2 * (1 - x) + y, other, half), rows(2 * (1 - x) + y, other, half), *sem(4), sib),
            _remote(rows(2 * x + (1 - y), other, half), rows(2 * x + (1 - y), other, half), *sem(5), sib),
            _remote(rows(diag, other, quarter), rows(diag, other, quarter), *sem(6), sib),
            _remote(rows(diag, other + quarter, quarter), rows(diag, other + quarter, quarter), *sem(7), sib),
        ]
        return first, arrivals, from_sibling

    def start(ins, outs, ssem, rsem):
        x, y, c, _ = _place()
        for w in range(n):
            half = outs[w].shape[1] // 2
            mine = outs[w].at[2 * x + y, pl.ds(c * half, half)]
            _remote(mine, mine, ssem.at[per * w], rsem.at[per * w], (1 - x, y, c)).start()
            _remote(mine, mine, ssem.at[per * w + 1], rsem.at[per * w + 1], (x, 1 - y, c)).start()

    def finish(ins, outs, ssem, rsem):
        plans = [plan(outs, ssem, rsem, w) for w in range(n)]
        started = []
        for direct in (True, False):
            for first, arrivals, _ in plans:
                for arrived, onward in (arrivals[:2] if direct else arrivals[2:]):
                    arrived.wait_recv()
                    for cp in onward:
                        cp.start()
                    started += onward
        for first, _, from_sibling in plans:
            for cp in from_sibling:
                cp.wait_recv()
            started += first
        for cp in started:
            cp.wait_send()

    return Exchange(bufs, [_sds(b.shape, b.dtype) for b in bufs], {w: w for w in range(n)}, per * n, start, finish)


def _ex_gather_direct(bufs):
    n = len(bufs)

    def copies(outs, ssem, rsem, only_first=False):
        x, y, c, chips = _place()
        me, sib = 2 * x + y, (x, y, 1 - c)
        first, relay, last = [], [], []
        for w in range(n):
            half = outs[w].shape[1] // 2
            mine = outs[w].at[me, pl.ds(c * half, half)]
            for k, (px, py) in enumerate(chips):
                sems = (ssem.at[6 * w + k], rsem.at[6 * w + k])
                sib_sems = (ssem.at[6 * w + 3 + k], rsem.at[6 * w + 3 + k])
                first.append(_remote(mine, mine, *sems, (px, py, c)))
                if only_first:
                    continue
                got = outs[w].at[2 * px + py, pl.ds(c * half, half)]
                relay.append((_remote(got, got, *sems, (px, py, c)), _remote(got, got, *sib_sems, sib)))
                theirs = outs[w].at[2 * px + py, pl.ds((1 - c) * half, half)]
                last.append(_remote(theirs, theirs, *sib_sems, sib))
        return first, relay, last

    def start(ins, outs, ssem, rsem):
        for cp in copies(outs, ssem, rsem, only_first=True)[0]:
            cp.start()

    def finish(ins, outs, ssem, rsem):
        first, relay, last = copies(outs, ssem, rsem)
        for arrived, onward in relay:
            arrived.wait_recv()
            onward.start()
        for cp in last:
            cp.wait_recv()
        for cp in first:
            cp.wait_send()
        for _, onward in relay:
            onward.wait_send()

    return Exchange(bufs, [_sds(b.shape, b.dtype) for b in bufs], {w: w for w in range(n)}, 6 * n, start, finish)


def _simple_exchange(arrays, landing, aliases, make_copies, sibling_only=False):
    def start(ins, outs, ssem, rsem):
        for cp, _ in make_copies(ins, outs, ssem, rsem, False):
            cp.start()

    def finish(ins, outs, ssem, rsem):
        cps = make_copies(ins, outs, ssem, rsem, True)
        for _, landed in cps:
            landed.wait_recv()
        for cp, _ in cps:
            cp.wait_send()

    return Exchange(arrays, landing, aliases, len(arrays) * 3, start, finish, sibling_only)


def _ex_pair_swap(grads):
    def make(ins, outs, ssem, rsem, landing):
        x, y, c, _ = _place()
        cps = [_remote(ins[w].at[:, 1 - c], outs[w], ssem.at[w], rsem.at[w], (x, y, 1 - c))
               for w in range(len(grads))]
        return [(cp, cp) for cp in cps]

    return _simple_exchange(grads, [_sds((NSH,) + g.shape[2:], g.dtype) for g in grads], {}, make, True)


def _ex_scatter(parts):
    def make(ins, outs, ssem, rsem, landing):
        x, y, c, chips = _place()
        out = []
        for w in range(len(parts)):
            for k, (px, py) in enumerate(chips):
                sems = (ssem.at[3 * w + k], rsem.at[3 * w + k])
                out.append((_remote(ins[w].at[2 * px + py], outs[w].at[k], *sems, (px, py, c)),
                            _remote(outs[w].at[k], outs[w].at[k], *sems, (px, py, c)) if landing else None))
        return out

    return _simple_exchange(parts, [_sds((3,) + p.shape[1:], p.dtype) for p in parts], {}, make)


def _ex_relay(bufs):
    def make(ins, outs, ssem, rsem, landing):
        x, y, c, chips = _place()
        sib = (x, y, 1 - c)
        out = []
        for w in range(len(bufs)):
            half = outs[w].shape[1] // 2
            for k, (px, py) in enumerate(chips):
                sems = (ssem.at[3 * w + k], rsem.at[3 * w + k])
                have = outs[w].at[2 * px + py, pl.ds(c * half, half)]
                miss = outs[w].at[2 * px + py, pl.ds((1 - c) * half, half)]
                out.append((_remote(have, have, *sems, sib), _remote(miss, miss, *sems, sib) if landing else None))
        return out

    return _simple_exchange(bufs, [_sds(b.shape, b.dtype) for b in bufs], {w: w for w in range(len(bufs))}, make, True)


def _ex_share(bufs):
    def make(ins, outs, ssem, rsem, landing):
        x, y, c, _ = _place()
        sib = (x, y, 1 - c)
        return [(_remote(outs[w].at[c], outs[w].at[c], ssem.at[w], rsem.at[w], sib),
                 _remote(outs[w].at[1 - c], outs[w].at[1 - c], ssem.at[w], rsem.at[w], sib) if landing else None)
                for w in range(len(bufs))]

    return _simple_exchange(bufs, [_sds(b.shape, b.dtype) for b in bufs], {w: w for w in range(len(bufs))}, make, True)


def _small_copies(slots, ssems, rsems, sending):
    x, y, c, _ = _place()
    out = []
    for m in range(1, 8):
        px, py, pc = x ^ (m >> 2), y ^ ((m >> 1) & 1), c ^ (m & 1)
        slot = slots.at[4 * x + 2 * y + c if sending else 4 * px + 2 * py + pc]
        out.append(_remote(slot, slot, ssems[m - 1], rsems[m - 1], (px, py, pc)))
    return out


def _small_gather_start(slots, name):
    def body(*refs):
        for cp in _small_copies(refs[0], refs[1:8], refs[8:15], True):
            cp.start()
        refs[-1][...] = jnp.zeros_like(refs[-1])

    outs = pl.pallas_call(
        body, name=name,
        out_shape=([pltpu.SemaphoreType.DMA(())] * 14 + [pltpu.HBM(slots.shape, slots.dtype)]
                   + [jax.ShapeDtypeStruct((8, 128), F32)]),
        in_specs=[_HBM], out_specs=[_SEM] * 14 + [_HBM, _VM], input_output_aliases={0: 14},
        compiler_params=pltpu.CompilerParams(has_side_effects=_EFFECT),
    )(*_in_hbm([slots]))
    return outs[:14], outs[14], outs[15]


def _small_gather_wait(sems, slots, after, name):
    def body(*refs):
        for cp in _small_copies(refs[0], refs[1:8], refs[8:15], True):
            cp.wait_send()
        for cp in _small_copies(refs[0], refs[1:8], refs[8:15], False):
            cp.wait_recv()

    return pl.pallas_call(
        body, name=name, out_shape=pltpu.HBM(slots.shape, slots.dtype),
        in_specs=[_HBM] + [_SEM] * 14 + [_ANY] * len(after), out_specs=_HBM, input_output_aliases={0: 0},
        compiler_params=pltpu.CompilerParams(has_side_effects=_EFFECT),
    )(slots, *sems, *after)


def _row_block(rows, cap=256):
    return max(t for t in range(16, cap + 1, 16) if rows % t == 0)


def _pair_sum(grad, got, c_idx, name):
    _, _, half, cols = grad.shape
    tr = _row_block(half, 512)

    def body(c_ref, a_ref, b_ref, o_ref):
        o_ref[...] = (a_ref[...].astype(F32) + b_ref[...].astype(F32)).astype(BF16)

    return pl.pallas_call(
        body, name=name, out_shape=_sds((NSH, half, cols), BF16),
        grid_spec=pltpu.PrefetchScalarGridSpec(
            num_scalar_prefetch=1, grid=(NSH, half // tr),
            in_specs=[pl.BlockSpec((None, None, tr, cols), lambda j, r, c: (j, c[0], r, 0)),
                      pl.BlockSpec((None, tr, cols), lambda j, r, c: (j, r, 0))],
            out_specs=pl.BlockSpec((None, tr, cols), lambda j, r, c: (j, r, 0))),
        compiler_params=_params(("arbitrary", "arbitrary"), 32),
    )(c_idx, *_in_hbm([grad, got]))


def _chip_sum(own, got, place, name):
    _, half, cols = own.shape
    tr = _row_block(half, 512)

    def body(place_ref, own_ref, got_ref, o_ref):
        acc = own_ref[...].astype(F32)
        for k in range(3):
            acc = acc + got_ref[k].astype(F32)
        o_ref[...] = acc

    return pl.pallas_call(
        body, name=name, out_shape=_sds((2, half, cols), F32),
        grid_spec=pltpu.PrefetchScalarGridSpec(
            num_scalar_prefetch=1, grid=(half // tr,),
            in_specs=[pl.BlockSpec((None, tr, cols), lambda r, p: (p[0], r, 0)),
                      pl.BlockSpec((3, tr, cols), lambda r, p: (0, r, 0))],
            out_specs=pl.BlockSpec((None, tr, cols), lambda r, p: (p[1], r, 0))),
        compiler_params=_params(("arbitrary",), 32),
    )(place, *_in_hbm([own, got]))


def _adamw_math(w, g, m, v):
    m = B1 * m + (1.0 - B1) * g
    v = B2 * v + (1.0 - B2) * (g * g)
    m_hat = m / (1.0 - B1 ** STEP)
    v_hat = v / (1.0 - B2 ** STEP)
    return -LR * (m_hat / (jnp.sqrt(v_hat) + AEPS) + WD * w), m, v


def _adamw(w, g, m, v, name, after=()):
    rows, cols = w.shape
    tr = _row_block(rows)

    def body(w_ref, g_ref, m_ref, v_ref, go_ref, d_ref, nm_ref, nv_ref):
        g = g_ref[...]
        go_ref[...] = g
        d_ref[...], nm_ref[...], nv_ref[...] = _adamw_math(w_ref[...], g, m_ref[...], v_ref[...])

    blk = pl.BlockSpec((tr, cols), lambda r: (r, 0))
    return _call(
        body, (w, g, m, v), name=name, grid=(rows // tr,), out_shape=[_sds(w.shape, F32)] * 4,
        in_specs=[blk] * 4, out_specs=[blk] * 4,
        compiler_params=_params(("arbitrary",), 32), free=(0, 2, 3), after=after)


def _small_update(gathered, w, m, v):
    rows = w.shape[0]

    def body(ga_ref, w_ref, m_ref, v_ref, *out_refs):
        g = ga_ref[0:rows, :]
        for dev in range(1, 8):
            g = g + ga_ref[dev * rows:(dev + 1) * rows, :]
        results = (g,) + _adamw_math(w_ref[...], g, m_ref[...], v_ref[...])
        for i, res in enumerate(results):
            out_refs[i][...] = res[:SMALL_HEAD, :]
            out_refs[4 + i][...] = res[SMALL_HEAD:, :]

    outs = pl.pallas_call(
        body, name="small_update",
        out_shape=[jax.ShapeDtypeStruct((SMALL_HEAD, 128), F32)] * 4
        + [jax.ShapeDtypeStruct((rows - SMALL_HEAD, 128), F32)] * 4,
        in_specs=[_VM] * 4, out_specs=[_VM] * 8,
    )(gathered, w, m, v)
    return outs[:4], outs[4:]


SMALL = ("ffn1_norm", "mix_norm", "ffn2_norm", "final_norm", "pool_scale", "loss", "pool_w_group")
SMALL_HEAD = 48
BIG = ("ffn1_w_gate_up", "ffn1_w_down", "w_in", "w_branch_pool", "w_branch_attn", "w_out",
       "ffn2_w_gate_up", "ffn2_w_down")
ORDER = ("ffn1_norm", "ffn1_w_gate_up", "ffn1_w_down", "mix_norm", "w_in", "pool_w_group", "pool_scale",
         "w_branch_pool", "w_branch_attn", "w_out", "ffn2_norm", "ffn2_w_gate_up", "ffn2_w_down", "final_norm")
SMALL_ROWS = 560


def _pack_small(t):
    parts = []
    for k in SMALL:
        rows = t[k].reshape(-1, 128) if k in t else jnp.zeros((1, 128), F32)
        parts.append(jnp.pad(rows, ((0, -rows.shape[0] % 8), (0, 0))))
    packed = jnp.concatenate(parts, axis=0)
    assert packed.shape == (SMALL_ROWS, 128), packed.shape
    return packed


def _unpack_small(head, group, like):
    out, at = {"pool_w_group": group.reshape(like["pool_w_group"].shape)}, 0
    for k in SMALL[:-1]:
        n = like[k].size // 128 if k in like else 1
        out[k] = head[at:at + n].reshape(like[k].shape) if k in like else head[at, 0]
        at += n + (-n % 8)
    return out


def _halves(g):
    return g.reshape(NSH, 2, g.shape[1] // 2, g.shape[2])


def kernel(x, ffn1_norm, ffn1_w_gate_up, ffn1_w_down, mix_norm, w_in, pool_w_group, pool_scale, w_branch_pool, w_branch_attn, w_out, ffn2_norm, ffn2_w_gate_up, ffn2_w_down, final_norm, loss_target, m_ffn1_norm, m_ffn1_w_gate_up, m_ffn1_w_down, m_mix_norm, m_w_in, m_pool_w_group, m_pool_scale, m_w_branch_pool, m_w_branch_attn, m_w_out, m_ffn2_norm, m_ffn2_w_gate_up, m_ffn2_w_down, m_final_norm, v_ffn1_norm, v_ffn1_w_gate_up, v_ffn1_w_down, v_mix_norm, v_w_in, v_pool_w_group, v_pool_scale, v_w_branch_pool, v_w_branch_attn, v_w_out, v_ffn2_norm, v_ffn2_w_gate_up, v_ffn2_w_down, v_final_norm):
    wts = dict(ffn1_norm=ffn1_norm, ffn1_w_gate_up=ffn1_w_gate_up, ffn1_w_down=ffn1_w_down, mix_norm=mix_norm,
               w_in=w_in, pool_w_group=pool_w_group, pool_scale=pool_scale, w_branch_pool=w_branch_pool,
               w_branch_attn=w_branch_attn, w_out=w_out, ffn2_norm=ffn2_norm, ffn2_w_gate_up=ffn2_w_gate_up,
               ffn2_w_down=ffn2_w_down, final_norm=final_norm)
    mom = dict(ffn1_norm=m_ffn1_norm, ffn1_w_gate_up=m_ffn1_w_gate_up, ffn1_w_down=m_ffn1_w_down,
               mix_norm=m_mix_norm, w_in=m_w_in, pool_w_group=m_pool_w_group, pool_scale=m_pool_scale,
               w_branch_pool=m_w_branch_pool, w_branch_attn=m_w_branch_attn, w_out=m_w_out,
               ffn2_norm=m_ffn2_norm, ffn2_w_gate_up=m_ffn2_w_gate_up, ffn2_w_down=m_ffn2_w_down,
               final_norm=m_final_norm)
    var = dict(ffn1_norm=v_ffn1_norm, ffn1_w_gate_up=v_ffn1_w_gate_up, ffn1_w_down=v_ffn1_w_down,
               mix_norm=v_mix_norm, w_in=v_w_in, pool_w_group=v_pool_w_group, pool_scale=v_pool_scale,
               w_branch_pool=v_w_branch_pool, w_branch_attn=v_w_branch_attn, w_out=v_w_out,
               ffn2_norm=v_ffn2_norm, ffn2_w_gate_up=v_ffn2_w_gate_up, ffn2_w_down=v_ffn2_w_down,
               final_norm=v_final_norm)

    c_idx = lax.axis_index("c").astype(jnp.int32).reshape(1)
    me_idx = (2 * lax.axis_index("x") + lax.axis_index("y")).astype(jnp.int32).reshape(1)
    place = jnp.concatenate([me_idx, c_idx])
    x0, tgt = x[0], loss_target[0]
    wgrp = pool_w_group[0].astype(BF16)
    g1, gm, g2, gf = ffn1_norm, mix_norm, ffn2_norm, final_norm.reshape(1, D)
    grad, delta, new_m, new_v = {}, {}, {}, {}

    def pair_sums(keys, parts, got):
        return [_pair_sum(parts[i], got[i], c_idx, "pair_sum_" + k) for i, k in enumerate(keys)]

    def chip_sums(keys, chip_parts, owned):
        return [_chip_sum(chip_parts[i], owned[i], place, "chip_sum_" + k) for i, k in enumerate(keys)]

    def adamw(k, after=()):
        outs = _adamw(wts[k][0], grad[k][0], mom[k][0], var[k][0], "adamw_" + k, after=after)
        grad[k], delta[k], new_m[k], new_v[k] = (o.reshape(wts[k].shape) for o in outs)

    own = {k: _cast_into_block(wts[k][0], me_idx, "cast_" + k) for k in BIG}
    first, late = ("ffn1_w_gate_up", "ffn1_w_down"), ("w_branch_pool", "w_branch_attn", "w_out",
                                                       "ffn2_w_gate_up", "ffn2_w_down")
    full = dict(zip(first, _exchange_alone(_ex_gather([own[k] for k in first]), "gather_ffn1")))
    wgu1, wd1 = full["ffn1_w_gate_up"], full["ffn1_w_down"].reshape(DFF, D)
    (h1, n1, gu1, a1), (win,) = _ffn_fwd(x0, g1, wgu1, wd1, "ffn1_fwd", exchange=_ex_gather_direct([own["w_in"]]))
    sems_l, thru_l, token_l = _gather_start([own[k_] for k_ in late], [h1], "gather_late_start")
    u, xp, q, k, v, gp, gs = _mix_in(h1, gm, win, after=(token_l,))
    o_sb, ctot = _attn_fwd(q, k, v)
    arrived = _gather_wait(sems_l, thru_l, [o_sb], "gather_late_wait")
    wbp, wba, wout = _exchange_alone(_ex_relay(arrived[:3]), "relay_mix")
    wout = wout.reshape(D, D)
    (h2, pm, p, yp, ys, mm), (wgu2, wd2) = _mix_out(h1, xp, o_sb, gp, gs, wgrp, pool_scale, wbp, wba, wout,
                                                    exchange=_ex_relay(arrived[3:]))
    wd2 = wd2.reshape(DFF, D)
    dh3, loss_row, d_gf, n3, gu3, a3 = _ffn_fwd(h2, g2, wgu2, wd2, "ffn2_fwd", head=(tgt, gf))

    def grad_gate_up(n, dgu, name, exchange=None):
        res = _wgrad(n, dgu, NSH, D, name, exchange=exchange)
        return [_halves(res)] if exchange is None else ([_halves(res[0])], res[1])

    def grad_down(a, dh, name, exchange=None):
        res = _wgrad(a, dh, 1, FFS, name, exchange=exchange)
        halves = lambda g: [_halves(g.reshape(NSH, DFF // NSH, D))]
        return halves(res) if exchange is None else (halves(res[0]), res[1])

    k_gu2, k_d2, k_gu1, k_d1, k_in = (("ffn2_w_gate_up",), ("ffn2_w_down",), ("ffn1_w_gate_up",),
                                      ("ffn1_w_down",), ("w_in",))
    dh2, dgu3, d_g2 = _ffn_bwd(dh3, h2, g2, gu3, wgu2, wd2, "ffn2_bwd")
    pa = grad_gate_up(n3, dgu3, "wgrad_gu2") + grad_down(a3, dh3, "wgrad_d2")
    (dlg, dyp, dys, do_sb, dyg, dxp, d_scale), got_a = _mix_bwd_out(
        dh2, gp, gs, yp, ys, pm, wgrp, pool_scale, wbp, wba, wout, exchange=_ex_pair_swap(pa))
    chip_a = pair_sums(k_gu2 + k_d2, pa, got_a)
    kb = ("w_out", "w_branch_pool", "w_branch_attn")
    pb = [_halves(_wgrad(mm, dh2, 1, D, "wgrad_out").reshape(NSH, D // NSH, D)),
          _halves(_wgrad(p, dyp, NSH, PW, "wgrad_bp")), _halves(_wgrad(o_sb, dys, NSH, SBW, "wgrad_ba"))]
    k_a, k_in = k_gu2 + k_d2, k_in + kb
    sems_a, thru_a, token_a = _scatter_start(chip_a, "scatter_a_start")
    dq, dk, dv = _attn_bwd(q, k, v, do_sb, ctot, after=(token_a,))
    chip_a, owned_a = _scatter_wait(sems_a, thru_a, [dq], "scatter_a_wait")
    halves_a = chip_sums(k_a, chip_a, owned_a)
    (dh1, d_gm, dproj), both_a = _mix_bwd_in(dh2, h1, gm, (dxp, dq, dk, dv, dlg), win, exchange=_ex_share(halves_a))
    for i, k_ in enumerate(k_a):
        grad[k_] = both_a[i].reshape(wts[k_].shape)

    p_in = [_halves(_wgrad(u, dproj, NSH, D, "wgrad_in"))] + pb
    p_d1, got_in = grad_down(a1, dh1, "wgrad_d1", exchange=_ex_pair_swap(p_in))
    sems_in, thru_in, token_in = _scatter_start(pair_sums(k_in, p_in, got_in), "scatter_in_start")
    dgu1, got_d1 = _ffn_bwd_act(dh1, gu1, wd1, "ffn1_bwd_act", exchange=_ex_pair_swap(p_d1), after=(token_in,))
    sems_d1, thru_d1, token_d1 = _scatter_start(pair_sums(k_d1, p_d1, got_d1), "scatter_d1_start")
    p_gu1 = [_halves(_wgrad(n1, dgu1, NSH, D, "wgrad_gu1", after=(token_in, token_d1)))]
    chip_in, owned_in = _scatter_wait(sems_in, thru_in, p_gu1, "scatter_in_wait")
    chip_d1, owned_d1 = _scatter_wait(sems_d1, thru_d1, p_gu1, "scatter_d1_wait")
    halves_in, halves_d1 = chip_sums(k_in, chip_in, owned_in), chip_sums(k_d1, chip_d1, owned_d1)
    landed = _exchange_alone(_join(_ex_pair_swap(p_gu1), _ex_share(halves_in)), "pair_swap_gu1")
    for i, k_ in enumerate(k_in):
        grad[k_] = landed[1 + i].reshape(wts[k_].shape)
    sems, thru, token = _scatter_start(pair_sums(k_gu1, p_gu1, landed[:1]), "scatter_gu1_start")
    for k_ in k_a + k_in:
        adamw(k_, after=(token,))
    dx, d_g1 = _ffn_bwd_in(dh1, x0, g1, dgu1, wgu1, "ffn1_bwd_in", after=(token,))
    small_g = dict(ffn1_norm=d_g1, mix_norm=d_gm, ffn2_norm=d_g2, final_norm=d_gf, pool_scale=d_scale,
                   pool_w_group=_wgrad_groups(pm, dyg), loss=loss_row)
    dev = 4 * lax.axis_index("x") + 2 * lax.axis_index("y") + lax.axis_index("c")
    slots = lax.dynamic_update_slice(jnp.zeros((8, SMALL_ROWS, 128), F32), _pack_small(small_g)[None], (dev, 0, 0))
    sems_s, slots, token_s = _small_gather_start(slots, "small_gather_start")

    chip_gu1, owned_gu1 = _scatter_wait(sems, thru, [dx] + [delta[k_] for k_ in k_a + k_in], "scatter_gu1_wait")
    both = _exchange_alone(_ex_share(halves_d1 + chip_sums(k_gu1, chip_gu1, owned_gu1)), "share_last",
                           after=(token_s,))
    grad["ffn1_w_down"] = both[0].reshape(ffn1_w_down.shape)
    grad["ffn1_w_gate_up"] = both[1].reshape(ffn1_w_gate_up.shape)
    for k_ in k_d1 + k_gu1:
        adamw(k_, after=(token_s,))
    gathered = _small_gather_wait(sems_s, slots, [delta[k_] for k_ in k_d1 + k_gu1], "small_gather_wait")
    gathered = gathered.reshape(8 * SMALL_ROWS, 128)
    heads, groups = _small_update(gathered, _pack_small(wts), _pack_small(mom), _pack_small(var))
    for dst, head, group in zip((grad, delta, new_m, new_v), heads, groups):
        vals = _unpack_small(head, group, wts)
        if dst is grad:
            loss = vals["loss"]
        vals.pop("loss")
        dst.update(vals)
    return (loss, dx[None], *[grad[k_] for k_ in ORDER], *[delta[k_] for k_ in ORDER],
            *[new_m[k_] for k_ in ORDER], *[new_v[k_] for k_ in ORDER])
```

```python
import dataclasses
import functools

import jax
import jax.numpy as jnp
from jax import lax
from jax.experimental import pallas as pl
from jax.experimental.pallas import tpu as pltpu

F32 = jnp.float32
BF16 = jnp.bfloat16

S = 2048
D = 1024
DFF = 2816
FFS = 2 * DFF // 4
NSH = 4
PW = 512
PG = 128
POOL_WINDOWS = (2, 4, 8, 16)
HALO = 16
SBW = 512
DH = 64
EPS = 1e-6
SCALE = 0.125
LOG2E = 1.4426950408889634
TA = 256
QB = 2
MIB = 1024 * 1024

LR, B1, B2, AEPS, WD, STEP = 0.001, 0.9, 0.999, 1e-08, 0.01, 10

_VM = pl.BlockSpec(memory_space=pltpu.VMEM)
_ANY = pl.BlockSpec(memory_space=pl.ANY)
MESH = pl.DeviceIdType.MESH
SIBLING_PAIR_ID = 1


def _nn(a, b):
    return jnp.dot(a, b, preferred_element_type=F32)


def _nt(a, b):
    return lax.dot_general(a, b, (((1,), (1,)), ((), ())), preferred_element_type=F32)


def _tn(a, b):
    return lax.dot_general(a, b, (((0,), (0,)), ((), ())), preferred_element_type=F32)


def _params(sem, vmem_mib):
    return pltpu.CompilerParams(dimension_semantics=sem, vmem_limit_bytes=vmem_mib * MIB)


def _rows(tm, width):
    return pl.BlockSpec((tm, width), lambda i: (i, 0))


def _fixed(shape):
    return pl.BlockSpec(shape, lambda *_: (0,) * len(shape))


def _sds(shape, dtype):
    return pltpu.HBM(shape, dtype)


def _in_hbm(args):
    return [pltpu.with_memory_space_constraint(a, pltpu.HBM) for a in args]


def _stage(pairs):
    @pl.when(pl.program_id(0) == 0)
    def _():
        for src, dst in pairs:
            pltpu.sync_copy(src, dst)


def _vmem_like(*arrays):
    return [pltpu.VMEM(a.shape, a.dtype) for a in arrays]


class Exchange:
    def __init__(self, arrays, landing, aliases, n_sems, start, finish, sibling_only=False):
        self.arrays, self.landing, self.aliases, self.n_sems = list(arrays), list(landing), dict(aliases), n_sems
        self.start, self.finish = start, finish
        self.sibling_only = sibling_only

    def enter(self):
        if self.sibling_only:
            barrier = pltpu.get_barrier_semaphore()
            sibling = (lax.axis_index("x"), lax.axis_index("y"), 1 - lax.axis_index("c"))
            pl.semaphore_signal(barrier, inc=1, device_id=sibling, device_id_type=MESH)
            pl.semaphore_wait(barrier, 1)

    def params(self, compiler_params=None):
        kw = dict(collective_id=SIBLING_PAIR_ID) if self.sibling_only else {}
        if compiler_params is None:
            return pltpu.CompilerParams(**kw)
        return dataclasses.replace(compiler_params, **kw)


def _join(a, b):
    na, la = len(a.arrays), len(a.landing)

    def both(fa, fb):
        def run(ins, outs, ssem, rsem):
            fa(ins[:na], outs[:la], ssem.at[pl.ds(0, a.n_sems)], rsem.at[pl.ds(0, a.n_sems)])
            fb(ins[na:], outs[la:], ssem.at[pl.ds(a.n_sems, b.n_sems)], rsem.at[pl.ds(a.n_sems, b.n_sems)])
        return run

    aliases = {**a.aliases, **{na + i: la + j for i, j in b.aliases.items()}}
    return Exchange(a.arrays + b.arrays, a.landing + b.landing, aliases, a.n_sems + b.n_sems,
                    both(a.start, b.start), both(a.finish, b.finish), a.sibling_only and b.sibling_only)


def _call(body, args, *, name, grid, in_specs, out_specs, out_shape, scratch_shapes=(), compiler_params=None,
          exchange=None, free=(), after=()):
    args = [a if i in free else pltpu.with_memory_space_constraint(a, pltpu.HBM) for i, a in enumerate(args)]
    if exchange is None:
        n_in = len(in_specs)

        def plain(*refs):
            body(*refs[:n_in], *refs[n_in + len(after):])

        return pl.pallas_call(plain, name=name, grid=grid, in_specs=list(in_specs) + [_ANY] * len(after),
                              out_specs=out_specs, out_shape=out_shape, scratch_shapes=list(scratch_shapes),
                              compiler_params=compiler_params)(*args, *after)
    ex = exchange
    n_in, n_out, n_scr = len(in_specs), len(out_specs), len(scratch_shapes)
    na, nl = len(ex.arrays), len(ex.landing)

    def hosted(*refs):
        at = [0]

        def take(n):
            at[0] += n
            return refs[at[0] - n:at[0]]

        k_in, _, e_in, k_out, e_out, k_scr = take(n_in), take(len(after)), take(na), take(n_out), take(nl), take(n_scr)
        ssem, rsem = take(2)
        ids = [pl.program_id(a) for a in range(len(grid))]
        first = functools.reduce(jnp.logical_and, [i == 0 for i in ids])
        last = functools.reduce(jnp.logical_and, [i == g - 1 for i, g in zip(ids, grid)])

        @pl.when(first)
        def _():
            ex.enter()
            ex.start(e_in, e_out, ssem, rsem)

        body(*k_in, *k_out, *k_scr)

        @pl.when(last)
        def _():
            ex.finish(e_in, e_out, ssem, rsem)

    outs = pl.pallas_call(
        hosted, name=name, grid=grid,
        in_specs=list(in_specs) + [_ANY] * (len(after) + na), out_specs=list(out_specs) + [_ANY] * nl,
        out_shape=list(out_shape) + ex.landing,
        scratch_shapes=list(scratch_shapes) + [pltpu.SemaphoreType.DMA((ex.n_sems,))] * 2,
        input_output_aliases={n_in + len(after) + i: n_out + j for i, j in ex.aliases.items()},
        compiler_params=ex.params(compiler_params),
    )(*args, *after, *_in_hbm(ex.arrays))
    return outs[:n_out], outs[n_out:]


def _exchange_alone(ex, name, after=()):
    na, nl = len(ex.arrays), len(ex.landing)

    def body(*refs):
        outs = refs[na + len(after):na + len(after) + nl]
        ex.enter()
        ex.start(refs[:na], outs, refs[-2], refs[-1])
        ex.finish(refs[:na], outs, refs[-2], refs[-1])

    return pl.pallas_call(
        body, name=name, in_specs=[_ANY] * (na + len(after)), out_specs=[_ANY] * nl,
        out_shape=ex.landing, scratch_shapes=[pltpu.SemaphoreType.DMA((ex.n_sems,))] * 2,
        input_output_aliases=ex.aliases, compiler_params=ex.params(),
    )(*_in_hbm(ex.arrays), *after)


_HBM = pl.BlockSpec(memory_space=pltpu.HBM)
_SEM = pl.BlockSpec(memory_space=pltpu.SEMAPHORE)
_EFFECT = pltpu.SideEffectType.DATAFLOW_SIDE_EFFECTING


def _scatter_copies(srcs, lands, ssems, rsems):
    x, y, c, chips = _place()
    return [_remote(srcs[w].at[2 * px + py], lands[w].at[k], ssems[3 * w + k], rsems[3 * w + k], (px, py, c))
            for w in range(len(srcs)) for k, (px, py) in enumerate(chips)]


def _scatter_start(parts, name):
    parts = list(parts)
    n, ncp = len(parts), 3 * len(parts)
    lands = [lax.empty((3,) + p.shape[1:], p.dtype) for p in parts]

    def body(*refs):
        srcs, land_refs = refs[:n], refs[n:2 * n]
        ssems, rsems = refs[2 * n:2 * n + ncp], refs[2 * n + ncp:2 * n + 2 * ncp]
        for cp in _scatter_copies(srcs, land_refs, ssems, rsems):
            cp.start()
        token = refs[-1]
        token[...] = jnp.zeros_like(token)

    outs = pl.pallas_call(
        body, name=name,
        out_shape=([pltpu.SemaphoreType.DMA(())] * (2 * ncp) + [pltpu.HBM(a.shape, a.dtype) for a in parts + lands]
                   + [jax.ShapeDtypeStruct((8, 128), F32)]),
        in_specs=[_HBM] * (2 * n), out_specs=[_SEM] * (2 * ncp) + [_HBM] * (2 * n) + [_VM],
        input_output_aliases={i: 2 * ncp + i for i in range(2 * n)},
        compiler_params=pltpu.CompilerParams(has_side_effects=_EFFECT),
    )(*_in_hbm(parts), *_in_hbm(lands))
    sems, thru, token = outs[:2 * ncp], outs[2 * ncp:2 * ncp + 2 * n], outs[-1]
    return sems, thru, token


def _scatter_wait(sems, thru, after, name):
    n = len(thru) // 2
    ncp = 3 * n

    def body(*refs):
        srcs, land_refs = refs[:n], refs[n:2 * n]
        ssems, rsems = refs[2 * n:2 * n + ncp], refs[2 * n + ncp:2 * n + 2 * ncp]
        for cp in _scatter_copies(srcs, land_refs, ssems, rsems):
            cp.wait_send()
            cp.wait_recv()

    outs = pl.pallas_call(
        body, name=name, out_shape=[pltpu.HBM(a.shape, a.dtype) for a in thru],
        in_specs=[_HBM] * (2 * n) + [_SEM] * (2 * ncp) + [_ANY] * len(after), out_specs=[_HBM] * (2 * n),
        input_output_aliases={i: i for i in range(2 * n)},
        compiler_params=pltpu.CompilerParams(has_side_effects=_EFFECT),
    )(*thru, *sems, *after)
    return outs[:n], outs[n:]


def _gather_copies(bufs, ssems, rsems, sending):
    x, y, c, chips = _place()
    out = []
    for w, ref in enumerate(bufs):
        half = ref.shape[1] // 2
        for k, (px, py) in enumerate(chips):
            rows = ref.at[2 * x + y if sending else 2 * px + py, pl.ds(c * half, half)]
            out.append(_remote(rows, rows, ssems[3 * w + k], rsems[3 * w + k], (px, py, c)))
    return out


def _gather_start(bufs, after, name):
    n, ncp = len(bufs), 3 * len(bufs)

    def body(*refs):
        ssems, rsems = refs[n + len(after):n + len(after) + ncp], refs[n + len(after) + ncp:n + len(after) + 2 * ncp]
        for cp in _gather_copies(refs[:n], ssems, rsems, True):
            cp.start()
        token = refs[-1]
        token[...] = jnp.zeros_like(token)

    outs = pl.pallas_call(
        body, name=name,
        out_shape=([pltpu.SemaphoreType.DMA(())] * (2 * ncp) + [pltpu.HBM(a.shape, a.dtype) for a in bufs]
                   + [jax.ShapeDtypeStruct((8, 128), F32)]),
        in_specs=[_HBM] * n + [_ANY] * len(after), out_specs=[_SEM] * (2 * ncp) + [_HBM] * n + [_VM],
        input_output_aliases={i: 2 * ncp + i for i in range(n)},
        compiler_params=pltpu.CompilerParams(has_side_effects=_EFFECT),
    )(*_in_hbm(bufs), *after)
    return outs[:2 * ncp], outs[2 * ncp:2 * ncp + n], outs[-1]


def _gather_wait(sems, thru, after, name):
    n = len(thru)
    ncp = 3 * n

    def body(*refs):
        ssems, rsems = refs[n:n + ncp], refs[n + ncp:n + 2 * ncp]
        for cp in _gather_copies(refs[:n], ssems, rsems, True):
            cp.wait_send()
        for cp in _gather_copies(refs[:n], ssems, rsems, False):
            cp.wait_recv()

    return pl.pallas_call(
        body, name=name, out_shape=[pltpu.HBM(a.shape, a.dtype) for a in thru],
        in_specs=[_HBM] * n + [_SEM] * (2 * ncp) + [_ANY] * len(after), out_specs=[_HBM] * n,
        input_output_aliases={i: i for i in range(n)},
        compiler_params=pltpu.CompilerParams(has_side_effects=_EFFECT),
    )(*thru, *sems, *after)


def _rms(x):
    r = lax.rsqrt(jnp.mean(x * x, axis=-1, keepdims=True) + EPS)
    return r, x * r


def _rms_bwd(dn, xr, r, gain):
    dng = dn * gain
    dx = r * (dng - xr * jnp.mean(dng * xr, axis=-1, keepdims=True))
    return dx, jnp.sum(dn * xr, axis=0, keepdims=True)


def _ffn_fwd(x, gain, wgu, wd, name, exchange=None, head=None):
    tm = 256

    def body(x_ref, g_ref, wgu_hbm, wd_hbm, *rest):
        if head is None:
            h_ref, n_ref, gu_ref, a_ref, wgu_ref, wd_ref = rest
        else:
            t_ref, gf_ref, h_ref, loss_ref, dgf_ref, n_ref, gu_ref, a_ref, wgu_ref, wd_ref = rest
        _stage([(wgu_hbm, wgu_ref), (wd_hbm, wd_ref)])
        x = x_ref[...]
        _, xr = _rms(x)
        n = (xr * g_ref[...]).astype(BF16)
        n_ref[...] = n
        acc = jnp.zeros((tm, D), F32)
        for j in range(2):
            g = _nn(n, wgu_ref[j])
            u = _nn(n, wgu_ref[2 + j])
            gu_ref[:, j * FFS:(j + 1) * FFS] = g.astype(BF16)
            gu_ref[:, (2 + j) * FFS:(3 + j) * FFS] = u.astype(BF16)
            half_act = (0.5 * (g * jax.nn.sigmoid(g) * u)).astype(BF16)
            a_ref[:, j * FFS:(j + 1) * FFS] = half_act
            acc = acc + _nn(half_act, wd_ref[j * FFS:(j + 1) * FFS, :])
        h = x + acc
        if head is None:
            h_ref[...] = h
            return
        gf = gf_ref[...]
        r, hr = _rms(h)
        err = hr * gf - t_ref[...]
        dh, dgain = _rms_bwd(err * (1.0 / D), hr, r, gf)
        h_ref[...] = dh

        @pl.when(pl.program_id(0) == 0)
        def _():
            dgf_ref[...] = jnp.zeros_like(dgf_ref)
            loss_ref[...] = jnp.zeros_like(loss_ref)

        dgf_ref[...] += dgain
        loss_ref[...] += jnp.full((1, 128), (0.5 / D) * jnp.sum(err * err), F32)

    saved_specs = [_rows(tm, D), _rows(tm, 4 * FFS), _rows(tm, DFF)]
    saved_shapes = [_sds((S, D), BF16), _sds((S, 4 * FFS), BF16), _sds((S, DFF), BF16)]
    if head is None:
        return _call(
            body, (x, gain, wgu, wd), name=name, grid=(S // tm,),
            in_specs=[_rows(tm, D), _fixed((1, D)), _ANY, _ANY],
            out_specs=[_rows(tm, D)] + saved_specs, out_shape=[_sds((S, D), F32)] + saved_shapes,
            scratch_shapes=_vmem_like(wgu, wd),
            compiler_params=_params(("arbitrary",), 56), exchange=exchange)
    return _call(
        body, (x, gain, wgu, wd, *head), name=name, grid=(S // tm,),
        in_specs=[_rows(tm, D), _fixed((1, D)), _ANY, _ANY, _rows(tm, D), _fixed((1, D))],
        out_specs=[_rows(tm, D), _fixed((1, 128)), _fixed((1, D))] + saved_specs,
        out_shape=[_sds((S, D), F32), _sds((1, 128), F32), _sds((1, D), F32)] + saved_shapes,
        scratch_shapes=_vmem_like(wgu, wd),
        compiler_params=_params(("arbitrary",), 56), exchange=exchange, free=(4, 5))


def _ffn_bwd(dh, x, gain, gu, wgu, wd, name):
    tm = 256

    def body(dh_ref, x_ref, g_ref, gu_ref, wgu_hbm, wd_hbm, dx_ref, dgu_ref, dg_ref, wgu_ref, wd_ref):
        _stage([(wgu_hbm, wgu_ref), (wd_hbm, wd_ref)])
        dh = dh_ref[...]
        dhb = dh.astype(BF16)
        dn = jnp.zeros((tm, D), F32)
        for j in range(2):
            g = gu_ref[:, j * FFS:(j + 1) * FFS].astype(F32)
            u = gu_ref[:, (2 + j) * FFS:(3 + j) * FFS].astype(F32)
            da = 0.5 * _nt(dhb, wd_ref[j * FFS:(j + 1) * FFS, :])
            sg = jax.nn.sigmoid(g)
            dgb = (da * u * (sg * (1.0 + g * (1.0 - sg)))).astype(BF16)
            dub = (da * (g * sg)).astype(BF16)
            dgu_ref[:, j * FFS:(j + 1) * FFS] = dgb
            dgu_ref[:, (2 + j) * FFS:(3 + j) * FFS] = dub
            dn = dn + _nt(dgb, wgu_ref[j]) + _nt(dub, wgu_ref[2 + j])
        r, xr = _rms(x_ref[...])
        dx, dgain = _rms_bwd(dn, xr, r, g_ref[...])
        dx_ref[...] = dh + dx

        @pl.when(pl.program_id(0) == 0)
        def _():
            dg_ref[...] = jnp.zeros_like(dg_ref)

        dg_ref[...] += dgain

    return _call(
        body, (dh, x, gain, gu, wgu, wd), name=name, grid=(S // tm,),
        in_specs=[_rows(tm, D), _rows(tm, D), _fixed((1, D)), _rows(tm, 4 * FFS), _ANY, _ANY],
        out_specs=[_rows(tm, D), _rows(tm, 4 * FFS), _fixed((1, D))],
        out_shape=[_sds((S, D), F32), _sds((S, 4 * FFS), BF16), _sds((1, D), F32)],
        scratch_shapes=_vmem_like(wgu, wd), compiler_params=_params(("arbitrary",), 56))


def _ffn_bwd_act(dh, gu, wd, name, exchange=None, after=()):
    tm = 512

    def body(dh_ref, gu_ref, wd_hbm, dgu_ref, wd_ref):
        _stage([(wd_hbm, wd_ref)])
        dhb = dh_ref[...].astype(BF16)
        for j in range(2):
            g = gu_ref[:, j * FFS:(j + 1) * FFS].astype(F32)
            u = gu_ref[:, (2 + j) * FFS:(3 + j) * FFS].astype(F32)
            da = 0.5 * _nt(dhb, wd_ref[j * FFS:(j + 1) * FFS, :])
            sg = jax.nn.sigmoid(g)
            dgu_ref[:, j * FFS:(j + 1) * FFS] = (da * u * (sg * (1.0 + g * (1.0 - sg)))).astype(BF16)
            dgu_ref[:, (2 + j) * FFS:(3 + j) * FFS] = (da * (g * sg)).astype(BF16)

    res = _call(
        body, (dh, gu, wd), name=name, grid=(S // tm,),
        in_specs=[_rows(tm, D), _rows(tm, 4 * FFS), _ANY], out_specs=[_rows(tm, 4 * FFS)],
        out_shape=[_sds((S, 4 * FFS), BF16)], scratch_shapes=_vmem_like(wd),
        compiler_params=_params(("arbitrary",), 56), exchange=exchange, after=after)
    return res[0] if exchange is None else (res[0][0], res[1])


def _ffn_bwd_in(dh, x, gain, dgu, wgu, name, exchange=None, after=()):
    tm = 512

    def body(dh_ref, x_ref, g_ref, dgu_ref, wgu_hbm, dx_ref, dg_ref, wgu_ref):
        _stage([(wgu_hbm, wgu_ref)])
        dn = jnp.zeros((tm, D), F32)
        for j in range(NSH):
            dn = dn + _nt(dgu_ref[:, j * FFS:(j + 1) * FFS], wgu_ref[j])
        r, xr = _rms(x_ref[...])
        dx, dgain = _rms_bwd(dn, xr, r, g_ref[...])
        dx_ref[...] = dh_ref[...] + dx

        @pl.when(pl.program_id(0) == 0)
        def _():
            dg_ref[...] = jnp.zeros_like(dg_ref)

        dg_ref[...] += dgain

    return _call(
        body, (dh, x, gain, dgu, wgu), name=name, grid=(S // tm,),
        in_specs=[_rows(tm, D), _rows(tm, D), _fixed((1, D)), _rows(tm, 4 * FFS), _ANY],
        out_specs=[_rows(tm, D), _fixed((1, D))],
        out_shape=[_sds((S, D), F32), _sds((1, D), F32)],
        scratch_shapes=_vmem_like(wgu),
        compiler_params=_params(("arbitrary",), 56), exchange=exchange, after=after)


def _mix_in(h, gain, w_in, after=()):
    tm = 512

    def body(h_ref, g_ref, w_hbm, u_ref, xp_ref, q_ref, k_ref, v_ref, gp_ref, gs_ref, w_ref):
        _stage([(w_hbm, w_ref)])
        _, hr = _rms(h_ref[...])
        u = (hr * g_ref[...]).astype(BF16)
        u_ref[...] = u
        p0 = _nn(u, w_ref[0])
        xp_ref[...] = p0[:, :PW]
        q_ref[...] = p0[:, PW:].astype(BF16)
        p1 = _nn(u, w_ref[1])
        k_ref[...] = p1[:, :SBW].astype(BF16)
        v_ref[...] = p1[:, SBW:].astype(BF16)
        gp_ref[...] = jax.nn.sigmoid(_nn(u, w_ref[2])).astype(BF16)
        gs_ref[...] = jax.nn.sigmoid(_nn(u, w_ref[3])).astype(BF16)

    return _call(
        body, (h, gain, w_in), name="mix_in", grid=(S // tm,),
        in_specs=[_rows(tm, D), _fixed((1, D)), _ANY],
        out_specs=[_rows(tm, D), _rows(tm, PW), _rows(tm, SBW), _rows(tm, SBW), _rows(tm, SBW),
                   _rows(tm, D), _rows(tm, D)],
        out_shape=[_sds((S, D), BF16), _sds((S, PW), F32), _sds((S, SBW), BF16), _sds((S, SBW), BF16),
                   _sds((S, SBW), BF16), _sds((S, D), BF16), _sds((S, D), BF16)],
        scratch_shapes=_vmem_like(w_in),
        compiler_params=_params(("arbitrary",), 48), free=(1,), after=after)


def _hilo_dot(x, tri):
    hi = x.astype(BF16)
    lo = (x - hi.astype(F32)).astype(BF16)
    return _nn(hi, tri) + _nn(lo, tri)


def _log_terms(qk):
    z2 = qk * (SCALE * LOG2E)
    lb = jnp.minimum(z2, 0.0) - jnp.log2(1.0 + jnp.exp2(-jnp.abs(z2)))
    return lb, lb - z2


def _head_masks():
    lane = lax.broadcasted_iota(jnp.int32, (1, 2 * DH), 1)
    return (lane < DH, lane >= DH)


def _attn_fwd(q, k, v, exchange=None):
    T = TA

    def body(q_ref, k_ref, v_ref, o_ref, c_ref):
        i2 = 2 * pl.program_id(1)
        row = lax.broadcasted_iota(jnp.int32, (T, T), 0)
        col = lax.broadcasted_iota(jnp.int32, (T, T), 1)
        after = (row > col).astype(BF16)
        causal = col < row
        masks = _head_masks()
        qms = {}
        for b in range(QB):
            q2 = q_ref[b * T:(b + 1) * T, :]
            for h, hm in enumerate(masks):
                qms[b, h] = jnp.where(hm, q2, jnp.zeros_like(q2))

        def blocks(keys, pairs, carries, os):
            ks, vms = [], []
            for j in keys:
                rows = pl.ds(pl.multiple_of(j * T, T), T)
                vj = v_ref[rows, :]
                ks.append(k_ref[rows, :])
                vms.append([jnp.where(hm, vj, jnp.zeros_like(vj)) for hm in masks])
            units = [(n, h) for n in range(len(pairs)) for h in range(2)]
            qks = {(n, h): _nt(qms[pairs[n][0], h], ks[pairs[n][1]]) for n, h in units}
            lbs, l1ms = {}, {}
            for u in units:
                lbs[u], l1m = _log_terms(qks[u])
                l1ms[u] = jnp.where(causal, l1m, 0.0) if pairs[u[0]][2] else l1m
            cins = {u: _hilo_dot(l1ms[u], after) for u in units}
            carries, os = dict(carries), list(os)
            for n, h in units:
                b, key, diag = pairs[n]
                a = jnp.exp2(lbs[n, h] + cins[n, h] + carries[b, h])
                if diag:
                    a = jnp.where(causal, a, 0.0)
                os[b] = os[b] + _nn(a.astype(BF16), vms[key][h])
                carries[b, h] = carries[b, h] + jnp.sum(l1ms[n, h], axis=1, keepdims=True)
            return carries, tuple(os)

        carries = {(b, h): jnp.zeros((T, 1), F32) for b in range(QB) for h in range(2)}
        os = tuple(jnp.zeros((T, 2 * DH), F32) for _ in range(QB))
        carries, os = blocks([i2 + 1, i2], [(1, 0, True), (0, 1, True), (1, 1, False)], carries, os)
        carries, os = lax.fori_loop(
            0, i2 // 2,
            lambda t, c: blocks([i2 - 1 - 2 * t, i2 - 2 - 2 * t],
                                [(0, 0, False), (1, 0, False), (0, 1, False), (1, 1, False)], c[0], c[1]),
            (carries, os))
        for b in range(QB):
            o_ref[b * T:(b + 1) * T, :] = os[b].astype(BF16)
            c_ref[b * T:(b + 1) * T, :] = jnp.where(masks[0], carries[b, 0], carries[b, 1])

    blk = pl.BlockSpec((QB * T, 2 * DH), lambda p, i: (i, p))
    full = pl.BlockSpec((S, 2 * DH), lambda p, i: (0, p))
    return _call(
        body, (q, k, v), name="attn_fwd", grid=(SBW // (2 * DH), S // (QB * T)),
        in_specs=[blk, full, full], out_specs=[blk, blk],
        out_shape=[_sds((S, SBW), BF16), _sds((S, SBW), F32)],
        compiler_params=_params(("arbitrary", "arbitrary"), 40), exchange=exchange)


def _attn_bwd(q, k, v, do, ctot, after=()):
    T = TA
    nq = S // (QB * T)

    def body(q_ref, k_ref, v_ref, do_ref, c_ref, dq_ref, dk_ref, dv_ref, dk_acc, dv_acc):
        step = pl.program_id(1)
        i2 = 2 * step

        @pl.when(step == 0)
        def _():
            dk_acc[...] = jnp.zeros_like(dk_acc)
            dv_acc[...] = jnp.zeros_like(dv_acc)

        row = lax.broadcasted_iota(jnp.int32, (T, T), 0)
        col = lax.broadcasted_iota(jnp.int32, (T, T), 1)
        upto = (row <= col).astype(BF16)
        before = (row < col).astype(BF16)
        causal = col < row
        masks = _head_masks()
        qms, doms, ctots = {}, {}, {}
        for b in range(QB):
            q2, do2 = q_ref[b * T:(b + 1) * T, :], do_ref[b * T:(b + 1) * T, :]
            for h, hm in enumerate(masks):
                qms[b, h] = jnp.where(hm, q2, jnp.zeros_like(q2))
                doms[b, h] = jnp.where(hm, do2, jnp.zeros_like(do2))
                ctots[b, h] = c_ref[b * T:(b + 1) * T, h * DH:h * DH + 1]

        def blocks(keys, pairs, sums, dqs):
            rows = [pl.ds(pl.multiple_of(j * T, T), T) for j in keys]
            ks, vs = [k_ref[r, :] for r in rows], [v_ref[r, :] for r in rows]
            kms = [[jnp.where(hm, kj, jnp.zeros_like(kj)) for hm in masks] for kj in ks]
            units = [(n, h) for n in range(len(pairs)) for h in range(2)]
            qks = {(n, h): _nt(qms[pairs[n][0], h], ks[pairs[n][1]]) for n, h in units}
            das = {(n, h): _nt(doms[pairs[n][0], h], vs[pairs[n][1]]) for n, h in units}
            lbs, l1ms = {}, {}
            for u in units:
                lbs[u], l1m = _log_terms(qks[u])
                l1ms[u] = jnp.where(causal, l1m, 0.0) if pairs[u[0]][2] else l1m
            pins = {u: _hilo_dot(l1ms[u], upto) for u in units}
            sums = dict(sums)
            a_s, dls, cps = {}, {}, {}
            for n, h in units:
                b, _, diag = pairs[n]
                cl, cp = sums[b, h]
                a = jnp.exp2(lbs[n, h] + (ctots[b, h] - cl) - pins[n, h])
                if diag:
                    a = jnp.where(causal, a, 0.0)
                a_s[n, h] = a.astype(BF16)
                dls[n, h] = das[n, h] * a
                cps[n, h] = cp
                sums[b, h] = (cl + jnp.sum(l1ms[n, h], axis=1, keepdims=True),
                              cp + jnp.sum(dls[n, h], axis=1, keepdims=True))
            pexs = {u: _hilo_dot(dls[u], before) for u in units}
            dzbs = {}
            for u in units:
                dz = dls[u] - jnp.exp2(lbs[u]) * (dls[u] + pexs[u] + cps[u])
                if pairs[u[0]][2]:
                    dz = jnp.where(causal, dz, 0.0)
                dzbs[u] = dz.astype(BF16)
            dqs = list(dqs)
            for n, h in units:
                dqs[pairs[n][0]] = dqs[pairs[n][0]] + _nn(dzbs[n, h], kms[pairs[n][1]][h])
            for key, r in enumerate(rows):
                mine = [(n, h) for n, h in units if pairs[n][1] == key]
                dk_acc[r, :] += functools.reduce(jnp.add, [_tn(dzbs[u], qms[pairs[u[0]][0], u[1]]) for u in mine])
                dv_acc[r, :] += functools.reduce(jnp.add, [_tn(a_s[u], doms[pairs[u[0]][0], u[1]]) for u in mine])
            return sums, tuple(dqs)

        zero = jnp.zeros((T, 1), F32)
        sums = {(b, h): (zero, zero) for b in range(QB) for h in range(2)}
        dqs = tuple(jnp.zeros((T, 2 * DH), F32) for _ in range(QB))
        sums, dqs = lax.fori_loop(
            0, i2 // 2,
            lambda t, c: blocks([2 * t, 2 * t + 1],
                                [(0, 0, False), (1, 0, False), (0, 1, False), (1, 1, False)], c[0], c[1]),
            (sums, dqs))
        _, dqs = blocks([i2, i2 + 1], [(0, 0, True), (1, 0, False), (1, 1, True)], sums, dqs)
        for b in range(QB):
            dq_ref[b * T:(b + 1) * T, :] = (dqs[b] * SCALE).astype(BF16)

        @pl.when(step == nq - 1)
        def _():
            dk_ref[...] = (dk_acc[...] * SCALE).astype(BF16)
            dv_ref[...] = dv_acc[...].astype(BF16)

    blk = pl.BlockSpec((QB * T, 2 * DH), lambda p, i: (i, p))
    full = pl.BlockSpec((S, 2 * DH), lambda p, i: (0, p))
    return _call(
        body, (q, k, v, do, ctot), name="attn_bwd", grid=(SBW // (2 * DH), nq),
        in_specs=[blk, full, full, blk, blk], out_specs=[blk, full, full],
        out_shape=[_sds((S, SBW), BF16), _sds((S, SBW), BF16), _sds((S, SBW), BF16)],
        scratch_shapes=[pltpu.VMEM((S, 2 * DH), F32), pltpu.VMEM((S, 2 * DH), F32)],
        compiler_params=_params(("arbitrary", "arbitrary"), 40), after=after)


def _pool_counts(first_row, tm):
    pos = first_row + lax.broadcasted_iota(jnp.int32, (tm, 1), 0)
    return [jnp.minimum(pos + 1, w).astype(F32) for w in POOL_WINDOWS]


def _mix_out(h, xp, o_sb, gp, gs, w_group, scale, w_bp, w_ba, w_out, exchange=None):
    tm = 512

    def body(h_ref, xp_ref, o_ref, gp_ref, gs_ref, wg_hbm, sc_ref, wbp_hbm, wba_hbm, wo_hbm,
             h2_ref, pm_ref, p_ref, yp_ref, ys_ref, m_ref, halo, wg_ref, wbp_ref, wba_ref, wo_ref):
        _stage([(wg_hbm, wg_ref), (wbp_hbm, wbp_ref), (wba_hbm, wba_ref), (wo_hbm, wo_ref)])
        i = pl.program_id(0)

        @pl.when(i == 0)
        def _():
            halo[...] = jnp.zeros_like(halo)

        xp = xp_ref[...]
        ext = jnp.concatenate([halo[...], xp], axis=0)
        halo[...] = xp[tm - HALO:, :]
        counts = _pool_counts(i * tm, tm)
        for gi in range(len(POOL_WINDOWS)):
            lanes = slice(gi * PG, (gi + 1) * PG)
            win = ext[:, lanes]
            for step in range(gi + 1):
                win = win + pltpu.roll(win, 1 << step, 0)
            pm = (win[HALO:, :] / counts[gi] - xp[:, lanes]).astype(BF16)
            pm_ref[:, lanes] = pm
            p_ref[:, lanes] = (_nn(pm, wg_ref[gi]) * sc_ref[:, lanes]).astype(BF16)
        pb = p_ref[...]
        ob = o_ref[...]
        for j in range(NSH):
            cols = slice(j * (D // NSH), (j + 1) * (D // NSH))
            yp = _nn(pb, wbp_ref[j])
            ys = _nn(ob, wba_ref[j])
            yp_ref[:, cols] = yp.astype(BF16)
            ys_ref[:, cols] = ys.astype(BF16)
            m_ref[:, cols] = (gp_ref[:, cols].astype(F32) * yp + gs_ref[:, cols].astype(F32) * ys).astype(BF16)
        h2_ref[...] = h_ref[...] + _nn(m_ref[...], wo_ref[...])

    return _call(
        body, (h, xp, o_sb, gp, gs, w_group, scale, w_bp, w_ba, w_out), name="mix_out", grid=(S // tm,),
        in_specs=[_rows(tm, D), _rows(tm, PW), _rows(tm, SBW), _rows(tm, D), _rows(tm, D),
                  _ANY, _fixed((1, PW)), _ANY, _ANY, _ANY],
        out_specs=[_rows(tm, D), _rows(tm, PW), _rows(tm, PW), _rows(tm, D), _rows(tm, D), _rows(tm, D)],
        out_shape=[_sds((S, D), F32), _sds((S, PW), BF16), _sds((S, PW), BF16), _sds((S, D), BF16),
                   _sds((S, D), BF16), _sds((S, D), BF16)],
        scratch_shapes=[pltpu.VMEM((HALO, PW), F32)] + _vmem_like(w_group, w_bp, w_ba, w_out),
        compiler_params=_params(("arbitrary",), 48), free=(5, 6), exchange=exchange)


def _mix_bwd_out(dh, gp, gs, yp, ys, pm, w_group, scale, w_bp, w_ba, w_out, exchange=None):
    tm = 512
    nt = S // tm

    def body(dh_ref, gp_ref, gs_ref, yp_ref, ys_ref, pm_ref, wg_hbm, sc_ref, wbp_hbm, wba_hbm, wo_hbm,
             dlg_ref, dyp_ref, dys_ref, do_ref, dyg_ref, dxp_ref, dsc_ref, halo, wg_ref, wbp_ref, wba_ref, wo_ref):
        _stage([(wg_hbm, wg_ref), (wbp_hbm, wbp_ref), (wba_hbm, wba_ref), (wo_hbm, wo_ref)])
        step = pl.program_id(0)

        @pl.when(step == 0)
        def _():
            halo[...] = jnp.zeros_like(halo)
            dsc_ref[...] = jnp.zeros_like(dsc_ref)

        dm = _nt(dh_ref[...].astype(BF16), wo_ref[...])
        gp = gp_ref[...].astype(F32)
        gs = gs_ref[...].astype(F32)
        yp = yp_ref[...].astype(F32)
        ys = ys_ref[...].astype(F32)
        dlg_ref[:, :D] = (dm * yp * gp * (1.0 - gp)).astype(BF16)
        dlg_ref[:, D:] = (dm * ys * gs * (1.0 - gs)).astype(BF16)
        dyp_ref[...] = (dm * gp).astype(BF16)
        dys_ref[...] = (dm * gs).astype(BF16)
        dp = jnp.zeros((tm, PW), F32)
        do = jnp.zeros((tm, SBW), F32)
        for j in range(NSH):
            cols = slice(j * (D // NSH), (j + 1) * (D // NSH))
            dp = dp + _nt(dyp_ref[:, cols], wbp_ref[j])
            do = do + _nt(dys_ref[:, cols], wba_ref[j])
        do_ref[...] = do.astype(BF16)
        counts = _pool_counts((nt - 1 - step) * tm, tm)
        dscale = []
        for gi in range(len(POOL_WINDOWS)):
            lanes = slice(gi * PG, (gi + 1) * PG)
            dpg = dp[:, lanes]
            dscale.append(jnp.sum(dpg * _nn(pm_ref[:, lanes], wg_ref[gi]), axis=0, keepdims=True))
            dyg = (dpg * sc_ref[:, lanes]).astype(BF16)
            dyg_ref[:, lanes] = dyg
            dpm = _nt(dyg, wg_ref[gi])
            per = dpm / counts[gi]
            win = jnp.concatenate([per, halo[:, lanes]], axis=0)
            halo[:, lanes] = per[:HALO, :]
            for s in range(gi + 1):
                win = win + pltpu.roll(win, tm + HALO - (1 << s), 0)
            dxp_ref[:, lanes] = (win[:tm, :] - dpm).astype(BF16)
        dsc_ref[...] += jnp.concatenate(dscale, axis=1)

    rev = lambda width: pl.BlockSpec((tm, width), lambda i: (nt - 1 - i, 0))
    return _call(
        body, (dh, gp, gs, yp, ys, pm, w_group, scale, w_bp, w_ba, w_out), name="mix_bwd_out", grid=(nt,),
        in_specs=[rev(D), rev(D), rev(D), rev(D), rev(D), rev(PW), _ANY, _fixed((1, PW)), _ANY, _ANY, _ANY],
        out_specs=[rev(2 * D), rev(D), rev(D), rev(SBW), rev(PW), rev(PW), _fixed((1, PW))],
        out_shape=[_sds((S, 2 * D), BF16), _sds((S, D), BF16), _sds((S, D), BF16), _sds((S, SBW), BF16),
                   _sds((S, PW), BF16), _sds((S, PW), BF16), _sds((1, PW), F32)],
        scratch_shapes=[pltpu.VMEM((HALO, PW), F32)] + _vmem_like(w_group, w_bp, w_ba, w_out),
        compiler_params=_params(("arbitrary",), 48), exchange=exchange)


def _mix_bwd_in(dh, h, gain, pieces, w_in, exchange=None):
    tm = 512
    widths = [p.shape[1] for p in pieces]

    def body(dh_ref, h_ref, g_ref, *rest):
        piece_refs, (w_hbm, dx_ref, dg_ref, dp_ref, w_ref) = rest[:len(pieces)], rest[len(pieces):]
        _stage([(w_hbm, w_ref)])
        at = 0
        for ref, width in zip(piece_refs, widths):
            dp_ref[:, at:at + width] = ref[...]
            at += width
        du = jnp.zeros((tm, D), F32)
        for j in range(NSH):
            du = du + _nt(dp_ref[:, j * D:(j + 1) * D], w_ref[j])
        r, hr = _rms(h_ref[...])
        dx, dgain = _rms_bwd(du, hr, r, g_ref[...])
        dx_ref[...] = dh_ref[...] + dx

        @pl.when(pl.program_id(0) == 0)
        def _():
            dg_ref[...] = jnp.zeros_like(dg_ref)

        dg_ref[...] += dgain

    return _call(
        body, (dh, h, gain, *pieces, w_in), name="mix_bwd_in", grid=(S // tm,),
        in_specs=[_rows(tm, D), _rows(tm, D), _fixed((1, D))] + [_rows(tm, w) for w in widths] + [_ANY],
        out_specs=[_rows(tm, D), _fixed((1, D)), _rows(tm, 4 * D)],
        out_shape=[_sds((S, D), F32), _sds((1, D), F32), _sds((S, 4 * D), BF16)],
        scratch_shapes=_vmem_like(w_in),
        compiler_params=_params(("arbitrary",), 48), exchange=exchange)


def _wgrad(a, b, nblk, ti, name, out_dtype=BF16, exchange=None, after=()):
    ka, n = a.shape[1], b.shape[1]
    ns = n // nblk

    def body(a_ref, b_ref, o_ref):
        o_ref[...] = _tn(a_ref[...].astype(BF16), b_ref[...].astype(BF16)).astype(out_dtype)

    res = _call(
        body, (a, b), name=name, grid=(nblk, ka // ti),
        in_specs=[pl.BlockSpec((S, ti), lambda j, i: (0, i)), pl.BlockSpec((S, ns), lambda j, i: (0, j))],
        out_specs=[pl.BlockSpec((None, ti, ns), lambda j, i: (j, i, 0))],
        out_shape=[_sds((nblk, ka, ns), out_dtype)],
        compiler_params=_params(("arbitrary", "arbitrary"), 56), exchange=exchange, after=after)
    return res[0] if exchange is None else (res[0][0], res[1])


def _wgrad_groups(pm, dyg):
    def body(a_ref, b_ref, o_ref):
        o_ref[...] = _tn(a_ref[...], b_ref[...])

    col = pl.BlockSpec((S, PG), lambda g: (0, g))
    return pl.pallas_call(
        body, name="wgrad_groups", grid=(PW // PG,),
        in_specs=[col, col], out_specs=pl.BlockSpec((None, PG, PG), lambda g: (g, 0, 0)),
        out_shape=_sds((PW // PG, PG, PG), F32),
        compiler_params=_params(("arbitrary",), 32),
    )(*_in_hbm([pm, dyg]))


def _place():
    x, y, c = lax.axis_index("x"), lax.axis_index("y"), lax.axis_index("c")
    chips = [(1 - x, y), (x, 1 - y), (1 - x, 1 - y)]
    return x, y, c, chips


def _remote(src, dst, ssem, rsem, dev):
    return pltpu.make_async_remote_copy(src_ref=src, dst_ref=dst, send_sem=ssem, recv_sem=rsem,
                                        device_id=dev, device_id_type=MESH)


def _cast_into_block(ws, me_idx, name):
    steps = 4
    shapes = [(w.shape[0] // steps, w.shape[1]) for w in ws]

    def body(me_ref, *refs):
        for w_ref, o_ref in zip(refs[:len(ws)], refs[len(ws):]):
            o_ref[...] = w_ref[...].astype(BF16)

    return pl.pallas_call(
        body, name=name, out_shape=[_sds((NSH,) + w.shape, BF16) for w in ws],
        grid_spec=pltpu.PrefetchScalarGridSpec(
            num_scalar_prefetch=1, grid=(steps,),
            in_specs=[pl.BlockSpec((r, c), lambda s, me: (s, 0)) for r, c in shapes],
            out_specs=[pl.BlockSpec((None, r, c), lambda s, me: (me[0], s, 0)) for r, c in shapes]),
        compiler_params=_params(("arbitrary",), 32),
    )(me_idx, *ws)


def _ex_gather(bufs):
    n = len(bufs)
    per = 8

    def plan(outs, ssem, rsem, w):
        x, y, c, _ = _place()
        sib, nbr_x, nbr_y = (x, y, 1 - c), (1 - x, y, c), (x, 1 - y, c)
        half = outs[w].shape[1] // 2
        quarter = half // 2
        sem = lambda k: (ssem.at[per * w + k], rsem.at[per * w + k])
        rows = lambda blk, start, size: outs[w].at[blk, pl.ds(start, size)]
        mine = rows(2 * x + y, c * half, half)
        from_x = rows(2 * (1 - x) + y, c * half, half)
        from_y = rows(2 * x + (1 - y), c * half, half)
        diag = 2 * (1 - x) + (1 - y)
        pass_y = rows(2 * (1 - x) + y, c * half, quarter)
        pass_x = rows(2 * x + (1 - y), c * half + quarter, quarter)
        diag_0, diag_1 = rows(diag, c * half, quarter), rows(diag, c * half + quarter, quarter)
        first = [_remote(mine, mine, *sem(0), nbr_x), _remote(mine, mine, *sem(1), nbr_y)]
        arrivals = [
            (_remote(from_x, from_x, *sem(0), nbr_x),
             [_remote(pass_y, pass_y, *sem(2), nbr_y), _remote(from_x, from_x, *sem(4), sib)]),
            (_remote(from_y, from_y, *sem(1), nbr_y),
             [_remote(pass_x, pass_x, *sem(3), nbr_x), _remote(from_y, from_y, *sem(5), sib)]),
            (_remote(diag_0, diag_0, *sem(2), nbr_y), [_remote(diag_0, diag_0, *sem(6), sib)]),
            (_remote(diag_1, diag_1, *sem(3), nbr_x), [_remote(diag_1, diag_1, *sem(7), sib)]),
        ]
        other = (1 - c) * half
        from_sibling = [
            _remote(rows(2 * (1 - x) + y, other, half), rows(2 * (1 - x) + y, other, half), *sem(4), sib),
            _remote(rows(2 * x + (1 - y), other, half), rows(2 * x + (1 - y), other, half), *sem(5), sib),
            _remote(rows(diag, other, quarter), rows(diag, other, quarter), *sem(6), sib),
            _remote(rows(diag, other + quarter, quarter), rows(diag, other + quarter, quarter), *sem(7), sib),
        ]
        return first, arrivals, from_sibling

    def start(ins, outs, ssem, rsem):
        x, y, c, _ = _place()
        for w in range(n):
            half = outs[w].shape[1] // 2
            mine = outs[w].at[2 * x + y, pl.ds(c * half, half)]
            _remote(mine, mine, ssem.at[per * w], rsem.at[per * w], (1 - x, y, c)).start()
            _remote(mine, mine, ssem.at[per * w + 1], rsem.at[per * w + 1], (x, 1 - y, c)).start()

    def finish(ins, outs, ssem, rsem):
        plans = [plan(outs, ssem, rsem, w) for w in range(n)]
        started = []
        for direct in (True, False):
            for first, arrivals, _ in plans:
                for arrived, onward in (arrivals[:2] if direct else arrivals[2:]):
                    arrived.wait_recv()
                    for cp in onward:
                        cp.start()
                    started += onward
        for first, _, from_sibling in plans:
            for cp in from_sibling:
                cp.wait_recv()
            started += first
        for cp in started:
            cp.wait_send()

    return Exchange(bufs, [_sds(b.shape, b.dtype) for b in bufs], {w: w for w in range(n)}, per * n, start, finish)


def _ex_gather_direct(bufs):
    n = len(bufs)

    def copies(outs, ssem, rsem, only_first=False):
        x, y, c, chips = _place()
        me, sib = 2 * x + y, (x, y, 1 - c)
        first, relay, last = [], [], []
        for w in range(n):
            half = outs[w].shape[1] // 2
            mine = outs[w].at[me, pl.ds(c * half, half)]
            for k, (px, py) in enumerate(chips):
                sems = (ssem.at[6 * w + k], rsem.at[6 * w + k])
                sib_sems = (ssem.at[6 * w + 3 + k], rsem.at[6 * w + 3 + k])
                first.append(_remote(mine, mine, *sems, (px, py, c)))
                if only_first:
                    continue
                got = outs[w].at[2 * px + py, pl.ds(c * half, half)]
                relay.append((_remote(got, got, *sems, (px, py, c)), _remote(got, got, *sib_sems, sib)))
                theirs = outs[w].at[2 * px + py, pl.ds((1 - c) * half, half)]
                last.append(_remote(theirs, theirs, *sib_sems, sib))
        return first, relay, last

    def start(ins, outs, ssem, rsem):
        for cp in copies(outs, ssem, rsem, only_first=True)[0]:
            cp.start()

    def finish(ins, outs, ssem, rsem):
        first, relay, last = copies(outs, ssem, rsem)
        for arrived, onward in relay:
            arrived.wait_recv()
            onward.start()
        for cp in last:
            cp.wait_recv()
        for cp in first:
            cp.wait_send()
        for _, onward in relay:
            onward.wait_send()

    return Exchange(bufs, [_sds(b.shape, b.dtype) for b in bufs], {w: w for w in range(n)}, 6 * n, start, finish)


def _simple_exchange(arrays, landing, aliases, make_copies, sibling_only=False):
    def start(ins, outs, ssem, rsem):
        for cp, _ in make_copies(ins, outs, ssem, rsem, False):
            cp.start()

    def finish(ins, outs, ssem, rsem):
        cps = make_copies(ins, outs, ssem, rsem, True)
        for _, landed in cps:
            landed.wait_recv()
        for cp, _ in cps:
            cp.wait_send()

    return Exchange(arrays, landing, aliases, len(arrays) * 3, start, finish, sibling_only)


def _ex_pair_swap(grads):
    def make(ins, outs, ssem, rsem, landing):
        x, y, c, _ = _place()
        cps = [_remote(ins[w].at[:, 1 - c], outs[w], ssem.at[w], rsem.at[w], (x, y, 1 - c))
               for w in range(len(grads))]
        return [(cp, cp) for cp in cps]

    return _simple_exchange(grads, [_sds((NSH,) + g.shape[2:], g.dtype) for g in grads], {}, make, True)


def _ex_scatter(parts):
    def make(ins, outs, ssem, rsem, landing):
        x, y, c, chips = _place()
        out = []
        for w in range(len(parts)):
            for k, (px, py) in enumerate(chips):
                sems = (ssem.at[3 * w + k], rsem.at[3 * w + k])
                out.append((_remote(ins[w].at[2 * px + py], outs[w].at[k], *sems, (px, py, c)),
                            _remote(outs[w].at[k], outs[w].at[k], *sems, (px, py, c)) if landing else None))
        return out

    return _simple_exchange(parts, [_sds((3,) + p.shape[1:], p.dtype) for p in parts], {}, make)


def _ex_relay(bufs):
    def make(ins, outs, ssem, rsem, landing):
        x, y, c, chips = _place()
        sib = (x, y, 1 - c)
        out = []
        for w in range(len(bufs)):
            half = outs[w].shape[1] // 2
            for k, (px, py) in enumerate(chips):
                sems = (ssem.at[3 * w + k], rsem.at[3 * w + k])
                have = outs[w].at[2 * px + py, pl.ds(c * half, half)]
                miss = outs[w].at[2 * px + py, pl.ds((1 - c) * half, half)]
                out.append((_remote(have, have, *sems, sib), _remote(miss, miss, *sems, sib) if landing else None))
        return out

    return _simple_exchange(bufs, [_sds(b.shape, b.dtype) for b in bufs], {w: w for w in range(len(bufs))}, make, True)


def _ex_share(bufs):
    def make(ins, outs, ssem, rsem, landing):
        x, y, c, _ = _place()
        sib = (x, y, 1 - c)
        return [(_remote(outs[w].at[c], outs[w].at[c], ssem.at[w], rsem.at[w], sib),
                 _remote(outs[w].at[1 - c], outs[w].at[1 - c], ssem.at[w], rsem.at[w], sib) if landing else None)
                for w in range(len(bufs))]

    return _simple_exchange(bufs, [_sds(b.shape, b.dtype) for b in bufs], {w: w for w in range(len(bufs))}, make, True)


def _small_copies(slots, ssems, rsems, sending):
    x, y, c, _ = _place()
    out = []
    for m in range(1, 8):
        px, py, pc = x ^ (m >> 2), y ^ ((m >> 1) & 1), c ^ (m & 1)
        slot = slots.at[4 * x + 2 * y + c if sending else 4 * px + 2 * py + pc]
        out.append(_remote(slot, slot, ssems[m - 1], rsems[m - 1], (px, py, pc)))
    return out


def _small_gather_start(slots, name):
    def body(*refs):
        for cp in _small_copies(refs[0], refs[1:8], refs[8:15], True):
            cp.start()
        refs[-1][...] = jnp.zeros_like(refs[-1])

    outs = pl.pallas_call(
        body, name=name,
        out_shape=([pltpu.SemaphoreType.DMA(())] * 14 + [pltpu.HBM(slots.shape, slots.dtype)]
                   + [jax.ShapeDtypeStruct((8, 128), F32)]),
        in_specs=[_HBM], out_specs=[_SEM] * 14 + [_HBM, _VM], input_output_aliases={0: 14},
        compiler_params=pltpu.CompilerParams(has_side_effects=_EFFECT),
    )(*_in_hbm([slots]))
    return outs[:14], outs[14], outs[15]


def _small_gather_wait(sems, slots, after, name):
    def body(*refs):
        for cp in _small_copies(refs[0], refs[1:8], refs[8:15], True):
            cp.wait_send()
        for cp in _small_copies(refs[0], refs[1:8], refs[8:15], False):
            cp.wait_recv()

    return pl.pallas_call(
        body, name=name, out_shape=pltpu.HBM(slots.shape, slots.dtype),
        in_specs=[_HBM] + [_SEM] * 14 + [_ANY] * len(after), out_specs=_HBM, input_output_aliases={0: 0},
        compiler_params=pltpu.CompilerParams(has_side_effects=_EFFECT),
    )(slots, *sems, *after)


def _row_block(rows, cap=256):
    return max(t for t in range(16, cap + 1, 16) if rows % t == 0)


def _pair_sum(grads, gots, c_idx, name):
    n = len(grads)

    def body(c_ref, *refs):
        for a_ref, b_ref, o_ref in zip(refs[:n], refs[n:2 * n], refs[2 * n:]):
            o_ref[...] = (a_ref[...].astype(F32) + b_ref[...].astype(F32)).astype(BF16)

    halves = [g.shape[2:] for g in grads]
    return list(pl.pallas_call(
        body, name=name, out_shape=[_sds((NSH,) + h, BF16) for h in halves],
        grid_spec=pltpu.PrefetchScalarGridSpec(
            num_scalar_prefetch=1, grid=(NSH,),
            in_specs=[pl.BlockSpec((None, None) + h, lambda j, c: (j, c[0], 0, 0)) for h in halves]
            + [pl.BlockSpec((None,) + h, lambda j, c: (j, 0, 0)) for h in halves],
            out_specs=[pl.BlockSpec((None,) + h, lambda j, c: (j, 0, 0)) for h in halves]),
        compiler_params=_params(("arbitrary",), 40),
    )(c_idx, *_in_hbm(list(grads) + list(gots))))


def _chip_sum(owns, gots, place, name):
    n = len(owns)

    def body(place_ref, *refs):
        for own_ref, got_ref, o_ref in zip(refs[:n], refs[n:2 * n], refs[2 * n:]):
            acc = own_ref[...].astype(F32)
            for k in range(3):
                acc = acc + got_ref[k].astype(F32)
            o_ref[...] = acc

    shapes = [(o.shape[1] // 2, o.shape[2]) for o in owns]
    return list(pl.pallas_call(
        body, name=name, out_shape=[_sds((2, 2 * r, c), F32) for r, c in shapes],
        grid_spec=pltpu.PrefetchScalarGridSpec(
            num_scalar_prefetch=1, grid=(2,),
            in_specs=[pl.BlockSpec((None, r, c), lambda s, p: (p[0], s, 0)) for r, c in shapes]
            + [pl.BlockSpec((3, r, c), lambda s, p: (0, s, 0)) for r, c in shapes],
            out_specs=[pl.BlockSpec((None, r, c), lambda s, p: (p[1], s, 0)) for r, c in shapes]),
        compiler_params=_params(("arbitrary",), 40),
    )(place, *_in_hbm(list(owns) + list(gots))))


def _adamw_math(w, g, m, v):
    m = B1 * m + (1.0 - B1) * g
    v = B2 * v + (1.0 - B2) * (g * g)
    m_hat = m / (1.0 - B1 ** STEP)
    v_hat = v / (1.0 - B2 ** STEP)
    return -LR * (m_hat / (jnp.sqrt(v_hat) + AEPS) + WD * w), m, v


def _adamw(w, g, m, v, name, after=()):
    rows, cols = w.shape
    tr = _row_block(rows)

    def body(w_ref, g_ref, m_ref, v_ref, go_ref, d_ref, nm_ref, nv_ref):
        g = g_ref[...]
        go_ref[...] = g
        d_ref[...], nm_ref[...], nv_ref[...] = _adamw_math(w_ref[...], g, m_ref[...], v_ref[...])

    blk = pl.BlockSpec((tr, cols), lambda r: (r, 0))
    return _call(
        body, (w, g, m, v), name=name, grid=(rows // tr,), out_shape=[_sds(w.shape, F32)] * 4,
        in_specs=[blk] * 4, out_specs=[blk] * 4,
        compiler_params=_params(("arbitrary",), 32), free=(0, 2, 3), after=after)


def _small_update(gathered, w, m, v):
    rows = w.shape[0]

    def body(ga_ref, w_ref, m_ref, v_ref, *out_refs):
        g = ga_ref[0:rows, :]
        for dev in range(1, 8):
            g = g + ga_ref[dev * rows:(dev + 1) * rows, :]
        results = (g,) + _adamw_math(w_ref[...], g, m_ref[...], v_ref[...])
        for i, res in enumerate(results):
            out_refs[i][...] = res[:SMALL_HEAD, :]
            out_refs[4 + i][...] = res[SMALL_HEAD:, :]

    outs = pl.pallas_call(
        body, name="small_update",
        out_shape=[jax.ShapeDtypeStruct((SMALL_HEAD, 128), F32)] * 4
        + [jax.ShapeDtypeStruct((rows - SMALL_HEAD, 128), F32)] * 4,
        in_specs=[_VM] * 4, out_specs=[_VM] * 8,
    )(gathered, w, m, v)
    return outs[:4], outs[4:]


SMALL = ("ffn1_norm", "mix_norm", "ffn2_norm", "final_norm", "pool_scale", "loss", "pool_w_group")
SMALL_HEAD = 48
BIG = ("ffn1_w_gate_up", "ffn1_w_down", "w_in", "w_branch_pool", "w_branch_attn", "w_out",
       "ffn2_w_gate_up", "ffn2_w_down")
ORDER = ("ffn1_norm", "ffn1_w_gate_up", "ffn1_w_down", "mix_norm", "w_in", "pool_w_group", "pool_scale",
         "w_branch_pool", "w_branch_attn", "w_out", "ffn2_norm", "ffn2_w_gate_up", "ffn2_w_down", "final_norm")
SMALL_ROWS = 560


def _pack_small(t):
    parts = []
    for k in SMALL:
        rows = t[k].reshape(-1, 128) if k in t else jnp.zeros((1, 128), F32)
        parts.append(jnp.pad(rows, ((0, -rows.shape[0] % 8), (0, 0))))
    packed = jnp.concatenate(parts, axis=0)
    assert packed.shape == (SMALL_ROWS, 128), packed.shape
    return packed


def _unpack_small(head, group, like):
    out, at = {"pool_w_group": group.reshape(like["pool_w_group"].shape)}, 0
    for k in SMALL[:-1]:
        n = like[k].size // 128 if k in like else 1
        out[k] = head[at:at + n].reshape(like[k].shape) if k in like else head[at, 0]
        at += n + (-n % 8)
    return out


def _halves(g):
    return g.reshape(NSH, 2, g.shape[1] // 2, g.shape[2])


def kernel(x, ffn1_norm, ffn1_w_gate_up, ffn1_w_down, mix_norm, w_in, pool_w_group, pool_scale, w_branch_pool, w_branch_attn, w_out, ffn2_norm, ffn2_w_gate_up, ffn2_w_down, final_norm, loss_target, m_ffn1_norm, m_ffn1_w_gate_up, m_ffn1_w_down, m_mix_norm, m_w_in, m_pool_w_group, m_pool_scale, m_w_branch_pool, m_w_branch_attn, m_w_out, m_ffn2_norm, m_ffn2_w_gate_up, m_ffn2_w_down, m_final_norm, v_ffn1_norm, v_ffn1_w_gate_up, v_ffn1_w_down, v_mix_norm, v_w_in, v_pool_w_group, v_pool_scale, v_w_branch_pool, v_w_branch_attn, v_w_out, v_ffn2_norm, v_ffn2_w_gate_up, v_ffn2_w_down, v_final_norm):
    wts = dict(ffn1_norm=ffn1_norm, ffn1_w_gate_up=ffn1_w_gate_up, ffn1_w_down=ffn1_w_down, mix_norm=mix_norm,
               w_in=w_in, pool_w_group=pool_w_group, pool_scale=pool_scale, w_branch_pool=w_branch_pool,
               w_branch_attn=w_branch_attn, w_out=w_out, ffn2_norm=ffn2_norm, ffn2_w_gate_up=ffn2_w_gate_up,
               ffn2_w_down=ffn2_w_down, final_norm=final_norm)
    mom = dict(ffn1_norm=m_ffn1_norm, ffn1_w_gate_up=m_ffn1_w_gate_up, ffn1_w_down=m_ffn1_w_down,
               mix_norm=m_mix_norm, w_in=m_w_in, pool_w_group=m_pool_w_group, pool_scale=m_pool_scale,
               w_branch_pool=m_w_branch_pool, w_branch_attn=m_w_branch_attn, w_out=m_w_out,
               ffn2_norm=m_ffn2_norm, ffn2_w_gate_up=m_ffn2_w_gate_up, ffn2_w_down=m_ffn2_w_down,
               final_norm=m_final_norm)
    var = dict(ffn1_norm=v_ffn1_norm, ffn1_w_gate_up=v_ffn1_w_gate_up, ffn1_w_down=v_ffn1_w_down,
               mix_norm=v_mix_norm, w_in=v_w_in, pool_w_group=v_pool_w_group, pool_scale=v_pool_scale,
               w_branch_pool=v_w_branch_pool, w_branch_attn=v_w_branch_attn, w_out=v_w_out,
               ffn2_norm=v_ffn2_norm, ffn2_w_gate_up=v_ffn2_w_gate_up, ffn2_w_down=v_ffn2_w_down,
               final_norm=v_final_norm)

    c_idx = lax.axis_index("c").astype(jnp.int32).reshape(1)
    me_idx = (2 * lax.axis_index("x") + lax.axis_index("y")).astype(jnp.int32).reshape(1)
    place = jnp.concatenate([me_idx, c_idx])
    x0, tgt = x[0], loss_target[0]
    wgrp = pool_w_group[0].astype(BF16)
    g1, gm, g2, gf = ffn1_norm, mix_norm, ffn2_norm, final_norm.reshape(1, D)
    grad, delta, new_m, new_v = {}, {}, {}, {}

    def pair_sums(keys, parts, got):
        return _pair_sum(parts, got, c_idx, "pair_sum_" + keys[0])

    def chip_sums(keys, chip_parts, owned):
        return _chip_sum(chip_parts, owned, place, "chip_sum_" + keys[0])

    def adamw(k, after=()):
        outs = _adamw(wts[k][0], grad[k][0], mom[k][0], var[k][0], "adamw_" + k, after=after)
        grad[k], delta[k], new_m[k], new_v[k] = (o.reshape(wts[k].shape) for o in outs)

    first, late = ("ffn1_w_gate_up", "ffn1_w_down"), ("w_branch_pool", "w_branch_attn", "w_out",
                                                       "ffn2_w_gate_up", "ffn2_w_down")
    own = {}
    for group in (first, ("w_in",), late):
        own.update(zip(group, _cast_into_block([wts[k][0] for k in group], me_idx, "cast_" + group[0])))
    full = dict(zip(first, _exchange_alone(_ex_gather([own[k] for k in first]), "gather_ffn1")))
    wgu1, wd1 = full["ffn1_w_gate_up"], full["ffn1_w_down"].reshape(DFF, D)
    (h1, n1, gu1, a1), (win,) = _ffn_fwd(x0, g1, wgu1, wd1, "ffn1_fwd", exchange=_ex_gather_direct([own["w_in"]]))
    sems_l, thru_l, token_l = _gather_start([own[k_] for k_ in late], [h1], "gather_late_start")
    u, xp, q, k, v, gp, gs = _mix_in(h1, gm, win, after=(token_l,))
    o_sb, ctot = _attn_fwd(q, k, v)
    arrived = _gather_wait(sems_l, thru_l, [o_sb], "gather_late_wait")
    wbp, wba, wout = _exchange_alone(_ex_relay(arrived[:3]), "relay_mix")
    wout = wout.reshape(D, D)
    (h2, pm, p, yp, ys, mm), (wgu2, wd2) = _mix_out(h1, xp, o_sb, gp, gs, wgrp, pool_scale, wbp, wba, wout,
                                                    exchange=_ex_relay(arrived[3:]))
    wd2 = wd2.reshape(DFF, D)
    dh3, loss_row, d_gf, n3, gu3, a3 = _ffn_fwd(h2, g2, wgu2, wd2, "ffn2_fwd", head=(tgt, gf))

    def grad_gate_up(n, dgu, name, exchange=None):
        res = _wgrad(n, dgu, NSH, D, name, exchange=exchange)
        return [_halves(res)] if exchange is None else ([_halves(res[0])], res[1])

    def grad_down(a, dh, name, exchange=None):
        res = _wgrad(a, dh, 1, FFS, name, exchange=exchange)
        halves = lambda g: [_halves(g.reshape(NSH, DFF // NSH, D))]
        return halves(res) if exchange is None else (halves(res[0]), res[1])

    k_gu2, k_d2, k_gu1, k_d1, k_in = (("ffn2_w_gate_up",), ("ffn2_w_down",), ("ffn1_w_gate_up",),
                                      ("ffn1_w_down",), ("w_in",))
    dh2, dgu3, d_g2 = _ffn_bwd(dh3, h2, g2, gu3, wgu2, wd2, "ffn2_bwd")
    pa = grad_gate_up(n3, dgu3, "wgrad_gu2") + grad_down(a3, dh3, "wgrad_d2")
    (dlg, dyp, dys, do_sb, dyg, dxp, d_scale), got_a = _mix_bwd_out(
        dh2, gp, gs, yp, ys, pm, wgrp, pool_scale, wbp, wba, wout, exchange=_ex_pair_swap(pa))
    chip_a = pair_sums(k_gu2 + k_d2, pa, got_a)
    kb = ("w_out", "w_branch_pool", "w_branch_attn")
    pb = [_halves(_wgrad(mm, dh2, 1, D, "wgrad_out").reshape(NSH, D // NSH, D)),
          _halves(_wgrad(p, dyp, NSH, PW, "wgrad_bp")), _halves(_wgrad(o_sb, dys, NSH, SBW, "wgrad_ba"))]
    k_a, k_in = k_gu2 + k_d2, k_in + kb
    sems_a, thru_a, token_a = _scatter_start(chip_a, "scatter_a_start")
    dq, dk, dv = _attn_bwd(q, k, v, do_sb, ctot, after=(token_a,))
    chip_a, owned_a = _scatter_wait(sems_a, thru_a, [dq], "scatter_a_wait")
    halves_a = chip_sums(k_a, chip_a, owned_a)
    (dh1, d_gm, dproj), both_a = _mix_bwd_in(dh2, h1, gm, (dxp, dq, dk, dv, dlg), win, exchange=_ex_share(halves_a))
    for i, k_ in enumerate(k_a):
        grad[k_] = both_a[i].reshape(wts[k_].shape)

    p_in = [_halves(_wgrad(u, dproj, NSH, D, "wgrad_in"))] + pb
    p_d1, got_in = grad_down(a1, dh1, "wgrad_d1", exchange=_ex_pair_swap(p_in))
    sems_in, thru_in, token_in = _scatter_start(pair_sums(k_in, p_in, got_in), "scatter_in_start")
    dgu1, got_d1 = _ffn_bwd_act(dh1, gu1, wd1, "ffn1_bwd_act", exchange=_ex_pair_swap(p_d1), after=(token_in,))
    sems_d1, thru_d1, token_d1 = _scatter_start(pair_sums(k_d1, p_d1, got_d1), "scatter_d1_start")
    p_gu1 = [_halves(_wgrad(n1, dgu1, NSH, D, "wgrad_gu1", after=(token_in, token_d1)))]
    chip_in, owned_in = _scatter_wait(sems_in, thru_in, p_gu1, "scatter_in_wait")
    chip_d1, owned_d1 = _scatter_wait(sems_d1, thru_d1, p_gu1, "scatter_d1_wait")
    halves_in, halves_d1 = chip_sums(k_in, chip_in, owned_in), chip_sums(k_d1, chip_d1, owned_d1)
    landed = _exchange_alone(_join(_ex_pair_swap(p_gu1), _ex_share(halves_in)), "pair_swap_gu1")
    for i, k_ in enumerate(k_in):
        grad[k_] = landed[1 + i].reshape(wts[k_].shape)
    sems, thru, token = _scatter_start(pair_sums(k_gu1, p_gu1, landed[:1]), "scatter_gu1_start")
    for k_ in k_a + k_in:
        adamw(k_, after=(token,))
    dx, d_g1 = _ffn_bwd_in(dh1, x0, g1, dgu1, wgu1, "ffn1_bwd_in", after=(token,))
    small_g = dict(ffn1_norm=d_g1, mix_norm=d_gm, ffn2_norm=d_g2, final_norm=d_gf, pool_scale=d_scale,
                   pool_w_group=_wgrad_groups(pm, dyg), loss=loss_row)
    dev = 4 * lax.axis_index("x") + 2 * lax.axis_index("y") + lax.axis_index("c")
    slots = lax.dynamic_update_slice(jnp.zeros((8, SMALL_ROWS, 128), F32), _pack_small(small_g)[None], (dev, 0, 0))
    sems_s, slots, token_s = _small_gather_start(slots, "small_gather_start")

    chip_gu1, owned_gu1 = _scatter_wait(sems, thru, [dx] + [delta[k_] for k_ in k_a + k_in], "scatter_gu1_wait")
    both = _exchange_alone(_ex_share(halves_d1 + chip_sums(k_gu1, chip_gu1, owned_gu1)), "share_last",
                           after=(token_s,))
    grad["ffn1_w_down"] = both[0].reshape(ffn1_w_down.shape)
    grad["ffn1_w_gate_up"] = both[1].reshape(ffn1_w_gate_up.shape)
    for k_ in k_d1 + k_gu1:
        adamw(k_, after=(token_s,))
    gathered = _small_gather_wait(sems_s, slots, [delta[k_] for k_ in k_d1 + k_gu1], "small_gather_wait")
    gathered = gathered.reshape(8 * SMALL_ROWS, 128)
    heads, groups = _small_update(gathered, _pack_small(wts), _pack_small(mom), _pack_small(var))
    for dst, head, group in zip((grad, delta, new_m, new_v), heads, groups):
        vals = _unpack_small(head, group, wts)
        if dst is grad:
            loss = vals["loss"]
        vals.pop("loss")
        dst.update(vals)
    return (loss, dx[None], *[grad[k_] for k_ in ORDER], *[delta[k_] for k_ in ORDER],
            *[new_m[k_] for k_ in ORDER], *[new_v[k_] for k_ in ORDER])
```

```python
import dataclasses
import functools

import jax
import jax.numpy as jnp
from jax import lax
from jax.experimental import pallas as pl
from jax.experimental.pallas import tpu as pltpu

F32 = jnp.float32
BF16 = jnp.bfloat16

S = 2048
D = 1024
DFF = 2816
FFS = 2 * DFF // 4
NSH = 4
PW = 512
PG = 128
POOL_WINDOWS = (2, 4, 8, 16)
HALO = 16
SBW = 512
DH = 64
EPS = 1e-6
SCALE = 0.125
LOG2E = 1.4426950408889634
TA = 256
QB = 2
MIB = 1024 * 1024

LR, B1, B2, AEPS, WD, STEP = 0.001, 0.9, 0.999, 1e-08, 0.01, 10

_VM = pl.BlockSpec(memory_space=pltpu.VMEM)
_ANY = pl.BlockSpec(memory_space=pl.ANY)
MESH = pl.DeviceIdType.MESH
SIBLING_PAIR_ID = 1


def _nn(a, b):
    return jnp.dot(a, b, preferred_element_type=F32)


def _nt(a, b):
    return lax.dot_general(a, b, (((1,), (1,)), ((), ())), preferred_element_type=F32)


def _tn(a, b):
    return lax.dot_general(a, b, (((0,), (0,)), ((), ())), preferred_element_type=F32)


def _params(sem, vmem_mib):
    return pltpu.CompilerParams(dimension_semantics=sem, vmem_limit_bytes=vmem_mib * MIB)


def _rows(tm, width):
    return pl.BlockSpec((tm, width), lambda i: (i, 0))


def _fixed(shape):
    return pl.BlockSpec(shape, lambda *_: (0,) * len(shape))


def _sds(shape, dtype):
    return pltpu.HBM(shape, dtype)


def _in_hbm(args):
    return [pltpu.with_memory_space_constraint(a, pltpu.HBM) for a in args]


def _stage(pairs):
    @pl.when(pl.program_id(0) == 0)
    def _():
        for src, dst in pairs:
            pltpu.sync_copy(src, dst)


def _vmem_like(*arrays):
    return [pltpu.VMEM(a.shape, a.dtype) for a in arrays]


class Exchange:
    def __init__(self, arrays, landing, aliases, n_sems, start, finish, sibling_only=False):
        self.arrays, self.landing, self.aliases, self.n_sems = list(arrays), list(landing), dict(aliases), n_sems
        self.start, self.finish = start, finish
        self.sibling_only = sibling_only

    def enter(self):
        if self.sibling_only:
            barrier = pltpu.get_barrier_semaphore()
            sibling = (lax.axis_index("x"), lax.axis_index("y"), 1 - lax.axis_index("c"))
            pl.semaphore_signal(barrier, inc=1, device_id=sibling, device_id_type=MESH)
            pl.semaphore_wait(barrier, 1)

    def params(self, compiler_params=None):
        kw = dict(collective_id=SIBLING_PAIR_ID) if self.sibling_only else {}
        if compiler_params is None:
            return pltpu.CompilerParams(**kw)
        return dataclasses.replace(compiler_params, **kw)


def _join(a, b):
    na, la = len(a.arrays), len(a.landing)

    def both(fa, fb):
        def run(ins, outs, ssem, rsem):
            fa(ins[:na], outs[:la], ssem.at[pl.ds(0, a.n_sems)], rsem.at[pl.ds(0, a.n_sems)])
            fb(ins[na:], outs[la:], ssem.at[pl.ds(a.n_sems, b.n_sems)], rsem.at[pl.ds(a.n_sems, b.n_sems)])
        return run

    aliases = {**a.aliases, **{na + i: la + j for i, j in b.aliases.items()}}
    return Exchange(a.arrays + b.arrays, a.landing + b.landing, aliases, a.n_sems + b.n_sems,
                    both(a.start, b.start), both(a.finish, b.finish), a.sibling_only and b.sibling_only)


def _call(body, args, *, name, grid, in_specs, out_specs, out_shape, scratch_shapes=(), compiler_params=None,
          exchange=None, free=(), after=()):
    args = [a if i in free else pltpu.with_memory_space_constraint(a, pltpu.HBM) for i, a in enumerate(args)]
    if exchange is None:
        n_in = len(in_specs)

        def plain(*refs):
            body(*refs[:n_in], *refs[n_in + len(after):])

        return pl.pallas_call(plain, name=name, grid=grid, in_specs=list(in_specs) + [_ANY] * len(after),
                              out_specs=out_specs, out_shape=out_shape, scratch_shapes=list(scratch_shapes),
                              compiler_params=compiler_params)(*args, *after)
    ex = exchange
    n_in, n_out, n_scr = len(in_specs), len(out_specs), len(scratch_shapes)
    na, nl = len(ex.arrays), len(ex.landing)

    def hosted(*refs):
        at = [0]

        def take(n):
            at[0] += n
            return refs[at[0] - n:at[0]]

        k_in, _, e_in, k_out, e_out, k_scr = take(n_in), take(len(after)), take(na), take(n_out), take(nl), take(n_scr)
        ssem, rsem = take(2)
        ids = [pl.program_id(a) for a in range(len(grid))]
        first = functools.reduce(jnp.logical_and, [i == 0 for i in ids])
        last = functools.reduce(jnp.logical_and, [i == g - 1 for i, g in zip(ids, grid)])

        @pl.when(first)
        def _():
            ex.enter()
            ex.start(e_in, e_out, ssem, rsem)

        body(*k_in, *k_out, *k_scr)

        @pl.when(last)
        def _():
            ex.finish(e_in, e_out, ssem, rsem)

    outs = pl.pallas_call(
        hosted, name=name, grid=grid,
        in_specs=list(in_specs) + [_ANY] * (len(after) + na), out_specs=list(out_specs) + [_ANY] * nl,
        out_shape=list(out_shape) + ex.landing,
        scratch_shapes=list(scratch_shapes) + [pltpu.SemaphoreType.DMA((ex.n_sems,))] * 2,
        input_output_aliases={n_in + len(after) + i: n_out + j for i, j in ex.aliases.items()},
        compiler_params=ex.params(compiler_params),
    )(*args, *after, *_in_hbm(ex.arrays))
    return outs[:n_out], outs[n_out:]


def _exchange_alone(ex, name, after=()):
    na, nl = len(ex.arrays), len(ex.landing)

    def body(*refs):
        outs = refs[na + len(after):na + len(after) + nl]
        ex.enter()
        ex.start(refs[:na], outs, refs[-2], refs[-1])
        ex.finish(refs[:na], outs, refs[-2], refs[-1])

    return pl.pallas_call(
        body, name=name, in_specs=[_ANY] * (na + len(after)), out_specs=[_ANY] * nl,
        out_shape=ex.landing, scratch_shapes=[pltpu.SemaphoreType.DMA((ex.n_sems,))] * 2,
        input_output_aliases=ex.aliases, compiler_params=ex.params(),
    )(*_in_hbm(ex.arrays), *after)


_HBM = pl.BlockSpec(memory_space=pltpu.HBM)
_SEM = pl.BlockSpec(memory_space=pltpu.SEMAPHORE)
_EFFECT = pltpu.SideEffectType.DATAFLOW_SIDE_EFFECTING


def _scatter_copies(srcs, lands, ssems, rsems):
    x, y, c, chips = _place()
    return [_remote(srcs[w].at[2 * px + py], lands[w].at[k], ssems[3 * w + k], rsems[3 * w + k], (px, py, c))
            for w in range(len(srcs)) for k, (px, py) in enumerate(chips)]


def _scatter_start(parts, name):
    parts = list(parts)
    n, ncp = len(parts), 3 * len(parts)
    lands = [lax.empty((3,) + p.shape[1:], p.dtype) for p in parts]

    def body(*refs):
        srcs, land_refs = refs[:n], refs[n:2 * n]
        ssems, rsems = refs[2 * n:2 * n + ncp], refs[2 * n + ncp:2 * n + 2 * ncp]
        for cp in _scatter_copies(srcs, land_refs, ssems, rsems):
            cp.start()
        token = refs[-1]
        token[...] = jnp.zeros_like(token)

    outs = pl.pallas_call(
        body, name=name,
        out_shape=([pltpu.SemaphoreType.DMA(())] * (2 * ncp) + [pltpu.HBM(a.shape, a.dtype) for a in parts + lands]
                   + [jax.ShapeDtypeStruct((8, 128), F32)]),
        in_specs=[_HBM] * (2 * n), out_specs=[_SEM] * (2 * ncp) + [_HBM] * (2 * n) + [_VM],
        input_output_aliases={i: 2 * ncp + i for i in range(2 * n)},
        compiler_params=pltpu.CompilerParams(has_side_effects=_EFFECT),
    )(*_in_hbm(parts), *_in_hbm(lands))
    sems, thru, token = outs[:2 * ncp], outs[2 * ncp:2 * ncp + 2 * n], outs[-1]
    return sems, thru, token


def _scatter_wait(sems, thru, after, name):
    n = len(thru) // 2
    ncp = 3 * n

    def body(*refs):
        srcs, land_refs = refs[:n], refs[n:2 * n]
        ssems, rsems = refs[2 * n:2 * n + ncp], refs[2 * n + ncp:2 * n + 2 * ncp]
        for cp in _scatter_copies(srcs, land_refs, ssems, rsems):
            cp.wait_send()
            cp.wait_recv()

    outs = pl.pallas_call(
        body, name=name, out_shape=[pltpu.HBM(a.shape, a.dtype) for a in thru],
        in_specs=[_HBM] * (2 * n) + [_SEM] * (2 * ncp) + [_ANY] * len(after), out_specs=[_HBM] * (2 * n),
        input_output_aliases={i: i for i in range(2 * n)},
        compiler_params=pltpu.CompilerParams(has_side_effects=_EFFECT),
    )(*thru, *sems, *after)
    return outs[:n], outs[n:]


def _gather_copies(bufs, ssems, rsems, sending):
    x, y, c, chips = _place()
    out = []
    for w, ref in enumerate(bufs):
        half = ref.shape[1] // 2
        for k, (px, py) in enumerate(chips):
            rows = ref.at[2 * x + y if sending else 2 * px + py, pl.ds(c * half, half)]
            out.append(_remote(rows, rows, ssems[3 * w + k], rsems[3 * w + k], (px, py, c)))
    return out


def _gather_start(bufs, after, name):
    n, ncp = len(bufs), 3 * len(bufs)

    def body(*refs):
        ssems, rsems = refs[n + len(after):n + len(after) + ncp], refs[n + len(after) + ncp:n + len(after) + 2 * ncp]
        for cp in _gather_copies(refs[:n], ssems, rsems, True):
            cp.start()
        token = refs[-1]
        token[...] = jnp.zeros_like(token)

    outs = pl.pallas_call(
        body, name=name,
        out_shape=([pltpu.SemaphoreType.DMA(())] * (2 * ncp) + [pltpu.HBM(a.shape, a.dtype) for a in bufs]
                   + [jax.ShapeDtypeStruct((8, 128), F32)]),
        in_specs=[_HBM] * n + [_ANY] * len(after), out_specs=[_SEM] * (2 * ncp) + [_HBM] * n + [_VM],
        input_output_aliases={i: 2 * ncp + i for i in range(n)},
        compiler_params=pltpu.CompilerParams(has_side_effects=_EFFECT),
    )(*_in_hbm(bufs), *after)
    return outs[:2 * ncp], outs[2 * ncp:2 * ncp + n], outs[-1]


def _gather_wait(sems, thru, after, name):
    n = len(thru)
    ncp = 3 * n

    def body(*refs):
        ssems, rsems = refs[n:n + ncp], refs[n + ncp:n + 2 * ncp]
        for cp in _gather_copies(refs[:n], ssems, rsems, True):
            cp.wait_send()
        for cp in _gather_copies(refs[:n], ssems, rsems, False):
            cp.wait_recv()

    return pl.pallas_call(
        body, name=name, out_shape=[pltpu.HBM(a.shape, a.dtype) for a in thru],
        in_specs=[_HBM] * n + [_SEM] * (2 * ncp) + [_ANY] * len(after), out_specs=[_HBM] * n,
        input_output_aliases={i: i for i in range(n)},
        compiler_params=pltpu.CompilerParams(has_side_effects=_EFFECT),
    )(*thru, *sems, *after)


def _rms(x):
    r = lax.rsqrt(jnp.mean(x * x, axis=-1, keepdims=True) + EPS)
    return r, x * r


def _rms_bwd(dn, xr, r, gain):
    dng = dn * gain
    dx = r * (dng - xr * jnp.mean(dng * xr, axis=-1, keepdims=True))
    return dx, jnp.sum(dn * xr, axis=0, keepdims=True)


def _ffn_fwd(x, gain, wgu, wd, name, exchange=None, head=None):
    tm = 256

    def body(x_ref, g_ref, wgu_hbm, wd_hbm, *rest):
        if head is None:
            h_ref, n_ref, gu_ref, a_ref, wgu_ref, wd_ref = rest
        else:
            t_ref, gf_ref, h_ref, loss_ref, dgf_ref, n_ref, gu_ref, a_ref, wgu_ref, wd_ref = rest
        _stage([(wgu_hbm, wgu_ref), (wd_hbm, wd_ref)])
        x = x_ref[...]
        _, xr = _rms(x)
        n = (xr * g_ref[...]).astype(BF16)
        n_ref[...] = n
        acc = jnp.zeros((tm, D), F32)
        for j in range(2):
            g = _nn(n, wgu_ref[j])
            u = _nn(n, wgu_ref[2 + j])
            gu_ref[:, j * FFS:(j + 1) * FFS] = g.astype(BF16)
            gu_ref[:, (2 + j) * FFS:(3 + j) * FFS] = u.astype(BF16)
            half_act = (0.5 * (g * jax.nn.sigmoid(g) * u)).astype(BF16)
            a_ref[:, j * FFS:(j + 1) * FFS] = half_act
            acc = acc + _nn(half_act, wd_ref[j * FFS:(j + 1) * FFS, :])
        h = x + acc
        if head is None:
            h_ref[...] = h
            return
        gf = gf_ref[...]
        r, hr = _rms(h)
        err = hr * gf - t_ref[...]
        dh, dgain = _rms_bwd(err * (1.0 / D), hr, r, gf)
        h_ref[...] = dh

        @pl.when(pl.program_id(0) == 0)
        def _():
            dgf_ref[...] = jnp.zeros_like(dgf_ref)
            loss_ref[...] = jnp.zeros_like(loss_ref)

        dgf_ref[...] += dgain
        loss_ref[...] += jnp.full((1, 128), (0.5 / D) * jnp.sum(err * err), F32)

    saved_specs = [_rows(tm, D), _rows(tm, 4 * FFS), _rows(tm, DFF)]
    saved_shapes = [_sds((S, D), BF16), _sds((S, 4 * FFS), BF16), _sds((S, DFF), BF16)]
    if head is None:
        return _call(
            body, (x, gain, wgu, wd), name=name, grid=(S // tm,),
            in_specs=[_rows(tm, D), _fixed((1, D)), _ANY, _ANY],
            out_specs=[_rows(tm, D)] + saved_specs, out_shape=[_sds((S, D), F32)] + saved_shapes,
            scratch_shapes=_vmem_like(wgu, wd),
            compiler_params=_params(("arbitrary",), 56), exchange=exchange)
    return _call(
        body, (x, gain, wgu, wd, *head), name=name, grid=(S // tm,),
        in_specs=[_rows(tm, D), _fixed((1, D)), _ANY, _ANY, _rows(tm, D), _fixed((1, D))],
        out_specs=[_rows(tm, D), _fixed((1, 128)), _fixed((1, D))] + saved_specs,
        out_shape=[_sds((S, D), F32), _sds((1, 128), F32), _sds((1, D), F32)] + saved_shapes,
        scratch_shapes=_vmem_like(wgu, wd),
        compiler_params=_params(("arbitrary",), 56), exchange=exchange, free=(4, 5))


def _ffn_bwd(dh, x, gain, gu, wgu, wd, name):
    tm = 256

    def body(dh_ref, x_ref, g_ref, gu_ref, wgu_hbm, wd_hbm, dx_ref, dgu_ref, dg_ref, wgu_ref, wd_ref):
        _stage([(wgu_hbm, wgu_ref), (wd_hbm, wd_ref)])
        dh = dh_ref[...]
        dhb = dh.astype(BF16)
        dn = jnp.zeros((tm, D), F32)
        for j in range(2):
            g = gu_ref[:, j * FFS:(j + 1) * FFS].astype(F32)
            u = gu_ref[:, (2 + j) * FFS:(3 + j) * FFS].astype(F32)
            da = 0.5 * _nt(dhb, wd_ref[j * FFS:(j + 1) * FFS, :])
            sg = jax.nn.sigmoid(g)
            dgb = (da * u * (sg * (1.0 + g * (1.0 - sg)))).astype(BF16)
            dub = (da * (g * sg)).astype(BF16)
            dgu_ref[:, j * FFS:(j + 1) * FFS] = dgb
            dgu_ref[:, (2 + j) * FFS:(3 + j) * FFS] = dub
            dn = dn + _nt(dgb, wgu_ref[j]) + _nt(dub, wgu_ref[2 + j])
        r, xr = _rms(x_ref[...])
        dx, dgain = _rms_bwd(dn, xr, r, g_ref[...])
        dx_ref[...] = dh + dx

        @pl.when(pl.program_id(0) == 0)
        def _():
            dg_ref[...] = jnp.zeros_like(dg_ref)

        dg_ref[...] += dgain

    return _call(
        body, (dh, x, gain, gu, wgu, wd), name=name, grid=(S // tm,),
        in_specs=[_rows(tm, D), _rows(tm, D), _fixed((1, D)), _rows(tm, 4 * FFS), _ANY, _ANY],
        out_specs=[_rows(tm, D), _rows(tm, 4 * FFS), _fixed((1, D))],
        out_shape=[_sds((S, D), F32), _sds((S, 4 * FFS), BF16), _sds((1, D), F32)],
        scratch_shapes=_vmem_like(wgu, wd), compiler_params=_params(("arbitrary",), 56))


def _ffn_bwd_act(dh, gu, wd, name, exchange=None, after=()):
    tm = 512

    def body(dh_ref, gu_ref, wd_hbm, dgu_ref, wd_ref):
        _stage([(wd_hbm, wd_ref)])
        dhb = dh_ref[...].astype(BF16)
        for j in range(2):
            g = gu_ref[:, j * FFS:(j + 1) * FFS].astype(F32)
            u = gu_ref[:, (2 + j) * FFS:(3 + j) * FFS].astype(F32)
            da = 0.5 * _nt(dhb, wd_ref[j * FFS:(j + 1) * FFS, :])
            sg = jax.nn.sigmoid(g)
            dgu_ref[:, j * FFS:(j + 1) * FFS] = (da * u * (sg * (1.0 + g * (1.0 - sg)))).astype(BF16)
            dgu_ref[:, (2 + j) * FFS:(3 + j) * FFS] = (da * (g * sg)).astype(BF16)

    res = _call(
        body, (dh, gu, wd), name=name, grid=(S // tm,),
        in_specs=[_rows(tm, D), _rows(tm, 4 * FFS), _ANY], out_specs=[_rows(tm, 4 * FFS)],
        out_shape=[_sds((S, 4 * FFS), BF16)], scratch_shapes=_vmem_like(wd),
        compiler_params=_params(("arbitrary",), 56), exchange=exchange, after=after)
    return res[0] if exchange is None else (res[0][0], res[1])


def _ffn_bwd_in(dh, x, gain, dgu, wgu, name, exchange=None, after=()):
    tm = 512

    def body(dh_ref, x_ref, g_ref, dgu_ref, wgu_hbm, dx_ref, dg_ref, wgu_ref):
        _stage([(wgu_hbm, wgu_ref)])
        dn = jnp.zeros((tm, D), F32)
        for j in range(NSH):
            dn = dn + _nt(dgu_ref[:, j * FFS:(j + 1) * FFS], wgu_ref[j])
        r, xr = _rms(x_ref[...])
        dx, dgain = _rms_bwd(dn, xr, r, g_ref[...])
        dx_ref[...] = dh_ref[...] + dx

        @pl.when(pl.program_id(0) == 0)
        def _():
            dg_ref[...] = jnp.zeros_like(dg_ref)

        dg_ref[...] += dgain

    return _call(
        body, (dh, x, gain, dgu, wgu), name=name, grid=(S // tm,),
        in_specs=[_rows(tm, D), _rows(tm, D), _fixed((1, D)), _rows(tm, 4 * FFS), _ANY],
        out_specs=[_rows(tm, D), _fixed((1, D))],
        out_shape=[_sds((S, D), F32), _sds((1, D), F32)],
        scratch_shapes=_vmem_like(wgu),
        compiler_params=_params(("arbitrary",), 56), exchange=exchange, after=after)


def _mix_in(h, gain, w_in, after=()):
    tm = 512

    def body(h_ref, g_ref, w_hbm, u_ref, xp_ref, q_ref, k_ref, v_ref, gp_ref, gs_ref, w_ref):
        _stage([(w_hbm, w_ref)])
        _, hr = _rms(h_ref[...])
        u = (hr * g_ref[...]).astype(BF16)
        u_ref[...] = u
        p0 = _nn(u, w_ref[0])
        xp_ref[...] = p0[:, :PW]
        q_ref[...] = p0[:, PW:].astype(BF16)
        p1 = _nn(u, w_ref[1])
        k_ref[...] = p1[:, :SBW].astype(BF16)
        v_ref[...] = p1[:, SBW:].astype(BF16)
        gp_ref[...] = jax.nn.sigmoid(_nn(u, w_ref[2])).astype(BF16)
        gs_ref[...] = jax.nn.sigmoid(_nn(u, w_ref[3])).astype(BF16)

    return _call(
        body, (h, gain, w_in), name="mix_in", grid=(S // tm,),
        in_specs=[_rows(tm, D), _fixed((1, D)), _ANY],
        out_specs=[_rows(tm, D), _rows(tm, PW), _rows(tm, SBW), _rows(tm, SBW), _rows(tm, SBW),
                   _rows(tm, D), _rows(tm, D)],
        out_shape=[_sds((S, D), BF16), _sds((S, PW), F32), _sds((S, SBW), BF16), _sds((S, SBW), BF16),
                   _sds((S, SBW), BF16), _sds((S, D), BF16), _sds((S, D), BF16)],
        scratch_shapes=_vmem_like(w_in),
        compiler_params=_params(("arbitrary",), 48), free=(1,), after=after)


def _hilo_dot(x, tri):
    hi = x.astype(BF16)
    lo = (x - hi.astype(F32)).astype(BF16)
    return _nn(hi, tri) + _nn(lo, tri)


def _log_terms(qk):
    z2 = qk * (SCALE * LOG2E)
    lb = jnp.minimum(z2, 0.0) - jnp.log2(1.0 + jnp.exp2(-jnp.abs(z2)))
    return lb, lb - z2


def _head_masks():
    lane = lax.broadcasted_iota(jnp.int32, (1, 2 * DH), 1)
    return (lane < DH, lane >= DH)


def _attn_fwd(q, k, v, exchange=None):
    T = TA

    def body(q_ref, k_ref, v_ref, o_ref, c_ref):
        i2 = 2 * pl.program_id(1)
        row = lax.broadcasted_iota(jnp.int32, (T, T), 0)
        col = lax.broadcasted_iota(jnp.int32, (T, T), 1)
        after = (row > col).astype(BF16)
        causal = col < row
        masks = _head_masks()
        qms = {}
        for b in range(QB):
            q2 = q_ref[b * T:(b + 1) * T, :]
            for h, hm in enumerate(masks):
                qms[b, h] = jnp.where(hm, q2, jnp.zeros_like(q2))

        def blocks(keys, pairs, carries, os):
            ks, vms = [], []
            for j in keys:
                rows = pl.ds(pl.multiple_of(j * T, T), T)
                vj = v_ref[rows, :]
                ks.append(k_ref[rows, :])
                vms.append([jnp.where(hm, vj, jnp.zeros_like(vj)) for hm in masks])
            units = [(n, h) for n in range(len(pairs)) for h in range(2)]
            qks = {(n, h): _nt(qms[pairs[n][0], h], ks[pairs[n][1]]) for n, h in units}
            lbs, l1ms = {}, {}
            for u in units:
                lbs[u], l1m = _log_terms(qks[u])
                l1ms[u] = jnp.where(causal, l1m, 0.0) if pairs[u[0]][2] else l1m
            cins = {u: _hilo_dot(l1ms[u], after) for u in units}
            carries, os = dict(carries), list(os)
            for n, h in units:
                b, key, diag = pairs[n]
                a = jnp.exp2(lbs[n, h] + cins[n, h] + carries[b, h])
                if diag:
                    a = jnp.where(causal, a, 0.0)
                os[b] = os[b] + _nn(a.astype(BF16), vms[key][h])
                carries[b, h] = carries[b, h] + jnp.sum(l1ms[n, h], axis=1, keepdims=True)
            return carries, tuple(os)

        carries = {(b, h): jnp.zeros((T, 1), F32) for b in range(QB) for h in range(2)}
        os = tuple(jnp.zeros((T, 2 * DH), F32) for _ in range(QB))
        carries, os = blocks([i2 + 1, i2], [(1, 0, True), (0, 1, True), (1, 1, False)], carries, os)
        carries, os = lax.fori_loop(
            0, i2 // 2,
            lambda t, c: blocks([i2 - 1 - 2 * t, i2 - 2 - 2 * t],
                                [(0, 0, False), (1, 0, False), (0, 1, False), (1, 1, False)], c[0], c[1]),
            (carries, os))
        for b in range(QB):
            o_ref[b * T:(b + 1) * T, :] = os[b].astype(BF16)
            c_ref[b * T:(b + 1) * T, :] = jnp.where(masks[0], carries[b, 0], carries[b, 1])

    blk = pl.BlockSpec((QB * T, 2 * DH), lambda p, i: (i, p))
    full = pl.BlockSpec((S, 2 * DH), lambda p, i: (0, p))
    return _call(
        body, (q, k, v), name="attn_fwd", grid=(SBW // (2 * DH), S // (QB * T)),
        in_specs=[blk, full, full], out_specs=[blk, blk],
        out_shape=[_sds((S, SBW), BF16), _sds((S, SBW), F32)],
        compiler_params=_params(("arbitrary", "arbitrary"), 40), exchange=exchange)


def _attn_bwd(q, k, v, do, ctot, after=()):
    T = TA
    nq = S // (QB * T)

    def body(q_ref, k_ref, v_ref, do_ref, c_ref, dq_ref, dk_ref, dv_ref, dk_acc, dv_acc):
        step = pl.program_id(1)
        i2 = 2 * step

        @pl.when(step == 0)
        def _():
            dk_acc[...] = jnp.zeros_like(dk_acc)
            dv_acc[...] = jnp.zeros_like(dv_acc)

        row = lax.broadcasted_iota(jnp.int32, (T, T), 0)
        col = lax.broadcasted_iota(jnp.int32, (T, T), 1)
        upto = (row <= col).astype(BF16)
        before = (row < col).astype(BF16)
        causal = col < row
        masks = _head_masks()
        qms, doms, ctots = {}, {}, {}
        for b in range(QB):
            q2, do2 = q_ref[b * T:(b + 1) * T, :], do_ref[b * T:(b + 1) * T, :]
            for h, hm in enumerate(masks):
                qms[b, h] = jnp.where(hm, q2, jnp.zeros_like(q2))
                doms[b, h] = jnp.where(hm, do2, jnp.zeros_like(do2))
                ctots[b, h] = c_ref[b * T:(b + 1) * T, h * DH:h * DH + 1]

        def blocks(keys, pairs, sums, dqs):
            rows = [pl.ds(pl.multiple_of(j * T, T), T) for j in keys]
            ks, vs = [k_ref[r, :] for r in rows], [v_ref[r, :] for r in rows]
            kms = [[jnp.where(hm, kj, jnp.zeros_like(kj)) for hm in masks] for kj in ks]
            units = [(n, h) for n in range(len(pairs)) for h in range(2)]
            qks = {(n, h): _nt(qms[pairs[n][0], h], ks[pairs[n][1]]) for n, h in units}
            das = {(n, h): _nt(doms[pairs[n][0], h], vs[pairs[n][1]]) for n, h in units}
            lbs, l1ms = {}, {}
            for u in units:
                lbs[u], l1m = _log_terms(qks[u])
                l1ms[u] = jnp.where(causal, l1m, 0.0) if pairs[u[0]][2] else l1m
            pins = {u: _hilo_dot(l1ms[u], upto) for u in units}
            sums = dict(sums)
            a_s, dls, cps = {}, {}, {}
            for n, h in units:
                b, _, diag = pairs[n]
                cl, cp = sums[b, h]
                a = jnp.exp2(lbs[n, h] + (ctots[b, h] - cl) - pins[n, h])
                if diag:
                    a = jnp.where(causal, a, 0.0)
                a_s[n, h] = a.astype(BF16)
                dls[n, h] = das[n, h] * a
                cps[n, h] = cp
                sums[b, h] = (cl + jnp.sum(l1ms[n, h], axis=1, keepdims=True),
                              cp + jnp.sum(dls[n, h], axis=1, keepdims=True))
            pexs = {u: _hilo_dot(dls[u], before) for u in units}
            dzbs = {}
            for u in units:
                dz = dls[u] - jnp.exp2(lbs[u]) * (dls[u] + pexs[u] + cps[u])
                if pairs[u[0]][2]:
                    dz = jnp.where(causal, dz, 0.0)
                dzbs[u] = dz.astype(BF16)
            dqs = list(dqs)
            for n, h in units:
                dqs[pairs[n][0]] = dqs[pairs[n][0]] + _nn(dzbs[n, h], kms[pairs[n][1]][h])
            for key, r in enumerate(rows):
                mine = [(n, h) for n, h in units if pairs[n][1] == key]
                dk_acc[r, :] += functools.reduce(jnp.add, [_tn(dzbs[u], qms[pairs[u[0]][0], u[1]]) for u in mine])
                dv_acc[r, :] += functools.reduce(jnp.add, [_tn(a_s[u], doms[pairs[u[0]][0], u[1]]) for u in mine])
            return sums, tuple(dqs)

        zero = jnp.zeros((T, 1), F32)
        sums = {(b, h): (zero, zero) for b in range(QB) for h in range(2)}
        dqs = tuple(jnp.zeros((T, 2 * DH), F32) for _ in range(QB))
        sums, dqs = lax.fori_loop(
            0, i2 // 2,
            lambda t, c: blocks([2 * t, 2 * t + 1],
                                [(0, 0, False), (1, 0, False), (0, 1, False), (1, 1, False)], c[0], c[1]),
            (sums, dqs))
        _, dqs = blocks([i2, i2 + 1], [(0, 0, True), (1, 0, False), (1, 1, True)], sums, dqs)
        for b in range(QB):
            dq_ref[b * T:(b + 1) * T, :] = (dqs[b] * SCALE).astype(BF16)

        @pl.when(step == nq - 1)
        def _():
            dk_ref[...] = (dk_acc[...] * SCALE).astype(BF16)
            dv_ref[...] = dv_acc[...].astype(BF16)

    blk = pl.BlockSpec((QB * T, 2 * DH), lambda p, i: (i, p))
    full = pl.BlockSpec((S, 2 * DH), lambda p, i: (0, p))
    return _call(
        body, (q, k, v, do, ctot), name="attn_bwd", grid=(SBW // (2 * DH), nq),
        in_specs=[blk, full, full, blk, blk], out_specs=[blk, full, full],
        out_shape=[_sds((S, SBW), BF16), _sds((S, SBW), BF16), _sds((S, SBW), BF16)],
        scratch_shapes=[pltpu.VMEM((S, 2 * DH), F32), pltpu.VMEM((S, 2 * DH), F32)],
        compiler_params=_params(("arbitrary", "arbitrary"), 40), after=after)


def _pool_counts(first_row, tm):
    pos = first_row + lax.broadcasted_iota(jnp.int32, (tm, 1), 0)
    return [jnp.minimum(pos + 1, w).astype(F32) for w in POOL_WINDOWS]


def _mix_out(h, xp, o_sb, gp, gs, w_group, scale, w_bp, w_ba, w_out, exchange=None):
    tm = 512

    def body(h_ref, xp_ref, o_ref, gp_ref, gs_ref, wg_hbm, sc_ref, wbp_hbm, wba_hbm, wo_hbm,
             h2_ref, pm_ref, p_ref, yp_ref, ys_ref, m_ref, halo, wg_ref, wbp_ref, wba_ref, wo_ref):
        _stage([(wg_hbm, wg_ref), (wbp_hbm, wbp_ref), (wba_hbm, wba_ref), (wo_hbm, wo_ref)])
        i = pl.program_id(0)

        @pl.when(i == 0)
        def _():
            halo[...] = jnp.zeros_like(halo)

        xp = xp_ref[...]
        ext = jnp.concatenate([halo[...], xp], axis=0)
        halo[...] = xp[tm - HALO:, :]
        counts = _pool_counts(i * tm, tm)
        for gi in range(len(POOL_WINDOWS)):
            lanes = slice(gi * PG, (gi + 1) * PG)
            win = ext[:, lanes]
            for step in range(gi + 1):
                win = win + pltpu.roll(win, 1 << step, 0)
            pm = (win[HALO:, :] / counts[gi] - xp[:, lanes]).astype(BF16)
            pm_ref[:, lanes] = pm
            p_ref[:, lanes] = (_nn(pm, wg_ref[gi]) * sc_ref[:, lanes]).astype(BF16)
        pb = p_ref[...]
        ob = o_ref[...]
        for j in range(NSH):
            cols = slice(j * (D // NSH), (j + 1) * (D // NSH))
            yp = _nn(pb, wbp_ref[j])
            ys = _nn(ob, wba_ref[j])
            yp_ref[:, cols] = yp.astype(BF16)
            ys_ref[:, cols] = ys.astype(BF16)
            m_ref[:, cols] = (gp_ref[:, cols].astype(F32) * yp + gs_ref[:, cols].astype(F32) * ys).astype(BF16)
        h2_ref[...] = h_ref[...] + _nn(m_ref[...], wo_ref[...])

    return _call(
        body, (h, xp, o_sb, gp, gs, w_group, scale, w_bp, w_ba, w_out), name="mix_out", grid=(S // tm,),
        in_specs=[_rows(tm, D), _rows(tm, PW), _rows(tm, SBW), _rows(tm, D), _rows(tm, D),
                  _ANY, _fixed((1, PW)), _ANY, _ANY, _ANY],
        out_specs=[_rows(tm, D), _rows(tm, PW), _rows(tm, PW), _rows(tm, D), _rows(tm, D), _rows(tm, D)],
        out_shape=[_sds((S, D), F32), _sds((S, PW), BF16), _sds((S, PW), BF16), _sds((S, D), BF16),
                   _sds((S, D), BF16), _sds((S, D), BF16)],
        scratch_shapes=[pltpu.VMEM((HALO, PW), F32)] + _vmem_like(w_group, w_bp, w_ba, w_out),
        compiler_params=_params(("arbitrary",), 48), free=(5, 6), exchange=exchange)


def _mix_bwd_out(dh, gp, gs, yp, ys, pm, w_group, scale, w_bp, w_ba, w_out, exchange=None):
    tm = 512
    nt = S // tm

    def body(dh_ref, gp_ref, gs_ref, yp_ref, ys_ref, pm_ref, wg_hbm, sc_ref, wbp_hbm, wba_hbm, wo_hbm,
             dlg_ref, dyp_ref, dys_ref, do_ref, dyg_ref, dxp_ref, dsc_ref, halo, wg_ref, wbp_ref, wba_ref, wo_ref):
        _stage([(wg_hbm, wg_ref), (wbp_hbm, wbp_ref), (wba_hbm, wba_ref), (wo_hbm, wo_ref)])
        step = pl.program_id(0)

        @pl.when(step == 0)
        def _():
            halo[...] = jnp.zeros_like(halo)
            dsc_ref[...] = jnp.zeros_like(dsc_ref)

        dm = _nt(dh_ref[...].astype(BF16), wo_ref[...])
        gp = gp_ref[...].astype(F32)
        gs = gs_ref[...].astype(F32)
        yp = yp_ref[...].astype(F32)
        ys = ys_ref[...].astype(F32)
        dlg_ref[:, :D] = (dm * yp * gp * (1.0 - gp)).astype(BF16)
        dlg_ref[:, D:] = (dm * ys * gs * (1.0 - gs)).astype(BF16)
        dyp_ref[...] = (dm * gp).astype(BF16)
        dys_ref[...] = (dm * gs).astype(BF16)
        dp = jnp.zeros((tm, PW), F32)
        do = jnp.zeros((tm, SBW), F32)
        for j in range(NSH):
            cols = slice(j * (D // NSH), (j + 1) * (D // NSH))
            dp = dp + _nt(dyp_ref[:, cols], wbp_ref[j])
            do = do + _nt(dys_ref[:, cols], wba_ref[j])
        do_ref[...] = do.astype(BF16)
        counts = _pool_counts((nt - 1 - step) * tm, tm)
        dscale = []
        for gi in range(len(POOL_WINDOWS)):
            lanes = slice(gi * PG, (gi + 1) * PG)
            dpg = dp[:, lanes]
            dscale.append(jnp.sum(dpg * _nn(pm_ref[:, lanes], wg_ref[gi]), axis=0, keepdims=True))
            dyg = (dpg * sc_ref[:, lanes]).astype(BF16)
            dyg_ref[:, lanes] = dyg
            dpm = _nt(dyg, wg_ref[gi])
            per = dpm / counts[gi]
            win = jnp.concatenate([per, halo[:, lanes]], axis=0)
            halo[:, lanes] = per[:HALO, :]
            for s in range(gi + 1):
                win = win + pltpu.roll(win, tm + HALO - (1 << s), 0)
            dxp_ref[:, lanes] = (win[:tm, :] - dpm).astype(BF16)
        dsc_ref[...] += jnp.concatenate(dscale, axis=1)

    rev = lambda width: pl.BlockSpec((tm, width), lambda i: (nt - 1 - i, 0))
    return _call(
        body, (dh, gp, gs, yp, ys, pm, w_group, scale, w_bp, w_ba, w_out), name="mix_bwd_out", grid=(nt,),
        in_specs=[rev(D), rev(D), rev(D), rev(D), rev(D), rev(PW), _ANY, _fixed((1, PW)), _ANY, _ANY, _ANY],
        out_specs=[rev(2 * D), rev(D), rev(D), rev(SBW), rev(PW), rev(PW), _fixed((1, PW))],
        out_shape=[_sds((S, 2 * D), BF16), _sds((S, D), BF16), _sds((S, D), BF16), _sds((S, SBW), BF16),
                   _sds((S, PW), BF16), _sds((S, PW), BF16), _sds((1, PW), F32)],
        scratch_shapes=[pltpu.VMEM((HALO, PW), F32)] + _vmem_like(w_group, w_bp, w_ba, w_out),
        compiler_params=_params(("arbitrary",), 48), exchange=exchange)


def _mix_bwd_in(dh, h, gain, pieces, w_in, exchange=None):
    tm = 512
    widths = [p.shape[1] for p in pieces]

    def body(dh_ref, h_ref, g_ref, *rest):
        piece_refs, (w_hbm, dx_ref, dg_ref, dp_ref, w_ref) = rest[:len(pieces)], rest[len(pieces):]
        _stage([(w_hbm, w_ref)])
        at = 0
        for ref, width in zip(piece_refs, widths):
            dp_ref[:, at:at + width] = ref[...]
            at += width
        du = jnp.zeros((tm, D), F32)
        for j in range(NSH):
            du = du + _nt(dp_ref[:, j * D:(j + 1) * D], w_ref[j])
        r, hr = _rms(h_ref[...])
        dx, dgain = _rms_bwd(du, hr, r, g_ref[...])
        dx_ref[...] = dh_ref[...] + dx

        @pl.when(pl.program_id(0) == 0)
        def _():
            dg_ref[...] = jnp.zeros_like(dg_ref)

        dg_ref[...] += dgain

    return _call(
        body, (dh, h, gain, *pieces, w_in), name="mix_bwd_in", grid=(S // tm,),
        in_specs=[_rows(tm, D), _rows(tm, D), _fixed((1, D))] + [_rows(tm, w) for w in widths] + [_ANY],
        out_specs=[_rows(tm, D), _fixed((1, D)), _rows(tm, 4 * D)],
        out_shape=[_sds((S, D), F32), _sds((1, D), F32), _sds((S, 4 * D), BF16)],
        scratch_shapes=_vmem_like(w_in),
        compiler_params=_params(("arbitrary",), 48), exchange=exchange)


def _wgrad(a, b, nblk, ti, name, out_dtype=BF16, exchange=None, after=()):
    ka, n = a.shape[1], b.shape[1]
    ns = n // nblk

    def body(a_ref, b_ref, o_ref):
        o_ref[...] = _tn(a_ref[...].astype(BF16), b_ref[...].astype(BF16)).astype(out_dtype)

    res = _call(
        body, (a, b), name=name, grid=(nblk, ka // ti),
        in_specs=[pl.BlockSpec((S, ti), lambda j, i: (0, i)), pl.BlockSpec((S, ns), lambda j, i: (0, j))],
        out_specs=[pl.BlockSpec((None, ti, ns), lambda j, i: (j, i, 0))],
        out_shape=[_sds((nblk, ka, ns), out_dtype)],
        compiler_params=_params(("arbitrary", "arbitrary"), 56), exchange=exchange, after=after)
    return res[0] if exchange is None else (res[0][0], res[1])


def _wgrad_groups(pm, dyg):
    def body(a_ref, b_ref, o_ref):
        o_ref[...] = _tn(a_ref[...], b_ref[...])

    col = pl.BlockSpec((S, PG), lambda g: (0, g))
    return pl.pallas_call(
        body, name="wgrad_groups", grid=(PW // PG,),
        in_specs=[col, col], out_specs=pl.BlockSpec((None, PG, PG), lambda g: (g, 0, 0)),
        out_shape=_sds((PW // PG, PG, PG), F32),
        compiler_params=_params(("arbitrary",), 32),
    )(*_in_hbm([pm, dyg]))


def _place():
    x, y, c = lax.axis_index("x"), lax.axis_index("y"), lax.axis_index("c")
    chips = [(1 - x, y), (x, 1 - y), (1 - x, 1 - y)]
    return x, y, c, chips


def _remote(src, dst, ssem, rsem, dev):
    return pltpu.make_async_remote_copy(src_ref=src, dst_ref=dst, send_sem=ssem, recv_sem=rsem,
                                        device_id=dev, device_id_type=MESH)


def _cast_into_block(ws, me_idx, name):
    steps = 4
    shapes = [(w.shape[0] // steps, w.shape[1]) for w in ws]

    def body(me_ref, *refs):
        for w_ref, o_ref in zip(refs[:len(ws)], refs[len(ws):]):
            o_ref[...] = w_ref[...].astype(BF16)

    return pl.pallas_call(
        body, name=name, out_shape=[_sds((NSH,) + w.shape, BF16) for w in ws],
        grid_spec=pltpu.PrefetchScalarGridSpec(
            num_scalar_prefetch=1, grid=(steps,),
            in_specs=[pl.BlockSpec((r, c), lambda s, me: (s, 0)) for r, c in shapes],
            out_specs=[pl.BlockSpec((None, r, c), lambda s, me: (me[0], s, 0)) for r, c in shapes]),
        compiler_params=_params(("arbitrary",), 32),
    )(me_idx, *ws)


def _ex_gather(bufs):
    n = len(bufs)
    per = 8

    def plan(outs, ssem, rsem, w):
        x, y, c, _ = _place()
        sib, nbr_x, nbr_y = (x, y, 1 - c), (1 - x, y, c), (x, 1 - y, c)
        half = outs[w].shape[1] // 2
        quarter = half // 2
        sem = lambda k: (ssem.at[per * w + k], rsem.at[per * w + k])
        rows = lambda blk, start, size: outs[w].at[blk, pl.ds(start, size)]
        mine = rows(2 * x + y, c * half, half)
        from_x = rows(2 * (1 - x) + y, c * half, half)
        from_y = rows(2 * x + (1 - y), c * half, half)
        diag = 2 * (1 - x) + (1 - y)
        pass_y = rows(2 * (1 - x) + y, c * half, quarter)
        pass_x = rows(2 * x + (1 - y), c * half + quarter, quarter)
        diag_0, diag_1 = rows(diag, c * half, quarter), rows(diag, c * half + quarter, quarter)
        first = [_remote(mine, mine, *sem(0), nbr_x), _remote(mine, mine, *sem(1), nbr_y)]
        arrivals = [
            (_remote(from_x, from_x, *sem(0), nbr_x),
             [_remote(pass_y, pass_y, *sem(2), nbr_y), _remote(from_x, from_x, *sem(4), sib)]),
            (_remote(from_y, from_y, *sem(1), nbr_y),
             [_remote(pass_x, pass_x, *sem(3), nbr_x), _remote(from_y, from_y, *sem(5), sib)]),
            (_remote(diag_0, diag_0, *sem(2), nbr_y), [_remote(diag_0, diag_0, *sem(6), sib)]),
            (_remote(diag_1, diag_1, *sem(3), nbr_x), [_remote(diag_1, diag_1, *sem(7), sib)]),
        ]
        other = (1 - c) * half
        from_sibling = [
            _remote(rows(2 * (1 - x) + y, other, half), rows(2 * (1 - x) + y, other, half), *sem(4), sib),
            _remote(rows(2 * x + (1 - y), other, half), rows(2 * x + (1 - y), other, half), *sem(5), sib),
            _remote(rows(diag, other, quarter), rows(diag, other, quarter), *sem(6), sib),
            _remote(rows(diag, other + quarter, quarter), rows(diag, other + quarter, quarter), *sem(7), sib),
        ]
        return first, arrivals, from_sibling

    def start(ins, outs, ssem, rsem):
        x, y, c, _ = _place()
        for w in range(n):
            half = outs[w].shape[1] // 2
            mine = outs[w].at[2 * x + y, pl.ds(c * half, half)]
            _remote(mine, mine, ssem.at[per * w], rsem.at[per * w], (1 - x, y, c)).start()
            _remote(mine, mine, ssem.at[per * w + 1], rsem.at[per * w + 1], (x, 1 - y, c)).start()

    def finish(ins, outs, ssem, rsem):
        plans = [plan(outs, ssem, rsem, w) for w in range(n)]
        started = []
        for direct in (True, False):
            for first, arrivals, _ in plans:
                for arrived, onward in (arrivals[:2] if direct else arrivals[2:]):
                    arrived.wait_recv()
                    for cp in onward:
                        cp.start()
                    started += onward
        for first, _, from_sibling in plans:
            for cp in from_sibling:
                cp.wait_recv()
            started += first
        for cp in started:
            cp.wait_send()

    return Exchange(bufs, [_sds(b.shape, b.dtype) for b in bufs], {w: w for w in range(n)}, per * n, start, finish)


def _ex_gather_direct(bufs):
    n = len(bufs)

    def copies(outs, ssem, rsem, only_first=False):
        x, y, c, chips = _place()
        me, sib = 2 * x + y, (x, y, 1 - c)
        first, relay, last = [], [], []
        for w in range(n):
            half = outs[w].shape[1] // 2
            mine = outs[w].at[me, pl.ds(c * half, half)]
            for k, (px, py) in enumerate(chips):
                sems = (ssem.at[6 * w + k], rsem.at[6 * w + k])
                sib_sems = (ssem.at[6 * w + 3 + k], rsem.at[6 * w + 3 + k])
                first.append(_remote(mine, mine, *sems, (px, py, c)))
                if only_first:
                    continue
                got = outs[w].at[2 * px + py, pl.ds(c * half, half)]
                relay.append((_remote(got, got, *sems, (px, py, c)), _remote(got, got, *sib_sems, sib)))
                theirs = outs[w].at[2 * px + py, pl.ds((1 - c) * half, half)]
                last.append(_remote(theirs, theirs, *sib_sems, sib))
        return first, relay, last

    def start(ins, outs, ssem, rsem):
        for cp in copies(outs, ssem, rsem, only_first=True)[0]:
            cp.start()

    def finish(ins, outs, ssem, rsem):
        first, relay, last = copies(outs, ssem, rsem)
        for arrived, onward in relay:
            arrived.wait_recv()
            onward.start()
        for cp in last:
            cp.wait_recv()
        for cp in first:
            cp.wait_send()
        for _, onward in relay:
            onward.wait_send()

    return Exchange(bufs, [_sds(b.shape, b.dtype) for b in bufs], {w: w for w in range(n)}, 6 * n, start, finish)


def _simple_exchange(arrays, landing, aliases, make_copies, sibling_only=False):
    def start(ins, outs, ssem, rsem):
        for cp, _ in make_copies(ins, outs, ssem, rsem, False):
            cp.start()

    def finish(ins, outs, ssem, rsem):
        cps = make_copies(ins, outs, ssem, rsem, True)
        for _, landed in cps:
            landed.wait_recv()
        for cp, _ in cps:
            cp.wait_send()

    return Exchange(arrays, landing, aliases, len(arrays) * 3, start, finish, sibling_only)


def _ex_pair_swap(grads):
    def make(ins, outs, ssem, rsem, landing):
        x, y, c, _ = _place()
        cps = [_remote(ins[w].at[:, 1 - c], outs[w], ssem.at[w], rsem.at[w], (x, y, 1 - c))
               for w in range(len(grads))]
        return [(cp, cp) for cp in cps]

    return _simple_exchange(grads, [_sds((NSH,) + g.shape[2:], g.dtype) for g in grads], {}, make, True)


def _ex_scatter(parts):
    def make(ins, outs, ssem, rsem, landing):
        x, y, c, chips = _place()
        out = []
        for w in range(len(parts)):
            for k, (px, py) in enumerate(chips):
                sems = (ssem.at[3 * w + k], rsem.at[3 * w + k])
                out.append((_remote(ins[w].at[2 * px + py], outs[w].at[k], *sems, (px, py, c)),
                            _remote(outs[w].at[k], outs[w].at[k], *sems, (px, py, c)) if landing else None))
        return out

    return _simple_exchange(parts, [_sds((3,) + p.shape[1:], p.dtype) for p in parts], {}, make)


def _ex_relay(bufs):
    def make(ins, outs, ssem, rsem, landing):
        x, y, c, chips = _place()
        sib = (x, y, 1 - c)
        out = []
        for w in range(len(bufs)):
            half = outs[w].shape[1] // 2
            for k, (px, py) in enumerate(chips):
                sems = (ssem.at[3 * w + k], rsem.at[3 * w + k])
                have = outs[w].at[2 * px + py, pl.ds(c * half, half)]
                miss = outs[w].at[2 * px + py, pl.ds((1 - c) * half, half)]
                out.append((_remote(have, have, *sems, sib), _remote(miss, miss, *sems, sib) if landing else None))
        return out

    return _simple_exchange(bufs, [_sds(b.shape, b.dtype) for b in bufs], {w: w for w in range(len(bufs))}, make, True)


def _ex_share(bufs):
    def make(ins, outs, ssem, rsem, landing):
        x, y, c, _ = _place()
        sib = (x, y, 1 - c)
        return [(_remote(outs[w].at[c], outs[w].at[c], ssem.at[w], rsem.at[w], sib),
                 _remote(outs[w].at[1 - c], outs[w].at[1 - c], ssem.at[w], rsem.at[w], sib) if landing else None)
                for w in range(len(bufs))]

    return _simple_exchange(bufs, [_sds(b.shape, b.dtype) for b in bufs], {w: w for w in range(len(bufs))}, make, True)


def _small_copies(slots, ssems, rsems, sending):
    x, y, c, _ = _place()
    out = []
    for m in range(1, 8):
        px, py, pc = x ^ (m >> 2), y ^ ((m >> 1) & 1), c ^ (m & 1)
        slot = slots.at[4 * x + 2 * y + c if sending else 4 * px + 2 * py + pc]
        out.append(_remote(slot, slot, ssems[m - 1], rsems[m - 1], (px, py, pc)))
    return out


def _small_gather_start(slots, name):
    def body(*refs):
        for cp in _small_copies(refs[0], refs[1:8], refs[8:15], True):
            cp.start()
        refs[-1][...] = jnp.zeros_like(refs[-1])

    outs = pl.pallas_call(
        body, name=name,
        out_shape=([pltpu.SemaphoreType.DMA(())] * 14 + [pltpu.HBM(slots.shape, slots.dtype)]
                   + [jax.ShapeDtypeStruct((8, 128), F32)]),
        in_specs=[_HBM], out_specs=[_SEM] * 14 + [_HBM, _VM], input_output_aliases={0: 14},
        compiler_params=pltpu.CompilerParams(has_side_effects=_EFFECT),
    )(*_in_hbm([slots]))
    return outs[:14], outs[14], outs[15]


def _small_gather_wait(sems, slots, after, name):
    def body(*refs):
        for cp in _small_copies(refs[0], refs[1:8], refs[8:15], True):
            cp.wait_send()
        for cp in _small_copies(refs[0], refs[1:8], refs[8:15], False):
            cp.wait_recv()

    return pl.pallas_call(
        body, name=name, out_shape=pltpu.HBM(slots.shape, slots.dtype),
        in_specs=[_HBM] + [_SEM] * 14 + [_ANY] * len(after), out_specs=_HBM, input_output_aliases={0: 0},
        compiler_params=pltpu.CompilerParams(has_side_effects=_EFFECT),
    )(slots, *sems, *after)


def _row_block(rows, cap=256):
    return max(t for t in range(16, cap + 1, 16) if rows % t == 0)


def _pair_sum(grads, gots, c_idx, name):
    n = len(grads)

    def body(c_ref, *refs):
        for a_ref, b_ref, o_ref in zip(refs[:n], refs[n:2 * n], refs[2 * n:]):
            o_ref[...] = (a_ref[...].astype(F32) + b_ref[...].astype(F32)).astype(BF16)

    halves = [g.shape[2:] for g in grads]
    return list(pl.pallas_call(
        body, name=name, out_shape=[_sds((NSH,) + h, BF16) for h in halves],
        grid_spec=pltpu.PrefetchScalarGridSpec(
            num_scalar_prefetch=1, grid=(NSH,),
            in_specs=[pl.BlockSpec((None, None) + h, lambda j, c: (j, c[0], 0, 0)) for h in halves]
            + [pl.BlockSpec((None,) + h, lambda j, c: (j, 0, 0)) for h in halves],
            out_specs=[pl.BlockSpec((None,) + h, lambda j, c: (j, 0, 0)) for h in halves]),
        compiler_params=_params(("arbitrary",), 40),
    )(c_idx, *_in_hbm(list(grads) + list(gots))))


def _chip_sum(owns, gots, place, name):
    n = len(owns)

    def body(place_ref, *refs):
        for own_ref, got_ref, o_ref in zip(refs[:n], refs[n:2 * n], refs[2 * n:]):
            acc = own_ref[...].astype(F32)
            for k in range(3):
                acc = acc + got_ref[k].astype(F32)
            o_ref[...] = acc

    shapes = [(o.shape[1] // 2, o.shape[2]) for o in owns]
    return list(pl.pallas_call(
        body, name=name, out_shape=[_sds((2, 2 * r, c), F32) for r, c in shapes],
        grid_spec=pltpu.PrefetchScalarGridSpec(
            num_scalar_prefetch=1, grid=(2,),
            in_specs=[pl.BlockSpec((None, r, c), lambda s, p: (p[0], s, 0)) for r, c in shapes]
            + [pl.BlockSpec((3, r, c), lambda s, p: (0, s, 0)) for r, c in shapes],
            out_specs=[pl.BlockSpec((None, r, c), lambda s, p: (p[1], s, 0)) for r, c in shapes]),
        compiler_params=_params(("arbitrary",), 40),
    )(place, *_in_hbm(list(owns) + list(gots))))


def _adamw_math(w, g, m, v):
    m = B1 * m + (1.0 - B1) * g
    v = B2 * v + (1.0 - B2) * (g * g)
    m_hat = m / (1.0 - B1 ** STEP)
    v_hat = v / (1.0 - B2 ** STEP)
    return -LR * (m_hat / (jnp.sqrt(v_hat) + AEPS) + WD * w), m, v


def _adamw(ws, gs, ms, vs, name, after=()):
    n, steps = len(ws), 4

    def body(*refs):
        ins, outs = refs[:4 * n], refs[4 * n:]
        for i in range(n):
            w_ref, g_ref, m_ref, v_ref = ins[4 * i:4 * i + 4]
            go_ref, d_ref, nm_ref, nv_ref = outs[4 * i:4 * i + 4]
            g = g_ref[...]
            go_ref[...] = g
            d_ref[...], nm_ref[...], nv_ref[...] = _adamw_math(w_ref[...], g, m_ref[...], v_ref[...])

    args, specs, shapes, free = [], [], [], []
    for i, (w, g, m, v) in enumerate(zip(ws, gs, ms, vs)):
        args += [w, g, m, v]
        specs += [pl.BlockSpec((w.shape[0] // steps, w.shape[1]), lambda r: (r, 0))] * 4
        shapes += [_sds(w.shape, F32)] * 4
        free += [4 * i, 4 * i + 2, 4 * i + 3]
    outs = _call(body, args, name=name, grid=(steps,), out_shape=shapes, in_specs=specs, out_specs=specs,
                 compiler_params=_params(("arbitrary",), 48), free=tuple(free), after=after)
    return [outs[4 * i:4 * i + 4] for i in range(n)]


def _small_update(gathered, w, m, v):
    rows = w.shape[0]

    def body(ga_ref, w_ref, m_ref, v_ref, *out_refs):
        g = ga_ref[0:rows, :]
        for dev in range(1, 8):
            g = g + ga_ref[dev * rows:(dev + 1) * rows, :]
        results = (g,) + _adamw_math(w_ref[...], g, m_ref[...], v_ref[...])
        for i, res in enumerate(results):
            out_refs[i][...] = res[:SMALL_HEAD, :]
            out_refs[4 + i][...] = res[SMALL_HEAD:, :]

    outs = pl.pallas_call(
        body, name="small_update",
        out_shape=[jax.ShapeDtypeStruct((SMALL_HEAD, 128), F32)] * 4
        + [jax.ShapeDtypeStruct((rows - SMALL_HEAD, 128), F32)] * 4,
        in_specs=[_VM] * 4, out_specs=[_VM] * 8,
    )(gathered, w, m, v)
    return outs[:4], outs[4:]


SMALL = ("ffn1_norm", "mix_norm", "ffn2_norm", "final_norm", "pool_scale", "loss", "pool_w_group")
SMALL_HEAD = 48
BIG = ("ffn1_w_gate_up", "ffn1_w_down", "w_in", "w_branch_pool", "w_branch_attn", "w_out",
       "ffn2_w_gate_up", "ffn2_w_down")
ORDER = ("ffn1_norm", "ffn1_w_gate_up", "ffn1_w_down", "mix_norm", "w_in", "pool_w_group", "pool_scale",
         "w_branch_pool", "w_branch_attn", "w_out", "ffn2_norm", "ffn2_w_gate_up", "ffn2_w_down", "final_norm")
SMALL_ROWS = 560


def _pack_small(t):
    parts = []
    for k in SMALL:
        rows = t[k].reshape(-1, 128) if k in t else jnp.zeros((1, 128), F32)
        parts.append(jnp.pad(rows, ((0, -rows.shape[0] % 8), (0, 0))))
    packed = jnp.concatenate(parts, axis=0)
    assert packed.shape == (SMALL_ROWS, 128), packed.shape
    return packed


def _unpack_small(head, group, like):
    out, at = {"pool_w_group": group.reshape(like["pool_w_group"].shape)}, 0
    for k in SMALL[:-1]:
        n = like[k].size // 128 if k in like else 1
        out[k] = head[at:at + n].reshape(like[k].shape) if k in like else head[at, 0]
        at += n + (-n % 8)
    return out


def _halves(g):
    return g.reshape(NSH, 2, g.shape[1] // 2, g.shape[2])


def kernel(x, ffn1_norm, ffn1_w_gate_up, ffn1_w_down, mix_norm, w_in, pool_w_group, pool_scale, w_branch_pool, w_branch_attn, w_out, ffn2_norm, ffn2_w_gate_up, ffn2_w_down, final_norm, loss_target, m_ffn1_norm, m_ffn1_w_gate_up, m_ffn1_w_down, m_mix_norm, m_w_in, m_pool_w_group, m_pool_scale, m_w_branch_pool, m_w_branch_attn, m_w_out, m_ffn2_norm, m_ffn2_w_gate_up, m_ffn2_w_down, m_final_norm, v_ffn1_norm, v_ffn1_w_gate_up, v_ffn1_w_down, v_mix_norm, v_w_in, v_pool_w_group, v_pool_scale, v_w_branch_pool, v_w_branch_attn, v_w_out, v_ffn2_norm, v_ffn2_w_gate_up, v_ffn2_w_down, v_final_norm):
    wts = dict(ffn1_norm=ffn1_norm, ffn1_w_gate_up=ffn1_w_gate_up, ffn1_w_down=ffn1_w_down, mix_norm=mix_norm,
               w_in=w_in, pool_w_group=pool_w_group, pool_scale=pool_scale, w_branch_pool=w_branch_pool,
               w_branch_attn=w_branch_attn, w_out=w_out, ffn2_norm=ffn2_norm, ffn2_w_gate_up=ffn2_w_gate_up,
               ffn2_w_down=ffn2_w_down, final_norm=final_norm)
    mom = dict(ffn1_norm=m_ffn1_norm, ffn1_w_gate_up=m_ffn1_w_gate_up, ffn1_w_down=m_ffn1_w_down,
               mix_norm=m_mix_norm, w_in=m_w_in, pool_w_group=m_pool_w_group, pool_scale=m_pool_scale,
               w_branch_pool=m_w_branch_pool, w_branch_attn=m_w_branch_attn, w_out=m_w_out,
               ffn2_norm=m_ffn2_norm, ffn2_w_gate_up=m_ffn2_w_gate_up, ffn2_w_down=m_ffn2_w_down,
               final_norm=m_final_norm)
    var = dict(ffn1_norm=v_ffn1_norm, ffn1_w_gate_up=v_ffn1_w_gate_up, ffn1_w_down=v_ffn1_w_down,
               mix_norm=v_mix_norm, w_in=v_w_in, pool_w_group=v_pool_w_group, pool_scale=v_pool_scale,
               w_branch_pool=v_w_branch_pool, w_branch_attn=v_w_branch_attn, w_out=v_w_out,
               ffn2_norm=v_ffn2_norm, ffn2_w_gate_up=v_ffn2_w_gate_up, ffn2_w_down=v_ffn2_w_down,
               final_norm=v_final_norm)

    c_idx = lax.axis_index("c").astype(jnp.int32).reshape(1)
    me_idx = (2 * lax.axis_index("x") + lax.axis_index("y")).astype(jnp.int32).reshape(1)
    place = jnp.concatenate([me_idx, c_idx])
    x0, tgt = x[0], loss_target[0]
    wgrp = pool_w_group[0].astype(BF16)
    g1, gm, g2, gf = ffn1_norm, mix_norm, ffn2_norm, final_norm.reshape(1, D)
    grad, delta, new_m, new_v = {}, {}, {}, {}

    def pair_sums(keys, parts, got):
        return _pair_sum(parts, got, c_idx, "pair_sum_" + keys[0])

    def chip_sums(keys, chip_parts, owned):
        return _chip_sum(chip_parts, owned, place, "chip_sum_" + keys[0])

    def adamw(keys, after=()):
        outs = _adamw([wts[k][0] for k in keys], [grad[k][0] for k in keys], [mom[k][0] for k in keys],
                      [var[k][0] for k in keys], "adamw_" + keys[0], after=after)
        for k, res in zip(keys, outs):
            grad[k], delta[k], new_m[k], new_v[k] = (o.reshape(wts[k].shape) for o in res)

    first, late = ("ffn1_w_gate_up", "ffn1_w_down"), ("w_branch_pool", "w_branch_attn", "w_out",
                                                       "ffn2_w_gate_up", "ffn2_w_down")
    own = {}
    for group in (first, ("w_in",), late):
        own.update(zip(group, _cast_into_block([wts[k][0] for k in group], me_idx, "cast_" + group[0])))
    full = dict(zip(first, _exchange_alone(_ex_gather([own[k] for k in first]), "gather_ffn1")))
    wgu1, wd1 = full["ffn1_w_gate_up"], full["ffn1_w_down"].reshape(DFF, D)
    (h1, n1, gu1, a1), (win,) = _ffn_fwd(x0, g1, wgu1, wd1, "ffn1_fwd", exchange=_ex_gather_direct([own["w_in"]]))
    sems_l, thru_l, token_l = _gather_start([own[k_] for k_ in late], [h1], "gather_late_start")
    u, xp, q, k, v, gp, gs = _mix_in(h1, gm, win, after=(token_l,))
    o_sb, ctot = _attn_fwd(q, k, v)
    arrived = _gather_wait(sems_l, thru_l, [o_sb], "gather_late_wait")
    wbp, wba, wout = _exchange_alone(_ex_relay(arrived[:3]), "relay_mix")
    wout = wout.reshape(D, D)
    (h2, pm, p, yp, ys, mm), (wgu2, wd2) = _mix_out(h1, xp, o_sb, gp, gs, wgrp, pool_scale, wbp, wba, wout,
                                                    exchange=_ex_relay(arrived[3:]))
    wd2 = wd2.reshape(DFF, D)
    dh3, loss_row, d_gf, n3, gu3, a3 = _ffn_fwd(h2, g2, wgu2, wd2, "ffn2_fwd", head=(tgt, gf))

    def grad_gate_up(n, dgu, name, exchange=None):
        res = _wgrad(n, dgu, NSH, D, name, exchange=exchange)
        return [_halves(res)] if exchange is None else ([_halves(res[0])], res[1])

    def grad_down(a, dh, name, exchange=None):
        res = _wgrad(a, dh, 1, FFS, name, exchange=exchange)
        halves = lambda g: [_halves(g.reshape(NSH, DFF // NSH, D))]
        return halves(res) if exchange is None else (halves(res[0]), res[1])

    k_gu2, k_d2, k_gu1, k_d1, k_in = (("ffn2_w_gate_up",), ("ffn2_w_down",), ("ffn1_w_gate_up",),
                                      ("ffn1_w_down",), ("w_in",))
    dh2, dgu3, d_g2 = _ffn_bwd(dh3, h2, g2, gu3, wgu2, wd2, "ffn2_bwd")
    pa = grad_gate_up(n3, dgu3, "wgrad_gu2") + grad_down(a3, dh3, "wgrad_d2")
    (dlg, dyp, dys, do_sb, dyg, dxp, d_scale), got_a = _mix_bwd_out(
        dh2, gp, gs, yp, ys, pm, wgrp, pool_scale, wbp, wba, wout, exchange=_ex_pair_swap(pa))
    chip_a = pair_sums(k_gu2 + k_d2, pa, got_a)
    kb = ("w_out", "w_branch_pool", "w_branch_attn")
    pb = [_halves(_wgrad(mm, dh2, 1, D, "wgrad_out").reshape(NSH, D // NSH, D)),
          _halves(_wgrad(p, dyp, NSH, PW, "wgrad_bp")), _halves(_wgrad(o_sb, dys, NSH, SBW, "wgrad_ba"))]
    k_a, k_in = k_gu2 + k_d2, k_in + kb
    sems_a, thru_a, token_a = _scatter_start(chip_a, "scatter_a_start")
    dq, dk, dv = _attn_bwd(q, k, v, do_sb, ctot, after=(token_a,))
    chip_a, owned_a = _scatter_wait(sems_a, thru_a, [dq], "scatter_a_wait")
    halves_a = chip_sums(k_a, chip_a, owned_a)
    (dh1, d_gm, dproj), both_a = _mix_bwd_in(dh2, h1, gm, (dxp, dq, dk, dv, dlg), win, exchange=_ex_share(halves_a))
    for i, k_ in enumerate(k_a):
        grad[k_] = both_a[i].reshape(wts[k_].shape)

    p_in = [_halves(_wgrad(u, dproj, NSH, D, "wgrad_in"))] + pb
    p_d1, got_in = grad_down(a1, dh1, "wgrad_d1", exchange=_ex_pair_swap(p_in))
    sems_in, thru_in, token_in = _scatter_start(pair_sums(k_in, p_in, got_in), "scatter_in_start")
    dgu1, got_d1 = _ffn_bwd_act(dh1, gu1, wd1, "ffn1_bwd_act", exchange=_ex_pair_swap(p_d1), after=(token_in,))
    sems_d1, thru_d1, token_d1 = _scatter_start(pair_sums(k_d1, p_d1, got_d1), "scatter_d1_start")
    p_gu1 = [_halves(_wgrad(n1, dgu1, NSH, D, "wgrad_gu1", after=(token_in, token_d1)))]
    chip_in, owned_in = _scatter_wait(sems_in, thru_in, p_gu1, "scatter_in_wait")
    chip_d1, owned_d1 = _scatter_wait(sems_d1, thru_d1, p_gu1, "scatter_d1_wait")
    halves_in, halves_d1 = chip_sums(k_in, chip_in, owned_in), chip_sums(k_d1, chip_d1, owned_d1)
    landed = _exchange_alone(_join(_ex_pair_swap(p_gu1), _ex_share(halves_in)), "pair_swap_gu1")
    for i, k_ in enumerate(k_in):
        grad[k_] = landed[1 + i].reshape(wts[k_].shape)
    sems, thru, token = _scatter_start(pair_sums(k_gu1, p_gu1, landed[:1]), "scatter_gu1_start")
    adamw(k_a, after=(token,))
    adamw(k_in, after=(token,))
    dx, d_g1 = _ffn_bwd_in(dh1, x0, g1, dgu1, wgu1, "ffn1_bwd_in", after=(token,))
    small_g = dict(ffn1_norm=d_g1, mix_norm=d_gm, ffn2_norm=d_g2, final_norm=d_gf, pool_scale=d_scale,
                   pool_w_group=_wgrad_groups(pm, dyg), loss=loss_row)
    dev = 4 * lax.axis_index("x") + 2 * lax.axis_index("y") + lax.axis_index("c")
    slots = lax.dynamic_update_slice(jnp.zeros((8, SMALL_ROWS, 128), F32), _pack_small(small_g)[None], (dev, 0, 0))
    sems_s, slots, token_s = _small_gather_start(slots, "small_gather_start")

    chip_gu1, owned_gu1 = _scatter_wait(sems, thru, [dx] + [delta[k_] for k_ in k_a + k_in], "scatter_gu1_wait")
    both = _exchange_alone(_ex_share(halves_d1 + chip_sums(k_gu1, chip_gu1, owned_gu1)), "share_last",
                           after=(token_s,))
    grad["ffn1_w_down"] = both[0].reshape(ffn1_w_down.shape)
    grad["ffn1_w_gate_up"] = both[1].reshape(ffn1_w_gate_up.shape)
    adamw(k_d1 + k_gu1, after=(token_s,))
    gathered = _small_gather_wait(sems_s, slots, [delta[k_] for k_ in k_d1 + k_gu1], "small_gather_wait")
    gathered = gathered.reshape(8 * SMALL_ROWS, 128)
    heads, groups = _small_update(gathered, _pack_small(wts), _pack_small(mom), _pack_small(var))
    for dst, head, group in zip((grad, delta, new_m, new_v), heads, groups):
        vals = _unpack_small(head, group, wts)
        if dst is grad:
            loss = vals["loss"]
        vals.pop("loss")
        dst.update(vals)
    return (loss, dx[None], *[grad[k_] for k_ in ORDER], *[delta[k_] for k_ in ORDER],
            *[new_m[k_] for k_ in ORDER], *[new_v[k_] for k_ in ORDER])
```

```python
import dataclasses
import functools

import jax
import jax.numpy as jnp
from jax import lax
from jax.experimental import pallas as pl
from jax.experimental.pallas import tpu as pltpu

F32 = jnp.float32
BF16 = jnp.bfloat16

S = 2048
D = 1024
DFF = 2816
FFS = 2 * DFF // 4
NSH = 4
PW = 512
PG = 128
POOL_WINDOWS = (2, 4, 8, 16)
HALO = 16
SBW = 512
DH = 64
EPS = 1e-6
SCALE = 0.125
LOG2E = 1.4426950408889634
TA = 256
QB = 2
MIB = 1024 * 1024

LR, B1, B2, AEPS, WD, STEP = 0.001, 0.9, 0.999, 1e-08, 0.01, 10

_VM = pl.BlockSpec(memory_space=pltpu.VMEM)
_ANY = pl.BlockSpec(memory_space=pl.ANY)
MESH = pl.DeviceIdType.MESH
SIBLING_PAIR_ID = 1


def _nn(a, b):
    return jnp.dot(a, b, preferred_element_type=F32)


def _nt(a, b):
    return lax.dot_general(a, b, (((1,), (1,)), ((), ())), preferred_element_type=F32)


def _tn(a, b):
    return lax.dot_general(a, b, (((0,), (0,)), ((), ())), preferred_element_type=F32)


def _params(sem, vmem_mib):
    return pltpu.CompilerParams(dimension_semantics=sem, vmem_limit_bytes=vmem_mib * MIB)


def _rows(tm, width):
    return pl.BlockSpec((tm, width), lambda i: (i, 0))


def _fixed(shape):
    return pl.BlockSpec(shape, lambda *_: (0,) * len(shape))


def _sds(shape, dtype):
    return pltpu.HBM(shape, dtype)


def _in_hbm(args):
    return [pltpu.with_memory_space_constraint(a, pltpu.HBM) for a in args]


def _stage(pairs):
    @pl.when(pl.program_id(0) == 0)
    def _():
        for src, dst in pairs:
            pltpu.sync_copy(src, dst)


def _vmem_like(*arrays):
    return [pltpu.VMEM(a.shape, a.dtype) for a in arrays]


class Exchange:
    def __init__(self, arrays, landing, aliases, n_sems, start, finish, sibling_only=False):
        self.arrays, self.landing, self.aliases, self.n_sems = list(arrays), list(landing), dict(aliases), n_sems
        self.start, self.finish = start, finish
        self.sibling_only = sibling_only

    def enter(self):
        if self.sibling_only:
            barrier = pltpu.get_barrier_semaphore()
            sibling = (lax.axis_index("x"), lax.axis_index("y"), 1 - lax.axis_index("c"))
            pl.semaphore_signal(barrier, inc=1, device_id=sibling, device_id_type=MESH)
            pl.semaphore_wait(barrier, 1)

    def params(self, compiler_params=None):
        kw = dict(collective_id=SIBLING_PAIR_ID) if self.sibling_only else {}
        if compiler_params is None:
            return pltpu.CompilerParams(**kw)
        return dataclasses.replace(compiler_params, **kw)


def _join(a, b):
    na, la = len(a.arrays), len(a.landing)

    def both(fa, fb):
        def run(ins, outs, ssem, rsem):
            fa(ins[:na], outs[:la], ssem.at[pl.ds(0, a.n_sems)], rsem.at[pl.ds(0, a.n_sems)])
            fb(ins[na:], outs[la:], ssem.at[pl.ds(a.n_sems, b.n_sems)], rsem.at[pl.ds(a.n_sems, b.n_sems)])
        return run

    aliases = {**a.aliases, **{na + i: la + j for i, j in b.aliases.items()}}
    return Exchange(a.arrays + b.arrays, a.landing + b.landing, aliases, a.n_sems + b.n_sems,
                    both(a.start, b.start), both(a.finish, b.finish), a.sibling_only and b.sibling_only)


def _call(body, args, *, name, grid, in_specs, out_specs, out_shape, scratch_shapes=(), compiler_params=None,
          exchange=None, free=(), after=()):
    args = [a if i in free else pltpu.with_memory_space_constraint(a, pltpu.HBM) for i, a in enumerate(args)]
    if exchange is None:
        n_in = len(in_specs)

        def plain(*refs):
            body(*refs[:n_in], *refs[n_in + len(after):])

        return pl.pallas_call(plain, name=name, grid=grid, in_specs=list(in_specs) + [_ANY] * len(after),
                              out_specs=out_specs, out_shape=out_shape, scratch_shapes=list(scratch_shapes),
                              compiler_params=compiler_params)(*args, *after)
    ex = exchange
    n_in, n_out, n_scr = len(in_specs), len(out_specs), len(scratch_shapes)
    na, nl = len(ex.arrays), len(ex.landing)

    def hosted(*refs):
        at = [0]

        def take(n):
            at[0] += n
            return refs[at[0] - n:at[0]]

        k_in, _, e_in, k_out, e_out, k_scr = take(n_in), take(len(after)), take(na), take(n_out), take(nl), take(n_scr)
        ssem, rsem = take(2)
        ids = [pl.program_id(a) for a in range(len(grid))]
        first = functools.reduce(jnp.logical_and, [i == 0 for i in ids])
        last = functools.reduce(jnp.logical_and, [i == g - 1 for i, g in zip(ids, grid)])

        @pl.when(first)
        def _():
            ex.enter()
            ex.start(e_in, e_out, ssem, rsem)

        body(*k_in, *k_out, *k_scr)

        @pl.when(last)
        def _():
            ex.finish(e_in, e_out, ssem, rsem)

    outs = pl.pallas_call(
        hosted, name=name, grid=grid,
        in_specs=list(in_specs) + [_ANY] * (len(after) + na), out_specs=list(out_specs) + [_ANY] * nl,
        out_shape=list(out_shape) + ex.landing,
        scratch_shapes=list(scratch_shapes) + [pltpu.SemaphoreType.DMA((ex.n_sems,))] * 2,
        input_output_aliases={n_in + len(after) + i: n_out + j for i, j in ex.aliases.items()},
        compiler_params=ex.params(compiler_params),
    )(*args, *after, *_in_hbm(ex.arrays))
    return outs[:n_out], outs[n_out:]


def _exchange_alone(ex, name, after=()):
    na, nl = len(ex.arrays), len(ex.landing)

    def body(*refs):
        outs = refs[na + len(after):na + len(after) + nl]
        ex.enter()
        ex.start(refs[:na], outs, refs[-2], refs[-1])
        ex.finish(refs[:na], outs, refs[-2], refs[-1])

    return pl.pallas_call(
        body, name=name, in_specs=[_ANY] * (na + len(after)), out_specs=[_ANY] * nl,
        out_shape=ex.landing, scratch_shapes=[pltpu.SemaphoreType.DMA((ex.n_sems,))] * 2,
        input_output_aliases=ex.aliases, compiler_params=ex.params(),
    )(*_in_hbm(ex.arrays), *after)


_HBM = pl.BlockSpec(memory_space=pltpu.HBM)
_SEM = pl.BlockSpec(memory_space=pltpu.SEMAPHORE)
_EFFECT = pltpu.SideEffectType.DATAFLOW_SIDE_EFFECTING


def _scatter_copies(srcs, lands, ssems, rsems):
    x, y, c, chips = _place()
    return [_remote(srcs[w].at[2 * px + py], lands[w].at[k], ssems[3 * w + k], rsems[3 * w + k], (px, py, c))
            for w in range(len(srcs)) for k, (px, py) in enumerate(chips)]


def _scatter_start(parts, name):
    parts = list(parts)
    n, ncp = len(parts), 3 * len(parts)
    lands = [lax.empty((3,) + p.shape[1:], p.dtype) for p in parts]

    def body(*refs):
        srcs, land_refs = refs[:n], refs[n:2 * n]
        ssems, rsems = refs[2 * n:2 * n + ncp], refs[2 * n + ncp:2 * n + 2 * ncp]
        for cp in _scatter_copies(srcs, land_refs, ssems, rsems):
            cp.start()
        token = refs[-1]
        token[...] = jnp.zeros_like(token)

    outs = pl.pallas_call(
        body, name=name,
        out_shape=([pltpu.SemaphoreType.DMA(())] * (2 * ncp) + [pltpu.HBM(a.shape, a.dtype) for a in parts + lands]
                   + [jax.ShapeDtypeStruct((8, 128), F32)]),
        in_specs=[_HBM] * (2 * n), out_specs=[_SEM] * (2 * ncp) + [_HBM] * (2 * n) + [_VM],
        input_output_aliases={i: 2 * ncp + i for i in range(2 * n)},
        compiler_params=pltpu.CompilerParams(has_side_effects=_EFFECT),
    )(*_in_hbm(parts), *_in_hbm(lands))
    sems, thru, token = outs[:2 * ncp], outs[2 * ncp:2 * ncp + 2 * n], outs[-1]
    return sems, thru, token


def _scatter_wait(sems, thru, after, name):
    n = len(thru) // 2
    ncp = 3 * n

    def body(*refs):
        srcs, land_refs = refs[:n], refs[n:2 * n]
        ssems, rsems = refs[2 * n:2 * n + ncp], refs[2 * n + ncp:2 * n + 2 * ncp]
        for cp in _scatter_copies(srcs, land_refs, ssems, rsems):
            cp.wait_send()
            cp.wait_recv()

    outs = pl.pallas_call(
        body, name=name, out_shape=[pltpu.HBM(a.shape, a.dtype) for a in thru],
        in_specs=[_HBM] * (2 * n) + [_SEM] * (2 * ncp) + [_ANY] * len(after), out_specs=[_HBM] * (2 * n),
        input_output_aliases={i: i for i in range(2 * n)},
        compiler_params=pltpu.CompilerParams(has_side_effects=_EFFECT),
    )(*thru, *sems, *after)
    return outs[:n], outs[n:]


def _gather_copies(bufs, ssems, rsems, sending):
    x, y, c, chips = _place()
    out = []
    for w, ref in enumerate(bufs):
        half = ref.shape[1] // 2
        for k, (px, py) in enumerate(chips):
            rows = ref.at[2 * x + y if sending else 2 * px + py, pl.ds(c * half, half)]
            out.append(_remote(rows, rows, ssems[3 * w + k], rsems[3 * w + k], (px, py, c)))
    return out


def _gather_start(bufs, after, name):
    n, ncp = len(bufs), 3 * len(bufs)

    def body(*refs):
        ssems, rsems = refs[n + len(after):n + len(after) + ncp], refs[n + len(after) + ncp:n + len(after) + 2 * ncp]
        for cp in _gather_copies(refs[:n], ssems, rsems, True):
            cp.start()
        token = refs[-1]
        token[...] = jnp.zeros_like(token)

    outs = pl.pallas_call(
        body, name=name,
        out_shape=([pltpu.SemaphoreType.DMA(())] * (2 * ncp) + [pltpu.HBM(a.shape, a.dtype) for a in bufs]
                   + [jax.ShapeDtypeStruct((8, 128), F32)]),
        in_specs=[_HBM] * n + [_ANY] * len(after), out_specs=[_SEM] * (2 * ncp) + [_HBM] * n + [_VM],
        input_output_aliases={i: 2 * ncp + i for i in range(n)},
        compiler_params=pltpu.CompilerParams(has_side_effects=_EFFECT),
    )(*_in_hbm(bufs), *after)
    return outs[:2 * ncp], outs[2 * ncp:2 * ncp + n], outs[-1]


def _gather_wait(sems, thru, after, name):
    n = len(thru)
    ncp = 3 * n

    def body(*refs):
        ssems, rsems = refs[n:n + ncp], refs[n + ncp:n + 2 * ncp]
        for cp in _gather_copies(refs[:n], ssems, rsems, True):
            cp.wait_send()
        for cp in _gather_copies(refs[:n], ssems, rsems, False):
            cp.wait_recv()

    return pl.pallas_call(
        body, name=name, out_shape=[pltpu.HBM(a.shape, a.dtype) for a in thru],
        in_specs=[_HBM] * n + [_SEM] * (2 * ncp) + [_ANY] * len(after), out_specs=[_HBM] * n,
        input_output_aliases={i: i for i in range(n)},
        compiler_params=pltpu.CompilerParams(has_side_effects=_EFFECT),
    )(*thru, *sems, *after)


def _rms(x):
    r = lax.rsqrt(jnp.mean(x * x, axis=-1, keepdims=True) + EPS)
    return r, x * r


def _rms_bwd(dn, xr, r, gain):
    dng = dn * gain
    dx = r * (dng - xr * jnp.mean(dng * xr, axis=-1, keepdims=True))
    return dx, jnp.sum(dn * xr, axis=0, keepdims=True)


def _ffn_fwd(x, gain, wgu, wd, name, exchange=None, head=None):
    tm = 256

    def body(x_ref, g_ref, wgu_hbm, wd_hbm, *rest):
        if head is None:
            h_ref, n_ref, gu_ref, a_ref, wgu_ref, wd_ref = rest
        else:
            t_ref, gf_ref, h_ref, loss_ref, dgf_ref, n_ref, gu_ref, a_ref, wgu_ref, wd_ref = rest
        _stage([(wgu_hbm, wgu_ref), (wd_hbm, wd_ref)])
        x = x_ref[...]
        _, xr = _rms(x)
        n = (xr * g_ref[...]).astype(BF16)
        n_ref[...] = n
        acc = jnp.zeros((tm, D), F32)
        for j in range(2):
            g = _nn(n, wgu_ref[j])
            u = _nn(n, wgu_ref[2 + j])
            gu_ref[:, j * FFS:(j + 1) * FFS] = g.astype(BF16)
            gu_ref[:, (2 + j) * FFS:(3 + j) * FFS] = u.astype(BF16)
            half_act = (0.5 * (g * jax.nn.sigmoid(g) * u)).astype(BF16)
            a_ref[:, j * FFS:(j + 1) * FFS] = half_act
            acc = acc + _nn(half_act, wd_ref[j * FFS:(j + 1) * FFS, :])
        h = x + acc
        if head is None:
            h_ref[...] = h
            return
        gf = gf_ref[...]
        r, hr = _rms(h)
        err = hr * gf - t_ref[...]
        dh, dgain = _rms_bwd(err * (1.0 / D), hr, r, gf)
        h_ref[...] = dh

        @pl.when(pl.program_id(0) == 0)
        def _():
            dgf_ref[...] = jnp.zeros_like(dgf_ref)
            loss_ref[...] = jnp.zeros_like(loss_ref)

        dgf_ref[...] += dgain
        loss_ref[...] += jnp.full((1, 128), (0.5 / D) * jnp.sum(err * err), F32)

    saved_specs = [_rows(tm, D), _rows(tm, 4 * FFS), _rows(tm, DFF)]
    saved_shapes = [_sds((S, D), BF16), _sds((S, 4 * FFS), BF16), _sds((S, DFF), BF16)]
    if head is None:
        return _call(
            body, (x, gain, wgu, wd), name=name, grid=(S // tm,),
            in_specs=[_rows(tm, D), _fixed((1, D)), _ANY, _ANY],
            out_specs=[_rows(tm, D)] + saved_specs, out_shape=[_sds((S, D), F32)] + saved_shapes,
            scratch_shapes=_vmem_like(wgu, wd),
            compiler_params=_params(("arbitrary",), 56), exchange=exchange)
    return _call(
        body, (x, gain, wgu, wd, *head), name=name, grid=(S // tm,),
        in_specs=[_rows(tm, D), _fixed((1, D)), _ANY, _ANY, _rows(tm, D), _fixed((1, D))],
        out_specs=[_rows(tm, D), _fixed((1, 128)), _fixed((1, D))] + saved_specs,
        out_shape=[_sds((S, D), F32), _sds((1, 128), F32), _sds((1, D), F32)] + saved_shapes,
        scratch_shapes=_vmem_like(wgu, wd),
        compiler_params=_params(("arbitrary",), 56), exchange=exchange, free=(4, 5))


def _ffn_bwd(dh, x, gain, gu, wgu, wd, name):
    tm = 256

    def body(dh_ref, x_ref, g_ref, gu_ref, wgu_hbm, wd_hbm, dx_ref, dgu_ref, dg_ref, wgu_ref, wd_ref):
        _stage([(wgu_hbm, wgu_ref), (wd_hbm, wd_ref)])
        dh = dh_ref[...]
        dhb = dh.astype(BF16)
        dn = jnp.zeros((tm, D), F32)
        for j in range(2):
            g = gu_ref[:, j * FFS:(j + 1) * FFS].astype(F32)
            u = gu_ref[:, (2 + j) * FFS:(3 + j) * FFS].astype(F32)
            da = 0.5 * _nt(dhb, wd_ref[j * FFS:(j + 1) * FFS, :])
            sg = jax.nn.sigmoid(g)
            dgb = (da * u * (sg * (1.0 + g * (1.0 - sg)))).astype(BF16)
            dub = (da * (g * sg)).astype(BF16)
            dgu_ref[:, j * FFS:(j + 1) * FFS] = dgb
            dgu_ref[:, (2 + j) * FFS:(3 + j) * FFS] = dub
            dn = dn + _nt(dgb, wgu_ref[j]) + _nt(dub, wgu_ref[2 + j])
        r, xr = _rms(x_ref[...])
        dx, dgain = _rms_bwd(dn, xr, r, g_ref[...])
        dx_ref[...] = dh + dx

        @pl.when(pl.program_id(0) == 0)
        def _():
            dg_ref[...] = jnp.zeros_like(dg_ref)

        dg_ref[...] += dgain

    return _call(
        body, (dh, x, gain, gu, wgu, wd), name=name, grid=(S // tm,),
        in_specs=[_rows(tm, D), _rows(tm, D), _fixed((1, D)), _rows(tm, 4 * FFS), _ANY, _ANY],
        out_specs=[_rows(tm, D), _rows(tm, 4 * FFS), _fixed((1, D))],
        out_shape=[_sds((S, D), F32), _sds((S, 4 * FFS), BF16), _sds((1, D), F32)],
        scratch_shapes=_vmem_like(wgu, wd), compiler_params=_params(("arbitrary",), 56))


def _ffn_bwd_act(dh, gu, wd, name, exchange=None, after=()):
    tm = 512

    def body(dh_ref, gu_ref, wd_hbm, dgu_ref, wd_ref):
        _stage([(wd_hbm, wd_ref)])
        dhb = dh_ref[...].astype(BF16)
        for j in range(2):
            g = gu_ref[:, j * FFS:(j + 1) * FFS].astype(F32)
            u = gu_ref[:, (2 + j) * FFS:(3 + j) * FFS].astype(F32)
            da = 0.5 * _nt(dhb, wd_ref[j * FFS:(j + 1) * FFS, :])
            sg = jax.nn.sigmoid(g)
            dgu_ref[:, j * FFS:(j + 1) * FFS] = (da * u * (sg * (1.0 + g * (1.0 - sg)))).astype(BF16)
            dgu_ref[:, (2 + j) * FFS:(3 + j) * FFS] = (da * (g * sg)).astype(BF16)

    res = _call(
        body, (dh, gu, wd), name=name, grid=(S // tm,),
        in_specs=[_rows(tm, D), _rows(tm, 4 * FFS), _ANY], out_specs=[_rows(tm, 4 * FFS)],
        out_shape=[_sds((S, 4 * FFS), BF16)], scratch_shapes=_vmem_like(wd),
        compiler_params=_params(("arbitrary",), 56), exchange=exchange, after=after)
    return res[0] if exchange is None else (res[0][0], res[1])


def _ffn_bwd_in(dh, x, gain, dgu, wgu, name, exchange=None, after=()):
    tm = 512

    def body(dh_ref, x_ref, g_ref, dgu_ref, wgu_hbm, dx_ref, dg_ref, wgu_ref):
        _stage([(wgu_hbm, wgu_ref)])
        dn = jnp.zeros((tm, D), F32)
        for j in range(NSH):
            dn = dn + _nt(dgu_ref[:, j * FFS:(j + 1) * FFS], wgu_ref[j])
        r, xr = _rms(x_ref[...])
        dx, dgain = _rms_bwd(dn, xr, r, g_ref[...])
        dx_ref[...] = dh_ref[...] + dx

        @pl.when(pl.program_id(0) == 0)
        def _():
            dg_ref[...] = jnp.zeros_like(dg_ref)

        dg_ref[...] += dgain

    return _call(
        body, (dh, x, gain, dgu, wgu), name=name, grid=(S // tm,),
        in_specs=[_rows(tm, D), _rows(tm, D), _fixed((1, D)), _rows(tm, 4 * FFS), _ANY],
        out_specs=[_rows(tm, D), _fixed((1, D))],
        out_shape=[_sds((S, D), F32), _sds((1, D), F32)],
        scratch_shapes=_vmem_like(wgu),
        compiler_params=_params(("arbitrary",), 56), exchange=exchange, after=after)


def _mix_in(h, gain, w_in, after=()):
    tm = 512

    def body(h_ref, g_ref, w_hbm, u_ref, xp_ref, q_ref, k_ref, v_ref, gp_ref, gs_ref, w_ref):
        _stage([(w_hbm, w_ref)])
        _, hr = _rms(h_ref[...])
        u = (hr * g_ref[...]).astype(BF16)
        u_ref[...] = u
        p0 = _nn(u, w_ref[0])
        xp_ref[...] = p0[:, :PW]
        q_ref[...] = p0[:, PW:].astype(BF16)
        p1 = _nn(u, w_ref[1])
        k_ref[...] = p1[:, :SBW].astype(BF16)
        v_ref[...] = p1[:, SBW:].astype(BF16)
        gp_ref[...] = jax.nn.sigmoid(_nn(u, w_ref[2])).astype(BF16)
        gs_ref[...] = jax.nn.sigmoid(_nn(u, w_ref[3])).astype(BF16)

    return _call(
        body, (h, gain, w_in), name="mix_in", grid=(S // tm,),
        in_specs=[_rows(tm, D), _fixed((1, D)), _ANY],
        out_specs=[_rows(tm, D), _rows(tm, PW), _rows(tm, SBW), _rows(tm, SBW), _rows(tm, SBW),
                   _rows(tm, D), _rows(tm, D)],
        out_shape=[_sds((S, D), BF16), _sds((S, PW), F32), _sds((S, SBW), BF16), _sds((S, SBW), BF16),
                   _sds((S, SBW), BF16), _sds((S, D), BF16), _sds((S, D), BF16)],
        scratch_shapes=_vmem_like(w_in),
        compiler_params=_params(("arbitrary",), 48), free=(1,), after=after)


def _hilo_dot(x, tri):
    hi = x.astype(BF16)
    lo = (x - hi.astype(F32)).astype(BF16)
    return _nn(hi, tri) + _nn(lo, tri)


def _log_terms(qk):
    z2 = qk * (SCALE * LOG2E)
    lb = jnp.minimum(z2, 0.0) - jnp.log2(1.0 + jnp.exp2(-jnp.abs(z2)))
    return lb, lb - z2


def _head_masks():
    lane = lax.broadcasted_iota(jnp.int32, (1, 2 * DH), 1)
    return (lane < DH, lane >= DH)


def _attn_fwd(q, k, v, exchange=None):
    T = TA

    def body(q_ref, k_ref, v_ref, o_ref, c_ref):
        i2 = 2 * pl.program_id(1)
        row = lax.broadcasted_iota(jnp.int32, (T, T), 0)
        col = lax.broadcasted_iota(jnp.int32, (T, T), 1)
        after = (row > col).astype(BF16)
        causal = col < row
        masks = _head_masks()
        qms = {}
        for b in range(QB):
            q2 = q_ref[b * T:(b + 1) * T, :]
            for h, hm in enumerate(masks):
                qms[b, h] = jnp.where(hm, q2, jnp.zeros_like(q2))

        def blocks(keys, pairs, carries, os):
            ks, vms = [], []
            for j in keys:
                rows = pl.ds(pl.multiple_of(j * T, T), T)
                vj = v_ref[rows, :]
                ks.append(k_ref[rows, :])
                vms.append([jnp.where(hm, vj, jnp.zeros_like(vj)) for hm in masks])
            units = [(n, h) for n in range(len(pairs)) for h in range(2)]
            qks = {(n, h): _nt(qms[pairs[n][0], h], ks[pairs[n][1]]) for n, h in units}
            lbs, l1ms = {}, {}
            for u in units:
                lbs[u], l1m = _log_terms(qks[u])
                l1ms[u] = jnp.where(causal, l1m, 0.0) if pairs[u[0]][2] else l1m
            cins = {u: _hilo_dot(l1ms[u], after) for u in units}
            carries, os = dict(carries), list(os)
            for n, h in units:
                b, key, diag = pairs[n]
                a = jnp.exp2(lbs[n, h] + cins[n, h] + carries[b, h])
                if diag:
                    a = jnp.where(causal, a, 0.0)
                os[b] = os[b] + _nn(a.astype(BF16), vms[key][h])
                carries[b, h] = carries[b, h] + jnp.sum(l1ms[n, h], axis=1, keepdims=True)
            return carries, tuple(os)

        carries = {(b, h): jnp.zeros((T, 1), F32) for b in range(QB) for h in range(2)}
        os = tuple(jnp.zeros((T, 2 * DH), F32) for _ in range(QB))
        carries, os = blocks([i2 + 1, i2], [(1, 0, True), (0, 1, True), (1, 1, False)], carries, os)
        carries, os = lax.fori_loop(
            0, i2 // 2,
            lambda t, c: blocks([i2 - 1 - 2 * t, i2 - 2 - 2 * t],
                                [(0, 0, False), (1, 0, False), (0, 1, False), (1, 1, False)], c[0], c[1]),
            (carries, os))
        for b in range(QB):
            o_ref[b * T:(b + 1) * T, :] = os[b].astype(BF16)
            c_ref[b * T:(b + 1) * T, :] = jnp.where(masks[0], carries[b, 0], carries[b, 1])

    blk = pl.BlockSpec((QB * T, 2 * DH), lambda p, i: (i, p))
    full = pl.BlockSpec((S, 2 * DH), lambda p, i: (0, p))
    return _call(
        body, (q, k, v), name="attn_fwd", grid=(SBW // (2 * DH), S // (QB * T)),
        in_specs=[blk, full, full], out_specs=[blk, blk],
        out_shape=[_sds((S, SBW), BF16), _sds((S, SBW), F32)],
        compiler_params=_params(("arbitrary", "arbitrary"), 40), exchange=exchange)


def _attn_bwd(q, k, v, do, ctot, after=()):
    T = TA
    nq = S // (QB * T)

    def body(q_ref, k_ref, v_ref, do_ref, c_ref, dq_ref, dk_ref, dv_ref, dk_acc, dv_acc):
        step = pl.program_id(1)
        i2 = 2 * step

        @pl.when(step == 0)
        def _():
            dk_acc[...] = jnp.zeros_like(dk_acc)
            dv_acc[...] = jnp.zeros_like(dv_acc)

        row = lax.broadcasted_iota(jnp.int32, (T, T), 0)
        col = lax.broadcasted_iota(jnp.int32, (T, T), 1)
        upto = (row <= col).astype(BF16)
        before = (row < col).astype(BF16)
        causal = col < row
        masks = _head_masks()
        qms, doms, ctots = {}, {}, {}
        for b in range(QB):
            q2, do2 = q_ref[b * T:(b + 1) * T, :], do_ref[b * T:(b + 1) * T, :]
            for h, hm in enumerate(masks):
                qms[b, h] = jnp.where(hm, q2, jnp.zeros_like(q2))
                doms[b, h] = jnp.where(hm, do2, jnp.zeros_like(do2))
                ctots[b, h] = c_ref[b * T:(b + 1) * T, h * DH:h * DH + 1]

        def blocks(keys, pairs, sums, dqs):
            rows = [pl.ds(pl.multiple_of(j * T, T), T) for j in keys]
            ks, vs = [k_ref[r, :] for r in rows], [v_ref[r, :] for r in rows]
            kms = [[jnp.where(hm, kj, jnp.zeros_like(kj)) for hm in masks] for kj in ks]
            units = [(n, h) for n in range(len(pairs)) for h in range(2)]
            qks = {(n, h): _nt(qms[pairs[n][0], h], ks[pairs[n][1]]) for n, h in units}
            das = {(n, h): _nt(doms[pairs[n][0], h], vs[pairs[n][1]]) for n, h in units}
            lbs, l1ms = {}, {}
            for u in units:
                lbs[u], l1m = _log_terms(qks[u])
                l1ms[u] = jnp.where(causal, l1m, 0.0) if pairs[u[0]][2] else l1m
            pins = {u: _hilo_dot(l1ms[u], upto) for u in units}
            sums = dict(sums)
            a_s, dls, cps = {}, {}, {}
            for n, h in units:
                b, _, diag = pairs[n]
                cl, cp = sums[b, h]
                a = jnp.exp2(lbs[n, h] + (ctots[b, h] - cl) - pins[n, h])
                if diag:
                    a = jnp.where(causal, a, 0.0)
                a_s[n, h] = a.astype(BF16)
                dls[n, h] = das[n, h] * a
                cps[n, h] = cp
                sums[b, h] = (cl + jnp.sum(l1ms[n, h], axis=1, keepdims=True),
                              cp + jnp.sum(dls[n, h], axis=1, keepdims=True))
            pexs = {u: _hilo_dot(dls[u], before) for u in units}
            dzbs = {}
            for u in units:
                dz = dls[u] - jnp.exp2(lbs[u]) * (dls[u] + pexs[u] + cps[u])
                if pairs[u[0]][2]:
                    dz = jnp.where(causal, dz, 0.0)
                dzbs[u] = dz.astype(BF16)
            dqs = list(dqs)
            for n, h in units:
                dqs[pairs[n][0]] = dqs[pairs[n][0]] + _nn(dzbs[n, h], kms[pairs[n][1]][h])
            for key, r in enumerate(rows):
                mine = [(n, h) for n, h in units if pairs[n][1] == key]
                dk_acc[r, :] += functools.reduce(jnp.add, [_tn(dzbs[u], qms[pairs[u[0]][0], u[1]]) for u in mine])
                dv_acc[r, :] += functools.reduce(jnp.add, [_tn(a_s[u], doms[pairs[u[0]][0], u[1]]) for u in mine])
            return sums, tuple(dqs)

        zero = jnp.zeros((T, 1), F32)
        sums = {(b, h): (zero, zero) for b in range(QB) for h in range(2)}
        dqs = tuple(jnp.zeros((T, 2 * DH), F32) for _ in range(QB))
        sums, dqs = lax.fori_loop(
            0, i2 // 2,
            lambda t, c: blocks([2 * t, 2 * t + 1],
                                [(0, 0, False), (1, 0, False), (0, 1, False), (1, 1, False)], c[0], c[1]),
            (sums, dqs))
        _, dqs = blocks([i2, i2 + 1], [(0, 0, True), (1, 0, False), (1, 1, True)], sums, dqs)
        for b in range(QB):
            dq_ref[b * T:(b + 1) * T, :] = (dqs[b] * SCALE).astype(BF16)

        @pl.when(step == nq - 1)
        def _():
            dk_ref[...] = (dk_acc[...] * SCALE).astype(BF16)
            dv_ref[...] = dv_acc[...].astype(BF16)

    blk = pl.BlockSpec((QB * T, 2 * DH), lambda p, i: (i, p))
    full = pl.BlockSpec((S, 2 * DH), lambda p, i: (0, p))
    return _call(
        body, (q, k, v, do, ctot), name="attn_bwd", grid=(SBW // (2 * DH), nq),
        in_specs=[blk, full, full, blk, blk], out_specs=[blk, full, full],
        out_shape=[_sds((S, SBW), BF16), _sds((S, SBW), BF16), _sds((S, SBW), BF16)],
        scratch_shapes=[pltpu.VMEM((S, 2 * DH), F32), pltpu.VMEM((S, 2 * DH), F32)],
        compiler_params=_params(("arbitrary", "arbitrary"), 40), after=after)


def _pool_counts(first_row, tm):
    pos = first_row + lax.broadcasted_iota(jnp.int32, (tm, 1), 0)
    return [jnp.minimum(pos + 1, w).astype(F32) for w in POOL_WINDOWS]


def _mix_out(h, xp, o_sb, gp, gs, w_group, scale, w_bp, w_ba, w_out, exchange=None):
    tm = 512

    def body(h_ref, xp_ref, o_ref, gp_ref, gs_ref, wg_hbm, sc_ref, wbp_hbm, wba_hbm, wo_hbm,
             h2_ref, pm_ref, p_ref, yp_ref, ys_ref, m_ref, halo, wg_ref, wbp_ref, wba_ref, wo_ref):
        _stage([(wg_hbm, wg_ref), (wbp_hbm, wbp_ref), (wba_hbm, wba_ref), (wo_hbm, wo_ref)])
        i = pl.program_id(0)

        @pl.when(i == 0)
        def _():
            halo[...] = jnp.zeros_like(halo)

        xp = xp_ref[...]
        ext = jnp.concatenate([halo[...], xp], axis=0)
        halo[...] = xp[tm - HALO:, :]
        counts = _pool_counts(i * tm, tm)
        for gi in range(len(POOL_WINDOWS)):
            lanes = slice(gi * PG, (gi + 1) * PG)
            win = ext[:, lanes]
            for step in range(gi + 1):
                win = win + pltpu.roll(win, 1 << step, 0)
            pm = (win[HALO:, :] / counts[gi] - xp[:, lanes]).astype(BF16)
            pm_ref[:, lanes] = pm
            p_ref[:, lanes] = (_nn(pm, wg_ref[gi]) * sc_ref[:, lanes]).astype(BF16)
        pb = p_ref[...]
        ob = o_ref[...]
        for j in range(NSH):
            cols = slice(j * (D // NSH), (j + 1) * (D // NSH))
            yp = _nn(pb, wbp_ref[j])
            ys = _nn(ob, wba_ref[j])
            yp_ref[:, cols] = yp.astype(BF16)
            ys_ref[:, cols] = ys.astype(BF16)
            m_ref[:, cols] = (gp_ref[:, cols].astype(F32) * yp + gs_ref[:, cols].astype(F32) * ys).astype(BF16)
        h2_ref[...] = h_ref[...] + _nn(m_ref[...], wo_ref[...])

    return _call(
        body, (h, xp, o_sb, gp, gs, w_group, scale, w_bp, w_ba, w_out), name="mix_out", grid=(S // tm,),
        in_specs=[_rows(tm, D), _rows(tm, PW), _rows(tm, SBW), _rows(tm, D), _rows(tm, D),
                  _ANY, _fixed((1, PW)), _ANY, _ANY, _ANY],
        out_specs=[_rows(tm, D), _rows(tm, PW), _rows(tm, PW), _rows(tm, D), _rows(tm, D), _rows(tm, D)],
        out_shape=[_sds((S, D), F32), _sds((S, PW), BF16), _sds((S, PW), BF16), _sds((S, D), BF16),
                   _sds((S, D), BF16), _sds((S, D), BF16)],
        scratch_shapes=[pltpu.VMEM((HALO, PW), F32)] + _vmem_like(w_group, w_bp, w_ba, w_out),
        compiler_params=_params(("arbitrary",), 48), free=(5, 6), exchange=exchange)


def _mix_bwd_out(dh, gp, gs, yp, ys, pm, w_group, scale, w_bp, w_ba, w_out, exchange=None):
    tm = 512
    nt = S // tm

    def body(dh_ref, gp_ref, gs_ref, yp_ref, ys_ref, pm_ref, wg_hbm, sc_ref, wbp_hbm, wba_hbm, wo_hbm,
             dlg_ref, dyp_ref, dys_ref, do_ref, dyg_ref, dxp_ref, dsc_ref, halo, wg_ref, wbp_ref, wba_ref, wo_ref):
        _stage([(wg_hbm, wg_ref), (wbp_hbm, wbp_ref), (wba_hbm, wba_ref), (wo_hbm, wo_ref)])
        step = pl.program_id(0)

        @pl.when(step == 0)
        def _():
            halo[...] = jnp.zeros_like(halo)
            dsc_ref[...] = jnp.zeros_like(dsc_ref)

        dm = _nt(dh_ref[...].astype(BF16), wo_ref[...])
        gp = gp_ref[...].astype(F32)
        gs = gs_ref[...].astype(F32)
        yp = yp_ref[...].astype(F32)
        ys = ys_ref[...].astype(F32)
        dlg_ref[:, :D] = (dm * yp * gp * (1.0 - gp)).astype(BF16)
        dlg_ref[:, D:] = (dm * ys * gs * (1.0 - gs)).astype(BF16)
        dyp_ref[...] = (dm * gp).astype(BF16)
        dys_ref[...] = (dm * gs).astype(BF16)
        dp = jnp.zeros((tm, PW), F32)
        do = jnp.zeros((tm, SBW), F32)
        for j in range(NSH):
            cols = slice(j * (D // NSH), (j + 1) * (D // NSH))
            dp = dp + _nt(dyp_ref[:, cols], wbp_ref[j])
            do = do + _nt(dys_ref[:, cols], wba_ref[j])
        do_ref[...] = do.astype(BF16)
        counts = _pool_counts((nt - 1 - step) * tm, tm)
        dscale = []
        for gi in range(len(POOL_WINDOWS)):
            lanes = slice(gi * PG, (gi + 1) * PG)
            dpg = dp[:, lanes]
            dscale.append(jnp.sum(dpg * _nn(pm_ref[:, lanes], wg_ref[gi]), axis=0, keepdims=True))
            dyg = (dpg * sc_ref[:, lanes]).astype(BF16)
            dyg_ref[:, lanes] = dyg
            dpm = _nt(dyg, wg_ref[gi])
            per = dpm / counts[gi]
            win = jnp.concatenate([per, halo[:, lanes]], axis=0)
            halo[:, lanes] = per[:HALO, :]
            for s in range(gi + 1):
                win = win + pltpu.roll(win, tm + HALO - (1 << s), 0)
            dxp_ref[:, lanes] = (win[:tm, :] - dpm).astype(BF16)
        dsc_ref[...] += jnp.concatenate(dscale, axis=1)

    rev = lambda width: pl.BlockSpec((tm, width), lambda i: (nt - 1 - i, 0))
    return _call(
        body, (dh, gp, gs, yp, ys, pm, w_group, scale, w_bp, w_ba, w_out), name="mix_bwd_out", grid=(nt,),
        in_specs=[rev(D), rev(D), rev(D), rev(D), rev(D), rev(PW), _ANY, _fixed((1, PW)), _ANY, _ANY, _ANY],
        out_specs=[rev(2 * D), rev(D), rev(D), rev(SBW), rev(PW), rev(PW), _fixed((1, PW))],
        out_shape=[_sds((S, 2 * D), BF16), _sds((S, D), BF16), _sds((S, D), BF16), _sds((S, SBW), BF16),
                   _sds((S, PW), BF16), _sds((S, PW), BF16), _sds((1, PW), F32)],
        scratch_shapes=[pltpu.VMEM((HALO, PW), F32)] + _vmem_like(w_group, w_bp, w_ba, w_out),
        compiler_params=_params(("arbitrary",), 48), exchange=exchange)


def _mix_bwd_in(dh, h, gain, pieces, w_in, exchange=None):
    tm = 512
    widths = [p.shape[1] for p in pieces]

    def body(dh_ref, h_ref, g_ref, *rest):
        piece_refs, (w_hbm, dx_ref, dg_ref, dp_ref, w_ref) = rest[:len(pieces)], rest[len(pieces):]
        _stage([(w_hbm, w_ref)])
        at = 0
        for ref, width in zip(piece_refs, widths):
            dp_ref[:, at:at + width] = ref[...]
            at += width
        du = jnp.zeros((tm, D), F32)
        for j in range(NSH):
            du = du + _nt(dp_ref[:, j * D:(j + 1) * D], w_ref[j])
        r, hr = _rms(h_ref[...])
        dx, dgain = _rms_bwd(du, hr, r, g_ref[...])
        dx_ref[...] = dh_ref[...] + dx

        @pl.when(pl.program_id(0) == 0)
        def _():
            dg_ref[...] = jnp.zeros_like(dg_ref)

        dg_ref[...] += dgain

    return _call(
        body, (dh, h, gain, *pieces, w_in), name="mix_bwd_in", grid=(S // tm,),
        in_specs=[_rows(tm, D), _rows(tm, D), _fixed((1, D))] + [_rows(tm, w) for w in widths] + [_ANY],
        out_specs=[_rows(tm, D), _fixed((1, D)), _rows(tm, 4 * D)],
        out_shape=[_sds((S, D), F32), _sds((1, D), F32), _sds((S, 4 * D), BF16)],
        scratch_shapes=_vmem_like(w_in),
        compiler_params=_params(("arbitrary",), 48), exchange=exchange)


def _wgrad(a, b, nblk, ti, name, out_dtype=BF16, exchange=None, after=()):
    ka, n = a.shape[1], b.shape[1]
    ns = n // nblk

    def body(a_ref, b_ref, o_ref):
        o_ref[...] = _tn(a_ref[...].astype(BF16), b_ref[...].astype(BF16)).astype(out_dtype)

    res = _call(
        body, (a, b), name=name, grid=(nblk, ka // ti),
        in_specs=[pl.BlockSpec((S, ti), lambda j, i: (0, i)), pl.BlockSpec((S, ns), lambda j, i: (0, j))],
        out_specs=[pl.BlockSpec((None, ti, ns), lambda j, i: (j, i, 0))],
        out_shape=[_sds((nblk, ka, ns), out_dtype)],
        compiler_params=_params(("arbitrary", "arbitrary"), 56), exchange=exchange, after=after)
    return res[0] if exchange is None else (res[0][0], res[1])


def _wgrad_branches(p, dyp, o_sb, dys, pm, dyg):
    cols = D // NSH

    def body(p_ref, dyp_ref, o_ref, dys_ref, pm_ref, dyg_ref, gbp_ref, gba_ref, gg_ref):
        gbp_ref[...] = _tn(p_ref[...], dyp_ref[...]).astype(BF16)
        gba_ref[...] = _tn(o_ref[...], dys_ref[...]).astype(BF16)
        gg_ref[...] = _tn(pm_ref[...], dyg_ref[...])

    whole = lambda width: pl.BlockSpec((S, width), lambda j: (0, 0))
    col = lambda width: pl.BlockSpec((S, width), lambda j: (0, j))
    return _call(
        body, (p, dyp, o_sb, dys, pm, dyg), name="wgrad_branches", grid=(NSH,),
        in_specs=[whole(PW), col(cols), whole(SBW), col(cols), col(PG), col(PG)],
        out_specs=[pl.BlockSpec((None, PW, cols), lambda j: (j, 0, 0)),
                   pl.BlockSpec((None, SBW, cols), lambda j: (j, 0, 0)),
                   pl.BlockSpec((None, PG, PG), lambda j: (j, 0, 0))],
        out_shape=[_sds((NSH, PW, cols), BF16), _sds((NSH, SBW, cols), BF16), _sds((NSH, PG, PG), F32)],
        compiler_params=_params(("arbitrary",), 40))


def _place():
    x, y, c = lax.axis_index("x"), lax.axis_index("y"), lax.axis_index("c")
    chips = [(1 - x, y), (x, 1 - y), (1 - x, 1 - y)]
    return x, y, c, chips


def _remote(src, dst, ssem, rsem, dev):
    return pltpu.make_async_remote_copy(src_ref=src, dst_ref=dst, send_sem=ssem, recv_sem=rsem,
                                        device_id=dev, device_id_type=MESH)


def _cast_into_block(ws, me_idx, name):
    steps = 4
    shapes = [(w.shape[0] // steps, w.shape[1]) for w in ws]

    def body(me_ref, *refs):
        for w_ref, o_ref in zip(refs[:len(ws)], refs[len(ws):]):
            o_ref[...] = w_ref[...].astype(BF16)

    return pl.pallas_call(
        body, name=name, out_shape=[_sds((NSH,) + w.shape, BF16) for w in ws],
        grid_spec=pltpu.PrefetchScalarGridSpec(
            num_scalar_prefetch=1, grid=(steps,),
            in_specs=[pl.BlockSpec((r, c), lambda s, me: (s, 0)) for r, c in shapes],
            out_specs=[pl.BlockSpec((None, r, c), lambda s, me: (me[0], s, 0)) for r, c in shapes]),
        compiler_params=_params(("arbitrary",), 32),
    )(me_idx, *ws)


def _ex_gather(bufs):
    n = len(bufs)
    per = 8

    def plan(outs, ssem, rsem, w):
        x, y, c, _ = _place()
        sib, nbr_x, nbr_y = (x, y, 1 - c), (1 - x, y, c), (x, 1 - y, c)
        half = outs[w].shape[1] // 2
        quarter = half // 2
        sem = lambda k: (ssem.at[per * w + k], rsem.at[per * w + k])
        rows = lambda blk, start, size: outs[w].at[blk, pl.ds(start, size)]
        mine = rows(2 * x + y, c * half, half)
        from_x = rows(2 * (1 - x) + y, c * half, half)
        from_y = rows(2 * x + (1 - y), c * half, half)
        diag = 2 * (1 - x) + (1 - y)
        pass_y = rows(2 * (1 - x) + y, c * half, quarter)
        pass_x = rows(2 * x + (1 - y), c * half + quarter, quarter)
        diag_0, diag_1 = rows(diag, c * half, quarter), rows(diag, c * half + quarter, quarter)
        first = [_remote(mine, mine, *sem(0), nbr_x), _remote(mine, mine, *sem(1), nbr_y)]
        arrivals = [
            (_remote(from_x, from_x, *sem(0), nbr_x),
             [_remote(pass_y, pass_y, *sem(2), nbr_y), _remote(from_x, from_x, *sem(4), sib)]),
            (_remote(from_y, from_y, *sem(1), nbr_y),
             [_remote(pass_x, pass_x, *sem(3), nbr_x), _remote(from_y, from_y, *sem(5), sib)]),
            (_remote(diag_0, diag_0, *sem(2), nbr_y), [_remote(diag_0, diag_0, *sem(6), sib)]),
            (_remote(diag_1, diag_1, *sem(3), nbr_x), [_remote(diag_1, diag_1, *sem(7), sib)]),
        ]
        other = (1 - c) * half
        from_sibling = [
            _remote(rows(2 * (1 - x) + y, other, half), rows(2 * (1 - x) + y, other, half), *sem(4), sib),
            _remote(rows(2 * x + (1 - y), other, half), rows(2 * x + (1 - y), other, half), *sem(5), sib),
            _remote(rows(diag, other, quarter), rows(diag, other, quarter), *sem(6), sib),
            _remote(rows(diag, other + quarter, quarter), rows(diag, other + quarter, quarter), *sem(7), sib),
        ]
        return first, arrivals, from_sibling

    def start(ins, outs, ssem, rsem):
        x, y, c, _ = _place()
        for w in range(n):
            half = outs[w].shape[1] // 2
            mine = outs[w].at[2 * x + y, pl.ds(c * half, half)]
            _remote(mine, mine, ssem.at[per * w], rsem.at[per * w], (1 - x, y, c)).start()
            _remote(mine, mine, ssem.at[per * w + 1], rsem.at[per * w + 1], (x, 1 - y, c)).start()

    def finish(ins, outs, ssem, rsem):
        plans = [plan(outs, ssem, rsem, w) for w in range(n)]
        started = []
        for direct in (True, False):
            for first, arrivals, _ in plans:
                for arrived, onward in (arrivals[:2] if direct else arrivals[2:]):
                    arrived.wait_recv()
                    for cp in onward:
                        cp.start()
                    started += onward
        for first, _, from_sibling in plans:
            for cp in from_sibling:
                cp.wait_recv()
            started += first
        for cp in started:
            cp.wait_send()

    return Exchange(bufs, [_sds(b.shape, b.dtype) for b in bufs], {w: w for w in range(n)}, per * n, start, finish)


def _ex_gather_direct(bufs):
    n = len(bufs)

    def copies(outs, ssem, rsem, only_first=False):
        x, y, c, chips = _place()
        me, sib = 2 * x + y, (x, y, 1 - c)
        first, relay, last = [], [], []
        for w in range(n):
            half = outs[w].shape[1] // 2
            mine = outs[w].at[me, pl.ds(c * half, half)]
            for k, (px, py) in enumerate(chips):
                sems = (ssem.at[6 * w + k], rsem.at[6 * w + k])
                sib_sems = (ssem.at[6 * w + 3 + k], rsem.at[6 * w + 3 + k])
                first.append(_remote(mine, mine, *sems, (px, py, c)))
                if only_first:
                    continue
                got = outs[w].at[2 * px + py, pl.ds(c * half, half)]
                relay.append((_remote(got, got, *sems, (px, py, c)), _remote(got, got, *sib_sems, sib)))
                theirs = outs[w].at[2 * px + py, pl.ds((1 - c) * half, half)]
                last.append(_remote(theirs, theirs, *sib_sems, sib))
        return first, relay, last

    def start(ins, outs, ssem, rsem):
        for cp in copies(outs, ssem, rsem, only_first=True)[0]:
            cp.start()

    def finish(ins, outs, ssem, rsem):
        first, relay, last = copies(outs, ssem, rsem)
        for arrived, onward in relay:
            arrived.wait_recv()
            onward.start()
        for cp in last:
            cp.wait_recv()
        for cp in first:
            cp.wait_send()
        for _, onward in relay:
            onward.wait_send()

    return Exchange(bufs, [_sds(b.shape, b.dtype) for b in bufs], {w: w for w in range(n)}, 6 * n, start, finish)


def _simple_exchange(arrays, landing, aliases, make_copies, sibling_only=False):
    def start(ins, outs, ssem, rsem):
        for cp, _ in make_copies(ins, outs, ssem, rsem, False):
            cp.start()

    def finish(ins, outs, ssem, rsem):
        cps = make_copies(ins, outs, ssem, rsem, True)
        for _, landed in cps:
            landed.wait_recv()
        for cp, _ in cps:
            cp.wait_send()

    return Exchange(arrays, landing, aliases, len(arrays) * 3, start, finish, sibling_only)


def _ex_pair_swap(grads):
    def make(ins, outs, ssem, rsem, landing):
        x, y, c, _ = _place()
        cps = [_remote(ins[w].at[:, 1 - c], outs[w], ssem.at[w], rsem.at[w], (x, y, 1 - c))
               for w in range(len(grads))]
        return [(cp, cp) for cp in cps]

    return _simple_exchange(grads, [_sds((NSH,) + g.shape[2:], g.dtype) for g in grads], {}, make, True)


def _ex_scatter(parts):
    def make(ins, outs, ssem, rsem, landing):
        x, y, c, chips = _place()
        out = []
        for w in range(len(parts)):
            for k, (px, py) in enumerate(chips):
                sems = (ssem.at[3 * w + k], rsem.at[3 * w + k])
                out.append((_remote(ins[w].at[2 * px + py], outs[w].at[k], *sems, (px, py, c)),
                            _remote(outs[w].at[k], outs[w].at[k], *sems, (px, py, c)) if landing else None))
        return out

    return _simple_exchange(parts, [_sds((3,) + p.shape[1:], p.dtype) for p in parts], {}, make)


def _ex_relay(bufs):
    def make(ins, outs, ssem, rsem, landing):
        x, y, c, chips = _place()
        sib = (x, y, 1 - c)
        out = []
        for w in range(len(bufs)):
            half = outs[w].shape[1] // 2
            for k, (px, py) in enumerate(chips):
                sems = (ssem.at[3 * w + k], rsem.at[3 * w + k])
                have = outs[w].at[2 * px + py, pl.ds(c * half, half)]
                miss = outs[w].at[2 * px + py, pl.ds((1 - c) * half, half)]
                out.append((_remote(have, have, *sems, sib), _remote(miss, miss, *sems, sib) if landing else None))
        return out

    return _simple_exchange(bufs, [_sds(b.shape, b.dtype) for b in bufs], {w: w for w in range(len(bufs))}, make, True)


def _ex_share(bufs):
    def make(ins, outs, ssem, rsem, landing):
        x, y, c, _ = _place()
        sib = (x, y, 1 - c)
        return [(_remote(outs[w].at[c], outs[w].at[c], ssem.at[w], rsem.at[w], sib),
                 _remote(outs[w].at[1 - c], outs[w].at[1 - c], ssem.at[w], rsem.at[w], sib) if landing else None)
                for w in range(len(bufs))]

    return _simple_exchange(bufs, [_sds(b.shape, b.dtype) for b in bufs], {w: w for w in range(len(bufs))}, make, True)


def _small_copies(slots, ssems, rsems, sending):
    x, y, c, _ = _place()
    out = []
    for m in range(1, 8):
        px, py, pc = x ^ (m >> 2), y ^ ((m >> 1) & 1), c ^ (m & 1)
        slot = slots.at[4 * x + 2 * y + c if sending else 4 * px + 2 * py + pc]
        out.append(_remote(slot, slot, ssems[m - 1], rsems[m - 1], (px, py, pc)))
    return out


def _small_gather_start(slots, name):
    def body(*refs):
        for cp in _small_copies(refs[0], refs[1:8], refs[8:15], True):
            cp.start()
        refs[-1][...] = jnp.zeros_like(refs[-1])

    outs = pl.pallas_call(
        body, name=name,
        out_shape=([pltpu.SemaphoreType.DMA(())] * 14 + [pltpu.HBM(slots.shape, slots.dtype)]
                   + [jax.ShapeDtypeStruct((8, 128), F32)]),
        in_specs=[_HBM], out_specs=[_SEM] * 14 + [_HBM, _VM], input_output_aliases={0: 14},
        compiler_params=pltpu.CompilerParams(has_side_effects=_EFFECT),
    )(*_in_hbm([slots]))
    return outs[:14], outs[14], outs[15]


def _small_gather_wait(sems, slots, after, name):
    def body(*refs):
        for cp in _small_copies(refs[0], refs[1:8], refs[8:15], True):
            cp.wait_send()
        for cp in _small_copies(refs[0], refs[1:8], refs[8:15], False):
            cp.wait_recv()

    return pl.pallas_call(
        body, name=name, out_shape=pltpu.HBM(slots.shape, slots.dtype),
        in_specs=[_HBM] + [_SEM] * 14 + [_ANY] * len(after), out_specs=_HBM, input_output_aliases={0: 0},
        compiler_params=pltpu.CompilerParams(has_side_effects=_EFFECT),
    )(slots, *sems, *after)


def _row_block(rows, cap=256):
    return max(t for t in range(16, cap + 1, 16) if rows % t == 0)


def _pair_sum(grads, gots, c_idx, name):
    n = len(grads)

    def body(c_ref, *refs):
        for a_ref, b_ref, o_ref in zip(refs[:n], refs[n:2 * n], refs[2 * n:]):
            o_ref[...] = (a_ref[...].astype(F32) + b_ref[...].astype(F32)).astype(BF16)

    halves = [g.shape[2:] for g in grads]
    return list(pl.pallas_call(
        body, name=name, out_shape=[_sds((NSH,) + h, BF16) for h in halves],
        grid_spec=pltpu.PrefetchScalarGridSpec(
            num_scalar_prefetch=1, grid=(NSH,),
            in_specs=[pl.BlockSpec((None, None) + h, lambda j, c: (j, c[0], 0, 0)) for h in halves]
            + [pl.BlockSpec((None,) + h, lambda j, c: (j, 0, 0)) for h in halves],
            out_specs=[pl.BlockSpec((None,) + h, lambda j, c: (j, 0, 0)) for h in halves]),
        compiler_params=_params(("arbitrary",), 40),
    )(c_idx, *_in_hbm(list(grads) + list(gots))))


def _chip_sum(owns, gots, place, name):
    n = len(owns)

    def body(place_ref, *refs):
        for own_ref, got_ref, o_ref in zip(refs[:n], refs[n:2 * n], refs[2 * n:]):
            acc = own_ref[...].astype(F32)
            for k in range(3):
                acc = acc + got_ref[k].astype(F32)
            o_ref[...] = acc

    shapes = [(o.shape[1] // 2, o.shape[2]) for o in owns]
    return list(pl.pallas_call(
        body, name=name, out_shape=[_sds((2, 2 * r, c), F32) for r, c in shapes],
        grid_spec=pltpu.PrefetchScalarGridSpec(
            num_scalar_prefetch=1, grid=(2,),
            in_specs=[pl.BlockSpec((None, r, c), lambda s, p: (p[0], s, 0)) for r, c in shapes]
            + [pl.BlockSpec((3, r, c), lambda s, p: (0, s, 0)) for r, c in shapes],
            out_specs=[pl.BlockSpec((None, r, c), lambda s, p: (p[1], s, 0)) for r, c in shapes]),
        compiler_params=_params(("arbitrary",), 40),
    )(place, *_in_hbm(list(owns) + list(gots))))


def _adamw_math(w, g, m, v):
    m = B1 * m + (1.0 - B1) * g
    v = B2 * v + (1.0 - B2) * (g * g)
    m_hat = m / (1.0 - B1 ** STEP)
    v_hat = v / (1.0 - B2 ** STEP)
    return -LR * (m_hat / (jnp.sqrt(v_hat) + AEPS) + WD * w), m, v


def _adamw(ws, gs, ms, vs, name, after=()):
    n, steps = len(ws), 4

    def body(*refs):
        ins, outs = refs[:4 * n], refs[4 * n:]
        for i in range(n):
            w_ref, g_ref, m_ref, v_ref = ins[4 * i:4 * i + 4]
            go_ref, d_ref, nm_ref, nv_ref = outs[4 * i:4 * i + 4]
            g = g_ref[...]
            go_ref[...] = g
            d_ref[...], nm_ref[...], nv_ref[...] = _adamw_math(w_ref[...], g, m_ref[...], v_ref[...])

    args, specs, shapes, free = [], [], [], []
    for i, (w, g, m, v) in enumerate(zip(ws, gs, ms, vs)):
        args += [w, g, m, v]
        specs += [pl.BlockSpec((w.shape[0] // steps, w.shape[1]), lambda r: (r, 0))] * 4
        shapes += [_sds(w.shape, F32)] * 4
        free += [4 * i, 4 * i + 2, 4 * i + 3]
    outs = _call(body, args, name=name, grid=(steps,), out_shape=shapes, in_specs=specs, out_specs=specs,
                 compiler_params=_params(("arbitrary",), 48), free=tuple(free), after=after)
    return [outs[4 * i:4 * i + 4] for i in range(n)]


def _small_update(gathered, w, m, v):
    rows = w.shape[0]

    def body(ga_ref, w_ref, m_ref, v_ref, *out_refs):
        g = ga_ref[0:rows, :]
        for dev in range(1, 8):
            g = g + ga_ref[dev * rows:(dev + 1) * rows, :]
        results = (g,) + _adamw_math(w_ref[...], g, m_ref[...], v_ref[...])
        for i, res in enumerate(results):
            out_refs[i][...] = res[:SMALL_HEAD, :]
            out_refs[4 + i][...] = res[SMALL_HEAD:, :]

    outs = pl.pallas_call(
        body, name="small_update",
        out_shape=[jax.ShapeDtypeStruct((SMALL_HEAD, 128), F32)] * 4
        + [jax.ShapeDtypeStruct((rows - SMALL_HEAD, 128), F32)] * 4,
        in_specs=[_VM] * 4, out_specs=[_VM] * 8,
    )(gathered, w, m, v)
    return outs[:4], outs[4:]


SMALL = ("ffn1_norm", "mix_norm", "ffn2_norm", "final_norm", "pool_scale", "loss", "pool_w_group")
SMALL_HEAD = 48
BIG = ("ffn1_w_gate_up", "ffn1_w_down", "w_in", "w_branch_pool", "w_branch_attn", "w_out",
       "ffn2_w_gate_up", "ffn2_w_down")
ORDER = ("ffn1_norm", "ffn1_w_gate_up", "ffn1_w_down", "mix_norm", "w_in", "pool_w_group", "pool_scale",
         "w_branch_pool", "w_branch_attn", "w_out", "ffn2_norm", "ffn2_w_gate_up", "ffn2_w_down", "final_norm")
SMALL_ROWS = 560


def _pack_small(t):
    parts = []
    for k in SMALL:
        rows = t[k].reshape(-1, 128) if k in t else jnp.zeros((1, 128), F32)
        parts.append(jnp.pad(rows, ((0, -rows.shape[0] % 8), (0, 0))))
    packed = jnp.concatenate(parts, axis=0)
    assert packed.shape == (SMALL_ROWS, 128), packed.shape
    return packed


def _unpack_small(head, group, like):
    out, at = {"pool_w_group": group.reshape(like["pool_w_group"].shape)}, 0
    for k in SMALL[:-1]:
        n = like[k].size // 128 if k in like else 1
        out[k] = head[at:at + n].reshape(like[k].shape) if k in like else head[at, 0]
        at += n + (-n % 8)
    return out


def _halves(g):
    return g.reshape(NSH, 2, g.shape[1] // 2, g.shape[2])


def kernel(x, ffn1_norm, ffn1_w_gate_up, ffn1_w_down, mix_norm, w_in, pool_w_group, pool_scale, w_branch_pool, w_branch_attn, w_out, ffn2_norm, ffn2_w_gate_up, ffn2_w_down, final_norm, loss_target, m_ffn1_norm, m_ffn1_w_gate_up, m_ffn1_w_down, m_mix_norm, m_w_in, m_pool_w_group, m_pool_scale, m_w_branch_pool, m_w_branch_attn, m_w_out, m_ffn2_norm, m_ffn2_w_gate_up, m_ffn2_w_down, m_final_norm, v_ffn1_norm, v_ffn1_w_gate_up, v_ffn1_w_down, v_mix_norm, v_w_in, v_pool_w_group, v_pool_scale, v_w_branch_pool, v_w_branch_attn, v_w_out, v_ffn2_norm, v_ffn2_w_gate_up, v_ffn2_w_down, v_final_norm):
    wts = dict(ffn1_norm=ffn1_norm, ffn1_w_gate_up=ffn1_w_gate_up, ffn1_w_down=ffn1_w_down, mix_norm=mix_norm,
               w_in=w_in, pool_w_group=pool_w_group, pool_scale=pool_scale, w_branch_pool=w_branch_pool,
               w_branch_attn=w_branch_attn, w_out=w_out, ffn2_norm=ffn2_norm, ffn2_w_gate_up=ffn2_w_gate_up,
               ffn2_w_down=ffn2_w_down, final_norm=final_norm)
    mom = dict(ffn1_norm=m_ffn1_norm, ffn1_w_gate_up=m_ffn1_w_gate_up, ffn1_w_down=m_ffn1_w_down,
               mix_norm=m_mix_norm, w_in=m_w_in, pool_w_group=m_pool_w_group, pool_scale=m_pool_scale,
               w_branch_pool=m_w_branch_pool, w_branch_attn=m_w_branch_attn, w_out=m_w_out,
               ffn2_norm=m_ffn2_norm, ffn2_w_gate_up=m_ffn2_w_gate_up, ffn2_w_down=m_ffn2_w_down,
               final_norm=m_final_norm)
    var = dict(ffn1_norm=v_ffn1_norm, ffn1_w_gate_up=v_ffn1_w_gate_up, ffn1_w_down=v_ffn1_w_down,
               mix_norm=v_mix_norm, w_in=v_w_in, pool_w_group=v_pool_w_group, pool_scale=v_pool_scale,
               w_branch_pool=v_w_branch_pool, w_branch_attn=v_w_branch_attn, w_out=v_w_out,
               ffn2_norm=v_ffn2_norm, ffn2_w_gate_up=v_ffn2_w_gate_up, ffn2_w_down=v_ffn2_w_down,
               final_norm=v_final_norm)

    c_idx = lax.axis_index("c").astype(jnp.int32).reshape(1)
    me_idx = (2 * lax.axis_index("x") + lax.axis_index("y")).astype(jnp.int32).reshape(1)
    place = jnp.concatenate([me_idx, c_idx])
    x0, tgt = x[0], loss_target[0]
    wgrp = pool_w_group[0].astype(BF16)
    g1, gm, g2, gf = ffn1_norm, mix_norm, ffn2_norm, final_norm.reshape(1, D)
    grad, delta, new_m, new_v = {}, {}, {}, {}

    def pair_sums(keys, parts, got):
        return _pair_sum(parts, got, c_idx, "pair_sum_" + keys[0])

    def chip_sums(keys, chip_parts, owned):
        return _chip_sum(chip_parts, owned, place, "chip_sum_" + keys[0])

    def adamw(keys, after=()):
        outs = _adamw([wts[k][0] for k in keys], [grad[k][0] for k in keys], [mom[k][0] for k in keys],
                      [var[k][0] for k in keys], "adamw_" + keys[0], after=after)
        for k, res in zip(keys, outs):
            grad[k], delta[k], new_m[k], new_v[k] = (o.reshape(wts[k].shape) for o in res)

    first, late = ("ffn1_w_gate_up", "ffn1_w_down"), ("w_branch_pool", "w_branch_attn", "w_out",
                                                       "ffn2_w_gate_up", "ffn2_w_down")
    own = {}
    for group in (first, ("w_in",), late):
        own.update(zip(group, _cast_into_block([wts[k][0] for k in group], me_idx, "cast_" + group[0])))
    full = dict(zip(first, _exchange_alone(_ex_gather([own[k] for k in first]), "gather_ffn1")))
    wgu1, wd1 = full["ffn1_w_gate_up"], full["ffn1_w_down"].reshape(DFF, D)
    (h1, n1, gu1, a1), (win,) = _ffn_fwd(x0, g1, wgu1, wd1, "ffn1_fwd", exchange=_ex_gather_direct([own["w_in"]]))
    sems_l, thru_l, token_l = _gather_start([own[k_] for k_ in late], [h1], "gather_late_start")
    u, xp, q, k, v, gp, gs = _mix_in(h1, gm, win, after=(token_l,))
    o_sb, ctot = _attn_fwd(q, k, v)
    arrived = _gather_wait(sems_l, thru_l, [o_sb], "gather_late_wait")
    wbp, wba, wout = _exchange_alone(_ex_relay(arrived[:3]), "relay_mix")
    wout = wout.reshape(D, D)
    (h2, pm, p, yp, ys, mm), (wgu2, wd2) = _mix_out(h1, xp, o_sb, gp, gs, wgrp, pool_scale, wbp, wba, wout,
                                                    exchange=_ex_relay(arrived[3:]))
    wd2 = wd2.reshape(DFF, D)
    dh3, loss_row, d_gf, n3, gu3, a3 = _ffn_fwd(h2, g2, wgu2, wd2, "ffn2_fwd", head=(tgt, gf))

    def grad_gate_up(n, dgu, name, exchange=None):
        res = _wgrad(n, dgu, NSH, D, name, exchange=exchange)
        return [_halves(res)] if exchange is None else ([_halves(res[0])], res[1])

    def grad_down(a, dh, name, exchange=None):
        res = _wgrad(a, dh, 1, FFS, name, exchange=exchange)
        halves = lambda g: [_halves(g.reshape(NSH, DFF // NSH, D))]
        return halves(res) if exchange is None else (halves(res[0]), res[1])

    k_gu2, k_d2, k_gu1, k_d1, k_in = (("ffn2_w_gate_up",), ("ffn2_w_down",), ("ffn1_w_gate_up",),
                                      ("ffn1_w_down",), ("w_in",))
    dh2, dgu3, d_g2 = _ffn_bwd(dh3, h2, g2, gu3, wgu2, wd2, "ffn2_bwd")
    pa = grad_gate_up(n3, dgu3, "wgrad_gu2") + grad_down(a3, dh3, "wgrad_d2")
    (dlg, dyp, dys, do_sb, dyg, dxp, d_scale), got_a = _mix_bwd_out(
        dh2, gp, gs, yp, ys, pm, wgrp, pool_scale, wbp, wba, wout, exchange=_ex_pair_swap(pa))
    chip_a = pair_sums(k_gu2 + k_d2, pa, got_a)
    kb = ("w_out", "w_branch_pool", "w_branch_attn")
    g_bp, g_ba, d_group = _wgrad_branches(p, dyp, o_sb, dys, pm, dyg)
    pb = [_halves(_wgrad(mm, dh2, 1, D, "wgrad_out").reshape(NSH, D // NSH, D)), _halves(g_bp), _halves(g_ba)]
    k_a, k_in = k_gu2 + k_d2, k_in + kb
    sems_a, thru_a, token_a = _scatter_start(chip_a, "scatter_a_start")
    dq, dk, dv = _attn_bwd(q, k, v, do_sb, ctot, after=(token_a,))
    chip_a, owned_a = _scatter_wait(sems_a, thru_a, [dq], "scatter_a_wait")
    halves_a = chip_sums(k_a, chip_a, owned_a)
    (dh1, d_gm, dproj), both_a = _mix_bwd_in(dh2, h1, gm, (dxp, dq, dk, dv, dlg), win, exchange=_ex_share(halves_a))
    for i, k_ in enumerate(k_a):
        grad[k_] = both_a[i].reshape(wts[k_].shape)

    p_in = [_halves(_wgrad(u, dproj, NSH, D, "wgrad_in"))] + pb
    p_d1, got_in = grad_down(a1, dh1, "wgrad_d1", exchange=_ex_pair_swap(p_in))
    sems_in, thru_in, token_in = _scatter_start(pair_sums(k_in, p_in, got_in), "scatter_in_start")
    dgu1, got_d1 = _ffn_bwd_act(dh1, gu1, wd1, "ffn1_bwd_act", exchange=_ex_pair_swap(p_d1), after=(token_in,))
    sems_d1, thru_d1, token_d1 = _scatter_start(pair_sums(k_d1, p_d1, got_d1), "scatter_d1_start")
    p_gu1 = [_halves(_wgrad(n1, dgu1, NSH, D, "wgrad_gu1", after=(token_in, token_d1)))]
    chip_in, owned_in = _scatter_wait(sems_in, thru_in, p_gu1, "scatter_in_wait")
    chip_d1, owned_d1 = _scatter_wait(sems_d1, thru_d1, p_gu1, "scatter_d1_wait")
    halves_in = chip_sums(k_in, chip_in, owned_in)
    landed = _exchange_alone(_join(_ex_pair_swap(p_gu1), _ex_share(halves_in)), "pair_swap_gu1")
    for i, k_ in enumerate(k_in):
        grad[k_] = landed[1 + i].reshape(wts[k_].shape)
    sems, thru, token = _scatter_start(pair_sums(k_gu1, p_gu1, landed[:1]), "scatter_gu1_start")
    adamw(k_a, after=(token,))
    adamw(k_in, after=(token,))
    dx, d_g1 = _ffn_bwd_in(dh1, x0, g1, dgu1, wgu1, "ffn1_bwd_in", after=(token,))
    small_g = dict(ffn1_norm=d_g1, mix_norm=d_gm, ffn2_norm=d_g2, final_norm=d_gf, pool_scale=d_scale,
                   pool_w_group=d_group, loss=loss_row)
    dev = 4 * lax.axis_index("x") + 2 * lax.axis_index("y") + lax.axis_index("c")
    slots = lax.dynamic_update_slice(jnp.zeros((8, SMALL_ROWS, 128), F32), _pack_small(small_g)[None], (dev, 0, 0))
    sems_s, slots, token_s = _small_gather_start(slots, "small_gather_start")

    chip_gu1, owned_gu1 = _scatter_wait(sems, thru, [dx] + [delta[k_] for k_ in k_a + k_in], "scatter_gu1_wait")
    halves_last = chip_sums(k_d1 + k_gu1, chip_d1 + chip_gu1, owned_d1 + owned_gu1)
    both = _exchange_alone(_ex_share(halves_last), "share_last",
                           after=(token_s,))
    grad["ffn1_w_down"] = both[0].reshape(ffn1_w_down.shape)
    grad["ffn1_w_gate_up"] = both[1].reshape(ffn1_w_gate_up.shape)
    adamw(k_d1 + k_gu1, after=(token_s,))
    gathered = _small_gather_wait(sems_s, slots, [delta[k_] for k_ in k_d1 + k_gu1], "small_gather_wait")
    gathered = gathered.reshape(8 * SMALL_ROWS, 128)
    heads, groups = _small_update(gathered, _pack_small(wts), _pack_small(mom), _pack_small(var))
    for dst, head, group in zip((grad, delta, new_m, new_v), heads, groups):
        vals = _unpack_small(head, group, wts)
        if dst is grad:
            loss = vals["loss"]
        vals.pop("loss")
        dst.update(vals)
    return (loss, dx[None], *[grad[k_] for k_ in ORDER], *[delta[k_] for k_ in ORDER],
            *[new_m[k_] for k_ in ORDER], *[new_v[k_] for k_ in ORDER])
```

```python
import dataclasses
import functools

import jax
import jax.numpy as jnp
from jax import lax
from jax.experimental import pallas as pl
from jax.experimental.pallas import tpu as pltpu

F32 = jnp.float32
BF16 = jnp.bfloat16

S = 2048
D = 1024
DFF = 2816
FFS = 2 * DFF // 4
NSH = 4
PW = 512
PG = 128
POOL_WINDOWS = (2, 4, 8, 16)
HALO = 16
SBW = 512
DH = 64
EPS = 1e-6
SCALE = 0.125
LOG2E = 1.4426950408889634
TA = 256
QB = 2
MIB = 1024 * 1024

LR, B1, B2, AEPS, WD, STEP = 0.001, 0.9, 0.999, 1e-08, 0.01, 10

_VM = pl.BlockSpec(memory_space=pltpu.VMEM)
_ANY = pl.BlockSpec(memory_space=pl.ANY)
MESH = pl.DeviceIdType.MESH
SIBLING_PAIR_ID = 1


def _nn(a, b):
    return jnp.dot(a, b, preferred_element_type=F32)


def _nt(a, b):
    return lax.dot_general(a, b, (((1,), (1,)), ((), ())), preferred_element_type=F32)


def _tn(a, b):
    return lax.dot_general(a, b, (((0,), (0,)), ((), ())), preferred_element_type=F32)


def _params(sem, vmem_mib):
    return pltpu.CompilerParams(dimension_semantics=sem, vmem_limit_bytes=vmem_mib * MIB)


def _rows(tm, width):
    return pl.BlockSpec((tm, width), lambda i: (i, 0))


def _fixed(shape):
    return pl.BlockSpec(shape, lambda *_: (0,) * len(shape))


def _sds(shape, dtype):
    return pltpu.HBM(shape, dtype)


def _in_hbm(args):
    return [pltpu.with_memory_space_constraint(a, pltpu.HBM) for a in args]


def _stage(pairs):
    @pl.when(pl.program_id(0) == 0)
    def _():
        for src, dst in pairs:
            pltpu.sync_copy(src, dst)


def _vmem_like(*arrays):
    return [pltpu.VMEM(a.shape, a.dtype) for a in arrays]


class Exchange:
    def __init__(self, arrays, landing, aliases, n_sems, start, finish, sibling_only=False):
        self.arrays, self.landing, self.aliases, self.n_sems = list(arrays), list(landing), dict(aliases), n_sems
        self.start, self.finish = start, finish
        self.sibling_only = sibling_only

    def enter(self):
        if self.sibling_only:
            barrier = pltpu.get_barrier_semaphore()
            sibling = (lax.axis_index("x"), lax.axis_index("y"), 1 - lax.axis_index("c"))
            pl.semaphore_signal(barrier, inc=1, device_id=sibling, device_id_type=MESH)
            pl.semaphore_wait(barrier, 1)

    def params(self, compiler_params=None):
        kw = dict(collective_id=SIBLING_PAIR_ID) if self.sibling_only else {}
        if compiler_params is None:
            return pltpu.CompilerParams(**kw)
        return dataclasses.replace(compiler_params, **kw)


def _join(a, b):
    na, la = len(a.arrays), len(a.landing)

    def both(fa, fb):
        def run(ins, outs, ssem, rsem):
            fa(ins[:na], outs[:la], ssem.at[pl.ds(0, a.n_sems)], rsem.at[pl.ds(0, a.n_sems)])
            fb(ins[na:], outs[la:], ssem.at[pl.ds(a.n_sems, b.n_sems)], rsem.at[pl.ds(a.n_sems, b.n_sems)])
        return run

    aliases = {**a.aliases, **{na + i: la + j for i, j in b.aliases.items()}}
    return Exchange(a.arrays + b.arrays, a.landing + b.landing, aliases, a.n_sems + b.n_sems,
                    both(a.start, b.start), both(a.finish, b.finish), a.sibling_only and b.sibling_only)


def _call(body, args, *, name, grid, in_specs, out_specs, out_shape, scratch_shapes=(), compiler_params=None,
          exchange=None, free=(), after=()):
    args = [a if i in free else pltpu.with_memory_space_constraint(a, pltpu.HBM) for i, a in enumerate(args)]
    if exchange is None:
        n_in = len(in_specs)

        def plain(*refs):
            body(*refs[:n_in], *refs[n_in + len(after):])

        return pl.pallas_call(plain, name=name, grid=grid, in_specs=list(in_specs) + [_ANY] * len(after),
                              out_specs=out_specs, out_shape=out_shape, scratch_shapes=list(scratch_shapes),
                              compiler_params=compiler_params)(*args, *after)
    ex = exchange
    n_in, n_out, n_scr = len(in_specs), len(out_specs), len(scratch_shapes)
    na, nl = len(ex.arrays), len(ex.landing)

    def hosted(*refs):
        at = [0]

        def take(n):
            at[0] += n
            return refs[at[0] - n:at[0]]

        k_in, _, e_in, k_out, e_out, k_scr = take(n_in), take(len(after)), take(na), take(n_out), take(nl), take(n_scr)
        ssem, rsem = take(2)
        ids = [pl.program_id(a) for a in range(len(grid))]
        first = functools.reduce(jnp.logical_and, [i == 0 for i in ids])
        last = functools.reduce(jnp.logical_and, [i == g - 1 for i, g in zip(ids, grid)])

        @pl.when(first)
        def _():
            ex.enter()
            ex.start(e_in, e_out, ssem, rsem)

        body(*k_in, *k_out, *k_scr)

        @pl.when(last)
        def _():
            ex.finish(e_in, e_out, ssem, rsem)

    outs = pl.pallas_call(
        hosted, name=name, grid=grid,
        in_specs=list(in_specs) + [_ANY] * (len(after) + na), out_specs=list(out_specs) + [_ANY] * nl,
        out_shape=list(out_shape) + ex.landing,
        scratch_shapes=list(scratch_shapes) + [pltpu.SemaphoreType.DMA((ex.n_sems,))] * 2,
        input_output_aliases={n_in + len(after) + i: n_out + j for i, j in ex.aliases.items()},
        compiler_params=ex.params(compiler_params),
    )(*args, *after, *_in_hbm(ex.arrays))
    return outs[:n_out], outs[n_out:]


def _exchange_alone(ex, name, after=()):
    na, nl = len(ex.arrays), len(ex.landing)

    def body(*refs):
        outs = refs[na + len(after):na + len(after) + nl]
        ex.enter()
        ex.start(refs[:na], outs, refs[-2], refs[-1])
        ex.finish(refs[:na], outs, refs[-2], refs[-1])

    return pl.pallas_call(
        body, name=name, in_specs=[_ANY] * (na + len(after)), out_specs=[_ANY] * nl,
        out_shape=ex.landing, scratch_shapes=[pltpu.SemaphoreType.DMA((ex.n_sems,))] * 2,
        input_output_aliases=ex.aliases, compiler_params=ex.params(),
    )(*_in_hbm(ex.arrays), *after)


_HBM = pl.BlockSpec(memory_space=pltpu.HBM)
_SEM = pl.BlockSpec(memory_space=pltpu.SEMAPHORE)
_EFFECT = pltpu.SideEffectType.DATAFLOW_SIDE_EFFECTING


def _scatter_copies(srcs, lands, ssems, rsems):
    x, y, c, chips = _place()
    return [_remote(srcs[w].at[2 * px + py], lands[w].at[k], ssems[3 * w + k], rsems[3 * w + k], (px, py, c))
            for w in range(len(srcs)) for k, (px, py) in enumerate(chips)]


def _scatter_start(parts, name):
    parts = list(parts)
    n, ncp = len(parts), 3 * len(parts)
    lands = [lax.empty((3,) + p.shape[1:], p.dtype) for p in parts]

    def body(*refs):
        srcs, land_refs = refs[:n], refs[n:2 * n]
        ssems, rsems = refs[2 * n:2 * n + ncp], refs[2 * n + ncp:2 * n + 2 * ncp]
        for cp in _scatter_copies(srcs, land_refs, ssems, rsems):
            cp.start()
        token = refs[-1]
        token[...] = jnp.zeros_like(token)

    outs = pl.pallas_call(
        body, name=name,
        out_shape=([pltpu.SemaphoreType.DMA(())] * (2 * ncp) + [pltpu.HBM(a.shape, a.dtype) for a in parts + lands]
                   + [jax.ShapeDtypeStruct((8, 128), F32)]),
        in_specs=[_HBM] * (2 * n), out_specs=[_SEM] * (2 * ncp) + [_HBM] * (2 * n) + [_VM],
        input_output_aliases={i: 2 * ncp + i for i in range(2 * n)},
        compiler_params=pltpu.CompilerParams(has_side_effects=_EFFECT),
    )(*_in_hbm(parts), *_in_hbm(lands))
    sems, thru, token = outs[:2 * ncp], outs[2 * ncp:2 * ncp + 2 * n], outs[-1]
    return sems, thru, token


def _scatter_wait(sems, thru, after, name):
    n = len(thru) // 2
    ncp = 3 * n

    def body(*refs):
        srcs, land_refs = refs[:n], refs[n:2 * n]
        ssems, rsems = refs[2 * n:2 * n + ncp], refs[2 * n + ncp:2 * n + 2 * ncp]
        for cp in _scatter_copies(srcs, land_refs, ssems, rsems):
            cp.wait_send()
            cp.wait_recv()

    outs = pl.pallas_call(
        body, name=name, out_shape=[pltpu.HBM(a.shape, a.dtype) for a in thru],
        in_specs=[_HBM] * (2 * n) + [_SEM] * (2 * ncp) + [_ANY] * len(after), out_specs=[_HBM] * (2 * n),
        input_output_aliases={i: i for i in range(2 * n)},
        compiler_params=pltpu.CompilerParams(has_side_effects=_EFFECT),
    )(*thru, *sems, *after)
    return outs[:n], outs[n:]


def _gather_copies(bufs, ssems, rsems, sending):
    x, y, c, chips = _place()
    out = []
    for w, ref in enumerate(bufs):
        half = ref.shape[1] // 2
        for k, (px, py) in enumerate(chips):
            rows = ref.at[2 * x + y if sending else 2 * px + py, pl.ds(c * half, half)]
            out.append(_remote(rows, rows, ssems[3 * w + k], rsems[3 * w + k], (px, py, c)))
    return out


def _gather_start(bufs, after, name):
    n, ncp = len(bufs), 3 * len(bufs)

    def body(*refs):
        ssems, rsems = refs[n + len(after):n + len(after) + ncp], refs[n + len(after) + ncp:n + len(after) + 2 * ncp]
        for cp in _gather_copies(refs[:n], ssems, rsems, True):
            cp.start()
        token = refs[-1]
        token[...] = jnp.zeros_like(token)

    outs = pl.pallas_call(
        body, name=name,
        out_shape=([pltpu.SemaphoreType.DMA(())] * (2 * ncp) + [pltpu.HBM(a.shape, a.dtype) for a in bufs]
                   + [jax.ShapeDtypeStruct((8, 128), F32)]),
        in_specs=[_HBM] * n + [_ANY] * len(after), out_specs=[_SEM] * (2 * ncp) + [_HBM] * n + [_VM],
        input_output_aliases={i: 2 * ncp + i for i in range(n)},
        compiler_params=pltpu.CompilerParams(has_side_effects=_EFFECT),
    )(*_in_hbm(bufs), *after)
    return outs[:2 * ncp], outs[2 * ncp:2 * ncp + n], outs[-1]


def _gather_wait(sems, thru, after, name):
    n = len(thru)
    ncp = 3 * n

    def body(*refs):
        ssems, rsems = refs[n:n + ncp], refs[n + ncp:n + 2 * ncp]
        for cp in _gather_copies(refs[:n], ssems, rsems, True):
            cp.wait_send()
        for cp in _gather_copies(refs[:n], ssems, rsems, False):
            cp.wait_recv()

    return pl.pallas_call(
        body, name=name, out_shape=[pltpu.HBM(a.shape, a.dtype) for a in thru],
        in_specs=[_HBM] * n + [_SEM] * (2 * ncp) + [_ANY] * len(after), out_specs=[_HBM] * n,
        input_output_aliases={i: i for i in range(n)},
        compiler_params=pltpu.CompilerParams(has_side_effects=_EFFECT),
    )(*thru, *sems, *after)


def _rms(x):
    r = lax.rsqrt(jnp.mean(x * x, axis=-1, keepdims=True) + EPS)
    return r, x * r


def _rms_bwd(dn, xr, r, gain):
    dng = dn * gain
    dx = r * (dng - xr * jnp.mean(dng * xr, axis=-1, keepdims=True))
    return dx, jnp.sum(dn * xr, axis=0, keepdims=True)


def _ffn_fwd(x, gain, wgu, wd, name, exchange=None, head=None):
    tm = 256

    def body(x_ref, g_ref, wgu_hbm, wd_hbm, *rest):
        if head is None:
            h_ref, n_ref, gu_ref, a_ref, wgu_ref, wd_ref = rest
        else:
            t_ref, gf_ref, h_ref, loss_ref, dgf_ref, n_ref, gu_ref, a_ref, wgu_ref, wd_ref = rest
        _stage([(wgu_hbm, wgu_ref), (wd_hbm, wd_ref)])
        x = x_ref[...]
        _, xr = _rms(x)
        n = (xr * g_ref[...]).astype(BF16)
        n_ref[...] = n
        acc = jnp.zeros((tm, D), F32)
        for j in range(2):
            g = _nn(n, wgu_ref[j])
            u = _nn(n, wgu_ref[2 + j])
            gu_ref[:, j * FFS:(j + 1) * FFS] = g.astype(BF16)
            gu_ref[:, (2 + j) * FFS:(3 + j) * FFS] = u.astype(BF16)
            half_act = (0.5 * (g * jax.nn.sigmoid(g) * u)).astype(BF16)
            a_ref[:, j * FFS:(j + 1) * FFS] = half_act
            acc = acc + _nn(half_act, wd_ref[j * FFS:(j + 1) * FFS, :])
        h = x + acc
        if head is None:
            h_ref[...] = h
            return
        gf = gf_ref[...]
        r, hr = _rms(h)
        err = hr * gf - t_ref[...]
        dh, dgain = _rms_bwd(err * (1.0 / D), hr, r, gf)
        h_ref[...] = dh

        @pl.when(pl.program_id(0) == 0)
        def _():
            dgf_ref[...] = jnp.zeros_like(dgf_ref)
            loss_ref[...] = jnp.zeros_like(loss_ref)

        dgf_ref[...] += dgain
        loss_ref[...] += jnp.full((1, 128), (0.5 / D) * jnp.sum(err * err), F32)

    saved_specs = [_rows(tm, D), _rows(tm, 4 * FFS), _rows(tm, DFF)]
    saved_shapes = [_sds((S, D), BF16), _sds((S, 4 * FFS), BF16), _sds((S, DFF), BF16)]
    if head is None:
        return _call(
            body, (x, gain, wgu, wd), name=name, grid=(S // tm,),
            in_specs=[_rows(tm, D), _fixed((1, D)), _ANY, _ANY],
            out_specs=[_rows(tm, D)] + saved_specs, out_shape=[_sds((S, D), F32)] + saved_shapes,
            scratch_shapes=_vmem_like(wgu, wd),
            compiler_params=_params(("arbitrary",), 56), exchange=exchange)
    return _call(
        body, (x, gain, wgu, wd, *head), name=name, grid=(S // tm,),
        in_specs=[_rows(tm, D), _fixed((1, D)), _ANY, _ANY, _rows(tm, D), _fixed((1, D))],
        out_specs=[_rows(tm, D), _fixed((1, 128)), _fixed((1, D))] + saved_specs,
        out_shape=[_sds((S, D), F32), _sds((1, 128), F32), _sds((1, D), F32)] + saved_shapes,
        scratch_shapes=_vmem_like(wgu, wd),
        compiler_params=_params(("arbitrary",), 56), exchange=exchange, free=(4, 5))


def _ffn_bwd(dh, x, gain, gu, wgu, wd, name):
    tm = 256

    def body(dh_ref, x_ref, g_ref, gu_ref, wgu_hbm, wd_hbm, dx_ref, dgu_ref, dg_ref, wgu_ref, wd_ref):
        _stage([(wgu_hbm, wgu_ref), (wd_hbm, wd_ref)])
        dh = dh_ref[...]
        dhb = dh.astype(BF16)
        dn = jnp.zeros((tm, D), F32)
        for j in range(2):
            g = gu_ref[:, j * FFS:(j + 1) * FFS].astype(F32)
            u = gu_ref[:, (2 + j) * FFS:(3 + j) * FFS].astype(F32)
            da = 0.5 * _nt(dhb, wd_ref[j * FFS:(j + 1) * FFS, :])
            sg = jax.nn.sigmoid(g)
            dgb = (da * u * (sg * (1.0 + g * (1.0 - sg)))).astype(BF16)
            dub = (da * (g * sg)).astype(BF16)
            dgu_ref[:, j * FFS:(j + 1) * FFS] = dgb
            dgu_ref[:, (2 + j) * FFS:(3 + j) * FFS] = dub
            dn = dn + _nt(dgb, wgu_ref[j]) + _nt(dub, wgu_ref[2 + j])
        r, xr = _rms(x_ref[...])
        dx, dgain = _rms_bwd(dn, xr, r, g_ref[...])
        dx_ref[...] = dh + dx

        @pl.when(pl.program_id(0) == 0)
        def _():
            dg_ref[...] = jnp.zeros_like(dg_ref)

        dg_ref[...] += dgain

    return _call(
        body, (dh, x, gain, gu, wgu, wd), name=name, grid=(S // tm,),
        in_specs=[_rows(tm, D), _rows(tm, D), _fixed((1, D)), _rows(tm, 4 * FFS), _ANY, _ANY],
        out_specs=[_rows(tm, D), _rows(tm, 4 * FFS), _fixed((1, D))],
        out_shape=[_sds((S, D), F32), _sds((S, 4 * FFS), BF16), _sds((1, D), F32)],
        scratch_shapes=_vmem_like(wgu, wd), compiler_params=_params(("arbitrary",), 56))


def _ffn_bwd_act(dh, gu, wd, name, exchange=None, after=()):
    tm = 512

    def body(dh_ref, gu_ref, wd_hbm, dgu_ref, wd_ref):
        _stage([(wd_hbm, wd_ref)])
        dhb = dh_ref[...].astype(BF16)
        for j in range(2):
            g = gu_ref[:, j * FFS:(j + 1) * FFS].astype(F32)
            u = gu_ref[:, (2 + j) * FFS:(3 + j) * FFS].astype(F32)
            da = 0.5 * _nt(dhb, wd_ref[j * FFS:(j + 1) * FFS, :])
            sg = jax.nn.sigmoid(g)
            dgu_ref[:, j * FFS:(j + 1) * FFS] = (da * u * (sg * (1.0 + g * (1.0 - sg)))).astype(BF16)
            dgu_ref[:, (2 + j) * FFS:(3 + j) * FFS] = (da * (g * sg)).astype(BF16)

    res = _call(
        body, (dh, gu, wd), name=name, grid=(S // tm,),
        in_specs=[_rows(tm, D), _rows(tm, 4 * FFS), _ANY], out_specs=[_rows(tm, 4 * FFS)],
        out_shape=[_sds((S, 4 * FFS), BF16)], scratch_shapes=_vmem_like(wd),
        compiler_params=_params(("arbitrary",), 56), exchange=exchange, after=after)
    return res[0] if exchange is None else (res[0][0], res[1])


def _ffn_bwd_in(dh, x, gain, dgu, wgu, name, exchange=None, after=()):
    tm = 512

    def body(dh_ref, x_ref, g_ref, dgu_ref, wgu_hbm, dx_ref, dg_ref, wgu_ref):
        _stage([(wgu_hbm, wgu_ref)])
        dn = jnp.zeros((tm, D), F32)
        for j in range(NSH):
            dn = dn + _nt(dgu_ref[:, j * FFS:(j + 1) * FFS], wgu_ref[j])
        r, xr = _rms(x_ref[...])
        dx, dgain = _rms_bwd(dn, xr, r, g_ref[...])
        dx_ref[...] = dh_ref[...] + dx

        @pl.when(pl.program_id(0) == 0)
        def _():
            dg_ref[...] = jnp.zeros_like(dg_ref)

        dg_ref[...] += dgain

    return _call(
        body, (dh, x, gain, dgu, wgu), name=name, grid=(S // tm,),
        in_specs=[_rows(tm, D), _rows(tm, D), _fixed((1, D)), _rows(tm, 4 * FFS), _ANY],
        out_specs=[_rows(tm, D), _fixed((1, D))],
        out_shape=[_sds((S, D), F32), _sds((1, D), F32)],
        scratch_shapes=_vmem_like(wgu),
        compiler_params=_params(("arbitrary",), 56), exchange=exchange, after=after)


def _mix_in(h, gain, w_in, after=()):
    tm = 512

    def body(h_ref, g_ref, w_hbm, u_ref, xp_ref, q_ref, k_ref, v_ref, gp_ref, gs_ref, w_ref):
        _stage([(w_hbm, w_ref)])
        _, hr = _rms(h_ref[...])
        u = (hr * g_ref[...]).astype(BF16)
        u_ref[...] = u
        p0 = _nn(u, w_ref[0])
        xp_ref[...] = p0[:, :PW]
        q_ref[...] = p0[:, PW:].astype(BF16)
        p1 = _nn(u, w_ref[1])
        k_ref[...] = p1[:, :SBW].astype(BF16)
        v_ref[...] = p1[:, SBW:].astype(BF16)
        gp_ref[...] = jax.nn.sigmoid(_nn(u, w_ref[2])).astype(BF16)
        gs_ref[...] = jax.nn.sigmoid(_nn(u, w_ref[3])).astype(BF16)

    return _call(
        body, (h, gain, w_in), name="mix_in", grid=(S // tm,),
        in_specs=[_rows(tm, D), _fixed((1, D)), _ANY],
        out_specs=[_rows(tm, D), _rows(tm, PW), _rows(tm, SBW), _rows(tm, SBW), _rows(tm, SBW),
                   _rows(tm, D), _rows(tm, D)],
        out_shape=[_sds((S, D), BF16), _sds((S, PW), F32), _sds((S, SBW), BF16), _sds((S, SBW), BF16),
                   _sds((S, SBW), BF16), _sds((S, D), BF16), _sds((S, D), BF16)],
        scratch_shapes=_vmem_like(w_in),
        compiler_params=_params(("arbitrary",), 48), free=(1,), after=after)


def _hilo_dot(x, tri):
    hi = x.astype(BF16)
    lo = (x - hi.astype(F32)).astype(BF16)
    return _nn(hi, tri) + _nn(lo, tri)


def _log_terms(qk):
    z2 = qk * (SCALE * LOG2E)
    lb = jnp.minimum(z2, 0.0) - jnp.log2(1.0 + jnp.exp2(-jnp.abs(z2)))
    return lb, lb - z2


def _head_masks():
    lane = lax.broadcasted_iota(jnp.int32, (1, 2 * DH), 1)
    return (lane < DH, lane >= DH)


def _attn_fwd(q, k, v, exchange=None):
    T = TA

    def body(q_ref, k_ref, v_ref, o_ref, c_ref):
        i2 = 2 * pl.program_id(1)
        row = lax.broadcasted_iota(jnp.int32, (T, T), 0)
        col = lax.broadcasted_iota(jnp.int32, (T, T), 1)
        after = (row > col).astype(BF16)
        causal = col < row
        masks = _head_masks()
        qms = {}
        for b in range(QB):
            q2 = q_ref[b * T:(b + 1) * T, :]
            for h, hm in enumerate(masks):
                qms[b, h] = jnp.where(hm, q2, jnp.zeros_like(q2))

        def blocks(keys, pairs, carries, os):
            ks, vms = [], []
            for j in keys:
                rows = pl.ds(pl.multiple_of(j * T, T), T)
                vj = v_ref[rows, :]
                ks.append(k_ref[rows, :])
                vms.append([jnp.where(hm, vj, jnp.zeros_like(vj)) for hm in masks])
            units = [(n, h) for n in range(len(pairs)) for h in range(2)]
            qks = {(n, h): _nt(qms[pairs[n][0], h], ks[pairs[n][1]]) for n, h in units}
            lbs, l1ms = {}, {}
            for u in units:
                lbs[u], l1m = _log_terms(qks[u])
                l1ms[u] = jnp.where(causal, l1m, 0.0) if pairs[u[0]][2] else l1m
            cins = {u: _hilo_dot(l1ms[u], after) for u in units}
            carries, os = dict(carries), list(os)
            for n, h in units:
                b, key, diag = pairs[n]
                a = jnp.exp2(lbs[n, h] + cins[n, h] + carries[b, h])
                if diag:
                    a = jnp.where(causal, a, 0.0)
                os[b] = os[b] + _nn(a.astype(BF16), vms[key][h])
                carries[b, h] = carries[b, h] + jnp.sum(l1ms[n, h], axis=1, keepdims=True)
            return carries, tuple(os)

        carries = {(b, h): jnp.zeros((T, 1), F32) for b in range(QB) for h in range(2)}
        os = tuple(jnp.zeros((T, 2 * DH), F32) for _ in range(QB))
        carries, os = blocks([i2 + 1, i2], [(1, 0, True), (0, 1, True), (1, 1, False)], carries, os)
        carries, os = lax.fori_loop(
            0, i2 // 2,
            lambda t, c: blocks([i2 - 1 - 2 * t, i2 - 2 - 2 * t],
                                [(0, 0, False), (1, 0, False), (0, 1, False), (1, 1, False)], c[0], c[1]),
            (carries, os))
        for b in range(QB):
            o_ref[b * T:(b + 1) * T, :] = os[b].astype(BF16)
            c_ref[b * T:(b + 1) * T, :] = jnp.where(masks[0], carries[b, 0], carries[b, 1])

    blk = pl.BlockSpec((QB * T, 2 * DH), lambda p, i: (i, p))
    full = pl.BlockSpec((S, 2 * DH), lambda p, i: (0, p))
    return _call(
        body, (q, k, v), name="attn_fwd", grid=(SBW // (2 * DH), S // (QB * T)),
        in_specs=[blk, full, full], out_specs=[blk, blk],
        out_shape=[_sds((S, SBW), BF16), _sds((S, SBW), F32)],
        compiler_params=_params(("arbitrary", "arbitrary"), 40), exchange=exchange)


def _attn_bwd(q, k, v, do, ctot, after=()):
    T = TA
    nq = S // (QB * T)

    def body(q_ref, k_ref, v_ref, do_ref, c_ref, dq_ref, dk_ref, dv_ref, dk_acc, dv_acc):
        step = pl.program_id(1)
        i2 = 2 * step

        @pl.when(step == 0)
        def _():
            dk_acc[...] = jnp.zeros_like(dk_acc)
            dv_acc[...] = jnp.zeros_like(dv_acc)

        row = lax.broadcasted_iota(jnp.int32, (T, T), 0)
        col = lax.broadcasted_iota(jnp.int32, (T, T), 1)
        upto = (row <= col).astype(BF16)
        before = (row < col).astype(BF16)
        causal = col < row
        masks = _head_masks()
        qms, doms, ctots = {}, {}, {}
        for b in range(QB):
            q2, do2 = q_ref[b * T:(b + 1) * T, :], do_ref[b * T:(b + 1) * T, :]
            for h, hm in enumerate(masks):
                qms[b, h] = jnp.where(hm, q2, jnp.zeros_like(q2))
                doms[b, h] = jnp.where(hm, do2, jnp.zeros_like(do2))
                ctots[b, h] = c_ref[b * T:(b + 1) * T, h * DH:h * DH + 1]

        def blocks(keys, pairs, sums, dqs):
            rows = [pl.ds(pl.multiple_of(j * T, T), T) for j in keys]
            ks, vs = [k_ref[r, :] for r in rows], [v_ref[r, :] for r in rows]
            kms = [[jnp.where(hm, kj, jnp.zeros_like(kj)) for hm in masks] for kj in ks]
            units = [(n, h) for n in range(len(pairs)) for h in range(2)]
            qks = {(n, h): _nt(qms[pairs[n][0], h], ks[pairs[n][1]]) for n, h in units}
            das = {(n, h): _nt(doms[pairs[n][0], h], vs[pairs[n][1]]) for n, h in units}
            lbs, l1ms = {}, {}
            for u in units:
                lbs[u], l1m = _log_terms(qks[u])
                l1ms[u] = jnp.where(causal, l1m, 0.0) if pairs[u[0]][2] else l1m
            pins = {u: _hilo_dot(l1ms[u], upto) for u in units}
            sums = dict(sums)
            a_s, dls, cps = {}, {}, {}
            for n, h in units:
                b, _, diag = pairs[n]
                cl, cp = sums[b, h]
                a = jnp.exp2(lbs[n, h] + (ctots[b, h] - cl) - pins[n, h])
                if diag:
                    a = jnp.where(causal, a, 0.0)
                a_s[n, h] = a.astype(BF16)
                dls[n, h] = das[n, h] * a
                cps[n, h] = cp
                sums[b, h] = (cl + jnp.sum(l1ms[n, h], axis=1, keepdims=True),
                              cp + jnp.sum(dls[n, h], axis=1, keepdims=True))
            pexs = {u: _hilo_dot(dls[u], before) for u in units}
            dzbs = {}
            for u in units:
                dz = dls[u] - jnp.exp2(lbs[u]) * (dls[u] + pexs[u] + cps[u])
                if pairs[u[0]][2]:
                    dz = jnp.where(causal, dz, 0.0)
                dzbs[u] = dz.astype(BF16)
            dqs = list(dqs)
            for n, h in units:
                dqs[pairs[n][0]] = dqs[pairs[n][0]] + _nn(dzbs[n, h], kms[pairs[n][1]][h])
            for key, r in enumerate(rows):
                mine = [(n, h) for n, h in units if pairs[n][1] == key]
                dk_acc[r, :] += functools.reduce(jnp.add, [_tn(dzbs[u], qms[pairs[u[0]][0], u[1]]) for u in mine])
                dv_acc[r, :] += functools.reduce(jnp.add, [_tn(a_s[u], doms[pairs[u[0]][0], u[1]]) for u in mine])
            return sums, tuple(dqs)

        zero = jnp.zeros((T, 1), F32)
        sums = {(b, h): (zero, zero) for b in range(QB) for h in range(2)}
        dqs = tuple(jnp.zeros((T, 2 * DH), F32) for _ in range(QB))
        sums, dqs = lax.fori_loop(
            0, i2 // 2,
            lambda t, c: blocks([2 * t, 2 * t + 1],
                                [(0, 0, False), (1, 0, False), (0, 1, False), (1, 1, False)], c[0], c[1]),
            (sums, dqs))
        _, dqs = blocks([i2, i2 + 1], [(0, 0, True), (1, 0, False), (1, 1, True)], sums, dqs)
        for b in range(QB):
            dq_ref[b * T:(b + 1) * T, :] = (dqs[b] * SCALE).astype(BF16)

        @pl.when(step == nq - 1)
        def _():
            dk_ref[...] = (dk_acc[...] * SCALE).astype(BF16)
            dv_ref[...] = dv_acc[...].astype(BF16)

    blk = pl.BlockSpec((QB * T, 2 * DH), lambda p, i: (i, p))
    full = pl.BlockSpec((S, 2 * DH), lambda p, i: (0, p))
    return _call(
        body, (q, k, v, do, ctot), name="attn_bwd", grid=(SBW // (2 * DH), nq),
        in_specs=[blk, full, full, blk, blk], out_specs=[blk, full, full],
        out_shape=[_sds((S, SBW), BF16), _sds((S, SBW), BF16), _sds((S, SBW), BF16)],
        scratch_shapes=[pltpu.VMEM((S, 2 * DH), F32), pltpu.VMEM((S, 2 * DH), F32)],
        compiler_params=_params(("arbitrary", "arbitrary"), 40), after=after)


def _pool_counts(first_row, tm):
    pos = first_row + lax.broadcasted_iota(jnp.int32, (tm, 1), 0)
    return [jnp.minimum(pos + 1, w).astype(F32) for w in POOL_WINDOWS]


def _mix_out(h, xp, o_sb, gp, gs, w_group, scale, w_bp, w_ba, w_out, exchange=None):
    tm = 512

    def body(h_ref, xp_ref, o_ref, gp_ref, gs_ref, wg_hbm, sc_ref, wbp_hbm, wba_hbm, wo_hbm,
             h2_ref, pm_ref, p_ref, yp_ref, ys_ref, m_ref, halo, wg_ref, wbp_ref, wba_ref, wo_ref):
        _stage([(wg_hbm, wg_ref), (wbp_hbm, wbp_ref), (wba_hbm, wba_ref), (wo_hbm, wo_ref)])
        i = pl.program_id(0)

        @pl.when(i == 0)
        def _():
            halo[...] = jnp.zeros_like(halo)

        xp = xp_ref[...]
        ext = jnp.concatenate([halo[...], xp], axis=0)
        halo[...] = xp[tm - HALO:, :]
        counts = _pool_counts(i * tm, tm)
        for gi in range(len(POOL_WINDOWS)):
            lanes = slice(gi * PG, (gi + 1) * PG)
            win = ext[:, lanes]
            for step in range(gi + 1):
                win = win + pltpu.roll(win, 1 << step, 0)
            pm = (win[HALO:, :] / counts[gi] - xp[:, lanes]).astype(BF16)
            pm_ref[:, lanes] = pm
            p_ref[:, lanes] = (_nn(pm, wg_ref[gi]) * sc_ref[:, lanes]).astype(BF16)
        pb = p_ref[...]
        ob = o_ref[...]
        for j in range(NSH):
            cols = slice(j * (D // NSH), (j + 1) * (D // NSH))
            yp = _nn(pb, wbp_ref[j])
            ys = _nn(ob, wba_ref[j])
            yp_ref[:, cols] = yp.astype(BF16)
            ys_ref[:, cols] = ys.astype(BF16)
            m_ref[:, cols] = (gp_ref[:, cols].astype(F32) * yp + gs_ref[:, cols].astype(F32) * ys).astype(BF16)
        h2_ref[...] = h_ref[...] + _nn(m_ref[...], wo_ref[...])

    return _call(
        body, (h, xp, o_sb, gp, gs, w_group, scale, w_bp, w_ba, w_out), name="mix_out", grid=(S // tm,),
        in_specs=[_rows(tm, D), _rows(tm, PW), _rows(tm, SBW), _rows(tm, D), _rows(tm, D),
                  _ANY, _fixed((1, PW)), _ANY, _ANY, _ANY],
        out_specs=[_rows(tm, D), _rows(tm, PW), _rows(tm, PW), _rows(tm, D), _rows(tm, D), _rows(tm, D)],
        out_shape=[_sds((S, D), F32), _sds((S, PW), BF16), _sds((S, PW), BF16), _sds((S, D), BF16),
                   _sds((S, D), BF16), _sds((S, D), BF16)],
        scratch_shapes=[pltpu.VMEM((HALO, PW), F32)] + _vmem_like(w_group, w_bp, w_ba, w_out),
        compiler_params=_params(("arbitrary",), 48), free=(5, 6), exchange=exchange)


def _mix_bwd_out(dh, gp, gs, yp, ys, pm, w_group, scale, w_bp, w_ba, w_out, exchange=None):
    tm = 512
    nt = S // tm

    def body(dh_ref, gp_ref, gs_ref, yp_ref, ys_ref, pm_ref, wg_hbm, sc_ref, wbp_hbm, wba_hbm, wo_hbm,
             dlg_ref, dyp_ref, dys_ref, do_ref, dyg_ref, dxp_ref, dsc_ref, halo, wg_ref, wbp_ref, wba_ref, wo_ref):
        _stage([(wg_hbm, wg_ref), (wbp_hbm, wbp_ref), (wba_hbm, wba_ref), (wo_hbm, wo_ref)])
        step = pl.program_id(0)

        @pl.when(step == 0)
        def _():
            halo[...] = jnp.zeros_like(halo)
            dsc_ref[...] = jnp.zeros_like(dsc_ref)

        dm = _nt(dh_ref[...].astype(BF16), wo_ref[...])
        gp = gp_ref[...].astype(F32)
        gs = gs_ref[...].astype(F32)
        yp = yp_ref[...].astype(F32)
        ys = ys_ref[...].astype(F32)
        dlg_ref[:, :D] = (dm * yp * gp * (1.0 - gp)).astype(BF16)
        dlg_ref[:, D:] = (dm * ys * gs * (1.0 - gs)).astype(BF16)
        dyp_ref[...] = (dm * gp).astype(BF16)
        dys_ref[...] = (dm * gs).astype(BF16)
        dp = jnp.zeros((tm, PW), F32)
        do = jnp.zeros((tm, SBW), F32)
        for j in range(NSH):
            cols = slice(j * (D // NSH), (j + 1) * (D // NSH))
            dp = dp + _nt(dyp_ref[:, cols], wbp_ref[j])
            do = do + _nt(dys_ref[:, cols], wba_ref[j])
        do_ref[...] = do.astype(BF16)
        counts = _pool_counts((nt - 1 - step) * tm, tm)
        dscale = []
        for gi in range(len(POOL_WINDOWS)):
            lanes = slice(gi * PG, (gi + 1) * PG)
            dpg = dp[:, lanes]
            dscale.append(jnp.sum(dpg * _nn(pm_ref[:, lanes], wg_ref[gi]), axis=0, keepdims=True))
            dyg = (dpg * sc_ref[:, lanes]).astype(BF16)
            dyg_ref[:, lanes] = dyg
            dpm = _nt(dyg, wg_ref[gi])
            per = dpm / counts[gi]
            win = jnp.concatenate([per, halo[:, lanes]], axis=0)
            halo[:, lanes] = per[:HALO, :]
            for s in range(gi + 1):
                win = win + pltpu.roll(win, tm + HALO - (1 << s), 0)
            dxp_ref[:, lanes] = (win[:tm, :] - dpm).astype(BF16)
        dsc_ref[...] += jnp.concatenate(dscale, axis=1)

    rev = lambda width: pl.BlockSpec((tm, width), lambda i: (nt - 1 - i, 0))
    return _call(
        body, (dh, gp, gs, yp, ys, pm, w_group, scale, w_bp, w_ba, w_out), name="mix_bwd_out", grid=(nt,),
        in_specs=[rev(D), rev(D), rev(D), rev(D), rev(D), rev(PW), _ANY, _fixed((1, PW)), _ANY, _ANY, _ANY],
        out_specs=[rev(2 * D), rev(D), rev(D), rev(SBW), rev(PW), rev(PW), _fixed((1, PW))],
        out_shape=[_sds((S, 2 * D), BF16), _sds((S, D), BF16), _sds((S, D), BF16), _sds((S, SBW), BF16),
                   _sds((S, PW), BF16), _sds((S, PW), BF16), _sds((1, PW), F32)],
        scratch_shapes=[pltpu.VMEM((HALO, PW), F32)] + _vmem_like(w_group, w_bp, w_ba, w_out),
        compiler_params=_params(("arbitrary",), 48), exchange=exchange)


def _mix_bwd_in(dh, h, gain, pieces, w_in, exchange=None):
    tm = 512
    widths = [p.shape[1] for p in pieces]

    def body(dh_ref, h_ref, g_ref, *rest):
        piece_refs, (w_hbm, dx_ref, dg_ref, w_ref, dp_ref) = rest[:len(pieces)], rest[len(pieces):]
        _stage([(w_hbm, w_ref)])
        at = 0
        for ref, width in zip(piece_refs, widths):
            dp_ref[:, at:at + width] = ref[...]
            at += width
        du = jnp.zeros((tm, D), F32)
        for j in range(NSH):
            du = du + _nt(dp_ref[:, j * D:(j + 1) * D], w_ref[j])
        r, hr = _rms(h_ref[...])
        dx, dgain = _rms_bwd(du, hr, r, g_ref[...])
        dx_ref[...] = dh_ref[...] + dx

        @pl.when(pl.program_id(0) == 0)
        def _():
            dg_ref[...] = jnp.zeros_like(dg_ref)

        dg_ref[...] += dgain

    return _call(
        body, (dh, h, gain, *pieces, w_in), name="mix_bwd_in", grid=(S // tm,),
        in_specs=[_rows(tm, D), _rows(tm, D), _fixed((1, D))] + [_rows(tm, w) for w in widths] + [_ANY],
        out_specs=[_rows(tm, D), _fixed((1, D))],
        out_shape=[_sds((S, D), F32), _sds((1, D), F32)],
        scratch_shapes=_vmem_like(w_in) + [pltpu.VMEM((tm, 4 * D), BF16)],
        compiler_params=_params(("arbitrary",), 48), exchange=exchange)


def _wgrad_in(u, pieces):
    dxp, dq, dk, dv, dlg = pieces

    def body(u_ref, dxp_ref, dq_ref, dk_ref, dv_ref, dlg_ref, o_ref):
        j = pl.program_id(0)
        u = u_ref[...]

        def two(left_ref, right_ref):
            o_ref[:, :PW] = _tn(u, left_ref[...]).astype(BF16)
            o_ref[:, PW:] = _tn(u, right_ref[...]).astype(BF16)

        pl.when(j == 0)(lambda: two(dxp_ref, dq_ref))
        pl.when(j == 1)(lambda: two(dk_ref, dv_ref))

        @pl.when(j >= 2)
        def _():
            o_ref[...] = _tn(u, dlg_ref[...]).astype(BF16)

    whole = lambda width: pl.BlockSpec((S, width), lambda j: (0, 0))
    return _call(
        body, (u, dxp, dq, dk, dv, dlg), name="wgrad_in", grid=(NSH,),
        in_specs=[whole(D), whole(PW), whole(SBW), whole(SBW), whole(SBW),
                  pl.BlockSpec((S, D), lambda j: (0, jnp.maximum(j - 2, 0)))],
        out_specs=[pl.BlockSpec((None, D, D), lambda j: (j, 0, 0))], out_shape=[_sds((NSH, D, D), BF16)],
        compiler_params=_params(("arbitrary",), 56))[0]


def _wgrad(a, b, nblk, ti, name, out_dtype=BF16, exchange=None, after=()):
    ka, n = a.shape[1], b.shape[1]
    ns = n // nblk

    def body(a_ref, b_ref, o_ref):
        o_ref[...] = _tn(a_ref[...].astype(BF16), b_ref[...].astype(BF16)).astype(out_dtype)

    res = _call(
        body, (a, b), name=name, grid=(nblk, ka // ti),
        in_specs=[pl.BlockSpec((S, ti), lambda j, i: (0, i)), pl.BlockSpec((S, ns), lambda j, i: (0, j))],
        out_specs=[pl.BlockSpec((None, ti, ns), lambda j, i: (j, i, 0))],
        out_shape=[_sds((nblk, ka, ns), out_dtype)],
        compiler_params=_params(("arbitrary", "arbitrary"), 56), exchange=exchange, after=after)
    return res[0] if exchange is None else (res[0][0], res[1])


def _wgrad_branches(p, dyp, o_sb, dys, pm, dyg):
    cols = D // NSH

    def body(p_ref, dyp_ref, o_ref, dys_ref, pm_ref, dyg_ref, gbp_ref, gba_ref, gg_ref):
        gbp_ref[...] = _tn(p_ref[...], dyp_ref[...]).astype(BF16)
        gba_ref[...] = _tn(o_ref[...], dys_ref[...]).astype(BF16)
        gg_ref[...] = _tn(pm_ref[...], dyg_ref[...])

    whole = lambda width: pl.BlockSpec((S, width), lambda j: (0, 0))
    col = lambda width: pl.BlockSpec((S, width), lambda j: (0, j))
    return _call(
        body, (p, dyp, o_sb, dys, pm, dyg), name="wgrad_branches", grid=(NSH,),
        in_specs=[whole(PW), col(cols), whole(SBW), col(cols), col(PG), col(PG)],
        out_specs=[pl.BlockSpec((None, PW, cols), lambda j: (j, 0, 0)),
                   pl.BlockSpec((None, SBW, cols), lambda j: (j, 0, 0)),
                   pl.BlockSpec((None, PG, PG), lambda j: (j, 0, 0))],
        out_shape=[_sds((NSH, PW, cols), BF16), _sds((NSH, SBW, cols), BF16), _sds((NSH, PG, PG), F32)],
        compiler_params=_params(("arbitrary",), 40))


def _place():
    x, y, c = lax.axis_index("x"), lax.axis_index("y"), lax.axis_index("c")
    chips = [(1 - x, y), (x, 1 - y), (1 - x, 1 - y)]
    return x, y, c, chips


def _remote(src, dst, ssem, rsem, dev):
    return pltpu.make_async_remote_copy(src_ref=src, dst_ref=dst, send_sem=ssem, recv_sem=rsem,
                                        device_id=dev, device_id_type=MESH)


def _cast_into_block(ws, me_idx, name):
    steps = 4
    shapes = [(w.shape[0] // steps, w.shape[1]) for w in ws]

    def body(me_ref, *refs):
        for w_ref, o_ref in zip(refs[:len(ws)], refs[len(ws):]):
            o_ref[...] = w_ref[...].astype(BF16)

    return pl.pallas_call(
        body, name=name, out_shape=[_sds((NSH,) + w.shape, BF16) for w in ws],
        grid_spec=pltpu.PrefetchScalarGridSpec(
            num_scalar_prefetch=1, grid=(steps,),
            in_specs=[pl.BlockSpec((r, c), lambda s, me: (s, 0)) for r, c in shapes],
            out_specs=[pl.BlockSpec((None, r, c), lambda s, me: (me[0], s, 0)) for r, c in shapes]),
        compiler_params=_params(("arbitrary",), 32),
    )(me_idx, *ws)


def _ex_gather(bufs):
    n = len(bufs)
    per = 8

    def plan(outs, ssem, rsem, w):
        x, y, c, _ = _place()
        sib, nbr_x, nbr_y = (x, y, 1 - c), (1 - x, y, c), (x, 1 - y, c)
        half = outs[w].shape[1] // 2
        quarter = half // 2
        sem = lambda k: (ssem.at[per * w + k], rsem.at[per * w + k])
        rows = lambda blk, start, size: outs[w].at[blk, pl.ds(start, size)]
        mine = rows(2 * x + y, c * half, half)
        from_x = rows(2 * (1 - x) + y, c * half, half)
        from_y = rows(2 * x + (1 - y), c * half, half)
        diag = 2 * (1 - x) + (1 - y)
        pass_y = rows(2 * (1 - x) + y, c * half, quarter)
        pass_x = rows(2 * x + (1 - y), c * half + quarter, quarter)
        diag_0, diag_1 = rows(diag, c * half, quarter), rows(diag, c * half + quarter, quarter)
        first = [_remote(mine, mine, *sem(0), nbr_x), _remote(mine, mine, *sem(1), nbr_y)]
        arrivals = [
            (_remote(from_x, from_x, *sem(0), nbr_x),
             [_remote(pass_y, pass_y, *sem(2), nbr_y), _remote(from_x, from_x, *sem(4), sib)]),
            (_remote(from_y, from_y, *sem(1), nbr_y),
             [_remote(pass_x, pass_x, *sem(3), nbr_x), _remote(from_y, from_y, *sem(5), sib)]),
            (_remote(diag_0, diag_0, *sem(2), nbr_y), [_remote(diag_0, diag_0, *sem(6), sib)]),
            (_remote(diag_1, diag_1, *sem(3), nbr_x), [_remote(diag_1, diag_1, *sem(7), sib)]),
        ]
        other = (1 - c) * half
        from_sibling = [
            _remote(rows(2 * (1 - x) + y, other, half), rows(2 * (1 - x) + y, other, half), *sem(4), sib),
            _remote(rows(2 * x + (1 - y), other, half), rows(2 * x + (1 - y), other, half), *sem(5), sib),
            _remote(rows(diag, other, quarter), rows(diag, other, quarter), *sem(6), sib),
            _remote(rows(diag, other + quarter, quarter), rows(diag, other + quarter, quarter), *sem(7), sib),
        ]
        return first, arrivals, from_sibling

    def start(ins, outs, ssem, rsem):
        x, y, c, _ = _place()
        for w in range(n):
            half = outs[w].shape[1] // 2
            mine = outs[w].at[2 * x + y, pl.ds(c * half, half)]
            _remote(mine, mine, ssem.at[per * w], rsem.at[per * w], (1 - x, y, c)).start()
            _remote(mine, mine, ssem.at[per * w + 1], rsem.at[per * w + 1], (x, 1 - y, c)).start()

    def finish(ins, outs, ssem, rsem):
        plans = [plan(outs, ssem, rsem, w) for w in range(n)]
        started = []
        for direct in (True, False):
            for first, arrivals, _ in plans:
                for arrived, onward in (arrivals[:2] if direct else arrivals[2:]):
                    arrived.wait_recv()
                    for cp in onward:
                        cp.start()
                    started += onward
        for first, _, from_sibling in plans:
            for cp in from_sibling:
                cp.wait_recv()
            started += first
        for cp in started:
            cp.wait_send()

    return Exchange(bufs, [_sds(b.shape, b.dtype) for b in bufs], {w: w for w in range(n)}, per * n, start, finish)


def _ex_gather_direct(bufs):
    n = len(bufs)

    def copies(outs, ssem, rsem, only_first=False):
        x, y, c, chips = _place()
        me, sib = 2 * x + y, (x, y, 1 - c)
        first, relay, last = [], [], []
        for w in range(n):
            half = outs[w].shape[1] // 2
            mine = outs[w].at[me, pl.ds(c * half, half)]
            for k, (px, py) in enumerate(chips):
                sems = (ssem.at[6 * w + k], rsem.at[6 * w + k])
                sib_sems = (ssem.at[6 * w + 3 + k], rsem.at[6 * w + 3 + k])
                first.append(_remote(mine, mine, *sems, (px, py, c)))
                if only_first:
                    continue
                got = outs[w].at[2 * px + py, pl.ds(c * half, half)]
                relay.append((_remote(got, got, *sems, (px, py, c)), _remote(got, got, *sib_sems, sib)))
                theirs = outs[w].at[2 * px + py, pl.ds((1 - c) * half, half)]
                last.append(_remote(theirs, theirs, *sib_sems, sib))
        return first, relay, last

    def start(ins, outs, ssem, rsem):
        for cp in copies(outs, ssem, rsem, only_first=True)[0]:
            cp.start()

    def finish(ins, outs, ssem, rsem):
        first, relay, last = copies(outs, ssem, rsem)
        for arrived, onward in relay:
            arrived.wait_recv()
            onward.start()
        for cp in last:
            cp.wait_recv()
        for cp in first:
            cp.wait_send()
        for _, onward in relay:
            onward.wait_send()

    return Exchange(bufs, [_sds(b.shape, b.dtype) for b in bufs], {w: w for w in range(n)}, 6 * n, start, finish)


def _simple_exchange(arrays, landing, aliases, make_copies, sibling_only=False):
    def start(ins, outs, ssem, rsem):
        for cp, _ in make_copies(ins, outs, ssem, rsem, False):
            cp.start()

    def finish(ins, outs, ssem, rsem):
        cps = make_copies(ins, outs, ssem, rsem, True)
        for _, landed in cps:
            landed.wait_recv()
        for cp, _ in cps:
            cp.wait_send()

    return Exchange(arrays, landing, aliases, len(arrays) * 3, start, finish, sibling_only)


def _ex_pair_swap(grads):
    def make(ins, outs, ssem, rsem, landing):
        x, y, c, _ = _place()
        cps = [_remote(ins[w].at[:, 1 - c], outs[w], ssem.at[w], rsem.at[w], (x, y, 1 - c))
               for w in range(len(grads))]
        return [(cp, cp) for cp in cps]

    return _simple_exchange(grads, [_sds((NSH,) + g.shape[2:], g.dtype) for g in grads], {}, make, True)


def _ex_relay(bufs):
    def make(ins, outs, ssem, rsem, landing):
        x, y, c, chips = _place()
        sib = (x, y, 1 - c)
        out = []
        for w in range(len(bufs)):
            half = outs[w].shape[1] // 2
            for k, (px, py) in enumerate(chips):
                sems = (ssem.at[3 * w + k], rsem.at[3 * w + k])
                have = outs[w].at[2 * px + py, pl.ds(c * half, half)]
                miss = outs[w].at[2 * px + py, pl.ds((1 - c) * half, half)]
                out.append((_remote(have, have, *sems, sib), _remote(miss, miss, *sems, sib) if landing else None))
        return out

    return _simple_exchange(bufs, [_sds(b.shape, b.dtype) for b in bufs], {w: w for w in range(len(bufs))}, make, True)


def _ex_share(bufs):
    def make(ins, outs, ssem, rsem, landing):
        x, y, c, _ = _place()
        sib = (x, y, 1 - c)
        return [(_remote(outs[w].at[c], outs[w].at[c], ssem.at[w], rsem.at[w], sib),
                 _remote(outs[w].at[1 - c], outs[w].at[1 - c], ssem.at[w], rsem.at[w], sib) if landing else None)
                for w in range(len(bufs))]

    return _simple_exchange(bufs, [_sds(b.shape, b.dtype) for b in bufs], {w: w for w in range(len(bufs))}, make, True)


def _small_copies(slots, ssems, rsems, sending):
    x, y, c, _ = _place()
    out = []
    for m in range(1, 8):
        px, py, pc = x ^ (m >> 2), y ^ ((m >> 1) & 1), c ^ (m & 1)
        slot = slots.at[4 * x + 2 * y + c if sending else 4 * px + 2 * py + pc]
        out.append(_remote(slot, slot, ssems[m - 1], rsems[m - 1], (px, py, pc)))
    return out


def _small_gather_start(slots, name):
    def body(*refs):
        for cp in _small_copies(refs[0], refs[1:8], refs[8:15], True):
            cp.start()
        refs[-1][...] = jnp.zeros_like(refs[-1])

    outs = pl.pallas_call(
        body, name=name,
        out_shape=([pltpu.SemaphoreType.DMA(())] * 14 + [pltpu.HBM(slots.shape, slots.dtype)]
                   + [jax.ShapeDtypeStruct((8, 128), F32)]),
        in_specs=[_HBM], out_specs=[_SEM] * 14 + [_HBM, _VM], input_output_aliases={0: 14},
        compiler_params=pltpu.CompilerParams(has_side_effects=_EFFECT),
    )(*_in_hbm([slots]))
    return outs[:14], outs[14], outs[15]


def _small_gather_wait(sems, slots, after, name):
    def body(*refs):
        for cp in _small_copies(refs[0], refs[1:8], refs[8:15], True):
            cp.wait_send()
        for cp in _small_copies(refs[0], refs[1:8], refs[8:15], False):
            cp.wait_recv()

    return pl.pallas_call(
        body, name=name, out_shape=pltpu.HBM(slots.shape, slots.dtype),
        in_specs=[_HBM] + [_SEM] * 14 + [_ANY] * len(after), out_specs=_HBM, input_output_aliases={0: 0},
        compiler_params=pltpu.CompilerParams(has_side_effects=_EFFECT),
    )(slots, *sems, *after)


def _pair_sum(grads, gots, c_idx, name):
    n = len(grads)

    def body(c_ref, *refs):
        for a_ref, b_ref, o_ref in zip(refs[:n], refs[n:2 * n], refs[2 * n:]):
            o_ref[...] = (a_ref[...].astype(F32) + b_ref[...].astype(F32)).astype(BF16)

    halves = [g.shape[2:] for g in grads]
    return list(pl.pallas_call(
        body, name=name, out_shape=[_sds((NSH,) + h, BF16) for h in halves],
        grid_spec=pltpu.PrefetchScalarGridSpec(
            num_scalar_prefetch=1, grid=(NSH,),
            in_specs=[pl.BlockSpec((None, None) + h, lambda j, c: (j, c[0], 0, 0)) for h in halves]
            + [pl.BlockSpec((None,) + h, lambda j, c: (j, 0, 0)) for h in halves],
            out_specs=[pl.BlockSpec((None,) + h, lambda j, c: (j, 0, 0)) for h in halves]),
        compiler_params=_params(("arbitrary",), 40),
    )(c_idx, *_in_hbm(list(grads) + list(gots))))


def _chip_sum(owns, gots, place, name):
    n = len(owns)

    def body(place_ref, *refs):
        for own_ref, got_ref, o_ref in zip(refs[:n], refs[n:2 * n], refs[2 * n:]):
            acc = own_ref[...].astype(F32)
            for k in range(3):
                acc = acc + got_ref[k].astype(F32)
            o_ref[...] = acc

    shapes = [(o.shape[1] // 2, o.shape[2]) for o in owns]
    return list(pl.pallas_call(
        body, name=name, out_shape=[_sds((2, 2 * r, c), F32) for r, c in shapes],
        grid_spec=pltpu.PrefetchScalarGridSpec(
            num_scalar_prefetch=1, grid=(2,),
            in_specs=[pl.BlockSpec((None, r, c), lambda s, p: (p[0], s, 0)) for r, c in shapes]
            + [pl.BlockSpec((3, r, c), lambda s, p: (0, s, 0)) for r, c in shapes],
            out_specs=[pl.BlockSpec((None, r, c), lambda s, p: (p[1], s, 0)) for r, c in shapes]),
        compiler_params=_params(("arbitrary",), 40),
    )(place, *_in_hbm(list(owns) + list(gots))))


def _adamw_math(w, g, m, v):
    m = B1 * m + (1.0 - B1) * g
    v = B2 * v + (1.0 - B2) * (g * g)
    m_hat = m / (1.0 - B1 ** STEP)
    v_hat = v / (1.0 - B2 ** STEP)
    return -LR * (m_hat / (jnp.sqrt(v_hat) + AEPS) + WD * w), m, v


def _adamw(ws, gs, ms, vs, name, after=()):
    n, steps = len(ws), 4

    def body(*refs):
        ins, outs = refs[:4 * n], refs[4 * n:]
        for i in range(n):
            w_ref, g_ref, m_ref, v_ref = ins[4 * i:4 * i + 4]
            go_ref, d_ref, nm_ref, nv_ref = outs[4 * i:4 * i + 4]
            g = g_ref[...]
            go_ref[...] = g
            d_ref[...], nm_ref[...], nv_ref[...] = _adamw_math(w_ref[...], g, m_ref[...], v_ref[...])

    args, specs, shapes, free = [], [], [], []
    for i, (w, g, m, v) in enumerate(zip(ws, gs, ms, vs)):
        args += [w, g, m, v]
        specs += [pl.BlockSpec((w.shape[0] // steps, w.shape[1]), lambda r: (r, 0))] * 4
        shapes += [_sds(w.shape, F32)] * 4
        free += [4 * i, 4 * i + 2, 4 * i + 3]
    outs = _call(body, args, name=name, grid=(steps,), out_shape=shapes, in_specs=specs, out_specs=specs,
                 compiler_params=_params(("arbitrary",), 48), free=tuple(free), after=after)
    return [outs[4 * i:4 * i + 4] for i in range(n)]


def _small_update(gathered, w, m, v):
    rows = w.shape[0]

    def body(ga_ref, w_ref, m_ref, v_ref, *out_refs):
        g = ga_ref[0:rows, :]
        for dev in range(1, 8):
            g = g + ga_ref[dev * rows:(dev + 1) * rows, :]
        results = (g,) + _adamw_math(w_ref[...], g, m_ref[...], v_ref[...])
        for i, res in enumerate(results):
            out_refs[i][...] = res[:SMALL_HEAD, :]
            out_refs[4 + i][...] = res[SMALL_HEAD:, :]

    outs = pl.pallas_call(
        body, name="small_update",
        out_shape=[jax.ShapeDtypeStruct((SMALL_HEAD, 128), F32)] * 4
        + [jax.ShapeDtypeStruct((rows - SMALL_HEAD, 128), F32)] * 4,
        in_specs=[_VM] * 4, out_specs=[_VM] * 8,
    )(gathered, w, m, v)
    return outs[:4], outs[4:]


SMALL = ("ffn1_norm", "mix_norm", "ffn2_norm", "final_norm", "pool_scale", "loss", "pool_w_group")
SMALL_HEAD = 48
BIG = ("ffn1_w_gate_up", "ffn1_w_down", "w_in", "w_branch_pool", "w_branch_attn", "w_out",
       "ffn2_w_gate_up", "ffn2_w_down")
ORDER = ("ffn1_norm", "ffn1_w_gate_up", "ffn1_w_down", "mix_norm", "w_in", "pool_w_group", "pool_scale",
         "w_branch_pool", "w_branch_attn", "w_out", "ffn2_norm", "ffn2_w_gate_up", "ffn2_w_down", "final_norm")
SMALL_ROWS = 560


def _pack_small(t):
    parts = []
    for k in SMALL:
        rows = t[k].reshape(-1, 128) if k in t else jnp.zeros((1, 128), F32)
        parts.append(jnp.pad(rows, ((0, -rows.shape[0] % 8), (0, 0))))
    packed = jnp.concatenate(parts, axis=0)
    assert packed.shape == (SMALL_ROWS, 128), packed.shape
    return packed


def _unpack_small(head, group, like):
    out, at = {"pool_w_group": group.reshape(like["pool_w_group"].shape)}, 0
    for k in SMALL[:-1]:
        n = like[k].size // 128 if k in like else 1
        out[k] = head[at:at + n].reshape(like[k].shape) if k in like else head[at, 0]
        at += n + (-n % 8)
    return out


def _halves(g):
    return g.reshape(NSH, 2, g.shape[1] // 2, g.shape[2])


def kernel(x, ffn1_norm, ffn1_w_gate_up, ffn1_w_down, mix_norm, w_in, pool_w_group, pool_scale, w_branch_pool, w_branch_attn, w_out, ffn2_norm, ffn2_w_gate_up, ffn2_w_down, final_norm, loss_target, m_ffn1_norm, m_ffn1_w_gate_up, m_ffn1_w_down, m_mix_norm, m_w_in, m_pool_w_group, m_pool_scale, m_w_branch_pool, m_w_branch_attn, m_w_out, m_ffn2_norm, m_ffn2_w_gate_up, m_ffn2_w_down, m_final_norm, v_ffn1_norm, v_ffn1_w_gate_up, v_ffn1_w_down, v_mix_norm, v_w_in, v_pool_w_group, v_pool_scale, v_w_branch_pool, v_w_branch_attn, v_w_out, v_ffn2_norm, v_ffn2_w_gate_up, v_ffn2_w_down, v_final_norm):
    wts = dict(ffn1_norm=ffn1_norm, ffn1_w_gate_up=ffn1_w_gate_up, ffn1_w_down=ffn1_w_down, mix_norm=mix_norm,
               w_in=w_in, pool_w_group=pool_w_group, pool_scale=pool_scale, w_branch_pool=w_branch_pool,
               w_branch_attn=w_branch_attn, w_out=w_out, ffn2_norm=ffn2_norm, ffn2_w_gate_up=ffn2_w_gate_up,
               ffn2_w_down=ffn2_w_down, final_norm=final_norm)
    mom = dict(ffn1_norm=m_ffn1_norm, ffn1_w_gate_up=m_ffn1_w_gate_up, ffn1_w_down=m_ffn1_w_down,
               mix_norm=m_mix_norm, w_in=m_w_in, pool_w_group=m_pool_w_group, pool_scale=m_pool_scale,
               w_branch_pool=m_w_branch_pool, w_branch_attn=m_w_branch_attn, w_out=m_w_out,
               ffn2_norm=m_ffn2_norm, ffn2_w_gate_up=m_ffn2_w_gate_up, ffn2_w_down=m_ffn2_w_down,
               final_norm=m_final_norm)
    var = dict(ffn1_norm=v_ffn1_norm, ffn1_w_gate_up=v_ffn1_w_gate_up, ffn1_w_down=v_ffn1_w_down,
               mix_norm=v_mix_norm, w_in=v_w_in, pool_w_group=v_pool_w_group, pool_scale=v_pool_scale,
               w_branch_pool=v_w_branch_pool, w_branch_attn=v_w_branch_attn, w_out=v_w_out,
               ffn2_norm=v_ffn2_norm, ffn2_w_gate_up=v_ffn2_w_gate_up, ffn2_w_down=v_ffn2_w_down,
               final_norm=v_final_norm)

    c_idx = lax.axis_index("c").astype(jnp.int32).reshape(1)
    me_idx = (2 * lax.axis_index("x") + lax.axis_index("y")).astype(jnp.int32).reshape(1)
    place = jnp.concatenate([me_idx, c_idx])
    x0, tgt = x[0], loss_target[0]
    wgrp = pool_w_group[0].astype(BF16)
    g1, gm, g2, gf = ffn1_norm, mix_norm, ffn2_norm, final_norm.reshape(1, D)
    grad, delta, new_m, new_v = {}, {}, {}, {}

    def pair_sums(keys, parts, got):
        return _pair_sum(parts, got, c_idx, "pair_sum_" + keys[0])

    def chip_sums(keys, chip_parts, owned):
        return _chip_sum(chip_parts, owned, place, "chip_sum_" + keys[0])

    def adamw(keys, after=()):
        outs = _adamw([wts[k][0] for k in keys], [grad[k][0] for k in keys], [mom[k][0] for k in keys],
                      [var[k][0] for k in keys], "adamw_" + keys[0], after=after)
        for k, res in zip(keys, outs):
            grad[k], delta[k], new_m[k], new_v[k] = (o.reshape(wts[k].shape) for o in res)

    first, late = ("ffn1_w_gate_up", "ffn1_w_down"), ("w_branch_pool", "w_branch_attn", "w_out",
                                                       "ffn2_w_gate_up", "ffn2_w_down")
    own = {}
    for group in (first, ("w_in",), late):
        own.update(zip(group, _cast_into_block([wts[k][0] for k in group], me_idx, "cast_" + group[0])))
    full = dict(zip(first, _exchange_alone(_ex_gather([own[k] for k in first]), "gather_ffn1")))
    wgu1, wd1 = full["ffn1_w_gate_up"], full["ffn1_w_down"].reshape(DFF, D)
    (h1, n1, gu1, a1), (win,) = _ffn_fwd(x0, g1, wgu1, wd1, "ffn1_fwd", exchange=_ex_gather_direct([own["w_in"]]))
    sems_l, thru_l, token_l = _gather_start([own[k_] for k_ in late], [h1], "gather_late_start")
    u, xp, q, k, v, gp, gs = _mix_in(h1, gm, win, after=(token_l,))
    o_sb, ctot = _attn_fwd(q, k, v)
    arrived = _gather_wait(sems_l, thru_l, [o_sb], "gather_late_wait")
    wbp, wba, wout = _exchange_alone(_ex_relay(arrived[:3]), "relay_mix")
    wout = wout.reshape(D, D)
    (h2, pm, p, yp, ys, mm), (wgu2, wd2) = _mix_out(h1, xp, o_sb, gp, gs, wgrp, pool_scale, wbp, wba, wout,
                                                    exchange=_ex_relay(arrived[3:]))
    wd2 = wd2.reshape(DFF, D)
    dh3, loss_row, d_gf, n3, gu3, a3 = _ffn_fwd(h2, g2, wgu2, wd2, "ffn2_fwd", head=(tgt, gf))

    def grad_gate_up(n, dgu, name, exchange=None):
        res = _wgrad(n, dgu, NSH, D, name, exchange=exchange)
        return [_halves(res)] if exchange is None else ([_halves(res[0])], res[1])

    def grad_down(a, dh, name, exchange=None):
        res = _wgrad(a, dh, 1, FFS, name, exchange=exchange)
        halves = lambda g: [_halves(g.reshape(NSH, DFF // NSH, D))]
        return halves(res) if exchange is None else (halves(res[0]), res[1])

    k_gu2, k_d2, k_gu1, k_d1, k_in = (("ffn2_w_gate_up",), ("ffn2_w_down",), ("ffn1_w_gate_up",),
                                      ("ffn1_w_down",), ("w_in",))
    dh2, dgu3, d_g2 = _ffn_bwd(dh3, h2, g2, gu3, wgu2, wd2, "ffn2_bwd")
    pa = grad_gate_up(n3, dgu3, "wgrad_gu2") + grad_down(a3, dh3, "wgrad_d2")
    (dlg, dyp, dys, do_sb, dyg, dxp, d_scale), got_a = _mix_bwd_out(
        dh2, gp, gs, yp, ys, pm, wgrp, pool_scale, wbp, wba, wout, exchange=_ex_pair_swap(pa))
    chip_a = pair_sums(k_gu2 + k_d2, pa, got_a)
    kb = ("w_out", "w_branch_pool", "w_branch_attn")
    g_bp, g_ba, d_group = _wgrad_branches(p, dyp, o_sb, dys, pm, dyg)
    pb = [_halves(_wgrad(mm, dh2, 1, D, "wgrad_out").reshape(NSH, D // NSH, D)), _halves(g_bp), _halves(g_ba)]
    k_a, k_in = k_gu2 + k_d2, k_in + kb
    sems_a, thru_a, token_a = _scatter_start(chip_a, "scatter_a_start")
    dq, dk, dv = _attn_bwd(q, k, v, do_sb, ctot, after=(token_a,))
    chip_a, owned_a = _scatter_wait(sems_a, thru_a, [dq], "scatter_a_wait")
    halves_a = chip_sums(k_a, chip_a, owned_a)
    dproj = (dxp, dq, dk, dv, dlg)
    (dh1, d_gm), both_a = _mix_bwd_in(dh2, h1, gm, dproj, win, exchange=_ex_share(halves_a))
    for i, k_ in enumerate(k_a):
        grad[k_] = both_a[i].reshape(wts[k_].shape)

    p_in = [_halves(_wgrad_in(u, dproj))] + pb
    p_d1, got_in = grad_down(a1, dh1, "wgrad_d1", exchange=_ex_pair_swap(p_in))
    sems_in, thru_in, token_in = _scatter_start(pair_sums(k_in, p_in, got_in), "scatter_in_start")
    dgu1, got_d1 = _ffn_bwd_act(dh1, gu1, wd1, "ffn1_bwd_act", exchange=_ex_pair_swap(p_d1), after=(token_in,))
    sems_d1, thru_d1, token_d1 = _scatter_start(pair_sums(k_d1, p_d1, got_d1), "scatter_d1_start")
    p_gu1 = [_halves(_wgrad(n1, dgu1, NSH, D, "wgrad_gu1", after=(token_in, token_d1)))]
    chip_in, owned_in = _scatter_wait(sems_in, thru_in, p_gu1, "scatter_in_wait")
    chip_d1, owned_d1 = _scatter_wait(sems_d1, thru_d1, p_gu1, "scatter_d1_wait")
    halves_in = chip_sums(k_in, chip_in, owned_in)
    landed = _exchange_alone(_join(_ex_pair_swap(p_gu1), _ex_share(halves_in)), "pair_swap_gu1")
    for i, k_ in enumerate(k_in):
        grad[k_] = landed[1 + i].reshape(wts[k_].shape)
    sems, thru, token = _scatter_start(pair_sums(k_gu1, p_gu1, landed[:1]), "scatter_gu1_start")
    adamw(k_a, after=(token,))
    adamw(k_in, after=(token,))
    dx, d_g1 = _ffn_bwd_in(dh1, x0, g1, dgu1, wgu1, "ffn1_bwd_in", after=(token,))
    small_g = dict(ffn1_norm=d_g1, mix_norm=d_gm, ffn2_norm=d_g2, final_norm=d_gf, pool_scale=d_scale,
                   pool_w_group=d_group, loss=loss_row)
    dev = 4 * lax.axis_index("x") + 2 * lax.axis_index("y") + lax.axis_index("c")
    slots = lax.dynamic_update_slice(jnp.zeros((8, SMALL_ROWS, 128), F32), _pack_small(small_g)[None], (dev, 0, 0))
    sems_s, slots, token_s = _small_gather_start(slots, "small_gather_start")

    chip_gu1, owned_gu1 = _scatter_wait(sems, thru, [dx] + [delta[k_] for k_ in k_a + k_in], "scatter_gu1_wait")
    halves_last = chip_sums(k_d1 + k_gu1, chip_d1 + chip_gu1, owned_d1 + owned_gu1)
    both = _exchange_alone(_ex_share(halves_last), "share_last",
                           after=(token_s,))
    grad["ffn1_w_down"] = both[0].reshape(ffn1_w_down.shape)
    grad["ffn1_w_gate_up"] = both[1].reshape(ffn1_w_gate_up.shape)
    adamw(k_d1 + k_gu1, after=(token_s,))
    gathered = _small_gather_wait(sems_s, slots, [delta[k_] for k_ in k_d1 + k_gu1], "small_gather_wait")
    gathered = gathered.reshape(8 * SMALL_ROWS, 128)
    heads, groups = _small_update(gathered, _pack_small(wts), _pack_small(mom), _pack_small(var))
    for dst, head, group in zip((grad, delta, new_m, new_v), heads, groups):
        vals = _unpack_small(head, group, wts)
        if dst is grad:
            loss = vals["loss"]
        vals.pop("loss")
        dst.update(vals)
    return (loss, dx[None], *[grad[k_] for k_ in ORDER], *[delta[k_] for k_ in ORDER],
            *[new_m[k_] for k_ in ORDER], *[new_v[k_] for k_ in ORDER])
```

```python
import dataclasses
import functools

import jax
import jax.numpy as jnp
from jax import lax
from jax.experimental import pallas as pl
from jax.experimental.pallas import tpu as pltpu

F32 = jnp.float32
BF16 = jnp.bfloat16

S = 2048
D = 1024
DFF = 2816
FFS = 2 * DFF // 4
NSH = 4
PW = 512
PG = 128
POOL_WINDOWS = (2, 4, 8, 16)
HALO = 16
SBW = 512
DH = 64
EPS = 1e-6
SCALE = 0.125
LOG2E = 1.4426950408889634
TA = 256
QB = 2
MIB = 1024 * 1024

LR, B1, B2, AEPS, WD, STEP = 0.001, 0.9, 0.999, 1e-08, 0.01, 10

_VM = pl.BlockSpec(memory_space=pltpu.VMEM)
_ANY = pl.BlockSpec(memory_space=pl.ANY)
MESH = pl.DeviceIdType.MESH
SIBLING_PAIR_ID = 1


def _nn(a, b):
    return jnp.dot(a, b, preferred_element_type=F32)


def _nt(a, b):
    return lax.dot_general(a, b, (((1,), (1,)), ((), ())), preferred_element_type=F32)


def _tn(a, b):
    return lax.dot_general(a, b, (((0,), (0,)), ((), ())), preferred_element_type=F32)


def _params(sem, vmem_mib):
    return pltpu.CompilerParams(dimension_semantics=sem, vmem_limit_bytes=vmem_mib * MIB)


def _rows(tm, width):
    return pl.BlockSpec((tm, width), lambda i: (i, 0))


def _fixed(shape):
    return pl.BlockSpec(shape, lambda *_: (0,) * len(shape))


def _sds(shape, dtype):
    return pltpu.HBM(shape, dtype)


def _in_hbm(args):
    return [pltpu.with_memory_space_constraint(a, pltpu.HBM) for a in args]


def _stage(pairs):
    @pl.when(pl.program_id(0) == 0)
    def _():
        for src, dst in pairs:
            pltpu.sync_copy(src, dst)


def _vmem_like(*arrays):
    return [pltpu.VMEM(a.shape, a.dtype) for a in arrays]


class Exchange:
    def __init__(self, arrays, landing, aliases, n_sems, start, finish, sibling_only=False):
        self.arrays, self.landing, self.aliases, self.n_sems = list(arrays), list(landing), dict(aliases), n_sems
        self.start, self.finish = start, finish
        self.sibling_only = sibling_only

    def enter(self):
        if self.sibling_only:
            barrier = pltpu.get_barrier_semaphore()
            sibling = (lax.axis_index("x"), lax.axis_index("y"), 1 - lax.axis_index("c"))
            pl.semaphore_signal(barrier, inc=1, device_id=sibling, device_id_type=MESH)
            pl.semaphore_wait(barrier, 1)

    def params(self, compiler_params=None):
        kw = dict(collective_id=SIBLING_PAIR_ID) if self.sibling_only else {}
        if compiler_params is None:
            return pltpu.CompilerParams(**kw)
        return dataclasses.replace(compiler_params, **kw)


def _join(a, b):
    na, la = len(a.arrays), len(a.landing)

    def both(fa, fb):
        def run(ins, outs, ssem, rsem):
            fa(ins[:na], outs[:la], ssem.at[pl.ds(0, a.n_sems)], rsem.at[pl.ds(0, a.n_sems)])
            fb(ins[na:], outs[la:], ssem.at[pl.ds(a.n_sems, b.n_sems)], rsem.at[pl.ds(a.n_sems, b.n_sems)])
        return run

    aliases = {**a.aliases, **{na + i: la + j for i, j in b.aliases.items()}}
    return Exchange(a.arrays + b.arrays, a.landing + b.landing, aliases, a.n_sems + b.n_sems,
                    both(a.start, b.start), both(a.finish, b.finish), a.sibling_only and b.sibling_only)


def _call(body, args, *, name, grid, in_specs, out_specs, out_shape, scratch_shapes=(), compiler_params=None,
          exchange=None, free=(), after=()):
    args = [a if i in free else pltpu.with_memory_space_constraint(a, pltpu.HBM) for i, a in enumerate(args)]
    if exchange is None:
        n_in = len(in_specs)

        def plain(*refs):
            body(*refs[:n_in], *refs[n_in + len(after):])

        return pl.pallas_call(plain, name=name, grid=grid, in_specs=list(in_specs) + [_ANY] * len(after),
                              out_specs=out_specs, out_shape=out_shape, scratch_shapes=list(scratch_shapes),
                              compiler_params=compiler_params)(*args, *after)
    ex = exchange
    n_in, n_out, n_scr = len(in_specs), len(out_specs), len(scratch_shapes)
    na, nl = len(ex.arrays), len(ex.landing)

    def hosted(*refs):
        at = [0]

        def take(n):
            at[0] += n
            return refs[at[0] - n:at[0]]

        k_in, _, e_in, k_out, e_out, k_scr = take(n_in), take(len(after)), take(na), take(n_out), take(nl), take(n_scr)
        ssem, rsem = take(2)
        ids = [pl.program_id(a) for a in range(len(grid))]
        first = functools.reduce(jnp.logical_and, [i == 0 for i in ids])
        last = functools.reduce(jnp.logical_and, [i == g - 1 for i, g in zip(ids, grid)])

        @pl.when(first)
        def _():
            ex.enter()
            ex.start(e_in, e_out, ssem, rsem)

        body(*k_in, *k_out, *k_scr)

        @pl.when(last)
        def _():
            ex.finish(e_in, e_out, ssem, rsem)

    outs = pl.pallas_call(
        hosted, name=name, grid=grid,
        in_specs=list(in_specs) + [_ANY] * (len(after) + na), out_specs=list(out_specs) + [_ANY] * nl,
        out_shape=list(out_shape) + ex.landing,
        scratch_shapes=list(scratch_shapes) + [pltpu.SemaphoreType.DMA((ex.n_sems,))] * 2,
        input_output_aliases={n_in + len(after) + i: n_out + j for i, j in ex.aliases.items()},
        compiler_params=ex.params(compiler_params),
    )(*args, *after, *_in_hbm(ex.arrays))
    return outs[:n_out], outs[n_out:]


def _exchange_alone(ex, name, after=()):
    na, nl = len(ex.arrays), len(ex.landing)

    def body(*refs):
        outs = refs[na + len(after):na + len(after) + nl]
        ex.enter()
        ex.start(refs[:na], outs, refs[-2], refs[-1])
        ex.finish(refs[:na], outs, refs[-2], refs[-1])

    return pl.pallas_call(
        body, name=name, in_specs=[_ANY] * (na + len(after)), out_specs=[_ANY] * nl,
        out_shape=ex.landing, scratch_shapes=[pltpu.SemaphoreType.DMA((ex.n_sems,))] * 2,
        input_output_aliases=ex.aliases, compiler_params=ex.params(),
    )(*_in_hbm(ex.arrays), *after)


_HBM = pl.BlockSpec(memory_space=pltpu.HBM)
_SEM = pl.BlockSpec(memory_space=pltpu.SEMAPHORE)
_EFFECT = pltpu.SideEffectType.DATAFLOW_SIDE_EFFECTING


def _scatter_copies(srcs, lands, ssems, rsems):
    x, y, c, chips = _place()
    return [_remote(srcs[w].at[2 * px + py], lands[w].at[k], ssems[3 * w + k], rsems[3 * w + k], (px, py, c))
            for w in range(len(srcs)) for k, (px, py) in enumerate(chips)]


def _scatter_start(parts, name):
    parts = list(parts)
    n, ncp = len(parts), 3 * len(parts)
    lands = [lax.empty((3,) + p.shape[1:], p.dtype) for p in parts]

    def body(*refs):
        srcs, land_refs = refs[:n], refs[n:2 * n]
        ssems, rsems = refs[2 * n:2 * n + ncp], refs[2 * n + ncp:2 * n + 2 * ncp]
        for cp in _scatter_copies(srcs, land_refs, ssems, rsems):
            cp.start()
        token = refs[-1]
        token[...] = jnp.zeros_like(token)

    outs = pl.pallas_call(
        body, name=name,
        out_shape=([pltpu.SemaphoreType.DMA(())] * (2 * ncp) + [pltpu.HBM(a.shape, a.dtype) for a in parts + lands]
                   + [jax.ShapeDtypeStruct((8, 128), F32)]),
        in_specs=[_HBM] * (2 * n), out_specs=[_SEM] * (2 * ncp) + [_HBM] * (2 * n) + [_VM],
        input_output_aliases={i: 2 * ncp + i for i in range(2 * n)},
        compiler_params=pltpu.CompilerParams(has_side_effects=_EFFECT),
    )(*_in_hbm(parts), *_in_hbm(lands))
    sems, thru, token = outs[:2 * ncp], outs[2 * ncp:2 * ncp + 2 * n], outs[-1]
    return sems, thru, token


def _scatter_wait(sems, thru, after, name):
    n = len(thru) // 2
    ncp = 3 * n

    def body(*refs):
        srcs, land_refs = refs[:n], refs[n:2 * n]
        ssems, rsems = refs[2 * n:2 * n + ncp], refs[2 * n + ncp:2 * n + 2 * ncp]
        for cp in _scatter_copies(srcs, land_refs, ssems, rsems):
            cp.wait_send()
            cp.wait_recv()

    outs = pl.pallas_call(
        body, name=name, out_shape=[pltpu.HBM(a.shape, a.dtype) for a in thru],
        in_specs=[_HBM] * (2 * n) + [_SEM] * (2 * ncp) + [_ANY] * len(after), out_specs=[_HBM] * (2 * n),
        input_output_aliases={i: i for i in range(2 * n)},
        compiler_params=pltpu.CompilerParams(has_side_effects=_EFFECT),
    )(*thru, *sems, *after)
    return outs[:n], outs[n:]


def _swap_copies(srcs, lands, ssems, rsems):
    x, y, c, _ = _place()
    return [_remote(srcs[w].at[:, 1 - c], lands[w], ssems[w], rsems[w], (x, y, 1 - c)) for w in range(len(srcs))]


def _swap_start(grads, name):
    grads = list(grads)
    n = len(grads)
    lands = [lax.empty((NSH,) + g.shape[2:], g.dtype) for g in grads]

    def body(*refs):
        barrier = pltpu.get_barrier_semaphore()
        sibling = (lax.axis_index("x"), lax.axis_index("y"), 1 - lax.axis_index("c"))
        pl.semaphore_signal(barrier, inc=1, device_id=sibling, device_id_type=MESH)
        pl.semaphore_wait(barrier, 1)
        for cp in _swap_copies(refs[:n], refs[n:2 * n], refs[2 * n:3 * n], refs[3 * n:4 * n]):
            cp.start()
        refs[-1][...] = jnp.zeros_like(refs[-1])

    outs = pl.pallas_call(
        body, name=name,
        out_shape=([pltpu.SemaphoreType.DMA(())] * (2 * n) + [pltpu.HBM(a.shape, a.dtype) for a in grads + lands]
                   + [jax.ShapeDtypeStruct((8, 128), F32)]),
        in_specs=[_HBM] * (2 * n), out_specs=[_SEM] * (2 * n) + [_HBM] * (2 * n) + [_VM],
        input_output_aliases={i: 2 * n + i for i in range(2 * n)},
        compiler_params=pltpu.CompilerParams(has_side_effects=_EFFECT, collective_id=SIBLING_PAIR_ID),
    )(*_in_hbm(grads), *_in_hbm(lands))
    return outs[:2 * n], outs[2 * n:4 * n], outs[-1]


def _swap_wait(sems, thru, after, name):
    n = len(thru) // 2

    def body(*refs):
        for cp in _swap_copies(refs[:n], refs[n:2 * n], refs[2 * n:3 * n], refs[3 * n:4 * n]):
            cp.wait_send()
            cp.wait_recv()

    outs = pl.pallas_call(
        body, name=name, out_shape=[pltpu.HBM(a.shape, a.dtype) for a in thru],
        in_specs=[_HBM] * (2 * n) + [_SEM] * (2 * n) + [_ANY] * len(after), out_specs=[_HBM] * (2 * n),
        input_output_aliases={i: i for i in range(2 * n)},
        compiler_params=pltpu.CompilerParams(has_side_effects=_EFFECT),
    )(*thru, *sems, *after)
    return outs[:n], outs[n:]


def _gather_copies(bufs, ssems, rsems, sending):
    x, y, c, chips = _place()
    out = []
    for w, ref in enumerate(bufs):
        half = ref.shape[1] // 2
        for k, (px, py) in enumerate(chips):
            rows = ref.at[2 * x + y if sending else 2 * px + py, pl.ds(c * half, half)]
            out.append(_remote(rows, rows, ssems[3 * w + k], rsems[3 * w + k], (px, py, c)))
    return out


def _gather_start(bufs, after, name):
    n, ncp = len(bufs), 3 * len(bufs)

    def body(*refs):
        ssems, rsems = refs[n + len(after):n + len(after) + ncp], refs[n + len(after) + ncp:n + len(after) + 2 * ncp]
        for cp in _gather_copies(refs[:n], ssems, rsems, True):
            cp.start()
        token = refs[-1]
        token[...] = jnp.zeros_like(token)

    outs = pl.pallas_call(
        body, name=name,
        out_shape=([pltpu.SemaphoreType.DMA(())] * (2 * ncp) + [pltpu.HBM(a.shape, a.dtype) for a in bufs]
                   + [jax.ShapeDtypeStruct((8, 128), F32)]),
        in_specs=[_HBM] * n + [_ANY] * len(after), out_specs=[_SEM] * (2 * ncp) + [_HBM] * n + [_VM],
        input_output_aliases={i: 2 * ncp + i for i in range(n)},
        compiler_params=pltpu.CompilerParams(has_side_effects=_EFFECT),
    )(*_in_hbm(bufs), *after)
    return outs[:2 * ncp], outs[2 * ncp:2 * ncp + n], outs[-1]


def _gather_wait(sems, thru, after, name):
    n = len(thru)
    ncp = 3 * n

    def body(*refs):
        ssems, rsems = refs[n:n + ncp], refs[n + ncp:n + 2 * ncp]
        for cp in _gather_copies(refs[:n], ssems, rsems, True):
            cp.wait_send()
        for cp in _gather_copies(refs[:n], ssems, rsems, False):
            cp.wait_recv()

    return pl.pallas_call(
        body, name=name, out_shape=[pltpu.HBM(a.shape, a.dtype) for a in thru],
        in_specs=[_HBM] * n + [_SEM] * (2 * ncp) + [_ANY] * len(after), out_specs=[_HBM] * n,
        input_output_aliases={i: i for i in range(n)},
        compiler_params=pltpu.CompilerParams(has_side_effects=_EFFECT),
    )(*thru, *sems, *after)


def _rms(x):
    r = lax.rsqrt(jnp.mean(x * x, axis=-1, keepdims=True) + EPS)
    return r, x * r


def _rms_bwd(dn, xr, r, gain):
    dng = dn * gain
    dx = r * (dng - xr * jnp.mean(dng * xr, axis=-1, keepdims=True))
    return dx, jnp.sum(dn * xr, axis=0, keepdims=True)


def _ffn_fwd(x, gain, wgu, wd, name, exchange=None, head=None):
    tm = 256

    def body(x_ref, g_ref, wgu_hbm, wd_hbm, *rest):
        if head is None:
            h_ref, n_ref, gu_ref, a_ref, wgu_ref, wd_ref = rest
        else:
            t_ref, gf_ref, h_ref, loss_ref, dgf_ref, n_ref, gu_ref, a_ref, wgu_ref, wd_ref = rest
        _stage([(wgu_hbm, wgu_ref), (wd_hbm, wd_ref)])
        x = x_ref[...]
        _, xr = _rms(x)
        n = (xr * g_ref[...]).astype(BF16)
        n_ref[...] = n
        acc = jnp.zeros((tm, D), F32)
        for j in range(2):
            g = _nn(n, wgu_ref[j])
            u = _nn(n, wgu_ref[2 + j])
            gu_ref[:, j * FFS:(j + 1) * FFS] = g.astype(BF16)
            gu_ref[:, (2 + j) * FFS:(3 + j) * FFS] = u.astype(BF16)
            half_act = (0.5 * (g * jax.nn.sigmoid(g) * u)).astype(BF16)
            a_ref[:, j * FFS:(j + 1) * FFS] = half_act
            acc = acc + _nn(half_act, wd_ref[j * FFS:(j + 1) * FFS, :])
        h = x + acc
        if head is None:
            h_ref[...] = h
            return
        gf = gf_ref[...]
        r, hr = _rms(h)
        err = hr * gf - t_ref[...]
        dh, dgain = _rms_bwd(err * (1.0 / D), hr, r, gf)
        h_ref[...] = dh

        @pl.when(pl.program_id(0) == 0)
        def _():
            dgf_ref[...] = jnp.zeros_like(dgf_ref)
            loss_ref[...] = jnp.zeros_like(loss_ref)

        dgf_ref[...] += dgain
        loss_ref[...] += jnp.full((1, 128), (0.5 / D) * jnp.sum(err * err), F32)

    saved_specs = [_rows(tm, D), _rows(tm, 4 * FFS), _rows(tm, DFF)]
    saved_shapes = [_sds((S, D), BF16), _sds((S, 4 * FFS), BF16), _sds((S, DFF), BF16)]
    if head is None:
        return _call(
            body, (x, gain, wgu, wd), name=name, grid=(S // tm,),
            in_specs=[_rows(tm, D), _fixed((1, D)), _ANY, _ANY],
            out_specs=[_rows(tm, D)] + saved_specs, out_shape=[_sds((S, D), F32)] + saved_shapes,
            scratch_shapes=_vmem_like(wgu, wd),
            compiler_params=_params(("arbitrary",), 56), exchange=exchange)
    return _call(
        body, (x, gain, wgu, wd, *head), name=name, grid=(S // tm,),
        in_specs=[_rows(tm, D), _fixed((1, D)), _ANY, _ANY, _rows(tm, D), _fixed((1, D))],
        out_specs=[_rows(tm, D), _fixed((1, 128)), _fixed((1, D))] + saved_specs,
        out_shape=[_sds((S, D), F32), _sds((1, 128), F32), _sds((1, D), F32)] + saved_shapes,
        scratch_shapes=_vmem_like(wgu, wd),
        compiler_params=_params(("arbitrary",), 56), exchange=exchange, free=(4, 5))


def _ffn_bwd(dh, x, gain, gu, wgu, wd, name):
    tm = 256

    def body(dh_ref, x_ref, g_ref, gu_ref, wgu_hbm, wd_hbm, dx_ref, dgu_ref, dg_ref, wgu_ref, wd_ref):
        _stage([(wgu_hbm, wgu_ref), (wd_hbm, wd_ref)])
        dh = dh_ref[...]
        dhb = dh.astype(BF16)
        dn = jnp.zeros((tm, D), F32)
        for j in range(2):
            g = gu_ref[:, j * FFS:(j + 1) * FFS].astype(F32)
            u = gu_ref[:, (2 + j) * FFS:(3 + j) * FFS].astype(F32)
            da = 0.5 * _nt(dhb, wd_ref[j * FFS:(j + 1) * FFS, :])
            sg = jax.nn.sigmoid(g)
            dgb = (da * u * (sg * (1.0 + g * (1.0 - sg)))).astype(BF16)
            dub = (da * (g * sg)).astype(BF16)
            dgu_ref[:, j * FFS:(j + 1) * FFS] = dgb
            dgu_ref[:, (2 + j) * FFS:(3 + j) * FFS] = dub
            dn = dn + _nt(dgb, wgu_ref[j]) + _nt(dub, wgu_ref[2 + j])
        r, xr = _rms(x_ref[...])
        dx, dgain = _rms_bwd(dn, xr, r, g_ref[...])
        dx_ref[...] = dh + dx

        @pl.when(pl.program_id(0) == 0)
        def _():
            dg_ref[...] = jnp.zeros_like(dg_ref)

        dg_ref[...] += dgain

    return _call(
        body, (dh, x, gain, gu, wgu, wd), name=name, grid=(S // tm,),
        in_specs=[_rows(tm, D), _rows(tm, D), _fixed((1, D)), _rows(tm, 4 * FFS), _ANY, _ANY],
        out_specs=[_rows(tm, D), _rows(tm, 4 * FFS), _fixed((1, D))],
        out_shape=[_sds((S, D), F32), _sds((S, 4 * FFS), BF16), _sds((1, D), F32)],
        scratch_shapes=_vmem_like(wgu, wd), compiler_params=_params(("arbitrary",), 56))


def _ffn_bwd_act(dh, gu, wd, name, exchange=None, after=()):
    tm = 512

    def body(dh_ref, gu_ref, wd_hbm, dgu_ref, wd_ref):
        _stage([(wd_hbm, wd_ref)])
        dhb = dh_ref[...].astype(BF16)
        for j in range(2):
            g = gu_ref[:, j * FFS:(j + 1) * FFS].astype(F32)
            u = gu_ref[:, (2 + j) * FFS:(3 + j) * FFS].astype(F32)
            da = 0.5 * _nt(dhb, wd_ref[j * FFS:(j + 1) * FFS, :])
            sg = jax.nn.sigmoid(g)
            dgu_ref[:, j * FFS:(j + 1) * FFS] = (da * u * (sg * (1.0 + g * (1.0 - sg)))).astype(BF16)
            dgu_ref[:, (2 + j) * FFS:(3 + j) * FFS] = (da * (g * sg)).astype(BF16)

    res = _call(
        body, (dh, gu, wd), name=name, grid=(S // tm,),
        in_specs=[_rows(tm, D), _rows(tm, 4 * FFS), _ANY], out_specs=[_rows(tm, 4 * FFS)],
        out_shape=[_sds((S, 4 * FFS), BF16)], scratch_shapes=_vmem_like(wd),
        compiler_params=_params(("arbitrary",), 56), exchange=exchange, after=after)
    return res[0] if exchange is None else (res[0][0], res[1])


def _ffn_bwd_in(dh, x, gain, dgu, wgu, name, exchange=None, after=()):
    tm = 512

    def body(dh_ref, x_ref, g_ref, dgu_ref, wgu_hbm, dx_ref, dg_ref, wgu_ref):
        _stage([(wgu_hbm, wgu_ref)])
        dn = jnp.zeros((tm, D), F32)
        for j in range(NSH):
            dn = dn + _nt(dgu_ref[:, j * FFS:(j + 1) * FFS], wgu_ref[j])
        r, xr = _rms(x_ref[...])
        dx, dgain = _rms_bwd(dn, xr, r, g_ref[...])
        dx_ref[...] = dh_ref[...] + dx

        @pl.when(pl.program_id(0) == 0)
        def _():
            dg_ref[...] = jnp.zeros_like(dg_ref)

        dg_ref[...] += dgain

    return _call(
        body, (dh, x, gain, dgu, wgu), name=name, grid=(S // tm,),
        in_specs=[_rows(tm, D), _rows(tm, D), _fixed((1, D)), _rows(tm, 4 * FFS), _ANY],
        out_specs=[_rows(tm, D), _fixed((1, D))],
        out_shape=[_sds((S, D), F32), _sds((1, D), F32)],
        scratch_shapes=_vmem_like(wgu),
        compiler_params=_params(("arbitrary",), 56), exchange=exchange, after=after)


def _mix_in(h, gain, w_in, after=()):
    tm = 512

    def body(h_ref, g_ref, w_hbm, u_ref, xp_ref, q_ref, k_ref, v_ref, gp_ref, gs_ref, w_ref):
        _stage([(w_hbm, w_ref)])
        _, hr = _rms(h_ref[...])
        u = (hr * g_ref[...]).astype(BF16)
        u_ref[...] = u
        p0 = _nn(u, w_ref[0])
        xp_ref[...] = p0[:, :PW]
        q_ref[...] = p0[:, PW:].astype(BF16)
        p1 = _nn(u, w_ref[1])
        k_ref[...] = p1[:, :SBW].astype(BF16)
        v_ref[...] = p1[:, SBW:].astype(BF16)
        gp_ref[...] = jax.nn.sigmoid(_nn(u, w_ref[2])).astype(BF16)
        gs_ref[...] = jax.nn.sigmoid(_nn(u, w_ref[3])).astype(BF16)

    return _call(
        body, (h, gain, w_in), name="mix_in", grid=(S // tm,),
        in_specs=[_rows(tm, D), _fixed((1, D)), _ANY],
        out_specs=[_rows(tm, D), _rows(tm, PW), _rows(tm, SBW), _rows(tm, SBW), _rows(tm, SBW),
                   _rows(tm, D), _rows(tm, D)],
        out_shape=[_sds((S, D), BF16), _sds((S, PW), F32), _sds((S, SBW), BF16), _sds((S, SBW), BF16),
                   _sds((S, SBW), BF16), _sds((S, D), BF16), _sds((S, D), BF16)],
        scratch_shapes=_vmem_like(w_in),
        compiler_params=_params(("arbitrary",), 48), free=(1,), after=after)


def _hilo_dot(x, tri):
    hi = x.astype(BF16)
    lo = (x - hi.astype(F32)).astype(BF16)
    return _nn(hi, tri) + _nn(lo, tri)


def _log_terms(qk):
    z2 = qk * (SCALE * LOG2E)
    lb = jnp.minimum(z2, 0.0) - jnp.log2(1.0 + jnp.exp2(-jnp.abs(z2)))
    return lb, lb - z2


def _head_masks():
    lane = lax.broadcasted_iota(jnp.int32, (1, 2 * DH), 1)
    return (lane < DH, lane >= DH)


def _attn_fwd(q, k, v, exchange=None):
    T = TA

    def body(q_ref, k_ref, v_ref, o_ref, c_ref):
        i2 = 2 * pl.program_id(1)
        row = lax.broadcasted_iota(jnp.int32, (T, T), 0)
        col = lax.broadcasted_iota(jnp.int32, (T, T), 1)
        after = (row > col).astype(BF16)
        causal = col < row
        masks = _head_masks()
        qms = {}
        for b in range(QB):
            q2 = q_ref[b * T:(b + 1) * T, :]
            for h, hm in enumerate(masks):
                qms[b, h] = jnp.where(hm, q2, jnp.zeros_like(q2))

        def blocks(keys, pairs, carries, os):
            ks, vms = [], []
            for j in keys:
                rows = pl.ds(pl.multiple_of(j * T, T), T)
                vj = v_ref[rows, :]
                ks.append(k_ref[rows, :])
                vms.append([jnp.where(hm, vj, jnp.zeros_like(vj)) for hm in masks])
            units = [(n, h) for n in range(len(pairs)) for h in range(2)]
            qks = {(n, h): _nt(qms[pairs[n][0], h], ks[pairs[n][1]]) for n, h in units}
            lbs, l1ms = {}, {}
            for u in units:
                lbs[u], l1m = _log_terms(qks[u])
                l1ms[u] = jnp.where(causal, l1m, 0.0) if pairs[u[0]][2] else l1m
            cins = {u: _hilo_dot(l1ms[u], after) for u in units}
            carries, os = dict(carries), list(os)
            for n, h in units:
                b, key, diag = pairs[n]
                a = jnp.exp2(lbs[n, h] + cins[n, h] + carries[b, h])
                if diag:
                    a = jnp.where(causal, a, 0.0)
                os[b] = os[b] + _nn(a.astype(BF16), vms[key][h])
                carries[b, h] = carries[b, h] + jnp.sum(l1ms[n, h], axis=1, keepdims=True)
            return carries, tuple(os)

        carries = {(b, h): jnp.zeros((T, 1), F32) for b in range(QB) for h in range(2)}
        os = tuple(jnp.zeros((T, 2 * DH), F32) for _ in range(QB))
        carries, os = blocks([i2 + 1, i2], [(1, 0, True), (0, 1, True), (1, 1, False)], carries, os)
        carries, os = lax.fori_loop(
            0, i2 // 2,
            lambda t, c: blocks([i2 - 1 - 2 * t, i2 - 2 - 2 * t],
                                [(0, 0, False), (1, 0, False), (0, 1, False), (1, 1, False)], c[0], c[1]),
            (carries, os))
        for b in range(QB):
            o_ref[b * T:(b + 1) * T, :] = os[b].astype(BF16)
            c_ref[b * T:(b + 1) * T, :] = jnp.where(masks[0], carries[b, 0], carries[b, 1])

    blk = pl.BlockSpec((QB * T, 2 * DH), lambda p, i: (i, p))
    full = pl.BlockSpec((S, 2 * DH), lambda p, i: (0, p))
    return _call(
        body, (q, k, v), name="attn_fwd", grid=(SBW // (2 * DH), S // (QB * T)),
        in_specs=[blk, full, full], out_specs=[blk, blk],
        out_shape=[_sds((S, SBW), BF16), _sds((S, SBW), F32)],
        compiler_params=_params(("arbitrary", "arbitrary"), 40), exchange=exchange)


def _attn_bwd(q, k, v, do, ctot, after=()):
    T = TA
    nq = S // (QB * T)

    def body(q_ref, k_ref, v_ref, do_ref, c_ref, dq_ref, dk_ref, dv_ref, dk_acc, dv_acc):
        step = pl.program_id(1)
        i2 = 2 * step

        @pl.when(step == 0)
        def _():
            dk_acc[...] = jnp.zeros_like(dk_acc)
            dv_acc[...] = jnp.zeros_like(dv_acc)

        row = lax.broadcasted_iota(jnp.int32, (T, T), 0)
        col = lax.broadcasted_iota(jnp.int32, (T, T), 1)
        upto = (row <= col).astype(BF16)
        before = (row < col).astype(BF16)
        causal = col < row
        masks = _head_masks()
        qms, doms, ctots = {}, {}, {}
        for b in range(QB):
            q2, do2 = q_ref[b * T:(b + 1) * T, :], do_ref[b * T:(b + 1) * T, :]
            for h, hm in enumerate(masks):
                qms[b, h] = jnp.where(hm, q2, jnp.zeros_like(q2))
                doms[b, h] = jnp.where(hm, do2, jnp.zeros_like(do2))
                ctots[b, h] = c_ref[b * T:(b + 1) * T, h * DH:h * DH + 1]

        def blocks(keys, pairs, sums, dqs):
            rows = [pl.ds(pl.multiple_of(j * T, T), T) for j in keys]
            ks, vs = [k_ref[r, :] for r in rows], [v_ref[r, :] for r in rows]
            kms = [[jnp.where(hm, kj, jnp.zeros_like(kj)) for hm in masks] for kj in ks]
            units = [(n, h) for n in range(len(pairs)) for h in range(2)]
            qks = {(n, h): _nt(qms[pairs[n][0], h], ks[pairs[n][1]]) for n, h in units}
            das = {(n, h): _nt(doms[pairs[n][0], h], vs[pairs[n][1]]) for n, h in units}
            lbs, l1ms = {}, {}
            for u in units:
                lbs[u], l1m = _log_terms(qks[u])
                l1ms[u] = jnp.where(causal, l1m, 0.0) if pairs[u[0]][2] else l1m
            pins = {u: _hilo_dot(l1ms[u], upto) for u in units}
            sums = dict(sums)
            a_s, dls, cps = {}, {}, {}
            for n, h in units:
                b, _, diag = pairs[n]
                cl, cp = sums[b, h]
                a = jnp.exp2(lbs[n, h] + (ctots[b, h] - cl) - pins[n, h])
                if diag:
                    a = jnp.where(causal, a, 0.0)
                a_s[n, h] = a.astype(BF16)
                dls[n, h] = das[n, h] * a
                cps[n, h] = cp
                sums[b, h] = (cl + jnp.sum(l1ms[n, h], axis=1, keepdims=True),
                              cp + jnp.sum(dls[n, h], axis=1, keepdims=True))
            pexs = {u: _hilo_dot(dls[u], before) for u in units}
            dzbs = {}
            for u in units:
                dz = dls[u] - jnp.exp2(lbs[u]) * (dls[u] + pexs[u] + cps[u])
                if pairs[u[0]][2]:
                    dz = jnp.where(causal, dz, 0.0)
                dzbs[u] = dz.astype(BF16)
            dqs = list(dqs)
            for n, h in units:
                dqs[pairs[n][0]] = dqs[pairs[n][0]] + _nn(dzbs[n, h], kms[pairs[n][1]][h])
            for key, r in enumerate(rows):
                mine = [(n, h) for n, h in units if pairs[n][1] == key]
                dk_acc[r, :] += functools.reduce(jnp.add, [_tn(dzbs[u], qms[pairs[u[0]][0], u[1]]) for u in mine])
                dv_acc[r, :] += functools.reduce(jnp.add, [_tn(a_s[u], doms[pairs[u[0]][0], u[1]]) for u in mine])
            return sums, tuple(dqs)

        zero = jnp.zeros((T, 1), F32)
        sums = {(b, h): (zero, zero) for b in range(QB) for h in range(2)}
        dqs = tuple(jnp.zeros((T, 2 * DH), F32) for _ in range(QB))
        sums, dqs = lax.fori_loop(
            0, i2 // 2,
            lambda t, c: blocks([2 * t, 2 * t + 1],
                                [(0, 0, False), (1, 0, False), (0, 1, False), (1, 1, False)], c[0], c[1]),
            (sums, dqs))
        _, dqs = blocks([i2, i2 + 1], [(0, 0, True), (1, 0, False), (1, 1, True)], sums, dqs)
        for b in range(QB):
            dq_ref[b * T:(b + 1) * T, :] = (dqs[b] * SCALE).astype(BF16)

        @pl.when(step == nq - 1)
        def _():
            dk_ref[...] = (dk_acc[...] * SCALE).astype(BF16)
            dv_ref[...] = dv_acc[...].astype(BF16)

    blk = pl.BlockSpec((QB * T, 2 * DH), lambda p, i: (i, p))
    full = pl.BlockSpec((S, 2 * DH), lambda p, i: (0, p))
    return _call(
        body, (q, k, v, do, ctot), name="attn_bwd", grid=(SBW // (2 * DH), nq),
        in_specs=[blk, full, full, blk, blk], out_specs=[blk, full, full],
        out_shape=[_sds((S, SBW), BF16), _sds((S, SBW), BF16), _sds((S, SBW), BF16)],
        scratch_shapes=[pltpu.VMEM((S, 2 * DH), F32), pltpu.VMEM((S, 2 * DH), F32)],
        compiler_params=_params(("arbitrary", "arbitrary"), 40), after=after)


def _pool_counts(first_row, tm):
    pos = first_row + lax.broadcasted_iota(jnp.int32, (tm, 1), 0)
    return [jnp.minimum(pos + 1, w).astype(F32) for w in POOL_WINDOWS]


def _mix_out(h, xp, o_sb, gp, gs, w_group, scale, w_bp, w_ba, w_out, exchange=None):
    tm = 512

    def body(h_ref, xp_ref, o_ref, gp_ref, gs_ref, wg_hbm, sc_ref, wbp_hbm, wba_hbm, wo_hbm,
             h2_ref, pm_ref, p_ref, yp_ref, ys_ref, m_ref, halo, wg_ref, wbp_ref, wba_ref, wo_ref):
        _stage([(wg_hbm, wg_ref), (wbp_hbm, wbp_ref), (wba_hbm, wba_ref), (wo_hbm, wo_ref)])
        i = pl.program_id(0)

        @pl.when(i == 0)
        def _():
            halo[...] = jnp.zeros_like(halo)

        xp = xp_ref[...]
        ext = jnp.concatenate([halo[...], xp], axis=0)
        halo[...] = xp[tm - HALO:, :]
        counts = _pool_counts(i * tm, tm)
        for gi in range(len(POOL_WINDOWS)):
            lanes = slice(gi * PG, (gi + 1) * PG)
            win = ext[:, lanes]
            for step in range(gi + 1):
                win = win + pltpu.roll(win, 1 << step, 0)
            pm = (win[HALO:, :] / counts[gi] - xp[:, lanes]).astype(BF16)
            pm_ref[:, lanes] = pm
            p_ref[:, lanes] = (_nn(pm, wg_ref[gi]) * sc_ref[:, lanes]).astype(BF16)
        pb = p_ref[...]
        ob = o_ref[...]
        for j in range(NSH):
            cols = slice(j * (D // NSH), (j + 1) * (D // NSH))
            yp = _nn(pb, wbp_ref[j])
            ys = _nn(ob, wba_ref[j])
            yp_ref[:, cols] = yp.astype(BF16)
            ys_ref[:, cols] = ys.astype(BF16)
            m_ref[:, cols] = (gp_ref[:, cols].astype(F32) * yp + gs_ref[:, cols].astype(F32) * ys).astype(BF16)
        h2_ref[...] = h_ref[...] + _nn(m_ref[...], wo_ref[...])

    return _call(
        body, (h, xp, o_sb, gp, gs, w_group, scale, w_bp, w_ba, w_out), name="mix_out", grid=(S // tm,),
        in_specs=[_rows(tm, D), _rows(tm, PW), _rows(tm, SBW), _rows(tm, D), _rows(tm, D),
                  _ANY, _fixed((1, PW)), _ANY, _ANY, _ANY],
        out_specs=[_rows(tm, D), _rows(tm, PW), _rows(tm, PW), _rows(tm, D), _rows(tm, D), _rows(tm, D)],
        out_shape=[_sds((S, D), F32), _sds((S, PW), BF16), _sds((S, PW), BF16), _sds((S, D), BF16),
                   _sds((S, D), BF16), _sds((S, D), BF16)],
        scratch_shapes=[pltpu.VMEM((HALO, PW), F32)] + _vmem_like(w_group, w_bp, w_ba, w_out),
        compiler_params=_params(("arbitrary",), 48), free=(5, 6), exchange=exchange)


def _mix_bwd_out(dh, gp, gs, yp, ys, pm, w_group, scale, w_bp, w_ba, w_out, exchange=None):
    tm = 512
    nt = S // tm

    def body(dh_ref, gp_ref, gs_ref, yp_ref, ys_ref, pm_ref, wg_hbm, sc_ref, wbp_hbm, wba_hbm, wo_hbm,
             dlg_ref, dyp_ref, dys_ref, do_ref, dyg_ref, dxp_ref, dsc_ref, halo, wg_ref, wbp_ref, wba_ref, wo_ref):
        _stage([(wg_hbm, wg_ref), (wbp_hbm, wbp_ref), (wba_hbm, wba_ref), (wo_hbm, wo_ref)])
        step = pl.program_id(0)

        @pl.when(step == 0)
        def _():
            halo[...] = jnp.zeros_like(halo)
            dsc_ref[...] = jnp.zeros_like(dsc_ref)

        dm = _nt(dh_ref[...].astype(BF16), wo_ref[...])
        gp = gp_ref[...].astype(F32)
        gs = gs_ref[...].astype(F32)
        yp = yp_ref[...].astype(F32)
        ys = ys_ref[...].astype(F32)
        dlg_ref[:, :D] = (dm * yp * gp * (1.0 - gp)).astype(BF16)
        dlg_ref[:, D:] = (dm * ys * gs * (1.0 - gs)).astype(BF16)
        dyp_ref[...] = (dm * gp).astype(BF16)
        dys_ref[...] = (dm * gs).astype(BF16)
        dp = jnp.zeros((tm, PW), F32)
        do = jnp.zeros((tm, SBW), F32)
        for j in range(NSH):
            cols = slice(j * (D // NSH), (j + 1) * (D // NSH))
            dp = dp + _nt(dyp_ref[:, cols], wbp_ref[j])
            do = do + _nt(dys_ref[:, cols], wba_ref[j])
        do_ref[...] = do.astype(BF16)
        counts = _pool_counts((nt - 1 - step) * tm, tm)
        dscale = []
        for gi in range(len(POOL_WINDOWS)):
            lanes = slice(gi * PG, (gi + 1) * PG)
            dpg = dp[:, lanes]
            dscale.append(jnp.sum(dpg * _nn(pm_ref[:, lanes], wg_ref[gi]), axis=0, keepdims=True))
            dyg = (dpg * sc_ref[:, lanes]).astype(BF16)
            dyg_ref[:, lanes] = dyg
            dpm = _nt(dyg, wg_ref[gi])
            per = dpm / counts[gi]
            win = jnp.concatenate([per, halo[:, lanes]], axis=0)
            halo[:, lanes] = per[:HALO, :]
            for s in range(gi + 1):
                win = win + pltpu.roll(win, tm + HALO - (1 << s), 0)
            dxp_ref[:, lanes] = (win[:tm, :] - dpm).astype(BF16)
        dsc_ref[...] += jnp.concatenate(dscale, axis=1)

    rev = lambda width: pl.BlockSpec((tm, width), lambda i: (nt - 1 - i, 0))
    return _call(
        body, (dh, gp, gs, yp, ys, pm, w_group, scale, w_bp, w_ba, w_out), name="mix_bwd_out", grid=(nt,),
        in_specs=[rev(D), rev(D), rev(D), rev(D), rev(D), rev(PW), _ANY, _fixed((1, PW)), _ANY, _ANY, _ANY],
        out_specs=[rev(2 * D), rev(D), rev(D), rev(SBW), rev(PW), rev(PW), _fixed((1, PW))],
        out_shape=[_sds((S, 2 * D), BF16), _sds((S, D), BF16), _sds((S, D), BF16), _sds((S, SBW), BF16),
                   _sds((S, PW), BF16), _sds((S, PW), BF16), _sds((1, PW), F32)],
        scratch_shapes=[pltpu.VMEM((HALO, PW), F32)] + _vmem_like(w_group, w_bp, w_ba, w_out),
        compiler_params=_params(("arbitrary",), 48), exchange=exchange)


def _mix_bwd_in(dh, h, gain, pieces, w_in, exchange=None):
    tm = 512
    widths = [p.shape[1] for p in pieces]

    def body(dh_ref, h_ref, g_ref, *rest):
        piece_refs, (w_hbm, dx_ref, dg_ref, w_ref, dp_ref) = rest[:len(pieces)], rest[len(pieces):]
        _stage([(w_hbm, w_ref)])
        at = 0
        for ref, width in zip(piece_refs, widths):
            dp_ref[:, at:at + width] = ref[...]
            at += width
        du = jnp.zeros((tm, D), F32)
        for j in range(NSH):
            du = du + _nt(dp_ref[:, j * D:(j + 1) * D], w_ref[j])
        r, hr = _rms(h_ref[...])
        dx, dgain = _rms_bwd(du, hr, r, g_ref[...])
        dx_ref[...] = dh_ref[...] + dx

        @pl.when(pl.program_id(0) == 0)
        def _():
            dg_ref[...] = jnp.zeros_like(dg_ref)

        dg_ref[...] += dgain

    return _call(
        body, (dh, h, gain, *pieces, w_in), name="mix_bwd_in", grid=(S // tm,),
        in_specs=[_rows(tm, D), _rows(tm, D), _fixed((1, D))] + [_rows(tm, w) for w in widths] + [_ANY],
        out_specs=[_rows(tm, D), _fixed((1, D))],
        out_shape=[_sds((S, D), F32), _sds((1, D), F32)],
        scratch_shapes=_vmem_like(w_in) + [pltpu.VMEM((tm, 4 * D), BF16)],
        compiler_params=_params(("arbitrary",), 48), exchange=exchange)


def _wgrad_in(u, pieces):
    dxp, dq, dk, dv, dlg = pieces

    def body(u_ref, dxp_ref, dq_ref, dk_ref, dv_ref, dlg_ref, o_ref):
        j = pl.program_id(0)
        u = u_ref[...]

        def two(left_ref, right_ref):
            o_ref[:, :PW] = _tn(u, left_ref[...]).astype(BF16)
            o_ref[:, PW:] = _tn(u, right_ref[...]).astype(BF16)

        pl.when(j == 0)(lambda: two(dxp_ref, dq_ref))
        pl.when(j == 1)(lambda: two(dk_ref, dv_ref))

        @pl.when(j >= 2)
        def _():
            o_ref[...] = _tn(u, dlg_ref[...]).astype(BF16)

    whole = lambda width: pl.BlockSpec((S, width), lambda j: (0, 0))
    return _call(
        body, (u, dxp, dq, dk, dv, dlg), name="wgrad_in", grid=(NSH,),
        in_specs=[whole(D), whole(PW), whole(SBW), whole(SBW), whole(SBW),
                  pl.BlockSpec((S, D), lambda j: (0, jnp.maximum(j - 2, 0)))],
        out_specs=[pl.BlockSpec((None, D, D), lambda j: (j, 0, 0))], out_shape=[_sds((NSH, D, D), BF16)],
        compiler_params=_params(("arbitrary",), 56))[0]


def _wgrad(a, b, nblk, ti, name, out_dtype=BF16, exchange=None, after=()):
    ka, n = a.shape[1], b.shape[1]
    ns = n // nblk

    def body(a_ref, b_ref, o_ref):
        o_ref[...] = _tn(a_ref[...].astype(BF16), b_ref[...].astype(BF16)).astype(out_dtype)

    res = _call(
        body, (a, b), name=name, grid=(nblk, ka // ti),
        in_specs=[pl.BlockSpec((S, ti), lambda j, i: (0, i)), pl.BlockSpec((S, ns), lambda j, i: (0, j))],
        out_specs=[pl.BlockSpec((None, ti, ns), lambda j, i: (j, i, 0))],
        out_shape=[_sds((nblk, ka, ns), out_dtype)],
        compiler_params=_params(("arbitrary", "arbitrary"), 56), exchange=exchange, after=after)
    return res[0] if exchange is None else (res[0][0], res[1])


def _wgrad_branches(p, dyp, o_sb, dys, pm, dyg):
    cols = D // NSH

    def body(p_ref, dyp_ref, o_ref, dys_ref, pm_ref, dyg_ref, gbp_ref, gba_ref, gg_ref):
        gbp_ref[...] = _tn(p_ref[...], dyp_ref[...]).astype(BF16)
        gba_ref[...] = _tn(o_ref[...], dys_ref[...]).astype(BF16)
        gg_ref[...] = _tn(pm_ref[...], dyg_ref[...])

    whole = lambda width: pl.BlockSpec((S, width), lambda j: (0, 0))
    col = lambda width: pl.BlockSpec((S, width), lambda j: (0, j))
    return _call(
        body, (p, dyp, o_sb, dys, pm, dyg), name="wgrad_branches", grid=(NSH,),
        in_specs=[whole(PW), col(cols), whole(SBW), col(cols), col(PG), col(PG)],
        out_specs=[pl.BlockSpec((None, PW, cols), lambda j: (j, 0, 0)),
                   pl.BlockSpec((None, SBW, cols), lambda j: (j, 0, 0)),
                   pl.BlockSpec((None, PG, PG), lambda j: (j, 0, 0))],
        out_shape=[_sds((NSH, PW, cols), BF16), _sds((NSH, SBW, cols), BF16), _sds((NSH, PG, PG), F32)],
        compiler_params=_params(("arbitrary",), 40))


def _place():
    x, y, c = lax.axis_index("x"), lax.axis_index("y"), lax.axis_index("c")
    chips = [(1 - x, y), (x, 1 - y), (1 - x, 1 - y)]
    return x, y, c, chips


def _remote(src, dst, ssem, rsem, dev):
    return pltpu.make_async_remote_copy(src_ref=src, dst_ref=dst, send_sem=ssem, recv_sem=rsem,
                                        device_id=dev, device_id_type=MESH)


def _cast_into_block(ws, me_idx, name):
    steps = 4
    shapes = [(w.shape[0] // steps, w.shape[1]) for w in ws]

    def body(me_ref, *refs):
        for w_ref, o_ref in zip(refs[:len(ws)], refs[len(ws):]):
            o_ref[...] = w_ref[...].astype(BF16)

    return pl.pallas_call(
        body, name=name, out_shape=[_sds((NSH,) + w.shape, BF16) for w in ws],
        grid_spec=pltpu.PrefetchScalarGridSpec(
            num_scalar_prefetch=1, grid=(steps,),
            in_specs=[pl.BlockSpec((r, c), lambda s, me: (s, 0)) for r, c in shapes],
            out_specs=[pl.BlockSpec((None, r, c), lambda s, me: (me[0], s, 0)) for r, c in shapes]),
        compiler_params=_params(("arbitrary",), 32),
    )(me_idx, *ws)


def _ex_gather(bufs):
    n = len(bufs)
    per = 8

    def plan(outs, ssem, rsem, w):
        x, y, c, _ = _place()
        sib, nbr_x, nbr_y = (x, y, 1 - c), (1 - x, y, c), (x, 1 - y, c)
        half = outs[w].shape[1] // 2
        quarter = half // 2
        sem = lambda k: (ssem.at[per * w + k], rsem.at[per * w + k])
        rows = lambda blk, start, size: outs[w].at[blk, pl.ds(start, size)]
        mine = rows(2 * x + y, c * half, half)
        from_x = rows(2 * (1 - x) + y, c * half, half)
        from_y = rows(2 * x + (1 - y), c * half, half)
        diag = 2 * (1 - x) + (1 - y)
        pass_y = rows(2 * (1 - x) + y, c * half, quarter)
        pass_x = rows(2 * x + (1 - y), c * half + quarter, quarter)
        diag_0, diag_1 = rows(diag, c * half, quarter), rows(diag, c * half + quarter, quarter)
        first = [_remote(mine, mine, *sem(0), nbr_x), _remote(mine, mine, *sem(1), nbr_y)]
        arrivals = [
            (_remote(from_x, from_x, *sem(0), nbr_x),
             [_remote(pass_y, pass_y, *sem(2), nbr_y), _remote(from_x, from_x, *sem(4), sib)]),
            (_remote(from_y, from_y, *sem(1), nbr_y),
             [_remote(pass_x, pass_x, *sem(3), nbr_x), _remote(from_y, from_y, *sem(5), sib)]),
            (_remote(diag_0, diag_0, *sem(2), nbr_y), [_remote(diag_0, diag_0, *sem(6), sib)]),
            (_remote(diag_1, diag_1, *sem(3), nbr_x), [_remote(diag_1, diag_1, *sem(7), sib)]),
        ]
        other = (1 - c) * half
        from_sibling = [
            _remote(rows(2 * (1 - x) + y, other, half), rows(2 * (1 - x) + y, other, half), *sem(4), sib),
            _remote(rows(2 * x + (1 - y), other, half), rows(2 * x + (1 - y), other, half), *sem(5), sib),
            _remote(rows(diag, other, quarter), rows(diag, other, quarter), *sem(6), sib),
            _remote(rows(diag, other + quarter, quarter), rows(diag, other + quarter, quarter), *sem(7), sib),
        ]
        return first, arrivals, from_sibling

    def start(ins, outs, ssem, rsem):
        x, y, c, _ = _place()
        for w in range(n):
            half = outs[w].shape[1] // 2
            mine = outs[w].at[2 * x + y, pl.ds(c * half, half)]
            _remote(mine, mine, ssem.at[per * w], rsem.at[per * w], (1 - x, y, c)).start()
            _remote(mine, mine, ssem.at[per * w + 1], rsem.at[per * w + 1], (x, 1 - y, c)).start()

    def finish(ins, outs, ssem, rsem):
        plans = [plan(outs, ssem, rsem, w) for w in range(n)]
        started = []
        for direct in (True, False):
            for first, arrivals, _ in plans:
                for arrived, onward in (arrivals[:2] if direct else arrivals[2:]):
                    arrived.wait_recv()
                    for cp in onward:
                        cp.start()
                    started += onward
        for first, _, from_sibling in plans:
            for cp in from_sibling:
                cp.wait_recv()
            started += first
        for cp in started:
            cp.wait_send()

    return Exchange(bufs, [_sds(b.shape, b.dtype) for b in bufs], {w: w for w in range(n)}, per * n, start, finish)


def _ex_gather_direct(bufs):
    n = len(bufs)

    def copies(outs, ssem, rsem, only_first=False):
        x, y, c, chips = _place()
        me, sib = 2 * x + y, (x, y, 1 - c)
        first, relay, last = [], [], []
        for w in range(n):
            half = outs[w].shape[1] // 2
            mine = outs[w].at[me, pl.ds(c * half, half)]
            for k, (px, py) in enumerate(chips):
                sems = (ssem.at[6 * w + k], rsem.at[6 * w + k])
                sib_sems = (ssem.at[6 * w + 3 + k], rsem.at[6 * w + 3 + k])
                first.append(_remote(mine, mine, *sems, (px, py, c)))
                if only_first:
                    continue
                got = outs[w].at[2 * px + py, pl.ds(c * half, half)]
                relay.append((_remote(got, got, *sems, (px, py, c)), _remote(got, got, *sib_sems, sib)))
                theirs = outs[w].at[2 * px + py, pl.ds((1 - c) * half, half)]
                last.append(_remote(theirs, theirs, *sib_sems, sib))
        return first, relay, last

    def start(ins, outs, ssem, rsem):
        for cp in copies(outs, ssem, rsem, only_first=True)[0]:
            cp.start()

    def finish(ins, outs, ssem, rsem):
        first, relay, last = copies(outs, ssem, rsem)
        for arrived, onward in relay:
            arrived.wait_recv()
            onward.start()
        for cp in last:
            cp.wait_recv()
        for cp in first:
            cp.wait_send()
        for _, onward in relay:
            onward.wait_send()

    return Exchange(bufs, [_sds(b.shape, b.dtype) for b in bufs], {w: w for w in range(n)}, 6 * n, start, finish)


def _simple_exchange(arrays, landing, aliases, make_copies, sibling_only=False):
    def start(ins, outs, ssem, rsem):
        for cp, _ in make_copies(ins, outs, ssem, rsem, False):
            cp.start()

    def finish(ins, outs, ssem, rsem):
        cps = make_copies(ins, outs, ssem, rsem, True)
        for _, landed in cps:
            landed.wait_recv()
        for cp, _ in cps:
            cp.wait_send()

    return Exchange(arrays, landing, aliases, len(arrays) * 3, start, finish, sibling_only)


def _ex_pair_swap(grads):
    def make(ins, outs, ssem, rsem, landing):
        x, y, c, _ = _place()
        cps = [_remote(ins[w].at[:, 1 - c], outs[w], ssem.at[w], rsem.at[w], (x, y, 1 - c))
               for w in range(len(grads))]
        return [(cp, cp) for cp in cps]

    return _simple_exchange(grads, [_sds((NSH,) + g.shape[2:], g.dtype) for g in grads], {}, make, True)


def _ex_relay(bufs):
    def make(ins, outs, ssem, rsem, landing):
        x, y, c, chips = _place()
        sib = (x, y, 1 - c)
        out = []
        for w in range(len(bufs)):
            half = outs[w].shape[1] // 2
            for k, (px, py) in enumerate(chips):
                sems = (ssem.at[3 * w + k], rsem.at[3 * w + k])
                have = outs[w].at[2 * px + py, pl.ds(c * half, half)]
                miss = outs[w].at[2 * px + py, pl.ds((1 - c) * half, half)]
                out.append((_remote(have, have, *sems, sib), _remote(miss, miss, *sems, sib) if landing else None))
        return out

    return _simple_exchange(bufs, [_sds(b.shape, b.dtype) for b in bufs], {w: w for w in range(len(bufs))}, make, True)


def _ex_share(bufs):
    def make(ins, outs, ssem, rsem, landing):
        x, y, c, _ = _place()
        sib = (x, y, 1 - c)
        return [(_remote(outs[w].at[c], outs[w].at[c], ssem.at[w], rsem.at[w], sib),
                 _remote(outs[w].at[1 - c], outs[w].at[1 - c], ssem.at[w], rsem.at[w], sib) if landing else None)
                for w in range(len(bufs))]

    return _simple_exchange(bufs, [_sds(b.shape, b.dtype) for b in bufs], {w: w for w in range(len(bufs))}, make, True)


def _small_copies(slots, ssems, rsems, sending):
    x, y, c, _ = _place()
    out = []
    for m in range(1, 8):
        px, py, pc = x ^ (m >> 2), y ^ ((m >> 1) & 1), c ^ (m & 1)
        slot = slots.at[4 * x + 2 * y + c if sending else 4 * px + 2 * py + pc]
        out.append(_remote(slot, slot, ssems[m - 1], rsems[m - 1], (px, py, pc)))
    return out


def _small_gather_start(slots, name):
    def body(*refs):
        for cp in _small_copies(refs[0], refs[1:8], refs[8:15], True):
            cp.start()
        refs[-1][...] = jnp.zeros_like(refs[-1])

    outs = pl.pallas_call(
        body, name=name,
        out_shape=([pltpu.SemaphoreType.DMA(())] * 14 + [pltpu.HBM(slots.shape, slots.dtype)]
                   + [jax.ShapeDtypeStruct((8, 128), F32)]),
        in_specs=[_HBM], out_specs=[_SEM] * 14 + [_HBM, _VM], input_output_aliases={0: 14},
        compiler_params=pltpu.CompilerParams(has_side_effects=_EFFECT),
    )(*_in_hbm([slots]))
    return outs[:14], outs[14], outs[15]


def _small_gather_wait(sems, slots, after, name):
    def body(*refs):
        for cp in _small_copies(refs[0], refs[1:8], refs[8:15], True):
            cp.wait_send()
        for cp in _small_copies(refs[0], refs[1:8], refs[8:15], False):
            cp.wait_recv()

    return pl.pallas_call(
        body, name=name, out_shape=pltpu.HBM(slots.shape, slots.dtype),
        in_specs=[_HBM] + [_SEM] * 14 + [_ANY] * len(after), out_specs=_HBM, input_output_aliases={0: 0},
        compiler_params=pltpu.CompilerParams(has_side_effects=_EFFECT),
    )(slots, *sems, *after)


def _pair_sum(grads, gots, c_idx, name):
    n = len(grads)

    def body(c_ref, *refs):
        for a_ref, b_ref, o_ref in zip(refs[:n], refs[n:2 * n], refs[2 * n:]):
            o_ref[...] = (a_ref[...].astype(F32) + b_ref[...].astype(F32)).astype(BF16)

    halves = [g.shape[2:] for g in grads]
    return list(pl.pallas_call(
        body, name=name, out_shape=[_sds((NSH,) + h, BF16) for h in halves],
        grid_spec=pltpu.PrefetchScalarGridSpec(
            num_scalar_prefetch=1, grid=(NSH,),
            in_specs=[pl.BlockSpec((None, None) + h, lambda j, c: (j, c[0], 0, 0)) for h in halves]
            + [pl.BlockSpec((None,) + h, lambda j, c: (j, 0, 0)) for h in halves],
            out_specs=[pl.BlockSpec((None,) + h, lambda j, c: (j, 0, 0)) for h in halves]),
        compiler_params=_params(("arbitrary",), 40),
    )(c_idx, *_in_hbm(list(grads) + list(gots))))


def _chip_sum(owns, gots, place, name):
    n = len(owns)

    def body(place_ref, *refs):
        for own_ref, got_ref, o_ref in zip(refs[:n], refs[n:2 * n], refs[2 * n:]):
            acc = own_ref[...].astype(F32)
            for k in range(3):
                acc = acc + got_ref[k].astype(F32)
            o_ref[...] = acc

    shapes = [(o.shape[1] // 2, o.shape[2]) for o in owns]
    return list(pl.pallas_call(
        body, name=name, out_shape=[_sds((2, 2 * r, c), F32) for r, c in shapes],
        grid_spec=pltpu.PrefetchScalarGridSpec(
            num_scalar_prefetch=1, grid=(2,),
            in_specs=[pl.BlockSpec((None, r, c), lambda s, p: (p[0], s, 0)) for r, c in shapes]
            + [pl.BlockSpec((3, r, c), lambda s, p: (0, s, 0)) for r, c in shapes],
            out_specs=[pl.BlockSpec((None, r, c), lambda s, p: (p[1], s, 0)) for r, c in shapes]),
        compiler_params=_params(("arbitrary",), 40),
    )(place, *_in_hbm(list(owns) + list(gots))))


def _adamw_math(w, g, m, v):
    m = B1 * m + (1.0 - B1) * g
    v = B2 * v + (1.0 - B2) * (g * g)
    m_hat = m / (1.0 - B1 ** STEP)
    v_hat = v / (1.0 - B2 ** STEP)
    return -LR * (m_hat / (jnp.sqrt(v_hat) + AEPS) + WD * w), m, v


def _adamw(ws, gs, ms, vs, name, after=()):
    n, steps = len(ws), 4

    def body(*refs):
        ins, outs = refs[:4 * n], refs[4 * n:]
        for i in range(n):
            w_ref, g_ref, m_ref, v_ref = ins[4 * i:4 * i + 4]
            go_ref, d_ref, nm_ref, nv_ref = outs[4 * i:4 * i + 4]
            g = g_ref[...]
            go_ref[...] = g
            d_ref[...], nm_ref[...], nv_ref[...] = _adamw_math(w_ref[...], g, m_ref[...], v_ref[...])

    args, specs, shapes, free = [], [], [], []
    for i, (w, g, m, v) in enumerate(zip(ws, gs, ms, vs)):
        args += [w, g, m, v]
        specs += [pl.BlockSpec((w.shape[0] // steps, w.shape[1]), lambda r: (r, 0))] * 4
        shapes += [_sds(w.shape, F32)] * 4
        free += [4 * i, 4 * i + 2, 4 * i + 3]
    outs = _call(body, args, name=name, grid=(steps,), out_shape=shapes, in_specs=specs, out_specs=specs,
                 compiler_params=_params(("arbitrary",), 48), free=tuple(free), after=after)
    return [outs[4 * i:4 * i + 4] for i in range(n)]


def _small_update(gathered, w, m, v):
    rows = w.shape[0]

    def body(ga_ref, w_ref, m_ref, v_ref, *out_refs):
        g = ga_ref[0:rows, :]
        for dev in range(1, 8):
            g = g + ga_ref[dev * rows:(dev + 1) * rows, :]
        results = (g,) + _adamw_math(w_ref[...], g, m_ref[...], v_ref[...])
        for i, res in enumerate(results):
            out_refs[i][...] = res[:SMALL_HEAD, :]
            out_refs[4 + i][...] = res[SMALL_HEAD:, :]

    outs = pl.pallas_call(
        body, name="small_update",
        out_shape=[jax.ShapeDtypeStruct((SMALL_HEAD, 128), F32)] * 4
        + [jax.ShapeDtypeStruct((rows - SMALL_HEAD, 128), F32)] * 4,
        in_specs=[_VM] * 4, out_specs=[_VM] * 8,
    )(gathered, w, m, v)
    return outs[:4], outs[4:]


SMALL = ("ffn1_norm", "mix_norm", "ffn2_norm", "final_norm", "pool_scale", "loss", "pool_w_group")
SMALL_HEAD = 48
BIG = ("ffn1_w_gate_up", "ffn1_w_down", "w_in", "w_branch_pool", "w_branch_attn", "w_out",
       "ffn2_w_gate_up", "ffn2_w_down")
ORDER = ("ffn1_norm", "ffn1_w_gate_up", "ffn1_w_down", "mix_norm", "w_in", "pool_w_group", "pool_scale",
         "w_branch_pool", "w_branch_attn", "w_out", "ffn2_norm", "ffn2_w_gate_up", "ffn2_w_down", "final_norm")
SMALL_ROWS = 560


def _pack_small(t):
    parts = []
    for k in SMALL:
        rows = t[k].reshape(-1, 128) if k in t else jnp.zeros((1, 128), F32)
        parts.append(jnp.pad(rows, ((0, -rows.shape[0] % 8), (0, 0))))
    packed = jnp.concatenate(parts, axis=0)
    assert packed.shape == (SMALL_ROWS, 128), packed.shape
    return packed


def _unpack_small(head, group, like):
    out, at = {"pool_w_group": group.reshape(like["pool_w_group"].shape)}, 0
    for k in SMALL[:-1]:
        n = like[k].size // 128 if k in like else 1
        out[k] = head[at:at + n].reshape(like[k].shape) if k in like else head[at, 0]
        at += n + (-n % 8)
    return out


def _halves(g):
    return g.reshape(NSH, 2, g.shape[1] // 2, g.shape[2])


def kernel(x, ffn1_norm, ffn1_w_gate_up, ffn1_w_down, mix_norm, w_in, pool_w_group, pool_scale, w_branch_pool, w_branch_attn, w_out, ffn2_norm, ffn2_w_gate_up, ffn2_w_down, final_norm, loss_target, m_ffn1_norm, m_ffn1_w_gate_up, m_ffn1_w_down, m_mix_norm, m_w_in, m_pool_w_group, m_pool_scale, m_w_branch_pool, m_w_branch_attn, m_w_out, m_ffn2_norm, m_ffn2_w_gate_up, m_ffn2_w_down, m_final_norm, v_ffn1_norm, v_ffn1_w_gate_up, v_ffn1_w_down, v_mix_norm, v_w_in, v_pool_w_group, v_pool_scale, v_w_branch_pool, v_w_branch_attn, v_w_out, v_ffn2_norm, v_ffn2_w_gate_up, v_ffn2_w_down, v_final_norm):
    wts = dict(ffn1_norm=ffn1_norm, ffn1_w_gate_up=ffn1_w_gate_up, ffn1_w_down=ffn1_w_down, mix_norm=mix_norm,
               w_in=w_in, pool_w_group=pool_w_group, pool_scale=pool_scale, w_branch_pool=w_branch_pool,
               w_branch_attn=w_branch_attn, w_out=w_out, ffn2_norm=ffn2_norm, ffn2_w_gate_up=ffn2_w_gate_up,
               ffn2_w_down=ffn2_w_down, final_norm=final_norm)
    mom = dict(ffn1_norm=m_ffn1_norm, ffn1_w_gate_up=m_ffn1_w_gate_up, ffn1_w_down=m_ffn1_w_down,
               mix_norm=m_mix_norm, w_in=m_w_in, pool_w_group=m_pool_w_group, pool_scale=m_pool_scale,
               w_branch_pool=m_w_branch_pool, w_branch_attn=m_w_branch_attn, w_out=m_w_out,
               ffn2_norm=m_ffn2_norm, ffn2_w_gate_up=m_ffn2_w_gate_up, ffn2_w_down=m_ffn2_w_down,
               final_norm=m_final_norm)
    var = dict(ffn1_norm=v_ffn1_norm, ffn1_w_gate_up=v_ffn1_w_gate_up, ffn1_w_down=v_ffn1_w_down,
               mix_norm=v_mix_norm, w_in=v_w_in, pool_w_group=v_pool_w_group, pool_scale=v_pool_scale,
               w_branch_pool=v_w_branch_pool, w_branch_attn=v_w_branch_attn, w_out=v_w_out,
               ffn2_norm=v_ffn2_norm, ffn2_w_gate_up=v_ffn2_w_gate_up, ffn2_w_down=v_ffn2_w_down,
               final_norm=v_final_norm)

    c_idx = lax.axis_index("c").astype(jnp.int32).reshape(1)
    me_idx = (2 * lax.axis_index("x") + lax.axis_index("y")).astype(jnp.int32).reshape(1)
    place = jnp.concatenate([me_idx, c_idx])
    x0, tgt = x[0], loss_target[0]
    wgrp = pool_w_group[0].astype(BF16)
    g1, gm, g2, gf = ffn1_norm, mix_norm, ffn2_norm, final_norm.reshape(1, D)
    grad, delta, new_m, new_v = {}, {}, {}, {}

    def pair_sums(keys, parts, got):
        return _pair_sum(parts, got, c_idx, "pair_sum_" + keys[0])

    def chip_sums(keys, chip_parts, owned):
        return _chip_sum(chip_parts, owned, place, "chip_sum_" + keys[0])

    def adamw(keys, after=()):
        outs = _adamw([wts[k][0] for k in keys], [grad[k][0] for k in keys], [mom[k][0] for k in keys],
                      [var[k][0] for k in keys], "adamw_" + keys[0], after=after)
        for k, res in zip(keys, outs):
            grad[k], delta[k], new_m[k], new_v[k] = (o.reshape(wts[k].shape) for o in res)

    first, late = ("ffn1_w_gate_up", "ffn1_w_down"), ("w_branch_pool", "w_branch_attn", "w_out",
                                                       "ffn2_w_gate_up", "ffn2_w_down")
    own = {}
    for group in (first, ("w_in",), late):
        own.update(zip(group, _cast_into_block([wts[k][0] for k in group], me_idx, "cast_" + group[0])))
    full = dict(zip(first, _exchange_alone(_ex_gather([own[k] for k in first]), "gather_ffn1")))
    wgu1, wd1 = full["ffn1_w_gate_up"], full["ffn1_w_down"].reshape(DFF, D)
    (h1, n1, gu1, a1), (win,) = _ffn_fwd(x0, g1, wgu1, wd1, "ffn1_fwd", exchange=_ex_gather_direct([own["w_in"]]))
    sems_l, thru_l, token_l = _gather_start([own[k_] for k_ in late], [h1], "gather_late_start")
    u, xp, q, k, v, gp, gs = _mix_in(h1, gm, win, after=(token_l,))
    o_sb, ctot = _attn_fwd(q, k, v)
    arrived = _gather_wait(sems_l, thru_l, [o_sb], "gather_late_wait")
    wbp, wba, wout = _exchange_alone(_ex_relay(arrived[:3]), "relay_mix")
    wout = wout.reshape(D, D)
    (h2, pm, p, yp, ys, mm), (wgu2, wd2) = _mix_out(h1, xp, o_sb, gp, gs, wgrp, pool_scale, wbp, wba, wout,
                                                    exchange=_ex_relay(arrived[3:]))
    wd2 = wd2.reshape(DFF, D)
    dh3, loss_row, d_gf, n3, gu3, a3 = _ffn_fwd(h2, g2, wgu2, wd2, "ffn2_fwd", head=(tgt, gf))

    def grad_gate_up(n, dgu, name, exchange=None):
        res = _wgrad(n, dgu, NSH, D, name, exchange=exchange)
        return [_halves(res)] if exchange is None else ([_halves(res[0])], res[1])

    def grad_down(a, dh, name, exchange=None):
        res = _wgrad(a, dh, 1, FFS, name, exchange=exchange)
        halves = lambda g: [_halves(g.reshape(NSH, DFF // NSH, D))]
        return halves(res) if exchange is None else (halves(res[0]), res[1])

    k_gu2, k_d2, k_gu1, k_d1, k_in = (("ffn2_w_gate_up",), ("ffn2_w_down",), ("ffn1_w_gate_up",),
                                      ("ffn1_w_down",), ("w_in",))
    dh2, dgu3, d_g2 = _ffn_bwd(dh3, h2, g2, gu3, wgu2, wd2, "ffn2_bwd")
    pa = grad_gate_up(n3, dgu3, "wgrad_gu2") + grad_down(a3, dh3, "wgrad_d2")
    (dlg, dyp, dys, do_sb, dyg, dxp, d_scale), got_a = _mix_bwd_out(
        dh2, gp, gs, yp, ys, pm, wgrp, pool_scale, wbp, wba, wout, exchange=_ex_pair_swap(pa))
    chip_a = pair_sums(k_gu2 + k_d2, pa, got_a)
    kb = ("w_out", "w_branch_pool", "w_branch_attn")
    g_bp, g_ba, d_group = _wgrad_branches(p, dyp, o_sb, dys, pm, dyg)
    pb = [_halves(_wgrad(mm, dh2, 1, D, "wgrad_out").reshape(NSH, D // NSH, D)), _halves(g_bp), _halves(g_ba)]
    k_a, k_in = k_gu2 + k_d2, k_in + kb
    sems_a, thru_a, token_a = _scatter_start(chip_a, "scatter_a_start")
    dq, dk, dv = _attn_bwd(q, k, v, do_sb, ctot, after=(token_a,))
    chip_a, owned_a = _scatter_wait(sems_a, thru_a, [dq], "scatter_a_wait")
    halves_a = chip_sums(k_a, chip_a, owned_a)
    dproj = (dxp, dq, dk, dv, dlg)
    (dh1, d_gm), both_a = _mix_bwd_in(dh2, h1, gm, dproj, win, exchange=_ex_share(halves_a))
    for i, k_ in enumerate(k_a):
        grad[k_] = both_a[i].reshape(wts[k_].shape)

    p_in = [_halves(_wgrad_in(u, dproj))] + pb
    p_d1, got_in = grad_down(a1, dh1, "wgrad_d1", exchange=_ex_pair_swap(p_in))
    sems_in, thru_in, token_in = _scatter_start(pair_sums(k_in, p_in, got_in), "scatter_in_start")
    dgu1, got_d1 = _ffn_bwd_act(dh1, gu1, wd1, "ffn1_bwd_act", exchange=_ex_pair_swap(p_d1), after=(token_in,))
    sems_d1, thru_d1, token_d1 = _scatter_start(pair_sums(k_d1, p_d1, got_d1), "scatter_d1_start")
    p_gu1 = [_halves(_wgrad(n1, dgu1, NSH, D, "wgrad_gu1", after=(token_in, token_d1)))]
    sems_w, thru_w, token_w = _swap_start(p_gu1, "swap_gu1_start")
    chip_in, owned_in = _scatter_wait(sems_in, thru_in, [token_w], "scatter_in_wait")
    chip_d1, owned_d1 = _scatter_wait(sems_d1, thru_d1, [token_w], "scatter_d1_wait")
    halves_in = chip_sums(k_in, chip_in, owned_in)
    p_gu1, got_gu1 = _swap_wait(sems_w, thru_w, halves_in, "swap_gu1_wait")
    sems, thru, token = _scatter_start(pair_sums(k_gu1, p_gu1, got_gu1), "scatter_gu1_start")
    landed = _exchange_alone(_ex_share(halves_in), "share_in", after=(token,))
    for i, k_ in enumerate(k_in):
        grad[k_] = landed[i].reshape(wts[k_].shape)
    adamw(k_a, after=(token,))
    adamw(k_in, after=(token,))
    dx, d_g1 = _ffn_bwd_in(dh1, x0, g1, dgu1, wgu1, "ffn1_bwd_in", after=(token,))
    small_g = dict(ffn1_norm=d_g1, mix_norm=d_gm, ffn2_norm=d_g2, final_norm=d_gf, pool_scale=d_scale,
                   pool_w_group=d_group, loss=loss_row)
    dev = 4 * lax.axis_index("x") + 2 * lax.axis_index("y") + lax.axis_index("c")
    slots = lax.dynamic_update_slice(jnp.zeros((8, SMALL_ROWS, 128), F32), _pack_small(small_g)[None], (dev, 0, 0))
    sems_s, slots, token_s = _small_gather_start(slots, "small_gather_start")

    chip_gu1, owned_gu1 = _scatter_wait(sems, thru, [dx] + [delta[k_] for k_ in k_a + k_in], "scatter_gu1_wait")
    halves_last = chip_sums(k_d1 + k_gu1, chip_d1 + chip_gu1, owned_d1 + owned_gu1)
    both = _exchange_alone(_ex_share(halves_last), "share_last",
                           after=(token_s,))
    grad["ffn1_w_down"] = both[0].reshape(ffn1_w_down.shape)
    grad["ffn1_w_gate_up"] = both[1].reshape(ffn1_w_gate_up.shape)
    adamw(k_d1 + k_gu1, after=(token_s,))
    gathered = _small_gather_wait(sems_s, slots, [delta[k_] for k_ in k_d1 + k_gu1], "small_gather_wait")
    gathered = gathered.reshape(8 * SMALL_ROWS, 128)
    heads, groups = _small_update(gathered, _pack_small(wts), _pack_small(mom), _pack_small(var))
    for dst, head, group in zip((grad, delta, new_m, new_v), heads, groups):
        vals = _unpack_small(head, group, wts)
        if dst is grad:
            loss = vals["loss"]
        vals.pop("loss")
        dst.update(vals)
    return (loss, dx[None], *[grad[k_] for k_ in ORDER], *[delta[k_] for k_ in ORDER],
            *[new_m[k_] for k_ in ORDER], *[new_v[k_] for k_ in ORDER])
```

```python
import dataclasses
import functools

import jax
import jax.numpy as jnp
from jax import lax
from jax.experimental import pallas as pl
from jax.experimental.pallas import tpu as pltpu

F32 = jnp.float32
BF16 = jnp.bfloat16

S = 2048
D = 1024
DFF = 2816
FFS = 2 * DFF // 4
NSH = 4
PW = 512
PG = 128
POOL_WINDOWS = (2, 4, 8, 16)
HALO = 16
SBW = 512
DH = 64
EPS = 1e-6
SCALE = 0.125
LOG2E = 1.4426950408889634
TA = 256
QB = 2
MIB = 1024 * 1024

LR, B1, B2, AEPS, WD, STEP = 0.001, 0.9, 0.999, 1e-08, 0.01, 10

_VM = pl.BlockSpec(memory_space=pltpu.VMEM)
_ANY = pl.BlockSpec(memory_space=pl.ANY)
MESH = pl.DeviceIdType.MESH
SIBLING_PAIR_ID = 1


def _nn(a, b):
    return jnp.dot(a, b, preferred_element_type=F32)


def _nt(a, b):
    return lax.dot_general(a, b, (((1,), (1,)), ((), ())), preferred_element_type=F32)


def _tn(a, b):
    return lax.dot_general(a, b, (((0,), (0,)), ((), ())), preferred_element_type=F32)


def _params(sem, vmem_mib):
    return pltpu.CompilerParams(dimension_semantics=sem, vmem_limit_bytes=vmem_mib * MIB)


def _rows(tm, width):
    return pl.BlockSpec((tm, width), lambda i: (i, 0))


def _fixed(shape):
    return pl.BlockSpec(shape, lambda *_: (0,) * len(shape))


def _sds(shape, dtype):
    return pltpu.HBM(shape, dtype)


def _in_hbm(args):
    return [pltpu.with_memory_space_constraint(a, pltpu.HBM) for a in args]


def _stage(pairs):
    @pl.when(pl.program_id(0) == 0)
    def _():
        for src, dst in pairs:
            pltpu.sync_copy(src, dst)


def _vmem_like(*arrays):
    return [pltpu.VMEM(a.shape, a.dtype) for a in arrays]


class Exchange:
    def __init__(self, arrays, landing, aliases, n_sems, start, finish, sibling_only=False):
        self.arrays, self.landing, self.aliases, self.n_sems = list(arrays), list(landing), dict(aliases), n_sems
        self.start, self.finish = start, finish
        self.sibling_only = sibling_only

    def enter(self):
        if self.sibling_only:
            barrier = pltpu.get_barrier_semaphore()
            sibling = (lax.axis_index("x"), lax.axis_index("y"), 1 - lax.axis_index("c"))
            pl.semaphore_signal(barrier, inc=1, device_id=sibling, device_id_type=MESH)
            pl.semaphore_wait(barrier, 1)

    def params(self, compiler_params=None):
        kw = dict(collective_id=SIBLING_PAIR_ID) if self.sibling_only else {}
        if compiler_params is None:
            return pltpu.CompilerParams(**kw)
        return dataclasses.replace(compiler_params, **kw)


def _call(body, args, *, name, grid, in_specs, out_specs, out_shape, scratch_shapes=(), compiler_params=None,
          exchange=None, free=(), after=()):
    args = [a if i in free else pltpu.with_memory_space_constraint(a, pltpu.HBM) for i, a in enumerate(args)]
    if exchange is None:
        n_in = len(in_specs)

        def plain(*refs):
            body(*refs[:n_in], *refs[n_in + len(after):])

        return pl.pallas_call(plain, name=name, grid=grid, in_specs=list(in_specs) + [_ANY] * len(after),
                              out_specs=out_specs, out_shape=out_shape, scratch_shapes=list(scratch_shapes),
                              compiler_params=compiler_params)(*args, *after)
    ex = exchange
    n_in, n_out, n_scr = len(in_specs), len(out_specs), len(scratch_shapes)
    na, nl = len(ex.arrays), len(ex.landing)

    def hosted(*refs):
        at = [0]

        def take(n):
            at[0] += n
            return refs[at[0] - n:at[0]]

        k_in, _, e_in, k_out, e_out, k_scr = take(n_in), take(len(after)), take(na), take(n_out), take(nl), take(n_scr)
        ssem, rsem = take(2)
        ids = [pl.program_id(a) for a in range(len(grid))]
        first = functools.reduce(jnp.logical_and, [i == 0 for i in ids])
        last = functools.reduce(jnp.logical_and, [i == g - 1 for i, g in zip(ids, grid)])

        @pl.when(first)
        def _():
            ex.enter()
            ex.start(e_in, e_out, ssem, rsem)

        body(*k_in, *k_out, *k_scr)

        @pl.when(last)
        def _():
            ex.finish(e_in, e_out, ssem, rsem)

    outs = pl.pallas_call(
        hosted, name=name, grid=grid,
        in_specs=list(in_specs) + [_ANY] * (len(after) + na), out_specs=list(out_specs) + [_ANY] * nl,
        out_shape=list(out_shape) + ex.landing,
        scratch_shapes=list(scratch_shapes) + [pltpu.SemaphoreType.DMA((ex.n_sems,))] * 2,
        input_output_aliases={n_in + len(after) + i: n_out + j for i, j in ex.aliases.items()},
        compiler_params=ex.params(compiler_params),
    )(*args, *after, *_in_hbm(ex.arrays))
    return outs[:n_out], outs[n_out:]


def _exchange_alone(ex, name, after=()):
    na, nl = len(ex.arrays), len(ex.landing)

    def body(*refs):
        outs = refs[na + len(after):na + len(after) + nl]
        ex.enter()
        ex.start(refs[:na], outs, refs[-2], refs[-1])
        ex.finish(refs[:na], outs, refs[-2], refs[-1])

    return pl.pallas_call(
        body, name=name, in_specs=[_ANY] * (na + len(after)), out_specs=[_ANY] * nl,
        out_shape=ex.landing, scratch_shapes=[pltpu.SemaphoreType.DMA((ex.n_sems,))] * 2,
        input_output_aliases=ex.aliases, compiler_params=ex.params(),
    )(*_in_hbm(ex.arrays), *after)


_HBM = pl.BlockSpec(memory_space=pltpu.HBM)
_SEM = pl.BlockSpec(memory_space=pltpu.SEMAPHORE)
_EFFECT = pltpu.SideEffectType.DATAFLOW_SIDE_EFFECTING


def _scatter_copies(srcs, lands, ssems, rsems):
    x, y, c, chips = _place()
    return [_remote(srcs[w].at[2 * px + py], lands[w].at[k], ssems[3 * w + k], rsems[3 * w + k], (px, py, c))
            for w in range(len(srcs)) for k, (px, py) in enumerate(chips)]


def _scatter_start(parts, name):
    parts = list(parts)
    n, ncp = len(parts), 3 * len(parts)
    lands = [lax.empty((3,) + p.shape[1:], p.dtype) for p in parts]

    def body(*refs):
        srcs, land_refs = refs[:n], refs[n:2 * n]
        ssems, rsems = refs[2 * n:2 * n + ncp], refs[2 * n + ncp:2 * n + 2 * ncp]
        for cp in _scatter_copies(srcs, land_refs, ssems, rsems):
            cp.start()
        token = refs[-1]
        token[...] = jnp.zeros_like(token)

    outs = pl.pallas_call(
        body, name=name,
        out_shape=([pltpu.SemaphoreType.DMA(())] * (2 * ncp) + [pltpu.HBM(a.shape, a.dtype) for a in parts + lands]
                   + [jax.ShapeDtypeStruct((8, 128), F32)]),
        in_specs=[_HBM] * (2 * n), out_specs=[_SEM] * (2 * ncp) + [_HBM] * (2 * n) + [_VM],
        input_output_aliases={i: 2 * ncp + i for i in range(2 * n)},
        compiler_params=pltpu.CompilerParams(has_side_effects=_EFFECT),
    )(*_in_hbm(parts), *_in_hbm(lands))
    sems, thru, token = outs[:2 * ncp], outs[2 * ncp:2 * ncp + 2 * n], outs[-1]
    return sems, thru, token


def _scatter_wait(sems, thru, after, name):
    n = len(thru) // 2
    ncp = 3 * n

    def body(*refs):
        srcs, land_refs = refs[:n], refs[n:2 * n]
        ssems, rsems = refs[2 * n:2 * n + ncp], refs[2 * n + ncp:2 * n + 2 * ncp]
        for cp in _scatter_copies(srcs, land_refs, ssems, rsems):
            cp.wait_send()
            cp.wait_recv()

    outs = pl.pallas_call(
        body, name=name, out_shape=[pltpu.HBM(a.shape, a.dtype) for a in thru],
        in_specs=[_HBM] * (2 * n) + [_SEM] * (2 * ncp) + [_ANY] * len(after), out_specs=[_HBM] * (2 * n),
        input_output_aliases={i: i for i in range(2 * n)},
        compiler_params=pltpu.CompilerParams(has_side_effects=_EFFECT),
    )(*thru, *sems, *after)
    return outs[:n], outs[n:]


def _swap_copies(srcs, lands, ssems, rsems):
    x, y, c, _ = _place()
    return [_remote(srcs[w].at[:, 1 - c], lands[w], ssems[w], rsems[w], (x, y, 1 - c)) for w in range(len(srcs))]


def _swap_start(grads, name):
    grads = list(grads)
    n = len(grads)
    lands = [lax.empty((NSH,) + g.shape[2:], g.dtype) for g in grads]

    def body(*refs):
        barrier = pltpu.get_barrier_semaphore()
        sibling = (lax.axis_index("x"), lax.axis_index("y"), 1 - lax.axis_index("c"))
        pl.semaphore_signal(barrier, inc=1, device_id=sibling, device_id_type=MESH)
        pl.semaphore_wait(barrier, 1)
        for cp in _swap_copies(refs[:n], refs[n:2 * n], refs[2 * n:3 * n], refs[3 * n:4 * n]):
            cp.start()
        refs[-1][...] = jnp.zeros_like(refs[-1])

    outs = pl.pallas_call(
        body, name=name,
        out_shape=([pltpu.SemaphoreType.DMA(())] * (2 * n) + [pltpu.HBM(a.shape, a.dtype) for a in grads + lands]
                   + [jax.ShapeDtypeStruct((8, 128), F32)]),
        in_specs=[_HBM] * (2 * n), out_specs=[_SEM] * (2 * n) + [_HBM] * (2 * n) + [_VM],
        input_output_aliases={i: 2 * n + i for i in range(2 * n)},
        compiler_params=pltpu.CompilerParams(has_side_effects=_EFFECT, collective_id=SIBLING_PAIR_ID),
    )(*_in_hbm(grads), *_in_hbm(lands))
    return outs[:2 * n], outs[2 * n:4 * n], outs[-1]


def _swap_wait(sems, thru, after, name):
    n = len(thru) // 2

    def body(*refs):
        for cp in _swap_copies(refs[:n], refs[n:2 * n], refs[2 * n:3 * n], refs[3 * n:4 * n]):
            cp.wait_send()
            cp.wait_recv()

    outs = pl.pallas_call(
        body, name=name, out_shape=[pltpu.HBM(a.shape, a.dtype) for a in thru],
        in_specs=[_HBM] * (2 * n) + [_SEM] * (2 * n) + [_ANY] * len(after), out_specs=[_HBM] * (2 * n),
        input_output_aliases={i: i for i in range(2 * n)},
        compiler_params=pltpu.CompilerParams(has_side_effects=_EFFECT),
    )(*thru, *sems, *after)
    return outs[:n], outs[n:]


def _share_copies(bufs, ssems, rsems, sending):
    x, y, c, _ = _place()
    out = []
    for w, ref in enumerate(bufs):
        slot = ref.at[c if sending else 1 - c]
        out.append(_remote(slot, slot, ssems[w], rsems[w], (x, y, 1 - c)))
    return out


def _share_start(bufs, after, name):
    bufs = list(bufs)
    n = len(bufs)

    def body(*refs):
        barrier = pltpu.get_barrier_semaphore()
        sibling = (lax.axis_index("x"), lax.axis_index("y"), 1 - lax.axis_index("c"))
        pl.semaphore_signal(barrier, inc=1, device_id=sibling, device_id_type=MESH)
        pl.semaphore_wait(barrier, 1)
        at = n + len(after)
        for cp in _share_copies(refs[:n], refs[at:at + n], refs[at + n:at + 2 * n], True):
            cp.start()
        refs[-1][...] = jnp.zeros_like(refs[-1])

    outs = pl.pallas_call(
        body, name=name,
        out_shape=([pltpu.SemaphoreType.DMA(())] * (2 * n) + [pltpu.HBM(a.shape, a.dtype) for a in bufs]
                   + [jax.ShapeDtypeStruct((8, 128), F32)]),
        in_specs=[_HBM] * n + [_ANY] * len(after), out_specs=[_SEM] * (2 * n) + [_HBM] * n + [_VM],
        input_output_aliases={i: 2 * n + i for i in range(n)},
        compiler_params=pltpu.CompilerParams(has_side_effects=_EFFECT, collective_id=SIBLING_PAIR_ID),
    )(*_in_hbm(bufs), *after)
    return outs[:2 * n], outs[2 * n:3 * n], outs[-1]


def _share_wait(sems, thru, after, name):
    n = len(thru)

    def body(*refs):
        for cp in _share_copies(refs[:n], refs[n:2 * n], refs[2 * n:3 * n], True):
            cp.wait_send()
        for cp in _share_copies(refs[:n], refs[n:2 * n], refs[2 * n:3 * n], False):
            cp.wait_recv()

    return pl.pallas_call(
        body, name=name, out_shape=[pltpu.HBM(a.shape, a.dtype) for a in thru],
        in_specs=[_HBM] * n + [_SEM] * (2 * n) + [_ANY] * len(after), out_specs=[_HBM] * n,
        input_output_aliases={i: i for i in range(n)},
        compiler_params=pltpu.CompilerParams(has_side_effects=_EFFECT),
    )(*thru, *sems, *after)


def _gather_copies(bufs, ssems, rsems, sending):
    x, y, c, chips = _place()
    out = []
    for w, ref in enumerate(bufs):
        half = ref.shape[1] // 2
        for k, (px, py) in enumerate(chips):
            rows = ref.at[2 * x + y if sending else 2 * px + py, pl.ds(c * half, half)]
            out.append(_remote(rows, rows, ssems[3 * w + k], rsems[3 * w + k], (px, py, c)))
    return out


def _gather_start(bufs, after, name):
    n, ncp = len(bufs), 3 * len(bufs)

    def body(*refs):
        ssems, rsems = refs[n + len(after):n + len(after) + ncp], refs[n + len(after) + ncp:n + len(after) + 2 * ncp]
        for cp in _gather_copies(refs[:n], ssems, rsems, True):
            cp.start()
        token = refs[-1]
        token[...] = jnp.zeros_like(token)

    outs = pl.pallas_call(
        body, name=name,
        out_shape=([pltpu.SemaphoreType.DMA(())] * (2 * ncp) + [pltpu.HBM(a.shape, a.dtype) for a in bufs]
                   + [jax.ShapeDtypeStruct((8, 128), F32)]),
        in_specs=[_HBM] * n + [_ANY] * len(after), out_specs=[_SEM] * (2 * ncp) + [_HBM] * n + [_VM],
        input_output_aliases={i: 2 * ncp + i for i in range(n)},
        compiler_params=pltpu.CompilerParams(has_side_effects=_EFFECT),
    )(*_in_hbm(bufs), *after)
    return outs[:2 * ncp], outs[2 * ncp:2 * ncp + n], outs[-1]


def _gather_wait(sems, thru, after, name):
    n = len(thru)
    ncp = 3 * n

    def body(*refs):
        ssems, rsems = refs[n:n + ncp], refs[n + ncp:n + 2 * ncp]
        for cp in _gather_copies(refs[:n], ssems, rsems, True):
            cp.wait_send()
        for cp in _gather_copies(refs[:n], ssems, rsems, False):
            cp.wait_recv()

    return pl.pallas_call(
        body, name=name, out_shape=[pltpu.HBM(a.shape, a.dtype) for a in thru],
        in_specs=[_HBM] * n + [_SEM] * (2 * ncp) + [_ANY] * len(after), out_specs=[_HBM] * n,
        input_output_aliases={i: i for i in range(n)},
        compiler_params=pltpu.CompilerParams(has_side_effects=_EFFECT),
    )(*thru, *sems, *after)


def _rms(x):
    r = lax.rsqrt(jnp.mean(x * x, axis=-1, keepdims=True) + EPS)
    return r, x * r


def _rms_bwd(dn, xr, r, gain):
    dng = dn * gain
    dx = r * (dng - xr * jnp.mean(dng * xr, axis=-1, keepdims=True))
    return dx, jnp.sum(dn * xr, axis=0, keepdims=True)


def _ffn_fwd(x, gain, wgu, wd, name, exchange=None, head=None):
    tm = 256

    def body(x_ref, g_ref, wgu_hbm, wd_hbm, *rest):
        if head is None:
            h_ref, n_ref, gu_ref, a_ref, wgu_ref, wd_ref = rest
        else:
            t_ref, gf_ref, h_ref, loss_ref, dgf_ref, n_ref, gu_ref, a_ref, wgu_ref, wd_ref = rest
        _stage([(wgu_hbm, wgu_ref), (wd_hbm, wd_ref)])
        x = x_ref[...]
        _, xr = _rms(x)
        n = (xr * g_ref[...]).astype(BF16)
        n_ref[...] = n
        acc = jnp.zeros((tm, D), F32)
        for j in range(2):
            g = _nn(n, wgu_ref[j])
            u = _nn(n, wgu_ref[2 + j])
            gu_ref[:, j * FFS:(j + 1) * FFS] = g.astype(BF16)
            gu_ref[:, (2 + j) * FFS:(3 + j) * FFS] = u.astype(BF16)
            half_act = (0.5 * (g * jax.nn.sigmoid(g) * u)).astype(BF16)
            a_ref[:, j * FFS:(j + 1) * FFS] = half_act
            acc = acc + _nn(half_act, wd_ref[j * FFS:(j + 1) * FFS, :])
        h = x + acc
        if head is None:
            h_ref[...] = h
            return
        gf = gf_ref[...]
        r, hr = _rms(h)
        err = hr * gf - t_ref[...]
        dh, dgain = _rms_bwd(err * (1.0 / D), hr, r, gf)
        h_ref[...] = dh

        @pl.when(pl.program_id(0) == 0)
        def _():
            dgf_ref[...] = jnp.zeros_like(dgf_ref)
            loss_ref[...] = jnp.zeros_like(loss_ref)

        dgf_ref[...] += dgain
        loss_ref[...] += jnp.full((1, 128), (0.5 / D) * jnp.sum(err * err), F32)

    saved_specs = [_rows(tm, D), _rows(tm, 4 * FFS), _rows(tm, DFF)]
    saved_shapes = [_sds((S, D), BF16), _sds((S, 4 * FFS), BF16), _sds((S, DFF), BF16)]
    if head is None:
        return _call(
            body, (x, gain, wgu, wd), name=name, grid=(S // tm,),
            in_specs=[_rows(tm, D), _fixed((1, D)), _ANY, _ANY],
            out_specs=[_rows(tm, D)] + saved_specs, out_shape=[_sds((S, D), F32)] + saved_shapes,
            scratch_shapes=_vmem_like(wgu, wd),
            compiler_params=_params(("arbitrary",), 56), exchange=exchange)
    return _call(
        body, (x, gain, wgu, wd, *head), name=name, grid=(S // tm,),
        in_specs=[_rows(tm, D), _fixed((1, D)), _ANY, _ANY, _rows(tm, D), _fixed((1, D))],
        out_specs=[_rows(tm, D), _fixed((1, 128)), _fixed((1, D))] + saved_specs,
        out_shape=[_sds((S, D), F32), _sds((1, 128), F32), _sds((1, D), F32)] + saved_shapes,
        scratch_shapes=_vmem_like(wgu, wd),
        compiler_params=_params(("arbitrary",), 56), exchange=exchange, free=(4, 5))


def _ffn_bwd(dh, x, gain, gu, wgu, wd, name):
    tm = 256

    def body(dh_ref, x_ref, g_ref, gu_ref, wgu_hbm, wd_hbm, dx_ref, dgu_ref, dg_ref, wgu_ref, wd_ref):
        _stage([(wgu_hbm, wgu_ref), (wd_hbm, wd_ref)])
        dh = dh_ref[...]
        dhb = dh.astype(BF16)
        dn = jnp.zeros((tm, D), F32)
        for j in range(2):
            g = gu_ref[:, j * FFS:(j + 1) * FFS].astype(F32)
            u = gu_ref[:, (2 + j) * FFS:(3 + j) * FFS].astype(F32)
            da = 0.5 * _nt(dhb, wd_ref[j * FFS:(j + 1) * FFS, :])
            sg = jax.nn.sigmoid(g)
            dgb = (da * u * (sg * (1.0 + g * (1.0 - sg)))).astype(BF16)
            dub = (da * (g * sg)).astype(BF16)
            dgu_ref[:, j * FFS:(j + 1) * FFS] = dgb
            dgu_ref[:, (2 + j) * FFS:(3 + j) * FFS] = dub
            dn = dn + _nt(dgb, wgu_ref[j]) + _nt(dub, wgu_ref[2 + j])
        r, xr = _rms(x_ref[...])
        dx, dgain = _rms_bwd(dn, xr, r, g_ref[...])
        dx_ref[...] = dh + dx

        @pl.when(pl.program_id(0) == 0)
        def _():
            dg_ref[...] = jnp.zeros_like(dg_ref)

        dg_ref[...] += dgain

    return _call(
        body, (dh, x, gain, gu, wgu, wd), name=name, grid=(S // tm,),
        in_specs=[_rows(tm, D), _rows(tm, D), _fixed((1, D)), _rows(tm, 4 * FFS), _ANY, _ANY],
        out_specs=[_rows(tm, D), _rows(tm, 4 * FFS), _fixed((1, D))],
        out_shape=[_sds((S, D), F32), _sds((S, 4 * FFS), BF16), _sds((1, D), F32)],
        scratch_shapes=_vmem_like(wgu, wd), compiler_params=_params(("arbitrary",), 56))


def _ffn_bwd_act(dh, gu, wd, name, exchange=None, after=()):
    tm = 512

    def body(dh_ref, gu_ref, wd_hbm, dgu_ref, wd_ref):
        _stage([(wd_hbm, wd_ref)])
        dhb = dh_ref[...].astype(BF16)
        for j in range(2):
            g = gu_ref[:, j * FFS:(j + 1) * FFS].astype(F32)
            u = gu_ref[:, (2 + j) * FFS:(3 + j) * FFS].astype(F32)
            da = 0.5 * _nt(dhb, wd_ref[j * FFS:(j + 1) * FFS, :])
            sg = jax.nn.sigmoid(g)
            dgu_ref[:, j * FFS:(j + 1) * FFS] = (da * u * (sg * (1.0 + g * (1.0 - sg)))).astype(BF16)
            dgu_ref[:, (2 + j) * FFS:(3 + j) * FFS] = (da * (g * sg)).astype(BF16)

    res = _call(
        body, (dh, gu, wd), name=name, grid=(S // tm,),
        in_specs=[_rows(tm, D), _rows(tm, 4 * FFS), _ANY], out_specs=[_rows(tm, 4 * FFS)],
        out_shape=[_sds((S, 4 * FFS), BF16)], scratch_shapes=_vmem_like(wd),
        compiler_params=_params(("arbitrary",), 56), exchange=exchange, after=after)
    return res[0] if exchange is None else (res[0][0], res[1])


def _ffn_bwd_in(dh, x, gain, dgu, wgu, name, exchange=None, after=()):
    tm = 512

    def body(dh_ref, x_ref, g_ref, dgu_ref, wgu_hbm, dx_ref, dg_ref, wgu_ref):
        _stage([(wgu_hbm, wgu_ref)])
        dn = jnp.zeros((tm, D), F32)
        for j in range(NSH):
            dn = dn + _nt(dgu_ref[:, j * FFS:(j + 1) * FFS], wgu_ref[j])
        r, xr = _rms(x_ref[...])
        dx, dgain = _rms_bwd(dn, xr, r, g_ref[...])
        dx_ref[...] = dh_ref[...] + dx

        @pl.when(pl.program_id(0) == 0)
        def _():
            dg_ref[...] = jnp.zeros_like(dg_ref)

        dg_ref[...] += dgain

    return _call(
        body, (dh, x, gain, dgu, wgu), name=name, grid=(S // tm,),
        in_specs=[_rows(tm, D), _rows(tm, D), _fixed((1, D)), _rows(tm, 4 * FFS), _ANY],
        out_specs=[_rows(tm, D), _fixed((1, D))],
        out_shape=[_sds((S, D), F32), _sds((1, D), F32)],
        scratch_shapes=_vmem_like(wgu),
        compiler_params=_params(("arbitrary",), 56), exchange=exchange, after=after)


def _mix_in(h, gain, w_in, after=()):
    tm = 512

    def body(h_ref, g_ref, w_hbm, u_ref, xp_ref, q_ref, k_ref, v_ref, gp_ref, gs_ref, w_ref):
        _stage([(w_hbm, w_ref)])
        _, hr = _rms(h_ref[...])
        u = (hr * g_ref[...]).astype(BF16)
        u_ref[...] = u
        p0 = _nn(u, w_ref[0])
        xp_ref[...] = p0[:, :PW]
        q_ref[...] = p0[:, PW:].astype(BF16)
        p1 = _nn(u, w_ref[1])
        k_ref[...] = p1[:, :SBW].astype(BF16)
        v_ref[...] = p1[:, SBW:].astype(BF16)
        gp_ref[...] = jax.nn.sigmoid(_nn(u, w_ref[2])).astype(BF16)
        gs_ref[...] = jax.nn.sigmoid(_nn(u, w_ref[3])).astype(BF16)

    return _call(
        body, (h, gain, w_in), name="mix_in", grid=(S // tm,),
        in_specs=[_rows(tm, D), _fixed((1, D)), _ANY],
        out_specs=[_rows(tm, D), _rows(tm, PW), _rows(tm, SBW), _rows(tm, SBW), _rows(tm, SBW),
                   _rows(tm, D), _rows(tm, D)],
        out_shape=[_sds((S, D), BF16), _sds((S, PW), F32), _sds((S, SBW), BF16), _sds((S, SBW), BF16),
                   _sds((S, SBW), BF16), _sds((S, D), BF16), _sds((S, D), BF16)],
        scratch_shapes=_vmem_like(w_in),
        compiler_params=_params(("arbitrary",), 48), free=(1,), after=after)


def _hilo_dot(x, tri):
    hi = x.astype(BF16)
    lo = (x - hi.astype(F32)).astype(BF16)
    return _nn(hi, tri) + _nn(lo, tri)


def _log_terms(qk):
    z2 = qk * (SCALE * LOG2E)
    lb = jnp.minimum(z2, 0.0) - jnp.log2(1.0 + jnp.exp2(-jnp.abs(z2)))
    return lb, lb - z2


def _head_masks():
    lane = lax.broadcasted_iota(jnp.int32, (1, 2 * DH), 1)
    return (lane < DH, lane >= DH)


def _attn_fwd(q, k, v, exchange=None):
    T = TA

    def body(q_ref, k_ref, v_ref, o_ref, c_ref):
        i2 = 2 * pl.program_id(1)
        row = lax.broadcasted_iota(jnp.int32, (T, T), 0)
        col = lax.broadcasted_iota(jnp.int32, (T, T), 1)
        after = (row > col).astype(BF16)
        causal = col < row
        masks = _head_masks()
        qms = {}
        for b in range(QB):
            q2 = q_ref[b * T:(b + 1) * T, :]
            for h, hm in enumerate(masks):
                qms[b, h] = jnp.where(hm, q2, jnp.zeros_like(q2))

        def blocks(keys, pairs, carries, os):
            ks, vms = [], []
            for j in keys:
                rows = pl.ds(pl.multiple_of(j * T, T), T)
                vj = v_ref[rows, :]
                ks.append(k_ref[rows, :])
                vms.append([jnp.where(hm, vj, jnp.zeros_like(vj)) for hm in masks])
            units = [(n, h) for n in range(len(pairs)) for h in range(2)]
            qks = {(n, h): _nt(qms[pairs[n][0], h], ks[pairs[n][1]]) for n, h in units}
            lbs, l1ms = {}, {}
            for u in units:
                lbs[u], l1m = _log_terms(qks[u])
                l1ms[u] = jnp.where(causal, l1m, 0.0) if pairs[u[0]][2] else l1m
            cins = {u: _hilo_dot(l1ms[u], after) for u in units}
            carries, os = dict(carries), list(os)
            for n, h in units:
                b, key, diag = pairs[n]
                a = jnp.exp2(lbs[n, h] + cins[n, h] + carries[b, h])
                if diag:
                    a = jnp.where(causal, a, 0.0)
                os[b] = os[b] + _nn(a.astype(BF16), vms[key][h])
                carries[b, h] = carries[b, h] + jnp.sum(l1ms[n, h], axis=1, keepdims=True)
            return carries, tuple(os)

        carries = {(b, h): jnp.zeros((T, 1), F32) for b in range(QB) for h in range(2)}
        os = tuple(jnp.zeros((T, 2 * DH), F32) for _ in range(QB))
        carries, os = blocks([i2 + 1, i2], [(1, 0, True), (0, 1, True), (1, 1, False)], carries, os)
        carries, os = lax.fori_loop(
            0, i2 // 2,
            lambda t, c: blocks([i2 - 1 - 2 * t, i2 - 2 - 2 * t],
                                [(0, 0, False), (1, 0, False), (0, 1, False), (1, 1, False)], c[0], c[1]),
            (carries, os))
        for b in range(QB):
            o_ref[b * T:(b + 1) * T, :] = os[b].astype(BF16)
            c_ref[b * T:(b + 1) * T, :] = jnp.where(masks[0], carries[b, 0], carries[b, 1])

    blk = pl.BlockSpec((QB * T, 2 * DH), lambda p, i: (i, p))
    full = pl.BlockSpec((S, 2 * DH), lambda p, i: (0, p))
    return _call(
        body, (q, k, v), name="attn_fwd", grid=(SBW // (2 * DH), S // (QB * T)),
        in_specs=[blk, full, full], out_specs=[blk, blk],
        out_shape=[_sds((S, SBW), BF16), _sds((S, SBW), F32)],
        compiler_params=_params(("arbitrary", "arbitrary"), 40), exchange=exchange)


def _attn_bwd(q, k, v, do, ctot, after=()):
    T = TA
    nq = S // (QB * T)

    def body(q_ref, k_ref, v_ref, do_ref, c_ref, dq_ref, dk_ref, dv_ref, dk_acc, dv_acc):
        step = pl.program_id(1)
        i2 = 2 * step

        @pl.when(step == 0)
        def _():
            dk_acc[...] = jnp.zeros_like(dk_acc)
            dv_acc[...] = jnp.zeros_like(dv_acc)

        row = lax.broadcasted_iota(jnp.int32, (T, T), 0)
        col = lax.broadcasted_iota(jnp.int32, (T, T), 1)
        upto = (row <= col).astype(BF16)
        before = (row < col).astype(BF16)
        causal = col < row
        masks = _head_masks()
        qms, doms, ctots = {}, {}, {}
        for b in range(QB):
            q2, do2 = q_ref[b * T:(b + 1) * T, :], do_ref[b * T:(b + 1) * T, :]
            for h, hm in enumerate(masks):
                qms[b, h] = jnp.where(hm, q2, jnp.zeros_like(q2))
                doms[b, h] = jnp.where(hm, do2, jnp.zeros_like(do2))
                ctots[b, h] = c_ref[b * T:(b + 1) * T, h * DH:h * DH + 1]

        def blocks(keys, pairs, sums, dqs):
            rows = [pl.ds(pl.multiple_of(j * T, T), T) for j in keys]
            ks, vs = [k_ref[r, :] for r in rows], [v_ref[r, :] for r in rows]
            kms = [[jnp.where(hm, kj, jnp.zeros_like(kj)) for hm in masks] for kj in ks]
            units = [(n, h) for n in range(len(pairs)) for h in range(2)]
            qks = {(n, h): _nt(qms[pairs[n][0], h], ks[pairs[n][1]]) for n, h in units}
            das = {(n, h): _nt(doms[pairs[n][0], h], vs[pairs[n][1]]) for n, h in units}
            lbs, l1ms = {}, {}
            for u in units:
                lbs[u], l1m = _log_terms(qks[u])
                l1ms[u] = jnp.where(causal, l1m, 0.0) if pairs[u[0]][2] else l1m
            pins = {u: _hilo_dot(l1ms[u], upto) for u in units}
            sums = dict(sums)
            a_s, dls, cps = {}, {}, {}
            for n, h in units:
                b, _, diag = pairs[n]
                cl, cp = sums[b, h]
                a = jnp.exp2(lbs[n, h] + (ctots[b, h] - cl) - pins[n, h])
                if diag:
                    a = jnp.where(causal, a, 0.0)
                a_s[n, h] = a.astype(BF16)
                dls[n, h] = das[n, h] * a
                cps[n, h] = cp
                sums[b, h] = (cl + jnp.sum(l1ms[n, h], axis=1, keepdims=True),
                              cp + jnp.sum(dls[n, h], axis=1, keepdims=True))
            pexs = {u: _hilo_dot(dls[u], before) for u in units}
            dzbs = {}
            for u in units:
                dz = dls[u] - jnp.exp2(lbs[u]) * (dls[u] + pexs[u] + cps[u])
                if pairs[u[0]][2]:
                    dz = jnp.where(causal, dz, 0.0)
                dzbs[u] = dz.astype(BF16)
            dqs = list(dqs)
            for n, h in units:
                dqs[pairs[n][0]] = dqs[pairs[n][0]] + _nn(dzbs[n, h], kms[pairs[n][1]][h])
            for key, r in enumerate(rows):
                mine = [(n, h) for n, h in units if pairs[n][1] == key]
                dk_acc[r, :] += functools.reduce(jnp.add, [_tn(dzbs[u], qms[pairs[u[0]][0], u[1]]) for u in mine])
                dv_acc[r, :] += functools.reduce(jnp.add, [_tn(a_s[u], doms[pairs[u[0]][0], u[1]]) for u in mine])
            return sums, tuple(dqs)

        zero = jnp.zeros((T, 1), F32)
        sums = {(b, h): (zero, zero) for b in range(QB) for h in range(2)}
        dqs = tuple(jnp.zeros((T, 2 * DH), F32) for _ in range(QB))
        sums, dqs = lax.fori_loop(
            0, i2 // 2,
            lambda t, c: blocks([2 * t, 2 * t + 1],
                                [(0, 0, False), (1, 0, False), (0, 1, False), (1, 1, False)], c[0], c[1]),
            (sums, dqs))
        _, dqs = blocks([i2, i2 + 1], [(0, 0, True), (1, 0, False), (1, 1, True)], sums, dqs)
        for b in range(QB):
            dq_ref[b * T:(b + 1) * T, :] = (dqs[b] * SCALE).astype(BF16)

        @pl.when(step == nq - 1)
        def _():
            dk_ref[...] = (dk_acc[...] * SCALE).astype(BF16)
            dv_ref[...] = dv_acc[...].astype(BF16)

    blk = pl.BlockSpec((QB * T, 2 * DH), lambda p, i: (i, p))
    full = pl.BlockSpec((S, 2 * DH), lambda p, i: (0, p))
    return _call(
        body, (q, k, v, do, ctot), name="attn_bwd", grid=(SBW // (2 * DH), nq),
        in_specs=[blk, full, full, blk, blk], out_specs=[blk, full, full],
        out_shape=[_sds((S, SBW), BF16), _sds((S, SBW), BF16), _sds((S, SBW), BF16)],
        scratch_shapes=[pltpu.VMEM((S, 2 * DH), F32), pltpu.VMEM((S, 2 * DH), F32)],
        compiler_params=_params(("arbitrary", "arbitrary"), 40), after=after)


def _pool_counts(first_row, tm):
    pos = first_row + lax.broadcasted_iota(jnp.int32, (tm, 1), 0)
    return [jnp.minimum(pos + 1, w).astype(F32) for w in POOL_WINDOWS]


def _mix_out(h, xp, o_sb, gp, gs, w_group, scale, w_bp, w_ba, w_out, exchange=None):
    tm = 512

    def body(h_ref, xp_ref, o_ref, gp_ref, gs_ref, wg_hbm, sc_ref, wbp_hbm, wba_hbm, wo_hbm,
             h2_ref, pm_ref, p_ref, yp_ref, ys_ref, m_ref, halo, wg_ref, wbp_ref, wba_ref, wo_ref):
        _stage([(wg_hbm, wg_ref), (wbp_hbm, wbp_ref), (wba_hbm, wba_ref), (wo_hbm, wo_ref)])
        i = pl.program_id(0)

        @pl.when(i == 0)
        def _():
            halo[...] = jnp.zeros_like(halo)

        xp = xp_ref[...]
        ext = jnp.concatenate([halo[...], xp], axis=0)
        halo[...] = xp[tm - HALO:, :]
        counts = _pool_counts(i * tm, tm)
        for gi in range(len(POOL_WINDOWS)):
            lanes = slice(gi * PG, (gi + 1) * PG)
            win = ext[:, lanes]
            for step in range(gi + 1):
                win = win + pltpu.roll(win, 1 << step, 0)
            pm = (win[HALO:, :] / counts[gi] - xp[:, lanes]).astype(BF16)
            pm_ref[:, lanes] = pm
            p_ref[:, lanes] = (_nn(pm, wg_ref[gi]) * sc_ref[:, lanes]).astype(BF16)
        pb = p_ref[...]
        ob = o_ref[...]
        for j in range(NSH):
            cols = slice(j * (D // NSH), (j + 1) * (D // NSH))
            yp = _nn(pb, wbp_ref[j])
            ys = _nn(ob, wba_ref[j])
            yp_ref[:, cols] = yp.astype(BF16)
            ys_ref[:, cols] = ys.astype(BF16)
            m_ref[:, cols] = (gp_ref[:, cols].astype(F32) * yp + gs_ref[:, cols].astype(F32) * ys).astype(BF16)
        h2_ref[...] = h_ref[...] + _nn(m_ref[...], wo_ref[...])

    return _call(
        body, (h, xp, o_sb, gp, gs, w_group, scale, w_bp, w_ba, w_out), name="mix_out", grid=(S // tm,),
        in_specs=[_rows(tm, D), _rows(tm, PW), _rows(tm, SBW), _rows(tm, D), _rows(tm, D),
                  _ANY, _fixed((1, PW)), _ANY, _ANY, _ANY],
        out_specs=[_rows(tm, D), _rows(tm, PW), _rows(tm, PW), _rows(tm, D), _rows(tm, D), _rows(tm, D)],
        out_shape=[_sds((S, D), F32), _sds((S, PW), BF16), _sds((S, PW), BF16), _sds((S, D), BF16),
                   _sds((S, D), BF16), _sds((S, D), BF16)],
        scratch_shapes=[pltpu.VMEM((HALO, PW), F32)] + _vmem_like(w_group, w_bp, w_ba, w_out),
        compiler_params=_params(("arbitrary",), 48), free=(5, 6), exchange=exchange)


def _mix_bwd_out(dh, gp, gs, yp, ys, pm, w_group, scale, w_bp, w_ba, w_out, exchange=None):
    tm = 512
    nt = S // tm

    def body(dh_ref, gp_ref, gs_ref, yp_ref, ys_ref, pm_ref, wg_hbm, sc_ref, wbp_hbm, wba_hbm, wo_hbm,
             dlg_ref, dyp_ref, dys_ref, do_ref, dyg_ref, dxp_ref, dsc_ref, halo, wg_ref, wbp_ref, wba_ref, wo_ref):
        _stage([(wg_hbm, wg_ref), (wbp_hbm, wbp_ref), (wba_hbm, wba_ref), (wo_hbm, wo_ref)])
        step = pl.program_id(0)

        @pl.when(step == 0)
        def _():
            halo[...] = jnp.zeros_like(halo)
            dsc_ref[...] = jnp.zeros_like(dsc_ref)

        dm = _nt(dh_ref[...].astype(BF16), wo_ref[...])
        gp = gp_ref[...].astype(F32)
        gs = gs_ref[...].astype(F32)
        yp = yp_ref[...].astype(F32)
        ys = ys_ref[...].astype(F32)
        dlg_ref[:, :D] = (dm * yp * gp * (1.0 - gp)).astype(BF16)
        dlg_ref[:, D:] = (dm * ys * gs * (1.0 - gs)).astype(BF16)
        dyp_ref[...] = (dm * gp).astype(BF16)
        dys_ref[...] = (dm * gs).astype(BF16)
        dp = jnp.zeros((tm, PW), F32)
        do = jnp.zeros((tm, SBW), F32)
        for j in range(NSH):
            cols = slice(j * (D // NSH), (j + 1) * (D // NSH))
            dp = dp + _nt(dyp_ref[:, cols], wbp_ref[j])
            do = do + _nt(dys_ref[:, cols], wba_ref[j])
        do_ref[...] = do.astype(BF16)
        counts = _pool_counts((nt - 1 - step) * tm, tm)
        dscale = []
        for gi in range(len(POOL_WINDOWS)):
            lanes = slice(gi * PG, (gi + 1) * PG)
            dpg = dp[:, lanes]
            dscale.append(jnp.sum(dpg * _nn(pm_ref[:, lanes], wg_ref[gi]), axis=0, keepdims=True))
            dyg = (dpg * sc_ref[:, lanes]).astype(BF16)
            dyg_ref[:, lanes] = dyg
            dpm = _nt(dyg, wg_ref[gi])
            per = dpm / counts[gi]
            win = jnp.concatenate([per, halo[:, lanes]], axis=0)
            halo[:, lanes] = per[:HALO, :]
            for s in range(gi + 1):
                win = win + pltpu.roll(win, tm + HALO - (1 << s), 0)
            dxp_ref[:, lanes] = (win[:tm, :] - dpm).astype(BF16)
        dsc_ref[...] += jnp.concatenate(dscale, axis=1)

    rev = lambda width: pl.BlockSpec((tm, width), lambda i: (nt - 1 - i, 0))
    return _call(
        body, (dh, gp, gs, yp, ys, pm, w_group, scale, w_bp, w_ba, w_out), name="mix_bwd_out", grid=(nt,),
        in_specs=[rev(D), rev(D), rev(D), rev(D), rev(D), rev(PW), _ANY, _fixed((1, PW)), _ANY, _ANY, _ANY],
        out_specs=[rev(2 * D), rev(D), rev(D), rev(SBW), rev(PW), rev(PW), _fixed((1, PW))],
        out_shape=[_sds((S, 2 * D), BF16), _sds((S, D), BF16), _sds((S, D), BF16), _sds((S, SBW), BF16),
                   _sds((S, PW), BF16), _sds((S, PW), BF16), _sds((1, PW), F32)],
        scratch_shapes=[pltpu.VMEM((HALO, PW), F32)] + _vmem_like(w_group, w_bp, w_ba, w_out),
        compiler_params=_params(("arbitrary",), 48), exchange=exchange)


def _mix_bwd_in(dh, h, gain, pieces, w_in, exchange=None):
    tm = 512
    widths = [p.shape[1] for p in pieces]

    def body(dh_ref, h_ref, g_ref, *rest):
        piece_refs, (w_hbm, dx_ref, dg_ref, w_ref, dp_ref) = rest[:len(pieces)], rest[len(pieces):]
        _stage([(w_hbm, w_ref)])
        at = 0
        for ref, width in zip(piece_refs, widths):
            dp_ref[:, at:at + width] = ref[...]
            at += width
        du = jnp.zeros((tm, D), F32)
        for j in range(NSH):
            du = du + _nt(dp_ref[:, j * D:(j + 1) * D], w_ref[j])
        r, hr = _rms(h_ref[...])
        dx, dgain = _rms_bwd(du, hr, r, g_ref[...])
        dx_ref[...] = dh_ref[...] + dx

        @pl.when(pl.program_id(0) == 0)
        def _():
            dg_ref[...] = jnp.zeros_like(dg_ref)

        dg_ref[...] += dgain

    return _call(
        body, (dh, h, gain, *pieces, w_in), name="mix_bwd_in", grid=(S // tm,),
        in_specs=[_rows(tm, D), _rows(tm, D), _fixed((1, D))] + [_rows(tm, w) for w in widths] + [_ANY],
        out_specs=[_rows(tm, D), _fixed((1, D))],
        out_shape=[_sds((S, D), F32), _sds((1, D), F32)],
        scratch_shapes=_vmem_like(w_in) + [pltpu.VMEM((tm, 4 * D), BF16)],
        compiler_params=_params(("arbitrary",), 48), exchange=exchange)


def _wgrad_in(u, pieces):
    dxp, dq, dk, dv, dlg = pieces

    def body(u_ref, dxp_ref, dq_ref, dk_ref, dv_ref, dlg_ref, o_ref):
        j = pl.program_id(0)
        u = u_ref[...]

        def two(left_ref, right_ref):
            o_ref[:, :PW] = _tn(u, left_ref[...]).astype(BF16)
            o_ref[:, PW:] = _tn(u, right_ref[...]).astype(BF16)

        pl.when(j == 0)(lambda: two(dxp_ref, dq_ref))
        pl.when(j == 1)(lambda: two(dk_ref, dv_ref))

        @pl.when(j >= 2)
        def _():
            o_ref[...] = _tn(u, dlg_ref[...]).astype(BF16)

    whole = lambda width: pl.BlockSpec((S, width), lambda j: (0, 0))
    return _call(
        body, (u, dxp, dq, dk, dv, dlg), name="wgrad_in", grid=(NSH,),
        in_specs=[whole(D), whole(PW), whole(SBW), whole(SBW), whole(SBW),
                  pl.BlockSpec((S, D), lambda j: (0, jnp.maximum(j - 2, 0)))],
        out_specs=[pl.BlockSpec((None, D, D), lambda j: (j, 0, 0))], out_shape=[_sds((NSH, D, D), BF16)],
        compiler_params=_params(("arbitrary",), 56))[0]


def _wgrad(a, b, nblk, ti, name, out_dtype=BF16, exchange=None, after=()):
    ka, n = a.shape[1], b.shape[1]
    ns = n // nblk

    def body(a_ref, b_ref, o_ref):
        o_ref[...] = _tn(a_ref[...].astype(BF16), b_ref[...].astype(BF16)).astype(out_dtype)

    res = _call(
        body, (a, b), name=name, grid=(nblk, ka // ti),
        in_specs=[pl.BlockSpec((S, ti), lambda j, i: (0, i)), pl.BlockSpec((S, ns), lambda j, i: (0, j))],
        out_specs=[pl.BlockSpec((None, ti, ns), lambda j, i: (j, i, 0))],
        out_shape=[_sds((nblk, ka, ns), out_dtype)],
        compiler_params=_params(("arbitrary", "arbitrary"), 56), exchange=exchange, after=after)
    return res[0] if exchange is None else (res[0][0], res[1])


def _wgrad_branches(p, dyp, o_sb, dys, pm, dyg):
    cols = D // NSH

    def body(p_ref, dyp_ref, o_ref, dys_ref, pm_ref, dyg_ref, gbp_ref, gba_ref, gg_ref):
        gbp_ref[...] = _tn(p_ref[...], dyp_ref[...]).astype(BF16)
        gba_ref[...] = _tn(o_ref[...], dys_ref[...]).astype(BF16)
        gg_ref[...] = _tn(pm_ref[...], dyg_ref[...])

    whole = lambda width: pl.BlockSpec((S, width), lambda j: (0, 0))
    col = lambda width: pl.BlockSpec((S, width), lambda j: (0, j))
    return _call(
        body, (p, dyp, o_sb, dys, pm, dyg), name="wgrad_branches", grid=(NSH,),
        in_specs=[whole(PW), col(cols), whole(SBW), col(cols), col(PG), col(PG)],
        out_specs=[pl.BlockSpec((None, PW, cols), lambda j: (j, 0, 0)),
                   pl.BlockSpec((None, SBW, cols), lambda j: (j, 0, 0)),
                   pl.BlockSpec((None, PG, PG), lambda j: (j, 0, 0))],
        out_shape=[_sds((NSH, PW, cols), BF16), _sds((NSH, SBW, cols), BF16), _sds((NSH, PG, PG), F32)],
        compiler_params=_params(("arbitrary",), 40))


def _place():
    x, y, c = lax.axis_index("x"), lax.axis_index("y"), lax.axis_index("c")
    chips = [(1 - x, y), (x, 1 - y), (1 - x, 1 - y)]
    return x, y, c, chips


def _remote(src, dst, ssem, rsem, dev):
    return pltpu.make_async_remote_copy(src_ref=src, dst_ref=dst, send_sem=ssem, recv_sem=rsem,
                                        device_id=dev, device_id_type=MESH)


def _cast_into_block(ws, me_idx, name):
    steps = 4
    shapes = [(w.shape[0] // steps, w.shape[1]) for w in ws]

    def body(me_ref, *refs):
        for w_ref, o_ref in zip(refs[:len(ws)], refs[len(ws):]):
            o_ref[...] = w_ref[...].astype(BF16)

    return pl.pallas_call(
        body, name=name, out_shape=[_sds((NSH,) + w.shape, BF16) for w in ws],
        grid_spec=pltpu.PrefetchScalarGridSpec(
            num_scalar_prefetch=1, grid=(steps,),
            in_specs=[pl.BlockSpec((r, c), lambda s, me: (s, 0)) for r, c in shapes],
            out_specs=[pl.BlockSpec((None, r, c), lambda s, me: (me[0], s, 0)) for r, c in shapes]),
        compiler_params=_params(("arbitrary",), 32),
    )(me_idx, *ws)


def _ex_gather(bufs):
    n = len(bufs)
    per = 8

    def plan(outs, ssem, rsem, w):
        x, y, c, _ = _place()
        sib, nbr_x, nbr_y = (x, y, 1 - c), (1 - x, y, c), (x, 1 - y, c)
        half = outs[w].shape[1] // 2
        quarter = half // 2
        sem = lambda k: (ssem.at[per * w + k], rsem.at[per * w + k])
        rows = lambda blk, start, size: outs[w].at[blk, pl.ds(start, size)]
        mine = rows(2 * x + y, c * half, half)
        from_x = rows(2 * (1 - x) + y, c * half, half)
        from_y = rows(2 * x + (1 - y), c * half, half)
        diag = 2 * (1 - x) + (1 - y)
        pass_y = rows(2 * (1 - x) + y, c * half, quarter)
        pass_x = rows(2 * x + (1 - y), c * half + quarter, quarter)
        diag_0, diag_1 = rows(diag, c * half, quarter), rows(diag, c * half + quarter, quarter)
        first = [_remote(mine, mine, *sem(0), nbr_x), _remote(mine, mine, *sem(1), nbr_y)]
        arrivals = [
            (_remote(from_x, from_x, *sem(0), nbr_x),
             [_remote(pass_y, pass_y, *sem(2), nbr_y), _remote(from_x, from_x, *sem(4), sib)]),
            (_remote(from_y, from_y, *sem(1), nbr_y),
             [_remote(pass_x, pass_x, *sem(3), nbr_x), _remote(from_y, from_y, *sem(5), sib)]),
            (_remote(diag_0, diag_0, *sem(2), nbr_y), [_remote(diag_0, diag_0, *sem(6), sib)]),
            (_remote(diag_1, diag_1, *sem(3), nbr_x), [_remote(diag_1, diag_1, *sem(7), sib)]),
        ]
        other = (1 - c) * half
        from_sibling = [
            _remote(rows(2 * (1 - x) + y, other, half), rows(2 * (1 - x) + y, other, half), *sem(4), sib),
            _remote(rows(2 * x + (1 - y), other, half), rows(2 * x + (1 - y), other, half), *sem(5), sib),
            _remote(rows(diag, other, quarter), rows(diag, other, quarter), *sem(6), sib),
            _remote(rows(diag, other + quarter, quarter), rows(diag, other + quarter, quarter), *sem(7), sib),
        ]
        return first, arrivals, from_sibling

    def start(ins, outs, ssem, rsem):
        x, y, c, _ = _place()
        for w in range(n):
            half = outs[w].shape[1] // 2
            mine = outs[w].at[2 * x + y, pl.ds(c * half, half)]
            _remote(mine, mine, ssem.at[per * w], rsem.at[per * w], (1 - x, y, c)).start()
            _remote(mine, mine, ssem.at[per * w + 1], rsem.at[per * w + 1], (x, 1 - y, c)).start()

    def finish(ins, outs, ssem, rsem):
        plans = [plan(outs, ssem, rsem, w) for w in range(n)]
        started = []
        for direct in (True, False):
            for first, arrivals, _ in plans:
                for arrived, onward in (arrivals[:2] if direct else arrivals[2:]):
                    arrived.wait_recv()
                    for cp in onward:
                        cp.start()
                    started += onward
        for first, _, from_sibling in plans:
            for cp in from_sibling:
                cp.wait_recv()
            started += first
        for cp in started:
            cp.wait_send()

    return Exchange(bufs, [_sds(b.shape, b.dtype) for b in bufs], {w: w for w in range(n)}, per * n, start, finish)


def _ex_gather_direct(bufs):
    n = len(bufs)

    def copies(outs, ssem, rsem, only_first=False):
        x, y, c, chips = _place()
        me, sib = 2 * x + y, (x, y, 1 - c)
        first, relay, last = [], [], []
        for w in range(n):
            half = outs[w].shape[1] // 2
            mine = outs[w].at[me, pl.ds(c * half, half)]
            for k, (px, py) in enumerate(chips):
                sems = (ssem.at[6 * w + k], rsem.at[6 * w + k])
                sib_sems = (ssem.at[6 * w + 3 + k], rsem.at[6 * w + 3 + k])
                first.append(_remote(mine, mine, *sems, (px, py, c)))
                if only_first:
                    continue
                got = outs[w].at[2 * px + py, pl.ds(c * half, half)]
                relay.append((_remote(got, got, *sems, (px, py, c)), _remote(got, got, *sib_sems, sib)))
                theirs = outs[w].at[2 * px + py, pl.ds((1 - c) * half, half)]
                last.append(_remote(theirs, theirs, *sib_sems, sib))
        return first, relay, last

    def start(ins, outs, ssem, rsem):
        for cp in copies(outs, ssem, rsem, only_first=True)[0]:
            cp.start()

    def finish(ins, outs, ssem, rsem):
        first, relay, last = copies(outs, ssem, rsem)
        for arrived, onward in relay:
            arrived.wait_recv()
            onward.start()
        for cp in last:
            cp.wait_recv()
        for cp in first:
            cp.wait_send()
        for _, onward in relay:
            onward.wait_send()

    return Exchange(bufs, [_sds(b.shape, b.dtype) for b in bufs], {w: w for w in range(n)}, 6 * n, start, finish)


def _simple_exchange(arrays, landing, aliases, make_copies, sibling_only=False):
    def start(ins, outs, ssem, rsem):
        for cp, _ in make_copies(ins, outs, ssem, rsem, False):
            cp.start()

    def finish(ins, outs, ssem, rsem):
        cps = make_copies(ins, outs, ssem, rsem, True)
        for _, landed in cps:
            landed.wait_recv()
        for cp, _ in cps:
            cp.wait_send()

    return Exchange(arrays, landing, aliases, len(arrays) * 3, start, finish, sibling_only)


def _ex_pair_swap(grads):
    def make(ins, outs, ssem, rsem, landing):
        x, y, c, _ = _place()
        cps = [_remote(ins[w].at[:, 1 - c], outs[w], ssem.at[w], rsem.at[w], (x, y, 1 - c))
               for w in range(len(grads))]
        return [(cp, cp) for cp in cps]

    return _simple_exchange(grads, [_sds((NSH,) + g.shape[2:], g.dtype) for g in grads], {}, make, True)


def _ex_relay(bufs):
    def make(ins, outs, ssem, rsem, landing):
        x, y, c, chips = _place()
        sib = (x, y, 1 - c)
        out = []
        for w in range(len(bufs)):
            half = outs[w].shape[1] // 2
            for k, (px, py) in enumerate(chips):
                sems = (ssem.at[3 * w + k], rsem.at[3 * w + k])
                have = outs[w].at[2 * px + py, pl.ds(c * half, half)]
                miss = outs[w].at[2 * px + py, pl.ds((1 - c) * half, half)]
                out.append((_remote(have, have, *sems, sib), _remote(miss, miss, *sems, sib) if landing else None))
        return out

    return _simple_exchange(bufs, [_sds(b.shape, b.dtype) for b in bufs], {w: w for w in range(len(bufs))}, make, True)


def _ex_share(bufs):
    def make(ins, outs, ssem, rsem, landing):
        x, y, c, _ = _place()
        sib = (x, y, 1 - c)
        return [(_remote(outs[w].at[c], outs[w].at[c], ssem.at[w], rsem.at[w], sib),
                 _remote(outs[w].at[1 - c], outs[w].at[1 - c], ssem.at[w], rsem.at[w], sib) if landing else None)
                for w in range(len(bufs))]

    return _simple_exchange(bufs, [_sds(b.shape, b.dtype) for b in bufs], {w: w for w in range(len(bufs))}, make, True)


def _small_copies(slots, ssems, rsems, sending):
    x, y, c, _ = _place()
    out = []
    for m in range(1, 8):
        px, py, pc = x ^ (m >> 2), y ^ ((m >> 1) & 1), c ^ (m & 1)
        slot = slots.at[4 * x + 2 * y + c if sending else 4 * px + 2 * py + pc]
        out.append(_remote(slot, slot, ssems[m - 1], rsems[m - 1], (px, py, pc)))
    return out


def _small_gather_start(slots, name):
    def body(*refs):
        for cp in _small_copies(refs[0], refs[1:8], refs[8:15], True):
            cp.start()
        refs[-1][...] = jnp.zeros_like(refs[-1])

    outs = pl.pallas_call(
        body, name=name,
        out_shape=([pltpu.SemaphoreType.DMA(())] * 14 + [pltpu.HBM(slots.shape, slots.dtype)]
                   + [jax.ShapeDtypeStruct((8, 128), F32)]),
        in_specs=[_HBM], out_specs=[_SEM] * 14 + [_HBM, _VM], input_output_aliases={0: 14},
        compiler_params=pltpu.CompilerParams(has_side_effects=_EFFECT),
    )(*_in_hbm([slots]))
    return outs[:14], outs[14], outs[15]


def _small_gather_wait(sems, slots, after, name):
    def body(*refs):
        for cp in _small_copies(refs[0], refs[1:8], refs[8:15], True):
            cp.wait_send()
        for cp in _small_copies(refs[0], refs[1:8], refs[8:15], False):
            cp.wait_recv()

    return pl.pallas_call(
        body, name=name, out_shape=pltpu.HBM(slots.shape, slots.dtype),
        in_specs=[_HBM] + [_SEM] * 14 + [_ANY] * len(after), out_specs=_HBM, input_output_aliases={0: 0},
        compiler_params=pltpu.CompilerParams(has_side_effects=_EFFECT),
    )(slots, *sems, *after)


def _pair_sum(grads, gots, c_idx, name):
    n = len(grads)

    def body(c_ref, *refs):
        for a_ref, b_ref, o_ref in zip(refs[:n], refs[n:2 * n], refs[2 * n:]):
            o_ref[...] = (a_ref[...].astype(F32) + b_ref[...].astype(F32)).astype(BF16)

    halves = [g.shape[2:] for g in grads]
    return list(pl.pallas_call(
        body, name=name, out_shape=[_sds((NSH,) + h, BF16) for h in halves],
        grid_spec=pltpu.PrefetchScalarGridSpec(
            num_scalar_prefetch=1, grid=(NSH,),
            in_specs=[pl.BlockSpec((None, None) + h, lambda j, c: (j, c[0], 0, 0)) for h in halves]
            + [pl.BlockSpec((None,) + h, lambda j, c: (j, 0, 0)) for h in halves],
            out_specs=[pl.BlockSpec((None,) + h, lambda j, c: (j, 0, 0)) for h in halves]),
        compiler_params=_params(("arbitrary",), 40),
    )(c_idx, *_in_hbm(list(grads) + list(gots))))


def _chip_sum(owns, gots, place, name):
    n = len(owns)

    def body(place_ref, *refs):
        for own_ref, got_ref, o_ref in zip(refs[:n], refs[n:2 * n], refs[2 * n:]):
            acc = own_ref[...].astype(F32)
            for k in range(3):
                acc = acc + got_ref[k].astype(F32)
            o_ref[...] = acc

    shapes = [(o.shape[1] // 2, o.shape[2]) for o in owns]
    return list(pl.pallas_call(
        body, name=name, out_shape=[_sds((2, 2 * r, c), F32) for r, c in shapes],
        grid_spec=pltpu.PrefetchScalarGridSpec(
            num_scalar_prefetch=1, grid=(2,),
            in_specs=[pl.BlockSpec((None, r, c), lambda s, p: (p[0], s, 0)) for r, c in shapes]
            + [pl.BlockSpec((3, r, c), lambda s, p: (0, s, 0)) for r, c in shapes],
            out_specs=[pl.BlockSpec((None, r, c), lambda s, p: (p[1], s, 0)) for r, c in shapes]),
        compiler_params=_params(("arbitrary",), 40),
    )(place, *_in_hbm(list(owns) + list(gots))))


def _adamw_math(w, g, m, v):
    m = B1 * m + (1.0 - B1) * g
    v = B2 * v + (1.0 - B2) * (g * g)
    m_hat = m / (1.0 - B1 ** STEP)
    v_hat = v / (1.0 - B2 ** STEP)
    return -LR * (m_hat / (jnp.sqrt(v_hat) + AEPS) + WD * w), m, v


def _adamw(ws, gs, ms, vs, name, after=()):
    n, steps = len(ws), 4

    def body(*refs):
        ins, outs = refs[:4 * n], refs[4 * n:]
        for i in range(n):
            w_ref, g_ref, m_ref, v_ref = ins[4 * i:4 * i + 4]
            go_ref, d_ref, nm_ref, nv_ref = outs[4 * i:4 * i + 4]
            g = g_ref[...]
            go_ref[...] = g
            d_ref[...], nm_ref[...], nv_ref[...] = _adamw_math(w_ref[...], g, m_ref[...], v_ref[...])

    args, specs, shapes, free = [], [], [], []
    for i, (w, g, m, v) in enumerate(zip(ws, gs, ms, vs)):
        args += [w, g, m, v]
        specs += [pl.BlockSpec((w.shape[0] // steps, w.shape[1]), lambda r: (r, 0))] * 4
        shapes += [_sds(w.shape, F32)] * 4
        free += [4 * i, 4 * i + 2, 4 * i + 3]
    outs = _call(body, args, name=name, grid=(steps,), out_shape=shapes, in_specs=specs, out_specs=specs,
                 compiler_params=_params(("arbitrary",), 48), free=tuple(free), after=after)
    return [outs[4 * i:4 * i + 4] for i in range(n)]


def _small_update(gathered, w, m, v, entries):
    rows = w.shape[0]

    def body(ga_ref, w_ref, m_ref, v_ref, *out_refs):
        g = ga_ref[0:rows, :]
        for dev in range(1, 8):
            g = g + ga_ref[dev * rows:(dev + 1) * rows, :]
        results = (g,) + _adamw_math(w_ref[...], g, m_ref[...], v_ref[...])
        for i, res in enumerate(results):
            for j, (first, n) in enumerate(entries):
                out_refs[i * len(entries) + j][...] = res[first:first + n, :]

    outs = pl.pallas_call(
        body, name="small_update",
        out_shape=[jax.ShapeDtypeStruct((n, 128), F32) for _ in range(4) for _, n in entries],
        in_specs=[_VM] * 4, out_specs=[_VM] * (4 * len(entries)),
    )(gathered, w, m, v)
    return [outs[i * len(entries):(i + 1) * len(entries)] for i in range(4)]


SMALL = ("ffn1_norm", "mix_norm", "ffn2_norm", "final_norm", "pool_scale", "loss", "pool_w_group")
BIG = ("ffn1_w_gate_up", "ffn1_w_down", "w_in", "w_branch_pool", "w_branch_attn", "w_out",
       "ffn2_w_gate_up", "ffn2_w_down")
ORDER = ("ffn1_norm", "ffn1_w_gate_up", "ffn1_w_down", "mix_norm", "w_in", "pool_w_group", "pool_scale",
         "w_branch_pool", "w_branch_attn", "w_out", "ffn2_norm", "ffn2_w_gate_up", "ffn2_w_down", "final_norm")
SMALL_ROWS = 560


def _pack_small(t):
    parts = []
    for k in SMALL:
        rows = t[k].reshape(-1, 128) if k in t else jnp.zeros((1, 128), F32)
        parts.append(jnp.pad(rows, ((0, -rows.shape[0] % 8), (0, 0))))
    packed = jnp.concatenate(parts, axis=0)
    assert packed.shape == (SMALL_ROWS, 128), packed.shape
    return packed


def _small_entries(like):
    out, at = [], 0
    for k in SMALL:
        n = like[k].size // 128 if k in like else 1
        out.append((at, n + (-n % 8)))
        at += n + (-n % 8)
    return out


def _halves(g):
    return g.reshape(NSH, 2, g.shape[1] // 2, g.shape[2])


def kernel(x, ffn1_norm, ffn1_w_gate_up, ffn1_w_down, mix_norm, w_in, pool_w_group, pool_scale, w_branch_pool, w_branch_attn, w_out, ffn2_norm, ffn2_w_gate_up, ffn2_w_down, final_norm, loss_target, m_ffn1_norm, m_ffn1_w_gate_up, m_ffn1_w_down, m_mix_norm, m_w_in, m_pool_w_group, m_pool_scale, m_w_branch_pool, m_w_branch_attn, m_w_out, m_ffn2_norm, m_ffn2_w_gate_up, m_ffn2_w_down, m_final_norm, v_ffn1_norm, v_ffn1_w_gate_up, v_ffn1_w_down, v_mix_norm, v_w_in, v_pool_w_group, v_pool_scale, v_w_branch_pool, v_w_branch_attn, v_w_out, v_ffn2_norm, v_ffn2_w_gate_up, v_ffn2_w_down, v_final_norm):
    wts = dict(ffn1_norm=ffn1_norm, ffn1_w_gate_up=ffn1_w_gate_up, ffn1_w_down=ffn1_w_down, mix_norm=mix_norm,
               w_in=w_in, pool_w_group=pool_w_group, pool_scale=pool_scale, w_branch_pool=w_branch_pool,
               w_branch_attn=w_branch_attn, w_out=w_out, ffn2_norm=ffn2_norm, ffn2_w_gate_up=ffn2_w_gate_up,
               ffn2_w_down=ffn2_w_down, final_norm=final_norm)
    mom = dict(ffn1_norm=m_ffn1_norm, ffn1_w_gate_up=m_ffn1_w_gate_up, ffn1_w_down=m_ffn1_w_down,
               mix_norm=m_mix_norm, w_in=m_w_in, pool_w_group=m_pool_w_group, pool_scale=m_pool_scale,
               w_branch_pool=m_w_branch_pool, w_branch_attn=m_w_branch_attn, w_out=m_w_out,
               ffn2_norm=m_ffn2_norm, ffn2_w_gate_up=m_ffn2_w_gate_up, ffn2_w_down=m_ffn2_w_down,
               final_norm=m_final_norm)
    var = dict(ffn1_norm=v_ffn1_norm, ffn1_w_gate_up=v_ffn1_w_gate_up, ffn1_w_down=v_ffn1_w_down,
               mix_norm=v_mix_norm, w_in=v_w_in, pool_w_group=v_pool_w_group, pool_scale=v_pool_scale,
               w_branch_pool=v_w_branch_pool, w_branch_attn=v_w_branch_attn, w_out=v_w_out,
               ffn2_norm=v_ffn2_norm, ffn2_w_gate_up=v_ffn2_w_gate_up, ffn2_w_down=v_ffn2_w_down,
               final_norm=v_final_norm)

    c_idx = lax.axis_index("c").astype(jnp.int32).reshape(1)
    me_idx = (2 * lax.axis_index("x") + lax.axis_index("y")).astype(jnp.int32).reshape(1)
    place = jnp.concatenate([me_idx, c_idx])
    x0, tgt = x[0], loss_target[0]
    wgrp = pool_w_group[0].astype(BF16)
    g1, gm, g2, gf = ffn1_norm, mix_norm, ffn2_norm, final_norm.reshape(1, D)
    grad, delta, new_m, new_v = {}, {}, {}, {}

    def pair_sums(keys, parts, got):
        return _pair_sum(parts, got, c_idx, "pair_sum_" + keys[0])

    def chip_sums(keys, chip_parts, owned):
        return _chip_sum(chip_parts, owned, place, "chip_sum_" + keys[0])

    def adamw(keys, after=()):
        outs = _adamw([wts[k][0] for k in keys], [grad[k][0] for k in keys], [mom[k][0] for k in keys],
                      [var[k][0] for k in keys], "adamw_" + keys[0], after=after)
        for k, res in zip(keys, outs):
            grad[k], delta[k], new_m[k], new_v[k] = (o.reshape(wts[k].shape) for o in res)

    first, late = ("ffn1_w_gate_up", "ffn1_w_down"), ("w_branch_pool", "w_branch_attn", "w_out",
                                                       "ffn2_w_gate_up", "ffn2_w_down")
    own = {}
    for group in (first, ("w_in",), late):
        own.update(zip(group, _cast_into_block([wts[k][0] for k in group], me_idx, "cast_" + group[0])))
    full = dict(zip(first, _exchange_alone(_ex_gather([own[k] for k in first]), "gather_ffn1")))
    wgu1, wd1 = full["ffn1_w_gate_up"], full["ffn1_w_down"].reshape(DFF, D)
    (h1, n1, gu1, a1), (win,) = _ffn_fwd(x0, g1, wgu1, wd1, "ffn1_fwd", exchange=_ex_gather_direct([own["w_in"]]))
    sems_l, thru_l, token_l = _gather_start([own[k_] for k_ in late], [h1], "gather_late_start")
    u, xp, q, k, v, gp, gs = _mix_in(h1, gm, win, after=(token_l,))
    o_sb, ctot = _attn_fwd(q, k, v)
    arrived = _gather_wait(sems_l, thru_l, [o_sb], "gather_late_wait")
    wbp, wba, wout = _exchange_alone(_ex_relay(arrived[:3]), "relay_mix")
    wout = wout.reshape(D, D)
    (h2, pm, p, yp, ys, mm), (wgu2, wd2) = _mix_out(h1, xp, o_sb, gp, gs, wgrp, pool_scale, wbp, wba, wout,
                                                    exchange=_ex_relay(arrived[3:]))
    wd2 = wd2.reshape(DFF, D)
    dh3, loss_row, d_gf, n3, gu3, a3 = _ffn_fwd(h2, g2, wgu2, wd2, "ffn2_fwd", head=(tgt, gf))

    def grad_gate_up(n, dgu, name, exchange=None):
        res = _wgrad(n, dgu, NSH, D, name, exchange=exchange)
        return [_halves(res)] if exchange is None else ([_halves(res[0])], res[1])

    def grad_down(a, dh, name, exchange=None):
        res = _wgrad(a, dh, 1, FFS, name, exchange=exchange)
        halves = lambda g: [_halves(g.reshape(NSH, DFF // NSH, D))]
        return halves(res) if exchange is None else (halves(res[0]), res[1])

    k_gu2, k_d2, k_gu1, k_d1, k_in = (("ffn2_w_gate_up",), ("ffn2_w_down",), ("ffn1_w_gate_up",),
                                      ("ffn1_w_down",), ("w_in",))
    dh2, dgu3, d_g2 = _ffn_bwd(dh3, h2, g2, gu3, wgu2, wd2, "ffn2_bwd")
    pa = grad_gate_up(n3, dgu3, "wgrad_gu2") + grad_down(a3, dh3, "wgrad_d2")
    (dlg, dyp, dys, do_sb, dyg, dxp, d_scale), got_a = _mix_bwd_out(
        dh2, gp, gs, yp, ys, pm, wgrp, pool_scale, wbp, wba, wout, exchange=_ex_pair_swap(pa))
    chip_a = pair_sums(k_gu2 + k_d2, pa, got_a)
    kb = ("w_out", "w_branch_pool", "w_branch_attn")
    g_bp, g_ba, d_group = _wgrad_branches(p, dyp, o_sb, dys, pm, dyg)
    pb = [_halves(_wgrad(mm, dh2, 1, D, "wgrad_out").reshape(NSH, D // NSH, D)), _halves(g_bp), _halves(g_ba)]
    k_a, k_in = k_gu2 + k_d2, k_in + kb
    sems_a, thru_a, token_a = _scatter_start(chip_a, "scatter_a_start")
    dq, dk, dv = _attn_bwd(q, k, v, do_sb, ctot, after=(token_a,))
    chip_a, owned_a = _scatter_wait(sems_a, thru_a, [dq], "scatter_a_wait")
    halves_a = chip_sums(k_a, chip_a, owned_a)
    dproj = (dxp, dq, dk, dv, dlg)
    (dh1, d_gm), both_a = _mix_bwd_in(dh2, h1, gm, dproj, win, exchange=_ex_share(halves_a))
    for i, k_ in enumerate(k_a):
        grad[k_] = both_a[i].reshape(wts[k_].shape)

    p_in = [_halves(_wgrad_in(u, dproj))] + pb
    p_d1, got_in = grad_down(a1, dh1, "wgrad_d1", exchange=_ex_pair_swap(p_in))
    sems_in, thru_in, token_in = _scatter_start(pair_sums(k_in, p_in, got_in), "scatter_in_start")
    dgu1, got_d1 = _ffn_bwd_act(dh1, gu1, wd1, "ffn1_bwd_act", exchange=_ex_pair_swap(p_d1), after=(token_in,))
    sems_d1, thru_d1, token_d1 = _scatter_start(pair_sums(k_d1, p_d1, got_d1), "scatter_d1_start")
    p_gu1 = [_halves(_wgrad(n1, dgu1, NSH, D, "wgrad_gu1", after=(token_in, token_d1)))]
    sems_w, thru_w, token_w = _swap_start(p_gu1, "swap_gu1_start")
    chip_in, owned_in = _scatter_wait(sems_in, thru_in, [token_w], "scatter_in_wait")
    chip_d1, owned_d1 = _scatter_wait(sems_d1, thru_d1, [token_w], "scatter_d1_wait")
    halves_in = chip_sums(k_in, chip_in, owned_in)
    p_gu1, got_gu1 = _swap_wait(sems_w, thru_w, halves_in, "swap_gu1_wait")
    sems, thru, token = _scatter_start(pair_sums(k_gu1, p_gu1, got_gu1), "scatter_gu1_start")
    sems_h, thru_h, token_h = _share_start(halves_in, [token], "share_in_start")
    adamw(k_a, after=(token_h,))
    landed = _share_wait(sems_h, thru_h, [delta[k_a[0]]], "share_in_wait")
    for i, k_ in enumerate(k_in):
        grad[k_] = landed[i].reshape(wts[k_].shape)
    adamw(k_in)
    dx, d_g1 = _ffn_bwd_in(dh1, x0, g1, dgu1, wgu1, "ffn1_bwd_in", after=(token,))
    small_g = dict(ffn1_norm=d_g1, mix_norm=d_gm, ffn2_norm=d_g2, final_norm=d_gf, pool_scale=d_scale,
                   pool_w_group=d_group, loss=loss_row)
    dev = 4 * lax.axis_index("x") + 2 * lax.axis_index("y") + lax.axis_index("c")
    slots = lax.dynamic_update_slice(jnp.zeros((8, SMALL_ROWS, 128), F32), _pack_small(small_g)[None], (dev, 0, 0))
    sems_s, slots, token_s = _small_gather_start(slots, "small_gather_start")

    chip_gu1, owned_gu1 = _scatter_wait(sems, thru, [dx] + [delta[k_] for k_ in k_a + k_in], "scatter_gu1_wait")
    halves_last = chip_sums(k_d1 + k_gu1, chip_d1 + chip_gu1, owned_d1 + owned_gu1)
    both = _exchange_alone(_ex_share(halves_last), "share_last",
                           after=(token_s,))
    grad["ffn1_w_down"] = both[0].reshape(ffn1_w_down.shape)
    grad["ffn1_w_gate_up"] = both[1].reshape(ffn1_w_gate_up.shape)
    adamw(k_d1 + k_gu1, after=(token_s,))
    gathered = _small_gather_wait(sems_s, slots, [delta[k_] for k_ in k_d1 + k_gu1], "small_gather_wait")
    gathered = gathered.reshape(8 * SMALL_ROWS, 128)
    results = _small_update(gathered, _pack_small(wts), _pack_small(mom), _pack_small(var), _small_entries(wts))
    for dst, entries in zip((grad, delta, new_m, new_v), results):
        for k_, rows in zip(SMALL, entries):
            if k_ in wts:
                dst[k_] = rows[:wts[k_].size // 128].reshape(wts[k_].shape)
            elif dst is grad:
                loss = rows[0, 0]
    return (loss, dx[None], *[grad[k_] for k_ in ORDER], *[delta[k_] for k_ in ORDER],
            *[new_m[k_] for k_ in ORDER], *[new_v[k_] for k_ in ORDER])
```

```python
import dataclasses
import functools

import jax
import jax.numpy as jnp
from jax import lax
from jax.experimental import pallas as pl
from jax.experimental.pallas import tpu as pltpu

F32 = jnp.float32
BF16 = jnp.bfloat16

S = 2048
D = 1024
DFF = 2816
FFS = 2 * DFF // 4
NSH = 4
PW = 512
PG = 128
POOL_WINDOWS = (2, 4, 8, 16)
HALO = 16
SBW = 512
DH = 64
EPS = 1e-6
SCALE = 0.125
LOG2E = 1.4426950408889634
TA = 256
QB = 2
MIB = 1024 * 1024

LR, B1, B2, AEPS, WD, STEP = 0.001, 0.9, 0.999, 1e-08, 0.01, 10

_VM = pl.BlockSpec(memory_space=pltpu.VMEM)
_ANY = pl.BlockSpec(memory_space=pl.ANY)
MESH = pl.DeviceIdType.MESH
SIBLING_PAIR_ID = 1


def _nn(a, b):
    return jnp.dot(a, b, preferred_element_type=F32)


def _nt(a, b):
    return lax.dot_general(a, b, (((1,), (1,)), ((), ())), preferred_element_type=F32)


def _tn(a, b):
    return lax.dot_general(a, b, (((0,), (0,)), ((), ())), preferred_element_type=F32)


def _params(sem, vmem_mib):
    return pltpu.CompilerParams(dimension_semantics=sem, vmem_limit_bytes=vmem_mib * MIB)


def _rows(tm, width):
    return pl.BlockSpec((tm, width), lambda i: (i, 0))


def _fixed(shape):
    return pl.BlockSpec(shape, lambda *_: (0,) * len(shape))


def _sds(shape, dtype):
    return pltpu.HBM(shape, dtype)


def _in_hbm(args):
    return [pltpu.with_memory_space_constraint(a, pltpu.HBM) for a in args]


def _stage(pairs):
    @pl.when(pl.program_id(0) == 0)
    def _():
        for src, dst in pairs:
            pltpu.sync_copy(src, dst)


def _vmem_like(*arrays):
    return [pltpu.VMEM(a.shape, a.dtype) for a in arrays]


class Exchange:
    def __init__(self, arrays, landing, aliases, n_sems, start, finish, sibling_only=False):
        self.arrays, self.landing, self.aliases, self.n_sems = list(arrays), list(landing), dict(aliases), n_sems
        self.start, self.finish = start, finish
        self.sibling_only = sibling_only

    def enter(self):
        if self.sibling_only:
            barrier = pltpu.get_barrier_semaphore()
            sibling = (lax.axis_index("x"), lax.axis_index("y"), 1 - lax.axis_index("c"))
            pl.semaphore_signal(barrier, inc=1, device_id=sibling, device_id_type=MESH)
            pl.semaphore_wait(barrier, 1)

    def params(self, compiler_params=None):
        kw = dict(collective_id=SIBLING_PAIR_ID) if self.sibling_only else {}
        if compiler_params is None:
            return pltpu.CompilerParams(**kw)
        return dataclasses.replace(compiler_params, **kw)


def _call(body, args, *, name, grid, in_specs, out_specs, out_shape, scratch_shapes=(), compiler_params=None,
          exchange=None, free=(), after=()):
    args = [a if i in free else pltpu.with_memory_space_constraint(a, pltpu.HBM) for i, a in enumerate(args)]
    if exchange is None:
        n_in = len(in_specs)

        def plain(*refs):
            body(*refs[:n_in], *refs[n_in + len(after):])

        return pl.pallas_call(plain, name=name, grid=grid, in_specs=list(in_specs) + [_ANY] * len(after),
                              out_specs=out_specs, out_shape=out_shape, scratch_shapes=list(scratch_shapes),
                              compiler_params=compiler_params)(*args, *after)
    ex = exchange
    n_in, n_out, n_scr = len(in_specs), len(out_specs), len(scratch_shapes)
    na, nl = len(ex.arrays), len(ex.landing)

    def hosted(*refs):
        at = [0]

        def take(n):
            at[0] += n
            return refs[at[0] - n:at[0]]

        k_in, _, e_in, k_out, e_out, k_scr = take(n_in), take(len(after)), take(na), take(n_out), take(nl), take(n_scr)
        ssem, rsem = take(2)
        ids = [pl.program_id(a) for a in range(len(grid))]
        first = functools.reduce(jnp.logical_and, [i == 0 for i in ids])
        last = functools.reduce(jnp.logical_and, [i == g - 1 for i, g in zip(ids, grid)])

        @pl.when(first)
        def _():
            ex.enter()
            ex.start(e_in, e_out, ssem, rsem)

        body(*k_in, *k_out, *k_scr)

        @pl.when(last)
        def _():
            ex.finish(e_in, e_out, ssem, rsem)

    outs = pl.pallas_call(
        hosted, name=name, grid=grid,
        in_specs=list(in_specs) + [_ANY] * (len(after) + na), out_specs=list(out_specs) + [_ANY] * nl,
        out_shape=list(out_shape) + ex.landing,
        scratch_shapes=list(scratch_shapes) + [pltpu.SemaphoreType.DMA((ex.n_sems,))] * 2,
        input_output_aliases={n_in + len(after) + i: n_out + j for i, j in ex.aliases.items()},
        compiler_params=ex.params(compiler_params),
    )(*args, *after, *_in_hbm(ex.arrays))
    return outs[:n_out], outs[n_out:]


def _exchange_alone(ex, name, after=()):
    na, nl = len(ex.arrays), len(ex.landing)

    def body(*refs):
        outs = refs[na + len(after):na + len(after) + nl]
        ex.enter()
        ex.start(refs[:na], outs, refs[-2], refs[-1])
        ex.finish(refs[:na], outs, refs[-2], refs[-1])

    return pl.pallas_call(
        body, name=name, in_specs=[_ANY] * (na + len(after)), out_specs=[_ANY] * nl,
        out_shape=ex.landing, scratch_shapes=[pltpu.SemaphoreType.DMA((ex.n_sems,))] * 2,
        input_output_aliases=ex.aliases, compiler_params=ex.params(),
    )(*_in_hbm(ex.arrays), *after)


_HBM = pl.BlockSpec(memory_space=pltpu.HBM)
_SEM = pl.BlockSpec(memory_space=pltpu.SEMAPHORE)
_EFFECT = pltpu.SideEffectType.DATAFLOW_SIDE_EFFECTING


def _scatter_copies(srcs, lands, ssems, rsems):
    x, y, c, chips = _place()
    return [_remote(srcs[w].at[2 * px + py], lands[w].at[k], ssems[3 * w + k], rsems[3 * w + k], (px, py, c))
            for w in range(len(srcs)) for k, (px, py) in enumerate(chips)]


def _scatter_start(parts, name):
    parts = list(parts)
    n, ncp = len(parts), 3 * len(parts)
    lands = [lax.empty((3,) + p.shape[1:], p.dtype) for p in parts]

    def body(*refs):
        srcs, land_refs = refs[:n], refs[n:2 * n]
        ssems, rsems = refs[2 * n:2 * n + ncp], refs[2 * n + ncp:2 * n + 2 * ncp]
        for cp in _scatter_copies(srcs, land_refs, ssems, rsems):
            cp.start()
        token = refs[-1]
        token[...] = jnp.zeros_like(token)

    outs = pl.pallas_call(
        body, name=name,
        out_shape=([pltpu.SemaphoreType.DMA(())] * (2 * ncp) + [pltpu.HBM(a.shape, a.dtype) for a in parts + lands]
                   + [jax.ShapeDtypeStruct((8, 128), F32)]),
        in_specs=[_HBM] * (2 * n), out_specs=[_SEM] * (2 * ncp) + [_HBM] * (2 * n) + [_VM],
        input_output_aliases={i: 2 * ncp + i for i in range(2 * n)},
        compiler_params=pltpu.CompilerParams(has_side_effects=_EFFECT),
    )(*_in_hbm(parts), *_in_hbm(lands))
    sems, thru, token = outs[:2 * ncp], outs[2 * ncp:2 * ncp + 2 * n], outs[-1]
    return sems, thru, token


def _scatter_wait(sems, thru, after, name):
    n = len(thru) // 2
    ncp = 3 * n

    def body(*refs):
        srcs, land_refs = refs[:n], refs[n:2 * n]
        ssems, rsems = refs[2 * n:2 * n + ncp], refs[2 * n + ncp:2 * n + 2 * ncp]
        for cp in _scatter_copies(srcs, land_refs, ssems, rsems):
            cp.wait_send()
            cp.wait_recv()

    outs = pl.pallas_call(
        body, name=name, out_shape=[pltpu.HBM(a.shape, a.dtype) for a in thru],
        in_specs=[_HBM] * (2 * n) + [_SEM] * (2 * ncp) + [_ANY] * len(after), out_specs=[_HBM] * (2 * n),
        input_output_aliases={i: i for i in range(2 * n)},
        compiler_params=pltpu.CompilerParams(has_side_effects=_EFFECT),
    )(*thru, *sems, *after)
    return outs[:n], outs[n:]


def _swap_copies(srcs, lands, ssems, rsems):
    x, y, c, _ = _place()
    return [_remote(srcs[w].at[:, 1 - c], lands[w], ssems[w], rsems[w], (x, y, 1 - c)) for w in range(len(srcs))]


def _swap_start(grads, name):
    grads = list(grads)
    n = len(grads)
    lands = [lax.empty((NSH,) + g.shape[2:], g.dtype) for g in grads]

    def body(*refs):
        barrier = pltpu.get_barrier_semaphore()
        sibling = (lax.axis_index("x"), lax.axis_index("y"), 1 - lax.axis_index("c"))
        pl.semaphore_signal(barrier, inc=1, device_id=sibling, device_id_type=MESH)
        pl.semaphore_wait(barrier, 1)
        for cp in _swap_copies(refs[:n], refs[n:2 * n], refs[2 * n:3 * n], refs[3 * n:4 * n]):
            cp.start()
        refs[-1][...] = jnp.zeros_like(refs[-1])

    outs = pl.pallas_call(
        body, name=name,
        out_shape=([pltpu.SemaphoreType.DMA(())] * (2 * n) + [pltpu.HBM(a.shape, a.dtype) for a in grads + lands]
                   + [jax.ShapeDtypeStruct((8, 128), F32)]),
        in_specs=[_HBM] * (2 * n), out_specs=[_SEM] * (2 * n) + [_HBM] * (2 * n) + [_VM],
        input_output_aliases={i: 2 * n + i for i in range(2 * n)},
        compiler_params=pltpu.CompilerParams(has_side_effects=_EFFECT, collective_id=SIBLING_PAIR_ID),
    )(*_in_hbm(grads), *_in_hbm(lands))
    return outs[:2 * n], outs[2 * n:4 * n], outs[-1]


def _swap_wait(sems, thru, after, name):
    n = len(thru) // 2

    def body(*refs):
        for cp in _swap_copies(refs[:n], refs[n:2 * n], refs[2 * n:3 * n], refs[3 * n:4 * n]):
            cp.wait_send()
            cp.wait_recv()

    outs = pl.pallas_call(
        body, name=name, out_shape=[pltpu.HBM(a.shape, a.dtype) for a in thru],
        in_specs=[_HBM] * (2 * n) + [_SEM] * (2 * n) + [_ANY] * len(after), out_specs=[_HBM] * (2 * n),
        input_output_aliases={i: i for i in range(2 * n)},
        compiler_params=pltpu.CompilerParams(has_side_effects=_EFFECT),
    )(*thru, *sems, *after)
    return outs[:n], outs[n:]


def _share_copies(bufs, ssems, rsems, sending):
    x, y, c, _ = _place()
    out = []
    for w, ref in enumerate(bufs):
        slot = ref.at[c if sending else 1 - c]
        out.append(_remote(slot, slot, ssems[w], rsems[w], (x, y, 1 - c)))
    return out


def _share_start(bufs, after, name):
    bufs = list(bufs)
    n = len(bufs)

    def body(*refs):
        barrier = pltpu.get_barrier_semaphore()
        sibling = (lax.axis_index("x"), lax.axis_index("y"), 1 - lax.axis_index("c"))
        pl.semaphore_signal(barrier, inc=1, device_id=sibling, device_id_type=MESH)
        pl.semaphore_wait(barrier, 1)
        at = n + len(after)
        for cp in _share_copies(refs[:n], refs[at:at + n], refs[at + n:at + 2 * n], True):
            cp.start()
        refs[-1][...] = jnp.zeros_like(refs[-1])

    outs = pl.pallas_call(
        body, name=name,
        out_shape=([pltpu.SemaphoreType.DMA(())] * (2 * n) + [pltpu.HBM(a.shape, a.dtype) for a in bufs]
                   + [jax.ShapeDtypeStruct((8, 128), F32)]),
        in_specs=[_HBM] * n + [_ANY] * len(after), out_specs=[_SEM] * (2 * n) + [_HBM] * n + [_VM],
        input_output_aliases={i: 2 * n + i for i in range(n)},
        compiler_params=pltpu.CompilerParams(has_side_effects=_EFFECT, collective_id=SIBLING_PAIR_ID),
    )(*_in_hbm(bufs), *after)
    return outs[:2 * n], outs[2 * n:3 * n], outs[-1]


def _share_wait(sems, thru, after, name):
    n = len(thru)

    def body(*refs):
        for cp in _share_copies(refs[:n], refs[n:2 * n], refs[2 * n:3 * n], True):
            cp.wait_send()
        for cp in _share_copies(refs[:n], refs[n:2 * n], refs[2 * n:3 * n], False):
            cp.wait_recv()

    return pl.pallas_call(
        body, name=name, out_shape=[pltpu.HBM(a.shape, a.dtype) for a in thru],
        in_specs=[_HBM] * n + [_SEM] * (2 * n) + [_ANY] * len(after), out_specs=[_HBM] * n,
        input_output_aliases={i: i for i in range(n)},
        compiler_params=pltpu.CompilerParams(has_side_effects=_EFFECT),
    )(*thru, *sems, *after)


def _gather_copies(bufs, ssems, rsems, sending):
    x, y, c, chips = _place()
    out = []
    for w, ref in enumerate(bufs):
        half = ref.shape[1] // 2
        for k, (px, py) in enumerate(chips):
            rows = ref.at[2 * x + y if sending else 2 * px + py, pl.ds(c * half, half)]
            out.append(_remote(rows, rows, ssems[3 * w + k], rsems[3 * w + k], (px, py, c)))
    return out


def _gather_start(bufs, after, name):
    n, ncp = len(bufs), 3 * len(bufs)

    def body(*refs):
        ssems, rsems = refs[n + len(after):n + len(after) + ncp], refs[n + len(after) + ncp:n + len(after) + 2 * ncp]
        for cp in _gather_copies(refs[:n], ssems, rsems, True):
            cp.start()
        token = refs[-1]
        token[...] = jnp.zeros_like(token)

    outs = pl.pallas_call(
        body, name=name,
        out_shape=([pltpu.SemaphoreType.DMA(())] * (2 * ncp) + [pltpu.HBM(a.shape, a.dtype) for a in bufs]
                   + [jax.ShapeDtypeStruct((8, 128), F32)]),
        in_specs=[_HBM] * n + [_ANY] * len(after), out_specs=[_SEM] * (2 * ncp) + [_HBM] * n + [_VM],
        input_output_aliases={i: 2 * ncp + i for i in range(n)},
        compiler_params=pltpu.CompilerParams(has_side_effects=_EFFECT),
    )(*_in_hbm(bufs), *after)
    return outs[:2 * ncp], outs[2 * ncp:2 * ncp + n], outs[-1]


def _gather_wait(sems, thru, after, name):
    n = len(thru)
    ncp = 3 * n

    def body(*refs):
        ssems, rsems = refs[n:n + ncp], refs[n + ncp:n + 2 * ncp]
        for cp in _gather_copies(refs[:n], ssems, rsems, True):
            cp.wait_send()
        for cp in _gather_copies(refs[:n], ssems, rsems, False):
            cp.wait_recv()

    return pl.pallas_call(
        body, name=name, out_shape=[pltpu.HBM(a.shape, a.dtype) for a in thru],
        in_specs=[_HBM] * n + [_SEM] * (2 * ncp) + [_ANY] * len(after), out_specs=[_HBM] * n,
        input_output_aliases={i: i for i in range(n)},
        compiler_params=pltpu.CompilerParams(has_side_effects=_EFFECT),
    )(*thru, *sems, *after)


def _rms(x):
    r = lax.rsqrt(jnp.mean(x * x, axis=-1, keepdims=True) + EPS)
    return r, x * r


def _rms_bwd(dn, xr, r, gain):
    dng = dn * gain
    dx = r * (dng - xr * jnp.mean(dng * xr, axis=-1, keepdims=True))
    return dx, jnp.sum(dn * xr, axis=0, keepdims=True)


def _ffn_fwd(x, gain, wgu, wd, name, exchange=None, head=None):
    tm = 256

    def body(x_ref, g_ref, wgu_hbm, wd_hbm, *rest):
        if head is None:
            h_ref, n_ref, gu_ref, a_ref, wgu_ref, wd_ref = rest
        else:
            t_ref, gf_ref, h_ref, loss_ref, dgf_ref, n_ref, gu_ref, a_ref, wgu_ref, wd_ref = rest
        _stage([(wgu_hbm, wgu_ref), (wd_hbm, wd_ref)])
        x = x_ref[...]
        _, xr = _rms(x)
        n = (xr * g_ref[...]).astype(BF16)
        n_ref[...] = n
        acc = jnp.zeros((tm, D), F32)
        for j in range(2):
            g = _nn(n, wgu_ref[j])
            u = _nn(n, wgu_ref[2 + j])
            gu_ref[:, j * FFS:(j + 1) * FFS] = g.astype(BF16)
            gu_ref[:, (2 + j) * FFS:(3 + j) * FFS] = u.astype(BF16)
            half_act = (0.5 * (g * jax.nn.sigmoid(g) * u)).astype(BF16)
            a_ref[:, j * FFS:(j + 1) * FFS] = half_act
            acc = acc + _nn(half_act, wd_ref[j * FFS:(j + 1) * FFS, :])
        h = x + acc
        if head is None:
            h_ref[...] = h
            return
        gf = gf_ref[...]
        r, hr = _rms(h)
        err = hr * gf - t_ref[...]
        dh, dgain = _rms_bwd(err * (1.0 / D), hr, r, gf)
        h_ref[...] = dh

        @pl.when(pl.program_id(0) == 0)
        def _():
            dgf_ref[...] = jnp.zeros_like(dgf_ref)
            loss_ref[...] = jnp.zeros_like(loss_ref)

        dgf_ref[...] += dgain
        loss_ref[...] += jnp.full((1, 128), (0.5 / D) * jnp.sum(err * err), F32)

    saved_specs = [_rows(tm, D), _rows(tm, 4 * FFS), _rows(tm, DFF)]
    saved_shapes = [_sds((S, D), BF16), _sds((S, 4 * FFS), BF16), _sds((S, DFF), BF16)]
    if head is None:
        return _call(
            body, (x, gain, wgu, wd), name=name, grid=(S // tm,),
            in_specs=[_rows(tm, D), _fixed((1, D)), _ANY, _ANY],
            out_specs=[_rows(tm, D)] + saved_specs, out_shape=[_sds((S, D), F32)] + saved_shapes,
            scratch_shapes=_vmem_like(wgu, wd),
            compiler_params=_params(("arbitrary",), 56), exchange=exchange)
    return _call(
        body, (x, gain, wgu, wd, *head), name=name, grid=(S // tm,),
        in_specs=[_rows(tm, D), _fixed((1, D)), _ANY, _ANY, _rows(tm, D), _fixed((1, D))],
        out_specs=[_rows(tm, D), _fixed((1, 128)), _fixed((1, D))] + saved_specs,
        out_shape=[_sds((S, D), F32), _sds((1, 128), F32), _sds((1, D), F32)] + saved_shapes,
        scratch_shapes=_vmem_like(wgu, wd),
        compiler_params=_params(("arbitrary",), 56), exchange=exchange, free=(4, 5))


def _ffn_bwd(dh, x, gain, gu, wgu, wd, name):
    tm = 256

    def body(dh_ref, x_ref, g_ref, gu_ref, wgu_hbm, wd_hbm, dx_ref, dgu_ref, dg_ref, wgu_ref, wd_ref):
        _stage([(wgu_hbm, wgu_ref), (wd_hbm, wd_ref)])
        dh = dh_ref[...]
        dhb = dh.astype(BF16)
        dn = jnp.zeros((tm, D), F32)
        for j in range(2):
            g = gu_ref[:, j * FFS:(j + 1) * FFS].astype(F32)
            u = gu_ref[:, (2 + j) * FFS:(3 + j) * FFS].astype(F32)
            da = 0.5 * _nt(dhb, wd_ref[j * FFS:(j + 1) * FFS, :])
            sg = jax.nn.sigmoid(g)
            dgb = (da * u * (sg * (1.0 + g * (1.0 - sg)))).astype(BF16)
            dub = (da * (g * sg)).astype(BF16)
            dgu_ref[:, j * FFS:(j + 1) * FFS] = dgb
            dgu_ref[:, (2 + j) * FFS:(3 + j) * FFS] = dub
            dn = dn + _nt(dgb, wgu_ref[j]) + _nt(dub, wgu_ref[2 + j])
        r, xr = _rms(x_ref[...])
        dx, dgain = _rms_bwd(dn, xr, r, g_ref[...])
        dx_ref[...] = dh + dx

        @pl.when(pl.program_id(0) == 0)
        def _():
            dg_ref[...] = jnp.zeros_like(dg_ref)

        dg_ref[...] += dgain

    return _call(
        body, (dh, x, gain, gu, wgu, wd), name=name, grid=(S // tm,),
        in_specs=[_rows(tm, D), _rows(tm, D), _fixed((1, D)), _rows(tm, 4 * FFS), _ANY, _ANY],
        out_specs=[_rows(tm, D), _rows(tm, 4 * FFS), _fixed((1, D))],
        out_shape=[_sds((S, D), F32), _sds((S, 4 * FFS), BF16), _sds((1, D), F32)],
        scratch_shapes=_vmem_like(wgu, wd), compiler_params=_params(("arbitrary",), 56))


def _ffn_bwd_act(dh, gu, wd, name, exchange=None, after=()):
    tm = 512

    def body(dh_ref, gu_ref, wd_hbm, dgu_ref, wd_ref):
        _stage([(wd_hbm, wd_ref)])
        dhb = dh_ref[...].astype(BF16)
        for j in range(2):
            g = gu_ref[:, j * FFS:(j + 1) * FFS].astype(F32)
            u = gu_ref[:, (2 + j) * FFS:(3 + j) * FFS].astype(F32)
            da = 0.5 * _nt(dhb, wd_ref[j * FFS:(j + 1) * FFS, :])
            sg = jax.nn.sigmoid(g)
            dgu_ref[:, j * FFS:(j + 1) * FFS] = (da * u * (sg * (1.0 + g * (1.0 - sg)))).astype(BF16)
            dgu_ref[:, (2 + j) * FFS:(3 + j) * FFS] = (da * (g * sg)).astype(BF16)

    res = _call(
        body, (dh, gu, wd), name=name, grid=(S // tm,),
        in_specs=[_rows(tm, D), _rows(tm, 4 * FFS), _ANY], out_specs=[_rows(tm, 4 * FFS)],
        out_shape=[_sds((S, 4 * FFS), BF16)], scratch_shapes=_vmem_like(wd),
        compiler_params=_params(("arbitrary",), 56), exchange=exchange, after=after)
    return res[0] if exchange is None else (res[0][0], res[1])


def _ffn_bwd_in(dh, x, gain, dgu, wgu, name, exchange=None, after=()):
    tm = 512

    def body(dh_ref, x_ref, g_ref, dgu_ref, wgu_hbm, dx_ref, dg_ref, wgu_ref):
        _stage([(wgu_hbm, wgu_ref)])
        dn = jnp.zeros((tm, D), F32)
        for j in range(NSH):
            dn = dn + _nt(dgu_ref[:, j * FFS:(j + 1) * FFS], wgu_ref[j])
        r, xr = _rms(x_ref[...])
        dx, dgain = _rms_bwd(dn, xr, r, g_ref[...])
        dx_ref[...] = dh_ref[...] + dx

        @pl.when(pl.program_id(0) == 0)
        def _():
            dg_ref[...] = jnp.zeros_like(dg_ref)

        dg_ref[...] += dgain

    return _call(
        body, (dh, x, gain, dgu, wgu), name=name, grid=(S // tm,),
        in_specs=[_rows(tm, D), _rows(tm, D), _fixed((1, D)), _rows(tm, 4 * FFS), _ANY],
        out_specs=[_rows(tm, D), _fixed((1, D))],
        out_shape=[_sds((S, D), F32), _sds((1, D), F32)],
        scratch_shapes=_vmem_like(wgu),
        compiler_params=_params(("arbitrary",), 56), exchange=exchange, after=after)


def _mix_in(h, gain, w_in, after=()):
    tm = 512

    def body(h_ref, g_ref, w_hbm, u_ref, xp_ref, q_ref, k_ref, v_ref, gp_ref, gs_ref, w_ref):
        _stage([(w_hbm, w_ref)])
        _, hr = _rms(h_ref[...])
        u = (hr * g_ref[...]).astype(BF16)
        u_ref[...] = u
        p0 = _nn(u, w_ref[0])
        xp_ref[...] = p0[:, :PW]
        q_ref[...] = p0[:, PW:].astype(BF16)
        p1 = _nn(u, w_ref[1])
        k_ref[...] = p1[:, :SBW].astype(BF16)
        v_ref[...] = p1[:, SBW:].astype(BF16)
        gp_ref[...] = jax.nn.sigmoid(_nn(u, w_ref[2])).astype(BF16)
        gs_ref[...] = jax.nn.sigmoid(_nn(u, w_ref[3])).astype(BF16)

    return _call(
        body, (h, gain, w_in), name="mix_in", grid=(S // tm,),
        in_specs=[_rows(tm, D), _fixed((1, D)), _ANY],
        out_specs=[_rows(tm, D), _rows(tm, PW), _rows(tm, SBW), _rows(tm, SBW), _rows(tm, SBW),
                   _rows(tm, D), _rows(tm, D)],
        out_shape=[_sds((S, D), BF16), _sds((S, PW), F32), _sds((S, SBW), BF16), _sds((S, SBW), BF16),
                   _sds((S, SBW), BF16), _sds((S, D), BF16), _sds((S, D), BF16)],
        scratch_shapes=_vmem_like(w_in),
        compiler_params=_params(("arbitrary",), 48), free=(1,), after=after)


def _hilo_dot(x, tri):
    hi = x.astype(BF16)
    lo = (x - hi.astype(F32)).astype(BF16)
    return _nn(hi, tri) + _nn(lo, tri)


def _log_terms(qk):
    z2 = qk * (SCALE * LOG2E)
    lb = jnp.minimum(z2, 0.0) - jnp.log2(1.0 + jnp.exp2(-jnp.abs(z2)))
    return lb, lb - z2


def _head_masks():
    lane = lax.broadcasted_iota(jnp.int32, (1, 2 * DH), 1)
    return (lane < DH, lane >= DH)


def _attn_fwd(q, k, v, exchange=None):
    T = TA

    def body(q_ref, k_ref, v_ref, o_ref, c_ref):
        i2 = 2 * pl.program_id(1)
        row = lax.broadcasted_iota(jnp.int32, (T, T), 0)
        col = lax.broadcasted_iota(jnp.int32, (T, T), 1)
        after = (row > col).astype(BF16)
        causal = col < row
        masks = _head_masks()
        qms = {}
        for b in range(QB):
            q2 = q_ref[b * T:(b + 1) * T, :]
            for h, hm in enumerate(masks):
                qms[b, h] = jnp.where(hm, q2, jnp.zeros_like(q2))

        def blocks(keys, pairs, carries, os):
            ks, vms = [], []
            for j in keys:
                rows = pl.ds(pl.multiple_of(j * T, T), T)
                vj = v_ref[rows, :]
                ks.append(k_ref[rows, :])
                vms.append([jnp.where(hm, vj, jnp.zeros_like(vj)) for hm in masks])
            units = [(n, h) for n in range(len(pairs)) for h in range(2)]
            qks = {(n, h): _nt(qms[pairs[n][0], h], ks[pairs[n][1]]) for n, h in units}
            lbs, l1ms = {}, {}
            for u in units:
                lbs[u], l1m = _log_terms(qks[u])
                l1ms[u] = jnp.where(causal, l1m, 0.0) if pairs[u[0]][2] else l1m
            cins = {u: _hilo_dot(l1ms[u], after) for u in units}
            carries, os = dict(carries), list(os)
            for n, h in units:
                b, key, diag = pairs[n]
                a = jnp.exp2(lbs[n, h] + cins[n, h] + carries[b, h])
                if diag:
                    a = jnp.where(causal, a, 0.0)
                os[b] = os[b] + _nn(a.astype(BF16), vms[key][h])
                carries[b, h] = carries[b, h] + jnp.sum(l1ms[n, h], axis=1, keepdims=True)
            return carries, tuple(os)

        carries = {(b, h): jnp.zeros((T, 1), F32) for b in range(QB) for h in range(2)}
        os = tuple(jnp.zeros((T, 2 * DH), F32) for _ in range(QB))
        carries, os = blocks([i2 + 1, i2], [(1, 0, True), (0, 1, True), (1, 1, False)], carries, os)
        carries, os = lax.fori_loop(
            0, i2 // 2,
            lambda t, c: blocks([i2 - 1 - 2 * t, i2 - 2 - 2 * t],
                                [(0, 0, False), (1, 0, False), (0, 1, False), (1, 1, False)], c[0], c[1]),
            (carries, os))
        for b in range(QB):
            o_ref[b * T:(b + 1) * T, :] = os[b].astype(BF16)
            c_ref[b * T:(b + 1) * T, :] = jnp.where(masks[0], carries[b, 0], carries[b, 1])

    blk = pl.BlockSpec((QB * T, 2 * DH), lambda p, i: (i, p))
    full = pl.BlockSpec((S, 2 * DH), lambda p, i: (0, p))
    return _call(
        body, (q, k, v), name="attn_fwd", grid=(SBW // (2 * DH), S // (QB * T)),
        in_specs=[blk, full, full], out_specs=[blk, blk],
        out_shape=[_sds((S, SBW), BF16), _sds((S, SBW), F32)],
        compiler_params=_params(("arbitrary", "arbitrary"), 40), exchange=exchange)


def _attn_bwd(q, k, v, do, ctot, after=()):
    T = TA
    nq = S // (QB * T)

    def body(q_ref, k_ref, v_ref, do_ref, c_ref, dq_ref, dk_ref, dv_ref, dk_acc, dv_acc):
        step = pl.program_id(1)
        i2 = 2 * step

        @pl.when(step == 0)
        def _():
            dk_acc[...] = jnp.zeros_like(dk_acc)
            dv_acc[...] = jnp.zeros_like(dv_acc)

        row = lax.broadcasted_iota(jnp.int32, (T, T), 0)
        col = lax.broadcasted_iota(jnp.int32, (T, T), 1)
        upto = (row <= col).astype(BF16)
        before = (row < col).astype(BF16)
        causal = col < row
        masks = _head_masks()
        qms, doms, ctots = {}, {}, {}
        for b in range(QB):
            q2, do2 = q_ref[b * T:(b + 1) * T, :], do_ref[b * T:(b + 1) * T, :]
            for h, hm in enumerate(masks):
                qms[b, h] = jnp.where(hm, q2, jnp.zeros_like(q2))
                doms[b, h] = jnp.where(hm, do2, jnp.zeros_like(do2))
                ctots[b, h] = c_ref[b * T:(b + 1) * T, h * DH:h * DH + 1]

        def blocks(keys, pairs, sums, dqs):
            rows = [pl.ds(pl.multiple_of(j * T, T), T) for j in keys]
            ks, vs = [k_ref[r, :] for r in rows], [v_ref[r, :] for r in rows]
            kms = [[jnp.where(hm, kj, jnp.zeros_like(kj)) for hm in masks] for kj in ks]
            units = [(n, h) for n in range(len(pairs)) for h in range(2)]
            qks = {(n, h): _nt(qms[pairs[n][0], h], ks[pairs[n][1]]) for n, h in units}
            das = {(n, h): _nt(doms[pairs[n][0], h], vs[pairs[n][1]]) for n, h in units}
            lbs, l1ms = {}, {}
            for u in units:
                lbs[u], l1m = _log_terms(qks[u])
                l1ms[u] = jnp.where(causal, l1m, 0.0) if pairs[u[0]][2] else l1m
            pins = {u: _hilo_dot(l1ms[u], upto) for u in units}
            sums = dict(sums)
            a_s, dls, cps = {}, {}, {}
            for n, h in units:
                b, _, diag = pairs[n]
                cl, cp = sums[b, h]
                a = jnp.exp2(lbs[n, h] + (ctots[b, h] - cl) - pins[n, h])
                if diag:
                    a = jnp.where(causal, a, 0.0)
                a_s[n, h] = a.astype(BF16)
                dls[n, h] = das[n, h] * a
                cps[n, h] = cp
                sums[b, h] = (cl + jnp.sum(l1ms[n, h], axis=1, keepdims=True),
                              cp + jnp.sum(dls[n, h], axis=1, keepdims=True))
            pexs = {u: _hilo_dot(dls[u], before) for u in units}
            dzbs = {}
            for u in units:
                dz = dls[u] - jnp.exp2(lbs[u]) * (dls[u] + pexs[u] + cps[u])
                if pairs[u[0]][2]:
                    dz = jnp.where(causal, dz, 0.0)
                dzbs[u] = dz.astype(BF16)
            dqs = list(dqs)
            for n, h in units:
                dqs[pairs[n][0]] = dqs[pairs[n][0]] + _nn(dzbs[n, h], kms[pairs[n][1]][h])
            for key, r in enumerate(rows):
                mine = [(n, h) for n, h in units if pairs[n][1] == key]
                dk_acc[r, :] += functools.reduce(jnp.add, [_tn(dzbs[u], qms[pairs[u[0]][0], u[1]]) for u in mine])
                dv_acc[r, :] += functools.reduce(jnp.add, [_tn(a_s[u], doms[pairs[u[0]][0], u[1]]) for u in mine])
            return sums, tuple(dqs)

        zero = jnp.zeros((T, 1), F32)
        sums = {(b, h): (zero, zero) for b in range(QB) for h in range(2)}
        dqs = tuple(jnp.zeros((T, 2 * DH), F32) for _ in range(QB))
        sums, dqs = lax.fori_loop(
            0, i2 // 2,
            lambda t, c: blocks([2 * t, 2 * t + 1],
                                [(0, 0, False), (1, 0, False), (0, 1, False), (1, 1, False)], c[0], c[1]),
            (sums, dqs))
        _, dqs = blocks([i2, i2 + 1], [(0, 0, True), (1, 0, False), (1, 1, True)], sums, dqs)
        for b in range(QB):
            dq_ref[b * T:(b + 1) * T, :] = (dqs[b] * SCALE).astype(BF16)

        @pl.when(step == nq - 1)
        def _():
            dk_ref[...] = (dk_acc[...] * SCALE).astype(BF16)
            dv_ref[...] = dv_acc[...].astype(BF16)

    blk = pl.BlockSpec((QB * T, 2 * DH), lambda p, i: (i, p))
    full = pl.BlockSpec((S, 2 * DH), lambda p, i: (0, p))
    return _call(
        body, (q, k, v, do, ctot), name="attn_bwd", grid=(SBW // (2 * DH), nq),
        in_specs=[blk, full, full, blk, blk], out_specs=[blk, full, full],
        out_shape=[_sds((S, SBW), BF16), _sds((S, SBW), BF16), _sds((S, SBW), BF16)],
        scratch_shapes=[pltpu.VMEM((S, 2 * DH), F32), pltpu.VMEM((S, 2 * DH), F32)],
        compiler_params=_params(("arbitrary", "arbitrary"), 40), after=after)


def _pool_counts(first_row, tm):
    pos = first_row + lax.broadcasted_iota(jnp.int32, (tm, 1), 0)
    return [jnp.minimum(pos + 1, w).astype(F32) for w in POOL_WINDOWS]


def _mix_out(h, xp, o_sb, gp, gs, w_group, scale, w_bp, w_ba, w_out, exchange=None):
    tm = 512

    def body(h_ref, xp_ref, o_ref, gp_ref, gs_ref, wg_hbm, sc_ref, wbp_hbm, wba_hbm, wo_hbm,
             h2_ref, pm_ref, p_ref, yp_ref, ys_ref, m_ref, halo, wg_ref, wbp_ref, wba_ref, wo_ref):
        _stage([(wg_hbm, wg_ref), (wbp_hbm, wbp_ref), (wba_hbm, wba_ref), (wo_hbm, wo_ref)])
        i = pl.program_id(0)

        @pl.when(i == 0)
        def _():
            halo[...] = jnp.zeros_like(halo)

        xp = xp_ref[...]
        ext = jnp.concatenate([halo[...], xp], axis=0)
        halo[...] = xp[tm - HALO:, :]
        counts = _pool_counts(i * tm, tm)
        for gi in range(len(POOL_WINDOWS)):
            lanes = slice(gi * PG, (gi + 1) * PG)
            win = ext[:, lanes]
            for step in range(gi + 1):
                win = win + pltpu.roll(win, 1 << step, 0)
            pm = (win[HALO:, :] / counts[gi] - xp[:, lanes]).astype(BF16)
            pm_ref[:, lanes] = pm
            p_ref[:, lanes] = (_nn(pm, wg_ref[gi]) * sc_ref[:, lanes]).astype(BF16)
        pb = p_ref[...]
        ob = o_ref[...]
        for j in range(NSH):
            cols = slice(j * (D // NSH), (j + 1) * (D // NSH))
            yp = _nn(pb, wbp_ref[j])
            ys = _nn(ob, wba_ref[j])
            yp_ref[:, cols] = yp.astype(BF16)
            ys_ref[:, cols] = ys.astype(BF16)
            m_ref[:, cols] = (gp_ref[:, cols].astype(F32) * yp + gs_ref[:, cols].astype(F32) * ys).astype(BF16)
        h2_ref[...] = h_ref[...] + _nn(m_ref[...], wo_ref[...])

    return _call(
        body, (h, xp, o_sb, gp, gs, w_group, scale, w_bp, w_ba, w_out), name="mix_out", grid=(S // tm,),
        in_specs=[_rows(tm, D), _rows(tm, PW), _rows(tm, SBW), _rows(tm, D), _rows(tm, D),
                  _ANY, _fixed((1, PW)), _ANY, _ANY, _ANY],
        out_specs=[_rows(tm, D), _rows(tm, PW), _rows(tm, PW), _rows(tm, D), _rows(tm, D), _rows(tm, D)],
        out_shape=[_sds((S, D), F32), _sds((S, PW), BF16), _sds((S, PW), BF16), _sds((S, D), BF16),
                   _sds((S, D), BF16), _sds((S, D), BF16)],
        scratch_shapes=[pltpu.VMEM((HALO, PW), F32)] + _vmem_like(w_group, w_bp, w_ba, w_out),
        compiler_params=_params(("arbitrary",), 48), free=(5, 6), exchange=exchange)


def _mix_bwd_out(dh, gp, gs, yp, ys, pm, w_group, scale, w_bp, w_ba, w_out, exchange=None):
    tm = 512
    nt = S // tm

    def body(dh_ref, gp_ref, gs_ref, yp_ref, ys_ref, pm_ref, wg_hbm, sc_ref, wbp_hbm, wba_hbm, wo_hbm,
             dlg_ref, dyp_ref, dys_ref, do_ref, dyg_ref, dxp_ref, dsc_ref, halo, wg_ref, wbp_ref, wba_ref, wo_ref):
        _stage([(wg_hbm, wg_ref), (wbp_hbm, wbp_ref), (wba_hbm, wba_ref), (wo_hbm, wo_ref)])
        step = pl.program_id(0)

        @pl.when(step == 0)
        def _():
            halo[...] = jnp.zeros_like(halo)
            dsc_ref[...] = jnp.zeros_like(dsc_ref)

        dm = _nt(dh_ref[...].astype(BF16), wo_ref[...])
        gp = gp_ref[...].astype(F32)
        gs = gs_ref[...].astype(F32)
        yp = yp_ref[...].astype(F32)
        ys = ys_ref[...].astype(F32)
        dlg_ref[:, :D] = (dm * yp * gp * (1.0 - gp)).astype(BF16)
        dlg_ref[:, D:] = (dm * ys * gs * (1.0 - gs)).astype(BF16)
        dyp_ref[...] = (dm * gp).astype(BF16)
        dys_ref[...] = (dm * gs).astype(BF16)
        dp = jnp.zeros((tm, PW), F32)
        do = jnp.zeros((tm, SBW), F32)
        for j in range(NSH):
            cols = slice(j * (D // NSH), (j + 1) * (D // NSH))
            dp = dp + _nt(dyp_ref[:, cols], wbp_ref[j])
            do = do + _nt(dys_ref[:, cols], wba_ref[j])
        do_ref[...] = do.astype(BF16)
        counts = _pool_counts((nt - 1 - step) * tm, tm)
        dscale = []
        for gi in range(len(POOL_WINDOWS)):
            lanes = slice(gi * PG, (gi + 1) * PG)
            dpg = dp[:, lanes]
            dscale.append(jnp.sum(dpg * _nn(pm_ref[:, lanes], wg_ref[gi]), axis=0, keepdims=True))
            dyg = (dpg * sc_ref[:, lanes]).astype(BF16)
            dyg_ref[:, lanes] = dyg
            dpm = _nt(dyg, wg_ref[gi])
            per = dpm / counts[gi]
            win = jnp.concatenate([per, halo[:, lanes]], axis=0)
            halo[:, lanes] = per[:HALO, :]
            for s in range(gi + 1):
                win = win + pltpu.roll(win, tm + HALO - (1 << s), 0)
            dxp_ref[:, lanes] = (win[:tm, :] - dpm).astype(BF16)
        dsc_ref[...] += jnp.concatenate(dscale, axis=1)

    rev = lambda width: pl.BlockSpec((tm, width), lambda i: (nt - 1 - i, 0))
    return _call(
        body, (dh, gp, gs, yp, ys, pm, w_group, scale, w_bp, w_ba, w_out), name="mix_bwd_out", grid=(nt,),
        in_specs=[rev(D), rev(D), rev(D), rev(D), rev(D), rev(PW), _ANY, _fixed((1, PW)), _ANY, _ANY, _ANY],
        out_specs=[rev(2 * D), rev(D), rev(D), rev(SBW), rev(PW), rev(PW), _fixed((1, PW))],
        out_shape=[_sds((S, 2 * D), BF16), _sds((S, D), BF16), _sds((S, D), BF16), _sds((S, SBW), BF16),
                   _sds((S, PW), BF16), _sds((S, PW), BF16), _sds((1, PW), F32)],
        scratch_shapes=[pltpu.VMEM((HALO, PW), F32)] + _vmem_like(w_group, w_bp, w_ba, w_out),
        compiler_params=_params(("arbitrary",), 48), exchange=exchange)


def _mix_bwd_in(dh, h, gain, pieces, w_in, exchange=None):
    tm = 512
    widths = [p.shape[1] for p in pieces]

    def body(dh_ref, h_ref, g_ref, *rest):
        piece_refs, (w_hbm, dx_ref, dg_ref, w_ref, dp_ref) = rest[:len(pieces)], rest[len(pieces):]
        _stage([(w_hbm, w_ref)])
        at = 0
        for ref, width in zip(piece_refs, widths):
            dp_ref[:, at:at + width] = ref[...]
            at += width
        du = jnp.zeros((tm, D), F32)
        for j in range(NSH):
            du = du + _nt(dp_ref[:, j * D:(j + 1) * D], w_ref[j])
        r, hr = _rms(h_ref[...])
        dx, dgain = _rms_bwd(du, hr, r, g_ref[...])
        dx_ref[...] = dh_ref[...] + dx

        @pl.when(pl.program_id(0) == 0)
        def _():
            dg_ref[...] = jnp.zeros_like(dg_ref)

        dg_ref[...] += dgain

    return _call(
        body, (dh, h, gain, *pieces, w_in), name="mix_bwd_in", grid=(S // tm,),
        in_specs=[_rows(tm, D), _rows(tm, D), _fixed((1, D))] + [_rows(tm, w) for w in widths] + [_ANY],
        out_specs=[_rows(tm, D), _fixed((1, D))],
        out_shape=[_sds((S, D), F32), _sds((1, D), F32)],
        scratch_shapes=_vmem_like(w_in) + [pltpu.VMEM((tm, 4 * D), BF16)],
        compiler_params=_params(("arbitrary",), 48), exchange=exchange)


def _wgrad_in(u, pieces):
    dxp, dq, dk, dv, dlg = pieces

    def body(u_ref, dxp_ref, dq_ref, dk_ref, dv_ref, dlg_ref, o_ref):
        j = pl.program_id(0)
        u = u_ref[...]

        def two(left_ref, right_ref):
            o_ref[:, :PW] = _tn(u, left_ref[...]).astype(BF16)
            o_ref[:, PW:] = _tn(u, right_ref[...]).astype(BF16)

        pl.when(j == 0)(lambda: two(dxp_ref, dq_ref))
        pl.when(j == 1)(lambda: two(dk_ref, dv_ref))

        @pl.when(j >= 2)
        def _():
            o_ref[...] = _tn(u, dlg_ref[...]).astype(BF16)

    whole = lambda width: pl.BlockSpec((S, width), lambda j: (0, 0))
    return _call(
        body, (u, dxp, dq, dk, dv, dlg), name="wgrad_in", grid=(NSH,),
        in_specs=[whole(D), whole(PW), whole(SBW), whole(SBW), whole(SBW),
                  pl.BlockSpec((S, D), lambda j: (0, jnp.maximum(j - 2, 0)))],
        out_specs=[pl.BlockSpec((None, D, D), lambda j: (j, 0, 0))], out_shape=[_sds((NSH, D, D), BF16)],
        compiler_params=_params(("arbitrary",), 56))[0]


def _wgrad(a, b, nblk, ti, name, out_dtype=BF16, exchange=None, after=()):
    ka, n = a.shape[1], b.shape[1]
    ns = n // nblk

    def body(a_ref, b_ref, o_ref):
        o_ref[...] = _tn(a_ref[...].astype(BF16), b_ref[...].astype(BF16)).astype(out_dtype)

    res = _call(
        body, (a, b), name=name, grid=(nblk, ka // ti),
        in_specs=[pl.BlockSpec((S, ti), lambda j, i: (0, i)), pl.BlockSpec((S, ns), lambda j, i: (0, j))],
        out_specs=[pl.BlockSpec((None, ti, ns), lambda j, i: (j, i, 0))],
        out_shape=[_sds((nblk, ka, ns), out_dtype)],
        compiler_params=_params(("arbitrary", "arbitrary"), 56), exchange=exchange, after=after)
    return res[0] if exchange is None else (res[0][0], res[1])


def _wgrad_branches(p, dyp, o_sb, dys, pm, dyg):
    cols = D // NSH

    def body(p_ref, dyp_ref, o_ref, dys_ref, pm_ref, dyg_ref, gbp_ref, gba_ref, gg_ref):
        gbp_ref[...] = _tn(p_ref[...], dyp_ref[...]).astype(BF16)
        gba_ref[...] = _tn(o_ref[...], dys_ref[...]).astype(BF16)
        gg_ref[...] = _tn(pm_ref[...], dyg_ref[...])

    whole = lambda width: pl.BlockSpec((S, width), lambda j: (0, 0))
    col = lambda width: pl.BlockSpec((S, width), lambda j: (0, j))
    return _call(
        body, (p, dyp, o_sb, dys, pm, dyg), name="wgrad_branches", grid=(NSH,),
        in_specs=[whole(PW), col(cols), whole(SBW), col(cols), col(PG), col(PG)],
        out_specs=[pl.BlockSpec((None, PW, cols), lambda j: (j, 0, 0)),
                   pl.BlockSpec((None, SBW, cols), lambda j: (j, 0, 0)),
                   pl.BlockSpec((None, PG, PG), lambda j: (j, 0, 0))],
        out_shape=[_sds((NSH, PW, cols), BF16), _sds((NSH, SBW, cols), BF16), _sds((NSH, PG, PG), F32)],
        compiler_params=_params(("arbitrary",), 40))


def _place():
    x, y, c = lax.axis_index("x"), lax.axis_index("y"), lax.axis_index("c")
    chips = [(1 - x, y), (x, 1 - y), (1 - x, 1 - y)]
    return x, y, c, chips


def _remote(src, dst, ssem, rsem, dev):
    return pltpu.make_async_remote_copy(src_ref=src, dst_ref=dst, send_sem=ssem, recv_sem=rsem,
                                        device_id=dev, device_id_type=MESH)


def _cast_into_block(ws, me_idx, name):
    steps = 4
    shapes = [(w.shape[0] // steps, w.shape[1]) for w in ws]

    def body(me_ref, *refs):
        for w_ref, o_ref in zip(refs[:len(ws)], refs[len(ws):]):
            o_ref[...] = w_ref[...].astype(BF16)

    return pl.pallas_call(
        body, name=name, out_shape=[_sds((NSH,) + w.shape, BF16) for w in ws],
        grid_spec=pltpu.PrefetchScalarGridSpec(
            num_scalar_prefetch=1, grid=(steps,),
            in_specs=[pl.BlockSpec((r, c), lambda s, me: (s, 0)) for r, c in shapes],
            out_specs=[pl.BlockSpec((None, r, c), lambda s, me: (me[0], s, 0)) for r, c in shapes]),
        compiler_params=_params(("arbitrary",), 32),
    )(me_idx, *ws)


def _ex_gather(bufs):
    n = len(bufs)
    per = 8

    def plan(outs, ssem, rsem, w):
        x, y, c, _ = _place()
        sib, nbr_x, nbr_y = (x, y, 1 - c), (1 - x, y, c), (x, 1 - y, c)
        half = outs[w].shape[1] // 2
        quarter = half // 2
        sem = lambda k: (ssem.at[per * w + k], rsem.at[per * w + k])
        rows = lambda blk, start, size: outs[w].at[blk, pl.ds(start, size)]
        mine = rows(2 * x + y, c * half, half)
        from_x = rows(2 * (1 - x) + y, c * half, half)
        from_y = rows(2 * x + (1 - y), c * half, half)
        diag = 2 * (1 - x) + (1 - y)
        pass_y = rows(2 * (1 - x) + y, c * half, quarter)
        pass_x = rows(2 * x + (1 - y), c * half + quarter, quarter)
        diag_0, diag_1 = rows(diag, c * half, quarter), rows(diag, c * half + quarter, quarter)
        first = [_remote(mine, mine, *sem(0), nbr_x), _remote(mine, mine, *sem(1), nbr_y)]
        arrivals = [
            (_remote(from_x, from_x, *sem(0), nbr_x),
             [_remote(pass_y, pass_y, *sem(2), nbr_y), _remote(from_x, from_x, *sem(4), sib)]),
            (_remote(from_y, from_y, *sem(1), nbr_y),
             [_remote(pass_x, pass_x, *sem(3), nbr_x), _remote(from_y, from_y, *sem(5), sib)]),
            (_remote(diag_0, diag_0, *sem(2), nbr_y), [_remote(diag_0, diag_0, *sem(6), sib)]),
            (_remote(diag_1, diag_1, *sem(3), nbr_x), [_remote(diag_1, diag_1, *sem(7), sib)]),
        ]
        other = (1 - c) * half
        from_sibling = [
            _remote(rows(2 * (1 - x) + y, other, half), rows(2 * (1 - x) + y, other, half), *sem(4), sib),
            _remote(rows(2 * x + (1 - y), other, half), rows(2 * x + (1 - y), other, half), *sem(5), sib),
            _remote(rows(diag, other, quarter), rows(diag, other, quarter), *sem(6), sib),
            _remote(rows(diag, other + quarter, quarter), rows(diag, other + quarter, quarter), *sem(7), sib),
        ]
        return first, arrivals, from_sibling

    def start(ins, outs, ssem, rsem):
        x, y, c, _ = _place()
        for w in range(n):
            half = outs[w].shape[1] // 2
            mine = outs[w].at[2 * x + y, pl.ds(c * half, half)]
            _remote(mine, mine, ssem.at[per * w], rsem.at[per * w], (1 - x, y, c)).start()
            _remote(mine, mine, ssem.at[per * w + 1], rsem.at[per * w + 1], (x, 1 - y, c)).start()

    def finish(ins, outs, ssem, rsem):
        plans = [plan(outs, ssem, rsem, w) for w in range(n)]
        started = []
        for direct in (True, False):
            for first, arrivals, _ in plans:
                for arrived, onward in (arrivals[:2] if direct else arrivals[2:]):
                    arrived.wait_recv()
                    for cp in onward:
                        cp.start()
                    started += onward
        for first, _, from_sibling in plans:
            for cp in from_sibling:
                cp.wait_recv()
            started += first
        for cp in started:
            cp.wait_send()

    return Exchange(bufs, [_sds(b.shape, b.dtype) for b in bufs], {w: w for w in range(n)}, per * n, start, finish)


def _ex_gather_direct(bufs):
    n = len(bufs)

    def copies(outs, ssem, rsem, only_first=False):
        x, y, c, chips = _place()
        me, sib = 2 * x + y, (x, y, 1 - c)
        first, relay, last = [], [], []
        for w in range(n):
            half = outs[w].shape[1] // 2
            mine = outs[w].at[me, pl.ds(c * half, half)]
            for k, (px, py) in enumerate(chips):
                sems = (ssem.at[6 * w + k], rsem.at[6 * w + k])
                sib_sems = (ssem.at[6 * w + 3 + k], rsem.at[6 * w + 3 + k])
                first.append(_remote(mine, mine, *sems, (px, py, c)))
                if only_first:
                    continue
                got = outs[w].at[2 * px + py, pl.ds(c * half, half)]
                relay.append((_remote(got, got, *sems, (px, py, c)), _remote(got, got, *sib_sems, sib)))
                theirs = outs[w].at[2 * px + py, pl.ds((1 - c) * half, half)]
                last.append(_remote(theirs, theirs, *sib_sems, sib))
        return first, relay, last

    def start(ins, outs, ssem, rsem):
        for cp in copies(outs, ssem, rsem, only_first=True)[0]:
            cp.start()

    def finish(ins, outs, ssem, rsem):
        first, relay, last = copies(outs, ssem, rsem)
        for arrived, onward in relay:
            arrived.wait_recv()
            onward.start()
        for cp in last:
            cp.wait_recv()
        for cp in first:
            cp.wait_send()
        for _, onward in relay:
            onward.wait_send()

    return Exchange(bufs, [_sds(b.shape, b.dtype) for b in bufs], {w: w for w in range(n)}, 6 * n, start, finish)


def _simple_exchange(arrays, landing, aliases, make_copies, sibling_only=False):
    def start(ins, outs, ssem, rsem):
        for cp, _ in make_copies(ins, outs, ssem, rsem, False):
            cp.start()

    def finish(ins, outs, ssem, rsem):
        cps = make_copies(ins, outs, ssem, rsem, True)
        for _, landed in cps:
            landed.wait_recv()
        for cp, _ in cps:
            cp.wait_send()

    return Exchange(arrays, landing, aliases, len(arrays) * 3, start, finish, sibling_only)


def _ex_pair_swap(grads):
    def make(ins, outs, ssem, rsem, landing):
        x, y, c, _ = _place()
        cps = [_remote(ins[w].at[:, 1 - c], outs[w], ssem.at[w], rsem.at[w], (x, y, 1 - c))
               for w in range(len(grads))]
        return [(cp, cp) for cp in cps]

    return _simple_exchange(grads, [_sds((NSH,) + g.shape[2:], g.dtype) for g in grads], {}, make, True)


def _ex_relay(bufs):
    def make(ins, outs, ssem, rsem, landing):
        x, y, c, chips = _place()
        sib = (x, y, 1 - c)
        out = []
        for w in range(len(bufs)):
            half = outs[w].shape[1] // 2
            for k, (px, py) in enumerate(chips):
                sems = (ssem.at[3 * w + k], rsem.at[3 * w + k])
                have = outs[w].at[2 * px + py, pl.ds(c * half, half)]
                miss = outs[w].at[2 * px + py, pl.ds((1 - c) * half, half)]
                out.append((_remote(have, have, *sems, sib), _remote(miss, miss, *sems, sib) if landing else None))
        return out

    return _simple_exchange(bufs, [_sds(b.shape, b.dtype) for b in bufs], {w: w for w in range(len(bufs))}, make, True)


def _ex_share(bufs):
    def make(ins, outs, ssem, rsem, landing):
        x, y, c, _ = _place()
        sib = (x, y, 1 - c)
        return [(_remote(outs[w].at[c], outs[w].at[c], ssem.at[w], rsem.at[w], sib),
                 _remote(outs[w].at[1 - c], outs[w].at[1 - c], ssem.at[w], rsem.at[w], sib) if landing else None)
                for w in range(len(bufs))]

    return _simple_exchange(bufs, [_sds(b.shape, b.dtype) for b in bufs], {w: w for w in range(len(bufs))}, make, True)


def _small_copies(slots, ssems, rsems, sending):
    x, y, c, _ = _place()
    out = []
    for m in range(1, 8):
        px, py, pc = x ^ (m >> 2), y ^ ((m >> 1) & 1), c ^ (m & 1)
        slot = slots.at[4 * x + 2 * y + c if sending else 4 * px + 2 * py + pc]
        out.append(_remote(slot, slot, ssems[m - 1], rsems[m - 1], (px, py, pc)))
    return out


def _small_gather_start(slots, name):
    def body(*refs):
        for cp in _small_copies(refs[0], refs[1:8], refs[8:15], True):
            cp.start()
        refs[-1][...] = jnp.zeros_like(refs[-1])

    outs = pl.pallas_call(
        body, name=name,
        out_shape=([pltpu.SemaphoreType.DMA(())] * 14 + [pltpu.HBM(slots.shape, slots.dtype)]
                   + [jax.ShapeDtypeStruct((8, 128), F32)]),
        in_specs=[_HBM], out_specs=[_SEM] * 14 + [_HBM, _VM], input_output_aliases={0: 14},
        compiler_params=pltpu.CompilerParams(has_side_effects=_EFFECT),
    )(*_in_hbm([slots]))
    return outs[:14], outs[14], outs[15]


def _small_gather_wait(sems, slots, after, name):
    def body(*refs):
        for cp in _small_copies(refs[0], refs[1:8], refs[8:15], True):
            cp.wait_send()
        for cp in _small_copies(refs[0], refs[1:8], refs[8:15], False):
            cp.wait_recv()

    return pl.pallas_call(
        body, name=name, out_shape=pltpu.HBM(slots.shape, slots.dtype),
        in_specs=[_HBM] + [_SEM] * 14 + [_ANY] * len(after), out_specs=_HBM, input_output_aliases={0: 0},
        compiler_params=pltpu.CompilerParams(has_side_effects=_EFFECT),
    )(slots, *sems, *after)


def _pair_sum(grads, gots, c_idx, name):
    n = len(grads)

    def body(c_ref, *refs):
        for a_ref, b_ref, o_ref in zip(refs[:n], refs[n:2 * n], refs[2 * n:]):
            o_ref[...] = (a_ref[...].astype(F32) + b_ref[...].astype(F32)).astype(BF16)

    halves = [g.shape[2:] for g in grads]
    return list(pl.pallas_call(
        body, name=name, out_shape=[_sds((NSH,) + h, BF16) for h in halves],
        grid_spec=pltpu.PrefetchScalarGridSpec(
            num_scalar_prefetch=1, grid=(NSH,),
            in_specs=[pl.BlockSpec((None, None) + h, lambda j, c: (j, c[0], 0, 0)) for h in halves]
            + [pl.BlockSpec((None,) + h, lambda j, c: (j, 0, 0)) for h in halves],
            out_specs=[pl.BlockSpec((None,) + h, lambda j, c: (j, 0, 0)) for h in halves]),
        compiler_params=_params(("arbitrary",), 40),
    )(c_idx, *_in_hbm(list(grads) + list(gots))))


def _chip_sum(owns, gots, place, name):
    n = len(owns)

    def body(place_ref, *refs):
        for own_ref, got_ref, o_ref in zip(refs[:n], refs[n:2 * n], refs[2 * n:]):
            acc = own_ref[...].astype(F32)
            for k in range(3):
                acc = acc + got_ref[k].astype(F32)
            o_ref[...] = acc

    shapes = [(o.shape[1] // 2, o.shape[2]) for o in owns]
    return list(pl.pallas_call(
        body, name=name, out_shape=[_sds((2, 2 * r, c), F32) for r, c in shapes],
        grid_spec=pltpu.PrefetchScalarGridSpec(
            num_scalar_prefetch=1, grid=(2,),
            in_specs=[pl.BlockSpec((None, r, c), lambda s, p: (p[0], s, 0)) for r, c in shapes]
            + [pl.BlockSpec((3, r, c), lambda s, p: (0, s, 0)) for r, c in shapes],
            out_specs=[pl.BlockSpec((None, r, c), lambda s, p: (p[1], s, 0)) for r, c in shapes]),
        compiler_params=_params(("arbitrary",), 40),
    )(place, *_in_hbm(list(owns) + list(gots))))


def _adamw_math(w, g, m, v):
    m = B1 * m + (1.0 - B1) * g
    v = B2 * v + (1.0 - B2) * (g * g)
    m_hat = m / (1.0 - B1 ** STEP)
    v_hat = v / (1.0 - B2 ** STEP)
    return -LR * (m_hat / (jnp.sqrt(v_hat) + AEPS) + WD * w), m, v


def _adamw(ws, gs, ms, vs, name, after=()):
    n, steps = len(ws), 4

    def body(*refs):
        ins, outs = refs[:4 * n], refs[4 * n:]
        for i in range(n):
            w_ref, g_ref, m_ref, v_ref = ins[4 * i:4 * i + 4]
            go_ref, d_ref, nm_ref, nv_ref = outs[4 * i:4 * i + 4]
            g = g_ref[...]
            go_ref[...] = g
            d_ref[...], nm_ref[...], nv_ref[...] = _adamw_math(w_ref[...], g, m_ref[...], v_ref[...])

    args, specs, shapes, free = [], [], [], []
    for i, (w, g, m, v) in enumerate(zip(ws, gs, ms, vs)):
        args += [w, g, m, v]
        specs += [pl.BlockSpec((w.shape[0] // steps, w.shape[1]), lambda r: (r, 0))] * 4
        shapes += [_sds(w.shape, F32)] * 4
        free += [4 * i, 4 * i + 2, 4 * i + 3]
    outs = _call(body, args, name=name, grid=(steps,), out_shape=shapes, in_specs=specs, out_specs=specs,
                 compiler_params=_params(("arbitrary",), 48), free=tuple(free), after=after)
    return [outs[4 * i:4 * i + 4] for i in range(n)]


def _small_update(gathered, w, m, v, entries):
    rows = w.shape[0]

    def body(ga_ref, w_ref, m_ref, v_ref, *out_refs):
        for j, (first, n) in enumerate(entries):
            mine = slice(first, first + n)
            g = ga_ref[mine, :]
            for dev in range(1, 8):
                g = g + ga_ref[dev * rows + first:dev * rows + first + n, :]
            results = (g,) + _adamw_math(w_ref[mine, :], g, m_ref[mine, :], v_ref[mine, :])
            for i, res in enumerate(results):
                out_refs[i * len(entries) + j][...] = res

    outs = pl.pallas_call(
        body, name="small_update",
        out_shape=[jax.ShapeDtypeStruct((n, 128), F32) for _ in range(4) for _, n in entries],
        in_specs=[_VM] * 4, out_specs=[_VM] * (4 * len(entries)),
    )(gathered, w, m, v)
    return [outs[i * len(entries):(i + 1) * len(entries)] for i in range(4)]


SMALL = ("ffn1_norm", "mix_norm", "ffn2_norm", "final_norm", "pool_scale", "loss", "pool_w_group")
BIG = ("ffn1_w_gate_up", "ffn1_w_down", "w_in", "w_branch_pool", "w_branch_attn", "w_out",
       "ffn2_w_gate_up", "ffn2_w_down")
ORDER = ("ffn1_norm", "ffn1_w_gate_up", "ffn1_w_down", "mix_norm", "w_in", "pool_w_group", "pool_scale",
         "w_branch_pool", "w_branch_attn", "w_out", "ffn2_norm", "ffn2_w_gate_up", "ffn2_w_down", "final_norm")
SMALL_ROWS = 560


def _pack_small(t):
    parts = []
    for k in SMALL:
        rows = t[k].reshape(-1, 128) if k in t else jnp.zeros((1, 128), F32)
        parts.append(jnp.pad(rows, ((0, -rows.shape[0] % 8), (0, 0))))
    packed = jnp.concatenate(parts, axis=0)
    assert packed.shape == (SMALL_ROWS, 128), packed.shape
    return packed


def _small_entries(like):
    out, at = [], 0
    for k in SMALL:
        n = like[k].size // 128 if k in like else 1
        out.append((at, n))
        at += n + (-n % 8)
    return out


def _halves(g):
    return g.reshape(NSH, 2, g.shape[1] // 2, g.shape[2])


def kernel(x, ffn1_norm, ffn1_w_gate_up, ffn1_w_down, mix_norm, w_in, pool_w_group, pool_scale, w_branch_pool, w_branch_attn, w_out, ffn2_norm, ffn2_w_gate_up, ffn2_w_down, final_norm, loss_target, m_ffn1_norm, m_ffn1_w_gate_up, m_ffn1_w_down, m_mix_norm, m_w_in, m_pool_w_group, m_pool_scale, m_w_branch_pool, m_w_branch_attn, m_w_out, m_ffn2_norm, m_ffn2_w_gate_up, m_ffn2_w_down, m_final_norm, v_ffn1_norm, v_ffn1_w_gate_up, v_ffn1_w_down, v_mix_norm, v_w_in, v_pool_w_group, v_pool_scale, v_w_branch_pool, v_w_branch_attn, v_w_out, v_ffn2_norm, v_ffn2_w_gate_up, v_ffn2_w_down, v_final_norm):
    wts = dict(ffn1_norm=ffn1_norm, ffn1_w_gate_up=ffn1_w_gate_up, ffn1_w_down=ffn1_w_down, mix_norm=mix_norm,
               w_in=w_in, pool_w_group=pool_w_group, pool_scale=pool_scale, w_branch_pool=w_branch_pool,
               w_branch_attn=w_branch_attn, w_out=w_out, ffn2_norm=ffn2_norm, ffn2_w_gate_up=ffn2_w_gate_up,
               ffn2_w_down=ffn2_w_down, final_norm=final_norm)
    mom = dict(ffn1_norm=m_ffn1_norm, ffn1_w_gate_up=m_ffn1_w_gate_up, ffn1_w_down=m_ffn1_w_down,
               mix_norm=m_mix_norm, w_in=m_w_in, pool_w_group=m_pool_w_group, pool_scale=m_pool_scale,
               w_branch_pool=m_w_branch_pool, w_branch_attn=m_w_branch_attn, w_out=m_w_out,
               ffn2_norm=m_ffn2_norm, ffn2_w_gate_up=m_ffn2_w_gate_up, ffn2_w_down=m_ffn2_w_down,
               final_norm=m_final_norm)
    var = dict(ffn1_norm=v_ffn1_norm, ffn1_w_gate_up=v_ffn1_w_gate_up, ffn1_w_down=v_ffn1_w_down,
               mix_norm=v_mix_norm, w_in=v_w_in, pool_w_group=v_pool_w_group, pool_scale=v_pool_scale,
               w_branch_pool=v_w_branch_pool, w_branch_attn=v_w_branch_attn, w_out=v_w_out,
               ffn2_norm=v_ffn2_norm, ffn2_w_gate_up=v_ffn2_w_gate_up, ffn2_w_down=v_ffn2_w_down,
               final_norm=v_final_norm)

    c_idx = lax.axis_index("c").astype(jnp.int32).reshape(1)
    me_idx = (2 * lax.axis_index("x") + lax.axis_index("y")).astype(jnp.int32).reshape(1)
    place = jnp.concatenate([me_idx, c_idx])
    x0, tgt = x[0], loss_target[0]
    wgrp = pool_w_group[0].astype(BF16)
    g1, gm, g2, gf = ffn1_norm, mix_norm, ffn2_norm, final_norm.reshape(1, D)
    grad, delta, new_m, new_v = {}, {}, {}, {}

    def pair_sums(keys, parts, got):
        return _pair_sum(parts, got, c_idx, "pair_sum_" + keys[0])

    def chip_sums(keys, chip_parts, owned):
        return _chip_sum(chip_parts, owned, place, "chip_sum_" + keys[0])

    def adamw(keys, after=()):
        outs = _adamw([wts[k][0] for k in keys], [grad[k][0] for k in keys], [mom[k][0] for k in keys],
                      [var[k][0] for k in keys], "adamw_" + keys[0], after=after)
        for k, res in zip(keys, outs):
            grad[k], delta[k], new_m[k], new_v[k] = (o.reshape(wts[k].shape) for o in res)

    first, late = ("ffn1_w_gate_up", "ffn1_w_down"), ("w_branch_pool", "w_branch_attn", "w_out",
                                                       "ffn2_w_gate_up", "ffn2_w_down")
    own = {}
    for group in (first, ("w_in",), late):
        own.update(zip(group, _cast_into_block([wts[k][0] for k in group], me_idx, "cast_" + group[0])))
    full = dict(zip(first, _exchange_alone(_ex_gather([own[k] for k in first]), "gather_ffn1")))
    wgu1, wd1 = full["ffn1_w_gate_up"], full["ffn1_w_down"].reshape(DFF, D)
    (h1, n1, gu1, a1), (win,) = _ffn_fwd(x0, g1, wgu1, wd1, "ffn1_fwd", exchange=_ex_gather_direct([own["w_in"]]))
    sems_l, thru_l, token_l = _gather_start([own[k_] for k_ in late], [h1], "gather_late_start")
    u, xp, q, k, v, gp, gs = _mix_in(h1, gm, win, after=(token_l,))
    o_sb, ctot = _attn_fwd(q, k, v)
    arrived = _gather_wait(sems_l, thru_l, [o_sb], "gather_late_wait")
    wbp, wba, wout = _exchange_alone(_ex_relay(arrived[:3]), "relay_mix")
    wout = wout.reshape(D, D)
    (h2, pm, p, yp, ys, mm), (wgu2, wd2) = _mix_out(h1, xp, o_sb, gp, gs, wgrp, pool_scale, wbp, wba, wout,
                                                    exchange=_ex_relay(arrived[3:]))
    wd2 = wd2.reshape(DFF, D)
    dh3, loss_row, d_gf, n3, gu3, a3 = _ffn_fwd(h2, g2, wgu2, wd2, "ffn2_fwd", head=(tgt, gf))

    def grad_gate_up(n, dgu, name, exchange=None):
        res = _wgrad(n, dgu, NSH, D, name, exchange=exchange)
        return [_halves(res)] if exchange is None else ([_halves(res[0])], res[1])

    def grad_down(a, dh, name, exchange=None):
        res = _wgrad(a, dh, 1, FFS, name, exchange=exchange)
        halves = lambda g: [_halves(g.reshape(NSH, DFF // NSH, D))]
        return halves(res) if exchange is None else (halves(res[0]), res[1])

    k_gu2, k_d2, k_gu1, k_d1, k_in = (("ffn2_w_gate_up",), ("ffn2_w_down",), ("ffn1_w_gate_up",),
                                      ("ffn1_w_down",), ("w_in",))
    dh2, dgu3, d_g2 = _ffn_bwd(dh3, h2, g2, gu3, wgu2, wd2, "ffn2_bwd")
    pa = grad_gate_up(n3, dgu3, "wgrad_gu2") + grad_down(a3, dh3, "wgrad_d2")
    (dlg, dyp, dys, do_sb, dyg, dxp, d_scale), got_a = _mix_bwd_out(
        dh2, gp, gs, yp, ys, pm, wgrp, pool_scale, wbp, wba, wout, exchange=_ex_pair_swap(pa))
    chip_a = pair_sums(k_gu2 + k_d2, pa, got_a)
    kb = ("w_out", "w_branch_pool", "w_branch_attn")
    g_bp, g_ba, d_group = _wgrad_branches(p, dyp, o_sb, dys, pm, dyg)
    pb = [_halves(_wgrad(mm, dh2, 1, D, "wgrad_out").reshape(NSH, D // NSH, D)), _halves(g_bp), _halves(g_ba)]
    k_a, k_in = k_gu2 + k_d2, k_in + kb
    sems_a, thru_a, token_a = _scatter_start(chip_a, "scatter_a_start")
    dq, dk, dv = _attn_bwd(q, k, v, do_sb, ctot, after=(token_a,))
    chip_a, owned_a = _scatter_wait(sems_a, thru_a, [dq], "scatter_a_wait")
    halves_a = chip_sums(k_a, chip_a, owned_a)
    dproj = (dxp, dq, dk, dv, dlg)
    (dh1, d_gm), both_a = _mix_bwd_in(dh2, h1, gm, dproj, win, exchange=_ex_share(halves_a))
    for i, k_ in enumerate(k_a):
        grad[k_] = both_a[i].reshape(wts[k_].shape)

    p_in = [_halves(_wgrad_in(u, dproj))] + pb
    p_d1, got_in = grad_down(a1, dh1, "wgrad_d1", exchange=_ex_pair_swap(p_in))
    sems_in, thru_in, token_in = _scatter_start(pair_sums(k_in, p_in, got_in), "scatter_in_start")
    dgu1, got_d1 = _ffn_bwd_act(dh1, gu1, wd1, "ffn1_bwd_act", exchange=_ex_pair_swap(p_d1), after=(token_in,))
    sems_d1, thru_d1, token_d1 = _scatter_start(pair_sums(k_d1, p_d1, got_d1), "scatter_d1_start")
    p_gu1 = [_halves(_wgrad(n1, dgu1, NSH, D, "wgrad_gu1", after=(token_in, token_d1)))]
    sems_w, thru_w, token_w = _swap_start(p_gu1, "swap_gu1_start")
    chip_in, owned_in = _scatter_wait(sems_in, thru_in, [token_w], "scatter_in_wait")
    chip_d1, owned_d1 = _scatter_wait(sems_d1, thru_d1, [token_w], "scatter_d1_wait")
    halves_in = chip_sums(k_in, chip_in, owned_in)
    p_gu1, got_gu1 = _swap_wait(sems_w, thru_w, halves_in, "swap_gu1_wait")
    sems, thru, token = _scatter_start(pair_sums(k_gu1, p_gu1, got_gu1), "scatter_gu1_start")
    sems_h, thru_h, token_h = _share_start(halves_in, [token], "share_in_start")
    adamw(k_a, after=(token_h,))
    landed = _share_wait(sems_h, thru_h, [delta[k_a[0]]], "share_in_wait")
    for i, k_ in enumerate(k_in):
        grad[k_] = landed[i].reshape(wts[k_].shape)
    adamw(k_in)
    dx, d_g1 = _ffn_bwd_in(dh1, x0, g1, dgu1, wgu1, "ffn1_bwd_in", after=(token,))
    small_g = dict(ffn1_norm=d_g1, mix_norm=d_gm, ffn2_norm=d_g2, final_norm=d_gf, pool_scale=d_scale,
                   pool_w_group=d_group, loss=loss_row)
    dev = 4 * lax.axis_index("x") + 2 * lax.axis_index("y") + lax.axis_index("c")
    slots = lax.dynamic_update_slice(jnp.zeros((8, SMALL_ROWS, 128), F32), _pack_small(small_g)[None], (dev, 0, 0))
    sems_s, slots, token_s = _small_gather_start(slots, "small_gather_start")

    chip_gu1, owned_gu1 = _scatter_wait(sems, thru, [dx] + [delta[k_] for k_ in k_a + k_in], "scatter_gu1_wait")
    halves_last = chip_sums(k_d1 + k_gu1, chip_d1 + chip_gu1, owned_d1 + owned_gu1)
    both = _exchange_alone(_ex_share(halves_last), "share_last",
                           after=(token_s,))
    grad["ffn1_w_down"] = both[0].reshape(ffn1_w_down.shape)
    grad["ffn1_w_gate_up"] = both[1].reshape(ffn1_w_gate_up.shape)
    adamw(k_d1 + k_gu1, after=(token_s,))
    gathered = _small_gather_wait(sems_s, slots, [delta[k_] for k_ in k_d1 + k_gu1], "small_gather_wait")
    gathered = gathered.reshape(8 * SMALL_ROWS, 128)
    results = _small_update(gathered, _pack_small(wts), _pack_small(mom), _pack_small(var), _small_entries(wts))
    for dst, entries in zip((grad, delta, new_m, new_v), results):
        for k_, rows in zip(SMALL, entries):
            if k_ in wts:
                dst[k_] = rows.reshape(wts[k_].shape)
            elif dst is grad:
                loss = rows[0, 0]
    return (loss, dx[None], *[grad[k_] for k_ in ORDER], *[delta[k_] for k_ in ORDER],
            *[new_m[k_] for k_ in ORDER], *[new_v[k_] for k_ in ORDER])
```

```python
import dataclasses
import functools

import jax
import jax.numpy as jnp
from jax import lax
from jax.experimental import pallas as pl
from jax.experimental.pallas import tpu as pltpu

F32 = jnp.float32
BF16 = jnp.bfloat16

S = 2048
D = 1024
DFF = 2816
FFS = 2 * DFF // 4
NSH = 4
PW = 512
PG = 128
POOL_WINDOWS = (2, 4, 8, 16)
HALO = 16
SBW = 512
DH = 64
EPS = 1e-6
SCALE = 0.125
LOG2E = 1.4426950408889634
TA = 256
QB = 2
MIB = 1024 * 1024

LR, B1, B2, AEPS, WD, STEP = 0.001, 0.9, 0.999, 1e-08, 0.01, 10

_VM = pl.BlockSpec(memory_space=pltpu.VMEM)
_ANY = pl.BlockSpec(memory_space=pl.ANY)
MESH = pl.DeviceIdType.MESH
SIBLING_PAIR_ID = 1


def _nn(a, b):
    return jnp.dot(a, b, preferred_element_type=F32)


def _nt(a, b):
    return lax.dot_general(a, b, (((1,), (1,)), ((), ())), preferred_element_type=F32)


def _tn(a, b):
    return lax.dot_general(a, b, (((0,), (0,)), ((), ())), preferred_element_type=F32)


def _params(sem, vmem_mib):
    return pltpu.CompilerParams(dimension_semantics=sem, vmem_limit_bytes=vmem_mib * MIB)


def _rows(tm, width):
    return pl.BlockSpec((tm, width), lambda i: (i, 0))


def _fixed(shape):
    return pl.BlockSpec(shape, lambda *_: (0,) * len(shape))


def _sds(shape, dtype):
    return pltpu.HBM(shape, dtype)


def _in_hbm(args):
    return [pltpu.with_memory_space_constraint(a, pltpu.HBM) for a in args]


def _stage(pairs):
    pieces = 4

    def copy_all(sems):
        copies = []
        for src, dst in pairs:
            step = src.shape[0] // pieces
            for p in range(pieces):
                part = pl.ds(p * step, step)
                copies.append(pltpu.make_async_copy(src.at[part], dst.at[part], sems.at[len(copies)]))
        for c in copies:
            c.start()
        for c in copies:
            c.wait()

    @pl.when(pl.program_id(0) == 0)
    def _():
        pl.run_scoped(copy_all, pltpu.SemaphoreType.DMA((pieces * len(pairs),)))


def _vmem_like(*arrays):
    return [pltpu.VMEM(a.shape, a.dtype) for a in arrays]


class Exchange:
    def __init__(self, arrays, landing, aliases, n_sems, start, finish, sibling_only=False):
        self.arrays, self.landing, self.aliases, self.n_sems = list(arrays), list(landing), dict(aliases), n_sems
        self.start, self.finish = start, finish
        self.sibling_only = sibling_only

    def enter(self):
        if self.sibling_only:
            barrier = pltpu.get_barrier_semaphore()
            sibling = (lax.axis_index("x"), lax.axis_index("y"), 1 - lax.axis_index("c"))
            pl.semaphore_signal(barrier, inc=1, device_id=sibling, device_id_type=MESH)
            pl.semaphore_wait(barrier, 1)

    def params(self, compiler_params=None):
        kw = dict(collective_id=SIBLING_PAIR_ID) if self.sibling_only else {}
        if compiler_params is None:
            return pltpu.CompilerParams(**kw)
        return dataclasses.replace(compiler_params, **kw)


def _call(body, args, *, name, grid, in_specs, out_specs, out_shape, scratch_shapes=(), compiler_params=None,
          exchange=None, free=(), after=()):
    args = [a if i in free else pltpu.with_memory_space_constraint(a, pltpu.HBM) for i, a in enumerate(args)]
    if exchange is None:
        n_in = len(in_specs)

        def plain(*refs):
            body(*refs[:n_in], *refs[n_in + len(after):])

        return pl.pallas_call(plain, name=name, grid=grid, in_specs=list(in_specs) + [_ANY] * len(after),
                              out_specs=out_specs, out_shape=out_shape, scratch_shapes=list(scratch_shapes),
                              compiler_params=compiler_params)(*args, *after)
    ex = exchange
    n_in, n_out, n_scr = len(in_specs), len(out_specs), len(scratch_shapes)
    na, nl = len(ex.arrays), len(ex.landing)

    def hosted(*refs):
        at = [0]

        def take(n):
            at[0] += n
            return refs[at[0] - n:at[0]]

        k_in, _, e_in, k_out, e_out, k_scr = take(n_in), take(len(after)), take(na), take(n_out), take(nl), take(n_scr)
        ssem, rsem = take(2)
        ids = [pl.program_id(a) for a in range(len(grid))]
        first = functools.reduce(jnp.logical_and, [i == 0 for i in ids])
        last = functools.reduce(jnp.logical_and, [i == g - 1 for i, g in zip(ids, grid)])

        @pl.when(first)
        def _():
            ex.enter()
            ex.start(e_in, e_out, ssem, rsem)

        body(*k_in, *k_out, *k_scr)

        @pl.when(last)
        def _():
            ex.finish(e_in, e_out, ssem, rsem)

    outs = pl.pallas_call(
        hosted, name=name, grid=grid,
        in_specs=list(in_specs) + [_ANY] * (len(after) + na), out_specs=list(out_specs) + [_ANY] * nl,
        out_shape=list(out_shape) + ex.landing,
        scratch_shapes=list(scratch_shapes) + [pltpu.SemaphoreType.DMA((ex.n_sems,))] * 2,
        input_output_aliases={n_in + len(after) + i: n_out + j for i, j in ex.aliases.items()},
        compiler_params=ex.params(compiler_params),
    )(*args, *after, *_in_hbm(ex.arrays))
    return outs[:n_out], outs[n_out:]


def _exchange_alone(ex, name, after=()):
    na, nl = len(ex.arrays), len(ex.landing)

    def body(*refs):
        outs = refs[na + len(after):na + len(after) + nl]
        ex.enter()
        ex.start(refs[:na], outs, refs[-2], refs[-1])
        ex.finish(refs[:na], outs, refs[-2], refs[-1])

    return pl.pallas_call(
        body, name=name, in_specs=[_ANY] * (na + len(after)), out_specs=[_ANY] * nl,
        out_shape=ex.landing, scratch_shapes=[pltpu.SemaphoreType.DMA((ex.n_sems,))] * 2,
        input_output_aliases=ex.aliases, compiler_params=ex.params(),
    )(*_in_hbm(ex.arrays), *after)


_HBM = pl.BlockSpec(memory_space=pltpu.HBM)
_SEM = pl.BlockSpec(memory_space=pltpu.SEMAPHORE)
_EFFECT = pltpu.SideEffectType.DATAFLOW_SIDE_EFFECTING


def _scatter_copies(srcs, lands, ssems, rsems):
    x, y, c, chips = _place()
    return [_remote(srcs[w].at[2 * px + py], lands[w].at[k], ssems[3 * w + k], rsems[3 * w + k], (px, py, c))
            for w in range(len(srcs)) for k, (px, py) in enumerate(chips)]


def _scatter_start(parts, name):
    parts = list(parts)
    n, ncp = len(parts), 3 * len(parts)
    lands = [lax.empty((3,) + p.shape[1:], p.dtype) for p in parts]

    def body(*refs):
        srcs, land_refs = refs[:n], refs[n:2 * n]
        ssems, rsems = refs[2 * n:2 * n + ncp], refs[2 * n + ncp:2 * n + 2 * ncp]
        for cp in _scatter_copies(srcs, land_refs, ssems, rsems):
            cp.start()
        token = refs[-1]
        token[...] = jnp.zeros_like(token)

    outs = pl.pallas_call(
        body, name=name,
        out_shape=([pltpu.SemaphoreType.DMA(())] * (2 * ncp) + [pltpu.HBM(a.shape, a.dtype) for a in parts + lands]
                   + [jax.ShapeDtypeStruct((8, 128), F32)]),
        in_specs=[_HBM] * (2 * n), out_specs=[_SEM] * (2 * ncp) + [_HBM] * (2 * n) + [_VM],
        input_output_aliases={i: 2 * ncp + i for i in range(2 * n)},
        compiler_params=pltpu.CompilerParams(has_side_effects=_EFFECT),
    )(*_in_hbm(parts), *_in_hbm(lands))
    sems, thru, token = outs[:2 * ncp], outs[2 * ncp:2 * ncp + 2 * n], outs[-1]
    return sems, thru, token


def _scatter_wait(sems, thru, after, name):
    n = len(thru) // 2
    ncp = 3 * n

    def body(*refs):
        srcs, land_refs = refs[:n], refs[n:2 * n]
        ssems, rsems = refs[2 * n:2 * n + ncp], refs[2 * n + ncp:2 * n + 2 * ncp]
        for cp in _scatter_copies(srcs, land_refs, ssems, rsems):
            cp.wait_send()
            cp.wait_recv()

    outs = pl.pallas_call(
        body, name=name, out_shape=[pltpu.HBM(a.shape, a.dtype) for a in thru],
        in_specs=[_HBM] * (2 * n) + [_SEM] * (2 * ncp) + [_ANY] * len(after), out_specs=[_HBM] * (2 * n),
        input_output_aliases={i: i for i in range(2 * n)},
        compiler_params=pltpu.CompilerParams(has_side_effects=_EFFECT),
    )(*thru, *sems, *after)
    return outs[:n], outs[n:]


def _swap_copies(srcs, lands, ssems, rsems):
    x, y, c, _ = _place()
    return [_remote(srcs[w].at[:, 1 - c], lands[w], ssems[w], rsems[w], (x, y, 1 - c)) for w in range(len(srcs))]


def _swap_start(grads, name):
    grads = list(grads)
    n = len(grads)
    lands = [lax.empty((NSH,) + g.shape[2:], g.dtype) for g in grads]

    def body(*refs):
        barrier = pltpu.get_barrier_semaphore()
        sibling = (lax.axis_index("x"), lax.axis_index("y"), 1 - lax.axis_index("c"))
        pl.semaphore_signal(barrier, inc=1, device_id=sibling, device_id_type=MESH)
        pl.semaphore_wait(barrier, 1)
        for cp in _swap_copies(refs[:n], refs[n:2 * n], refs[2 * n:3 * n], refs[3 * n:4 * n]):
            cp.start()
        refs[-1][...] = jnp.zeros_like(refs[-1])

    outs = pl.pallas_call(
        body, name=name,
        out_shape=([pltpu.SemaphoreType.DMA(())] * (2 * n) + [pltpu.HBM(a.shape, a.dtype) for a in grads + lands]
                   + [jax.ShapeDtypeStruct((8, 128), F32)]),
        in_specs=[_HBM] * (2 * n), out_specs=[_SEM] * (2 * n) + [_HBM] * (2 * n) + [_VM],
        input_output_aliases={i: 2 * n + i for i in range(2 * n)},
        compiler_params=pltpu.CompilerParams(has_side_effects=_EFFECT, collective_id=SIBLING_PAIR_ID),
    )(*_in_hbm(grads), *_in_hbm(lands))
    return outs[:2 * n], outs[2 * n:4 * n], outs[-1]


def _swap_wait(sems, thru, after, name):
    n = len(thru) // 2

    def body(*refs):
        for cp in _swap_copies(refs[:n], refs[n:2 * n], refs[2 * n:3 * n], refs[3 * n:4 * n]):
            cp.wait_send()
            cp.wait_recv()

    outs = pl.pallas_call(
        body, name=name, out_shape=[pltpu.HBM(a.shape, a.dtype) for a in thru],
        in_specs=[_HBM] * (2 * n) + [_SEM] * (2 * n) + [_ANY] * len(after), out_specs=[_HBM] * (2 * n),
        input_output_aliases={i: i for i in range(2 * n)},
        compiler_params=pltpu.CompilerParams(has_side_effects=_EFFECT),
    )(*thru, *sems, *after)
    return outs[:n], outs[n:]


def _share_copies(bufs, ssems, rsems, sending):
    x, y, c, _ = _place()
    out = []
    for w, ref in enumerate(bufs):
        slot = ref.at[c if sending else 1 - c]
        out.append(_remote(slot, slot, ssems[w], rsems[w], (x, y, 1 - c)))
    return out


def _share_start(bufs, after, name):
    bufs = list(bufs)
    n = len(bufs)

    def body(*refs):
        barrier = pltpu.get_barrier_semaphore()
        sibling = (lax.axis_index("x"), lax.axis_index("y"), 1 - lax.axis_index("c"))
        pl.semaphore_signal(barrier, inc=1, device_id=sibling, device_id_type=MESH)
        pl.semaphore_wait(barrier, 1)
        at = n + len(after)
        for cp in _share_copies(refs[:n], refs[at:at + n], refs[at + n:at + 2 * n], True):
            cp.start()
        refs[-1][...] = jnp.zeros_like(refs[-1])

    outs = pl.pallas_call(
        body, name=name,
        out_shape=([pltpu.SemaphoreType.DMA(())] * (2 * n) + [pltpu.HBM(a.shape, a.dtype) for a in bufs]
                   + [jax.ShapeDtypeStruct((8, 128), F32)]),
        in_specs=[_HBM] * n + [_ANY] * len(after), out_specs=[_SEM] * (2 * n) + [_HBM] * n + [_VM],
        input_output_aliases={i: 2 * n + i for i in range(n)},
        compiler_params=pltpu.CompilerParams(has_side_effects=_EFFECT, collective_id=SIBLING_PAIR_ID),
    )(*_in_hbm(bufs), *after)
    return outs[:2 * n], outs[2 * n:3 * n], outs[-1]


def _share_wait(sems, thru, after, name):
    n = len(thru)

    def body(*refs):
        for cp in _share_copies(refs[:n], refs[n:2 * n], refs[2 * n:3 * n], True):
            cp.wait_send()
        for cp in _share_copies(refs[:n], refs[n:2 * n], refs[2 * n:3 * n], False):
            cp.wait_recv()

    return pl.pallas_call(
        body, name=name, out_shape=[pltpu.HBM(a.shape, a.dtype) for a in thru],
        in_specs=[_HBM] * n + [_SEM] * (2 * n) + [_ANY] * len(after), out_specs=[_HBM] * n,
        input_output_aliases={i: i for i in range(n)},
        compiler_params=pltpu.CompilerParams(has_side_effects=_EFFECT),
    )(*thru, *sems, *after)


def _gather_copies(bufs, ssems, rsems, sending):
    x, y, c, chips = _place()
    out = []
    for w, ref in enumerate(bufs):
        half = ref.shape[1] // 2
        for k, (px, py) in enumerate(chips):
            rows = ref.at[2 * x + y if sending else 2 * px + py, pl.ds(c * half, half)]
            out.append(_remote(rows, rows, ssems[3 * w + k], rsems[3 * w + k], (px, py, c)))
    return out


def _gather_start(bufs, after, name):
    n, ncp = len(bufs), 3 * len(bufs)

    def body(*refs):
        ssems, rsems = refs[n + len(after):n + len(after) + ncp], refs[n + len(after) + ncp:n + len(after) + 2 * ncp]
        for cp in _gather_copies(refs[:n], ssems, rsems, True):
            cp.start()
        token = refs[-1]
        token[...] = jnp.zeros_like(token)

    outs = pl.pallas_call(
        body, name=name,
        out_shape=([pltpu.SemaphoreType.DMA(())] * (2 * ncp) + [pltpu.HBM(a.shape, a.dtype) for a in bufs]
                   + [jax.ShapeDtypeStruct((8, 128), F32)]),
        in_specs=[_HBM] * n + [_ANY] * len(after), out_specs=[_SEM] * (2 * ncp) + [_HBM] * n + [_VM],
        input_output_aliases={i: 2 * ncp + i for i in range(n)},
        compiler_params=pltpu.CompilerParams(has_side_effects=_EFFECT),
    )(*_in_hbm(bufs), *after)
    return outs[:2 * ncp], outs[2 * ncp:2 * ncp + n], outs[-1]


def _gather_wait(sems, thru, after, name):
    n = len(thru)
    ncp = 3 * n

    def body(*refs):
        ssems, rsems = refs[n:n + ncp], refs[n + ncp:n + 2 * ncp]
        for cp in _gather_copies(refs[:n], ssems, rsems, True):
            cp.wait_send()
        for cp in _gather_copies(refs[:n], ssems, rsems, False):
            cp.wait_recv()

    return pl.pallas_call(
        body, name=name, out_shape=[pltpu.HBM(a.shape, a.dtype) for a in thru],
        in_specs=[_HBM] * n + [_SEM] * (2 * ncp) + [_ANY] * len(after), out_specs=[_HBM] * n,
        input_output_aliases={i: i for i in range(n)},
        compiler_params=pltpu.CompilerParams(has_side_effects=_EFFECT),
    )(*thru, *sems, *after)


def _rms(x):
    r = lax.rsqrt(jnp.mean(x * x, axis=-1, keepdims=True) + EPS)
    return r, x * r


def _rms_bwd(dn, xr, r, gain):
    dng = dn * gain
    dx = r * (dng - xr * jnp.mean(dng * xr, axis=-1, keepdims=True))
    return dx, jnp.sum(dn * xr, axis=0, keepdims=True)


def _ffn_fwd(x, gain, wgu, wd, name, exchange=None, head=None):
    tm = 256

    def body(x_ref, g_ref, wgu_hbm, wd_hbm, *rest):
        if head is None:
            h_ref, n_ref, gu_ref, a_ref, wgu_ref, wd_ref = rest
        else:
            t_ref, gf_ref, h_ref, loss_ref, dgf_ref, n_ref, gu_ref, a_ref, wgu_ref, wd_ref = rest
        _stage([(wgu_hbm, wgu_ref), (wd_hbm, wd_ref)])
        x = x_ref[...]
        _, xr = _rms(x)
        n = (xr * g_ref[...]).astype(BF16)
        n_ref[...] = n
        acc = jnp.zeros((tm, D), F32)
        for j in range(2):
            g = _nn(n, wgu_ref[j])
            u = _nn(n, wgu_ref[2 + j])
            gu_ref[:, j * FFS:(j + 1) * FFS] = g.astype(BF16)
            gu_ref[:, (2 + j) * FFS:(3 + j) * FFS] = u.astype(BF16)
            half_act = (0.5 * (g * jax.nn.sigmoid(g) * u)).astype(BF16)
            a_ref[:, j * FFS:(j + 1) * FFS] = half_act
            acc = acc + _nn(half_act, wd_ref[j * FFS:(j + 1) * FFS, :])
        h = x + acc
        if head is None:
            h_ref[...] = h
            return
        gf = gf_ref[...]
        r, hr = _rms(h)
        err = hr * gf - t_ref[...]
        dh, dgain = _rms_bwd(err * (1.0 / D), hr, r, gf)
        h_ref[...] = dh

        @pl.when(pl.program_id(0) == 0)
        def _():
            dgf_ref[...] = jnp.zeros_like(dgf_ref)
            loss_ref[...] = jnp.zeros_like(loss_ref)

        dgf_ref[...] += dgain
        loss_ref[...] += jnp.full((1, 128), (0.5 / D) * jnp.sum(err * err), F32)

    saved_specs = [_rows(tm, D), _rows(tm, 4 * FFS), _rows(tm, DFF)]
    saved_shapes = [_sds((S, D), BF16), _sds((S, 4 * FFS), BF16), _sds((S, DFF), BF16)]
    if head is None:
        return _call(
            body, (x, gain, wgu, wd), name=name, grid=(S // tm,),
            in_specs=[_rows(tm, D), _fixed((1, D)), _ANY, _ANY],
            out_specs=[_rows(tm, D)] + saved_specs, out_shape=[_sds((S, D), F32)] + saved_shapes,
            scratch_shapes=_vmem_like(wgu, wd),
            compiler_params=_params(("arbitrary",), 56), exchange=exchange)
    return _call(
        body, (x, gain, wgu, wd, *head), name=name, grid=(S // tm,),
        in_specs=[_rows(tm, D), _fixed((1, D)), _ANY, _ANY, _rows(tm, D), _fixed((1, D))],
        out_specs=[_rows(tm, D), _fixed((1, 128)), _fixed((1, D))] + saved_specs,
        out_shape=[_sds((S, D), F32), _sds((1, 128), F32), _sds((1, D), F32)] + saved_shapes,
        scratch_shapes=_vmem_like(wgu, wd),
        compiler_params=_params(("arbitrary",), 56), exchange=exchange, free=(4, 5))


def _ffn_bwd(dh, x, gain, gu, wgu, wd, name):
    tm = 256

    def body(dh_ref, x_ref, g_ref, gu_ref, wgu_hbm, wd_hbm, dx_ref, dgu_ref, dg_ref, wgu_ref, wd_ref):
        _stage([(wgu_hbm, wgu_ref), (wd_hbm, wd_ref)])
        dh = dh_ref[...]
        dhb = dh.astype(BF16)
        dn = jnp.zeros((tm, D), F32)
        for j in range(2):
            g = gu_ref[:, j * FFS:(j + 1) * FFS].astype(F32)
            u = gu_ref[:, (2 + j) * FFS:(3 + j) * FFS].astype(F32)
            da = 0.5 * _nt(dhb, wd_ref[j * FFS:(j + 1) * FFS, :])
            sg = jax.nn.sigmoid(g)
            dgb = (da * u * (sg * (1.0 + g * (1.0 - sg)))).astype(BF16)
            dub = (da * (g * sg)).astype(BF16)
            dgu_ref[:, j * FFS:(j + 1) * FFS] = dgb
            dgu_ref[:, (2 + j) * FFS:(3 + j) * FFS] = dub
            dn = dn + _nt(dgb, wgu_ref[j]) + _nt(dub, wgu_ref[2 + j])
        r, xr = _rms(x_ref[...])
        dx, dgain = _rms_bwd(dn, xr, r, g_ref[...])
        dx_ref[...] = dh + dx

        @pl.when(pl.program_id(0) == 0)
        def _():
            dg_ref[...] = jnp.zeros_like(dg_ref)

        dg_ref[...] += dgain

    return _call(
        body, (dh, x, gain, gu, wgu, wd), name=name, grid=(S // tm,),
        in_specs=[_rows(tm, D), _rows(tm, D), _fixed((1, D)), _rows(tm, 4 * FFS), _ANY, _ANY],
        out_specs=[_rows(tm, D), _rows(tm, 4 * FFS), _fixed((1, D))],
        out_shape=[_sds((S, D), F32), _sds((S, 4 * FFS), BF16), _sds((1, D), F32)],
        scratch_shapes=_vmem_like(wgu, wd), compiler_params=_params(("arbitrary",), 56))


def _ffn_bwd_act(dh, gu, wd, name, exchange=None, after=()):
    tm = 512

    def body(dh_ref, gu_ref, wd_hbm, dgu_ref, wd_ref):
        _stage([(wd_hbm, wd_ref)])
        dhb = dh_ref[...].astype(BF16)
        for j in range(2):
            g = gu_ref[:, j * FFS:(j + 1) * FFS].astype(F32)
            u = gu_ref[:, (2 + j) * FFS:(3 + j) * FFS].astype(F32)
            da = 0.5 * _nt(dhb, wd_ref[j * FFS:(j + 1) * FFS, :])
            sg = jax.nn.sigmoid(g)
            dgu_ref[:, j * FFS:(j + 1) * FFS] = (da * u * (sg * (1.0 + g * (1.0 - sg)))).astype(BF16)
            dgu_ref[:, (2 + j) * FFS:(3 + j) * FFS] = (da * (g * sg)).astype(BF16)

    res = _call(
        body, (dh, gu, wd), name=name, grid=(S // tm,),
        in_specs=[_rows(tm, D), _rows(tm, 4 * FFS), _ANY], out_specs=[_rows(tm, 4 * FFS)],
        out_shape=[_sds((S, 4 * FFS), BF16)], scratch_shapes=_vmem_like(wd),
        compiler_params=_params(("arbitrary",), 56), exchange=exchange, after=after)
    return res[0] if exchange is None else (res[0][0], res[1])


def _ffn_bwd_in(dh, x, gain, dgu, wgu, name, exchange=None, after=()):
    tm = 512

    def body(dh_ref, x_ref, g_ref, dgu_ref, wgu_hbm, dx_ref, dg_ref, wgu_ref):
        _stage([(wgu_hbm, wgu_ref)])
        dn = jnp.zeros((tm, D), F32)
        for j in range(NSH):
            dn = dn + _nt(dgu_ref[:, j * FFS:(j + 1) * FFS], wgu_ref[j])
        r, xr = _rms(x_ref[...])
        dx, dgain = _rms_bwd(dn, xr, r, g_ref[...])
        dx_ref[...] = dh_ref[...] + dx

        @pl.when(pl.program_id(0) == 0)
        def _():
            dg_ref[...] = jnp.zeros_like(dg_ref)

        dg_ref[...] += dgain

    return _call(
        body, (dh, x, gain, dgu, wgu), name=name, grid=(S // tm,),
        in_specs=[_rows(tm, D), _rows(tm, D), _fixed((1, D)), _rows(tm, 4 * FFS), _ANY],
        out_specs=[_rows(tm, D), _fixed((1, D))],
        out_shape=[_sds((S, D), F32), _sds((1, D), F32)],
        scratch_shapes=_vmem_like(wgu),
        compiler_params=_params(("arbitrary",), 56), exchange=exchange, after=after)


def _mix_in(h, gain, w_in, after=()):
    tm = 512

    def body(h_ref, g_ref, w_hbm, u_ref, xp_ref, q_ref, k_ref, v_ref, gp_ref, gs_ref, w_ref):
        _stage([(w_hbm, w_ref)])
        _, hr = _rms(h_ref[...])
        u = (hr * g_ref[...]).astype(BF16)
        u_ref[...] = u
        p0 = _nn(u, w_ref[0])
        xp_ref[...] = p0[:, :PW]
        q_ref[...] = p0[:, PW:].astype(BF16)
        p1 = _nn(u, w_ref[1])
        k_ref[...] = p1[:, :SBW].astype(BF16)
        v_ref[...] = p1[:, SBW:].astype(BF16)
        gp_ref[...] = jax.nn.sigmoid(_nn(u, w_ref[2])).astype(BF16)
        gs_ref[...] = jax.nn.sigmoid(_nn(u, w_ref[3])).astype(BF16)

    return _call(
        body, (h, gain, w_in), name="mix_in", grid=(S // tm,),
        in_specs=[_rows(tm, D), _fixed((1, D)), _ANY],
        out_specs=[_rows(tm, D), _rows(tm, PW), _rows(tm, SBW), _rows(tm, SBW), _rows(tm, SBW),
                   _rows(tm, D), _rows(tm, D)],
        out_shape=[_sds((S, D), BF16), _sds((S, PW), F32), _sds((S, SBW), BF16), _sds((S, SBW), BF16),
                   _sds((S, SBW), BF16), _sds((S, D), BF16), _sds((S, D), BF16)],
        scratch_shapes=_vmem_like(w_in),
        compiler_params=_params(("arbitrary",), 48), free=(1,), after=after)


def _hilo_dot(x, tri):
    hi = x.astype(BF16)
    lo = (x - hi.astype(F32)).astype(BF16)
    return _nn(hi, tri) + _nn(lo, tri)


def _log_terms(qk):
    z2 = qk * (SCALE * LOG2E)
    lb = jnp.minimum(z2, 0.0) - jnp.log2(1.0 + jnp.exp2(-jnp.abs(z2)))
    return lb, lb - z2


def _head_masks():
    lane = lax.broadcasted_iota(jnp.int32, (1, 2 * DH), 1)
    return (lane < DH, lane >= DH)


def _attn_fwd(q, k, v, exchange=None):
    T = TA

    def body(q_ref, k_ref, v_ref, o_ref, c_ref):
        i2 = 2 * pl.program_id(1)
        row = lax.broadcasted_iota(jnp.int32, (T, T), 0)
        col = lax.broadcasted_iota(jnp.int32, (T, T), 1)
        after = (row > col).astype(BF16)
        causal = col < row
        masks = _head_masks()
        qms = {}
        for b in range(QB):
            q2 = q_ref[b * T:(b + 1) * T, :]
            for h, hm in enumerate(masks):
                qms[b, h] = jnp.where(hm, q2, jnp.zeros_like(q2))

        def blocks(keys, pairs, carries, os):
            ks, vms = [], []
            for j in keys:
                rows = pl.ds(pl.multiple_of(j * T, T), T)
                vj = v_ref[rows, :]
                ks.append(k_ref[rows, :])
                vms.append([jnp.where(hm, vj, jnp.zeros_like(vj)) for hm in masks])
            units = [(n, h) for n in range(len(pairs)) for h in range(2)]
            qks = {(n, h): _nt(qms[pairs[n][0], h], ks[pairs[n][1]]) for n, h in units}
            lbs, l1ms = {}, {}
            for u in units:
                lbs[u], l1m = _log_terms(qks[u])
                l1ms[u] = jnp.where(causal, l1m, 0.0) if pairs[u[0]][2] else l1m
            cins = {u: _hilo_dot(l1ms[u], after) for u in units}
            carries, os = dict(carries), list(os)
            for n, h in units:
                b, key, diag = pairs[n]
                a = jnp.exp2(lbs[n, h] + cins[n, h] + carries[b, h])
                if diag:
                    a = jnp.where(causal, a, 0.0)
                os[b] = os[b] + _nn(a.astype(BF16), vms[key][h])
                carries[b, h] = carries[b, h] + jnp.sum(l1ms[n, h], axis=1, keepdims=True)
            return carries, tuple(os)

        carries = {(b, h): jnp.zeros((T, 1), F32) for b in range(QB) for h in range(2)}
        os = tuple(jnp.zeros((T, 2 * DH), F32) for _ in range(QB))
        carries, os = blocks([i2 + 1, i2], [(1, 0, True), (0, 1, True), (1, 1, False)], carries, os)
        carries, os = lax.fori_loop(
            0, i2 // 2,
            lambda t, c: blocks([i2 - 1 - 2 * t, i2 - 2 - 2 * t],
                                [(0, 0, False), (1, 0, False), (0, 1, False), (1, 1, False)], c[0], c[1]),
            (carries, os))
        for b in range(QB):
            o_ref[b * T:(b + 1) * T, :] = os[b].astype(BF16)
            c_ref[b * T:(b + 1) * T, :] = jnp.where(masks[0], carries[b, 0], carries[b, 1])

    blk = pl.BlockSpec((QB * T, 2 * DH), lambda p, i: (i, p))
    full = pl.BlockSpec((S, 2 * DH), lambda p, i: (0, p))
    return _call(
        body, (q, k, v), name="attn_fwd", grid=(SBW // (2 * DH), S // (QB * T)),
        in_specs=[blk, full, full], out_specs=[blk, blk],
        out_shape=[_sds((S, SBW), BF16), _sds((S, SBW), F32)],
        compiler_params=_params(("arbitrary", "arbitrary"), 40), exchange=exchange)


def _attn_bwd(q, k, v, do, ctot, after=()):
    T = TA
    nq = S // (QB * T)

    def body(q_ref, k_ref, v_ref, do_ref, c_ref, dq_ref, dk_ref, dv_ref, dk_acc, dv_acc):
        step = pl.program_id(1)
        i2 = 2 * step

        @pl.when(step == 0)
        def _():
            dk_acc[...] = jnp.zeros_like(dk_acc)
            dv_acc[...] = jnp.zeros_like(dv_acc)

        row = lax.broadcasted_iota(jnp.int32, (T, T), 0)
        col = lax.broadcasted_iota(jnp.int32, (T, T), 1)
        upto = (row <= col).astype(BF16)
        before = (row < col).astype(BF16)
        causal = col < row
        masks = _head_masks()
        qms, doms, ctots = {}, {}, {}
        for b in range(QB):
            q2, do2 = q_ref[b * T:(b + 1) * T, :], do_ref[b * T:(b + 1) * T, :]
            for h, hm in enumerate(masks):
                qms[b, h] = jnp.where(hm, q2, jnp.zeros_like(q2))
                doms[b, h] = jnp.where(hm, do2, jnp.zeros_like(do2))
                ctots[b, h] = c_ref[b * T:(b + 1) * T, h * DH:h * DH + 1]

        def blocks(keys, pairs, sums, dqs):
            rows = [pl.ds(pl.multiple_of(j * T, T), T) for j in keys]
            ks, vs = [k_ref[r, :] for r in rows], [v_ref[r, :] for r in rows]
            kms = [[jnp.where(hm, kj, jnp.zeros_like(kj)) for hm in masks] for kj in ks]
            units = [(n, h) for n in range(len(pairs)) for h in range(2)]
            qks = {(n, h): _nt(qms[pairs[n][0], h], ks[pairs[n][1]]) for n, h in units}
            das = {(n, h): _nt(doms[pairs[n][0], h], vs[pairs[n][1]]) for n, h in units}
            lbs, l1ms = {}, {}
            for u in units:
                lbs[u], l1m = _log_terms(qks[u])
                l1ms[u] = jnp.where(causal, l1m, 0.0) if pairs[u[0]][2] else l1m
            pins = {u: _hilo_dot(l1ms[u], upto) for u in units}
            sums = dict(sums)
            a_s, dls, cps = {}, {}, {}
            for n, h in units:
                b, _, diag = pairs[n]
                cl, cp = sums[b, h]
                a = jnp.exp2(lbs[n, h] + (ctots[b, h] - cl) - pins[n, h])
                if diag:
                    a = jnp.where(causal, a, 0.0)
                a_s[n, h] = a.astype(BF16)
                dls[n, h] = das[n, h] * a
                cps[n, h] = cp
                sums[b, h] = (cl + jnp.sum(l1ms[n, h], axis=1, keepdims=True),
                              cp + jnp.sum(dls[n, h], axis=1, keepdims=True))
            pexs = {u: _hilo_dot(dls[u], before) for u in units}
            dzbs = {}
            for u in units:
                dz = dls[u] - jnp.exp2(lbs[u]) * (dls[u] + pexs[u] + cps[u])
                if pairs[u[0]][2]:
                    dz = jnp.where(causal, dz, 0.0)
                dzbs[u] = dz.astype(BF16)
            dqs = list(dqs)
            for n, h in units:
                dqs[pairs[n][0]] = dqs[pairs[n][0]] + _nn(dzbs[n, h], kms[pairs[n][1]][h])
            for key, r in enumerate(rows):
                mine = [(n, h) for n, h in units if pairs[n][1] == key]
                dk_acc[r, :] += functools.reduce(jnp.add, [_tn(dzbs[u], qms[pairs[u[0]][0], u[1]]) for u in mine])
                dv_acc[r, :] += functools.reduce(jnp.add, [_tn(a_s[u], doms[pairs[u[0]][0], u[1]]) for u in mine])
            return sums, tuple(dqs)

        zero = jnp.zeros((T, 1), F32)
        sums = {(b, h): (zero, zero) for b in range(QB) for h in range(2)}
        dqs = tuple(jnp.zeros((T, 2 * DH), F32) for _ in range(QB))
        sums, dqs = lax.fori_loop(
            0, i2 // 2,
            lambda t, c: blocks([2 * t, 2 * t + 1],
                                [(0, 0, False), (1, 0, False), (0, 1, False), (1, 1, False)], c[0], c[1]),
            (sums, dqs))
        _, dqs = blocks([i2, i2 + 1], [(0, 0, True), (1, 0, False), (1, 1, True)], sums, dqs)
        for b in range(QB):
            dq_ref[b * T:(b + 1) * T, :] = (dqs[b] * SCALE).astype(BF16)

        @pl.when(step == nq - 1)
        def _():
            dk_ref[...] = (dk_acc[...] * SCALE).astype(BF16)
            dv_ref[...] = dv_acc[...].astype(BF16)

    blk = pl.BlockSpec((QB * T, 2 * DH), lambda p, i: (i, p))
    full = pl.BlockSpec((S, 2 * DH), lambda p, i: (0, p))
    return _call(
        body, (q, k, v, do, ctot), name="attn_bwd", grid=(SBW // (2 * DH), nq),
        in_specs=[blk, full, full, blk, blk], out_specs=[blk, full, full],
        out_shape=[_sds((S, SBW), BF16), _sds((S, SBW), BF16), _sds((S, SBW), BF16)],
        scratch_shapes=[pltpu.VMEM((S, 2 * DH), F32), pltpu.VMEM((S, 2 * DH), F32)],
        compiler_params=_params(("arbitrary", "arbitrary"), 40), after=after)


def _pool_counts(first_row, tm):
    pos = first_row + lax.broadcasted_iota(jnp.int32, (tm, 1), 0)
    return [jnp.minimum(pos + 1, w).astype(F32) for w in POOL_WINDOWS]


def _mix_out(h, xp, o_sb, gp, gs, w_group, scale, w_bp, w_ba, w_out, exchange=None):
    tm = 512

    def body(h_ref, xp_ref, o_ref, gp_ref, gs_ref, wg_hbm, sc_ref, wbp_hbm, wba_hbm, wo_hbm,
             h2_ref, pm_ref, p_ref, yp_ref, ys_ref, m_ref, halo, wg_ref, wbp_ref, wba_ref, wo_ref):
        _stage([(wg_hbm, wg_ref), (wbp_hbm, wbp_ref), (wba_hbm, wba_ref), (wo_hbm, wo_ref)])
        i = pl.program_id(0)

        @pl.when(i == 0)
        def _():
            halo[...] = jnp.zeros_like(halo)

        xp = xp_ref[...]
        ext = jnp.concatenate([halo[...], xp], axis=0)
        halo[...] = xp[tm - HALO:, :]
        counts = _pool_counts(i * tm, tm)
        for gi in range(len(POOL_WINDOWS)):
            lanes = slice(gi * PG, (gi + 1) * PG)
            win = ext[:, lanes]
            for step in range(gi + 1):
                win = win + pltpu.roll(win, 1 << step, 0)
            pm = (win[HALO:, :] / counts[gi] - xp[:, lanes]).astype(BF16)
            pm_ref[:, lanes] = pm
            p_ref[:, lanes] = (_nn(pm, wg_ref[gi]) * sc_ref[:, lanes]).astype(BF16)
        pb = p_ref[...]
        ob = o_ref[...]
        for j in range(NSH):
            cols = slice(j * (D // NSH), (j + 1) * (D // NSH))
            yp = _nn(pb, wbp_ref[j])
            ys = _nn(ob, wba_ref[j])
            yp_ref[:, cols] = yp.astype(BF16)
            ys_ref[:, cols] = ys.astype(BF16)
            m_ref[:, cols] = (gp_ref[:, cols].astype(F32) * yp + gs_ref[:, cols].astype(F32) * ys).astype(BF16)
        h2_ref[...] = h_ref[...] + _nn(m_ref[...], wo_ref[...])

    return _call(
        body, (h, xp, o_sb, gp, gs, w_group, scale, w_bp, w_ba, w_out), name="mix_out", grid=(S // tm,),
        in_specs=[_rows(tm, D), _rows(tm, PW), _rows(tm, SBW), _rows(tm, D), _rows(tm, D),
                  _ANY, _fixed((1, PW)), _ANY, _ANY, _ANY],
        out_specs=[_rows(tm, D), _rows(tm, PW), _rows(tm, PW), _rows(tm, D), _rows(tm, D), _rows(tm, D)],
        out_shape=[_sds((S, D), F32), _sds((S, PW), BF16), _sds((S, PW), BF16), _sds((S, D), BF16),
                   _sds((S, D), BF16), _sds((S, D), BF16)],
        scratch_shapes=[pltpu.VMEM((HALO, PW), F32)] + _vmem_like(w_group, w_bp, w_ba, w_out),
        compiler_params=_params(("arbitrary",), 48), free=(5, 6), exchange=exchange)


def _mix_bwd_out(dh, gp, gs, yp, ys, pm, w_group, scale, w_bp, w_ba, w_out, exchange=None):
    tm = 512
    nt = S // tm

    def body(dh_ref, gp_ref, gs_ref, yp_ref, ys_ref, pm_ref, wg_hbm, sc_ref, wbp_hbm, wba_hbm, wo_hbm,
             dlg_ref, dyp_ref, dys_ref, do_ref, dyg_ref, dxp_ref, dsc_ref, halo, wg_ref, wbp_ref, wba_ref, wo_ref):
        _stage([(wg_hbm, wg_ref), (wbp_hbm, wbp_ref), (wba_hbm, wba_ref), (wo_hbm, wo_ref)])
        step = pl.program_id(0)

        @pl.when(step == 0)
        def _():
            halo[...] = jnp.zeros_like(halo)
            dsc_ref[...] = jnp.zeros_like(dsc_ref)

        dm = _nt(dh_ref[...].astype(BF16), wo_ref[...])
        gp = gp_ref[...].astype(F32)
        gs = gs_ref[...].astype(F32)
        yp = yp_ref[...].astype(F32)
        ys = ys_ref[...].astype(F32)
        dlg_ref[:, :D] = (dm * yp * gp * (1.0 - gp)).astype(BF16)
        dlg_ref[:, D:] = (dm * ys * gs * (1.0 - gs)).astype(BF16)
        dyp_ref[...] = (dm * gp).astype(BF16)
        dys_ref[...] = (dm * gs).astype(BF16)
        dp = jnp.zeros((tm, PW), F32)
        do = jnp.zeros((tm, SBW), F32)
        for j in range(NSH):
            cols = slice(j * (D // NSH), (j + 1) * (D // NSH))
            dp = dp + _nt(dyp_ref[:, cols], wbp_ref[j])
            do = do + _nt(dys_ref[:, cols], wba_ref[j])
        do_ref[...] = do.astype(BF16)
        counts = _pool_counts((nt - 1 - step) * tm, tm)
        dscale = []
        for gi in range(len(POOL_WINDOWS)):
            lanes = slice(gi * PG, (gi + 1) * PG)
            dpg = dp[:, lanes]
            dscale.append(jnp.sum(dpg * _nn(pm_ref[:, lanes], wg_ref[gi]), axis=0, keepdims=True))
            dyg = (dpg * sc_ref[:, lanes]).astype(BF16)
            dyg_ref[:, lanes] = dyg
            dpm = _nt(dyg, wg_ref[gi])
            per = dpm / counts[gi]
            win = jnp.concatenate([per, halo[:, lanes]], axis=0)
            halo[:, lanes] = per[:HALO, :]
            for s in range(gi + 1):
                win = win + pltpu.roll(win, tm + HALO - (1 << s), 0)
            dxp_ref[:, lanes] = (win[:tm, :] - dpm).astype(BF16)
        dsc_ref[...] += jnp.concatenate(dscale, axis=1)

    rev = lambda width: pl.BlockSpec((tm, width), lambda i: (nt - 1 - i, 0))
    return _call(
        body, (dh, gp, gs, yp, ys, pm, w_group, scale, w_bp, w_ba, w_out), name="mix_bwd_out", grid=(nt,),
        in_specs=[rev(D), rev(D), rev(D), rev(D), rev(D), rev(PW), _ANY, _fixed((1, PW)), _ANY, _ANY, _ANY],
        out_specs=[rev(2 * D), rev(D), rev(D), rev(SBW), rev(PW), rev(PW), _fixed((1, PW))],
        out_shape=[_sds((S, 2 * D), BF16), _sds((S, D), BF16), _sds((S, D), BF16), _sds((S, SBW), BF16),
                   _sds((S, PW), BF16), _sds((S, PW), BF16), _sds((1, PW), F32)],
        scratch_shapes=[pltpu.VMEM((HALO, PW), F32)] + _vmem_like(w_group, w_bp, w_ba, w_out),
        compiler_params=_params(("arbitrary",), 48), exchange=exchange)


def _mix_bwd_in(dh, h, gain, pieces, w_in, exchange=None):
    tm = 512
    widths = [p.shape[1] for p in pieces]

    def body(dh_ref, h_ref, g_ref, *rest):
        piece_refs, (w_hbm, dx_ref, dg_ref, w_ref, dp_ref) = rest[:len(pieces)], rest[len(pieces):]
        _stage([(w_hbm, w_ref)])
        at = 0
        for ref, width in zip(piece_refs, widths):
            dp_ref[:, at:at + width] = ref[...]
            at += width
        du = jnp.zeros((tm, D), F32)
        for j in range(NSH):
            du = du + _nt(dp_ref[:, j * D:(j + 1) * D], w_ref[j])
        r, hr = _rms(h_ref[...])
        dx, dgain = _rms_bwd(du, hr, r, g_ref[...])
        dx_ref[...] = dh_ref[...] + dx

        @pl.when(pl.program_id(0) == 0)
        def _():
            dg_ref[...] = jnp.zeros_like(dg_ref)

        dg_ref[...] += dgain

    return _call(
        body, (dh, h, gain, *pieces, w_in), name="mix_bwd_in", grid=(S // tm,),
        in_specs=[_rows(tm, D), _rows(tm, D), _fixed((1, D))] + [_rows(tm, w) for w in widths] + [_ANY],
        out_specs=[_rows(tm, D), _fixed((1, D))],
        out_shape=[_sds((S, D), F32), _sds((1, D), F32)],
        scratch_shapes=_vmem_like(w_in) + [pltpu.VMEM((tm, 4 * D), BF16)],
        compiler_params=_params(("arbitrary",), 48), exchange=exchange)


def _wgrad_in(u, pieces):
    dxp, dq, dk, dv, dlg = pieces

    def body(u_ref, dxp_ref, dq_ref, dk_ref, dv_ref, dlg_ref, o_ref):
        j = pl.program_id(0)
        u = u_ref[...]

        def two(left_ref, right_ref):
            o_ref[:, :PW] = _tn(u, left_ref[...]).astype(BF16)
            o_ref[:, PW:] = _tn(u, right_ref[...]).astype(BF16)

        pl.when(j == 0)(lambda: two(dxp_ref, dq_ref))
        pl.when(j == 1)(lambda: two(dk_ref, dv_ref))

        @pl.when(j >= 2)
        def _():
            o_ref[...] = _tn(u, dlg_ref[...]).astype(BF16)

    whole = lambda width: pl.BlockSpec((S, width), lambda j: (0, 0))
    return _call(
        body, (u, dxp, dq, dk, dv, dlg), name="wgrad_in", grid=(NSH,),
        in_specs=[whole(D), whole(PW), whole(SBW), whole(SBW), whole(SBW),
                  pl.BlockSpec((S, D), lambda j: (0, jnp.maximum(j - 2, 0)))],
        out_specs=[pl.BlockSpec((None, D, D), lambda j: (j, 0, 0))], out_shape=[_sds((NSH, D, D), BF16)],
        compiler_params=_params(("arbitrary",), 56))[0]


def _wgrad(a, b, nblk, ti, name, out_dtype=BF16, exchange=None, after=()):
    ka, n = a.shape[1], b.shape[1]
    ns = n // nblk

    def body(a_ref, b_ref, o_ref):
        o_ref[...] = _tn(a_ref[...].astype(BF16), b_ref[...].astype(BF16)).astype(out_dtype)

    res = _call(
        body, (a, b), name=name, grid=(nblk, ka // ti),
        in_specs=[pl.BlockSpec((S, ti), lambda j, i: (0, i)), pl.BlockSpec((S, ns), lambda j, i: (0, j))],
        out_specs=[pl.BlockSpec((None, ti, ns), lambda j, i: (j, i, 0))],
        out_shape=[_sds((nblk, ka, ns), out_dtype)],
        compiler_params=_params(("arbitrary", "arbitrary"), 56), exchange=exchange, after=after)
    return res[0] if exchange is None else (res[0][0], res[1])


def _wgrad_branches(p, dyp, o_sb, dys, pm, dyg):
    cols = D // NSH

    def body(p_ref, dyp_ref, o_ref, dys_ref, pm_ref, dyg_ref, gbp_ref, gba_ref, gg_ref):
        gbp_ref[...] = _tn(p_ref[...], dyp_ref[...]).astype(BF16)
        gba_ref[...] = _tn(o_ref[...], dys_ref[...]).astype(BF16)
        gg_ref[...] = _tn(pm_ref[...], dyg_ref[...])

    whole = lambda width: pl.BlockSpec((S, width), lambda j: (0, 0))
    col = lambda width: pl.BlockSpec((S, width), lambda j: (0, j))
    return _call(
        body, (p, dyp, o_sb, dys, pm, dyg), name="wgrad_branches", grid=(NSH,),
        in_specs=[whole(PW), col(cols), whole(SBW), col(cols), col(PG), col(PG)],
        out_specs=[pl.BlockSpec((None, PW, cols), lambda j: (j, 0, 0)),
                   pl.BlockSpec((None, SBW, cols), lambda j: (j, 0, 0)),
                   pl.BlockSpec((None, PG, PG), lambda j: (j, 0, 0))],
        out_shape=[_sds((NSH, PW, cols), BF16), _sds((NSH, SBW, cols), BF16), _sds((NSH, PG, PG), F32)],
        compiler_params=_params(("arbitrary",), 40))


def _place():
    x, y, c = lax.axis_index("x"), lax.axis_index("y"), lax.axis_index("c")
    chips = [(1 - x, y), (x, 1 - y), (1 - x, 1 - y)]
    return x, y, c, chips


def _remote(src, dst, ssem, rsem, dev):
    return pltpu.make_async_remote_copy(src_ref=src, dst_ref=dst, send_sem=ssem, recv_sem=rsem,
                                        device_id=dev, device_id_type=MESH)


def _cast_into_block(ws, me_idx, name):
    steps = 4
    shapes = [(w.shape[0] // steps, w.shape[1]) for w in ws]

    def body(me_ref, *refs):
        for w_ref, o_ref in zip(refs[:len(ws)], refs[len(ws):]):
            o_ref[...] = w_ref[...].astype(BF16)

    return pl.pallas_call(
        body, name=name, out_shape=[_sds((NSH,) + w.shape, BF16) for w in ws],
        grid_spec=pltpu.PrefetchScalarGridSpec(
            num_scalar_prefetch=1, grid=(steps,),
            in_specs=[pl.BlockSpec((r, c), lambda s, me: (s, 0)) for r, c in shapes],
            out_specs=[pl.BlockSpec((None, r, c), lambda s, me: (me[0], s, 0)) for r, c in shapes]),
        compiler_params=_params(("arbitrary",), 32),
    )(me_idx, *ws)


def _ex_gather(bufs):
    n = len(bufs)
    per = 8

    def plan(outs, ssem, rsem, w):
        x, y, c, _ = _place()
        sib, nbr_x, nbr_y = (x, y, 1 - c), (1 - x, y, c), (x, 1 - y, c)
        half = outs[w].shape[1] // 2
        quarter = half // 2
        sem = lambda k: (ssem.at[per * w + k], rsem.at[per * w + k])
        rows = lambda blk, start, size: outs[w].at[blk, pl.ds(start, size)]
        mine = rows(2 * x + y, c * half, half)
        from_x = rows(2 * (1 - x) + y, c * half, half)
        from_y = rows(2 * x + (1 - y), c * half, half)
        diag = 2 * (1 - x) + (1 - y)
        pass_y = rows(2 * (1 - x) + y, c * half, quarter)
        pass_x = rows(2 * x + (1 - y), c * half + quarter, quarter)
        diag_0, diag_1 = rows(diag, c * half, quarter), rows(diag, c * half + quarter, quarter)
        first = [_remote(mine, mine, *sem(0), nbr_x), _remote(mine, mine, *sem(1), nbr_y)]
        arrivals = [
            (_remote(from_x, from_x, *sem(0), nbr_x),
             [_remote(pass_y, pass_y, *sem(2), nbr_y), _remote(from_x, from_x, *sem(4), sib)]),
            (_remote(from_y, from_y, *sem(1), nbr_y),
             [_remote(pass_x, pass_x, *sem(3), nbr_x), _remote(from_y, from_y, *sem(5), sib)]),
            (_remote(diag_0, diag_0, *sem(2), nbr_y), [_remote(diag_0, diag_0, *sem(6), sib)]),
            (_remote(diag_1, diag_1, *sem(3), nbr_x), [_remote(diag_1, diag_1, *sem(7), sib)]),
        ]
        other = (1 - c) * half
        from_sibling = [
            _remote(rows(2 * (1 - x) + y, other, half), rows(2 * (1 - x) + y, other, half), *sem(4), sib),
            _remote(rows(2 * x + (1 - y), other, half), rows(2 * x + (1 - y), other, half), *sem(5), sib),
            _remote(rows(diag, other, quarter), rows(diag, other, quarter), *sem(6), sib),
            _remote(rows(diag, other + quarter, quarter), rows(diag, other + quarter, quarter), *sem(7), sib),
        ]
        return first, arrivals, from_sibling

    def start(ins, outs, ssem, rsem):
        x, y, c, _ = _place()
        for w in range(n):
            half = outs[w].shape[1] // 2
            mine = outs[w].at[2 * x + y, pl.ds(c * half, half)]
            _remote(mine, mine, ssem.at[per * w], rsem.at[per * w], (1 - x, y, c)).start()
            _remote(mine, mine, ssem.at[per * w + 1], rsem.at[per * w + 1], (x, 1 - y, c)).start()

    def finish(ins, outs, ssem, rsem):
        plans = [plan(outs, ssem, rsem, w) for w in range(n)]
        started = []
        for direct in (True, False):
            for first, arrivals, _ in plans:
                for arrived, onward in (arrivals[:2] if direct else arrivals[2:]):
                    arrived.wait_recv()
                    for cp in onward:
                        cp.start()
                    started += onward
        for first, _, from_sibling in plans:
            for cp in from_sibling:
                cp.wait_recv()
            started += first
        for cp in started:
            cp.wait_send()

    return Exchange(bufs, [_sds(b.shape, b.dtype) for b in bufs], {w: w for w in range(n)}, per * n, start, finish)


def _ex_gather_direct(bufs):
    n = len(bufs)

    def copies(outs, ssem, rsem, only_first=False):
        x, y, c, chips = _place()
        me, sib = 2 * x + y, (x, y, 1 - c)
        first, relay, last = [], [], []
        for w in range(n):
            half = outs[w].shape[1] // 2
            mine = outs[w].at[me, pl.ds(c * half, half)]
            for k, (px, py) in enumerate(chips):
                sems = (ssem.at[6 * w + k], rsem.at[6 * w + k])
                sib_sems = (ssem.at[6 * w + 3 + k], rsem.at[6 * w + 3 + k])
                first.append(_remote(mine, mine, *sems, (px, py, c)))
                if only_first:
                    continue
                got = outs[w].at[2 * px + py, pl.ds(c * half, half)]
                relay.append((_remote(got, got, *sems, (px, py, c)), _remote(got, got, *sib_sems, sib)))
                theirs = outs[w].at[2 * px + py, pl.ds((1 - c) * half, half)]
                last.append(_remote(theirs, theirs, *sib_sems, sib))
        return first, relay, last

    def start(ins, outs, ssem, rsem):
        for cp in copies(outs, ssem, rsem, only_first=True)[0]:
            cp.start()

    def finish(ins, outs, ssem, rsem):
        first, relay, last = copies(outs, ssem, rsem)
        for arrived, onward in relay:
            arrived.wait_recv()
            onward.start()
        for cp in last:
            cp.wait_recv()
        for cp in first:
            cp.wait_send()
        for _, onward in relay:
            onward.wait_send()

    return Exchange(bufs, [_sds(b.shape, b.dtype) for b in bufs], {w: w for w in range(n)}, 6 * n, start, finish)


def _simple_exchange(arrays, landing, aliases, make_copies, sibling_only=False):
    def start(ins, outs, ssem, rsem):
        for cp, _ in make_copies(ins, outs, ssem, rsem, False):
            cp.start()

    def finish(ins, outs, ssem, rsem):
        cps = make_copies(ins, outs, ssem, rsem, True)
        for _, landed in cps:
            landed.wait_recv()
        for cp, _ in cps:
            cp.wait_send()

    return Exchange(arrays, landing, aliases, len(arrays) * 3, start, finish, sibling_only)


def _ex_pair_swap(grads):
    def make(ins, outs, ssem, rsem, landing):
        x, y, c, _ = _place()
        cps = [_remote(ins[w].at[:, 1 - c], outs[w], ssem.at[w], rsem.at[w], (x, y, 1 - c))
               for w in range(len(grads))]
        return [(cp, cp) for cp in cps]

    return _simple_exchange(grads, [_sds((NSH,) + g.shape[2:], g.dtype) for g in grads], {}, make, True)


def _ex_relay(bufs):
    def make(ins, outs, ssem, rsem, landing):
        x, y, c, chips = _place()
        sib = (x, y, 1 - c)
        out = []
        for w in range(len(bufs)):
            half = outs[w].shape[1] // 2
            for k, (px, py) in enumerate(chips):
                sems = (ssem.at[3 * w + k], rsem.at[3 * w + k])
                have = outs[w].at[2 * px + py, pl.ds(c * half, half)]
                miss = outs[w].at[2 * px + py, pl.ds((1 - c) * half, half)]
                out.append((_remote(have, have, *sems, sib), _remote(miss, miss, *sems, sib) if landing else None))
        return out

    return _simple_exchange(bufs, [_sds(b.shape, b.dtype) for b in bufs], {w: w for w in range(len(bufs))}, make, True)


def _ex_share(bufs):
    def make(ins, outs, ssem, rsem, landing):
        x, y, c, _ = _place()
        sib = (x, y, 1 - c)
        return [(_remote(outs[w].at[c], outs[w].at[c], ssem.at[w], rsem.at[w], sib),
                 _remote(outs[w].at[1 - c], outs[w].at[1 - c], ssem.at[w], rsem.at[w], sib) if landing else None)
                for w in range(len(bufs))]

    return _simple_exchange(bufs, [_sds(b.shape, b.dtype) for b in bufs], {w: w for w in range(len(bufs))}, make, True)


def _small_copies(slots, ssems, rsems, sending):
    x, y, c, _ = _place()
    out = []
    for m in range(1, 8):
        px, py, pc = x ^ (m >> 2), y ^ ((m >> 1) & 1), c ^ (m & 1)
        slot = slots.at[4 * x + 2 * y + c if sending else 4 * px + 2 * py + pc]
        out.append(_remote(slot, slot, ssems[m - 1], rsems[m - 1], (px, py, pc)))
    return out


def _small_gather_start(slots, name):
    def body(*refs):
        for cp in _small_copies(refs[0], refs[1:8], refs[8:15], True):
            cp.start()
        refs[-1][...] = jnp.zeros_like(refs[-1])

    outs = pl.pallas_call(
        body, name=name,
        out_shape=([pltpu.SemaphoreType.DMA(())] * 14 + [pltpu.HBM(slots.shape, slots.dtype)]
                   + [jax.ShapeDtypeStruct((8, 128), F32)]),
        in_specs=[_HBM], out_specs=[_SEM] * 14 + [_HBM, _VM], input_output_aliases={0: 14},
        compiler_params=pltpu.CompilerParams(has_side_effects=_EFFECT),
    )(*_in_hbm([slots]))
    return outs[:14], outs[14], outs[15]


def _small_gather_wait(sems, slots, after, name):
    def body(*refs):
        for cp in _small_copies(refs[0], refs[1:8], refs[8:15], True):
            cp.wait_send()
        for cp in _small_copies(refs[0], refs[1:8], refs[8:15], False):
            cp.wait_recv()

    return pl.pallas_call(
        body, name=name, out_shape=pltpu.HBM(slots.shape, slots.dtype),
        in_specs=[_HBM] + [_SEM] * 14 + [_ANY] * len(after), out_specs=_HBM, input_output_aliases={0: 0},
        compiler_params=pltpu.CompilerParams(has_side_effects=_EFFECT),
    )(slots, *sems, *after)


def _pair_sum(grads, gots, c_idx, name):
    n = len(grads)

    def body(c_ref, *refs):
        for a_ref, b_ref, o_ref in zip(refs[:n], refs[n:2 * n], refs[2 * n:]):
            o_ref[...] = (a_ref[...].astype(F32) + b_ref[...].astype(F32)).astype(BF16)

    halves = [g.shape[2:] for g in grads]
    return list(pl.pallas_call(
        body, name=name, out_shape=[_sds((NSH,) + h, BF16) for h in halves],
        grid_spec=pltpu.PrefetchScalarGridSpec(
            num_scalar_prefetch=1, grid=(NSH,),
            in_specs=[pl.BlockSpec((None, None) + h, lambda j, c: (j, c[0], 0, 0)) for h in halves]
            + [pl.BlockSpec((None,) + h, lambda j, c: (j, 0, 0)) for h in halves],
            out_specs=[pl.BlockSpec((None,) + h, lambda j, c: (j, 0, 0)) for h in halves]),
        compiler_params=_params(("arbitrary",), 40),
    )(c_idx, *_in_hbm(list(grads) + list(gots))))


def _chip_sum(owns, gots, place, name):
    n = len(owns)

    def body(place_ref, *refs):
        for own_ref, got_ref, o_ref in zip(refs[:n], refs[n:2 * n], refs[2 * n:]):
            acc = own_ref[...].astype(F32)
            for k in range(3):
                acc = acc + got_ref[k].astype(F32)
            o_ref[...] = acc

    shapes = [(o.shape[1] // 2, o.shape[2]) for o in owns]
    return list(pl.pallas_call(
        body, name=name, out_shape=[_sds((2, 2 * r, c), F32) for r, c in shapes],
        grid_spec=pltpu.PrefetchScalarGridSpec(
            num_scalar_prefetch=1, grid=(2,),
            in_specs=[pl.BlockSpec((None, r, c), lambda s, p: (p[0], s, 0)) for r, c in shapes]
            + [pl.BlockSpec((3, r, c), lambda s, p: (0, s, 0)) for r, c in shapes],
            out_specs=[pl.BlockSpec((None, r, c), lambda s, p: (p[1], s, 0)) for r, c in shapes]),
        compiler_params=_params(("arbitrary",), 40),
    )(place, *_in_hbm(list(owns) + list(gots))))


def _adamw_math(w, g, m, v):
    m = B1 * m + (1.0 - B1) * g
    v = B2 * v + (1.0 - B2) * (g * g)
    m_hat = m / (1.0 - B1 ** STEP)
    v_hat = v / (1.0 - B2 ** STEP)
    return -LR * (m_hat / (jnp.sqrt(v_hat) + AEPS) + WD * w), m, v


def _adamw(ws, gs, ms, vs, name, after=()):
    n, steps = len(ws), 4

    def body(*refs):
        ins, outs = refs[:4 * n], refs[4 * n:]
        for i in range(n):
            w_ref, g_ref, m_ref, v_ref = ins[4 * i:4 * i + 4]
            go_ref, d_ref, nm_ref, nv_ref = outs[4 * i:4 * i + 4]
            g = g_ref[...]
            go_ref[...] = g
            d_ref[...], nm_ref[...], nv_ref[...] = _adamw_math(w_ref[...], g, m_ref[...], v_ref[...])

    args, specs, shapes, free = [], [], [], []
    for i, (w, g, m, v) in enumerate(zip(ws, gs, ms, vs)):
        args += [w, g, m, v]
        specs += [pl.BlockSpec((w.shape[0] // steps, w.shape[1]), lambda r: (r, 0))] * 4
        shapes += [_sds(w.shape, F32)] * 4
        free += [4 * i, 4 * i + 2, 4 * i + 3]
    outs = _call(body, args, name=name, grid=(steps,), out_shape=shapes, in_specs=specs, out_specs=specs,
                 compiler_params=_params(("arbitrary",), 48), free=tuple(free), after=after)
    return [outs[4 * i:4 * i + 4] for i in range(n)]


def _small_update(gathered, w, m, v, entries):
    rows = w.shape[0]

    def body(ga_ref, w_ref, m_ref, v_ref, *out_refs):
        for j, (first, n) in enumerate(entries):
            mine = slice(first, first + n)
            g = ga_ref[mine, :]
            for dev in range(1, 8):
                g = g + ga_ref[dev * rows + first:dev * rows + first + n, :]
            results = (g,) + _adamw_math(w_ref[mine, :], g, m_ref[mine, :], v_ref[mine, :])
            for i, res in enumerate(results):
                out_refs[i * len(entries) + j][...] = res

    outs = pl.pallas_call(
        body, name="small_update",
        out_shape=[jax.ShapeDtypeStruct((n, 128), F32) for _ in range(4) for _, n in entries],
        in_specs=[_VM] * 4, out_specs=[_VM] * (4 * len(entries)),
    )(gathered, w, m, v)
    return [outs[i * len(entries):(i + 1) * len(entries)] for i in range(4)]


SMALL = ("ffn1_norm", "mix_norm", "ffn2_norm", "final_norm", "pool_scale", "loss", "pool_w_group")
BIG = ("ffn1_w_gate_up", "ffn1_w_down", "w_in", "w_branch_pool", "w_branch_attn", "w_out",
       "ffn2_w_gate_up", "ffn2_w_down")
ORDER = ("ffn1_norm", "ffn1_w_gate_up", "ffn1_w_down", "mix_norm", "w_in", "pool_w_group", "pool_scale",
         "w_branch_pool", "w_branch_attn", "w_out", "ffn2_norm", "ffn2_w_gate_up", "ffn2_w_down", "final_norm")
SMALL_ROWS = 560


def _pack_small(t):
    parts = []
    for k in SMALL:
        rows = t[k].reshape(-1, 128) if k in t else jnp.zeros((1, 128), F32)
        parts.append(jnp.pad(rows, ((0, -rows.shape[0] % 8), (0, 0))))
    packed = jnp.concatenate(parts, axis=0)
    assert packed.shape == (SMALL_ROWS, 128), packed.shape
    return packed


def _small_entries(like):
    out, at = [], 0
    for k in SMALL:
        n = like[k].size // 128 if k in like else 1
        out.append((at, n))
        at += n + (-n % 8)
    return out


def _halves(g):
    return g.reshape(NSH, 2, g.shape[1] // 2, g.shape[2])


def kernel(x, ffn1_norm, ffn1_w_gate_up, ffn1_w_down, mix_norm, w_in, pool_w_group, pool_scale, w_branch_pool, w_branch_attn, w_out, ffn2_norm, ffn2_w_gate_up, ffn2_w_down, final_norm, loss_target, m_ffn1_norm, m_ffn1_w_gate_up, m_ffn1_w_down, m_mix_norm, m_w_in, m_pool_w_group, m_pool_scale, m_w_branch_pool, m_w_branch_attn, m_w_out, m_ffn2_norm, m_ffn2_w_gate_up, m_ffn2_w_down, m_final_norm, v_ffn1_norm, v_ffn1_w_gate_up, v_ffn1_w_down, v_mix_norm, v_w_in, v_pool_w_group, v_pool_scale, v_w_branch_pool, v_w_branch_attn, v_w_out, v_ffn2_norm, v_ffn2_w_gate_up, v_ffn2_w_down, v_final_norm):
    wts = dict(ffn1_norm=ffn1_norm, ffn1_w_gate_up=ffn1_w_gate_up, ffn1_w_down=ffn1_w_down, mix_norm=mix_norm,
               w_in=w_in, pool_w_group=pool_w_group, pool_scale=pool_scale, w_branch_pool=w_branch_pool,
               w_branch_attn=w_branch_attn, w_out=w_out, ffn2_norm=ffn2_norm, ffn2_w_gate_up=ffn2_w_gate_up,
               ffn2_w_down=ffn2_w_down, final_norm=final_norm)
    mom = dict(ffn1_norm=m_ffn1_norm, ffn1_w_gate_up=m_ffn1_w_gate_up, ffn1_w_down=m_ffn1_w_down,
               mix_norm=m_mix_norm, w_in=m_w_in, pool_w_group=m_pool_w_group, pool_scale=m_pool_scale,
               w_branch_pool=m_w_branch_pool, w_branch_attn=m_w_branch_attn, w_out=m_w_out,
               ffn2_norm=m_ffn2_norm, ffn2_w_gate_up=m_ffn2_w_gate_up, ffn2_w_down=m_ffn2_w_down,
               final_norm=m_final_norm)
    var = dict(ffn1_norm=v_ffn1_norm, ffn1_w_gate_up=v_ffn1_w_gate_up, ffn1_w_down=v_ffn1_w_down,
               mix_norm=v_mix_norm, w_in=v_w_in, pool_w_group=v_pool_w_group, pool_scale=v_pool_scale,
               w_branch_pool=v_w_branch_pool, w_branch_attn=v_w_branch_attn, w_out=v_w_out,
               ffn2_norm=v_ffn2_norm, ffn2_w_gate_up=v_ffn2_w_gate_up, ffn2_w_down=v_ffn2_w_down,
               final_norm=v_final_norm)

    c_idx = lax.axis_index("c").astype(jnp.int32).reshape(1)
    me_idx = (2 * lax.axis_index("x") + lax.axis_index("y")).astype(jnp.int32).reshape(1)
    place = jnp.concatenate([me_idx, c_idx])
    x0, tgt = x[0], loss_target[0]
    wgrp = pool_w_group[0].astype(BF16)
    g1, gm, g2, gf = ffn1_norm, mix_norm, ffn2_norm, final_norm.reshape(1, D)
    grad, delta, new_m, new_v = {}, {}, {}, {}

    def pair_sums(keys, parts, got):
        return _pair_sum(parts, got, c_idx, "pair_sum_" + keys[0])

    def chip_sums(keys, chip_parts, owned):
        return _chip_sum(chip_parts, owned, place, "chip_sum_" + keys[0])

    def adamw(keys, after=()):
        outs = _adamw([wts[k][0] for k in keys], [grad[k][0] for k in keys], [mom[k][0] for k in keys],
                      [var[k][0] for k in keys], "adamw_" + keys[0], after=after)
        for k, res in zip(keys, outs):
            grad[k], delta[k], new_m[k], new_v[k] = (o.reshape(wts[k].shape) for o in res)

    first, late = ("ffn1_w_gate_up", "ffn1_w_down"), ("w_branch_pool", "w_branch_attn", "w_out",
                                                       "ffn2_w_gate_up", "ffn2_w_down")
    own = {}
    for group in (first, ("w_in",), late):
        own.update(zip(group, _cast_into_block([wts[k][0] for k in group], me_idx, "cast_" + group[0])))
    full = dict(zip(first, _exchange_alone(_ex_gather([own[k] for k in first]), "gather_ffn1")))
    wgu1, wd1 = full["ffn1_w_gate_up"], full["ffn1_w_down"].reshape(DFF, D)
    (h1, n1, gu1, a1), (win,) = _ffn_fwd(x0, g1, wgu1, wd1, "ffn1_fwd", exchange=_ex_gather_direct([own["w_in"]]))
    sems_l, thru_l, token_l = _gather_start([own[k_] for k_ in late], [h1], "gather_late_start")
    u, xp, q, k, v, gp, gs = _mix_in(h1, gm, win, after=(token_l,))
    o_sb, ctot = _attn_fwd(q, k, v)
    arrived = _gather_wait(sems_l, thru_l, [o_sb], "gather_late_wait")
    wbp, wba, wout = _exchange_alone(_ex_relay(arrived[:3]), "relay_mix")
    wout = wout.reshape(D, D)
    (h2, pm, p, yp, ys, mm), (wgu2, wd2) = _mix_out(h1, xp, o_sb, gp, gs, wgrp, pool_scale, wbp, wba, wout,
                                                    exchange=_ex_relay(arrived[3:]))
    wd2 = wd2.reshape(DFF, D)
    dh3, loss_row, d_gf, n3, gu3, a3 = _ffn_fwd(h2, g2, wgu2, wd2, "ffn2_fwd", head=(tgt, gf))

    def grad_gate_up(n, dgu, name, exchange=None):
        res = _wgrad(n, dgu, NSH, D, name, exchange=exchange)
        return [_halves(res)] if exchange is None else ([_halves(res[0])], res[1])

    def grad_down(a, dh, name, exchange=None):
        res = _wgrad(a, dh, 1, FFS, name, exchange=exchange)
        halves = lambda g: [_halves(g.reshape(NSH, DFF // NSH, D))]
        return halves(res) if exchange is None else (halves(res[0]), res[1])

    k_gu2, k_d2, k_gu1, k_d1, k_in = (("ffn2_w_gate_up",), ("ffn2_w_down",), ("ffn1_w_gate_up",),
                                      ("ffn1_w_down",), ("w_in",))
    dh2, dgu3, d_g2 = _ffn_bwd(dh3, h2, g2, gu3, wgu2, wd2, "ffn2_bwd")
    pa = grad_gate_up(n3, dgu3, "wgrad_gu2") + grad_down(a3, dh3, "wgrad_d2")
    (dlg, dyp, dys, do_sb, dyg, dxp, d_scale), got_a = _mix_bwd_out(
        dh2, gp, gs, yp, ys, pm, wgrp, pool_scale, wbp, wba, wout, exchange=_ex_pair_swap(pa))
    chip_a = pair_sums(k_gu2 + k_d2, pa, got_a)
    kb = ("w_out", "w_branch_pool", "w_branch_attn")
    g_bp, g_ba, d_group = _wgrad_branches(p, dyp, o_sb, dys, pm, dyg)
    pb = [_halves(_wgrad(mm, dh2, 1, D, "wgrad_out").reshape(NSH, D // NSH, D)), _halves(g_bp), _halves(g_ba)]
    k_a, k_in = k_gu2 + k_d2, k_in + kb
    sems_a, thru_a, token_a = _scatter_start(chip_a, "scatter_a_start")
    dq, dk, dv = _attn_bwd(q, k, v, do_sb, ctot, after=(token_a,))
    chip_a, owned_a = _scatter_wait(sems_a, thru_a, [dq], "scatter_a_wait")
    halves_a = chip_sums(k_a, chip_a, owned_a)
    dproj = (dxp, dq, dk, dv, dlg)
    (dh1, d_gm), both_a = _mix_bwd_in(dh2, h1, gm, dproj, win, exchange=_ex_share(halves_a))
    for i, k_ in enumerate(k_a):
        grad[k_] = both_a[i].reshape(wts[k_].shape)

    p_in = [_halves(_wgrad_in(u, dproj))] + pb
    p_d1, got_in = grad_down(a1, dh1, "wgrad_d1", exchange=_ex_pair_swap(p_in))
    sems_in, thru_in, token_in = _scatter_start(pair_sums(k_in, p_in, got_in), "scatter_in_start")
    dgu1, got_d1 = _ffn_bwd_act(dh1, gu1, wd1, "ffn1_bwd_act", exchange=_ex_pair_swap(p_d1), after=(token_in,))
    sems_d1, thru_d1, token_d1 = _scatter_start(pair_sums(k_d1, p_d1, got_d1), "scatter_d1_start")
    p_gu1 = [_halves(_wgrad(n1, dgu1, NSH, D, "wgrad_gu1", after=(token_in, token_d1)))]
    sems_w, thru_w, token_w = _swap_start(p_gu1, "swap_gu1_start")
    chip_in, owned_in = _scatter_wait(sems_in, thru_in, [token_w], "scatter_in_wait")
    chip_d1, owned_d1 = _scatter_wait(sems_d1, thru_d1, [token_w], "scatter_d1_wait")
    halves_in = chip_sums(k_in, chip_in, owned_in)
    p_gu1, got_gu1 = _swap_wait(sems_w, thru_w, halves_in, "swap_gu1_wait")
    sems, thru, token = _scatter_start(pair_sums(k_gu1, p_gu1, got_gu1), "scatter_gu1_start")
    sems_h, thru_h, token_h = _share_start(halves_in, [token], "share_in_start")
    adamw(k_a, after=(token_h,))
    landed = _share_wait(sems_h, thru_h, [delta[k_a[0]]], "share_in_wait")
    for i, k_ in enumerate(k_in):
        grad[k_] = landed[i].reshape(wts[k_].shape)
    adamw(k_in)
    dx, d_g1 = _ffn_bwd_in(dh1, x0, g1, dgu1, wgu1, "ffn1_bwd_in", after=(token,))
    small_g = dict(ffn1_norm=d_g1, mix_norm=d_gm, ffn2_norm=d_g2, final_norm=d_gf, pool_scale=d_scale,
                   pool_w_group=d_group, loss=loss_row)
    dev = 4 * lax.axis_index("x") + 2 * lax.axis_index("y") + lax.axis_index("c")
    slots = lax.dynamic_update_slice(jnp.zeros((8, SMALL_ROWS, 128), F32), _pack_small(small_g)[None], (dev, 0, 0))
    sems_s, slots, token_s = _small_gather_start(slots, "small_gather_start")

    chip_gu1, owned_gu1 = _scatter_wait(sems, thru, [dx] + [delta[k_] for k_ in k_a + k_in], "scatter_gu1_wait")
    halves_last = chip_sums(k_d1 + k_gu1, chip_d1 + chip_gu1, owned_d1 + owned_gu1)
    both = _exchange_alone(_ex_share(halves_last), "share_last",
                           after=(token_s,))
    grad["ffn1_w_down"] = both[0].reshape(ffn1_w_down.shape)
    grad["ffn1_w_gate_up"] = both[1].reshape(ffn1_w_gate_up.shape)
    adamw(k_d1 + k_gu1, after=(token_s,))
    gathered = _small_gather_wait(sems_s, slots, [delta[k_] for k_ in k_d1 + k_gu1], "small_gather_wait")
    gathered = gathered.reshape(8 * SMALL_ROWS, 128)
    results = _small_update(gathered, _pack_small(wts), _pack_small(mom), _pack_small(var), _small_entries(wts))
    for dst, entries in zip((grad, delta, new_m, new_v), results):
        for k_, rows in zip(SMALL, entries):
            if k_ in wts:
                dst[k_] = rows.reshape(wts[k_].shape)
            elif dst is grad:
                loss = rows[0, 0]
    return (loss, dx[None], *[grad[k_] for k_ in ORDER], *[delta[k_] for k_ in ORDER],
            *[new_m[k_] for k_ in ORDER], *[new_v[k_] for k_ in ORDER])
```

```python
import dataclasses
import functools

import jax
import jax.numpy as jnp
from jax import lax
from jax.experimental import pallas as pl
from jax.experimental.pallas import tpu as pltpu

F32 = jnp.float32
BF16 = jnp.bfloat16

S = 2048
D = 1024
DFF = 2816
FFS = 2 * DFF // 4
NSH = 4
PW = 512
PG = 128
POOL_WINDOWS = (2, 4, 8, 16)
HALO = 16
SBW = 512
DH = 64
EPS = 1e-6
SCALE = 0.125
LOG2E = 1.4426950408889634
TA = 256
QB = 2
MIB = 1024 * 1024

LR, B1, B2, AEPS, WD, STEP = 0.001, 0.9, 0.999, 1e-08, 0.01, 10

_VM = pl.BlockSpec(memory_space=pltpu.VMEM)
_ANY = pl.BlockSpec(memory_space=pl.ANY)
MESH = pl.DeviceIdType.MESH
SIBLING_PAIR_ID = 1


def _nn(a, b):
    return jnp.dot(a, b, preferred_element_type=F32)


def _nt(a, b):
    return lax.dot_general(a, b, (((1,), (1,)), ((), ())), preferred_element_type=F32)


def _tn(a, b):
    return lax.dot_general(a, b, (((0,), (0,)), ((), ())), preferred_element_type=F32)


def _params(sem, vmem_mib):
    return pltpu.CompilerParams(dimension_semantics=sem, vmem_limit_bytes=vmem_mib * MIB)


def _rows(tm, width):
    return pl.BlockSpec((tm, width), lambda i: (i, 0))


def _fixed(shape):
    return pl.BlockSpec(shape, lambda *_: (0,) * len(shape))


def _sds(shape, dtype):
    return pltpu.HBM(shape, dtype)


def _in_hbm(args):
    return [pltpu.with_memory_space_constraint(a, pltpu.HBM) for a in args]


def _stage(pairs):
    pieces = 4

    def copy_all(sems):
        copies = []
        for src, dst in pairs:
            step = src.shape[0] // pieces
            for p in range(pieces):
                part = pl.ds(p * step, step)
                if len(dst.shape) == len(src.shape):
                    piece = (src.at[part], dst.at[part])
                else:
                    piece = (src.at[p], dst.at[:, pl.ds(p * src.shape[2], src.shape[2])])
                copies.append(pltpu.make_async_copy(*piece, sems.at[len(copies)]))
        for c in copies:
            c.start()
        for c in copies:
            c.wait()

    @pl.when(pl.program_id(0) == 0)
    def _():
        pl.run_scoped(copy_all, pltpu.SemaphoreType.DMA((pieces * len(pairs),)))


def _vmem_like(*arrays):
    return [pltpu.VMEM(a.shape, a.dtype) for a in arrays]


def _vmem_wide(w):
    return pltpu.VMEM((w.shape[1], w.shape[0] * w.shape[2]), w.dtype)


FF_CHUNKS = ((0, 1536), (1536, DFF - 1536))


class Exchange:
    def __init__(self, arrays, landing, aliases, n_sems, start, finish, sibling_only=False):
        self.arrays, self.landing, self.aliases, self.n_sems = list(arrays), list(landing), dict(aliases), n_sems
        self.start, self.finish = start, finish
        self.sibling_only = sibling_only

    def enter(self):
        if self.sibling_only:
            barrier = pltpu.get_barrier_semaphore()
            sibling = (lax.axis_index("x"), lax.axis_index("y"), 1 - lax.axis_index("c"))
            pl.semaphore_signal(barrier, inc=1, device_id=sibling, device_id_type=MESH)
            pl.semaphore_wait(barrier, 1)

    def params(self, compiler_params=None):
        kw = dict(collective_id=SIBLING_PAIR_ID) if self.sibling_only else {}
        if compiler_params is None:
            return pltpu.CompilerParams(**kw)
        return dataclasses.replace(compiler_params, **kw)


def _call(body, args, *, name, grid, in_specs, out_specs, out_shape, scratch_shapes=(), compiler_params=None,
          exchange=None, free=(), after=()):
    args = [a if i in free else pltpu.with_memory_space_constraint(a, pltpu.HBM) for i, a in enumerate(args)]
    if exchange is None:
        n_in = len(in_specs)

        def plain(*refs):
            body(*refs[:n_in], *refs[n_in + len(after):])

        return pl.pallas_call(plain, name=name, grid=grid, in_specs=list(in_specs) + [_ANY] * len(after),
                              out_specs=out_specs, out_shape=out_shape, scratch_shapes=list(scratch_shapes),
                              compiler_params=compiler_params)(*args, *after)
    ex = exchange
    n_in, n_out, n_scr = len(in_specs), len(out_specs), len(scratch_shapes)
    na, nl = len(ex.arrays), len(ex.landing)

    def hosted(*refs):
        at = [0]

        def take(n):
            at[0] += n
            return refs[at[0] - n:at[0]]

        k_in, _, e_in, k_out, e_out, k_scr = take(n_in), take(len(after)), take(na), take(n_out), take(nl), take(n_scr)
        ssem, rsem = take(2)
        ids = [pl.program_id(a) for a in range(len(grid))]
        first = functools.reduce(jnp.logical_and, [i == 0 for i in ids])
        last = functools.reduce(jnp.logical_and, [i == g - 1 for i, g in zip(ids, grid)])

        @pl.when(first)
        def _():
            ex.enter()
            ex.start(e_in, e_out, ssem, rsem)

        body(*k_in, *k_out, *k_scr)

        @pl.when(last)
        def _():
            ex.finish(e_in, e_out, ssem, rsem)

    outs = pl.pallas_call(
        hosted, name=name, grid=grid,
        in_specs=list(in_specs) + [_ANY] * (len(after) + na), out_specs=list(out_specs) + [_ANY] * nl,
        out_shape=list(out_shape) + ex.landing,
        scratch_shapes=list(scratch_shapes) + [pltpu.SemaphoreType.DMA((ex.n_sems,))] * 2,
        input_output_aliases={n_in + len(after) + i: n_out + j for i, j in ex.aliases.items()},
        compiler_params=ex.params(compiler_params),
    )(*args, *after, *_in_hbm(ex.arrays))
    return outs[:n_out], outs[n_out:]


def _exchange_alone(ex, name, after=()):
    na, nl = len(ex.arrays), len(ex.landing)

    def body(*refs):
        outs = refs[na + len(after):na + len(after) + nl]
        ex.enter()
        ex.start(refs[:na], outs, refs[-2], refs[-1])
        ex.finish(refs[:na], outs, refs[-2], refs[-1])

    return pl.pallas_call(
        body, name=name, in_specs=[_ANY] * (na + len(after)), out_specs=[_ANY] * nl,
        out_shape=ex.landing, scratch_shapes=[pltpu.SemaphoreType.DMA((ex.n_sems,))] * 2,
        input_output_aliases=ex.aliases, compiler_params=ex.params(),
    )(*_in_hbm(ex.arrays), *after)


_HBM = pl.BlockSpec(memory_space=pltpu.HBM)
_SEM = pl.BlockSpec(memory_space=pltpu.SEMAPHORE)
_EFFECT = pltpu.SideEffectType.DATAFLOW_SIDE_EFFECTING


def _scatter_copies(srcs, lands, ssems, rsems):
    x, y, c, chips = _place()
    return [_remote(srcs[w].at[2 * px + py], lands[w].at[k], ssems[3 * w + k], rsems[3 * w + k], (px, py, c))
            for w in range(len(srcs)) for k, (px, py) in enumerate(chips)]


def _scatter_start(parts, name):
    parts = list(parts)
    n, ncp = len(parts), 3 * len(parts)
    lands = [lax.empty((3,) + p.shape[1:], p.dtype) for p in parts]

    def body(*refs):
        srcs, land_refs = refs[:n], refs[n:2 * n]
        ssems, rsems = refs[2 * n:2 * n + ncp], refs[2 * n + ncp:2 * n + 2 * ncp]
        for cp in _scatter_copies(srcs, land_refs, ssems, rsems):
            cp.start()
        token = refs[-1]
        token[...] = jnp.zeros_like(token)

    outs = pl.pallas_call(
        body, name=name,
        out_shape=([pltpu.SemaphoreType.DMA(())] * (2 * ncp) + [pltpu.HBM(a.shape, a.dtype) for a in parts + lands]
                   + [jax.ShapeDtypeStruct((8, 128), F32)]),
        in_specs=[_HBM] * (2 * n), out_specs=[_SEM] * (2 * ncp) + [_HBM] * (2 * n) + [_VM],
        input_output_aliases={i: 2 * ncp + i for i in range(2 * n)},
        compiler_params=pltpu.CompilerParams(has_side_effects=_EFFECT),
    )(*_in_hbm(parts), *_in_hbm(lands))
    sems, thru, token = outs[:2 * ncp], outs[2 * ncp:2 * ncp + 2 * n], outs[-1]
    return sems, thru, token


def _scatter_wait(sems, thru, after, name):
    n = len(thru) // 2
    ncp = 3 * n

    def body(*refs):
        srcs, land_refs = refs[:n], refs[n:2 * n]
        ssems, rsems = refs[2 * n:2 * n + ncp], refs[2 * n + ncp:2 * n + 2 * ncp]
        for cp in _scatter_copies(srcs, land_refs, ssems, rsems):
            cp.wait_send()
            cp.wait_recv()

    outs = pl.pallas_call(
        body, name=name, out_shape=[pltpu.HBM(a.shape, a.dtype) for a in thru],
        in_specs=[_HBM] * (2 * n) + [_SEM] * (2 * ncp) + [_ANY] * len(after), out_specs=[_HBM] * (2 * n),
        input_output_aliases={i: i for i in range(2 * n)},
        compiler_params=pltpu.CompilerParams(has_side_effects=_EFFECT),
    )(*thru, *sems, *after)
    return outs[:n], outs[n:]


def _swap_copies(srcs, lands, ssems, rsems):
    x, y, c, _ = _place()
    return [_remote(srcs[w].at[:, 1 - c], lands[w], ssems[w], rsems[w], (x, y, 1 - c)) for w in range(len(srcs))]


def _swap_start(grads, name):
    grads = list(grads)
    n = len(grads)
    lands = [lax.empty((NSH,) + g.shape[2:], g.dtype) for g in grads]

    def body(*refs):
        barrier = pltpu.get_barrier_semaphore()
        sibling = (lax.axis_index("x"), lax.axis_index("y"), 1 - lax.axis_index("c"))
        pl.semaphore_signal(barrier, inc=1, device_id=sibling, device_id_type=MESH)
        pl.semaphore_wait(barrier, 1)
        for cp in _swap_copies(refs[:n], refs[n:2 * n], refs[2 * n:3 * n], refs[3 * n:4 * n]):
            cp.start()
        refs[-1][...] = jnp.zeros_like(refs[-1])

    outs = pl.pallas_call(
        body, name=name,
        out_shape=([pltpu.SemaphoreType.DMA(())] * (2 * n) + [pltpu.HBM(a.shape, a.dtype) for a in grads + lands]
                   + [jax.ShapeDtypeStruct((8, 128), F32)]),
        in_specs=[_HBM] * (2 * n), out_specs=[_SEM] * (2 * n) + [_HBM] * (2 * n) + [_VM],
        input_output_aliases={i: 2 * n + i for i in range(2 * n)},
        compiler_params=pltpu.CompilerParams(has_side_effects=_EFFECT, collective_id=SIBLING_PAIR_ID),
    )(*_in_hbm(grads), *_in_hbm(lands))
    return outs[:2 * n], outs[2 * n:4 * n], outs[-1]


def _swap_wait(sems, thru, after, name):
    n = len(thru) // 2

    def body(*refs):
        for cp in _swap_copies(refs[:n], refs[n:2 * n], refs[2 * n:3 * n], refs[3 * n:4 * n]):
            cp.wait_send()
            cp.wait_recv()

    outs = pl.pallas_call(
        body, name=name, out_shape=[pltpu.HBM(a.shape, a.dtype) for a in thru],
        in_specs=[_HBM] * (2 * n) + [_SEM] * (2 * n) + [_ANY] * len(after), out_specs=[_HBM] * (2 * n),
        input_output_aliases={i: i for i in range(2 * n)},
        compiler_params=pltpu.CompilerParams(has_side_effects=_EFFECT),
    )(*thru, *sems, *after)
    return outs[:n], outs[n:]


def _share_copies(bufs, ssems, rsems, sending):
    x, y, c, _ = _place()
    out = []
    for w, ref in enumerate(bufs):
        slot = ref.at[c if sending else 1 - c]
        out.append(_remote(slot, slot, ssems[w], rsems[w], (x, y, 1 - c)))
    return out


def _share_start(bufs, after, name):
    bufs = list(bufs)
    n = len(bufs)

    def body(*refs):
        barrier = pltpu.get_barrier_semaphore()
        sibling = (lax.axis_index("x"), lax.axis_index("y"), 1 - lax.axis_index("c"))
        pl.semaphore_signal(barrier, inc=1, device_id=sibling, device_id_type=MESH)
        pl.semaphore_wait(barrier, 1)
        at = n + len(after)
        for cp in _share_copies(refs[:n], refs[at:at + n], refs[at + n:at + 2 * n], True):
            cp.start()
        refs[-1][...] = jnp.zeros_like(refs[-1])

    outs = pl.pallas_call(
        body, name=name,
        out_shape=([pltpu.SemaphoreType.DMA(())] * (2 * n) + [pltpu.HBM(a.shape, a.dtype) for a in bufs]
                   + [jax.ShapeDtypeStruct((8, 128), F32)]),
        in_specs=[_HBM] * n + [_ANY] * len(after), out_specs=[_SEM] * (2 * n) + [_HBM] * n + [_VM],
        input_output_aliases={i: 2 * n + i for i in range(n)},
        compiler_params=pltpu.CompilerParams(has_side_effects=_EFFECT, collective_id=SIBLING_PAIR_ID),
    )(*_in_hbm(bufs), *after)
    return outs[:2 * n], outs[2 * n:3 * n], outs[-1]


def _share_wait(sems, thru, after, name):
    n = len(thru)

    def body(*refs):
        for cp in _share_copies(refs[:n], refs[n:2 * n], refs[2 * n:3 * n], True):
            cp.wait_send()
        for cp in _share_copies(refs[:n], refs[n:2 * n], refs[2 * n:3 * n], False):
            cp.wait_recv()

    return pl.pallas_call(
        body, name=name, out_shape=[pltpu.HBM(a.shape, a.dtype) for a in thru],
        in_specs=[_HBM] * n + [_SEM] * (2 * n) + [_ANY] * len(after), out_specs=[_HBM] * n,
        input_output_aliases={i: i for i in range(n)},
        compiler_params=pltpu.CompilerParams(has_side_effects=_EFFECT),
    )(*thru, *sems, *after)


def _gather_copies(bufs, ssems, rsems, sending):
    x, y, c, chips = _place()
    out = []
    for w, ref in enumerate(bufs):
        half = ref.shape[1] // 2
        for k, (px, py) in enumerate(chips):
            rows = ref.at[2 * x + y if sending else 2 * px + py, pl.ds(c * half, half)]
            out.append(_remote(rows, rows, ssems[3 * w + k], rsems[3 * w + k], (px, py, c)))
    return out


def _gather_start(bufs, after, name):
    n, ncp = len(bufs), 3 * len(bufs)

    def body(*refs):
        ssems, rsems = refs[n + len(after):n + len(after) + ncp], refs[n + len(after) + ncp:n + len(after) + 2 * ncp]
        for cp in _gather_copies(refs[:n], ssems, rsems, True):
            cp.start()
        token = refs[-1]
        token[...] = jnp.zeros_like(token)

    outs = pl.pallas_call(
        body, name=name,
        out_shape=([pltpu.SemaphoreType.DMA(())] * (2 * ncp) + [pltpu.HBM(a.shape, a.dtype) for a in bufs]
                   + [jax.ShapeDtypeStruct((8, 128), F32)]),
        in_specs=[_HBM] * n + [_ANY] * len(after), out_specs=[_SEM] * (2 * ncp) + [_HBM] * n + [_VM],
        input_output_aliases={i: 2 * ncp + i for i in range(n)},
        compiler_params=pltpu.CompilerParams(has_side_effects=_EFFECT),
    )(*_in_hbm(bufs), *after)
    return outs[:2 * ncp], outs[2 * ncp:2 * ncp + n], outs[-1]


def _gather_wait(sems, thru, after, name):
    n = len(thru)
    ncp = 3 * n

    def body(*refs):
        ssems, rsems = refs[n:n + ncp], refs[n + ncp:n + 2 * ncp]
        for cp in _gather_copies(refs[:n], ssems, rsems, True):
            cp.wait_send()
        for cp in _gather_copies(refs[:n], ssems, rsems, False):
            cp.wait_recv()

    return pl.pallas_call(
        body, name=name, out_shape=[pltpu.HBM(a.shape, a.dtype) for a in thru],
        in_specs=[_HBM] * n + [_SEM] * (2 * ncp) + [_ANY] * len(after), out_specs=[_HBM] * n,
        input_output_aliases={i: i for i in range(n)},
        compiler_params=pltpu.CompilerParams(has_side_effects=_EFFECT),
    )(*thru, *sems, *after)


def _rms(x):
    r = lax.rsqrt(jnp.mean(x * x, axis=-1, keepdims=True) + EPS)
    return r, x * r


def _rms_bwd(dn, xr, r, gain):
    dng = dn * gain
    dx = r * (dng - xr * jnp.mean(dng * xr, axis=-1, keepdims=True))
    return dx, jnp.sum(dn * xr, axis=0, keepdims=True)


def _ffn_fwd(x, gain, wgu, wd, name, exchange=None, head=None):
    tm = 256

    def body(x_ref, g_ref, wgu_hbm, wd_hbm, *rest):
        if head is None:
            h_ref, n_ref, gu_ref, a_ref, wgu_ref, wd_ref = rest
        else:
            t_ref, gf_ref, h_ref, loss_ref, dgf_ref, n_ref, gu_ref, a_ref, wgu_ref, wd_ref = rest
        _stage([(wgu_hbm, wgu_ref), (wd_hbm, wd_ref)])
        x = x_ref[...]
        _, xr = _rms(x)
        n = (xr * g_ref[...]).astype(BF16)
        n_ref[...] = n
        acc = jnp.zeros((tm, D), F32)
        for c0, cn in FF_CHUNKS:
            g = _nn(n, wgu_ref[:, c0:c0 + cn])
            u = _nn(n, wgu_ref[:, DFF + c0:DFF + c0 + cn])
            gu_ref[:, c0:c0 + cn] = g.astype(BF16)
            gu_ref[:, DFF + c0:DFF + c0 + cn] = u.astype(BF16)
            half_act = (0.5 * (g * jax.nn.sigmoid(g) * u)).astype(BF16)
            a_ref[:, c0:c0 + cn] = half_act
            acc = acc + _nn(half_act, wd_ref[c0:c0 + cn, :])
        h = x + acc
        if head is None:
            h_ref[...] = h
            return
        gf = gf_ref[...]
        r, hr = _rms(h)
        err = hr * gf - t_ref[...]
        dh, dgain = _rms_bwd(err * (1.0 / D), hr, r, gf)
        h_ref[...] = dh

        @pl.when(pl.program_id(0) == 0)
        def _():
            dgf_ref[...] = jnp.zeros_like(dgf_ref)
            loss_ref[...] = jnp.zeros_like(loss_ref)

        dgf_ref[...] += dgain
        loss_ref[...] += jnp.full((1, 128), (0.5 / D) * jnp.sum(err * err), F32)

    saved_specs = [_rows(tm, D), _rows(tm, 4 * FFS), _rows(tm, DFF)]
    saved_shapes = [_sds((S, D), BF16), _sds((S, 4 * FFS), BF16), _sds((S, DFF), BF16)]
    if head is None:
        return _call(
            body, (x, gain, wgu, wd), name=name, grid=(S // tm,),
            in_specs=[_rows(tm, D), _fixed((1, D)), _ANY, _ANY],
            out_specs=[_rows(tm, D)] + saved_specs, out_shape=[_sds((S, D), F32)] + saved_shapes,
            scratch_shapes=[_vmem_wide(wgu)] + _vmem_like(wd),
            compiler_params=_params(("arbitrary",), 56), exchange=exchange)
    return _call(
        body, (x, gain, wgu, wd, *head), name=name, grid=(S // tm,),
        in_specs=[_rows(tm, D), _fixed((1, D)), _ANY, _ANY, _rows(tm, D), _fixed((1, D))],
        out_specs=[_rows(tm, D), _fixed((1, 128)), _fixed((1, D))] + saved_specs,
        out_shape=[_sds((S, D), F32), _sds((1, 128), F32), _sds((1, D), F32)] + saved_shapes,
        scratch_shapes=[_vmem_wide(wgu)] + _vmem_like(wd),
        compiler_params=_params(("arbitrary",), 56), exchange=exchange, free=(4, 5))


def _ffn_bwd(dh, x, gain, gu, wgu, wd, name):
    tm = 256

    def body(dh_ref, x_ref, g_ref, gu_ref, wgu_hbm, wd_hbm, dx_ref, dgu_ref, dg_ref, wgu_ref, wd_ref):
        _stage([(wgu_hbm, wgu_ref), (wd_hbm, wd_ref)])
        dh = dh_ref[...]
        dhb = dh.astype(BF16)
        dn = jnp.zeros((tm, D), F32)
        for c0, cn in FF_CHUNKS:
            g = gu_ref[:, c0:c0 + cn].astype(F32)
            u = gu_ref[:, DFF + c0:DFF + c0 + cn].astype(F32)
            da = 0.5 * _nt(dhb, wd_ref[c0:c0 + cn, :])
            sg = jax.nn.sigmoid(g)
            dgb = (da * u * (sg * (1.0 + g * (1.0 - sg)))).astype(BF16)
            dub = (da * (g * sg)).astype(BF16)
            dgu_ref[:, c0:c0 + cn] = dgb
            dgu_ref[:, DFF + c0:DFF + c0 + cn] = dub
            dn = dn + _nt(dgb, wgu_ref[:, c0:c0 + cn]) + _nt(dub, wgu_ref[:, DFF + c0:DFF + c0 + cn])
        r, xr = _rms(x_ref[...])
        dx, dgain = _rms_bwd(dn, xr, r, g_ref[...])
        dx_ref[...] = dh + dx

        @pl.when(pl.program_id(0) == 0)
        def _():
            dg_ref[...] = jnp.zeros_like(dg_ref)

        dg_ref[...] += dgain

    return _call(
        body, (dh, x, gain, gu, wgu, wd), name=name, grid=(S // tm,),
        in_specs=[_rows(tm, D), _rows(tm, D), _fixed((1, D)), _rows(tm, 4 * FFS), _ANY, _ANY],
        out_specs=[_rows(tm, D), _rows(tm, 4 * FFS), _fixed((1, D))],
        out_shape=[_sds((S, D), F32), _sds((S, 4 * FFS), BF16), _sds((1, D), F32)],
        scratch_shapes=[_vmem_wide(wgu)] + _vmem_like(wd), compiler_params=_params(("arbitrary",), 56))


def _ffn_bwd_act(dh, gu, wd, name, exchange=None, after=()):
    tm = 512

    def body(dh_ref, gu_ref, wd_hbm, dgu_ref, wd_ref):
        _stage([(wd_hbm, wd_ref)])
        dhb = dh_ref[...].astype(BF16)
        for c0, cn in FF_CHUNKS:
            g = gu_ref[:, c0:c0 + cn].astype(F32)
            u = gu_ref[:, DFF + c0:DFF + c0 + cn].astype(F32)
            da = 0.5 * _nt(dhb, wd_ref[c0:c0 + cn, :])
            sg = jax.nn.sigmoid(g)
            dgu_ref[:, c0:c0 + cn] = (da * u * (sg * (1.0 + g * (1.0 - sg)))).astype(BF16)
            dgu_ref[:, DFF + c0:DFF + c0 + cn] = (da * (g * sg)).astype(BF16)

    res = _call(
        body, (dh, gu, wd), name=name, grid=(S // tm,),
        in_specs=[_rows(tm, D), _rows(tm, 4 * FFS), _ANY], out_specs=[_rows(tm, 4 * FFS)],
        out_shape=[_sds((S, 4 * FFS), BF16)], scratch_shapes=_vmem_like(wd),
        compiler_params=_params(("arbitrary",), 56), exchange=exchange, after=after)
    return res[0] if exchange is None else (res[0][0], res[1])


def _ffn_bwd_in(dh, x, gain, dgu, wgu, name, exchange=None, after=()):
    tm = 512

    def body(dh_ref, x_ref, g_ref, dgu_ref, wgu_hbm, dx_ref, dg_ref, wgu_ref):
        _stage([(wgu_hbm, wgu_ref)])
        dn = jnp.zeros((tm, D), F32)
        for half in (0, DFF):
            for c0, cn in FF_CHUNKS:
                cols = slice(half + c0, half + c0 + cn)
                dn = dn + _nt(dgu_ref[:, cols], wgu_ref[:, cols])
        r, xr = _rms(x_ref[...])
        dx, dgain = _rms_bwd(dn, xr, r, g_ref[...])
        dx_ref[...] = dh_ref[...] + dx

        @pl.when(pl.program_id(0) == 0)
        def _():
            dg_ref[...] = jnp.zeros_like(dg_ref)

        dg_ref[...] += dgain

    return _call(
        body, (dh, x, gain, dgu, wgu), name=name, grid=(S // tm,),
        in_specs=[_rows(tm, D), _rows(tm, D), _fixed((1, D)), _rows(tm, 4 * FFS), _ANY],
        out_specs=[_rows(tm, D), _fixed((1, D))],
        out_shape=[_sds((S, D), F32), _sds((1, D), F32)],
        scratch_shapes=[_vmem_wide(wgu)],
        compiler_params=_params(("arbitrary",), 56), exchange=exchange, after=after)


def _mix_in(h, gain, w_in, after=()):
    tm = 512

    def body(h_ref, g_ref, w_hbm, u_ref, xp_ref, q_ref, k_ref, v_ref, gp_ref, gs_ref, w_ref):
        _stage([(w_hbm, w_ref)])
        _, hr = _rms(h_ref[...])
        u = (hr * g_ref[...]).astype(BF16)
        u_ref[...] = u
        p0 = _nn(u, w_ref[0])
        xp_ref[...] = p0[:, :PW]
        q_ref[...] = p0[:, PW:].astype(BF16)
        p1 = _nn(u, w_ref[1])
        k_ref[...] = p1[:, :SBW].astype(BF16)
        v_ref[...] = p1[:, SBW:].astype(BF16)
        gp_ref[...] = jax.nn.sigmoid(_nn(u, w_ref[2])).astype(BF16)
        gs_ref[...] = jax.nn.sigmoid(_nn(u, w_ref[3])).astype(BF16)

    return _call(
        body, (h, gain, w_in), name="mix_in", grid=(S // tm,),
        in_specs=[_rows(tm, D), _fixed((1, D)), _ANY],
        out_specs=[_rows(tm, D), _rows(tm, PW), _rows(tm, SBW), _rows(tm, SBW), _rows(tm, SBW),
                   _rows(tm, D), _rows(tm, D)],
        out_shape=[_sds((S, D), BF16), _sds((S, PW), F32), _sds((S, SBW), BF16), _sds((S, SBW), BF16),
                   _sds((S, SBW), BF16), _sds((S, D), BF16), _sds((S, D), BF16)],
        scratch_shapes=_vmem_like(w_in),
        compiler_params=_params(("arbitrary",), 48), free=(1,), after=after)


def _hilo_dot(x, tri):
    hi = x.astype(BF16)
    lo = (x - hi.astype(F32)).astype(BF16)
    return _nn(hi, tri) + _nn(lo, tri)


def _log_terms(qk):
    z2 = qk * (SCALE * LOG2E)
    lb = jnp.minimum(z2, 0.0) - jnp.log2(1.0 + jnp.exp2(-jnp.abs(z2)))
    return lb, lb - z2


def _head_masks():
    lane = lax.broadcasted_iota(jnp.int32, (1, 2 * DH), 1)
    return (lane < DH, lane >= DH)


def _attn_fwd(q, k, v, exchange=None):
    T = TA

    def body(q_ref, k_ref, v_ref, o_ref, c_ref):
        i2 = 2 * pl.program_id(1)
        row = lax.broadcasted_iota(jnp.int32, (T, T), 0)
        col = lax.broadcasted_iota(jnp.int32, (T, T), 1)
        after = (row > col).astype(BF16)
        causal = col < row
        masks = _head_masks()
        qms = {}
        for b in range(QB):
            q2 = q_ref[b * T:(b + 1) * T, :]
            for h, hm in enumerate(masks):
                qms[b, h] = jnp.where(hm, q2, jnp.zeros_like(q2))

        def blocks(keys, pairs, carries, os):
            ks, vms = [], []
            for j in keys:
                rows = pl.ds(pl.multiple_of(j * T, T), T)
                vj = v_ref[rows, :]
                ks.append(k_ref[rows, :])
                vms.append([jnp.where(hm, vj, jnp.zeros_like(vj)) for hm in masks])
            units = [(n, h) for n in range(len(pairs)) for h in range(2)]
            qks = {(n, h): _nt(qms[pairs[n][0], h], ks[pairs[n][1]]) for n, h in units}
            lbs, l1ms = {}, {}
            for u in units:
                lbs[u], l1m = _log_terms(qks[u])
                l1ms[u] = jnp.where(causal, l1m, 0.0) if pairs[u[0]][2] else l1m
            cins = {u: _hilo_dot(l1ms[u], after) for u in units}
            carries, os = dict(carries), list(os)
            for n, h in units:
                b, key, diag = pairs[n]
                a = jnp.exp2(lbs[n, h] + cins[n, h] + carries[b, h])
                if diag:
                    a = jnp.where(causal, a, 0.0)
                os[b] = os[b] + _nn(a.astype(BF16), vms[key][h])
                carries[b, h] = carries[b, h] + jnp.sum(l1ms[n, h], axis=1, keepdims=True)
            return carries, tuple(os)

        carries = {(b, h): jnp.zeros((T, 1), F32) for b in range(QB) for h in range(2)}
        os = tuple(jnp.zeros((T, 2 * DH), F32) for _ in range(QB))
        carries, os = blocks([i2 + 1, i2], [(1, 0, True), (0, 1, True), (1, 1, False)], carries, os)
        carries, os = lax.fori_loop(
            0, i2 // 2,
            lambda t, c: blocks([i2 - 1 - 2 * t, i2 - 2 - 2 * t],
                                [(0, 0, False), (1, 0, False), (0, 1, False), (1, 1, False)], c[0], c[1]),
            (carries, os))
        for b in range(QB):
            o_ref[b * T:(b + 1) * T, :] = os[b].astype(BF16)
            c_ref[b * T:(b + 1) * T, :] = jnp.where(masks[0], carries[b, 0], carries[b, 1])

    blk = pl.BlockSpec((QB * T, 2 * DH), lambda p, i: (i, p))
    full = pl.BlockSpec((S, 2 * DH), lambda p, i: (0, p))
    return _call(
        body, (q, k, v), name="attn_fwd", grid=(SBW // (2 * DH), S // (QB * T)),
        in_specs=[blk, full, full], out_specs=[blk, blk],
        out_shape=[_sds((S, SBW), BF16), _sds((S, SBW), F32)],
        compiler_params=_params(("arbitrary", "arbitrary"), 40), exchange=exchange)


def _attn_bwd(q, k, v, do, ctot, after=()):
    T = TA
    nq = S // (QB * T)

    def body(q_ref, k_ref, v_ref, do_ref, c_ref, dq_ref, dk_ref, dv_ref, dk_acc, dv_acc):
        step = pl.program_id(1)
        i2 = 2 * step

        @pl.when(step == 0)
        def _():
            dk_acc[...] = jnp.zeros_like(dk_acc)
            dv_acc[...] = jnp.zeros_like(dv_acc)

        row = lax.broadcasted_iota(jnp.int32, (T, T), 0)
        col = lax.broadcasted_iota(jnp.int32, (T, T), 1)
        upto = (row <= col).astype(BF16)
        before = (row < col).astype(BF16)
        causal = col < row
        masks = _head_masks()
        qms, doms, ctots = {}, {}, {}
        for b in range(QB):
            q2, do2 = q_ref[b * T:(b + 1) * T, :], do_ref[b * T:(b + 1) * T, :]
            for h, hm in enumerate(masks):
                qms[b, h] = jnp.where(hm, q2, jnp.zeros_like(q2))
                doms[b, h] = jnp.where(hm, do2, jnp.zeros_like(do2))
                ctots[b, h] = c_ref[b * T:(b + 1) * T, h * DH:h * DH + 1]

        def blocks(keys, pairs, sums, dqs):
            rows = [pl.ds(pl.multiple_of(j * T, T), T) for j in keys]
            ks, vs = [k_ref[r, :] for r in rows], [v_ref[r, :] for r in rows]
            kms = [[jnp.where(hm, kj, jnp.zeros_like(kj)) for hm in masks] for kj in ks]
            units = [(n, h) for n in range(len(pairs)) for h in range(2)]
            qks = {(n, h): _nt(qms[pairs[n][0], h], ks[pairs[n][1]]) for n, h in units}
            das = {(n, h): _nt(doms[pairs[n][0], h], vs[pairs[n][1]]) for n, h in units}
            lbs, l1ms = {}, {}
            for u in units:
                lbs[u], l1m = _log_terms(qks[u])
                l1ms[u] = jnp.where(causal, l1m, 0.0) if pairs[u[0]][2] else l1m
            pins = {u: _hilo_dot(l1ms[u], upto) for u in units}
            sums = dict(sums)
            a_s, dls, cps = {}, {}, {}
            for n, h in units:
                b, _, diag = pairs[n]
                cl, cp = sums[b, h]
                a = jnp.exp2(lbs[n, h] + (ctots[b, h] - cl) - pins[n, h])
                if diag:
                    a = jnp.where(causal, a, 0.0)
                a_s[n, h] = a.astype(BF16)
                dls[n, h] = das[n, h] * a
                cps[n, h] = cp
                sums[b, h] = (cl + jnp.sum(l1ms[n, h], axis=1, keepdims=True),
                              cp + jnp.sum(dls[n, h], axis=1, keepdims=True))
            pexs = {u: _hilo_dot(dls[u], before) for u in units}
            dzbs = {}
            for u in units:
                dz = dls[u] - jnp.exp2(lbs[u]) * (dls[u] + pexs[u] + cps[u])
                if pairs[u[0]][2]:
                    dz = jnp.where(causal, dz, 0.0)
                dzbs[u] = dz.astype(BF16)
            dqs = list(dqs)
            for n, h in units:
                dqs[pairs[n][0]] = dqs[pairs[n][0]] + _nn(dzbs[n, h], kms[pairs[n][1]][h])
            for key, r in enumerate(rows):
                mine = [(n, h) for n, h in units if pairs[n][1] == key]
                dk_acc[r, :] += functools.reduce(jnp.add, [_tn(dzbs[u], qms[pairs[u[0]][0], u[1]]) for u in mine])
                dv_acc[r, :] += functools.reduce(jnp.add, [_tn(a_s[u], doms[pairs[u[0]][0], u[1]]) for u in mine])
            return sums, tuple(dqs)

        zero = jnp.zeros((T, 1), F32)
        sums = {(b, h): (zero, zero) for b in range(QB) for h in range(2)}
        dqs = tuple(jnp.zeros((T, 2 * DH), F32) for _ in range(QB))
        sums, dqs = lax.fori_loop(
            0, i2 // 2,
            lambda t, c: blocks([2 * t, 2 * t + 1],
                                [(0, 0, False), (1, 0, False), (0, 1, False), (1, 1, False)], c[0], c[1]),
            (sums, dqs))
        _, dqs = blocks([i2, i2 + 1], [(0, 0, True), (1, 0, False), (1, 1, True)], sums, dqs)
        for b in range(QB):
            dq_ref[b * T:(b + 1) * T, :] = (dqs[b] * SCALE).astype(BF16)

        @pl.when(step == nq - 1)
        def _():
            dk_ref[...] = (dk_acc[...] * SCALE).astype(BF16)
            dv_ref[...] = dv_acc[...].astype(BF16)

    blk = pl.BlockSpec((QB * T, 2 * DH), lambda p, i: (i, p))
    full = pl.BlockSpec((S, 2 * DH), lambda p, i: (0, p))
    return _call(
        body, (q, k, v, do, ctot), name="attn_bwd", grid=(SBW // (2 * DH), nq),
        in_specs=[blk, full, full, blk, blk], out_specs=[blk, full, full],
        out_shape=[_sds((S, SBW), BF16), _sds((S, SBW), BF16), _sds((S, SBW), BF16)],
        scratch_shapes=[pltpu.VMEM((S, 2 * DH), F32), pltpu.VMEM((S, 2 * DH), F32)],
        compiler_params=_params(("arbitrary", "arbitrary"), 40), after=after)


def _pool_counts(first_row, tm):
    pos = first_row + lax.broadcasted_iota(jnp.int32, (tm, 1), 0)
    return [jnp.minimum(pos + 1, w).astype(F32) for w in POOL_WINDOWS]


def _mix_out(h, xp, o_sb, gp, gs, w_group, scale, w_bp, w_ba, w_out, exchange=None):
    tm = 512

    def body(h_ref, xp_ref, o_ref, gp_ref, gs_ref, wg_hbm, sc_ref, wbp_hbm, wba_hbm, wo_hbm,
             h2_ref, pm_ref, p_ref, yp_ref, ys_ref, m_ref, halo, wg_ref, wbp_ref, wba_ref, wo_ref):
        _stage([(wg_hbm, wg_ref), (wbp_hbm, wbp_ref), (wba_hbm, wba_ref), (wo_hbm, wo_ref)])
        i = pl.program_id(0)

        @pl.when(i == 0)
        def _():
            halo[...] = jnp.zeros_like(halo)

        xp = xp_ref[...]
        ext = jnp.concatenate([halo[...], xp], axis=0)
        halo[...] = xp[tm - HALO:, :]
        counts = _pool_counts(i * tm, tm)
        for gi in range(len(POOL_WINDOWS)):
            lanes = slice(gi * PG, (gi + 1) * PG)
            win = ext[:, lanes]
            for step in range(gi + 1):
                win = win + pltpu.roll(win, 1 << step, 0)
            pm = (win[HALO:, :] / counts[gi] - xp[:, lanes]).astype(BF16)
            pm_ref[:, lanes] = pm
            p_ref[:, lanes] = (_nn(pm, wg_ref[gi]) * sc_ref[:, lanes]).astype(BF16)
        pb = p_ref[...]
        ob = o_ref[...]
        for j in range(NSH):
            cols = slice(j * (D // NSH), (j + 1) * (D // NSH))
            yp = _nn(pb, wbp_ref[j])
            ys = _nn(ob, wba_ref[j])
            yp_ref[:, cols] = yp.astype(BF16)
            ys_ref[:, cols] = ys.astype(BF16)
            m_ref[:, cols] = (gp_ref[:, cols].astype(F32) * yp + gs_ref[:, cols].astype(F32) * ys).astype(BF16)
        h2_ref[...] = h_ref[...] + _nn(m_ref[...], wo_ref[...])

    return _call(
        body, (h, xp, o_sb, gp, gs, w_group, scale, w_bp, w_ba, w_out), name="mix_out", grid=(S // tm,),
        in_specs=[_rows(tm, D), _rows(tm, PW), _rows(tm, SBW), _rows(tm, D), _rows(tm, D),
                  _ANY, _fixed((1, PW)), _ANY, _ANY, _ANY],
        out_specs=[_rows(tm, D), _rows(tm, PW), _rows(tm, PW), _rows(tm, D), _rows(tm, D), _rows(tm, D)],
        out_shape=[_sds((S, D), F32), _sds((S, PW), BF16), _sds((S, PW), BF16), _sds((S, D), BF16),
                   _sds((S, D), BF16), _sds((S, D), BF16)],
        scratch_shapes=[pltpu.VMEM((HALO, PW), F32)] + _vmem_like(w_group, w_bp, w_ba, w_out),
        compiler_params=_params(("arbitrary",), 48), free=(5, 6), exchange=exchange)


def _mix_bwd_out(dh, gp, gs, yp, ys, pm, w_group, scale, w_bp, w_ba, w_out, exchange=None):
    tm = 512
    nt = S // tm

    def body(dh_ref, gp_ref, gs_ref, yp_ref, ys_ref, pm_ref, wg_hbm, sc_ref, wbp_hbm, wba_hbm, wo_hbm,
             dlg_ref, dyp_ref, dys_ref, do_ref, dyg_ref, dxp_ref, dsc_ref, halo, wg_ref, wbp_ref, wba_ref, wo_ref):
        _stage([(wg_hbm, wg_ref), (wbp_hbm, wbp_ref), (wba_hbm, wba_ref), (wo_hbm, wo_ref)])
        step = pl.program_id(0)

        @pl.when(step == 0)
        def _():
            halo[...] = jnp.zeros_like(halo)
            dsc_ref[...] = jnp.zeros_like(dsc_ref)

        dm = _nt(dh_ref[...].astype(BF16), wo_ref[...])
        gp = gp_ref[...].astype(F32)
        gs = gs_ref[...].astype(F32)
        yp = yp_ref[...].astype(F32)
        ys = ys_ref[...].astype(F32)
        dlg_ref[:, :D] = (dm * yp * gp * (1.0 - gp)).astype(BF16)
        dlg_ref[:, D:] = (dm * ys * gs * (1.0 - gs)).astype(BF16)
        dyp_ref[...] = (dm * gp).astype(BF16)
        dys_ref[...] = (dm * gs).astype(BF16)
        dp = jnp.zeros((tm, PW), F32)
        do = jnp.zeros((tm, SBW), F32)
        for j in range(NSH):
            cols = slice(j * (D // NSH), (j + 1) * (D // NSH))
            dp = dp + _nt(dyp_ref[:, cols], wbp_ref[j])
            do = do + _nt(dys_ref[:, cols], wba_ref[j])
        do_ref[...] = do.astype(BF16)
        counts = _pool_counts((nt - 1 - step) * tm, tm)
        dscale = []
        for gi in range(len(POOL_WINDOWS)):
            lanes = slice(gi * PG, (gi + 1) * PG)
            dpg = dp[:, lanes]
            dscale.append(jnp.sum(dpg * _nn(pm_ref[:, lanes], wg_ref[gi]), axis=0, keepdims=True))
            dyg = (dpg * sc_ref[:, lanes]).astype(BF16)
            dyg_ref[:, lanes] = dyg
            dpm = _nt(dyg, wg_ref[gi])
            per = dpm / counts[gi]
            win = jnp.concatenate([per, halo[:, lanes]], axis=0)
            halo[:, lanes] = per[:HALO, :]
            for s in range(gi + 1):
                win = win + pltpu.roll(win, tm + HALO - (1 << s), 0)
            dxp_ref[:, lanes] = (win[:tm, :] - dpm).astype(BF16)
        dsc_ref[...] += jnp.concatenate(dscale, axis=1)

    rev = lambda width: pl.BlockSpec((tm, width), lambda i: (nt - 1 - i, 0))
    return _call(
        body, (dh, gp, gs, yp, ys, pm, w_group, scale, w_bp, w_ba, w_out), name="mix_bwd_out", grid=(nt,),
        in_specs=[rev(D), rev(D), rev(D), rev(D), rev(D), rev(PW), _ANY, _fixed((1, PW)), _ANY, _ANY, _ANY],
        out_specs=[rev(2 * D), rev(D), rev(D), rev(SBW), rev(PW), rev(PW), _fixed((1, PW))],
        out_shape=[_sds((S, 2 * D), BF16), _sds((S, D), BF16), _sds((S, D), BF16), _sds((S, SBW), BF16),
                   _sds((S, PW), BF16), _sds((S, PW), BF16), _sds((1, PW), F32)],
        scratch_shapes=[pltpu.VMEM((HALO, PW), F32)] + _vmem_like(w_group, w_bp, w_ba, w_out),
        compiler_params=_params(("arbitrary",), 48), exchange=exchange)


def _mix_bwd_in(dh, h, gain, pieces, w_in, exchange=None):
    tm = 512
    widths = [p.shape[1] for p in pieces]

    def body(dh_ref, h_ref, g_ref, *rest):
        piece_refs, (w_hbm, dx_ref, dg_ref, w_ref, dp_ref) = rest[:len(pieces)], rest[len(pieces):]
        _stage([(w_hbm, w_ref)])
        at = 0
        for ref, width in zip(piece_refs, widths):
            dp_ref[:, at:at + width] = ref[...]
            at += width
        du = jnp.zeros((tm, D), F32)
        for j in range(NSH):
            du = du + _nt(dp_ref[:, j * D:(j + 1) * D], w_ref[j])
        r, hr = _rms(h_ref[...])
        dx, dgain = _rms_bwd(du, hr, r, g_ref[...])
        dx_ref[...] = dh_ref[...] + dx

        @pl.when(pl.program_id(0) == 0)
        def _():
            dg_ref[...] = jnp.zeros_like(dg_ref)

        dg_ref[...] += dgain

    return _call(
        body, (dh, h, gain, *pieces, w_in), name="mix_bwd_in", grid=(S // tm,),
        in_specs=[_rows(tm, D), _rows(tm, D), _fixed((1, D))] + [_rows(tm, w) for w in widths] + [_ANY],
        out_specs=[_rows(tm, D), _fixed((1, D))],
        out_shape=[_sds((S, D), F32), _sds((1, D), F32)],
        scratch_shapes=_vmem_like(w_in) + [pltpu.VMEM((tm, 4 * D), BF16)],
        compiler_params=_params(("arbitrary",), 48), exchange=exchange)


def _wgrad_in(u, pieces):
    dxp, dq, dk, dv, dlg = pieces

    def body(u_ref, dxp_ref, dq_ref, dk_ref, dv_ref, dlg_ref, o_ref):
        j = pl.program_id(0)
        u = u_ref[...]

        def two(left_ref, right_ref):
            o_ref[:, :PW] = _tn(u, left_ref[...]).astype(BF16)
            o_ref[:, PW:] = _tn(u, right_ref[...]).astype(BF16)

        pl.when(j == 0)(lambda: two(dxp_ref, dq_ref))
        pl.when(j == 1)(lambda: two(dk_ref, dv_ref))

        @pl.when(j >= 2)
        def _():
            o_ref[...] = _tn(u, dlg_ref[...]).astype(BF16)

    whole = lambda width: pl.BlockSpec((S, width), lambda j: (0, 0))
    return _call(
        body, (u, dxp, dq, dk, dv, dlg), name="wgrad_in", grid=(NSH,),
        in_specs=[whole(D), whole(PW), whole(SBW), whole(SBW), whole(SBW),
                  pl.BlockSpec((S, D), lambda j: (0, jnp.maximum(j - 2, 0)))],
        out_specs=[pl.BlockSpec((None, D, D), lambda j: (j, 0, 0))], out_shape=[_sds((NSH, D, D), BF16)],
        compiler_params=_params(("arbitrary",), 56))[0]


def _wgrad(a, b, nblk, ti, name, out_dtype=BF16, exchange=None, after=()):
    ka, n = a.shape[1], b.shape[1]
    ns = n // nblk

    def body(a_ref, b_ref, o_ref):
        o_ref[...] = _tn(a_ref[...].astype(BF16), b_ref[...].astype(BF16)).astype(out_dtype)

    res = _call(
        body, (a, b), name=name, grid=(nblk, ka // ti),
        in_specs=[pl.BlockSpec((S, ti), lambda j, i: (0, i)), pl.BlockSpec((S, ns), lambda j, i: (0, j))],
        out_specs=[pl.BlockSpec((None, ti, ns), lambda j, i: (j, i, 0))],
        out_shape=[_sds((nblk, ka, ns), out_dtype)],
        compiler_params=_params(("arbitrary", "arbitrary"), 56), exchange=exchange, after=after)
    return res[0] if exchange is None else (res[0][0], res[1])


def _wgrad_branches(p, dyp, o_sb, dys, pm, dyg):
    cols = D // NSH

    def body(p_ref, dyp_ref, o_ref, dys_ref, pm_ref, dyg_ref, gbp_ref, gba_ref, gg_ref):
        gbp_ref[...] = _tn(p_ref[...], dyp_ref[...]).astype(BF16)
        gba_ref[...] = _tn(o_ref[...], dys_ref[...]).astype(BF16)
        gg_ref[...] = _tn(pm_ref[...], dyg_ref[...])

    whole = lambda width: pl.BlockSpec((S, width), lambda j: (0, 0))
    col = lambda width: pl.BlockSpec((S, width), lambda j: (0, j))
    return _call(
        body, (p, dyp, o_sb, dys, pm, dyg), name="wgrad_branches", grid=(NSH,),
        in_specs=[whole(PW), col(cols), whole(SBW), col(cols), col(PG), col(PG)],
        out_specs=[pl.BlockSpec((None, PW, cols), lambda j: (j, 0, 0)),
                   pl.BlockSpec((None, SBW, cols), lambda j: (j, 0, 0)),
                   pl.BlockSpec((None, PG, PG), lambda j: (j, 0, 0))],
        out_shape=[_sds((NSH, PW, cols), BF16), _sds((NSH, SBW, cols), BF16), _sds((NSH, PG, PG), F32)],
        compiler_params=_params(("arbitrary",), 40))


def _place():
    x, y, c = lax.axis_index("x"), lax.axis_index("y"), lax.axis_index("c")
    chips = [(1 - x, y), (x, 1 - y), (1 - x, 1 - y)]
    return x, y, c, chips


def _remote(src, dst, ssem, rsem, dev):
    return pltpu.make_async_remote_copy(src_ref=src, dst_ref=dst, send_sem=ssem, recv_sem=rsem,
                                        device_id=dev, device_id_type=MESH)


def _cast_into_block(ws, me_idx, name):
    steps = 4
    shapes = [(w.shape[0] // steps, w.shape[1]) for w in ws]

    def body(me_ref, *refs):
        for w_ref, o_ref in zip(refs[:len(ws)], refs[len(ws):]):
            o_ref[...] = w_ref[...].astype(BF16)

    return pl.pallas_call(
        body, name=name, out_shape=[_sds((NSH,) + w.shape, BF16) for w in ws],
        grid_spec=pltpu.PrefetchScalarGridSpec(
            num_scalar_prefetch=1, grid=(steps,),
            in_specs=[pl.BlockSpec((r, c), lambda s, me: (s, 0)) for r, c in shapes],
            out_specs=[pl.BlockSpec((None, r, c), lambda s, me: (me[0], s, 0)) for r, c in shapes]),
        compiler_params=_params(("arbitrary",), 32),
    )(me_idx, *ws)


def _ex_gather(bufs):
    n = len(bufs)
    per = 8

    def plan(outs, ssem, rsem, w):
        x, y, c, _ = _place()
        sib, nbr_x, nbr_y = (x, y, 1 - c), (1 - x, y, c), (x, 1 - y, c)
        half = outs[w].shape[1] // 2
        quarter = half // 2
        sem = lambda k: (ssem.at[per * w + k], rsem.at[per * w + k])
        rows = lambda blk, start, size: outs[w].at[blk, pl.ds(start, size)]
        mine = rows(2 * x + y, c * half, half)
        from_x = rows(2 * (1 - x) + y, c * half, half)
        from_y = rows(2 * x + (1 - y), c * half, half)
        diag = 2 * (1 - x) + (1 - y)
        pass_y = rows(2 * (1 - x) + y, c * half, quarter)
        pass_x = rows(2 * x + (1 - y), c * half + quarter, quarter)
        diag_0, diag_1 = rows(diag, c * half, quarter), rows(diag, c * half + quarter, quarter)
        first = [_remote(mine, mine, *sem(0), nbr_x), _remote(mine, mine, *sem(1), nbr_y)]
        arrivals = [
            (_remote(from_x, from_x, *sem(0), nbr_x),
             [_remote(pass_y, pass_y, *sem(2), nbr_y), _remote(from_x, from_x, *sem(4), sib)]),
            (_remote(from_y, from_y, *sem(1), nbr_y),
             [_remote(pass_x, pass_x, *sem(3), nbr_x), _remote(from_y, from_y, *sem(5), sib)]),
            (_remote(diag_0, diag_0, *sem(2), nbr_y), [_remote(diag_0, diag_0, *sem(6), sib)]),
            (_remote(diag_1, diag_1, *sem(3), nbr_x), [_remote(diag_1, diag_1, *sem(7), sib)]),
        ]
        other = (1 - c) * half
        from_sibling = [
            _remote(rows(2 * (1 - x) + y, other, half), rows(2 * (1 - x) + y, other, half), *sem(4), sib),
            _remote(rows(2 * x + (1 - y), other, half), rows(2 * x + (1 - y), other, half), *sem(5), sib),
            _remote(rows(diag, other, quarter), rows(diag, other, quarter), *sem(6), sib),
            _remote(rows(diag, other + quarter, quarter), rows(diag, other + quarter, quarter), *sem(7), sib),
        ]
        return first, arrivals, from_sibling

    def start(ins, outs, ssem, rsem):
        x, y, c, _ = _place()
        for w in range(n):
            half = outs[w].shape[1] // 2
            mine = outs[w].at[2 * x + y, pl.ds(c * half, half)]
            _remote(mine, mine, ssem.at[per * w], rsem.at[per * w], (1 - x, y, c)).start()
            _remote(mine, mine, ssem.at[per * w + 1], rsem.at[per * w + 1], (x, 1 - y, c)).start()

    def finish(ins, outs, ssem, rsem):
        plans = [plan(outs, ssem, rsem, w) for w in range(n)]
        started = []
        for direct in (True, False):
            for first, arrivals, _ in plans:
                for arrived, onward in (arrivals[:2] if direct else arrivals[2:]):
                    arrived.wait_recv()
                    for cp in onward:
                        cp.start()
                    started += onward
        for first, _, from_sibling in plans:
            for cp in from_sibling:
                cp.wait_recv()
            started += first
        for cp in started:
            cp.wait_send()

    return Exchange(bufs, [_sds(b.shape, b.dtype) for b in bufs], {w: w for w in range(n)}, per * n, start, finish)


def _ex_gather_direct(bufs):
    n = len(bufs)

    def copies(outs, ssem, rsem, only_first=False):
        x, y, c, chips = _place()
        me, sib = 2 * x + y, (x, y, 1 - c)
        first, relay, last = [], [], []
        for w in range(n):
            half = outs[w].shape[1] // 2
            mine = outs[w].at[me, pl.ds(c * half, half)]
            for k, (px, py) in enumerate(chips):
                sems = (ssem.at[6 * w + k], rsem.at[6 * w + k])
                sib_sems = (ssem.at[6 * w + 3 + k], rsem.at[6 * w + 3 + k])
                first.append(_remote(mine, mine, *sems, (px, py, c)))
                if only_first:
                    continue
                got = outs[w].at[2 * px + py, pl.ds(c * half, half)]
                relay.append((_remote(got, got, *sems, (px, py, c)), _remote(got, got, *sib_sems, sib)))
                theirs = outs[w].at[2 * px + py, pl.ds((1 - c) * half, half)]
                last.append(_remote(theirs, theirs, *sib_sems, sib))
        return first, relay, last

    def start(ins, outs, ssem, rsem):
        for cp in copies(outs, ssem, rsem, only_first=True)[0]:
            cp.start()

    def finish(ins, outs, ssem, rsem):
        first, relay, last = copies(outs, ssem, rsem)
        for arrived, onward in relay:
            arrived.wait_recv()
            onward.start()
        for cp in last:
            cp.wait_recv()
        for cp in first:
            cp.wait_send()
        for _, onward in relay:
            onward.wait_send()

    return Exchange(bufs, [_sds(b.shape, b.dtype) for b in bufs], {w: w for w in range(n)}, 6 * n, start, finish)


def _simple_exchange(arrays, landing, aliases, make_copies, sibling_only=False):
    def start(ins, outs, ssem, rsem):
        for cp, _ in make_copies(ins, outs, ssem, rsem, False):
            cp.start()

    def finish(ins, outs, ssem, rsem):
        cps = make_copies(ins, outs, ssem, rsem, True)
        for _, landed in cps:
            landed.wait_recv()
        for cp, _ in cps:
            cp.wait_send()

    return Exchange(arrays, landing, aliases, len(arrays) * 3, start, finish, sibling_only)


def _ex_pair_swap(grads):
    def make(ins, outs, ssem, rsem, landing):
        x, y, c, _ = _place()
        cps = [_remote(ins[w].at[:, 1 - c], outs[w], ssem.at[w], rsem.at[w], (x, y, 1 - c))
               for w in range(len(grads))]
        return [(cp, cp) for cp in cps]

    return _simple_exchange(grads, [_sds((NSH,) + g.shape[2:], g.dtype) for g in grads], {}, make, True)


def _ex_relay(bufs):
    def make(ins, outs, ssem, rsem, landing):
        x, y, c, chips = _place()
        sib = (x, y, 1 - c)
        out = []
        for w in range(len(bufs)):
            half = outs[w].shape[1] // 2
            for k, (px, py) in enumerate(chips):
                sems = (ssem.at[3 * w + k], rsem.at[3 * w + k])
                have = outs[w].at[2 * px + py, pl.ds(c * half, half)]
                miss = outs[w].at[2 * px + py, pl.ds((1 - c) * half, half)]
                out.append((_remote(have, have, *sems, sib), _remote(miss, miss, *sems, sib) if landing else None))
        return out

    return _simple_exchange(bufs, [_sds(b.shape, b.dtype) for b in bufs], {w: w for w in range(len(bufs))}, make, True)


def _ex_share(bufs):
    def make(ins, outs, ssem, rsem, landing):
        x, y, c, _ = _place()
        sib = (x, y, 1 - c)
        return [(_remote(outs[w].at[c], outs[w].at[c], ssem.at[w], rsem.at[w], sib),
                 _remote(outs[w].at[1 - c], outs[w].at[1 - c], ssem.at[w], rsem.at[w], sib) if landing else None)
                for w in range(len(bufs))]

    return _simple_exchange(bufs, [_sds(b.shape, b.dtype) for b in bufs], {w: w for w in range(len(bufs))}, make, True)


def _small_copies(slots, ssems, rsems, sending):
    x, y, c, _ = _place()
    out = []
    for m in range(1, 8):
        px, py, pc = x ^ (m >> 2), y ^ ((m >> 1) & 1), c ^ (m & 1)
        slot = slots.at[4 * x + 2 * y + c if sending else 4 * px + 2 * py + pc]
        out.append(_remote(slot, slot, ssems[m - 1], rsems[m - 1], (px, py, pc)))
    return out


def _small_gather_start(slots, name):
    def body(*refs):
        for cp in _small_copies(refs[0], refs[1:8], refs[8:15], True):
            cp.start()
        refs[-1][...] = jnp.zeros_like(refs[-1])

    outs = pl.pallas_call(
        body, name=name,
        out_shape=([pltpu.SemaphoreType.DMA(())] * 14 + [pltpu.HBM(slots.shape, slots.dtype)]
                   + [jax.ShapeDtypeStruct((8, 128), F32)]),
        in_specs=[_HBM], out_specs=[_SEM] * 14 + [_HBM, _VM], input_output_aliases={0: 14},
        compiler_params=pltpu.CompilerParams(has_side_effects=_EFFECT),
    )(*_in_hbm([slots]))
    return outs[:14], outs[14], outs[15]


def _small_gather_wait(sems, slots, after, name):
    def body(*refs):
        for cp in _small_copies(refs[0], refs[1:8], refs[8:15], True):
            cp.wait_send()
        for cp in _small_copies(refs[0], refs[1:8], refs[8:15], False):
            cp.wait_recv()

    return pl.pallas_call(
        body, name=name, out_shape=pltpu.HBM(slots.shape, slots.dtype),
        in_specs=[_HBM] + [_SEM] * 14 + [_ANY] * len(after), out_specs=_HBM, input_output_aliases={0: 0},
        compiler_params=pltpu.CompilerParams(has_side_effects=_EFFECT),
    )(slots, *sems, *after)


def _pair_sum(grads, gots, c_idx, name):
    n = len(grads)

    def body(c_ref, *refs):
        for a_ref, b_ref, o_ref in zip(refs[:n], refs[n:2 * n], refs[2 * n:]):
            o_ref[...] = (a_ref[...].astype(F32) + b_ref[...].astype(F32)).astype(BF16)

    halves = [g.shape[2:] for g in grads]
    return list(pl.pallas_call(
        body, name=name, out_shape=[_sds((NSH,) + h, BF16) for h in halves],
        grid_spec=pltpu.PrefetchScalarGridSpec(
            num_scalar_prefetch=1, grid=(NSH,),
            in_specs=[pl.BlockSpec((None, None) + h, lambda j, c: (j, c[0], 0, 0)) for h in halves]
            + [pl.BlockSpec((None,) + h, lambda j, c: (j, 0, 0)) for h in halves],
            out_specs=[pl.BlockSpec((None,) + h, lambda j, c: (j, 0, 0)) for h in halves]),
        compiler_params=_params(("arbitrary",), 40),
    )(c_idx, *_in_hbm(list(grads) + list(gots))))


def _chip_sum(owns, gots, place, name):
    n = len(owns)

    def body(place_ref, *refs):
        for own_ref, got_ref, o_ref in zip(refs[:n], refs[n:2 * n], refs[2 * n:]):
            acc = own_ref[...].astype(F32)
            for k in range(3):
                acc = acc + got_ref[k].astype(F32)
            o_ref[...] = acc

    shapes = [(o.shape[1] // 2, o.shape[2]) for o in owns]
    return list(pl.pallas_call(
        body, name=name, out_shape=[_sds((2, 2 * r, c), F32) for r, c in shapes],
        grid_spec=pltpu.PrefetchScalarGridSpec(
            num_scalar_prefetch=1, grid=(2,),
            in_specs=[pl.BlockSpec((None, r, c), lambda s, p: (p[0], s, 0)) for r, c in shapes]
            + [pl.BlockSpec((3, r, c), lambda s, p: (0, s, 0)) for r, c in shapes],
            out_specs=[pl.BlockSpec((None, r, c), lambda s, p: (p[1], s, 0)) for r, c in shapes]),
        compiler_params=_params(("arbitrary",), 40),
    )(place, *_in_hbm(list(owns) + list(gots))))


def _adamw_math(w, g, m, v):
    m = B1 * m + (1.0 - B1) * g
    v = B2 * v + (1.0 - B2) * (g * g)
    m_hat = m / (1.0 - B1 ** STEP)
    v_hat = v / (1.0 - B2 ** STEP)
    return -LR * (m_hat / (jnp.sqrt(v_hat) + AEPS) + WD * w), m, v


def _adamw(ws, gs, ms, vs, name, after=()):
    n, steps = len(ws), 4

    def body(*refs):
        ins, outs = refs[:4 * n], refs[4 * n:]
        for i in range(n):
            w_ref, g_ref, m_ref, v_ref = ins[4 * i:4 * i + 4]
            go_ref, d_ref, nm_ref, nv_ref = outs[4 * i:4 * i + 4]
            g = g_ref[...]
            go_ref[...] = g
            d_ref[...], nm_ref[...], nv_ref[...] = _adamw_math(w_ref[...], g, m_ref[...], v_ref[...])

    args, specs, shapes, free = [], [], [], []
    for i, (w, g, m, v) in enumerate(zip(ws, gs, ms, vs)):
        args += [w, g, m, v]
        specs += [pl.BlockSpec((w.shape[0] // steps, w.shape[1]), lambda r: (r, 0))] * 4
        shapes += [_sds(w.shape, F32)] * 4
        free += [4 * i, 4 * i + 2, 4 * i + 3]
    outs = _call(body, args, name=name, grid=(steps,), out_shape=shapes, in_specs=specs, out_specs=specs,
                 compiler_params=_params(("arbitrary",), 48), free=tuple(free), after=after)
    return [outs[4 * i:4 * i + 4] for i in range(n)]


def _small_update(gathered, w, m, v, entries):
    rows = w.shape[0]

    def body(ga_ref, w_ref, m_ref, v_ref, *out_refs):
        for j, (first, n) in enumerate(entries):
            mine = slice(first, first + n)
            g = ga_ref[mine, :]
            for dev in range(1, 8):
                g = g + ga_ref[dev * rows + first:dev * rows + first + n, :]
            results = (g,) + _adamw_math(w_ref[mine, :], g, m_ref[mine, :], v_ref[mine, :])
            for i, res in enumerate(results):
                out_refs[i * len(entries) + j][...] = res

    outs = pl.pallas_call(
        body, name="small_update",
        out_shape=[jax.ShapeDtypeStruct((n, 128), F32) for _ in range(4) for _, n in entries],
        in_specs=[_VM] * 4, out_specs=[_VM] * (4 * len(entries)),
    )(gathered, w, m, v)
    return [outs[i * len(entries):(i + 1) * len(entries)] for i in range(4)]


SMALL = ("ffn1_norm", "mix_norm", "ffn2_norm", "final_norm", "pool_scale", "loss", "pool_w_group")
BIG = ("ffn1_w_gate_up", "ffn1_w_down", "w_in", "w_branch_pool", "w_branch_attn", "w_out",
       "ffn2_w_gate_up", "ffn2_w_down")
ORDER = ("ffn1_norm", "ffn1_w_gate_up", "ffn1_w_down", "mix_norm", "w_in", "pool_w_group", "pool_scale",
         "w_branch_pool", "w_branch_attn", "w_out", "ffn2_norm", "ffn2_w_gate_up", "ffn2_w_down", "final_norm")
SMALL_ROWS = 560


def _pack_small(t):
    parts = []
    for k in SMALL:
        rows = t[k].reshape(-1, 128) if k in t else jnp.zeros((1, 128), F32)
        parts.append(jnp.pad(rows, ((0, -rows.shape[0] % 8), (0, 0))))
    packed = jnp.concatenate(parts, axis=0)
    assert packed.shape == (SMALL_ROWS, 128), packed.shape
    return packed


def _small_entries(like):
    out, at = [], 0
    for k in SMALL:
        n = like[k].size // 128 if k in like else 1
        out.append((at, n))
        at += n + (-n % 8)
    return out


def _halves(g):
    return g.reshape(NSH, 2, g.shape[1] // 2, g.shape[2])


def kernel(x, ffn1_norm, ffn1_w_gate_up, ffn1_w_down, mix_norm, w_in, pool_w_group, pool_scale, w_branch_pool, w_branch_attn, w_out, ffn2_norm, ffn2_w_gate_up, ffn2_w_down, final_norm, loss_target, m_ffn1_norm, m_ffn1_w_gate_up, m_ffn1_w_down, m_mix_norm, m_w_in, m_pool_w_group, m_pool_scale, m_w_branch_pool, m_w_branch_attn, m_w_out, m_ffn2_norm, m_ffn2_w_gate_up, m_ffn2_w_down, m_final_norm, v_ffn1_norm, v_ffn1_w_gate_up, v_ffn1_w_down, v_mix_norm, v_w_in, v_pool_w_group, v_pool_scale, v_w_branch_pool, v_w_branch_attn, v_w_out, v_ffn2_norm, v_ffn2_w_gate_up, v_ffn2_w_down, v_final_norm):
    wts = dict(ffn1_norm=ffn1_norm, ffn1_w_gate_up=ffn1_w_gate_up, ffn1_w_down=ffn1_w_down, mix_norm=mix_norm,
               w_in=w_in, pool_w_group=pool_w_group, pool_scale=pool_scale, w_branch_pool=w_branch_pool,
               w_branch_attn=w_branch_attn, w_out=w_out, ffn2_norm=ffn2_norm, ffn2_w_gate_up=ffn2_w_gate_up,
               ffn2_w_down=ffn2_w_down, final_norm=final_norm)
    mom = dict(ffn1_norm=m_ffn1_norm, ffn1_w_gate_up=m_ffn1_w_gate_up, ffn1_w_down=m_ffn1_w_down,
               mix_norm=m_mix_norm, w_in=m_w_in, pool_w_group=m_pool_w_group, pool_scale=m_pool_scale,
               w_branch_pool=m_w_branch_pool, w_branch_attn=m_w_branch_attn, w_out=m_w_out,
               ffn2_norm=m_ffn2_norm, ffn2_w_gate_up=m_ffn2_w_gate_up, ffn2_w_down=m_ffn2_w_down,
               final_norm=m_final_norm)
    var = dict(ffn1_norm=v_ffn1_norm, ffn1_w_gate_up=v_ffn1_w_gate_up, ffn1_w_down=v_ffn1_w_down,
               mix_norm=v_mix_norm, w_in=v_w_in, pool_w_group=v_pool_w_group, pool_scale=v_pool_scale,
               w_branch_pool=v_w_branch_pool, w_branch_attn=v_w_branch_attn, w_out=v_w_out,
               ffn2_norm=v_ffn2_norm, ffn2_w_gate_up=v_ffn2_w_gate_up, ffn2_w_down=v_ffn2_w_down,
               final_norm=v_final_norm)

    c_idx = lax.axis_index("c").astype(jnp.int32).reshape(1)
    me_idx = (2 * lax.axis_index("x") + lax.axis_index("y")).astype(jnp.int32).reshape(1)
    place = jnp.concatenate([me_idx, c_idx])
    x0, tgt = x[0], loss_target[0]
    wgrp = pool_w_group[0].astype(BF16)
    g1, gm, g2, gf = ffn1_norm, mix_norm, ffn2_norm, final_norm.reshape(1, D)
    grad, delta, new_m, new_v = {}, {}, {}, {}

    def pair_sums(keys, parts, got):
        return _pair_sum(parts, got, c_idx, "pair_sum_" + keys[0])

    def chip_sums(keys, chip_parts, owned):
        return _chip_sum(chip_parts, owned, place, "chip_sum_" + keys[0])

    def adamw(keys, after=()):
        outs = _adamw([wts[k][0] for k in keys], [grad[k][0] for k in keys], [mom[k][0] for k in keys],
                      [var[k][0] for k in keys], "adamw_" + keys[0], after=after)
        for k, res in zip(keys, outs):
            grad[k], delta[k], new_m[k], new_v[k] = (o.reshape(wts[k].shape) for o in res)

    first, late = ("ffn1_w_gate_up", "ffn1_w_down"), ("w_branch_pool", "w_branch_attn", "w_out",
                                                       "ffn2_w_gate_up", "ffn2_w_down")
    own = {}
    for group in (first, ("w_in",), late):
        own.update(zip(group, _cast_into_block([wts[k][0] for k in group], me_idx, "cast_" + group[0])))
    full = dict(zip(first, _exchange_alone(_ex_gather([own[k] for k in first]), "gather_ffn1")))
    wgu1, wd1 = full["ffn1_w_gate_up"], full["ffn1_w_down"].reshape(DFF, D)
    (h1, n1, gu1, a1), (win,) = _ffn_fwd(x0, g1, wgu1, wd1, "ffn1_fwd", exchange=_ex_gather_direct([own["w_in"]]))
    sems_l, thru_l, token_l = _gather_start([own[k_] for k_ in late], [h1], "gather_late_start")
    u, xp, q, k, v, gp, gs = _mix_in(h1, gm, win, after=(token_l,))
    o_sb, ctot = _attn_fwd(q, k, v)
    arrived = _gather_wait(sems_l, thru_l, [o_sb], "gather_late_wait")
    wbp, wba, wout = _exchange_alone(_ex_relay(arrived[:3]), "relay_mix")
    wout = wout.reshape(D, D)
    (h2, pm, p, yp, ys, mm), (wgu2, wd2) = _mix_out(h1, xp, o_sb, gp, gs, wgrp, pool_scale, wbp, wba, wout,
                                                    exchange=_ex_relay(arrived[3:]))
    wd2 = wd2.reshape(DFF, D)
    dh3, loss_row, d_gf, n3, gu3, a3 = _ffn_fwd(h2, g2, wgu2, wd2, "ffn2_fwd", head=(tgt, gf))

    def grad_gate_up(n, dgu, name, exchange=None):
        res = _wgrad(n, dgu, NSH, D, name, exchange=exchange)
        return [_halves(res)] if exchange is None else ([_halves(res[0])], res[1])

    def grad_down(a, dh, name, exchange=None):
        res = _wgrad(a, dh, 1, FFS, name, exchange=exchange)
        halves = lambda g: [_halves(g.reshape(NSH, DFF // NSH, D))]
        return halves(res) if exchange is None else (halves(res[0]), res[1])

    k_gu2, k_d2, k_gu1, k_d1, k_in = (("ffn2_w_gate_up",), ("ffn2_w_down",), ("ffn1_w_gate_up",),
                                      ("ffn1_w_down",), ("w_in",))
    dh2, dgu3, d_g2 = _ffn_bwd(dh3, h2, g2, gu3, wgu2, wd2, "ffn2_bwd")
    pa = grad_gate_up(n3, dgu3, "wgrad_gu2") + grad_down(a3, dh3, "wgrad_d2")
    (dlg, dyp, dys, do_sb, dyg, dxp, d_scale), got_a = _mix_bwd_out(
        dh2, gp, gs, yp, ys, pm, wgrp, pool_scale, wbp, wba, wout, exchange=_ex_pair_swap(pa))
    chip_a = pair_sums(k_gu2 + k_d2, pa, got_a)
    kb = ("w_out", "w_branch_pool", "w_branch_attn")
    g_bp, g_ba, d_group = _wgrad_branches(p, dyp, o_sb, dys, pm, dyg)
    pb = [_halves(_wgrad(mm, dh2, 1, D, "wgrad_out").reshape(NSH, D // NSH, D)), _halves(g_bp), _halves(g_ba)]
    k_a, k_in = k_gu2 + k_d2, k_in + kb
    sems_a, thru_a, token_a = _scatter_start(chip_a, "scatter_a_start")
    dq, dk, dv = _attn_bwd(q, k, v, do_sb, ctot, after=(token_a,))
    chip_a, owned_a = _scatter_wait(sems_a, thru_a, [dq], "scatter_a_wait")
    halves_a = chip_sums(k_a, chip_a, owned_a)
    dproj = (dxp, dq, dk, dv, dlg)
    (dh1, d_gm), both_a = _mix_bwd_in(dh2, h1, gm, dproj, win, exchange=_ex_share(halves_a))
    for i, k_ in enumerate(k_a):
        grad[k_] = both_a[i].reshape(wts[k_].shape)

    p_in = [_halves(_wgrad_in(u, dproj))] + pb
    p_d1, got_in = grad_down(a1, dh1, "wgrad_d1", exchange=_ex_pair_swap(p_in))
    sems_in, thru_in, token_in = _scatter_start(pair_sums(k_in, p_in, got_in), "scatter_in_start")
    dgu1, got_d1 = _ffn_bwd_act(dh1, gu1, wd1, "ffn1_bwd_act", exchange=_ex_pair_swap(p_d1), after=(token_in,))
    sems_d1, thru_d1, token_d1 = _scatter_start(pair_sums(k_d1, p_d1, got_d1), "scatter_d1_start")
    p_gu1 = [_halves(_wgrad(n1, dgu1, NSH, D, "wgrad_gu1", after=(token_in, token_d1)))]
    sems_w, thru_w, token_w = _swap_start(p_gu1, "swap_gu1_start")
    chip_in, owned_in = _scatter_wait(sems_in, thru_in, [token_w], "scatter_in_wait")
    chip_d1, owned_d1 = _scatter_wait(sems_d1, thru_d1, [token_w], "scatter_d1_wait")
    halves_in = chip_sums(k_in, chip_in, owned_in)
    p_gu1, got_gu1 = _swap_wait(sems_w, thru_w, halves_in, "swap_gu1_wait")
    sems, thru, token = _scatter_start(pair_sums(k_gu1, p_gu1, got_gu1), "scatter_gu1_start")
    sems_h, thru_h, token_h = _share_start(halves_in, [token], "share_in_start")
    adamw(k_a, after=(token_h,))
    landed = _share_wait(sems_h, thru_h, [delta[k_a[0]]], "share_in_wait")
    for i, k_ in enumerate(k_in):
        grad[k_] = landed[i].reshape(wts[k_].shape)
    adamw(k_in)
    dx, d_g1 = _ffn_bwd_in(dh1, x0, g1, dgu1, wgu1, "ffn1_bwd_in", after=(token,))
    small_g = dict(ffn1_norm=d_g1, mix_norm=d_gm, ffn2_norm=d_g2, final_norm=d_gf, pool_scale=d_scale,
                   pool_w_group=d_group, loss=loss_row)
    dev = 4 * lax.axis_index("x") + 2 * lax.axis_index("y") + lax.axis_index("c")
    slots = lax.dynamic_update_slice(jnp.zeros((8, SMALL_ROWS, 128), F32), _pack_small(small_g)[None], (dev, 0, 0))
    sems_s, slots, token_s = _small_gather_start(slots, "small_gather_start")

    chip_gu1, owned_gu1 = _scatter_wait(sems, thru, [dx] + [delta[k_] for k_ in k_a + k_in], "scatter_gu1_wait")
    halves_last = chip_sums(k_d1 + k_gu1, chip_d1 + chip_gu1, owned_d1 + owned_gu1)
    both = _exchange_alone(_ex_share(halves_last), "share_last",
                           after=(token_s,))
    grad["ffn1_w_down"] = both[0].reshape(ffn1_w_down.shape)
    grad["ffn1_w_gate_up"] = both[1].reshape(ffn1_w_gate_up.shape)
    adamw(k_d1 + k_gu1, after=(token_s,))
    gathered = _small_gather_wait(sems_s, slots, [delta[k_] for k_ in k_d1 + k_gu1], "small_gather_wait")
    gathered = gathered.reshape(8 * SMALL_ROWS, 128)
    results = _small_update(gathered, _pack_small(wts), _pack_small(mom), _pack_small(var), _small_entries(wts))
    for dst, entries in zip((grad, delta, new_m, new_v), results):
        for k_, rows in zip(SMALL, entries):
            if k_ in wts:
                dst[k_] = rows.reshape(wts[k_].shape)
            elif dst is grad:
                loss = rows[0, 0]
    return (loss, dx[None], *[grad[k_] for k_ in ORDER], *[delta[k_] for k_ in ORDER],
            *[new_m[k_] for k_ in ORDER], *[new_v[k_] for k_ in ORDER])
```

```python
import dataclasses
import functools

import jax
import jax.numpy as jnp
from jax import lax
from jax.experimental import pallas as pl
from jax.experimental.pallas import tpu as pltpu

F32 = jnp.float32
BF16 = jnp.bfloat16

S = 2048
D = 1024
DFF = 2816
FFS = 2 * DFF // 4
NSH = 4
PW = 512
PG = 128
POOL_WINDOWS = (2, 4, 8, 16)
HALO = 16
SBW = 512
DH = 64
EPS = 1e-6
SCALE = 0.125
LOG2E = 1.4426950408889634
TA = 256
QB = 2
MIB = 1024 * 1024

LR, B1, B2, AEPS, WD, STEP = 0.001, 0.9, 0.999, 1e-08, 0.01, 10

_VM = pl.BlockSpec(memory_space=pltpu.VMEM)
_ANY = pl.BlockSpec(memory_space=pl.ANY)
MESH = pl.DeviceIdType.MESH
SIBLING_PAIR_ID = 1


def _nn(a, b):
    return jnp.dot(a, b, preferred_element_type=F32)


def _nt(a, b):
    return lax.dot_general(a, b, (((1,), (1,)), ((), ())), preferred_element_type=F32)


def _tn(a, b):
    return lax.dot_general(a, b, (((0,), (0,)), ((), ())), preferred_element_type=F32)


def _params(sem, vmem_mib):
    return pltpu.CompilerParams(dimension_semantics=sem, vmem_limit_bytes=vmem_mib * MIB)


def _rows(tm, width):
    return pl.BlockSpec((tm, width), lambda i: (i, 0))


def _fixed(shape):
    return pl.BlockSpec(shape, lambda *_: (0,) * len(shape))


def _sds(shape, dtype):
    return pltpu.HBM(shape, dtype)


def _in_hbm(args):
    return [pltpu.with_memory_space_constraint(a, pltpu.HBM) for a in args]


def _stage(pairs):
    pieces = 4

    def copy_all(sems):
        copies = []
        for src, dst in pairs:
            step = src.shape[0] // pieces
            for p in range(pieces):
                part = pl.ds(p * step, step)
                if len(dst.shape) == len(src.shape):
                    piece = (src.at[part], dst.at[part])
                else:
                    piece = (src.at[p], dst.at[:, pl.ds(p * src.shape[2], src.shape[2])])
                copies.append(pltpu.make_async_copy(*piece, sems.at[len(copies)]))
        for c in copies:
            c.start()
        for c in copies:
            c.wait()

    @pl.when(pl.program_id(0) == 0)
    def _():
        pl.run_scoped(copy_all, pltpu.SemaphoreType.DMA((pieces * len(pairs),)))


def _vmem_like(*arrays):
    return [pltpu.VMEM(a.shape, a.dtype) for a in arrays]


def _vmem_wide(w):
    return pltpu.VMEM((w.shape[1], w.shape[0] * w.shape[2]), w.dtype)


FF_CHUNKS = ((0, 1536), (1536, DFF - 1536))


class Exchange:
    def __init__(self, arrays, landing, aliases, n_sems, start, finish, sibling_only=False):
        self.arrays, self.landing, self.aliases, self.n_sems = list(arrays), list(landing), dict(aliases), n_sems
        self.start, self.finish = start, finish
        self.sibling_only = sibling_only

    def enter(self):
        if self.sibling_only:
            barrier = pltpu.get_barrier_semaphore()
            sibling = (lax.axis_index("x"), lax.axis_index("y"), 1 - lax.axis_index("c"))
            pl.semaphore_signal(barrier, inc=1, device_id=sibling, device_id_type=MESH)
            pl.semaphore_wait(barrier, 1)

    def params(self, compiler_params=None):
        kw = dict(collective_id=SIBLING_PAIR_ID) if self.sibling_only else {}
        if compiler_params is None:
            return pltpu.CompilerParams(**kw)
        return dataclasses.replace(compiler_params, **kw)


def _call(body, args, *, name, grid, in_specs, out_specs, out_shape, scratch_shapes=(), compiler_params=None,
          exchange=None, free=(), after=()):
    args = [a if i in free else pltpu.with_memory_space_constraint(a, pltpu.HBM) for i, a in enumerate(args)]
    if exchange is None:
        n_in = len(in_specs)

        def plain(*refs):
            body(*refs[:n_in], *refs[n_in + len(after):])

        return pl.pallas_call(plain, name=name, grid=grid, in_specs=list(in_specs) + [_ANY] * len(after),
                              out_specs=out_specs, out_shape=out_shape, scratch_shapes=list(scratch_shapes),
                              compiler_params=compiler_params)(*args, *after)
    ex = exchange
    n_in, n_out, n_scr = len(in_specs), len(out_specs), len(scratch_shapes)
    na, nl = len(ex.arrays), len(ex.landing)

    def hosted(*refs):
        at = [0]

        def take(n):
            at[0] += n
            return refs[at[0] - n:at[0]]

        k_in, _, e_in, k_out, e_out, k_scr = take(n_in), take(len(after)), take(na), take(n_out), take(nl), take(n_scr)
        ssem, rsem = take(2)
        ids = [pl.program_id(a) for a in range(len(grid))]
        first = functools.reduce(jnp.logical_and, [i == 0 for i in ids])
        last = functools.reduce(jnp.logical_and, [i == g - 1 for i, g in zip(ids, grid)])

        @pl.when(first)
        def _():
            ex.enter()
            ex.start(e_in, e_out, ssem, rsem)

        body(*k_in, *k_out, *k_scr)

        @pl.when(last)
        def _():
            ex.finish(e_in, e_out, ssem, rsem)

    outs = pl.pallas_call(
        hosted, name=name, grid=grid,
        in_specs=list(in_specs) + [_ANY] * (len(after) + na), out_specs=list(out_specs) + [_ANY] * nl,
        out_shape=list(out_shape) + ex.landing,
        scratch_shapes=list(scratch_shapes) + [pltpu.SemaphoreType.DMA((ex.n_sems,))] * 2,
        input_output_aliases={n_in + len(after) + i: n_out + j for i, j in ex.aliases.items()},
        compiler_params=ex.params(compiler_params),
    )(*args, *after, *_in_hbm(ex.arrays))
    return outs[:n_out], outs[n_out:]


def _exchange_alone(ex, name, after=()):
    na, nl = len(ex.arrays), len(ex.landing)

    def body(*refs):
        outs = refs[na + len(after):na + len(after) + nl]
        ex.enter()
        ex.start(refs[:na], outs, refs[-2], refs[-1])
        ex.finish(refs[:na], outs, refs[-2], refs[-1])

    return pl.pallas_call(
        body, name=name, in_specs=[_ANY] * (na + len(after)), out_specs=[_ANY] * nl,
        out_shape=ex.landing, scratch_shapes=[pltpu.SemaphoreType.DMA((ex.n_sems,))] * 2,
        input_output_aliases=ex.aliases, compiler_params=ex.params(),
    )(*_in_hbm(ex.arrays), *after)


_HBM = pl.BlockSpec(memory_space=pltpu.HBM)
_SEM = pl.BlockSpec(memory_space=pltpu.SEMAPHORE)
_EFFECT = pltpu.SideEffectType.DATAFLOW_SIDE_EFFECTING


def _scatter_copies(srcs, lands, ssems, rsems):
    x, y, c, chips = _place()
    return [_remote(srcs[w].at[2 * px + py], lands[w].at[k], ssems[3 * w + k], rsems[3 * w + k], (px, py, c))
            for w in range(len(srcs)) for k, (px, py) in enumerate(chips)]


def _scatter_start(parts, name):
    parts = list(parts)
    n, ncp = len(parts), 3 * len(parts)
    lands = [lax.empty((3,) + p.shape[1:], p.dtype) for p in parts]

    def body(*refs):
        srcs, land_refs = refs[:n], refs[n:2 * n]
        ssems, rsems = refs[2 * n:2 * n + ncp], refs[2 * n + ncp:2 * n + 2 * ncp]
        for cp in _scatter_copies(srcs, land_refs, ssems, rsems):
            cp.start()
        token = refs[-1]
        token[...] = jnp.zeros_like(token)

    outs = pl.pallas_call(
        body, name=name,
        out_shape=([pltpu.SemaphoreType.DMA(())] * (2 * ncp) + [pltpu.HBM(a.shape, a.dtype) for a in parts + lands]
                   + [jax.ShapeDtypeStruct((8, 128), F32)]),
        in_specs=[_HBM] * (2 * n), out_specs=[_SEM] * (2 * ncp) + [_HBM] * (2 * n) + [_VM],
        input_output_aliases={i: 2 * ncp + i for i in range(2 * n)},
        compiler_params=pltpu.CompilerParams(has_side_effects=_EFFECT),
    )(*_in_hbm(parts), *_in_hbm(lands))
    sems, thru, token = outs[:2 * ncp], outs[2 * ncp:2 * ncp + 2 * n], outs[-1]
    return sems, thru, token


def _scatter_wait(sems, thru, after, name):
    n = len(thru) // 2
    ncp = 3 * n

    def body(*refs):
        srcs, land_refs = refs[:n], refs[n:2 * n]
        ssems, rsems = refs[2 * n:2 * n + ncp], refs[2 * n + ncp:2 * n + 2 * ncp]
        for cp in _scatter_copies(srcs, land_refs, ssems, rsems):
            cp.wait_send()
            cp.wait_recv()

    outs = pl.pallas_call(
        body, name=name, out_shape=[pltpu.HBM(a.shape, a.dtype) for a in thru],
        in_specs=[_HBM] * (2 * n) + [_SEM] * (2 * ncp) + [_ANY] * len(after), out_specs=[_HBM] * (2 * n),
        input_output_aliases={i: i for i in range(2 * n)},
        compiler_params=pltpu.CompilerParams(has_side_effects=_EFFECT),
    )(*thru, *sems, *after)
    return outs[:n], outs[n:]


def _swap_copies(srcs, lands, ssems, rsems):
    x, y, c, _ = _place()
    return [_remote(srcs[w].at[:, 1 - c], lands[w], ssems[w], rsems[w], (x, y, 1 - c)) for w in range(len(srcs))]


def _swap_start(grads, name):
    grads = list(grads)
    n = len(grads)
    lands = [lax.empty((NSH,) + g.shape[2:], g.dtype) for g in grads]

    def body(*refs):
        barrier = pltpu.get_barrier_semaphore()
        sibling = (lax.axis_index("x"), lax.axis_index("y"), 1 - lax.axis_index("c"))
        pl.semaphore_signal(barrier, inc=1, device_id=sibling, device_id_type=MESH)
        pl.semaphore_wait(barrier, 1)
        for cp in _swap_copies(refs[:n], refs[n:2 * n], refs[2 * n:3 * n], refs[3 * n:4 * n]):
            cp.start()
        refs[-1][...] = jnp.zeros_like(refs[-1])

    outs = pl.pallas_call(
        body, name=name,
        out_shape=([pltpu.SemaphoreType.DMA(())] * (2 * n) + [pltpu.HBM(a.shape, a.dtype) for a in grads + lands]
                   + [jax.ShapeDtypeStruct((8, 128), F32)]),
        in_specs=[_HBM] * (2 * n), out_specs=[_SEM] * (2 * n) + [_HBM] * (2 * n) + [_VM],
        input_output_aliases={i: 2 * n + i for i in range(2 * n)},
        compiler_params=pltpu.CompilerParams(has_side_effects=_EFFECT, collective_id=SIBLING_PAIR_ID),
    )(*_in_hbm(grads), *_in_hbm(lands))
    return outs[:2 * n], outs[2 * n:4 * n], outs[-1]


def _swap_wait(sems, thru, after, name):
    n = len(thru) // 2

    def body(*refs):
        for cp in _swap_copies(refs[:n], refs[n:2 * n], refs[2 * n:3 * n], refs[3 * n:4 * n]):
            cp.wait_send()
            cp.wait_recv()

    outs = pl.pallas_call(
        body, name=name, out_shape=[pltpu.HBM(a.shape, a.dtype) for a in thru],
        in_specs=[_HBM] * (2 * n) + [_SEM] * (2 * n) + [_ANY] * len(after), out_specs=[_HBM] * (2 * n),
        input_output_aliases={i: i for i in range(2 * n)},
        compiler_params=pltpu.CompilerParams(has_side_effects=_EFFECT),
    )(*thru, *sems, *after)
    return outs[:n], outs[n:]


def _share_copies(bufs, ssems, rsems, sending):
    x, y, c, _ = _place()
    out = []
    for w, ref in enumerate(bufs):
        slot = ref.at[c if sending else 1 - c]
        out.append(_remote(slot, slot, ssems[w], rsems[w], (x, y, 1 - c)))
    return out


def _share_start(bufs, after, name):
    bufs = list(bufs)
    n = len(bufs)

    def body(*refs):
        barrier = pltpu.get_barrier_semaphore()
        sibling = (lax.axis_index("x"), lax.axis_index("y"), 1 - lax.axis_index("c"))
        pl.semaphore_signal(barrier, inc=1, device_id=sibling, device_id_type=MESH)
        pl.semaphore_wait(barrier, 1)
        at = n + len(after)
        for cp in _share_copies(refs[:n], refs[at:at + n], refs[at + n:at + 2 * n], True):
            cp.start()
        refs[-1][...] = jnp.zeros_like(refs[-1])

    outs = pl.pallas_call(
        body, name=name,
        out_shape=([pltpu.SemaphoreType.DMA(())] * (2 * n) + [pltpu.HBM(a.shape, a.dtype) for a in bufs]
                   + [jax.ShapeDtypeStruct((8, 128), F32)]),
        in_specs=[_HBM] * n + [_ANY] * len(after), out_specs=[_SEM] * (2 * n) + [_HBM] * n + [_VM],
        input_output_aliases={i: 2 * n + i for i in range(n)},
        compiler_params=pltpu.CompilerParams(has_side_effects=_EFFECT, collective_id=SIBLING_PAIR_ID),
    )(*_in_hbm(bufs), *after)
    return outs[:2 * n], outs[2 * n:3 * n], outs[-1]


def _share_wait(sems, thru, after, name):
    n = len(thru)

    def body(*refs):
        for cp in _share_copies(refs[:n], refs[n:2 * n], refs[2 * n:3 * n], True):
            cp.wait_send()
        for cp in _share_copies(refs[:n], refs[n:2 * n], refs[2 * n:3 * n], False):
            cp.wait_recv()

    return pl.pallas_call(
        body, name=name, out_shape=[pltpu.HBM(a.shape, a.dtype) for a in thru],
        in_specs=[_HBM] * n + [_SEM] * (2 * n) + [_ANY] * len(after), out_specs=[_HBM] * n,
        input_output_aliases={i: i for i in range(n)},
        compiler_params=pltpu.CompilerParams(has_side_effects=_EFFECT),
    )(*thru, *sems, *after)


def _gather_copies(bufs, ssems, rsems, sending):
    x, y, c, chips = _place()
    out = []
    for w, ref in enumerate(bufs):
        half = ref.shape[1] // 2
        for k, (px, py) in enumerate(chips):
            rows = ref.at[2 * x + y if sending else 2 * px + py, pl.ds(c * half, half)]
            out.append(_remote(rows, rows, ssems[3 * w + k], rsems[3 * w + k], (px, py, c)))
    return out


def _gather_start(bufs, after, name):
    n, ncp = len(bufs), 3 * len(bufs)

    def body(*refs):
        ssems, rsems = refs[n + len(after):n + len(after) + ncp], refs[n + len(after) + ncp:n + len(after) + 2 * ncp]
        for cp in _gather_copies(refs[:n], ssems, rsems, True):
            cp.start()
        token = refs[-1]
        token[...] = jnp.zeros_like(token)

    outs = pl.pallas_call(
        body, name=name,
        out_shape=([pltpu.SemaphoreType.DMA(())] * (2 * ncp) + [pltpu.HBM(a.shape, a.dtype) for a in bufs]
                   + [jax.ShapeDtypeStruct((8, 128), F32)]),
        in_specs=[_HBM] * n + [_ANY] * len(after), out_specs=[_SEM] * (2 * ncp) + [_HBM] * n + [_VM],
        input_output_aliases={i: 2 * ncp + i for i in range(n)},
        compiler_params=pltpu.CompilerParams(has_side_effects=_EFFECT),
    )(*_in_hbm(bufs), *after)
    return outs[:2 * ncp], outs[2 * ncp:2 * ncp + n], outs[-1]


def _gather_wait(sems, thru, after, name):
    n = len(thru)
    ncp = 3 * n

    def body(*refs):
        ssems, rsems = refs[n:n + ncp], refs[n + ncp:n + 2 * ncp]
        for cp in _gather_copies(refs[:n], ssems, rsems, True):
            cp.wait_send()
        for cp in _gather_copies(refs[:n], ssems, rsems, False):
            cp.wait_recv()

    return pl.pallas_call(
        body, name=name, out_shape=[pltpu.HBM(a.shape, a.dtype) for a in thru],
        in_specs=[_HBM] * n + [_SEM] * (2 * ncp) + [_ANY] * len(after), out_specs=[_HBM] * n,
        input_output_aliases={i: i for i in range(n)},
        compiler_params=pltpu.CompilerParams(has_side_effects=_EFFECT),
    )(*thru, *sems, *after)


def _rms(x):
    r = lax.rsqrt(jnp.mean(x * x, axis=-1, keepdims=True) + EPS)
    return r, x * r


def _rms_bwd(dn, xr, r, gain):
    dng = dn * gain
    dx = r * (dng - xr * jnp.mean(dng * xr, axis=-1, keepdims=True))
    return dx, jnp.sum(dn * xr, axis=0, keepdims=True)


def _ffn_fwd(x, gain, wgu, wd, name, exchange=None, head=None):
    tm = 256

    def body(x_ref, g_ref, wgu_hbm, wd_hbm, *rest):
        if head is None:
            h_ref, n_ref, gu_ref, a_ref, wgu_ref, wd_ref = rest
        else:
            t_ref, gf_ref, h_ref, loss_ref, dgf_ref, n_ref, gu_ref, a_ref, wgu_ref, wd_ref = rest
        _stage([(wgu_hbm, wgu_ref), (wd_hbm, wd_ref)])
        x = x_ref[...]
        _, xr = _rms(x)
        n = (xr * g_ref[...]).astype(BF16)
        n_ref[...] = n
        acc = jnp.zeros((tm, D), F32)
        for c0, cn in FF_CHUNKS:
            g = _nn(n, wgu_ref[:, c0:c0 + cn])
            u = _nn(n, wgu_ref[:, DFF + c0:DFF + c0 + cn])
            gu_ref[:, c0:c0 + cn] = g.astype(BF16)
            gu_ref[:, DFF + c0:DFF + c0 + cn] = u.astype(BF16)
            half_act = (0.5 * (g * jax.nn.sigmoid(g) * u)).astype(BF16)
            a_ref[:, c0:c0 + cn] = half_act
            acc = acc + _nn(half_act, wd_ref[c0:c0 + cn, :])
        h = x + acc
        if head is None:
            h_ref[...] = h
            return
        gf = gf_ref[...]
        r, hr = _rms(h)
        err = hr * gf - t_ref[...]
        dh, dgain = _rms_bwd(err * (1.0 / D), hr, r, gf)
        h_ref[...] = dh

        @pl.when(pl.program_id(0) == 0)
        def _():
            dgf_ref[...] = jnp.zeros_like(dgf_ref)
            loss_ref[...] = jnp.zeros_like(loss_ref)

        dgf_ref[...] += dgain
        loss_ref[...] += jnp.full((1, 128), (0.5 / D) * jnp.sum(err * err), F32)

    saved_specs = [_rows(tm, D), _rows(tm, 4 * FFS), _rows(tm, DFF)]
    saved_shapes = [_sds((S, D), BF16), _sds((S, 4 * FFS), BF16), _sds((S, DFF), BF16)]
    if head is None:
        return _call(
            body, (x, gain, wgu, wd), name=name, grid=(S // tm,),
            in_specs=[_rows(tm, D), _fixed((1, D)), _ANY, _ANY],
            out_specs=[_rows(tm, D)] + saved_specs, out_shape=[_sds((S, D), F32)] + saved_shapes,
            scratch_shapes=[_vmem_wide(wgu)] + _vmem_like(wd),
            compiler_params=_params(("arbitrary",), 56), exchange=exchange)
    return _call(
        body, (x, gain, wgu, wd, *head), name=name, grid=(S // tm,),
        in_specs=[_rows(tm, D), _fixed((1, D)), _ANY, _ANY, _rows(tm, D), _fixed((1, D))],
        out_specs=[_rows(tm, D), _fixed((1, 128)), _fixed((1, D))] + saved_specs,
        out_shape=[_sds((S, D), F32), _sds((1, 128), F32), _sds((1, D), F32)] + saved_shapes,
        scratch_shapes=[_vmem_wide(wgu)] + _vmem_like(wd),
        compiler_params=_params(("arbitrary",), 56), exchange=exchange, free=(4, 5))


def _ffn_bwd(dh, x, gain, gu, wgu, wd, name):
    tm = 256

    def body(dh_ref, x_ref, g_ref, gu_ref, wgu_hbm, wd_hbm, dx_ref, dgu_ref, dg_ref, wgu_ref, wd_ref):
        _stage([(wgu_hbm, wgu_ref), (wd_hbm, wd_ref)])
        dh = dh_ref[...]
        dhb = dh.astype(BF16)
        dn = jnp.zeros((tm, D), F32)
        for c0, cn in FF_CHUNKS:
            g = gu_ref[:, c0:c0 + cn].astype(F32)
            u = gu_ref[:, DFF + c0:DFF + c0 + cn].astype(F32)
            da = 0.5 * _nt(dhb, wd_ref[c0:c0 + cn, :])
            sg = jax.nn.sigmoid(g)
            dgb = (da * u * (sg * (1.0 + g * (1.0 - sg)))).astype(BF16)
            dub = (da * (g * sg)).astype(BF16)
            dgu_ref[:, c0:c0 + cn] = dgb
            dgu_ref[:, DFF + c0:DFF + c0 + cn] = dub
            dn = dn + _nt(dgb, wgu_ref[:, c0:c0 + cn]) + _nt(dub, wgu_ref[:, DFF + c0:DFF + c0 + cn])
        r, xr = _rms(x_ref[...])
        dx, dgain = _rms_bwd(dn, xr, r, g_ref[...])
        dx_ref[...] = dh + dx

        @pl.when(pl.program_id(0) == 0)
        def _():
            dg_ref[...] = jnp.zeros_like(dg_ref)

        dg_ref[...] += dgain

    return _call(
        body, (dh, x, gain, gu, wgu, wd), name=name, grid=(S // tm,),
        in_specs=[_rows(tm, D), _rows(tm, D), _fixed((1, D)), _rows(tm, 4 * FFS), _ANY, _ANY],
        out_specs=[_rows(tm, D), _rows(tm, 4 * FFS), _fixed((1, D))],
        out_shape=[_sds((S, D), F32), _sds((S, 4 * FFS), BF16), _sds((1, D), F32)],
        scratch_shapes=[_vmem_wide(wgu)] + _vmem_like(wd), compiler_params=_params(("arbitrary",), 56))


def _ffn_bwd_act(dh, gu, wd, name, exchange=None, after=()):
    tm = 512

    def body(dh_ref, gu_ref, wd_hbm, dgu_ref, wd_ref):
        _stage([(wd_hbm, wd_ref)])
        dhb = dh_ref[...].astype(BF16)
        for c0, cn in FF_CHUNKS:
            g = gu_ref[:, c0:c0 + cn].astype(F32)
            u = gu_ref[:, DFF + c0:DFF + c0 + cn].astype(F32)
            da = 0.5 * _nt(dhb, wd_ref[c0:c0 + cn, :])
            sg = jax.nn.sigmoid(g)
            dgu_ref[:, c0:c0 + cn] = (da * u * (sg * (1.0 + g * (1.0 - sg)))).astype(BF16)
            dgu_ref[:, DFF + c0:DFF + c0 + cn] = (da * (g * sg)).astype(BF16)

    res = _call(
        body, (dh, gu, wd), name=name, grid=(S // tm,),
        in_specs=[_rows(tm, D), _rows(tm, 4 * FFS), _ANY], out_specs=[_rows(tm, 4 * FFS)],
        out_shape=[_sds((S, 4 * FFS), BF16)], scratch_shapes=_vmem_like(wd),
        compiler_params=_params(("arbitrary",), 56), exchange=exchange, after=after)
    return res[0] if exchange is None else (res[0][0], res[1])


def _ffn_bwd_in(dh, x, gain, dgu, wgu, name, exchange=None, after=()):
    tm = 512

    def body(dh_ref, x_ref, g_ref, dgu_ref, wgu_hbm, dx_ref, dg_ref, wgu_ref):
        _stage([(wgu_hbm, wgu_ref)])
        dn = jnp.zeros((tm, D), F32)
        for half in (0, DFF):
            for c0, cn in FF_CHUNKS:
                cols = slice(half + c0, half + c0 + cn)
                dn = dn + _nt(dgu_ref[:, cols], wgu_ref[:, cols])
        r, xr = _rms(x_ref[...])
        dx, dgain = _rms_bwd(dn, xr, r, g_ref[...])
        dx_ref[...] = dh_ref[...] + dx

        @pl.when(pl.program_id(0) == 0)
        def _():
            dg_ref[...] = jnp.zeros_like(dg_ref)

        dg_ref[...] += dgain

    return _call(
        body, (dh, x, gain, dgu, wgu), name=name, grid=(S // tm,),
        in_specs=[_rows(tm, D), _rows(tm, D), _fixed((1, D)), _rows(tm, 4 * FFS), _ANY],
        out_specs=[_rows(tm, D), _fixed((1, D))],
        out_shape=[_sds((S, D), F32), _sds((1, D), F32)],
        scratch_shapes=[_vmem_wide(wgu)],
        compiler_params=_params(("arbitrary",), 56), exchange=exchange, after=after)


def _mix_in(h, gain, w_in, after=()):
    tm = 512

    def body(h_ref, g_ref, w_hbm, u_ref, xp_ref, q_ref, k_ref, v_ref, gp_ref, gs_ref, w_ref):
        _stage([(w_hbm, w_ref)])
        _, hr = _rms(h_ref[...])
        u = (hr * g_ref[...]).astype(BF16)
        u_ref[...] = u
        p0 = _nn(u, w_ref[0])
        xp_ref[...] = p0[:, :PW]
        q_ref[...] = p0[:, PW:].astype(BF16)
        p1 = _nn(u, w_ref[1])
        k_ref[...] = p1[:, :SBW].astype(BF16)
        v_ref[...] = p1[:, SBW:].astype(BF16)
        gp_ref[...] = jax.nn.sigmoid(_nn(u, w_ref[2])).astype(BF16)
        gs_ref[...] = jax.nn.sigmoid(_nn(u, w_ref[3])).astype(BF16)

    return _call(
        body, (h, gain, w_in), name="mix_in", grid=(S // tm,),
        in_specs=[_rows(tm, D), _fixed((1, D)), _ANY],
        out_specs=[_rows(tm, D), _rows(tm, PW), _rows(tm, SBW), _rows(tm, SBW), _rows(tm, SBW),
                   _rows(tm, D), _rows(tm, D)],
        out_shape=[_sds((S, D), BF16), _sds((S, PW), F32), _sds((S, SBW), BF16), _sds((S, SBW), BF16),
                   _sds((S, SBW), BF16), _sds((S, D), BF16), _sds((S, D), BF16)],
        scratch_shapes=_vmem_like(w_in),
        compiler_params=_params(("arbitrary",), 48), free=(1,), after=after)


def _hilo_dot(x, tri):
    hi = x.astype(BF16)
    lo = (x - hi.astype(F32)).astype(BF16)
    return _nn(hi, tri) + _nn(lo, tri)


def _log_terms(qk):
    z2 = qk * (SCALE * LOG2E)
    lb = jnp.minimum(z2, 0.0) - jnp.log2(1.0 + jnp.exp2(-jnp.abs(z2)))
    return lb, lb - z2


def _head_masks():
    lane = lax.broadcasted_iota(jnp.int32, (1, 2 * DH), 1)
    return (lane < DH, lane >= DH)


def _attn_fwd(q, k, v, exchange=None):
    T = TA

    def body(q_ref, k_ref, v_ref, o_ref, c_ref):
        i2 = 2 * pl.program_id(1)
        row = lax.broadcasted_iota(jnp.int32, (T, T), 0)
        col = lax.broadcasted_iota(jnp.int32, (T, T), 1)
        after = (row > col).astype(BF16)
        causal = col < row
        masks = _head_masks()
        qms = {}
        for b in range(QB):
            q2 = q_ref[b * T:(b + 1) * T, :]
            for h, hm in enumerate(masks):
                qms[b, h] = jnp.where(hm, q2, jnp.zeros_like(q2))

        def blocks(keys, pairs, carries, os):
            ks, vms = [], []
            for j in keys:
                rows = pl.ds(pl.multiple_of(j * T, T), T)
                vj = v_ref[rows, :]
                ks.append(k_ref[rows, :])
                vms.append([jnp.where(hm, vj, jnp.zeros_like(vj)) for hm in masks])
            units = [(n, h) for n in range(len(pairs)) for h in range(2)]
            qks = {(n, h): _nt(qms[pairs[n][0], h], ks[pairs[n][1]]) for n, h in units}
            lbs, l1ms = {}, {}
            for u in units:
                lbs[u], l1m = _log_terms(qks[u])
                l1ms[u] = jnp.where(causal, l1m, 0.0) if pairs[u[0]][2] else l1m
            cins = {u: _hilo_dot(l1ms[u], after) for u in units}
            carries, os = dict(carries), list(os)
            for n, h in units:
                b, key, diag = pairs[n]
                a = jnp.exp2(lbs[n, h] + cins[n, h] + carries[b, h])
                if diag:
                    a = jnp.where(causal, a, 0.0)
                os[b] = os[b] + _nn(a.astype(BF16), vms[key][h])
                carries[b, h] = carries[b, h] + jnp.sum(l1ms[n, h], axis=1, keepdims=True)
            return carries, tuple(os)

        carries = {(b, h): jnp.zeros((T, 1), F32) for b in range(QB) for h in range(2)}
        os = tuple(jnp.zeros((T, 2 * DH), F32) for _ in range(QB))
        carries, os = blocks([i2 + 1, i2], [(1, 0, True), (0, 1, True), (1, 1, False)], carries, os)
        carries, os = lax.fori_loop(
            0, i2 // 2,
            lambda t, c: blocks([i2 - 1 - 2 * t, i2 - 2 - 2 * t],
                                [(0, 0, False), (1, 0, False), (0, 1, False), (1, 1, False)], c[0], c[1]),
            (carries, os))
        for b in range(QB):
            o_ref[b * T:(b + 1) * T, :] = os[b].astype(BF16)
            c_ref[b * T:(b + 1) * T, :] = jnp.where(masks[0], carries[b, 0], carries[b, 1])

    blk = pl.BlockSpec((QB * T, 2 * DH), lambda p, i: (i, p))
    full = pl.BlockSpec((S, 2 * DH), lambda p, i: (0, p))
    return _call(
        body, (q, k, v), name="attn_fwd", grid=(SBW // (2 * DH), S // (QB * T)),
        in_specs=[blk, full, full], out_specs=[blk, blk],
        out_shape=[_sds((S, SBW), BF16), _sds((S, SBW), F32)],
        compiler_params=_params(("arbitrary", "arbitrary"), 40), exchange=exchange)


def _attn_bwd(q, k, v, do, ctot, after=()):
    T = TA
    nq = S // (QB * T)

    def body(q_ref, k_ref, v_ref, do_ref, c_ref, dq_ref, dk_ref, dv_ref, dk_acc, dv_acc):
        step = pl.program_id(1)
        i2 = 2 * step

        @pl.when(step == 0)
        def _():
            dk_acc[...] = jnp.zeros_like(dk_acc)
            dv_acc[...] = jnp.zeros_like(dv_acc)

        row = lax.broadcasted_iota(jnp.int32, (T, T), 0)
        col = lax.broadcasted_iota(jnp.int32, (T, T), 1)
        upto = (row <= col).astype(BF16)
        before = (row < col).astype(BF16)
        causal = col < row
        masks = _head_masks()
        qms, doms, ctots = {}, {}, {}
        for b in range(QB):
            q2, do2 = q_ref[b * T:(b + 1) * T, :], do_ref[b * T:(b + 1) * T, :]
            for h, hm in enumerate(masks):
                qms[b, h] = jnp.where(hm, q2, jnp.zeros_like(q2))
                doms[b, h] = jnp.where(hm, do2, jnp.zeros_like(do2))
                ctots[b, h] = c_ref[b * T:(b + 1) * T, h * DH:h * DH + 1]

        def blocks(keys, pairs, sums, dqs):
            rows = [pl.ds(pl.multiple_of(j * T, T), T) for j in keys]
            ks, vs = [k_ref[r, :] for r in rows], [v_ref[r, :] for r in rows]
            kms = [[jnp.where(hm, kj, jnp.zeros_like(kj)) for hm in masks] for kj in ks]
            units = [(n, h) for n in range(len(pairs)) for h in range(2)]
            qks = {(n, h): _nt(qms[pairs[n][0], h], ks[pairs[n][1]]) for n, h in units}
            das = {(n, h): _nt(doms[pairs[n][0], h], vs[pairs[n][1]]) for n, h in units}
            lbs, l1ms = {}, {}
            for u in units:
                lbs[u], l1m = _log_terms(qks[u])
                l1ms[u] = jnp.where(causal, l1m, 0.0) if pairs[u[0]][2] else l1m
            pins = {u: _hilo_dot(l1ms[u], upto) for u in units}
            sums = dict(sums)
            a_s, dls, cps = {}, {}, {}
            for n, h in units:
                b, _, diag = pairs[n]
                cl, cp = sums[b, h]
                a = jnp.exp2(lbs[n, h] + (ctots[b, h] - cl) - pins[n, h])
                if diag:
                    a = jnp.where(causal, a, 0.0)
                a_s[n, h] = a.astype(BF16)
                dls[n, h] = das[n, h] * a
                cps[n, h] = cp
                sums[b, h] = (cl + jnp.sum(l1ms[n, h], axis=1, keepdims=True),
                              cp + jnp.sum(dls[n, h], axis=1, keepdims=True))
            pexs = {u: _hilo_dot(dls[u], before) for u in units}
            dzbs = {}
            for u in units:
                dz = dls[u] - jnp.exp2(lbs[u]) * (dls[u] + pexs[u] + cps[u])
                if pairs[u[0]][2]:
                    dz = jnp.where(causal, dz, 0.0)
                dzbs[u] = dz.astype(BF16)
            dqs = list(dqs)
            for n, h in units:
                dqs[pairs[n][0]] = dqs[pairs[n][0]] + _nn(dzbs[n, h], kms[pairs[n][1]][h])
            for key, r in enumerate(rows):
                mine = [(n, h) for n, h in units if pairs[n][1] == key]
                dk_acc[r, :] += functools.reduce(jnp.add, [_tn(dzbs[u], qms[pairs[u[0]][0], u[1]]) for u in mine])
                dv_acc[r, :] += functools.reduce(jnp.add, [_tn(a_s[u], doms[pairs[u[0]][0], u[1]]) for u in mine])
            return sums, tuple(dqs)

        zero = jnp.zeros((T, 1), F32)
        sums = {(b, h): (zero, zero) for b in range(QB) for h in range(2)}
        dqs = tuple(jnp.zeros((T, 2 * DH), F32) for _ in range(QB))
        sums, dqs = lax.fori_loop(
            0, i2 // 2,
            lambda t, c: blocks([2 * t, 2 * t + 1],
                                [(0, 0, False), (1, 0, False), (0, 1, False), (1, 1, False)], c[0], c[1]),
            (sums, dqs))
        _, dqs = blocks([i2, i2 + 1], [(0, 0, True), (1, 0, False), (1, 1, True)], sums, dqs)
        for b in range(QB):
            dq_ref[b * T:(b + 1) * T, :] = (dqs[b] * SCALE).astype(BF16)

        @pl.when(step == nq - 1)
        def _():
            dk_ref[...] = (dk_acc[...] * SCALE).astype(BF16)
            dv_ref[...] = dv_acc[...].astype(BF16)

    blk = pl.BlockSpec((QB * T, 2 * DH), lambda p, i: (i, p))
    full = pl.BlockSpec((S, 2 * DH), lambda p, i: (0, p))
    return _call(
        body, (q, k, v, do, ctot), name="attn_bwd", grid=(SBW // (2 * DH), nq),
        in_specs=[blk, full, full, blk, blk], out_specs=[blk, full, full],
        out_shape=[_sds((S, SBW), BF16), _sds((S, SBW), BF16), _sds((S, SBW), BF16)],
        scratch_shapes=[pltpu.VMEM((S, 2 * DH), F32), pltpu.VMEM((S, 2 * DH), F32)],
        compiler_params=_params(("arbitrary", "arbitrary"), 40), after=after)


def _pool_counts(first_row, tm):
    pos = first_row + lax.broadcasted_iota(jnp.int32, (tm, 1), 0)
    return [jnp.minimum(pos + 1, w).astype(F32) for w in POOL_WINDOWS]


def _mix_out(h, xp, o_sb, gp, gs, w_group, scale, w_bp, w_ba, w_out, exchange=None):
    tm = 512

    def body(h_ref, xp_ref, o_ref, gp_ref, gs_ref, wg_hbm, sc_ref, wbp_hbm, wba_hbm, wo_hbm,
             h2_ref, pm_ref, p_ref, yp_ref, ys_ref, m_ref, halo, wg_ref, wbp_ref, wba_ref, wo_ref):
        _stage([(wg_hbm, wg_ref), (wbp_hbm, wbp_ref), (wba_hbm, wba_ref), (wo_hbm, wo_ref)])
        i = pl.program_id(0)

        @pl.when(i == 0)
        def _():
            halo[...] = jnp.zeros_like(halo)

        xp = xp_ref[...]
        ext = jnp.concatenate([halo[...], xp], axis=0)
        halo[...] = xp[tm - HALO:, :]
        counts = _pool_counts(i * tm, tm)
        for gi in range(len(POOL_WINDOWS)):
            lanes = slice(gi * PG, (gi + 1) * PG)
            win = ext[:, lanes]
            for step in range(gi + 1):
                win = win + pltpu.roll(win, 1 << step, 0)
            pm = (win[HALO:, :] / counts[gi] - xp[:, lanes]).astype(BF16)
            pm_ref[:, lanes] = pm
            p_ref[:, lanes] = (_nn(pm, wg_ref[gi]) * sc_ref[:, lanes]).astype(BF16)
        pb = p_ref[...]
        ob = o_ref[...]
        for j in range(NSH):
            cols = slice(j * (D // NSH), (j + 1) * (D // NSH))
            yp = _nn(pb, wbp_ref[j])
            ys = _nn(ob, wba_ref[j])
            yp_ref[:, cols] = yp.astype(BF16)
            ys_ref[:, cols] = ys.astype(BF16)
            m_ref[:, cols] = (gp_ref[:, cols].astype(F32) * yp + gs_ref[:, cols].astype(F32) * ys).astype(BF16)
        h2_ref[...] = h_ref[...] + _nn(m_ref[...], wo_ref[...])

    return _call(
        body, (h, xp, o_sb, gp, gs, w_group, scale, w_bp, w_ba, w_out), name="mix_out", grid=(S // tm,),
        in_specs=[_rows(tm, D), _rows(tm, PW), _rows(tm, SBW), _rows(tm, D), _rows(tm, D),
                  _ANY, _fixed((1, PW)), _ANY, _ANY, _ANY],
        out_specs=[_rows(tm, D), _rows(tm, PW), _rows(tm, PW), _rows(tm, D), _rows(tm, D), _rows(tm, D)],
        out_shape=[_sds((S, D), F32), _sds((S, PW), BF16), _sds((S, PW), BF16), _sds((S, D), BF16),
                   _sds((S, D), BF16), _sds((S, D), BF16)],
        scratch_shapes=[pltpu.VMEM((HALO, PW), F32)] + _vmem_like(w_group, w_bp, w_ba, w_out),
        compiler_params=_params(("arbitrary",), 48), free=(5, 6), exchange=exchange)


def _mix_bwd_out(dh, gp, gs, yp, ys, pm, w_group, scale, w_bp, w_ba, w_out, exchange=None):
    tm = 512
    nt = S // tm

    def body(dh_ref, gp_ref, gs_ref, yp_ref, ys_ref, pm_ref, wg_hbm, sc_ref, wbp_hbm, wba_hbm, wo_hbm,
             dlg_ref, dyp_ref, dys_ref, do_ref, dyg_ref, dxp_ref, dsc_ref, halo, wg_ref, wbp_ref, wba_ref, wo_ref):
        _stage([(wg_hbm, wg_ref), (wbp_hbm, wbp_ref), (wba_hbm, wba_ref), (wo_hbm, wo_ref)])
        step = pl.program_id(0)

        @pl.when(step == 0)
        def _():
            halo[...] = jnp.zeros_like(halo)
            dsc_ref[...] = jnp.zeros_like(dsc_ref)

        dm = _nt(dh_ref[...].astype(BF16), wo_ref[...])
        gp = gp_ref[...].astype(F32)
        gs = gs_ref[...].astype(F32)
        yp = yp_ref[...].astype(F32)
        ys = ys_ref[...].astype(F32)
        dlg_ref[:, :D] = (dm * yp * gp * (1.0 - gp)).astype(BF16)
        dlg_ref[:, D:] = (dm * ys * gs * (1.0 - gs)).astype(BF16)
        dyp_ref[...] = (dm * gp).astype(BF16)
        dys_ref[...] = (dm * gs).astype(BF16)
        dp = jnp.zeros((tm, PW), F32)
        do = jnp.zeros((tm, SBW), F32)
        for j in range(NSH):
            cols = slice(j * (D // NSH), (j + 1) * (D // NSH))
            dp = dp + _nt(dyp_ref[:, cols], wbp_ref[j])
            do = do + _nt(dys_ref[:, cols], wba_ref[j])
        do_ref[...] = do.astype(BF16)
        counts = _pool_counts((nt - 1 - step) * tm, tm)
        dscale = []
        for gi in range(len(POOL_WINDOWS)):
            lanes = slice(gi * PG, (gi + 1) * PG)
            dpg = dp[:, lanes]
            dscale.append(jnp.sum(dpg * _nn(pm_ref[:, lanes], wg_ref[gi]), axis=0, keepdims=True))
            dyg = (dpg * sc_ref[:, lanes]).astype(BF16)
            dyg_ref[:, lanes] = dyg
            dpm = _nt(dyg, wg_ref[gi])
            per = dpm / counts[gi]
            win = jnp.concatenate([per, halo[:, lanes]], axis=0)
            halo[:, lanes] = per[:HALO, :]
            for s in range(gi + 1):
                win = win + pltpu.roll(win, tm + HALO - (1 << s), 0)
            dxp_ref[:, lanes] = (win[:tm, :] - dpm).astype(BF16)
        dsc_ref[...] += jnp.concatenate(dscale, axis=1)

    rev = lambda width: pl.BlockSpec((tm, width), lambda i: (nt - 1 - i, 0))
    return _call(
        body, (dh, gp, gs, yp, ys, pm, w_group, scale, w_bp, w_ba, w_out), name="mix_bwd_out", grid=(nt,),
        in_specs=[rev(D), rev(D), rev(D), rev(D), rev(D), rev(PW), _ANY, _fixed((1, PW)), _ANY, _ANY, _ANY],
        out_specs=[rev(2 * D), rev(D), rev(D), rev(SBW), rev(PW), rev(PW), _fixed((1, PW))],
        out_shape=[_sds((S, 2 * D), BF16), _sds((S, D), BF16), _sds((S, D), BF16), _sds((S, SBW), BF16),
                   _sds((S, PW), BF16), _sds((S, PW), BF16), _sds((1, PW), F32)],
        scratch_shapes=[pltpu.VMEM((HALO, PW), F32)] + _vmem_like(w_group, w_bp, w_ba, w_out),
        compiler_params=_params(("arbitrary",), 48), exchange=exchange)


def _mix_bwd_in(dh, h, gain, pieces, w_in, exchange=None):
    tm = 512
    widths = [p.shape[1] for p in pieces]

    def body(dh_ref, h_ref, g_ref, *rest):
        piece_refs, (w_hbm, dx_ref, dg_ref, w_ref, dp_ref) = rest[:len(pieces)], rest[len(pieces):]
        _stage([(w_hbm, w_ref)])
        at = 0
        for ref, width in zip(piece_refs, widths):
            dp_ref[:, at:at + width] = ref[...]
            at += width
        du = jnp.zeros((tm, D), F32)
        for j in range(NSH):
            du = du + _nt(dp_ref[:, j * D:(j + 1) * D], w_ref[j])
        r, hr = _rms(h_ref[...])
        dx, dgain = _rms_bwd(du, hr, r, g_ref[...])
        dx_ref[...] = dh_ref[...] + dx

        @pl.when(pl.program_id(0) == 0)
        def _():
            dg_ref[...] = jnp.zeros_like(dg_ref)

        dg_ref[...] += dgain

    return _call(
        body, (dh, h, gain, *pieces, w_in), name="mix_bwd_in", grid=(S // tm,),
        in_specs=[_rows(tm, D), _rows(tm, D), _fixed((1, D))] + [_rows(tm, w) for w in widths] + [_ANY],
        out_specs=[_rows(tm, D), _fixed((1, D))],
        out_shape=[_sds((S, D), F32), _sds((1, D), F32)],
        scratch_shapes=_vmem_like(w_in) + [pltpu.VMEM((tm, 4 * D), BF16)],
        compiler_params=_params(("arbitrary",), 48), exchange=exchange)


def _wgrad_in(u, pieces):
    dxp, dq, dk, dv, dlg = pieces

    def body(u_ref, dxp_ref, dq_ref, dk_ref, dv_ref, dlg_ref, o_ref):
        j = pl.program_id(0)
        u = u_ref[...]

        def two(left_ref, right_ref):
            o_ref[:, :PW] = _tn(u, left_ref[...]).astype(BF16)
            o_ref[:, PW:] = _tn(u, right_ref[...]).astype(BF16)

        pl.when(j == 0)(lambda: two(dxp_ref, dq_ref))
        pl.when(j == 1)(lambda: two(dk_ref, dv_ref))

        @pl.when(j >= 2)
        def _():
            o_ref[...] = _tn(u, dlg_ref[...]).astype(BF16)

    whole = lambda width: pl.BlockSpec((S, width), lambda j: (0, 0))
    return _call(
        body, (u, dxp, dq, dk, dv, dlg), name="wgrad_in", grid=(NSH,),
        in_specs=[whole(D), whole(PW), whole(SBW), whole(SBW), whole(SBW),
                  pl.BlockSpec((S, D), lambda j: (0, jnp.maximum(j - 2, 0)))],
        out_specs=[pl.BlockSpec((None, D, D), lambda j: (j, 0, 0))], out_shape=[_sds((NSH, D, D), BF16)],
        compiler_params=_params(("arbitrary",), 56))[0]


def _wgrad(a, b, nblk, ti, tk, name, out_dtype=BF16, exchange=None, after=(), shards=1):
    ka, n = a.shape[1], b.shape[1]
    ns = n // nblk
    nk = S // tk
    width = ns // shards

    def body(a_ref, b_ref, o_ref, acc_ref):
        k = pl.program_id(2)

        def product():
            return _tn(a_ref[...].astype(BF16), b_ref[...].astype(BF16))

        @pl.when(k == 0)
        def _():
            acc_ref[...] = product()

        if nk > 2:
            @pl.when(jnp.logical_and(k > 0, k < nk - 1))
            def _():
                acc_ref[...] += product()

        @pl.when(k == nk - 1)
        def _():
            total = acc_ref[...] + product()
            for s in range(shards):
                o_ref[s] = total[:, s * width:(s + 1) * width].astype(out_dtype)

    res = _call(
        body, (a, b), name=name, grid=(nblk, ka // ti, nk),
        in_specs=[pl.BlockSpec((tk, ti), lambda j, i, k: (k, i)), pl.BlockSpec((tk, ns), lambda j, i, k: (k, j))],
        out_specs=[pl.BlockSpec((shards, ti, width), lambda j, i, k: (j, i, 0))],
        out_shape=[_sds((nblk * shards, ka, width), out_dtype)],
        scratch_shapes=[pltpu.VMEM((ti, ns), F32)],
        compiler_params=_params(("arbitrary", "arbitrary", "arbitrary"), 56), exchange=exchange, after=after)
    return res[0] if exchange is None else (res[0][0], res[1])


def _wgrad_branches(p, dyp, o_sb, dys, pm, dyg):
    cols = D // NSH

    def body(p_ref, dyp_ref, o_ref, dys_ref, pm_ref, dyg_ref, gbp_ref, gba_ref, gg_ref):
        gbp_ref[...] = _tn(p_ref[...], dyp_ref[...]).astype(BF16)
        gba_ref[...] = _tn(o_ref[...], dys_ref[...]).astype(BF16)
        gg_ref[...] = _tn(pm_ref[...], dyg_ref[...])

    whole = lambda width: pl.BlockSpec((S, width), lambda j: (0, 0))
    col = lambda width: pl.BlockSpec((S, width), lambda j: (0, j))
    return _call(
        body, (p, dyp, o_sb, dys, pm, dyg), name="wgrad_branches", grid=(NSH,),
        in_specs=[whole(PW), col(cols), whole(SBW), col(cols), col(PG), col(PG)],
        out_specs=[pl.BlockSpec((None, PW, cols), lambda j: (j, 0, 0)),
                   pl.BlockSpec((None, SBW, cols), lambda j: (j, 0, 0)),
                   pl.BlockSpec((None, PG, PG), lambda j: (j, 0, 0))],
        out_shape=[_sds((NSH, PW, cols), BF16), _sds((NSH, SBW, cols), BF16), _sds((NSH, PG, PG), F32)],
        compiler_params=_params(("arbitrary",), 40))


def _place():
    x, y, c = lax.axis_index("x"), lax.axis_index("y"), lax.axis_index("c")
    chips = [(1 - x, y), (x, 1 - y), (1 - x, 1 - y)]
    return x, y, c, chips


def _remote(src, dst, ssem, rsem, dev):
    return pltpu.make_async_remote_copy(src_ref=src, dst_ref=dst, send_sem=ssem, recv_sem=rsem,
                                        device_id=dev, device_id_type=MESH)


def _cast_into_block(ws, me_idx, name):
    steps = 4
    shapes = [(w.shape[0] // steps, w.shape[1]) for w in ws]

    def body(me_ref, *refs):
        for w_ref, o_ref in zip(refs[:len(ws)], refs[len(ws):]):
            o_ref[...] = w_ref[...].astype(BF16)

    return pl.pallas_call(
        body, name=name, out_shape=[_sds((NSH,) + w.shape, BF16) for w in ws],
        grid_spec=pltpu.PrefetchScalarGridSpec(
            num_scalar_prefetch=1, grid=(steps,),
            in_specs=[pl.BlockSpec((r, c), lambda s, me: (s, 0)) for r, c in shapes],
            out_specs=[pl.BlockSpec((None, r, c), lambda s, me: (me[0], s, 0)) for r, c in shapes]),
        compiler_params=_params(("arbitrary",), 32),
    )(me_idx, *ws)


def _ex_gather(bufs):
    n = len(bufs)
    per = 8

    def plan(outs, ssem, rsem, w):
        x, y, c, _ = _place()
        sib, nbr_x, nbr_y = (x, y, 1 - c), (1 - x, y, c), (x, 1 - y, c)
        half = outs[w].shape[1] // 2
        quarter = half // 2
        sem = lambda k: (ssem.at[per * w + k], rsem.at[per * w + k])
        rows = lambda blk, start, size: outs[w].at[blk, pl.ds(start, size)]
        mine = rows(2 * x + y, c * half, half)
        from_x = rows(2 * (1 - x) + y, c * half, half)
        from_y = rows(2 * x + (1 - y), c * half, half)
        diag = 2 * (1 - x) + (1 - y)
        pass_y = rows(2 * (1 - x) + y, c * half, quarter)
        pass_x = rows(2 * x + (1 - y), c * half + quarter, quarter)
        diag_0, diag_1 = rows(diag, c * half, quarter), rows(diag, c * half + quarter, quarter)
        first = [_remote(mine, mine, *sem(0), nbr_x), _remote(mine, mine, *sem(1), nbr_y)]
        arrivals = [
            (_remote(from_x, from_x, *sem(0), nbr_x),
             [_remote(pass_y, pass_y, *sem(2), nbr_y), _remote(from_x, from_x, *sem(4), sib)]),
            (_remote(from_y, from_y, *sem(1), nbr_y),
             [_remote(pass_x, pass_x, *sem(3), nbr_x), _remote(from_y, from_y, *sem(5), sib)]),
            (_remote(diag_0, diag_0, *sem(2), nbr_y), [_remote(diag_0, diag_0, *sem(6), sib)]),
            (_remote(diag_1, diag_1, *sem(3), nbr_x), [_remote(diag_1, diag_1, *sem(7), sib)]),
        ]
        other = (1 - c) * half
        from_sibling = [
            _remote(rows(2 * (1 - x) + y, other, half), rows(2 * (1 - x) + y, other, half), *sem(4), sib),
            _remote(rows(2 * x + (1 - y), other, half), rows(2 * x + (1 - y), other, half), *sem(5), sib),
            _remote(rows(diag, other, quarter), rows(diag, other, quarter), *sem(6), sib),
            _remote(rows(diag, other + quarter, quarter), rows(diag, other + quarter, quarter), *sem(7), sib),
        ]
        return first, arrivals, from_sibling

    def start(ins, outs, ssem, rsem):
        x, y, c, _ = _place()
        for w in range(n):
            half = outs[w].shape[1] // 2
            mine = outs[w].at[2 * x + y, pl.ds(c * half, half)]
            _remote(mine, mine, ssem.at[per * w], rsem.at[per * w], (1 - x, y, c)).start()
            _remote(mine, mine, ssem.at[per * w + 1], rsem.at[per * w + 1], (x, 1 - y, c)).start()

    def finish(ins, outs, ssem, rsem):
        plans = [plan(outs, ssem, rsem, w) for w in range(n)]
        started = []
        for direct in (True, False):
            for first, arrivals, _ in plans:
                for arrived, onward in (arrivals[:2] if direct else arrivals[2:]):
                    arrived.wait_recv()
                    for cp in onward:
                        cp.start()
                    started += onward
        for first, _, from_sibling in plans:
            for cp in from_sibling:
                cp.wait_recv()
            started += first
        for cp in started:
            cp.wait_send()

    return Exchange(bufs, [_sds(b.shape, b.dtype) for b in bufs], {w: w for w in range(n)}, per * n, start, finish)


def _ex_gather_direct(bufs):
    n = len(bufs)

    def copies(outs, ssem, rsem, only_first=False):
        x, y, c, chips = _place()
        me, sib = 2 * x + y, (x, y, 1 - c)
        first, relay, last = [], [], []
        for w in range(n):
            half = outs[w].shape[1] // 2
            mine = outs[w].at[me, pl.ds(c * half, half)]
            for k, (px, py) in enumerate(chips):
                sems = (ssem.at[6 * w + k], rsem.at[6 * w + k])
                sib_sems = (ssem.at[6 * w + 3 + k], rsem.at[6 * w + 3 + k])
                first.append(_remote(mine, mine, *sems, (px, py, c)))
                if only_first:
                    continue
                got = outs[w].at[2 * px + py, pl.ds(c * half, half)]
                relay.append((_remote(got, got, *sems, (px, py, c)), _remote(got, got, *sib_sems, sib)))
                theirs = outs[w].at[2 * px + py, pl.ds((1 - c) * half, half)]
                last.append(_remote(theirs, theirs, *sib_sems, sib))
        return first, relay, last

    def start(ins, outs, ssem, rsem):
        for cp in copies(outs, ssem, rsem, only_first=True)[0]:
            cp.start()

    def finish(ins, outs, ssem, rsem):
        first, relay, last = copies(outs, ssem, rsem)
        for arrived, onward in relay:
            arrived.wait_recv()
            onward.start()
        for cp in last:
            cp.wait_recv()
        for cp in first:
            cp.wait_send()
        for _, onward in relay:
            onward.wait_send()

    return Exchange(bufs, [_sds(b.shape, b.dtype) for b in bufs], {w: w for w in range(n)}, 6 * n, start, finish)


def _simple_exchange(arrays, landing, aliases, make_copies, sibling_only=False):
    def start(ins, outs, ssem, rsem):
        for cp, _ in make_copies(ins, outs, ssem, rsem, False):
            cp.start()

    def finish(ins, outs, ssem, rsem):
        cps = make_copies(ins, outs, ssem, rsem, True)
        for _, landed in cps:
            landed.wait_recv()
        for cp, _ in cps:
            cp.wait_send()

    return Exchange(arrays, landing, aliases, len(arrays) * 3, start, finish, sibling_only)


def _ex_pair_swap(grads):
    def make(ins, outs, ssem, rsem, landing):
        x, y, c, _ = _place()
        cps = [_remote(ins[w].at[:, 1 - c], outs[w], ssem.at[w], rsem.at[w], (x, y, 1 - c))
               for w in range(len(grads))]
        return [(cp, cp) for cp in cps]

    return _simple_exchange(grads, [_sds((NSH,) + g.shape[2:], g.dtype) for g in grads], {}, make, True)


def _ex_relay(bufs):
    def make(ins, outs, ssem, rsem, landing):
        x, y, c, chips = _place()
        sib = (x, y, 1 - c)
        out = []
        for w in range(len(bufs)):
            half = outs[w].shape[1] // 2
            for k, (px, py) in enumerate(chips):
                sems = (ssem.at[3 * w + k], rsem.at[3 * w + k])
                have = outs[w].at[2 * px + py, pl.ds(c * half, half)]
                miss = outs[w].at[2 * px + py, pl.ds((1 - c) * half, half)]
                out.append((_remote(have, have, *sems, sib), _remote(miss, miss, *sems, sib) if landing else None))
        return out

    return _simple_exchange(bufs, [_sds(b.shape, b.dtype) for b in bufs], {w: w for w in range(len(bufs))}, make, True)


def _ex_share(bufs):
    def make(ins, outs, ssem, rsem, landing):
        x, y, c, _ = _place()
        sib = (x, y, 1 - c)
        return [(_remote(outs[w].at[c], outs[w].at[c], ssem.at[w], rsem.at[w], sib),
                 _remote(outs[w].at[1 - c], outs[w].at[1 - c], ssem.at[w], rsem.at[w], sib) if landing else None)
                for w in range(len(bufs))]

    return _simple_exchange(bufs, [_sds(b.shape, b.dtype) for b in bufs], {w: w for w in range(len(bufs))}, make, True)


def _small_copies(slots, ssems, rsems, sending):
    x, y, c, _ = _place()
    out = []
    for m in range(1, 8):
        px, py, pc = x ^ (m >> 2), y ^ ((m >> 1) & 1), c ^ (m & 1)
        slot = slots.at[4 * x + 2 * y + c if sending else 4 * px + 2 * py + pc]
        out.append(_remote(slot, slot, ssems[m - 1], rsems[m - 1], (px, py, pc)))
    return out


def _small_gather_start(slots, name):
    def body(*refs):
        for cp in _small_copies(refs[0], refs[1:8], refs[8:15], True):
            cp.start()
        refs[-1][...] = jnp.zeros_like(refs[-1])

    outs = pl.pallas_call(
        body, name=name,
        out_shape=([pltpu.SemaphoreType.DMA(())] * 14 + [pltpu.HBM(slots.shape, slots.dtype)]
                   + [jax.ShapeDtypeStruct((8, 128), F32)]),
        in_specs=[_HBM], out_specs=[_SEM] * 14 + [_HBM, _VM], input_output_aliases={0: 14},
        compiler_params=pltpu.CompilerParams(has_side_effects=_EFFECT),
    )(*_in_hbm([slots]))
    return outs[:14], outs[14], outs[15]


def _small_gather_wait(sems, slots, after, name):
    def body(*refs):
        for cp in _small_copies(refs[0], refs[1:8], refs[8:15], True):
            cp.wait_send()
        for cp in _small_copies(refs[0], refs[1:8], refs[8:15], False):
            cp.wait_recv()

    return pl.pallas_call(
        body, name=name, out_shape=pltpu.HBM(slots.shape, slots.dtype),
        in_specs=[_HBM] + [_SEM] * 14 + [_ANY] * len(after), out_specs=_HBM, input_output_aliases={0: 0},
        compiler_params=pltpu.CompilerParams(has_side_effects=_EFFECT),
    )(slots, *sems, *after)


def _pair_sum(grads, gots, c_idx, name):
    n = len(grads)

    def body(c_ref, *refs):
        for a_ref, b_ref, o_ref in zip(refs[:n], refs[n:2 * n], refs[2 * n:]):
            o_ref[...] = (a_ref[...].astype(F32) + b_ref[...].astype(F32)).astype(BF16)

    halves = [g.shape[2:] for g in grads]
    return list(pl.pallas_call(
        body, name=name, out_shape=[_sds((NSH,) + h, BF16) for h in halves],
        grid_spec=pltpu.PrefetchScalarGridSpec(
            num_scalar_prefetch=1, grid=(NSH,),
            in_specs=[pl.BlockSpec((None, None) + h, lambda j, c: (j, c[0], 0, 0)) for h in halves]
            + [pl.BlockSpec((None,) + h, lambda j, c: (j, 0, 0)) for h in halves],
            out_specs=[pl.BlockSpec((None,) + h, lambda j, c: (j, 0, 0)) for h in halves]),
        compiler_params=_params(("arbitrary",), 40),
    )(c_idx, *_in_hbm(list(grads) + list(gots))))


def _chip_sum(owns, gots, place, name):
    n = len(owns)

    def body(place_ref, *refs):
        for own_ref, got_ref, o_ref in zip(refs[:n], refs[n:2 * n], refs[2 * n:]):
            acc = own_ref[...].astype(F32)
            for k in range(3):
                acc = acc + got_ref[k].astype(F32)
            o_ref[...] = acc

    shapes = [(o.shape[1] // 2, o.shape[2]) for o in owns]
    return list(pl.pallas_call(
        body, name=name, out_shape=[_sds((2, 2 * r, c), F32) for r, c in shapes],
        grid_spec=pltpu.PrefetchScalarGridSpec(
            num_scalar_prefetch=1, grid=(2,),
            in_specs=[pl.BlockSpec((None, r, c), lambda s, p: (p[0], s, 0)) for r, c in shapes]
            + [pl.BlockSpec((3, r, c), lambda s, p: (0, s, 0)) for r, c in shapes],
            out_specs=[pl.BlockSpec((None, r, c), lambda s, p: (p[1], s, 0)) for r, c in shapes]),
        compiler_params=_params(("arbitrary",), 40),
    )(place, *_in_hbm(list(owns) + list(gots))))


def _adamw_math(w, g, m, v):
    m = B1 * m + (1.0 - B1) * g
    v = B2 * v + (1.0 - B2) * (g * g)
    m_hat = m / (1.0 - B1 ** STEP)
    v_hat = v / (1.0 - B2 ** STEP)
    return -LR * (m_hat / (jnp.sqrt(v_hat) + AEPS) + WD * w), m, v


def _adamw(ws, gs, ms, vs, name, after=()):
    n, steps = len(ws), 4

    def body(*refs):
        ins, outs = refs[:4 * n], refs[4 * n:]
        for i in range(n):
            w_ref, g_ref, m_ref, v_ref = ins[4 * i:4 * i + 4]
            go_ref, d_ref, nm_ref, nv_ref = outs[4 * i:4 * i + 4]
            g = g_ref[...]
            go_ref[...] = g
            d_ref[...], nm_ref[...], nv_ref[...] = _adamw_math(w_ref[...], g, m_ref[...], v_ref[...])

    args, specs, shapes, free = [], [], [], []
    for i, (w, g, m, v) in enumerate(zip(ws, gs, ms, vs)):
        args += [w, g, m, v]
        specs += [pl.BlockSpec((w.shape[0] // steps, w.shape[1]), lambda r: (r, 0))] * 4
        shapes += [_sds(w.shape, F32)] * 4
        free += [4 * i, 4 * i + 2, 4 * i + 3]
    outs = _call(body, args, name=name, grid=(steps,), out_shape=shapes, in_specs=specs, out_specs=specs,
                 compiler_params=_params(("arbitrary",), 48), free=tuple(free), after=after)
    return [outs[4 * i:4 * i + 4] for i in range(n)]


def _small_update(gathered, w, m, v, entries):
    rows = w.shape[0]

    def body(ga_ref, w_ref, m_ref, v_ref, *out_refs):
        for j, (first, n) in enumerate(entries):
            mine = slice(first, first + n)
            g = ga_ref[mine, :]
            for dev in range(1, 8):
                g = g + ga_ref[dev * rows + first:dev * rows + first + n, :]
            results = (g,) + _adamw_math(w_ref[mine, :], g, m_ref[mine, :], v_ref[mine, :])
            for i, res in enumerate(results):
                out_refs[i * len(entries) + j][...] = res

    outs = pl.pallas_call(
        body, name="small_update",
        out_shape=[jax.ShapeDtypeStruct((n, 128), F32) for _ in range(4) for _, n in entries],
        in_specs=[_VM] * 4, out_specs=[_VM] * (4 * len(entries)),
    )(gathered, w, m, v)
    return [outs[i * len(entries):(i + 1) * len(entries)] for i in range(4)]


SMALL = ("ffn1_norm", "mix_norm", "ffn2_norm", "final_norm", "pool_scale", "loss", "pool_w_group")
BIG = ("ffn1_w_gate_up", "ffn1_w_down", "w_in", "w_branch_pool", "w_branch_attn", "w_out",
       "ffn2_w_gate_up", "ffn2_w_down")
ORDER = ("ffn1_norm", "ffn1_w_gate_up", "ffn1_w_down", "mix_norm", "w_in", "pool_w_group", "pool_scale",
         "w_branch_pool", "w_branch_attn", "w_out", "ffn2_norm", "ffn2_w_gate_up", "ffn2_w_down", "final_norm")
SMALL_ROWS = 560


def _pack_small(t):
    parts = []
    for k in SMALL:
        rows = t[k].reshape(-1, 128) if k in t else jnp.zeros((1, 128), F32)
        parts.append(jnp.pad(rows, ((0, -rows.shape[0] % 8), (0, 0))))
    packed = jnp.concatenate(parts, axis=0)
    assert packed.shape == (SMALL_ROWS, 128), packed.shape
    return packed


def _small_entries(like):
    out, at = [], 0
    for k in SMALL:
        n = like[k].size // 128 if k in like else 1
        out.append((at, n))
        at += n + (-n % 8)
    return out


def _halves(g):
    return g.reshape(NSH, 2, g.shape[1] // 2, g.shape[2])


def kernel(x, ffn1_norm, ffn1_w_gate_up, ffn1_w_down, mix_norm, w_in, pool_w_group, pool_scale, w_branch_pool, w_branch_attn, w_out, ffn2_norm, ffn2_w_gate_up, ffn2_w_down, final_norm, loss_target, m_ffn1_norm, m_ffn1_w_gate_up, m_ffn1_w_down, m_mix_norm, m_w_in, m_pool_w_group, m_pool_scale, m_w_branch_pool, m_w_branch_attn, m_w_out, m_ffn2_norm, m_ffn2_w_gate_up, m_ffn2_w_down, m_final_norm, v_ffn1_norm, v_ffn1_w_gate_up, v_ffn1_w_down, v_mix_norm, v_w_in, v_pool_w_group, v_pool_scale, v_w_branch_pool, v_w_branch_attn, v_w_out, v_ffn2_norm, v_ffn2_w_gate_up, v_ffn2_w_down, v_final_norm):
    wts = dict(ffn1_norm=ffn1_norm, ffn1_w_gate_up=ffn1_w_gate_up, ffn1_w_down=ffn1_w_down, mix_norm=mix_norm,
               w_in=w_in, pool_w_group=pool_w_group, pool_scale=pool_scale, w_branch_pool=w_branch_pool,
               w_branch_attn=w_branch_attn, w_out=w_out, ffn2_norm=ffn2_norm, ffn2_w_gate_up=ffn2_w_gate_up,
               ffn2_w_down=ffn2_w_down, final_norm=final_norm)
    mom = dict(ffn1_norm=m_ffn1_norm, ffn1_w_gate_up=m_ffn1_w_gate_up, ffn1_w_down=m_ffn1_w_down,
               mix_norm=m_mix_norm, w_in=m_w_in, pool_w_group=m_pool_w_group, pool_scale=m_pool_scale,
               w_branch_pool=m_w_branch_pool, w_branch_attn=m_w_branch_attn, w_out=m_w_out,
               ffn2_norm=m_ffn2_norm, ffn2_w_gate_up=m_ffn2_w_gate_up, ffn2_w_down=m_ffn2_w_down,
               final_norm=m_final_norm)
    var = dict(ffn1_norm=v_ffn1_norm, ffn1_w_gate_up=v_ffn1_w_gate_up, ffn1_w_down=v_ffn1_w_down,
               mix_norm=v_mix_norm, w_in=v_w_in, pool_w_group=v_pool_w_group, pool_scale=v_pool_scale,
               w_branch_pool=v_w_branch_pool, w_branch_attn=v_w_branch_attn, w_out=v_w_out,
               ffn2_norm=v_ffn2_norm, ffn2_w_gate_up=v_ffn2_w_gate_up, ffn2_w_down=v_ffn2_w_down,
               final_norm=v_final_norm)

    c_idx = lax.axis_index("c").astype(jnp.int32).reshape(1)
    me_idx = (2 * lax.axis_index("x") + lax.axis_index("y")).astype(jnp.int32).reshape(1)
    place = jnp.concatenate([me_idx, c_idx])
    x0, tgt = x[0], loss_target[0]
    wgrp = pool_w_group[0].astype(BF16)
    g1, gm, g2, gf = ffn1_norm, mix_norm, ffn2_norm, final_norm.reshape(1, D)
    grad, delta, new_m, new_v = {}, {}, {}, {}

    def pair_sums(keys, parts, got):
        return _pair_sum(parts, got, c_idx, "pair_sum_" + keys[0])

    def chip_sums(keys, chip_parts, owned):
        return _chip_sum(chip_parts, owned, place, "chip_sum_" + keys[0])

    def adamw(keys, after=()):
        outs = _adamw([wts[k][0] for k in keys], [grad[k][0] for k in keys], [mom[k][0] for k in keys],
                      [var[k][0] for k in keys], "adamw_" + keys[0], after=after)
        for k, res in zip(keys, outs):
            grad[k], delta[k], new_m[k], new_v[k] = (o.reshape(wts[k].shape) for o in res)

    first, late = ("ffn1_w_gate_up", "ffn1_w_down"), ("w_branch_pool", "w_branch_attn", "w_out",
                                                       "ffn2_w_gate_up", "ffn2_w_down")
    own = {}
    for group in (first, ("w_in",), late):
        own.update(zip(group, _cast_into_block([wts[k][0] for k in group], me_idx, "cast_" + group[0])))
    full = dict(zip(first, _exchange_alone(_ex_gather([own[k] for k in first]), "gather_ffn1")))
    wgu1, wd1 = full["ffn1_w_gate_up"], full["ffn1_w_down"].reshape(DFF, D)
    (h1, n1, gu1, a1), (win,) = _ffn_fwd(x0, g1, wgu1, wd1, "ffn1_fwd", exchange=_ex_gather_direct([own["w_in"]]))
    sems_l, thru_l, token_l = _gather_start([own[k_] for k_ in late], [h1], "gather_late_start")
    u, xp, q, k, v, gp, gs = _mix_in(h1, gm, win, after=(token_l,))
    o_sb, ctot = _attn_fwd(q, k, v)
    arrived = _gather_wait(sems_l, thru_l, [o_sb], "gather_late_wait")
    wbp, wba, wout = _exchange_alone(_ex_relay(arrived[:3]), "relay_mix")
    wout = wout.reshape(D, D)
    (h2, pm, p, yp, ys, mm), (wgu2, wd2) = _mix_out(h1, xp, o_sb, gp, gs, wgrp, pool_scale, wbp, wba, wout,
                                                    exchange=_ex_relay(arrived[3:]))
    wd2 = wd2.reshape(DFF, D)
    dh3, loss_row, d_gf, n3, gu3, a3 = _ffn_fwd(h2, g2, wgu2, wd2, "ffn2_fwd", head=(tgt, gf))

    def grad_gate_up(n, dgu, name, exchange=None):
        res = _wgrad(n, dgu, 2, D, 512, name, exchange=exchange, shards=2)
        return [_halves(res)] if exchange is None else ([_halves(res[0])], res[1])

    def grad_down(a, dh, name, exchange=None):
        res = _wgrad(a, dh, 1, FFS, 1024, name, exchange=exchange)
        halves = lambda g: [_halves(g.reshape(NSH, DFF // NSH, D))]
        return halves(res) if exchange is None else (halves(res[0]), res[1])

    k_gu2, k_d2, k_gu1, k_d1, k_in = (("ffn2_w_gate_up",), ("ffn2_w_down",), ("ffn1_w_gate_up",),
                                      ("ffn1_w_down",), ("w_in",))
    dh2, dgu3, d_g2 = _ffn_bwd(dh3, h2, g2, gu3, wgu2, wd2, "ffn2_bwd")
    pa = grad_gate_up(n3, dgu3, "wgrad_gu2") + grad_down(a3, dh3, "wgrad_d2")
    (dlg, dyp, dys, do_sb, dyg, dxp, d_scale), got_a = _mix_bwd_out(
        dh2, gp, gs, yp, ys, pm, wgrp, pool_scale, wbp, wba, wout, exchange=_ex_pair_swap(pa))
    chip_a = pair_sums(k_gu2 + k_d2, pa, got_a)
    kb = ("w_out", "w_branch_pool", "w_branch_attn")
    g_bp, g_ba, d_group = _wgrad_branches(p, dyp, o_sb, dys, pm, dyg)
    pb = [_halves(_wgrad(mm, dh2, 1, D, 1024, "wgrad_out").reshape(NSH, D // NSH, D)), _halves(g_bp), _halves(g_ba)]
    k_a, k_in = k_gu2 + k_d2, k_in + kb
    sems_a, thru_a, token_a = _scatter_start(chip_a, "scatter_a_start")
    dq, dk, dv = _attn_bwd(q, k, v, do_sb, ctot, after=(token_a,))
    chip_a, owned_a = _scatter_wait(sems_a, thru_a, [dq], "scatter_a_wait")
    halves_a = chip_sums(k_a, chip_a, owned_a)
    dproj = (dxp, dq, dk, dv, dlg)
    (dh1, d_gm), both_a = _mix_bwd_in(dh2, h1, gm, dproj, win, exchange=_ex_share(halves_a))
    for i, k_ in enumerate(k_a):
        grad[k_] = both_a[i].reshape(wts[k_].shape)

    p_in = [_halves(_wgrad_in(u, dproj))] + pb
    p_d1, got_in = grad_down(a1, dh1, "wgrad_d1", exchange=_ex_pair_swap(p_in))
    sems_in, thru_in, token_in = _scatter_start(pair_sums(k_in, p_in, got_in), "scatter_in_start")
    dgu1, got_d1 = _ffn_bwd_act(dh1, gu1, wd1, "ffn1_bwd_act", exchange=_ex_pair_swap(p_d1), after=(token_in,))
    sems_d1, thru_d1, token_d1 = _scatter_start(pair_sums(k_d1, p_d1, got_d1), "scatter_d1_start")
    p_gu1 = [_halves(_wgrad(n1, dgu1, 2, D, 512, "wgrad_gu1", after=(token_in, token_d1), shards=2))]
    sems_w, thru_w, token_w = _swap_start(p_gu1, "swap_gu1_start")
    chip_in, owned_in = _scatter_wait(sems_in, thru_in, [token_w], "scatter_in_wait")
    chip_d1, owned_d1 = _scatter_wait(sems_d1, thru_d1, [token_w], "scatter_d1_wait")
    halves_in = chip_sums(k_in, chip_in, owned_in)
    p_gu1, got_gu1 = _swap_wait(sems_w, thru_w, halves_in, "swap_gu1_wait")
    sems, thru, token = _scatter_start(pair_sums(k_gu1, p_gu1, got_gu1), "scatter_gu1_start")
    sems_h, thru_h, token_h = _share_start(halves_in, [token], "share_in_start")
    adamw(k_a, after=(token_h,))
    landed = _share_wait(sems_h, thru_h, [delta[k_a[0]]], "share_in_wait")
    for i, k_ in enumerate(k_in):
        grad[k_] = landed[i].reshape(wts[k_].shape)
    adamw(k_in)
    dx, d_g1 = _ffn_bwd_in(dh1, x0, g1, dgu1, wgu1, "ffn1_bwd_in", after=(token,))
    small_g = dict(ffn1_norm=d_g1, mix_norm=d_gm, ffn2_norm=d_g2, final_norm=d_gf, pool_scale=d_scale,
                   pool_w_group=d_group, loss=loss_row)
    dev = 4 * lax.axis_index("x") + 2 * lax.axis_index("y") + lax.axis_index("c")
    slots = lax.dynamic_update_slice(jnp.zeros((8, SMALL_ROWS, 128), F32), _pack_small(small_g)[None], (dev, 0, 0))
    sems_s, slots, token_s = _small_gather_start(slots, "small_gather_start")

    chip_gu1, owned_gu1 = _scatter_wait(sems, thru, [dx] + [delta[k_] for k_ in k_a + k_in], "scatter_gu1_wait")
    halves_last = chip_sums(k_d1 + k_gu1, chip_d1 + chip_gu1, owned_d1 + owned_gu1)
    both = _exchange_alone(_ex_share(halves_last), "share_last",
                           after=(token_s,))
    grad["ffn1_w_down"] = both[0].reshape(ffn1_w_down.shape)
    grad["ffn1_w_gate_up"] = both[1].reshape(ffn1_w_gate_up.shape)
    adamw(k_d1 + k_gu1, after=(token_s,))
    gathered = _small_gather_wait(sems_s, slots, [delta[k_] for k_ in k_d1 + k_gu1], "small_gather_wait")
    gathered = gathered.reshape(8 * SMALL_ROWS, 128)
    results = _small_update(gathered, _pack_small(wts), _pack_small(mom), _pack_small(var), _small_entries(wts))
    for dst, entries in zip((grad, delta, new_m, new_v), results):
        for k_, rows in zip(SMALL, entries):
            if k_ in wts:
                dst[k_] = rows.reshape(wts[k_].shape)
            elif dst is grad:
                loss = rows[0, 0]
    return (loss, dx[None], *[grad[k_] for k_ in ORDER], *[delta[k_] for k_ in ORDER],
            *[new_m[k_] for k_ in ORDER], *[new_v[k_] for k_ in ORDER])
```

```python
import dataclasses
import functools

import jax
import jax.numpy as jnp
from jax import lax
from jax.experimental import pallas as pl
from jax.experimental.pallas import tpu as pltpu

F32 = jnp.float32
BF16 = jnp.bfloat16

S = 2048
D = 1024
DFF = 2816
FFS = 2 * DFF // 4
NSH = 4
PW = 512
PG = 128
POOL_WINDOWS = (2, 4, 8, 16)
HALO = 16
SBW = 512
DH = 64
EPS = 1e-6
SCALE = 0.125
LOG2E = 1.4426950408889634
TA = 256
QB = 2
MIB = 1024 * 1024

LR, B1, B2, AEPS, WD, STEP = 0.001, 0.9, 0.999, 1e-08, 0.01, 10

_VM = pl.BlockSpec(memory_space=pltpu.VMEM)
_ANY = pl.BlockSpec(memory_space=pl.ANY)
MESH = pl.DeviceIdType.MESH
SIBLING_PAIR_ID = 1


def _nn(a, b):
    return jnp.dot(a, b, preferred_element_type=F32)


def _nt(a, b):
    return lax.dot_general(a, b, (((1,), (1,)), ((), ())), preferred_element_type=F32)


def _tn(a, b):
    return lax.dot_general(a, b, (((0,), (0,)), ((), ())), preferred_element_type=F32)


def _params(sem, vmem_mib):
    return pltpu.CompilerParams(dimension_semantics=sem, vmem_limit_bytes=vmem_mib * MIB)


def _rows(tm, width):
    return pl.BlockSpec((tm, width), lambda i: (i, 0))


def _fixed(shape):
    return pl.BlockSpec(shape, lambda *_: (0,) * len(shape))


def _sds(shape, dtype):
    return pltpu.HBM(shape, dtype)


def _in_hbm(args):
    return [pltpu.with_memory_space_constraint(a, pltpu.HBM) for a in args]


def _stage(pairs):
    pieces = 4

    def copy_all(sems):
        copies = []
        for src, dst in pairs:
            step = src.shape[0] // pieces
            for p in range(pieces):
                part = pl.ds(p * step, step)
                if len(dst.shape) == len(src.shape):
                    piece = (src.at[part], dst.at[part])
                else:
                    piece = (src.at[p], dst.at[:, pl.ds(p * src.shape[2], src.shape[2])])
                copies.append(pltpu.make_async_copy(*piece, sems.at[len(copies)]))
        for c in copies:
            c.start()
        for c in copies:
            c.wait()

    @pl.when(pl.program_id(0) == 0)
    def _():
        pl.run_scoped(copy_all, pltpu.SemaphoreType.DMA((pieces * len(pairs),)))


def _vmem_like(*arrays):
    return [pltpu.VMEM(a.shape, a.dtype) for a in arrays]


def _vmem_wide(w):
    return pltpu.VMEM((w.shape[1], w.shape[0] * w.shape[2]), w.dtype)


FF_CHUNKS = ((0, 1536), (1536, DFF - 1536))


class Exchange:
    def __init__(self, arrays, landing, aliases, n_sems, start, finish, sibling_only=False):
        self.arrays, self.landing, self.aliases, self.n_sems = list(arrays), list(landing), dict(aliases), n_sems
        self.start, self.finish = start, finish
        self.sibling_only = sibling_only

    def enter(self):
        if self.sibling_only:
            barrier = pltpu.get_barrier_semaphore()
            sibling = (lax.axis_index("x"), lax.axis_index("y"), 1 - lax.axis_index("c"))
            pl.semaphore_signal(barrier, inc=1, device_id=sibling, device_id_type=MESH)
            pl.semaphore_wait(barrier, 1)

    def params(self, compiler_params=None):
        kw = dict(collective_id=SIBLING_PAIR_ID) if self.sibling_only else {}
        if compiler_params is None:
            return pltpu.CompilerParams(**kw)
        return dataclasses.replace(compiler_params, **kw)


def _call(body, args, *, name, grid, in_specs, out_specs, out_shape, scratch_shapes=(), compiler_params=None,
          exchange=None, free=(), after=()):
    args = [a if i in free else pltpu.with_memory_space_constraint(a, pltpu.HBM) for i, a in enumerate(args)]
    if exchange is None:
        n_in = len(in_specs)

        def plain(*refs):
            body(*refs[:n_in], *refs[n_in + len(after):])

        return pl.pallas_call(plain, name=name, grid=grid, in_specs=list(in_specs) + [_ANY] * len(after),
                              out_specs=out_specs, out_shape=out_shape, scratch_shapes=list(scratch_shapes),
                              compiler_params=compiler_params)(*args, *after)
    ex = exchange
    n_in, n_out, n_scr = len(in_specs), len(out_specs), len(scratch_shapes)
    na, nl = len(ex.arrays), len(ex.landing)

    def hosted(*refs):
        at = [0]

        def take(n):
            at[0] += n
            return refs[at[0] - n:at[0]]

        k_in, _, e_in, k_out, e_out, k_scr = take(n_in), take(len(after)), take(na), take(n_out), take(nl), take(n_scr)
        ssem, rsem = take(2)
        ids = [pl.program_id(a) for a in range(len(grid))]
        first = functools.reduce(jnp.logical_and, [i == 0 for i in ids])
        last = functools.reduce(jnp.logical_and, [i == g - 1 for i, g in zip(ids, grid)])

        @pl.when(first)
        def _():
            ex.enter()
            ex.start(e_in, e_out, ssem, rsem)

        body(*k_in, *k_out, *k_scr)

        @pl.when(last)
        def _():
            ex.finish(e_in, e_out, ssem, rsem)

    outs = pl.pallas_call(
        hosted, name=name, grid=grid,
        in_specs=list(in_specs) + [_ANY] * (len(after) + na), out_specs=list(out_specs) + [_ANY] * nl,
        out_shape=list(out_shape) + ex.landing,
        scratch_shapes=list(scratch_shapes) + [pltpu.SemaphoreType.DMA((ex.n_sems,))] * 2,
        input_output_aliases={n_in + len(after) + i: n_out + j for i, j in ex.aliases.items()},
        compiler_params=ex.params(compiler_params),
    )(*args, *after, *_in_hbm(ex.arrays))
    return outs[:n_out], outs[n_out:]


def _exchange_alone(ex, name, after=()):
    na, nl = len(ex.arrays), len(ex.landing)

    def body(*refs):
        outs = refs[na + len(after):na + len(after) + nl]
        ex.enter()
        ex.start(refs[:na], outs, refs[-2], refs[-1])
        ex.finish(refs[:na], outs, refs[-2], refs[-1])

    return pl.pallas_call(
        body, name=name, in_specs=[_ANY] * (na + len(after)), out_specs=[_ANY] * nl,
        out_shape=ex.landing, scratch_shapes=[pltpu.SemaphoreType.DMA((ex.n_sems,))] * 2,
        input_output_aliases=ex.aliases, compiler_params=ex.params(),
    )(*_in_hbm(ex.arrays), *after)


_HBM = pl.BlockSpec(memory_space=pltpu.HBM)
_SEM = pl.BlockSpec(memory_space=pltpu.SEMAPHORE)
_EFFECT = pltpu.SideEffectType.DATAFLOW_SIDE_EFFECTING


def _scatter_copies(srcs, lands, ssems, rsems):
    x, y, c, chips = _place()
    return [_remote(srcs[w].at[2 * px + py], lands[w].at[k], ssems[3 * w + k], rsems[3 * w + k], (px, py, c))
            for w in range(len(srcs)) for k, (px, py) in enumerate(chips)]


def _scatter_start(parts, name):
    parts = list(parts)
    n, ncp = len(parts), 3 * len(parts)
    lands = [lax.empty((3,) + p.shape[1:], p.dtype) for p in parts]

    def body(*refs):
        srcs, land_refs = refs[:n], refs[n:2 * n]
        ssems, rsems = refs[2 * n:2 * n + ncp], refs[2 * n + ncp:2 * n + 2 * ncp]
        for cp in _scatter_copies(srcs, land_refs, ssems, rsems):
            cp.start()
        token = refs[-1]
        token[...] = jnp.zeros_like(token)

    outs = pl.pallas_call(
        body, name=name,
        out_shape=([pltpu.SemaphoreType.DMA(())] * (2 * ncp) + [pltpu.HBM(a.shape, a.dtype) for a in parts + lands]
                   + [jax.ShapeDtypeStruct((8, 128), F32)]),
        in_specs=[_HBM] * (2 * n), out_specs=[_SEM] * (2 * ncp) + [_HBM] * (2 * n) + [_VM],
        input_output_aliases={i: 2 * ncp + i for i in range(2 * n)},
        compiler_params=pltpu.CompilerParams(has_side_effects=_EFFECT),
    )(*_in_hbm(parts), *_in_hbm(lands))
    sems, thru, token = outs[:2 * ncp], outs[2 * ncp:2 * ncp + 2 * n], outs[-1]
    return sems, thru, token


def _scatter_wait(sems, thru, after, name):
    n = len(thru) // 2
    ncp = 3 * n

    def body(*refs):
        srcs, land_refs = refs[:n], refs[n:2 * n]
        ssems, rsems = refs[2 * n:2 * n + ncp], refs[2 * n + ncp:2 * n + 2 * ncp]
        for cp in _scatter_copies(srcs, land_refs, ssems, rsems):
            cp.wait_send()
            cp.wait_recv()

    outs = pl.pallas_call(
        body, name=name, out_shape=[pltpu.HBM(a.shape, a.dtype) for a in thru],
        in_specs=[_HBM] * (2 * n) + [_SEM] * (2 * ncp) + [_ANY] * len(after), out_specs=[_HBM] * (2 * n),
        input_output_aliases={i: i for i in range(2 * n)},
        compiler_params=pltpu.CompilerParams(has_side_effects=_EFFECT),
    )(*thru, *sems, *after)
    return outs[:n], outs[n:]


def _swap_copies(srcs, lands, ssems, rsems):
    x, y, c, _ = _place()
    return [_remote(srcs[w].at[:, 1 - c], lands[w], ssems[w], rsems[w], (x, y, 1 - c)) for w in range(len(srcs))]


def _swap_start(grads, name):
    grads = list(grads)
    n = len(grads)
    lands = [lax.empty((NSH,) + g.shape[2:], g.dtype) for g in grads]

    def body(*refs):
        barrier = pltpu.get_barrier_semaphore()
        sibling = (lax.axis_index("x"), lax.axis_index("y"), 1 - lax.axis_index("c"))
        pl.semaphore_signal(barrier, inc=1, device_id=sibling, device_id_type=MESH)
        pl.semaphore_wait(barrier, 1)
        for cp in _swap_copies(refs[:n], refs[n:2 * n], refs[2 * n:3 * n], refs[3 * n:4 * n]):
            cp.start()
        refs[-1][...] = jnp.zeros_like(refs[-1])

    outs = pl.pallas_call(
        body, name=name,
        out_shape=([pltpu.SemaphoreType.DMA(())] * (2 * n) + [pltpu.HBM(a.shape, a.dtype) for a in grads + lands]
                   + [jax.ShapeDtypeStruct((8, 128), F32)]),
        in_specs=[_HBM] * (2 * n), out_specs=[_SEM] * (2 * n) + [_HBM] * (2 * n) + [_VM],
        input_output_aliases={i: 2 * n + i for i in range(2 * n)},
        compiler_params=pltpu.CompilerParams(has_side_effects=_EFFECT, collective_id=SIBLING_PAIR_ID),
    )(*_in_hbm(grads), *_in_hbm(lands))
    return outs[:2 * n], outs[2 * n:4 * n], outs[-1]


def _swap_wait(sems, thru, after, name):
    n = len(thru) // 2

    def body(*refs):
        for cp in _swap_copies(refs[:n], refs[n:2 * n], refs[2 * n:3 * n], refs[3 * n:4 * n]):
            cp.wait_send()
            cp.wait_recv()

    outs = pl.pallas_call(
        body, name=name, out_shape=[pltpu.HBM(a.shape, a.dtype) for a in thru],
        in_specs=[_HBM] * (2 * n) + [_SEM] * (2 * n) + [_ANY] * len(after), out_specs=[_HBM] * (2 * n),
        input_output_aliases={i: i for i in range(2 * n)},
        compiler_params=pltpu.CompilerParams(has_side_effects=_EFFECT),
    )(*thru, *sems, *after)
    return outs[:n], outs[n:]


def _share_copies(bufs, ssems, rsems, sending):
    x, y, c, _ = _place()
    out = []
    for w, ref in enumerate(bufs):
        slot = ref.at[c if sending else 1 - c]
        out.append(_remote(slot, slot, ssems[w], rsems[w], (x, y, 1 - c)))
    return out


def _share_start(bufs, after, name):
    bufs = list(bufs)
    n = len(bufs)

    def body(*refs):
        barrier = pltpu.get_barrier_semaphore()
        sibling = (lax.axis_index("x"), lax.axis_index("y"), 1 - lax.axis_index("c"))
        pl.semaphore_signal(barrier, inc=1, device_id=sibling, device_id_type=MESH)
        pl.semaphore_wait(barrier, 1)
        at = n + len(after)
        for cp in _share_copies(refs[:n], refs[at:at + n], refs[at + n:at + 2 * n], True):
            cp.start()
        refs[-1][...] = jnp.zeros_like(refs[-1])

    outs = pl.pallas_call(
        body, name=name,
        out_shape=([pltpu.SemaphoreType.DMA(())] * (2 * n) + [pltpu.HBM(a.shape, a.dtype) for a in bufs]
                   + [jax.ShapeDtypeStruct((8, 128), F32)]),
        in_specs=[_HBM] * n + [_ANY] * len(after), out_specs=[_SEM] * (2 * n) + [_HBM] * n + [_VM],
        input_output_aliases={i: 2 * n + i for i in range(n)},
        compiler_params=pltpu.CompilerParams(has_side_effects=_EFFECT, collective_id=SIBLING_PAIR_ID),
    )(*_in_hbm(bufs), *after)
    return outs[:2 * n], outs[2 * n:3 * n], outs[-1]


def _share_wait(sems, thru, after, name):
    n = len(thru)

    def body(*refs):
        for cp in _share_copies(refs[:n], refs[n:2 * n], refs[2 * n:3 * n], True):
            cp.wait_send()
        for cp in _share_copies(refs[:n], refs[n:2 * n], refs[2 * n:3 * n], False):
            cp.wait_recv()

    return pl.pallas_call(
        body, name=name, out_shape=[pltpu.HBM(a.shape, a.dtype) for a in thru],
        in_specs=[_HBM] * n + [_SEM] * (2 * n) + [_ANY] * len(after), out_specs=[_HBM] * n,
        input_output_aliases={i: i for i in range(n)},
        compiler_params=pltpu.CompilerParams(has_side_effects=_EFFECT),
    )(*thru, *sems, *after)


def _gather_copies(bufs, ssems, rsems, sending):
    x, y, c, chips = _place()
    out = []
    for w, ref in enumerate(bufs):
        half = ref.shape[1] // 2
        for k, (px, py) in enumerate(chips):
            rows = ref.at[2 * x + y if sending else 2 * px + py, pl.ds(c * half, half)]
            out.append(_remote(rows, rows, ssems[3 * w + k], rsems[3 * w + k], (px, py, c)))
    return out


def _gather_start(bufs, after, name):
    n, ncp = len(bufs), 3 * len(bufs)

    def body(*refs):
        ssems, rsems = refs[n + len(after):n + len(after) + ncp], refs[n + len(after) + ncp:n + len(after) + 2 * ncp]
        for cp in _gather_copies(refs[:n], ssems, rsems, True):
            cp.start()
        token = refs[-1]
        token[...] = jnp.zeros_like(token)

    outs = pl.pallas_call(
        body, name=name,
        out_shape=([pltpu.SemaphoreType.DMA(())] * (2 * ncp) + [pltpu.HBM(a.shape, a.dtype) for a in bufs]
                   + [jax.ShapeDtypeStruct((8, 128), F32)]),
        in_specs=[_HBM] * n + [_ANY] * len(after), out_specs=[_SEM] * (2 * ncp) + [_HBM] * n + [_VM],
        input_output_aliases={i: 2 * ncp + i for i in range(n)},
        compiler_params=pltpu.CompilerParams(has_side_effects=_EFFECT),
    )(*_in_hbm(bufs), *after)
    return outs[:2 * ncp], outs[2 * ncp:2 * ncp + n], outs[-1]


def _gather_wait(sems, thru, after, name):
    n = len(thru)
    ncp = 3 * n

    def body(*refs):
        ssems, rsems = refs[n:n + ncp], refs[n + ncp:n + 2 * ncp]
        for cp in _gather_copies(refs[:n], ssems, rsems, True):
            cp.wait_send()
        for cp in _gather_copies(refs[:n], ssems, rsems, False):
            cp.wait_recv()

    return pl.pallas_call(
        body, name=name, out_shape=[pltpu.HBM(a.shape, a.dtype) for a in thru],
        in_specs=[_HBM] * n + [_SEM] * (2 * ncp) + [_ANY] * len(after), out_specs=[_HBM] * n,
        input_output_aliases={i: i for i in range(n)},
        compiler_params=pltpu.CompilerParams(has_side_effects=_EFFECT),
    )(*thru, *sems, *after)


def _rms(x):
    r = lax.rsqrt(jnp.mean(x * x, axis=-1, keepdims=True) + EPS)
    return r, x * r


def _rms_bwd(dn, xr, r, gain):
    dng = dn * gain
    dx = r * (dng - xr * jnp.mean(dng * xr, axis=-1, keepdims=True))
    return dx, jnp.sum(dn * xr, axis=0, keepdims=True)


def _ffn_fwd(x, gain, wgu, wd, name, exchange=None, head=None):
    tm = 256

    def body(x_ref, g_ref, wgu_hbm, wd_hbm, *rest):
        if head is None:
            h_ref, n_ref, gu_ref, a_ref, wgu_ref, wd_ref = rest
        else:
            t_ref, gf_ref, h_ref, loss_ref, dgf_ref, n_ref, gu_ref, a_ref, wgu_ref, wd_ref = rest
        _stage([(wgu_hbm, wgu_ref), (wd_hbm, wd_ref)])
        x = x_ref[...]
        _, xr = _rms(x)
        n = (xr * g_ref[...]).astype(BF16)
        n_ref[...] = n
        acc = jnp.zeros((tm, D), F32)
        for c0, cn in FF_CHUNKS:
            g = _nn(n, wgu_ref[:, c0:c0 + cn])
            u = _nn(n, wgu_ref[:, DFF + c0:DFF + c0 + cn])
            gu_ref[:, c0:c0 + cn] = g.astype(BF16)
            gu_ref[:, DFF + c0:DFF + c0 + cn] = u.astype(BF16)
            half_act = (0.5 * (g * jax.nn.sigmoid(g) * u)).astype(BF16)
            a_ref[:, c0:c0 + cn] = half_act
            acc = acc + _nn(half_act, wd_ref[c0:c0 + cn, :])
        h = x + acc
        if head is None:
            h_ref[...] = h
            return
        gf = gf_ref[...]
        r, hr = _rms(h)
        err = hr * gf - t_ref[...]
        dh, dgain = _rms_bwd(err * (1.0 / D), hr, r, gf)
        h_ref[...] = dh

        @pl.when(pl.program_id(0) == 0)
        def _():
            dgf_ref[...] = jnp.zeros_like(dgf_ref)
            loss_ref[...] = jnp.zeros_like(loss_ref)

        dgf_ref[...] += dgain
        loss_ref[...] += jnp.full((1, 128), (0.5 / D) * jnp.sum(err * err), F32)

    saved_specs = [_rows(tm, D), _rows(tm, 4 * FFS), _rows(tm, DFF)]
    saved_shapes = [_sds((S, D), BF16), _sds((S, 4 * FFS), BF16), _sds((S, DFF), BF16)]
    if head is None:
        return _call(
            body, (x, gain, wgu, wd), name=name, grid=(S // tm,),
            in_specs=[_rows(tm, D), _fixed((1, D)), _ANY, _ANY],
            out_specs=[_rows(tm, D)] + saved_specs, out_shape=[_sds((S, D), F32)] + saved_shapes,
            scratch_shapes=[_vmem_wide(wgu)] + _vmem_like(wd),
            compiler_params=_params(("arbitrary",), 56), exchange=exchange)
    return _call(
        body, (x, gain, wgu, wd, *head), name=name, grid=(S // tm,),
        in_specs=[_rows(tm, D), _fixed((1, D)), _ANY, _ANY, _rows(tm, D), _fixed((1, D))],
        out_specs=[_rows(tm, D), _fixed((1, 128)), _fixed((1, D))] + saved_specs,
        out_shape=[_sds((S, D), F32), _sds((1, 128), F32), _sds((1, D), F32)] + saved_shapes,
        scratch_shapes=[_vmem_wide(wgu)] + _vmem_like(wd),
        compiler_params=_params(("arbitrary",), 56), exchange=exchange, free=(4, 5))


def _ffn_bwd(dh, x, gain, gu, wgu, wd, name):
    tm = 256

    def body(dh_ref, x_ref, g_ref, gu_ref, wgu_hbm, wd_hbm, dx_ref, dgu_ref, dg_ref, wgu_ref, wd_ref):
        _stage([(wgu_hbm, wgu_ref), (wd_hbm, wd_ref)])
        dh = dh_ref[...]
        dhb = dh.astype(BF16)
        dn = jnp.zeros((tm, D), F32)
        for c0, cn in FF_CHUNKS:
            g = gu_ref[:, c0:c0 + cn].astype(F32)
            u = gu_ref[:, DFF + c0:DFF + c0 + cn].astype(F32)
            da = 0.5 * _nt(dhb, wd_ref[c0:c0 + cn, :])
            sg = jax.nn.sigmoid(g)
            dgb = (da * u * (sg * (1.0 + g * (1.0 - sg)))).astype(BF16)
            dub = (da * (g * sg)).astype(BF16)
            dgu_ref[:, c0:c0 + cn] = dgb
            dgu_ref[:, DFF + c0:DFF + c0 + cn] = dub
            dn = dn + _nt(dgb, wgu_ref[:, c0:c0 + cn]) + _nt(dub, wgu_ref[:, DFF + c0:DFF + c0 + cn])
        r, xr = _rms(x_ref[...])
        dx, dgain = _rms_bwd(dn, xr, r, g_ref[...])
        dx_ref[...] = dh + dx

        @pl.when(pl.program_id(0) == 0)
        def _():
            dg_ref[...] = jnp.zeros_like(dg_ref)

        dg_ref[...] += dgain

    return _call(
        body, (dh, x, gain, gu, wgu, wd), name=name, grid=(S // tm,),
        in_specs=[_rows(tm, D), _rows(tm, D), _fixed((1, D)), _rows(tm, 4 * FFS), _ANY, _ANY],
        out_specs=[_rows(tm, D), _rows(tm, 4 * FFS), _fixed((1, D))],
        out_shape=[_sds((S, D), F32), _sds((S, 4 * FFS), BF16), _sds((1, D), F32)],
        scratch_shapes=[_vmem_wide(wgu)] + _vmem_like(wd), compiler_params=_params(("arbitrary",), 56))


def _ffn_bwd_act(dh, gu, wd, name, exchange=None, after=()):
    tm = 512

    def body(dh_ref, gu_ref, wd_hbm, dgu_ref, wd_ref):
        _stage([(wd_hbm, wd_ref)])
        dhb = dh_ref[...].astype(BF16)
        for c0, cn in FF_CHUNKS:
            g = gu_ref[:, c0:c0 + cn].astype(F32)
            u = gu_ref[:, DFF + c0:DFF + c0 + cn].astype(F32)
            da = 0.5 * _nt(dhb, wd_ref[c0:c0 + cn, :])
            sg = jax.nn.sigmoid(g)
            dgu_ref[:, c0:c0 + cn] = (da * u * (sg * (1.0 + g * (1.0 - sg)))).astype(BF16)
            dgu_ref[:, DFF + c0:DFF + c0 + cn] = (da * (g * sg)).astype(BF16)

    res = _call(
        body, (dh, gu, wd), name=name, grid=(S // tm,),
        in_specs=[_rows(tm, D), _rows(tm, 4 * FFS), _ANY], out_specs=[_rows(tm, 4 * FFS)],
        out_shape=[_sds((S, 4 * FFS), BF16)], scratch_shapes=_vmem_like(wd),
        compiler_params=_params(("arbitrary",), 56), exchange=exchange, after=after)
    return res[0] if exchange is None else (res[0][0], res[1])


def _ffn_bwd_in(dh, x, gain, dgu, wgu, name, exchange=None, after=()):
    tm = 512

    def body(dh_ref, x_ref, g_ref, dgu_ref, wgu_hbm, dx_ref, dg_ref, wgu_ref):
        _stage([(wgu_hbm, wgu_ref)])
        dn = jnp.zeros((tm, D), F32)
        for half in (0, DFF):
            for c0, cn in FF_CHUNKS:
                cols = slice(half + c0, half + c0 + cn)
                dn = dn + _nt(dgu_ref[:, cols], wgu_ref[:, cols])
        r, xr = _rms(x_ref[...])
        dx, dgain = _rms_bwd(dn, xr, r, g_ref[...])
        dx_ref[...] = dh_ref[...] + dx

        @pl.when(pl.program_id(0) == 0)
        def _():
            dg_ref[...] = jnp.zeros_like(dg_ref)

        dg_ref[...] += dgain

    return _call(
        body, (dh, x, gain, dgu, wgu), name=name, grid=(S // tm,),
        in_specs=[_rows(tm, D), _rows(tm, D), _fixed((1, D)), _rows(tm, 4 * FFS), _ANY],
        out_specs=[_rows(tm, D), _fixed((1, D))],
        out_shape=[_sds((S, D), F32), _sds((1, D), F32)],
        scratch_shapes=[_vmem_wide(wgu)],
        compiler_params=_params(("arbitrary",), 56), exchange=exchange, after=after)


def _mix_in(h, gain, w_in, after=()):
    tm = 512

    def body(h_ref, g_ref, w_hbm, u_ref, xp_ref, q_ref, k_ref, v_ref, gp_ref, gs_ref, w_ref):
        _stage([(w_hbm, w_ref)])
        _, hr = _rms(h_ref[...])
        u = (hr * g_ref[...]).astype(BF16)
        u_ref[...] = u
        p0 = _nn(u, w_ref[0])
        xp_ref[...] = p0[:, :PW]
        q_ref[...] = p0[:, PW:].astype(BF16)
        p1 = _nn(u, w_ref[1])
        k_ref[...] = p1[:, :SBW].astype(BF16)
        v_ref[...] = p1[:, SBW:].astype(BF16)
        gp_ref[...] = jax.nn.sigmoid(_nn(u, w_ref[2])).astype(BF16)
        gs_ref[...] = jax.nn.sigmoid(_nn(u, w_ref[3])).astype(BF16)

    return _call(
        body, (h, gain, w_in), name="mix_in", grid=(S // tm,),
        in_specs=[_rows(tm, D), _fixed((1, D)), _ANY],
        out_specs=[_rows(tm, D), _rows(tm, PW), _rows(tm, SBW), _rows(tm, SBW), _rows(tm, SBW),
                   _rows(tm, D), _rows(tm, D)],
        out_shape=[_sds((S, D), BF16), _sds((S, PW), F32), _sds((S, SBW), BF16), _sds((S, SBW), BF16),
                   _sds((S, SBW), BF16), _sds((S, D), BF16), _sds((S, D), BF16)],
        scratch_shapes=_vmem_like(w_in),
        compiler_params=_params(("arbitrary",), 48), free=(1,), after=after)


def _hilo_dot(x, tri):
    hi = x.astype(BF16)
    lo = (x - hi.astype(F32)).astype(BF16)
    return _nn(hi, tri) + _nn(lo, tri)


def _log_terms(qk):
    z2 = qk * (SCALE * LOG2E)
    lb = jnp.minimum(z2, 0.0) - jnp.log2(1.0 + jnp.exp2(-jnp.abs(z2)))
    return lb, lb - z2


def _head_masks():
    lane = lax.broadcasted_iota(jnp.int32, (1, 2 * DH), 1)
    return (lane < DH, lane >= DH)


def _attn_fwd(q, k, v, exchange=None):
    T = TA

    def body(q_ref, k_ref, v_ref, o_ref, c_ref):
        i2 = 2 * pl.program_id(1)
        row = lax.broadcasted_iota(jnp.int32, (T, T), 0)
        col = lax.broadcasted_iota(jnp.int32, (T, T), 1)
        after = (row > col).astype(BF16)
        causal = col < row
        masks = _head_masks()
        qms = {}
        for b in range(QB):
            q2 = q_ref[b * T:(b + 1) * T, :]
            for h, hm in enumerate(masks):
                qms[b, h] = jnp.where(hm, q2, jnp.zeros_like(q2))

        def blocks(keys, pairs, carries, os):
            ks, vms = [], []
            for j in keys:
                rows = pl.ds(pl.multiple_of(j * T, T), T)
                vj = v_ref[rows, :]
                ks.append(k_ref[rows, :])
                vms.append([jnp.where(hm, vj, jnp.zeros_like(vj)) for hm in masks])
            units = [(n, h) for n in range(len(pairs)) for h in range(2)]
            qks = {(n, h): _nt(qms[pairs[n][0], h], ks[pairs[n][1]]) for n, h in units}
            lbs, l1ms = {}, {}
            for u in units:
                lbs[u], l1m = _log_terms(qks[u])
                l1ms[u] = jnp.where(causal, l1m, 0.0) if pairs[u[0]][2] else l1m
            cins = {u: _hilo_dot(l1ms[u], after) for u in units}
            carries, os = dict(carries), list(os)
            for n, h in units:
                b, key, diag = pairs[n]
                a = jnp.exp2(lbs[n, h] + cins[n, h] + carries[b, h])
                if diag:
                    a = jnp.where(causal, a, 0.0)
                os[b] = os[b] + _nn(a.astype(BF16), vms[key][h])
                carries[b, h] = carries[b, h] + jnp.sum(l1ms[n, h], axis=1, keepdims=True)
            return carries, tuple(os)

        carries = {(b, h): jnp.zeros((T, 1), F32) for b in range(QB) for h in range(2)}
        os = tuple(jnp.zeros((T, 2 * DH), F32) for _ in range(QB))
        carries, os = blocks([i2 + 1, i2], [(1, 0, True), (0, 1, True), (1, 1, False)], carries, os)
        carries, os = lax.fori_loop(
            0, i2 // 2,
            lambda t, c: blocks([i2 - 1 - 2 * t, i2 - 2 - 2 * t],
                                [(0, 0, False), (1, 0, False), (0, 1, False), (1, 1, False)], c[0], c[1]),
            (carries, os))
        for b in range(QB):
            o_ref[b * T:(b + 1) * T, :] = os[b].astype(BF16)
            c_ref[b * T:(b + 1) * T, :] = jnp.where(masks[0], carries[b, 0], carries[b, 1])

    blk = pl.BlockSpec((QB * T, 2 * DH), lambda p, i: (i, p))
    full = pl.BlockSpec((S, 2 * DH), lambda p, i: (0, p))
    return _call(
        body, (q, k, v), name="attn_fwd", grid=(SBW // (2 * DH), S // (QB * T)),
        in_specs=[blk, full, full], out_specs=[blk, blk],
        out_shape=[_sds((S, SBW), BF16), _sds((S, SBW), F32)],
        compiler_params=_params(("arbitrary", "arbitrary"), 40), exchange=exchange)


def _attn_bwd(q, k, v, do, ctot, after=()):
    T = TA
    nq = S // (QB * T)

    def body(q_ref, k_ref, v_ref, do_ref, c_ref, dq_ref, dk_ref, dv_ref, dk_acc, dv_acc):
        step = pl.program_id(1)
        i2 = 2 * step

        @pl.when(step == 0)
        def _():
            dk_acc[...] = jnp.zeros_like(dk_acc)
            dv_acc[...] = jnp.zeros_like(dv_acc)

        row = lax.broadcasted_iota(jnp.int32, (T, T), 0)
        col = lax.broadcasted_iota(jnp.int32, (T, T), 1)
        upto = (row <= col).astype(BF16)
        before = (row < col).astype(BF16)
        causal = col < row
        masks = _head_masks()
        qms, doms, ctots = {}, {}, {}
        for b in range(QB):
            q2, do2 = q_ref[b * T:(b + 1) * T, :], do_ref[b * T:(b + 1) * T, :]
            for h, hm in enumerate(masks):
                qms[b, h] = jnp.where(hm, q2, jnp.zeros_like(q2))
                doms[b, h] = jnp.where(hm, do2, jnp.zeros_like(do2))
                ctots[b, h] = c_ref[b * T:(b + 1) * T, h * DH:h * DH + 1]

        def blocks(keys, pairs, sums, dqs):
            rows = [pl.ds(pl.multiple_of(j * T, T), T) for j in keys]
            ks, vs = [k_ref[r, :] for r in rows], [v_ref[r, :] for r in rows]
            kms = [[jnp.where(hm, kj, jnp.zeros_like(kj)) for hm in masks] for kj in ks]
            units = [(n, h) for n in range(len(pairs)) for h in range(2)]
            qks = {(n, h): _nt(qms[pairs[n][0], h], ks[pairs[n][1]]) for n, h in units}
            das = {(n, h): _nt(doms[pairs[n][0], h], vs[pairs[n][1]]) for n, h in units}
            lbs, l1ms = {}, {}
            for u in units:
                lbs[u], l1m = _log_terms(qks[u])
                l1ms[u] = jnp.where(causal, l1m, 0.0) if pairs[u[0]][2] else l1m
            pins = {u: _hilo_dot(l1ms[u], upto) for u in units}
            sums = dict(sums)
            a_s, dls, cps = {}, {}, {}
            for n, h in units:
                b, _, diag = pairs[n]
                cl, cp = sums[b, h]
                a = jnp.exp2(lbs[n, h] + (ctots[b, h] - cl) - pins[n, h])
                if diag:
                    a = jnp.where(causal, a, 0.0)
                a_s[n, h] = a.astype(BF16)
                dls[n, h] = das[n, h] * a
                cps[n, h] = cp
                sums[b, h] = (cl + jnp.sum(l1ms[n, h], axis=1, keepdims=True),
                              cp + jnp.sum(dls[n, h], axis=1, keepdims=True))
            pexs = {u: _hilo_dot(dls[u], before) for u in units}
            dzbs = {}
            for u in units:
                dz = dls[u] - jnp.exp2(lbs[u]) * (dls[u] + pexs[u] + cps[u])
                if pairs[u[0]][2]:
                    dz = jnp.where(causal, dz, 0.0)
                dzbs[u] = dz.astype(BF16)
            dqs = list(dqs)
            for n, h in units:
                dqs[pairs[n][0]] = dqs[pairs[n][0]] + _nn(dzbs[n, h], kms[pairs[n][1]][h])
            for key, r in enumerate(rows):
                mine = [(n, h) for n, h in units if pairs[n][1] == key]
                dk_acc[r, :] += functools.reduce(jnp.add, [_tn(dzbs[u], qms[pairs[u[0]][0], u[1]]) for u in mine])
                dv_acc[r, :] += functools.reduce(jnp.add, [_tn(a_s[u], doms[pairs[u[0]][0], u[1]]) for u in mine])
            return sums, tuple(dqs)

        zero = jnp.zeros((T, 1), F32)
        sums = {(b, h): (zero, zero) for b in range(QB) for h in range(2)}
        dqs = tuple(jnp.zeros((T, 2 * DH), F32) for _ in range(QB))
        sums, dqs = lax.fori_loop(
            0, i2 // 2,
            lambda t, c: blocks([2 * t, 2 * t + 1],
                                [(0, 0, False), (1, 0, False), (0, 1, False), (1, 1, False)], c[0], c[1]),
            (sums, dqs))
        _, dqs = blocks([i2, i2 + 1], [(0, 0, True), (1, 0, False), (1, 1, True)], sums, dqs)
        for b in range(QB):
            dq_ref[b * T:(b + 1) * T, :] = (dqs[b] * SCALE).astype(BF16)

        @pl.when(step == nq - 1)
        def _():
            dk_ref[...] = (dk_acc[...] * SCALE).astype(BF16)
            dv_ref[...] = dv_acc[...].astype(BF16)

    blk = pl.BlockSpec((QB * T, 2 * DH), lambda p, i: (i, p))
    full = pl.BlockSpec((S, 2 * DH), lambda p, i: (0, p))
    return _call(
        body, (q, k, v, do, ctot), name="attn_bwd", grid=(SBW // (2 * DH), nq),
        in_specs=[blk, full, full, blk, blk], out_specs=[blk, full, full],
        out_shape=[_sds((S, SBW), BF16), _sds((S, SBW), BF16), _sds((S, SBW), BF16)],
        scratch_shapes=[pltpu.VMEM((S, 2 * DH), F32), pltpu.VMEM((S, 2 * DH), F32)],
        compiler_params=_params(("arbitrary", "arbitrary"), 40), after=after)


def _pool_counts(first_row, tm):
    pos = first_row + lax.broadcasted_iota(jnp.int32, (tm, 1), 0)
    return [jnp.minimum(pos + 1, w).astype(F32) for w in POOL_WINDOWS]


def _mix_out(h, xp, o_sb, gp, gs, w_group, scale, w_bp, w_ba, w_out, exchange=None):
    tm = 512

    def body(h_ref, xp_ref, o_ref, gp_ref, gs_ref, wg_hbm, sc_ref, wbp_hbm, wba_hbm, wo_hbm,
             h2_ref, pm_ref, p_ref, yp_ref, ys_ref, m_ref, halo, wg_ref, wbp_ref, wba_ref, wo_ref):
        _stage([(wg_hbm, wg_ref), (wbp_hbm, wbp_ref), (wba_hbm, wba_ref), (wo_hbm, wo_ref)])
        i = pl.program_id(0)

        @pl.when(i == 0)
        def _():
            halo[...] = jnp.zeros_like(halo)

        xp = xp_ref[...]
        ext = jnp.concatenate([halo[...], xp], axis=0)
        halo[...] = xp[tm - HALO:, :]
        counts = _pool_counts(i * tm, tm)
        for gi in range(len(POOL_WINDOWS)):
            lanes = slice(gi * PG, (gi + 1) * PG)
            win = ext[:, lanes]
            for step in range(gi + 1):
                win = win + pltpu.roll(win, 1 << step, 0)
            pm = (win[HALO:, :] / counts[gi] - xp[:, lanes]).astype(BF16)
            pm_ref[:, lanes] = pm
            p_ref[:, lanes] = (_nn(pm, wg_ref[gi]) * sc_ref[:, lanes]).astype(BF16)
        pb = p_ref[...]
        ob = o_ref[...]
        for j in range(NSH):
            cols = slice(j * (D // NSH), (j + 1) * (D // NSH))
            yp = _nn(pb, wbp_ref[j])
            ys = _nn(ob, wba_ref[j])
            yp_ref[:, cols] = yp.astype(BF16)
            ys_ref[:, cols] = ys.astype(BF16)
            m_ref[:, cols] = (gp_ref[:, cols].astype(F32) * yp + gs_ref[:, cols].astype(F32) * ys).astype(BF16)
        h2_ref[...] = h_ref[...] + _nn(m_ref[...], wo_ref[...])

    return _call(
        body, (h, xp, o_sb, gp, gs, w_group, scale, w_bp, w_ba, w_out), name="mix_out", grid=(S // tm,),
        in_specs=[_rows(tm, D), _rows(tm, PW), _rows(tm, SBW), _rows(tm, D), _rows(tm, D),
                  _ANY, _fixed((1, PW)), _ANY, _ANY, _ANY],
        out_specs=[_rows(tm, D), _rows(tm, PW), _rows(tm, PW), _rows(tm, D), _rows(tm, D), _rows(tm, D)],
        out_shape=[_sds((S, D), F32), _sds((S, PW), BF16), _sds((S, PW), BF16), _sds((S, D), BF16),
                   _sds((S, D), BF16), _sds((S, D), BF16)],
        scratch_shapes=[pltpu.VMEM((HALO, PW), F32)] + _vmem_like(w_group, w_bp, w_ba, w_out),
        compiler_params=_params(("arbitrary",), 48), free=(5, 6), exchange=exchange)


def _mix_bwd_out(dh, gp, gs, yp, ys, pm, w_group, scale, w_bp, w_ba, w_out, exchange=None):
    tm = 512
    nt = S // tm

    def body(dh_ref, gp_ref, gs_ref, yp_ref, ys_ref, pm_ref, wg_hbm, sc_ref, wbp_hbm, wba_hbm, wo_hbm,
             dlg_ref, dyp_ref, dys_ref, do_ref, dyg_ref, dxp_ref, dsc_ref, halo, wg_ref, wbp_ref, wba_ref, wo_ref):
        _stage([(wg_hbm, wg_ref), (wbp_hbm, wbp_ref), (wba_hbm, wba_ref), (wo_hbm, wo_ref)])
        step = pl.program_id(0)

        @pl.when(step == 0)
        def _():
            halo[...] = jnp.zeros_like(halo)
            dsc_ref[...] = jnp.zeros_like(dsc_ref)

        dm = _nt(dh_ref[...].astype(BF16), wo_ref[...])
        gp = gp_ref[...].astype(F32)
        gs = gs_ref[...].astype(F32)
        yp = yp_ref[...].astype(F32)
        ys = ys_ref[...].astype(F32)
        dlg_ref[:, :D] = (dm * yp * gp * (1.0 - gp)).astype(BF16)
        dlg_ref[:, D:] = (dm * ys * gs * (1.0 - gs)).astype(BF16)
        dyp_ref[...] = (dm * gp).astype(BF16)
        dys_ref[...] = (dm * gs).astype(BF16)
        dp = jnp.zeros((tm, PW), F32)
        do = jnp.zeros((tm, SBW), F32)
        for j in range(NSH):
            cols = slice(j * (D // NSH), (j + 1) * (D // NSH))
            dp = dp + _nt(dyp_ref[:, cols], wbp_ref[j])
            do = do + _nt(dys_ref[:, cols], wba_ref[j])
        do_ref[...] = do.astype(BF16)
        counts = _pool_counts((nt - 1 - step) * tm, tm)
        dscale = []
        for gi in range(len(POOL_WINDOWS)):
            lanes = slice(gi * PG, (gi + 1) * PG)
            dpg = dp[:, lanes]
            dscale.append(jnp.sum(dpg * _nn(pm_ref[:, lanes], wg_ref[gi]), axis=0, keepdims=True))
            dyg = (dpg * sc_ref[:, lanes]).astype(BF16)
            dyg_ref[:, lanes] = dyg
            dpm = _nt(dyg, wg_ref[gi])
            per = dpm / counts[gi]
            win = jnp.concatenate([per, halo[:, lanes]], axis=0)
            halo[:, lanes] = per[:HALO, :]
            for s in range(gi + 1):
                win = win + pltpu.roll(win, tm + HALO - (1 << s), 0)
            dxp_ref[:, lanes] = (win[:tm, :] - dpm).astype(BF16)
        dsc_ref[...] += jnp.concatenate(dscale, axis=1)

    rev = lambda width: pl.BlockSpec((tm, width), lambda i: (nt - 1 - i, 0))
    return _call(
        body, (dh, gp, gs, yp, ys, pm, w_group, scale, w_bp, w_ba, w_out), name="mix_bwd_out", grid=(nt,),
        in_specs=[rev(D), rev(D), rev(D), rev(D), rev(D), rev(PW), _ANY, _fixed((1, PW)), _ANY, _ANY, _ANY],
        out_specs=[rev(2 * D), rev(D), rev(D), rev(SBW), rev(PW), rev(PW), _fixed((1, PW))],
        out_shape=[_sds((S, 2 * D), BF16), _sds((S, D), BF16), _sds((S, D), BF16), _sds((S, SBW), BF16),
                   _sds((S, PW), BF16), _sds((S, PW), BF16), _sds((1, PW), F32)],
        scratch_shapes=[pltpu.VMEM((HALO, PW), F32)] + _vmem_like(w_group, w_bp, w_ba, w_out),
        compiler_params=_params(("arbitrary",), 48), exchange=exchange)


def _mix_bwd_in(dh, h, gain, pieces, w_in, exchange=None):
    tm = 512
    widths = [p.shape[1] for p in pieces]

    def body(dh_ref, h_ref, g_ref, *rest):
        piece_refs, (w_hbm, dx_ref, dg_ref, w_ref, dp_ref) = rest[:len(pieces)], rest[len(pieces):]
        _stage([(w_hbm, w_ref)])
        at = 0
        for ref, width in zip(piece_refs, widths):
            dp_ref[:, at:at + width] = ref[...]
            at += width
        du = jnp.zeros((tm, D), F32)
        for j in range(NSH):
            du = du + _nt(dp_ref[:, j * D:(j + 1) * D], w_ref[j])
        r, hr = _rms(h_ref[...])
        dx, dgain = _rms_bwd(du, hr, r, g_ref[...])
        dx_ref[...] = dh_ref[...] + dx

        @pl.when(pl.program_id(0) == 0)
        def _():
            dg_ref[...] = jnp.zeros_like(dg_ref)

        dg_ref[...] += dgain

    return _call(
        body, (dh, h, gain, *pieces, w_in), name="mix_bwd_in", grid=(S // tm,),
        in_specs=[_rows(tm, D), _rows(tm, D), _fixed((1, D))] + [_rows(tm, w) for w in widths] + [_ANY],
        out_specs=[_rows(tm, D), _fixed((1, D))],
        out_shape=[_sds((S, D), F32), _sds((1, D), F32)],
        scratch_shapes=_vmem_like(w_in) + [pltpu.VMEM((tm, 4 * D), BF16)],
        compiler_params=_params(("arbitrary",), 48), exchange=exchange)


def _wgrad_in(u, pieces):
    dxp, dq, dk, dv, dlg = pieces

    def body(u_ref, dxp_ref, dq_ref, dk_ref, dv_ref, dlg_ref, o_ref):
        j = pl.program_id(0)
        u = u_ref[...]

        def two(left_ref, right_ref):
            o_ref[:, :PW] = _tn(u, left_ref[...]).astype(BF16)
            o_ref[:, PW:] = _tn(u, right_ref[...]).astype(BF16)

        pl.when(j == 0)(lambda: two(dxp_ref, dq_ref))
        pl.when(j == 1)(lambda: two(dk_ref, dv_ref))

        @pl.when(j >= 2)
        def _():
            o_ref[...] = _tn(u, dlg_ref[...]).astype(BF16)

    whole = lambda width: pl.BlockSpec((S, width), lambda j: (0, 0))
    return _call(
        body, (u, dxp, dq, dk, dv, dlg), name="wgrad_in", grid=(NSH,),
        in_specs=[whole(D), whole(PW), whole(SBW), whole(SBW), whole(SBW),
                  pl.BlockSpec((S, D), lambda j: (0, jnp.maximum(j - 2, 0)))],
        out_specs=[pl.BlockSpec((None, D, D), lambda j: (j, 0, 0))], out_shape=[_sds((NSH, D, D), BF16)],
        compiler_params=_params(("arbitrary",), 56))[0]


def _wgrad(a, b, nblk, ti, name, out_dtype=BF16, exchange=None, after=()):
    ka, n = a.shape[1], b.shape[1]
    ns = n // nblk

    def body(a_ref, b_ref, o_ref):
        o_ref[...] = _tn(a_ref[...].astype(BF16), b_ref[...].astype(BF16)).astype(out_dtype)

    res = _call(
        body, (a, b), name=name, grid=(nblk, ka // ti),
        in_specs=[pl.BlockSpec((S, ti), lambda j, i: (0, i)), pl.BlockSpec((S, ns), lambda j, i: (0, j))],
        out_specs=[pl.BlockSpec((None, ti, ns), lambda j, i: (j, i, 0))],
        out_shape=[_sds((nblk, ka, ns), out_dtype)],
        compiler_params=_params(("arbitrary", "arbitrary"), 56), exchange=exchange, after=after)
    return res[0] if exchange is None else (res[0][0], res[1])


def _wgrad_branches(p, dyp, o_sb, dys, pm, dyg):
    cols = D // NSH

    def body(p_ref, dyp_ref, o_ref, dys_ref, pm_ref, dyg_ref, gbp_ref, gba_ref, gg_ref):
        gbp_ref[...] = _tn(p_ref[...], dyp_ref[...]).astype(BF16)
        gba_ref[...] = _tn(o_ref[...], dys_ref[...]).astype(BF16)
        gg_ref[...] = _tn(pm_ref[...], dyg_ref[...])

    whole = lambda width: pl.BlockSpec((S, width), lambda j: (0, 0))
    col = lambda width: pl.BlockSpec((S, width), lambda j: (0, j))
    return _call(
        body, (p, dyp, o_sb, dys, pm, dyg), name="wgrad_branches", grid=(NSH,),
        in_specs=[whole(PW), col(cols), whole(SBW), col(cols), col(PG), col(PG)],
        out_specs=[pl.BlockSpec((None, PW, cols), lambda j: (j, 0, 0)),
                   pl.BlockSpec((None, SBW, cols), lambda j: (j, 0, 0)),
                   pl.BlockSpec((None, PG, PG), lambda j: (j, 0, 0))],
        out_shape=[_sds((NSH, PW, cols), BF16), _sds((NSH, SBW, cols), BF16), _sds((NSH, PG, PG), F32)],
        compiler_params=_params(("arbitrary",), 40))


def _place():
    x, y, c = lax.axis_index("x"), lax.axis_index("y"), lax.axis_index("c")
    chips = [(1 - x, y), (x, 1 - y), (1 - x, 1 - y)]
    return x, y, c, chips


def _remote(src, dst, ssem, rsem, dev):
    return pltpu.make_async_remote_copy(src_ref=src, dst_ref=dst, send_sem=ssem, recv_sem=rsem,
                                        device_id=dev, device_id_type=MESH)


def _cast_into_block(ws, me_idx, name):
    steps = 4
    shapes = [(w.shape[0] // steps, w.shape[1]) for w in ws]

    def body(me_ref, *refs):
        for w_ref, o_ref in zip(refs[:len(ws)], refs[len(ws):]):
            o_ref[...] = w_ref[...].astype(BF16)

    return pl.pallas_call(
        body, name=name, out_shape=[_sds((NSH,) + w.shape, BF16) for w in ws],
        grid_spec=pltpu.PrefetchScalarGridSpec(
            num_scalar_prefetch=1, grid=(steps,),
            in_specs=[pl.BlockSpec((r, c), lambda s, me: (s, 0)) for r, c in shapes],
            out_specs=[pl.BlockSpec((None, r, c), lambda s, me: (me[0], s, 0)) for r, c in shapes]),
        compiler_params=_params(("arbitrary",), 32),
    )(me_idx, *_in_hbm(ws))


def _ex_gather(bufs):
    n = len(bufs)
    per = 8

    def plan(outs, ssem, rsem, w):
        x, y, c, _ = _place()
        sib, nbr_x, nbr_y = (x, y, 1 - c), (1 - x, y, c), (x, 1 - y, c)
        half = outs[w].shape[1] // 2
        quarter = half // 2
        sem = lambda k: (ssem.at[per * w + k], rsem.at[per * w + k])
        rows = lambda blk, start, size: outs[w].at[blk, pl.ds(start, size)]
        mine = rows(2 * x + y, c * half, half)
        from_x = rows(2 * (1 - x) + y, c * half, half)
        from_y = rows(2 * x + (1 - y), c * half, half)
        diag = 2 * (1 - x) + (1 - y)
        pass_y = rows(2 * (1 - x) + y, c * half, quarter)
        pass_x = rows(2 * x + (1 - y), c * half + quarter, quarter)
        diag_0, diag_1 = rows(diag, c * half, quarter), rows(diag, c * half + quarter, quarter)
        first = [_remote(mine, mine, *sem(0), nbr_x), _remote(mine, mine, *sem(1), nbr_y)]
        arrivals = [
            (_remote(from_x, from_x, *sem(0), nbr_x),
             [_remote(pass_y, pass_y, *sem(2), nbr_y), _remote(from_x, from_x, *sem(4), sib)]),
            (_remote(from_y, from_y, *sem(1), nbr_y),
             [_remote(pass_x, pass_x, *sem(3), nbr_x), _remote(from_y, from_y, *sem(5), sib)]),
            (_remote(diag_0, diag_0, *sem(2), nbr_y), [_remote(diag_0, diag_0, *sem(6), sib)]),
            (_remote(diag_1, diag_1, *sem(3), nbr_x), [_remote(diag_1, diag_1, *sem(7), sib)]),
        ]
        other = (1 - c) * half
        from_sibling = [
            _remote(rows(2 * (1 - x) + y, other, half), rows(2 * (1 - x) + y, other, half), *sem(4), sib),
            _remote(rows(2 * x + (1 - y), other, half), rows(2 * x + (1 - y), other, half), *sem(5), sib),
            _remote(rows(diag, other, quarter), rows(diag, other, quarter), *sem(6), sib),
            _remote(rows(diag, other + quarter, quarter), rows(diag, other + quarter, quarter), *sem(7), sib),
        ]
        return first, arrivals, from_sibling

    def start(ins, outs, ssem, rsem):
        x, y, c, _ = _place()
        for w in range(n):
            half = outs[w].shape[1] // 2
            mine = outs[w].at[2 * x + y, pl.ds(c * half, half)]
            _remote(mine, mine, ssem.at[per * w], rsem.at[per * w], (1 - x, y, c)).start()
            _remote(mine, mine, ssem.at[per * w + 1], rsem.at[per * w + 1], (x, 1 - y, c)).start()

    def finish(ins, outs, ssem, rsem):
        plans = [plan(outs, ssem, rsem, w) for w in range(n)]
        started = []
        for direct in (True, False):
            for first, arrivals, _ in plans:
                for arrived, onward in (arrivals[:2] if direct else arrivals[2:]):
                    arrived.wait_recv()
                    for cp in onward:
                        cp.start()
                    started += onward
        for first, _, from_sibling in plans:
            for cp in from_sibling:
                cp.wait_recv()
            started += first
        for cp in started:
            cp.wait_send()

    return Exchange(bufs, [_sds(b.shape, b.dtype) for b in bufs], {w: w for w in range(n)}, per * n, start, finish)


def _ex_gather_direct(bufs):
    n = len(bufs)

    def copies(outs, ssem, rsem, only_first=False):
        x, y, c, chips = _place()
        me, sib = 2 * x + y, (x, y, 1 - c)
        first, relay, last = [], [], []
        for w in range(n):
            half = outs[w].shape[1] // 2
            mine = outs[w].at[me, pl.ds(c * half, half)]
            for k, (px, py) in enumerate(chips):
                sems = (ssem.at[6 * w + k], rsem.at[6 * w + k])
                sib_sems = (ssem.at[6 * w + 3 + k], rsem.at[6 * w + 3 + k])
                first.append(_remote(mine, mine, *sems, (px, py, c)))
                if only_first:
                    continue
                got = outs[w].at[2 * px + py, pl.ds(c * half, half)]
                relay.append((_remote(got, got, *sems, (px, py, c)), _remote(got, got, *sib_sems, sib)))
                theirs = outs[w].at[2 * px + py, pl.ds((1 - c) * half, half)]
                last.append(_remote(theirs, theirs, *sib_sems, sib))
        return first, relay, last

    def start(ins, outs, ssem, rsem):
        for cp in copies(outs, ssem, rsem, only_first=True)[0]:
            cp.start()

    def finish(ins, outs, ssem, rsem):
        first, relay, last = copies(outs, ssem, rsem)
        for arrived, onward in relay:
            arrived.wait_recv()
            onward.start()
        for cp in last:
            cp.wait_recv()
        for cp in first:
            cp.wait_send()
        for _, onward in relay:
            onward.wait_send()

    return Exchange(bufs, [_sds(b.shape, b.dtype) for b in bufs], {w: w for w in range(n)}, 6 * n, start, finish)


def _simple_exchange(arrays, landing, aliases, make_copies, sibling_only=False):
    def start(ins, outs, ssem, rsem):
        for cp, _ in make_copies(ins, outs, ssem, rsem, False):
            cp.start()

    def finish(ins, outs, ssem, rsem):
        cps = make_copies(ins, outs, ssem, rsem, True)
        for _, landed in cps:
            landed.wait_recv()
        for cp, _ in cps:
            cp.wait_send()

    return Exchange(arrays, landing, aliases, len(arrays) * 3, start, finish, sibling_only)


def _ex_pair_swap(grads):
    def make(ins, outs, ssem, rsem, landing):
        x, y, c, _ = _place()
        cps = [_remote(ins[w].at[:, 1 - c], outs[w], ssem.at[w], rsem.at[w], (x, y, 1 - c))
               for w in range(len(grads))]
        return [(cp, cp) for cp in cps]

    return _simple_exchange(grads, [_sds((NSH,) + g.shape[2:], g.dtype) for g in grads], {}, make, True)


def _ex_relay(bufs):
    def make(ins, outs, ssem, rsem, landing):
        x, y, c, chips = _place()
        sib = (x, y, 1 - c)
        out = []
        for w in range(len(bufs)):
            half = outs[w].shape[1] // 2
            for k, (px, py) in enumerate(chips):
                sems = (ssem.at[3 * w + k], rsem.at[3 * w + k])
                have = outs[w].at[2 * px + py, pl.ds(c * half, half)]
                miss = outs[w].at[2 * px + py, pl.ds((1 - c) * half, half)]
                out.append((_remote(have, have, *sems, sib), _remote(miss, miss, *sems, sib) if landing else None))
        return out

    return _simple_exchange(bufs, [_sds(b.shape, b.dtype) for b in bufs], {w: w for w in range(len(bufs))}, make, True)


def _ex_share(bufs):
    def make(ins, outs, ssem, rsem, landing):
        x, y, c, _ = _place()
        sib = (x, y, 1 - c)
        return [(_remote(outs[w].at[c], outs[w].at[c], ssem.at[w], rsem.at[w], sib),
                 _remote(outs[w].at[1 - c], outs[w].at[1 - c], ssem.at[w], rsem.at[w], sib) if landing else None)
                for w in range(len(bufs))]

    return _simple_exchange(bufs, [_sds(b.shape, b.dtype) for b in bufs], {w: w for w in range(len(bufs))}, make, True)


def _small_copies(slots, ssems, rsems, sending):
    x, y, c, _ = _place()
    out = []
    for m in range(1, 8):
        px, py, pc = x ^ (m >> 2), y ^ ((m >> 1) & 1), c ^ (m & 1)
        slot = slots.at[4 * x + 2 * y + c if sending else 4 * px + 2 * py + pc]
        out.append(_remote(slot, slot, ssems[m - 1], rsems[m - 1], (px, py, pc)))
    return out


def _small_gather_start(slots, name):
    def body(*refs):
        for cp in _small_copies(refs[0], refs[1:8], refs[8:15], True):
            cp.start()
        refs[-1][...] = jnp.zeros_like(refs[-1])

    outs = pl.pallas_call(
        body, name=name,
        out_shape=([pltpu.SemaphoreType.DMA(())] * 14 + [pltpu.HBM(slots.shape, slots.dtype)]
                   + [jax.ShapeDtypeStruct((8, 128), F32)]),
        in_specs=[_HBM], out_specs=[_SEM] * 14 + [_HBM, _VM], input_output_aliases={0: 14},
        compiler_params=pltpu.CompilerParams(has_side_effects=_EFFECT),
    )(*_in_hbm([slots]))
    return outs[:14], outs[14], outs[15]


def _small_gather_wait(sems, slots, after, name):
    def body(*refs):
        for cp in _small_copies(refs[0], refs[1:8], refs[8:15], True):
            cp.wait_send()
        for cp in _small_copies(refs[0], refs[1:8], refs[8:15], False):
            cp.wait_recv()

    return pl.pallas_call(
        body, name=name, out_shape=pltpu.HBM(slots.shape, slots.dtype),
        in_specs=[_HBM] + [_SEM] * 14 + [_ANY] * len(after), out_specs=_HBM, input_output_aliases={0: 0},
        compiler_params=pltpu.CompilerParams(has_side_effects=_EFFECT),
    )(slots, *sems, *after)


def _pair_sum(grads, gots, c_idx, name):
    n = len(grads)

    def body(c_ref, *refs):
        for a_ref, b_ref, o_ref in zip(refs[:n], refs[n:2 * n], refs[2 * n:]):
            o_ref[...] = (a_ref[...].astype(F32) + b_ref[...].astype(F32)).astype(BF16)

    halves = [g.shape[2:] for g in grads]
    return list(pl.pallas_call(
        body, name=name, out_shape=[_sds((NSH,) + h, BF16) for h in halves],
        grid_spec=pltpu.PrefetchScalarGridSpec(
            num_scalar_prefetch=1, grid=(NSH,),
            in_specs=[pl.BlockSpec((None, None) + h, lambda j, c: (j, c[0], 0, 0)) for h in halves]
            + [pl.BlockSpec((None,) + h, lambda j, c: (j, 0, 0)) for h in halves],
            out_specs=[pl.BlockSpec((None,) + h, lambda j, c: (j, 0, 0)) for h in halves]),
        compiler_params=_params(("arbitrary",), 40),
    )(c_idx, *_in_hbm(list(grads) + list(gots))))


def _chip_sum(owns, gots, place, name):
    n = len(owns)

    def body(place_ref, *refs):
        for own_ref, got_ref, o_ref in zip(refs[:n], refs[n:2 * n], refs[2 * n:]):
            acc = own_ref[...].astype(F32)
            for k in range(3):
                acc = acc + got_ref[k].astype(F32)
            o_ref[...] = acc

    shapes = [(o.shape[1] // 2, o.shape[2]) for o in owns]
    return list(pl.pallas_call(
        body, name=name, out_shape=[_sds((2, 2 * r, c), F32) for r, c in shapes],
        grid_spec=pltpu.PrefetchScalarGridSpec(
            num_scalar_prefetch=1, grid=(2,),
            in_specs=[pl.BlockSpec((None, r, c), lambda s, p: (p[0], s, 0)) for r, c in shapes]
            + [pl.BlockSpec((3, r, c), lambda s, p: (0, s, 0)) for r, c in shapes],
            out_specs=[pl.BlockSpec((None, r, c), lambda s, p: (p[1], s, 0)) for r, c in shapes]),
        compiler_params=_params(("arbitrary",), 40),
    )(place, *_in_hbm(list(owns) + list(gots))))


def _adamw_math(w, g, m, v):
    m = B1 * m + (1.0 - B1) * g
    v = B2 * v + (1.0 - B2) * (g * g)
    m_hat = m / (1.0 - B1 ** STEP)
    v_hat = v / (1.0 - B2 ** STEP)
    return -LR * (m_hat / (jnp.sqrt(v_hat) + AEPS) + WD * w), m, v


def _adamw(ws, gs, ms, vs, name, after=()):
    n, steps = len(ws), 4

    def body(*refs):
        ins, outs = refs[:4 * n], refs[4 * n:]
        for i in range(n):
            w_ref, g_ref, m_ref, v_ref = ins[4 * i:4 * i + 4]
            go_ref, d_ref, nm_ref, nv_ref = outs[4 * i:4 * i + 4]
            g = g_ref[...]
            go_ref[...] = g
            d_ref[...], nm_ref[...], nv_ref[...] = _adamw_math(w_ref[...], g, m_ref[...], v_ref[...])

    args, specs, shapes, free = [], [], [], []
    for i, (w, g, m, v) in enumerate(zip(ws, gs, ms, vs)):
        args += [w, g, m, v]
        specs += [pl.BlockSpec((w.shape[0] // steps, w.shape[1]), lambda r: (r, 0))] * 4
        shapes += [_sds(w.shape, F32)] * 4
        free += [4 * i, 4 * i + 2, 4 * i + 3]
    outs = _call(body, args, name=name, grid=(steps,), out_shape=shapes, in_specs=specs, out_specs=specs,
                 compiler_params=_params(("arbitrary",), 48), free=tuple(free), after=after)
    return [outs[4 * i:4 * i + 4] for i in range(n)]


def _small_update(gathered, w, m, v, entries):
    rows = w.shape[0]

    def body(ga_ref, w_ref, m_ref, v_ref, *out_refs):
        for j, (first, n) in enumerate(entries):
            mine = slice(first, first + n)
            g = ga_ref[mine, :]
            for dev in range(1, 8):
                g = g + ga_ref[dev * rows + first:dev * rows + first + n, :]
            results = (g,) + _adamw_math(w_ref[mine, :], g, m_ref[mine, :], v_ref[mine, :])
            for i, res in enumerate(results):
                out_refs[i * len(entries) + j][...] = res

    outs = pl.pallas_call(
        body, name="small_update",
        out_shape=[jax.ShapeDtypeStruct((n, 128), F32) for _ in range(4) for _, n in entries],
        in_specs=[_VM] * 4, out_specs=[_VM] * (4 * len(entries)),
    )(gathered, w, m, v)
    return [outs[i * len(entries):(i + 1) * len(entries)] for i in range(4)]


SMALL = ("ffn1_norm", "mix_norm", "ffn2_norm", "final_norm", "pool_scale", "loss", "pool_w_group")
BIG = ("ffn1_w_gate_up", "ffn1_w_down", "w_in", "w_branch_pool", "w_branch_attn", "w_out",
       "ffn2_w_gate_up", "ffn2_w_down")
ORDER = ("ffn1_norm", "ffn1_w_gate_up", "ffn1_w_down", "mix_norm", "w_in", "pool_w_group", "pool_scale",
         "w_branch_pool", "w_branch_attn", "w_out", "ffn2_norm", "ffn2_w_gate_up", "ffn2_w_down", "final_norm")
SMALL_ROWS = 560


def _pack_small(t):
    parts = []
    for k in SMALL:
        rows = t[k].reshape(-1, 128) if k in t else jnp.zeros((1, 128), F32)
        parts.append(jnp.pad(rows, ((0, -rows.shape[0] % 8), (0, 0))))
    packed = jnp.concatenate(parts, axis=0)
    assert packed.shape == (SMALL_ROWS, 128), packed.shape
    return packed


def _small_entries(like):
    out, at = [], 0
    for k in SMALL:
        n = like[k].size // 128 if k in like else 1
        out.append((at, n))
        at += n + (-n % 8)
    return out


def _halves(g):
    return g.reshape(NSH, 2, g.shape[1] // 2, g.shape[2])


def kernel(x, ffn1_norm, ffn1_w_gate_up, ffn1_w_down, mix_norm, w_in, pool_w_group, pool_scale, w_branch_pool, w_branch_attn, w_out, ffn2_norm, ffn2_w_gate_up, ffn2_w_down, final_norm, loss_target, m_ffn1_norm, m_ffn1_w_gate_up, m_ffn1_w_down, m_mix_norm, m_w_in, m_pool_w_group, m_pool_scale, m_w_branch_pool, m_w_branch_attn, m_w_out, m_ffn2_norm, m_ffn2_w_gate_up, m_ffn2_w_down, m_final_norm, v_ffn1_norm, v_ffn1_w_gate_up, v_ffn1_w_down, v_mix_norm, v_w_in, v_pool_w_group, v_pool_scale, v_w_branch_pool, v_w_branch_attn, v_w_out, v_ffn2_norm, v_ffn2_w_gate_up, v_ffn2_w_down, v_final_norm):
    wts = dict(ffn1_norm=ffn1_norm, ffn1_w_gate_up=ffn1_w_gate_up, ffn1_w_down=ffn1_w_down, mix_norm=mix_norm,
               w_in=w_in, pool_w_group=pool_w_group, pool_scale=pool_scale, w_branch_pool=w_branch_pool,
               w_branch_attn=w_branch_attn, w_out=w_out, ffn2_norm=ffn2_norm, ffn2_w_gate_up=ffn2_w_gate_up,
               ffn2_w_down=ffn2_w_down, final_norm=final_norm)
    mom = dict(ffn1_norm=m_ffn1_norm, ffn1_w_gate_up=m_ffn1_w_gate_up, ffn1_w_down=m_ffn1_w_down,
               mix_norm=m_mix_norm, w_in=m_w_in, pool_w_group=m_pool_w_group, pool_scale=m_pool_scale,
               w_branch_pool=m_w_branch_pool, w_branch_attn=m_w_branch_attn, w_out=m_w_out,
               ffn2_norm=m_ffn2_norm, ffn2_w_gate_up=m_ffn2_w_gate_up, ffn2_w_down=m_ffn2_w_down,
               final_norm=m_final_norm)
    var = dict(ffn1_norm=v_ffn1_norm, ffn1_w_gate_up=v_ffn1_w_gate_up, ffn1_w_down=v_ffn1_w_down,
               mix_norm=v_mix_norm, w_in=v_w_in, pool_w_group=v_pool_w_group, pool_scale=v_pool_scale,
               w_branch_pool=v_w_branch_pool, w_branch_attn=v_w_branch_attn, w_out=v_w_out,
               ffn2_norm=v_ffn2_norm, ffn2_w_gate_up=v_ffn2_w_gate_up, ffn2_w_down=v_ffn2_w_down,
               final_norm=v_final_norm)

    c_idx = lax.axis_index("c").astype(jnp.int32).reshape(1)
    me_idx = (2 * lax.axis_index("x") + lax.axis_index("y")).astype(jnp.int32).reshape(1)
    place = jnp.concatenate([me_idx, c_idx])
    x0, tgt = x[0], loss_target[0]
    wgrp = pool_w_group[0].astype(BF16)
    g1, gm, g2, gf = ffn1_norm, mix_norm, ffn2_norm, final_norm.reshape(1, D)
    grad, delta, new_m, new_v = {}, {}, {}, {}

    def pair_sums(keys, parts, got):
        return _pair_sum(parts, got, c_idx, "pair_sum_" + keys[0])

    def chip_sums(keys, chip_parts, owned):
        return _chip_sum(chip_parts, owned, place, "chip_sum_" + keys[0])

    def adamw(keys, after=()):
        outs = _adamw([wts[k][0] for k in keys], [grad[k][0] for k in keys], [mom[k][0] for k in keys],
                      [var[k][0] for k in keys], "adamw_" + keys[0], after=after)
        for k, res in zip(keys, outs):
            grad[k], delta[k], new_m[k], new_v[k] = (o.reshape(wts[k].shape) for o in res)

    first, late = ("ffn1_w_gate_up", "ffn1_w_down"), ("w_branch_pool", "w_branch_attn", "w_out",
                                                       "ffn2_w_gate_up", "ffn2_w_down")
    own = {}
    for group in (first, ("w_in",), late):
        own.update(zip(group, _cast_into_block([wts[k][0] for k in group], me_idx, "cast_" + group[0])))
    full = dict(zip(first, _exchange_alone(_ex_gather([own[k] for k in first]), "gather_ffn1")))
    wgu1, wd1 = full["ffn1_w_gate_up"], full["ffn1_w_down"].reshape(DFF, D)
    (h1, n1, gu1, a1), (win,) = _ffn_fwd(x0, g1, wgu1, wd1, "ffn1_fwd", exchange=_ex_gather_direct([own["w_in"]]))
    sems_l, thru_l, token_l = _gather_start([own[k_] for k_ in late], [h1], "gather_late_start")
    u, xp, q, k, v, gp, gs = _mix_in(h1, gm, win, after=(token_l,))
    o_sb, ctot = _attn_fwd(q, k, v)
    arrived = _gather_wait(sems_l, thru_l, [o_sb], "gather_late_wait")
    wbp, wba, wout = _exchange_alone(_ex_relay(arrived[:3]), "relay_mix")
    wout = wout.reshape(D, D)
    (h2, pm, p, yp, ys, mm), (wgu2, wd2) = _mix_out(h1, xp, o_sb, gp, gs, wgrp, pool_scale, wbp, wba, wout,
                                                    exchange=_ex_relay(arrived[3:]))
    wd2 = wd2.reshape(DFF, D)
    dh3, loss_row, d_gf, n3, gu3, a3 = _ffn_fwd(h2, g2, wgu2, wd2, "ffn2_fwd", head=(tgt, gf))

    def grad_gate_up(n, dgu, name, exchange=None):
        res = _wgrad(n, dgu, NSH, D, name, exchange=exchange)
        return [_halves(res)] if exchange is None else ([_halves(res[0])], res[1])

    def grad_down(a, dh, name, exchange=None):
        res = _wgrad(a, dh, 1, FFS, name, exchange=exchange)
        halves = lambda g: [_halves(g.reshape(NSH, DFF // NSH, D))]
        return halves(res) if exchange is None else (halves(res[0]), res[1])

    k_gu2, k_d2, k_gu1, k_d1, k_in = (("ffn2_w_gate_up",), ("ffn2_w_down",), ("ffn1_w_gate_up",),
                                      ("ffn1_w_down",), ("w_in",))
    dh2, dgu3, d_g2 = _ffn_bwd(dh3, h2, g2, gu3, wgu2, wd2, "ffn2_bwd")
    pa = grad_gate_up(n3, dgu3, "wgrad_gu2") + grad_down(a3, dh3, "wgrad_d2")
    (dlg, dyp, dys, do_sb, dyg, dxp, d_scale), got_a = _mix_bwd_out(
        dh2, gp, gs, yp, ys, pm, wgrp, pool_scale, wbp, wba, wout, exchange=_ex_pair_swap(pa))
    chip_a = pair_sums(k_gu2 + k_d2, pa, got_a)
    kb = ("w_out", "w_branch_pool", "w_branch_attn")
    g_bp, g_ba, d_group = _wgrad_branches(p, dyp, o_sb, dys, pm, dyg)
    pb = [_halves(_wgrad(mm, dh2, 1, D, "wgrad_out").reshape(NSH, D // NSH, D)), _halves(g_bp), _halves(g_ba)]
    k_a, k_in = k_gu2 + k_d2, k_in + kb
    sems_a, thru_a, token_a = _scatter_start(chip_a, "scatter_a_start")
    dq, dk, dv = _attn_bwd(q, k, v, do_sb, ctot, after=(token_a,))
    chip_a, owned_a = _scatter_wait(sems_a, thru_a, [dq], "scatter_a_wait")
    halves_a = chip_sums(k_a, chip_a, owned_a)
    dproj = (dxp, dq, dk, dv, dlg)
    (dh1, d_gm), both_a = _mix_bwd_in(dh2, h1, gm, dproj, win, exchange=_ex_share(halves_a))
    for i, k_ in enumerate(k_a):
        grad[k_] = both_a[i].reshape(wts[k_].shape)

    p_in = [_halves(_wgrad_in(u, dproj))] + pb
    p_d1, got_in = grad_down(a1, dh1, "wgrad_d1", exchange=_ex_pair_swap(p_in))
    sems_in, thru_in, token_in = _scatter_start(pair_sums(k_in, p_in, got_in), "scatter_in_start")
    dgu1, got_d1 = _ffn_bwd_act(dh1, gu1, wd1, "ffn1_bwd_act", exchange=_ex_pair_swap(p_d1), after=(token_in,))
    sems_d1, thru_d1, token_d1 = _scatter_start(pair_sums(k_d1, p_d1, got_d1), "scatter_d1_start")
    p_gu1 = [_halves(_wgrad(n1, dgu1, NSH, D, "wgrad_gu1", after=(token_in, token_d1)))]
    sems_w, thru_w, token_w = _swap_start(p_gu1, "swap_gu1_start")
    chip_in, owned_in = _scatter_wait(sems_in, thru_in, [token_w], "scatter_in_wait")
    chip_d1, owned_d1 = _scatter_wait(sems_d1, thru_d1, [token_w], "scatter_d1_wait")
    halves_in = chip_sums(k_in, chip_in, owned_in)
    p_gu1, got_gu1 = _swap_wait(sems_w, thru_w, halves_in, "swap_gu1_wait")
    sems, thru, token = _scatter_start(pair_sums(k_gu1, p_gu1, got_gu1), "scatter_gu1_start")
    sems_h, thru_h, token_h = _share_start(halves_in, [token], "share_in_start")
    adamw(k_a, after=(token_h,))
    landed = _share_wait(sems_h, thru_h, [delta[k_a[0]]], "share_in_wait")
    for i, k_ in enumerate(k_in):
        grad[k_] = landed[i].reshape(wts[k_].shape)
    adamw(k_in)
    dx, d_g1 = _ffn_bwd_in(dh1, x0, g1, dgu1, wgu1, "ffn1_bwd_in", after=(token,))
    small_g = dict(ffn1_norm=d_g1, mix_norm=d_gm, ffn2_norm=d_g2, final_norm=d_gf, pool_scale=d_scale,
                   pool_w_group=d_group, loss=loss_row)
    dev = 4 * lax.axis_index("x") + 2 * lax.axis_index("y") + lax.axis_index("c")
    slots = lax.dynamic_update_slice(jnp.zeros((8, SMALL_ROWS, 128), F32), _pack_small(small_g)[None], (dev, 0, 0))
    sems_s, slots, token_s = _small_gather_start(slots, "small_gather_start")

    chip_gu1, owned_gu1 = _scatter_wait(sems, thru, [dx] + [delta[k_] for k_ in k_a + k_in], "scatter_gu1_wait")
    halves_last = chip_sums(k_d1 + k_gu1, chip_d1 + chip_gu1, owned_d1 + owned_gu1)
    both = _exchange_alone(_ex_share(halves_last), "share_last",
                           after=(token_s,))
    grad["ffn1_w_down"] = both[0].reshape(ffn1_w_down.shape)
    grad["ffn1_w_gate_up"] = both[1].reshape(ffn1_w_gate_up.shape)
    adamw(k_d1 + k_gu1, after=(token_s,))
    gathered = _small_gather_wait(sems_s, slots, [delta[k_] for k_ in k_d1 + k_gu1], "small_gather_wait")
    gathered = gathered.reshape(8 * SMALL_ROWS, 128)
    results = _small_update(gathered, _pack_small(wts), _pack_small(mom), _pack_small(var), _small_entries(wts))
    for dst, entries in zip((grad, delta, new_m, new_v), results):
        for k_, rows in zip(SMALL, entries):
            if k_ in wts:
                dst[k_] = rows.reshape(wts[k_].shape)
            elif dst is grad:
                loss = rows[0, 0]
    return (loss, dx[None], *[grad[k_] for k_ in ORDER], *[delta[k_] for k_ in ORDER],
            *[new_m[k_] for k_ in ORDER], *[new_v[k_] for k_ in ORDER])
```

```python
import dataclasses
import functools

import jax
import jax.numpy as jnp
from jax import lax
from jax.experimental import pallas as pl
from jax.experimental.pallas import tpu as pltpu

F32 = jnp.float32
BF16 = jnp.bfloat16

S = 2048
D = 1024
DFF = 2816
FFS = 2 * DFF // 4
NSH = 4
PW = 512
PG = 128
POOL_WINDOWS = (2, 4, 8, 16)
HALO = 16
SBW = 512
DH = 64
EPS = 1e-6
SCALE = 0.125
LOG2E = 1.4426950408889634
TA = 256
QB = 2
MIB = 1024 * 1024

LR, B1, B2, AEPS, WD, STEP = 0.001, 0.9, 0.999, 1e-08, 0.01, 10

_VM = pl.BlockSpec(memory_space=pltpu.VMEM)
_ANY = pl.BlockSpec(memory_space=pl.ANY)
MESH = pl.DeviceIdType.MESH
SIBLING_PAIR_ID = 1


def _nn(a, b):
    return jnp.dot(a, b, preferred_element_type=F32)


def _nt(a, b):
    return lax.dot_general(a, b, (((1,), (1,)), ((), ())), preferred_element_type=F32)


def _tn(a, b):
    return lax.dot_general(a, b, (((0,), (0,)), ((), ())), preferred_element_type=F32)


def _params(sem, vmem_mib):
    return pltpu.CompilerParams(dimension_semantics=sem, vmem_limit_bytes=vmem_mib * MIB)


def _rows(tm, width):
    return pl.BlockSpec((tm, width), lambda i: (i, 0))


def _fixed(shape):
    return pl.BlockSpec(shape, lambda *_: (0,) * len(shape))


def _sds(shape, dtype):
    return pltpu.HBM(shape, dtype)


def _in_hbm(args):
    return [pltpu.with_memory_space_constraint(a, pltpu.HBM) for a in args]


def _stage(pairs):
    pieces = 4

    def copy_all(sems):
        copies = []
        for src, dst in pairs:
            step = src.shape[0] // pieces
            for p in range(pieces):
                part = pl.ds(p * step, step)
                if len(dst.shape) == len(src.shape):
                    piece = (src.at[part], dst.at[part])
                else:
                    piece = (src.at[p], dst.at[:, pl.ds(p * src.shape[2], src.shape[2])])
                copies.append(pltpu.make_async_copy(*piece, sems.at[len(copies)]))
        for c in copies:
            c.start()
        for c in copies:
            c.wait()

    @pl.when(pl.program_id(0) == 0)
    def _():
        pl.run_scoped(copy_all, pltpu.SemaphoreType.DMA((pieces * len(pairs),)))


def _vmem_like(*arrays):
    return [pltpu.VMEM(a.shape, a.dtype) for a in arrays]


def _vmem_wide(w):
    return pltpu.VMEM((w.shape[1], w.shape[0] * w.shape[2]), w.dtype)


FF_CHUNKS = ((0, 1536), (1536, DFF - 1536))


class Exchange:
    def __init__(self, arrays, landing, aliases, n_sems, start, finish, sibling_only=False):
        self.arrays, self.landing, self.aliases, self.n_sems = list(arrays), list(landing), dict(aliases), n_sems
        self.start, self.finish = start, finish
        self.sibling_only = sibling_only

    def enter(self):
        if self.sibling_only:
            barrier = pltpu.get_barrier_semaphore()
            sibling = (lax.axis_index("x"), lax.axis_index("y"), 1 - lax.axis_index("c"))
            pl.semaphore_signal(barrier, inc=1, device_id=sibling, device_id_type=MESH)
            pl.semaphore_wait(barrier, 1)

    def params(self, compiler_params=None):
        kw = dict(collective_id=SIBLING_PAIR_ID) if self.sibling_only else {}
        if compiler_params is None:
            return pltpu.CompilerParams(**kw)
        return dataclasses.replace(compiler_params, **kw)


def _call(body, args, *, name, grid, in_specs, out_specs, out_shape, scratch_shapes=(), compiler_params=None,
          exchange=None, free=(), after=()):
    args = [a if i in free else pltpu.with_memory_space_constraint(a, pltpu.HBM) for i, a in enumerate(args)]
    if exchange is None:
        n_in = len(in_specs)

        def plain(*refs):
            body(*refs[:n_in], *refs[n_in + len(after):])

        return pl.pallas_call(plain, name=name, grid=grid, in_specs=list(in_specs) + [_ANY] * len(after),
                              out_specs=out_specs, out_shape=out_shape, scratch_shapes=list(scratch_shapes),
                              compiler_params=compiler_params)(*args, *after)
    ex = exchange
    n_in, n_out, n_scr = len(in_specs), len(out_specs), len(scratch_shapes)
    na, nl = len(ex.arrays), len(ex.landing)

    def hosted(*refs):
        at = [0]

        def take(n):
            at[0] += n
            return refs[at[0] - n:at[0]]

        k_in, _, e_in, k_out, e_out, k_scr = take(n_in), take(len(after)), take(na), take(n_out), take(nl), take(n_scr)
        ssem, rsem = take(2)
        ids = [pl.program_id(a) for a in range(len(grid))]
        first = functools.reduce(jnp.logical_and, [i == 0 for i in ids])
        last = functools.reduce(jnp.logical_and, [i == g - 1 for i, g in zip(ids, grid)])

        @pl.when(first)
        def _():
            ex.enter()
            ex.start(e_in, e_out, ssem, rsem)

        body(*k_in, *k_out, *k_scr)

        @pl.when(last)
        def _():
            ex.finish(e_in, e_out, ssem, rsem)

    outs = pl.pallas_call(
        hosted, name=name, grid=grid,
        in_specs=list(in_specs) + [_ANY] * (len(after) + na), out_specs=list(out_specs) + [_ANY] * nl,
        out_shape=list(out_shape) + ex.landing,
        scratch_shapes=list(scratch_shapes) + [pltpu.SemaphoreType.DMA((ex.n_sems,))] * 2,
        input_output_aliases={n_in + len(after) + i: n_out + j for i, j in ex.aliases.items()},
        compiler_params=ex.params(compiler_params),
    )(*args, *after, *_in_hbm(ex.arrays))
    return outs[:n_out], outs[n_out:]


def _exchange_alone(ex, name, after=()):
    na, nl = len(ex.arrays), len(ex.landing)

    def body(*refs):
        outs = refs[na + len(after):na + len(after) + nl]
        ex.enter()
        ex.start(refs[:na], outs, refs[-2], refs[-1])
        ex.finish(refs[:na], outs, refs[-2], refs[-1])

    return pl.pallas_call(
        body, name=name, in_specs=[_ANY] * (na + len(after)), out_specs=[_ANY] * nl,
        out_shape=ex.landing, scratch_shapes=[pltpu.SemaphoreType.DMA((ex.n_sems,))] * 2,
        input_output_aliases=ex.aliases, compiler_params=ex.params(),
    )(*_in_hbm(ex.arrays), *after)


_HBM = pl.BlockSpec(memory_space=pltpu.HBM)
_SEM = pl.BlockSpec(memory_space=pltpu.SEMAPHORE)
_EFFECT = pltpu.SideEffectType.DATAFLOW_SIDE_EFFECTING


def _scatter_copies(srcs, lands, ssems, rsems):
    x, y, c, chips = _place()
    return [_remote(srcs[w].at[2 * px + py], lands[w].at[k], ssems[3 * w + k], rsems[3 * w + k], (px, py, c))
            for w in range(len(srcs)) for k, (px, py) in enumerate(chips)]


def _scatter_start(parts, name):
    parts = list(parts)
    n, ncp = len(parts), 3 * len(parts)
    lands = [lax.empty((3,) + p.shape[1:], p.dtype) for p in parts]

    def body(*refs):
        srcs, land_refs = refs[:n], refs[n:2 * n]
        ssems, rsems = refs[2 * n:2 * n + ncp], refs[2 * n + ncp:2 * n + 2 * ncp]
        for cp in _scatter_copies(srcs, land_refs, ssems, rsems):
            cp.start()
        token = refs[-1]
        token[...] = jnp.zeros_like(token)

    outs = pl.pallas_call(
        body, name=name,
        out_shape=([pltpu.SemaphoreType.DMA(())] * (2 * ncp) + [pltpu.HBM(a.shape, a.dtype) for a in parts + lands]
                   + [jax.ShapeDtypeStruct((8, 128), F32)]),
        in_specs=[_HBM] * (2 * n), out_specs=[_SEM] * (2 * ncp) + [_HBM] * (2 * n) + [_VM],
        input_output_aliases={i: 2 * ncp + i for i in range(2 * n)},
        compiler_params=pltpu.CompilerParams(has_side_effects=_EFFECT),
    )(*_in_hbm(parts), *_in_hbm(lands))
    sems, thru, token = outs[:2 * ncp], outs[2 * ncp:2 * ncp + 2 * n], outs[-1]
    return sems, thru, token


def _scatter_wait(sems, thru, after, name):
    n = len(thru) // 2
    ncp = 3 * n

    def body(*refs):
        srcs, land_refs = refs[:n], refs[n:2 * n]
        ssems, rsems = refs[2 * n:2 * n + ncp], refs[2 * n + ncp:2 * n + 2 * ncp]
        for cp in _scatter_copies(srcs, land_refs, ssems, rsems):
            cp.wait_send()
            cp.wait_recv()

    outs = pl.pallas_call(
        body, name=name, out_shape=[pltpu.HBM(a.shape, a.dtype) for a in thru],
        in_specs=[_HBM] * (2 * n) + [_SEM] * (2 * ncp) + [_ANY] * len(after), out_specs=[_HBM] * (2 * n),
        input_output_aliases={i: i for i in range(2 * n)},
        compiler_params=pltpu.CompilerParams(has_side_effects=_EFFECT),
    )(*thru, *sems, *after)
    return outs[:n], outs[n:]


def _swap_copies(srcs, lands, ssems, rsems):
    x, y, c, _ = _place()
    return [_remote(srcs[w].at[:, 1 - c], lands[w], ssems[w], rsems[w], (x, y, 1 - c)) for w in range(len(srcs))]


def _swap_start(grads, name):
    grads = list(grads)
    n = len(grads)
    lands = [lax.empty((NSH,) + g.shape[2:], g.dtype) for g in grads]

    def body(*refs):
        barrier = pltpu.get_barrier_semaphore()
        sibling = (lax.axis_index("x"), lax.axis_index("y"), 1 - lax.axis_index("c"))
        pl.semaphore_signal(barrier, inc=1, device_id=sibling, device_id_type=MESH)
        pl.semaphore_wait(barrier, 1)
        for cp in _swap_copies(refs[:n], refs[n:2 * n], refs[2 * n:3 * n], refs[3 * n:4 * n]):
            cp.start()
        refs[-1][...] = jnp.zeros_like(refs[-1])

    outs = pl.pallas_call(
        body, name=name,
        out_shape=([pltpu.SemaphoreType.DMA(())] * (2 * n) + [pltpu.HBM(a.shape, a.dtype) for a in grads + lands]
                   + [jax.ShapeDtypeStruct((8, 128), F32)]),
        in_specs=[_HBM] * (2 * n), out_specs=[_SEM] * (2 * n) + [_HBM] * (2 * n) + [_VM],
        input_output_aliases={i: 2 * n + i for i in range(2 * n)},
        compiler_params=pltpu.CompilerParams(has_side_effects=_EFFECT, collective_id=SIBLING_PAIR_ID),
    )(*_in_hbm(grads), *_in_hbm(lands))
    return outs[:2 * n], outs[2 * n:4 * n], outs[-1]


def _swap_wait(sems, thru, after, name):
    n = len(thru) // 2

    def body(*refs):
        for cp in _swap_copies(refs[:n], refs[n:2 * n], refs[2 * n:3 * n], refs[3 * n:4 * n]):
            cp.wait_send()
            cp.wait_recv()

    outs = pl.pallas_call(
        body, name=name, out_shape=[pltpu.HBM(a.shape, a.dtype) for a in thru],
        in_specs=[_HBM] * (2 * n) + [_SEM] * (2 * n) + [_ANY] * len(after), out_specs=[_HBM] * (2 * n),
        input_output_aliases={i: i for i in range(2 * n)},
        compiler_params=pltpu.CompilerParams(has_side_effects=_EFFECT),
    )(*thru, *sems, *after)
    return outs[:n], outs[n:]


def _share_copies(bufs, ssems, rsems, sending):
    x, y, c, _ = _place()
    out = []
    for w, ref in enumerate(bufs):
        slot = ref.at[c if sending else 1 - c]
        out.append(_remote(slot, slot, ssems[w], rsems[w], (x, y, 1 - c)))
    return out


def _share_start(bufs, after, name):
    bufs = list(bufs)
    n = len(bufs)

    def body(*refs):
        barrier = pltpu.get_barrier_semaphore()
        sibling = (lax.axis_index("x"), lax.axis_index("y"), 1 - lax.axis_index("c"))
        pl.semaphore_signal(barrier, inc=1, device_id=sibling, device_id_type=MESH)
        pl.semaphore_wait(barrier, 1)
        at = n + len(after)
        for cp in _share_copies(refs[:n], refs[at:at + n], refs[at + n:at + 2 * n], True):
            cp.start()
        refs[-1][...] = jnp.zeros_like(refs[-1])

    outs = pl.pallas_call(
        body, name=name,
        out_shape=([pltpu.SemaphoreType.DMA(())] * (2 * n) + [pltpu.HBM(a.shape, a.dtype) for a in bufs]
                   + [jax.ShapeDtypeStruct((8, 128), F32)]),
        in_specs=[_HBM] * n + [_ANY] * len(after), out_specs=[_SEM] * (2 * n) + [_HBM] * n + [_VM],
        input_output_aliases={i: 2 * n + i for i in range(n)},
        compiler_params=pltpu.CompilerParams(has_side_effects=_EFFECT, collective_id=SIBLING_PAIR_ID),
    )(*_in_hbm(bufs), *after)
    return outs[:2 * n], outs[2 * n:3 * n], outs[-1]


def _share_wait(sems, thru, after, name):
    n = len(thru)

    def body(*refs):
        for cp in _share_copies(refs[:n], refs[n:2 * n], refs[2 * n:3 * n], True):
            cp.wait_send()
        for cp in _share_copies(refs[:n], refs[n:2 * n], refs[2 * n:3 * n], False):
            cp.wait_recv()

    return pl.pallas_call(
        body, name=name, out_shape=[pltpu.HBM(a.shape, a.dtype) for a in thru],
        in_specs=[_HBM] * n + [_SEM] * (2 * n) + [_ANY] * len(after), out_specs=[_HBM] * n,
        input_output_aliases={i: i for i in range(n)},
        compiler_params=pltpu.CompilerParams(has_side_effects=_EFFECT),
    )(*thru, *sems, *after)


def _gather_copies(bufs, ssems, rsems, sending):
    x, y, c, chips = _place()
    out = []
    for w, ref in enumerate(bufs):
        half = ref.shape[1] // 2
        for k, (px, py) in enumerate(chips):
            rows = ref.at[2 * x + y if sending else 2 * px + py, pl.ds(c * half, half)]
            out.append(_remote(rows, rows, ssems[3 * w + k], rsems[3 * w + k], (px, py, c)))
    return out


def _gather_start(bufs, after, name):
    n, ncp = len(bufs), 3 * len(bufs)

    def body(*refs):
        ssems, rsems = refs[n + len(after):n + len(after) + ncp], refs[n + len(after) + ncp:n + len(after) + 2 * ncp]
        for cp in _gather_copies(refs[:n], ssems, rsems, True):
            cp.start()
        token = refs[-1]
        token[...] = jnp.zeros_like(token)

    outs = pl.pallas_call(
        body, name=name,
        out_shape=([pltpu.SemaphoreType.DMA(())] * (2 * ncp) + [pltpu.HBM(a.shape, a.dtype) for a in bufs]
                   + [jax.ShapeDtypeStruct((8, 128), F32)]),
        in_specs=[_HBM] * n + [_ANY] * len(after), out_specs=[_SEM] * (2 * ncp) + [_HBM] * n + [_VM],
        input_output_aliases={i: 2 * ncp + i for i in range(n)},
        compiler_params=pltpu.CompilerParams(has_side_effects=_EFFECT),
    )(*_in_hbm(bufs), *after)
    return outs[:2 * ncp], outs[2 * ncp:2 * ncp + n], outs[-1]


def _gather_wait(sems, thru, after, name):
    n = len(thru)
    ncp = 3 * n

    def body(*refs):
        ssems, rsems = refs[n:n + ncp], refs[n + ncp:n + 2 * ncp]
        for cp in _gather_copies(refs[:n], ssems, rsems, True):
            cp.wait_send()
        for cp in _gather_copies(refs[:n], ssems, rsems, False):
            cp.wait_recv()

    return pl.pallas_call(
        body, name=name, out_shape=[pltpu.HBM(a.shape, a.dtype) for a in thru],
        in_specs=[_HBM] * n + [_SEM] * (2 * ncp) + [_ANY] * len(after), out_specs=[_HBM] * n,
        input_output_aliases={i: i for i in range(n)},
        compiler_params=pltpu.CompilerParams(has_side_effects=_EFFECT),
    )(*thru, *sems, *after)


def _rms(x):
    r = lax.rsqrt(jnp.mean(x * x, axis=-1, keepdims=True) + EPS)
    return r, x * r


def _rms_bwd(dn, xr, r, gain):
    dng = dn * gain
    dx = r * (dng - xr * jnp.mean(dng * xr, axis=-1, keepdims=True))
    return dx, jnp.sum(dn * xr, axis=0, keepdims=True)


def _ffn_fwd(x, gain, wgu, wd, name, exchange=None, head=None):
    tm = 256

    def body(x_ref, g_ref, wgu_hbm, wd_hbm, *rest):
        if head is None:
            h_ref, n_ref, gu_ref, a_ref, wgu_ref, wd_ref = rest
        else:
            t_ref, gf_ref, h_ref, loss_ref, dgf_ref, n_ref, gu_ref, a_ref, wgu_ref, wd_ref = rest
        _stage([(wgu_hbm, wgu_ref), (wd_hbm, wd_ref)])
        x = x_ref[...]
        _, xr = _rms(x)
        n = (xr * g_ref[...]).astype(BF16)
        n_ref[...] = n
        acc = jnp.zeros((tm, D), F32)
        for c0, cn in FF_CHUNKS:
            g = _nn(n, wgu_ref[:, c0:c0 + cn])
            u = _nn(n, wgu_ref[:, DFF + c0:DFF + c0 + cn])
            gu_ref[:, c0:c0 + cn] = g.astype(BF16)
            gu_ref[:, DFF + c0:DFF + c0 + cn] = u.astype(BF16)
            half_act = (0.5 * (g * jax.nn.sigmoid(g) * u)).astype(BF16)
            a_ref[:, c0:c0 + cn] = half_act
            acc = acc + _nn(half_act, wd_ref[c0:c0 + cn, :])
        h = x + acc
        if head is None:
            h_ref[...] = h
            return
        gf = gf_ref[...]
        r, hr = _rms(h)
        err = hr * gf - t_ref[...]
        dh, dgain = _rms_bwd(err * (1.0 / D), hr, r, gf)
        h_ref[...] = dh

        @pl.when(pl.program_id(0) == 0)
        def _():
            dgf_ref[...] = jnp.zeros_like(dgf_ref)
            loss_ref[...] = jnp.zeros_like(loss_ref)

        dgf_ref[...] += dgain
        loss_ref[...] += jnp.full((1, 128), (0.5 / D) * jnp.sum(err * err), F32)

    saved_specs = [_rows(tm, D), _rows(tm, 4 * FFS), _rows(tm, DFF)]
    saved_shapes = [_sds((S, D), BF16), _sds((S, 4 * FFS), BF16), _sds((S, DFF), BF16)]
    if head is None:
        return _call(
            body, (x, gain, wgu, wd), name=name, grid=(S // tm,),
            in_specs=[_rows(tm, D), _fixed((1, D)), _ANY, _ANY],
            out_specs=[_rows(tm, D)] + saved_specs, out_shape=[_sds((S, D), F32)] + saved_shapes,
            scratch_shapes=[_vmem_wide(wgu)] + _vmem_like(wd),
            compiler_params=_params(("arbitrary",), 56), exchange=exchange)
    return _call(
        body, (x, gain, wgu, wd, *head), name=name, grid=(S // tm,),
        in_specs=[_rows(tm, D), _fixed((1, D)), _ANY, _ANY, _rows(tm, D), _fixed((1, D))],
        out_specs=[_rows(tm, D), _fixed((1, 128)), _fixed((1, D))] + saved_specs,
        out_shape=[_sds((S, D), F32), _sds((1, 128), F32), _sds((1, D), F32)] + saved_shapes,
        scratch_shapes=[_vmem_wide(wgu)] + _vmem_like(wd),
        compiler_params=_params(("arbitrary",), 56), exchange=exchange, free=(4, 5))


def _ffn_bwd(dh, x, gain, gu, wgu, wd, name):
    tm = 256

    def body(dh_ref, x_ref, g_ref, gu_ref, wgu_hbm, wd_hbm, dx_ref, dgu_ref, dg_ref, wgu_ref, wd_ref):
        _stage([(wgu_hbm, wgu_ref), (wd_hbm, wd_ref)])
        dh = dh_ref[...]
        dhb = dh.astype(BF16)
        dn = jnp.zeros((tm, D), F32)
        for c0, cn in FF_CHUNKS:
            g = gu_ref[:, c0:c0 + cn].astype(F32)
            u = gu_ref[:, DFF + c0:DFF + c0 + cn].astype(F32)
            da = 0.5 * _nt(dhb, wd_ref[c0:c0 + cn, :])
            sg = jax.nn.sigmoid(g)
            dgb = (da * u * (sg * (1.0 + g * (1.0 - sg)))).astype(BF16)
            dub = (da * (g * sg)).astype(BF16)
            dgu_ref[:, c0:c0 + cn] = dgb
            dgu_ref[:, DFF + c0:DFF + c0 + cn] = dub
            dn = dn + _nt(dgb, wgu_ref[:, c0:c0 + cn]) + _nt(dub, wgu_ref[:, DFF + c0:DFF + c0 + cn])
        r, xr = _rms(x_ref[...])
        dx, dgain = _rms_bwd(dn, xr, r, g_ref[...])
        dx_ref[...] = dh + dx

        @pl.when(pl.program_id(0) == 0)
        def _():
            dg_ref[...] = jnp.zeros_like(dg_ref)

        dg_ref[...] += dgain

    return _call(
        body, (dh, x, gain, gu, wgu, wd), name=name, grid=(S // tm,),
        in_specs=[_rows(tm, D), _rows(tm, D), _fixed((1, D)), _rows(tm, 4 * FFS), _ANY, _ANY],
        out_specs=[_rows(tm, D), _rows(tm, 4 * FFS), _fixed((1, D))],
        out_shape=[_sds((S, D), F32), _sds((S, 4 * FFS), BF16), _sds((1, D), F32)],
        scratch_shapes=[_vmem_wide(wgu)] + _vmem_like(wd), compiler_params=_params(("arbitrary",), 56))


def _ffn_bwd_act(dh, gu, wd, name, exchange=None, after=()):
    tm = 512

    def body(dh_ref, gu_ref, wd_hbm, dgu_ref, wd_ref):
        _stage([(wd_hbm, wd_ref)])
        dhb = dh_ref[...].astype(BF16)
        for c0, cn in FF_CHUNKS:
            g = gu_ref[:, c0:c0 + cn].astype(F32)
            u = gu_ref[:, DFF + c0:DFF + c0 + cn].astype(F32)
            da = 0.5 * _nt(dhb, wd_ref[c0:c0 + cn, :])
            sg = jax.nn.sigmoid(g)
            dgu_ref[:, c0:c0 + cn] = (da * u * (sg * (1.0 + g * (1.0 - sg)))).astype(BF16)
            dgu_ref[:, DFF + c0:DFF + c0 + cn] = (da * (g * sg)).astype(BF16)

    res = _call(
        body, (dh, gu, wd), name=name, grid=(S // tm,),
        in_specs=[_rows(tm, D), _rows(tm, 4 * FFS), _ANY], out_specs=[_rows(tm, 4 * FFS)],
        out_shape=[_sds((S, 4 * FFS), BF16)], scratch_shapes=_vmem_like(wd),
        compiler_params=_params(("arbitrary",), 56), exchange=exchange, after=after)
    return res[0] if exchange is None else (res[0][0], res[1])


def _ffn_bwd_in(dh, x, gain, dgu, wgu, name, exchange=None, after=()):
    tm = 512

    def body(dh_ref, x_ref, g_ref, dgu_ref, wgu_hbm, dx_ref, dg_ref, wgu_ref):
        _stage([(wgu_hbm, wgu_ref)])
        dn = jnp.zeros((tm, D), F32)
        for half in (0, DFF):
            for c0, cn in FF_CHUNKS:
                cols = slice(half + c0, half + c0 + cn)
                dn = dn + _nt(dgu_ref[:, cols], wgu_ref[:, cols])
        r, xr = _rms(x_ref[...])
        dx, dgain = _rms_bwd(dn, xr, r, g_ref[...])
        dx_ref[...] = dh_ref[...] + dx

        @pl.when(pl.program_id(0) == 0)
        def _():
            dg_ref[...] = jnp.zeros_like(dg_ref)

        dg_ref[...] += dgain

    return _call(
        body, (dh, x, gain, dgu, wgu), name=name, grid=(S // tm,),
        in_specs=[_rows(tm, D), _rows(tm, D), _fixed((1, D)), _rows(tm, 4 * FFS), _ANY],
        out_specs=[_rows(tm, D), _fixed((1, D))],
        out_shape=[_sds((S, D), F32), _sds((1, D), F32)],
        scratch_shapes=[_vmem_wide(wgu)],
        compiler_params=_params(("arbitrary",), 56), exchange=exchange, after=after)


def _mix_in(h, gain, w_in, after=()):
    tm = 512

    def body(h_ref, g_ref, w_hbm, u_ref, xp_ref, q_ref, k_ref, v_ref, gp_ref, gs_ref, w_ref):
        _stage([(w_hbm, w_ref)])
        _, hr = _rms(h_ref[...])
        u = (hr * g_ref[...]).astype(BF16)
        u_ref[...] = u
        p0 = _nn(u, w_ref[0])
        xp_ref[...] = p0[:, :PW]
        q_ref[...] = p0[:, PW:].astype(BF16)
        p1 = _nn(u, w_ref[1])
        k_ref[...] = p1[:, :SBW].astype(BF16)
        v_ref[...] = p1[:, SBW:].astype(BF16)
        gp_ref[...] = jax.nn.sigmoid(_nn(u, w_ref[2])).astype(BF16)
        gs_ref[...] = jax.nn.sigmoid(_nn(u, w_ref[3])).astype(BF16)

    return _call(
        body, (h, gain, w_in), name="mix_in", grid=(S // tm,),
        in_specs=[_rows(tm, D), _fixed((1, D)), _ANY],
        out_specs=[_rows(tm, D), _rows(tm, PW), _rows(tm, SBW), _rows(tm, SBW), _rows(tm, SBW),
                   _rows(tm, D), _rows(tm, D)],
        out_shape=[_sds((S, D), BF16), _sds((S, PW), F32), _sds((S, SBW), BF16), _sds((S, SBW), BF16),
                   _sds((S, SBW), BF16), _sds((S, D), BF16), _sds((S, D), BF16)],
        scratch_shapes=_vmem_like(w_in),
        compiler_params=_params(("arbitrary",), 48), free=(1,), after=after)


def _hilo_dot(x, tri):
    hi = x.astype(BF16)
    lo = (x - hi.astype(F32)).astype(BF16)
    return _nn(hi, tri) + _nn(lo, tri)


def _log_terms(qk):
    z2 = qk * (SCALE * LOG2E)
    lb = jnp.minimum(z2, 0.0) - jnp.log2(1.0 + jnp.exp2(-jnp.abs(z2)))
    return lb, lb - z2


def _head_masks():
    lane = lax.broadcasted_iota(jnp.int32, (1, 2 * DH), 1)
    return (lane < DH, lane >= DH)


def _attn_fwd(q, k, v, exchange=None):
    T = TA

    def body(q_ref, k_ref, v_ref, o_ref, c_ref):
        i2 = 2 * pl.program_id(1)
        row = lax.broadcasted_iota(jnp.int32, (T, T), 0)
        col = lax.broadcasted_iota(jnp.int32, (T, T), 1)
        after = (row > col).astype(BF16)
        causal = col < row
        masks = _head_masks()
        qms = {}
        for b in range(QB):
            q2 = q_ref[b * T:(b + 1) * T, :]
            for h, hm in enumerate(masks):
                qms[b, h] = jnp.where(hm, q2, jnp.zeros_like(q2))

        def blocks(keys, pairs, carries, os):
            ks, vms = [], []
            for j in keys:
                rows = pl.ds(pl.multiple_of(j * T, T), T)
                vj = v_ref[rows, :]
                ks.append(k_ref[rows, :])
                vms.append([jnp.where(hm, vj, jnp.zeros_like(vj)) for hm in masks])
            units = [(n, h) for n in range(len(pairs)) for h in range(2)]
            qks = {(n, h): _nt(qms[pairs[n][0], h], ks[pairs[n][1]]) for n, h in units}
            lbs, l1ms = {}, {}
            for u in units:
                lbs[u], l1m = _log_terms(qks[u])
                l1ms[u] = jnp.where(causal, l1m, 0.0) if pairs[u[0]][2] else l1m
            cins = {u: _hilo_dot(l1ms[u], after) for u in units}
            carries, os = dict(carries), list(os)
            for n, h in units:
                b, key, diag = pairs[n]
                a = jnp.exp2(lbs[n, h] + cins[n, h] + carries[b, h])
                if diag:
                    a = jnp.where(causal, a, 0.0)
                os[b] = os[b] + _nn(a.astype(BF16), vms[key][h])
                carries[b, h] = carries[b, h] + jnp.sum(l1ms[n, h], axis=1, keepdims=True)
            return carries, tuple(os)

        carries = {(b, h): jnp.zeros((T, 1), F32) for b in range(QB) for h in range(2)}
        os = tuple(jnp.zeros((T, 2 * DH), F32) for _ in range(QB))
        carries, os = blocks([i2 + 1, i2], [(1, 0, True), (0, 1, True), (1, 1, False)], carries, os)
        carries, os = lax.fori_loop(
            0, i2 // 2,
            lambda t, c: blocks([i2 - 1 - 2 * t, i2 - 2 - 2 * t],
                                [(0, 0, False), (1, 0, False), (0, 1, False), (1, 1, False)], c[0], c[1]),
            (carries, os))
        for b in range(QB):
            o_ref[b * T:(b + 1) * T, :] = os[b].astype(BF16)
            c_ref[b * T:(b + 1) * T, :] = jnp.where(masks[0], carries[b, 0], carries[b, 1])

    blk = pl.BlockSpec((QB * T, 2 * DH), lambda p, i: (i, p))
    full = pl.BlockSpec((S, 2 * DH), lambda p, i: (0, p))
    return _call(
        body, (q, k, v), name="attn_fwd", grid=(SBW // (2 * DH), S // (QB * T)),
        in_specs=[blk, full, full], out_specs=[blk, blk],
        out_shape=[_sds((S, SBW), BF16), _sds((S, SBW), F32)],
        compiler_params=_params(("arbitrary", "arbitrary"), 40), exchange=exchange)


def _attn_bwd(q, k, v, do, ctot, after=()):
    T = TA
    nq = S // (QB * T)

    def body(q_ref, k_ref, v_ref, do_ref, c_ref, dq_ref, dk_ref, dv_ref, dk_acc, dv_acc):
        step = pl.program_id(1)
        i2 = 2 * step

        @pl.when(step == 0)
        def _():
            dk_acc[...] = jnp.zeros_like(dk_acc)
            dv_acc[...] = jnp.zeros_like(dv_acc)

        row = lax.broadcasted_iota(jnp.int32, (T, T), 0)
        col = lax.broadcasted_iota(jnp.int32, (T, T), 1)
        upto = (row <= col).astype(BF16)
        before = (row < col).astype(BF16)
        causal = col < row
        masks = _head_masks()
        qms, doms, ctots = {}, {}, {}
        for b in range(QB):
            q2, do2 = q_ref[b * T:(b + 1) * T, :], do_ref[b * T:(b + 1) * T, :]
            for h, hm in enumerate(masks):
                qms[b, h] = jnp.where(hm, q2, jnp.zeros_like(q2))
                doms[b, h] = jnp.where(hm, do2, jnp.zeros_like(do2))
                ctots[b, h] = c_ref[b * T:(b + 1) * T, h * DH:h * DH + 1]

        def blocks(keys, pairs, sums, dqs):
            rows = [pl.ds(pl.multiple_of(j * T, T), T) for j in keys]
            ks, vs = [k_ref[r, :] for r in rows], [v_ref[r, :] for r in rows]
            kms = [[jnp.where(hm, kj, jnp.zeros_like(kj)) for hm in masks] for kj in ks]
            units = [(n, h) for n in range(len(pairs)) for h in range(2)]
            qks = {(n, h): _nt(qms[pairs[n][0], h], ks[pairs[n][1]]) for n, h in units}
            das = {(n, h): _nt(doms[pairs[n][0], h], vs[pairs[n][1]]) for n, h in units}
            lbs, l1ms = {}, {}
            for u in units:
                lbs[u], l1m = _log_terms(qks[u])
                l1ms[u] = jnp.where(causal, l1m, 0.0) if pairs[u[0]][2] else l1m
            pins = {u: _hilo_dot(l1ms[u], upto) for u in units}
            sums = dict(sums)
            a_s, dls, cps = {}, {}, {}
            for n, h in units:
                b, _, diag = pairs[n]
                cl, cp = sums[b, h]
                a = jnp.exp2(lbs[n, h] + (ctots[b, h] - cl) - pins[n, h])
                if diag:
                    a = jnp.where(causal, a, 0.0)
                a_s[n, h] = a.astype(BF16)
                dls[n, h] = das[n, h] * a
                cps[n, h] = cp
                sums[b, h] = (cl + jnp.sum(l1ms[n, h], axis=1, keepdims=True),
                              cp + jnp.sum(dls[n, h], axis=1, keepdims=True))
            pexs = {u: _hilo_dot(dls[u], before) for u in units}
            dzbs = {}
            for u in units:
                dz = dls[u] - jnp.exp2(lbs[u]) * (dls[u] + pexs[u] + cps[u])
                if pairs[u[0]][2]:
                    dz = jnp.where(causal, dz, 0.0)
                dzbs[u] = dz.astype(BF16)
            dqs = list(dqs)
            for n, h in units:
                dqs[pairs[n][0]] = dqs[pairs[n][0]] + _nn(dzbs[n, h], kms[pairs[n][1]][h])
            for key, r in enumerate(rows):
                mine = [(n, h) for n, h in units if pairs[n][1] == key]
                dk_acc[r, :] += functools.reduce(jnp.add, [_tn(dzbs[u], qms[pairs[u[0]][0], u[1]]) for u in mine])
                dv_acc[r, :] += functools.reduce(jnp.add, [_tn(a_s[u], doms[pairs[u[0]][0], u[1]]) for u in mine])
            return sums, tuple(dqs)

        zero = jnp.zeros((T, 1), F32)
        sums = {(b, h): (zero, zero) for b in range(QB) for h in range(2)}
        dqs = tuple(jnp.zeros((T, 2 * DH), F32) for _ in range(QB))
        sums, dqs = lax.fori_loop(
            0, i2 // 2,
            lambda t, c: blocks([2 * t, 2 * t + 1],
                                [(0, 0, False), (1, 0, False), (0, 1, False), (1, 1, False)], c[0], c[1]),
            (sums, dqs))
        _, dqs = blocks([i2, i2 + 1], [(0, 0, True), (1, 0, False), (1, 1, True)], sums, dqs)
        for b in range(QB):
            dq_ref[b * T:(b + 1) * T, :] = (dqs[b] * SCALE).astype(BF16)

        @pl.when(step == nq - 1)
        def _():
            dk_ref[...] = (dk_acc[...] * SCALE).astype(BF16)
            dv_ref[...] = dv_acc[...].astype(BF16)

    blk = pl.BlockSpec((QB * T, 2 * DH), lambda p, i: (i, p))
    full = pl.BlockSpec((S, 2 * DH), lambda p, i: (0, p))
    return _call(
        body, (q, k, v, do, ctot), name="attn_bwd", grid=(SBW // (2 * DH), nq),
        in_specs=[blk, full, full, blk, blk], out_specs=[blk, full, full],
        out_shape=[_sds((S, SBW), BF16), _sds((S, SBW), BF16), _sds((S, SBW), BF16)],
        scratch_shapes=[pltpu.VMEM((S, 2 * DH), F32), pltpu.VMEM((S, 2 * DH), F32)],
        compiler_params=_params(("arbitrary", "arbitrary"), 40), after=after)


def _pool_counts(first_row, tm):
    pos = first_row + lax.broadcasted_iota(jnp.int32, (tm, 1), 0)
    return [jnp.minimum(pos + 1, w).astype(F32) for w in POOL_WINDOWS]


def _mix_out(h, xp, o_sb, gp, gs, w_group, scale, w_bp, w_ba, w_out, exchange=None):
    tm = 512

    def body(h_ref, xp_ref, o_ref, gp_ref, gs_ref, wg_hbm, sc_ref, wbp_hbm, wba_hbm, wo_hbm,
             h2_ref, pm_ref, p_ref, yp_ref, ys_ref, m_ref, halo, wg_ref, wbp_ref, wba_ref, wo_ref):
        _stage([(wg_hbm, wg_ref), (wbp_hbm, wbp_ref), (wba_hbm, wba_ref), (wo_hbm, wo_ref)])
        i = pl.program_id(0)

        @pl.when(i == 0)
        def _():
            halo[...] = jnp.zeros_like(halo)

        xp = xp_ref[...]
        ext = jnp.concatenate([halo[...], xp], axis=0)
        halo[...] = xp[tm - HALO:, :]
        counts = _pool_counts(i * tm, tm)
        for gi in range(len(POOL_WINDOWS)):
            lanes = slice(gi * PG, (gi + 1) * PG)
            win = ext[:, lanes]
            for step in range(gi + 1):
                win = win + pltpu.roll(win, 1 << step, 0)
            pm = (win[HALO:, :] / counts[gi] - xp[:, lanes]).astype(BF16)
            pm_ref[:, lanes] = pm
            p_ref[:, lanes] = (_nn(pm, wg_ref[gi]) * sc_ref[:, lanes]).astype(BF16)
        pb = p_ref[...]
        ob = o_ref[...]
        for j in range(NSH):
            cols = slice(j * (D // NSH), (j + 1) * (D // NSH))
            yp = _nn(pb, wbp_ref[j])
            ys = _nn(ob, wba_ref[j])
            yp_ref[:, cols] = yp.astype(BF16)
            ys_ref[:, cols] = ys.astype(BF16)
            m_ref[:, cols] = (gp_ref[:, cols].astype(F32) * yp + gs_ref[:, cols].astype(F32) * ys).astype(BF16)
        h2_ref[...] = h_ref[...] + _nn(m_ref[...], wo_ref[...])

    return _call(
        body, (h, xp, o_sb, gp, gs, w_group, scale, w_bp, w_ba, w_out), name="mix_out", grid=(S // tm,),
        in_specs=[_rows(tm, D), _rows(tm, PW), _rows(tm, SBW), _rows(tm, D), _rows(tm, D),
                  _ANY, _fixed((1, PW)), _ANY, _ANY, _ANY],
        out_specs=[_rows(tm, D), _rows(tm, PW), _rows(tm, PW), _rows(tm, D), _rows(tm, D), _rows(tm, D)],
        out_shape=[_sds((S, D), F32), _sds((S, PW), BF16), _sds((S, PW), BF16), _sds((S, D), BF16),
                   _sds((S, D), BF16), _sds((S, D), BF16)],
        scratch_shapes=[pltpu.VMEM((HALO, PW), F32)] + _vmem_like(w_group, w_bp, w_ba, w_out),
        compiler_params=_params(("arbitrary",), 48), free=(5, 6), exchange=exchange)


def _mix_bwd_out(dh, gp, gs, yp, ys, pm, w_group, scale, w_bp, w_ba, w_out, exchange=None):
    tm = 512
    nt = S // tm

    def body(dh_ref, gp_ref, gs_ref, yp_ref, ys_ref, pm_ref, wg_hbm, sc_ref, wbp_hbm, wba_hbm, wo_hbm,
             dlg_ref, dyp_ref, dys_ref, do_ref, dyg_ref, dxp_ref, dsc_ref, halo, wg_ref, wbp_ref, wba_ref, wo_ref):
        _stage([(wg_hbm, wg_ref), (wbp_hbm, wbp_ref), (wba_hbm, wba_ref), (wo_hbm, wo_ref)])
        step = pl.program_id(0)

        @pl.when(step == 0)
        def _():
            halo[...] = jnp.zeros_like(halo)
            dsc_ref[...] = jnp.zeros_like(dsc_ref)

        dm = _nt(dh_ref[...].astype(BF16), wo_ref[...])
        gp = gp_ref[...].astype(F32)
        gs = gs_ref[...].astype(F32)
        yp = yp_ref[...].astype(F32)
        ys = ys_ref[...].astype(F32)
        dlg_ref[:, :D] = (dm * yp * gp * (1.0 - gp)).astype(BF16)
        dlg_ref[:, D:] = (dm * ys * gs * (1.0 - gs)).astype(BF16)
        dyp_ref[...] = (dm * gp).astype(BF16)
        dys_ref[...] = (dm * gs).astype(BF16)
        dp = jnp.zeros((tm, PW), F32)
        do = jnp.zeros((tm, SBW), F32)
        for j in range(NSH):
            cols = slice(j * (D // NSH), (j + 1) * (D // NSH))
            dp = dp + _nt(dyp_ref[:, cols], wbp_ref[j])
            do = do + _nt(dys_ref[:, cols], wba_ref[j])
        do_ref[...] = do.astype(BF16)
        counts = _pool_counts((nt - 1 - step) * tm, tm)
        dscale = []
        for gi in range(len(POOL_WINDOWS)):
            lanes = slice(gi * PG, (gi + 1) * PG)
            dpg = dp[:, lanes]
            dscale.append(jnp.sum(dpg * _nn(pm_ref[:, lanes], wg_ref[gi]), axis=0, keepdims=True))
            dyg = (dpg * sc_ref[:, lanes]).astype(BF16)
            dyg_ref[:, lanes] = dyg
            dpm = _nt(dyg, wg_ref[gi])
            per = dpm / counts[gi]
            win = jnp.concatenate([per, halo[:, lanes]], axis=0)
            halo[:, lanes] = per[:HALO, :]
            for s in range(gi + 1):
                win = win + pltpu.roll(win, tm + HALO - (1 << s), 0)
            dxp_ref[:, lanes] = (win[:tm, :] - dpm).astype(BF16)
        dsc_ref[...] += jnp.concatenate(dscale, axis=1)

    rev = lambda width: pl.BlockSpec((tm, width), lambda i: (nt - 1 - i, 0))
    return _call(
        body, (dh, gp, gs, yp, ys, pm, w_group, scale, w_bp, w_ba, w_out), name="mix_bwd_out", grid=(nt,),
        in_specs=[rev(D), rev(D), rev(D), rev(D), rev(D), rev(PW), _ANY, _fixed((1, PW)), _ANY, _ANY, _ANY],
        out_specs=[rev(2 * D), rev(D), rev(D), rev(SBW), rev(PW), rev(PW), _fixed((1, PW))],
        out_shape=[_sds((S, 2 * D), BF16), _sds((S, D), BF16), _sds((S, D), BF16), _sds((S, SBW), BF16),
                   _sds((S, PW), BF16), _sds((S, PW), BF16), _sds((1, PW), F32)],
        scratch_shapes=[pltpu.VMEM((HALO, PW), F32)] + _vmem_like(w_group, w_bp, w_ba, w_out),
        compiler_params=_params(("arbitrary",), 48), exchange=exchange)


def _mix_bwd_in(dh, h, gain, pieces, w_in, exchange=None):
    tm = 512
    widths = [p.shape[1] for p in pieces]

    def body(dh_ref, h_ref, g_ref, *rest):
        piece_refs, (w_hbm, dx_ref, dg_ref, w_ref, dp_ref) = rest[:len(pieces)], rest[len(pieces):]
        _stage([(w_hbm, w_ref)])
        at = 0
        for ref, width in zip(piece_refs, widths):
            dp_ref[:, at:at + width] = ref[...]
            at += width
        du = jnp.zeros((tm, D), F32)
        for j in range(NSH):
            du = du + _nt(dp_ref[:, j * D:(j + 1) * D], w_ref[j])
        r, hr = _rms(h_ref[...])
        dx, dgain = _rms_bwd(du, hr, r, g_ref[...])
        dx_ref[...] = dh_ref[...] + dx

        @pl.when(pl.program_id(0) == 0)
        def _():
            dg_ref[...] = jnp.zeros_like(dg_ref)

        dg_ref[...] += dgain

    return _call(
        body, (dh, h, gain, *pieces, w_in), name="mix_bwd_in", grid=(S // tm,),
        in_specs=[_rows(tm, D), _rows(tm, D), _fixed((1, D))] + [_rows(tm, w) for w in widths] + [_ANY],
        out_specs=[_rows(tm, D), _fixed((1, D))],
        out_shape=[_sds((S, D), F32), _sds((1, D), F32)],
        scratch_shapes=_vmem_like(w_in) + [pltpu.VMEM((tm, 4 * D), BF16)],
        compiler_params=_params(("arbitrary",), 48), exchange=exchange)


def _wgrad_in(u, pieces):
    dxp, dq, dk, dv, dlg = pieces

    def body(u_ref, dxp_ref, dq_ref, dk_ref, dv_ref, dlg_ref, o_ref):
        j = pl.program_id(0)
        u = u_ref[...]

        def two(left_ref, right_ref):
            o_ref[:, :PW] = _tn(u, left_ref[...]).astype(BF16)
            o_ref[:, PW:] = _tn(u, right_ref[...]).astype(BF16)

        pl.when(j == 0)(lambda: two(dxp_ref, dq_ref))
        pl.when(j == 1)(lambda: two(dk_ref, dv_ref))

        @pl.when(j >= 2)
        def _():
            o_ref[...] = _tn(u, dlg_ref[...]).astype(BF16)

    whole = lambda width: pl.BlockSpec((S, width), lambda j: (0, 0))
    return _call(
        body, (u, dxp, dq, dk, dv, dlg), name="wgrad_in", grid=(NSH,),
        in_specs=[whole(D), whole(PW), whole(SBW), whole(SBW), whole(SBW),
                  pl.BlockSpec((S, D), lambda j: (0, jnp.maximum(j - 2, 0)))],
        out_specs=[pl.BlockSpec((None, D, D), lambda j: (j, 0, 0))], out_shape=[_sds((NSH, D, D), BF16)],
        compiler_params=_params(("arbitrary",), 56))[0]


def _wgrad(a, b, nblk, ti, name, out_dtype=BF16, exchange=None, after=()):
    ka, n = a.shape[1], b.shape[1]
    ns = n // nblk

    def body(a_ref, b_ref, o_ref):
        o_ref[...] = _tn(a_ref[...].astype(BF16), b_ref[...].astype(BF16)).astype(out_dtype)

    res = _call(
        body, (a, b), name=name, grid=(nblk, ka // ti),
        in_specs=[pl.BlockSpec((S, ti), lambda j, i: (0, i)), pl.BlockSpec((S, ns), lambda j, i: (0, j))],
        out_specs=[pl.BlockSpec((None, ti, ns), lambda j, i: (j, i, 0))],
        out_shape=[_sds((nblk, ka, ns), out_dtype)],
        compiler_params=_params(("arbitrary", "arbitrary"), 56), exchange=exchange, after=after)
    return res[0] if exchange is None else (res[0][0], res[1])


def _wgrad_branches(p, dyp, o_sb, dys, pm, dyg):
    cols = D // NSH

    def body(p_ref, dyp_ref, o_ref, dys_ref, pm_ref, dyg_ref, gbp_ref, gba_ref, gg_ref):
        gbp_ref[...] = _tn(p_ref[...], dyp_ref[...]).astype(BF16)
        gba_ref[...] = _tn(o_ref[...], dys_ref[...]).astype(BF16)
        gg_ref[...] = _tn(pm_ref[...], dyg_ref[...])

    whole = lambda width: pl.BlockSpec((S, width), lambda j: (0, 0))
    col = lambda width: pl.BlockSpec((S, width), lambda j: (0, j))
    return _call(
        body, (p, dyp, o_sb, dys, pm, dyg), name="wgrad_branches", grid=(NSH,),
        in_specs=[whole(PW), col(cols), whole(SBW), col(cols), col(PG), col(PG)],
        out_specs=[pl.BlockSpec((None, PW, cols), lambda j: (j, 0, 0)),
                   pl.BlockSpec((None, SBW, cols), lambda j: (j, 0, 0)),
                   pl.BlockSpec((None, PG, PG), lambda j: (j, 0, 0))],
        out_shape=[_sds((NSH, PW, cols), BF16), _sds((NSH, SBW, cols), BF16), _sds((NSH, PG, PG), F32)],
        compiler_params=_params(("arbitrary",), 40))


def _place():
    x, y, c = lax.axis_index("x"), lax.axis_index("y"), lax.axis_index("c")
    chips = [(1 - x, y), (x, 1 - y), (1 - x, 1 - y)]
    return x, y, c, chips


def _remote(src, dst, ssem, rsem, dev):
    return pltpu.make_async_remote_copy(src_ref=src, dst_ref=dst, send_sem=ssem, recv_sem=rsem,
                                        device_id=dev, device_id_type=MESH)


def _cast_into_block(ws, me_idx, name):
    steps = 4
    shapes = [(w.shape[0] // steps, w.shape[1]) for w in ws]

    def body(me_ref, *refs):
        for w_ref, o_ref in zip(refs[:len(ws)], refs[len(ws):]):
            o_ref[...] = w_ref[...].astype(BF16)

    return pl.pallas_call(
        body, name=name, out_shape=[_sds((NSH,) + w.shape, BF16) for w in ws],
        grid_spec=pltpu.PrefetchScalarGridSpec(
            num_scalar_prefetch=1, grid=(steps,),
            in_specs=[pl.BlockSpec((r, c), lambda s, me: (s, 0)) for r, c in shapes],
            out_specs=[pl.BlockSpec((None, r, c), lambda s, me: (me[0], s, 0)) for r, c in shapes]),
        compiler_params=_params(("arbitrary",), 32),
    )(me_idx, *ws)


def _ex_gather(bufs):
    n = len(bufs)
    per = 8

    def plan(outs, ssem, rsem, w):
        x, y, c, _ = _place()
        sib, nbr_x, nbr_y = (x, y, 1 - c), (1 - x, y, c), (x, 1 - y, c)
        half = outs[w].shape[1] // 2
        quarter = half // 2
        sem = lambda k: (ssem.at[per * w + k], rsem.at[per * w + k])
        rows = lambda blk, start, size: outs[w].at[blk, pl.ds(start, size)]
        mine = rows(2 * x + y, c * half, half)
        from_x = rows(2 * (1 - x) + y, c * half, half)
        from_y = rows(2 * x + (1 - y), c * half, half)
        diag = 2 * (1 - x) + (1 - y)
        pass_y = rows(2 * (1 - x) + y, c * half, quarter)
        pass_x = rows(2 * x + (1 - y), c * half + quarter, quarter)
        diag_0, diag_1 = rows(diag, c * half, quarter), rows(diag, c * half + quarter, quarter)
        first = [_remote(mine, mine, *sem(0), nbr_x), _remote(mine, mine, *sem(1), nbr_y)]
        arrivals = [
            (_remote(from_x, from_x, *sem(0), nbr_x),
             [_remote(pass_y, pass_y, *sem(2), nbr_y), _remote(from_x, from_x, *sem(4), sib)]),
            (_remote(from_y, from_y, *sem(1), nbr_y),
             [_remote(pass_x, pass_x, *sem(3), nbr_x), _remote(from_y, from_y, *sem(5), sib)]),
            (_remote(diag_0, diag_0, *sem(2), nbr_y), [_remote(diag_0, diag_0, *sem(6), sib)]),
            (_remote(diag_1, diag_1, *sem(3), nbr_x), [_remote(diag_1, diag_1, *sem(7), sib)]),
        ]
        other = (1 - c) * half
        from_sibling = [
            _remote(rows(2 * (1 - x) + y, other, half), rows(2 * (1 - x) + y, other, half), *sem(4), sib),
            _remote(rows(2 * x + (1 - y), other, half), rows(2 * x + (1 - y), other, half), *sem(5), sib),
            _remote(rows(diag, other, quarter), rows(diag, other, quarter), *sem(6), sib),
            _remote(rows(diag, other + quarter, quarter), rows(diag, other + quarter, quarter), *sem(7), sib),
        ]
        return first, arrivals, from_sibling

    def start(ins, outs, ssem, rsem):
        x, y, c, _ = _place()
        for w in range(n):
            half = outs[w].shape[1] // 2
            mine = outs[w].at[2 * x + y, pl.ds(c * half, half)]
            _remote(mine, mine, ssem.at[per * w], rsem.at[per * w], (1 - x, y, c)).start()
            _remote(mine, mine, ssem.at[per * w + 1], rsem.at[per * w + 1], (x, 1 - y, c)).start()

    def finish(ins, outs, ssem, rsem):
        plans = [plan(outs, ssem, rsem, w) for w in range(n)]
        started = []
        for direct in (True, False):
            for first, arrivals, _ in plans:
                for arrived, onward in (arrivals[:2] if direct else arrivals[2:]):
                    arrived.wait_recv()
                    for cp in onward:
                        cp.start()
                    started += onward
        for first, _, from_sibling in plans:
            for cp in from_sibling:
                cp.wait_recv()
            started += first
        for cp in started:
            cp.wait_send()

    return Exchange(bufs, [_sds(b.shape, b.dtype) for b in bufs], {w: w for w in range(n)}, per * n, start, finish)


def _ex_gather_direct(bufs):
    n = len(bufs)

    def copies(outs, ssem, rsem, only_first=False):
        x, y, c, chips = _place()
        me, sib = 2 * x + y, (x, y, 1 - c)
        first, relay, last = [], [], []
        for w in range(n):
            half = outs[w].shape[1] // 2
            mine = outs[w].at[me, pl.ds(c * half, half)]
            for k, (px, py) in enumerate(chips):
                sems = (ssem.at[6 * w + k], rsem.at[6 * w + k])
                sib_sems = (ssem.at[6 * w + 3 + k], rsem.at[6 * w + 3 + k])
                first.append(_remote(mine, mine, *sems, (px, py, c)))
                if only_first:
                    continue
                got = outs[w].at[2 * px + py, pl.ds(c * half, half)]
                relay.append((_remote(got, got, *sems, (px, py, c)), _remote(got, got, *sib_sems, sib)))
                theirs = outs[w].at[2 * px + py, pl.ds((1 - c) * half, half)]
                last.append(_remote(theirs, theirs, *sib_sems, sib))
        return first, relay, last

    def start(ins, outs, ssem, rsem):
        for cp in copies(outs, ssem, rsem, only_first=True)[0]:
            cp.start()

    def finish(ins, outs, ssem, rsem):
        first, relay, last = copies(outs, ssem, rsem)
        for arrived, onward in relay:
            arrived.wait_recv()
            onward.start()
        for cp in last:
            cp.wait_recv()
        for cp in first:
            cp.wait_send()
        for _, onward in relay:
            onward.wait_send()

    return Exchange(bufs, [_sds(b.shape, b.dtype) for b in bufs], {w: w for w in range(n)}, 6 * n, start, finish)


def _simple_exchange(arrays, landing, aliases, make_copies, sibling_only=False):
    def start(ins, outs, ssem, rsem):
        for cp, _ in make_copies(ins, outs, ssem, rsem, False):
            cp.start()

    def finish(ins, outs, ssem, rsem):
        cps = make_copies(ins, outs, ssem, rsem, True)
        for _, landed in cps:
            landed.wait_recv()
        for cp, _ in cps:
            cp.wait_send()

    return Exchange(arrays, landing, aliases, len(arrays) * 3, start, finish, sibling_only)


def _ex_pair_swap(grads):
    def make(ins, outs, ssem, rsem, landing):
        x, y, c, _ = _place()
        cps = [_remote(ins[w].at[:, 1 - c], outs[w], ssem.at[w], rsem.at[w], (x, y, 1 - c))
               for w in range(len(grads))]
        return [(cp, cp) for cp in cps]

    return _simple_exchange(grads, [_sds((NSH,) + g.shape[2:], g.dtype) for g in grads], {}, make, True)


def _ex_relay(bufs):
    def make(ins, outs, ssem, rsem, landing):
        x, y, c, chips = _place()
        sib = (x, y, 1 - c)
        out = []
        for w in range(len(bufs)):
            half = outs[w].shape[1] // 2
            for k, (px, py) in enumerate(chips):
                sems = (ssem.at[3 * w + k], rsem.at[3 * w + k])
                have = outs[w].at[2 * px + py, pl.ds(c * half, half)]
                miss = outs[w].at[2 * px + py, pl.ds((1 - c) * half, half)]
                out.append((_remote(have, have, *sems, sib), _remote(miss, miss, *sems, sib) if landing else None))
        return out

    return _simple_exchange(bufs, [_sds(b.shape, b.dtype) for b in bufs], {w: w for w in range(len(bufs))}, make, True)


def _ex_share(bufs):
    def make(ins, outs, ssem, rsem, landing):
        x, y, c, _ = _place()
        sib = (x, y, 1 - c)
        return [(_remote(outs[w].at[c], outs[w].at[c], ssem.at[w], rsem.at[w], sib),
                 _remote(outs[w].at[1 - c], outs[w].at[1 - c], ssem.at[w], rsem.at[w], sib) if landing else None)
                for w in range(len(bufs))]

    return _simple_exchange(bufs, [_sds(b.shape, b.dtype) for b in bufs], {w: w for w in range(len(bufs))}, make, True)


def _small_copies(slots, ssems, rsems, sending):
    x, y, c, _ = _place()
    out = []
    for m in range(1, 8):
        px, py, pc = x ^ (m >> 2), y ^ ((m >> 1) & 1), c ^ (m & 1)
        slot = slots.at[4 * x + 2 * y + c if sending else 4 * px + 2 * py + pc]
        out.append(_remote(slot, slot, ssems[m - 1], rsems[m - 1], (px, py, pc)))
    return out


def _small_gather_start(slots, name, after=()):
    at = 1 + len(after)

    def body(*refs):
        for cp in _small_copies(refs[0], refs[at:at + 7], refs[at + 7:at + 14], True):
            cp.start()
        refs[-1][...] = jnp.zeros_like(refs[-1])

    outs = pl.pallas_call(
        body, name=name,
        out_shape=([pltpu.SemaphoreType.DMA(())] * 14 + [pltpu.HBM(slots.shape, slots.dtype)]
                   + [jax.ShapeDtypeStruct((8, 128), F32)]),
        in_specs=[_HBM] + [_ANY] * len(after), out_specs=[_SEM] * 14 + [_HBM, _VM], input_output_aliases={0: 14},
        compiler_params=pltpu.CompilerParams(has_side_effects=_EFFECT),
    )(*_in_hbm([slots]), *after)
    return outs[:14], outs[14], outs[15]


def _small_gather_wait(sems, slots, after, name):
    def body(*refs):
        for cp in _small_copies(refs[0], refs[1:8], refs[8:15], True):
            cp.wait_send()
        for cp in _small_copies(refs[0], refs[1:8], refs[8:15], False):
            cp.wait_recv()

    return pl.pallas_call(
        body, name=name, out_shape=pltpu.HBM(slots.shape, slots.dtype),
        in_specs=[_HBM] + [_SEM] * 14 + [_ANY] * len(after), out_specs=_HBM, input_output_aliases={0: 0},
        compiler_params=pltpu.CompilerParams(has_side_effects=_EFFECT),
    )(slots, *sems, *after)


def _pair_sum(grads, gots, c_idx, name):
    n = len(grads)

    def body(c_ref, *refs):
        for a_ref, b_ref, o_ref in zip(refs[:n], refs[n:2 * n], refs[2 * n:]):
            o_ref[...] = (a_ref[...].astype(F32) + b_ref[...].astype(F32)).astype(BF16)

    halves = [g.shape[2:] for g in grads]
    return list(pl.pallas_call(
        body, name=name, out_shape=[_sds((NSH,) + h, BF16) for h in halves],
        grid_spec=pltpu.PrefetchScalarGridSpec(
            num_scalar_prefetch=1, grid=(NSH,),
            in_specs=[pl.BlockSpec((None, None) + h, lambda j, c: (j, c[0], 0, 0)) for h in halves]
            + [pl.BlockSpec((None,) + h, lambda j, c: (j, 0, 0)) for h in halves],
            out_specs=[pl.BlockSpec((None,) + h, lambda j, c: (j, 0, 0)) for h in halves]),
        compiler_params=_params(("arbitrary",), 40),
    )(c_idx, *_in_hbm(list(grads) + list(gots))))


def _chip_sum(owns, gots, place, name):
    n = len(owns)

    def body(place_ref, *refs):
        for own_ref, got_ref, o_ref in zip(refs[:n], refs[n:2 * n], refs[2 * n:]):
            acc = own_ref[...].astype(F32)
            for k in range(3):
                acc = acc + got_ref[k].astype(F32)
            o_ref[...] = acc

    shapes = [(o.shape[1] // 2, o.shape[2]) for o in owns]
    return list(pl.pallas_call(
        body, name=name, out_shape=[_sds((2, 2 * r, c), F32) for r, c in shapes],
        grid_spec=pltpu.PrefetchScalarGridSpec(
            num_scalar_prefetch=1, grid=(2,),
            in_specs=[pl.BlockSpec((None, r, c), lambda s, p: (p[0], s, 0)) for r, c in shapes]
            + [pl.BlockSpec((3, r, c), lambda s, p: (0, s, 0)) for r, c in shapes],
            out_specs=[pl.BlockSpec((None, r, c), lambda s, p: (p[1], s, 0)) for r, c in shapes]),
        compiler_params=_params(("arbitrary",), 40),
    )(place, *_in_hbm(list(owns) + list(gots))))


def _adamw_math(w, g, m, v):
    m = B1 * m + (1.0 - B1) * g
    v = B2 * v + (1.0 - B2) * (g * g)
    m_hat = m / (1.0 - B1 ** STEP)
    v_hat = v / (1.0 - B2 ** STEP)
    return -LR * (m_hat / (jnp.sqrt(v_hat) + AEPS) + WD * w), m, v


def _adamw(ws, gs, ms, vs, name, after=()):
    n, steps = len(ws), 4

    def body(*refs):
        ins, outs = refs[:4 * n], refs[4 * n:]
        for i in range(n):
            w_ref, g_ref, m_ref, v_ref = ins[4 * i:4 * i + 4]
            go_ref, d_ref, nm_ref, nv_ref = outs[4 * i:4 * i + 4]
            g = g_ref[...]
            go_ref[...] = g
            d_ref[...], nm_ref[...], nv_ref[...] = _adamw_math(w_ref[...], g, m_ref[...], v_ref[...])

    args, specs, shapes, free = [], [], [], []
    for i, (w, g, m, v) in enumerate(zip(ws, gs, ms, vs)):
        args += [w, g, m, v]
        specs += [pl.BlockSpec((w.shape[0] // steps, w.shape[1]), lambda r: (r, 0))] * 4
        shapes += [_sds(w.shape, F32)] * 4
        free += [4 * i, 4 * i + 2, 4 * i + 3]
    outs = _call(body, args, name=name, grid=(steps,), out_shape=shapes, in_specs=specs, out_specs=specs,
                 compiler_params=_params(("arbitrary",), 48), free=tuple(free), after=after)
    return [outs[4 * i:4 * i + 4] for i in range(n)]


def _small_update(gathered, w, m, v, entries):
    rows = w.shape[0]

    def body(ga_ref, w_ref, m_ref, v_ref, *out_refs):
        for j, (first, n) in enumerate(entries):
            mine = slice(first, first + n)
            g = ga_ref[mine, :]
            for dev in range(1, 8):
                g = g + ga_ref[dev * rows + first:dev * rows + first + n, :]
            results = (g,) + _adamw_math(w_ref[mine, :], g, m_ref[mine, :], v_ref[mine, :])
            for i, res in enumerate(results):
                out_refs[i * len(entries) + j][...] = res

    outs = pl.pallas_call(
        body, name="small_update",
        out_shape=[jax.ShapeDtypeStruct((n, 128), F32) for _ in range(4) for _, n in entries],
        in_specs=[_VM] * 4, out_specs=[_VM] * (4 * len(entries)),
    )(gathered, w, m, v)
    return [outs[i * len(entries):(i + 1) * len(entries)] for i in range(4)]


SMALL = ("ffn1_norm", "mix_norm", "ffn2_norm", "final_norm", "pool_scale", "loss", "pool_w_group")
BIG = ("ffn1_w_gate_up", "ffn1_w_down", "w_in", "w_branch_pool", "w_branch_attn", "w_out",
       "ffn2_w_gate_up", "ffn2_w_down")
ORDER = ("ffn1_norm", "ffn1_w_gate_up", "ffn1_w_down", "mix_norm", "w_in", "pool_w_group", "pool_scale",
         "w_branch_pool", "w_branch_attn", "w_out", "ffn2_norm", "ffn2_w_gate_up", "ffn2_w_down", "final_norm")
SMALL_ROWS = 560


def _pack_small(t):
    parts = []
    for k in SMALL:
        rows = t[k].reshape(-1, 128) if k in t else jnp.zeros((1, 128), F32)
        parts.append(jnp.pad(rows, ((0, -rows.shape[0] % 8), (0, 0))))
    packed = jnp.concatenate(parts, axis=0)
    assert packed.shape == (SMALL_ROWS, 128), packed.shape
    return packed


def _small_entries(like):
    out, at = [], 0
    for k in SMALL:
        n = like[k].size // 128 if k in like else 1
        out.append((at, n))
        at += n + (-n % 8)
    return out


def _halves(g):
    return g.reshape(NSH, 2, g.shape[1] // 2, g.shape[2])


def kernel(x, ffn1_norm, ffn1_w_gate_up, ffn1_w_down, mix_norm, w_in, pool_w_group, pool_scale, w_branch_pool, w_branch_attn, w_out, ffn2_norm, ffn2_w_gate_up, ffn2_w_down, final_norm, loss_target, m_ffn1_norm, m_ffn1_w_gate_up, m_ffn1_w_down, m_mix_norm, m_w_in, m_pool_w_group, m_pool_scale, m_w_branch_pool, m_w_branch_attn, m_w_out, m_ffn2_norm, m_ffn2_w_gate_up, m_ffn2_w_down, m_final_norm, v_ffn1_norm, v_ffn1_w_gate_up, v_ffn1_w_down, v_mix_norm, v_w_in, v_pool_w_group, v_pool_scale, v_w_branch_pool, v_w_branch_attn, v_w_out, v_ffn2_norm, v_ffn2_w_gate_up, v_ffn2_w_down, v_final_norm):
    wts = dict(ffn1_norm=ffn1_norm, ffn1_w_gate_up=ffn1_w_gate_up, ffn1_w_down=ffn1_w_down, mix_norm=mix_norm,
               w_in=w_in, pool_w_group=pool_w_group, pool_scale=pool_scale, w_branch_pool=w_branch_pool,
               w_branch_attn=w_branch_attn, w_out=w_out, ffn2_norm=ffn2_norm, ffn2_w_gate_up=ffn2_w_gate_up,
               ffn2_w_down=ffn2_w_down, final_norm=final_norm)
    mom = dict(ffn1_norm=m_ffn1_norm, ffn1_w_gate_up=m_ffn1_w_gate_up, ffn1_w_down=m_ffn1_w_down,
               mix_norm=m_mix_norm, w_in=m_w_in, pool_w_group=m_pool_w_group, pool_scale=m_pool_scale,
               w_branch_pool=m_w_branch_pool, w_branch_attn=m_w_branch_attn, w_out=m_w_out,
               ffn2_norm=m_ffn2_norm, ffn2_w_gate_up=m_ffn2_w_gate_up, ffn2_w_down=m_ffn2_w_down,
               final_norm=m_final_norm)
    var = dict(ffn1_norm=v_ffn1_norm, ffn1_w_gate_up=v_ffn1_w_gate_up, ffn1_w_down=v_ffn1_w_down,
               mix_norm=v_mix_norm, w_in=v_w_in, pool_w_group=v_pool_w_group, pool_scale=v_pool_scale,
               w_branch_pool=v_w_branch_pool, w_branch_attn=v_w_branch_attn, w_out=v_w_out,
               ffn2_norm=v_ffn2_norm, ffn2_w_gate_up=v_ffn2_w_gate_up, ffn2_w_down=v_ffn2_w_down,
               final_norm=v_final_norm)

    c_idx = lax.axis_index("c").astype(jnp.int32).reshape(1)
    me_idx = (2 * lax.axis_index("x") + lax.axis_index("y")).astype(jnp.int32).reshape(1)
    place = jnp.concatenate([me_idx, c_idx])
    x0, tgt = x[0], loss_target[0]
    wgrp = pool_w_group[0].astype(BF16)
    g1, gm, g2, gf = ffn1_norm, mix_norm, ffn2_norm, final_norm.reshape(1, D)
    grad, delta, new_m, new_v = {}, {}, {}, {}

    def pair_sums(keys, parts, got):
        return _pair_sum(parts, got, c_idx, "pair_sum_" + keys[0])

    def chip_sums(keys, chip_parts, owned):
        return _chip_sum(chip_parts, owned, place, "chip_sum_" + keys[0])

    def adamw(keys, after=()):
        outs = _adamw([wts[k][0] for k in keys], [grad[k][0] for k in keys], [mom[k][0] for k in keys],
                      [var[k][0] for k in keys], "adamw_" + keys[0], after=after)
        for k, res in zip(keys, outs):
            grad[k], delta[k], new_m[k], new_v[k] = (o.reshape(wts[k].shape) for o in res)

    first, late = ("ffn1_w_gate_up", "ffn1_w_down"), ("w_branch_pool", "w_branch_attn", "w_out",
                                                       "ffn2_w_gate_up", "ffn2_w_down")
    own = {}
    for group in (first, ("w_in",), late):
        own.update(zip(group, _cast_into_block([wts[k][0] for k in group], me_idx, "cast_" + group[0])))
    full = dict(zip(first, _exchange_alone(_ex_gather([own[k] for k in first]), "gather_ffn1")))
    wgu1, wd1 = full["ffn1_w_gate_up"], full["ffn1_w_down"].reshape(DFF, D)
    (h1, n1, gu1, a1), (win,) = _ffn_fwd(x0, g1, wgu1, wd1, "ffn1_fwd", exchange=_ex_gather_direct([own["w_in"]]))
    sems_l, thru_l, token_l = _gather_start([own[k_] for k_ in late], [h1], "gather_late_start")
    u, xp, q, k, v, gp, gs = _mix_in(h1, gm, win, after=(token_l,))
    o_sb, ctot = _attn_fwd(q, k, v)
    arrived = _gather_wait(sems_l, thru_l, [o_sb], "gather_late_wait")
    wbp, wba, wout = _exchange_alone(_ex_relay(arrived[:3]), "relay_mix")
    wout = wout.reshape(D, D)
    (h2, pm, p, yp, ys, mm), (wgu2, wd2) = _mix_out(h1, xp, o_sb, gp, gs, wgrp, pool_scale, wbp, wba, wout,
                                                    exchange=_ex_relay(arrived[3:]))
    wd2 = wd2.reshape(DFF, D)
    dh3, loss_row, d_gf, n3, gu3, a3 = _ffn_fwd(h2, g2, wgu2, wd2, "ffn2_fwd", head=(tgt, gf))

    def grad_gate_up(n, dgu, name, exchange=None):
        res = _wgrad(n, dgu, NSH, D, name, exchange=exchange)
        return [_halves(res)] if exchange is None else ([_halves(res[0])], res[1])

    def grad_down(a, dh, name, exchange=None):
        res = _wgrad(a, dh, 1, FFS, name, exchange=exchange)
        halves = lambda g: [_halves(g.reshape(NSH, DFF // NSH, D))]
        return halves(res) if exchange is None else (halves(res[0]), res[1])

    k_gu2, k_d2, k_gu1, k_d1, k_in = (("ffn2_w_gate_up",), ("ffn2_w_down",), ("ffn1_w_gate_up",),
                                      ("ffn1_w_down",), ("w_in",))
    dh2, dgu3, d_g2 = _ffn_bwd(dh3, h2, g2, gu3, wgu2, wd2, "ffn2_bwd")
    pa = grad_gate_up(n3, dgu3, "wgrad_gu2") + grad_down(a3, dh3, "wgrad_d2")
    (dlg, dyp, dys, do_sb, dyg, dxp, d_scale), got_a = _mix_bwd_out(
        dh2, gp, gs, yp, ys, pm, wgrp, pool_scale, wbp, wba, wout, exchange=_ex_pair_swap(pa))
    chip_a = pair_sums(k_gu2 + k_d2, pa, got_a)
    kb = ("w_out", "w_branch_pool", "w_branch_attn")
    g_bp, g_ba, d_group = _wgrad_branches(p, dyp, o_sb, dys, pm, dyg)
    pb = [_halves(_wgrad(mm, dh2, 1, D, "wgrad_out").reshape(NSH, D // NSH, D)), _halves(g_bp), _halves(g_ba)]
    k_a, k_in = k_gu2 + k_d2, k_in + kb
    sems_a, thru_a, token_a = _scatter_start(chip_a, "scatter_a_start")
    dq, dk, dv = _attn_bwd(q, k, v, do_sb, ctot, after=(token_a,))
    chip_a, owned_a = _scatter_wait(sems_a, thru_a, [dq], "scatter_a_wait")
    halves_a = chip_sums(k_a, chip_a, owned_a)
    dproj = (dxp, dq, dk, dv, dlg)
    (dh1, d_gm), both_a = _mix_bwd_in(dh2, h1, gm, dproj, win, exchange=_ex_share(halves_a))
    for i, k_ in enumerate(k_a):
        grad[k_] = both_a[i].reshape(wts[k_].shape)

    p_in = [_halves(_wgrad_in(u, dproj))] + pb
    p_d1, got_in = grad_down(a1, dh1, "wgrad_d1", exchange=_ex_pair_swap(p_in))
    sems_in, thru_in, token_in = _scatter_start(pair_sums(k_in, p_in, got_in), "scatter_in_start")
    dgu1, got_d1 = _ffn_bwd_act(dh1, gu1, wd1, "ffn1_bwd_act", exchange=_ex_pair_swap(p_d1), after=(token_in,))
    sems_d1, thru_d1, token_d1 = _scatter_start(pair_sums(k_d1, p_d1, got_d1), "scatter_d1_start")
    p_gu1 = [_halves(_wgrad(n1, dgu1, NSH, D, "wgrad_gu1", after=(token_in, token_d1)))]
    sems_w, thru_w, token_w = _swap_start(p_gu1, "swap_gu1_start")
    chip_in, owned_in = _scatter_wait(sems_in, thru_in, [token_w], "scatter_in_wait")
    chip_d1, owned_d1 = _scatter_wait(sems_d1, thru_d1, [token_w], "scatter_d1_wait")
    halves_in = chip_sums(k_in, chip_in, owned_in)
    p_gu1, got_gu1 = _swap_wait(sems_w, thru_w, halves_in, "swap_gu1_wait")
    sems, thru, token = _scatter_start(pair_sums(k_gu1, p_gu1, got_gu1), "scatter_gu1_start")
    sems_h, thru_h, token_h = _share_start(halves_in, [token], "share_in_start")
    adamw(k_a, after=(token_h,))
    landed = _share_wait(sems_h, thru_h, [delta[k_a[0]]], "share_in_wait")
    for i, k_ in enumerate(k_in):
        grad[k_] = landed[i].reshape(wts[k_].shape)
    adamw(k_in)
    dx, d_g1 = _ffn_bwd_in(dh1, x0, g1, dgu1, wgu1, "ffn1_bwd_in", after=(token,))
    small_g = dict(ffn1_norm=d_g1, mix_norm=d_gm, ffn2_norm=d_g2, final_norm=d_gf, pool_scale=d_scale,
                   pool_w_group=d_group, loss=loss_row)
    dev = 4 * lax.axis_index("x") + 2 * lax.axis_index("y") + lax.axis_index("c")
    slots = lax.dynamic_update_slice(jnp.zeros((8, SMALL_ROWS, 128), F32), _pack_small(small_g)[None], (dev, 0, 0))
    chip_gu1, owned_gu1 = _scatter_wait(sems, thru, [dx] + [delta[k_] for k_ in k_a + k_in], "scatter_gu1_wait")
    halves_last = chip_sums(k_d1 + k_gu1, chip_d1 + chip_gu1, owned_d1 + owned_gu1)
    sems_l, thru_l, token_l = _share_start(halves_last, [], "share_last_start")
    sems_s, slots, token_s = _small_gather_start(slots, "small_gather_start", after=(token_l,))
    both = _share_wait(sems_l, thru_l, [token_s], "share_last_wait")
    grad["ffn1_w_down"] = both[0].reshape(ffn1_w_down.shape)
    grad["ffn1_w_gate_up"] = both[1].reshape(ffn1_w_gate_up.shape)
    adamw(k_d1 + k_gu1, after=(token_s,))
    gathered = _small_gather_wait(sems_s, slots, [delta[k_] for k_ in k_d1 + k_gu1], "small_gather_wait")
    gathered = gathered.reshape(8 * SMALL_ROWS, 128)
    results = _small_update(gathered, _pack_small(wts), _pack_small(mom), _pack_small(var), _small_entries(wts))
    for dst, entries in zip((grad, delta, new_m, new_v), results):
        for k_, rows in zip(SMALL, entries):
            if k_ in wts:
                dst[k_] = rows.reshape(wts[k_].shape)
            elif dst is grad:
                loss = rows[0, 0]
    return (loss, dx[None], *[grad[k_] for k_ in ORDER], *[delta[k_] for k_ in ORDER],
            *[new_m[k_] for k_ in ORDER], *[new_v[k_] for k_ in ORDER])
```

```python
import dataclasses
import functools

import jax
import jax.numpy as jnp
from jax import lax
from jax.experimental import pallas as pl
from jax.experimental.pallas import tpu as pltpu

F32 = jnp.float32
BF16 = jnp.bfloat16

S = 2048
D = 1024
DFF = 2816
FFS = 2 * DFF // 4
NSH = 4
PW = 512
PG = 128
POOL_WINDOWS = (2, 4, 8, 16)
HALO = 16
SBW = 512
DH = 64
EPS = 1e-6
SCALE = 0.125
LOG2E = 1.4426950408889634
TA = 256
QB = 2
MIB = 1024 * 1024

LR, B1, B2, AEPS, WD, STEP = 0.001, 0.9, 0.999, 1e-08, 0.01, 10

_VM = pl.BlockSpec(memory_space=pltpu.VMEM)
_ANY = pl.BlockSpec(memory_space=pl.ANY)
MESH = pl.DeviceIdType.MESH
SIBLING_PAIR_ID = 1


def _nn(a, b):
    return jnp.dot(a, b, preferred_element_type=F32)


def _nt(a, b):
    return lax.dot_general(a, b, (((1,), (1,)), ((), ())), preferred_element_type=F32)


def _tn(a, b):
    return lax.dot_general(a, b, (((0,), (0,)), ((), ())), preferred_element_type=F32)


def _params(sem, vmem_mib):
    return pltpu.CompilerParams(dimension_semantics=sem, vmem_limit_bytes=vmem_mib * MIB)


def _rows(tm, width):
    return pl.BlockSpec((tm, width), lambda i: (i, 0))


def _fixed(shape):
    return pl.BlockSpec(shape, lambda *_: (0,) * len(shape))


def _sds(shape, dtype):
    return pltpu.HBM(shape, dtype)


def _in_hbm(args):
    return [pltpu.with_memory_space_constraint(a, pltpu.HBM) for a in args]


def _stage(pairs):
    pieces = 4

    def copy_all(sems):
        copies = []
        for src, dst in pairs:
            step = src.shape[0] // pieces
            for p in range(pieces):
                part = pl.ds(p * step, step)
                if len(dst.shape) == len(src.shape):
                    piece = (src.at[part], dst.at[part])
                else:
                    piece = (src.at[p], dst.at[:, pl.ds(p * src.shape[2], src.shape[2])])
                copies.append(pltpu.make_async_copy(*piece, sems.at[len(copies)]))
        for c in copies:
            c.start()
        for c in copies:
            c.wait()

    @pl.when(pl.program_id(0) == 0)
    def _():
        pl.run_scoped(copy_all, pltpu.SemaphoreType.DMA((pieces * len(pairs),)))


def _vmem_like(*arrays):
    return [pltpu.VMEM(a.shape, a.dtype) for a in arrays]


def _vmem_wide(w):
    return pltpu.VMEM((w.shape[1], w.shape[0] * w.shape[2]), w.dtype)


FF_CHUNKS = ((0, 1536), (1536, DFF - 1536))


class Exchange:
    def __init__(self, arrays, landing, aliases, n_sems, start, finish, sibling_only=False):
        self.arrays, self.landing, self.aliases, self.n_sems = list(arrays), list(landing), dict(aliases), n_sems
        self.start, self.finish = start, finish
        self.sibling_only = sibling_only

    def enter(self):
        if self.sibling_only:
            barrier = pltpu.get_barrier_semaphore()
            sibling = (lax.axis_index("x"), lax.axis_index("y"), 1 - lax.axis_index("c"))
            pl.semaphore_signal(barrier, inc=1, device_id=sibling, device_id_type=MESH)
            pl.semaphore_wait(barrier, 1)

    def params(self, compiler_params=None):
        kw = dict(collective_id=SIBLING_PAIR_ID) if self.sibling_only else {}
        if compiler_params is None:
            return pltpu.CompilerParams(**kw)
        return dataclasses.replace(compiler_params, **kw)


def _call(body, args, *, name, grid, in_specs, out_specs, out_shape, scratch_shapes=(), compiler_params=None,
          exchange=None, free=(), after=()):
    args = [a if i in free else pltpu.with_memory_space_constraint(a, pltpu.HBM) for i, a in enumerate(args)]
    if exchange is None:
        n_in = len(in_specs)

        def plain(*refs):
            body(*refs[:n_in], *refs[n_in + len(after):])

        return pl.pallas_call(plain, name=name, grid=grid, in_specs=list(in_specs) + [_ANY] * len(after),
                              out_specs=out_specs, out_shape=out_shape, scratch_shapes=list(scratch_shapes),
                              compiler_params=compiler_params)(*args, *after)
    ex = exchange
    n_in, n_out, n_scr = len(in_specs), len(out_specs), len(scratch_shapes)
    na, nl = len(ex.arrays), len(ex.landing)

    def hosted(*refs):
        at = [0]

        def take(n):
            at[0] += n
            return refs[at[0] - n:at[0]]

        k_in, _, e_in, k_out, e_out, k_scr = take(n_in), take(len(after)), take(na), take(n_out), take(nl), take(n_scr)
        ssem, rsem = take(2)
        ids = [pl.program_id(a) for a in range(len(grid))]
        first = functools.reduce(jnp.logical_and, [i == 0 for i in ids])
        last = functools.reduce(jnp.logical_and, [i == g - 1 for i, g in zip(ids, grid)])

        @pl.when(first)
        def _():
            ex.enter()
            ex.start(e_in, e_out, ssem, rsem)

        body(*k_in, *k_out, *k_scr)

        @pl.when(last)
        def _():
            ex.finish(e_in, e_out, ssem, rsem)

    outs = pl.pallas_call(
        hosted, name=name, grid=grid,
        in_specs=list(in_specs) + [_ANY] * (len(after) + na), out_specs=list(out_specs) + [_ANY] * nl,
        out_shape=list(out_shape) + ex.landing,
        scratch_shapes=list(scratch_shapes) + [pltpu.SemaphoreType.DMA((ex.n_sems,))] * 2,
        input_output_aliases={n_in + len(after) + i: n_out + j for i, j in ex.aliases.items()},
        compiler_params=ex.params(compiler_params),
    )(*args, *after, *_in_hbm(ex.arrays))
    return outs[:n_out], outs[n_out:]


def _exchange_alone(ex, name, after=()):
    na, nl = len(ex.arrays), len(ex.landing)

    def body(*refs):
        outs = refs[na + len(after):na + len(after) + nl]
        ex.enter()
        ex.start(refs[:na], outs, refs[-2], refs[-1])
        ex.finish(refs[:na], outs, refs[-2], refs[-1])

    return pl.pallas_call(
        body, name=name, in_specs=[_ANY] * (na + len(after)), out_specs=[_ANY] * nl,
        out_shape=ex.landing, scratch_shapes=[pltpu.SemaphoreType.DMA((ex.n_sems,))] * 2,
        input_output_aliases=ex.aliases, compiler_params=ex.params(),
    )(*_in_hbm(ex.arrays), *after)


_HBM = pl.BlockSpec(memory_space=pltpu.HBM)
_SEM = pl.BlockSpec(memory_space=pltpu.SEMAPHORE)
_EFFECT = pltpu.SideEffectType.DATAFLOW_SIDE_EFFECTING


def _scatter_copies(srcs, lands, ssems, rsems):
    x, y, c, chips = _place()
    return [_remote(srcs[w].at[2 * px + py], lands[w].at[k], ssems[3 * w + k], rsems[3 * w + k], (px, py, c))
            for w in range(len(srcs)) for k, (px, py) in enumerate(chips)]


def _scatter_start(parts, name):
    parts = list(parts)
    n, ncp = len(parts), 3 * len(parts)
    lands = [lax.empty((3,) + p.shape[1:], p.dtype) for p in parts]

    def body(*refs):
        srcs, land_refs = refs[:n], refs[n:2 * n]
        ssems, rsems = refs[2 * n:2 * n + ncp], refs[2 * n + ncp:2 * n + 2 * ncp]
        for cp in _scatter_copies(srcs, land_refs, ssems, rsems):
            cp.start()
        token = refs[-1]
        token[...] = jnp.zeros_like(token)

    outs = pl.pallas_call(
        body, name=name,
        out_shape=([pltpu.SemaphoreType.DMA(())] * (2 * ncp) + [pltpu.HBM(a.shape, a.dtype) for a in parts + lands]
                   + [jax.ShapeDtypeStruct((8, 128), F32)]),
        in_specs=[_HBM] * (2 * n), out_specs=[_SEM] * (2 * ncp) + [_HBM] * (2 * n) + [_VM],
        input_output_aliases={i: 2 * ncp + i for i in range(2 * n)},
        compiler_params=pltpu.CompilerParams(has_side_effects=_EFFECT),
    )(*_in_hbm(parts), *_in_hbm(lands))
    sems, thru, token = outs[:2 * ncp], outs[2 * ncp:2 * ncp + 2 * n], outs[-1]
    return sems, thru, token


def _scatter_wait(sems, thru, after, name):
    n = len(thru) // 2
    ncp = 3 * n

    def body(*refs):
        srcs, land_refs = refs[:n], refs[n:2 * n]
        ssems, rsems = refs[2 * n:2 * n + ncp], refs[2 * n + ncp:2 * n + 2 * ncp]
        for cp in _scatter_copies(srcs, land_refs, ssems, rsems):
            cp.wait_send()
            cp.wait_recv()

    outs = pl.pallas_call(
        body, name=name, out_shape=[pltpu.HBM(a.shape, a.dtype) for a in thru],
        in_specs=[_HBM] * (2 * n) + [_SEM] * (2 * ncp) + [_ANY] * len(after), out_specs=[_HBM] * (2 * n),
        input_output_aliases={i: i for i in range(2 * n)},
        compiler_params=pltpu.CompilerParams(has_side_effects=_EFFECT),
    )(*thru, *sems, *after)
    return outs[:n], outs[n:]


def _swap_copies(srcs, lands, ssems, rsems):
    x, y, c, _ = _place()
    return [_remote(srcs[w].at[:, 1 - c], lands[w], ssems[w], rsems[w], (x, y, 1 - c)) for w in range(len(srcs))]


def _swap_start(grads, name):
    grads = list(grads)
    n = len(grads)
    lands = [lax.empty((NSH,) + g.shape[2:], g.dtype) for g in grads]

    def body(*refs):
        barrier = pltpu.get_barrier_semaphore()
        sibling = (lax.axis_index("x"), lax.axis_index("y"), 1 - lax.axis_index("c"))
        pl.semaphore_signal(barrier, inc=1, device_id=sibling, device_id_type=MESH)
        pl.semaphore_wait(barrier, 1)
        for cp in _swap_copies(refs[:n], refs[n:2 * n], refs[2 * n:3 * n], refs[3 * n:4 * n]):
            cp.start()
        refs[-1][...] = jnp.zeros_like(refs[-1])

    outs = pl.pallas_call(
        body, name=name,
        out_shape=([pltpu.SemaphoreType.DMA(())] * (2 * n) + [pltpu.HBM(a.shape, a.dtype) for a in grads + lands]
                   + [jax.ShapeDtypeStruct((8, 128), F32)]),
        in_specs=[_HBM] * (2 * n), out_specs=[_SEM] * (2 * n) + [_HBM] * (2 * n) + [_VM],
        input_output_aliases={i: 2 * n + i for i in range(2 * n)},
        compiler_params=pltpu.CompilerParams(has_side_effects=_EFFECT, collective_id=SIBLING_PAIR_ID),
    )(*_in_hbm(grads), *_in_hbm(lands))
    return outs[:2 * n], outs[2 * n:4 * n], outs[-1]


def _swap_wait(sems, thru, after, name):
    n = len(thru) // 2

    def body(*refs):
        for cp in _swap_copies(refs[:n], refs[n:2 * n], refs[2 * n:3 * n], refs[3 * n:4 * n]):
            cp.wait_send()
            cp.wait_recv()

    outs = pl.pallas_call(
        body, name=name, out_shape=[pltpu.HBM(a.shape, a.dtype) for a in thru],
        in_specs=[_HBM] * (2 * n) + [_SEM] * (2 * n) + [_ANY] * len(after), out_specs=[_HBM] * (2 * n),
        input_output_aliases={i: i for i in range(2 * n)},
        compiler_params=pltpu.CompilerParams(has_side_effects=_EFFECT),
    )(*thru, *sems, *after)
    return outs[:n], outs[n:]


def _share_copies(bufs, ssems, rsems, sending):
    x, y, c, _ = _place()
    out = []
    for w, ref in enumerate(bufs):
        slot = ref.at[c if sending else 1 - c]
        out.append(_remote(slot, slot, ssems[w], rsems[w], (x, y, 1 - c)))
    return out


def _share_start(bufs, after, name):
    bufs = list(bufs)
    n = len(bufs)

    def body(*refs):
        barrier = pltpu.get_barrier_semaphore()
        sibling = (lax.axis_index("x"), lax.axis_index("y"), 1 - lax.axis_index("c"))
        pl.semaphore_signal(barrier, inc=1, device_id=sibling, device_id_type=MESH)
        pl.semaphore_wait(barrier, 1)
        at = n + len(after)
        for cp in _share_copies(refs[:n], refs[at:at + n], refs[at + n:at + 2 * n], True):
            cp.start()
        refs[-1][...] = jnp.zeros_like(refs[-1])

    outs = pl.pallas_call(
        body, name=name,
        out_shape=([pltpu.SemaphoreType.DMA(())] * (2 * n) + [pltpu.HBM(a.shape, a.dtype) for a in bufs]
                   + [jax.ShapeDtypeStruct((8, 128), F32)]),
        in_specs=[_HBM] * n + [_ANY] * len(after), out_specs=[_SEM] * (2 * n) + [_HBM] * n + [_VM],
        input_output_aliases={i: 2 * n + i for i in range(n)},
        compiler_params=pltpu.CompilerParams(has_side_effects=_EFFECT, collective_id=SIBLING_PAIR_ID),
    )(*_in_hbm(bufs), *after)
    return outs[:2 * n], outs[2 * n:3 * n], outs[-1]


def _share_wait(sems, thru, after, name):
    n = len(thru)

    def body(*refs):
        for cp in _share_copies(refs[:n], refs[n:2 * n], refs[2 * n:3 * n], True):
            cp.wait_send()
        for cp in _share_copies(refs[:n], refs[n:2 * n], refs[2 * n:3 * n], False):
            cp.wait_recv()

    return pl.pallas_call(
        body, name=name, out_shape=[pltpu.HBM(a.shape, a.dtype) for a in thru],
        in_specs=[_HBM] * n + [_SEM] * (2 * n) + [_ANY] * len(after), out_specs=[_HBM] * n,
        input_output_aliases={i: i for i in range(n)},
        compiler_params=pltpu.CompilerParams(has_side_effects=_EFFECT),
    )(*thru, *sems, *after)


def _gather_copies(bufs, ssems, rsems, sending):
    x, y, c, chips = _place()
    out = []
    for w, ref in enumerate(bufs):
        half = ref.shape[1] // 2
        for k, (px, py) in enumerate(chips):
            rows = ref.at[2 * x + y if sending else 2 * px + py, pl.ds(c * half, half)]
            out.append(_remote(rows, rows, ssems[3 * w + k], rsems[3 * w + k], (px, py, c)))
    return out


def _gather_start(bufs, after, name):
    n, ncp = len(bufs), 3 * len(bufs)

    def body(*refs):
        ssems, rsems = refs[n + len(after):n + len(after) + ncp], refs[n + len(after) + ncp:n + len(after) + 2 * ncp]
        for cp in _gather_copies(refs[:n], ssems, rsems, True):
            cp.start()
        token = refs[-1]
        token[...] = jnp.zeros_like(token)

    outs = pl.pallas_call(
        body, name=name,
        out_shape=([pltpu.SemaphoreType.DMA(())] * (2 * ncp) + [pltpu.HBM(a.shape, a.dtype) for a in bufs]
                   + [jax.ShapeDtypeStruct((8, 128), F32)]),
        in_specs=[_HBM] * n + [_ANY] * len(after), out_specs=[_SEM] * (2 * ncp) + [_HBM] * n + [_VM],
        input_output_aliases={i: 2 * ncp + i for i in range(n)},
        compiler_params=pltpu.CompilerParams(has_side_effects=_EFFECT),
    )(*_in_hbm(bufs), *after)
    return outs[:2 * ncp], outs[2 * ncp:2 * ncp + n], outs[-1]


def _gather_wait(sems, thru, after, name):
    n = len(thru)
    ncp = 3 * n

    def body(*refs):
        ssems, rsems = refs[n:n + ncp], refs[n + ncp:n + 2 * ncp]
        for cp in _gather_copies(refs[:n], ssems, rsems, True):
            cp.wait_send()
        for cp in _gather_copies(refs[:n], ssems, rsems, False):
            cp.wait_recv()

    return pl.pallas_call(
        body, name=name, out_shape=[pltpu.HBM(a.shape, a.dtype) for a in thru],
        in_specs=[_HBM] * n + [_SEM] * (2 * ncp) + [_ANY] * len(after), out_specs=[_HBM] * n,
        input_output_aliases={i: i for i in range(n)},
        compiler_params=pltpu.CompilerParams(has_side_effects=_EFFECT),
    )(*thru, *sems, *after)


def _rms(x):
    r = lax.rsqrt(jnp.mean(x * x, axis=-1, keepdims=True) + EPS)
    return r, x * r


def _rms_bwd(dn, xr, r, gain):
    dng = dn * gain
    dx = r * (dng - xr * jnp.mean(dng * xr, axis=-1, keepdims=True))
    return dx, jnp.sum(dn * xr, axis=0, keepdims=True)


def _ffn_fwd(x, gain, wgu, wd, name, exchange=None, head=None):
    tm = 256

    def body(x_ref, g_ref, wgu_hbm, wd_hbm, *rest):
        if head is None:
            h_ref, n_ref, gu_ref, a_ref, wgu_ref, wd_ref = rest
        else:
            t_ref, gf_ref, h_ref, loss_ref, dgf_ref, n_ref, gu_ref, a_ref, wgu_ref, wd_ref = rest
        _stage([(wgu_hbm, wgu_ref), (wd_hbm, wd_ref)])
        x = x_ref[...]
        _, xr = _rms(x)
        n = (xr * g_ref[...]).astype(BF16)
        n_ref[...] = n
        acc = jnp.zeros((tm, D), F32)
        for c0, cn in FF_CHUNKS:
            g = _nn(n, wgu_ref[:, c0:c0 + cn])
            u = _nn(n, wgu_ref[:, DFF + c0:DFF + c0 + cn])
            gu_ref[:, c0:c0 + cn] = g.astype(BF16)
            gu_ref[:, DFF + c0:DFF + c0 + cn] = u.astype(BF16)
            half_act = (0.5 * (g * jax.nn.sigmoid(g) * u)).astype(BF16)
            a_ref[:, c0:c0 + cn] = half_act
            acc = acc + _nn(half_act, wd_ref[c0:c0 + cn, :])
        h = x + acc
        if head is None:
            h_ref[...] = h
            return
        gf = gf_ref[...]
        r, hr = _rms(h)
        err = hr * gf - t_ref[...]
        dh, dgain = _rms_bwd(err * (1.0 / D), hr, r, gf)
        h_ref[...] = dh

        @pl.when(pl.program_id(0) == 0)
        def _():
            dgf_ref[...] = jnp.zeros_like(dgf_ref)
            loss_ref[...] = jnp.zeros_like(loss_ref)

        dgf_ref[...] += dgain
        loss_ref[...] += jnp.full((1, 128), (0.5 / D) * jnp.sum(err * err), F32)

    saved_specs = [_rows(tm, D), _rows(tm, 4 * FFS), _rows(tm, DFF)]
    saved_shapes = [_sds((S, D), BF16), _sds((S, 4 * FFS), BF16), _sds((S, DFF), BF16)]
    if head is None:
        return _call(
            body, (x, gain, wgu, wd), name=name, grid=(S // tm,),
            in_specs=[_rows(tm, D), _fixed((1, D)), _ANY, _ANY],
            out_specs=[_rows(tm, D)] + saved_specs, out_shape=[_sds((S, D), F32)] + saved_shapes,
            scratch_shapes=[_vmem_wide(wgu)] + _vmem_like(wd),
            compiler_params=_params(("arbitrary",), 56), exchange=exchange)
    return _call(
        body, (x, gain, wgu, wd, *head), name=name, grid=(S // tm,),
        in_specs=[_rows(tm, D), _fixed((1, D)), _ANY, _ANY, _rows(tm, D), _fixed((1, D))],
        out_specs=[_rows(tm, D), _fixed((1, 128)), _fixed((1, D))] + saved_specs,
        out_shape=[_sds((S, D), F32), _sds((1, 128), F32), _sds((1, D), F32)] + saved_shapes,
        scratch_shapes=[_vmem_wide(wgu)] + _vmem_like(wd),
        compiler_params=_params(("arbitrary",), 56), exchange=exchange, free=(4, 5))


def _ffn_bwd(dh, x, gain, gu, wgu, wd, name):
    tm = 256

    def body(dh_ref, x_ref, g_ref, gu_ref, wgu_hbm, wd_hbm, dx_ref, dgu_ref, dg_ref, wgu_ref, wd_ref):
        _stage([(wgu_hbm, wgu_ref), (wd_hbm, wd_ref)])
        dh = dh_ref[...]
        dhb = dh.astype(BF16)
        dn = jnp.zeros((tm, D), F32)
        for c0, cn in FF_CHUNKS:
            g = gu_ref[:, c0:c0 + cn].astype(F32)
            u = gu_ref[:, DFF + c0:DFF + c0 + cn].astype(F32)
            da = 0.5 * _nt(dhb, wd_ref[c0:c0 + cn, :])
            sg = jax.nn.sigmoid(g)
            dgb = (da * u * (sg * (1.0 + g * (1.0 - sg)))).astype(BF16)
            dub = (da * (g * sg)).astype(BF16)
            dgu_ref[:, c0:c0 + cn] = dgb
            dgu_ref[:, DFF + c0:DFF + c0 + cn] = dub
            dn = dn + _nt(dgb, wgu_ref[:, c0:c0 + cn]) + _nt(dub, wgu_ref[:, DFF + c0:DFF + c0 + cn])
        r, xr = _rms(x_ref[...])
        dx, dgain = _rms_bwd(dn, xr, r, g_ref[...])
        dx_ref[...] = dh + dx

        @pl.when(pl.program_id(0) == 0)
        def _():
            dg_ref[...] = jnp.zeros_like(dg_ref)

        dg_ref[...] += dgain

    return _call(
        body, (dh, x, gain, gu, wgu, wd), name=name, grid=(S // tm,),
        in_specs=[_rows(tm, D), _rows(tm, D), _fixed((1, D)), _rows(tm, 4 * FFS), _ANY, _ANY],
        out_specs=[_rows(tm, D), _rows(tm, 4 * FFS), _fixed((1, D))],
        out_shape=[_sds((S, D), F32), _sds((S, 4 * FFS), BF16), _sds((1, D), F32)],
        scratch_shapes=[_vmem_wide(wgu)] + _vmem_like(wd), compiler_params=_params(("arbitrary",), 56))


def _ffn_last(x, gain, wgu, wd, target, gf, name):
    tm = 256

    def body(x_ref, g_ref, wgu_hbm, wd_hbm, t_ref, gf_ref, dx_ref, dgu_ref, dg_ref, loss_ref, dgf_ref, n_ref,
             a_ref, dh_ref, wgu_ref, wd_ref):
        _stage([(wgu_hbm, wgu_ref), (wd_hbm, wd_ref)])
        x = x_ref[...]
        r0, xr = _rms(x)
        n = (xr * g_ref[...]).astype(BF16)
        n_ref[...] = n
        acc = jnp.zeros((tm, D), F32)
        kept = []
        for c0, cn in FF_CHUNKS:
            g = _nn(n, wgu_ref[:, c0:c0 + cn])
            u = _nn(n, wgu_ref[:, DFF + c0:DFF + c0 + cn])
            kept.append((g.astype(BF16), u.astype(BF16)))
            half_act = (0.5 * (g * jax.nn.sigmoid(g) * u)).astype(BF16)
            a_ref[:, c0:c0 + cn] = half_act
            acc = acc + _nn(half_act, wd_ref[c0:c0 + cn, :])
        h = x + acc
        gf = gf_ref[...]
        r, hr = _rms(h)
        err = hr * gf - t_ref[...]
        dh, dgain_f = _rms_bwd(err * (1.0 / D), hr, r, gf)
        dh_ref[...] = dh
        dhb = dh.astype(BF16)
        dn = jnp.zeros((tm, D), F32)
        for (c0, cn), (gb, ub) in zip(FF_CHUNKS, kept):
            g, u = gb.astype(F32), ub.astype(F32)
            da = 0.5 * _nt(dhb, wd_ref[c0:c0 + cn, :])
            sg = jax.nn.sigmoid(g)
            dgb = (da * u * (sg * (1.0 + g * (1.0 - sg)))).astype(BF16)
            dub = (da * (g * sg)).astype(BF16)
            dgu_ref[:, c0:c0 + cn] = dgb
            dgu_ref[:, DFF + c0:DFF + c0 + cn] = dub
            dn = dn + _nt(dgb, wgu_ref[:, c0:c0 + cn]) + _nt(dub, wgu_ref[:, DFF + c0:DFF + c0 + cn])
        dx, dgain = _rms_bwd(dn, xr, r0, g_ref[...])
        dx_ref[...] = dh + dx

        @pl.when(pl.program_id(0) == 0)
        def _():
            dg_ref[...] = jnp.zeros_like(dg_ref)
            dgf_ref[...] = jnp.zeros_like(dgf_ref)
            loss_ref[...] = jnp.zeros_like(loss_ref)

        dg_ref[...] += dgain
        dgf_ref[...] += dgain_f
        loss_ref[...] += jnp.full((1, 128), (0.5 / D) * jnp.sum(err * err), F32)

    return _call(
        body, (x, gain, wgu, wd, target, gf), name=name, grid=(S // tm,),
        in_specs=[_rows(tm, D), _fixed((1, D)), _ANY, _ANY, _rows(tm, D), _fixed((1, D))],
        out_specs=[_rows(tm, D), _rows(tm, 4 * FFS), _fixed((1, D)), _fixed((1, 128)), _fixed((1, D)),
                   _rows(tm, D), _rows(tm, DFF), _rows(tm, D)],
        out_shape=[_sds((S, D), F32), _sds((S, 4 * FFS), BF16), _sds((1, D), F32), _sds((1, 128), F32),
                   _sds((1, D), F32), _sds((S, D), BF16), _sds((S, DFF), BF16), _sds((S, D), F32)],
        scratch_shapes=[_vmem_wide(wgu)] + _vmem_like(wd),
        compiler_params=_params(("arbitrary",), 58), free=(4, 5))


def _ffn_bwd_act(dh, gu, wd, name, exchange=None, after=()):
    tm = 512

    def body(dh_ref, gu_ref, wd_hbm, dgu_ref, wd_ref):
        _stage([(wd_hbm, wd_ref)])
        dhb = dh_ref[...].astype(BF16)
        for c0, cn in FF_CHUNKS:
            g = gu_ref[:, c0:c0 + cn].astype(F32)
            u = gu_ref[:, DFF + c0:DFF + c0 + cn].astype(F32)
            da = 0.5 * _nt(dhb, wd_ref[c0:c0 + cn, :])
            sg = jax.nn.sigmoid(g)
            dgu_ref[:, c0:c0 + cn] = (da * u * (sg * (1.0 + g * (1.0 - sg)))).astype(BF16)
            dgu_ref[:, DFF + c0:DFF + c0 + cn] = (da * (g * sg)).astype(BF16)

    res = _call(
        body, (dh, gu, wd), name=name, grid=(S // tm,),
        in_specs=[_rows(tm, D), _rows(tm, 4 * FFS), _ANY], out_specs=[_rows(tm, 4 * FFS)],
        out_shape=[_sds((S, 4 * FFS), BF16)], scratch_shapes=_vmem_like(wd),
        compiler_params=_params(("arbitrary",), 56), exchange=exchange, after=after)
    return res[0] if exchange is None else (res[0][0], res[1])


def _ffn_bwd_in(dh, x, gain, dgu, wgu, name, exchange=None, after=()):
    tm = 512

    def body(dh_ref, x_ref, g_ref, dgu_ref, wgu_hbm, dx_ref, dg_ref, wgu_ref):
        _stage([(wgu_hbm, wgu_ref)])
        dn = jnp.zeros((tm, D), F32)
        for half in (0, DFF):
            for c0, cn in FF_CHUNKS:
                cols = slice(half + c0, half + c0 + cn)
                dn = dn + _nt(dgu_ref[:, cols], wgu_ref[:, cols])
        r, xr = _rms(x_ref[...])
        dx, dgain = _rms_bwd(dn, xr, r, g_ref[...])
        dx_ref[...] = dh_ref[...] + dx

        @pl.when(pl.program_id(0) == 0)
        def _():
            dg_ref[...] = jnp.zeros_like(dg_ref)

        dg_ref[...] += dgain

    return _call(
        body, (dh, x, gain, dgu, wgu), name=name, grid=(S // tm,),
        in_specs=[_rows(tm, D), _rows(tm, D), _fixed((1, D)), _rows(tm, 4 * FFS), _ANY],
        out_specs=[_rows(tm, D), _fixed((1, D))],
        out_shape=[_sds((S, D), F32), _sds((1, D), F32)],
        scratch_shapes=[_vmem_wide(wgu)],
        compiler_params=_params(("arbitrary",), 56), exchange=exchange, after=after)


def _mix_in(h, gain, w_in, after=()):
    tm = 512

    def body(h_ref, g_ref, w_hbm, u_ref, xp_ref, q_ref, k_ref, v_ref, gp_ref, gs_ref, w_ref):
        _stage([(w_hbm, w_ref)])
        _, hr = _rms(h_ref[...])
        u = (hr * g_ref[...]).astype(BF16)
        u_ref[...] = u
        p0 = _nn(u, w_ref[0])
        xp_ref[...] = p0[:, :PW]
        q_ref[...] = p0[:, PW:].astype(BF16)
        p1 = _nn(u, w_ref[1])
        k_ref[...] = p1[:, :SBW].astype(BF16)
        v_ref[...] = p1[:, SBW:].astype(BF16)
        gp_ref[...] = jax.nn.sigmoid(_nn(u, w_ref[2])).astype(BF16)
        gs_ref[...] = jax.nn.sigmoid(_nn(u, w_ref[3])).astype(BF16)

    return _call(
        body, (h, gain, w_in), name="mix_in", grid=(S // tm,),
        in_specs=[_rows(tm, D), _fixed((1, D)), _ANY],
        out_specs=[_rows(tm, D), _rows(tm, PW), _rows(tm, SBW), _rows(tm, SBW), _rows(tm, SBW),
                   _rows(tm, D), _rows(tm, D)],
        out_shape=[_sds((S, D), BF16), _sds((S, PW), F32), _sds((S, SBW), BF16), _sds((S, SBW), BF16),
                   _sds((S, SBW), BF16), _sds((S, D), BF16), _sds((S, D), BF16)],
        scratch_shapes=_vmem_like(w_in),
        compiler_params=_params(("arbitrary",), 48), free=(1,), after=after)


def _hilo_dot(x, tri):
    hi = x.astype(BF16)
    lo = (x - hi.astype(F32)).astype(BF16)
    return _nn(hi, tri) + _nn(lo, tri)


def _log_terms(qk):
    z2 = qk * (SCALE * LOG2E)
    lb = jnp.minimum(z2, 0.0) - jnp.log2(1.0 + jnp.exp2(-jnp.abs(z2)))
    return lb, lb - z2


def _head_masks():
    lane = lax.broadcasted_iota(jnp.int32, (1, 2 * DH), 1)
    return (lane < DH, lane >= DH)


def _attn_fwd(q, k, v, exchange=None):
    T = TA

    def body(q_ref, k_ref, v_ref, o_ref, c_ref):
        i2 = 2 * pl.program_id(1)
        row = lax.broadcasted_iota(jnp.int32, (T, T), 0)
        col = lax.broadcasted_iota(jnp.int32, (T, T), 1)
        after = (row > col).astype(BF16)
        causal = col < row
        masks = _head_masks()
        qms = {}
        for b in range(QB):
            q2 = q_ref[b * T:(b + 1) * T, :]
            for h, hm in enumerate(masks):
                qms[b, h] = jnp.where(hm, q2, jnp.zeros_like(q2))

        def blocks(keys, pairs, carries, os):
            ks, vms = [], []
            for j in keys:
                rows = pl.ds(pl.multiple_of(j * T, T), T)
                vj = v_ref[rows, :]
                ks.append(k_ref[rows, :])
                vms.append([jnp.where(hm, vj, jnp.zeros_like(vj)) for hm in masks])
            units = [(n, h) for n in range(len(pairs)) for h in range(2)]
            qks = {(n, h): _nt(qms[pairs[n][0], h], ks[pairs[n][1]]) for n, h in units}
            lbs, l1ms = {}, {}
            for u in units:
                lbs[u], l1m = _log_terms(qks[u])
                l1ms[u] = jnp.where(causal, l1m, 0.0) if pairs[u[0]][2] else l1m
            cins = {u: _hilo_dot(l1ms[u], after) for u in units}
            carries, os = dict(carries), list(os)
            for n, h in units:
                b, key, diag = pairs[n]
                a = jnp.exp2(lbs[n, h] + cins[n, h] + carries[b, h])
                if diag:
                    a = jnp.where(causal, a, 0.0)
                os[b] = os[b] + _nn(a.astype(BF16), vms[key][h])
                carries[b, h] = carries[b, h] + jnp.sum(l1ms[n, h], axis=1, keepdims=True)
            return carries, tuple(os)

        carries = {(b, h): jnp.zeros((T, 1), F32) for b in range(QB) for h in range(2)}
        os = tuple(jnp.zeros((T, 2 * DH), F32) for _ in range(QB))
        carries, os = blocks([i2 + 1, i2], [(1, 0, True), (0, 1, True), (1, 1, False)], carries, os)
        carries, os = lax.fori_loop(
            0, i2 // 2,
            lambda t, c: blocks([i2 - 1 - 2 * t, i2 - 2 - 2 * t],
                                [(0, 0, False), (1, 0, False), (0, 1, False), (1, 1, False)], c[0], c[1]),
            (carries, os))
        for b in range(QB):
            o_ref[b * T:(b + 1) * T, :] = os[b].astype(BF16)
            c_ref[b * T:(b + 1) * T, :] = jnp.where(masks[0], carries[b, 0], carries[b, 1])

    blk = pl.BlockSpec((QB * T, 2 * DH), lambda p, i: (i, p))
    full = pl.BlockSpec((S, 2 * DH), lambda p, i: (0, p))
    return _call(
        body, (q, k, v), name="attn_fwd", grid=(SBW // (2 * DH), S // (QB * T)),
        in_specs=[blk, full, full], out_specs=[blk, blk],
        out_shape=[_sds((S, SBW), BF16), _sds((S, SBW), F32)],
        compiler_params=_params(("arbitrary", "arbitrary"), 40), exchange=exchange)


def _attn_bwd(q, k, v, do, ctot, after=()):
    T = TA
    nq = S // (QB * T)

    def body(q_ref, k_ref, v_ref, do_ref, c_ref, dq_ref, dk_ref, dv_ref, dk_acc, dv_acc):
        step = pl.program_id(1)
        i2 = 2 * step

        @pl.when(step == 0)
        def _():
            dk_acc[...] = jnp.zeros_like(dk_acc)
            dv_acc[...] = jnp.zeros_like(dv_acc)

        row = lax.broadcasted_iota(jnp.int32, (T, T), 0)
        col = lax.broadcasted_iota(jnp.int32, (T, T), 1)
        upto = (row <= col).astype(BF16)
        before = (row < col).astype(BF16)
        causal = col < row
        masks = _head_masks()
        qms, doms, ctots = {}, {}, {}
        for b in range(QB):
            q2, do2 = q_ref[b * T:(b + 1) * T, :], do_ref[b * T:(b + 1) * T, :]
            for h, hm in enumerate(masks):
                qms[b, h] = jnp.where(hm, q2, jnp.zeros_like(q2))
                doms[b, h] = jnp.where(hm, do2, jnp.zeros_like(do2))
                ctots[b, h] = c_ref[b * T:(b + 1) * T, h * DH:h * DH + 1]

        def blocks(keys, pairs, sums, dqs):
            rows = [pl.ds(pl.multiple_of(j * T, T), T) for j in keys]
            ks, vs = [k_ref[r, :] for r in rows], [v_ref[r, :] for r in rows]
            kms = [[jnp.where(hm, kj, jnp.zeros_like(kj)) for hm in masks] for kj in ks]
            units = [(n, h) for n in range(len(pairs)) for h in range(2)]
            qks = {(n, h): _nt(qms[pairs[n][0], h], ks[pairs[n][1]]) for n, h in units}
            das = {(n, h): _nt(doms[pairs[n][0], h], vs[pairs[n][1]]) for n, h in units}
            lbs, l1ms = {}, {}
            for u in units:
                lbs[u], l1m = _log_terms(qks[u])
                l1ms[u] = jnp.where(causal, l1m, 0.0) if pairs[u[0]][2] else l1m
            pins = {u: _hilo_dot(l1ms[u], upto) for u in units}
            sums = dict(sums)
            a_s, dls, cps = {}, {}, {}
            for n, h in units:
                b, _, diag = pairs[n]
                cl, cp = sums[b, h]
                a = jnp.exp2(lbs[n, h] + (ctots[b, h] - cl) - pins[n, h])
                if diag:
                    a = jnp.where(causal, a, 0.0)
                a_s[n, h] = a.astype(BF16)
                dls[n, h] = das[n, h] * a
                cps[n, h] = cp
                sums[b, h] = (cl + jnp.sum(l1ms[n, h], axis=1, keepdims=True),
                              cp + jnp.sum(dls[n, h], axis=1, keepdims=True))
            pexs = {u: _hilo_dot(dls[u], before) for u in units}
            dzbs = {}
            for u in units:
                dz = dls[u] - jnp.exp2(lbs[u]) * (dls[u] + pexs[u] + cps[u])
                if pairs[u[0]][2]:
                    dz = jnp.where(causal, dz, 0.0)
                dzbs[u] = dz.astype(BF16)
            dqs = list(dqs)
            for n, h in units:
                dqs[pairs[n][0]] = dqs[pairs[n][0]] + _nn(dzbs[n, h], kms[pairs[n][1]][h])
            for key, r in enumerate(rows):
                mine = [(n, h) for n, h in units if pairs[n][1] == key]
                dk_acc[r, :] += functools.reduce(jnp.add, [_tn(dzbs[u], qms[pairs[u[0]][0], u[1]]) for u in mine])
                dv_acc[r, :] += functools.reduce(jnp.add, [_tn(a_s[u], doms[pairs[u[0]][0], u[1]]) for u in mine])
            return sums, tuple(dqs)

        zero = jnp.zeros((T, 1), F32)
        sums = {(b, h): (zero, zero) for b in range(QB) for h in range(2)}
        dqs = tuple(jnp.zeros((T, 2 * DH), F32) for _ in range(QB))
        sums, dqs = lax.fori_loop(
            0, i2 // 2,
            lambda t, c: blocks([2 * t, 2 * t + 1],
                                [(0, 0, False), (1, 0, False), (0, 1, False), (1, 1, False)], c[0], c[1]),
            (sums, dqs))
        _, dqs = blocks([i2, i2 + 1], [(0, 0, True), (1, 0, False), (1, 1, True)], sums, dqs)
        for b in range(QB):
            dq_ref[b * T:(b + 1) * T, :] = (dqs[b] * SCALE).astype(BF16)

        @pl.when(step == nq - 1)
        def _():
            dk_ref[...] = (dk_acc[...] * SCALE).astype(BF16)
            dv_ref[...] = dv_acc[...].astype(BF16)

    blk = pl.BlockSpec((QB * T, 2 * DH), lambda p, i: (i, p))
    full = pl.BlockSpec((S, 2 * DH), lambda p, i: (0, p))
    return _call(
        body, (q, k, v, do, ctot), name="attn_bwd", grid=(SBW // (2 * DH), nq),
        in_specs=[blk, full, full, blk, blk], out_specs=[blk, full, full],
        out_shape=[_sds((S, SBW), BF16), _sds((S, SBW), BF16), _sds((S, SBW), BF16)],
        scratch_shapes=[pltpu.VMEM((S, 2 * DH), F32), pltpu.VMEM((S, 2 * DH), F32)],
        compiler_params=_params(("arbitrary", "arbitrary"), 40), after=after)


def _pool_counts(first_row, tm):
    pos = first_row + lax.broadcasted_iota(jnp.int32, (tm, 1), 0)
    return [jnp.minimum(pos + 1, w).astype(F32) for w in POOL_WINDOWS]


def _mix_out(h, xp, o_sb, gp, gs, w_group, scale, w_bp, w_ba, w_out, exchange=None):
    tm = 512

    def body(h_ref, xp_ref, o_ref, gp_ref, gs_ref, wg_hbm, sc_ref, wbp_hbm, wba_hbm, wo_hbm,
             h2_ref, pm_ref, p_ref, yp_ref, ys_ref, m_ref, halo, wg_ref, wbp_ref, wba_ref, wo_ref):
        _stage([(wg_hbm, wg_ref), (wbp_hbm, wbp_ref), (wba_hbm, wba_ref), (wo_hbm, wo_ref)])
        i = pl.program_id(0)

        @pl.when(i == 0)
        def _():
            halo[...] = jnp.zeros_like(halo)

        xp = xp_ref[...]
        ext = jnp.concatenate([halo[...], xp], axis=0)
        halo[...] = xp[tm - HALO:, :]
        counts = _pool_counts(i * tm, tm)
        for gi in range(len(POOL_WINDOWS)):
            lanes = slice(gi * PG, (gi + 1) * PG)
            win = ext[:, lanes]
            for step in range(gi + 1):
                win = win + pltpu.roll(win, 1 << step, 0)
            pm = (win[HALO:, :] / counts[gi] - xp[:, lanes]).astype(BF16)
            pm_ref[:, lanes] = pm
            p_ref[:, lanes] = (_nn(pm, wg_ref[gi]) * sc_ref[:, lanes]).astype(BF16)
        pb = p_ref[...]
        ob = o_ref[...]
        for j in range(NSH):
            cols = slice(j * (D // NSH), (j + 1) * (D // NSH))
            yp = _nn(pb, wbp_ref[j])
            ys = _nn(ob, wba_ref[j])
            yp_ref[:, cols] = yp.astype(BF16)
            ys_ref[:, cols] = ys.astype(BF16)
            m_ref[:, cols] = (gp_ref[:, cols].astype(F32) * yp + gs_ref[:, cols].astype(F32) * ys).astype(BF16)
        h2_ref[...] = h_ref[...] + _nn(m_ref[...], wo_ref[...])

    return _call(
        body, (h, xp, o_sb, gp, gs, w_group, scale, w_bp, w_ba, w_out), name="mix_out", grid=(S // tm,),
        in_specs=[_rows(tm, D), _rows(tm, PW), _rows(tm, SBW), _rows(tm, D), _rows(tm, D),
                  _ANY, _fixed((1, PW)), _ANY, _ANY, _ANY],
        out_specs=[_rows(tm, D), _rows(tm, PW), _rows(tm, PW), _rows(tm, D), _rows(tm, D), _rows(tm, D)],
        out_shape=[_sds((S, D), F32), _sds((S, PW), BF16), _sds((S, PW), BF16), _sds((S, D), BF16),
                   _sds((S, D), BF16), _sds((S, D), BF16)],
        scratch_shapes=[pltpu.VMEM((HALO, PW), F32)] + _vmem_like(w_group, w_bp, w_ba, w_out),
        compiler_params=_params(("arbitrary",), 48), free=(5, 6), exchange=exchange)


def _mix_bwd_out(dh, gp, gs, yp, ys, pm, w_group, scale, w_bp, w_ba, w_out, exchange=None):
    tm = 512
    nt = S // tm

    def body(dh_ref, gp_ref, gs_ref, yp_ref, ys_ref, pm_ref, wg_hbm, sc_ref, wbp_hbm, wba_hbm, wo_hbm,
             dlg_ref, dyp_ref, dys_ref, do_ref, dyg_ref, dxp_ref, dsc_ref, halo, wg_ref, wbp_ref, wba_ref, wo_ref):
        _stage([(wg_hbm, wg_ref), (wbp_hbm, wbp_ref), (wba_hbm, wba_ref), (wo_hbm, wo_ref)])
        step = pl.program_id(0)

        @pl.when(step == 0)
        def _():
            halo[...] = jnp.zeros_like(halo)
            dsc_ref[...] = jnp.zeros_like(dsc_ref)

        dm = _nt(dh_ref[...].astype(BF16), wo_ref[...])
        gp = gp_ref[...].astype(F32)
        gs = gs_ref[...].astype(F32)
        yp = yp_ref[...].astype(F32)
        ys = ys_ref[...].astype(F32)
        dlg_ref[:, :D] = (dm * yp * gp * (1.0 - gp)).astype(BF16)
        dlg_ref[:, D:] = (dm * ys * gs * (1.0 - gs)).astype(BF16)
        dyp_ref[...] = (dm * gp).astype(BF16)
        dys_ref[...] = (dm * gs).astype(BF16)
        dp = jnp.zeros((tm, PW), F32)
        do = jnp.zeros((tm, SBW), F32)
        for j in range(NSH):
            cols = slice(j * (D // NSH), (j + 1) * (D // NSH))
            dp = dp + _nt(dyp_ref[:, cols], wbp_ref[j])
            do = do + _nt(dys_ref[:, cols], wba_ref[j])
        do_ref[...] = do.astype(BF16)
        counts = _pool_counts((nt - 1 - step) * tm, tm)
        dscale = []
        for gi in range(len(POOL_WINDOWS)):
            lanes = slice(gi * PG, (gi + 1) * PG)
            dpg = dp[:, lanes]
            dscale.append(jnp.sum(dpg * _nn(pm_ref[:, lanes], wg_ref[gi]), axis=0, keepdims=True))
            dyg = (dpg * sc_ref[:, lanes]).astype(BF16)
            dyg_ref[:, lanes] = dyg
            dpm = _nt(dyg, wg_ref[gi])
            per = dpm / counts[gi]
            win = jnp.concatenate([per, halo[:, lanes]], axis=0)
            halo[:, lanes] = per[:HALO, :]
            for s in range(gi + 1):
                win = win + pltpu.roll(win, tm + HALO - (1 << s), 0)
            dxp_ref[:, lanes] = (win[:tm, :] - dpm).astype(BF16)
        dsc_ref[...] += jnp.concatenate(dscale, axis=1)

    rev = lambda width: pl.BlockSpec((tm, width), lambda i: (nt - 1 - i, 0))
    return _call(
        body, (dh, gp, gs, yp, ys, pm, w_group, scale, w_bp, w_ba, w_out), name="mix_bwd_out", grid=(nt,),
        in_specs=[rev(D), rev(D), rev(D), rev(D), rev(D), rev(PW), _ANY, _fixed((1, PW)), _ANY, _ANY, _ANY],
        out_specs=[rev(2 * D), rev(D), rev(D), rev(SBW), rev(PW), rev(PW), _fixed((1, PW))],
        out_shape=[_sds((S, 2 * D), BF16), _sds((S, D), BF16), _sds((S, D), BF16), _sds((S, SBW), BF16),
                   _sds((S, PW), BF16), _sds((S, PW), BF16), _sds((1, PW), F32)],
        scratch_shapes=[pltpu.VMEM((HALO, PW), F32)] + _vmem_like(w_group, w_bp, w_ba, w_out),
        compiler_params=_params(("arbitrary",), 48), exchange=exchange)


def _mix_bwd_in(dh, h, gain, pieces, w_in, exchange=None):
    tm = 512
    widths = [p.shape[1] for p in pieces]

    def body(dh_ref, h_ref, g_ref, *rest):
        piece_refs, (w_hbm, dx_ref, dg_ref, w_ref, dp_ref) = rest[:len(pieces)], rest[len(pieces):]
        _stage([(w_hbm, w_ref)])
        at = 0
        for ref, width in zip(piece_refs, widths):
            dp_ref[:, at:at + width] = ref[...]
            at += width
        du = jnp.zeros((tm, D), F32)
        for j in range(NSH):
            du = du + _nt(dp_ref[:, j * D:(j + 1) * D], w_ref[j])
        r, hr = _rms(h_ref[...])
        dx, dgain = _rms_bwd(du, hr, r, g_ref[...])
        dx_ref[...] = dh_ref[...] + dx

        @pl.when(pl.program_id(0) == 0)
        def _():
            dg_ref[...] = jnp.zeros_like(dg_ref)

        dg_ref[...] += dgain

    return _call(
        body, (dh, h, gain, *pieces, w_in), name="mix_bwd_in", grid=(S // tm,),
        in_specs=[_rows(tm, D), _rows(tm, D), _fixed((1, D))] + [_rows(tm, w) for w in widths] + [_ANY],
        out_specs=[_rows(tm, D), _fixed((1, D))],
        out_shape=[_sds((S, D), F32), _sds((1, D), F32)],
        scratch_shapes=_vmem_like(w_in) + [pltpu.VMEM((tm, 4 * D), BF16)],
        compiler_params=_params(("arbitrary",), 48), exchange=exchange)


def _wgrad_in(u, pieces):
    dxp, dq, dk, dv, dlg = pieces

    def body(u_ref, dxp_ref, dq_ref, dk_ref, dv_ref, dlg_ref, o_ref):
        j = pl.program_id(0)
        u = u_ref[...]

        def two(left_ref, right_ref):
            o_ref[:, :PW] = _tn(u, left_ref[...]).astype(BF16)
            o_ref[:, PW:] = _tn(u, right_ref[...]).astype(BF16)

        pl.when(j == 0)(lambda: two(dxp_ref, dq_ref))
        pl.when(j == 1)(lambda: two(dk_ref, dv_ref))

        @pl.when(j >= 2)
        def _():
            o_ref[...] = _tn(u, dlg_ref[...]).astype(BF16)

    whole = lambda width: pl.BlockSpec((S, width), lambda j: (0, 0))
    return _call(
        body, (u, dxp, dq, dk, dv, dlg), name="wgrad_in", grid=(NSH,),
        in_specs=[whole(D), whole(PW), whole(SBW), whole(SBW), whole(SBW),
                  pl.BlockSpec((S, D), lambda j: (0, jnp.maximum(j - 2, 0)))],
        out_specs=[pl.BlockSpec((None, D, D), lambda j: (j, 0, 0))], out_shape=[_sds((NSH, D, D), BF16)],
        compiler_params=_params(("arbitrary",), 56))[0]


def _wgrad(a, b, nblk, ti, name, out_dtype=BF16, exchange=None, after=()):
    ka, n = a.shape[1], b.shape[1]
    ns = n // nblk

    def body(a_ref, b_ref, o_ref):
        o_ref[...] = _tn(a_ref[...].astype(BF16), b_ref[...].astype(BF16)).astype(out_dtype)

    res = _call(
        body, (a, b), name=name, grid=(nblk, ka // ti),
        in_specs=[pl.BlockSpec((S, ti), lambda j, i: (0, i)), pl.BlockSpec((S, ns), lambda j, i: (0, j))],
        out_specs=[pl.BlockSpec((None, ti, ns), lambda j, i: (j, i, 0))],
        out_shape=[_sds((nblk, ka, ns), out_dtype)],
        compiler_params=_params(("arbitrary", "arbitrary"), 56), exchange=exchange, after=after)
    return res[0] if exchange is None else (res[0][0], res[1])


def _wgrad_branches(p, dyp, o_sb, dys, pm, dyg):
    cols = D // NSH

    def body(p_ref, dyp_ref, o_ref, dys_ref, pm_ref, dyg_ref, gbp_ref, gba_ref, gg_ref):
        gbp_ref[...] = _tn(p_ref[...], dyp_ref[...]).astype(BF16)
        gba_ref[...] = _tn(o_ref[...], dys_ref[...]).astype(BF16)
        gg_ref[...] = _tn(pm_ref[...], dyg_ref[...])

    whole = lambda width: pl.BlockSpec((S, width), lambda j: (0, 0))
    col = lambda width: pl.BlockSpec((S, width), lambda j: (0, j))
    return _call(
        body, (p, dyp, o_sb, dys, pm, dyg), name="wgrad_branches", grid=(NSH,),
        in_specs=[whole(PW), col(cols), whole(SBW), col(cols), col(PG), col(PG)],
        out_specs=[pl.BlockSpec((None, PW, cols), lambda j: (j, 0, 0)),
                   pl.BlockSpec((None, SBW, cols), lambda j: (j, 0, 0)),
                   pl.BlockSpec((None, PG, PG), lambda j: (j, 0, 0))],
        out_shape=[_sds((NSH, PW, cols), BF16), _sds((NSH, SBW, cols), BF16), _sds((NSH, PG, PG), F32)],
        compiler_params=_params(("arbitrary",), 40))


def _place():
    x, y, c = lax.axis_index("x"), lax.axis_index("y"), lax.axis_index("c")
    chips = [(1 - x, y), (x, 1 - y), (1 - x, 1 - y)]
    return x, y, c, chips


def _remote(src, dst, ssem, rsem, dev):
    return pltpu.make_async_remote_copy(src_ref=src, dst_ref=dst, send_sem=ssem, recv_sem=rsem,
                                        device_id=dev, device_id_type=MESH)


def _cast_into_block(ws, me_idx, name):
    steps = 4
    shapes = [(w.shape[0] // steps, w.shape[1]) for w in ws]

    def body(me_ref, *refs):
        for w_ref, o_ref in zip(refs[:len(ws)], refs[len(ws):]):
            o_ref[...] = w_ref[...].astype(BF16)

    return pl.pallas_call(
        body, name=name, out_shape=[_sds((NSH,) + w.shape, BF16) for w in ws],
        grid_spec=pltpu.PrefetchScalarGridSpec(
            num_scalar_prefetch=1, grid=(steps,),
            in_specs=[pl.BlockSpec((r, c), lambda s, me: (s, 0)) for r, c in shapes],
            out_specs=[pl.BlockSpec((None, r, c), lambda s, me: (me[0], s, 0)) for r, c in shapes]),
        compiler_params=_params(("arbitrary",), 32),
    )(me_idx, *ws)


def _ex_gather(bufs):
    n = len(bufs)
    per = 8

    def plan(outs, ssem, rsem, w):
        x, y, c, _ = _place()
        sib, nbr_x, nbr_y = (x, y, 1 - c), (1 - x, y, c), (x, 1 - y, c)
        half = outs[w].shape[1] // 2
        quarter = half // 2
        sem = lambda k: (ssem.at[per * w + k], rsem.at[per * w + k])
        rows = lambda blk, start, size: outs[w].at[blk, pl.ds(start, size)]
        mine = rows(2 * x + y, c * half, half)
        from_x = rows(2 * (1 - x) + y, c * half, half)
        from_y = rows(2 * x + (1 - y), c * half, half)
        diag = 2 * (1 - x) + (1 - y)
        pass_y = rows(2 * (1 - x) + y, c * half, quarter)
        pass_x = rows(2 * x + (1 - y), c * half + quarter, quarter)
        diag_0, diag_1 = rows(diag, c * half, quarter), rows(diag, c * half + quarter, quarter)
        first = [_remote(mine, mine, *sem(0), nbr_x), _remote(mine, mine, *sem(1), nbr_y)]
        arrivals = [
            (_remote(from_x, from_x, *sem(0), nbr_x),
             [_remote(pass_y, pass_y, *sem(2), nbr_y), _remote(from_x, from_x, *sem(4), sib)]),
            (_remote(from_y, from_y, *sem(1), nbr_y),
             [_remote(pass_x, pass_x, *sem(3), nbr_x), _remote(from_y, from_y, *sem(5), sib)]),
            (_remote(diag_0, diag_0, *sem(2), nbr_y), [_remote(diag_0, diag_0, *sem(6), sib)]),
            (_remote(diag_1, diag_1, *sem(3), nbr_x), [_remote(diag_1, diag_1, *sem(7), sib)]),
        ]
        other = (1 - c) * half
        from_sibling = [
            _remote(rows(2 * (1 - x) + y, other, half), rows(2 * (1 - x) + y, other, half), *sem(4), sib),
            _remote(rows(2 * x + (1 - y), other, half), rows(2 * x + (1 - y), other, half), *sem(5), sib),
            _remote(rows(diag, other, quarter), rows(diag, other, quarter), *sem(6), sib),
            _remote(rows(diag, other + quarter, quarter), rows(diag, other + quarter, quarter), *sem(7), sib),
        ]
        return first, arrivals, from_sibling

    def start(ins, outs, ssem, rsem):
        x, y, c, _ = _place()
        for w in range(n):
            half = outs[w].shape[1] // 2
            mine = outs[w].at[2 * x + y, pl.ds(c * half, half)]
            _remote(mine, mine, ssem.at[per * w], rsem.at[per * w], (1 - x, y, c)).start()
            _remote(mine, mine, ssem.at[per * w + 1], rsem.at[per * w + 1], (x, 1 - y, c)).start()

    def finish(ins, outs, ssem, rsem):
        plans = [plan(outs, ssem, rsem, w) for w in range(n)]
        started = []
        for direct in (True, False):
            for first, arrivals, _ in plans:
                for arrived, onward in (arrivals[:2] if direct else arrivals[2:]):
                    arrived.wait_recv()
                    for cp in onward:
                        cp.start()
                    started += onward
        for first, _, from_sibling in plans:
            for cp in from_sibling:
                cp.wait_recv()
            started += first
        for cp in started:
            cp.wait_send()

    return Exchange(bufs, [_sds(b.shape, b.dtype) for b in bufs], {w: w for w in range(n)}, per * n, start, finish)


def _ex_gather_direct(bufs):
    n = len(bufs)

    def copies(outs, ssem, rsem, only_first=False):
        x, y, c, chips = _place()
        me, sib = 2 * x + y, (x, y, 1 - c)
        first, relay, last = [], [], []
        for w in range(n):
            half = outs[w].shape[1] // 2
            mine = outs[w].at[me, pl.ds(c * half, half)]
            for k, (px, py) in enumerate(chips):
                sems = (ssem.at[6 * w + k], rsem.at[6 * w + k])
                sib_sems = (ssem.at[6 * w + 3 + k], rsem.at[6 * w + 3 + k])
                first.append(_remote(mine, mine, *sems, (px, py, c)))
                if only_first:
                    continue
                got = outs[w].at[2 * px + py, pl.ds(c * half, half)]
                relay.append((_remote(got, got, *sems, (px, py, c)), _remote(got, got, *sib_sems, sib)))
                theirs = outs[w].at[2 * px + py, pl.ds((1 - c) * half, half)]
                last.append(_remote(theirs, theirs, *sib_sems, sib))
        return first, relay, last

    def start(ins, outs, ssem, rsem):
        for cp in copies(outs, ssem, rsem, only_first=True)[0]:
            cp.start()

    def finish(ins, outs, ssem, rsem):
        first, relay, last = copies(outs, ssem, rsem)
        for arrived, onward in relay:
            arrived.wait_recv()
            onward.start()
        for cp in last:
            cp.wait_recv()
        for cp in first:
            cp.wait_send()
        for _, onward in relay:
            onward.wait_send()

    return Exchange(bufs, [_sds(b.shape, b.dtype) for b in bufs], {w: w for w in range(n)}, 6 * n, start, finish)


def _simple_exchange(arrays, landing, aliases, make_copies, sibling_only=False):
    def start(ins, outs, ssem, rsem):
        for cp, _ in make_copies(ins, outs, ssem, rsem, False):
            cp.start()

    def finish(ins, outs, ssem, rsem):
        cps = make_copies(ins, outs, ssem, rsem, True)
        for _, landed in cps:
            landed.wait_recv()
        for cp, _ in cps:
            cp.wait_send()

    return Exchange(arrays, landing, aliases, len(arrays) * 3, start, finish, sibling_only)


def _ex_pair_swap(grads):
    def make(ins, outs, ssem, rsem, landing):
        x, y, c, _ = _place()
        cps = [_remote(ins[w].at[:, 1 - c], outs[w], ssem.at[w], rsem.at[w], (x, y, 1 - c))
               for w in range(len(grads))]
        return [(cp, cp) for cp in cps]

    return _simple_exchange(grads, [_sds((NSH,) + g.shape[2:], g.dtype) for g in grads], {}, make, True)


def _ex_relay(bufs):
    def make(ins, outs, ssem, rsem, landing):
        x, y, c, chips = _place()
        sib = (x, y, 1 - c)
        out = []
        for w in range(len(bufs)):
            half = outs[w].shape[1] // 2
            for k, (px, py) in enumerate(chips):
                sems = (ssem.at[3 * w + k], rsem.at[3 * w + k])
                have = outs[w].at[2 * px + py, pl.ds(c * half, half)]
                miss = outs[w].at[2 * px + py, pl.ds((1 - c) * half, half)]
                out.append((_remote(have, have, *sems, sib), _remote(miss, miss, *sems, sib) if landing else None))
        return out

    return _simple_exchange(bufs, [_sds(b.shape, b.dtype) for b in bufs], {w: w for w in range(len(bufs))}, make, True)


def _ex_share(bufs):
    def make(ins, outs, ssem, rsem, landing):
        x, y, c, _ = _place()
        sib = (x, y, 1 - c)
        return [(_remote(outs[w].at[c], outs[w].at[c], ssem.at[w], rsem.at[w], sib),
                 _remote(outs[w].at[1 - c], outs[w].at[1 - c], ssem.at[w], rsem.at[w], sib) if landing else None)
                for w in range(len(bufs))]

    return _simple_exchange(bufs, [_sds(b.shape, b.dtype) for b in bufs], {w: w for w in range(len(bufs))}, make, True)


def _small_copies(slots, ssems, rsems, sending):
    x, y, c, _ = _place()
    out = []
    for m in range(1, 8):
        px, py, pc = x ^ (m >> 2), y ^ ((m >> 1) & 1), c ^ (m & 1)
        slot = slots.at[4 * x + 2 * y + c if sending else 4 * px + 2 * py + pc]
        out.append(_remote(slot, slot, ssems[m - 1], rsems[m - 1], (px, py, pc)))
    return out


def _small_gather_start(slots, name, after=()):
    at = 1 + len(after)

    def body(*refs):
        for cp in _small_copies(refs[0], refs[at:at + 7], refs[at + 7:at + 14], True):
            cp.start()
        refs[-1][...] = jnp.zeros_like(refs[-1])

    outs = pl.pallas_call(
        body, name=name,
        out_shape=([pltpu.SemaphoreType.DMA(())] * 14 + [pltpu.HBM(slots.shape, slots.dtype)]
                   + [jax.ShapeDtypeStruct((8, 128), F32)]),
        in_specs=[_HBM] + [_ANY] * len(after), out_specs=[_SEM] * 14 + [_HBM, _VM], input_output_aliases={0: 14},
        compiler_params=pltpu.CompilerParams(has_side_effects=_EFFECT),
    )(*_in_hbm([slots]), *after)
    return outs[:14], outs[14], outs[15]


def _small_gather_wait(sems, slots, after, name):
    def body(*refs):
        for cp in _small_copies(refs[0], refs[1:8], refs[8:15], True):
            cp.wait_send()
        for cp in _small_copies(refs[0], refs[1:8], refs[8:15], False):
            cp.wait_recv()

    return pl.pallas_call(
        body, name=name, out_shape=pltpu.HBM(slots.shape, slots.dtype),
        in_specs=[_HBM] + [_SEM] * 14 + [_ANY] * len(after), out_specs=_HBM, input_output_aliases={0: 0},
        compiler_params=pltpu.CompilerParams(has_side_effects=_EFFECT),
    )(slots, *sems, *after)


def _pair_sum(grads, gots, c_idx, name):
    n = len(grads)

    def body(c_ref, *refs):
        for a_ref, b_ref, o_ref in zip(refs[:n], refs[n:2 * n], refs[2 * n:]):
            o_ref[...] = (a_ref[...].astype(F32) + b_ref[...].astype(F32)).astype(BF16)

    halves = [g.shape[2:] for g in grads]
    return list(pl.pallas_call(
        body, name=name, out_shape=[_sds((NSH,) + h, BF16) for h in halves],
        grid_spec=pltpu.PrefetchScalarGridSpec(
            num_scalar_prefetch=1, grid=(NSH,),
            in_specs=[pl.BlockSpec((None, None) + h, lambda j, c: (j, c[0], 0, 0)) for h in halves]
            + [pl.BlockSpec((None,) + h, lambda j, c: (j, 0, 0)) for h in halves],
            out_specs=[pl.BlockSpec((None,) + h, lambda j, c: (j, 0, 0)) for h in halves]),
        compiler_params=_params(("arbitrary",), 40),
    )(c_idx, *_in_hbm(list(grads) + list(gots))))


def _chip_sum(owns, gots, place, name):
    n = len(owns)

    def body(place_ref, *refs):
        for own_ref, got_ref, o_ref in zip(refs[:n], refs[n:2 * n], refs[2 * n:]):
            acc = own_ref[...].astype(F32)
            for k in range(3):
                acc = acc + got_ref[k].astype(F32)
            o_ref[...] = acc

    shapes = [(o.shape[1] // 2, o.shape[2]) for o in owns]
    return list(pl.pallas_call(
        body, name=name, out_shape=[_sds((2, 2 * r, c), F32) for r, c in shapes],
        grid_spec=pltpu.PrefetchScalarGridSpec(
            num_scalar_prefetch=1, grid=(2,),
            in_specs=[pl.BlockSpec((None, r, c), lambda s, p: (p[0], s, 0)) for r, c in shapes]
            + [pl.BlockSpec((3, r, c), lambda s, p: (0, s, 0)) for r, c in shapes],
            out_specs=[pl.BlockSpec((None, r, c), lambda s, p: (p[1], s, 0)) for r, c in shapes]),
        compiler_params=_params(("arbitrary",), 40),
    )(place, *_in_hbm(list(owns) + list(gots))))


def _adamw_math(w, g, m, v):
    m = B1 * m + (1.0 - B1) * g
    v = B2 * v + (1.0 - B2) * (g * g)
    m_hat = m / (1.0 - B1 ** STEP)
    v_hat = v / (1.0 - B2 ** STEP)
    return -LR * (m_hat / (jnp.sqrt(v_hat) + AEPS) + WD * w), m, v


def _adamw(ws, gs, ms, vs, name, after=()):
    n, steps = len(ws), 4

    def body(*refs):
        ins, outs = refs[:4 * n], refs[4 * n:]
        for i in range(n):
            w_ref, g_ref, m_ref, v_ref = ins[4 * i:4 * i + 4]
            go_ref, d_ref, nm_ref, nv_ref = outs[4 * i:4 * i + 4]
            g = g_ref[...]
            go_ref[...] = g
            d_ref[...], nm_ref[...], nv_ref[...] = _adamw_math(w_ref[...], g, m_ref[...], v_ref[...])

    args, specs, shapes, free = [], [], [], []
    for i, (w, g, m, v) in enumerate(zip(ws, gs, ms, vs)):
        args += [w, g, m, v]
        specs += [pl.BlockSpec((w.shape[0] // steps, w.shape[1]), lambda r: (r, 0))] * 4
        shapes += [_sds(w.shape, F32)] * 4
        free += [4 * i, 4 * i + 2, 4 * i + 3]
    outs = _call(body, args, name=name, grid=(steps,), out_shape=shapes, in_specs=specs, out_specs=specs,
                 compiler_params=_params(("arbitrary",), 48), free=tuple(free), after=after)
    return [outs[4 * i:4 * i + 4] for i in range(n)]


def _small_update(gathered, w, m, v, entries):
    rows = w.shape[0]

    def body(ga_ref, w_ref, m_ref, v_ref, *out_refs):
        for j, (first, n) in enumerate(entries):
            mine = slice(first, first + n)
            g = ga_ref[mine, :]
            for dev in range(1, 8):
                g = g + ga_ref[dev * rows + first:dev * rows + first + n, :]
            results = (g,) + _adamw_math(w_ref[mine, :], g, m_ref[mine, :], v_ref[mine, :])
            for i, res in enumerate(results):
                out_refs[i * len(entries) + j][...] = res

    outs = pl.pallas_call(
        body, name="small_update",
        out_shape=[jax.ShapeDtypeStruct((n, 128), F32) for _ in range(4) for _, n in entries],
        in_specs=[_VM] * 4, out_specs=[_VM] * (4 * len(entries)),
    )(gathered, w, m, v)
    return [outs[i * len(entries):(i + 1) * len(entries)] for i in range(4)]


SMALL = ("ffn1_norm", "mix_norm", "ffn2_norm", "final_norm", "pool_scale", "loss", "pool_w_group")
BIG = ("ffn1_w_gate_up", "ffn1_w_down", "w_in", "w_branch_pool", "w_branch_attn", "w_out",
       "ffn2_w_gate_up", "ffn2_w_down")
ORDER = ("ffn1_norm", "ffn1_w_gate_up", "ffn1_w_down", "mix_norm", "w_in", "pool_w_group", "pool_scale",
         "w_branch_pool", "w_branch_attn", "w_out", "ffn2_norm", "ffn2_w_gate_up", "ffn2_w_down", "final_norm")
SMALL_ROWS = 560


def _pack_small(t):
    parts = []
    for k in SMALL:
        rows = t[k].reshape(-1, 128) if k in t else jnp.zeros((1, 128), F32)
        parts.append(jnp.pad(rows, ((0, -rows.shape[0] % 8), (0, 0))))
    packed = jnp.concatenate(parts, axis=0)
    assert packed.shape == (SMALL_ROWS, 128), packed.shape
    return packed


def _small_entries(like):
    out, at = [], 0
    for k in SMALL:
        n = like[k].size // 128 if k in like else 1
        out.append((at, n))
        at += n + (-n % 8)
    return out


def _halves(g):
    return g.reshape(NSH, 2, g.shape[1] // 2, g.shape[2])


def kernel(x, ffn1_norm, ffn1_w_gate_up, ffn1_w_down, mix_norm, w_in, pool_w_group, pool_scale, w_branch_pool, w_branch_attn, w_out, ffn2_norm, ffn2_w_gate_up, ffn2_w_down, final_norm, loss_target, m_ffn1_norm, m_ffn1_w_gate_up, m_ffn1_w_down, m_mix_norm, m_w_in, m_pool_w_group, m_pool_scale, m_w_branch_pool, m_w_branch_attn, m_w_out, m_ffn2_norm, m_ffn2_w_gate_up, m_ffn2_w_down, m_final_norm, v_ffn1_norm, v_ffn1_w_gate_up, v_ffn1_w_down, v_mix_norm, v_w_in, v_pool_w_group, v_pool_scale, v_w_branch_pool, v_w_branch_attn, v_w_out, v_ffn2_norm, v_ffn2_w_gate_up, v_ffn2_w_down, v_final_norm):
    wts = dict(ffn1_norm=ffn1_norm, ffn1_w_gate_up=ffn1_w_gate_up, ffn1_w_down=ffn1_w_down, mix_norm=mix_norm,
               w_in=w_in, pool_w_group=pool_w_group, pool_scale=pool_scale, w_branch_pool=w_branch_pool,
               w_branch_attn=w_branch_attn, w_out=w_out, ffn2_norm=ffn2_norm, ffn2_w_gate_up=ffn2_w_gate_up,
               ffn2_w_down=ffn2_w_down, final_norm=final_norm)
    mom = dict(ffn1_norm=m_ffn1_norm, ffn1_w_gate_up=m_ffn1_w_gate_up, ffn1_w_down=m_ffn1_w_down,
               mix_norm=m_mix_norm, w_in=m_w_in, pool_w_group=m_pool_w_group, pool_scale=m_pool_scale,
               w_branch_pool=m_w_branch_pool, w_branch_attn=m_w_branch_attn, w_out=m_w_out,
               ffn2_norm=m_ffn2_norm, ffn2_w_gate_up=m_ffn2_w_gate_up, ffn2_w_down=m_ffn2_w_down,
               final_norm=m_final_norm)
    var = dict(ffn1_norm=v_ffn1_norm, ffn1_w_gate_up=v_ffn1_w_gate_up, ffn1_w_down=v_ffn1_w_down,
               mix_norm=v_mix_norm, w_in=v_w_in, pool_w_group=v_pool_w_group, pool_scale=v_pool_scale,
               w_branch_pool=v_w_branch_pool, w_branch_attn=v_w_branch_attn, w_out=v_w_out,
               ffn2_norm=v_ffn2_norm, ffn2_w_gate_up=v_ffn2_w_gate_up, ffn2_w_down=v_ffn2_w_down,
               final_norm=v_final_norm)

    c_idx = lax.axis_index("c").astype(jnp.int32).reshape(1)
    me_idx = (2 * lax.axis_index("x") + lax.axis_index("y")).astype(jnp.int32).reshape(1)
    place = jnp.concatenate([me_idx, c_idx])
    x0, tgt = x[0], loss_target[0]
    wgrp = pool_w_group[0].astype(BF16)
    g1, gm, g2, gf = ffn1_norm, mix_norm, ffn2_norm, final_norm.reshape(1, D)
    grad, delta, new_m, new_v = {}, {}, {}, {}

    def pair_sums(keys, parts, got):
        return _pair_sum(parts, got, c_idx, "pair_sum_" + keys[0])

    def chip_sums(keys, chip_parts, owned):
        return _chip_sum(chip_parts, owned, place, "chip_sum_" + keys[0])

    def adamw(keys, after=()):
        outs = _adamw([wts[k][0] for k in keys], [grad[k][0] for k in keys], [mom[k][0] for k in keys],
                      [var[k][0] for k in keys], "adamw_" + keys[0], after=after)
        for k, res in zip(keys, outs):
            grad[k], delta[k], new_m[k], new_v[k] = (o.reshape(wts[k].shape) for o in res)

    first, late = ("ffn1_w_gate_up", "ffn1_w_down"), ("w_branch_pool", "w_branch_attn", "w_out",
                                                       "ffn2_w_gate_up", "ffn2_w_down")
    own = {}
    for group in (first, ("w_in",), late):
        own.update(zip(group, _cast_into_block([wts[k][0] for k in group], me_idx, "cast_" + group[0])))
    full = dict(zip(first, _exchange_alone(_ex_gather([own[k] for k in first]), "gather_ffn1")))
    wgu1, wd1 = full["ffn1_w_gate_up"], full["ffn1_w_down"].reshape(DFF, D)
    (h1, n1, gu1, a1), (win,) = _ffn_fwd(x0, g1, wgu1, wd1, "ffn1_fwd", exchange=_ex_gather_direct([own["w_in"]]))
    sems_l, thru_l, token_l = _gather_start([own[k_] for k_ in late], [h1], "gather_late_start")
    u, xp, q, k, v, gp, gs = _mix_in(h1, gm, win, after=(token_l,))
    o_sb, ctot = _attn_fwd(q, k, v)
    arrived = _gather_wait(sems_l, thru_l, [o_sb], "gather_late_wait")
    wbp, wba, wout = _exchange_alone(_ex_relay(arrived[:3]), "relay_mix")
    wout = wout.reshape(D, D)
    (h2, pm, p, yp, ys, mm), (wgu2, wd2) = _mix_out(h1, xp, o_sb, gp, gs, wgrp, pool_scale, wbp, wba, wout,
                                                    exchange=_ex_relay(arrived[3:]))
    wd2 = wd2.reshape(DFF, D)
    dh2, dgu3, d_g2, loss_row, d_gf, n3, a3, dh3 = _ffn_last(h2, g2, wgu2, wd2, tgt, gf, "ffn2")

    def grad_gate_up(n, dgu, name, exchange=None):
        res = _wgrad(n, dgu, NSH, D, name, exchange=exchange)
        return [_halves(res)] if exchange is None else ([_halves(res[0])], res[1])

    def grad_down(a, dh, name, exchange=None):
        res = _wgrad(a, dh, 1, FFS, name, exchange=exchange)
        halves = lambda g: [_halves(g.reshape(NSH, DFF // NSH, D))]
        return halves(res) if exchange is None else (halves(res[0]), res[1])

    k_gu2, k_d2, k_gu1, k_d1, k_in = (("ffn2_w_gate_up",), ("ffn2_w_down",), ("ffn1_w_gate_up",),
                                      ("ffn1_w_down",), ("w_in",))
    pa = grad_gate_up(n3, dgu3, "wgrad_gu2") + grad_down(a3, dh3, "wgrad_d2")
    (dlg, dyp, dys, do_sb, dyg, dxp, d_scale), got_a = _mix_bwd_out(
        dh2, gp, gs, yp, ys, pm, wgrp, pool_scale, wbp, wba, wout, exchange=_ex_pair_swap(pa))
    chip_a = pair_sums(k_gu2 + k_d2, pa, got_a)
    kb = ("w_out", "w_branch_pool", "w_branch_attn")
    g_bp, g_ba, d_group = _wgrad_branches(p, dyp, o_sb, dys, pm, dyg)
    pb = [_halves(_wgrad(mm, dh2, 1, D, "wgrad_out").reshape(NSH, D // NSH, D)), _halves(g_bp), _halves(g_ba)]
    k_a, k_in = k_gu2 + k_d2, k_in + kb
    sems_a, thru_a, token_a = _scatter_start(chip_a, "scatter_a_start")
    dq, dk, dv = _attn_bwd(q, k, v, do_sb, ctot, after=(token_a,))
    chip_a, owned_a = _scatter_wait(sems_a, thru_a, [dq], "scatter_a_wait")
    halves_a = chip_sums(k_a, chip_a, owned_a)
    dproj = (dxp, dq, dk, dv, dlg)
    (dh1, d_gm), both_a = _mix_bwd_in(dh2, h1, gm, dproj, win, exchange=_ex_share(halves_a))
    for i, k_ in enumerate(k_a):
        grad[k_] = both_a[i].reshape(wts[k_].shape)

    p_in = [_halves(_wgrad_in(u, dproj))] + pb
    p_d1, got_in = grad_down(a1, dh1, "wgrad_d1", exchange=_ex_pair_swap(p_in))
    sems_in, thru_in, token_in = _scatter_start(pair_sums(k_in, p_in, got_in), "scatter_in_start")
    dgu1, got_d1 = _ffn_bwd_act(dh1, gu1, wd1, "ffn1_bwd_act", exchange=_ex_pair_swap(p_d1), after=(token_in,))
    sems_d1, thru_d1, token_d1 = _scatter_start(pair_sums(k_d1, p_d1, got_d1), "scatter_d1_start")
    p_gu1 = [_halves(_wgrad(n1, dgu1, NSH, D, "wgrad_gu1", after=(token_in, token_d1)))]
    sems_w, thru_w, token_w = _swap_start(p_gu1, "swap_gu1_start")
    chip_in, owned_in = _scatter_wait(sems_in, thru_in, [token_w], "scatter_in_wait")
    chip_d1, owned_d1 = _scatter_wait(sems_d1, thru_d1, [token_w], "scatter_d1_wait")
    halves_in = chip_sums(k_in, chip_in, owned_in)
    p_gu1, got_gu1 = _swap_wait(sems_w, thru_w, halves_in, "swap_gu1_wait")
    sems, thru, token = _scatter_start(pair_sums(k_gu1, p_gu1, got_gu1), "scatter_gu1_start")
    sems_h, thru_h, token_h = _share_start(halves_in, [token], "share_in_start")
    adamw(k_a, after=(token_h,))
    landed = _share_wait(sems_h, thru_h, [delta[k_a[0]]], "share_in_wait")
    for i, k_ in enumerate(k_in):
        grad[k_] = landed[i].reshape(wts[k_].shape)
    adamw(k_in)
    dx, d_g1 = _ffn_bwd_in(dh1, x0, g1, dgu1, wgu1, "ffn1_bwd_in", after=(token,))
    small_g = dict(ffn1_norm=d_g1, mix_norm=d_gm, ffn2_norm=d_g2, final_norm=d_gf, pool_scale=d_scale,
                   pool_w_group=d_group, loss=loss_row)
    dev = 4 * lax.axis_index("x") + 2 * lax.axis_index("y") + lax.axis_index("c")
    slots = lax.dynamic_update_slice(jnp.zeros((8, SMALL_ROWS, 128), F32), _pack_small(small_g)[None], (dev, 0, 0))
    chip_gu1, owned_gu1 = _scatter_wait(sems, thru, [dx] + [delta[k_] for k_ in k_a + k_in], "scatter_gu1_wait")
    halves_last = chip_sums(k_d1 + k_gu1, chip_d1 + chip_gu1, owned_d1 + owned_gu1)
    sems_l, thru_l, token_l = _share_start(halves_last, [], "share_last_start")
    sems_s, slots, token_s = _small_gather_start(slots, "small_gather_start", after=(token_l,))
    both = _share_wait(sems_l, thru_l, [token_s], "share_last_wait")
    grad["ffn1_w_down"] = both[0].reshape(ffn1_w_down.shape)
    grad["ffn1_w_gate_up"] = both[1].reshape(ffn1_w_gate_up.shape)
    adamw(k_d1 + k_gu1, after=(token_s,))
    gathered = _small_gather_wait(sems_s, slots, [delta[k_] for k_ in k_d1 + k_gu1], "small_gather_wait")
    gathered = gathered.reshape(8 * SMALL_ROWS, 128)
    results = _small_update(gathered, _pack_small(wts), _pack_small(mom), _pack_small(var), _small_entries(wts))
    for dst, entries in zip((grad, delta, new_m, new_v), results):
        for k_, rows in zip(SMALL, entries):
            if k_ in wts:
                dst[k_] = rows.reshape(wts[k_].shape)
            elif dst is grad:
                loss = rows[0, 0]
    return (loss, dx[None], *[grad[k_] for k_ in ORDER], *[delta[k_] for k_ in ORDER],
            *[new_m[k_] for k_ in ORDER], *[new_v[k_] for k_ in ORDER])
```

```python
import dataclasses
import functools

import jax
import jax.numpy as jnp
from jax import lax
from jax.experimental import pallas as pl
from jax.experimental.pallas import tpu as pltpu

F32 = jnp.float32
BF16 = jnp.bfloat16

S = 2048
D = 1024
DFF = 2816
FFS = 2 * DFF // 4
NSH = 4
PW = 512
PG = 128
POOL_WINDOWS = (2, 4, 8, 16)
HALO = 16
SBW = 512
DH = 64
EPS = 1e-6
SCALE = 0.125
LOG2E = 1.4426950408889634
TA = 256
QB = 2
MIB = 1024 * 1024

LR, B1, B2, AEPS, WD, STEP = 0.001, 0.9, 0.999, 1e-08, 0.01, 10

_VM = pl.BlockSpec(memory_space=pltpu.VMEM)
_ANY = pl.BlockSpec(memory_space=pl.ANY)
MESH = pl.DeviceIdType.MESH
SIBLING_PAIR_ID = 1


def _nn(a, b):
    return jnp.dot(a, b, preferred_element_type=F32)


def _nt(a, b):
    return lax.dot_general(a, b, (((1,), (1,)), ((), ())), preferred_element_type=F32)


def _tn(a, b):
    return lax.dot_general(a, b, (((0,), (0,)), ((), ())), preferred_element_type=F32)


def _params(sem, vmem_mib):
    return pltpu.CompilerParams(dimension_semantics=sem, vmem_limit_bytes=vmem_mib * MIB)


def _rows(tm, width):
    return pl.BlockSpec((tm, width), lambda i: (i, 0))


def _fixed(shape):
    return pl.BlockSpec(shape, lambda *_: (0,) * len(shape))


def _sds(shape, dtype):
    return pltpu.HBM(shape, dtype)


def _in_hbm(args):
    return [pltpu.with_memory_space_constraint(a, pltpu.HBM) for a in args]


def _stage(pairs):
    pieces = 4

    def copy_all(sems):
        copies = []
        for src, dst in pairs:
            step = src.shape[0] // pieces
            for p in range(pieces):
                part = pl.ds(p * step, step)
                if len(dst.shape) == len(src.shape):
                    piece = (src.at[part], dst.at[part])
                else:
                    piece = (src.at[p], dst.at[:, pl.ds(p * src.shape[2], src.shape[2])])
                copies.append(pltpu.make_async_copy(*piece, sems.at[len(copies)]))
        for c in copies:
            c.start()
        for c in copies:
            c.wait()

    @pl.when(pl.program_id(0) == 0)
    def _():
        pl.run_scoped(copy_all, pltpu.SemaphoreType.DMA((pieces * len(pairs),)))


def _vmem_like(*arrays):
    return [pltpu.VMEM(a.shape, a.dtype) for a in arrays]


def _vmem_wide(w):
    return pltpu.VMEM((w.shape[1], w.shape[0] * w.shape[2]), w.dtype)


FF_CHUNKS = ((0, 1536), (1536, DFF - 1536))


def _wide_columns(src, dst, c0, cn):
    width, out = src.shape[2], []
    for p in range(src.shape[0]):
        lo, hi = max(c0, p * width), min(c0 + cn, (p + 1) * width)
        if lo < hi:
            out.append((src.at[p, :, pl.ds(lo - p * width, hi - lo)], dst.at[:, pl.ds(lo, hi - lo)]))
    return out


def _staged(groups, compute):
    first = pl.program_id(0) == 0

    def with_copies(sems):
        copies = []
        for group in groups:
            base = sum(len(g) for g in copies)
            copies.append([pltpu.make_async_copy(s, d, sems.at[base + i]) for i, (s, d) in enumerate(group)])
        for group in copies:
            for c in group:
                c.start()

        def ready(k):
            for c in copies[k]:
                c.wait()

        compute(ready)

    @pl.when(first)
    def _():
        pl.run_scoped(with_copies, pltpu.SemaphoreType.DMA((sum(len(g) for g in groups),)))

    @pl.when(jnp.logical_not(first))
    def _():
        compute(lambda k: None)


class Exchange:
    def __init__(self, arrays, landing, aliases, n_sems, start, finish, sibling_only=False):
        self.arrays, self.landing, self.aliases, self.n_sems = list(arrays), list(landing), dict(aliases), n_sems
        self.start, self.finish = start, finish
        self.sibling_only = sibling_only

    def enter(self):
        if self.sibling_only:
            barrier = pltpu.get_barrier_semaphore()
            sibling = (lax.axis_index("x"), lax.axis_index("y"), 1 - lax.axis_index("c"))
            pl.semaphore_signal(barrier, inc=1, device_id=sibling, device_id_type=MESH)
            pl.semaphore_wait(barrier, 1)

    def params(self, compiler_params=None):
        kw = dict(collective_id=SIBLING_PAIR_ID) if self.sibling_only else {}
        if compiler_params is None:
            return pltpu.CompilerParams(**kw)
        return dataclasses.replace(compiler_params, **kw)


def _call(body, args, *, name, grid, in_specs, out_specs, out_shape, scratch_shapes=(), compiler_params=None,
          exchange=None, free=(), after=()):
    args = [a if i in free else pltpu.with_memory_space_constraint(a, pltpu.HBM) for i, a in enumerate(args)]
    if exchange is None:
        n_in = len(in_specs)

        def plain(*refs):
            body(*refs[:n_in], *refs[n_in + len(after):])

        return pl.pallas_call(plain, name=name, grid=grid, in_specs=list(in_specs) + [_ANY] * len(after),
                              out_specs=out_specs, out_shape=out_shape, scratch_shapes=list(scratch_shapes),
                              compiler_params=compiler_params)(*args, *after)
    ex = exchange
    n_in, n_out, n_scr = len(in_specs), len(out_specs), len(scratch_shapes)
    na, nl = len(ex.arrays), len(ex.landing)

    def hosted(*refs):
        at = [0]

        def take(n):
            at[0] += n
            return refs[at[0] - n:at[0]]

        k_in, _, e_in, k_out, e_out, k_scr = take(n_in), take(len(after)), take(na), take(n_out), take(nl), take(n_scr)
        ssem, rsem = take(2)
        ids = [pl.program_id(a) for a in range(len(grid))]
        first = functools.reduce(jnp.logical_and, [i == 0 for i in ids])
        last = functools.reduce(jnp.logical_and, [i == g - 1 for i, g in zip(ids, grid)])

        @pl.when(first)
        def _():
            ex.enter()
            ex.start(e_in, e_out, ssem, rsem)

        body(*k_in, *k_out, *k_scr)

        @pl.when(last)
        def _():
            ex.finish(e_in, e_out, ssem, rsem)

    outs = pl.pallas_call(
        hosted, name=name, grid=grid,
        in_specs=list(in_specs) + [_ANY] * (len(after) + na), out_specs=list(out_specs) + [_ANY] * nl,
        out_shape=list(out_shape) + ex.landing,
        scratch_shapes=list(scratch_shapes) + [pltpu.SemaphoreType.DMA((ex.n_sems,))] * 2,
        input_output_aliases={n_in + len(after) + i: n_out + j for i, j in ex.aliases.items()},
        compiler_params=ex.params(compiler_params),
    )(*args, *after, *_in_hbm(ex.arrays))
    return outs[:n_out], outs[n_out:]


def _exchange_alone(ex, name, after=()):
    na, nl = len(ex.arrays), len(ex.landing)

    def body(*refs):
        outs = refs[na + len(after):na + len(after) + nl]
        ex.enter()
        ex.start(refs[:na], outs, refs[-2], refs[-1])
        ex.finish(refs[:na], outs, refs[-2], refs[-1])

    return pl.pallas_call(
        body, name=name, in_specs=[_ANY] * (na + len(after)), out_specs=[_ANY] * nl,
        out_shape=ex.landing, scratch_shapes=[pltpu.SemaphoreType.DMA((ex.n_sems,))] * 2,
        input_output_aliases=ex.aliases, compiler_params=ex.params(),
    )(*_in_hbm(ex.arrays), *after)


_HBM = pl.BlockSpec(memory_space=pltpu.HBM)
_SEM = pl.BlockSpec(memory_space=pltpu.SEMAPHORE)
_EFFECT = pltpu.SideEffectType.DATAFLOW_SIDE_EFFECTING


def _scatter_copies(srcs, lands, ssems, rsems):
    x, y, c, chips = _place()
    return [_remote(srcs[w].at[2 * px + py], lands[w].at[k], ssems[3 * w + k], rsems[3 * w + k], (px, py, c))
            for w in range(len(srcs)) for k, (px, py) in enumerate(chips)]


def _scatter_start(parts, name):
    parts = list(parts)
    n, ncp = len(parts), 3 * len(parts)
    lands = [lax.empty((3,) + p.shape[1:], p.dtype) for p in parts]

    def body(*refs):
        srcs, land_refs = refs[:n], refs[n:2 * n]
        ssems, rsems = refs[2 * n:2 * n + ncp], refs[2 * n + ncp:2 * n + 2 * ncp]
        for cp in _scatter_copies(srcs, land_refs, ssems, rsems):
            cp.start()
        token = refs[-1]
        token[...] = jnp.zeros_like(token)

    outs = pl.pallas_call(
        body, name=name,
        out_shape=([pltpu.SemaphoreType.DMA(())] * (2 * ncp) + [pltpu.HBM(a.shape, a.dtype) for a in parts + lands]
                   + [jax.ShapeDtypeStruct((8, 128), F32)]),
        in_specs=[_HBM] * (2 * n), out_specs=[_SEM] * (2 * ncp) + [_HBM] * (2 * n) + [_VM],
        input_output_aliases={i: 2 * ncp + i for i in range(2 * n)},
        compiler_params=pltpu.CompilerParams(has_side_effects=_EFFECT),
    )(*_in_hbm(parts), *_in_hbm(lands))
    sems, thru, token = outs[:2 * ncp], outs[2 * ncp:2 * ncp + 2 * n], outs[-1]
    return sems, thru, token


def _scatter_wait(sems, thru, after, name):
    n = len(thru) // 2
    ncp = 3 * n

    def body(*refs):
        srcs, land_refs = refs[:n], refs[n:2 * n]
        ssems, rsems = refs[2 * n:2 * n + ncp], refs[2 * n + ncp:2 * n + 2 * ncp]
        for cp in _scatter_copies(srcs, land_refs, ssems, rsems):
            cp.wait_send()
            cp.wait_recv()

    outs = pl.pallas_call(
        body, name=name, out_shape=[pltpu.HBM(a.shape, a.dtype) for a in thru],
        in_specs=[_HBM] * (2 * n) + [_SEM] * (2 * ncp) + [_ANY] * len(after), out_specs=[_HBM] * (2 * n),
        input_output_aliases={i: i for i in range(2 * n)},
        compiler_params=pltpu.CompilerParams(has_side_effects=_EFFECT),
    )(*thru, *sems, *after)
    return outs[:n], outs[n:]


def _swap_copies(srcs, lands, ssems, rsems):
    x, y, c, _ = _place()
    return [_remote(srcs[w].at[:, 1 - c], lands[w], ssems[w], rsems[w], (x, y, 1 - c)) for w in range(len(srcs))]


def _swap_start(grads, name):
    grads = list(grads)
    n = len(grads)
    lands = [lax.empty((NSH,) + g.shape[2:], g.dtype) for g in grads]

    def body(*refs):
        barrier = pltpu.get_barrier_semaphore()
        sibling = (lax.axis_index("x"), lax.axis_index("y"), 1 - lax.axis_index("c"))
        pl.semaphore_signal(barrier, inc=1, device_id=sibling, device_id_type=MESH)
        pl.semaphore_wait(barrier, 1)
        for cp in _swap_copies(refs[:n], refs[n:2 * n], refs[2 * n:3 * n], refs[3 * n:4 * n]):
            cp.start()
        refs[-1][...] = jnp.zeros_like(refs[-1])

    outs = pl.pallas_call(
        body, name=name,
        out_shape=([pltpu.SemaphoreType.DMA(())] * (2 * n) + [pltpu.HBM(a.shape, a.dtype) for a in grads + lands]
                   + [jax.ShapeDtypeStruct((8, 128), F32)]),
        in_specs=[_HBM] * (2 * n), out_specs=[_SEM] * (2 * n) + [_HBM] * (2 * n) + [_VM],
        input_output_aliases={i: 2 * n + i for i in range(2 * n)},
        compiler_params=pltpu.CompilerParams(has_side_effects=_EFFECT, collective_id=SIBLING_PAIR_ID),
    )(*_in_hbm(grads), *_in_hbm(lands))
    return outs[:2 * n], outs[2 * n:4 * n], outs[-1]


def _swap_wait(sems, thru, after, name):
    n = len(thru) // 2

    def body(*refs):
        for cp in _swap_copies(refs[:n], refs[n:2 * n], refs[2 * n:3 * n], refs[3 * n:4 * n]):
            cp.wait_send()
            cp.wait_recv()

    outs = pl.pallas_call(
        body, name=name, out_shape=[pltpu.HBM(a.shape, a.dtype) for a in thru],
        in_specs=[_HBM] * (2 * n) + [_SEM] * (2 * n) + [_ANY] * len(after), out_specs=[_HBM] * (2 * n),
        input_output_aliases={i: i for i in range(2 * n)},
        compiler_params=pltpu.CompilerParams(has_side_effects=_EFFECT),
    )(*thru, *sems, *after)
    return outs[:n], outs[n:]


def _share_copies(bufs, ssems, rsems, sending):
    x, y, c, _ = _place()
    out = []
    for w, ref in enumerate(bufs):
        slot = ref.at[c if sending else 1 - c]
        out.append(_remote(slot, slot, ssems[w], rsems[w], (x, y, 1 - c)))
    return out


def _share_start(bufs, after, name):
    bufs = list(bufs)
    n = len(bufs)

    def body(*refs):
        barrier = pltpu.get_barrier_semaphore()
        sibling = (lax.axis_index("x"), lax.axis_index("y"), 1 - lax.axis_index("c"))
        pl.semaphore_signal(barrier, inc=1, device_id=sibling, device_id_type=MESH)
        pl.semaphore_wait(barrier, 1)
        at = n + len(after)
        for cp in _share_copies(refs[:n], refs[at:at + n], refs[at + n:at + 2 * n], True):
            cp.start()
        refs[-1][...] = jnp.zeros_like(refs[-1])

    outs = pl.pallas_call(
        body, name=name,
        out_shape=([pltpu.SemaphoreType.DMA(())] * (2 * n) + [pltpu.HBM(a.shape, a.dtype) for a in bufs]
                   + [jax.ShapeDtypeStruct((8, 128), F32)]),
        in_specs=[_HBM] * n + [_ANY] * len(after), out_specs=[_SEM] * (2 * n) + [_HBM] * n + [_VM],
        input_output_aliases={i: 2 * n + i for i in range(n)},
        compiler_params=pltpu.CompilerParams(has_side_effects=_EFFECT, collective_id=SIBLING_PAIR_ID),
    )(*_in_hbm(bufs), *after)
    return outs[:2 * n], outs[2 * n:3 * n], outs[-1]


def _share_wait(sems, thru, after, name):
    n = len(thru)

    def body(*refs):
        for cp in _share_copies(refs[:n], refs[n:2 * n], refs[2 * n:3 * n], True):
            cp.wait_send()
        for cp in _share_copies(refs[:n], refs[n:2 * n], refs[2 * n:3 * n], False):
            cp.wait_recv()

    return pl.pallas_call(
        body, name=name, out_shape=[pltpu.HBM(a.shape, a.dtype) for a in thru],
        in_specs=[_HBM] * n + [_SEM] * (2 * n) + [_ANY] * len(after), out_specs=[_HBM] * n,
        input_output_aliases={i: i for i in range(n)},
        compiler_params=pltpu.CompilerParams(has_side_effects=_EFFECT),
    )(*thru, *sems, *after)


def _gather_copies(bufs, ssems, rsems, sending):
    x, y, c, chips = _place()
    out = []
    for w, ref in enumerate(bufs):
        half = ref.shape[1] // 2
        for k, (px, py) in enumerate(chips):
            rows = ref.at[2 * x + y if sending else 2 * px + py, pl.ds(c * half, half)]
            out.append(_remote(rows, rows, ssems[3 * w + k], rsems[3 * w + k], (px, py, c)))
    return out


def _gather_start(bufs, after, name):
    n, ncp = len(bufs), 3 * len(bufs)

    def body(*refs):
        ssems, rsems = refs[n + len(after):n + len(after) + ncp], refs[n + len(after) + ncp:n + len(after) + 2 * ncp]
        for cp in _gather_copies(refs[:n], ssems, rsems, True):
            cp.start()
        token = refs[-1]
        token[...] = jnp.zeros_like(token)

    outs = pl.pallas_call(
        body, name=name,
        out_shape=([pltpu.SemaphoreType.DMA(())] * (2 * ncp) + [pltpu.HBM(a.shape, a.dtype) for a in bufs]
                   + [jax.ShapeDtypeStruct((8, 128), F32)]),
        in_specs=[_HBM] * n + [_ANY] * len(after), out_specs=[_SEM] * (2 * ncp) + [_HBM] * n + [_VM],
        input_output_aliases={i: 2 * ncp + i for i in range(n)},
        compiler_params=pltpu.CompilerParams(has_side_effects=_EFFECT),
    )(*_in_hbm(bufs), *after)
    return outs[:2 * ncp], outs[2 * ncp:2 * ncp + n], outs[-1]


def _gather_wait(sems, thru, after, name):
    n = len(thru)
    ncp = 3 * n

    def body(*refs):
        ssems, rsems = refs[n:n + ncp], refs[n + ncp:n + 2 * ncp]
        for cp in _gather_copies(refs[:n], ssems, rsems, True):
            cp.wait_send()
        for cp in _gather_copies(refs[:n], ssems, rsems, False):
            cp.wait_recv()

    return pl.pallas_call(
        body, name=name, out_shape=[pltpu.HBM(a.shape, a.dtype) for a in thru],
        in_specs=[_HBM] * n + [_SEM] * (2 * ncp) + [_ANY] * len(after), out_specs=[_HBM] * n,
        input_output_aliases={i: i for i in range(n)},
        compiler_params=pltpu.CompilerParams(has_side_effects=_EFFECT),
    )(*thru, *sems, *after)


def _rms(x):
    r = lax.rsqrt(jnp.mean(x * x, axis=-1, keepdims=True) + EPS)
    return r, x * r


def _rms_bwd(dn, xr, r, gain):
    dng = dn * gain
    dx = r * (dng - xr * jnp.mean(dng * xr, axis=-1, keepdims=True))
    return dx, jnp.sum(dn * xr, axis=0, keepdims=True)


def _ffn_weight_groups(wgu_hbm, wgu_ref, wd_hbm, wd_ref):
    groups = []
    for c0, cn in FF_CHUNKS:
        groups += [_wide_columns(wgu_hbm, wgu_ref, c0, cn), _wide_columns(wgu_hbm, wgu_ref, DFF + c0, cn),
                   [(wd_hbm.at[pl.ds(c0, cn)], wd_ref.at[pl.ds(c0, cn)])]]
    return groups


def _ffn_fwd(x, gain, wgu, wd, name, exchange=None):
    tm = 256

    def body(x_ref, g_ref, wgu_hbm, wd_hbm, h_ref, n_ref, gu_ref, a_ref, wgu_ref, wd_ref):
        def compute(ready):
            x = x_ref[...]
            _, xr = _rms(x)
            n = (xr * g_ref[...]).astype(BF16)
            n_ref[...] = n
            acc = jnp.zeros((tm, D), F32)
            for i, (c0, cn) in enumerate(FF_CHUNKS):
                ready(3 * i)
                g = _nn(n, wgu_ref[:, c0:c0 + cn])
                ready(3 * i + 1)
                u = _nn(n, wgu_ref[:, DFF + c0:DFF + c0 + cn])
                gu_ref[:, c0:c0 + cn] = g.astype(BF16)
                gu_ref[:, DFF + c0:DFF + c0 + cn] = u.astype(BF16)
                half_act = (0.5 * (g * jax.nn.sigmoid(g) * u)).astype(BF16)
                a_ref[:, c0:c0 + cn] = half_act
                ready(3 * i + 2)
                acc = acc + _nn(half_act, wd_ref[c0:c0 + cn, :])
            h_ref[...] = x + acc

        _staged(_ffn_weight_groups(wgu_hbm, wgu_ref, wd_hbm, wd_ref), compute)

    return _call(
        body, (x, gain, wgu, wd), name=name, grid=(S // tm,),
        in_specs=[_rows(tm, D), _fixed((1, D)), _ANY, _ANY],
        out_specs=[_rows(tm, D), _rows(tm, D), _rows(tm, 4 * FFS), _rows(tm, DFF)],
        out_shape=[_sds((S, D), F32), _sds((S, D), BF16), _sds((S, 4 * FFS), BF16), _sds((S, DFF), BF16)],
        scratch_shapes=[_vmem_wide(wgu)] + _vmem_like(wd),
        compiler_params=_params(("arbitrary",), 56), exchange=exchange)


def _ffn_last(x, gain, wgu, wd, target, gf, name):
    tm = 256

    def body(x_ref, g_ref, wgu_hbm, wd_hbm, t_ref, gf_ref, dx_ref, dgu_ref, dg_ref, loss_ref, dgf_ref, n_ref,
             a_ref, dh_ref, wgu_ref, wd_ref):
        @pl.when(pl.program_id(0) == 0)
        def _():
            dg_ref[...] = jnp.zeros_like(dg_ref)
            dgf_ref[...] = jnp.zeros_like(dgf_ref)
            loss_ref[...] = jnp.zeros_like(loss_ref)

        def compute(ready):
            x = x_ref[...]
            r0, xr = _rms(x)
            n = (xr * g_ref[...]).astype(BF16)
            n_ref[...] = n
            acc = jnp.zeros((tm, D), F32)
            kept = []
            for i, (c0, cn) in enumerate(FF_CHUNKS):
                ready(3 * i)
                g = _nn(n, wgu_ref[:, c0:c0 + cn])
                ready(3 * i + 1)
                u = _nn(n, wgu_ref[:, DFF + c0:DFF + c0 + cn])
                kept.append((g.astype(BF16), u.astype(BF16)))
                half_act = (0.5 * (g * jax.nn.sigmoid(g) * u)).astype(BF16)
                a_ref[:, c0:c0 + cn] = half_act
                ready(3 * i + 2)
                acc = acc + _nn(half_act, wd_ref[c0:c0 + cn, :])
            h = x + acc
            gf = gf_ref[...]
            r, hr = _rms(h)
            err = hr * gf - t_ref[...]
            dh, dgain_f = _rms_bwd(err * (1.0 / D), hr, r, gf)
            dh_ref[...] = dh
            dhb = dh.astype(BF16)
            dn = jnp.zeros((tm, D), F32)
            for (c0, cn), (gb, ub) in zip(FF_CHUNKS, kept):
                g, u = gb.astype(F32), ub.astype(F32)
                da = 0.5 * _nt(dhb, wd_ref[c0:c0 + cn, :])
                sg = jax.nn.sigmoid(g)
                dgb = (da * u * (sg * (1.0 + g * (1.0 - sg)))).astype(BF16)
                dub = (da * (g * sg)).astype(BF16)
                dgu_ref[:, c0:c0 + cn] = dgb
                dgu_ref[:, DFF + c0:DFF + c0 + cn] = dub
                dn = dn + _nt(dgb, wgu_ref[:, c0:c0 + cn]) + _nt(dub, wgu_ref[:, DFF + c0:DFF + c0 + cn])
            dx, dgain = _rms_bwd(dn, xr, r0, g_ref[...])
            dx_ref[...] = dh + dx
            dg_ref[...] += dgain
            dgf_ref[...] += dgain_f
            loss_ref[...] += jnp.full((1, 128), (0.5 / D) * jnp.sum(err * err), F32)

        _staged(_ffn_weight_groups(wgu_hbm, wgu_ref, wd_hbm, wd_ref), compute)

    return _call(
        body, (x, gain, wgu, wd, target, gf), name=name, grid=(S // tm,),
        in_specs=[_rows(tm, D), _fixed((1, D)), _ANY, _ANY, _rows(tm, D), _fixed((1, D))],
        out_specs=[_rows(tm, D), _rows(tm, 4 * FFS), _fixed((1, D)), _fixed((1, 128)), _fixed((1, D)),
                   _rows(tm, D), _rows(tm, DFF), _rows(tm, D)],
        out_shape=[_sds((S, D), F32), _sds((S, 4 * FFS), BF16), _sds((1, D), F32), _sds((1, 128), F32),
                   _sds((1, D), F32), _sds((S, D), BF16), _sds((S, DFF), BF16), _sds((S, D), F32)],
        scratch_shapes=[_vmem_wide(wgu)] + _vmem_like(wd),
        compiler_params=_params(("arbitrary",), 58), free=(4, 5))


def _ffn_bwd_act(dh, gu, wd, name, exchange=None, after=()):
    tm = 512

    def body(dh_ref, gu_ref, wd_hbm, dgu_ref, wd_ref):
        _stage([(wd_hbm, wd_ref)])
        dhb = dh_ref[...].astype(BF16)
        for c0, cn in FF_CHUNKS:
            g = gu_ref[:, c0:c0 + cn].astype(F32)
            u = gu_ref[:, DFF + c0:DFF + c0 + cn].astype(F32)
            da = 0.5 * _nt(dhb, wd_ref[c0:c0 + cn, :])
            sg = jax.nn.sigmoid(g)
            dgu_ref[:, c0:c0 + cn] = (da * u * (sg * (1.0 + g * (1.0 - sg)))).astype(BF16)
            dgu_ref[:, DFF + c0:DFF + c0 + cn] = (da * (g * sg)).astype(BF16)

    res = _call(
        body, (dh, gu, wd), name=name, grid=(S // tm,),
        in_specs=[_rows(tm, D), _rows(tm, 4 * FFS), _ANY], out_specs=[_rows(tm, 4 * FFS)],
        out_shape=[_sds((S, 4 * FFS), BF16)], scratch_shapes=_vmem_like(wd),
        compiler_params=_params(("arbitrary",), 56), exchange=exchange, after=after)
    return res[0] if exchange is None else (res[0][0], res[1])


def _ffn_bwd_in(dh, x, gain, dgu, wgu, name, exchange=None, after=()):
    tm = 512

    def body(dh_ref, x_ref, g_ref, dgu_ref, wgu_hbm, dx_ref, dg_ref, wgu_ref):
        chunks = [(half + c0, cn) for half in (0, DFF) for c0, cn in FF_CHUNKS]

        @pl.when(pl.program_id(0) == 0)
        def _():
            dg_ref[...] = jnp.zeros_like(dg_ref)

        def compute(ready):
            dn = jnp.zeros((tm, D), F32)
            for k, (c0, cn) in enumerate(chunks):
                ready(k)
                dn = dn + _nt(dgu_ref[:, c0:c0 + cn], wgu_ref[:, c0:c0 + cn])
            r, xr = _rms(x_ref[...])
            dx, dgain = _rms_bwd(dn, xr, r, g_ref[...])
            dx_ref[...] = dh_ref[...] + dx
            dg_ref[...] += dgain

        _staged([_wide_columns(wgu_hbm, wgu_ref, c0, cn) for c0, cn in chunks], compute)

    return _call(
        body, (dh, x, gain, dgu, wgu), name=name, grid=(S // tm,),
        in_specs=[_rows(tm, D), _rows(tm, D), _fixed((1, D)), _rows(tm, 4 * FFS), _ANY],
        out_specs=[_rows(tm, D), _fixed((1, D))],
        out_shape=[_sds((S, D), F32), _sds((1, D), F32)],
        scratch_shapes=[_vmem_wide(wgu)],
        compiler_params=_params(("arbitrary",), 56), exchange=exchange, after=after)


def _mix_in(h, gain, w_in, after=()):
    tm = 512

    def body(h_ref, g_ref, w_hbm, u_ref, xp_ref, q_ref, k_ref, v_ref, gp_ref, gs_ref, w_ref):
        _stage([(w_hbm, w_ref)])
        _, hr = _rms(h_ref[...])
        u = (hr * g_ref[...]).astype(BF16)
        u_ref[...] = u
        p0 = _nn(u, w_ref[0])
        xp_ref[...] = p0[:, :PW]
        q_ref[...] = p0[:, PW:].astype(BF16)
        p1 = _nn(u, w_ref[1])
        k_ref[...] = p1[:, :SBW].astype(BF16)
        v_ref[...] = p1[:, SBW:].astype(BF16)
        gp_ref[...] = jax.nn.sigmoid(_nn(u, w_ref[2])).astype(BF16)
        gs_ref[...] = jax.nn.sigmoid(_nn(u, w_ref[3])).astype(BF16)

    return _call(
        body, (h, gain, w_in), name="mix_in", grid=(S // tm,),
        in_specs=[_rows(tm, D), _fixed((1, D)), _ANY],
        out_specs=[_rows(tm, D), _rows(tm, PW), _rows(tm, SBW), _rows(tm, SBW), _rows(tm, SBW),
                   _rows(tm, D), _rows(tm, D)],
        out_shape=[_sds((S, D), BF16), _sds((S, PW), F32), _sds((S, SBW), BF16), _sds((S, SBW), BF16),
                   _sds((S, SBW), BF16), _sds((S, D), BF16), _sds((S, D), BF16)],
        scratch_shapes=_vmem_like(w_in),
        compiler_params=_params(("arbitrary",), 48), free=(1,), after=after)


def _hilo_dot(x, tri):
    hi = x.astype(BF16)
    lo = (x - hi.astype(F32)).astype(BF16)
    return _nn(hi, tri) + _nn(lo, tri)


def _log_terms(qk):
    z2 = qk * (SCALE * LOG2E)
    lb = jnp.minimum(z2, 0.0) - jnp.log2(1.0 + jnp.exp2(-jnp.abs(z2)))
    return lb, lb - z2


def _head_masks():
    lane = lax.broadcasted_iota(jnp.int32, (1, 2 * DH), 1)
    return (lane < DH, lane >= DH)


def _attn_fwd(q, k, v, exchange=None):
    T = TA

    def body(q_ref, k_ref, v_ref, o_ref, c_ref):
        i2 = 2 * pl.program_id(1)
        row = lax.broadcasted_iota(jnp.int32, (T, T), 0)
        col = lax.broadcasted_iota(jnp.int32, (T, T), 1)
        after = (row > col).astype(BF16)
        causal = col < row
        masks = _head_masks()
        qms = {}
        for b in range(QB):
            q2 = q_ref[b * T:(b + 1) * T, :]
            for h, hm in enumerate(masks):
                qms[b, h] = jnp.where(hm, q2, jnp.zeros_like(q2))

        def blocks(keys, pairs, carries, os):
            ks, vms = [], []
            for j in keys:
                rows = pl.ds(pl.multiple_of(j * T, T), T)
                vj = v_ref[rows, :]
                ks.append(k_ref[rows, :])
                vms.append([jnp.where(hm, vj, jnp.zeros_like(vj)) for hm in masks])
            units = [(n, h) for n in range(len(pairs)) for h in range(2)]
            qks = {(n, h): _nt(qms[pairs[n][0], h], ks[pairs[n][1]]) for n, h in units}
            lbs, l1ms = {}, {}
            for u in units:
                lbs[u], l1m = _log_terms(qks[u])
                l1ms[u] = jnp.where(causal, l1m, 0.0) if pairs[u[0]][2] else l1m
            cins = {u: _hilo_dot(l1ms[u], after) for u in units}
            carries, os = dict(carries), list(os)
            for n, h in units:
                b, key, diag = pairs[n]
                a = jnp.exp2(lbs[n, h] + cins[n, h] + carries[b, h])
                if diag:
                    a = jnp.where(causal, a, 0.0)
                os[b] = os[b] + _nn(a.astype(BF16), vms[key][h])
                carries[b, h] = carries[b, h] + jnp.sum(l1ms[n, h], axis=1, keepdims=True)
            return carries, tuple(os)

        carries = {(b, h): jnp.zeros((T, 1), F32) for b in range(QB) for h in range(2)}
        os = tuple(jnp.zeros((T, 2 * DH), F32) for _ in range(QB))
        carries, os = blocks([i2 + 1, i2], [(1, 0, True), (0, 1, True), (1, 1, False)], carries, os)
        carries, os = lax.fori_loop(
            0, i2 // 2,
            lambda t, c: blocks([i2 - 1 - 2 * t, i2 - 2 - 2 * t],
                                [(0, 0, False), (1, 0, False), (0, 1, False), (1, 1, False)], c[0], c[1]),
            (carries, os))
        for b in range(QB):
            o_ref[b * T:(b + 1) * T, :] = os[b].astype(BF16)
            c_ref[b * T:(b + 1) * T, :] = jnp.where(masks[0], carries[b, 0], carries[b, 1])

    blk = pl.BlockSpec((QB * T, 2 * DH), lambda p, i: (i, p))
    full = pl.BlockSpec((S, 2 * DH), lambda p, i: (0, p))
    return _call(
        body, (q, k, v), name="attn_fwd", grid=(SBW // (2 * DH), S // (QB * T)),
        in_specs=[blk, full, full], out_specs=[blk, blk],
        out_shape=[_sds((S, SBW), BF16), _sds((S, SBW), F32)],
        compiler_params=_params(("arbitrary", "arbitrary"), 40), exchange=exchange)


def _attn_bwd(q, k, v, do, ctot, after=()):
    T = TA
    nq = S // (QB * T)

    def body(q_ref, k_ref, v_ref, do_ref, c_ref, dq_ref, dk_ref, dv_ref, dk_acc, dv_acc):
        step = pl.program_id(1)
        i2 = 2 * step

        @pl.when(step == 0)
        def _():
            dk_acc[...] = jnp.zeros_like(dk_acc)
            dv_acc[...] = jnp.zeros_like(dv_acc)

        row = lax.broadcasted_iota(jnp.int32, (T, T), 0)
        col = lax.broadcasted_iota(jnp.int32, (T, T), 1)
        upto = (row <= col).astype(BF16)
        before = (row < col).astype(BF16)
        causal = col < row
        masks = _head_masks()
        qms, doms, ctots = {}, {}, {}
        for b in range(QB):
            q2, do2 = q_ref[b * T:(b + 1) * T, :], do_ref[b * T:(b + 1) * T, :]
            for h, hm in enumerate(masks):
                qms[b, h] = jnp.where(hm, q2, jnp.zeros_like(q2))
                doms[b, h] = jnp.where(hm, do2, jnp.zeros_like(do2))
                ctots[b, h] = c_ref[b * T:(b + 1) * T, h * DH:h * DH + 1]

        def blocks(keys, pairs, sums, dqs):
            rows = [pl.ds(pl.multiple_of(j * T, T), T) for j in keys]
            ks, vs = [k_ref[r, :] for r in rows], [v_ref[r, :] for r in rows]
            kms = [[jnp.where(hm, kj, jnp.zeros_like(kj)) for hm in masks] for kj in ks]
            units = [(n, h) for n in range(len(pairs)) for h in range(2)]
            qks = {(n, h): _nt(qms[pairs[n][0], h], ks[pairs[n][1]]) for n, h in units}
            das = {(n, h): _nt(doms[pairs[n][0], h], vs[pairs[n][1]]) for n, h in units}
            lbs, l1ms = {}, {}
            for u in units:
                lbs[u], l1m = _log_terms(qks[u])
                l1ms[u] = jnp.where(causal, l1m, 0.0) if pairs[u[0]][2] else l1m
            pins = {u: _hilo_dot(l1ms[u], upto) for u in units}
            sums = dict(sums)
            a_s, dls, cps = {}, {}, {}
            for n, h in units:
                b, _, diag = pairs[n]
                cl, cp = sums[b, h]
                a = jnp.exp2(lbs[n, h] + (ctots[b, h] - cl) - pins[n, h])
                if diag:
                    a = jnp.where(causal, a, 0.0)
                a_s[n, h] = a.astype(BF16)
                dls[n, h] = das[n, h] * a
                cps[n, h] = cp
                sums[b, h] = (cl + jnp.sum(l1ms[n, h], axis=1, keepdims=True),
                              cp + jnp.sum(dls[n, h], axis=1, keepdims=True))
            pexs = {u: _hilo_dot(dls[u], before) for u in units}
            dzbs = {}
            for u in units:
                dz = dls[u] - jnp.exp2(lbs[u]) * (dls[u] + pexs[u] + cps[u])
                if pairs[u[0]][2]:
                    dz = jnp.where(causal, dz, 0.0)
                dzbs[u] = dz.astype(BF16)
            dqs = list(dqs)
            for n, h in units:
                dqs[pairs[n][0]] = dqs[pairs[n][0]] + _nn(dzbs[n, h], kms[pairs[n][1]][h])
            for key, r in enumerate(rows):
                mine = [(n, h) for n, h in units if pairs[n][1] == key]
                dk_acc[r, :] += functools.reduce(jnp.add, [_tn(dzbs[u], qms[pairs[u[0]][0], u[1]]) for u in mine])
                dv_acc[r, :] += functools.reduce(jnp.add, [_tn(a_s[u], doms[pairs[u[0]][0], u[1]]) for u in mine])
            return sums, tuple(dqs)

        zero = jnp.zeros((T, 1), F32)
        sums = {(b, h): (zero, zero) for b in range(QB) for h in range(2)}
        dqs = tuple(jnp.zeros((T, 2 * DH), F32) for _ in range(QB))
        sums, dqs = lax.fori_loop(
            0, i2 // 2,
            lambda t, c: blocks([2 * t, 2 * t + 1],
                                [(0, 0, False), (1, 0, False), (0, 1, False), (1, 1, False)], c[0], c[1]),
            (sums, dqs))
        _, dqs = blocks([i2, i2 + 1], [(0, 0, True), (1, 0, False), (1, 1, True)], sums, dqs)
        for b in range(QB):
            dq_ref[b * T:(b + 1) * T, :] = (dqs[b] * SCALE).astype(BF16)

        @pl.when(step == nq - 1)
        def _():
            dk_ref[...] = (dk_acc[...] * SCALE).astype(BF16)
            dv_ref[...] = dv_acc[...].astype(BF16)

    blk = pl.BlockSpec((QB * T, 2 * DH), lambda p, i: (i, p))
    full = pl.BlockSpec((S, 2 * DH), lambda p, i: (0, p))
    return _call(
        body, (q, k, v, do, ctot), name="attn_bwd", grid=(SBW // (2 * DH), nq),
        in_specs=[blk, full, full, blk, blk], out_specs=[blk, full, full],
        out_shape=[_sds((S, SBW), BF16), _sds((S, SBW), BF16), _sds((S, SBW), BF16)],
        scratch_shapes=[pltpu.VMEM((S, 2 * DH), F32), pltpu.VMEM((S, 2 * DH), F32)],
        compiler_params=_params(("arbitrary", "arbitrary"), 40), after=after)


def _pool_counts(first_row, tm):
    pos = first_row + lax.broadcasted_iota(jnp.int32, (tm, 1), 0)
    return [jnp.minimum(pos + 1, w).astype(F32) for w in POOL_WINDOWS]


def _mix_out(h, xp, o_sb, gp, gs, w_group, scale, w_bp, w_ba, w_out, exchange=None):
    tm = 512

    def body(h_ref, xp_ref, o_ref, gp_ref, gs_ref, wg_hbm, sc_ref, wbp_hbm, wba_hbm, wo_hbm,
             h2_ref, pm_ref, p_ref, yp_ref, ys_ref, m_ref, halo, wg_ref, wbp_ref, wba_ref, wo_ref):
        _stage([(wg_hbm, wg_ref), (wbp_hbm, wbp_ref), (wba_hbm, wba_ref), (wo_hbm, wo_ref)])
        i = pl.program_id(0)

        @pl.when(i == 0)
        def _():
            halo[...] = jnp.zeros_like(halo)

        xp = xp_ref[...]
        ext = jnp.concatenate([halo[...], xp], axis=0)
        halo[...] = xp[tm - HALO:, :]
        counts = _pool_counts(i * tm, tm)
        for gi in range(len(POOL_WINDOWS)):
            lanes = slice(gi * PG, (gi + 1) * PG)
            win = ext[:, lanes]
            for step in range(gi + 1):
                win = win + pltpu.roll(win, 1 << step, 0)
            pm = (win[HALO:, :] / counts[gi] - xp[:, lanes]).astype(BF16)
            pm_ref[:, lanes] = pm
            p_ref[:, lanes] = (_nn(pm, wg_ref[gi]) * sc_ref[:, lanes]).astype(BF16)
        pb = p_ref[...]
        ob = o_ref[...]
        for j in range(NSH):
            cols = slice(j * (D // NSH), (j + 1) * (D // NSH))
            yp = _nn(pb, wbp_ref[j])
            ys = _nn(ob, wba_ref[j])
            yp_ref[:, cols] = yp.astype(BF16)
            ys_ref[:, cols] = ys.astype(BF16)
            m_ref[:, cols] = (gp_ref[:, cols].astype(F32) * yp + gs_ref[:, cols].astype(F32) * ys).astype(BF16)
        h2_ref[...] = h_ref[...] + _nn(m_ref[...], wo_ref[...])

    return _call(
        body, (h, xp, o_sb, gp, gs, w_group, scale, w_bp, w_ba, w_out), name="mix_out", grid=(S // tm,),
        in_specs=[_rows(tm, D), _rows(tm, PW), _rows(tm, SBW), _rows(tm, D), _rows(tm, D),
                  _ANY, _fixed((1, PW)), _ANY, _ANY, _ANY],
        out_specs=[_rows(tm, D), _rows(tm, PW), _rows(tm, PW), _rows(tm, D), _rows(tm, D), _rows(tm, D)],
        out_shape=[_sds((S, D), F32), _sds((S, PW), BF16), _sds((S, PW), BF16), _sds((S, D), BF16),
                   _sds((S, D), BF16), _sds((S, D), BF16)],
        scratch_shapes=[pltpu.VMEM((HALO, PW), F32)] + _vmem_like(w_group, w_bp, w_ba, w_out),
        compiler_params=_params(("arbitrary",), 48), free=(5, 6), exchange=exchange)


def _mix_bwd_out(dh, gp, gs, yp, ys, pm, w_group, scale, w_bp, w_ba, w_out, exchange=None):
    tm = 512
    nt = S // tm

    def body(dh_ref, gp_ref, gs_ref, yp_ref, ys_ref, pm_ref, wg_hbm, sc_ref, wbp_hbm, wba_hbm, wo_hbm,
             dlg_ref, dyp_ref, dys_ref, do_ref, dyg_ref, dxp_ref, dsc_ref, halo, wg_ref, wbp_ref, wba_ref, wo_ref):
        _stage([(wg_hbm, wg_ref), (wbp_hbm, wbp_ref), (wba_hbm, wba_ref), (wo_hbm, wo_ref)])
        step = pl.program_id(0)

        @pl.when(step == 0)
        def _():
            halo[...] = jnp.zeros_like(halo)
            dsc_ref[...] = jnp.zeros_like(dsc_ref)

        dm = _nt(dh_ref[...].astype(BF16), wo_ref[...])
        gp = gp_ref[...].astype(F32)
        gs = gs_ref[...].astype(F32)
        yp = yp_ref[...].astype(F32)
        ys = ys_ref[...].astype(F32)
        dlg_ref[:, :D] = (dm * yp * gp * (1.0 - gp)).astype(BF16)
        dlg_ref[:, D:] = (dm * ys * gs * (1.0 - gs)).astype(BF16)
        dyp_ref[...] = (dm * gp).astype(BF16)
        dys_ref[...] = (dm * gs).astype(BF16)
        dp = jnp.zeros((tm, PW), F32)
        do = jnp.zeros((tm, SBW), F32)
        for j in range(NSH):
            cols = slice(j * (D // NSH), (j + 1) * (D // NSH))
            dp = dp + _nt(dyp_ref[:, cols], wbp_ref[j])
            do = do + _nt(dys_ref[:, cols], wba_ref[j])
        do_ref[...] = do.astype(BF16)
        counts = _pool_counts((nt - 1 - step) * tm, tm)
        dscale = []
        for gi in range(len(POOL_WINDOWS)):
            lanes = slice(gi * PG, (gi + 1) * PG)
            dpg = dp[:, lanes]
            dscale.append(jnp.sum(dpg * _nn(pm_ref[:, lanes], wg_ref[gi]), axis=0, keepdims=True))
            dyg = (dpg * sc_ref[:, lanes]).astype(BF16)
            dyg_ref[:, lanes] = dyg
            dpm = _nt(dyg, wg_ref[gi])
            per = dpm / counts[gi]
            win = jnp.concatenate([per, halo[:, lanes]], axis=0)
            halo[:, lanes] = per[:HALO, :]
            for s in range(gi + 1):
                win = win + pltpu.roll(win, tm + HALO - (1 << s), 0)
            dxp_ref[:, lanes] = (win[:tm, :] - dpm).astype(BF16)
        dsc_ref[...] += jnp.concatenate(dscale, axis=1)

    rev = lambda width: pl.BlockSpec((tm, width), lambda i: (nt - 1 - i, 0))
    return _call(
        body, (dh, gp, gs, yp, ys, pm, w_group, scale, w_bp, w_ba, w_out), name="mix_bwd_out", grid=(nt,),
        in_specs=[rev(D), rev(D), rev(D), rev(D), rev(D), rev(PW), _ANY, _fixed((1, PW)), _ANY, _ANY, _ANY],
        out_specs=[rev(2 * D), rev(D), rev(D), rev(SBW), rev(PW), rev(PW), _fixed((1, PW))],
        out_shape=[_sds((S, 2 * D), BF16), _sds((S, D), BF16), _sds((S, D), BF16), _sds((S, SBW), BF16),
                   _sds((S, PW), BF16), _sds((S, PW), BF16), _sds((1, PW), F32)],
        scratch_shapes=[pltpu.VMEM((HALO, PW), F32)] + _vmem_like(w_group, w_bp, w_ba, w_out),
        compiler_params=_params(("arbitrary",), 48), exchange=exchange)


def _mix_bwd_in(dh, h, gain, pieces, w_in, exchange=None):
    tm = 512
    widths = [p.shape[1] for p in pieces]

    def body(dh_ref, h_ref, g_ref, *rest):
        piece_refs, (w_hbm, dx_ref, dg_ref, w_ref, dp_ref) = rest[:len(pieces)], rest[len(pieces):]
        _stage([(w_hbm, w_ref)])
        at = 0
        for ref, width in zip(piece_refs, widths):
            dp_ref[:, at:at + width] = ref[...]
            at += width
        du = jnp.zeros((tm, D), F32)
        for j in range(NSH):
            du = du + _nt(dp_ref[:, j * D:(j + 1) * D], w_ref[j])
        r, hr = _rms(h_ref[...])
        dx, dgain = _rms_bwd(du, hr, r, g_ref[...])
        dx_ref[...] = dh_ref[...] + dx

        @pl.when(pl.program_id(0) == 0)
        def _():
            dg_ref[...] = jnp.zeros_like(dg_ref)

        dg_ref[...] += dgain

    return _call(
        body, (dh, h, gain, *pieces, w_in), name="mix_bwd_in", grid=(S // tm,),
        in_specs=[_rows(tm, D), _rows(tm, D), _fixed((1, D))] + [_rows(tm, w) for w in widths] + [_ANY],
        out_specs=[_rows(tm, D), _fixed((1, D))],
        out_shape=[_sds((S, D), F32), _sds((1, D), F32)],
        scratch_shapes=_vmem_like(w_in) + [pltpu.VMEM((tm, 4 * D), BF16)],
        compiler_params=_params(("arbitrary",), 48), exchange=exchange)


def _wgrad_in(u, pieces):
    dxp, dq, dk, dv, dlg = pieces

    def body(u_ref, dxp_ref, dq_ref, dk_ref, dv_ref, dlg_ref, o_ref):
        j = pl.program_id(0)
        u = u_ref[...]

        def two(left_ref, right_ref):
            o_ref[:, :PW] = _tn(u, left_ref[...]).astype(BF16)
            o_ref[:, PW:] = _tn(u, right_ref[...]).astype(BF16)

        pl.when(j == 0)(lambda: two(dxp_ref, dq_ref))
        pl.when(j == 1)(lambda: two(dk_ref, dv_ref))

        @pl.when(j >= 2)
        def _():
            o_ref[...] = _tn(u, dlg_ref[...]).astype(BF16)

    whole = lambda width: pl.BlockSpec((S, width), lambda j: (0, 0))
    return _call(
        body, (u, dxp, dq, dk, dv, dlg), name="wgrad_in", grid=(NSH,),
        in_specs=[whole(D), whole(PW), whole(SBW), whole(SBW), whole(SBW),
                  pl.BlockSpec((S, D), lambda j: (0, jnp.maximum(j - 2, 0)))],
        out_specs=[pl.BlockSpec((None, D, D), lambda j: (j, 0, 0))], out_shape=[_sds((NSH, D, D), BF16)],
        compiler_params=_params(("arbitrary",), 56))[0]


def _wgrad(a, b, nblk, ti, name, out_dtype=BF16, exchange=None, after=()):
    ka, n = a.shape[1], b.shape[1]
    ns = n // nblk

    def body(a_ref, b_ref, o_ref):
        o_ref[...] = _tn(a_ref[...].astype(BF16), b_ref[...].astype(BF16)).astype(out_dtype)

    res = _call(
        body, (a, b), name=name, grid=(nblk, ka // ti),
        in_specs=[pl.BlockSpec((S, ti), lambda j, i: (0, i)), pl.BlockSpec((S, ns), lambda j, i: (0, j))],
        out_specs=[pl.BlockSpec((None, ti, ns), lambda j, i: (j, i, 0))],
        out_shape=[_sds((nblk, ka, ns), out_dtype)],
        compiler_params=_params(("arbitrary", "arbitrary"), 56), exchange=exchange, after=after)
    return res[0] if exchange is None else (res[0][0], res[1])


def _wgrad_branches(p, dyp, o_sb, dys, pm, dyg):
    cols = D // NSH

    def body(p_ref, dyp_ref, o_ref, dys_ref, pm_ref, dyg_ref, gbp_ref, gba_ref, gg_ref):
        gbp_ref[...] = _tn(p_ref[...], dyp_ref[...]).astype(BF16)
        gba_ref[...] = _tn(o_ref[...], dys_ref[...]).astype(BF16)
        gg_ref[...] = _tn(pm_ref[...], dyg_ref[...])

    whole = lambda width: pl.BlockSpec((S, width), lambda j: (0, 0))
    col = lambda width: pl.BlockSpec((S, width), lambda j: (0, j))
    return _call(
        body, (p, dyp, o_sb, dys, pm, dyg), name="wgrad_branches", grid=(NSH,),
        in_specs=[whole(PW), col(cols), whole(SBW), col(cols), col(PG), col(PG)],
        out_specs=[pl.BlockSpec((None, PW, cols), lambda j: (j, 0, 0)),
                   pl.BlockSpec((None, SBW, cols), lambda j: (j, 0, 0)),
                   pl.BlockSpec((None, PG, PG), lambda j: (j, 0, 0))],
        out_shape=[_sds((NSH, PW, cols), BF16), _sds((NSH, SBW, cols), BF16), _sds((NSH, PG, PG), F32)],
        compiler_params=_params(("arbitrary",), 40))


def _place():
    x, y, c = lax.axis_index("x"), lax.axis_index("y"), lax.axis_index("c")
    chips = [(1 - x, y), (x, 1 - y), (1 - x, 1 - y)]
    return x, y, c, chips


def _remote(src, dst, ssem, rsem, dev):
    return pltpu.make_async_remote_copy(src_ref=src, dst_ref=dst, send_sem=ssem, recv_sem=rsem,
                                        device_id=dev, device_id_type=MESH)


def _cast_into_block(ws, me_idx, name):
    steps = 4
    shapes = [(w.shape[0] // steps, w.shape[1]) for w in ws]

    def body(me_ref, *refs):
        for w_ref, o_ref in zip(refs[:len(ws)], refs[len(ws):]):
            o_ref[...] = w_ref[...].astype(BF16)

    return pl.pallas_call(
        body, name=name, out_shape=[_sds((NSH,) + w.shape, BF16) for w in ws],
        grid_spec=pltpu.PrefetchScalarGridSpec(
            num_scalar_prefetch=1, grid=(steps,),
            in_specs=[pl.BlockSpec((r, c), lambda s, me: (s, 0)) for r, c in shapes],
            out_specs=[pl.BlockSpec((None, r, c), lambda s, me: (me[0], s, 0)) for r, c in shapes]),
        compiler_params=_params(("arbitrary",), 32),
    )(me_idx, *ws)


def _ex_gather(bufs):
    n = len(bufs)
    per = 8

    def plan(outs, ssem, rsem, w):
        x, y, c, _ = _place()
        sib, nbr_x, nbr_y = (x, y, 1 - c), (1 - x, y, c), (x, 1 - y, c)
        half = outs[w].shape[1] // 2
        quarter = half // 2
        sem = lambda k: (ssem.at[per * w + k], rsem.at[per * w + k])
        rows = lambda blk, start, size: outs[w].at[blk, pl.ds(start, size)]
        mine = rows(2 * x + y, c * half, half)
        from_x = rows(2 * (1 - x) + y, c * half, half)
        from_y = rows(2 * x + (1 - y), c * half, half)
        diag = 2 * (1 - x) + (1 - y)
        pass_y = rows(2 * (1 - x) + y, c * half, quarter)
        pass_x = rows(2 * x + (1 - y), c * half + quarter, quarter)
        diag_0, diag_1 = rows(diag, c * half, quarter), rows(diag, c * half + quarter, quarter)
        first = [_remote(mine, mine, *sem(0), nbr_x), _remote(mine, mine, *sem(1), nbr_y)]
        arrivals = [
            (_remote(from_x, from_x, *sem(0), nbr_x),
             [_remote(pass_y, pass_y, *sem(2), nbr_y), _remote(from_x, from_x, *sem(4), sib)]),
            (_remote(from_y, from_y, *sem(1), nbr_y),
             [_remote(pass_x, pass_x, *sem(3), nbr_x), _remote(from_y, from_y, *sem(5), sib)]),
            (_remote(diag_0, diag_0, *sem(2), nbr_y), [_remote(diag_0, diag_0, *sem(6), sib)]),
            (_remote(diag_1, diag_1, *sem(3), nbr_x), [_remote(diag_1, diag_1, *sem(7), sib)]),
        ]
        other = (1 - c) * half
        from_sibling = [
            _remote(rows(2 * (1 - x) + y, other, half), rows(2 * (1 - x) + y, other, half), *sem(4), sib),
            _remote(rows(2 * x + (1 - y), other, half), rows(2 * x + (1 - y), other, half), *sem(5), sib),
            _remote(rows(diag, other, quarter), rows(diag, other, quarter), *sem(6), sib),
            _remote(rows(diag, other + quarter, quarter), rows(diag, other + quarter, quarter), *sem(7), sib),
        ]
        return first, arrivals, from_sibling

    def start(ins, outs, ssem, rsem):
        x, y, c, _ = _place()
        for w in range(n):
            half = outs[w].shape[1] // 2
            mine = outs[w].at[2 * x + y, pl.ds(c * half, half)]
            _remote(mine, mine, ssem.at[per * w], rsem.at[per * w], (1 - x, y, c)).start()
            _remote(mine, mine, ssem.at[per * w + 1], rsem.at[per * w + 1], (x, 1 - y, c)).start()

    def finish(ins, outs, ssem, rsem):
        plans = [plan(outs, ssem, rsem, w) for w in range(n)]
        started = []
        for direct in (True, False):
            for first, arrivals, _ in plans:
                for arrived, onward in (arrivals[:2] if direct else arrivals[2:]):
                    arrived.wait_recv()
                    for cp in onward:
                        cp.start()
                    started += onward
        for first, _, from_sibling in plans:
            for cp in from_sibling:
                cp.wait_recv()
            started += first
        for cp in started:
            cp.wait_send()

    return Exchange(bufs, [_sds(b.shape, b.dtype) for b in bufs], {w: w for w in range(n)}, per * n, start, finish)


def _ex_gather_direct(bufs):
    n = len(bufs)

    def copies(outs, ssem, rsem, only_first=False):
        x, y, c, chips = _place()
        me, sib = 2 * x + y, (x, y, 1 - c)
        first, relay, last = [], [], []
        for w in range(n):
            half = outs[w].shape[1] // 2
            mine = outs[w].at[me, pl.ds(c * half, half)]
            for k, (px, py) in enumerate(chips):
                sems = (ssem.at[6 * w + k], rsem.at[6 * w + k])
                sib_sems = (ssem.at[6 * w + 3 + k], rsem.at[6 * w + 3 + k])
                first.append(_remote(mine, mine, *sems, (px, py, c)))
                if only_first:
                    continue
                got = outs[w].at[2 * px + py, pl.ds(c * half, half)]
                relay.append((_remote(got, got, *sems, (px, py, c)), _remote(got, got, *sib_sems, sib)))
                theirs = outs[w].at[2 * px + py, pl.ds((1 - c) * half, half)]
                last.append(_remote(theirs, theirs, *sib_sems, sib))
        return first, relay, last

    def start(ins, outs, ssem, rsem):
        for cp in copies(outs, ssem, rsem, only_first=True)[0]:
            cp.start()

    def finish(ins, outs, ssem, rsem):
        first, relay, last = copies(outs, ssem, rsem)
        for arrived, onward in relay:
            arrived.wait_recv()
            onward.start()
        for cp in last:
            cp.wait_recv()
        for cp in first:
            cp.wait_send()
        for _, onward in relay:
            onward.wait_send()

    return Exchange(bufs, [_sds(b.shape, b.dtype) for b in bufs], {w: w for w in range(n)}, 6 * n, start, finish)


def _simple_exchange(arrays, landing, aliases, make_copies, sibling_only=False):
    def start(ins, outs, ssem, rsem):
        for cp, _ in make_copies(ins, outs, ssem, rsem, False):
            cp.start()

    def finish(ins, outs, ssem, rsem):
        cps = make_copies(ins, outs, ssem, rsem, True)
        for _, landed in cps:
            landed.wait_recv()
        for cp, _ in cps:
            cp.wait_send()

    return Exchange(arrays, landing, aliases, len(arrays) * 3, start, finish, sibling_only)


def _ex_pair_swap(grads):
    def make(ins, outs, ssem, rsem, landing):
        x, y, c, _ = _place()
        cps = [_remote(ins[w].at[:, 1 - c], outs[w], ssem.at[w], rsem.at[w], (x, y, 1 - c))
               for w in range(len(grads))]
        return [(cp, cp) for cp in cps]

    return _simple_exchange(grads, [_sds((NSH,) + g.shape[2:], g.dtype) for g in grads], {}, make, True)


def _ex_relay(bufs):
    def make(ins, outs, ssem, rsem, landing):
        x, y, c, chips = _place()
        sib = (x, y, 1 - c)
        out = []
        for w in range(len(bufs)):
            half = outs[w].shape[1] // 2
            for k, (px, py) in enumerate(chips):
                sems = (ssem.at[3 * w + k], rsem.at[3 * w + k])
                have = outs[w].at[2 * px + py, pl.ds(c * half, half)]
                miss = outs[w].at[2 * px + py, pl.ds((1 - c) * half, half)]
                out.append((_remote(have, have, *sems, sib), _remote(miss, miss, *sems, sib) if landing else None))
        return out

    return _simple_exchange(bufs, [_sds(b.shape, b.dtype) for b in bufs], {w: w for w in range(len(bufs))}, make, True)


def _ex_share(bufs):
    def make(ins, outs, ssem, rsem, landing):
        x, y, c, _ = _place()
        sib = (x, y, 1 - c)
        return [(_remote(outs[w].at[c], outs[w].at[c], ssem.at[w], rsem.at[w], sib),
                 _remote(outs[w].at[1 - c], outs[w].at[1 - c], ssem.at[w], rsem.at[w], sib) if landing else None)
                for w in range(len(bufs))]

    return _simple_exchange(bufs, [_sds(b.shape, b.dtype) for b in bufs], {w: w for w in range(len(bufs))}, make, True)


def _small_copies(slots, ssems, rsems, sending):
    x, y, c, _ = _place()
    out = []
    for m in range(1, 8):
        px, py, pc = x ^ (m >> 2), y ^ ((m >> 1) & 1), c ^ (m & 1)
        slot = slots.at[4 * x + 2 * y + c if sending else 4 * px + 2 * py + pc]
        out.append(_remote(slot, slot, ssems[m - 1], rsems[m - 1], (px, py, pc)))
    return out


def _small_gather_start(slots, name, after=()):
    at = 1 + len(after)

    def body(*refs):
        for cp in _small_copies(refs[0], refs[at:at + 7], refs[at + 7:at + 14], True):
            cp.start()
        refs[-1][...] = jnp.zeros_like(refs[-1])

    outs = pl.pallas_call(
        body, name=name,
        out_shape=([pltpu.SemaphoreType.DMA(())] * 14 + [pltpu.HBM(slots.shape, slots.dtype)]
                   + [jax.ShapeDtypeStruct((8, 128), F32)]),
        in_specs=[_HBM] + [_ANY] * len(after), out_specs=[_SEM] * 14 + [_HBM, _VM], input_output_aliases={0: 14},
        compiler_params=pltpu.CompilerParams(has_side_effects=_EFFECT),
    )(*_in_hbm([slots]), *after)
    return outs[:14], outs[14], outs[15]


def _small_gather_wait(sems, slots, after, name):
    def body(*refs):
        for cp in _small_copies(refs[0], refs[1:8], refs[8:15], True):
            cp.wait_send()
        for cp in _small_copies(refs[0], refs[1:8], refs[8:15], False):
            cp.wait_recv()

    return pl.pallas_call(
        body, name=name, out_shape=pltpu.HBM(slots.shape, slots.dtype),
        in_specs=[_HBM] + [_SEM] * 14 + [_ANY] * len(after), out_specs=_HBM, input_output_aliases={0: 0},
        compiler_params=pltpu.CompilerParams(has_side_effects=_EFFECT),
    )(slots, *sems, *after)


def _pair_sum(grads, gots, c_idx, name):
    n = len(grads)

    def body(c_ref, *refs):
        for a_ref, b_ref, o_ref in zip(refs[:n], refs[n:2 * n], refs[2 * n:]):
            o_ref[...] = (a_ref[...].astype(F32) + b_ref[...].astype(F32)).astype(BF16)

    halves = [g.shape[2:] for g in grads]
    return list(pl.pallas_call(
        body, name=name, out_shape=[_sds((NSH,) + h, BF16) for h in halves],
        grid_spec=pltpu.PrefetchScalarGridSpec(
            num_scalar_prefetch=1, grid=(NSH,),
            in_specs=[pl.BlockSpec((None, None) + h, lambda j, c: (j, c[0], 0, 0)) for h in halves]
            + [pl.BlockSpec((None,) + h, lambda j, c: (j, 0, 0)) for h in halves],
            out_specs=[pl.BlockSpec((None,) + h, lambda j, c: (j, 0, 0)) for h in halves]),
        compiler_params=_params(("arbitrary",), 40),
    )(c_idx, *_in_hbm(list(grads) + list(gots))))


def _chip_sum(owns, gots, place, name):
    n = len(owns)

    def body(place_ref, *refs):
        for own_ref, got_ref, o_ref in zip(refs[:n], refs[n:2 * n], refs[2 * n:]):
            acc = own_ref[...].astype(F32)
            for k in range(3):
                acc = acc + got_ref[k].astype(F32)
            o_ref[...] = acc

    shapes = [(o.shape[1] // 2, o.shape[2]) for o in owns]
    return list(pl.pallas_call(
        body, name=name, out_shape=[_sds((2, 2 * r, c), F32) for r, c in shapes],
        grid_spec=pltpu.PrefetchScalarGridSpec(
            num_scalar_prefetch=1, grid=(2,),
            in_specs=[pl.BlockSpec((None, r, c), lambda s, p: (p[0], s, 0)) for r, c in shapes]
            + [pl.BlockSpec((3, r, c), lambda s, p: (0, s, 0)) for r, c in shapes],
            out_specs=[pl.BlockSpec((None, r, c), lambda s, p: (p[1], s, 0)) for r, c in shapes]),
        compiler_params=_params(("arbitrary",), 40),
    )(place, *_in_hbm(list(owns) + list(gots))))


def _adamw_math(w, g, m, v):
    m = B1 * m + (1.0 - B1) * g
    v = B2 * v + (1.0 - B2) * (g * g)
    m_hat = m / (1.0 - B1 ** STEP)
    v_hat = v / (1.0 - B2 ** STEP)
    return -LR * (m_hat / (jnp.sqrt(v_hat) + AEPS) + WD * w), m, v


def _adamw(ws, gs, ms, vs, name, after=()):
    n, steps = len(ws), 4

    def body(*refs):
        ins, outs = refs[:4 * n], refs[4 * n:]
        for i in range(n):
            w_ref, g_ref, m_ref, v_ref = ins[4 * i:4 * i + 4]
            go_ref, d_ref, nm_ref, nv_ref = outs[4 * i:4 * i + 4]
            g = g_ref[...]
            go_ref[...] = g
            d_ref[...], nm_ref[...], nv_ref[...] = _adamw_math(w_ref[...], g, m_ref[...], v_ref[...])

    args, specs, shapes, free = [], [], [], []
    for i, (w, g, m, v) in enumerate(zip(ws, gs, ms, vs)):
        args += [w, g, m, v]
        specs += [pl.BlockSpec((w.shape[0] // steps, w.shape[1]), lambda r: (r, 0))] * 4
        shapes += [_sds(w.shape, F32)] * 4
        free += [4 * i, 4 * i + 2, 4 * i + 3]
    outs = _call(body, args, name=name, grid=(steps,), out_shape=shapes, in_specs=specs, out_specs=specs,
                 compiler_params=_params(("arbitrary",), 48), free=tuple(free), after=after)
    return [outs[4 * i:4 * i + 4] for i in range(n)]


def _small_update(gathered, w, m, v, entries):
    rows = w.shape[0]

    def body(ga_ref, w_ref, m_ref, v_ref, *out_refs):
        for j, (first, n) in enumerate(entries):
            mine = slice(first, first + n)
            g = ga_ref[mine, :]
            for dev in range(1, 8):
                g = g + ga_ref[dev * rows + first:dev * rows + first + n, :]
            results = (g,) + _adamw_math(w_ref[mine, :], g, m_ref[mine, :], v_ref[mine, :])
            for i, res in enumerate(results):
                out_refs[i * len(entries) + j][...] = res

    outs = pl.pallas_call(
        body, name="small_update",
        out_shape=[jax.ShapeDtypeStruct((n, 128), F32) for _ in range(4) for _, n in entries],
        in_specs=[_VM] * 4, out_specs=[_VM] * (4 * len(entries)),
    )(gathered, w, m, v)
    return [outs[i * len(entries):(i + 1) * len(entries)] for i in range(4)]


SMALL = ("ffn1_norm", "mix_norm", "ffn2_norm", "final_norm", "pool_scale", "loss", "pool_w_group")
BIG = ("ffn1_w_gate_up", "ffn1_w_down", "w_in", "w_branch_pool", "w_branch_attn", "w_out",
       "ffn2_w_gate_up", "ffn2_w_down")
ORDER = ("ffn1_norm", "ffn1_w_gate_up", "ffn1_w_down", "mix_norm", "w_in", "pool_w_group", "pool_scale",
         "w_branch_pool", "w_branch_attn", "w_out", "ffn2_norm", "ffn2_w_gate_up", "ffn2_w_down", "final_norm")
SMALL_ROWS = 560


def _pack_small(t):
    parts = []
    for k in SMALL:
        rows = t[k].reshape(-1, 128) if k in t else jnp.zeros((1, 128), F32)
        parts.append(jnp.pad(rows, ((0, -rows.shape[0] % 8), (0, 0))))
    packed = jnp.concatenate(parts, axis=0)
    assert packed.shape == (SMALL_ROWS, 128), packed.shape
    return packed


def _small_entries(like):
    out, at = [], 0
    for k in SMALL:
        n = like[k].size // 128 if k in like else 1
        out.append((at, n))
        at += n + (-n % 8)
    return out


def _halves(g):
    return g.reshape(NSH, 2, g.shape[1] // 2, g.shape[2])


def kernel(x, ffn1_norm, ffn1_w_gate_up, ffn1_w_down, mix_norm, w_in, pool_w_group, pool_scale, w_branch_pool, w_branch_attn, w_out, ffn2_norm, ffn2_w_gate_up, ffn2_w_down, final_norm, loss_target, m_ffn1_norm, m_ffn1_w_gate_up, m_ffn1_w_down, m_mix_norm, m_w_in, m_pool_w_group, m_pool_scale, m_w_branch_pool, m_w_branch_attn, m_w_out, m_ffn2_norm, m_ffn2_w_gate_up, m_ffn2_w_down, m_final_norm, v_ffn1_norm, v_ffn1_w_gate_up, v_ffn1_w_down, v_mix_norm, v_w_in, v_pool_w_group, v_pool_scale, v_w_branch_pool, v_w_branch_attn, v_w_out, v_ffn2_norm, v_ffn2_w_gate_up, v_ffn2_w_down, v_final_norm):
    wts = dict(ffn1_norm=ffn1_norm, ffn1_w_gate_up=ffn1_w_gate_up, ffn1_w_down=ffn1_w_down, mix_norm=mix_norm,
               w_in=w_in, pool_w_group=pool_w_group, pool_scale=pool_scale, w_branch_pool=w_branch_pool,
               w_branch_attn=w_branch_attn, w_out=w_out, ffn2_norm=ffn2_norm, ffn2_w_gate_up=ffn2_w_gate_up,
               ffn2_w_down=ffn2_w_down, final_norm=final_norm)
    mom = dict(ffn1_norm=m_ffn1_norm, ffn1_w_gate_up=m_ffn1_w_gate_up, ffn1_w_down=m_ffn1_w_down,
               mix_norm=m_mix_norm, w_in=m_w_in, pool_w_group=m_pool_w_group, pool_scale=m_pool_scale,
               w_branch_pool=m_w_branch_pool, w_branch_attn=m_w_branch_attn, w_out=m_w_out,
               ffn2_norm=m_ffn2_norm, ffn2_w_gate_up=m_ffn2_w_gate_up, ffn2_w_down=m_ffn2_w_down,
               final_norm=m_final_norm)
    var = dict(ffn1_norm=v_ffn1_norm, ffn1_w_gate_up=v_ffn1_w_gate_up, ffn1_w_down=v_ffn1_w_down,
               mix_norm=v_mix_norm, w_in=v_w_in, pool_w_group=v_pool_w_group, pool_scale=v_pool_scale,
               w_branch_pool=v_w_branch_pool, w_branch_attn=v_w_branch_attn, w_out=v_w_out,
               ffn2_norm=v_ffn2_norm, ffn2_w_gate_up=v_ffn2_w_gate_up, ffn2_w_down=v_ffn2_w_down,
               final_norm=v_final_norm)

    c_idx = lax.axis_index("c").astype(jnp.int32).reshape(1)
    me_idx = (2 * lax.axis_index("x") + lax.axis_index("y")).astype(jnp.int32).reshape(1)
    place = jnp.concatenate([me_idx, c_idx])
    x0, tgt = x[0], loss_target[0]
    wgrp = pool_w_group[0].astype(BF16)
    g1, gm, g2, gf = ffn1_norm, mix_norm, ffn2_norm, final_norm.reshape(1, D)
    grad, delta, new_m, new_v = {}, {}, {}, {}

    def pair_sums(keys, parts, got):
        return _pair_sum(parts, got, c_idx, "pair_sum_" + keys[0])

    def chip_sums(keys, chip_parts, owned):
        return _chip_sum(chip_parts, owned, place, "chip_sum_" + keys[0])

    def adamw(keys, after=()):
        outs = _adamw([wts[k][0] for k in keys], [grad[k][0] for k in keys], [mom[k][0] for k in keys],
                      [var[k][0] for k in keys], "adamw_" + keys[0], after=after)
        for k, res in zip(keys, outs):
            grad[k], delta[k], new_m[k], new_v[k] = (o.reshape(wts[k].shape) for o in res)

    first, late = ("ffn1_w_gate_up", "ffn1_w_down"), ("w_branch_pool", "w_branch_attn", "w_out",
                                                       "ffn2_w_gate_up", "ffn2_w_down")
    own = {}
    for group in (first, ("w_in",), late):
        own.update(zip(group, _cast_into_block([wts[k][0] for k in group], me_idx, "cast_" + group[0])))
    full = dict(zip(first, _exchange_alone(_ex_gather([own[k] for k in first]), "gather_ffn1")))
    wgu1, wd1 = full["ffn1_w_gate_up"], full["ffn1_w_down"].reshape(DFF, D)
    (h1, n1, gu1, a1), (win,) = _ffn_fwd(x0, g1, wgu1, wd1, "ffn1_fwd", exchange=_ex_gather_direct([own["w_in"]]))
    sems_l, thru_l, token_l = _gather_start([own[k_] for k_ in late], [h1], "gather_late_start")
    u, xp, q, k, v, gp, gs = _mix_in(h1, gm, win, after=(token_l,))
    o_sb, ctot = _attn_fwd(q, k, v)
    arrived = _gather_wait(sems_l, thru_l, [o_sb], "gather_late_wait")
    wbp, wba, wout = _exchange_alone(_ex_relay(arrived[:3]), "relay_mix")
    wout = wout.reshape(D, D)
    (h2, pm, p, yp, ys, mm), (wgu2, wd2) = _mix_out(h1, xp, o_sb, gp, gs, wgrp, pool_scale, wbp, wba, wout,
                                                    exchange=_ex_relay(arrived[3:]))
    wd2 = wd2.reshape(DFF, D)
    dh2, dgu3, d_g2, loss_row, d_gf, n3, a3, dh3 = _ffn_last(h2, g2, wgu2, wd2, tgt, gf, "ffn2")

    def grad_gate_up(n, dgu, name, exchange=None):
        res = _wgrad(n, dgu, NSH, D, name, exchange=exchange)
        return [_halves(res)] if exchange is None else ([_halves(res[0])], res[1])

    def grad_down(a, dh, name, exchange=None):
        res = _wgrad(a, dh, 1, FFS, name, exchange=exchange)
        halves = lambda g: [_halves(g.reshape(NSH, DFF // NSH, D))]
        return halves(res) if exchange is None else (halves(res[0]), res[1])

    k_gu2, k_d2, k_gu1, k_d1, k_in = (("ffn2_w_gate_up",), ("ffn2_w_down",), ("ffn1_w_gate_up",),
                                      ("ffn1_w_down",), ("w_in",))
    pa = grad_gate_up(n3, dgu3, "wgrad_gu2") + grad_down(a3, dh3, "wgrad_d2")
    (dlg, dyp, dys, do_sb, dyg, dxp, d_scale), got_a = _mix_bwd_out(
        dh2, gp, gs, yp, ys, pm, wgrp, pool_scale, wbp, wba, wout, exchange=_ex_pair_swap(pa))
    chip_a = pair_sums(k_gu2 + k_d2, pa, got_a)
    kb = ("w_out", "w_branch_pool", "w_branch_attn")
    g_bp, g_ba, d_group = _wgrad_branches(p, dyp, o_sb, dys, pm, dyg)
    pb = [_halves(_wgrad(mm, dh2, 1, D, "wgrad_out").reshape(NSH, D // NSH, D)), _halves(g_bp), _halves(g_ba)]
    k_a, k_in = k_gu2 + k_d2, k_in + kb
    sems_a, thru_a, token_a = _scatter_start(chip_a, "scatter_a_start")
    dq, dk, dv = _attn_bwd(q, k, v, do_sb, ctot, after=(token_a,))
    chip_a, owned_a = _scatter_wait(sems_a, thru_a, [dq], "scatter_a_wait")
    halves_a = chip_sums(k_a, chip_a, owned_a)
    dproj = (dxp, dq, dk, dv, dlg)
    (dh1, d_gm), both_a = _mix_bwd_in(dh2, h1, gm, dproj, win, exchange=_ex_share(halves_a))
    for i, k_ in enumerate(k_a):
        grad[k_] = both_a[i].reshape(wts[k_].shape)

    p_in = [_halves(_wgrad_in(u, dproj))] + pb
    p_d1, got_in = grad_down(a1, dh1, "wgrad_d1", exchange=_ex_pair_swap(p_in))
    sems_in, thru_in, token_in = _scatter_start(pair_sums(k_in, p_in, got_in), "scatter_in_start")
    dgu1, got_d1 = _ffn_bwd_act(dh1, gu1, wd1, "ffn1_bwd_act", exchange=_ex_pair_swap(p_d1), after=(token_in,))
    sems_d1, thru_d1, token_d1 = _scatter_start(pair_sums(k_d1, p_d1, got_d1), "scatter_d1_start")
    p_gu1 = [_halves(_wgrad(n1, dgu1, NSH, D, "wgrad_gu1", after=(token_in, token_d1)))]
    sems_w, thru_w, token_w = _swap_start(p_gu1, "swap_gu1_start")
    chip_in, owned_in = _scatter_wait(sems_in, thru_in, [token_w], "scatter_in_wait")
    chip_d1, owned_d1 = _scatter_wait(sems_d1, thru_d1, [token_w], "scatter_d1_wait")
    halves_in = chip_sums(k_in, chip_in, owned_in)
    p_gu1, got_gu1 = _swap_wait(sems_w, thru_w, halves_in, "swap_gu1_wait")
    sems, thru, token = _scatter_start(pair_sums(k_gu1, p_gu1, got_gu1), "scatter_gu1_start")
    sems_h, thru_h, token_h = _share_start(halves_in, [token], "share_in_start")
    adamw(k_a, after=(token_h,))
    landed = _share_wait(sems_h, thru_h, [delta[k_a[0]]], "share_in_wait")
    for i, k_ in enumerate(k_in):
        grad[k_] = landed[i].reshape(wts[k_].shape)
    adamw(k_in)
    dx, d_g1 = _ffn_bwd_in(dh1, x0, g1, dgu1, wgu1, "ffn1_bwd_in", after=(token,))
    small_g = dict(ffn1_norm=d_g1, mix_norm=d_gm, ffn2_norm=d_g2, final_norm=d_gf, pool_scale=d_scale,
                   pool_w_group=d_group, loss=loss_row)
    dev = 4 * lax.axis_index("x") + 2 * lax.axis_index("y") + lax.axis_index("c")
    slots = lax.dynamic_update_slice(jnp.zeros((8, SMALL_ROWS, 128), F32), _pack_small(small_g)[None], (dev, 0, 0))
    chip_gu1, owned_gu1 = _scatter_wait(sems, thru, [dx] + [delta[k_] for k_ in k_a + k_in], "scatter_gu1_wait")
    halves_last = chip_sums(k_d1 + k_gu1, chip_d1 + chip_gu1, owned_d1 + owned_gu1)
    sems_l, thru_l, token_l = _share_start(halves_last, [], "share_last_start")
    sems_s, slots, token_s = _small_gather_start(slots, "small_gather_start", after=(token_l,))
    both = _share_wait(sems_l, thru_l, [token_s], "share_last_wait")
    grad["ffn1_w_down"] = both[0].reshape(ffn1_w_down.shape)
    grad["ffn1_w_gate_up"] = both[1].reshape(ffn1_w_gate_up.shape)
    adamw(k_d1 + k_gu1, after=(token_s,))
    gathered = _small_gather_wait(sems_s, slots, [delta[k_] for k_ in k_d1 + k_gu1], "small_gather_wait")
    gathered = gathered.reshape(8 * SMALL_ROWS, 128)
    results = _small_update(gathered, _pack_small(wts), _pack_small(mom), _pack_small(var), _small_entries(wts))
    for dst, entries in zip((grad, delta, new_m, new_v), results):
        for k_, rows in zip(SMALL, entries):
            if k_ in wts:
                dst[k_] = rows.reshape(wts[k_].shape)
            elif dst is grad:
                loss = rows[0, 0]
    return (loss, dx[None], *[grad[k_] for k_ in ORDER], *[delta[k_] for k_ in ORDER],
            *[new_m[k_] for k_ in ORDER], *[new_v[k_] for k_ in ORDER])
```

```python
import dataclasses
import functools

import jax
import jax.numpy as jnp
from jax import lax
from jax.experimental import pallas as pl
from jax.experimental.pallas import tpu as pltpu

F32 = jnp.float32
BF16 = jnp.bfloat16

S = 2048
D = 1024
DFF = 2816
FFS = 2 * DFF // 4
NSH = 4
PW = 512
PG = 128
POOL_WINDOWS = (2, 4, 8, 16)
HALO = 16
SBW = 512
DH = 64
EPS = 1e-6
SCALE = 0.125
LOG2E = 1.4426950408889634
TA = 256
QB = 2
MIB = 1024 * 1024

LR, B1, B2, AEPS, WD, STEP = 0.001, 0.9, 0.999, 1e-08, 0.01, 10

_VM = pl.BlockSpec(memory_space=pltpu.VMEM)
_ANY = pl.BlockSpec(memory_space=pl.ANY)
MESH = pl.DeviceIdType.MESH
SIBLING_PAIR_ID = 1


def _nn(a, b):
    return jnp.dot(a, b, preferred_element_type=F32)


def _nt(a, b):
    return lax.dot_general(a, b, (((1,), (1,)), ((), ())), preferred_element_type=F32)


def _tn(a, b):
    return lax.dot_general(a, b, (((0,), (0,)), ((), ())), preferred_element_type=F32)


def _params(sem, vmem_mib):
    return pltpu.CompilerParams(dimension_semantics=sem, vmem_limit_bytes=vmem_mib * MIB)


def _rows(tm, width):
    return pl.BlockSpec((tm, width), lambda i: (i, 0))


def _fixed(shape):
    return pl.BlockSpec(shape, lambda *_: (0,) * len(shape))


def _sds(shape, dtype):
    return pltpu.HBM(shape, dtype)


def _in_hbm(args):
    return [pltpu.with_memory_space_constraint(a, pltpu.HBM) for a in args]


def _stage(pairs):
    pieces = 4

    def copy_all(sems):
        copies = []
        for src, dst in pairs:
            step = src.shape[0] // pieces
            for p in range(pieces):
                part = pl.ds(p * step, step)
                if len(dst.shape) == len(src.shape):
                    piece = (src.at[part], dst.at[part])
                else:
                    piece = (src.at[p], dst.at[:, pl.ds(p * src.shape[2], src.shape[2])])
                copies.append(pltpu.make_async_copy(*piece, sems.at[len(copies)]))
        for c in copies:
            c.start()
        for c in copies:
            c.wait()

    @pl.when(pl.program_id(0) == 0)
    def _():
        pl.run_scoped(copy_all, pltpu.SemaphoreType.DMA((pieces * len(pairs),)))


def _vmem_like(*arrays):
    return [pltpu.VMEM(a.shape, a.dtype) for a in arrays]


def _vmem_wide(w):
    return pltpu.VMEM((w.shape[1], w.shape[0] * w.shape[2]), w.dtype)


FF_CHUNKS = ((0, 1536), (1536, DFF - 1536))


def _wide_columns(src, dst, c0, cn):
    width, out = src.shape[2], []
    for p in range(src.shape[0]):
        lo, hi = max(c0, p * width), min(c0 + cn, (p + 1) * width)
        if lo < hi:
            out.append((src.at[p, :, pl.ds(lo - p * width, hi - lo)], dst.at[:, pl.ds(lo, hi - lo)]))
    return out


def _staged(groups, compute):
    first = pl.program_id(0) == 0

    def with_copies(sems):
        copies = []
        for group in groups:
            base = sum(len(g) for g in copies)
            copies.append([pltpu.make_async_copy(s, d, sems.at[base + i]) for i, (s, d) in enumerate(group)])
        for group in copies:
            for c in group:
                c.start()

        def ready(k):
            for c in copies[k]:
                c.wait()

        compute(ready)

    @pl.when(first)
    def _():
        pl.run_scoped(with_copies, pltpu.SemaphoreType.DMA((sum(len(g) for g in groups),)))

    @pl.when(jnp.logical_not(first))
    def _():
        compute(lambda k: None)


class Exchange:
    def __init__(self, arrays, landing, aliases, n_sems, start, finish, sibling_only=False):
        self.arrays, self.landing, self.aliases, self.n_sems = list(arrays), list(landing), dict(aliases), n_sems
        self.start, self.finish = start, finish
        self.sibling_only = sibling_only

    def enter(self):
        if self.sibling_only:
            barrier = pltpu.get_barrier_semaphore()
            sibling = (lax.axis_index("x"), lax.axis_index("y"), 1 - lax.axis_index("c"))
            pl.semaphore_signal(barrier, inc=1, device_id=sibling, device_id_type=MESH)
            pl.semaphore_wait(barrier, 1)

    def params(self, compiler_params=None):
        kw = dict(collective_id=SIBLING_PAIR_ID) if self.sibling_only else {}
        if compiler_params is None:
            return pltpu.CompilerParams(**kw)
        return dataclasses.replace(compiler_params, **kw)


def _call(body, args, *, name, grid, in_specs, out_specs, out_shape, scratch_shapes=(), compiler_params=None,
          exchange=None, free=(), after=()):
    args = [a if i in free else pltpu.with_memory_space_constraint(a, pltpu.HBM) for i, a in enumerate(args)]
    if exchange is None:
        n_in = len(in_specs)

        def plain(*refs):
            body(*refs[:n_in], *refs[n_in + len(after):])

        return pl.pallas_call(plain, name=name, grid=grid, in_specs=list(in_specs) + [_ANY] * len(after),
                              out_specs=out_specs, out_shape=out_shape, scratch_shapes=list(scratch_shapes),
                              compiler_params=compiler_params)(*args, *after)
    ex = exchange
    n_in, n_out, n_scr = len(in_specs), len(out_specs), len(scratch_shapes)
    na, nl = len(ex.arrays), len(ex.landing)

    def hosted(*refs):
        at = [0]

        def take(n):
            at[0] += n
            return refs[at[0] - n:at[0]]

        k_in, _, e_in, k_out, e_out, k_scr = take(n_in), take(len(after)), take(na), take(n_out), take(nl), take(n_scr)
        ssem, rsem = take(2)
        ids = [pl.program_id(a) for a in range(len(grid))]
        first = functools.reduce(jnp.logical_and, [i == 0 for i in ids])
        last = functools.reduce(jnp.logical_and, [i == g - 1 for i, g in zip(ids, grid)])

        @pl.when(first)
        def _():
            ex.enter()
            ex.start(e_in, e_out, ssem, rsem)

        body(*k_in, *k_out, *k_scr)

        @pl.when(last)
        def _():
            ex.finish(e_in, e_out, ssem, rsem)

    outs = pl.pallas_call(
        hosted, name=name, grid=grid,
        in_specs=list(in_specs) + [_ANY] * (len(after) + na), out_specs=list(out_specs) + [_ANY] * nl,
        out_shape=list(out_shape) + ex.landing,
        scratch_shapes=list(scratch_shapes) + [pltpu.SemaphoreType.DMA((ex.n_sems,))] * 2,
        input_output_aliases={n_in + len(after) + i: n_out + j for i, j in ex.aliases.items()},
        compiler_params=ex.params(compiler_params),
    )(*args, *after, *_in_hbm(ex.arrays))
    return outs[:n_out], outs[n_out:]


def _exchange_alone(ex, name, after=()):
    na, nl = len(ex.arrays), len(ex.landing)

    def body(*refs):
        outs = refs[na + len(after):na + len(after) + nl]
        ex.enter()
        ex.start(refs[:na], outs, refs[-2], refs[-1])
        ex.finish(refs[:na], outs, refs[-2], refs[-1])

    return pl.pallas_call(
        body, name=name, in_specs=[_ANY] * (na + len(after)), out_specs=[_ANY] * nl,
        out_shape=ex.landing, scratch_shapes=[pltpu.SemaphoreType.DMA((ex.n_sems,))] * 2,
        input_output_aliases=ex.aliases, compiler_params=ex.params(),
    )(*_in_hbm(ex.arrays), *after)


_HBM = pl.BlockSpec(memory_space=pltpu.HBM)
_SEM = pl.BlockSpec(memory_space=pltpu.SEMAPHORE)
_EFFECT = pltpu.SideEffectType.DATAFLOW_SIDE_EFFECTING


def _scatter_copies(srcs, lands, ssems, rsems):
    x, y, c, chips = _place()
    return [_remote(srcs[w].at[2 * px + py], lands[w].at[k], ssems[3 * w + k], rsems[3 * w + k], (px, py, c))
            for w in range(len(srcs)) for k, (px, py) in enumerate(chips)]


def _scatter_start(parts, name):
    parts = list(parts)
    n, ncp = len(parts), 3 * len(parts)
    lands = [lax.empty((3,) + p.shape[1:], p.dtype) for p in parts]

    def body(*refs):
        srcs, land_refs = refs[:n], refs[n:2 * n]
        ssems, rsems = refs[2 * n:2 * n + ncp], refs[2 * n + ncp:2 * n + 2 * ncp]
        for cp in _scatter_copies(srcs, land_refs, ssems, rsems):
            cp.start()
        token = refs[-1]
        token[...] = jnp.zeros_like(token)

    outs = pl.pallas_call(
        body, name=name,
        out_shape=([pltpu.SemaphoreType.DMA(())] * (2 * ncp) + [pltpu.HBM(a.shape, a.dtype) for a in parts + lands]
                   + [jax.ShapeDtypeStruct((8, 128), F32)]),
        in_specs=[_HBM] * (2 * n), out_specs=[_SEM] * (2 * ncp) + [_HBM] * (2 * n) + [_VM],
        input_output_aliases={i: 2 * ncp + i for i in range(2 * n)},
        compiler_params=pltpu.CompilerParams(has_side_effects=_EFFECT),
    )(*_in_hbm(parts), *_in_hbm(lands))
    sems, thru, token = outs[:2 * ncp], outs[2 * ncp:2 * ncp + 2 * n], outs[-1]
    return sems, thru, token


def _scatter_wait(sems, thru, after, name):
    n = len(thru) // 2
    ncp = 3 * n

    def body(*refs):
        srcs, land_refs = refs[:n], refs[n:2 * n]
        ssems, rsems = refs[2 * n:2 * n + ncp], refs[2 * n + ncp:2 * n + 2 * ncp]
        for cp in _scatter_copies(srcs, land_refs, ssems, rsems):
            cp.wait_send()
            cp.wait_recv()

    outs = pl.pallas_call(
        body, name=name, out_shape=[pltpu.HBM(a.shape, a.dtype) for a in thru],
        in_specs=[_HBM] * (2 * n) + [_SEM] * (2 * ncp) + [_ANY] * len(after), out_specs=[_HBM] * (2 * n),
        input_output_aliases={i: i for i in range(2 * n)},
        compiler_params=pltpu.CompilerParams(has_side_effects=_EFFECT),
    )(*thru, *sems, *after)
    return outs[:n], outs[n:]


def _swap_copies(srcs, lands, ssems, rsems):
    x, y, c, _ = _place()
    return [_remote(srcs[w].at[:, 1 - c], lands[w], ssems[w], rsems[w], (x, y, 1 - c)) for w in range(len(srcs))]


def _swap_start(grads, name):
    grads = list(grads)
    n = len(grads)
    lands = [lax.empty((NSH,) + g.shape[2:], g.dtype) for g in grads]

    def body(*refs):
        barrier = pltpu.get_barrier_semaphore()
        sibling = (lax.axis_index("x"), lax.axis_index("y"), 1 - lax.axis_index("c"))
        pl.semaphore_signal(barrier, inc=1, device_id=sibling, device_id_type=MESH)
        pl.semaphore_wait(barrier, 1)
        for cp in _swap_copies(refs[:n], refs[n:2 * n], refs[2 * n:3 * n], refs[3 * n:4 * n]):
            cp.start()
        refs[-1][...] = jnp.zeros_like(refs[-1])

    outs = pl.pallas_call(
        body, name=name,
        out_shape=([pltpu.SemaphoreType.DMA(())] * (2 * n) + [pltpu.HBM(a.shape, a.dtype) for a in grads + lands]
                   + [jax.ShapeDtypeStruct((8, 128), F32)]),
        in_specs=[_HBM] * (2 * n), out_specs=[_SEM] * (2 * n) + [_HBM] * (2 * n) + [_VM],
        input_output_aliases={i: 2 * n + i for i in range(2 * n)},
        compiler_params=pltpu.CompilerParams(has_side_effects=_EFFECT, collective_id=SIBLING_PAIR_ID),
    )(*_in_hbm(grads), *_in_hbm(lands))
    return outs[:2 * n], outs[2 * n:4 * n], outs[-1]


def _swap_wait(sems, thru, after, name):
    n = len(thru) // 2

    def body(*refs):
        for cp in _swap_copies(refs[:n], refs[n:2 * n], refs[2 * n:3 * n], refs[3 * n:4 * n]):
            cp.wait_send()
            cp.wait_recv()

    outs = pl.pallas_call(
        body, name=name, out_shape=[pltpu.HBM(a.shape, a.dtype) for a in thru],
        in_specs=[_HBM] * (2 * n) + [_SEM] * (2 * n) + [_ANY] * len(after), out_specs=[_HBM] * (2 * n),
        input_output_aliases={i: i for i in range(2 * n)},
        compiler_params=pltpu.CompilerParams(has_side_effects=_EFFECT),
    )(*thru, *sems, *after)
    return outs[:n], outs[n:]


def _share_copies(bufs, ssems, rsems, sending):
    x, y, c, _ = _place()
    out = []
    for w, ref in enumerate(bufs):
        slot = ref.at[c if sending else 1 - c]
        out.append(_remote(slot, slot, ssems[w], rsems[w], (x, y, 1 - c)))
    return out


def _share_start(bufs, after, name):
    bufs = list(bufs)
    n = len(bufs)

    def body(*refs):
        barrier = pltpu.get_barrier_semaphore()
        sibling = (lax.axis_index("x"), lax.axis_index("y"), 1 - lax.axis_index("c"))
        pl.semaphore_signal(barrier, inc=1, device_id=sibling, device_id_type=MESH)
        pl.semaphore_wait(barrier, 1)
        at = n + len(after)
        for cp in _share_copies(refs[:n], refs[at:at + n], refs[at + n:at + 2 * n], True):
            cp.start()
        refs[-1][...] = jnp.zeros_like(refs[-1])

    outs = pl.pallas_call(
        body, name=name,
        out_shape=([pltpu.SemaphoreType.DMA(())] * (2 * n) + [pltpu.HBM(a.shape, a.dtype) for a in bufs]
                   + [jax.ShapeDtypeStruct((8, 128), F32)]),
        in_specs=[_HBM] * n + [_ANY] * len(after), out_specs=[_SEM] * (2 * n) + [_HBM] * n + [_VM],
        input_output_aliases={i: 2 * n + i for i in range(n)},
        compiler_params=pltpu.CompilerParams(has_side_effects=_EFFECT, collective_id=SIBLING_PAIR_ID),
    )(*_in_hbm(bufs), *after)
    return outs[:2 * n], outs[2 * n:3 * n], outs[-1]


def _share_wait(sems, thru, after, name):
    n = len(thru)

    def body(*refs):
        for cp in _share_copies(refs[:n], refs[n:2 * n], refs[2 * n:3 * n], True):
            cp.wait_send()
        for cp in _share_copies(refs[:n], refs[n:2 * n], refs[2 * n:3 * n], False):
            cp.wait_recv()

    return pl.pallas_call(
        body, name=name, out_shape=[pltpu.HBM(a.shape, a.dtype) for a in thru],
        in_specs=[_HBM] * n + [_SEM] * (2 * n) + [_ANY] * len(after), out_specs=[_HBM] * n,
        input_output_aliases={i: i for i in range(n)},
        compiler_params=pltpu.CompilerParams(has_side_effects=_EFFECT),
    )(*thru, *sems, *after)


def _gather_copies(bufs, ssems, rsems, sending):
    x, y, c, chips = _place()
    out = []
    for w, ref in enumerate(bufs):
        half = ref.shape[1] // 2
        for k, (px, py) in enumerate(chips):
            rows = ref.at[2 * x + y if sending else 2 * px + py, pl.ds(c * half, half)]
            out.append(_remote(rows, rows, ssems[3 * w + k], rsems[3 * w + k], (px, py, c)))
    return out


def _gather_start(bufs, after, name):
    n, ncp = len(bufs), 3 * len(bufs)

    def body(*refs):
        ssems, rsems = refs[n + len(after):n + len(after) + ncp], refs[n + len(after) + ncp:n + len(after) + 2 * ncp]
        for cp in _gather_copies(refs[:n], ssems, rsems, True):
            cp.start()
        token = refs[-1]
        token[...] = jnp.zeros_like(token)

    outs = pl.pallas_call(
        body, name=name,
        out_shape=([pltpu.SemaphoreType.DMA(())] * (2 * ncp) + [pltpu.HBM(a.shape, a.dtype) for a in bufs]
                   + [jax.ShapeDtypeStruct((8, 128), F32)]),
        in_specs=[_HBM] * n + [_ANY] * len(after), out_specs=[_SEM] * (2 * ncp) + [_HBM] * n + [_VM],
        input_output_aliases={i: 2 * ncp + i for i in range(n)},
        compiler_params=pltpu.CompilerParams(has_side_effects=_EFFECT),
    )(*_in_hbm(bufs), *after)
    return outs[:2 * ncp], outs[2 * ncp:2 * ncp + n], outs[-1]


def _gather_wait(sems, thru, after, name):
    n = len(thru)
    ncp = 3 * n

    def body(*refs):
        ssems, rsems = refs[n:n + ncp], refs[n + ncp:n + 2 * ncp]
        for cp in _gather_copies(refs[:n], ssems, rsems, True):
            cp.wait_send()
        for cp in _gather_copies(refs[:n], ssems, rsems, False):
            cp.wait_recv()

    return pl.pallas_call(
        body, name=name, out_shape=[pltpu.HBM(a.shape, a.dtype) for a in thru],
        in_specs=[_HBM] * n + [_SEM] * (2 * ncp) + [_ANY] * len(after), out_specs=[_HBM] * n,
        input_output_aliases={i: i for i in range(n)},
        compiler_params=pltpu.CompilerParams(has_side_effects=_EFFECT),
    )(*thru, *sems, *after)


def _rms(x):
    r = lax.rsqrt(jnp.mean(x * x, axis=-1, keepdims=True) + EPS)
    return r, x * r


def _rms_bwd(dn, xr, r, gain):
    dng = dn * gain
    dx = r * (dng - xr * jnp.mean(dng * xr, axis=-1, keepdims=True))
    return dx, jnp.sum(dn * xr, axis=0, keepdims=True)


def _ffn_weight_groups(wgu_hbm, wgu_ref, wd_hbm, wd_ref):
    groups = []
    for c0, cn in FF_CHUNKS:
        groups += [_wide_columns(wgu_hbm, wgu_ref, c0, cn), _wide_columns(wgu_hbm, wgu_ref, DFF + c0, cn),
                   [(wd_hbm.at[pl.ds(c0, cn)], wd_ref.at[pl.ds(c0, cn)])]]
    return groups


def _ffn_fwd(x, gain, wgu, wd, name, exchange=None):
    tm = 256

    def body(x_ref, g_ref, wgu_hbm, wd_hbm, h_ref, n_ref, gu_ref, a_ref, wgu_ref, wd_ref):
        def compute(ready):
            x = x_ref[...]
            _, xr = _rms(x)
            n = (xr * g_ref[...]).astype(BF16)
            n_ref[...] = n
            acc = jnp.zeros((tm, D), F32)
            for i, (c0, cn) in enumerate(FF_CHUNKS):
                ready(3 * i)
                g = _nn(n, wgu_ref[:, c0:c0 + cn])
                ready(3 * i + 1)
                u = _nn(n, wgu_ref[:, DFF + c0:DFF + c0 + cn])
                gu_ref[:, c0:c0 + cn] = g.astype(BF16)
                gu_ref[:, DFF + c0:DFF + c0 + cn] = u.astype(BF16)
                half_act = (0.5 * (g * jax.nn.sigmoid(g) * u)).astype(BF16)
                a_ref[:, c0:c0 + cn] = half_act
                ready(3 * i + 2)
                acc = acc + _nn(half_act, wd_ref[c0:c0 + cn, :])
            h_ref[...] = x + acc

        _staged(_ffn_weight_groups(wgu_hbm, wgu_ref, wd_hbm, wd_ref), compute)

    return _call(
        body, (x, gain, wgu, wd), name=name, grid=(S // tm,),
        in_specs=[_rows(tm, D), _fixed((1, D)), _ANY, _ANY],
        out_specs=[_rows(tm, D), _rows(tm, D), _rows(tm, 4 * FFS), _rows(tm, DFF)],
        out_shape=[_sds((S, D), F32), _sds((S, D), BF16), _sds((S, 4 * FFS), BF16), _sds((S, DFF), BF16)],
        scratch_shapes=[_vmem_wide(wgu)] + _vmem_like(wd),
        compiler_params=_params(("arbitrary",), 56), exchange=exchange)


def _ffn_last(x, gain, wgu, wd, target, gf, name):
    tm = 256

    def body(x_ref, g_ref, wgu_hbm, wd_hbm, t_ref, gf_ref, dx_ref, dgu_ref, dg_ref, loss_ref, dgf_ref, n_ref,
             a_ref, dh_ref, wgu_ref, wd_ref):
        @pl.when(pl.program_id(0) == 0)
        def _():
            dg_ref[...] = jnp.zeros_like(dg_ref)
            dgf_ref[...] = jnp.zeros_like(dgf_ref)
            loss_ref[...] = jnp.zeros_like(loss_ref)

        def compute():
            x = x_ref[...]
            r0, xr = _rms(x)
            n = (xr * g_ref[...]).astype(BF16)
            n_ref[...] = n
            acc = jnp.zeros((tm, D), F32)
            kept = []
            for c0, cn in FF_CHUNKS:
                g = _nn(n, wgu_ref[:, c0:c0 + cn])
                u = _nn(n, wgu_ref[:, DFF + c0:DFF + c0 + cn])
                kept.append((g.astype(BF16), u.astype(BF16)))
                half_act = (0.5 * (g * jax.nn.sigmoid(g) * u)).astype(BF16)
                a_ref[:, c0:c0 + cn] = half_act
                acc = acc + _nn(half_act, wd_ref[c0:c0 + cn, :])
            h = x + acc
            gf = gf_ref[...]
            r, hr = _rms(h)
            err = hr * gf - t_ref[...]
            dh, dgain_f = _rms_bwd(err * (1.0 / D), hr, r, gf)
            dh_ref[...] = dh
            dhb = dh.astype(BF16)
            dn = jnp.zeros((tm, D), F32)
            for (c0, cn), (gb, ub) in zip(FF_CHUNKS, kept):
                g, u = gb.astype(F32), ub.astype(F32)
                da = 0.5 * _nt(dhb, wd_ref[c0:c0 + cn, :])
                sg = jax.nn.sigmoid(g)
                dgb = (da * u * (sg * (1.0 + g * (1.0 - sg)))).astype(BF16)
                dub = (da * (g * sg)).astype(BF16)
                dgu_ref[:, c0:c0 + cn] = dgb
                dgu_ref[:, DFF + c0:DFF + c0 + cn] = dub
                dn = dn + _nt(dgb, wgu_ref[:, c0:c0 + cn]) + _nt(dub, wgu_ref[:, DFF + c0:DFF + c0 + cn])
            dx, dgain = _rms_bwd(dn, xr, r0, g_ref[...])
            dx_ref[...] = dh + dx
            dg_ref[...] += dgain
            dgf_ref[...] += dgain_f
            loss_ref[...] += jnp.full((1, 128), (0.5 / D) * jnp.sum(err * err), F32)

        _stage([(wgu_hbm, wgu_ref), (wd_hbm, wd_ref)])
        compute()

    return _call(
        body, (x, gain, wgu, wd, target, gf), name=name, grid=(S // tm,),
        in_specs=[_rows(tm, D), _fixed((1, D)), _ANY, _ANY, _rows(tm, D), _fixed((1, D))],
        out_specs=[_rows(tm, D), _rows(tm, 4 * FFS), _fixed((1, D)), _fixed((1, 128)), _fixed((1, D)),
                   _rows(tm, D), _rows(tm, DFF), _rows(tm, D)],
        out_shape=[_sds((S, D), F32), _sds((S, 4 * FFS), BF16), _sds((1, D), F32), _sds((1, 128), F32),
                   _sds((1, D), F32), _sds((S, D), BF16), _sds((S, DFF), BF16), _sds((S, D), F32)],
        scratch_shapes=[_vmem_wide(wgu)] + _vmem_like(wd),
        compiler_params=_params(("arbitrary",), 58), free=(4, 5))


def _ffn_bwd_act(dh, gu, wd, name, exchange=None, after=()):
    tm = 512

    def body(dh_ref, gu_ref, wd_hbm, dgu_ref, wd_ref):
        _stage([(wd_hbm, wd_ref)])
        dhb = dh_ref[...].astype(BF16)
        for c0, cn in FF_CHUNKS:
            g = gu_ref[:, c0:c0 + cn].astype(F32)
            u = gu_ref[:, DFF + c0:DFF + c0 + cn].astype(F32)
            da = 0.5 * _nt(dhb, wd_ref[c0:c0 + cn, :])
            sg = jax.nn.sigmoid(g)
            dgu_ref[:, c0:c0 + cn] = (da * u * (sg * (1.0 + g * (1.0 - sg)))).astype(BF16)
            dgu_ref[:, DFF + c0:DFF + c0 + cn] = (da * (g * sg)).astype(BF16)

    res = _call(
        body, (dh, gu, wd), name=name, grid=(S // tm,),
        in_specs=[_rows(tm, D), _rows(tm, 4 * FFS), _ANY], out_specs=[_rows(tm, 4 * FFS)],
        out_shape=[_sds((S, 4 * FFS), BF16)], scratch_shapes=_vmem_like(wd),
        compiler_params=_params(("arbitrary",), 56), exchange=exchange, after=after)
    return res[0] if exchange is None else (res[0][0], res[1])


def _ffn_bwd_in(dh, x, gain, dgu, wgu, name, exchange=None, after=()):
    tm = 512

    def body(dh_ref, x_ref, g_ref, dgu_ref, wgu_hbm, dx_ref, dg_ref, wgu_ref):
        chunks = [(half + c0, cn) for half in (0, DFF) for c0, cn in FF_CHUNKS]

        @pl.when(pl.program_id(0) == 0)
        def _():
            dg_ref[...] = jnp.zeros_like(dg_ref)

        def compute(ready):
            dn = jnp.zeros((tm, D), F32)
            for k, (c0, cn) in enumerate(chunks):
                ready(k)
                dn = dn + _nt(dgu_ref[:, c0:c0 + cn], wgu_ref[:, c0:c0 + cn])
            r, xr = _rms(x_ref[...])
            dx, dgain = _rms_bwd(dn, xr, r, g_ref[...])
            dx_ref[...] = dh_ref[...] + dx
            dg_ref[...] += dgain

        _staged([_wide_columns(wgu_hbm, wgu_ref, c0, cn) for c0, cn in chunks], compute)

    return _call(
        body, (dh, x, gain, dgu, wgu), name=name, grid=(S // tm,),
        in_specs=[_rows(tm, D), _rows(tm, D), _fixed((1, D)), _rows(tm, 4 * FFS), _ANY],
        out_specs=[_rows(tm, D), _fixed((1, D))],
        out_shape=[_sds((S, D), F32), _sds((1, D), F32)],
        scratch_shapes=[_vmem_wide(wgu)],
        compiler_params=_params(("arbitrary",), 56), exchange=exchange, after=after)


def _mix_in(h, gain, w_in, after=()):
    tm = 512

    def body(h_ref, g_ref, w_hbm, u_ref, xp_ref, q_ref, k_ref, v_ref, gp_ref, gs_ref, w_ref):
        def compute(ready):
            _, hr = _rms(h_ref[...])
            u = (hr * g_ref[...]).astype(BF16)
            u_ref[...] = u
            ready(0)
            p0 = _nn(u, w_ref[0])
            xp_ref[...] = p0[:, :PW]
            q_ref[...] = p0[:, PW:].astype(BF16)
            ready(1)
            p1 = _nn(u, w_ref[1])
            k_ref[...] = p1[:, :SBW].astype(BF16)
            v_ref[...] = p1[:, SBW:].astype(BF16)
            ready(2)
            gp_ref[...] = jax.nn.sigmoid(_nn(u, w_ref[2])).astype(BF16)
            ready(3)
            gs_ref[...] = jax.nn.sigmoid(_nn(u, w_ref[3])).astype(BF16)

        _staged([[(w_hbm.at[j], w_ref.at[j])] for j in range(NSH)], compute)

    return _call(
        body, (h, gain, w_in), name="mix_in", grid=(S // tm,),
        in_specs=[_rows(tm, D), _fixed((1, D)), _ANY],
        out_specs=[_rows(tm, D), _rows(tm, PW), _rows(tm, SBW), _rows(tm, SBW), _rows(tm, SBW),
                   _rows(tm, D), _rows(tm, D)],
        out_shape=[_sds((S, D), BF16), _sds((S, PW), F32), _sds((S, SBW), BF16), _sds((S, SBW), BF16),
                   _sds((S, SBW), BF16), _sds((S, D), BF16), _sds((S, D), BF16)],
        scratch_shapes=_vmem_like(w_in),
        compiler_params=_params(("arbitrary",), 48), free=(1,), after=after)


def _hilo_dot(x, tri):
    hi = x.astype(BF16)
    lo = (x - hi.astype(F32)).astype(BF16)
    return _nn(hi, tri) + _nn(lo, tri)


def _log_terms(qk):
    z2 = qk * (SCALE * LOG2E)
    lb = jnp.minimum(z2, 0.0) - jnp.log2(1.0 + jnp.exp2(-jnp.abs(z2)))
    return lb, lb - z2


def _head_masks():
    lane = lax.broadcasted_iota(jnp.int32, (1, 2 * DH), 1)
    return (lane < DH, lane >= DH)


def _attn_fwd(q, k, v, exchange=None):
    T = TA

    def body(q_ref, k_ref, v_ref, o_ref, c_ref):
        i2 = 2 * pl.program_id(1)
        row = lax.broadcasted_iota(jnp.int32, (T, T), 0)
        col = lax.broadcasted_iota(jnp.int32, (T, T), 1)
        after = (row > col).astype(BF16)
        causal = col < row
        masks = _head_masks()
        qms = {}
        for b in range(QB):
            q2 = q_ref[b * T:(b + 1) * T, :]
            for h, hm in enumerate(masks):
                qms[b, h] = jnp.where(hm, q2, jnp.zeros_like(q2))

        def blocks(keys, pairs, carries, os):
            ks, vms = [], []
            for j in keys:
                rows = pl.ds(pl.multiple_of(j * T, T), T)
                vj = v_ref[rows, :]
                ks.append(k_ref[rows, :])
                vms.append([jnp.where(hm, vj, jnp.zeros_like(vj)) for hm in masks])
            units = [(n, h) for n in range(len(pairs)) for h in range(2)]
            qks = {(n, h): _nt(qms[pairs[n][0], h], ks[pairs[n][1]]) for n, h in units}
            lbs, l1ms = {}, {}
            for u in units:
                lbs[u], l1m = _log_terms(qks[u])
                l1ms[u] = jnp.where(causal, l1m, 0.0) if pairs[u[0]][2] else l1m
            cins = {u: _hilo_dot(l1ms[u], after) for u in units}
            carries, os = dict(carries), list(os)
            for n, h in units:
                b, key, diag = pairs[n]
                a = jnp.exp2(lbs[n, h] + cins[n, h] + carries[b, h])
                if diag:
                    a = jnp.where(causal, a, 0.0)
                os[b] = os[b] + _nn(a.astype(BF16), vms[key][h])
                carries[b, h] = carries[b, h] + jnp.sum(l1ms[n, h], axis=1, keepdims=True)
            return carries, tuple(os)

        carries = {(b, h): jnp.zeros((T, 1), F32) for b in range(QB) for h in range(2)}
        os = tuple(jnp.zeros((T, 2 * DH), F32) for _ in range(QB))
        carries, os = blocks([i2 + 1, i2], [(1, 0, True), (0, 1, True), (1, 1, False)], carries, os)
        carries, os = lax.fori_loop(
            0, i2 // 2,
            lambda t, c: blocks([i2 - 1 - 2 * t, i2 - 2 - 2 * t],
                                [(0, 0, False), (1, 0, False), (0, 1, False), (1, 1, False)], c[0], c[1]),
            (carries, os))
        for b in range(QB):
            o_ref[b * T:(b + 1) * T, :] = os[b].astype(BF16)
            c_ref[b * T:(b + 1) * T, :] = jnp.where(masks[0], carries[b, 0], carries[b, 1])

    blk = pl.BlockSpec((QB * T, 2 * DH), lambda p, i: (i, p))
    full = pl.BlockSpec((S, 2 * DH), lambda p, i: (0, p))
    return _call(
        body, (q, k, v), name="attn_fwd", grid=(SBW // (2 * DH), S // (QB * T)),
        in_specs=[blk, full, full], out_specs=[blk, blk],
        out_shape=[_sds((S, SBW), BF16), _sds((S, SBW), F32)],
        compiler_params=_params(("arbitrary", "arbitrary"), 40), exchange=exchange)


def _attn_bwd(q, k, v, do, ctot, after=()):
    T = TA
    nq = S // (QB * T)

    def body(q_ref, k_ref, v_ref, do_ref, c_ref, dq_ref, dk_ref, dv_ref, dk_acc, dv_acc):
        step = pl.program_id(1)
        i2 = 2 * step

        @pl.when(step == 0)
        def _():
            dk_acc[...] = jnp.zeros_like(dk_acc)
            dv_acc[...] = jnp.zeros_like(dv_acc)

        row = lax.broadcasted_iota(jnp.int32, (T, T), 0)
        col = lax.broadcasted_iota(jnp.int32, (T, T), 1)
        upto = (row <= col).astype(BF16)
        before = (row < col).astype(BF16)
        causal = col < row
        masks = _head_masks()
        qms, doms, ctots = {}, {}, {}
        for b in range(QB):
            q2, do2 = q_ref[b * T:(b + 1) * T, :], do_ref[b * T:(b + 1) * T, :]
            for h, hm in enumerate(masks):
                qms[b, h] = jnp.where(hm, q2, jnp.zeros_like(q2))
                doms[b, h] = jnp.where(hm, do2, jnp.zeros_like(do2))
                ctots[b, h] = c_ref[b * T:(b + 1) * T, h * DH:h * DH + 1]

        def blocks(keys, pairs, sums, dqs):
            rows = [pl.ds(pl.multiple_of(j * T, T), T) for j in keys]
            ks, vs = [k_ref[r, :] for r in rows], [v_ref[r, :] for r in rows]
            kms = [[jnp.where(hm, kj, jnp.zeros_like(kj)) for hm in masks] for kj in ks]
            units = [(n, h) for n in range(len(pairs)) for h in range(2)]
            qks = {(n, h): _nt(qms[pairs[n][0], h], ks[pairs[n][1]]) for n, h in units}
            das = {(n, h): _nt(doms[pairs[n][0], h], vs[pairs[n][1]]) for n, h in units}
            lbs, l1ms = {}, {}
            for u in units:
                lbs[u], l1m = _log_terms(qks[u])
                l1ms[u] = jnp.where(causal, l1m, 0.0) if pairs[u[0]][2] else l1m
            pins = {u: _hilo_dot(l1ms[u], upto) for u in units}
            sums = dict(sums)
            a_s, dls, cps = {}, {}, {}
            for n, h in units:
                b, _, diag = pairs[n]
                cl, cp = sums[b, h]
                a = jnp.exp2(lbs[n, h] + (ctots[b, h] - cl) - pins[n, h])
                if diag:
                    a = jnp.where(causal, a, 0.0)
                a_s[n, h] = a.astype(BF16)
                dls[n, h] = das[n, h] * a
                cps[n, h] = cp
                sums[b, h] = (cl + jnp.sum(l1ms[n, h], axis=1, keepdims=True),
                              cp + jnp.sum(dls[n, h], axis=1, keepdims=True))
            pexs = {u: _hilo_dot(dls[u], before) for u in units}
            dzbs = {}
            for u in units:
                dz = dls[u] - jnp.exp2(lbs[u]) * (dls[u] + pexs[u] + cps[u])
                if pairs[u[0]][2]:
                    dz = jnp.where(causal, dz, 0.0)
                dzbs[u] = dz.astype(BF16)
            dqs = list(dqs)
            for n, h in units:
                dqs[pairs[n][0]] = dqs[pairs[n][0]] + _nn(dzbs[n, h], kms[pairs[n][1]][h])
            for key, r in enumerate(rows):
                mine = [(n, h) for n, h in units if pairs[n][1] == key]
                dk_acc[r, :] += functools.reduce(jnp.add, [_tn(dzbs[u], qms[pairs[u[0]][0], u[1]]) for u in mine])
                dv_acc[r, :] += functools.reduce(jnp.add, [_tn(a_s[u], doms[pairs[u[0]][0], u[1]]) for u in mine])
            return sums, tuple(dqs)

        zero = jnp.zeros((T, 1), F32)
        sums = {(b, h): (zero, zero) for b in range(QB) for h in range(2)}
        dqs = tuple(jnp.zeros((T, 2 * DH), F32) for _ in range(QB))
        sums, dqs = lax.fori_loop(
            0, i2 // 2,
            lambda t, c: blocks([2 * t, 2 * t + 1],
                                [(0, 0, False), (1, 0, False), (0, 1, False), (1, 1, False)], c[0], c[1]),
            (sums, dqs))
        _, dqs = blocks([i2, i2 + 1], [(0, 0, True), (1, 0, False), (1, 1, True)], sums, dqs)
        for b in range(QB):
            dq_ref[b * T:(b + 1) * T, :] = (dqs[b] * SCALE).astype(BF16)

        @pl.when(step == nq - 1)
        def _():
            dk_ref[...] = (dk_acc[...] * SCALE).astype(BF16)
            dv_ref[...] = dv_acc[...].astype(BF16)

    blk = pl.BlockSpec((QB * T, 2 * DH), lambda p, i: (i, p))
    full = pl.BlockSpec((S, 2 * DH), lambda p, i: (0, p))
    return _call(
        body, (q, k, v, do, ctot), name="attn_bwd", grid=(SBW // (2 * DH), nq),
        in_specs=[blk, full, full, blk, blk], out_specs=[blk, full, full],
        out_shape=[_sds((S, SBW), BF16), _sds((S, SBW), BF16), _sds((S, SBW), BF16)],
        scratch_shapes=[pltpu.VMEM((S, 2 * DH), F32), pltpu.VMEM((S, 2 * DH), F32)],
        compiler_params=_params(("arbitrary", "arbitrary"), 40), after=after)


def _pool_counts(first_row, tm):
    pos = first_row + lax.broadcasted_iota(jnp.int32, (tm, 1), 0)
    return [jnp.minimum(pos + 1, w).astype(F32) for w in POOL_WINDOWS]


def _mix_out(h, xp, o_sb, gp, gs, w_group, scale, w_bp, w_ba, w_out, exchange=None):
    tm = 512

    def body(h_ref, xp_ref, o_ref, gp_ref, gs_ref, wg_hbm, sc_ref, wbp_hbm, wba_hbm, wo_hbm,
             h2_ref, pm_ref, p_ref, yp_ref, ys_ref, m_ref, halo, wg_ref, wbp_ref, wba_ref, wo_ref):
        _stage([(wg_hbm, wg_ref), (wbp_hbm, wbp_ref), (wba_hbm, wba_ref), (wo_hbm, wo_ref)])
        i = pl.program_id(0)

        @pl.when(i == 0)
        def _():
            halo[...] = jnp.zeros_like(halo)

        xp = xp_ref[...]
        ext = jnp.concatenate([halo[...], xp], axis=0)
        halo[...] = xp[tm - HALO:, :]
        counts = _pool_counts(i * tm, tm)
        for gi in range(len(POOL_WINDOWS)):
            lanes = slice(gi * PG, (gi + 1) * PG)
            win = ext[:, lanes]
            for step in range(gi + 1):
                win = win + pltpu.roll(win, 1 << step, 0)
            pm = (win[HALO:, :] / counts[gi] - xp[:, lanes]).astype(BF16)
            pm_ref[:, lanes] = pm
            p_ref[:, lanes] = (_nn(pm, wg_ref[gi]) * sc_ref[:, lanes]).astype(BF16)
        pb = p_ref[...]
        ob = o_ref[...]
        for j in range(NSH):
            cols = slice(j * (D // NSH), (j + 1) * (D // NSH))
            yp = _nn(pb, wbp_ref[j])
            ys = _nn(ob, wba_ref[j])
            yp_ref[:, cols] = yp.astype(BF16)
            ys_ref[:, cols] = ys.astype(BF16)
            m_ref[:, cols] = (gp_ref[:, cols].astype(F32) * yp + gs_ref[:, cols].astype(F32) * ys).astype(BF16)
        h2_ref[...] = h_ref[...] + _nn(m_ref[...], wo_ref[...])

    return _call(
        body, (h, xp, o_sb, gp, gs, w_group, scale, w_bp, w_ba, w_out), name="mix_out", grid=(S // tm,),
        in_specs=[_rows(tm, D), _rows(tm, PW), _rows(tm, SBW), _rows(tm, D), _rows(tm, D),
                  _ANY, _fixed((1, PW)), _ANY, _ANY, _ANY],
        out_specs=[_rows(tm, D), _rows(tm, PW), _rows(tm, PW), _rows(tm, D), _rows(tm, D), _rows(tm, D)],
        out_shape=[_sds((S, D), F32), _sds((S, PW), BF16), _sds((S, PW), BF16), _sds((S, D), BF16),
                   _sds((S, D), BF16), _sds((S, D), BF16)],
        scratch_shapes=[pltpu.VMEM((HALO, PW), F32)] + _vmem_like(w_group, w_bp, w_ba, w_out),
        compiler_params=_params(("arbitrary",), 48), free=(5, 6), exchange=exchange)


def _mix_bwd_out(dh, gp, gs, yp, ys, pm, w_group, scale, w_bp, w_ba, w_out, exchange=None):
    tm = 512
    nt = S // tm

    def body(dh_ref, gp_ref, gs_ref, yp_ref, ys_ref, pm_ref, wg_hbm, sc_ref, wbp_hbm, wba_hbm, wo_hbm,
             dlg_ref, dyp_ref, dys_ref, do_ref, dyg_ref, dxp_ref, dsc_ref, halo, wg_ref, wbp_ref, wba_ref, wo_ref):
        _stage([(wg_hbm, wg_ref), (wbp_hbm, wbp_ref), (wba_hbm, wba_ref), (wo_hbm, wo_ref)])
        step = pl.program_id(0)

        @pl.when(step == 0)
        def _():
            halo[...] = jnp.zeros_like(halo)
            dsc_ref[...] = jnp.zeros_like(dsc_ref)

        dm = _nt(dh_ref[...].astype(BF16), wo_ref[...])
        gp = gp_ref[...].astype(F32)
        gs = gs_ref[...].astype(F32)
        yp = yp_ref[...].astype(F32)
        ys = ys_ref[...].astype(F32)
        dlg_ref[:, :D] = (dm * yp * gp * (1.0 - gp)).astype(BF16)
        dlg_ref[:, D:] = (dm * ys * gs * (1.0 - gs)).astype(BF16)
        dyp_ref[...] = (dm * gp).astype(BF16)
        dys_ref[...] = (dm * gs).astype(BF16)
        dp = jnp.zeros((tm, PW), F32)
        do = jnp.zeros((tm, SBW), F32)
        for j in range(NSH):
            cols = slice(j * (D // NSH), (j + 1) * (D // NSH))
            dp = dp + _nt(dyp_ref[:, cols], wbp_ref[j])
            do = do + _nt(dys_ref[:, cols], wba_ref[j])
        do_ref[...] = do.astype(BF16)
        counts = _pool_counts((nt - 1 - step) * tm, tm)
        dscale = []
        for gi in range(len(POOL_WINDOWS)):
            lanes = slice(gi * PG, (gi + 1) * PG)
            dpg = dp[:, lanes]
            dscale.append(jnp.sum(dpg * _nn(pm_ref[:, lanes], wg_ref[gi]), axis=0, keepdims=True))
            dyg = (dpg * sc_ref[:, lanes]).astype(BF16)
            dyg_ref[:, lanes] = dyg
            dpm = _nt(dyg, wg_ref[gi])
            per = dpm / counts[gi]
            win = jnp.concatenate([per, halo[:, lanes]], axis=0)
            halo[:, lanes] = per[:HALO, :]
            for s in range(gi + 1):
                win = win + pltpu.roll(win, tm + HALO - (1 << s), 0)
            dxp_ref[:, lanes] = (win[:tm, :] - dpm).astype(BF16)
        dsc_ref[...] += jnp.concatenate(dscale, axis=1)

    rev = lambda width: pl.BlockSpec((tm, width), lambda i: (nt - 1 - i, 0))
    return _call(
        body, (dh, gp, gs, yp, ys, pm, w_group, scale, w_bp, w_ba, w_out), name="mix_bwd_out", grid=(nt,),
        in_specs=[rev(D), rev(D), rev(D), rev(D), rev(D), rev(PW), _ANY, _fixed((1, PW)), _ANY, _ANY, _ANY],
        out_specs=[rev(2 * D), rev(D), rev(D), rev(SBW), rev(PW), rev(PW), _fixed((1, PW))],
        out_shape=[_sds((S, 2 * D), BF16), _sds((S, D), BF16), _sds((S, D), BF16), _sds((S, SBW), BF16),
                   _sds((S, PW), BF16), _sds((S, PW), BF16), _sds((1, PW), F32)],
        scratch_shapes=[pltpu.VMEM((HALO, PW), F32)] + _vmem_like(w_group, w_bp, w_ba, w_out),
        compiler_params=_params(("arbitrary",), 48), exchange=exchange)


def _mix_bwd_in(dh, h, gain, pieces, w_in, exchange=None):
    tm = 512
    widths = [p.shape[1] for p in pieces]

    def body(dh_ref, h_ref, g_ref, *rest):
        piece_refs, (w_hbm, dx_ref, dg_ref, w_ref, dp_ref) = rest[:len(pieces)], rest[len(pieces):]
        @pl.when(pl.program_id(0) == 0)
        def _():
            dg_ref[...] = jnp.zeros_like(dg_ref)

        def compute(ready):
            at = 0
            for ref, width in zip(piece_refs, widths):
                dp_ref[:, at:at + width] = ref[...]
                at += width
            du = jnp.zeros((tm, D), F32)
            for j in range(NSH):
                ready(j)
                du = du + _nt(dp_ref[:, j * D:(j + 1) * D], w_ref[j])
            r, hr = _rms(h_ref[...])
            dx, dgain = _rms_bwd(du, hr, r, g_ref[...])
            dx_ref[...] = dh_ref[...] + dx
            dg_ref[...] += dgain

        _staged([[(w_hbm.at[j], w_ref.at[j])] for j in range(NSH)], compute)

    return _call(
        body, (dh, h, gain, *pieces, w_in), name="mix_bwd_in", grid=(S // tm,),
        in_specs=[_rows(tm, D), _rows(tm, D), _fixed((1, D))] + [_rows(tm, w) for w in widths] + [_ANY],
        out_specs=[_rows(tm, D), _fixed((1, D))],
        out_shape=[_sds((S, D), F32), _sds((1, D), F32)],
        scratch_shapes=_vmem_like(w_in) + [pltpu.VMEM((tm, 4 * D), BF16)],
        compiler_params=_params(("arbitrary",), 48), exchange=exchange)


def _wgrad_in(u, pieces):
    dxp, dq, dk, dv, dlg = pieces

    def body(u_ref, dxp_ref, dq_ref, dk_ref, dv_ref, dlg_ref, o_ref):
        j = pl.program_id(0)
        u = u_ref[...]

        def two(left_ref, right_ref):
            o_ref[:, :PW] = _tn(u, left_ref[...]).astype(BF16)
            o_ref[:, PW:] = _tn(u, right_ref[...]).astype(BF16)

        pl.when(j == 0)(lambda: two(dxp_ref, dq_ref))
        pl.when(j == 1)(lambda: two(dk_ref, dv_ref))

        @pl.when(j >= 2)
        def _():
            o_ref[...] = _tn(u, dlg_ref[...]).astype(BF16)

    whole = lambda width: pl.BlockSpec((S, width), lambda j: (0, 0))
    return _call(
        body, (u, dxp, dq, dk, dv, dlg), name="wgrad_in", grid=(NSH,),
        in_specs=[whole(D), whole(PW), whole(SBW), whole(SBW), whole(SBW),
                  pl.BlockSpec((S, D), lambda j: (0, jnp.maximum(j - 2, 0)))],
        out_specs=[pl.BlockSpec((None, D, D), lambda j: (j, 0, 0))], out_shape=[_sds((NSH, D, D), BF16)],
        compiler_params=_params(("arbitrary",), 56))[0]


def _wgrad(a, b, nblk, ti, name, out_dtype=BF16, exchange=None, after=()):
    ka, n = a.shape[1], b.shape[1]
    ns = n // nblk

    def body(a_ref, b_ref, o_ref):
        o_ref[...] = _tn(a_ref[...].astype(BF16), b_ref[...].astype(BF16)).astype(out_dtype)

    res = _call(
        body, (a, b), name=name, grid=(nblk, ka // ti),
        in_specs=[pl.BlockSpec((S, ti), lambda j, i: (0, i)), pl.BlockSpec((S, ns), lambda j, i: (0, j))],
        out_specs=[pl.BlockSpec((None, ti, ns), lambda j, i: (j, i, 0))],
        out_shape=[_sds((nblk, ka, ns), out_dtype)],
        compiler_params=_params(("arbitrary", "arbitrary"), 56), exchange=exchange, after=after)
    return res[0] if exchange is None else (res[0][0], res[1])


def _wgrad_branches(p, dyp, o_sb, dys, pm, dyg):
    cols = D // NSH

    def body(p_ref, dyp_ref, o_ref, dys_ref, pm_ref, dyg_ref, gbp_ref, gba_ref, gg_ref):
        gbp_ref[...] = _tn(p_ref[...], dyp_ref[...]).astype(BF16)
        gba_ref[...] = _tn(o_ref[...], dys_ref[...]).astype(BF16)
        gg_ref[...] = _tn(pm_ref[...], dyg_ref[...])

    whole = lambda width: pl.BlockSpec((S, width), lambda j: (0, 0))
    col = lambda width: pl.BlockSpec((S, width), lambda j: (0, j))
    return _call(
        body, (p, dyp, o_sb, dys, pm, dyg), name="wgrad_branches", grid=(NSH,),
        in_specs=[whole(PW), col(cols), whole(SBW), col(cols), col(PG), col(PG)],
        out_specs=[pl.BlockSpec((None, PW, cols), lambda j: (j, 0, 0)),
                   pl.BlockSpec((None, SBW, cols), lambda j: (j, 0, 0)),
                   pl.BlockSpec((None, PG, PG), lambda j: (j, 0, 0))],
        out_shape=[_sds((NSH, PW, cols), BF16), _sds((NSH, SBW, cols), BF16), _sds((NSH, PG, PG), F32)],
        compiler_params=_params(("arbitrary",), 40))


def _place():
    x, y, c = lax.axis_index("x"), lax.axis_index("y"), lax.axis_index("c")
    chips = [(1 - x, y), (x, 1 - y), (1 - x, 1 - y)]
    return x, y, c, chips


def _remote(src, dst, ssem, rsem, dev):
    return pltpu.make_async_remote_copy(src_ref=src, dst_ref=dst, send_sem=ssem, recv_sem=rsem,
                                        device_id=dev, device_id_type=MESH)


def _cast_into_block(ws, me_idx, name):
    steps = 4
    shapes = [(w.shape[0] // steps, w.shape[1]) for w in ws]

    def body(me_ref, *refs):
        for w_ref, o_ref in zip(refs[:len(ws)], refs[len(ws):]):
            o_ref[...] = w_ref[...].astype(BF16)

    return pl.pallas_call(
        body, name=name, out_shape=[_sds((NSH,) + w.shape, BF16) for w in ws],
        grid_spec=pltpu.PrefetchScalarGridSpec(
            num_scalar_prefetch=1, grid=(steps,),
            in_specs=[pl.BlockSpec((r, c), lambda s, me: (s, 0)) for r, c in shapes],
            out_specs=[pl.BlockSpec((None, r, c), lambda s, me: (me[0], s, 0)) for r, c in shapes]),
        compiler_params=_params(("arbitrary",), 32),
    )(me_idx, *ws)


def _ex_gather(bufs):
    n = len(bufs)
    per = 8

    def plan(outs, ssem, rsem, w):
        x, y, c, _ = _place()
        sib, nbr_x, nbr_y = (x, y, 1 - c), (1 - x, y, c), (x, 1 - y, c)
        half = outs[w].shape[1] // 2
        quarter = half // 2
        sem = lambda k: (ssem.at[per * w + k], rsem.at[per * w + k])
        rows = lambda blk, start, size: outs[w].at[blk, pl.ds(start, size)]
        mine = rows(2 * x + y, c * half, half)
        from_x = rows(2 * (1 - x) + y, c * half, half)
        from_y = rows(2 * x + (1 - y), c * half, half)
        diag = 2 * (1 - x) + (1 - y)
        pass_y = rows(2 * (1 - x) + y, c * half, quarter)
        pass_x = rows(2 * x + (1 - y), c * half + quarter, quarter)
        diag_0, diag_1 = rows(diag, c * half, quarter), rows(diag, c * half + quarter, quarter)
        first = [_remote(mine, mine, *sem(0), nbr_x), _remote(mine, mine, *sem(1), nbr_y)]
        arrivals = [
            (_remote(from_x, from_x, *sem(0), nbr_x),
             [_remote(pass_y, pass_y, *sem(2), nbr_y), _remote(from_x, from_x, *sem(4), sib)]),
            (_remote(from_y, from_y, *sem(1), nbr_y),
             [_remote(pass_x, pass_x, *sem(3), nbr_x), _remote(from_y, from_y, *sem(5), sib)]),
            (_remote(diag_0, diag_0, *sem(2), nbr_y), [_remote(diag_0, diag_0, *sem(6), sib)]),
            (_remote(diag_1, diag_1, *sem(3), nbr_x), [_remote(diag_1, diag_1, *sem(7), sib)]),
        ]
        other = (1 - c) * half
        from_sibling = [
            _remote(rows(2 * (1 - x) + y, other, half), rows(2 * (1 - x) + y, other, half), *sem(4), sib),
            _remote(rows(2 * x + (1 - y), other, half), rows(2 * x + (1 - y), other, half), *sem(5), sib),
            _remote(rows(diag, other, quarter), rows(diag, other, quarter), *sem(6), sib),
            _remote(rows(diag, other + quarter, quarter), rows(diag, other + quarter, quarter), *sem(7), sib),
        ]
        return first, arrivals, from_sibling

    def start(ins, outs, ssem, rsem):
        x, y, c, _ = _place()
        for w in range(n):
            half = outs[w].shape[1] // 2
            mine = outs[w].at[2 * x + y, pl.ds(c * half, half)]
            _remote(mine, mine, ssem.at[per * w], rsem.at[per * w], (1 - x, y, c)).start()
            _remote(mine, mine, ssem.at[per * w + 1], rsem.at[per * w + 1], (x, 1 - y, c)).start()

    def finish(ins, outs, ssem, rsem):
        plans = [plan(outs, ssem, rsem, w) for w in range(n)]
        started = []
        for direct in (True, False):
            for first, arrivals, _ in plans:
                for arrived, onward in (arrivals[:2] if direct else arrivals[2:]):
                    arrived.wait_recv()
                    for cp in onward:
                        cp.start()
                    started += onward
        for first, _, from_sibling in plans:
            for cp in from_sibling:
                cp.wait_recv()
            started += first
        for cp in started:
            cp.wait_send()

    return Exchange(bufs, [_sds(b.shape, b.dtype) for b in bufs], {w: w for w in range(n)}, per * n, start, finish)


def _ex_gather_direct(bufs):
    n = len(bufs)

    def copies(outs, ssem, rsem, only_first=False):
        x, y, c, chips = _place()
        me, sib = 2 * x + y, (x, y, 1 - c)
        first, relay, last = [], [], []
        for w in range(n):
            half = outs[w].shape[1] // 2
            mine = outs[w].at[me, pl.ds(c * half, half)]
            for k, (px, py) in enumerate(chips):
                sems = (ssem.at[6 * w + k], rsem.at[6 * w + k])
                sib_sems = (ssem.at[6 * w + 3 + k], rsem.at[6 * w + 3 + k])
                first.append(_remote(mine, mine, *sems, (px, py, c)))
                if only_first:
                    continue
                got = outs[w].at[2 * px + py, pl.ds(c * half, half)]
                relay.append((_remote(got, got, *sems, (px, py, c)), _remote(got, got, *sib_sems, sib)))
                theirs = outs[w].at[2 * px + py, pl.ds((1 - c) * half, half)]
                last.append(_remote(theirs, theirs, *sib_sems, sib))
        return first, relay, last

    def start(ins, outs, ssem, rsem):
        for cp in copies(outs, ssem, rsem, only_first=True)[0]:
            cp.start()

    def finish(ins, outs, ssem, rsem):
        first, relay, last = copies(outs, ssem, rsem)
        for arrived, onward in relay:
            arrived.wait_recv()
            onward.start()
        for cp in last:
            cp.wait_recv()
        for cp in first:
            cp.wait_send()
        for _, onward in relay:
            onward.wait_send()

    return Exchange(bufs, [_sds(b.shape, b.dtype) for b in bufs], {w: w for w in range(n)}, 6 * n, start, finish)


def _simple_exchange(arrays, landing, aliases, make_copies, sibling_only=False):
    def start(ins, outs, ssem, rsem):
        for cp, _ in make_copies(ins, outs, ssem, rsem, False):
            cp.start()

    def finish(ins, outs, ssem, rsem):
        cps = make_copies(ins, outs, ssem, rsem, True)
        for _, landed in cps:
            landed.wait_recv()
        for cp, _ in cps:
            cp.wait_send()

    return Exchange(arrays, landing, aliases, len(arrays) * 3, start, finish, sibling_only)


def _ex_pair_swap(grads):
    def make(ins, outs, ssem, rsem, landing):
        x, y, c, _ = _place()
        cps = [_remote(ins[w].at[:, 1 - c], outs[w], ssem.at[w], rsem.at[w], (x, y, 1 - c))
               for w in range(len(grads))]
        return [(cp, cp) for cp in cps]

    return _simple_exchange(grads, [_sds((NSH,) + g.shape[2:], g.dtype) for g in grads], {}, make, True)


def _ex_relay(bufs):
    def make(ins, outs, ssem, rsem, landing):
        x, y, c, chips = _place()
        sib = (x, y, 1 - c)
        out = []
        for w in range(len(bufs)):
            half = outs[w].shape[1] // 2
            for k, (px, py) in enumerate(chips):
                sems = (ssem.at[3 * w + k], rsem.at[3 * w + k])
                have = outs[w].at[2 * px + py, pl.ds(c * half, half)]
                miss = outs[w].at[2 * px + py, pl.ds((1 - c) * half, half)]
                out.append((_remote(have, have, *sems, sib), _remote(miss, miss, *sems, sib) if landing else None))
        return out

    return _simple_exchange(bufs, [_sds(b.shape, b.dtype) for b in bufs], {w: w for w in range(len(bufs))}, make, True)


def _ex_share(bufs):
    def make(ins, outs, ssem, rsem, landing):
        x, y, c, _ = _place()
        sib = (x, y, 1 - c)
        return [(_remote(outs[w].at[c], outs[w].at[c], ssem.at[w], rsem.at[w], sib),
                 _remote(outs[w].at[1 - c], outs[w].at[1 - c], ssem.at[w], rsem.at[w], sib) if landing else None)
                for w in range(len(bufs))]

    return _simple_exchange(bufs, [_sds(b.shape, b.dtype) for b in bufs], {w: w for w in range(len(bufs))}, make, True)


def _small_copies(slots, ssems, rsems, sending):
    x, y, c, _ = _place()
    out = []
    for m in range(1, 8):
        px, py, pc = x ^ (m >> 2), y ^ ((m >> 1) & 1), c ^ (m & 1)
        slot = slots.at[4 * x + 2 * y + c if sending else 4 * px + 2 * py + pc]
        out.append(_remote(slot, slot, ssems[m - 1], rsems[m - 1], (px, py, pc)))
    return out


def _small_gather_start(slots, name, after=()):
    at = 1 + len(after)

    def body(*refs):
        for cp in _small_copies(refs[0], refs[at:at + 7], refs[at + 7:at + 14], True):
            cp.start()
        refs[-1][...] = jnp.zeros_like(refs[-1])

    outs = pl.pallas_call(
        body, name=name,
        out_shape=([pltpu.SemaphoreType.DMA(())] * 14 + [pltpu.HBM(slots.shape, slots.dtype)]
                   + [jax.ShapeDtypeStruct((8, 128), F32)]),
        in_specs=[_HBM] + [_ANY] * len(after), out_specs=[_SEM] * 14 + [_HBM, _VM], input_output_aliases={0: 14},
        compiler_params=pltpu.CompilerParams(has_side_effects=_EFFECT),
    )(*_in_hbm([slots]), *after)
    return outs[:14], outs[14], outs[15]


def _small_gather_wait(sems, slots, after, name):
    def body(*refs):
        for cp in _small_copies(refs[0], refs[1:8], refs[8:15], True):
            cp.wait_send()
        for cp in _small_copies(refs[0], refs[1:8], refs[8:15], False):
            cp.wait_recv()

    return pl.pallas_call(
        body, name=name, out_shape=pltpu.HBM(slots.shape, slots.dtype),
        in_specs=[_HBM] + [_SEM] * 14 + [_ANY] * len(after), out_specs=_HBM, input_output_aliases={0: 0},
        compiler_params=pltpu.CompilerParams(has_side_effects=_EFFECT),
    )(slots, *sems, *after)


def _pair_sum(grads, gots, c_idx, name):
    n = len(grads)

    def body(c_ref, *refs):
        for a_ref, b_ref, o_ref in zip(refs[:n], refs[n:2 * n], refs[2 * n:]):
            o_ref[...] = (a_ref[...].astype(F32) + b_ref[...].astype(F32)).astype(BF16)

    halves = [g.shape[2:] for g in grads]
    return list(pl.pallas_call(
        body, name=name, out_shape=[_sds((NSH,) + h, BF16) for h in halves],
        grid_spec=pltpu.PrefetchScalarGridSpec(
            num_scalar_prefetch=1, grid=(NSH,),
            in_specs=[pl.BlockSpec((None, None) + h, lambda j, c: (j, c[0], 0, 0)) for h in halves]
            + [pl.BlockSpec((None,) + h, lambda j, c: (j, 0, 0)) for h in halves],
            out_specs=[pl.BlockSpec((None,) + h, lambda j, c: (j, 0, 0)) for h in halves]),
        compiler_params=_params(("arbitrary",), 40),
    )(c_idx, *_in_hbm(list(grads) + list(gots))))


def _chip_sum(owns, gots, place, name):
    n = len(owns)

    def body(place_ref, *refs):
        for own_ref, got_ref, o_ref in zip(refs[:n], refs[n:2 * n], refs[2 * n:]):
            acc = own_ref[...].astype(F32)
            for k in range(3):
                acc = acc + got_ref[k].astype(F32)
            o_ref[...] = acc

    shapes = [(o.shape[1] // 2, o.shape[2]) for o in owns]
    return list(pl.pallas_call(
        body, name=name, out_shape=[_sds((2, 2 * r, c), F32) for r, c in shapes],
        grid_spec=pltpu.PrefetchScalarGridSpec(
            num_scalar_prefetch=1, grid=(2,),
            in_specs=[pl.BlockSpec((None, r, c), lambda s, p: (p[0], s, 0)) for r, c in shapes]
            + [pl.BlockSpec((3, r, c), lambda s, p: (0, s, 0)) for r, c in shapes],
            out_specs=[pl.BlockSpec((None, r, c), lambda s, p: (p[1], s, 0)) for r, c in shapes]),
        compiler_params=_params(("arbitrary",), 40),
    )(place, *_in_hbm(list(owns) + list(gots))))


def _adamw_math(w, g, m, v):
    m = B1 * m + (1.0 - B1) * g
    v = B2 * v + (1.0 - B2) * (g * g)
    m_hat = m / (1.0 - B1 ** STEP)
    v_hat = v / (1.0 - B2 ** STEP)
    return -LR * (m_hat / (jnp.sqrt(v_hat) + AEPS) + WD * w), m, v


def _adamw(ws, gs, ms, vs, name, after=()):
    n, steps = len(ws), 4

    def body(*refs):
        ins, outs = refs[:4 * n], refs[4 * n:]
        for i in range(n):
            w_ref, g_ref, m_ref, v_ref = ins[4 * i:4 * i + 4]
            go_ref, d_ref, nm_ref, nv_ref = outs[4 * i:4 * i + 4]
            g = g_ref[...]
            go_ref[...] = g
            d_ref[...], nm_ref[...], nv_ref[...] = _adamw_math(w_ref[...], g, m_ref[...], v_ref[...])

    args, specs, shapes, free = [], [], [], []
    for i, (w, g, m, v) in enumerate(zip(ws, gs, ms, vs)):
        args += [w, g, m, v]
        specs += [pl.BlockSpec((w.shape[0] // steps, w.shape[1]), lambda r: (r, 0))] * 4
        shapes += [_sds(w.shape, F32)] * 4
        free += [4 * i, 4 * i + 2, 4 * i + 3]
    outs = _call(body, args, name=name, grid=(steps,), out_shape=shapes, in_specs=specs, out_specs=specs,
                 compiler_params=_params(("arbitrary",), 48), free=tuple(free), after=after)
    return [outs[4 * i:4 * i + 4] for i in range(n)]


def _small_update(gathered, w, m, v, entries):
    rows = w.shape[0]

    def body(ga_ref, w_ref, m_ref, v_ref, *out_refs):
        for j, (first, n) in enumerate(entries):
            mine = slice(first, first + n)
            g = ga_ref[mine, :]
            for dev in range(1, 8):
                g = g + ga_ref[dev * rows + first:dev * rows + first + n, :]
            results = (g,) + _adamw_math(w_ref[mine, :], g, m_ref[mine, :], v_ref[mine, :])
            for i, res in enumerate(results):
                out_refs[i * len(entries) + j][...] = res

    outs = pl.pallas_call(
        body, name="small_update",
        out_shape=[jax.ShapeDtypeStruct((n, 128), F32) for _ in range(4) for _, n in entries],
        in_specs=[_VM] * 4, out_specs=[_VM] * (4 * len(entries)),
    )(gathered, w, m, v)
    return [outs[i * len(entries):(i + 1) * len(entries)] for i in range(4)]


SMALL = ("ffn1_norm", "mix_norm", "ffn2_norm", "final_norm", "pool_scale", "loss", "pool_w_group")
BIG = ("ffn1_w_gate_up", "ffn1_w_down", "w_in", "w_branch_pool", "w_branch_attn", "w_out",
       "ffn2_w_gate_up", "ffn2_w_down")
ORDER = ("ffn1_norm", "ffn1_w_gate_up", "ffn1_w_down", "mix_norm", "w_in", "pool_w_group", "pool_scale",
         "w_branch_pool", "w_branch_attn", "w_out", "ffn2_norm", "ffn2_w_gate_up", "ffn2_w_down", "final_norm")
SMALL_ROWS = 560


def _pack_small(t):
    parts = []
    for k in SMALL:
        rows = t[k].reshape(-1, 128) if k in t else jnp.zeros((1, 128), F32)
        parts.append(jnp.pad(rows, ((0, -rows.shape[0] % 8), (0, 0))))
    packed = jnp.concatenate(parts, axis=0)
    assert packed.shape == (SMALL_ROWS, 128), packed.shape
    return packed


def _small_entries(like):
    out, at = [], 0
    for k in SMALL:
        n = like[k].size // 128 if k in like else 1
        out.append((at, n))
        at += n + (-n % 8)
    return out


def _halves(g):
    return g.reshape(NSH, 2, g.shape[1] // 2, g.shape[2])


def kernel(x, ffn1_norm, ffn1_w_gate_up, ffn1_w_down, mix_norm, w_in, pool_w_group, pool_scale, w_branch_pool, w_branch_attn, w_out, ffn2_norm, ffn2_w_gate_up, ffn2_w_down, final_norm, loss_target, m_ffn1_norm, m_ffn1_w_gate_up, m_ffn1_w_down, m_mix_norm, m_w_in, m_pool_w_group, m_pool_scale, m_w_branch_pool, m_w_branch_attn, m_w_out, m_ffn2_norm, m_ffn2_w_gate_up, m_ffn2_w_down, m_final_norm, v_ffn1_norm, v_ffn1_w_gate_up, v_ffn1_w_down, v_mix_norm, v_w_in, v_pool_w_group, v_pool_scale, v_w_branch_pool, v_w_branch_attn, v_w_out, v_ffn2_norm, v_ffn2_w_gate_up, v_ffn2_w_down, v_final_norm):
    wts = dict(ffn1_norm=ffn1_norm, ffn1_w_gate_up=ffn1_w_gate_up, ffn1_w_down=ffn1_w_down, mix_norm=mix_norm,
               w_in=w_in, pool_w_group=pool_w_group, pool_scale=pool_scale, w_branch_pool=w_branch_pool,
               w_branch_attn=w_branch_attn, w_out=w_out, ffn2_norm=ffn2_norm, ffn2_w_gate_up=ffn2_w_gate_up,
               ffn2_w_down=ffn2_w_down, final_norm=final_norm)
    mom = dict(ffn1_norm=m_ffn1_norm, ffn1_w_gate_up=m_ffn1_w_gate_up, ffn1_w_down=m_ffn1_w_down,
               mix_norm=m_mix_norm, w_in=m_w_in, pool_w_group=m_pool_w_group, pool_scale=m_pool_scale,
               w_branch_pool=m_w_branch_pool, w_branch_attn=m_w_branch_attn, w_out=m_w_out,
               ffn2_norm=m_ffn2_norm, ffn2_w_gate_up=m_ffn2_w_gate_up, ffn2_w_down=m_ffn2_w_down,
               final_norm=m_final_norm)
    var = dict(ffn1_norm=v_ffn1_norm, ffn1_w_gate_up=v_ffn1_w_gate_up, ffn1_w_down=v_ffn1_w_down,
               mix_norm=v_mix_norm, w_in=v_w_in, pool_w_group=v_pool_w_group, pool_scale=v_pool_scale,
               w_branch_pool=v_w_branch_pool, w_branch_attn=v_w_branch_attn, w_out=v_w_out,
               ffn2_norm=v_ffn2_norm, ffn2_w_gate_up=v_ffn2_w_gate_up, ffn2_w_down=v_ffn2_w_down,
               final_norm=v_final_norm)

    c_idx = lax.axis_index("c").astype(jnp.int32).reshape(1)
    me_idx = (2 * lax.axis_index("x") + lax.axis_index("y")).astype(jnp.int32).reshape(1)
    place = jnp.concatenate([me_idx, c_idx])
    x0, tgt = x[0], loss_target[0]
    wgrp = pool_w_group[0].astype(BF16)
    g1, gm, g2, gf = ffn1_norm, mix_norm, ffn2_norm, final_norm.reshape(1, D)
    grad, delta, new_m, new_v = {}, {}, {}, {}

    def pair_sums(keys, parts, got):
        return _pair_sum(parts, got, c_idx, "pair_sum_" + keys[0])

    def chip_sums(keys, chip_parts, owned):
        return _chip_sum(chip_parts, owned, place, "chip_sum_" + keys[0])

    def adamw(keys, after=()):
        outs = _adamw([wts[k][0] for k in keys], [grad[k][0] for k in keys], [mom[k][0] for k in keys],
                      [var[k][0] for k in keys], "adamw_" + keys[0], after=after)
        for k, res in zip(keys, outs):
            grad[k], delta[k], new_m[k], new_v[k] = (o.reshape(wts[k].shape) for o in res)

    first, late = ("ffn1_w_gate_up", "ffn1_w_down"), ("w_branch_pool", "w_branch_attn", "w_out",
                                                       "ffn2_w_gate_up", "ffn2_w_down")
    own = {}
    for group in (first, ("w_in",), late):
        own.update(zip(group, _cast_into_block([wts[k][0] for k in group], me_idx, "cast_" + group[0])))
    full = dict(zip(first, _exchange_alone(_ex_gather([own[k] for k in first]), "gather_ffn1")))
    wgu1, wd1 = full["ffn1_w_gate_up"], full["ffn1_w_down"].reshape(DFF, D)
    (h1, n1, gu1, a1), (win,) = _ffn_fwd(x0, g1, wgu1, wd1, "ffn1_fwd", exchange=_ex_gather_direct([own["w_in"]]))
    sems_l, thru_l, token_l = _gather_start([own[k_] for k_ in late], [h1], "gather_late_start")
    u, xp, q, k, v, gp, gs = _mix_in(h1, gm, win, after=(token_l,))
    o_sb, ctot = _attn_fwd(q, k, v)
    arrived = _gather_wait(sems_l, thru_l, [o_sb], "gather_late_wait")
    wbp, wba, wout = _exchange_alone(_ex_relay(arrived[:3]), "relay_mix")
    wout = wout.reshape(D, D)
    (h2, pm, p, yp, ys, mm), (wgu2, wd2) = _mix_out(h1, xp, o_sb, gp, gs, wgrp, pool_scale, wbp, wba, wout,
                                                    exchange=_ex_relay(arrived[3:]))
    wd2 = wd2.reshape(DFF, D)
    dh2, dgu3, d_g2, loss_row, d_gf, n3, a3, dh3 = _ffn_last(h2, g2, wgu2, wd2, tgt, gf, "ffn2")

    def grad_gate_up(n, dgu, name, exchange=None):
        res = _wgrad(n, dgu, NSH, D, name, exchange=exchange)
        return [_halves(res)] if exchange is None else ([_halves(res[0])], res[1])

    def grad_down(a, dh, name, exchange=None):
        res = _wgrad(a, dh, 1, FFS, name, exchange=exchange)
        halves = lambda g: [_halves(g.reshape(NSH, DFF // NSH, D))]
        return halves(res) if exchange is None else (halves(res[0]), res[1])

    k_gu2, k_d2, k_gu1, k_d1, k_in = (("ffn2_w_gate_up",), ("ffn2_w_down",), ("ffn1_w_gate_up",),
                                      ("ffn1_w_down",), ("w_in",))
    pa = grad_gate_up(n3, dgu3, "wgrad_gu2") + grad_down(a3, dh3, "wgrad_d2")
    (dlg, dyp, dys, do_sb, dyg, dxp, d_scale), got_a = _mix_bwd_out(
        dh2, gp, gs, yp, ys, pm, wgrp, pool_scale, wbp, wba, wout, exchange=_ex_pair_swap(pa))
    chip_a = pair_sums(k_gu2 + k_d2, pa, got_a)
    kb = ("w_out", "w_branch_pool", "w_branch_attn")
    g_bp, g_ba, d_group = _wgrad_branches(p, dyp, o_sb, dys, pm, dyg)
    pb = [_halves(_wgrad(mm, dh2, 1, D, "wgrad_out").reshape(NSH, D // NSH, D)), _halves(g_bp), _halves(g_ba)]
    k_a, k_in = k_gu2 + k_d2, k_in + kb
    sems_a, thru_a, token_a = _scatter_start(chip_a, "scatter_a_start")
    dq, dk, dv = _attn_bwd(q, k, v, do_sb, ctot, after=(token_a,))
    chip_a, owned_a = _scatter_wait(sems_a, thru_a, [dq], "scatter_a_wait")
    halves_a = chip_sums(k_a, chip_a, owned_a)
    dproj = (dxp, dq, dk, dv, dlg)
    (dh1, d_gm), both_a = _mix_bwd_in(dh2, h1, gm, dproj, win, exchange=_ex_share(halves_a))
    for i, k_ in enumerate(k_a):
        grad[k_] = both_a[i].reshape(wts[k_].shape)

    p_in = [_halves(_wgrad_in(u, dproj))] + pb
    p_d1, got_in = grad_down(a1, dh1, "wgrad_d1", exchange=_ex_pair_swap(p_in))
    sems_in, thru_in, token_in = _scatter_start(pair_sums(k_in, p_in, got_in), "scatter_in_start")
    dgu1, got_d1 = _ffn_bwd_act(dh1, gu1, wd1, "ffn1_bwd_act", exchange=_ex_pair_swap(p_d1), after=(token_in,))
    sems_d1, thru_d1, token_d1 = _scatter_start(pair_sums(k_d1, p_d1, got_d1), "scatter_d1_start")
    p_gu1 = [_halves(_wgrad(n1, dgu1, NSH, D, "wgrad_gu1", after=(token_in, token_d1)))]
    sems_w, thru_w, token_w = _swap_start(p_gu1, "swap_gu1_start")
    chip_in, owned_in = _scatter_wait(sems_in, thru_in, [token_w], "scatter_in_wait")
    chip_d1, owned_d1 = _scatter_wait(sems_d1, thru_d1, [token_w], "scatter_d1_wait")
    halves_in = chip_sums(k_in, chip_in, owned_in)
    p_gu1, got_gu1 = _swap_wait(sems_w, thru_w, halves_in, "swap_gu1_wait")
    sems, thru, token = _scatter_start(pair_sums(k_gu1, p_gu1, got_gu1), "scatter_gu1_start")
    sems_h, thru_h, token_h = _share_start(halves_in, [token], "share_in_start")
    adamw(k_a, after=(token_h,))
    landed = _share_wait(sems_h, thru_h, [delta[k_a[0]]], "share_in_wait")
    for i, k_ in enumerate(k_in):
        grad[k_] = landed[i].reshape(wts[k_].shape)
    adamw(k_in)
    dx, d_g1 = _ffn_bwd_in(dh1, x0, g1, dgu1, wgu1, "ffn1_bwd_in", after=(token,))
    small_g = dict(ffn1_norm=d_g1, mix_norm=d_gm, ffn2_norm=d_g2, final_norm=d_gf, pool_scale=d_scale,
                   pool_w_group=d_group, loss=loss_row)
    dev = 4 * lax.axis_index("x") + 2 * lax.axis_index("y") + lax.axis_index("c")
    slots = lax.dynamic_update_slice(jnp.zeros((8, SMALL_ROWS, 128), F32), _pack_small(small_g)[None], (dev, 0, 0))
    chip_gu1, owned_gu1 = _scatter_wait(sems, thru, [dx] + [delta[k_] for k_ in k_a + k_in], "scatter_gu1_wait")
    halves_last = chip_sums(k_d1 + k_gu1, chip_d1 + chip_gu1, owned_d1 + owned_gu1)
    sems_l, thru_l, token_l = _share_start(halves_last, [], "share_last_start")
    sems_s, slots, token_s = _small_gather_start(slots, "small_gather_start", after=(token_l,))
    both = _share_wait(sems_l, thru_l, [token_s], "share_last_wait")
    grad["ffn1_w_down"] = both[0].reshape(ffn1_w_down.shape)
    grad["ffn1_w_gate_up"] = both[1].reshape(ffn1_w_gate_up.shape)
    adamw(k_d1 + k_gu1, after=(token_s,))
    gathered = _small_gather_wait(sems_s, slots, [delta[k_] for k_ in k_d1 + k_gu1], "small_gather_wait")
    gathered = gathered.reshape(8 * SMALL_ROWS, 128)
    results = _small_update(gathered, _pack_small(wts), _pack_small(mom), _pack_small(var), _small_entries(wts))
    for dst, entries in zip((grad, delta, new_m, new_v), results):
        for k_, rows in zip(SMALL, entries):
            if k_ in wts:
                dst[k_] = rows.reshape(wts[k_].shape)
            elif dst is grad:
                loss = rows[0, 0]
    return (loss, dx[None], *[grad[k_] for k_ in ORDER], *[delta[k_] for k_ in ORDER],
            *[new_m[k_] for k_ in ORDER], *[new_v[k_] for k_ in ORDER])
```

```python
import dataclasses
import functools

import jax
import jax.numpy as jnp
from jax import lax
from jax.experimental import pallas as pl
from jax.experimental.pallas import tpu as pltpu

F32 = jnp.float32
BF16 = jnp.bfloat16

S = 2048
D = 1024
DFF = 2816
FFS = 2 * DFF // 4
NSH = 4
PW = 512
PG = 128
POOL_WINDOWS = (2, 4, 8, 16)
HALO = 16
SBW = 512
DH = 64
EPS = 1e-6
SCALE = 0.125
LOG2E = 1.4426950408889634
TA = 256
QB = 2
MIB = 1024 * 1024

LR, B1, B2, AEPS, WD, STEP = 0.001, 0.9, 0.999, 1e-08, 0.01, 10

_VM = pl.BlockSpec(memory_space=pltpu.VMEM)
_ANY = pl.BlockSpec(memory_space=pl.ANY)
MESH = pl.DeviceIdType.MESH
SIBLING_PAIR_ID = 1


def _nn(a, b):
    return jnp.dot(a, b, preferred_element_type=F32)


def _nt(a, b):
    return lax.dot_general(a, b, (((1,), (1,)), ((), ())), preferred_element_type=F32)


def _tn(a, b):
    return lax.dot_general(a, b, (((0,), (0,)), ((), ())), preferred_element_type=F32)


def _params(sem, vmem_mib):
    return pltpu.CompilerParams(dimension_semantics=sem, vmem_limit_bytes=vmem_mib * MIB)


def _rows(tm, width):
    return pl.BlockSpec((tm, width), lambda i: (i, 0))


def _fixed(shape):
    return pl.BlockSpec(shape, lambda *_: (0,) * len(shape))


def _sds(shape, dtype):
    return pltpu.HBM(shape, dtype)


def _in_hbm(args):
    return [pltpu.with_memory_space_constraint(a, pltpu.HBM) for a in args]


def _stage(pairs):
    pieces = 4

    def copy_all(sems):
        copies = []
        for src, dst in pairs:
            step = src.shape[0] // pieces
            for p in range(pieces):
                part = pl.ds(p * step, step)
                if len(dst.shape) == len(src.shape):
                    piece = (src.at[part], dst.at[part])
                else:
                    piece = (src.at[p], dst.at[:, pl.ds(p * src.shape[2], src.shape[2])])
                copies.append(pltpu.make_async_copy(*piece, sems.at[len(copies)]))
        for c in copies:
            c.start()
        for c in copies:
            c.wait()

    @pl.when(pl.program_id(0) == 0)
    def _():
        pl.run_scoped(copy_all, pltpu.SemaphoreType.DMA((pieces * len(pairs),)))


def _vmem_like(*arrays):
    return [pltpu.VMEM(a.shape, a.dtype) for a in arrays]


def _vmem_wide(w):
    return pltpu.VMEM((w.shape[1], w.shape[0] * w.shape[2]), w.dtype)


FF_CHUNKS = ((0, 1536), (1536, DFF - 1536))


def _wide_columns(src, dst, c0, cn):
    width, out = src.shape[2], []
    for p in range(src.shape[0]):
        lo, hi = max(c0, p * width), min(c0 + cn, (p + 1) * width)
        if lo < hi:
            out.append((src.at[p, :, pl.ds(lo - p * width, hi - lo)], dst.at[:, pl.ds(lo, hi - lo)]))
    return out


def _staged(groups, compute):
    first = pl.program_id(0) == 0

    def with_copies(sems):
        copies = []
        for group in groups:
            base = sum(len(g) for g in copies)
            copies.append([pltpu.make_async_copy(s, d, sems.at[base + i]) for i, (s, d) in enumerate(group)])
        for group in copies:
            for c in group:
                c.start()

        def ready(k):
            for c in copies[k]:
                c.wait()

        compute(ready)

    @pl.when(first)
    def _():
        pl.run_scoped(with_copies, pltpu.SemaphoreType.DMA((sum(len(g) for g in groups),)))

    @pl.when(jnp.logical_not(first))
    def _():
        compute(lambda k: None)


class Exchange:
    def __init__(self, arrays, landing, aliases, n_sems, start, finish, sibling_only=False):
        self.arrays, self.landing, self.aliases, self.n_sems = list(arrays), list(landing), dict(aliases), n_sems
        self.start, self.finish = start, finish
        self.sibling_only = sibling_only

    def enter(self):
        if self.sibling_only:
            barrier = pltpu.get_barrier_semaphore()
            sibling = (lax.axis_index("x"), lax.axis_index("y"), 1 - lax.axis_index("c"))
            pl.semaphore_signal(barrier, inc=1, device_id=sibling, device_id_type=MESH)
            pl.semaphore_wait(barrier, 1)

    def params(self, compiler_params=None):
        kw = dict(collective_id=SIBLING_PAIR_ID) if self.sibling_only else {}
        if compiler_params is None:
            return pltpu.CompilerParams(**kw)
        return dataclasses.replace(compiler_params, **kw)


def _call(body, args, *, name, grid, in_specs, out_specs, out_shape, scratch_shapes=(), compiler_params=None,
          exchange=None, free=(), after=()):
    args = [a if i in free else pltpu.with_memory_space_constraint(a, pltpu.HBM) for i, a in enumerate(args)]
    if exchange is None:
        n_in = len(in_specs)

        def plain(*refs):
            body(*refs[:n_in], *refs[n_in + len(after):])

        return pl.pallas_call(plain, name=name, grid=grid, in_specs=list(in_specs) + [_ANY] * len(after),
                              out_specs=out_specs, out_shape=out_shape, scratch_shapes=list(scratch_shapes),
                              compiler_params=compiler_params)(*args, *after)
    ex = exchange
    n_in, n_out, n_scr = len(in_specs), len(out_specs), len(scratch_shapes)
    na, nl = len(ex.arrays), len(ex.landing)

    def hosted(*refs):
        at = [0]

        def take(n):
            at[0] += n
            return refs[at[0] - n:at[0]]

        k_in, _, e_in, k_out, e_out, k_scr = take(n_in), take(len(after)), take(na), take(n_out), take(nl), take(n_scr)
        ssem, rsem = take(2)
        ids = [pl.program_id(a) for a in range(len(grid))]
        first = functools.reduce(jnp.logical_and, [i == 0 for i in ids])
        last = functools.reduce(jnp.logical_and, [i == g - 1 for i, g in zip(ids, grid)])

        @pl.when(first)
        def _():
            ex.enter()
            ex.start(e_in, e_out, ssem, rsem)

        body(*k_in, *k_out, *k_scr)

        @pl.when(last)
        def _():
            ex.finish(e_in, e_out, ssem, rsem)

    outs = pl.pallas_call(
        hosted, name=name, grid=grid,
        in_specs=list(in_specs) + [_ANY] * (len(after) + na), out_specs=list(out_specs) + [_ANY] * nl,
        out_shape=list(out_shape) + ex.landing,
        scratch_shapes=list(scratch_shapes) + [pltpu.SemaphoreType.DMA((ex.n_sems,))] * 2,
        input_output_aliases={n_in + len(after) + i: n_out + j for i, j in ex.aliases.items()},
        compiler_params=ex.params(compiler_params),
    )(*args, *after, *_in_hbm(ex.arrays))
    return outs[:n_out], outs[n_out:]


def _exchange_alone(ex, name, after=()):
    na, nl = len(ex.arrays), len(ex.landing)

    def body(*refs):
        outs = refs[na + len(after):na + len(after) + nl]
        ex.enter()
        ex.start(refs[:na], outs, refs[-2], refs[-1])
        ex.finish(refs[:na], outs, refs[-2], refs[-1])

    return pl.pallas_call(
        body, name=name, in_specs=[_ANY] * (na + len(after)), out_specs=[_ANY] * nl,
        out_shape=ex.landing, scratch_shapes=[pltpu.SemaphoreType.DMA((ex.n_sems,))] * 2,
        input_output_aliases=ex.aliases, compiler_params=ex.params(),
    )(*_in_hbm(ex.arrays), *after)


_HBM = pl.BlockSpec(memory_space=pltpu.HBM)
_SEM = pl.BlockSpec(memory_space=pltpu.SEMAPHORE)
_EFFECT = pltpu.SideEffectType.DATAFLOW_SIDE_EFFECTING


def _scatter_copies(srcs, lands, ssems, rsems):
    x, y, c, chips = _place()
    return [_remote(srcs[w].at[2 * px + py], lands[w].at[k], ssems[3 * w + k], rsems[3 * w + k], (px, py, c))
            for w in range(len(srcs)) for k, (px, py) in enumerate(chips)]


def _scatter_start(parts, name):
    parts = list(parts)
    n, ncp = len(parts), 3 * len(parts)
    lands = [lax.empty((3,) + p.shape[1:], p.dtype) for p in parts]

    def body(*refs):
        srcs, land_refs = refs[:n], refs[n:2 * n]
        ssems, rsems = refs[2 * n:2 * n + ncp], refs[2 * n + ncp:2 * n + 2 * ncp]
        for cp in _scatter_copies(srcs, land_refs, ssems, rsems):
            cp.start()
        token = refs[-1]
        token[...] = jnp.zeros_like(token)

    outs = pl.pallas_call(
        body, name=name,
        out_shape=([pltpu.SemaphoreType.DMA(())] * (2 * ncp) + [pltpu.HBM(a.shape, a.dtype) for a in parts + lands]
                   + [jax.ShapeDtypeStruct((8, 128), F32)]),
        in_specs=[_HBM] * (2 * n), out_specs=[_SEM] * (2 * ncp) + [_HBM] * (2 * n) + [_VM],
        input_output_aliases={i: 2 * ncp + i for i in range(2 * n)},
        compiler_params=pltpu.CompilerParams(has_side_effects=_EFFECT),
    )(*_in_hbm(parts), *_in_hbm(lands))
    sems, thru, token = outs[:2 * ncp], outs[2 * ncp:2 * ncp + 2 * n], outs[-1]
    return sems, thru, token


def _scatter_wait(sems, thru, after, name):
    n = len(thru) // 2
    ncp = 3 * n

    def body(*refs):
        srcs, land_refs = refs[:n], refs[n:2 * n]
        ssems, rsems = refs[2 * n:2 * n + ncp], refs[2 * n + ncp:2 * n + 2 * ncp]
        for cp in _scatter_copies(srcs, land_refs, ssems, rsems):
            cp.wait_send()
            cp.wait_recv()

    outs = pl.pallas_call(
        body, name=name, out_shape=[pltpu.HBM(a.shape, a.dtype) for a in thru],
        in_specs=[_HBM] * (2 * n) + [_SEM] * (2 * ncp) + [_ANY] * len(after), out_specs=[_HBM] * (2 * n),
        input_output_aliases={i: i for i in range(2 * n)},
        compiler_params=pltpu.CompilerParams(has_side_effects=_EFFECT),
    )(*thru, *sems, *after)
    return outs[:n], outs[n:]


def _swap_copies(srcs, lands, ssems, rsems):
    x, y, c, _ = _place()
    return [_remote(srcs[w].at[:, 1 - c], lands[w], ssems[w], rsems[w], (x, y, 1 - c)) for w in range(len(srcs))]


def _swap_start(grads, name):
    grads = list(grads)
    n = len(grads)
    lands = [lax.empty((NSH,) + g.shape[2:], g.dtype) for g in grads]

    def body(*refs):
        barrier = pltpu.get_barrier_semaphore()
        sibling = (lax.axis_index("x"), lax.axis_index("y"), 1 - lax.axis_index("c"))
        pl.semaphore_signal(barrier, inc=1, device_id=sibling, device_id_type=MESH)
        pl.semaphore_wait(barrier, 1)
        for cp in _swap_copies(refs[:n], refs[n:2 * n], refs[2 * n:3 * n], refs[3 * n:4 * n]):
            cp.start()
        refs[-1][...] = jnp.zeros_like(refs[-1])

    outs = pl.pallas_call(
        body, name=name,
        out_shape=([pltpu.SemaphoreType.DMA(())] * (2 * n) + [pltpu.HBM(a.shape, a.dtype) for a in grads + lands]
                   + [jax.ShapeDtypeStruct((8, 128), F32)]),
        in_specs=[_HBM] * (2 * n), out_specs=[_SEM] * (2 * n) + [_HBM] * (2 * n) + [_VM],
        input_output_aliases={i: 2 * n + i for i in range(2 * n)},
        compiler_params=pltpu.CompilerParams(has_side_effects=_EFFECT, collective_id=SIBLING_PAIR_ID),
    )(*_in_hbm(grads), *_in_hbm(lands))
    return outs[:2 * n], outs[2 * n:4 * n], outs[-1]


def _swap_wait(sems, thru, after, name):
    n = len(thru) // 2

    def body(*refs):
        for cp in _swap_copies(refs[:n], refs[n:2 * n], refs[2 * n:3 * n], refs[3 * n:4 * n]):
            cp.wait_send()
            cp.wait_recv()

    outs = pl.pallas_call(
        body, name=name, out_shape=[pltpu.HBM(a.shape, a.dtype) for a in thru],
        in_specs=[_HBM] * (2 * n) + [_SEM] * (2 * n) + [_ANY] * len(after), out_specs=[_HBM] * (2 * n),
        input_output_aliases={i: i for i in range(2 * n)},
        compiler_params=pltpu.CompilerParams(has_side_effects=_EFFECT),
    )(*thru, *sems, *after)
    return outs[:n], outs[n:]


def _share_copies(bufs, ssems, rsems, sending):
    x, y, c, _ = _place()
    out = []
    for w, ref in enumerate(bufs):
        slot = ref.at[c if sending else 1 - c]
        out.append(_remote(slot, slot, ssems[w], rsems[w], (x, y, 1 - c)))
    return out


def _share_start(bufs, after, name):
    bufs = list(bufs)
    n = len(bufs)

    def body(*refs):
        barrier = pltpu.get_barrier_semaphore()
        sibling = (lax.axis_index("x"), lax.axis_index("y"), 1 - lax.axis_index("c"))
        pl.semaphore_signal(barrier, inc=1, device_id=sibling, device_id_type=MESH)
        pl.semaphore_wait(barrier, 1)
        at = n + len(after)
        for cp in _share_copies(refs[:n], refs[at:at + n], refs[at + n:at + 2 * n], True):
            cp.start()
        refs[-1][...] = jnp.zeros_like(refs[-1])

    outs = pl.pallas_call(
        body, name=name,
        out_shape=([pltpu.SemaphoreType.DMA(())] * (2 * n) + [pltpu.HBM(a.shape, a.dtype) for a in bufs]
                   + [jax.ShapeDtypeStruct((8, 128), F32)]),
        in_specs=[_HBM] * n + [_ANY] * len(after), out_specs=[_SEM] * (2 * n) + [_HBM] * n + [_VM],
        input_output_aliases={i: 2 * n + i for i in range(n)},
        compiler_params=pltpu.CompilerParams(has_side_effects=_EFFECT, collective_id=SIBLING_PAIR_ID),
    )(*_in_hbm(bufs), *after)
    return outs[:2 * n], outs[2 * n:3 * n], outs[-1]


def _share_wait(sems, thru, after, name):
    n = len(thru)

    def body(*refs):
        for cp in _share_copies(refs[:n], refs[n:2 * n], refs[2 * n:3 * n], True):
            cp.wait_send()
        for cp in _share_copies(refs[:n], refs[n:2 * n], refs[2 * n:3 * n], False):
            cp.wait_recv()

    return pl.pallas_call(
        body, name=name, out_shape=[pltpu.HBM(a.shape, a.dtype) for a in thru],
        in_specs=[_HBM] * n + [_SEM] * (2 * n) + [_ANY] * len(after), out_specs=[_HBM] * n,
        input_output_aliases={i: i for i in range(n)},
        compiler_params=pltpu.CompilerParams(has_side_effects=_EFFECT),
    )(*thru, *sems, *after)


def _gather_copies(bufs, ssems, rsems, sending):
    x, y, c, chips = _place()
    out = []
    for w, ref in enumerate(bufs):
        half = ref.shape[1] // 2
        for k, (px, py) in enumerate(chips):
            rows = ref.at[2 * x + y if sending else 2 * px + py, pl.ds(c * half, half)]
            out.append(_remote(rows, rows, ssems[3 * w + k], rsems[3 * w + k], (px, py, c)))
    return out


def _gather_start(bufs, after, name):
    n, ncp = len(bufs), 3 * len(bufs)

    def body(*refs):
        ssems, rsems = refs[n + len(after):n + len(after) + ncp], refs[n + len(after) + ncp:n + len(after) + 2 * ncp]
        for cp in _gather_copies(refs[:n], ssems, rsems, True):
            cp.start()
        token = refs[-1]
        token[...] = jnp.zeros_like(token)

    outs = pl.pallas_call(
        body, name=name,
        out_shape=([pltpu.SemaphoreType.DMA(())] * (2 * ncp) + [pltpu.HBM(a.shape, a.dtype) for a in bufs]
                   + [jax.ShapeDtypeStruct((8, 128), F32)]),
        in_specs=[_HBM] * n + [_ANY] * len(after), out_specs=[_SEM] * (2 * ncp) + [_HBM] * n + [_VM],
        input_output_aliases={i: 2 * ncp + i for i in range(n)},
        compiler_params=pltpu.CompilerParams(has_side_effects=_EFFECT),
    )(*_in_hbm(bufs), *after)
    return outs[:2 * ncp], outs[2 * ncp:2 * ncp + n], outs[-1]


def _gather_wait(sems, thru, after, name):
    n = len(thru)
    ncp = 3 * n

    def body(*refs):
        ssems, rsems = refs[n:n + ncp], refs[n + ncp:n + 2 * ncp]
        for cp in _gather_copies(refs[:n], ssems, rsems, True):
            cp.wait_send()
        for cp in _gather_copies(refs[:n], ssems, rsems, False):
            cp.wait_recv()

    return pl.pallas_call(
        body, name=name, out_shape=[pltpu.HBM(a.shape, a.dtype) for a in thru],
        in_specs=[_HBM] * n + [_SEM] * (2 * ncp) + [_ANY] * len(after), out_specs=[_HBM] * n,
        input_output_aliases={i: i for i in range(n)},
        compiler_params=pltpu.CompilerParams(has_side_effects=_EFFECT),
    )(*thru, *sems, *after)


def _rms(x):
    r = lax.rsqrt(jnp.mean(x * x, axis=-1, keepdims=True) + EPS)
    return r, x * r


def _rms_bwd(dn, xr, r, gain):
    dng = dn * gain
    dx = r * (dng - xr * jnp.mean(dng * xr, axis=-1, keepdims=True))
    return dx, jnp.sum(dn * xr, axis=0, keepdims=True)


def _ffn_weight_groups(wgu_hbm, wgu_ref, wd_hbm, wd_ref):
    groups = []
    for c0, cn in FF_CHUNKS:
        groups += [_wide_columns(wgu_hbm, wgu_ref, c0, cn), _wide_columns(wgu_hbm, wgu_ref, DFF + c0, cn),
                   [(wd_hbm.at[pl.ds(c0, cn)], wd_ref.at[pl.ds(c0, cn)])]]
    return groups


def _ffn_fwd(x, gain, wgu, wd, name, exchange=None):
    tm = 256

    def body(x_ref, g_ref, wgu_hbm, wd_hbm, h_ref, n_ref, gu_ref, a_ref, wgu_ref, wd_ref):
        def compute(ready):
            x = x_ref[...]
            _, xr = _rms(x)
            n = (xr * g_ref[...]).astype(BF16)
            n_ref[...] = n
            acc = jnp.zeros((tm, D), F32)
            for i, (c0, cn) in enumerate(FF_CHUNKS):
                ready(3 * i)
                g = _nn(n, wgu_ref[:, c0:c0 + cn])
                ready(3 * i + 1)
                u = _nn(n, wgu_ref[:, DFF + c0:DFF + c0 + cn])
                gu_ref[:, c0:c0 + cn] = g.astype(BF16)
                gu_ref[:, DFF + c0:DFF + c0 + cn] = u.astype(BF16)
                half_act = (0.5 * (g * jax.nn.sigmoid(g) * u)).astype(BF16)
                a_ref[:, c0:c0 + cn] = half_act
                ready(3 * i + 2)
                acc = acc + _nn(half_act, wd_ref[c0:c0 + cn, :])
            h_ref[...] = x + acc

        _staged(_ffn_weight_groups(wgu_hbm, wgu_ref, wd_hbm, wd_ref), compute)

    return _call(
        body, (x, gain, wgu, wd), name=name, grid=(S // tm,),
        in_specs=[_rows(tm, D), _fixed((1, D)), _ANY, _ANY],
        out_specs=[_rows(tm, D), _rows(tm, D), _rows(tm, 4 * FFS), _rows(tm, DFF)],
        out_shape=[_sds((S, D), F32), _sds((S, D), BF16), _sds((S, 4 * FFS), BF16), _sds((S, DFF), BF16)],
        scratch_shapes=[_vmem_wide(wgu)] + _vmem_like(wd),
        compiler_params=_params(("arbitrary",), 56), exchange=exchange)


def _ffn_last(x, gain, wgu, wd, target, gf, name):
    tm = 256

    def body(x_ref, g_ref, wgu_hbm, wd_hbm, t_ref, gf_ref, dx_ref, dgu_ref, dg_ref, loss_ref, dgf_ref, n_ref,
             a_ref, dh_ref, wgu_ref, wd_ref):
        @pl.when(pl.program_id(0) == 0)
        def _():
            dg_ref[...] = jnp.zeros_like(dg_ref)
            dgf_ref[...] = jnp.zeros_like(dgf_ref)
            loss_ref[...] = jnp.zeros_like(loss_ref)

        def compute():
            x = x_ref[...]
            r0, xr = _rms(x)
            n = (xr * g_ref[...]).astype(BF16)
            n_ref[...] = n
            acc = jnp.zeros((tm, D), F32)
            kept = []
            for c0, cn in FF_CHUNKS:
                g = _nn(n, wgu_ref[:, c0:c0 + cn])
                u = _nn(n, wgu_ref[:, DFF + c0:DFF + c0 + cn])
                kept.append((g.astype(BF16), u.astype(BF16)))
                half_act = (0.5 * (g * jax.nn.sigmoid(g) * u)).astype(BF16)
                a_ref[:, c0:c0 + cn] = half_act
                acc = acc + _nn(half_act, wd_ref[c0:c0 + cn, :])
            h = x + acc
            gf = gf_ref[...]
            r, hr = _rms(h)
            err = hr * gf - t_ref[...]
            dh, dgain_f = _rms_bwd(err * (1.0 / D), hr, r, gf)
            dh_ref[...] = dh
            dhb = dh.astype(BF16)
            dn = jnp.zeros((tm, D), F32)
            for (c0, cn), (gb, ub) in zip(FF_CHUNKS, kept):
                g, u = gb.astype(F32), ub.astype(F32)
                da = 0.5 * _nt(dhb, wd_ref[c0:c0 + cn, :])
                sg = jax.nn.sigmoid(g)
                dgb = (da * u * (sg * (1.0 + g * (1.0 - sg)))).astype(BF16)
                dub = (da * (g * sg)).astype(BF16)
                dgu_ref[:, c0:c0 + cn] = dgb
                dgu_ref[:, DFF + c0:DFF + c0 + cn] = dub
                dn = dn + _nt(dgb, wgu_ref[:, c0:c0 + cn]) + _nt(dub, wgu_ref[:, DFF + c0:DFF + c0 + cn])
            dx, dgain = _rms_bwd(dn, xr, r0, g_ref[...])
            dx_ref[...] = dh + dx
            dg_ref[...] += dgain
            dgf_ref[...] += dgain_f
            loss_ref[...] += jnp.full((1, 128), (0.5 / D) * jnp.sum(err * err), F32)

        _stage([(wgu_hbm, wgu_ref), (wd_hbm, wd_ref)])
        compute()

    return _call(
        body, (x, gain, wgu, wd, target, gf), name=name, grid=(S // tm,),
        in_specs=[_rows(tm, D), _fixed((1, D)), _ANY, _ANY, _rows(tm, D), _fixed((1, D))],
        out_specs=[_rows(tm, D), _rows(tm, 4 * FFS), _fixed((1, D)), _fixed((1, 128)), _fixed((1, D)),
                   _rows(tm, D), _rows(tm, DFF), _rows(tm, D)],
        out_shape=[_sds((S, D), F32), _sds((S, 4 * FFS), BF16), _sds((1, D), F32), _sds((1, 128), F32),
                   _sds((1, D), F32), _sds((S, D), BF16), _sds((S, DFF), BF16), _sds((S, D), F32)],
        scratch_shapes=[_vmem_wide(wgu)] + _vmem_like(wd),
        compiler_params=_params(("arbitrary",), 58), free=(4, 5))


def _ffn_bwd_act(dh, gu, wd, name, exchange=None, after=()):
    tm = 512

    def body(dh_ref, gu_ref, wd_hbm, dgu_ref, wd_ref):
        _stage([(wd_hbm, wd_ref)])
        dhb = dh_ref[...].astype(BF16)
        for c0, cn in FF_CHUNKS:
            g = gu_ref[:, c0:c0 + cn].astype(F32)
            u = gu_ref[:, DFF + c0:DFF + c0 + cn].astype(F32)
            da = 0.5 * _nt(dhb, wd_ref[c0:c0 + cn, :])
            sg = jax.nn.sigmoid(g)
            dgu_ref[:, c0:c0 + cn] = (da * u * (sg * (1.0 + g * (1.0 - sg)))).astype(BF16)
            dgu_ref[:, DFF + c0:DFF + c0 + cn] = (da * (g * sg)).astype(BF16)

    res = _call(
        body, (dh, gu, wd), name=name, grid=(S // tm,),
        in_specs=[_rows(tm, D), _rows(tm, 4 * FFS), _ANY], out_specs=[_rows(tm, 4 * FFS)],
        out_shape=[_sds((S, 4 * FFS), BF16)], scratch_shapes=_vmem_like(wd),
        compiler_params=_params(("arbitrary",), 56), exchange=exchange, after=after)
    return res[0] if exchange is None else (res[0][0], res[1])


def _ffn_bwd_in(dh, x, gain, dgu, wgu, name, exchange=None, after=()):
    tm = 512

    def body(dh_ref, x_ref, g_ref, dgu_ref, wgu_hbm, dx_ref, dg_ref, wgu_ref):
        chunks = [(half + c0, cn) for half in (0, DFF) for c0, cn in FF_CHUNKS]

        @pl.when(pl.program_id(0) == 0)
        def _():
            dg_ref[...] = jnp.zeros_like(dg_ref)

        def compute(ready):
            dn = jnp.zeros((tm, D), F32)
            for k, (c0, cn) in enumerate(chunks):
                ready(k)
                dn = dn + _nt(dgu_ref[:, c0:c0 + cn], wgu_ref[:, c0:c0 + cn])
            r, xr = _rms(x_ref[...])
            dx, dgain = _rms_bwd(dn, xr, r, g_ref[...])
            dx_ref[...] = dh_ref[...] + dx
            dg_ref[...] += dgain

        _staged([_wide_columns(wgu_hbm, wgu_ref, c0, cn) for c0, cn in chunks], compute)

    return _call(
        body, (dh, x, gain, dgu, wgu), name=name, grid=(S // tm,),
        in_specs=[_rows(tm, D), _rows(tm, D), _fixed((1, D)), _rows(tm, 4 * FFS), _ANY],
        out_specs=[_rows(tm, D), _fixed((1, D))],
        out_shape=[_sds((S, D), F32), _sds((1, D), F32)],
        scratch_shapes=[_vmem_wide(wgu)],
        compiler_params=_params(("arbitrary",), 56), exchange=exchange, after=after)


def _mix_in(h, gain, w_in, after=()):
    tm = 512

    def body(h_ref, g_ref, w_hbm, u_ref, xp_ref, q_ref, k_ref, v_ref, gp_ref, gs_ref, w_ref):
        def compute(ready):
            _, hr = _rms(h_ref[...])
            u = (hr * g_ref[...]).astype(BF16)
            u_ref[...] = u
            ready(0)
            p0 = _nn(u, w_ref[0])
            xp_ref[...] = p0[:, :PW]
            q_ref[...] = p0[:, PW:].astype(BF16)
            ready(1)
            p1 = _nn(u, w_ref[1])
            k_ref[...] = p1[:, :SBW].astype(BF16)
            v_ref[...] = p1[:, SBW:].astype(BF16)
            ready(2)
            gp_ref[...] = jax.nn.sigmoid(_nn(u, w_ref[2])).astype(BF16)
            ready(3)
            gs_ref[...] = jax.nn.sigmoid(_nn(u, w_ref[3])).astype(BF16)

        _staged([[(w_hbm.at[j], w_ref.at[j])] for j in range(NSH)], compute)

    return _call(
        body, (h, gain, w_in), name="mix_in", grid=(S // tm,),
        in_specs=[_rows(tm, D), _fixed((1, D)), _ANY],
        out_specs=[_rows(tm, D), _rows(tm, PW), _rows(tm, SBW), _rows(tm, SBW), _rows(tm, SBW),
                   _rows(tm, D), _rows(tm, D)],
        out_shape=[_sds((S, D), BF16), _sds((S, PW), F32), _sds((S, SBW), BF16), _sds((S, SBW), BF16),
                   _sds((S, SBW), BF16), _sds((S, D), BF16), _sds((S, D), BF16)],
        scratch_shapes=_vmem_like(w_in),
        compiler_params=_params(("arbitrary",), 48), free=(1,), after=after)


def _hilo_dot(x, tri):
    hi = x.astype(BF16)
    lo = (x - hi.astype(F32)).astype(BF16)
    return _nn(hi, tri) + _nn(lo, tri)


def _log_terms(qk):
    z2 = qk * (SCALE * LOG2E)
    lb = jnp.minimum(z2, 0.0) - jnp.log2(1.0 + jnp.exp2(-jnp.abs(z2)))
    return lb, lb - z2


def _head_masks():
    lane = lax.broadcasted_iota(jnp.int32, (1, 2 * DH), 1)
    return (lane < DH, lane >= DH)


def _attn_fwd(q, k, v, exchange=None):
    T = TA

    def body(q_ref, k_ref, v_ref, o_ref, c_ref):
        i2 = 2 * pl.program_id(1)
        row = lax.broadcasted_iota(jnp.int32, (T, T), 0)
        col = lax.broadcasted_iota(jnp.int32, (T, T), 1)
        after = (row > col).astype(BF16)
        causal = col < row
        masks = _head_masks()
        qms = {}
        for b in range(QB):
            q2 = q_ref[b * T:(b + 1) * T, :]
            for h, hm in enumerate(masks):
                qms[b, h] = jnp.where(hm, q2, jnp.zeros_like(q2))

        def blocks(keys, pairs, carries, os):
            ks, vms = [], []
            for j in keys:
                rows = pl.ds(pl.multiple_of(j * T, T), T)
                vj = v_ref[rows, :]
                ks.append(k_ref[rows, :])
                vms.append([jnp.where(hm, vj, jnp.zeros_like(vj)) for hm in masks])
            units = [(n, h) for n in range(len(pairs)) for h in range(2)]
            qks = {(n, h): _nt(qms[pairs[n][0], h], ks[pairs[n][1]]) for n, h in units}
            lbs, l1ms = {}, {}
            for u in units:
                lbs[u], l1m = _log_terms(qks[u])
                l1ms[u] = jnp.where(causal, l1m, 0.0) if pairs[u[0]][2] else l1m
            cins = {u: _hilo_dot(l1ms[u], after) for u in units}
            carries, os = dict(carries), list(os)
            for n, h in units:
                b, key, diag = pairs[n]
                a = jnp.exp2(lbs[n, h] + cins[n, h] + carries[b, h])
                if diag:
                    a = jnp.where(causal, a, 0.0)
                os[b] = os[b] + _nn(a.astype(BF16), vms[key][h])
                carries[b, h] = carries[b, h] + jnp.sum(l1ms[n, h], axis=1, keepdims=True)
            return carries, tuple(os)

        carries = {(b, h): jnp.zeros((T, 1), F32) for b in range(QB) for h in range(2)}
        os = tuple(jnp.zeros((T, 2 * DH), F32) for _ in range(QB))
        carries, os = blocks([i2 + 1, i2], [(1, 0, True), (0, 1, True), (1, 1, False)], carries, os)
        carries, os = lax.fori_loop(
            0, i2 // 2,
            lambda t, c: blocks([i2 - 1 - 2 * t, i2 - 2 - 2 * t],
                                [(0, 0, False), (1, 0, False), (0, 1, False), (1, 1, False)], c[0], c[1]),
            (carries, os))
        for b in range(QB):
            o_ref[b * T:(b + 1) * T, :] = os[b].astype(BF16)
            c_ref[b * T:(b + 1) * T, :] = jnp.where(masks[0], carries[b, 0], carries[b, 1])

    blk = pl.BlockSpec((QB * T, 2 * DH), lambda p, i: (i, p))
    full = pl.BlockSpec((S, 2 * DH), lambda p, i: (0, p))
    return _call(
        body, (q, k, v), name="attn_fwd", grid=(SBW // (2 * DH), S // (QB * T)),
        in_specs=[blk, full, full], out_specs=[blk, blk],
        out_shape=[_sds((S, SBW), BF16), _sds((S, SBW), F32)],
        compiler_params=_params(("arbitrary", "arbitrary"), 40), exchange=exchange)


def _attn_bwd(q, k, v, do, ctot, after=()):
    T = TA
    nq = S // (QB * T)

    def body(q_ref, k_ref, v_ref, do_ref, c_ref, dq_ref, dk_ref, dv_ref, dk_acc, dv_acc):
        step = pl.program_id(1)
        i2 = 2 * step

        @pl.when(step == 0)
        def _():
            dk_acc[...] = jnp.zeros_like(dk_acc)
            dv_acc[...] = jnp.zeros_like(dv_acc)

        row = lax.broadcasted_iota(jnp.int32, (T, T), 0)
        col = lax.broadcasted_iota(jnp.int32, (T, T), 1)
        upto = (row <= col).astype(BF16)
        before = (row < col).astype(BF16)
        causal = col < row
        masks = _head_masks()
        qms, doms, ctots = {}, {}, {}
        for b in range(QB):
            q2, do2 = q_ref[b * T:(b + 1) * T, :], do_ref[b * T:(b + 1) * T, :]
            for h, hm in enumerate(masks):
                qms[b, h] = jnp.where(hm, q2, jnp.zeros_like(q2))
                doms[b, h] = jnp.where(hm, do2, jnp.zeros_like(do2))
                ctots[b, h] = c_ref[b * T:(b + 1) * T, h * DH:h * DH + 1]

        def blocks(keys, pairs, sums, dqs):
            rows = [pl.ds(pl.multiple_of(j * T, T), T) for j in keys]
            ks, vs = [k_ref[r, :] for r in rows], [v_ref[r, :] for r in rows]
            kms = [[jnp.where(hm, kj, jnp.zeros_like(kj)) for hm in masks] for kj in ks]
            units = [(n, h) for n in range(len(pairs)) for h in range(2)]
            qks = {(n, h): _nt(qms[pairs[n][0], h], ks[pairs[n][1]]) for n, h in units}
            das = {(n, h): _nt(doms[pairs[n][0], h], vs[pairs[n][1]]) for n, h in units}
            lbs, l1ms = {}, {}
            for u in units:
                lbs[u], l1m = _log_terms(qks[u])
                l1ms[u] = jnp.where(causal, l1m, 0.0) if pairs[u[0]][2] else l1m
            pins = {u: _hilo_dot(l1ms[u], upto) for u in units}
            sums = dict(sums)
            a_s, dls, cps = {}, {}, {}
            for n, h in units:
                b, _, diag = pairs[n]
                cl, cp = sums[b, h]
                a = jnp.exp2(lbs[n, h] + (ctots[b, h] - cl) - pins[n, h])
                if diag:
                    a = jnp.where(causal, a, 0.0)
                a_s[n, h] = a.astype(BF16)
                dls[n, h] = das[n, h] * a
                cps[n, h] = cp
                sums[b, h] = (cl + jnp.sum(l1ms[n, h], axis=1, keepdims=True),
                              cp + jnp.sum(dls[n, h], axis=1, keepdims=True))
            pexs = {u: _hilo_dot(dls[u], before) for u in units}
            dzbs = {}
            for u in units:
                dz = dls[u] - jnp.exp2(lbs[u]) * (dls[u] + pexs[u] + cps[u])
                if pairs[u[0]][2]:
                    dz = jnp.where(causal, dz, 0.0)
                dzbs[u] = dz.astype(BF16)
            dqs = list(dqs)
            for n, h in units:
                dqs[pairs[n][0]] = dqs[pairs[n][0]] + _nn(dzbs[n, h], kms[pairs[n][1]][h])
            for key, r in enumerate(rows):
                mine = [(n, h) for n, h in units if pairs[n][1] == key]
                dk_acc[r, :] += functools.reduce(jnp.add, [_tn(dzbs[u], qms[pairs[u[0]][0], u[1]]) for u in mine])
                dv_acc[r, :] += functools.reduce(jnp.add, [_tn(a_s[u], doms[pairs[u[0]][0], u[1]]) for u in mine])
            return sums, tuple(dqs)

        zero = jnp.zeros((T, 1), F32)
        sums = {(b, h): (zero, zero) for b in range(QB) for h in range(2)}
        dqs = tuple(jnp.zeros((T, 2 * DH), F32) for _ in range(QB))
        sums, dqs = lax.fori_loop(
            0, i2 // 2,
            lambda t, c: blocks([2 * t, 2 * t + 1],
                                [(0, 0, False), (1, 0, False), (0, 1, False), (1, 1, False)], c[0], c[1]),
            (sums, dqs))
        _, dqs = blocks([i2, i2 + 1], [(0, 0, True), (1, 0, False), (1, 1, True)], sums, dqs)
        for b in range(QB):
            dq_ref[b * T:(b + 1) * T, :] = (dqs[b] * SCALE).astype(BF16)

        @pl.when(step == nq - 1)
        def _():
            dk_ref[...] = (dk_acc[...] * SCALE).astype(BF16)
            dv_ref[...] = dv_acc[...].astype(BF16)

    blk = pl.BlockSpec((QB * T, 2 * DH), lambda p, i: (i, p))
    full = pl.BlockSpec((S, 2 * DH), lambda p, i: (0, p))
    return _call(
        body, (q, k, v, do, ctot), name="attn_bwd", grid=(SBW // (2 * DH), nq),
        in_specs=[blk, full, full, blk, blk], out_specs=[blk, full, full],
        out_shape=[_sds((S, SBW), BF16), _sds((S, SBW), BF16), _sds((S, SBW), BF16)],
        scratch_shapes=[pltpu.VMEM((S, 2 * DH), F32), pltpu.VMEM((S, 2 * DH), F32)],
        compiler_params=_params(("arbitrary", "arbitrary"), 40), after=after)


def _pool_counts(first_row, tm):
    pos = first_row + lax.broadcasted_iota(jnp.int32, (tm, 1), 0)
    return [jnp.minimum(pos + 1, w).astype(F32) for w in POOL_WINDOWS]


def _mix_out(h, xp, o_sb, gp, gs, w_group, scale, w_bp, w_ba, w_out, exchange=None):
    tm = 512

    def body(h_ref, xp_ref, o_ref, gp_ref, gs_ref, wg_hbm, sc_ref, wbp_hbm, wba_hbm, wo_hbm,
             h2_ref, pm_ref, p_ref, yp_ref, ys_ref, m_ref, halo, wg_ref, wbp_ref, wba_ref, wo_ref):
        _stage([(wg_hbm, wg_ref), (wbp_hbm, wbp_ref), (wba_hbm, wba_ref), (wo_hbm, wo_ref)])
        i = pl.program_id(0)

        @pl.when(i == 0)
        def _():
            halo[...] = jnp.zeros_like(halo)

        xp = xp_ref[...]
        ext = jnp.concatenate([halo[...], xp], axis=0)
        halo[...] = xp[tm - HALO:, :]
        counts = _pool_counts(i * tm, tm)
        for gi in range(len(POOL_WINDOWS)):
            lanes = slice(gi * PG, (gi + 1) * PG)
            win = ext[:, lanes]
            for step in range(gi + 1):
                win = win + pltpu.roll(win, 1 << step, 0)
            pm = (win[HALO:, :] / counts[gi] - xp[:, lanes]).astype(BF16)
            pm_ref[:, lanes] = pm
            p_ref[:, lanes] = (_nn(pm, wg_ref[gi]) * sc_ref[:, lanes]).astype(BF16)
        pb = p_ref[...]
        ob = o_ref[...]
        for j in range(NSH):
            cols = slice(j * (D // NSH), (j + 1) * (D // NSH))
            yp = _nn(pb, wbp_ref[j])
            ys = _nn(ob, wba_ref[j])
            yp_ref[:, cols] = yp.astype(BF16)
            ys_ref[:, cols] = ys.astype(BF16)
            m_ref[:, cols] = (gp_ref[:, cols].astype(F32) * yp + gs_ref[:, cols].astype(F32) * ys).astype(BF16)
        h2_ref[...] = h_ref[...] + _nn(m_ref[...], wo_ref[...])

    return _call(
        body, (h, xp, o_sb, gp, gs, w_group, scale, w_bp, w_ba, w_out), name="mix_out", grid=(S // tm,),
        in_specs=[_rows(tm, D), _rows(tm, PW), _rows(tm, SBW), _rows(tm, D), _rows(tm, D),
                  _ANY, _fixed((1, PW)), _ANY, _ANY, _ANY],
        out_specs=[_rows(tm, D), _rows(tm, PW), _rows(tm, PW), _rows(tm, D), _rows(tm, D), _rows(tm, D)],
        out_shape=[_sds((S, D), F32), _sds((S, PW), BF16), _sds((S, PW), BF16), _sds((S, D), BF16),
                   _sds((S, D), BF16), _sds((S, D), BF16)],
        scratch_shapes=[pltpu.VMEM((HALO, PW), F32)] + _vmem_like(w_group, w_bp, w_ba, w_out),
        compiler_params=_params(("arbitrary",), 48), free=(5, 6), exchange=exchange)


def _mix_bwd_out(dh, gp, gs, yp, ys, pm, w_group, scale, w_bp, w_ba, w_out, exchange=None):
    tm = 512
    nt = S // tm

    def body(dh_ref, gp_ref, gs_ref, yp_ref, ys_ref, pm_ref, wg_hbm, sc_ref, wbp_hbm, wba_hbm, wo_hbm,
             dlg_ref, dyp_ref, dys_ref, do_ref, dyg_ref, dxp_ref, dsc_ref, halo, wg_ref, wbp_ref, wba_ref, wo_ref):
        _stage([(wg_hbm, wg_ref), (wbp_hbm, wbp_ref), (wba_hbm, wba_ref), (wo_hbm, wo_ref)])
        step = pl.program_id(0)

        @pl.when(step == 0)
        def _():
            halo[...] = jnp.zeros_like(halo)
            dsc_ref[...] = jnp.zeros_like(dsc_ref)

        dm = _nt(dh_ref[...].astype(BF16), wo_ref[...])
        gp = gp_ref[...].astype(F32)
        gs = gs_ref[...].astype(F32)
        yp = yp_ref[...].astype(F32)
        ys = ys_ref[...].astype(F32)
        dlg_ref[:, :D] = (dm * yp * gp * (1.0 - gp)).astype(BF16)
        dlg_ref[:, D:] = (dm * ys * gs * (1.0 - gs)).astype(BF16)
        dyp_ref[...] = (dm * gp).astype(BF16)
        dys_ref[...] = (dm * gs).astype(BF16)
        dp = jnp.zeros((tm, PW), F32)
        do = jnp.zeros((tm, SBW), F32)
        for j in range(NSH):
            cols = slice(j * (D // NSH), (j + 1) * (D // NSH))
            dp = dp + _nt(dyp_ref[:, cols], wbp_ref[j])
            do = do + _nt(dys_ref[:, cols], wba_ref[j])
        do_ref[...] = do.astype(BF16)
        counts = _pool_counts((nt - 1 - step) * tm, tm)
        dscale = []
        for gi in range(len(POOL_WINDOWS)):
            lanes = slice(gi * PG, (gi + 1) * PG)
            dpg = dp[:, lanes]
            dscale.append(jnp.sum(dpg * _nn(pm_ref[:, lanes], wg_ref[gi]), axis=0, keepdims=True))
            dyg = (dpg * sc_ref[:, lanes]).astype(BF16)
            dyg_ref[:, lanes] = dyg
            dpm = _nt(dyg, wg_ref[gi])
            per = dpm / counts[gi]
            win = jnp.concatenate([per, halo[:, lanes]], axis=0)
            halo[:, lanes] = per[:HALO, :]
            for s in range(gi + 1):
                win = win + pltpu.roll(win, tm + HALO - (1 << s), 0)
            dxp_ref[:, lanes] = (win[:tm, :] - dpm).astype(BF16)
        dsc_ref[...] += jnp.concatenate(dscale, axis=1)

    rev = lambda width: pl.BlockSpec((tm, width), lambda i: (nt - 1 - i, 0))
    return _call(
        body, (dh, gp, gs, yp, ys, pm, w_group, scale, w_bp, w_ba, w_out), name="mix_bwd_out", grid=(nt,),
        in_specs=[rev(D), rev(D), rev(D), rev(D), rev(D), rev(PW), _ANY, _fixed((1, PW)), _ANY, _ANY, _ANY],
        out_specs=[rev(2 * D), rev(D), rev(D), rev(SBW), rev(PW), rev(PW), _fixed((1, PW))],
        out_shape=[_sds((S, 2 * D), BF16), _sds((S, D), BF16), _sds((S, D), BF16), _sds((S, SBW), BF16),
                   _sds((S, PW), BF16), _sds((S, PW), BF16), _sds((1, PW), F32)],
        scratch_shapes=[pltpu.VMEM((HALO, PW), F32)] + _vmem_like(w_group, w_bp, w_ba, w_out),
        compiler_params=_params(("arbitrary",), 48), exchange=exchange)


def _mix_bwd_in(dh, h, gain, pieces, w_in, exchange=None):
    tm = 512
    widths = [p.shape[1] for p in pieces]

    def body(dh_ref, h_ref, g_ref, *rest):
        piece_refs, (w_hbm, dx_ref, dg_ref, w_ref, dp_ref) = rest[:len(pieces)], rest[len(pieces):]
        @pl.when(pl.program_id(0) == 0)
        def _():
            dg_ref[...] = jnp.zeros_like(dg_ref)

        def compute(ready):
            at = 0
            for ref, width in zip(piece_refs, widths):
                dp_ref[:, at:at + width] = ref[...]
                at += width
            du = jnp.zeros((tm, D), F32)
            for j in range(NSH):
                ready(j)
                du = du + _nt(dp_ref[:, j * D:(j + 1) * D], w_ref[j])
            r, hr = _rms(h_ref[...])
            dx, dgain = _rms_bwd(du, hr, r, g_ref[...])
            dx_ref[...] = dh_ref[...] + dx
            dg_ref[...] += dgain

        _staged([[(w_hbm.at[j], w_ref.at[j])] for j in range(NSH)], compute)

    return _call(
        body, (dh, h, gain, *pieces, w_in), name="mix_bwd_in", grid=(S // tm,),
        in_specs=[_rows(tm, D), _rows(tm, D), _fixed((1, D))] + [_rows(tm, w) for w in widths] + [_ANY],
        out_specs=[_rows(tm, D), _fixed((1, D))],
        out_shape=[_sds((S, D), F32), _sds((1, D), F32)],
        scratch_shapes=_vmem_like(w_in) + [pltpu.VMEM((tm, 4 * D), BF16)],
        compiler_params=_params(("arbitrary",), 48), exchange=exchange)


def _wgrad_in(u, pieces):
    dxp, dq, dk, dv, dlg = pieces

    def body(u_ref, dxp_ref, dq_ref, dk_ref, dv_ref, dlg_ref, o_ref):
        j = pl.program_id(0)
        u = u_ref[...]

        def two(left_ref, right_ref):
            o_ref[:, :PW] = _tn(u, left_ref[...]).astype(BF16)
            o_ref[:, PW:] = _tn(u, right_ref[...]).astype(BF16)

        pl.when(j == 0)(lambda: two(dxp_ref, dq_ref))
        pl.when(j == 1)(lambda: two(dk_ref, dv_ref))

        @pl.when(j >= 2)
        def _():
            o_ref[...] = _tn(u, dlg_ref[...]).astype(BF16)

    whole = lambda width: pl.BlockSpec((S, width), lambda j: (0, 0))
    return _call(
        body, (u, dxp, dq, dk, dv, dlg), name="wgrad_in", grid=(NSH,),
        in_specs=[whole(D), whole(PW), whole(SBW), whole(SBW), whole(SBW),
                  pl.BlockSpec((S, D), lambda j: (0, jnp.maximum(j - 2, 0)))],
        out_specs=[pl.BlockSpec((None, D, D), lambda j: (j, 0, 0))], out_shape=[_sds((NSH, D, D), BF16)],
        compiler_params=_params(("arbitrary",), 56))[0]


def _wgrad(a, b, nblk, ti, name, out_dtype=BF16, exchange=None, after=()):
    ka, n = a.shape[1], b.shape[1]
    ns = n // nblk

    def body(a_ref, b_ref, o_ref):
        o_ref[...] = _tn(a_ref[...].astype(BF16), b_ref[...].astype(BF16)).astype(out_dtype)

    res = _call(
        body, (a, b), name=name, grid=(nblk, ka // ti),
        in_specs=[pl.BlockSpec((S, ti), lambda j, i: (0, i)), pl.BlockSpec((S, ns), lambda j, i: (0, j))],
        out_specs=[pl.BlockSpec((None, ti, ns), lambda j, i: (j, i, 0))],
        out_shape=[_sds((nblk, ka, ns), out_dtype)],
        compiler_params=_params(("arbitrary", "arbitrary"), 56), exchange=exchange, after=after)
    return res[0] if exchange is None else (res[0][0], res[1])


def _wgrad_branches(p, dyp, o_sb, dys, pm, dyg):
    cols = D // NSH

    def body(p_ref, dyp_ref, o_ref, dys_ref, pm_ref, dyg_ref, gbp_ref, gba_ref, gg_ref):
        gbp_ref[...] = _tn(p_ref[...], dyp_ref[...]).astype(BF16)
        gba_ref[...] = _tn(o_ref[...], dys_ref[...]).astype(BF16)
        gg_ref[...] = _tn(pm_ref[...], dyg_ref[...])

    whole = lambda width: pl.BlockSpec((S, width), lambda j: (0, 0))
    col = lambda width: pl.BlockSpec((S, width), lambda j: (0, j))
    return _call(
        body, (p, dyp, o_sb, dys, pm, dyg), name="wgrad_branches", grid=(NSH,),
        in_specs=[whole(PW), col(cols), whole(SBW), col(cols), col(PG), col(PG)],
        out_specs=[pl.BlockSpec((None, PW, cols), lambda j: (j, 0, 0)),
                   pl.BlockSpec((None, SBW, cols), lambda j: (j, 0, 0)),
                   pl.BlockSpec((None, PG, PG), lambda j: (j, 0, 0))],
        out_shape=[_sds((NSH, PW, cols), BF16), _sds((NSH, SBW, cols), BF16), _sds((NSH, PG, PG), F32)],
        compiler_params=_params(("arbitrary",), 40))


def _place():
    x, y, c = lax.axis_index("x"), lax.axis_index("y"), lax.axis_index("c")
    chips = [(1 - x, y), (x, 1 - y), (1 - x, 1 - y)]
    return x, y, c, chips


def _remote(src, dst, ssem, rsem, dev):
    return pltpu.make_async_remote_copy(src_ref=src, dst_ref=dst, send_sem=ssem, recv_sem=rsem,
                                        device_id=dev, device_id_type=MESH)


def _cast_into_block(ws, me_idx, name):
    steps = 4
    shapes = [(w.shape[0] // steps, w.shape[1]) for w in ws]

    def body(me_ref, *refs):
        for w_ref, o_ref in zip(refs[:len(ws)], refs[len(ws):]):
            o_ref[...] = w_ref[...].astype(BF16)

    return pl.pallas_call(
        body, name=name, out_shape=[_sds((NSH,) + w.shape, BF16) for w in ws],
        grid_spec=pltpu.PrefetchScalarGridSpec(
            num_scalar_prefetch=1, grid=(steps,),
            in_specs=[pl.BlockSpec((r, c), lambda s, me: (s, 0)) for r, c in shapes],
            out_specs=[pl.BlockSpec((None, r, c), lambda s, me: (me[0], s, 0)) for r, c in shapes]),
        compiler_params=_params(("arbitrary",), 32),
    )(me_idx, *ws)


def _ex_gather(bufs):
    n = len(bufs)
    per = 8

    def plan(outs, ssem, rsem, w):
        x, y, c, _ = _place()
        sib, nbr_x, nbr_y = (x, y, 1 - c), (1 - x, y, c), (x, 1 - y, c)
        half = outs[w].shape[1] // 2
        quarter = half // 2
        sem = lambda k: (ssem.at[per * w + k], rsem.at[per * w + k])
        rows = lambda blk, start, size: outs[w].at[blk, pl.ds(start, size)]
        mine = rows(2 * x + y, c * half, half)
        from_x = rows(2 * (1 - x) + y, c * half, half)
        from_y = rows(2 * x + (1 - y), c * half, half)
        diag = 2 * (1 - x) + (1 - y)
        pass_y = rows(2 * (1 - x) + y, c * half, quarter)
        pass_x = rows(2 * x + (1 - y), c * half + quarter, quarter)
        diag_0, diag_1 = rows(diag, c * half, quarter), rows(diag, c * half + quarter, quarter)
        first = [_remote(mine, mine, *sem(0), nbr_x), _remote(mine, mine, *sem(1), nbr_y)]
        arrivals = [
            (_remote(from_x, from_x, *sem(0), nbr_x),
             [_remote(pass_y, pass_y, *sem(2), nbr_y), _remote(from_x, from_x, *sem(4), sib)]),
            (_remote(from_y, from_y, *sem(1), nbr_y),
             [_remote(pass_x, pass_x, *sem(3), nbr_x), _remote(from_y, from_y, *sem(5), sib)]),
            (_remote(diag_0, diag_0, *sem(2), nbr_y), [_remote(diag_0, diag_0, *sem(6), sib)]),
            (_remote(diag_1, diag_1, *sem(3), nbr_x), [_remote(diag_1, diag_1, *sem(7), sib)]),
        ]
        other = (1 - c) * half
        from_sibling = [
            _remote(rows(2 * (1 - x) + y, other, half), rows(2 * (1 - x) + y, other, half), *sem(4), sib),
            _remote(rows(2 * x + (1 - y), other, half), rows(2 * x + (1 - y), other, half), *sem(5), sib),
            _remote(rows(diag, other, quarter), rows(diag, other, quarter), *sem(6), sib),
            _remote(rows(diag, other + quarter, quarter), rows(diag, other + quarter, quarter), *sem(7), sib),
        ]
        return first, arrivals, from_sibling

    def start(ins, outs, ssem, rsem):
        x, y, c, _ = _place()
        for w in range(n):
            half = outs[w].shape[1] // 2
            mine = outs[w].at[2 * x + y, pl.ds(c * half, half)]
            _remote(mine, mine, ssem.at[per * w], rsem.at[per * w], (1 - x, y, c)).start()
            _remote(mine, mine, ssem.at[per * w + 1], rsem.at[per * w + 1], (x, 1 - y, c)).start()

    def finish(ins, outs, ssem, rsem):
        plans = [plan(outs, ssem, rsem, w) for w in range(n)]
        started = []
        for direct in (True, False):
            for first, arrivals, _ in plans:
                for arrived, onward in (arrivals[:2] if direct else arrivals[2:]):
                    arrived.wait_recv()
                    for cp in onward:
                        cp.start()
                    started += onward
        for first, _, from_sibling in plans:
            for cp in from_sibling:
                cp.wait_recv()
            started += first
        for cp in started:
            cp.wait_send()

    return Exchange(bufs, [_sds(b.shape, b.dtype) for b in bufs], {w: w for w in range(n)}, per * n, start, finish)


def _ex_gather_direct(bufs):
    n = len(bufs)

    def copies(outs, ssem, rsem, only_first=False):
        x, y, c, chips = _place()
        me, sib = 2 * x + y, (x, y, 1 - c)
        first, relay, last = [], [], []
        for w in range(n):
            half = outs[w].shape[1] // 2
            mine = outs[w].at[me, pl.ds(c * half, half)]
            for k, (px, py) in enumerate(chips):
                sems = (ssem.at[6 * w + k], rsem.at[6 * w + k])
                sib_sems = (ssem.at[6 * w + 3 + k], rsem.at[6 * w + 3 + k])
                first.append(_remote(mine, mine, *sems, (px, py, c)))
                if only_first:
                    continue
                got = outs[w].at[2 * px + py, pl.ds(c * half, half)]
                relay.append((_remote(got, got, *sems, (px, py, c)), _remote(got, got, *sib_sems, sib)))
                theirs = outs[w].at[2 * px + py, pl.ds((1 - c) * half, half)]
                last.append(_remote(theirs, theirs, *sib_sems, sib))
        return first, relay, last

    def start(ins, outs, ssem, rsem):
        for cp in copies(outs, ssem, rsem, only_first=True)[0]:
            cp.start()

    def finish(ins, outs, ssem, rsem):
        first, relay, last = copies(outs, ssem, rsem)
        for arrived, onward in relay:
            arrived.wait_recv()
            onward.start()
        for cp in last:
            cp.wait_recv()
        for cp in first:
            cp.wait_send()
        for _, onward in relay:
            onward.wait_send()

    return Exchange(bufs, [_sds(b.shape, b.dtype) for b in bufs], {w: w for w in range(n)}, 6 * n, start, finish)


def _simple_exchange(arrays, landing, aliases, make_copies, sibling_only=False):
    def start(ins, outs, ssem, rsem):
        for cp, _ in make_copies(ins, outs, ssem, rsem, False):
            cp.start()

    def finish(ins, outs, ssem, rsem):
        cps = make_copies(ins, outs, ssem, rsem, True)
        for _, landed in cps:
            landed.wait_recv()
        for cp, _ in cps:
            cp.wait_send()

    return Exchange(arrays, landing, aliases, len(arrays) * 3, start, finish, sibling_only)


def _ex_pair_swap(grads):
    def make(ins, outs, ssem, rsem, landing):
        x, y, c, _ = _place()
        cps = [_remote(ins[w].at[:, 1 - c], outs[w], ssem.at[w], rsem.at[w], (x, y, 1 - c))
               for w in range(len(grads))]
        return [(cp, cp) for cp in cps]

    return _simple_exchange(grads, [_sds((NSH,) + g.shape[2:], g.dtype) for g in grads], {}, make, True)


def _ex_relay(bufs):
    def make(ins, outs, ssem, rsem, landing):
        x, y, c, chips = _place()
        sib = (x, y, 1 - c)
        out = []
        for w in range(len(bufs)):
            half = outs[w].shape[1] // 2
            for k, (px, py) in enumerate(chips):
                sems = (ssem.at[3 * w + k], rsem.at[3 * w + k])
                have = outs[w].at[2 * px + py, pl.ds(c * half, half)]
                miss = outs[w].at[2 * px + py, pl.ds((1 - c) * half, half)]
                out.append((_remote(have, have, *sems, sib), _remote(miss, miss, *sems, sib) if landing else None))
        return out

    return _simple_exchange(bufs, [_sds(b.shape, b.dtype) for b in bufs], {w: w for w in range(len(bufs))}, make, True)


def _ex_share(bufs):
    def make(ins, outs, ssem, rsem, landing):
        x, y, c, _ = _place()
        sib = (x, y, 1 - c)
        return [(_remote(outs[w].at[c], outs[w].at[c], ssem.at[w], rsem.at[w], sib),
                 _remote(outs[w].at[1 - c], outs[w].at[1 - c], ssem.at[w], rsem.at[w], sib) if landing else None)
                for w in range(len(bufs))]

    return _simple_exchange(bufs, [_sds(b.shape, b.dtype) for b in bufs], {w: w for w in range(len(bufs))}, make, True)


def _small_copies(slots, ssems, rsems, sending):
    x, y, c, _ = _place()
    out = []
    for m in range(1, 8):
        px, py, pc = x ^ (m >> 2), y ^ ((m >> 1) & 1), c ^ (m & 1)
        slot = slots.at[4 * x + 2 * y + c if sending else 4 * px + 2 * py + pc]
        out.append(_remote(slot, slot, ssems[m - 1], rsems[m - 1], (px, py, pc)))
    return out


def _small_gather_start(slots, name, after=()):
    at = 1 + len(after)

    def body(*refs):
        for cp in _small_copies(refs[0], refs[at:at + 7], refs[at + 7:at + 14], True):
            cp.start()
        refs[-1][...] = jnp.zeros_like(refs[-1])

    outs = pl.pallas_call(
        body, name=name,
        out_shape=([pltpu.SemaphoreType.DMA(())] * 14 + [pltpu.HBM(slots.shape, slots.dtype)]
                   + [jax.ShapeDtypeStruct((8, 128), F32)]),
        in_specs=[_HBM] + [_ANY] * len(after), out_specs=[_SEM] * 14 + [_HBM, _VM], input_output_aliases={0: 14},
        compiler_params=pltpu.CompilerParams(has_side_effects=_EFFECT),
    )(*_in_hbm([slots]), *after)
    return outs[:14], outs[14], outs[15]


def _small_gather_wait(sems, slots, after, name):
    def body(*refs):
        for cp in _small_copies(refs[0], refs[1:8], refs[8:15], True):
            cp.wait_send()
        for cp in _small_copies(refs[0], refs[1:8], refs[8:15], False):
            cp.wait_recv()

    return pl.pallas_call(
        body, name=name, out_shape=pltpu.HBM(slots.shape, slots.dtype),
        in_specs=[_HBM] + [_SEM] * 14 + [_ANY] * len(after), out_specs=_HBM, input_output_aliases={0: 0},
        compiler_params=pltpu.CompilerParams(has_side_effects=_EFFECT),
    )(slots, *sems, *after)


def _pair_sum(grads, gots, c_idx, name):
    n = len(grads)

    def body(c_ref, *refs):
        for a_ref, b_ref, o_ref in zip(refs[:n], refs[n:2 * n], refs[2 * n:]):
            o_ref[...] = (a_ref[...].astype(F32) + b_ref[...].astype(F32)).astype(BF16)

    halves = [g.shape[2:] for g in grads]
    return list(pl.pallas_call(
        body, name=name, out_shape=[_sds((NSH,) + h, BF16) for h in halves],
        grid_spec=pltpu.PrefetchScalarGridSpec(
            num_scalar_prefetch=1, grid=(NSH,),
            in_specs=[pl.BlockSpec((None, None) + h, lambda j, c: (j, c[0], 0, 0)) for h in halves]
            + [pl.BlockSpec((None,) + h, lambda j, c: (j, 0, 0)) for h in halves],
            out_specs=[pl.BlockSpec((None,) + h, lambda j, c: (j, 0, 0)) for h in halves]),
        compiler_params=_params(("arbitrary",), 40),
    )(c_idx, *_in_hbm(list(grads) + list(gots))))


def _chip_sum(owns, gots, place, name):
    n = len(owns)

    def body(place_ref, *refs):
        for own_ref, got_ref, o_ref in zip(refs[:n], refs[n:2 * n], refs[2 * n:]):
            acc = own_ref[...].astype(F32)
            for k in range(3):
                acc = acc + got_ref[k].astype(F32)
            o_ref[...] = acc

    shapes = [(o.shape[1] // 2, o.shape[2]) for o in owns]
    return list(pl.pallas_call(
        body, name=name, out_shape=[_sds((2, 2 * r, c), F32) for r, c in shapes],
        grid_spec=pltpu.PrefetchScalarGridSpec(
            num_scalar_prefetch=1, grid=(2,),
            in_specs=[pl.BlockSpec((None, r, c), lambda s, p: (p[0], s, 0)) for r, c in shapes]
            + [pl.BlockSpec((3, r, c), lambda s, p: (0, s, 0)) for r, c in shapes],
            out_specs=[pl.BlockSpec((None, r, c), lambda s, p: (p[1], s, 0)) for r, c in shapes]),
        compiler_params=_params(("arbitrary",), 40),
    )(place, *_in_hbm(list(owns) + list(gots))))


def _adamw_math(w, g, m, v):
    m = B1 * m + (1.0 - B1) * g
    v = B2 * v + (1.0 - B2) * (g * g)
    m_hat = m / (1.0 - B1 ** STEP)
    v_hat = v / (1.0 - B2 ** STEP)
    return -LR * (m_hat / (jnp.sqrt(v_hat) + AEPS) + WD * w), m, v


def _adamw(ws, gs, ms, vs, name, after=()):
    n, steps = len(ws), 4

    def body(*refs):
        ins, outs = refs[:4 * n], refs[4 * n:]
        for i in range(n):
            w_ref, g_ref, m_ref, v_ref = ins[4 * i:4 * i + 4]
            go_ref, d_ref, nm_ref, nv_ref = outs[4 * i:4 * i + 4]
            g = g_ref[...]
            go_ref[...] = g
            d_ref[...], nm_ref[...], nv_ref[...] = _adamw_math(w_ref[...], g, m_ref[...], v_ref[...])

    args, specs, shapes, free = [], [], [], []
    for i, (w, g, m, v) in enumerate(zip(ws, gs, ms, vs)):
        args += [w, g, m, v]
        specs += [pl.BlockSpec((w.shape[0] // steps, w.shape[1]), lambda r: (r, 0))] * 4
        shapes += [_sds(w.shape, F32)] * 4
        free += [4 * i, 4 * i + 2, 4 * i + 3]
    outs = _call(body, args, name=name, grid=(steps,), out_shape=shapes, in_specs=specs, out_specs=specs,
                 compiler_params=_params(("arbitrary",), 48), free=tuple(free), after=after)
    return [outs[4 * i:4 * i + 4] for i in range(n)]


def _small_update(gathered, w, m, v, entries):
    rows = w.shape[0]

    def body(ga_ref, w_ref, m_ref, v_ref, *out_refs):
        for j, (first, n) in enumerate(entries):
            mine = slice(first, first + n)
            g = ga_ref[mine, :]
            for dev in range(1, 8):
                g = g + ga_ref[dev * rows + first:dev * rows + first + n, :]
            results = (g,) + _adamw_math(w_ref[mine, :], g, m_ref[mine, :], v_ref[mine, :])
            for i, res in enumerate(results):
                out_refs[i * len(entries) + j][...] = res

    outs = pl.pallas_call(
        body, name="small_update",
        out_shape=[jax.ShapeDtypeStruct((n, 128), F32) for _ in range(4) for _, n in entries],
        in_specs=[_VM] * 4, out_specs=[_VM] * (4 * len(entries)),
    )(gathered, w, m, v)
    return [outs[i * len(entries):(i + 1) * len(entries)] for i in range(4)]


SMALL = ("ffn1_norm", "mix_norm", "ffn2_norm", "final_norm", "pool_scale", "loss", "pool_w_group")
BIG = ("ffn1_w_gate_up", "ffn1_w_down", "w_in", "w_branch_pool", "w_branch_attn", "w_out",
       "ffn2_w_gate_up", "ffn2_w_down")
ORDER = ("ffn1_norm", "ffn1_w_gate_up", "ffn1_w_down", "mix_norm", "w_in", "pool_w_group", "pool_scale",
         "w_branch_pool", "w_branch_attn", "w_out", "ffn2_norm", "ffn2_w_gate_up", "ffn2_w_down", "final_norm")
SMALL_ROWS = 560


def _pack_small(t):
    parts = []
    for k in SMALL:
        rows = t[k].reshape(-1, 128) if k in t else jnp.zeros((1, 128), F32)
        parts.append(jnp.pad(rows, ((0, -rows.shape[0] % 8), (0, 0))))
    packed = jnp.concatenate(parts, axis=0)
    assert packed.shape == (SMALL_ROWS, 128), packed.shape
    return packed


def _small_entries(like):
    out, at = [], 0
    for k in SMALL:
        n = like[k].size // 128 if k in like else 1
        out.append((at, n))
        at += n + (-n % 8)
    return out


def _halves(g):
    return g.reshape(NSH, 2, g.shape[1] // 2, g.shape[2])


def kernel(x, ffn1_norm, ffn1_w_gate_up, ffn1_w_down, mix_norm, w_in, pool_w_group, pool_scale, w_branch_pool, w_branch_attn, w_out, ffn2_norm, ffn2_w_gate_up, ffn2_w_down, final_norm, loss_target, m_ffn1_norm, m_ffn1_w_gate_up, m_ffn1_w_down, m_mix_norm, m_w_in, m_pool_w_group, m_pool_scale, m_w_branch_pool, m_w_branch_attn, m_w_out, m_ffn2_norm, m_ffn2_w_gate_up, m_ffn2_w_down, m_final_norm, v_ffn1_norm, v_ffn1_w_gate_up, v_ffn1_w_down, v_mix_norm, v_w_in, v_pool_w_group, v_pool_scale, v_w_branch_pool, v_w_branch_attn, v_w_out, v_ffn2_norm, v_ffn2_w_gate_up, v_ffn2_w_down, v_final_norm):
    wts = dict(ffn1_norm=ffn1_norm, ffn1_w_gate_up=ffn1_w_gate_up, ffn1_w_down=ffn1_w_down, mix_norm=mix_norm,
               w_in=w_in, pool_w_group=pool_w_group, pool_scale=pool_scale, w_branch_pool=w_branch_pool,
               w_branch_attn=w_branch_attn, w_out=w_out, ffn2_norm=ffn2_norm, ffn2_w_gate_up=ffn2_w_gate_up,
               ffn2_w_down=ffn2_w_down, final_norm=final_norm)
    mom = dict(ffn1_norm=m_ffn1_norm, ffn1_w_gate_up=m_ffn1_w_gate_up, ffn1_w_down=m_ffn1_w_down,
               mix_norm=m_mix_norm, w_in=m_w_in, pool_w_group=m_pool_w_group, pool_scale=m_pool_scale,
               w_branch_pool=m_w_branch_pool, w_branch_attn=m_w_branch_attn, w_out=m_w_out,
               ffn2_norm=m_ffn2_norm, ffn2_w_gate_up=m_ffn2_w_gate_up, ffn2_w_down=m_ffn2_w_down,
               final_norm=m_final_norm)
    var = dict(ffn1_norm=v_ffn1_norm, ffn1_w_gate_up=v_ffn1_w_gate_up, ffn1_w_down=v_ffn1_w_down,
               mix_norm=v_mix_norm, w_in=v_w_in, pool_w_group=v_pool_w_group, pool_scale=v_pool_scale,
               w_branch_pool=v_w_branch_pool, w_branch_attn=v_w_branch_attn, w_out=v_w_out,
               ffn2_norm=v_ffn2_norm, ffn2_w_gate_up=v_ffn2_w_gate_up, ffn2_w_down=v_ffn2_w_down,
               final_norm=v_final_norm)

    c_idx = lax.axis_index("c").astype(jnp.int32).reshape(1)
    me_idx = (2 * lax.axis_index("x") + lax.axis_index("y")).astype(jnp.int32).reshape(1)
    place = jnp.concatenate([me_idx, c_idx])
    x0, tgt = x[0], loss_target[0]
    wgrp = pool_w_group[0].astype(BF16)
    g1, gm, g2, gf = ffn1_norm, mix_norm, ffn2_norm, final_norm.reshape(1, D)
    grad, delta, new_m, new_v = {}, {}, {}, {}

    def pair_sums(keys, parts, got):
        return _pair_sum(parts, got, c_idx, "pair_sum_" + keys[0])

    def chip_sums(keys, chip_parts, owned):
        return _chip_sum(chip_parts, owned, place, "chip_sum_" + keys[0])

    def adamw(keys, after=()):
        outs = _adamw([wts[k][0] for k in keys], [grad[k][0] for k in keys], [mom[k][0] for k in keys],
                      [var[k][0] for k in keys], "adamw_" + keys[0], after=after)
        for k, res in zip(keys, outs):
            grad[k], delta[k], new_m[k], new_v[k] = (o.reshape(wts[k].shape) for o in res)

    first, late = ("ffn1_w_gate_up", "ffn1_w_down"), ("w_branch_pool", "w_branch_attn", "w_out",
                                                       "ffn2_w_gate_up", "ffn2_w_down")
    gathered = first + ("w_in",) + late
    own = dict(zip(gathered, _cast_into_block([wts[k][0] for k in gathered], me_idx, "cast_" + first[0])))
    full = dict(zip(first, _exchange_alone(_ex_gather([own[k] for k in first]), "gather_ffn1")))
    wgu1, wd1 = full["ffn1_w_gate_up"], full["ffn1_w_down"].reshape(DFF, D)
    (h1, n1, gu1, a1), (win,) = _ffn_fwd(x0, g1, wgu1, wd1, "ffn1_fwd", exchange=_ex_gather_direct([own["w_in"]]))
    sems_l, thru_l, token_l = _gather_start([own[k_] for k_ in late], [h1], "gather_late_start")
    u, xp, q, k, v, gp, gs = _mix_in(h1, gm, win, after=(token_l,))
    o_sb, ctot = _attn_fwd(q, k, v)
    arrived = _gather_wait(sems_l, thru_l, [o_sb], "gather_late_wait")
    wbp, wba, wout = _exchange_alone(_ex_relay(arrived[:3]), "relay_mix")
    wout = wout.reshape(D, D)
    (h2, pm, p, yp, ys, mm), (wgu2, wd2) = _mix_out(h1, xp, o_sb, gp, gs, wgrp, pool_scale, wbp, wba, wout,
                                                    exchange=_ex_relay(arrived[3:]))
    wd2 = wd2.reshape(DFF, D)
    dh2, dgu3, d_g2, loss_row, d_gf, n3, a3, dh3 = _ffn_last(h2, g2, wgu2, wd2, tgt, gf, "ffn2")

    def grad_gate_up(n, dgu, name, exchange=None):
        res = _wgrad(n, dgu, NSH, D, name, exchange=exchange)
        return [_halves(res)] if exchange is None else ([_halves(res[0])], res[1])

    def grad_down(a, dh, name, exchange=None):
        res = _wgrad(a, dh, 1, FFS, name, exchange=exchange)
        halves = lambda g: [_halves(g.reshape(NSH, DFF // NSH, D))]
        return halves(res) if exchange is None else (halves(res[0]), res[1])

    k_gu2, k_d2, k_gu1, k_d1, k_in = (("ffn2_w_gate_up",), ("ffn2_w_down",), ("ffn1_w_gate_up",),
                                      ("ffn1_w_down",), ("w_in",))
    pa = grad_gate_up(n3, dgu3, "wgrad_gu2") + grad_down(a3, dh3, "wgrad_d2")
    (dlg, dyp, dys, do_sb, dyg, dxp, d_scale), got_a = _mix_bwd_out(
        dh2, gp, gs, yp, ys, pm, wgrp, pool_scale, wbp, wba, wout, exchange=_ex_pair_swap(pa))
    chip_a = pair_sums(k_gu2 + k_d2, pa, got_a)
    kb = ("w_out", "w_branch_pool", "w_branch_attn")
    g_bp, g_ba, d_group = _wgrad_branches(p, dyp, o_sb, dys, pm, dyg)
    pb = [_halves(_wgrad(mm, dh2, 1, D, "wgrad_out").reshape(NSH, D // NSH, D)), _halves(g_bp), _halves(g_ba)]
    k_a, k_in = k_gu2 + k_d2, k_in + kb
    sems_a, thru_a, token_a = _scatter_start(chip_a, "scatter_a_start")
    dq, dk, dv = _attn_bwd(q, k, v, do_sb, ctot, after=(token_a,))
    chip_a, owned_a = _scatter_wait(sems_a, thru_a, [dq], "scatter_a_wait")
    halves_a = chip_sums(k_a, chip_a, owned_a)
    dproj = (dxp, dq, dk, dv, dlg)
    (dh1, d_gm), both_a = _mix_bwd_in(dh2, h1, gm, dproj, win, exchange=_ex_share(halves_a))
    for i, k_ in enumerate(k_a):
        grad[k_] = both_a[i].reshape(wts[k_].shape)

    p_in = [_halves(_wgrad_in(u, dproj))] + pb
    p_d1, got_in = grad_down(a1, dh1, "wgrad_d1", exchange=_ex_pair_swap(p_in))
    sems_in, thru_in, token_in = _scatter_start(pair_sums(k_in, p_in, got_in), "scatter_in_start")
    dgu1, got_d1 = _ffn_bwd_act(dh1, gu1, wd1, "ffn1_bwd_act", exchange=_ex_pair_swap(p_d1), after=(token_in,))
    sems_d1, thru_d1, token_d1 = _scatter_start(pair_sums(k_d1, p_d1, got_d1), "scatter_d1_start")
    p_gu1 = [_halves(_wgrad(n1, dgu1, NSH, D, "wgrad_gu1", after=(token_in, token_d1)))]
    sems_w, thru_w, token_w = _swap_start(p_gu1, "swap_gu1_start")
    chip_in, owned_in = _scatter_wait(sems_in, thru_in, [token_w], "scatter_in_wait")
    chip_d1, owned_d1 = _scatter_wait(sems_d1, thru_d1, [token_w], "scatter_d1_wait")
    halves_in = chip_sums(k_in, chip_in, owned_in)
    p_gu1, got_gu1 = _swap_wait(sems_w, thru_w, halves_in, "swap_gu1_wait")
    sems, thru, token = _scatter_start(pair_sums(k_gu1, p_gu1, got_gu1), "scatter_gu1_start")
    sems_h, thru_h, token_h = _share_start(halves_in, [token], "share_in_start")
    adamw(k_a, after=(token_h,))
    landed = _share_wait(sems_h, thru_h, [delta[k_a[0]]], "share_in_wait")
    for i, k_ in enumerate(k_in):
        grad[k_] = landed[i].reshape(wts[k_].shape)
    adamw(k_in)
    dx, d_g1 = _ffn_bwd_in(dh1, x0, g1, dgu1, wgu1, "ffn1_bwd_in", after=(token,))
    small_g = dict(ffn1_norm=d_g1, mix_norm=d_gm, ffn2_norm=d_g2, final_norm=d_gf, pool_scale=d_scale,
                   pool_w_group=d_group, loss=loss_row)
    dev = 4 * lax.axis_index("x") + 2 * lax.axis_index("y") + lax.axis_index("c")
    slots = lax.dynamic_update_slice(jnp.zeros((8, SMALL_ROWS, 128), F32), _pack_small(small_g)[None], (dev, 0, 0))
    chip_gu1, owned_gu1 = _scatter_wait(sems, thru, [dx] + [delta[k_] for k_ in k_a + k_in], "scatter_gu1_wait")
    halves_last = chip_sums(k_d1 + k_gu1, chip_d1 + chip_gu1, owned_d1 + owned_gu1)
    sems_l, thru_l, token_l = _share_start(halves_last, [], "share_last_start")
    sems_s, slots, token_s = _small_gather_start(slots, "small_gather_start", after=(token_l,))
    both = _share_wait(sems_l, thru_l, [token_s], "share_last_wait")
    grad["ffn1_w_down"] = both[0].reshape(ffn1_w_down.shape)
    grad["ffn1_w_gate_up"] = both[1].reshape(ffn1_w_gate_up.shape)
    adamw(k_d1 + k_gu1, after=(token_s,))
    gathered = _small_gather_wait(sems_s, slots, [delta[k_] for k_ in k_d1 + k_gu1], "small_gather_wait")
    gathered = gathered.reshape(8 * SMALL_ROWS, 128)
    results = _small_update(gathered, _pack_small(wts), _pack_small(mom), _pack_small(var), _small_entries(wts))
    for dst, entries in zip((grad, delta, new_m, new_v), results):
        for k_, rows in zip(SMALL, entries):
            if k_ in wts:
                dst[k_] = rows.reshape(wts[k_].shape)
            elif dst is grad:
                loss = rows[0, 0]
    return (loss, dx[None], *[grad[k_] for k_ in ORDER], *[delta[k_] for k_ in ORDER],
            *[new_m[k_] for k_ in ORDER], *[new_v[k_] for k_ in ORDER])
```

```python
import dataclasses
import functools

import jax
import jax.numpy as jnp
from jax import lax
from jax.experimental import pallas as pl
from jax.experimental.pallas import tpu as pltpu

F32 = jnp.float32
BF16 = jnp.bfloat16

S = 2048
D = 1024
DFF = 2816
FFS = 2 * DFF // 4
NSH = 4
PW = 512
PG = 128
POOL_WINDOWS = (2, 4, 8, 16)
HALO = 16
SBW = 512
DH = 64
EPS = 1e-6
SCALE = 0.125
LOG2E = 1.4426950408889634
TA = 256
QB = 2
MIB = 1024 * 1024

LR, B1, B2, AEPS, WD, STEP = 0.001, 0.9, 0.999, 1e-08, 0.01, 10

_VM = pl.BlockSpec(memory_space=pltpu.VMEM)
_ANY = pl.BlockSpec(memory_space=pl.ANY)
MESH = pl.DeviceIdType.MESH
SIBLING_PAIR_ID = 1


def _nn(a, b):
    return jnp.dot(a, b, preferred_element_type=F32)


def _nt(a, b):
    return lax.dot_general(a, b, (((1,), (1,)), ((), ())), preferred_element_type=F32)


def _tn(a, b):
    return lax.dot_general(a, b, (((0,), (0,)), ((), ())), preferred_element_type=F32)


def _params(sem, vmem_mib):
    return pltpu.CompilerParams(dimension_semantics=sem, vmem_limit_bytes=vmem_mib * MIB)


def _rows(tm, width):
    return pl.BlockSpec((tm, width), lambda i: (i, 0))


def _fixed(shape):
    return pl.BlockSpec(shape, lambda *_: (0,) * len(shape))


def _sds(shape, dtype):
    return pltpu.HBM(shape, dtype)


def _in_hbm(args):
    return [pltpu.with_memory_space_constraint(a, pltpu.HBM) for a in args]


def _stage(pairs):
    pieces = 4

    def copy_all(sems):
        copies = []
        for src, dst in pairs:
            step = src.shape[0] // pieces
            for p in range(pieces):
                part = pl.ds(p * step, step)
                if len(dst.shape) == len(src.shape):
                    piece = (src.at[part], dst.at[part])
                else:
                    piece = (src.at[p], dst.at[:, pl.ds(p * src.shape[2], src.shape[2])])
                copies.append(pltpu.make_async_copy(*piece, sems.at[len(copies)]))
        for c in copies:
            c.start()
        for c in copies:
            c.wait()

    @pl.when(pl.program_id(0) == 0)
    def _():
        pl.run_scoped(copy_all, pltpu.SemaphoreType.DMA((pieces * len(pairs),)))


def _vmem_like(*arrays):
    return [pltpu.VMEM(a.shape, a.dtype) for a in arrays]


def _vmem_wide(w):
    return pltpu.VMEM((w.shape[1], w.shape[0] * w.shape[2]), w.dtype)


FF_CHUNKS = ((0, 1536), (1536, DFF - 1536))


def _wide_columns(src, dst, c0, cn):
    width, out = src.shape[2], []
    for p in range(src.shape[0]):
        lo, hi = max(c0, p * width), min(c0 + cn, (p + 1) * width)
        if lo < hi:
            out.append((src.at[p, :, pl.ds(lo - p * width, hi - lo)], dst.at[:, pl.ds(lo, hi - lo)]))
    return out


def _staged(groups, compute):
    first = pl.program_id(0) == 0

    def with_copies(sems):
        copies = []
        for group in groups:
            base = sum(len(g) for g in copies)
            copies.append([pltpu.make_async_copy(s, d, sems.at[base + i]) for i, (s, d) in enumerate(group)])
        for group in copies:
            for c in group:
                c.start()

        def ready(k):
            for c in copies[k]:
                c.wait()

        compute(ready)

    @pl.when(first)
    def _():
        pl.run_scoped(with_copies, pltpu.SemaphoreType.DMA((sum(len(g) for g in groups),)))

    @pl.when(jnp.logical_not(first))
    def _():
        compute(lambda k: None)


class Exchange:
    def __init__(self, arrays, landing, aliases, n_sems, start, finish, sibling_only=False):
        self.arrays, self.landing, self.aliases, self.n_sems = list(arrays), list(landing), dict(aliases), n_sems
        self.start, self.finish = start, finish
        self.sibling_only = sibling_only

    def enter(self):
        if self.sibling_only:
            barrier = pltpu.get_barrier_semaphore()
            sibling = (lax.axis_index("x"), lax.axis_index("y"), 1 - lax.axis_index("c"))
            pl.semaphore_signal(barrier, inc=1, device_id=sibling, device_id_type=MESH)
            pl.semaphore_wait(barrier, 1)

    def params(self, compiler_params=None):
        kw = dict(collective_id=SIBLING_PAIR_ID) if self.sibling_only else {}
        if compiler_params is None:
            return pltpu.CompilerParams(**kw)
        return dataclasses.replace(compiler_params, **kw)


def _call(body, args, *, name, grid, in_specs, out_specs, out_shape, scratch_shapes=(), compiler_params=None,
          exchange=None, free=(), after=()):
    args = [a if i in free else pltpu.with_memory_space_constraint(a, pltpu.HBM) for i, a in enumerate(args)]
    if exchange is None:
        n_in = len(in_specs)

        def plain(*refs):
            body(*refs[:n_in], *refs[n_in + len(after):])

        return pl.pallas_call(plain, name=name, grid=grid, in_specs=list(in_specs) + [_ANY] * len(after),
                              out_specs=out_specs, out_shape=out_shape, scratch_shapes=list(scratch_shapes),
                              compiler_params=compiler_params)(*args, *after)
    ex = exchange
    n_in, n_out, n_scr = len(in_specs), len(out_specs), len(scratch_shapes)
    na, nl = len(ex.arrays), len(ex.landing)

    def hosted(*refs):
        at = [0]

        def take(n):
            at[0] += n
            return refs[at[0] - n:at[0]]

        k_in, _, e_in, k_out, e_out, k_scr = take(n_in), take(len(after)), take(na), take(n_out), take(nl), take(n_scr)
        ssem, rsem = take(2)
        ids = [pl.program_id(a) for a in range(len(grid))]
        first = functools.reduce(jnp.logical_and, [i == 0 for i in ids])
        last = functools.reduce(jnp.logical_and, [i == g - 1 for i, g in zip(ids, grid)])

        @pl.when(first)
        def _():
            ex.enter()
            ex.start(e_in, e_out, ssem, rsem)

        body(*k_in, *k_out, *k_scr)

        @pl.when(last)
        def _():
            ex.finish(e_in, e_out, ssem, rsem)

    outs = pl.pallas_call(
        hosted, name=name, grid=grid,
        in_specs=list(in_specs) + [_ANY] * (len(after) + na), out_specs=list(out_specs) + [_ANY] * nl,
        out_shape=list(out_shape) + ex.landing,
        scratch_shapes=list(scratch_shapes) + [pltpu.SemaphoreType.DMA((ex.n_sems,))] * 2,
        input_output_aliases={n_in + len(after) + i: n_out + j for i, j in ex.aliases.items()},
        compiler_params=ex.params(compiler_params),
    )(*args, *after, *_in_hbm(ex.arrays))
    return outs[:n_out], outs[n_out:]


def _exchange_alone(ex, name, after=()):
    na, nl = len(ex.arrays), len(ex.landing)

    def body(*refs):
        outs = refs[na + len(after):na + len(after) + nl]
        ex.enter()
        ex.start(refs[:na], outs, refs[-2], refs[-1])
        ex.finish(refs[:na], outs, refs[-2], refs[-1])

    return pl.pallas_call(
        body, name=name, in_specs=[_ANY] * (na + len(after)), out_specs=[_ANY] * nl,
        out_shape=ex.landing, scratch_shapes=[pltpu.SemaphoreType.DMA((ex.n_sems,))] * 2,
        input_output_aliases=ex.aliases, compiler_params=ex.params(),
    )(*_in_hbm(ex.arrays), *after)


_HBM = pl.BlockSpec(memory_space=pltpu.HBM)
_SEM = pl.BlockSpec(memory_space=pltpu.SEMAPHORE)
_EFFECT = pltpu.SideEffectType.DATAFLOW_SIDE_EFFECTING


def _scatter_copies(srcs, lands, ssems, rsems):
    x, y, c, chips = _place()
    return [_remote(srcs[w].at[2 * px + py], lands[w].at[k], ssems[3 * w + k], rsems[3 * w + k], (px, py, c))
            for w in range(len(srcs)) for k, (px, py) in enumerate(chips)]


def _scatter_start(parts, name):
    parts = list(parts)
    n, ncp = len(parts), 3 * len(parts)
    lands = [lax.empty((3,) + p.shape[1:], p.dtype) for p in parts]

    def body(*refs):
        srcs, land_refs = refs[:n], refs[n:2 * n]
        ssems, rsems = refs[2 * n:2 * n + ncp], refs[2 * n + ncp:2 * n + 2 * ncp]
        for cp in _scatter_copies(srcs, land_refs, ssems, rsems):
            cp.start()
        token = refs[-1]
        token[...] = jnp.zeros_like(token)

    outs = pl.pallas_call(
        body, name=name,
        out_shape=([pltpu.SemaphoreType.DMA(())] * (2 * ncp) + [pltpu.HBM(a.shape, a.dtype) for a in parts + lands]
                   + [jax.ShapeDtypeStruct((8, 128), F32)]),
        in_specs=[_HBM] * (2 * n), out_specs=[_SEM] * (2 * ncp) + [_HBM] * (2 * n) + [_VM],
        input_output_aliases={i: 2 * ncp + i for i in range(2 * n)},
        compiler_params=pltpu.CompilerParams(has_side_effects=_EFFECT),
    )(*_in_hbm(parts), *_in_hbm(lands))
    sems, thru, token = outs[:2 * ncp], outs[2 * ncp:2 * ncp + 2 * n], outs[-1]
    return sems, thru, token


def _scatter_wait(sems, thru, after, name):
    n = len(thru) // 2
    ncp = 3 * n

    def body(*refs):
        srcs, land_refs = refs[:n], refs[n:2 * n]
        ssems, rsems = refs[2 * n:2 * n + ncp], refs[2 * n + ncp:2 * n + 2 * ncp]
        for cp in _scatter_copies(srcs, land_refs, ssems, rsems):
            cp.wait_send()
            cp.wait_recv()

    outs = pl.pallas_call(
        body, name=name, out_shape=[pltpu.HBM(a.shape, a.dtype) for a in thru],
        in_specs=[_HBM] * (2 * n) + [_SEM] * (2 * ncp) + [_ANY] * len(after), out_specs=[_HBM] * (2 * n),
        input_output_aliases={i: i for i in range(2 * n)},
        compiler_params=pltpu.CompilerParams(has_side_effects=_EFFECT),
    )(*thru, *sems, *after)
    return outs[:n], outs[n:]


def _swap_copies(srcs, lands, ssems, rsems):
    x, y, c, _ = _place()
    return [_remote(srcs[w].at[:, 1 - c], lands[w], ssems[w], rsems[w], (x, y, 1 - c)) for w in range(len(srcs))]


def _swap_start(grads, name):
    grads = list(grads)
    n = len(grads)
    lands = [lax.empty((NSH,) + g.shape[2:], g.dtype) for g in grads]

    def body(*refs):
        barrier = pltpu.get_barrier_semaphore()
        sibling = (lax.axis_index("x"), lax.axis_index("y"), 1 - lax.axis_index("c"))
        pl.semaphore_signal(barrier, inc=1, device_id=sibling, device_id_type=MESH)
        pl.semaphore_wait(barrier, 1)
        for cp in _swap_copies(refs[:n], refs[n:2 * n], refs[2 * n:3 * n], refs[3 * n:4 * n]):
            cp.start()
        refs[-1][...] = jnp.zeros_like(refs[-1])

    outs = pl.pallas_call(
        body, name=name,
        out_shape=([pltpu.SemaphoreType.DMA(())] * (2 * n) + [pltpu.HBM(a.shape, a.dtype) for a in grads + lands]
                   + [jax.ShapeDtypeStruct((8, 128), F32)]),
        in_specs=[_HBM] * (2 * n), out_specs=[_SEM] * (2 * n) + [_HBM] * (2 * n) + [_VM],
        input_output_aliases={i: 2 * n + i for i in range(2 * n)},
        compiler_params=pltpu.CompilerParams(has_side_effects=_EFFECT, collective_id=SIBLING_PAIR_ID),
    )(*_in_hbm(grads), *_in_hbm(lands))
    return outs[:2 * n], outs[2 * n:4 * n], outs[-1]


def _swap_wait(sems, thru, after, name):
    n = len(thru) // 2

    def body(*refs):
        for cp in _swap_copies(refs[:n], refs[n:2 * n], refs[2 * n:3 * n], refs[3 * n:4 * n]):
            cp.wait_send()
            cp.wait_recv()

    outs = pl.pallas_call(
        body, name=name, out_shape=[pltpu.HBM(a.shape, a.dtype) for a in thru],
        in_specs=[_HBM] * (2 * n) + [_SEM] * (2 * n) + [_ANY] * len(after), out_specs=[_HBM] * (2 * n),
        input_output_aliases={i: i for i in range(2 * n)},
        compiler_params=pltpu.CompilerParams(has_side_effects=_EFFECT),
    )(*thru, *sems, *after)
    return outs[:n], outs[n:]


def _share_copies(bufs, ssems, rsems, sending):
    x, y, c, _ = _place()
    out = []
    for w, ref in enumerate(bufs):
        slot = ref.at[c if sending else 1 - c]
        out.append(_remote(slot, slot, ssems[w], rsems[w], (x, y, 1 - c)))
    return out


def _share_start(bufs, after, name):
    bufs = list(bufs)
    n = len(bufs)

    def body(*refs):
        barrier = pltpu.get_barrier_semaphore()
        sibling = (lax.axis_index("x"), lax.axis_index("y"), 1 - lax.axis_index("c"))
        pl.semaphore_signal(barrier, inc=1, device_id=sibling, device_id_type=MESH)
        pl.semaphore_wait(barrier, 1)
        at = n + len(after)
        for cp in _share_copies(refs[:n], refs[at:at + n], refs[at + n:at + 2 * n], True):
            cp.start()
        refs[-1][...] = jnp.zeros_like(refs[-1])

    outs = pl.pallas_call(
        body, name=name,
        out_shape=([pltpu.SemaphoreType.DMA(())] * (2 * n) + [pltpu.HBM(a.shape, a.dtype) for a in bufs]
                   + [jax.ShapeDtypeStruct((8, 128), F32)]),
        in_specs=[_HBM] * n + [_ANY] * len(after), out_specs=[_SEM] * (2 * n) + [_HBM] * n + [_VM],
        input_output_aliases={i: 2 * n + i for i in range(n)},
        compiler_params=pltpu.CompilerParams(has_side_effects=_EFFECT, collective_id=SIBLING_PAIR_ID),
    )(*_in_hbm(bufs), *after)
    return outs[:2 * n], outs[2 * n:3 * n], outs[-1]


def _share_wait(sems, thru, after, name):
    n = len(thru)

    def body(*refs):
        for cp in _share_copies(refs[:n], refs[n:2 * n], refs[2 * n:3 * n], True):
            cp.wait_send()
        for cp in _share_copies(refs[:n], refs[n:2 * n], refs[2 * n:3 * n], False):
            cp.wait_recv()

    return pl.pallas_call(
        body, name=name, out_shape=[pltpu.HBM(a.shape, a.dtype) for a in thru],
        in_specs=[_HBM] * n + [_SEM] * (2 * n) + [_ANY] * len(after), out_specs=[_HBM] * n,
        input_output_aliases={i: i for i in range(n)},
        compiler_params=pltpu.CompilerParams(has_side_effects=_EFFECT),
    )(*thru, *sems, *after)


def _gather_copies(bufs, ssems, rsems, sending):
    x, y, c, chips = _place()
    out = []
    for w, ref in enumerate(bufs):
        half = ref.shape[1] // 2
        for k, (px, py) in enumerate(chips):
            rows = ref.at[2 * x + y if sending else 2 * px + py, pl.ds(c * half, half)]
            out.append(_remote(rows, rows, ssems[3 * w + k], rsems[3 * w + k], (px, py, c)))
    return out


def _gather_start(bufs, after, name):
    n, ncp = len(bufs), 3 * len(bufs)

    def body(*refs):
        ssems, rsems = refs[n + len(after):n + len(after) + ncp], refs[n + len(after) + ncp:n + len(after) + 2 * ncp]
        for cp in _gather_copies(refs[:n], ssems, rsems, True):
            cp.start()
        token = refs[-1]
        token[...] = jnp.zeros_like(token)

    outs = pl.pallas_call(
        body, name=name,
        out_shape=([pltpu.SemaphoreType.DMA(())] * (2 * ncp) + [pltpu.HBM(a.shape, a.dtype) for a in bufs]
                   + [jax.ShapeDtypeStruct((8, 128), F32)]),
        in_specs=[_HBM] * n + [_ANY] * len(after), out_specs=[_SEM] * (2 * ncp) + [_HBM] * n + [_VM],
        input_output_aliases={i: 2 * ncp + i for i in range(n)},
        compiler_params=pltpu.CompilerParams(has_side_effects=_EFFECT),
    )(*_in_hbm(bufs), *after)
    return outs[:2 * ncp], outs[2 * ncp:2 * ncp + n], outs[-1]


def _gather_wait(sems, thru, after, name):
    n = len(thru)
    ncp = 3 * n

    def body(*refs):
        ssems, rsems = refs[n:n + ncp], refs[n + ncp:n + 2 * ncp]
        for cp in _gather_copies(refs[:n], ssems, rsems, True):
            cp.wait_send()
        for cp in _gather_copies(refs[:n], ssems, rsems, False):
            cp.wait_recv()

    return pl.pallas_call(
        body, name=name, out_shape=[pltpu.HBM(a.shape, a.dtype) for a in thru],
        in_specs=[_HBM] * n + [_SEM] * (2 * ncp) + [_ANY] * len(after), out_specs=[_HBM] * n,
        input_output_aliases={i: i for i in range(n)},
        compiler_params=pltpu.CompilerParams(has_side_effects=_EFFECT),
    )(*thru, *sems, *after)


def _rms(x):
    r = lax.rsqrt(jnp.mean(x * x, axis=-1, keepdims=True) + EPS)
    return r, x * r


def _rms_bwd(dn, xr, r, gain):
    dng = dn * gain
    dx = r * (dng - xr * jnp.mean(dng * xr, axis=-1, keepdims=True))
    return dx, jnp.sum(dn * xr, axis=0, keepdims=True)


def _ffn_weight_groups(wgu_hbm, wgu_ref, wd_hbm, wd_ref):
    groups = []
    for c0, cn in FF_CHUNKS:
        groups += [_wide_columns(wgu_hbm, wgu_ref, c0, cn), _wide_columns(wgu_hbm, wgu_ref, DFF + c0, cn),
                   [(wd_hbm.at[pl.ds(c0, cn)], wd_ref.at[pl.ds(c0, cn)])]]
    return groups


def _ffn_fwd(x, gain, wgu, wd, name, exchange=None):
    tm = 256

    def body(x_ref, g_ref, wgu_hbm, wd_hbm, h_ref, n_ref, gu_ref, a_ref, wgu_ref, wd_ref):
        def compute(ready):
            x = x_ref[...]
            _, xr = _rms(x)
            n = (xr * g_ref[...]).astype(BF16)
            n_ref[...] = n
            acc = jnp.zeros((tm, D), F32)
            for i, (c0, cn) in enumerate(FF_CHUNKS):
                ready(3 * i)
                g = _nn(n, wgu_ref[:, c0:c0 + cn])
                ready(3 * i + 1)
                u = _nn(n, wgu_ref[:, DFF + c0:DFF + c0 + cn])
                gu_ref[:, c0:c0 + cn] = g.astype(BF16)
                gu_ref[:, DFF + c0:DFF + c0 + cn] = u.astype(BF16)
                half_act = (0.5 * (g * jax.nn.sigmoid(g) * u)).astype(BF16)
                a_ref[:, c0:c0 + cn] = half_act
                ready(3 * i + 2)
                acc = acc + _nn(half_act, wd_ref[c0:c0 + cn, :])
            h_ref[...] = x + acc

        _staged(_ffn_weight_groups(wgu_hbm, wgu_ref, wd_hbm, wd_ref), compute)

    return _call(
        body, (x, gain, wgu, wd), name=name, grid=(S // tm,),
        in_specs=[_rows(tm, D), _fixed((1, D)), _ANY, _ANY],
        out_specs=[_rows(tm, D), _rows(tm, D), _rows(tm, 4 * FFS), _rows(tm, DFF)],
        out_shape=[_sds((S, D), F32), _sds((S, D), BF16), _sds((S, 4 * FFS), BF16), _sds((S, DFF), BF16)],
        scratch_shapes=[_vmem_wide(wgu)] + _vmem_like(wd),
        compiler_params=_params(("arbitrary",), 56), exchange=exchange)


def _ffn_last(x, gain, wgu, wd, target, gf, name):
    tm = 256

    def body(x_ref, g_ref, wgu_hbm, wd_hbm, t_ref, gf_ref, dx_ref, dgu_ref, dg_ref, loss_ref, dgf_ref, n_ref,
             a_ref, dh_ref, wgu_ref, wd_ref):
        @pl.when(pl.program_id(0) == 0)
        def _():
            dg_ref[...] = jnp.zeros_like(dg_ref)
            dgf_ref[...] = jnp.zeros_like(dgf_ref)
            loss_ref[...] = jnp.zeros_like(loss_ref)

        def compute():
            x = x_ref[...]
            r0, xr = _rms(x)
            n = (xr * g_ref[...]).astype(BF16)
            n_ref[...] = n
            acc = jnp.zeros((tm, D), F32)
            kept = []
            for c0, cn in FF_CHUNKS:
                g = _nn(n, wgu_ref[:, c0:c0 + cn])
                u = _nn(n, wgu_ref[:, DFF + c0:DFF + c0 + cn])
                kept.append((g.astype(BF16), u.astype(BF16)))
                half_act = (0.5 * (g * jax.nn.sigmoid(g) * u)).astype(BF16)
                a_ref[:, c0:c0 + cn] = half_act
                acc = acc + _nn(half_act, wd_ref[c0:c0 + cn, :])
            h = x + acc
            gf = gf_ref[...]
            r, hr = _rms(h)
            err = hr * gf - t_ref[...]
            dh, dgain_f = _rms_bwd(err * (1.0 / D), hr, r, gf)
            dh_ref[...] = dh
            dhb = dh.astype(BF16)
            dn = jnp.zeros((tm, D), F32)
            for (c0, cn), (gb, ub) in zip(FF_CHUNKS, kept):
                g, u = gb.astype(F32), ub.astype(F32)
                da = 0.5 * _nt(dhb, wd_ref[c0:c0 + cn, :])
                sg = jax.nn.sigmoid(g)
                dgb = (da * u * (sg * (1.0 + g * (1.0 - sg)))).astype(BF16)
                dub = (da * (g * sg)).astype(BF16)
                dgu_ref[:, c0:c0 + cn] = dgb
                dgu_ref[:, DFF + c0:DFF + c0 + cn] = dub
                dn = dn + _nt(dgb, wgu_ref[:, c0:c0 + cn]) + _nt(dub, wgu_ref[:, DFF + c0:DFF + c0 + cn])
            dx, dgain = _rms_bwd(dn, xr, r0, g_ref[...])
            dx_ref[...] = dh + dx
            dg_ref[...] += dgain
            dgf_ref[...] += dgain_f
            loss_ref[...] += jnp.full((1, 128), (0.5 / D) * jnp.sum(err * err), F32)

        _stage([(wgu_hbm, wgu_ref), (wd_hbm, wd_ref)])
        compute()

    return _call(
        body, (x, gain, wgu, wd, target, gf), name=name, grid=(S // tm,),
        in_specs=[_rows(tm, D), _fixed((1, D)), _ANY, _ANY, _rows(tm, D), _fixed((1, D))],
        out_specs=[_rows(tm, D), _rows(tm, 4 * FFS), _fixed((1, D)), _fixed((1, 128)), _fixed((1, D)),
                   _rows(tm, D), _rows(tm, DFF), _rows(tm, D)],
        out_shape=[_sds((S, D), F32), _sds((S, 4 * FFS), BF16), _sds((1, D), F32), _sds((1, 128), F32),
                   _sds((1, D), F32), _sds((S, D), BF16), _sds((S, DFF), BF16), _sds((S, D), F32)],
        scratch_shapes=[_vmem_wide(wgu)] + _vmem_like(wd),
        compiler_params=_params(("arbitrary",), 58), free=(4, 5))


def _ffn_bwd_act(dh, gu, wd, name, exchange=None, after=()):
    tm = 512

    def body(dh_ref, gu_ref, wd_hbm, dgu_ref, wd_ref):
        def compute(ready):
            dhb = dh_ref[...].astype(BF16)
            for i, (c0, cn) in enumerate(FF_CHUNKS):
                g = gu_ref[:, c0:c0 + cn].astype(F32)
                u = gu_ref[:, DFF + c0:DFF + c0 + cn].astype(F32)
                ready(i)
                da = 0.5 * _nt(dhb, wd_ref[c0:c0 + cn, :])
                sg = jax.nn.sigmoid(g)
                dgu_ref[:, c0:c0 + cn] = (da * u * (sg * (1.0 + g * (1.0 - sg)))).astype(BF16)
                dgu_ref[:, DFF + c0:DFF + c0 + cn] = (da * (g * sg)).astype(BF16)

        _staged([[(wd_hbm.at[pl.ds(c0, cn)], wd_ref.at[pl.ds(c0, cn)])] for c0, cn in FF_CHUNKS], compute)

    res = _call(
        body, (dh, gu, wd), name=name, grid=(S // tm,),
        in_specs=[_rows(tm, D), _rows(tm, 4 * FFS), _ANY], out_specs=[_rows(tm, 4 * FFS)],
        out_shape=[_sds((S, 4 * FFS), BF16)], scratch_shapes=_vmem_like(wd),
        compiler_params=_params(("arbitrary",), 56), exchange=exchange, after=after)
    return res[0] if exchange is None else (res[0][0], res[1])


def _ffn_bwd_in(dh, x, gain, dgu, wgu, name, exchange=None, after=()):
    tm = 512

    def body(dh_ref, x_ref, g_ref, dgu_ref, wgu_hbm, dx_ref, dg_ref, wgu_ref):
        chunks = [(half + c0, cn) for half in (0, DFF) for c0, cn in FF_CHUNKS]

        @pl.when(pl.program_id(0) == 0)
        def _():
            dg_ref[...] = jnp.zeros_like(dg_ref)

        def compute(ready):
            dn = jnp.zeros((tm, D), F32)
            for k, (c0, cn) in enumerate(chunks):
                ready(k)
                dn = dn + _nt(dgu_ref[:, c0:c0 + cn], wgu_ref[:, c0:c0 + cn])
            r, xr = _rms(x_ref[...])
            dx, dgain = _rms_bwd(dn, xr, r, g_ref[...])
            dx_ref[...] = dh_ref[...] + dx
            dg_ref[...] += dgain

        _staged([_wide_columns(wgu_hbm, wgu_ref, c0, cn) for c0, cn in chunks], compute)

    return _call(
        body, (dh, x, gain, dgu, wgu), name=name, grid=(S // tm,),
        in_specs=[_rows(tm, D), _rows(tm, D), _fixed((1, D)), _rows(tm, 4 * FFS), _ANY],
        out_specs=[_rows(tm, D), _fixed((1, D))],
        out_shape=[_sds((S, D), F32), _sds((1, D), F32)],
        scratch_shapes=[_vmem_wide(wgu)],
        compiler_params=_params(("arbitrary",), 56), exchange=exchange, after=after)


def _mix_in(h, gain, w_in, after=()):
    tm = 512

    def body(h_ref, g_ref, w_hbm, u_ref, xp_ref, q_ref, k_ref, v_ref, gp_ref, gs_ref, w_ref):
        def compute(ready):
            _, hr = _rms(h_ref[...])
            u = (hr * g_ref[...]).astype(BF16)
            u_ref[...] = u
            ready(0)
            p0 = _nn(u, w_ref[0])
            xp_ref[...] = p0[:, :PW]
            q_ref[...] = p0[:, PW:].astype(BF16)
            ready(1)
            p1 = _nn(u, w_ref[1])
            k_ref[...] = p1[:, :SBW].astype(BF16)
            v_ref[...] = p1[:, SBW:].astype(BF16)
            ready(2)
            gp_ref[...] = jax.nn.sigmoid(_nn(u, w_ref[2])).astype(BF16)
            ready(3)
            gs_ref[...] = jax.nn.sigmoid(_nn(u, w_ref[3])).astype(BF16)

        _staged([[(w_hbm.at[j], w_ref.at[j])] for j in range(NSH)], compute)

    return _call(
        body, (h, gain, w_in), name="mix_in", grid=(S // tm,),
        in_specs=[_rows(tm, D), _fixed((1, D)), _ANY],
        out_specs=[_rows(tm, D), _rows(tm, PW), _rows(tm, SBW), _rows(tm, SBW), _rows(tm, SBW),
                   _rows(tm, D), _rows(tm, D)],
        out_shape=[_sds((S, D), BF16), _sds((S, PW), F32), _sds((S, SBW), BF16), _sds((S, SBW), BF16),
                   _sds((S, SBW), BF16), _sds((S, D), BF16), _sds((S, D), BF16)],
        scratch_shapes=_vmem_like(w_in),
        compiler_params=_params(("arbitrary",), 48), free=(1,), after=after)


def _hilo_dot(x, tri):
    hi = x.astype(BF16)
    lo = (x - hi.astype(F32)).astype(BF16)
    return _nn(hi, tri) + _nn(lo, tri)


def _log_terms(qk):
    z2 = qk * (SCALE * LOG2E)
    lb = jnp.minimum(z2, 0.0) - jnp.log2(1.0 + jnp.exp2(-jnp.abs(z2)))
    return lb, lb - z2


def _head_masks():
    lane = lax.broadcasted_iota(jnp.int32, (1, 2 * DH), 1)
    return (lane < DH, lane >= DH)


def _attn_fwd(q, k, v, exchange=None):
    T = TA

    def body(q_ref, k_ref, v_ref, o_ref, c_ref):
        i2 = 2 * pl.program_id(1)
        row = lax.broadcasted_iota(jnp.int32, (T, T), 0)
        col = lax.broadcasted_iota(jnp.int32, (T, T), 1)
        after = (row > col).astype(BF16)
        causal = col < row
        masks = _head_masks()
        qms = {}
        for b in range(QB):
            q2 = q_ref[b * T:(b + 1) * T, :]
            for h, hm in enumerate(masks):
                qms[b, h] = jnp.where(hm, q2, jnp.zeros_like(q2))

        def blocks(keys, pairs, carries, os):
            ks, vms = [], []
            for j in keys:
                rows = pl.ds(pl.multiple_of(j * T, T), T)
                vj = v_ref[rows, :]
                ks.append(k_ref[rows, :])
                vms.append([jnp.where(hm, vj, jnp.zeros_like(vj)) for hm in masks])
            units = [(n, h) for n in range(len(pairs)) for h in range(2)]
            qks = {(n, h): _nt(qms[pairs[n][0], h], ks[pairs[n][1]]) for n, h in units}
            lbs, l1ms = {}, {}
            for u in units:
                lbs[u], l1m = _log_terms(qks[u])
                l1ms[u] = jnp.where(causal, l1m, 0.0) if pairs[u[0]][2] else l1m
            cins = {u: _hilo_dot(l1ms[u], after) for u in units}
            carries, os = dict(carries), list(os)
            for n, h in units:
                b, key, diag = pairs[n]
                a = jnp.exp2(lbs[n, h] + cins[n, h] + carries[b, h])
                if diag:
                    a = jnp.where(causal, a, 0.0)
                os[b] = os[b] + _nn(a.astype(BF16), vms[key][h])
                carries[b, h] = carries[b, h] + jnp.sum(l1ms[n, h], axis=1, keepdims=True)
            return carries, tuple(os)

        carries = {(b, h): jnp.zeros((T, 1), F32) for b in range(QB) for h in range(2)}
        os = tuple(jnp.zeros((T, 2 * DH), F32) for _ in range(QB))
        carries, os = blocks([i2 + 1, i2], [(1, 0, True), (0, 1, True), (1, 1, False)], carries, os)
        carries, os = lax.fori_loop(
            0, i2 // 2,
            lambda t, c: blocks([i2 - 1 - 2 * t, i2 - 2 - 2 * t],
                                [(0, 0, False), (1, 0, False), (0, 1, False), (1, 1, False)], c[0], c[1]),
            (carries, os))
        for b in range(QB):
            o_ref[b * T:(b + 1) * T, :] = os[b].astype(BF16)
            c_ref[b * T:(b + 1) * T, :] = jnp.where(masks[0], carries[b, 0], carries[b, 1])

    blk = pl.BlockSpec((QB * T, 2 * DH), lambda p, i: (i, p))
    full = pl.BlockSpec((S, 2 * DH), lambda p, i: (0, p))
    return _call(
        body, (q, k, v), name="attn_fwd", grid=(SBW // (2 * DH), S // (QB * T)),
        in_specs=[blk, full, full], out_specs=[blk, blk],
        out_shape=[_sds((S, SBW), BF16), _sds((S, SBW), F32)],
        compiler_params=_params(("arbitrary", "arbitrary"), 40), exchange=exchange)


def _attn_bwd(q, k, v, do, ctot, after=()):
    T = TA
    nq = S // (QB * T)

    def body(q_ref, k_ref, v_ref, do_ref, c_ref, dq_ref, dk_ref, dv_ref, dk_acc, dv_acc):
        step = pl.program_id(1)
        i2 = 2 * step

        @pl.when(step == 0)
        def _():
            dk_acc[...] = jnp.zeros_like(dk_acc)
            dv_acc[...] = jnp.zeros_like(dv_acc)

        row = lax.broadcasted_iota(jnp.int32, (T, T), 0)
        col = lax.broadcasted_iota(jnp.int32, (T, T), 1)
        upto = (row <= col).astype(BF16)
        before = (row < col).astype(BF16)
        causal = col < row
        masks = _head_masks()
        qms, doms, ctots = {}, {}, {}
        for b in range(QB):
            q2, do2 = q_ref[b * T:(b + 1) * T, :], do_ref[b * T:(b + 1) * T, :]
            for h, hm in enumerate(masks):
                qms[b, h] = jnp.where(hm, q2, jnp.zeros_like(q2))
                doms[b, h] = jnp.where(hm, do2, jnp.zeros_like(do2))
                ctots[b, h] = c_ref[b * T:(b + 1) * T, h * DH:h * DH + 1]

        def blocks(keys, pairs, sums, dqs):
            rows = [pl.ds(pl.multiple_of(j * T, T), T) for j in keys]
            ks, vs = [k_ref[r, :] for r in rows], [v_ref[r, :] for r in rows]
            kms = [[jnp.where(hm, kj, jnp.zeros_like(kj)) for hm in masks] for kj in ks]
            units = [(n, h) for n in range(len(pairs)) for h in range(2)]
            qks = {(n, h): _nt(qms[pairs[n][0], h], ks[pairs[n][1]]) for n, h in units}
            das = {(n, h): _nt(doms[pairs[n][0], h], vs[pairs[n][1]]) for n, h in units}
            lbs, l1ms = {}, {}
            for u in units:
                lbs[u], l1m = _log_terms(qks[u])
                l1ms[u] = jnp.where(causal, l1m, 0.0) if pairs[u[0]][2] else l1m
            pins = {u: _hilo_dot(l1ms[u], upto) for u in units}
            sums = dict(sums)
            a_s, dls, cps = {}, {}, {}
            for n, h in units:
                b, _, diag = pairs[n]
                cl, cp = sums[b, h]
                a = jnp.exp2(lbs[n, h] + (ctots[b, h] - cl) - pins[n, h])
                if diag:
                    a = jnp.where(causal, a, 0.0)
                a_s[n, h] = a.astype(BF16)
                dls[n, h] = das[n, h] * a
                cps[n, h] = cp
                sums[b, h] = (cl + jnp.sum(l1ms[n, h], axis=1, keepdims=True),
                              cp + jnp.sum(dls[n, h], axis=1, keepdims=True))
            pexs = {u: _hilo_dot(dls[u], before) for u in units}
            dzbs = {}
            for u in units:
                dz = dls[u] - jnp.exp2(lbs[u]) * (dls[u] + pexs[u] + cps[u])
                if pairs[u[0]][2]:
                    dz = jnp.where(causal, dz, 0.0)
                dzbs[u] = dz.astype(BF16)
            dqs = list(dqs)
            for n, h in units:
                dqs[pairs[n][0]] = dqs[pairs[n][0]] + _nn(dzbs[n, h], kms[pairs[n][1]][h])
            for key, r in enumerate(rows):
                mine = [(n, h) for n, h in units if pairs[n][1] == key]
                dk_acc[r, :] += functools.reduce(jnp.add, [_tn(dzbs[u], qms[pairs[u[0]][0], u[1]]) for u in mine])
                dv_acc[r, :] += functools.reduce(jnp.add, [_tn(a_s[u], doms[pairs[u[0]][0], u[1]]) for u in mine])
            return sums, tuple(dqs)

        zero = jnp.zeros((T, 1), F32)
        sums = {(b, h): (zero, zero) for b in range(QB) for h in range(2)}
        dqs = tuple(jnp.zeros((T, 2 * DH), F32) for _ in range(QB))
        sums, dqs = lax.fori_loop(
            0, i2 // 2,
            lambda t, c: blocks([2 * t, 2 * t + 1],
                                [(0, 0, False), (1, 0, False), (0, 1, False), (1, 1, False)], c[0], c[1]),
            (sums, dqs))
        _, dqs = blocks([i2, i2 + 1], [(0, 0, True), (1, 0, False), (1, 1, True)], sums, dqs)
        for b in range(QB):
            dq_ref[b * T:(b + 1) * T, :] = (dqs[b] * SCALE).astype(BF16)

        @pl.when(step == nq - 1)
        def _():
            dk_ref[...] = (dk_acc[...] * SCALE).astype(BF16)
            dv_ref[...] = dv_acc[...].astype(BF16)

    blk = pl.BlockSpec((QB * T, 2 * DH), lambda p, i: (i, p))
    full = pl.BlockSpec((S, 2 * DH), lambda p, i: (0, p))
    return _call(
        body, (q, k, v, do, ctot), name="attn_bwd", grid=(SBW // (2 * DH), nq),
        in_specs=[blk, full, full, blk, blk], out_specs=[blk, full, full],
        out_shape=[_sds((S, SBW), BF16), _sds((S, SBW), BF16), _sds((S, SBW), BF16)],
        scratch_shapes=[pltpu.VMEM((S, 2 * DH), F32), pltpu.VMEM((S, 2 * DH), F32)],
        compiler_params=_params(("arbitrary", "arbitrary"), 40), after=after)


def _pool_counts(first_row, tm):
    pos = first_row + lax.broadcasted_iota(jnp.int32, (tm, 1), 0)
    return [jnp.minimum(pos + 1, w).astype(F32) for w in POOL_WINDOWS]


def _mix_out(h, xp, o_sb, gp, gs, w_group, scale, w_bp, w_ba, w_out, exchange=None):
    tm = 512

    def body(h_ref, xp_ref, o_ref, gp_ref, gs_ref, wg_hbm, sc_ref, wbp_hbm, wba_hbm, wo_hbm,
             h2_ref, pm_ref, p_ref, yp_ref, ys_ref, m_ref, halo, wg_ref, wbp_ref, wba_ref, wo_ref):
        _stage([(wg_hbm, wg_ref), (wbp_hbm, wbp_ref), (wba_hbm, wba_ref), (wo_hbm, wo_ref)])
        i = pl.program_id(0)

        @pl.when(i == 0)
        def _():
            halo[...] = jnp.zeros_like(halo)

        xp = xp_ref[...]
        ext = jnp.concatenate([halo[...], xp], axis=0)
        halo[...] = xp[tm - HALO:, :]
        counts = _pool_counts(i * tm, tm)
        for gi in range(len(POOL_WINDOWS)):
            lanes = slice(gi * PG, (gi + 1) * PG)
            win = ext[:, lanes]
            for step in range(gi + 1):
                win = win + pltpu.roll(win, 1 << step, 0)
            pm = (win[HALO:, :] / counts[gi] - xp[:, lanes]).astype(BF16)
            pm_ref[:, lanes] = pm
            p_ref[:, lanes] = (_nn(pm, wg_ref[gi]) * sc_ref[:, lanes]).astype(BF16)
        pb = p_ref[...]
        ob = o_ref[...]
        for j in range(NSH):
            cols = slice(j * (D // NSH), (j + 1) * (D // NSH))
            yp = _nn(pb, wbp_ref[j])
            ys = _nn(ob, wba_ref[j])
            yp_ref[:, cols] = yp.astype(BF16)
            ys_ref[:, cols] = ys.astype(BF16)
            m_ref[:, cols] = (gp_ref[:, cols].astype(F32) * yp + gs_ref[:, cols].astype(F32) * ys).astype(BF16)
        h2_ref[...] = h_ref[...] + _nn(m_ref[...], wo_ref[...])

    return _call(
        body, (h, xp, o_sb, gp, gs, w_group, scale, w_bp, w_ba, w_out), name="mix_out", grid=(S // tm,),
        in_specs=[_rows(tm, D), _rows(tm, PW), _rows(tm, SBW), _rows(tm, D), _rows(tm, D),
                  _ANY, _fixed((1, PW)), _ANY, _ANY, _ANY],
        out_specs=[_rows(tm, D), _rows(tm, PW), _rows(tm, PW), _rows(tm, D), _rows(tm, D), _rows(tm, D)],
        out_shape=[_sds((S, D), F32), _sds((S, PW), BF16), _sds((S, PW), BF16), _sds((S, D), BF16),
                   _sds((S, D), BF16), _sds((S, D), BF16)],
        scratch_shapes=[pltpu.VMEM((HALO, PW), F32)] + _vmem_like(w_group, w_bp, w_ba, w_out),
        compiler_params=_params(("arbitrary",), 48), free=(5, 6), exchange=exchange)


def _mix_bwd_out(dh, gp, gs, yp, ys, pm, w_group, scale, w_bp, w_ba, w_out, exchange=None):
    tm = 512
    nt = S // tm

    def body(dh_ref, gp_ref, gs_ref, yp_ref, ys_ref, pm_ref, wg_hbm, sc_ref, wbp_hbm, wba_hbm, wo_hbm,
             dlg_ref, dyp_ref, dys_ref, do_ref, dyg_ref, dxp_ref, dsc_ref, halo, wg_ref, wbp_ref, wba_ref, wo_ref):
        _stage([(wg_hbm, wg_ref), (wbp_hbm, wbp_ref), (wba_hbm, wba_ref), (wo_hbm, wo_ref)])
        step = pl.program_id(0)

        @pl.when(step == 0)
        def _():
            halo[...] = jnp.zeros_like(halo)
            dsc_ref[...] = jnp.zeros_like(dsc_ref)

        dm = _nt(dh_ref[...].astype(BF16), wo_ref[...])
        gp = gp_ref[...].astype(F32)
        gs = gs_ref[...].astype(F32)
        yp = yp_ref[...].astype(F32)
        ys = ys_ref[...].astype(F32)
        dlg_ref[:, :D] = (dm * yp * gp * (1.0 - gp)).astype(BF16)
        dlg_ref[:, D:] = (dm * ys * gs * (1.0 - gs)).astype(BF16)
        dyp_ref[...] = (dm * gp).astype(BF16)
        dys_ref[...] = (dm * gs).astype(BF16)
        dp = jnp.zeros((tm, PW), F32)
        do = jnp.zeros((tm, SBW), F32)
        for j in range(NSH):
            cols = slice(j * (D // NSH), (j + 1) * (D // NSH))
            dp = dp + _nt(dyp_ref[:, cols], wbp_ref[j])
            do = do + _nt(dys_ref[:, cols], wba_ref[j])
        do_ref[...] = do.astype(BF16)
        counts = _pool_counts((nt - 1 - step) * tm, tm)
        dscale = []
        for gi in range(len(POOL_WINDOWS)):
            lanes = slice(gi * PG, (gi + 1) * PG)
            dpg = dp[:, lanes]
            dscale.append(jnp.sum(dpg * _nn(pm_ref[:, lanes], wg_ref[gi]), axis=0, keepdims=True))
            dyg = (dpg * sc_ref[:, lanes]).astype(BF16)
            dyg_ref[:, lanes] = dyg
            dpm = _nt(dyg, wg_ref[gi])
            per = dpm / counts[gi]
            win = jnp.concatenate([per, halo[:, lanes]], axis=0)
            halo[:, lanes] = per[:HALO, :]
            for s in range(gi + 1):
                win = win + pltpu.roll(win, tm + HALO - (1 << s), 0)
            dxp_ref[:, lanes] = (win[:tm, :] - dpm).astype(BF16)
        dsc_ref[...] += jnp.concatenate(dscale, axis=1)

    rev = lambda width: pl.BlockSpec((tm, width), lambda i: (nt - 1 - i, 0))
    return _call(
        body, (dh, gp, gs, yp, ys, pm, w_group, scale, w_bp, w_ba, w_out), name="mix_bwd_out", grid=(nt,),
        in_specs=[rev(D), rev(D), rev(D), rev(D), rev(D), rev(PW), _ANY, _fixed((1, PW)), _ANY, _ANY, _ANY],
        out_specs=[rev(2 * D), rev(D), rev(D), rev(SBW), rev(PW), rev(PW), _fixed((1, PW))],
        out_shape=[_sds((S, 2 * D), BF16), _sds((S, D), BF16), _sds((S, D), BF16), _sds((S, SBW), BF16),
                   _sds((S, PW), BF16), _sds((S, PW), BF16), _sds((1, PW), F32)],
        scratch_shapes=[pltpu.VMEM((HALO, PW), F32)] + _vmem_like(w_group, w_bp, w_ba, w_out),
        compiler_params=_params(("arbitrary",), 48), exchange=exchange)


def _mix_bwd_in(dh, h, gain, pieces, w_in, exchange=None):
    tm = 512
    widths = [p.shape[1] for p in pieces]

    def body(dh_ref, h_ref, g_ref, *rest):
        piece_refs, (w_hbm, dx_ref, dg_ref, w_ref, dp_ref) = rest[:len(pieces)], rest[len(pieces):]
        @pl.when(pl.program_id(0) == 0)
        def _():
            dg_ref[...] = jnp.zeros_like(dg_ref)

        def compute(ready):
            at = 0
            for ref, width in zip(piece_refs, widths):
                dp_ref[:, at:at + width] = ref[...]
                at += width
            du = jnp.zeros((tm, D), F32)
            for j in range(NSH):
                ready(j)
                du = du + _nt(dp_ref[:, j * D:(j + 1) * D], w_ref[j])
            r, hr = _rms(h_ref[...])
            dx, dgain = _rms_bwd(du, hr, r, g_ref[...])
            dx_ref[...] = dh_ref[...] + dx
            dg_ref[...] += dgain

        _staged([[(w_hbm.at[j], w_ref.at[j])] for j in range(NSH)], compute)

    return _call(
        body, (dh, h, gain, *pieces, w_in), name="mix_bwd_in", grid=(S // tm,),
        in_specs=[_rows(tm, D), _rows(tm, D), _fixed((1, D))] + [_rows(tm, w) for w in widths] + [_ANY],
        out_specs=[_rows(tm, D), _fixed((1, D))],
        out_shape=[_sds((S, D), F32), _sds((1, D), F32)],
        scratch_shapes=_vmem_like(w_in) + [pltpu.VMEM((tm, 4 * D), BF16)],
        compiler_params=_params(("arbitrary",), 48), exchange=exchange)


def _wgrad_in(u, pieces):
    dxp, dq, dk, dv, dlg = pieces

    def body(u_ref, dxp_ref, dq_ref, dk_ref, dv_ref, dlg_ref, o_ref):
        j = pl.program_id(0)
        u = u_ref[...]

        def two(left_ref, right_ref):
            o_ref[:, :PW] = _tn(u, left_ref[...]).astype(BF16)
            o_ref[:, PW:] = _tn(u, right_ref[...]).astype(BF16)

        pl.when(j == 0)(lambda: two(dxp_ref, dq_ref))
        pl.when(j == 1)(lambda: two(dk_ref, dv_ref))

        @pl.when(j >= 2)
        def _():
            o_ref[...] = _tn(u, dlg_ref[...]).astype(BF16)

    whole = lambda width: pl.BlockSpec((S, width), lambda j: (0, 0))
    return _call(
        body, (u, dxp, dq, dk, dv, dlg), name="wgrad_in", grid=(NSH,),
        in_specs=[whole(D), whole(PW), whole(SBW), whole(SBW), whole(SBW),
                  pl.BlockSpec((S, D), lambda j: (0, jnp.maximum(j - 2, 0)))],
        out_specs=[pl.BlockSpec((None, D, D), lambda j: (j, 0, 0))], out_shape=[_sds((NSH, D, D), BF16)],
        compiler_params=_params(("arbitrary",), 56))[0]


def _wgrad(a, b, nblk, ti, name, out_dtype=BF16, exchange=None, after=()):
    ka, n = a.shape[1], b.shape[1]
    ns = n // nblk

    def body(a_ref, b_ref, o_ref):
        o_ref[...] = _tn(a_ref[...].astype(BF16), b_ref[...].astype(BF16)).astype(out_dtype)

    res = _call(
        body, (a, b), name=name, grid=(nblk, ka // ti),
        in_specs=[pl.BlockSpec((S, ti), lambda j, i: (0, i)), pl.BlockSpec((S, ns), lambda j, i: (0, j))],
        out_specs=[pl.BlockSpec((None, ti, ns), lambda j, i: (j, i, 0))],
        out_shape=[_sds((nblk, ka, ns), out_dtype)],
        compiler_params=_params(("arbitrary", "arbitrary"), 56), exchange=exchange, after=after)
    return res[0] if exchange is None else (res[0][0], res[1])


def _wgrad_branches(p, dyp, o_sb, dys, pm, dyg, mm, dh):
    cols = D // NSH

    def body(p_ref, dyp_ref, o_ref, dys_ref, pm_ref, dyg_ref, mm_ref, dh_ref, gbp_ref, gba_ref, gg_ref, go_ref):
        gbp_ref[...] = _tn(p_ref[...], dyp_ref[...]).astype(BF16)
        gba_ref[...] = _tn(o_ref[...], dys_ref[...]).astype(BF16)
        gg_ref[...] = _tn(pm_ref[...], dyg_ref[...])
        go_ref[...] = _tn(mm_ref[...], dh_ref[...].astype(BF16)).astype(BF16)

    whole = lambda width: pl.BlockSpec((S, width), lambda j: (0, 0))
    col = lambda width: pl.BlockSpec((S, width), lambda j: (0, j))
    return _call(
        body, (p, dyp, o_sb, dys, pm, dyg, mm, dh), name="wgrad_branches", grid=(NSH,),
        in_specs=[whole(PW), col(cols), whole(SBW), col(cols), col(PG), col(PG), whole(D), col(cols)],
        out_specs=[pl.BlockSpec((None, PW, cols), lambda j: (j, 0, 0)),
                   pl.BlockSpec((None, SBW, cols), lambda j: (j, 0, 0)),
                   pl.BlockSpec((None, PG, PG), lambda j: (j, 0, 0)),
                   pl.BlockSpec((D, cols), lambda j: (0, j))],
        out_shape=[_sds((NSH, PW, cols), BF16), _sds((NSH, SBW, cols), BF16), _sds((NSH, PG, PG), F32),
                   _sds((D, D), BF16)],
        compiler_params=_params(("arbitrary",), 40))


def _place():
    x, y, c = lax.axis_index("x"), lax.axis_index("y"), lax.axis_index("c")
    chips = [(1 - x, y), (x, 1 - y), (1 - x, 1 - y)]
    return x, y, c, chips


def _remote(src, dst, ssem, rsem, dev):
    return pltpu.make_async_remote_copy(src_ref=src, dst_ref=dst, send_sem=ssem, recv_sem=rsem,
                                        device_id=dev, device_id_type=MESH)


def _cast_into_block(ws, me_idx, name):
    steps = 4
    shapes = [(w.shape[0] // steps, w.shape[1]) for w in ws]

    def body(me_ref, *refs):
        for w_ref, o_ref in zip(refs[:len(ws)], refs[len(ws):]):
            o_ref[...] = w_ref[...].astype(BF16)

    return pl.pallas_call(
        body, name=name, out_shape=[_sds((NSH,) + w.shape, BF16) for w in ws],
        grid_spec=pltpu.PrefetchScalarGridSpec(
            num_scalar_prefetch=1, grid=(steps,),
            in_specs=[pl.BlockSpec((r, c), lambda s, me: (s, 0)) for r, c in shapes],
            out_specs=[pl.BlockSpec((None, r, c), lambda s, me: (me[0], s, 0)) for r, c in shapes]),
        compiler_params=_params(("arbitrary",), 32),
    )(me_idx, *ws)


def _ex_gather(bufs):
    n = len(bufs)
    per = 8

    def plan(outs, ssem, rsem, w):
        x, y, c, _ = _place()
        sib, nbr_x, nbr_y = (x, y, 1 - c), (1 - x, y, c), (x, 1 - y, c)
        half = outs[w].shape[1] // 2
        quarter = half // 2
        sem = lambda k: (ssem.at[per * w + k], rsem.at[per * w + k])
        rows = lambda blk, start, size: outs[w].at[blk, pl.ds(start, size)]
        mine = rows(2 * x + y, c * half, half)
        from_x = rows(2 * (1 - x) + y, c * half, half)
        from_y = rows(2 * x + (1 - y), c * half, half)
        diag = 2 * (1 - x) + (1 - y)
        pass_y = rows(2 * (1 - x) + y, c * half, quarter)
        pass_x = rows(2 * x + (1 - y), c * half + quarter, quarter)
        diag_0, diag_1 = rows(diag, c * half, quarter), rows(diag, c * half + quarter, quarter)
        first = [_remote(mine, mine, *sem(0), nbr_x), _remote(mine, mine, *sem(1), nbr_y)]
        arrivals = [
            (_remote(from_x, from_x, *sem(0), nbr_x),
             [_remote(pass_y, pass_y, *sem(2), nbr_y), _remote(from_x, from_x, *sem(4), sib)]),
            (_remote(from_y, from_y, *sem(1), nbr_y),
             [_remote(pass_x, pass_x, *sem(3), nbr_x), _remote(from_y, from_y, *sem(5), sib)]),
            (_remote(diag_0, diag_0, *sem(2), nbr_y), [_remote(diag_0, diag_0, *sem(6), sib)]),
            (_remote(diag_1, diag_1, *sem(3), nbr_x), [_remote(diag_1, diag_1, *sem(7), sib)]),
        ]
        other = (1 - c) * half
        from_sibling = [
            _remote(rows(2 * (1 - x) + y, other, half), rows(2 * (1 - x) + y, other, half), *sem(4), sib),
            _remote(rows(2 * x + (1 - y), other, half), rows(2 * x + (1 - y), other, half), *sem(5), sib),
            _remote(rows(diag, other, quarter), rows(diag, other, quarter), *sem(6), sib),
            _remote(rows(diag, other + quarter, quarter), rows(diag, other + quarter, quarter), *sem(7), sib),
        ]
        return first, arrivals, from_sibling

    def start(ins, outs, ssem, rsem):
        x, y, c, _ = _place()
        for w in range(n):
            half = outs[w].shape[1] // 2
            mine = outs[w].at[2 * x + y, pl.ds(c * half, half)]
            _remote(mine, mine, ssem.at[per * w], rsem.at[per * w], (1 - x, y, c)).start()
            _remote(mine, mine, ssem.at[per * w + 1], rsem.at[per * w + 1], (x, 1 - y, c)).start()

    def finish(ins, outs, ssem, rsem):
        plans = [plan(outs, ssem, rsem, w) for w in range(n)]
        started = []
        for direct in (True, False):
            for first, arrivals, _ in plans:
                for arrived, onward in (arrivals[:2] if direct else arrivals[2:]):
                    arrived.wait_recv()
                    for cp in onward:
                        cp.start()
                    started += onward
        for first, _, from_sibling in plans:
            for cp in from_sibling:
                cp.wait_recv()
            started += first
        for cp in started:
            cp.wait_send()

    return Exchange(bufs, [_sds(b.shape, b.dtype) for b in bufs], {w: w for w in range(n)}, per * n, start, finish)


def _ex_gather_direct(bufs):
    n = len(bufs)

    def copies(outs, ssem, rsem, only_first=False):
        x, y, c, chips = _place()
        me, sib = 2 * x + y, (x, y, 1 - c)
        first, relay, last = [], [], []
        for w in range(n):
            half = outs[w].shape[1] // 2
            mine = outs[w].at[me, pl.ds(c * half, half)]
            for k, (px, py) in enumerate(chips):
                sems = (ssem.at[6 * w + k], rsem.at[6 * w + k])
                sib_sems = (ssem.at[6 * w + 3 + k], rsem.at[6 * w + 3 + k])
                first.append(_remote(mine, mine, *sems, (px, py, c)))
                if only_first:
                    continue
                got = outs[w].at[2 * px + py, pl.ds(c * half, half)]
                relay.append((_remote(got, got, *sems, (px, py, c)), _remote(got, got, *sib_sems, sib)))
                theirs = outs[w].at[2 * px + py, pl.ds((1 - c) * half, half)]
                last.append(_remote(theirs, theirs, *sib_sems, sib))
        return first, relay, last

    def start(ins, outs, ssem, rsem):
        for cp in copies(outs, ssem, rsem, only_first=True)[0]:
            cp.start()

    def finish(ins, outs, ssem, rsem):
        first, relay, last = copies(outs, ssem, rsem)
        for arrived, onward in relay:
            arrived.wait_recv()
            onward.start()
        for cp in last:
            cp.wait_recv()
        for cp in first:
            cp.wait_send()
        for _, onward in relay:
            onward.wait_send()

    return Exchange(bufs, [_sds(b.shape, b.dtype) for b in bufs], {w: w for w in range(n)}, 6 * n, start, finish)


def _simple_exchange(arrays, landing, aliases, make_copies, sibling_only=False):
    def start(ins, outs, ssem, rsem):
        for cp, _ in make_copies(ins, outs, ssem, rsem, False):
            cp.start()

    def finish(ins, outs, ssem, rsem):
        cps = make_copies(ins, outs, ssem, rsem, True)
        for _, landed in cps:
            landed.wait_recv()
        for cp, _ in cps:
            cp.wait_send()

    return Exchange(arrays, landing, aliases, len(arrays) * 3, start, finish, sibling_only)


def _ex_pair_swap(grads):
    def make(ins, outs, ssem, rsem, landing):
        x, y, c, _ = _place()
        cps = [_remote(ins[w].at[:, 1 - c], outs[w], ssem.at[w], rsem.at[w], (x, y, 1 - c))
               for w in range(len(grads))]
        return [(cp, cp) for cp in cps]

    return _simple_exchange(grads, [_sds((NSH,) + g.shape[2:], g.dtype) for g in grads], {}, make, True)


def _ex_relay(bufs):
    def make(ins, outs, ssem, rsem, landing):
        x, y, c, chips = _place()
        sib = (x, y, 1 - c)
        out = []
        for w in range(len(bufs)):
            half = outs[w].shape[1] // 2
            for k, (px, py) in enumerate(chips):
                sems = (ssem.at[3 * w + k], rsem.at[3 * w + k])
                have = outs[w].at[2 * px + py, pl.ds(c * half, half)]
                miss = outs[w].at[2 * px + py, pl.ds((1 - c) * half, half)]
                out.append((_remote(have, have, *sems, sib), _remote(miss, miss, *sems, sib) if landing else None))
        return out

    return _simple_exchange(bufs, [_sds(b.shape, b.dtype) for b in bufs], {w: w for w in range(len(bufs))}, make, True)


def _ex_share(bufs):
    def make(ins, outs, ssem, rsem, landing):
        x, y, c, _ = _place()
        sib = (x, y, 1 - c)
        return [(_remote(outs[w].at[c], outs[w].at[c], ssem.at[w], rsem.at[w], sib),
                 _remote(outs[w].at[1 - c], outs[w].at[1 - c], ssem.at[w], rsem.at[w], sib) if landing else None)
                for w in range(len(bufs))]

    return _simple_exchange(bufs, [_sds(b.shape, b.dtype) for b in bufs], {w: w for w in range(len(bufs))}, make, True)


def _small_copies(slots, ssems, rsems, sending):
    x, y, c, _ = _place()
    out = []
    for m in range(1, 8):
        px, py, pc = x ^ (m >> 2), y ^ ((m >> 1) & 1), c ^ (m & 1)
        slot = slots.at[4 * x + 2 * y + c if sending else 4 * px + 2 * py + pc]
        out.append(_remote(slot, slot, ssems[m - 1], rsems[m - 1], (px, py, pc)))
    return out


def _small_gather_start(slots, name, after=()):
    at = 1 + len(after)

    def body(*refs):
        for cp in _small_copies(refs[0], refs[at:at + 7], refs[at + 7:at + 14], True):
            cp.start()
        refs[-1][...] = jnp.zeros_like(refs[-1])

    outs = pl.pallas_call(
        body, name=name,
        out_shape=([pltpu.SemaphoreType.DMA(())] * 14 + [pltpu.HBM(slots.shape, slots.dtype)]
                   + [jax.ShapeDtypeStruct((8, 128), F32)]),
        in_specs=[_HBM] + [_ANY] * len(after), out_specs=[_SEM] * 14 + [_HBM, _VM], input_output_aliases={0: 14},
        compiler_params=pltpu.CompilerParams(has_side_effects=_EFFECT),
    )(*_in_hbm([slots]), *after)
    return outs[:14], outs[14], outs[15]


def _small_gather_wait(sems, slots, after, name):
    def body(*refs):
        for cp in _small_copies(refs[0], refs[1:8], refs[8:15], True):
            cp.wait_send()
        for cp in _small_copies(refs[0], refs[1:8], refs[8:15], False):
            cp.wait_recv()

    return pl.pallas_call(
        body, name=name, out_shape=pltpu.HBM(slots.shape, slots.dtype),
        in_specs=[_HBM] + [_SEM] * 14 + [_ANY] * len(after), out_specs=_HBM, input_output_aliases={0: 0},
        compiler_params=pltpu.CompilerParams(has_side_effects=_EFFECT),
    )(slots, *sems, *after)


def _pair_sum(grads, gots, c_idx, name):
    n = len(grads)

    def body(c_ref, *refs):
        for a_ref, b_ref, o_ref in zip(refs[:n], refs[n:2 * n], refs[2 * n:]):
            o_ref[...] = (a_ref[...].astype(F32) + b_ref[...].astype(F32)).astype(BF16)

    halves = [g.shape[2:] for g in grads]
    return list(pl.pallas_call(
        body, name=name, out_shape=[_sds((NSH,) + h, BF16) for h in halves],
        grid_spec=pltpu.PrefetchScalarGridSpec(
            num_scalar_prefetch=1, grid=(NSH,),
            in_specs=[pl.BlockSpec((None, None) + h, lambda j, c: (j, c[0], 0, 0)) for h in halves]
            + [pl.BlockSpec((None,) + h, lambda j, c: (j, 0, 0)) for h in halves],
            out_specs=[pl.BlockSpec((None,) + h, lambda j, c: (j, 0, 0)) for h in halves]),
        compiler_params=_params(("arbitrary",), 40),
    )(c_idx, *_in_hbm(list(grads) + list(gots))))


def _chip_sum(owns, gots, place, name):
    n = len(owns)

    def body(place_ref, *refs):
        for own_ref, got_ref, o_ref in zip(refs[:n], refs[n:2 * n], refs[2 * n:]):
            acc = own_ref[...].astype(F32)
            for k in range(3):
                acc = acc + got_ref[k].astype(F32)
            o_ref[...] = acc

    shapes = [(o.shape[1] // 2, o.shape[2]) for o in owns]
    return list(pl.pallas_call(
        body, name=name, out_shape=[_sds((2, 2 * r, c), F32) for r, c in shapes],
        grid_spec=pltpu.PrefetchScalarGridSpec(
            num_scalar_prefetch=1, grid=(2,),
            in_specs=[pl.BlockSpec((None, r, c), lambda s, p: (p[0], s, 0)) for r, c in shapes]
            + [pl.BlockSpec((3, r, c), lambda s, p: (0, s, 0)) for r, c in shapes],
            out_specs=[pl.BlockSpec((None, r, c), lambda s, p: (p[1], s, 0)) for r, c in shapes]),
        compiler_params=_params(("arbitrary",), 40),
    )(place, *_in_hbm(list(owns) + list(gots))))


def _adamw_math(w, g, m, v):
    m = B1 * m + (1.0 - B1) * g
    v = B2 * v + (1.0 - B2) * (g * g)
    m_hat = m / (1.0 - B1 ** STEP)
    v_hat = v / (1.0 - B2 ** STEP)
    return -LR * (m_hat / (jnp.sqrt(v_hat) + AEPS) + WD * w), m, v


def _adamw(ws, gs, ms, vs, name, after=()):
    n, steps = len(ws), 4

    def body(*refs):
        ins, outs = refs[:4 * n], refs[4 * n:]
        for i in range(n):
            w_ref, g_ref, m_ref, v_ref = ins[4 * i:4 * i + 4]
            go_ref, d_ref, nm_ref, nv_ref = outs[4 * i:4 * i + 4]
            g = g_ref[...]
            go_ref[...] = g
            d_ref[...], nm_ref[...], nv_ref[...] = _adamw_math(w_ref[...], g, m_ref[...], v_ref[...])

    args, specs, shapes, free = [], [], [], []
    for i, (w, g, m, v) in enumerate(zip(ws, gs, ms, vs)):
        args += [w, g, m, v]
        specs += [pl.BlockSpec((w.shape[0] // steps, w.shape[1]), lambda r: (r, 0))] * 4
        shapes += [_sds(w.shape, F32)] * 4
        free += [4 * i, 4 * i + 2, 4 * i + 3]
    outs = _call(body, args, name=name, grid=(steps,), out_shape=shapes, in_specs=specs, out_specs=specs,
                 compiler_params=_params(("arbitrary",), 48), free=tuple(free), after=after)
    return [outs[4 * i:4 * i + 4] for i in range(n)]


def _small_update(gathered, w, m, v, entries):
    rows = w.shape[0]

    def body(ga_ref, w_ref, m_ref, v_ref, *out_refs):
        for j, (first, n) in enumerate(entries):
            mine = slice(first, first + n)
            g = ga_ref[mine, :]
            for dev in range(1, 8):
                g = g + ga_ref[dev * rows + first:dev * rows + first + n, :]
            results = (g,) + _adamw_math(w_ref[mine, :], g, m_ref[mine, :], v_ref[mine, :])
            for i, res in enumerate(results):
                out_refs[i * len(entries) + j][...] = res

    outs = pl.pallas_call(
        body, name="small_update",
        out_shape=[jax.ShapeDtypeStruct((n, 128), F32) for _ in range(4) for _, n in entries],
        in_specs=[_VM] * 4, out_specs=[_VM] * (4 * len(entries)),
    )(gathered, w, m, v)
    return [outs[i * len(entries):(i + 1) * len(entries)] for i in range(4)]


SMALL = ("ffn1_norm", "mix_norm", "ffn2_norm", "final_norm", "pool_scale", "loss", "pool_w_group")
BIG = ("ffn1_w_gate_up", "ffn1_w_down", "w_in", "w_branch_pool", "w_branch_attn", "w_out",
       "ffn2_w_gate_up", "ffn2_w_down")
ORDER = ("ffn1_norm", "ffn1_w_gate_up", "ffn1_w_down", "mix_norm", "w_in", "pool_w_group", "pool_scale",
         "w_branch_pool", "w_branch_attn", "w_out", "ffn2_norm", "ffn2_w_gate_up", "ffn2_w_down", "final_norm")
SMALL_ROWS = 560


def _pack_small(t):
    parts = []
    for k in SMALL:
        rows = t[k].reshape(-1, 128) if k in t else jnp.zeros((1, 128), F32)
        parts.append(jnp.pad(rows, ((0, -rows.shape[0] % 8), (0, 0))))
    packed = jnp.concatenate(parts, axis=0)
    assert packed.shape == (SMALL_ROWS, 128), packed.shape
    return packed


def _small_entries(like):
    out, at = [], 0
    for k in SMALL:
        n = like[k].size // 128 if k in like else 1
        out.append((at, n))
        at += n + (-n % 8)
    return out


def _halves(g):
    return g.reshape(NSH, 2, g.shape[1] // 2, g.shape[2])


def kernel(x, ffn1_norm, ffn1_w_gate_up, ffn1_w_down, mix_norm, w_in, pool_w_group, pool_scale, w_branch_pool, w_branch_attn, w_out, ffn2_norm, ffn2_w_gate_up, ffn2_w_down, final_norm, loss_target, m_ffn1_norm, m_ffn1_w_gate_up, m_ffn1_w_down, m_mix_norm, m_w_in, m_pool_w_group, m_pool_scale, m_w_branch_pool, m_w_branch_attn, m_w_out, m_ffn2_norm, m_ffn2_w_gate_up, m_ffn2_w_down, m_final_norm, v_ffn1_norm, v_ffn1_w_gate_up, v_ffn1_w_down, v_mix_norm, v_w_in, v_pool_w_group, v_pool_scale, v_w_branch_pool, v_w_branch_attn, v_w_out, v_ffn2_norm, v_ffn2_w_gate_up, v_ffn2_w_down, v_final_norm):
    wts = dict(ffn1_norm=ffn1_norm, ffn1_w_gate_up=ffn1_w_gate_up, ffn1_w_down=ffn1_w_down, mix_norm=mix_norm,
               w_in=w_in, pool_w_group=pool_w_group, pool_scale=pool_scale, w_branch_pool=w_branch_pool,
               w_branch_attn=w_branch_attn, w_out=w_out, ffn2_norm=ffn2_norm, ffn2_w_gate_up=ffn2_w_gate_up,
               ffn2_w_down=ffn2_w_down, final_norm=final_norm)
    mom = dict(ffn1_norm=m_ffn1_norm, ffn1_w_gate_up=m_ffn1_w_gate_up, ffn1_w_down=m_ffn1_w_down,
               mix_norm=m_mix_norm, w_in=m_w_in, pool_w_group=m_pool_w_group, pool_scale=m_pool_scale,
               w_branch_pool=m_w_branch_pool, w_branch_attn=m_w_branch_attn, w_out=m_w_out,
               ffn2_norm=m_ffn2_norm, ffn2_w_gate_up=m_ffn2_w_gate_up, ffn2_w_down=m_ffn2_w_down,
               final_norm=m_final_norm)
    var = dict(ffn1_norm=v_ffn1_norm, ffn1_w_gate_up=v_ffn1_w_gate_up, ffn1_w_down=v_ffn1_w_down,
               mix_norm=v_mix_norm, w_in=v_w_in, pool_w_group=v_pool_w_group, pool_scale=v_pool_scale,
               w_branch_pool=v_w_branch_pool, w_branch_attn=v_w_branch_attn, w_out=v_w_out,
               ffn2_norm=v_ffn2_norm, ffn2_w_gate_up=v_ffn2_w_gate_up, ffn2_w_down=v_ffn2_w_down,
               final_norm=v_final_norm)

    c_idx = lax.axis_index("c").astype(jnp.int32).reshape(1)
    me_idx = (2 * lax.axis_index("x") + lax.axis_index("y")).astype(jnp.int32).reshape(1)
    place = jnp.concatenate([me_idx, c_idx])
    x0, tgt = x[0], loss_target[0]
    wgrp = pool_w_group[0].astype(BF16)
    g1, gm, g2, gf = ffn1_norm, mix_norm, ffn2_norm, final_norm.reshape(1, D)
    grad, delta, new_m, new_v = {}, {}, {}, {}

    def pair_sums(keys, parts, got):
        return _pair_sum(parts, got, c_idx, "pair_sum_" + keys[0])

    def chip_sums(keys, chip_parts, owned):
        return _chip_sum(chip_parts, owned, place, "chip_sum_" + keys[0])

    def adamw(keys, after=()):
        outs = _adamw([wts[k][0] for k in keys], [grad[k][0] for k in keys], [mom[k][0] for k in keys],
                      [var[k][0] for k in keys], "adamw_" + keys[0], after=after)
        for k, res in zip(keys, outs):
            grad[k], delta[k], new_m[k], new_v[k] = (o.reshape(wts[k].shape) for o in res)

    first, late = ("ffn1_w_gate_up", "ffn1_w_down"), ("w_branch_pool", "w_branch_attn", "w_out",
                                                       "ffn2_w_gate_up", "ffn2_w_down")
    own = {}
    for group in (first, ("w_in",), late):
        own.update(zip(group, _cast_into_block([wts[k][0] for k in group], me_idx, "cast_" + group[0])))
    full = dict(zip(first, _exchange_alone(_ex_gather([own[k] for k in first]), "gather_ffn1")))
    wgu1, wd1 = full["ffn1_w_gate_up"], full["ffn1_w_down"].reshape(DFF, D)
    (h1, n1, gu1, a1), (win,) = _ffn_fwd(x0, g1, wgu1, wd1, "ffn1_fwd", exchange=_ex_gather_direct([own["w_in"]]))
    sems_l, thru_l, token_l = _gather_start([own[k_] for k_ in late], [h1], "gather_late_start")
    u, xp, q, k, v, gp, gs = _mix_in(h1, gm, win, after=(token_l,))
    o_sb, ctot = _attn_fwd(q, k, v)
    arrived = _gather_wait(sems_l, thru_l, [o_sb], "gather_late_wait")
    wbp, wba, wout = _exchange_alone(_ex_relay(arrived[:3]), "relay_mix")
    wout = wout.reshape(D, D)
    (h2, pm, p, yp, ys, mm), (wgu2, wd2) = _mix_out(h1, xp, o_sb, gp, gs, wgrp, pool_scale, wbp, wba, wout,
                                                    exchange=_ex_relay(arrived[3:]))
    wd2 = wd2.reshape(DFF, D)
    dh2, dgu3, d_g2, loss_row, d_gf, n3, a3, dh3 = _ffn_last(h2, g2, wgu2, wd2, tgt, gf, "ffn2")

    def grad_gate_up(n, dgu, name, exchange=None):
        res = _wgrad(n, dgu, NSH, D, name, exchange=exchange)
        return [_halves(res)] if exchange is None else ([_halves(res[0])], res[1])

    def grad_down(a, dh, name, exchange=None):
        res = _wgrad(a, dh, 1, FFS, name, exchange=exchange)
        halves = lambda g: [_halves(g.reshape(NSH, DFF // NSH, D))]
        return halves(res) if exchange is None else (halves(res[0]), res[1])

    k_gu2, k_d2, k_gu1, k_d1, k_in = (("ffn2_w_gate_up",), ("ffn2_w_down",), ("ffn1_w_gate_up",),
                                      ("ffn1_w_down",), ("w_in",))
    pa = grad_gate_up(n3, dgu3, "wgrad_gu2") + grad_down(a3, dh3, "wgrad_d2")
    (dlg, dyp, dys, do_sb, dyg, dxp, d_scale), got_a = _mix_bwd_out(
        dh2, gp, gs, yp, ys, pm, wgrp, pool_scale, wbp, wba, wout, exchange=_ex_pair_swap(pa))
    chip_a = pair_sums(k_gu2 + k_d2, pa, got_a)
    kb = ("w_out", "w_branch_pool", "w_branch_attn")
    g_bp, g_ba, d_group, g_out = _wgrad_branches(p, dyp, o_sb, dys, pm, dyg, mm, dh2)
    pb = [_halves(g_out.reshape(NSH, D // NSH, D)), _halves(g_bp), _halves(g_ba)]
    k_a, k_in = k_gu2 + k_d2, k_in + kb
    sems_a, thru_a, token_a = _scatter_start(chip_a, "scatter_a_start")
    dq, dk, dv = _attn_bwd(q, k, v, do_sb, ctot, after=(token_a,))
    chip_a, owned_a = _scatter_wait(sems_a, thru_a, [dq], "scatter_a_wait")
    halves_a = chip_sums(k_a, chip_a, owned_a)
    dproj = (dxp, dq, dk, dv, dlg)
    (dh1, d_gm), both_a = _mix_bwd_in(dh2, h1, gm, dproj, win, exchange=_ex_share(halves_a))
    for i, k_ in enumerate(k_a):
        grad[k_] = both_a[i].reshape(wts[k_].shape)

    p_in = [_halves(_wgrad_in(u, dproj))] + pb
    p_d1, got_in = grad_down(a1, dh1, "wgrad_d1", exchange=_ex_pair_swap(p_in))
    sems_in, thru_in, token_in = _scatter_start(pair_sums(k_in, p_in, got_in), "scatter_in_start")
    dgu1, got_d1 = _ffn_bwd_act(dh1, gu1, wd1, "ffn1_bwd_act", exchange=_ex_pair_swap(p_d1), after=(token_in,))
    sems_d1, thru_d1, token_d1 = _scatter_start(pair_sums(k_d1, p_d1, got_d1), "scatter_d1_start")
    p_gu1 = [_halves(_wgrad(n1, dgu1, NSH, D, "wgrad_gu1", after=(token_in, token_d1)))]
    sems_w, thru_w, token_w = _swap_start(p_gu1, "swap_gu1_start")
    chip_in, owned_in = _scatter_wait(sems_in, thru_in, [token_w], "scatter_in_wait")
    chip_d1, owned_d1 = _scatter_wait(sems_d1, thru_d1, [token_w], "scatter_d1_wait")
    halves_in = chip_sums(k_in, chip_in, owned_in)
    p_gu1, got_gu1 = _swap_wait(sems_w, thru_w, halves_in, "swap_gu1_wait")
    sems, thru, token = _scatter_start(pair_sums(k_gu1, p_gu1, got_gu1), "scatter_gu1_start")
    sems_h, thru_h, token_h = _share_start(halves_in, [token], "share_in_start")
    adamw(k_a, after=(token_h,))
    landed = _share_wait(sems_h, thru_h, [delta[k_a[0]]], "share_in_wait")
    for i, k_ in enumerate(k_in):
        grad[k_] = landed[i].reshape(wts[k_].shape)
    adamw(k_in)
    dx, d_g1 = _ffn_bwd_in(dh1, x0, g1, dgu1, wgu1, "ffn1_bwd_in", after=(token,))
    small_g = dict(ffn1_norm=d_g1, mix_norm=d_gm, ffn2_norm=d_g2, final_norm=d_gf, pool_scale=d_scale,
                   pool_w_group=d_group, loss=loss_row)
    dev = 4 * lax.axis_index("x") + 2 * lax.axis_index("y") + lax.axis_index("c")
    slots = lax.dynamic_update_slice(jnp.zeros((8, SMALL_ROWS, 128), F32), _pack_small(small_g)[None], (dev, 0, 0))
    chip_gu1, owned_gu1 = _scatter_wait(sems, thru, [dx] + [delta[k_] for k_ in k_a + k_in], "scatter_gu1_wait")
    halves_last = chip_sums(k_d1 + k_gu1, chip_d1 + chip_gu1, owned_d1 + owned_gu1)
    sems_l, thru_l, token_l = _share_start(halves_last, [], "share_last_start")
    sems_s, slots, token_s = _small_gather_start(slots, "small_gather_start", after=(token_l,))
    both = _share_wait(sems_l, thru_l, [token_s], "share_last_wait")
    grad["ffn1_w_down"] = both[0].reshape(ffn1_w_down.shape)
    grad["ffn1_w_gate_up"] = both[1].reshape(ffn1_w_gate_up.shape)
    adamw(k_d1 + k_gu1, after=(token_s,))
    gathered = _small_gather_wait(sems_s, slots, [delta[k_] for k_ in k_d1 + k_gu1], "small_gather_wait")
    gathered = gathered.reshape(8 * SMALL_ROWS, 128)
    results = _small_update(gathered, _pack_small(wts), _pack_small(mom), _pack_small(var), _small_entries(wts))
    for dst, entries in zip((grad, delta, new_m, new_v), results):
        for k_, rows in zip(SMALL, entries):
            if k_ in wts:
                dst[k_] = rows.reshape(wts[k_].shape)
            elif dst is grad:
                loss = rows[0, 0]
    return (loss, dx[None], *[grad[k_] for k_ in ORDER], *[delta[k_] for k_ in ORDER],
            *[new_m[k_] for k_ in ORDER], *[new_v[k_] for k_ in ORDER])
```

```python
import dataclasses
import functools

import jax
import jax.numpy as jnp
from jax import lax
from jax.experimental import pallas as pl
from jax.experimental.pallas import tpu as pltpu

F32 = jnp.float32
BF16 = jnp.bfloat16

S = 2048
D = 1024
DFF = 2816
FFS = 2 * DFF // 4
NSH = 4
PW = 512
PG = 128
POOL_WINDOWS = (2, 4, 8, 16)
HALO = 16
SBW = 512
DH = 64
EPS = 1e-6
SCALE = 0.125
LOG2E = 1.4426950408889634
TA = 256
QB = 2
MIB = 1024 * 1024

LR, B1, B2, AEPS, WD, STEP = 0.001, 0.9, 0.999, 1e-08, 0.01, 10

_VM = pl.BlockSpec(memory_space=pltpu.VMEM)
_ANY = pl.BlockSpec(memory_space=pl.ANY)
MESH = pl.DeviceIdType.MESH
SIBLING_PAIR_ID = 1


def _nn(a, b):
    return jnp.dot(a, b, preferred_element_type=F32)


def _nt(a, b):
    return lax.dot_general(a, b, (((1,), (1,)), ((), ())), preferred_element_type=F32)


def _tn(a, b):
    return lax.dot_general(a, b, (((0,), (0,)), ((), ())), preferred_element_type=F32)


def _params(sem, vmem_mib):
    return pltpu.CompilerParams(dimension_semantics=sem, vmem_limit_bytes=vmem_mib * MIB)


def _rows(tm, width):
    return pl.BlockSpec((tm, width), lambda i: (i, 0))


def _fixed(shape):
    return pl.BlockSpec(shape, lambda *_: (0,) * len(shape))


def _sds(shape, dtype):
    return pltpu.HBM(shape, dtype)


def _in_hbm(args):
    return [pltpu.with_memory_space_constraint(a, pltpu.HBM) for a in args]


def _stage(pairs):
    pieces = 4

    def copy_all(sems):
        copies = []
        for src, dst in pairs:
            step = src.shape[0] // pieces
            for p in range(pieces):
                part = pl.ds(p * step, step)
                if len(dst.shape) == len(src.shape):
                    piece = (src.at[part], dst.at[part])
                else:
                    piece = (src.at[p], dst.at[:, pl.ds(p * src.shape[2], src.shape[2])])
                copies.append(pltpu.make_async_copy(*piece, sems.at[len(copies)]))
        for c in copies:
            c.start()
        for c in copies:
            c.wait()

    @pl.when(pl.program_id(0) == 0)
    def _():
        pl.run_scoped(copy_all, pltpu.SemaphoreType.DMA((pieces * len(pairs),)))


def _vmem_like(*arrays):
    return [pltpu.VMEM(a.shape, a.dtype) for a in arrays]


def _vmem_wide(w):
    return pltpu.VMEM((w.shape[1], w.shape[0] * w.shape[2]), w.dtype)


FF_CHUNKS = ((0, 1536), (1536, DFF - 1536))


def _wide_columns(src, dst, c0, cn):
    width, out = src.shape[2], []
    for p in range(src.shape[0]):
        lo, hi = max(c0, p * width), min(c0 + cn, (p + 1) * width)
        if lo < hi:
            out.append((src.at[p, :, pl.ds(lo - p * width, hi - lo)], dst.at[:, pl.ds(lo, hi - lo)]))
    return out


def _staged(groups, compute):
    first = pl.program_id(0) == 0

    def with_copies(sems):
        copies = []
        for group in groups:
            base = sum(len(g) for g in copies)
            copies.append([pltpu.make_async_copy(s, d, sems.at[base + i]) for i, (s, d) in enumerate(group)])
        for group in copies:
            for c in group:
                c.start()

        def ready(k):
            for c in copies[k]:
                c.wait()

        compute(ready)

    @pl.when(first)
    def _():
        pl.run_scoped(with_copies, pltpu.SemaphoreType.DMA((sum(len(g) for g in groups),)))

    @pl.when(jnp.logical_not(first))
    def _():
        compute(lambda k: None)


class Exchange:
    def __init__(self, arrays, landing, aliases, n_sems, start, finish, sibling_only=False):
        self.arrays, self.landing, self.aliases, self.n_sems = list(arrays), list(landing), dict(aliases), n_sems
        self.start, self.finish = start, finish
        self.sibling_only = sibling_only

    def enter(self):
        if self.sibling_only:
            barrier = pltpu.get_barrier_semaphore()
            sibling = (lax.axis_index("x"), lax.axis_index("y"), 1 - lax.axis_index("c"))
            pl.semaphore_signal(barrier, inc=1, device_id=sibling, device_id_type=MESH)
            pl.semaphore_wait(barrier, 1)

    def params(self, compiler_params=None):
        kw = dict(collective_id=SIBLING_PAIR_ID) if self.sibling_only else {}
        if compiler_params is None:
            return pltpu.CompilerParams(**kw)
        return dataclasses.replace(compiler_params, **kw)


def _call(body, args, *, name, grid, in_specs, out_specs, out_shape, scratch_shapes=(), compiler_params=None,
          exchange=None, free=(), after=()):
    args = [a if i in free else pltpu.with_memory_space_constraint(a, pltpu.HBM) for i, a in enumerate(args)]
    if exchange is None:
        n_in = len(in_specs)

        def plain(*refs):
            body(*refs[:n_in], *refs[n_in + len(after):])

        return pl.pallas_call(plain, name=name, grid=grid, in_specs=list(in_specs) + [_ANY] * len(after),
                              out_specs=out_specs, out_shape=out_shape, scratch_shapes=list(scratch_shapes),
                              compiler_params=compiler_params)(*args, *after)
    ex = exchange
    n_in, n_out, n_scr = len(in_specs), len(out_specs), len(scratch_shapes)
    na, nl = len(ex.arrays), len(ex.landing)

    def hosted(*refs):
        at = [0]

        def take(n):
            at[0] += n
            return refs[at[0] - n:at[0]]

        k_in, _, e_in, k_out, e_out, k_scr = take(n_in), take(len(after)), take(na), take(n_out), take(nl), take(n_scr)
        ssem, rsem = take(2)
        ids = [pl.program_id(a) for a in range(len(grid))]
        first = functools.reduce(jnp.logical_and, [i == 0 for i in ids])
        last = functools.reduce(jnp.logical_and, [i == g - 1 for i, g in zip(ids, grid)])

        @pl.when(first)
        def _():
            ex.enter()
            ex.start(e_in, e_out, ssem, rsem)

        body(*k_in, *k_out, *k_scr)

        @pl.when(last)
        def _():
            ex.finish(e_in, e_out, ssem, rsem)

    outs = pl.pallas_call(
        hosted, name=name, grid=grid,
        in_specs=list(in_specs) + [_ANY] * (len(after) + na), out_specs=list(out_specs) + [_ANY] * nl,
        out_shape=list(out_shape) + ex.landing,
        scratch_shapes=list(scratch_shapes) + [pltpu.SemaphoreType.DMA((ex.n_sems,))] * 2,
        input_output_aliases={n_in + len(after) + i: n_out + j for i, j in ex.aliases.items()},
        compiler_params=ex.params(compiler_params),
    )(*args, *after, *_in_hbm(ex.arrays))
    return outs[:n_out], outs[n_out:]


def _exchange_alone(ex, name, after=()):
    na, nl = len(ex.arrays), len(ex.landing)

    def body(*refs):
        outs = refs[na + len(after):na + len(after) + nl]
        ex.enter()
        ex.start(refs[:na], outs, refs[-2], refs[-1])
        ex.finish(refs[:na], outs, refs[-2], refs[-1])

    return pl.pallas_call(
        body, name=name, in_specs=[_ANY] * (na + len(after)), out_specs=[_ANY] * nl,
        out_shape=ex.landing, scratch_shapes=[pltpu.SemaphoreType.DMA((ex.n_sems,))] * 2,
        input_output_aliases=ex.aliases, compiler_params=ex.params(),
    )(*_in_hbm(ex.arrays), *after)


_HBM = pl.BlockSpec(memory_space=pltpu.HBM)
_SEM = pl.BlockSpec(memory_space=pltpu.SEMAPHORE)
_EFFECT = pltpu.SideEffectType.DATAFLOW_SIDE_EFFECTING


def _scatter_copies(srcs, lands, ssems, rsems):
    x, y, c, chips = _place()
    return [_remote(srcs[w].at[2 * px + py], lands[w].at[k], ssems[3 * w + k], rsems[3 * w + k], (px, py, c))
            for w in range(len(srcs)) for k, (px, py) in enumerate(chips)]


def _scatter_start(parts, name):
    parts = list(parts)
    n, ncp = len(parts), 3 * len(parts)
    lands = [lax.empty((3,) + p.shape[1:], p.dtype) for p in parts]

    def body(*refs):
        srcs, land_refs = refs[:n], refs[n:2 * n]
        ssems, rsems = refs[2 * n:2 * n + ncp], refs[2 * n + ncp:2 * n + 2 * ncp]
        for cp in _scatter_copies(srcs, land_refs, ssems, rsems):
            cp.start()
        token = refs[-1]
        token[...] = jnp.zeros_like(token)

    outs = pl.pallas_call(
        body, name=name,
        out_shape=([pltpu.SemaphoreType.DMA(())] * (2 * ncp) + [pltpu.HBM(a.shape, a.dtype) for a in parts + lands]
                   + [jax.ShapeDtypeStruct((8, 128), F32)]),
        in_specs=[_HBM] * (2 * n), out_specs=[_SEM] * (2 * ncp) + [_HBM] * (2 * n) + [_VM],
        input_output_aliases={i: 2 * ncp + i for i in range(2 * n)},
        compiler_params=pltpu.CompilerParams(has_side_effects=_EFFECT),
    )(*_in_hbm(parts), *_in_hbm(lands))
    sems, thru, token = outs[:2 * ncp], outs[2 * ncp:2 * ncp + 2 * n], outs[-1]
    return sems, thru, token


def _scatter_wait(sems, thru, after, name):
    n = len(thru) // 2
    ncp = 3 * n

    def body(*refs):
        srcs, land_refs = refs[:n], refs[n:2 * n]
        ssems, rsems = refs[2 * n:2 * n + ncp], refs[2 * n + ncp:2 * n + 2 * ncp]
        for cp in _scatter_copies(srcs, land_refs, ssems, rsems):
            cp.wait_send()
            cp.wait_recv()

    outs = pl.pallas_call(
        body, name=name, out_shape=[pltpu.HBM(a.shape, a.dtype) for a in thru],
        in_specs=[_HBM] * (2 * n) + [_SEM] * (2 * ncp) + [_ANY] * len(after), out_specs=[_HBM] * (2 * n),
        input_output_aliases={i: i for i in range(2 * n)},
        compiler_params=pltpu.CompilerParams(has_side_effects=_EFFECT),
    )(*thru, *sems, *after)
    return outs[:n], outs[n:]


def _swap_copies(srcs, lands, ssems, rsems):
    x, y, c, _ = _place()
    return [_remote(srcs[w].at[:, 1 - c], lands[w], ssems[w], rsems[w], (x, y, 1 - c)) for w in range(len(srcs))]


def _swap_start(grads, name):
    grads = list(grads)
    n = len(grads)
    lands = [lax.empty((NSH,) + g.shape[2:], g.dtype) for g in grads]

    def body(*refs):
        barrier = pltpu.get_barrier_semaphore()
        sibling = (lax.axis_index("x"), lax.axis_index("y"), 1 - lax.axis_index("c"))
        pl.semaphore_signal(barrier, inc=1, device_id=sibling, device_id_type=MESH)
        pl.semaphore_wait(barrier, 1)
        for cp in _swap_copies(refs[:n], refs[n:2 * n], refs[2 * n:3 * n], refs[3 * n:4 * n]):
            cp.start()
        refs[-1][...] = jnp.zeros_like(refs[-1])

    outs = pl.pallas_call(
        body, name=name,
        out_shape=([pltpu.SemaphoreType.DMA(())] * (2 * n) + [pltpu.HBM(a.shape, a.dtype) for a in grads + lands]
                   + [jax.ShapeDtypeStruct((8, 128), F32)]),
        in_specs=[_HBM] * (2 * n), out_specs=[_SEM] * (2 * n) + [_HBM] * (2 * n) + [_VM],
        input_output_aliases={i: 2 * n + i for i in range(2 * n)},
        compiler_params=pltpu.CompilerParams(has_side_effects=_EFFECT, collective_id=SIBLING_PAIR_ID),
    )(*_in_hbm(grads), *_in_hbm(lands))
    return outs[:2 * n], outs[2 * n:4 * n], outs[-1]


def _swap_wait(sems, thru, after, name):
    n = len(thru) // 2

    def body(*refs):
        for cp in _swap_copies(refs[:n], refs[n:2 * n], refs[2 * n:3 * n], refs[3 * n:4 * n]):
            cp.wait_send()
            cp.wait_recv()

    outs = pl.pallas_call(
        body, name=name, out_shape=[pltpu.HBM(a.shape, a.dtype) for a in thru],
        in_specs=[_HBM] * (2 * n) + [_SEM] * (2 * n) + [_ANY] * len(after), out_specs=[_HBM] * (2 * n),
        input_output_aliases={i: i for i in range(2 * n)},
        compiler_params=pltpu.CompilerParams(has_side_effects=_EFFECT),
    )(*thru, *sems, *after)
    return outs[:n], outs[n:]


def _share_copies(bufs, ssems, rsems, sending):
    x, y, c, _ = _place()
    out = []
    for w, ref in enumerate(bufs):
        slot = ref.at[c if sending else 1 - c]
        out.append(_remote(slot, slot, ssems[w], rsems[w], (x, y, 1 - c)))
    return out


def _share_start(bufs, after, name):
    bufs = list(bufs)
    n = len(bufs)

    def body(*refs):
        barrier = pltpu.get_barrier_semaphore()
        sibling = (lax.axis_index("x"), lax.axis_index("y"), 1 - lax.axis_index("c"))
        pl.semaphore_signal(barrier, inc=1, device_id=sibling, device_id_type=MESH)
        pl.semaphore_wait(barrier, 1)
        at = n + len(after)
        for cp in _share_copies(refs[:n], refs[at:at + n], refs[at + n:at + 2 * n], True):
            cp.start()
        refs[-1][...] = jnp.zeros_like(refs[-1])

    outs = pl.pallas_call(
        body, name=name,
        out_shape=([pltpu.SemaphoreType.DMA(())] * (2 * n) + [pltpu.HBM(a.shape, a.dtype) for a in bufs]
                   + [jax.ShapeDtypeStruct((8, 128), F32)]),
        in_specs=[_HBM] * n + [_ANY] * len(after), out_specs=[_SEM] * (2 * n) + [_HBM] * n + [_VM],
        input_output_aliases={i: 2 * n + i for i in range(n)},
        compiler_params=pltpu.CompilerParams(has_side_effects=_EFFECT, collective_id=SIBLING_PAIR_ID),
    )(*_in_hbm(bufs), *after)
    return outs[:2 * n], outs[2 * n:3 * n], outs[-1]


def _share_wait(sems, thru, after, name):
    n = len(thru)

    def body(*refs):
        for cp in _share_copies(refs[:n], refs[n:2 * n], refs[2 * n:3 * n], True):
            cp.wait_send()
        for cp in _share_copies(refs[:n], refs[n:2 * n], refs[2 * n:3 * n], False):
            cp.wait_recv()

    return pl.pallas_call(
        body, name=name, out_shape=[pltpu.HBM(a.shape, a.dtype) for a in thru],
        in_specs=[_HBM] * n + [_SEM] * (2 * n) + [_ANY] * len(after), out_specs=[_HBM] * n,
        input_output_aliases={i: i for i in range(n)},
        compiler_params=pltpu.CompilerParams(has_side_effects=_EFFECT),
    )(*thru, *sems, *after)


def _gather_copies(bufs, ssems, rsems, sending):
    x, y, c, chips = _place()
    out = []
    for w, ref in enumerate(bufs):
        half = ref.shape[1] // 2
        for k, (px, py) in enumerate(chips):
            rows = ref.at[2 * x + y if sending else 2 * px + py, pl.ds(c * half, half)]
            out.append(_remote(rows, rows, ssems[3 * w + k], rsems[3 * w + k], (px, py, c)))
    return out


def _gather_start(bufs, after, name):
    n, ncp = len(bufs), 3 * len(bufs)

    def body(*refs):
        ssems, rsems = refs[n + len(after):n + len(after) + ncp], refs[n + len(after) + ncp:n + len(after) + 2 * ncp]
        for cp in _gather_copies(refs[:n], ssems, rsems, True):
            cp.start()
        token = refs[-1]
        token[...] = jnp.zeros_like(token)

    outs = pl.pallas_call(
        body, name=name,
        out_shape=([pltpu.SemaphoreType.DMA(())] * (2 * ncp) + [pltpu.HBM(a.shape, a.dtype) for a in bufs]
                   + [jax.ShapeDtypeStruct((8, 128), F32)]),
        in_specs=[_HBM] * n + [_ANY] * len(after), out_specs=[_SEM] * (2 * ncp) + [_HBM] * n + [_VM],
        input_output_aliases={i: 2 * ncp + i for i in range(n)},
        compiler_params=pltpu.CompilerParams(has_side_effects=_EFFECT),
    )(*_in_hbm(bufs), *after)
    return outs[:2 * ncp], outs[2 * ncp:2 * ncp + n], outs[-1]


def _gather_wait(sems, thru, after, name):
    n = len(thru)
    ncp = 3 * n

    def body(*refs):
        ssems, rsems = refs[n:n + ncp], refs[n + ncp:n + 2 * ncp]
        for cp in _gather_copies(refs[:n], ssems, rsems, True):
            cp.wait_send()
        for cp in _gather_copies(refs[:n], ssems, rsems, False):
            cp.wait_recv()

    return pl.pallas_call(
        body, name=name, out_shape=[pltpu.HBM(a.shape, a.dtype) for a in thru],
        in_specs=[_HBM] * n + [_SEM] * (2 * ncp) + [_ANY] * len(after), out_specs=[_HBM] * n,
        input_output_aliases={i: i for i in range(n)},
        compiler_params=pltpu.CompilerParams(has_side_effects=_EFFECT),
    )(*thru, *sems, *after)


def _rms(x):
    r = lax.rsqrt(jnp.mean(x * x, axis=-1, keepdims=True) + EPS)
    return r, x * r


def _rms_bwd(dn, xr, r, gain):
    dng = dn * gain
    dx = r * (dng - xr * jnp.mean(dng * xr, axis=-1, keepdims=True))
    return dx, jnp.sum(dn * xr, axis=0, keepdims=True)


def _ffn_weight_groups(wgu_hbm, wgu_ref, wd_hbm, wd_ref):
    groups = []
    for c0, cn in FF_CHUNKS:
        groups += [_wide_columns(wgu_hbm, wgu_ref, c0, cn), _wide_columns(wgu_hbm, wgu_ref, DFF + c0, cn),
                   [(wd_hbm.at[pl.ds(c0, cn)], wd_ref.at[pl.ds(c0, cn)])]]
    return groups


def _ffn_fwd(x, gain, wgu, wd, name, exchange=None):
    tm = 256

    def body(x_ref, g_ref, wgu_hbm, wd_hbm, h_ref, n_ref, gu_ref, a_ref, wgu_ref, wd_ref):
        def compute(ready):
            x = x_ref[...]
            _, xr = _rms(x)
            n = (xr * g_ref[...]).astype(BF16)
            n_ref[...] = n
            acc = jnp.zeros((tm, D), F32)
            for i, (c0, cn) in enumerate(FF_CHUNKS):
                ready(3 * i)
                g = _nn(n, wgu_ref[:, c0:c0 + cn])
                ready(3 * i + 1)
                u = _nn(n, wgu_ref[:, DFF + c0:DFF + c0 + cn])
                gu_ref[:, c0:c0 + cn] = g.astype(BF16)
                gu_ref[:, DFF + c0:DFF + c0 + cn] = u.astype(BF16)
                half_act = (0.5 * (g * jax.nn.sigmoid(g) * u)).astype(BF16)
                a_ref[:, c0:c0 + cn] = half_act
                ready(3 * i + 2)
                acc = acc + _nn(half_act, wd_ref[c0:c0 + cn, :])
            h_ref[...] = x + acc

        _staged(_ffn_weight_groups(wgu_hbm, wgu_ref, wd_hbm, wd_ref), compute)

    return _call(
        body, (x, gain, wgu, wd), name=name, grid=(S // tm,),
        in_specs=[_rows(tm, D), _fixed((1, D)), _ANY, _ANY],
        out_specs=[_rows(tm, D), _rows(tm, D), _rows(tm, 4 * FFS), _rows(tm, DFF)],
        out_shape=[_sds((S, D), F32), _sds((S, D), BF16), _sds((S, 4 * FFS), BF16), _sds((S, DFF), BF16)],
        scratch_shapes=[_vmem_wide(wgu)] + _vmem_like(wd),
        compiler_params=_params(("arbitrary",), 56), exchange=exchange)


def _ffn_last(x, gain, wgu, wd, target, gf, name):
    tm = 256

    def body(x_ref, g_ref, wgu_hbm, wd_hbm, t_ref, gf_ref, dx_ref, dgu_ref, dg_ref, loss_ref, dgf_ref, n_ref,
             a_ref, dh_ref, wgu_ref, wd_ref):
        @pl.when(pl.program_id(0) == 0)
        def _():
            dg_ref[...] = jnp.zeros_like(dg_ref)
            dgf_ref[...] = jnp.zeros_like(dgf_ref)
            loss_ref[...] = jnp.zeros_like(loss_ref)

        def compute():
            x = x_ref[...]
            r0, xr = _rms(x)
            n = (xr * g_ref[...]).astype(BF16)
            n_ref[...] = n
            acc = jnp.zeros((tm, D), F32)
            kept = []
            for c0, cn in FF_CHUNKS:
                g = _nn(n, wgu_ref[:, c0:c0 + cn])
                u = _nn(n, wgu_ref[:, DFF + c0:DFF + c0 + cn])
                kept.append((g.astype(BF16), u.astype(BF16)))
                half_act = (0.5 * (g * jax.nn.sigmoid(g) * u)).astype(BF16)
                a_ref[:, c0:c0 + cn] = half_act
                acc = acc + _nn(half_act, wd_ref[c0:c0 + cn, :])
            h = x + acc
            gf = gf_ref[...]
            r, hr = _rms(h)
            err = hr * gf - t_ref[...]
            dh, dgain_f = _rms_bwd(err * (1.0 / D), hr, r, gf)
            dhb = dh.astype(BF16)
            dh_ref[...] = dhb
            dn = jnp.zeros((tm, D), F32)
            for (c0, cn), (gb, ub) in zip(FF_CHUNKS, kept):
                g, u = gb.astype(F32), ub.astype(F32)
                da = 0.5 * _nt(dhb, wd_ref[c0:c0 + cn, :])
                sg = jax.nn.sigmoid(g)
                dgb = (da * u * (sg * (1.0 + g * (1.0 - sg)))).astype(BF16)
                dub = (da * (g * sg)).astype(BF16)
                dgu_ref[:, c0:c0 + cn] = dgb
                dgu_ref[:, DFF + c0:DFF + c0 + cn] = dub
                dn = dn + _nt(dgb, wgu_ref[:, c0:c0 + cn]) + _nt(dub, wgu_ref[:, DFF + c0:DFF + c0 + cn])
            dx, dgain = _rms_bwd(dn, xr, r0, g_ref[...])
            dx_ref[...] = dh + dx
            dg_ref[...] += dgain
            dgf_ref[...] += dgain_f
            loss_ref[...] += jnp.full((1, 128), (0.5 / D) * jnp.sum(err * err), F32)

        _stage([(wgu_hbm, wgu_ref), (wd_hbm, wd_ref)])
        compute()

    return _call(
        body, (x, gain, wgu, wd, target, gf), name=name, grid=(S // tm,),
        in_specs=[_rows(tm, D), _fixed((1, D)), _ANY, _ANY, _rows(tm, D), _fixed((1, D))],
        out_specs=[_rows(tm, D), _rows(tm, 4 * FFS), _fixed((1, D)), _fixed((1, 128)), _fixed((1, D)),
                   _rows(tm, D), _rows(tm, DFF), _rows(tm, D)],
        out_shape=[_sds((S, D), F32), _sds((S, 4 * FFS), BF16), _sds((1, D), F32), _sds((1, 128), F32),
                   _sds((1, D), F32), _sds((S, D), BF16), _sds((S, DFF), BF16), _sds((S, D), BF16)],
        scratch_shapes=[_vmem_wide(wgu)] + _vmem_like(wd),
        compiler_params=_params(("arbitrary",), 58), free=(4, 5))


def _ffn_bwd_act(dh, gu, wd, name, exchange=None, after=()):
    tm = 512

    def body(dh_ref, gu_ref, wd_hbm, dgu_ref, wd_ref):
        def compute(ready):
            dhb = dh_ref[...].astype(BF16)
            for i, (c0, cn) in enumerate(FF_CHUNKS):
                g = gu_ref[:, c0:c0 + cn].astype(F32)
                u = gu_ref[:, DFF + c0:DFF + c0 + cn].astype(F32)
                ready(i)
                da = 0.5 * _nt(dhb, wd_ref[c0:c0 + cn, :])
                sg = jax.nn.sigmoid(g)
                dgu_ref[:, c0:c0 + cn] = (da * u * (sg * (1.0 + g * (1.0 - sg)))).astype(BF16)
                dgu_ref[:, DFF + c0:DFF + c0 + cn] = (da * (g * sg)).astype(BF16)

        _staged([[(wd_hbm.at[pl.ds(c0, cn)], wd_ref.at[pl.ds(c0, cn)])] for c0, cn in FF_CHUNKS], compute)

    res = _call(
        body, (dh, gu, wd), name=name, grid=(S // tm,),
        in_specs=[_rows(tm, D), _rows(tm, 4 * FFS), _ANY], out_specs=[_rows(tm, 4 * FFS)],
        out_shape=[_sds((S, 4 * FFS), BF16)], scratch_shapes=_vmem_like(wd),
        compiler_params=_params(("arbitrary",), 56), exchange=exchange, after=after)
    return res[0] if exchange is None else (res[0][0], res[1])


def _ffn_bwd_in(dh, x, gain, dgu, wgu, name, exchange=None, after=()):
    tm = 512

    def body(dh_ref, x_ref, g_ref, dgu_ref, wgu_hbm, dx_ref, dg_ref, wgu_ref):
        chunks = [(half + c0, cn) for half in (0, DFF) for c0, cn in FF_CHUNKS]

        @pl.when(pl.program_id(0) == 0)
        def _():
            dg_ref[...] = jnp.zeros_like(dg_ref)

        def compute(ready):
            dn = jnp.zeros((tm, D), F32)
            for k, (c0, cn) in enumerate(chunks):
                ready(k)
                dn = dn + _nt(dgu_ref[:, c0:c0 + cn], wgu_ref[:, c0:c0 + cn])
            r, xr = _rms(x_ref[...])
            dx, dgain = _rms_bwd(dn, xr, r, g_ref[...])
            dx_ref[...] = dh_ref[...] + dx
            dg_ref[...] += dgain

        _staged([_wide_columns(wgu_hbm, wgu_ref, c0, cn) for c0, cn in chunks], compute)

    return _call(
        body, (dh, x, gain, dgu, wgu), name=name, grid=(S // tm,),
        in_specs=[_rows(tm, D), _rows(tm, D), _fixed((1, D)), _rows(tm, 4 * FFS), _ANY],
        out_specs=[_rows(tm, D), _fixed((1, D))],
        out_shape=[_sds((S, D), F32), _sds((1, D), F32)],
        scratch_shapes=[_vmem_wide(wgu)],
        compiler_params=_params(("arbitrary",), 56), exchange=exchange, after=after)


def _mix_in(h, gain, w_in, after=()):
    tm = 512

    def body(h_ref, g_ref, w_hbm, u_ref, xp_ref, q_ref, k_ref, v_ref, gp_ref, gs_ref, w_ref):
        def compute(ready):
            _, hr = _rms(h_ref[...])
            u = (hr * g_ref[...]).astype(BF16)
            u_ref[...] = u
            ready(0)
            p0 = _nn(u, w_ref[0])
            xp_ref[...] = p0[:, :PW]
            q_ref[...] = p0[:, PW:].astype(BF16)
            ready(1)
            p1 = _nn(u, w_ref[1])
            k_ref[...] = p1[:, :SBW].astype(BF16)
            v_ref[...] = p1[:, SBW:].astype(BF16)
            ready(2)
            gp_ref[...] = jax.nn.sigmoid(_nn(u, w_ref[2])).astype(BF16)
            ready(3)
            gs_ref[...] = jax.nn.sigmoid(_nn(u, w_ref[3])).astype(BF16)

        _staged([[(w_hbm.at[j], w_ref.at[j])] for j in range(NSH)], compute)

    return _call(
        body, (h, gain, w_in), name="mix_in", grid=(S // tm,),
        in_specs=[_rows(tm, D), _fixed((1, D)), _ANY],
        out_specs=[_rows(tm, D), _rows(tm, PW), _rows(tm, SBW), _rows(tm, SBW), _rows(tm, SBW),
                   _rows(tm, D), _rows(tm, D)],
        out_shape=[_sds((S, D), BF16), _sds((S, PW), F32), _sds((S, SBW), BF16), _sds((S, SBW), BF16),
                   _sds((S, SBW), BF16), _sds((S, D), BF16), _sds((S, D), BF16)],
        scratch_shapes=_vmem_like(w_in),
        compiler_params=_params(("arbitrary",), 48), free=(1,), after=after)


def _hilo_dot(x, tri):
    hi = x.astype(BF16)
    lo = (x - hi.astype(F32)).astype(BF16)
    return _nn(hi, tri) + _nn(lo, tri)


def _log_terms(qk):
    z2 = qk * (SCALE * LOG2E)
    lb = jnp.minimum(z2, 0.0) - jnp.log2(1.0 + jnp.exp2(-jnp.abs(z2)))
    return lb, lb - z2


def _head_masks():
    lane = lax.broadcasted_iota(jnp.int32, (1, 2 * DH), 1)
    return (lane < DH, lane >= DH)


def _attn_fwd(q, k, v, exchange=None):
    T = TA

    def body(q_ref, k_ref, v_ref, o_ref, c_ref):
        i2 = 2 * pl.program_id(1)
        row = lax.broadcasted_iota(jnp.int32, (T, T), 0)
        col = lax.broadcasted_iota(jnp.int32, (T, T), 1)
        after = (row > col).astype(BF16)
        causal = col < row
        masks = _head_masks()
        qms = {}
        for b in range(QB):
            q2 = q_ref[b * T:(b + 1) * T, :]
            for h, hm in enumerate(masks):
                qms[b, h] = jnp.where(hm, q2, jnp.zeros_like(q2))

        def blocks(keys, pairs, carries, os):
            ks, vms = [], []
            for j in keys:
                rows = pl.ds(pl.multiple_of(j * T, T), T)
                vj = v_ref[rows, :]
                ks.append(k_ref[rows, :])
                vms.append([jnp.where(hm, vj, jnp.zeros_like(vj)) for hm in masks])
            units = [(n, h) for n in range(len(pairs)) for h in range(2)]
            qks = {(n, h): _nt(qms[pairs[n][0], h], ks[pairs[n][1]]) for n, h in units}
            lbs, l1ms = {}, {}
            for u in units:
                lbs[u], l1m = _log_terms(qks[u])
                l1ms[u] = jnp.where(causal, l1m, 0.0) if pairs[u[0]][2] else l1m
            cins = {u: _hilo_dot(l1ms[u], after) for u in units}
            carries, os = dict(carries), list(os)
            for n, h in units:
                b, key, diag = pairs[n]
                a = jnp.exp2(lbs[n, h] + cins[n, h] + carries[b, h])
                if diag:
                    a = jnp.where(causal, a, 0.0)
                os[b] = os[b] + _nn(a.astype(BF16), vms[key][h])
                carries[b, h] = carries[b, h] + jnp.sum(l1ms[n, h], axis=1, keepdims=True)
            return carries, tuple(os)

        carries = {(b, h): jnp.zeros((T, 1), F32) for b in range(QB) for h in range(2)}
        os = tuple(jnp.zeros((T, 2 * DH), F32) for _ in range(QB))
        carries, os = blocks([i2 + 1, i2], [(1, 0, True), (0, 1, True), (1, 1, False)], carries, os)
        carries, os = lax.fori_loop(
            0, i2 // 2,
            lambda t, c: blocks([i2 - 1 - 2 * t, i2 - 2 - 2 * t],
                                [(0, 0, False), (1, 0, False), (0, 1, False), (1, 1, False)], c[0], c[1]),
            (carries, os))
        for b in range(QB):
            o_ref[b * T:(b + 1) * T, :] = os[b].astype(BF16)
            c_ref[b * T:(b + 1) * T, :] = jnp.where(masks[0], carries[b, 0], carries[b, 1])

    blk = pl.BlockSpec((QB * T, 2 * DH), lambda p, i: (i, p))
    full = pl.BlockSpec((S, 2 * DH), lambda p, i: (0, p))
    return _call(
        body, (q, k, v), name="attn_fwd", grid=(SBW // (2 * DH), S // (QB * T)),
        in_specs=[blk, full, full], out_specs=[blk, blk],
        out_shape=[_sds((S, SBW), BF16), _sds((S, SBW), F32)],
        compiler_params=_params(("arbitrary", "arbitrary"), 40), exchange=exchange)


def _attn_bwd(q, k, v, do, ctot, after=()):
    T = TA
    nq = S // (QB * T)

    def body(q_ref, k_ref, v_ref, do_ref, c_ref, dq_ref, dk_ref, dv_ref, dk_acc, dv_acc):
        step = pl.program_id(1)
        i2 = 2 * step

        @pl.when(step == 0)
        def _():
            dk_acc[...] = jnp.zeros_like(dk_acc)
            dv_acc[...] = jnp.zeros_like(dv_acc)

        row = lax.broadcasted_iota(jnp.int32, (T, T), 0)
        col = lax.broadcasted_iota(jnp.int32, (T, T), 1)
        upto = (row <= col).astype(BF16)
        before = (row < col).astype(BF16)
        causal = col < row
        masks = _head_masks()
        qms, doms, ctots = {}, {}, {}
        for b in range(QB):
            q2, do2 = q_ref[b * T:(b + 1) * T, :], do_ref[b * T:(b + 1) * T, :]
            for h, hm in enumerate(masks):
                qms[b, h] = jnp.where(hm, q2, jnp.zeros_like(q2))
                doms[b, h] = jnp.where(hm, do2, jnp.zeros_like(do2))
                ctots[b, h] = c_ref[b * T:(b + 1) * T, h * DH:h * DH + 1]

        def blocks(keys, pairs, sums, dqs):
            rows = [pl.ds(pl.multiple_of(j * T, T), T) for j in keys]
            ks, vs = [k_ref[r, :] for r in rows], [v_ref[r, :] for r in rows]
            kms = [[jnp.where(hm, kj, jnp.zeros_like(kj)) for hm in masks] for kj in ks]
            units = [(n, h) for n in range(len(pairs)) for h in range(2)]
            qks = {(n, h): _nt(qms[pairs[n][0], h], ks[pairs[n][1]]) for n, h in units}
            das = {(n, h): _nt(doms[pairs[n][0], h], vs[pairs[n][1]]) for n, h in units}
            lbs, l1ms = {}, {}
            for u in units:
                lbs[u], l1m = _log_terms(qks[u])
                l1ms[u] = jnp.where(causal, l1m, 0.0) if pairs[u[0]][2] else l1m
            pins = {u: _hilo_dot(l1ms[u], upto) for u in units}
            sums = dict(sums)
            a_s, dls, cps = {}, {}, {}
            for n, h in units:
                b, _, diag = pairs[n]
                cl, cp = sums[b, h]
                a = jnp.exp2(lbs[n, h] + (ctots[b, h] - cl) - pins[n, h])
                if diag:
                    a = jnp.where(causal, a, 0.0)
                a_s[n, h] = a.astype(BF16)
                dls[n, h] = das[n, h] * a
                cps[n, h] = cp
                sums[b, h] = (cl + jnp.sum(l1ms[n, h], axis=1, keepdims=True),
                              cp + jnp.sum(dls[n, h], axis=1, keepdims=True))
            pexs = {u: _hilo_dot(dls[u], before) for u in units}
            dzbs = {}
            for u in units:
                dz = dls[u] - jnp.exp2(lbs[u]) * (dls[u] + pexs[u] + cps[u])
                if pairs[u[0]][2]:
                    dz = jnp.where(causal, dz, 0.0)
                dzbs[u] = dz.astype(BF16)
            dqs = list(dqs)
            for n, h in units:
                dqs[pairs[n][0]] = dqs[pairs[n][0]] + _nn(dzbs[n, h], kms[pairs[n][1]][h])
            for key, r in enumerate(rows):
                mine = [(n, h) for n, h in units if pairs[n][1] == key]
                dk_acc[r, :] += functools.reduce(jnp.add, [_tn(dzbs[u], qms[pairs[u[0]][0], u[1]]) for u in mine])
                dv_acc[r, :] += functools.reduce(jnp.add, [_tn(a_s[u], doms[pairs[u[0]][0], u[1]]) for u in mine])
            return sums, tuple(dqs)

        zero = jnp.zeros((T, 1), F32)
        sums = {(b, h): (zero, zero) for b in range(QB) for h in range(2)}
        dqs = tuple(jnp.zeros((T, 2 * DH), F32) for _ in range(QB))
        sums, dqs = lax.fori_loop(
            0, i2 // 2,
            lambda t, c: blocks([2 * t, 2 * t + 1],
                                [(0, 0, False), (1, 0, False), (0, 1, False), (1, 1, False)], c[0], c[1]),
            (sums, dqs))
        _, dqs = blocks([i2, i2 + 1], [(0, 0, True), (1, 0, False), (1, 1, True)], sums, dqs)
        for b in range(QB):
            dq_ref[b * T:(b + 1) * T, :] = (dqs[b] * SCALE).astype(BF16)

        @pl.when(step == nq - 1)
        def _():
            dk_ref[...] = (dk_acc[...] * SCALE).astype(BF16)
            dv_ref[...] = dv_acc[...].astype(BF16)

    blk = pl.BlockSpec((QB * T, 2 * DH), lambda p, i: (i, p))
    full = pl.BlockSpec((S, 2 * DH), lambda p, i: (0, p))
    return _call(
        body, (q, k, v, do, ctot), name="attn_bwd", grid=(SBW // (2 * DH), nq),
        in_specs=[blk, full, full, blk, blk], out_specs=[blk, full, full],
        out_shape=[_sds((S, SBW), BF16), _sds((S, SBW), BF16), _sds((S, SBW), BF16)],
        scratch_shapes=[pltpu.VMEM((S, 2 * DH), F32), pltpu.VMEM((S, 2 * DH), F32)],
        compiler_params=_params(("arbitrary", "arbitrary"), 40), after=after)


def _pool_counts(first_row, tm):
    pos = first_row + lax.broadcasted_iota(jnp.int32, (tm, 1), 0)
    return [jnp.minimum(pos + 1, w).astype(F32) for w in POOL_WINDOWS]


def _mix_out(h, xp, o_sb, gp, gs, w_group, scale, w_bp, w_ba, w_out, exchange=None):
    tm = 512

    def body(h_ref, xp_ref, o_ref, gp_ref, gs_ref, wg_hbm, sc_ref, wbp_hbm, wba_hbm, wo_hbm,
             h2_ref, pm_ref, p_ref, yp_ref, ys_ref, m_ref, halo, wg_ref, wbp_ref, wba_ref, wo_ref):
        _stage([(wg_hbm, wg_ref), (wbp_hbm, wbp_ref), (wba_hbm, wba_ref), (wo_hbm, wo_ref)])
        i = pl.program_id(0)

        @pl.when(i == 0)
        def _():
            halo[...] = jnp.zeros_like(halo)

        xp = xp_ref[...]
        ext = jnp.concatenate([halo[...], xp], axis=0)
        halo[...] = xp[tm - HALO:, :]
        counts = _pool_counts(i * tm, tm)
        for gi in range(len(POOL_WINDOWS)):
            lanes = slice(gi * PG, (gi + 1) * PG)
            win = ext[:, lanes]
            for step in range(gi + 1):
                win = win + pltpu.roll(win, 1 << step, 0)
            pm = (win[HALO:, :] / counts[gi] - xp[:, lanes]).astype(BF16)
            pm_ref[:, lanes] = pm
            p_ref[:, lanes] = (_nn(pm, wg_ref[gi]) * sc_ref[:, lanes]).astype(BF16)
        pb = p_ref[...]
        ob = o_ref[...]
        for j in range(NSH):
            cols = slice(j * (D // NSH), (j + 1) * (D // NSH))
            yp = _nn(pb, wbp_ref[j])
            ys = _nn(ob, wba_ref[j])
            yp_ref[:, cols] = yp.astype(BF16)
            ys_ref[:, cols] = ys.astype(BF16)
            m_ref[:, cols] = (gp_ref[:, cols].astype(F32) * yp + gs_ref[:, cols].astype(F32) * ys).astype(BF16)
        h2_ref[...] = h_ref[...] + _nn(m_ref[...], wo_ref[...])

    return _call(
        body, (h, xp, o_sb, gp, gs, w_group, scale, w_bp, w_ba, w_out), name="mix_out", grid=(S // tm,),
        in_specs=[_rows(tm, D), _rows(tm, PW), _rows(tm, SBW), _rows(tm, D), _rows(tm, D),
                  _ANY, _fixed((1, PW)), _ANY, _ANY, _ANY],
        out_specs=[_rows(tm, D), _rows(tm, PW), _rows(tm, PW), _rows(tm, D), _rows(tm, D), _rows(tm, D)],
        out_shape=[_sds((S, D), F32), _sds((S, PW), BF16), _sds((S, PW), BF16), _sds((S, D), BF16),
                   _sds((S, D), BF16), _sds((S, D), BF16)],
        scratch_shapes=[pltpu.VMEM((HALO, PW), F32)] + _vmem_like(w_group, w_bp, w_ba, w_out),
        compiler_params=_params(("arbitrary",), 48), free=(5, 6), exchange=exchange)


def _mix_bwd_out(dh, gp, gs, yp, ys, pm, w_group, scale, w_bp, w_ba, w_out, exchange=None):
    tm = 512
    nt = S // tm

    def body(dh_ref, gp_ref, gs_ref, yp_ref, ys_ref, pm_ref, wg_hbm, sc_ref, wbp_hbm, wba_hbm, wo_hbm,
             dlg_ref, dyp_ref, dys_ref, do_ref, dyg_ref, dxp_ref, dsc_ref, halo, wg_ref, wbp_ref, wba_ref, wo_ref):
        _stage([(wg_hbm, wg_ref), (wbp_hbm, wbp_ref), (wba_hbm, wba_ref), (wo_hbm, wo_ref)])
        step = pl.program_id(0)

        @pl.when(step == 0)
        def _():
            halo[...] = jnp.zeros_like(halo)
            dsc_ref[...] = jnp.zeros_like(dsc_ref)

        dm = _nt(dh_ref[...].astype(BF16), wo_ref[...])
        gp = gp_ref[...].astype(F32)
        gs = gs_ref[...].astype(F32)
        yp = yp_ref[...].astype(F32)
        ys = ys_ref[...].astype(F32)
        dlg_ref[:, :D] = (dm * yp * gp * (1.0 - gp)).astype(BF16)
        dlg_ref[:, D:] = (dm * ys * gs * (1.0 - gs)).astype(BF16)
        dyp_ref[...] = (dm * gp).astype(BF16)
        dys_ref[...] = (dm * gs).astype(BF16)
        dp = jnp.zeros((tm, PW), F32)
        do = jnp.zeros((tm, SBW), F32)
        for j in range(NSH):
            cols = slice(j * (D // NSH), (j + 1) * (D // NSH))
            dp = dp + _nt(dyp_ref[:, cols], wbp_ref[j])
            do = do + _nt(dys_ref[:, cols], wba_ref[j])
        do_ref[...] = do.astype(BF16)
        counts = _pool_counts((nt - 1 - step) * tm, tm)
        dscale = []
        for gi in range(len(POOL_WINDOWS)):
            lanes = slice(gi * PG, (gi + 1) * PG)
            dpg = dp[:, lanes]
            dscale.append(jnp.sum(dpg * _nn(pm_ref[:, lanes], wg_ref[gi]), axis=0, keepdims=True))
            dyg = (dpg * sc_ref[:, lanes]).astype(BF16)
            dyg_ref[:, lanes] = dyg
            dpm = _nt(dyg, wg_ref[gi])
            per = dpm / counts[gi]
            win = jnp.concatenate([per, halo[:, lanes]], axis=0)
            halo[:, lanes] = per[:HALO, :]
            for s in range(gi + 1):
                win = win + pltpu.roll(win, tm + HALO - (1 << s), 0)
            dxp_ref[:, lanes] = (win[:tm, :] - dpm).astype(BF16)
        dsc_ref[...] += jnp.concatenate(dscale, axis=1)

    rev = lambda width: pl.BlockSpec((tm, width), lambda i: (nt - 1 - i, 0))
    return _call(
        body, (dh, gp, gs, yp, ys, pm, w_group, scale, w_bp, w_ba, w_out), name="mix_bwd_out", grid=(nt,),
        in_specs=[rev(D), rev(D), rev(D), rev(D), rev(D), rev(PW), _ANY, _fixed((1, PW)), _ANY, _ANY, _ANY],
        out_specs=[rev(2 * D), rev(D), rev(D), rev(SBW), rev(PW), rev(PW), _fixed((1, PW))],
        out_shape=[_sds((S, 2 * D), BF16), _sds((S, D), BF16), _sds((S, D), BF16), _sds((S, SBW), BF16),
                   _sds((S, PW), BF16), _sds((S, PW), BF16), _sds((1, PW), F32)],
        scratch_shapes=[pltpu.VMEM((HALO, PW), F32)] + _vmem_like(w_group, w_bp, w_ba, w_out),
        compiler_params=_params(("arbitrary",), 48), exchange=exchange)


def _mix_bwd_in(dh, h, gain, pieces, w_in, exchange=None):
    tm = 512
    widths = [p.shape[1] for p in pieces]

    def body(dh_ref, h_ref, g_ref, *rest):
        piece_refs, (w_hbm, dx_ref, dg_ref, w_ref, dp_ref) = rest[:len(pieces)], rest[len(pieces):]
        @pl.when(pl.program_id(0) == 0)
        def _():
            dg_ref[...] = jnp.zeros_like(dg_ref)

        def compute(ready):
            at = 0
            for ref, width in zip(piece_refs, widths):
                dp_ref[:, at:at + width] = ref[...]
                at += width
            du = jnp.zeros((tm, D), F32)
            for j in range(NSH):
                ready(j)
                du = du + _nt(dp_ref[:, j * D:(j + 1) * D], w_ref[j])
            r, hr = _rms(h_ref[...])
            dx, dgain = _rms_bwd(du, hr, r, g_ref[...])
            dx_ref[...] = dh_ref[...] + dx
            dg_ref[...] += dgain

        _staged([[(w_hbm.at[j], w_ref.at[j])] for j in range(NSH)], compute)

    return _call(
        body, (dh, h, gain, *pieces, w_in), name="mix_bwd_in", grid=(S // tm,),
        in_specs=[_rows(tm, D), _rows(tm, D), _fixed((1, D))] + [_rows(tm, w) for w in widths] + [_ANY],
        out_specs=[_rows(tm, D), _fixed((1, D))],
        out_shape=[_sds((S, D), F32), _sds((1, D), F32)],
        scratch_shapes=_vmem_like(w_in) + [pltpu.VMEM((tm, 4 * D), BF16)],
        compiler_params=_params(("arbitrary",), 48), exchange=exchange)


def _wgrad_in(u, pieces):
    dxp, dq, dk, dv, dlg = pieces

    def body(u_ref, dxp_ref, dq_ref, dk_ref, dv_ref, dlg_ref, o_ref):
        j = pl.program_id(0)
        u = u_ref[...]

        def two(left_ref, right_ref):
            o_ref[:, :PW] = _tn(u, left_ref[...]).astype(BF16)
            o_ref[:, PW:] = _tn(u, right_ref[...]).astype(BF16)

        pl.when(j == 0)(lambda: two(dxp_ref, dq_ref))
        pl.when(j == 1)(lambda: two(dk_ref, dv_ref))

        @pl.when(j >= 2)
        def _():
            o_ref[...] = _tn(u, dlg_ref[...]).astype(BF16)

    whole = lambda width: pl.BlockSpec((S, width), lambda j: (0, 0))
    return _call(
        body, (u, dxp, dq, dk, dv, dlg), name="wgrad_in", grid=(NSH,),
        in_specs=[whole(D), whole(PW), whole(SBW), whole(SBW), whole(SBW),
                  pl.BlockSpec((S, D), lambda j: (0, jnp.maximum(j - 2, 0)))],
        out_specs=[pl.BlockSpec((None, D, D), lambda j: (j, 0, 0))], out_shape=[_sds((NSH, D, D), BF16)],
        compiler_params=_params(("arbitrary",), 56))[0]


def _wgrad(a, b, nblk, ti, name, out_dtype=BF16, exchange=None, after=()):
    ka, n = a.shape[1], b.shape[1]
    ns = n // nblk

    def body(a_ref, b_ref, o_ref):
        o_ref[...] = _tn(a_ref[...].astype(BF16), b_ref[...].astype(BF16)).astype(out_dtype)

    res = _call(
        body, (a, b), name=name, grid=(nblk, ka // ti),
        in_specs=[pl.BlockSpec((S, ti), lambda j, i: (0, i)), pl.BlockSpec((S, ns), lambda j, i: (0, j))],
        out_specs=[pl.BlockSpec((None, ti, ns), lambda j, i: (j, i, 0))],
        out_shape=[_sds((nblk, ka, ns), out_dtype)],
        compiler_params=_params(("arbitrary", "arbitrary"), 56), exchange=exchange, after=after)
    return res[0] if exchange is None else (res[0][0], res[1])


def _wgrad_ffn(n, dgu, a, dh, name):
    ti, n_gu = D // 2, 2 * NSH

    def body(n_ref, dgu_ref, a_ref, dh_ref, ggu_ref, gd_ref):
        s = pl.program_id(0)

        @pl.when(s < n_gu)
        def _():
            ggu_ref[...] = _tn(n_ref[...], dgu_ref[...]).astype(BF16)

        @pl.when(s >= n_gu)
        def _():
            gd_ref[...] = _tn(a_ref[...], dh_ref[...]).astype(BF16)

    rows = lambda s: jnp.where(s < n_gu, s % 2, 1)
    block = lambda s: jnp.minimum(s // 2, NSH - 1)
    down = lambda s: jnp.maximum(s - n_gu, 0)
    return _call(
        body, (n, dgu, a, dh), name=name, grid=(n_gu + 2,),
        in_specs=[pl.BlockSpec((S, ti), lambda s: (0, rows(s))), pl.BlockSpec((S, FFS), lambda s: (0, block(s))),
                  pl.BlockSpec((S, FFS), lambda s: (0, down(s))), pl.BlockSpec((S, D), lambda s: (0, 0))],
        out_specs=[pl.BlockSpec((None, ti, FFS), lambda s: (block(s), rows(s), 0)),
                   pl.BlockSpec((FFS, D), lambda s: (down(s), 0))],
        out_shape=[_sds((NSH, D, FFS), BF16), _sds((DFF, D), BF16)],
        compiler_params=_params(("arbitrary",), 56))


def _wgrad_branches(p, dyp, o_sb, dys, pm, dyg, mm, dh):
    cols = D // NSH

    def body(p_ref, dyp_ref, o_ref, dys_ref, pm_ref, dyg_ref, mm_ref, dh_ref, gbp_ref, gba_ref, gg_ref, go_ref):
        gbp_ref[...] = _tn(p_ref[...], dyp_ref[...]).astype(BF16)
        gba_ref[...] = _tn(o_ref[...], dys_ref[...]).astype(BF16)
        gg_ref[...] = _tn(pm_ref[...], dyg_ref[...])
        go_ref[...] = _tn(mm_ref[...], dh_ref[...].astype(BF16)).astype(BF16)

    whole = lambda width: pl.BlockSpec((S, width), lambda j: (0, 0))
    col = lambda width: pl.BlockSpec((S, width), lambda j: (0, j))
    return _call(
        body, (p, dyp, o_sb, dys, pm, dyg, mm, dh), name="wgrad_branches", grid=(NSH,),
        in_specs=[whole(PW), col(cols), whole(SBW), col(cols), col(PG), col(PG), whole(D), col(cols)],
        out_specs=[pl.BlockSpec((None, PW, cols), lambda j: (j, 0, 0)),
                   pl.BlockSpec((None, SBW, cols), lambda j: (j, 0, 0)),
                   pl.BlockSpec((None, PG, PG), lambda j: (j, 0, 0)),
                   pl.BlockSpec((D, cols), lambda j: (0, j))],
        out_shape=[_sds((NSH, PW, cols), BF16), _sds((NSH, SBW, cols), BF16), _sds((NSH, PG, PG), F32),
                   _sds((D, D), BF16)],
        compiler_params=_params(("arbitrary",), 40))


def _place():
    x, y, c = lax.axis_index("x"), lax.axis_index("y"), lax.axis_index("c")
    chips = [(1 - x, y), (x, 1 - y), (1 - x, 1 - y)]
    return x, y, c, chips


def _remote(src, dst, ssem, rsem, dev):
    return pltpu.make_async_remote_copy(src_ref=src, dst_ref=dst, send_sem=ssem, recv_sem=rsem,
                                        device_id=dev, device_id_type=MESH)


def _cast_into_block(ws, me_idx, name):
    steps = 4
    shapes = [(w.shape[0] // steps, w.shape[1]) for w in ws]

    def body(me_ref, *refs):
        for w_ref, o_ref in zip(refs[:len(ws)], refs[len(ws):]):
            o_ref[...] = w_ref[...].astype(BF16)

    return pl.pallas_call(
        body, name=name, out_shape=[_sds((NSH,) + w.shape, BF16) for w in ws],
        grid_spec=pltpu.PrefetchScalarGridSpec(
            num_scalar_prefetch=1, grid=(steps,),
            in_specs=[pl.BlockSpec((r, c), lambda s, me: (s, 0)) for r, c in shapes],
            out_specs=[pl.BlockSpec((None, r, c), lambda s, me: (me[0], s, 0)) for r, c in shapes]),
        compiler_params=_params(("arbitrary",), 32),
    )(me_idx, *ws)


def _ex_gather(bufs):
    n = len(bufs)
    per = 8

    def plan(outs, ssem, rsem, w):
        x, y, c, _ = _place()
        sib, nbr_x, nbr_y = (x, y, 1 - c), (1 - x, y, c), (x, 1 - y, c)
        half = outs[w].shape[1] // 2
        quarter = half // 2
        sem = lambda k: (ssem.at[per * w + k], rsem.at[per * w + k])
        rows = lambda blk, start, size: outs[w].at[blk, pl.ds(start, size)]
        mine = rows(2 * x + y, c * half, half)
        from_x = rows(2 * (1 - x) + y, c * half, half)
        from_y = rows(2 * x + (1 - y), c * half, half)
        diag = 2 * (1 - x) + (1 - y)
        pass_y = rows(2 * (1 - x) + y, c * half, quarter)
        pass_x = rows(2 * x + (1 - y), c * half + quarter, quarter)
        diag_0, diag_1 = rows(diag, c * half, quarter), rows(diag, c * half + quarter, quarter)
        first = [_remote(mine, mine, *sem(0), nbr_x), _remote(mine, mine, *sem(1), nbr_y)]
        arrivals = [
            (_remote(from_x, from_x, *sem(0), nbr_x),
             [_remote(pass_y, pass_y, *sem(2), nbr_y), _remote(from_x, from_x, *sem(4), sib)]),
            (_remote(from_y, from_y, *sem(1), nbr_y),
             [_remote(pass_x, pass_x, *sem(3), nbr_x), _remote(from_y, from_y, *sem(5), sib)]),
            (_remote(diag_0, diag_0, *sem(2), nbr_y), [_remote(diag_0, diag_0, *sem(6), sib)]),
            (_remote(diag_1, diag_1, *sem(3), nbr_x), [_remote(diag_1, diag_1, *sem(7), sib)]),
        ]
        other = (1 - c) * half
        from_sibling = [
            _remote(rows(2 * (1 - x) + y, other, half), rows(2 * (1 - x) + y, other, half), *sem(4), sib),
            _remote(rows(2 * x + (1 - y), other, half), rows(2 * x + (1 - y), other, half), *sem(5), sib),
            _remote(rows(diag, other, quarter), rows(diag, other, quarter), *sem(6), sib),
            _remote(rows(diag, other + quarter, quarter), rows(diag, other + quarter, quarter), *sem(7), sib),
        ]
        return first, arrivals, from_sibling

    def start(ins, outs, ssem, rsem):
        x, y, c, _ = _place()
        for w in range(n):
            half = outs[w].shape[1] // 2
            mine = outs[w].at[2 * x + y, pl.ds(c * half, half)]
            _remote(mine, mine, ssem.at[per * w], rsem.at[per * w], (1 - x, y, c)).start()
            _remote(mine, mine, ssem.at[per * w + 1], rsem.at[per * w + 1], (x, 1 - y, c)).start()

    def finish(ins, outs, ssem, rsem):
        plans = [plan(outs, ssem, rsem, w) for w in range(n)]
        started = []
        for direct in (True, False):
            for first, arrivals, _ in plans:
                for arrived, onward in (arrivals[:2] if direct else arrivals[2:]):
                    arrived.wait_recv()
                    for cp in onward:
                        cp.start()
                    started += onward
        for first, _, from_sibling in plans:
            for cp in from_sibling:
                cp.wait_recv()
            started += first
        for cp in started:
            cp.wait_send()

    return Exchange(bufs, [_sds(b.shape, b.dtype) for b in bufs], {w: w for w in range(n)}, per * n, start, finish)


def _ex_gather_direct(bufs):
    n = len(bufs)

    def copies(outs, ssem, rsem, only_first=False):
        x, y, c, chips = _place()
        me, sib = 2 * x + y, (x, y, 1 - c)
        first, relay, last = [], [], []
        for w in range(n):
            half = outs[w].shape[1] // 2
            mine = outs[w].at[me, pl.ds(c * half, half)]
            for k, (px, py) in enumerate(chips):
                sems = (ssem.at[6 * w + k], rsem.at[6 * w + k])
                sib_sems = (ssem.at[6 * w + 3 + k], rsem.at[6 * w + 3 + k])
                first.append(_remote(mine, mine, *sems, (px, py, c)))
                if only_first:
                    continue
                got = outs[w].at[2 * px + py, pl.ds(c * half, half)]
                relay.append((_remote(got, got, *sems, (px, py, c)), _remote(got, got, *sib_sems, sib)))
                theirs = outs[w].at[2 * px + py, pl.ds((1 - c) * half, half)]
                last.append(_remote(theirs, theirs, *sib_sems, sib))
        return first, relay, last

    def start(ins, outs, ssem, rsem):
        for cp in copies(outs, ssem, rsem, only_first=True)[0]:
            cp.start()

    def finish(ins, outs, ssem, rsem):
        first, relay, last = copies(outs, ssem, rsem)
        for arrived, onward in relay:
            arrived.wait_recv()
            onward.start()
        for cp in last:
            cp.wait_recv()
        for cp in first:
            cp.wait_send()
        for _, onward in relay:
            onward.wait_send()

    return Exchange(bufs, [_sds(b.shape, b.dtype) for b in bufs], {w: w for w in range(n)}, 6 * n, start, finish)


def _simple_exchange(arrays, landing, aliases, make_copies, sibling_only=False):
    def start(ins, outs, ssem, rsem):
        for cp, _ in make_copies(ins, outs, ssem, rsem, False):
            cp.start()

    def finish(ins, outs, ssem, rsem):
        cps = make_copies(ins, outs, ssem, rsem, True)
        for _, landed in cps:
            landed.wait_recv()
        for cp, _ in cps:
            cp.wait_send()

    return Exchange(arrays, landing, aliases, len(arrays) * 3, start, finish, sibling_only)


def _ex_pair_swap(grads):
    def make(ins, outs, ssem, rsem, landing):
        x, y, c, _ = _place()
        cps = [_remote(ins[w].at[:, 1 - c], outs[w], ssem.at[w], rsem.at[w], (x, y, 1 - c))
               for w in range(len(grads))]
        return [(cp, cp) for cp in cps]

    return _simple_exchange(grads, [_sds((NSH,) + g.shape[2:], g.dtype) for g in grads], {}, make, True)


def _ex_relay(bufs):
    def make(ins, outs, ssem, rsem, landing):
        x, y, c, chips = _place()
        sib = (x, y, 1 - c)
        out = []
        for w in range(len(bufs)):
            half = outs[w].shape[1] // 2
            for k, (px, py) in enumerate(chips):
                sems = (ssem.at[3 * w + k], rsem.at[3 * w + k])
                have = outs[w].at[2 * px + py, pl.ds(c * half, half)]
                miss = outs[w].at[2 * px + py, pl.ds((1 - c) * half, half)]
                out.append((_remote(have, have, *sems, sib), _remote(miss, miss, *sems, sib) if landing else None))
        return out

    return _simple_exchange(bufs, [_sds(b.shape, b.dtype) for b in bufs], {w: w for w in range(len(bufs))}, make, True)


def _ex_share(bufs):
    def make(ins, outs, ssem, rsem, landing):
        x, y, c, _ = _place()
        sib = (x, y, 1 - c)
        return [(_remote(outs[w].at[c], outs[w].at[c], ssem.at[w], rsem.at[w], sib),
                 _remote(outs[w].at[1 - c], outs[w].at[1 - c], ssem.at[w], rsem.at[w], sib) if landing else None)
                for w in range(len(bufs))]

    return _simple_exchange(bufs, [_sds(b.shape, b.dtype) for b in bufs], {w: w for w in range(len(bufs))}, make, True)


def _small_copies(slots, ssems, rsems, sending):
    x, y, c, _ = _place()
    out = []
    for m in range(1, 8):
        px, py, pc = x ^ (m >> 2), y ^ ((m >> 1) & 1), c ^ (m & 1)
        slot = slots.at[4 * x + 2 * y + c if sending else 4 * px + 2 * py + pc]
        out.append(_remote(slot, slot, ssems[m - 1], rsems[m - 1], (px, py, pc)))
    return out


def _small_gather_start(slots, name, after=()):
    at = 1 + len(after)

    def body(*refs):
        for cp in _small_copies(refs[0], refs[at:at + 7], refs[at + 7:at + 14], True):
            cp.start()
        refs[-1][...] = jnp.zeros_like(refs[-1])

    outs = pl.pallas_call(
        body, name=name,
        out_shape=([pltpu.SemaphoreType.DMA(())] * 14 + [pltpu.HBM(slots.shape, slots.dtype)]
                   + [jax.ShapeDtypeStruct((8, 128), F32)]),
        in_specs=[_HBM] + [_ANY] * len(after), out_specs=[_SEM] * 14 + [_HBM, _VM], input_output_aliases={0: 14},
        compiler_params=pltpu.CompilerParams(has_side_effects=_EFFECT),
    )(*_in_hbm([slots]), *after)
    return outs[:14], outs[14], outs[15]


def _small_gather_wait(sems, slots, after, name):
    def body(*refs):
        for cp in _small_copies(refs[0], refs[1:8], refs[8:15], True):
            cp.wait_send()
        for cp in _small_copies(refs[0], refs[1:8], refs[8:15], False):
            cp.wait_recv()

    return pl.pallas_call(
        body, name=name, out_shape=pltpu.HBM(slots.shape, slots.dtype),
        in_specs=[_HBM] + [_SEM] * 14 + [_ANY] * len(after), out_specs=_HBM, input_output_aliases={0: 0},
        compiler_params=pltpu.CompilerParams(has_side_effects=_EFFECT),
    )(slots, *sems, *after)


def _pair_sum(grads, gots, c_idx, name):
    n = len(grads)

    def body(c_ref, *refs):
        for a_ref, b_ref, o_ref in zip(refs[:n], refs[n:2 * n], refs[2 * n:]):
            o_ref[...] = (a_ref[...].astype(F32) + b_ref[...].astype(F32)).astype(BF16)

    halves = [g.shape[2:] for g in grads]
    return list(pl.pallas_call(
        body, name=name, out_shape=[_sds((NSH,) + h, BF16) for h in halves],
        grid_spec=pltpu.PrefetchScalarGridSpec(
            num_scalar_prefetch=1, grid=(NSH,),
            in_specs=[pl.BlockSpec((None, None) + h, lambda j, c: (j, c[0], 0, 0)) for h in halves]
            + [pl.BlockSpec((None,) + h, lambda j, c: (j, 0, 0)) for h in halves],
            out_specs=[pl.BlockSpec((None,) + h, lambda j, c: (j, 0, 0)) for h in halves]),
        compiler_params=_params(("arbitrary",), 40),
    )(c_idx, *_in_hbm(list(grads) + list(gots))))


def _chip_sum(owns, gots, place, name):
    n = len(owns)

    def body(place_ref, *refs):
        for own_ref, got_ref, o_ref in zip(refs[:n], refs[n:2 * n], refs[2 * n:]):
            acc = own_ref[...].astype(F32)
            for k in range(3):
                acc = acc + got_ref[k].astype(F32)
            o_ref[...] = acc

    shapes = [(o.shape[1] // 2, o.shape[2]) for o in owns]
    return list(pl.pallas_call(
        body, name=name, out_shape=[_sds((2, 2 * r, c), F32) for r, c in shapes],
        grid_spec=pltpu.PrefetchScalarGridSpec(
            num_scalar_prefetch=1, grid=(2,),
            in_specs=[pl.BlockSpec((None, r, c), lambda s, p: (p[0], s, 0)) for r, c in shapes]
            + [pl.BlockSpec((3, r, c), lambda s, p: (0, s, 0)) for r, c in shapes],
            out_specs=[pl.BlockSpec((None, r, c), lambda s, p: (p[1], s, 0)) for r, c in shapes]),
        compiler_params=_params(("arbitrary",), 40),
    )(place, *_in_hbm(list(owns) + list(gots))))


def _adamw_math(w, g, m, v):
    m = B1 * m + (1.0 - B1) * g
    v = B2 * v + (1.0 - B2) * (g * g)
    m_hat = m / (1.0 - B1 ** STEP)
    v_hat = v / (1.0 - B2 ** STEP)
    return -LR * (m_hat / (jnp.sqrt(v_hat) + AEPS) + WD * w), m, v


def _adamw(ws, gs, ms, vs, name, after=()):
    n, steps = len(ws), 4

    def body(*refs):
        ins, outs = refs[:4 * n], refs[4 * n:]
        for i in range(n):
            w_ref, g_ref, m_ref, v_ref = ins[4 * i:4 * i + 4]
            go_ref, d_ref, nm_ref, nv_ref = outs[4 * i:4 * i + 4]
            g = g_ref[...]
            go_ref[...] = g
            d_ref[...], nm_ref[...], nv_ref[...] = _adamw_math(w_ref[...], g, m_ref[...], v_ref[...])

    args, specs, shapes, free = [], [], [], []
    for i, (w, g, m, v) in enumerate(zip(ws, gs, ms, vs)):
        args += [w, g, m, v]
        specs += [pl.BlockSpec((w.shape[0] // steps, w.shape[1]), lambda r: (r, 0))] * 4
        shapes += [_sds(w.shape, F32)] * 4
        free += [4 * i, 4 * i + 2, 4 * i + 3]
    outs = _call(body, args, name=name, grid=(steps,), out_shape=shapes, in_specs=specs, out_specs=specs,
                 compiler_params=_params(("arbitrary",), 48), free=tuple(free), after=after)
    return [outs[4 * i:4 * i + 4] for i in range(n)]


def _small_update(gathered, w, m, v, entries):
    rows = w.shape[0]

    def body(ga_ref, w_ref, m_ref, v_ref, *out_refs):
        for j, (first, n) in enumerate(entries):
            mine = slice(first, first + n)
            g = ga_ref[mine, :]
            for dev in range(1, 8):
                g = g + ga_ref[dev * rows + first:dev * rows + first + n, :]
            results = (g,) + _adamw_math(w_ref[mine, :], g, m_ref[mine, :], v_ref[mine, :])
            for i, res in enumerate(results):
                out_refs[i * len(entries) + j][...] = res

    outs = pl.pallas_call(
        body, name="small_update",
        out_shape=[jax.ShapeDtypeStruct((n, 128), F32) for _ in range(4) for _, n in entries],
        in_specs=[_VM] * 4, out_specs=[_VM] * (4 * len(entries)),
    )(gathered, w, m, v)
    return [outs[i * len(entries):(i + 1) * len(entries)] for i in range(4)]


SMALL = ("ffn1_norm", "mix_norm", "ffn2_norm", "final_norm", "pool_scale", "loss", "pool_w_group")
BIG = ("ffn1_w_gate_up", "ffn1_w_down", "w_in", "w_branch_pool", "w_branch_attn", "w_out",
       "ffn2_w_gate_up", "ffn2_w_down")
ORDER = ("ffn1_norm", "ffn1_w_gate_up", "ffn1_w_down", "mix_norm", "w_in", "pool_w_group", "pool_scale",
         "w_branch_pool", "w_branch_attn", "w_out", "ffn2_norm", "ffn2_w_gate_up", "ffn2_w_down", "final_norm")
SMALL_ROWS = 560


def _pack_small(t):
    parts = []
    for k in SMALL:
        rows = t[k].reshape(-1, 128) if k in t else jnp.zeros((1, 128), F32)
        parts.append(jnp.pad(rows, ((0, -rows.shape[0] % 8), (0, 0))))
    packed = jnp.concatenate(parts, axis=0)
    assert packed.shape == (SMALL_ROWS, 128), packed.shape
    return packed


def _small_entries(like):
    out, at = [], 0
    for k in SMALL:
        n = like[k].size // 128 if k in like else 1
        out.append((at, n))
        at += n + (-n % 8)
    return out


def _halves(g):
    return g.reshape(NSH, 2, g.shape[1] // 2, g.shape[2])


def kernel(x, ffn1_norm, ffn1_w_gate_up, ffn1_w_down, mix_norm, w_in, pool_w_group, pool_scale, w_branch_pool, w_branch_attn, w_out, ffn2_norm, ffn2_w_gate_up, ffn2_w_down, final_norm, loss_target, m_ffn1_norm, m_ffn1_w_gate_up, m_ffn1_w_down, m_mix_norm, m_w_in, m_pool_w_group, m_pool_scale, m_w_branch_pool, m_w_branch_attn, m_w_out, m_ffn2_norm, m_ffn2_w_gate_up, m_ffn2_w_down, m_final_norm, v_ffn1_norm, v_ffn1_w_gate_up, v_ffn1_w_down, v_mix_norm, v_w_in, v_pool_w_group, v_pool_scale, v_w_branch_pool, v_w_branch_attn, v_w_out, v_ffn2_norm, v_ffn2_w_gate_up, v_ffn2_w_down, v_final_norm):
    wts = dict(ffn1_norm=ffn1_norm, ffn1_w_gate_up=ffn1_w_gate_up, ffn1_w_down=ffn1_w_down, mix_norm=mix_norm,
               w_in=w_in, pool_w_group=pool_w_group, pool_scale=pool_scale, w_branch_pool=w_branch_pool,
               w_branch_attn=w_branch_attn, w_out=w_out, ffn2_norm=ffn2_norm, ffn2_w_gate_up=ffn2_w_gate_up,
               ffn2_w_down=ffn2_w_down, final_norm=final_norm)
    mom = dict(ffn1_norm=m_ffn1_norm, ffn1_w_gate_up=m_ffn1_w_gate_up, ffn1_w_down=m_ffn1_w_down,
               mix_norm=m_mix_norm, w_in=m_w_in, pool_w_group=m_pool_w_group, pool_scale=m_pool_scale,
               w_branch_pool=m_w_branch_pool, w_branch_attn=m_w_branch_attn, w_out=m_w_out,
               ffn2_norm=m_ffn2_norm, ffn2_w_gate_up=m_ffn2_w_gate_up, ffn2_w_down=m_ffn2_w_down,
               final_norm=m_final_norm)
    var = dict(ffn1_norm=v_ffn1_norm, ffn1_w_gate_up=v_ffn1_w_gate_up, ffn1_w_down=v_ffn1_w_down,
               mix_norm=v_mix_norm, w_in=v_w_in, pool_w_group=v_pool_w_group, pool_scale=v_pool_scale,
               w_branch_pool=v_w_branch_pool, w_branch_attn=v_w_branch_attn, w_out=v_w_out,
               ffn2_norm=v_ffn2_norm, ffn2_w_gate_up=v_ffn2_w_gate_up, ffn2_w_down=v_ffn2_w_down,
               final_norm=v_final_norm)

    c_idx = lax.axis_index("c").astype(jnp.int32).reshape(1)
    me_idx = (2 * lax.axis_index("x") + lax.axis_index("y")).astype(jnp.int32).reshape(1)
    place = jnp.concatenate([me_idx, c_idx])
    x0, tgt = x[0], loss_target[0]
    wgrp = pool_w_group[0].astype(BF16)
    g1, gm, g2, gf = ffn1_norm, mix_norm, ffn2_norm, final_norm.reshape(1, D)
    grad, delta, new_m, new_v = {}, {}, {}, {}

    def pair_sums(keys, parts, got):
        return _pair_sum(parts, got, c_idx, "pair_sum_" + keys[0])

    def chip_sums(keys, chip_parts, owned):
        return _chip_sum(chip_parts, owned, place, "chip_sum_" + keys[0])

    def adamw(keys, after=()):
        outs = _adamw([wts[k][0] for k in keys], [grad[k][0] for k in keys], [mom[k][0] for k in keys],
                      [var[k][0] for k in keys], "adamw_" + keys[0], after=after)
        for k, res in zip(keys, outs):
            grad[k], delta[k], new_m[k], new_v[k] = (o.reshape(wts[k].shape) for o in res)

    first, late = ("ffn1_w_gate_up", "ffn1_w_down"), ("w_branch_pool", "w_branch_attn", "w_out",
                                                       "ffn2_w_gate_up", "ffn2_w_down")
    own = {}
    for group in (first, ("w_in",), late):
        own.update(zip(group, _cast_into_block([wts[k][0] for k in group], me_idx, "cast_" + group[0])))
    full = dict(zip(first, _exchange_alone(_ex_gather([own[k] for k in first]), "gather_ffn1")))
    wgu1, wd1 = full["ffn1_w_gate_up"], full["ffn1_w_down"].reshape(DFF, D)
    (h1, n1, gu1, a1), (win,) = _ffn_fwd(x0, g1, wgu1, wd1, "ffn1_fwd", exchange=_ex_gather_direct([own["w_in"]]))
    sems_l, thru_l, token_l = _gather_start([own[k_] for k_ in late], [h1], "gather_late_start")
    u, xp, q, k, v, gp, gs = _mix_in(h1, gm, win, after=(token_l,))
    o_sb, ctot = _attn_fwd(q, k, v)
    arrived = _gather_wait(sems_l, thru_l, [o_sb], "gather_late_wait")
    wbp, wba, wout = _exchange_alone(_ex_relay(arrived[:3]), "relay_mix")
    wout = wout.reshape(D, D)
    (h2, pm, p, yp, ys, mm), (wgu2, wd2) = _mix_out(h1, xp, o_sb, gp, gs, wgrp, pool_scale, wbp, wba, wout,
                                                    exchange=_ex_relay(arrived[3:]))
    wd2 = wd2.reshape(DFF, D)
    dh2, dgu3, d_g2, loss_row, d_gf, n3, a3, dh3 = _ffn_last(h2, g2, wgu2, wd2, tgt, gf, "ffn2")

    def grad_down(a, dh, name, exchange=None):
        res = _wgrad(a, dh, 1, FFS, name, exchange=exchange)
        halves = lambda g: [_halves(g.reshape(NSH, DFF // NSH, D))]
        return halves(res) if exchange is None else (halves(res[0]), res[1])

    k_gu2, k_d2, k_gu1, k_d1, k_in = (("ffn2_w_gate_up",), ("ffn2_w_down",), ("ffn1_w_gate_up",),
                                      ("ffn1_w_down",), ("w_in",))
    g_gu2, g_d2 = _wgrad_ffn(n3, dgu3, a3, dh3, "wgrad_ffn2")
    pa = [_halves(g_gu2), _halves(g_d2.reshape(NSH, DFF // NSH, D))]
    (dlg, dyp, dys, do_sb, dyg, dxp, d_scale), got_a = _mix_bwd_out(
        dh2, gp, gs, yp, ys, pm, wgrp, pool_scale, wbp, wba, wout, exchange=_ex_pair_swap(pa))
    chip_a = pair_sums(k_gu2 + k_d2, pa, got_a)
    kb = ("w_out", "w_branch_pool", "w_branch_attn")
    g_bp, g_ba, d_group, g_out = _wgrad_branches(p, dyp, o_sb, dys, pm, dyg, mm, dh2)
    pb = [_halves(g_out.reshape(NSH, D // NSH, D)), _halves(g_bp), _halves(g_ba)]
    k_a, k_in = k_gu2 + k_d2, k_in + kb
    sems_a, thru_a, token_a = _scatter_start(chip_a, "scatter_a_start")
    dq, dk, dv = _attn_bwd(q, k, v, do_sb, ctot, after=(token_a,))
    chip_a, owned_a = _scatter_wait(sems_a, thru_a, [dq], "scatter_a_wait")
    halves_a = chip_sums(k_a, chip_a, owned_a)
    dproj = (dxp, dq, dk, dv, dlg)
    (dh1, d_gm), both_a = _mix_bwd_in(dh2, h1, gm, dproj, win, exchange=_ex_share(halves_a))
    for i, k_ in enumerate(k_a):
        grad[k_] = both_a[i].reshape(wts[k_].shape)

    p_in = [_halves(_wgrad_in(u, dproj))] + pb
    p_d1, got_in = grad_down(a1, dh1, "wgrad_d1", exchange=_ex_pair_swap(p_in))
    sems_in, thru_in, token_in = _scatter_start(pair_sums(k_in, p_in, got_in), "scatter_in_start")
    dgu1, got_d1 = _ffn_bwd_act(dh1, gu1, wd1, "ffn1_bwd_act", exchange=_ex_pair_swap(p_d1), after=(token_in,))
    sems_d1, thru_d1, token_d1 = _scatter_start(pair_sums(k_d1, p_d1, got_d1), "scatter_d1_start")
    p_gu1 = [_halves(_wgrad(n1, dgu1, NSH, D, "wgrad_gu1", after=(token_in, token_d1)))]
    sems_w, thru_w, token_w = _swap_start(p_gu1, "swap_gu1_start")
    chip_in, owned_in = _scatter_wait(sems_in, thru_in, [token_w], "scatter_in_wait")
    chip_d1, owned_d1 = _scatter_wait(sems_d1, thru_d1, [token_w], "scatter_d1_wait")
    halves_in = chip_sums(k_in, chip_in, owned_in)
    p_gu1, got_gu1 = _swap_wait(sems_w, thru_w, halves_in, "swap_gu1_wait")
    sems, thru, token = _scatter_start(pair_sums(k_gu1, p_gu1, got_gu1), "scatter_gu1_start")
    sems_h, thru_h, token_h = _share_start(halves_in, [token], "share_in_start")
    adamw(k_a, after=(token_h,))
    landed = _share_wait(sems_h, thru_h, [delta[k_a[0]]], "share_in_wait")
    for i, k_ in enumerate(k_in):
        grad[k_] = landed[i].reshape(wts[k_].shape)
    adamw(k_in)
    dx, d_g1 = _ffn_bwd_in(dh1, x0, g1, dgu1, wgu1, "ffn1_bwd_in", after=(token,))
    small_g = dict(ffn1_norm=d_g1, mix_norm=d_gm, ffn2_norm=d_g2, final_norm=d_gf, pool_scale=d_scale,
                   pool_w_group=d_group, loss=loss_row)
    dev = 4 * lax.axis_index("x") + 2 * lax.axis_index("y") + lax.axis_index("c")
    slots = lax.dynamic_update_slice(jnp.zeros((8, SMALL_ROWS, 128), F32), _pack_small(small_g)[None], (dev, 0, 0))
    chip_gu1, owned_gu1 = _scatter_wait(sems, thru, [dx] + [delta[k_] for k_ in k_a + k_in], "scatter_gu1_wait")
    halves_last = chip_sums(k_d1 + k_gu1, chip_d1 + chip_gu1, owned_d1 + owned_gu1)
    sems_l, thru_l, token_l = _share_start(halves_last, [], "share_last_start")
    sems_s, slots, token_s = _small_gather_start(slots, "small_gather_start", after=(token_l,))
    both = _share_wait(sems_l, thru_l, [token_s], "share_last_wait")
    grad["ffn1_w_down"] = both[0].reshape(ffn1_w_down.shape)
    grad["ffn1_w_gate_up"] = both[1].reshape(ffn1_w_gate_up.shape)
    adamw(k_d1 + k_gu1, after=(token_s,))
    gathered = _small_gather_wait(sems_s, slots, [delta[k_] for k_ in k_d1 + k_gu1], "small_gather_wait")
    gathered = gathered.reshape(8 * SMALL_ROWS, 128)
    results = _small_update(gathered, _pack_small(wts), _pack_small(mom), _pack_small(var), _small_entries(wts))
    for dst, entries in zip((grad, delta, new_m, new_v), results):
        for k_, rows in zip(SMALL, entries):
            if k_ in wts:
                dst[k_] = rows.reshape(wts[k_].shape)
            elif dst is grad:
                loss = rows[0, 0]
    return (loss, dx[None], *[grad[k_] for k_ in ORDER], *[delta[k_] for k_ in ORDER],
            *[new_m[k_] for k_ in ORDER], *[new_v[k_] for k_ in ORDER])
```

```python
import dataclasses
import functools

import jax
import jax.numpy as jnp
from jax import lax
from jax.experimental import pallas as pl
from jax.experimental.pallas import tpu as pltpu

F32 = jnp.float32
BF16 = jnp.bfloat16

S = 2048
D = 1024
DFF = 2816
FFS = 2 * DFF // 4
NSH = 4
PW = 512
PG = 128
POOL_WINDOWS = (2, 4, 8, 16)
HALO = 16
SBW = 512
DH = 64
EPS = 1e-6
SCALE = 0.125
LOG2E = 1.4426950408889634
TA = 256
QB = 2
MIB = 1024 * 1024

LR, B1, B2, AEPS, WD, STEP = 0.001, 0.9, 0.999, 1e-08, 0.01, 10

_VM = pl.BlockSpec(memory_space=pltpu.VMEM)
_ANY = pl.BlockSpec(memory_space=pl.ANY)
MESH = pl.DeviceIdType.MESH
SIBLING_PAIR_ID = 1


def _nn(a, b):
    return jnp.dot(a, b, preferred_element_type=F32)


def _nt(a, b):
    return lax.dot_general(a, b, (((1,), (1,)), ((), ())), preferred_element_type=F32)


def _tn(a, b):
    return lax.dot_general(a, b, (((0,), (0,)), ((), ())), preferred_element_type=F32)


def _params(sem, vmem_mib):
    return pltpu.CompilerParams(dimension_semantics=sem, vmem_limit_bytes=vmem_mib * MIB)


def _rows(tm, width):
    return pl.BlockSpec((tm, width), lambda i: (i, 0))


def _fixed(shape):
    return pl.BlockSpec(shape, lambda *_: (0,) * len(shape))


def _sds(shape, dtype):
    return pltpu.HBM(shape, dtype)


def _in_hbm(args):
    return [pltpu.with_memory_space_constraint(a, pltpu.HBM) for a in args]


def _stage(pairs):
    pieces = 4

    def copy_all(sems):
        copies = []
        for src, dst in pairs:
            step = src.shape[0] // pieces
            for p in range(pieces):
                part = pl.ds(p * step, step)
                if len(dst.shape) == len(src.shape):
                    piece = (src.at[part], dst.at[part])
                else:
                    piece = (src.at[p], dst.at[:, pl.ds(p * src.shape[2], src.shape[2])])
                copies.append(pltpu.make_async_copy(*piece, sems.at[len(copies)]))
        for c in copies:
            c.start()
        for c in copies:
            c.wait()

    @pl.when(pl.program_id(0) == 0)
    def _():
        pl.run_scoped(copy_all, pltpu.SemaphoreType.DMA((pieces * len(pairs),)))


def _vmem_like(*arrays):
    return [pltpu.VMEM(a.shape, a.dtype) for a in arrays]


def _vmem_wide(w):
    return pltpu.VMEM((w.shape[1], w.shape[0] * w.shape[2]), w.dtype)


FF_CHUNKS = ((0, 1024), (1024, 1024), (2048, DFF - 2048))


def _wide_columns(src, dst, c0, cn):
    width, out = src.shape[2], []
    for p in range(src.shape[0]):
        lo, hi = max(c0, p * width), min(c0 + cn, (p + 1) * width)
        if lo < hi:
            out.append((src.at[p, :, pl.ds(lo - p * width, hi - lo)], dst.at[:, pl.ds(lo, hi - lo)]))
    return out


def _staged(groups, compute):
    first = pl.program_id(0) == 0

    def with_copies(sems):
        copies = []
        for group in groups:
            base = sum(len(g) for g in copies)
            copies.append([pltpu.make_async_copy(s, d, sems.at[base + i]) for i, (s, d) in enumerate(group)])
        for group in copies:
            for c in group:
                c.start()

        def ready(k):
            for c in copies[k]:
                c.wait()

        compute(ready)

    @pl.when(first)
    def _():
        pl.run_scoped(with_copies, pltpu.SemaphoreType.DMA((sum(len(g) for g in groups),)))

    @pl.when(jnp.logical_not(first))
    def _():
        compute(lambda k: None)


class Exchange:
    def __init__(self, arrays, landing, aliases, n_sems, start, finish, sibling_only=False):
        self.arrays, self.landing, self.aliases, self.n_sems = list(arrays), list(landing), dict(aliases), n_sems
        self.start, self.finish = start, finish
        self.sibling_only = sibling_only

    def enter(self):
        if self.sibling_only:
            barrier = pltpu.get_barrier_semaphore()
            sibling = (lax.axis_index("x"), lax.axis_index("y"), 1 - lax.axis_index("c"))
            pl.semaphore_signal(barrier, inc=1, device_id=sibling, device_id_type=MESH)
            pl.semaphore_wait(barrier, 1)

    def params(self, compiler_params=None):
        kw = dict(collective_id=SIBLING_PAIR_ID) if self.sibling_only else {}
        if compiler_params is None:
            return pltpu.CompilerParams(**kw)
        return dataclasses.replace(compiler_params, **kw)


def _call(body, args, *, name, grid, in_specs, out_specs, out_shape, scratch_shapes=(), compiler_params=None,
          exchange=None, free=(), after=()):
    args = [a if i in free else pltpu.with_memory_space_constraint(a, pltpu.HBM) for i, a in enumerate(args)]
    if exchange is None:
        n_in = len(in_specs)

        def plain(*refs):
            body(*refs[:n_in], *refs[n_in + len(after):])

        return pl.pallas_call(plain, name=name, grid=grid, in_specs=list(in_specs) + [_ANY] * len(after),
                              out_specs=out_specs, out_shape=out_shape, scratch_shapes=list(scratch_shapes),
                              compiler_params=compiler_params)(*args, *after)
    ex = exchange
    n_in, n_out, n_scr = len(in_specs), len(out_specs), len(scratch_shapes)
    na, nl = len(ex.arrays), len(ex.landing)

    def hosted(*refs):
        at = [0]

        def take(n):
            at[0] += n
            return refs[at[0] - n:at[0]]

        k_in, _, e_in, k_out, e_out, k_scr = take(n_in), take(len(after)), take(na), take(n_out), take(nl), take(n_scr)
        ssem, rsem = take(2)
        ids = [pl.program_id(a) for a in range(len(grid))]
        first = functools.reduce(jnp.logical_and, [i == 0 for i in ids])
        last = functools.reduce(jnp.logical_and, [i == g - 1 for i, g in zip(ids, grid)])

        @pl.when(first)
        def _():
            ex.enter()
            ex.start(e_in, e_out, ssem, rsem)

        body(*k_in, *k_out, *k_scr)

        @pl.when(last)
        def _():
            ex.finish(e_in, e_out, ssem, rsem)

    outs = pl.pallas_call(
        hosted, name=name, grid=grid,
        in_specs=list(in_specs) + [_ANY] * (len(after) + na), out_specs=list(out_specs) + [_ANY] * nl,
        out_shape=list(out_shape) + ex.landing,
        scratch_shapes=list(scratch_shapes) + [pltpu.SemaphoreType.DMA((ex.n_sems,))] * 2,
        input_output_aliases={n_in + len(after) + i: n_out + j for i, j in ex.aliases.items()},
        compiler_params=ex.params(compiler_params),
    )(*args, *after, *_in_hbm(ex.arrays))
    return outs[:n_out], outs[n_out:]


def _exchange_alone(ex, name, after=()):
    na, nl = len(ex.arrays), len(ex.landing)

    def body(*refs):
        outs = refs[na + len(after):na + len(after) + nl]
        ex.enter()
        ex.start(refs[:na], outs, refs[-2], refs[-1])
        ex.finish(refs[:na], outs, refs[-2], refs[-1])

    return pl.pallas_call(
        body, name=name, in_specs=[_ANY] * (na + len(after)), out_specs=[_ANY] * nl,
        out_shape=ex.landing, scratch_shapes=[pltpu.SemaphoreType.DMA((ex.n_sems,))] * 2,
        input_output_aliases=ex.aliases, compiler_params=ex.params(),
    )(*_in_hbm(ex.arrays), *after)


_HBM = pl.BlockSpec(memory_space=pltpu.HBM)
_SEM = pl.BlockSpec(memory_space=pltpu.SEMAPHORE)
_EFFECT = pltpu.SideEffectType.DATAFLOW_SIDE_EFFECTING


def _scatter_copies(srcs, lands, ssems, rsems):
    x, y, c, chips = _place()
    return [_remote(srcs[w].at[2 * px + py], lands[w].at[k], ssems[3 * w + k], rsems[3 * w + k], (px, py, c))
            for w in range(len(srcs)) for k, (px, py) in enumerate(chips)]


def _scatter_start(parts, name):
    parts = list(parts)
    n, ncp = len(parts), 3 * len(parts)
    lands = [lax.empty((3,) + p.shape[1:], p.dtype) for p in parts]

    def body(*refs):
        srcs, land_refs = refs[:n], refs[n:2 * n]
        ssems, rsems = refs[2 * n:2 * n + ncp], refs[2 * n + ncp:2 * n + 2 * ncp]
        for cp in _scatter_copies(srcs, land_refs, ssems, rsems):
            cp.start()
        token = refs[-1]
        token[...] = jnp.zeros_like(token)

    outs = pl.pallas_call(
        body, name=name,
        out_shape=([pltpu.SemaphoreType.DMA(())] * (2 * ncp) + [pltpu.HBM(a.shape, a.dtype) for a in parts + lands]
                   + [jax.ShapeDtypeStruct((8, 128), F32)]),
        in_specs=[_HBM] * (2 * n), out_specs=[_SEM] * (2 * ncp) + [_HBM] * (2 * n) + [_VM],
        input_output_aliases={i: 2 * ncp + i for i in range(2 * n)},
        compiler_params=pltpu.CompilerParams(has_side_effects=_EFFECT),
    )(*_in_hbm(parts), *_in_hbm(lands))
    sems, thru, token = outs[:2 * ncp], outs[2 * ncp:2 * ncp + 2 * n], outs[-1]
    return sems, thru, token


def _scatter_wait(sems, thru, after, name):
    n = len(thru) // 2
    ncp = 3 * n

    def body(*refs):
        srcs, land_refs = refs[:n], refs[n:2 * n]
        ssems, rsems = refs[2 * n:2 * n + ncp], refs[2 * n + ncp:2 * n + 2 * ncp]
        for cp in _scatter_copies(srcs, land_refs, ssems, rsems):
            cp.wait_send()
            cp.wait_recv()

    outs = pl.pallas_call(
        body, name=name, out_shape=[pltpu.HBM(a.shape, a.dtype) for a in thru],
        in_specs=[_HBM] * (2 * n) + [_SEM] * (2 * ncp) + [_ANY] * len(after), out_specs=[_HBM] * (2 * n),
        input_output_aliases={i: i for i in range(2 * n)},
        compiler_params=pltpu.CompilerParams(has_side_effects=_EFFECT),
    )(*thru, *sems, *after)
    return outs[:n], outs[n:]


def _swap_copies(srcs, lands, ssems, rsems):
    x, y, c, _ = _place()
    return [_remote(srcs[w].at[:, 1 - c], lands[w], ssems[w], rsems[w], (x, y, 1 - c)) for w in range(len(srcs))]


def _swap_start(grads, name):
    grads = list(grads)
    n = len(grads)
    lands = [lax.empty((NSH,) + g.shape[2:], g.dtype) for g in grads]

    def body(*refs):
        barrier = pltpu.get_barrier_semaphore()
        sibling = (lax.axis_index("x"), lax.axis_index("y"), 1 - lax.axis_index("c"))
        pl.semaphore_signal(barrier, inc=1, device_id=sibling, device_id_type=MESH)
        pl.semaphore_wait(barrier, 1)
        for cp in _swap_copies(refs[:n], refs[n:2 * n], refs[2 * n:3 * n], refs[3 * n:4 * n]):
            cp.start()
        refs[-1][...] = jnp.zeros_like(refs[-1])

    outs = pl.pallas_call(
        body, name=name,
        out_shape=([pltpu.SemaphoreType.DMA(())] * (2 * n) + [pltpu.HBM(a.shape, a.dtype) for a in grads + lands]
                   + [jax.ShapeDtypeStruct((8, 128), F32)]),
        in_specs=[_HBM] * (2 * n), out_specs=[_SEM] * (2 * n) + [_HBM] * (2 * n) + [_VM],
        input_output_aliases={i: 2 * n + i for i in range(2 * n)},
        compiler_params=pltpu.CompilerParams(has_side_effects=_EFFECT, collective_id=SIBLING_PAIR_ID),
    )(*_in_hbm(grads), *_in_hbm(lands))
    return outs[:2 * n], outs[2 * n:4 * n], outs[-1]


def _swap_wait(sems, thru, after, name):
    n = len(thru) // 2

    def body(*refs):
        for cp in _swap_copies(refs[:n], refs[n:2 * n], refs[2 * n:3 * n], refs[3 * n:4 * n]):
            cp.wait_send()
            cp.wait_recv()

    outs = pl.pallas_call(
        body, name=name, out_shape=[pltpu.HBM(a.shape, a.dtype) for a in thru],
        in_specs=[_HBM] * (2 * n) + [_SEM] * (2 * n) + [_ANY] * len(after), out_specs=[_HBM] * (2 * n),
        input_output_aliases={i: i for i in range(2 * n)},
        compiler_params=pltpu.CompilerParams(has_side_effects=_EFFECT),
    )(*thru, *sems, *after)
    return outs[:n], outs[n:]


def _share_copies(bufs, ssems, rsems, sending):
    x, y, c, _ = _place()
    out = []
    for w, ref in enumerate(bufs):
        slot = ref.at[c if sending else 1 - c]
        out.append(_remote(slot, slot, ssems[w], rsems[w], (x, y, 1 - c)))
    return out


def _share_start(bufs, after, name):
    bufs = list(bufs)
    n = len(bufs)

    def body(*refs):
        barrier = pltpu.get_barrier_semaphore()
        sibling = (lax.axis_index("x"), lax.axis_index("y"), 1 - lax.axis_index("c"))
        pl.semaphore_signal(barrier, inc=1, device_id=sibling, device_id_type=MESH)
        pl.semaphore_wait(barrier, 1)
        at = n + len(after)
        for cp in _share_copies(refs[:n], refs[at:at + n], refs[at + n:at + 2 * n], True):
            cp.start()
        refs[-1][...] = jnp.zeros_like(refs[-1])

    outs = pl.pallas_call(
        body, name=name,
        out_shape=([pltpu.SemaphoreType.DMA(())] * (2 * n) + [pltpu.HBM(a.shape, a.dtype) for a in bufs]
                   + [jax.ShapeDtypeStruct((8, 128), F32)]),
        in_specs=[_HBM] * n + [_ANY] * len(after), out_specs=[_SEM] * (2 * n) + [_HBM] * n + [_VM],
        input_output_aliases={i: 2 * n + i for i in range(n)},
        compiler_params=pltpu.CompilerParams(has_side_effects=_EFFECT, collective_id=SIBLING_PAIR_ID),
    )(*_in_hbm(bufs), *after)
    return outs[:2 * n], outs[2 * n:3 * n], outs[-1]


def _share_wait(sems, thru, after, name):
    n = len(thru)

    def body(*refs):
        for cp in _share_copies(refs[:n], refs[n:2 * n], refs[2 * n:3 * n], True):
            cp.wait_send()
        for cp in _share_copies(refs[:n], refs[n:2 * n], refs[2 * n:3 * n], False):
            cp.wait_recv()

    return pl.pallas_call(
        body, name=name, out_shape=[pltpu.HBM(a.shape, a.dtype) for a in thru],
        in_specs=[_HBM] * n + [_SEM] * (2 * n) + [_ANY] * len(after), out_specs=[_HBM] * n,
        input_output_aliases={i: i for i in range(n)},
        compiler_params=pltpu.CompilerParams(has_side_effects=_EFFECT),
    )(*thru, *sems, *after)


def _gather_copies(bufs, ssems, rsems, sending):
    x, y, c, chips = _place()
    out = []
    for w, ref in enumerate(bufs):
        half = ref.shape[1] // 2
        for k, (px, py) in enumerate(chips):
            rows = ref.at[2 * x + y if sending else 2 * px + py, pl.ds(c * half, half)]
            out.append(_remote(rows, rows, ssems[3 * w + k], rsems[3 * w + k], (px, py, c)))
    return out


def _gather_start(bufs, after, name):
    n, ncp = len(bufs), 3 * len(bufs)

    def body(*refs):
        ssems, rsems = refs[n + len(after):n + len(after) + ncp], refs[n + len(after) + ncp:n + len(after) + 2 * ncp]
        for cp in _gather_copies(refs[:n], ssems, rsems, True):
            cp.start()
        token = refs[-1]
        token[...] = jnp.zeros_like(token)

    outs = pl.pallas_call(
        body, name=name,
        out_shape=([pltpu.SemaphoreType.DMA(())] * (2 * ncp) + [pltpu.HBM(a.shape, a.dtype) for a in bufs]
                   + [jax.ShapeDtypeStruct((8, 128), F32)]),
        in_specs=[_HBM] * n + [_ANY] * len(after), out_specs=[_SEM] * (2 * ncp) + [_HBM] * n + [_VM],
        input_output_aliases={i: 2 * ncp + i for i in range(n)},
        compiler_params=pltpu.CompilerParams(has_side_effects=_EFFECT),
    )(*_in_hbm(bufs), *after)
    return outs[:2 * ncp], outs[2 * ncp:2 * ncp + n], outs[-1]


def _gather_wait(sems, thru, after, name):
    n = len(thru)
    ncp = 3 * n

    def body(*refs):
        ssems, rsems = refs[n:n + ncp], refs[n + ncp:n + 2 * ncp]
        for cp in _gather_copies(refs[:n], ssems, rsems, True):
            cp.wait_send()
        for cp in _gather_copies(refs[:n], ssems, rsems, False):
            cp.wait_recv()

    return pl.pallas_call(
        body, name=name, out_shape=[pltpu.HBM(a.shape, a.dtype) for a in thru],
        in_specs=[_HBM] * n + [_SEM] * (2 * ncp) + [_ANY] * len(after), out_specs=[_HBM] * n,
        input_output_aliases={i: i for i in range(n)},
        compiler_params=pltpu.CompilerParams(has_side_effects=_EFFECT),
    )(*thru, *sems, *after)


def _rms(x):
    r = lax.rsqrt(jnp.mean(x * x, axis=-1, keepdims=True) + EPS)
    return r, x * r


def _rms_bwd(dn, xr, r, gain):
    dng = dn * gain
    dx = r * (dng - xr * jnp.mean(dng * xr, axis=-1, keepdims=True))
    return dx, jnp.sum(dn * xr, axis=0, keepdims=True)


def _ffn_weight_groups(wgu_hbm, wgu_ref, wd_hbm, wd_ref):
    groups = []
    for c0, cn in FF_CHUNKS:
        groups += [_wide_columns(wgu_hbm, wgu_ref, c0, cn), _wide_columns(wgu_hbm, wgu_ref, DFF + c0, cn),
                   [(wd_hbm.at[pl.ds(c0, cn)], wd_ref.at[pl.ds(c0, cn)])]]
    return groups


def _ffn_fwd(x, gain, wgu, wd, name, exchange=None):
    tm = 256

    def body(x_ref, g_ref, wgu_hbm, wd_hbm, h_ref, n_ref, gu_ref, a_ref, wgu_ref, wd_ref):
        def compute(ready):
            x = x_ref[...]
            _, xr = _rms(x)
            n = (xr * g_ref[...]).astype(BF16)
            n_ref[...] = n
            acc = jnp.zeros((tm, D), F32)
            for i, (c0, cn) in enumerate(FF_CHUNKS):
                ready(3 * i)
                g = _nn(n, wgu_ref[:, c0:c0 + cn])
                ready(3 * i + 1)
                u = _nn(n, wgu_ref[:, DFF + c0:DFF + c0 + cn])
                gu_ref[:, c0:c0 + cn] = g.astype(BF16)
                gu_ref[:, DFF + c0:DFF + c0 + cn] = u.astype(BF16)
                half_act = (0.5 * (g * jax.nn.sigmoid(g) * u)).astype(BF16)
                a_ref[:, c0:c0 + cn] = half_act
                ready(3 * i + 2)
                acc = acc + _nn(half_act, wd_ref[c0:c0 + cn, :])
            h_ref[...] = x + acc

        _staged(_ffn_weight_groups(wgu_hbm, wgu_ref, wd_hbm, wd_ref), compute)

    return _call(
        body, (x, gain, wgu, wd), name=name, grid=(S // tm,),
        in_specs=[_rows(tm, D), _fixed((1, D)), _ANY, _ANY],
        out_specs=[_rows(tm, D), _rows(tm, D), _rows(tm, 4 * FFS), _rows(tm, DFF)],
        out_shape=[_sds((S, D), F32), _sds((S, D), BF16), _sds((S, 4 * FFS), BF16), _sds((S, DFF), BF16)],
        scratch_shapes=[_vmem_wide(wgu)] + _vmem_like(wd),
        compiler_params=_params(("arbitrary",), 56), exchange=exchange)


def _ffn_last(x, gain, wgu, wd, target, gf, name):
    tm = 256

    def body(x_ref, g_ref, wgu_hbm, wd_hbm, t_ref, gf_ref, dx_ref, dgu_ref, dg_ref, loss_ref, dgf_ref, n_ref,
             a_ref, dh_ref, wgu_ref, wd_ref):
        @pl.when(pl.program_id(0) == 0)
        def _():
            dg_ref[...] = jnp.zeros_like(dg_ref)
            dgf_ref[...] = jnp.zeros_like(dgf_ref)
            loss_ref[...] = jnp.zeros_like(loss_ref)

        def compute():
            x = x_ref[...]
            r0, xr = _rms(x)
            n = (xr * g_ref[...]).astype(BF16)
            n_ref[...] = n
            acc = jnp.zeros((tm, D), F32)
            kept = []
            for c0, cn in FF_CHUNKS:
                g = _nn(n, wgu_ref[:, c0:c0 + cn])
                u = _nn(n, wgu_ref[:, DFF + c0:DFF + c0 + cn])
                kept.append((g.astype(BF16), u.astype(BF16)))
                half_act = (0.5 * (g * jax.nn.sigmoid(g) * u)).astype(BF16)
                a_ref[:, c0:c0 + cn] = half_act
                acc = acc + _nn(half_act, wd_ref[c0:c0 + cn, :])
            h = x + acc
            gf = gf_ref[...]
            r, hr = _rms(h)
            err = hr * gf - t_ref[...]
            dh, dgain_f = _rms_bwd(err * (1.0 / D), hr, r, gf)
            dhb = dh.astype(BF16)
            dh_ref[...] = dhb
            dn = jnp.zeros((tm, D), F32)
            for (c0, cn), (gb, ub) in zip(FF_CHUNKS, kept):
                g, u = gb.astype(F32), ub.astype(F32)
                da = 0.5 * _nt(dhb, wd_ref[c0:c0 + cn, :])
                sg = jax.nn.sigmoid(g)
                dgb = (da * u * (sg * (1.0 + g * (1.0 - sg)))).astype(BF16)
                dub = (da * (g * sg)).astype(BF16)
                dgu_ref[:, c0:c0 + cn] = dgb
                dgu_ref[:, DFF + c0:DFF + c0 + cn] = dub
                dn = dn + _nt(dgb, wgu_ref[:, c0:c0 + cn]) + _nt(dub, wgu_ref[:, DFF + c0:DFF + c0 + cn])
            dx, dgain = _rms_bwd(dn, xr, r0, g_ref[...])
            dx_ref[...] = dh + dx
            dg_ref[...] += dgain
            dgf_ref[...] += dgain_f
            loss_ref[...] += jnp.full((1, 128), (0.5 / D) * jnp.sum(err * err), F32)

        _stage([(wgu_hbm, wgu_ref), (wd_hbm, wd_ref)])
        compute()

    return _call(
        body, (x, gain, wgu, wd, target, gf), name=name, grid=(S // tm,),
        in_specs=[_rows(tm, D), _fixed((1, D)), _ANY, _ANY, _rows(tm, D), _fixed((1, D))],
        out_specs=[_rows(tm, D), _rows(tm, 4 * FFS), _fixed((1, D)), _fixed((1, 128)), _fixed((1, D)),
                   _rows(tm, D), _rows(tm, DFF), _rows(tm, D)],
        out_shape=[_sds((S, D), F32), _sds((S, 4 * FFS), BF16), _sds((1, D), F32), _sds((1, 128), F32),
                   _sds((1, D), F32), _sds((S, D), BF16), _sds((S, DFF), BF16), _sds((S, D), BF16)],
        scratch_shapes=[_vmem_wide(wgu)] + _vmem_like(wd),
        compiler_params=_params(("arbitrary",), 58), free=(4, 5))


def _ffn_bwd_act(dh, gu, wd, name, exchange=None, after=()):
    tm = 512

    def body(dh_ref, gu_ref, wd_hbm, dgu_ref, wd_ref):
        def compute(ready):
            dhb = dh_ref[...].astype(BF16)
            for i, (c0, cn) in enumerate(FF_CHUNKS):
                g = gu_ref[:, c0:c0 + cn].astype(F32)
                u = gu_ref[:, DFF + c0:DFF + c0 + cn].astype(F32)
                ready(i)
                da = 0.5 * _nt(dhb, wd_ref[c0:c0 + cn, :])
                sg = jax.nn.sigmoid(g)
                dgu_ref[:, c0:c0 + cn] = (da * u * (sg * (1.0 + g * (1.0 - sg)))).astype(BF16)
                dgu_ref[:, DFF + c0:DFF + c0 + cn] = (da * (g * sg)).astype(BF16)

        _staged([[(wd_hbm.at[pl.ds(c0, cn)], wd_ref.at[pl.ds(c0, cn)])] for c0, cn in FF_CHUNKS], compute)

    res = _call(
        body, (dh, gu, wd), name=name, grid=(S // tm,),
        in_specs=[_rows(tm, D), _rows(tm, 4 * FFS), _ANY], out_specs=[_rows(tm, 4 * FFS)],
        out_shape=[_sds((S, 4 * FFS), BF16)], scratch_shapes=_vmem_like(wd),
        compiler_params=_params(("arbitrary",), 56), exchange=exchange, after=after)
    return res[0] if exchange is None else (res[0][0], res[1])


def _ffn_bwd_in(dh, x, gain, dgu, wgu, name, exchange=None, after=()):
    tm = 512

    def body(dh_ref, x_ref, g_ref, dgu_ref, wgu_hbm, dx_ref, dg_ref, wgu_ref):
        chunks = [(half + c0, cn) for half in (0, DFF) for c0, cn in FF_CHUNKS]

        @pl.when(pl.program_id(0) == 0)
        def _():
            dg_ref[...] = jnp.zeros_like(dg_ref)

        def compute(ready):
            dn = jnp.zeros((tm, D), F32)
            for k, (c0, cn) in enumerate(chunks):
                ready(k)
                dn = dn + _nt(dgu_ref[:, c0:c0 + cn], wgu_ref[:, c0:c0 + cn])
            r, xr = _rms(x_ref[...])
            dx, dgain = _rms_bwd(dn, xr, r, g_ref[...])
            dx_ref[...] = dh_ref[...] + dx
            dg_ref[...] += dgain

        _staged([_wide_columns(wgu_hbm, wgu_ref, c0, cn) for c0, cn in chunks], compute)

    return _call(
        body, (dh, x, gain, dgu, wgu), name=name, grid=(S // tm,),
        in_specs=[_rows(tm, D), _rows(tm, D), _fixed((1, D)), _rows(tm, 4 * FFS), _ANY],
        out_specs=[_rows(tm, D), _fixed((1, D))],
        out_shape=[_sds((S, D), F32), _sds((1, D), F32)],
        scratch_shapes=[_vmem_wide(wgu)],
        compiler_params=_params(("arbitrary",), 56), exchange=exchange, after=after)


def _mix_in(h, gain, w_in, after=()):
    tm = 512

    def body(h_ref, g_ref, w_hbm, u_ref, xp_ref, q_ref, k_ref, v_ref, gp_ref, gs_ref, w_ref):
        def compute(ready):
            _, hr = _rms(h_ref[...])
            u = (hr * g_ref[...]).astype(BF16)
            u_ref[...] = u
            ready(0)
            p0 = _nn(u, w_ref[0])
            xp_ref[...] = p0[:, :PW]
            q_ref[...] = p0[:, PW:].astype(BF16)
            ready(1)
            p1 = _nn(u, w_ref[1])
            k_ref[...] = p1[:, :SBW].astype(BF16)
            v_ref[...] = p1[:, SBW:].astype(BF16)
            ready(2)
            gp_ref[...] = jax.nn.sigmoid(_nn(u, w_ref[2])).astype(BF16)
            ready(3)
            gs_ref[...] = jax.nn.sigmoid(_nn(u, w_ref[3])).astype(BF16)

        _staged([[(w_hbm.at[j], w_ref.at[j])] for j in range(NSH)], compute)

    return _call(
        body, (h, gain, w_in), name="mix_in", grid=(S // tm,),
        in_specs=[_rows(tm, D), _fixed((1, D)), _ANY],
        out_specs=[_rows(tm, D), _rows(tm, PW), _rows(tm, SBW), _rows(tm, SBW), _rows(tm, SBW),
                   _rows(tm, D), _rows(tm, D)],
        out_shape=[_sds((S, D), BF16), _sds((S, PW), F32), _sds((S, SBW), BF16), _sds((S, SBW), BF16),
                   _sds((S, SBW), BF16), _sds((S, D), BF16), _sds((S, D), BF16)],
        scratch_shapes=_vmem_like(w_in),
        compiler_params=_params(("arbitrary",), 48), free=(1,), after=after)


def _hilo_dot(x, tri):
    hi = x.astype(BF16)
    lo = (x - hi.astype(F32)).astype(BF16)
    return _nn(hi, tri) + _nn(lo, tri)


def _log_terms(qk):
    z2 = qk * (SCALE * LOG2E)
    lb = jnp.minimum(z2, 0.0) - jnp.log2(1.0 + jnp.exp2(-jnp.abs(z2)))
    return lb, lb - z2


def _head_masks():
    lane = lax.broadcasted_iota(jnp.int32, (1, 2 * DH), 1)
    return (lane < DH, lane >= DH)


def _attn_fwd(q, k, v, exchange=None):
    T = TA

    def body(q_ref, k_ref, v_ref, o_ref, c_ref):
        i2 = 2 * pl.program_id(1)
        row = lax.broadcasted_iota(jnp.int32, (T, T), 0)
        col = lax.broadcasted_iota(jnp.int32, (T, T), 1)
        after = (row > col).astype(BF16)
        causal = col < row
        masks = _head_masks()
        qms = {}
        for b in range(QB):
            q2 = q_ref[b * T:(b + 1) * T, :]
            for h, hm in enumerate(masks):
                qms[b, h] = jnp.where(hm, q2, jnp.zeros_like(q2))

        def blocks(keys, pairs, carries, os):
            ks, vms = [], []
            for j in keys:
                rows = pl.ds(pl.multiple_of(j * T, T), T)
                vj = v_ref[rows, :]
                ks.append(k_ref[rows, :])
                vms.append([jnp.where(hm, vj, jnp.zeros_like(vj)) for hm in masks])
            units = [(n, h) for n in range(len(pairs)) for h in range(2)]
            qks = {(n, h): _nt(qms[pairs[n][0], h], ks[pairs[n][1]]) for n, h in units}
            lbs, l1ms = {}, {}
            for u in units:
                lbs[u], l1m = _log_terms(qks[u])
                l1ms[u] = jnp.where(causal, l1m, 0.0) if pairs[u[0]][2] else l1m
            cins = {u: _hilo_dot(l1ms[u], after) for u in units}
            carries, os = dict(carries), list(os)
            for n, h in units:
                b, key, diag = pairs[n]
                a = jnp.exp2(lbs[n, h] + cins[n, h] + carries[b, h])
                if diag:
                    a = jnp.where(causal, a, 0.0)
                os[b] = os[b] + _nn(a.astype(BF16), vms[key][h])
                carries[b, h] = carries[b, h] + jnp.sum(l1ms[n, h], axis=1, keepdims=True)
            return carries, tuple(os)

        carries = {(b, h): jnp.zeros((T, 1), F32) for b in range(QB) for h in range(2)}
        os = tuple(jnp.zeros((T, 2 * DH), F32) for _ in range(QB))
        carries, os = blocks([i2 + 1, i2], [(1, 0, True), (0, 1, True), (1, 1, False)], carries, os)
        carries, os = lax.fori_loop(
            0, i2 // 2,
            lambda t, c: blocks([i2 - 1 - 2 * t, i2 - 2 - 2 * t],
                                [(0, 0, False), (1, 0, False), (0, 1, False), (1, 1, False)], c[0], c[1]),
            (carries, os))
        for b in range(QB):
            o_ref[b * T:(b + 1) * T, :] = os[b].astype(BF16)
            c_ref[b * T:(b + 1) * T, :] = jnp.where(masks[0], carries[b, 0], carries[b, 1])

    blk = pl.BlockSpec((QB * T, 2 * DH), lambda p, i: (i, p))
    full = pl.BlockSpec((S, 2 * DH), lambda p, i: (0, p))
    return _call(
        body, (q, k, v), name="attn_fwd", grid=(SBW // (2 * DH), S // (QB * T)),
        in_specs=[blk, full, full], out_specs=[blk, blk],
        out_shape=[_sds((S, SBW), BF16), _sds((S, SBW), F32)],
        compiler_params=_params(("arbitrary", "arbitrary"), 40), exchange=exchange)


def _attn_bwd(q, k, v, do, ctot, after=()):
    T = TA
    nq = S // (QB * T)

    def body(q_ref, k_ref, v_ref, do_ref, c_ref, dq_ref, dk_ref, dv_ref, dk_acc, dv_acc):
        step = pl.program_id(1)
        i2 = 2 * step

        @pl.when(step == 0)
        def _():
            dk_acc[...] = jnp.zeros_like(dk_acc)
            dv_acc[...] = jnp.zeros_like(dv_acc)

        row = lax.broadcasted_iota(jnp.int32, (T, T), 0)
        col = lax.broadcasted_iota(jnp.int32, (T, T), 1)
        upto = (row <= col).astype(BF16)
        before = (row < col).astype(BF16)
        causal = col < row
        masks = _head_masks()
        qms, doms, ctots = {}, {}, {}
        for b in range(QB):
            q2, do2 = q_ref[b * T:(b + 1) * T, :], do_ref[b * T:(b + 1) * T, :]
            for h, hm in enumerate(masks):
                qms[b, h] = jnp.where(hm, q2, jnp.zeros_like(q2))
                doms[b, h] = jnp.where(hm, do2, jnp.zeros_like(do2))
                ctots[b, h] = c_ref[b * T:(b + 1) * T, h * DH:h * DH + 1]

        def blocks(keys, pairs, sums, dqs):
            rows = [pl.ds(pl.multiple_of(j * T, T), T) for j in keys]
            ks, vs = [k_ref[r, :] for r in rows], [v_ref[r, :] for r in rows]
            kms = [[jnp.where(hm, kj, jnp.zeros_like(kj)) for hm in masks] for kj in ks]
            units = [(n, h) for n in range(len(pairs)) for h in range(2)]
            qks = {(n, h): _nt(qms[pairs[n][0], h], ks[pairs[n][1]]) for n, h in units}
            das = {(n, h): _nt(doms[pairs[n][0], h], vs[pairs[n][1]]) for n, h in units}
            lbs, l1ms = {}, {}
            for u in units:
                lbs[u], l1m = _log_terms(qks[u])
                l1ms[u] = jnp.where(causal, l1m, 0.0) if pairs[u[0]][2] else l1m
            pins = {u: _hilo_dot(l1ms[u], upto) for u in units}
            sums = dict(sums)
            a_s, dls, cps = {}, {}, {}
            for n, h in units:
                b, _, diag = pairs[n]
                cl, cp = sums[b, h]
                a = jnp.exp2(lbs[n, h] + (ctots[b, h] - cl) - pins[n, h])
                if diag:
                    a = jnp.where(causal, a, 0.0)
                a_s[n, h] = a.astype(BF16)
                dls[n, h] = das[n, h] * a
                cps[n, h] = cp
                sums[b, h] = (cl + jnp.sum(l1ms[n, h], axis=1, keepdims=True),
                              cp + jnp.sum(dls[n, h], axis=1, keepdims=True))
            pexs = {u: _hilo_dot(dls[u], before) for u in units}
            dzbs = {}
            for u in units:
                dz = dls[u] - jnp.exp2(lbs[u]) * (dls[u] + pexs[u] + cps[u])
                if pairs[u[0]][2]:
                    dz = jnp.where(causal, dz, 0.0)
                dzbs[u] = dz.astype(BF16)
            dqs = list(dqs)
            for n, h in units:
                dqs[pairs[n][0]] = dqs[pairs[n][0]] + _nn(dzbs[n, h], kms[pairs[n][1]][h])
            for key, r in enumerate(rows):
                mine = [(n, h) for n, h in units if pairs[n][1] == key]
                dk_acc[r, :] += functools.reduce(jnp.add, [_tn(dzbs[u], qms[pairs[u[0]][0], u[1]]) for u in mine])
                dv_acc[r, :] += functools.reduce(jnp.add, [_tn(a_s[u], doms[pairs[u[0]][0], u[1]]) for u in mine])
            return sums, tuple(dqs)

        zero = jnp.zeros((T, 1), F32)
        sums = {(b, h): (zero, zero) for b in range(QB) for h in range(2)}
        dqs = tuple(jnp.zeros((T, 2 * DH), F32) for _ in range(QB))
        sums, dqs = lax.fori_loop(
            0, i2 // 2,
            lambda t, c: blocks([2 * t, 2 * t + 1],
                                [(0, 0, False), (1, 0, False), (0, 1, False), (1, 1, False)], c[0], c[1]),
            (sums, dqs))
        _, dqs = blocks([i2, i2 + 1], [(0, 0, True), (1, 0, False), (1, 1, True)], sums, dqs)
        for b in range(QB):
            dq_ref[b * T:(b + 1) * T, :] = (dqs[b] * SCALE).astype(BF16)

        @pl.when(step == nq - 1)
        def _():
            dk_ref[...] = (dk_acc[...] * SCALE).astype(BF16)
            dv_ref[...] = dv_acc[...].astype(BF16)

    blk = pl.BlockSpec((QB * T, 2 * DH), lambda p, i: (i, p))
    full = pl.BlockSpec((S, 2 * DH), lambda p, i: (0, p))
    return _call(
        body, (q, k, v, do, ctot), name="attn_bwd", grid=(SBW // (2 * DH), nq),
        in_specs=[blk, full, full, blk, blk], out_specs=[blk, full, full],
        out_shape=[_sds((S, SBW), BF16), _sds((S, SBW), BF16), _sds((S, SBW), BF16)],
        scratch_shapes=[pltpu.VMEM((S, 2 * DH), F32), pltpu.VMEM((S, 2 * DH), F32)],
        compiler_params=_params(("arbitrary", "arbitrary"), 40), after=after)


def _pool_counts(first_row, tm):
    pos = first_row + lax.broadcasted_iota(jnp.int32, (tm, 1), 0)
    return [jnp.minimum(pos + 1, w).astype(F32) for w in POOL_WINDOWS]


def _mix_out(h, xp, o_sb, gp, gs, w_group, scale, w_bp, w_ba, w_out, exchange=None):
    tm = 512

    def body(h_ref, xp_ref, o_ref, gp_ref, gs_ref, wg_hbm, sc_ref, wbp_hbm, wba_hbm, wo_hbm,
             h2_ref, pm_ref, p_ref, yp_ref, ys_ref, m_ref, halo, wg_ref, wbp_ref, wba_ref, wo_ref):
        _stage([(wg_hbm, wg_ref), (wbp_hbm, wbp_ref), (wba_hbm, wba_ref), (wo_hbm, wo_ref)])
        i = pl.program_id(0)

        @pl.when(i == 0)
        def _():
            halo[...] = jnp.zeros_like(halo)

        xp = xp_ref[...]
        ext = jnp.concatenate([halo[...], xp], axis=0)
        halo[...] = xp[tm - HALO:, :]
        counts = _pool_counts(i * tm, tm)
        for gi in range(len(POOL_WINDOWS)):
            lanes = slice(gi * PG, (gi + 1) * PG)
            win = ext[:, lanes]
            for step in range(gi + 1):
                win = win + pltpu.roll(win, 1 << step, 0)
            pm = (win[HALO:, :] / counts[gi] - xp[:, lanes]).astype(BF16)
            pm_ref[:, lanes] = pm
            p_ref[:, lanes] = (_nn(pm, wg_ref[gi]) * sc_ref[:, lanes]).astype(BF16)
        pb = p_ref[...]
        ob = o_ref[...]
        for j in range(NSH):
            cols = slice(j * (D // NSH), (j + 1) * (D // NSH))
            yp = _nn(pb, wbp_ref[j])
            ys = _nn(ob, wba_ref[j])
            yp_ref[:, cols] = yp.astype(BF16)
            ys_ref[:, cols] = ys.astype(BF16)
            m_ref[:, cols] = (gp_ref[:, cols].astype(F32) * yp + gs_ref[:, cols].astype(F32) * ys).astype(BF16)
        h2_ref[...] = h_ref[...] + _nn(m_ref[...], wo_ref[...])

    return _call(
        body, (h, xp, o_sb, gp, gs, w_group, scale, w_bp, w_ba, w_out), name="mix_out", grid=(S // tm,),
        in_specs=[_rows(tm, D), _rows(tm, PW), _rows(tm, SBW), _rows(tm, D), _rows(tm, D),
                  _ANY, _fixed((1, PW)), _ANY, _ANY, _ANY],
        out_specs=[_rows(tm, D), _rows(tm, PW), _rows(tm, PW), _rows(tm, D), _rows(tm, D), _rows(tm, D)],
        out_shape=[_sds((S, D), F32), _sds((S, PW), BF16), _sds((S, PW), BF16), _sds((S, D), BF16),
                   _sds((S, D), BF16), _sds((S, D), BF16)],
        scratch_shapes=[pltpu.VMEM((HALO, PW), F32)] + _vmem_like(w_group, w_bp, w_ba, w_out),
        compiler_params=_params(("arbitrary",), 48), free=(5, 6), exchange=exchange)


def _mix_bwd_out(dh, gp, gs, yp, ys, pm, w_group, scale, w_bp, w_ba, w_out, exchange=None):
    tm = 512
    nt = S // tm

    def body(dh_ref, gp_ref, gs_ref, yp_ref, ys_ref, pm_ref, wg_hbm, sc_ref, wbp_hbm, wba_hbm, wo_hbm,
             dlg_ref, dyp_ref, dys_ref, do_ref, dyg_ref, dxp_ref, dsc_ref, halo, wg_ref, wbp_ref, wba_ref, wo_ref):
        _stage([(wg_hbm, wg_ref), (wbp_hbm, wbp_ref), (wba_hbm, wba_ref), (wo_hbm, wo_ref)])
        step = pl.program_id(0)

        @pl.when(step == 0)
        def _():
            halo[...] = jnp.zeros_like(halo)
            dsc_ref[...] = jnp.zeros_like(dsc_ref)

        dm = _nt(dh_ref[...].astype(BF16), wo_ref[...])
        gp = gp_ref[...].astype(F32)
        gs = gs_ref[...].astype(F32)
        yp = yp_ref[...].astype(F32)
        ys = ys_ref[...].astype(F32)
        dlg_ref[:, :D] = (dm * yp * gp * (1.0 - gp)).astype(BF16)
        dlg_ref[:, D:] = (dm * ys * gs * (1.0 - gs)).astype(BF16)
        dyp_ref[...] = (dm * gp).astype(BF16)
        dys_ref[...] = (dm * gs).astype(BF16)
        dp = jnp.zeros((tm, PW), F32)
        do = jnp.zeros((tm, SBW), F32)
        for j in range(NSH):
            cols = slice(j * (D // NSH), (j + 1) * (D // NSH))
            dp = dp + _nt(dyp_ref[:, cols], wbp_ref[j])
            do = do + _nt(dys_ref[:, cols], wba_ref[j])
        do_ref[...] = do.astype(BF16)
        counts = _pool_counts((nt - 1 - step) * tm, tm)
        dscale = []
        for gi in range(len(POOL_WINDOWS)):
            lanes = slice(gi * PG, (gi + 1) * PG)
            dpg = dp[:, lanes]
            dscale.append(jnp.sum(dpg * _nn(pm_ref[:, lanes], wg_ref[gi]), axis=0, keepdims=True))
            dyg = (dpg * sc_ref[:, lanes]).astype(BF16)
            dyg_ref[:, lanes] = dyg
            dpm = _nt(dyg, wg_ref[gi])
            per = dpm / counts[gi]
            win = jnp.concatenate([per, halo[:, lanes]], axis=0)
            halo[:, lanes] = per[:HALO, :]
            for s in range(gi + 1):
                win = win + pltpu.roll(win, tm + HALO - (1 << s), 0)
            dxp_ref[:, lanes] = (win[:tm, :] - dpm).astype(BF16)
        dsc_ref[...] += jnp.concatenate(dscale, axis=1)

    rev = lambda width: pl.BlockSpec((tm, width), lambda i: (nt - 1 - i, 0))
    return _call(
        body, (dh, gp, gs, yp, ys, pm, w_group, scale, w_bp, w_ba, w_out), name="mix_bwd_out", grid=(nt,),
        in_specs=[rev(D), rev(D), rev(D), rev(D), rev(D), rev(PW), _ANY, _fixed((1, PW)), _ANY, _ANY, _ANY],
        out_specs=[rev(2 * D), rev(D), rev(D), rev(SBW), rev(PW), rev(PW), _fixed((1, PW))],
        out_shape=[_sds((S, 2 * D), BF16), _sds((S, D), BF16), _sds((S, D), BF16), _sds((S, SBW), BF16),
                   _sds((S, PW), BF16), _sds((S, PW), BF16), _sds((1, PW), F32)],
        scratch_shapes=[pltpu.VMEM((HALO, PW), F32)] + _vmem_like(w_group, w_bp, w_ba, w_out),
        compiler_params=_params(("arbitrary",), 48), exchange=exchange)


def _mix_bwd_in(dh, h, gain, pieces, w_in, exchange=None):
    tm = 512
    widths = [p.shape[1] for p in pieces]

    def body(dh_ref, h_ref, g_ref, *rest):
        piece_refs, (w_hbm, dx_ref, dg_ref, w_ref, dp_ref) = rest[:len(pieces)], rest[len(pieces):]
        @pl.when(pl.program_id(0) == 0)
        def _():
            dg_ref[...] = jnp.zeros_like(dg_ref)

        def compute(ready):
            at = 0
            for ref, width in zip(piece_refs, widths):
                dp_ref[:, at:at + width] = ref[...]
                at += width
            du = jnp.zeros((tm, D), F32)
            for j in range(NSH):
                ready(j)
                du = du + _nt(dp_ref[:, j * D:(j + 1) * D], w_ref[j])
            r, hr = _rms(h_ref[...])
            dx, dgain = _rms_bwd(du, hr, r, g_ref[...])
            dx_ref[...] = dh_ref[...] + dx
            dg_ref[...] += dgain

        _staged([[(w_hbm.at[j], w_ref.at[j])] for j in range(NSH)], compute)

    return _call(
        body, (dh, h, gain, *pieces, w_in), name="mix_bwd_in", grid=(S // tm,),
        in_specs=[_rows(tm, D), _rows(tm, D), _fixed((1, D))] + [_rows(tm, w) for w in widths] + [_ANY],
        out_specs=[_rows(tm, D), _fixed((1, D))],
        out_shape=[_sds((S, D), F32), _sds((1, D), F32)],
        scratch_shapes=_vmem_like(w_in) + [pltpu.VMEM((tm, 4 * D), BF16)],
        compiler_params=_params(("arbitrary",), 48), exchange=exchange)


def _wgrad_in(u, pieces):
    dxp, dq, dk, dv, dlg = pieces

    def body(u_ref, dxp_ref, dq_ref, dk_ref, dv_ref, dlg_ref, o_ref):
        j = pl.program_id(0)
        u = u_ref[...]

        def two(left_ref, right_ref):
            o_ref[:, :PW] = _tn(u, left_ref[...]).astype(BF16)
            o_ref[:, PW:] = _tn(u, right_ref[...]).astype(BF16)

        pl.when(j == 0)(lambda: two(dxp_ref, dq_ref))
        pl.when(j == 1)(lambda: two(dk_ref, dv_ref))

        @pl.when(j >= 2)
        def _():
            o_ref[...] = _tn(u, dlg_ref[...]).astype(BF16)

    whole = lambda width: pl.BlockSpec((S, width), lambda j: (0, 0))
    return _call(
        body, (u, dxp, dq, dk, dv, dlg), name="wgrad_in", grid=(NSH,),
        in_specs=[whole(D), whole(PW), whole(SBW), whole(SBW), whole(SBW),
                  pl.BlockSpec((S, D), lambda j: (0, jnp.maximum(j - 2, 0)))],
        out_specs=[pl.BlockSpec((None, D, D), lambda j: (j, 0, 0))], out_shape=[_sds((NSH, D, D), BF16)],
        compiler_params=_params(("arbitrary",), 56))[0]


def _wgrad(a, b, nblk, ti, name, out_dtype=BF16, exchange=None, after=()):
    ka, n = a.shape[1], b.shape[1]
    ns = n // nblk

    def body(a_ref, b_ref, o_ref):
        o_ref[...] = _tn(a_ref[...].astype(BF16), b_ref[...].astype(BF16)).astype(out_dtype)

    res = _call(
        body, (a, b), name=name, grid=(nblk, ka // ti),
        in_specs=[pl.BlockSpec((S, ti), lambda j, i: (0, i)), pl.BlockSpec((S, ns), lambda j, i: (0, j))],
        out_specs=[pl.BlockSpec((None, ti, ns), lambda j, i: (j, i, 0))],
        out_shape=[_sds((nblk, ka, ns), out_dtype)],
        compiler_params=_params(("arbitrary", "arbitrary"), 56), exchange=exchange, after=after)
    return res[0] if exchange is None else (res[0][0], res[1])


def _wgrad_ffn(n, dgu, a, dh, name):
    ti, n_gu = D // 2, 2 * NSH

    def body(n_ref, dgu_ref, a_ref, dh_ref, ggu_ref, gd_ref):
        s = pl.program_id(0)

        @pl.when(s < n_gu)
        def _():
            ggu_ref[...] = _tn(n_ref[...], dgu_ref[...]).astype(BF16)

        @pl.when(s >= n_gu)
        def _():
            gd_ref[...] = _tn(a_ref[...], dh_ref[...]).astype(BF16)

    rows = lambda s: jnp.where(s < n_gu, s % 2, 1)
    block = lambda s: jnp.minimum(s // 2, NSH - 1)
    down = lambda s: jnp.maximum(s - n_gu, 0)
    return _call(
        body, (n, dgu, a, dh), name=name, grid=(n_gu + 2,),
        in_specs=[pl.BlockSpec((S, ti), lambda s: (0, rows(s))), pl.BlockSpec((S, FFS), lambda s: (0, block(s))),
                  pl.BlockSpec((S, FFS), lambda s: (0, down(s))), pl.BlockSpec((S, D), lambda s: (0, 0))],
        out_specs=[pl.BlockSpec((None, ti, FFS), lambda s: (block(s), rows(s), 0)),
                   pl.BlockSpec((FFS, D), lambda s: (down(s), 0))],
        out_shape=[_sds((NSH, D, FFS), BF16), _sds((DFF, D), BF16)],
        compiler_params=_params(("arbitrary",), 56))


def _wgrad_branches(p, dyp, o_sb, dys, pm, dyg, mm, dh):
    cols = D // NSH

    def body(p_ref, dyp_ref, o_ref, dys_ref, pm_ref, dyg_ref, mm_ref, dh_ref, gbp_ref, gba_ref, gg_ref, go_ref):
        gbp_ref[...] = _tn(p_ref[...], dyp_ref[...]).astype(BF16)
        gba_ref[...] = _tn(o_ref[...], dys_ref[...]).astype(BF16)
        gg_ref[...] = _tn(pm_ref[...], dyg_ref[...])
        go_ref[...] = _tn(mm_ref[...], dh_ref[...].astype(BF16)).astype(BF16)

    whole = lambda width: pl.BlockSpec((S, width), lambda j: (0, 0))
    col = lambda width: pl.BlockSpec((S, width), lambda j: (0, j))
    return _call(
        body, (p, dyp, o_sb, dys, pm, dyg, mm, dh), name="wgrad_branches", grid=(NSH,),
        in_specs=[whole(PW), col(cols), whole(SBW), col(cols), col(PG), col(PG), whole(D), col(cols)],
        out_specs=[pl.BlockSpec((None, PW, cols), lambda j: (j, 0, 0)),
                   pl.BlockSpec((None, SBW, cols), lambda j: (j, 0, 0)),
                   pl.BlockSpec((None, PG, PG), lambda j: (j, 0, 0)),
                   pl.BlockSpec((D, cols), lambda j: (0, j))],
        out_shape=[_sds((NSH, PW, cols), BF16), _sds((NSH, SBW, cols), BF16), _sds((NSH, PG, PG), F32),
                   _sds((D, D), BF16)],
        compiler_params=_params(("arbitrary",), 40))


def _place():
    x, y, c = lax.axis_index("x"), lax.axis_index("y"), lax.axis_index("c")
    chips = [(1 - x, y), (x, 1 - y), (1 - x, 1 - y)]
    return x, y, c, chips


def _remote(src, dst, ssem, rsem, dev):
    return pltpu.make_async_remote_copy(src_ref=src, dst_ref=dst, send_sem=ssem, recv_sem=rsem,
                                        device_id=dev, device_id_type=MESH)


def _cast_into_block(ws, me_idx, name):
    steps = 4
    shapes = [(w.shape[0] // steps, w.shape[1]) for w in ws]

    def body(me_ref, *refs):
        for w_ref, o_ref in zip(refs[:len(ws)], refs[len(ws):]):
            o_ref[...] = w_ref[...].astype(BF16)

    return pl.pallas_call(
        body, name=name, out_shape=[_sds((NSH,) + w.shape, BF16) for w in ws],
        grid_spec=pltpu.PrefetchScalarGridSpec(
            num_scalar_prefetch=1, grid=(steps,),
            in_specs=[pl.BlockSpec((r, c), lambda s, me: (s, 0)) for r, c in shapes],
            out_specs=[pl.BlockSpec((None, r, c), lambda s, me: (me[0], s, 0)) for r, c in shapes]),
        compiler_params=_params(("arbitrary",), 32),
    )(me_idx, *ws)


def _ex_gather(bufs):
    n = len(bufs)
    per = 8

    def plan(outs, ssem, rsem, w):
        x, y, c, _ = _place()
        sib, nbr_x, nbr_y = (x, y, 1 - c), (1 - x, y, c), (x, 1 - y, c)
        half = outs[w].shape[1] // 2
        quarter = half // 2
        sem = lambda k: (ssem.at[per * w + k], rsem.at[per * w + k])
        rows = lambda blk, start, size: outs[w].at[blk, pl.ds(start, size)]
        mine = rows(2 * x + y, c * half, half)
        from_x = rows(2 * (1 - x) + y, c * half, half)
        from_y = rows(2 * x + (1 - y), c * half, half)
        diag = 2 * (1 - x) + (1 - y)
        pass_y = rows(2 * (1 - x) + y, c * half, quarter)
        pass_x = rows(2 * x + (1 - y), c * half + quarter, quarter)
        diag_0, diag_1 = rows(diag, c * half, quarter), rows(diag, c * half + quarter, quarter)
        first = [_remote(mine, mine, *sem(0), nbr_x), _remote(mine, mine, *sem(1), nbr_y)]
        arrivals = [
            (_remote(from_x, from_x, *sem(0), nbr_x),
             [_remote(pass_y, pass_y, *sem(2), nbr_y), _remote(from_x, from_x, *sem(4), sib)]),
            (_remote(from_y, from_y, *sem(1), nbr_y),
             [_remote(pass_x, pass_x, *sem(3), nbr_x), _remote(from_y, from_y, *sem(5), sib)]),
            (_remote(diag_0, diag_0, *sem(2), nbr_y), [_remote(diag_0, diag_0, *sem(6), sib)]),
            (_remote(diag_1, diag_1, *sem(3), nbr_x), [_remote(diag_1, diag_1, *sem(7), sib)]),
        ]
        other = (1 - c) * half
        from_sibling = [
            _remote(rows(2 * (1 - x) + y, other, half), rows(2 * (1 - x) + y, other, half), *sem(4), sib),
            _remote(rows(2 * x + (1 - y), other, half), rows(2 * x + (1 - y), other, half), *sem(5), sib),
            _remote(rows(diag, other, quarter), rows(diag, other, quarter), *sem(6), sib),
            _remote(rows(diag, other + quarter, quarter), rows(diag, other + quarter, quarter), *sem(7), sib),
        ]
        return first, arrivals, from_sibling

    def start(ins, outs, ssem, rsem):
        x, y, c, _ = _place()
        for w in range(n):
            half = outs[w].shape[1] // 2
            mine = outs[w].at[2 * x + y, pl.ds(c * half, half)]
            _remote(mine, mine, ssem.at[per * w], rsem.at[per * w], (1 - x, y, c)).start()
            _remote(mine, mine, ssem.at[per * w + 1], rsem.at[per * w + 1], (x, 1 - y, c)).start()

    def finish(ins, outs, ssem, rsem):
        plans = [plan(outs, ssem, rsem, w) for w in range(n)]
        started = []
        for direct in (True, False):
            for first, arrivals, _ in plans:
                for arrived, onward in (arrivals[:2] if direct else arrivals[2:]):
                    arrived.wait_recv()
                    for cp in onward:
                        cp.start()
                    started += onward
        for first, _, from_sibling in plans:
            for cp in from_sibling:
                cp.wait_recv()
            started += first
        for cp in started:
            cp.wait_send()

    return Exchange(bufs, [_sds(b.shape, b.dtype) for b in bufs], {w: w for w in range(n)}, per * n, start, finish)


def _ex_gather_direct(bufs):
    n = len(bufs)

    def copies(outs, ssem, rsem, only_first=False):
        x, y, c, chips = _place()
        me, sib = 2 * x + y, (x, y, 1 - c)
        first, relay, last = [], [], []
        for w in range(n):
            half = outs[w].shape[1] // 2
            mine = outs[w].at[me, pl.ds(c * half, half)]
            for k, (px, py) in enumerate(chips):
                sems = (ssem.at[6 * w + k], rsem.at[6 * w + k])
                sib_sems = (ssem.at[6 * w + 3 + k], rsem.at[6 * w + 3 + k])
                first.append(_remote(mine, mine, *sems, (px, py, c)))
                if only_first:
                    continue
                got = outs[w].at[2 * px + py, pl.ds(c * half, half)]
                relay.append((_remote(got, got, *sems, (px, py, c)), _remote(got, got, *sib_sems, sib)))
                theirs = outs[w].at[2 * px + py, pl.ds((1 - c) * half, half)]
                last.append(_remote(theirs, theirs, *sib_sems, sib))
        return first, relay, last

    def start(ins, outs, ssem, rsem):
        for cp in copies(outs, ssem, rsem, only_first=True)[0]:
            cp.start()

    def finish(ins, outs, ssem, rsem):
        first, relay, last = copies(outs, ssem, rsem)
        for arrived, onward in relay:
            arrived.wait_recv()
            onward.start()
        for cp in last:
            cp.wait_recv()
        for cp in first:
            cp.wait_send()
        for _, onward in relay:
            onward.wait_send()

    return Exchange(bufs, [_sds(b.shape, b.dtype) for b in bufs], {w: w for w in range(n)}, 6 * n, start, finish)


def _simple_exchange(arrays, landing, aliases, make_copies, sibling_only=False):
    def start(ins, outs, ssem, rsem):
        for cp, _ in make_copies(ins, outs, ssem, rsem, False):
            cp.start()

    def finish(ins, outs, ssem, rsem):
        cps = make_copies(ins, outs, ssem, rsem, True)
        for _, landed in cps:
            landed.wait_recv()
        for cp, _ in cps:
            cp.wait_send()

    return Exchange(arrays, landing, aliases, len(arrays) * 3, start, finish, sibling_only)


def _ex_pair_swap(grads):
    def make(ins, outs, ssem, rsem, landing):
        x, y, c, _ = _place()
        cps = [_remote(ins[w].at[:, 1 - c], outs[w], ssem.at[w], rsem.at[w], (x, y, 1 - c))
               for w in range(len(grads))]
        return [(cp, cp) for cp in cps]

    return _simple_exchange(grads, [_sds((NSH,) + g.shape[2:], g.dtype) for g in grads], {}, make, True)


def _ex_relay(bufs):
    def make(ins, outs, ssem, rsem, landing):
        x, y, c, chips = _place()
        sib = (x, y, 1 - c)
        out = []
        for w in range(len(bufs)):
            half = outs[w].shape[1] // 2
            for k, (px, py) in enumerate(chips):
                sems = (ssem.at[3 * w + k], rsem.at[3 * w + k])
                have = outs[w].at[2 * px + py, pl.ds(c * half, half)]
                miss = outs[w].at[2 * px + py, pl.ds((1 - c) * half, half)]
                out.append((_remote(have, have, *sems, sib), _remote(miss, miss, *sems, sib) if landing else None))
        return out

    return _simple_exchange(bufs, [_sds(b.shape, b.dtype) for b in bufs], {w: w for w in range(len(bufs))}, make, True)


def _ex_share(bufs):
    def make(ins, outs, ssem, rsem, landing):
        x, y, c, _ = _place()
        sib = (x, y, 1 - c)
        return [(_remote(outs[w].at[c], outs[w].at[c], ssem.at[w], rsem.at[w], sib),
                 _remote(outs[w].at[1 - c], outs[w].at[1 - c], ssem.at[w], rsem.at[w], sib) if landing else None)
                for w in range(len(bufs))]

    return _simple_exchange(bufs, [_sds(b.shape, b.dtype) for b in bufs], {w: w for w in range(len(bufs))}, make, True)


def _small_copies(slots, ssems, rsems, sending):
    x, y, c, _ = _place()
    out = []
    for m in range(1, 8):
        px, py, pc = x ^ (m >> 2), y ^ ((m >> 1) & 1), c ^ (m & 1)
        slot = slots.at[4 * x + 2 * y + c if sending else 4 * px + 2 * py + pc]
        out.append(_remote(slot, slot, ssems[m - 1], rsems[m - 1], (px, py, pc)))
    return out


def _small_gather_start(slots, name, after=()):
    at = 1 + len(after)

    def body(*refs):
        for cp in _small_copies(refs[0], refs[at:at + 7], refs[at + 7:at + 14], True):
            cp.start()
        refs[-1][...] = jnp.zeros_like(refs[-1])

    outs = pl.pallas_call(
        body, name=name,
        out_shape=([pltpu.SemaphoreType.DMA(())] * 14 + [pltpu.HBM(slots.shape, slots.dtype)]
                   + [jax.ShapeDtypeStruct((8, 128), F32)]),
        in_specs=[_HBM] + [_ANY] * len(after), out_specs=[_SEM] * 14 + [_HBM, _VM], input_output_aliases={0: 14},
        compiler_params=pltpu.CompilerParams(has_side_effects=_EFFECT),
    )(*_in_hbm([slots]), *after)
    return outs[:14], outs[14], outs[15]


def _small_gather_wait(sems, slots, after, name):
    def body(*refs):
        for cp in _small_copies(refs[0], refs[1:8], refs[8:15], True):
            cp.wait_send()
        for cp in _small_copies(refs[0], refs[1:8], refs[8:15], False):
            cp.wait_recv()

    return pl.pallas_call(
        body, name=name, out_shape=pltpu.HBM(slots.shape, slots.dtype),
        in_specs=[_HBM] + [_SEM] * 14 + [_ANY] * len(after), out_specs=_HBM, input_output_aliases={0: 0},
        compiler_params=pltpu.CompilerParams(has_side_effects=_EFFECT),
    )(slots, *sems, *after)


def _pair_sum(grads, gots, c_idx, name):
    n = len(grads)

    def body(c_ref, *refs):
        for a_ref, b_ref, o_ref in zip(refs[:n], refs[n:2 * n], refs[2 * n:]):
            o_ref[...] = (a_ref[...].astype(F32) + b_ref[...].astype(F32)).astype(BF16)

    halves = [g.shape[2:] for g in grads]
    return list(pl.pallas_call(
        body, name=name, out_shape=[_sds((NSH,) + h, BF16) for h in halves],
        grid_spec=pltpu.PrefetchScalarGridSpec(
            num_scalar_prefetch=1, grid=(NSH,),
            in_specs=[pl.BlockSpec((None, None) + h, lambda j, c: (j, c[0], 0, 0)) for h in halves]
            + [pl.BlockSpec((None,) + h, lambda j, c: (j, 0, 0)) for h in halves],
            out_specs=[pl.BlockSpec((None,) + h, lambda j, c: (j, 0, 0)) for h in halves]),
        compiler_params=_params(("arbitrary",), 40),
    )(c_idx, *_in_hbm(list(grads) + list(gots))))


def _chip_sum(owns, gots, place, name):
    n = len(owns)

    def body(place_ref, *refs):
        for own_ref, got_ref, o_ref in zip(refs[:n], refs[n:2 * n], refs[2 * n:]):
            acc = own_ref[...].astype(F32)
            for k in range(3):
                acc = acc + got_ref[k].astype(F32)
            o_ref[...] = acc

    shapes = [(o.shape[1] // 2, o.shape[2]) for o in owns]
    return list(pl.pallas_call(
        body, name=name, out_shape=[_sds((2, 2 * r, c), F32) for r, c in shapes],
        grid_spec=pltpu.PrefetchScalarGridSpec(
            num_scalar_prefetch=1, grid=(2,),
            in_specs=[pl.BlockSpec((None, r, c), lambda s, p: (p[0], s, 0)) for r, c in shapes]
            + [pl.BlockSpec((3, r, c), lambda s, p: (0, s, 0)) for r, c in shapes],
            out_specs=[pl.BlockSpec((None, r, c), lambda s, p: (p[1], s, 0)) for r, c in shapes]),
        compiler_params=_params(("arbitrary",), 40),
    )(place, *_in_hbm(list(owns) + list(gots))))


def _adamw_math(w, g, m, v):
    m = B1 * m + (1.0 - B1) * g
    v = B2 * v + (1.0 - B2) * (g * g)
    m_hat = m / (1.0 - B1 ** STEP)
    v_hat = v / (1.0 - B2 ** STEP)
    return -LR * (m_hat / (jnp.sqrt(v_hat) + AEPS) + WD * w), m, v


def _adamw(ws, gs, ms, vs, name, after=()):
    n, steps = len(ws), 4

    def body(*refs):
        ins, outs = refs[:4 * n], refs[4 * n:]
        for i in range(n):
            w_ref, g_ref, m_ref, v_ref = ins[4 * i:4 * i + 4]
            go_ref, d_ref, nm_ref, nv_ref = outs[4 * i:4 * i + 4]
            g = g_ref[...]
            go_ref[...] = g
            d_ref[...], nm_ref[...], nv_ref[...] = _adamw_math(w_ref[...], g, m_ref[...], v_ref[...])

    args, specs, shapes, free = [], [], [], []
    for i, (w, g, m, v) in enumerate(zip(ws, gs, ms, vs)):
        args += [w, g, m, v]
        specs += [pl.BlockSpec((w.shape[0] // steps, w.shape[1]), lambda r: (r, 0))] * 4
        shapes += [_sds(w.shape, F32)] * 4
        free += [4 * i, 4 * i + 2, 4 * i + 3]
    outs = _call(body, args, name=name, grid=(steps,), out_shape=shapes, in_specs=specs, out_specs=specs,
                 compiler_params=_params(("arbitrary",), 48), free=tuple(free), after=after)
    return [outs[4 * i:4 * i + 4] for i in range(n)]


def _small_update(gathered, w, m, v, entries):
    rows = w.shape[0]

    def body(ga_ref, w_ref, m_ref, v_ref, *out_refs):
        for j, (first, n) in enumerate(entries):
            mine = slice(first, first + n)
            g = ga_ref[mine, :]
            for dev in range(1, 8):
                g = g + ga_ref[dev * rows + first:dev * rows + first + n, :]
            results = (g,) + _adamw_math(w_ref[mine, :], g, m_ref[mine, :], v_ref[mine, :])
            for i, res in enumerate(results):
                out_refs[i * len(entries) + j][...] = res

    outs = pl.pallas_call(
        body, name="small_update",
        out_shape=[jax.ShapeDtypeStruct((n, 128), F32) for _ in range(4) for _, n in entries],
        in_specs=[_VM] * 4, out_specs=[_VM] * (4 * len(entries)),
    )(gathered, w, m, v)
    return [outs[i * len(entries):(i + 1) * len(entries)] for i in range(4)]


SMALL = ("ffn1_norm", "mix_norm", "ffn2_norm", "final_norm", "pool_scale", "loss", "pool_w_group")
BIG = ("ffn1_w_gate_up", "ffn1_w_down", "w_in", "w_branch_pool", "w_branch_attn", "w_out",
       "ffn2_w_gate_up", "ffn2_w_down")
ORDER = ("ffn1_norm", "ffn1_w_gate_up", "ffn1_w_down", "mix_norm", "w_in", "pool_w_group", "pool_scale",
         "w_branch_pool", "w_branch_attn", "w_out", "ffn2_norm", "ffn2_w_gate_up", "ffn2_w_down", "final_norm")
SMALL_ROWS = 560


def _pack_small(t):
    parts = []
    for k in SMALL:
        rows = t[k].reshape(-1, 128) if k in t else jnp.zeros((1, 128), F32)
        parts.append(jnp.pad(rows, ((0, -rows.shape[0] % 8), (0, 0))))
    packed = jnp.concatenate(parts, axis=0)
    assert packed.shape == (SMALL_ROWS, 128), packed.shape
    return packed


def _small_entries(like):
    out, at = [], 0
    for k in SMALL:
        n = like[k].size // 128 if k in like else 1
        out.append((at, n))
        at += n + (-n % 8)
    return out


def _halves(g):
    return g.reshape(NSH, 2, g.shape[1] // 2, g.shape[2])


def kernel(x, ffn1_norm, ffn1_w_gate_up, ffn1_w_down, mix_norm, w_in, pool_w_group, pool_scale, w_branch_pool, w_branch_attn, w_out, ffn2_norm, ffn2_w_gate_up, ffn2_w_down, final_norm, loss_target, m_ffn1_norm, m_ffn1_w_gate_up, m_ffn1_w_down, m_mix_norm, m_w_in, m_pool_w_group, m_pool_scale, m_w_branch_pool, m_w_branch_attn, m_w_out, m_ffn2_norm, m_ffn2_w_gate_up, m_ffn2_w_down, m_final_norm, v_ffn1_norm, v_ffn1_w_gate_up, v_ffn1_w_down, v_mix_norm, v_w_in, v_pool_w_group, v_pool_scale, v_w_branch_pool, v_w_branch_attn, v_w_out, v_ffn2_norm, v_ffn2_w_gate_up, v_ffn2_w_down, v_final_norm):
    wts = dict(ffn1_norm=ffn1_norm, ffn1_w_gate_up=ffn1_w_gate_up, ffn1_w_down=ffn1_w_down, mix_norm=mix_norm,
               w_in=w_in, pool_w_group=pool_w_group, pool_scale=pool_scale, w_branch_pool=w_branch_pool,
               w_branch_attn=w_branch_attn, w_out=w_out, ffn2_norm=ffn2_norm, ffn2_w_gate_up=ffn2_w_gate_up,
               ffn2_w_down=ffn2_w_down, final_norm=final_norm)
    mom = dict(ffn1_norm=m_ffn1_norm, ffn1_w_gate_up=m_ffn1_w_gate_up, ffn1_w_down=m_ffn1_w_down,
               mix_norm=m_mix_norm, w_in=m_w_in, pool_w_group=m_pool_w_group, pool_scale=m_pool_scale,
               w_branch_pool=m_w_branch_pool, w_branch_attn=m_w_branch_attn, w_out=m_w_out,
               ffn2_norm=m_ffn2_norm, ffn2_w_gate_up=m_ffn2_w_gate_up, ffn2_w_down=m_ffn2_w_down,
               final_norm=m_final_norm)
    var = dict(ffn1_norm=v_ffn1_norm, ffn1_w_gate_up=v_ffn1_w_gate_up, ffn1_w_down=v_ffn1_w_down,
               mix_norm=v_mix_norm, w_in=v_w_in, pool_w_group=v_pool_w_group, pool_scale=v_pool_scale,
               w_branch_pool=v_w_branch_pool, w_branch_attn=v_w_branch_attn, w_out=v_w_out,
               ffn2_norm=v_ffn2_norm, ffn2_w_gate_up=v_ffn2_w_gate_up, ffn2_w_down=v_ffn2_w_down,
               final_norm=v_final_norm)

    c_idx = lax.axis_index("c").astype(jnp.int32).reshape(1)
    me_idx = (2 * lax.axis_index("x") + lax.axis_index("y")).astype(jnp.int32).reshape(1)
    place = jnp.concatenate([me_idx, c_idx])
    x0, tgt = x[0], loss_target[0]
    wgrp = pool_w_group[0].astype(BF16)
    g1, gm, g2, gf = ffn1_norm, mix_norm, ffn2_norm, final_norm.reshape(1, D)
    grad, delta, new_m, new_v = {}, {}, {}, {}

    def pair_sums(keys, parts, got):
        return _pair_sum(parts, got, c_idx, "pair_sum_" + keys[0])

    def chip_sums(keys, chip_parts, owned):
        return _chip_sum(chip_parts, owned, place, "chip_sum_" + keys[0])

    def adamw(keys, after=()):
        outs = _adamw([wts[k][0] for k in keys], [grad[k][0] for k in keys], [mom[k][0] for k in keys],
                      [var[k][0] for k in keys], "adamw_" + keys[0], after=after)
        for k, res in zip(keys, outs):
            grad[k], delta[k], new_m[k], new_v[k] = (o.reshape(wts[k].shape) for o in res)

    first, late = ("ffn1_w_gate_up", "ffn1_w_down"), ("w_branch_pool", "w_branch_attn", "w_out",
                                                       "ffn2_w_gate_up", "ffn2_w_down")
    own = {}
    for group in (first, ("w_in",), late):
        own.update(zip(group, _cast_into_block([wts[k][0] for k in group], me_idx, "cast_" + group[0])))
    full = dict(zip(first, _exchange_alone(_ex_gather([own[k] for k in first]), "gather_ffn1")))
    wgu1, wd1 = full["ffn1_w_gate_up"], full["ffn1_w_down"].reshape(DFF, D)
    (h1, n1, gu1, a1), (win,) = _ffn_fwd(x0, g1, wgu1, wd1, "ffn1_fwd", exchange=_ex_gather_direct([own["w_in"]]))
    sems_l, thru_l, token_l = _gather_start([own[k_] for k_ in late], [h1], "gather_late_start")
    u, xp, q, k, v, gp, gs = _mix_in(h1, gm, win, after=(token_l,))
    o_sb, ctot = _attn_fwd(q, k, v)
    arrived = _gather_wait(sems_l, thru_l, [o_sb], "gather_late_wait")
    wbp, wba, wout = _exchange_alone(_ex_relay(arrived[:3]), "relay_mix")
    wout = wout.reshape(D, D)
    (h2, pm, p, yp, ys, mm), (wgu2, wd2) = _mix_out(h1, xp, o_sb, gp, gs, wgrp, pool_scale, wbp, wba, wout,
                                                    exchange=_ex_relay(arrived[3:]))
    wd2 = wd2.reshape(DFF, D)
    dh2, dgu3, d_g2, loss_row, d_gf, n3, a3, dh3 = _ffn_last(h2, g2, wgu2, wd2, tgt, gf, "ffn2")

    def grad_down(a, dh, name, exchange=None):
        res = _wgrad(a, dh, 1, FFS, name, exchange=exchange)
        halves = lambda g: [_halves(g.reshape(NSH, DFF // NSH, D))]
        return halves(res) if exchange is None else (halves(res[0]), res[1])

    k_gu2, k_d2, k_gu1, k_d1, k_in = (("ffn2_w_gate_up",), ("ffn2_w_down",), ("ffn1_w_gate_up",),
                                      ("ffn1_w_down",), ("w_in",))
    g_gu2, g_d2 = _wgrad_ffn(n3, dgu3, a3, dh3, "wgrad_ffn2")
    pa = [_halves(g_gu2), _halves(g_d2.reshape(NSH, DFF // NSH, D))]
    (dlg, dyp, dys, do_sb, dyg, dxp, d_scale), got_a = _mix_bwd_out(
        dh2, gp, gs, yp, ys, pm, wgrp, pool_scale, wbp, wba, wout, exchange=_ex_pair_swap(pa))
    chip_a = pair_sums(k_gu2 + k_d2, pa, got_a)
    kb = ("w_out", "w_branch_pool", "w_branch_attn")
    g_bp, g_ba, d_group, g_out = _wgrad_branches(p, dyp, o_sb, dys, pm, dyg, mm, dh2)
    pb = [_halves(g_out.reshape(NSH, D // NSH, D)), _halves(g_bp), _halves(g_ba)]
    k_a, k_in = k_gu2 + k_d2, k_in + kb
    sems_a, thru_a, token_a = _scatter_start(chip_a, "scatter_a_start")
    dq, dk, dv = _attn_bwd(q, k, v, do_sb, ctot, after=(token_a,))
    chip_a, owned_a = _scatter_wait(sems_a, thru_a, [dq], "scatter_a_wait")
    halves_a = chip_sums(k_a, chip_a, owned_a)
    dproj = (dxp, dq, dk, dv, dlg)
    (dh1, d_gm), both_a = _mix_bwd_in(dh2, h1, gm, dproj, win, exchange=_ex_share(halves_a))
    for i, k_ in enumerate(k_a):
        grad[k_] = both_a[i].reshape(wts[k_].shape)

    p_in = [_halves(_wgrad_in(u, dproj))] + pb
    p_d1, got_in = grad_down(a1, dh1, "wgrad_d1", exchange=_ex_pair_swap(p_in))
    sems_in, thru_in, token_in = _scatter_start(pair_sums(k_in, p_in, got_in), "scatter_in_start")
    dgu1, got_d1 = _ffn_bwd_act(dh1, gu1, wd1, "ffn1_bwd_act", exchange=_ex_pair_swap(p_d1), after=(token_in,))
    sems_d1, thru_d1, token_d1 = _scatter_start(pair_sums(k_d1, p_d1, got_d1), "scatter_d1_start")
    p_gu1 = [_halves(_wgrad(n1, dgu1, NSH, D, "wgrad_gu1", after=(token_in, token_d1)))]
    sems_w, thru_w, token_w = _swap_start(p_gu1, "swap_gu1_start")
    chip_in, owned_in = _scatter_wait(sems_in, thru_in, [token_w], "scatter_in_wait")
    chip_d1, owned_d1 = _scatter_wait(sems_d1, thru_d1, [token_w], "scatter_d1_wait")
    halves_in = chip_sums(k_in, chip_in, owned_in)
    p_gu1, got_gu1 = _swap_wait(sems_w, thru_w, halves_in, "swap_gu1_wait")
    sems, thru, token = _scatter_start(pair_sums(k_gu1, p_gu1, got_gu1), "scatter_gu1_start")
    sems_h, thru_h, token_h = _share_start(halves_in, [token], "share_in_start")
    adamw(k_a, after=(token_h,))
    landed = _share_wait(sems_h, thru_h, [delta[k_a[0]]], "share_in_wait")
    for i, k_ in enumerate(k_in):
        grad[k_] = landed[i].reshape(wts[k_].shape)
    adamw(k_in)
    dx, d_g1 = _ffn_bwd_in(dh1, x0, g1, dgu1, wgu1, "ffn1_bwd_in", after=(token,))
    small_g = dict(ffn1_norm=d_g1, mix_norm=d_gm, ffn2_norm=d_g2, final_norm=d_gf, pool_scale=d_scale,
                   pool_w_group=d_group, loss=loss_row)
    dev = 4 * lax.axis_index("x") + 2 * lax.axis_index("y") + lax.axis_index("c")
    slots = lax.dynamic_update_slice(jnp.zeros((8, SMALL_ROWS, 128), F32), _pack_small(small_g)[None], (dev, 0, 0))
    chip_gu1, owned_gu1 = _scatter_wait(sems, thru, [dx] + [delta[k_] for k_ in k_a + k_in], "scatter_gu1_wait")
    halves_last = chip_sums(k_d1 + k_gu1, chip_d1 + chip_gu1, owned_d1 + owned_gu1)
    sems_l, thru_l, token_l = _share_start(halves_last, [], "share_last_start")
    sems_s, slots, token_s = _small_gather_start(slots, "small_gather_start", after=(token_l,))
    both = _share_wait(sems_l, thru_l, [token_s], "share_last_wait")
    grad["ffn1_w_down"] = both[0].reshape(ffn1_w_down.shape)
    grad["ffn1_w_gate_up"] = both[1].reshape(ffn1_w_gate_up.shape)
    adamw(k_d1 + k_gu1, after=(token_s,))
    gathered = _small_gather_wait(sems_s, slots, [delta[k_] for k_ in k_d1 + k_gu1], "small_gather_wait")
    gathered = gathered.reshape(8 * SMALL_ROWS, 128)
    results = _small_update(gathered, _pack_small(wts), _pack_small(mom), _pack_small(var), _small_entries(wts))
    for dst, entries in zip((grad, delta, new_m, new_v), results):
        for k_, rows in zip(SMALL, entries):
            if k_ in wts:
                dst[k_] = rows.reshape(wts[k_].shape)
            elif dst is grad:
                loss = rows[0, 0]
    return (loss, dx[None], *[grad[k_] for k_ in ORDER], *[delta[k_] for k_ in ORDER],
            *[new_m[k_] for k_ in ORDER], *[new_v[k_] for k_ in ORDER])
```

```python
import dataclasses
import functools

import jax
import jax.numpy as jnp
from jax import lax
from jax.experimental import pallas as pl
from jax.experimental.pallas import tpu as pltpu

F32 = jnp.float32
BF16 = jnp.bfloat16

S = 2048
D = 1024
DFF = 2816
FFS = 2 * DFF // 4
NSH = 4
PW = 512
PG = 128
POOL_WINDOWS = (2, 4, 8, 16)
HALO = 16
SBW = 512
DH = 64
EPS = 1e-6
SCALE = 0.125
LOG2E = 1.4426950408889634
TA = 256
QB = 2
MIB = 1024 * 1024

LR, B1, B2, AEPS, WD, STEP = 0.001, 0.9, 0.999, 1e-08, 0.01, 10

_VM = pl.BlockSpec(memory_space=pltpu.VMEM)
_ANY = pl.BlockSpec(memory_space=pl.ANY)
MESH = pl.DeviceIdType.MESH
SIBLING_PAIR_ID = 1


def _nn(a, b):
    return jnp.dot(a, b, preferred_element_type=F32)


def _nt(a, b):
    return lax.dot_general(a, b, (((1,), (1,)), ((), ())), preferred_element_type=F32)


def _tn(a, b):
    return lax.dot_general(a, b, (((0,), (0,)), ((), ())), preferred_element_type=F32)


def _params(sem, vmem_mib):
    return pltpu.CompilerParams(dimension_semantics=sem, vmem_limit_bytes=vmem_mib * MIB)


def _rows(tm, width):
    return pl.BlockSpec((tm, width), lambda i: (i, 0))


def _fixed(shape):
    return pl.BlockSpec(shape, lambda *_: (0,) * len(shape))


def _sds(shape, dtype):
    return pltpu.HBM(shape, dtype)


def _in_hbm(args):
    return [pltpu.with_memory_space_constraint(a, pltpu.HBM) for a in args]


def _stage(pairs):
    pieces = 4

    def copy_all(sems):
        copies = []
        for src, dst in pairs:
            step = src.shape[0] // pieces
            for p in range(pieces):
                part = pl.ds(p * step, step)
                if len(dst.shape) == len(src.shape):
                    piece = (src.at[part], dst.at[part])
                else:
                    piece = (src.at[p], dst.at[:, pl.ds(p * src.shape[2], src.shape[2])])
                copies.append(pltpu.make_async_copy(*piece, sems.at[len(copies)]))
        for c in copies:
            c.start()
        for c in copies:
            c.wait()

    @pl.when(pl.program_id(0) == 0)
    def _():
        pl.run_scoped(copy_all, pltpu.SemaphoreType.DMA((pieces * len(pairs),)))


def _vmem_like(*arrays):
    return [pltpu.VMEM(a.shape, a.dtype) for a in arrays]


def _vmem_wide(w):
    return pltpu.VMEM((w.shape[1], w.shape[0] * w.shape[2]), w.dtype)


FF_CHUNKS = ((0, 1536), (1536, DFF - 1536))


def _wide_columns(src, dst, c0, cn):
    width, out = src.shape[2], []
    for p in range(src.shape[0]):
        lo, hi = max(c0, p * width), min(c0 + cn, (p + 1) * width)
        if lo < hi:
            out.append((src.at[p, :, pl.ds(lo - p * width, hi - lo)], dst.at[:, pl.ds(lo, hi - lo)]))
    return out


def _staged(groups, compute):
    first = pl.program_id(0) == 0

    def with_copies(sems):
        copies = []
        for group in groups:
            base = sum(len(g) for g in copies)
            copies.append([pltpu.make_async_copy(s, d, sems.at[base + i]) for i, (s, d) in enumerate(group)])
        for group in copies:
            for c in group:
                c.start()

        def ready(k):
            for c in copies[k]:
                c.wait()

        compute(ready)

    @pl.when(first)
    def _():
        pl.run_scoped(with_copies, pltpu.SemaphoreType.DMA((sum(len(g) for g in groups),)))

    @pl.when(jnp.logical_not(first))
    def _():
        compute(lambda k: None)


class Exchange:
    def __init__(self, arrays, landing, aliases, n_sems, start, finish, sibling_only=False):
        self.arrays, self.landing, self.aliases, self.n_sems = list(arrays), list(landing), dict(aliases), n_sems
        self.start, self.finish = start, finish
        self.sibling_only = sibling_only

    def enter(self):
        if self.sibling_only:
            barrier = pltpu.get_barrier_semaphore()
            sibling = (lax.axis_index("x"), lax.axis_index("y"), 1 - lax.axis_index("c"))
            pl.semaphore_signal(barrier, inc=1, device_id=sibling, device_id_type=MESH)
            pl.semaphore_wait(barrier, 1)

    def params(self, compiler_params=None):
        kw = dict(collective_id=SIBLING_PAIR_ID) if self.sibling_only else {}
        if compiler_params is None:
            return pltpu.CompilerParams(**kw)
        return dataclasses.replace(compiler_params, **kw)


def _call(body, args, *, name, grid, in_specs, out_specs, out_shape, scratch_shapes=(), compiler_params=None,
          exchange=None, free=(), after=()):
    args = [a if i in free else pltpu.with_memory_space_constraint(a, pltpu.HBM) for i, a in enumerate(args)]
    if exchange is None:
        n_in = len(in_specs)

        def plain(*refs):
            body(*refs[:n_in], *refs[n_in + len(after):])

        return pl.pallas_call(plain, name=name, grid=grid, in_specs=list(in_specs) + [_ANY] * len(after),
                              out_specs=out_specs, out_shape=out_shape, scratch_shapes=list(scratch_shapes),
                              compiler_params=compiler_params)(*args, *after)
    ex = exchange
    n_in, n_out, n_scr = len(in_specs), len(out_specs), len(scratch_shapes)
    na, nl = len(ex.arrays), len(ex.landing)

    def hosted(*refs):
        at = [0]

        def take(n):
            at[0] += n
            return refs[at[0] - n:at[0]]

        k_in, _, e_in, k_out, e_out, k_scr = take(n_in), take(len(after)), take(na), take(n_out), take(nl), take(n_scr)
        ssem, rsem = take(2)
        ids = [pl.program_id(a) for a in range(len(grid))]
        first = functools.reduce(jnp.logical_and, [i == 0 for i in ids])
        last = functools.reduce(jnp.logical_and, [i == g - 1 for i, g in zip(ids, grid)])

        @pl.when(first)
        def _():
            ex.enter()
            ex.start(e_in, e_out, ssem, rsem)

        body(*k_in, *k_out, *k_scr)

        @pl.when(last)
        def _():
            ex.finish(e_in, e_out, ssem, rsem)

    outs = pl.pallas_call(
        hosted, name=name, grid=grid,
        in_specs=list(in_specs) + [_ANY] * (len(after) + na), out_specs=list(out_specs) + [_ANY] * nl,
        out_shape=list(out_shape) + ex.landing,
        scratch_shapes=list(scratch_shapes) + [pltpu.SemaphoreType.DMA((ex.n_sems,))] * 2,
        input_output_aliases={n_in + len(after) + i: n_out + j for i, j in ex.aliases.items()},
        compiler_params=ex.params(compiler_params),
    )(*args, *after, *_in_hbm(ex.arrays))
    return outs[:n_out], outs[n_out:]


def _exchange_alone(ex, name, after=()):
    na, nl = len(ex.arrays), len(ex.landing)

    def body(*refs):
        outs = refs[na + len(after):na + len(after) + nl]
        ex.enter()
        ex.start(refs[:na], outs, refs[-2], refs[-1])
        ex.finish(refs[:na], outs, refs[-2], refs[-1])

    return pl.pallas_call(
        body, name=name, in_specs=[_ANY] * (na + len(after)), out_specs=[_ANY] * nl,
        out_shape=ex.landing, scratch_shapes=[pltpu.SemaphoreType.DMA((ex.n_sems,))] * 2,
        input_output_aliases=ex.aliases, compiler_params=ex.params(),
    )(*_in_hbm(ex.arrays), *after)


_HBM = pl.BlockSpec(memory_space=pltpu.HBM)
_SEM = pl.BlockSpec(memory_space=pltpu.SEMAPHORE)
_EFFECT = pltpu.SideEffectType.DATAFLOW_SIDE_EFFECTING


def _scatter_copies(srcs, lands, ssems, rsems):
    x, y, c, chips = _place()
    return [_remote(srcs[w].at[2 * px + py], lands[w].at[k], ssems[3 * w + k], rsems[3 * w + k], (px, py, c))
            for w in range(len(srcs)) for k, (px, py) in enumerate(chips)]


def _scatter_start(parts, name):
    parts = list(parts)
    n, ncp = len(parts), 3 * len(parts)
    lands = [lax.empty((3,) + p.shape[1:], p.dtype) for p in parts]

    def body(*refs):
        srcs, land_refs = refs[:n], refs[n:2 * n]
        ssems, rsems = refs[2 * n:2 * n + ncp], refs[2 * n + ncp:2 * n + 2 * ncp]
        for cp in _scatter_copies(srcs, land_refs, ssems, rsems):
            cp.start()
        token = refs[-1]
        token[...] = jnp.zeros_like(token)

    outs = pl.pallas_call(
        body, name=name,
        out_shape=([pltpu.SemaphoreType.DMA(())] * (2 * ncp) + [pltpu.HBM(a.shape, a.dtype) for a in parts + lands]
                   + [jax.ShapeDtypeStruct((8, 128), F32)]),
        in_specs=[_HBM] * (2 * n), out_specs=[_SEM] * (2 * ncp) + [_HBM] * (2 * n) + [_VM],
        input_output_aliases={i: 2 * ncp + i for i in range(2 * n)},
        compiler_params=pltpu.CompilerParams(has_side_effects=_EFFECT),
    )(*_in_hbm(parts), *_in_hbm(lands))
    sems, thru, token = outs[:2 * ncp], outs[2 * ncp:2 * ncp + 2 * n], outs[-1]
    return sems, thru, token


def _scatter_wait(sems, thru, after, name):
    n = len(thru) // 2
    ncp = 3 * n

    def body(*refs):
        srcs, land_refs = refs[:n], refs[n:2 * n]
        ssems, rsems = refs[2 * n:2 * n + ncp], refs[2 * n + ncp:2 * n + 2 * ncp]
        for cp in _scatter_copies(srcs, land_refs, ssems, rsems):
            cp.wait_send()
            cp.wait_recv()

    outs = pl.pallas_call(
        body, name=name, out_shape=[pltpu.HBM(a.shape, a.dtype) for a in thru],
        in_specs=[_HBM] * (2 * n) + [_SEM] * (2 * ncp) + [_ANY] * len(after), out_specs=[_HBM] * (2 * n),
        input_output_aliases={i: i for i in range(2 * n)},
        compiler_params=pltpu.CompilerParams(has_side_effects=_EFFECT),
    )(*thru, *sems, *after)
    return outs[:n], outs[n:]


def _swap_copies(srcs, lands, ssems, rsems):
    x, y, c, _ = _place()
    return [_remote(srcs[w].at[:, 1 - c], lands[w], ssems[w], rsems[w], (x, y, 1 - c)) for w in range(len(srcs))]


def _swap_start(grads, name):
    grads = list(grads)
    n = len(grads)
    lands = [lax.empty((NSH,) + g.shape[2:], g.dtype) for g in grads]

    def body(*refs):
        barrier = pltpu.get_barrier_semaphore()
        sibling = (lax.axis_index("x"), lax.axis_index("y"), 1 - lax.axis_index("c"))
        pl.semaphore_signal(barrier, inc=1, device_id=sibling, device_id_type=MESH)
        pl.semaphore_wait(barrier, 1)
        for cp in _swap_copies(refs[:n], refs[n:2 * n], refs[2 * n:3 * n], refs[3 * n:4 * n]):
            cp.start()
        refs[-1][...] = jnp.zeros_like(refs[-1])

    outs = pl.pallas_call(
        body, name=name,
        out_shape=([pltpu.SemaphoreType.DMA(())] * (2 * n) + [pltpu.HBM(a.shape, a.dtype) for a in grads + lands]
                   + [jax.ShapeDtypeStruct((8, 128), F32)]),
        in_specs=[_HBM] * (2 * n), out_specs=[_SEM] * (2 * n) + [_HBM] * (2 * n) + [_VM],
        input_output_aliases={i: 2 * n + i for i in range(2 * n)},
        compiler_params=pltpu.CompilerParams(has_side_effects=_EFFECT, collective_id=SIBLING_PAIR_ID),
    )(*_in_hbm(grads), *_in_hbm(lands))
    return outs[:2 * n], outs[2 * n:4 * n], outs[-1]


def _swap_wait(sems, thru, after, name):
    n = len(thru) // 2

    def body(*refs):
        for cp in _swap_copies(refs[:n], refs[n:2 * n], refs[2 * n:3 * n], refs[3 * n:4 * n]):
            cp.wait_send()
            cp.wait_recv()

    outs = pl.pallas_call(
        body, name=name, out_shape=[pltpu.HBM(a.shape, a.dtype) for a in thru],
        in_specs=[_HBM] * (2 * n) + [_SEM] * (2 * n) + [_ANY] * len(after), out_specs=[_HBM] * (2 * n),
        input_output_aliases={i: i for i in range(2 * n)},
        compiler_params=pltpu.CompilerParams(has_side_effects=_EFFECT),
    )(*thru, *sems, *after)
    return outs[:n], outs[n:]


def _share_copies(bufs, ssems, rsems, sending):
    x, y, c, _ = _place()
    out = []
    for w, ref in enumerate(bufs):
        slot = ref.at[c if sending else 1 - c]
        out.append(_remote(slot, slot, ssems[w], rsems[w], (x, y, 1 - c)))
    return out


def _share_start(bufs, after, name):
    bufs = list(bufs)
    n = len(bufs)

    def body(*refs):
        barrier = pltpu.get_barrier_semaphore()
        sibling = (lax.axis_index("x"), lax.axis_index("y"), 1 - lax.axis_index("c"))
        pl.semaphore_signal(barrier, inc=1, device_id=sibling, device_id_type=MESH)
        pl.semaphore_wait(barrier, 1)
        at = n + len(after)
        for cp in _share_copies(refs[:n], refs[at:at + n], refs[at + n:at + 2 * n], True):
            cp.start()
        refs[-1][...] = jnp.zeros_like(refs[-1])

    outs = pl.pallas_call(
        body, name=name,
        out_shape=([pltpu.SemaphoreType.DMA(())] * (2 * n) + [pltpu.HBM(a.shape, a.dtype) for a in bufs]
                   + [jax.ShapeDtypeStruct((8, 128), F32)]),
        in_specs=[_HBM] * n + [_ANY] * len(after), out_specs=[_SEM] * (2 * n) + [_HBM] * n + [_VM],
        input_output_aliases={i: 2 * n + i for i in range(n)},
        compiler_params=pltpu.CompilerParams(has_side_effects=_EFFECT, collective_id=SIBLING_PAIR_ID),
    )(*_in_hbm(bufs), *after)
    return outs[:2 * n], outs[2 * n:3 * n], outs[-1]


def _share_wait(sems, thru, after, name):
    n = len(thru)

    def body(*refs):
        for cp in _share_copies(refs[:n], refs[n:2 * n], refs[2 * n:3 * n], True):
            cp.wait_send()
        for cp in _share_copies(refs[:n], refs[n:2 * n], refs[2 * n:3 * n], False):
            cp.wait_recv()

    return pl.pallas_call(
        body, name=name, out_shape=[pltpu.HBM(a.shape, a.dtype) for a in thru],
        in_specs=[_HBM] * n + [_SEM] * (2 * n) + [_ANY] * len(after), out_specs=[_HBM] * n,
        input_output_aliases={i: i for i in range(n)},
        compiler_params=pltpu.CompilerParams(has_side_effects=_EFFECT),
    )(*thru, *sems, *after)


def _gather_copies(bufs, ssems, rsems, sending):
    x, y, c, chips = _place()
    out = []
    for w, ref in enumerate(bufs):
        half = ref.shape[1] // 2
        for k, (px, py) in enumerate(chips):
            rows = ref.at[2 * x + y if sending else 2 * px + py, pl.ds(c * half, half)]
            out.append(_remote(rows, rows, ssems[3 * w + k], rsems[3 * w + k], (px, py, c)))
    return out


def _gather_start(bufs, after, name):
    n, ncp = len(bufs), 3 * len(bufs)

    def body(*refs):
        ssems, rsems = refs[n + len(after):n + len(after) + ncp], refs[n + len(after) + ncp:n + len(after) + 2 * ncp]
        for cp in _gather_copies(refs[:n], ssems, rsems, True):
            cp.start()
        token = refs[-1]
        token[...] = jnp.zeros_like(token)

    outs = pl.pallas_call(
        body, name=name,
        out_shape=([pltpu.SemaphoreType.DMA(())] * (2 * ncp) + [pltpu.HBM(a.shape, a.dtype) for a in bufs]
                   + [jax.ShapeDtypeStruct((8, 128), F32)]),
        in_specs=[_HBM] * n + [_ANY] * len(after), out_specs=[_SEM] * (2 * ncp) + [_HBM] * n + [_VM],
        input_output_aliases={i: 2 * ncp + i for i in range(n)},
        compiler_params=pltpu.CompilerParams(has_side_effects=_EFFECT),
    )(*_in_hbm(bufs), *after)
    return outs[:2 * ncp], outs[2 * ncp:2 * ncp + n], outs[-1]


def _gather_wait(sems, thru, after, name):
    n = len(thru)
    ncp = 3 * n

    def body(*refs):
        ssems, rsems = refs[n:n + ncp], refs[n + ncp:n + 2 * ncp]
        for cp in _gather_copies(refs[:n], ssems, rsems, True):
            cp.wait_send()
        for cp in _gather_copies(refs[:n], ssems, rsems, False):
            cp.wait_recv()

    return pl.pallas_call(
        body, name=name, out_shape=[pltpu.HBM(a.shape, a.dtype) for a in thru],
        in_specs=[_HBM] * n + [_SEM] * (2 * ncp) + [_ANY] * len(after), out_specs=[_HBM] * n,
        input_output_aliases={i: i for i in range(n)},
        compiler_params=pltpu.CompilerParams(has_side_effects=_EFFECT),
    )(*thru, *sems, *after)


def _rms(x):
    r = lax.rsqrt(jnp.mean(x * x, axis=-1, keepdims=True) + EPS)
    return r, x * r


def _rms_bwd(dn, xr, r, gain):
    dng = dn * gain
    dx = r * (dng - xr * jnp.mean(dng * xr, axis=-1, keepdims=True))
    return dx, jnp.sum(dn * xr, axis=0, keepdims=True)


def _ffn_weight_groups(wgu_hbm, wgu_ref, wd_hbm, wd_ref):
    groups = []
    for c0, cn in FF_CHUNKS:
        groups += [_wide_columns(wgu_hbm, wgu_ref, c0, cn), _wide_columns(wgu_hbm, wgu_ref, DFF + c0, cn),
                   [(wd_hbm.at[pl.ds(c0, cn)], wd_ref.at[pl.ds(c0, cn)])]]
    return groups


def _ffn_fwd(x, gain, wgu, wd, name, exchange=None):
    tm = 256

    def body(x_ref, g_ref, wgu_hbm, wd_hbm, h_ref, n_ref, gu_ref, a_ref, wgu_ref, wd_ref):
        def compute(ready):
            x = x_ref[...]
            _, xr = _rms(x)
            n = (xr * g_ref[...]).astype(BF16)
            n_ref[...] = n
            acc = jnp.zeros((tm, D), F32)
            for i, (c0, cn) in enumerate(FF_CHUNKS):
                ready(3 * i)
                g = _nn(n, wgu_ref[:, c0:c0 + cn])
                ready(3 * i + 1)
                u = _nn(n, wgu_ref[:, DFF + c0:DFF + c0 + cn])
                gu_ref[:, c0:c0 + cn] = g.astype(BF16)
                gu_ref[:, DFF + c0:DFF + c0 + cn] = u.astype(BF16)
                half_act = (0.5 * (g * jax.nn.sigmoid(g) * u)).astype(BF16)
                a_ref[:, c0:c0 + cn] = half_act
                ready(3 * i + 2)
                acc = acc + _nn(half_act, wd_ref[c0:c0 + cn, :])
            h_ref[...] = x + acc

        _staged(_ffn_weight_groups(wgu_hbm, wgu_ref, wd_hbm, wd_ref), compute)

    return _call(
        body, (x, gain, wgu, wd), name=name, grid=(S // tm,),
        in_specs=[_rows(tm, D), _fixed((1, D)), _ANY, _ANY],
        out_specs=[_rows(tm, D), _rows(tm, D), _rows(tm, 4 * FFS), _rows(tm, DFF)],
        out_shape=[_sds((S, D), F32), _sds((S, D), BF16), _sds((S, 4 * FFS), BF16), _sds((S, DFF), BF16)],
        scratch_shapes=[_vmem_wide(wgu)] + _vmem_like(wd),
        compiler_params=_params(("arbitrary",), 56), exchange=exchange)


def _ffn_last(x, gain, wgu, wd, target, gf, name):
    tm = 256

    def body(x_ref, g_ref, wgu_hbm, wd_hbm, t_ref, gf_ref, dx_ref, dgu_ref, dg_ref, loss_ref, dgf_ref, n_ref,
             a_ref, dh_ref, wgu_ref, wd_ref):
        @pl.when(pl.program_id(0) == 0)
        def _():
            dg_ref[...] = jnp.zeros_like(dg_ref)
            dgf_ref[...] = jnp.zeros_like(dgf_ref)
            loss_ref[...] = jnp.zeros_like(loss_ref)

        def compute():
            x = x_ref[...]
            r0, xr = _rms(x)
            n = (xr * g_ref[...]).astype(BF16)
            n_ref[...] = n
            acc = jnp.zeros((tm, D), F32)
            kept = []
            for c0, cn in FF_CHUNKS:
                g = _nn(n, wgu_ref[:, c0:c0 + cn])
                u = _nn(n, wgu_ref[:, DFF + c0:DFF + c0 + cn])
                kept.append((g.astype(BF16), u.astype(BF16)))
                half_act = (0.5 * (g * jax.nn.sigmoid(g) * u)).astype(BF16)
                a_ref[:, c0:c0 + cn] = half_act
                acc = acc + _nn(half_act, wd_ref[c0:c0 + cn, :])
            h = x + acc
            gf = gf_ref[...]
            r, hr = _rms(h)
            err = hr * gf - t_ref[...]
            dh, dgain_f = _rms_bwd(err * (1.0 / D), hr, r, gf)
            dhb = dh.astype(BF16)
            dh_ref[...] = dhb
            dn = jnp.zeros((tm, D), F32)
            for (c0, cn), (gb, ub) in zip(FF_CHUNKS, kept):
                g, u = gb.astype(F32), ub.astype(F32)
                da = 0.5 * _nt(dhb, wd_ref[c0:c0 + cn, :])
                sg = jax.nn.sigmoid(g)
                dgb = (da * u * (sg * (1.0 + g * (1.0 - sg)))).astype(BF16)
                dub = (da * (g * sg)).astype(BF16)
                dgu_ref[:, c0:c0 + cn] = dgb
                dgu_ref[:, DFF + c0:DFF + c0 + cn] = dub
                dn = dn + _nt(dgb, wgu_ref[:, c0:c0 + cn]) + _nt(dub, wgu_ref[:, DFF + c0:DFF + c0 + cn])
            dx, dgain = _rms_bwd(dn, xr, r0, g_ref[...])
            dx_ref[...] = dh + dx
            dg_ref[...] += dgain
            dgf_ref[...] += dgain_f
            loss_ref[...] += jnp.full((1, 128), (0.5 / D) * jnp.sum(err * err), F32)

        _stage([(wgu_hbm, wgu_ref), (wd_hbm, wd_ref)])
        compute()

    return _call(
        body, (x, gain, wgu, wd, target, gf), name=name, grid=(S // tm,),
        in_specs=[_rows(tm, D), _fixed((1, D)), _ANY, _ANY, _rows(tm, D), _fixed((1, D))],
        out_specs=[_rows(tm, D), _rows(tm, 4 * FFS), _fixed((1, D)), _fixed((1, 128)), _fixed((1, D)),
                   _rows(tm, D), _rows(tm, DFF), _rows(tm, D)],
        out_shape=[_sds((S, D), F32), _sds((S, 4 * FFS), BF16), _sds((1, D), F32), _sds((1, 128), F32),
                   _sds((1, D), F32), _sds((S, D), BF16), _sds((S, DFF), BF16), _sds((S, D), BF16)],
        scratch_shapes=[_vmem_wide(wgu)] + _vmem_like(wd),
        compiler_params=_params(("arbitrary",), 58), free=(4, 5))


def _ffn_bwd_act(dh, gu, wd, name, exchange=None, after=()):
    tm = 512

    def body(dh_ref, gu_ref, wd_hbm, dgu_ref, wd_ref):
        def compute(ready):
            dhb = dh_ref[...].astype(BF16)
            for i, (c0, cn) in enumerate(FF_CHUNKS):
                g = gu_ref[:, c0:c0 + cn].astype(F32)
                u = gu_ref[:, DFF + c0:DFF + c0 + cn].astype(F32)
                ready(i)
                da = 0.5 * _nt(dhb, wd_ref[c0:c0 + cn, :])
                sg = jax.nn.sigmoid(g)
                dgu_ref[:, c0:c0 + cn] = (da * u * (sg * (1.0 + g * (1.0 - sg)))).astype(BF16)
                dgu_ref[:, DFF + c0:DFF + c0 + cn] = (da * (g * sg)).astype(BF16)

        _staged([[(wd_hbm.at[pl.ds(c0, cn)], wd_ref.at[pl.ds(c0, cn)])] for c0, cn in FF_CHUNKS], compute)

    res = _call(
        body, (dh, gu, wd), name=name, grid=(S // tm,),
        in_specs=[_rows(tm, D), _rows(tm, 4 * FFS), _ANY], out_specs=[_rows(tm, 4 * FFS)],
        out_shape=[_sds((S, 4 * FFS), BF16)], scratch_shapes=_vmem_like(wd),
        compiler_params=_params(("arbitrary",), 56), exchange=exchange, after=after)
    return res[0] if exchange is None else (res[0][0], res[1])


def _ffn_bwd_in(dh, x, gain, dgu, wgu, name, exchange=None, after=()):
    tm = 512

    def body(dh_ref, x_ref, g_ref, dgu_ref, wgu_hbm, dx_ref, dg_ref, wgu_ref):
        chunks = [(half + c0, cn) for half in (0, DFF) for c0, cn in FF_CHUNKS]

        @pl.when(pl.program_id(0) == 0)
        def _():
            dg_ref[...] = jnp.zeros_like(dg_ref)

        def compute(ready):
            dn = jnp.zeros((tm, D), F32)
            for k, (c0, cn) in enumerate(chunks):
                ready(k)
                dn = dn + _nt(dgu_ref[:, c0:c0 + cn], wgu_ref[:, c0:c0 + cn])
            r, xr = _rms(x_ref[...])
            dx, dgain = _rms_bwd(dn, xr, r, g_ref[...])
            dx_ref[...] = dh_ref[...] + dx
            dg_ref[...] += dgain

        _staged([_wide_columns(wgu_hbm, wgu_ref, c0, cn) for c0, cn in chunks], compute)

    return _call(
        body, (dh, x, gain, dgu, wgu), name=name, grid=(S // tm,),
        in_specs=[_rows(tm, D), _rows(tm, D), _fixed((1, D)), _rows(tm, 4 * FFS), _ANY],
        out_specs=[_rows(tm, D), _fixed((1, D))],
        out_shape=[_sds((S, D), F32), _sds((1, D), F32)],
        scratch_shapes=[_vmem_wide(wgu)],
        compiler_params=_params(("arbitrary",), 56), exchange=exchange, after=after)


def _mix_in(h, gain, w_in, after=()):
    tm = 512

    def body(h_ref, g_ref, w_hbm, u_ref, xp_ref, q_ref, k_ref, v_ref, gp_ref, gs_ref, w_ref):
        def compute(ready):
            _, hr = _rms(h_ref[...])
            u = (hr * g_ref[...]).astype(BF16)
            u_ref[...] = u
            ready(0)
            p0 = _nn(u, w_ref[0])
            xp_ref[...] = p0[:, :PW]
            q_ref[...] = p0[:, PW:].astype(BF16)
            ready(1)
            p1 = _nn(u, w_ref[1])
            k_ref[...] = p1[:, :SBW].astype(BF16)
            v_ref[...] = p1[:, SBW:].astype(BF16)
            ready(2)
            gp_ref[...] = jax.nn.sigmoid(_nn(u, w_ref[2])).astype(BF16)
            ready(3)
            gs_ref[...] = jax.nn.sigmoid(_nn(u, w_ref[3])).astype(BF16)

        _staged([[(w_hbm.at[j], w_ref.at[j])] for j in range(NSH)], compute)

    return _call(
        body, (h, gain, w_in), name="mix_in", grid=(S // tm,),
        in_specs=[_rows(tm, D), _fixed((1, D)), _ANY],
        out_specs=[_rows(tm, D), _rows(tm, PW), _rows(tm, SBW), _rows(tm, SBW), _rows(tm, SBW),
                   _rows(tm, D), _rows(tm, D)],
        out_shape=[_sds((S, D), BF16), _sds((S, PW), F32), _sds((S, SBW), BF16), _sds((S, SBW), BF16),
                   _sds((S, SBW), BF16), _sds((S, D), BF16), _sds((S, D), BF16)],
        scratch_shapes=_vmem_like(w_in),
        compiler_params=_params(("arbitrary",), 48), free=(1,), after=after)


def _hilo_dot(x, tri):
    hi = x.astype(BF16)
    lo = (x - hi.astype(F32)).astype(BF16)
    return _nn(hi, tri) + _nn(lo, tri)


def _log_terms(qk):
    z2 = qk * (SCALE * LOG2E)
    lb = jnp.minimum(z2, 0.0) - jnp.log2(1.0 + jnp.exp2(-jnp.abs(z2)))
    return lb, lb - z2


def _head_masks():
    lane = lax.broadcasted_iota(jnp.int32, (1, 2 * DH), 1)
    return (lane < DH, lane >= DH)


def _attn_fwd(q, k, v, exchange=None):
    T = TA

    def body(q_ref, k_ref, v_ref, o_ref, c_ref):
        i2 = 2 * pl.program_id(1)
        row = lax.broadcasted_iota(jnp.int32, (T, T), 0)
        col = lax.broadcasted_iota(jnp.int32, (T, T), 1)
        after = (row > col).astype(BF16)
        causal = col < row
        masks = _head_masks()
        qms = {}
        for b in range(QB):
            q2 = q_ref[b * T:(b + 1) * T, :]
            for h, hm in enumerate(masks):
                qms[b, h] = jnp.where(hm, q2, jnp.zeros_like(q2))

        def blocks(keys, pairs, carries, os):
            ks, vms = [], []
            for j in keys:
                rows = pl.ds(pl.multiple_of(j * T, T), T)
                vj = v_ref[rows, :]
                ks.append(k_ref[rows, :])
                vms.append([jnp.where(hm, vj, jnp.zeros_like(vj)) for hm in masks])
            units = [(n, h) for n in range(len(pairs)) for h in range(2)]
            qks = {(n, h): _nt(qms[pairs[n][0], h], ks[pairs[n][1]]) for n, h in units}
            lbs, l1ms = {}, {}
            for u in units:
                lbs[u], l1m = _log_terms(qks[u])
                l1ms[u] = jnp.where(causal, l1m, 0.0) if pairs[u[0]][2] else l1m
            cins = {u: _hilo_dot(l1ms[u], after) for u in units}
            carries, os = dict(carries), list(os)
            for n, h in units:
                b, key, diag = pairs[n]
                a = jnp.exp2(lbs[n, h] + cins[n, h] + carries[b, h])
                if diag:
                    a = jnp.where(causal, a, 0.0)
                os[b] = os[b] + _nn(a.astype(BF16), vms[key][h])
                carries[b, h] = carries[b, h] + jnp.sum(l1ms[n, h], axis=1, keepdims=True)
            return carries, tuple(os)

        carries = {(b, h): jnp.zeros((T, 1), F32) for b in range(QB) for h in range(2)}
        os = tuple(jnp.zeros((T, 2 * DH), F32) for _ in range(QB))
        carries, os = blocks([i2 + 1, i2], [(1, 0, True), (0, 1, True), (1, 1, False)], carries, os)
        carries, os = lax.fori_loop(
            0, i2 // 2,
            lambda t, c: blocks([i2 - 1 - 2 * t, i2 - 2 - 2 * t],
                                [(0, 0, False), (1, 0, False), (0, 1, False), (1, 1, False)], c[0], c[1]),
            (carries, os))
        for b in range(QB):
            o_ref[b * T:(b + 1) * T, :] = os[b].astype(BF16)
            c_ref[b * T:(b + 1) * T, :] = jnp.where(masks[0], carries[b, 0], carries[b, 1])

    blk = pl.BlockSpec((QB * T, 2 * DH), lambda p, i: (i, p))
    full = pl.BlockSpec((S, 2 * DH), lambda p, i: (0, p))
    return _call(
        body, (q, k, v), name="attn_fwd", grid=(SBW // (2 * DH), S // (QB * T)),
        in_specs=[blk, full, full], out_specs=[blk, blk],
        out_shape=[_sds((S, SBW), BF16), _sds((S, SBW), F32)],
        compiler_params=_params(("arbitrary", "arbitrary"), 40), exchange=exchange)


def _attn_bwd(q, k, v, do, ctot, after=()):
    T = TA
    nq = S // (QB * T)

    def body(q_ref, k_ref, v_ref, do_ref, c_ref, dq_ref, dk_ref, dv_ref, dk_acc, dv_acc):
        step = pl.program_id(1)
        i2 = 2 * step

        @pl.when(step == 0)
        def _():
            dk_acc[...] = jnp.zeros_like(dk_acc)
            dv_acc[...] = jnp.zeros_like(dv_acc)

        row = lax.broadcasted_iota(jnp.int32, (T, T), 0)
        col = lax.broadcasted_iota(jnp.int32, (T, T), 1)
        upto = (row <= col).astype(BF16)
        before = (row < col).astype(BF16)
        causal = col < row
        masks = _head_masks()
        qms, doms, ctots = {}, {}, {}
        for b in range(QB):
            q2, do2 = q_ref[b * T:(b + 1) * T, :], do_ref[b * T:(b + 1) * T, :]
            for h, hm in enumerate(masks):
                qms[b, h] = jnp.where(hm, q2, jnp.zeros_like(q2))
                doms[b, h] = jnp.where(hm, do2, jnp.zeros_like(do2))
                ctots[b, h] = c_ref[b * T:(b + 1) * T, h * DH:h * DH + 1]

        def blocks(keys, pairs, sums, dqs):
            rows = [pl.ds(pl.multiple_of(j * T, T), T) for j in keys]
            ks, vs = [k_ref[r, :] for r in rows], [v_ref[r, :] for r in rows]
            kms = [[jnp.where(hm, kj, jnp.zeros_like(kj)) for hm in masks] for kj in ks]
            units = [(n, h) for n in range(len(pairs)) for h in range(2)]
            qks = {(n, h): _nt(qms[pairs[n][0], h], ks[pairs[n][1]]) for n, h in units}
            das = {(n, h): _nt(doms[pairs[n][0], h], vs[pairs[n][1]]) for n, h in units}
            lbs, l1ms = {}, {}
            for u in units:
                lbs[u], l1m = _log_terms(qks[u])
                l1ms[u] = jnp.where(causal, l1m, 0.0) if pairs[u[0]][2] else l1m
            pins = {u: _hilo_dot(l1ms[u], upto) for u in units}
            sums = dict(sums)
            a_s, dls, cps = {}, {}, {}
            for n, h in units:
                b, _, diag = pairs[n]
                cl, cp = sums[b, h]
                a = jnp.exp2(lbs[n, h] + (ctots[b, h] - cl) - pins[n, h])
                if diag:
                    a = jnp.where(causal, a, 0.0)
                a_s[n, h] = a.astype(BF16)
                dls[n, h] = das[n, h] * a
                cps[n, h] = cp
                sums[b, h] = (cl + jnp.sum(l1ms[n, h], axis=1, keepdims=True),
                              cp + jnp.sum(dls[n, h], axis=1, keepdims=True))
            pexs = {u: _hilo_dot(dls[u], before) for u in units}
            dzbs = {}
            for u in units:
                dz = dls[u] - jnp.exp2(lbs[u]) * (dls[u] + pexs[u] + cps[u])
                if pairs[u[0]][2]:
                    dz = jnp.where(causal, dz, 0.0)
                dzbs[u] = dz.astype(BF16)
            dqs = list(dqs)
            for n, h in units:
                dqs[pairs[n][0]] = dqs[pairs[n][0]] + _nn(dzbs[n, h], kms[pairs[n][1]][h])
            for key, r in enumerate(rows):
                mine = [(n, h) for n, h in units if pairs[n][1] == key]
                dk_acc[r, :] += functools.reduce(jnp.add, [_tn(dzbs[u], qms[pairs[u[0]][0], u[1]]) for u in mine])
                dv_acc[r, :] += functools.reduce(jnp.add, [_tn(a_s[u], doms[pairs[u[0]][0], u[1]]) for u in mine])
            return sums, tuple(dqs)

        zero = jnp.zeros((T, 1), F32)
        sums = {(b, h): (zero, zero) for b in range(QB) for h in range(2)}
        dqs = tuple(jnp.zeros((T, 2 * DH), F32) for _ in range(QB))
        sums, dqs = lax.fori_loop(
            0, i2 // 2,
            lambda t, c: blocks([2 * t, 2 * t + 1],
                                [(0, 0, False), (1, 0, False), (0, 1, False), (1, 1, False)], c[0], c[1]),
            (sums, dqs))
        _, dqs = blocks([i2, i2 + 1], [(0, 0, True), (1, 0, False), (1, 1, True)], sums, dqs)
        for b in range(QB):
            dq_ref[b * T:(b + 1) * T, :] = (dqs[b] * SCALE).astype(BF16)

        @pl.when(step == nq - 1)
        def _():
            dk_ref[...] = (dk_acc[...] * SCALE).astype(BF16)
            dv_ref[...] = dv_acc[...].astype(BF16)

    blk = pl.BlockSpec((QB * T, 2 * DH), lambda p, i: (i, p))
    full = pl.BlockSpec((S, 2 * DH), lambda p, i: (0, p))
    return _call(
        body, (q, k, v, do, ctot), name="attn_bwd", grid=(SBW // (2 * DH), nq),
        in_specs=[blk, full, full, blk, blk], out_specs=[blk, full, full],
        out_shape=[_sds((S, SBW), BF16), _sds((S, SBW), BF16), _sds((S, SBW), BF16)],
        scratch_shapes=[pltpu.VMEM((S, 2 * DH), F32), pltpu.VMEM((S, 2 * DH), F32)],
        compiler_params=_params(("arbitrary", "arbitrary"), 40), after=after)


def _pool_counts(first_row, tm):
    pos = first_row + lax.broadcasted_iota(jnp.int32, (tm, 1), 0)
    return [jnp.minimum(pos + 1, w).astype(F32) for w in POOL_WINDOWS]


def _mix_out(h, xp, o_sb, gp, gs, w_group, scale, w_bp, w_ba, w_out, exchange=None):
    tm = 512

    def body(h_ref, xp_ref, o_ref, gp_ref, gs_ref, wg_hbm, sc_ref, wbp_hbm, wba_hbm, wo_hbm,
             h2_ref, pm_ref, p_ref, yp_ref, ys_ref, m_ref, halo, wg_ref, wbp_ref, wba_ref, wo_ref):
        _stage([(wg_hbm, wg_ref), (wbp_hbm, wbp_ref), (wba_hbm, wba_ref), (wo_hbm, wo_ref)])
        i = pl.program_id(0)

        @pl.when(i == 0)
        def _():
            halo[...] = jnp.zeros_like(halo)

        xp = xp_ref[...]
        ext = jnp.concatenate([halo[...], xp], axis=0)
        halo[...] = xp[tm - HALO:, :]
        counts = _pool_counts(i * tm, tm)
        for gi in range(len(POOL_WINDOWS)):
            lanes = slice(gi * PG, (gi + 1) * PG)
            win = ext[:, lanes]
            for step in range(gi + 1):
                win = win + pltpu.roll(win, 1 << step, 0)
            pm = (win[HALO:, :] / counts[gi] - xp[:, lanes]).astype(BF16)
            pm_ref[:, lanes] = pm
            p_ref[:, lanes] = (_nn(pm, wg_ref[gi]) * sc_ref[:, lanes]).astype(BF16)
        pb = p_ref[...]
        ob = o_ref[...]
        for j in range(NSH):
            cols = slice(j * (D // NSH), (j + 1) * (D // NSH))
            yp = _nn(pb, wbp_ref[j])
            ys = _nn(ob, wba_ref[j])
            yp_ref[:, cols] = yp.astype(BF16)
            ys_ref[:, cols] = ys.astype(BF16)
            m_ref[:, cols] = (gp_ref[:, cols].astype(F32) * yp + gs_ref[:, cols].astype(F32) * ys).astype(BF16)
        h2_ref[...] = h_ref[...] + _nn(m_ref[...], wo_ref[...])

    return _call(
        body, (h, xp, o_sb, gp, gs, w_group, scale, w_bp, w_ba, w_out), name="mix_out", grid=(S // tm,),
        in_specs=[_rows(tm, D), _rows(tm, PW), _rows(tm, SBW), _rows(tm, D), _rows(tm, D),
                  _ANY, _fixed((1, PW)), _ANY, _ANY, _ANY],
        out_specs=[_rows(tm, D), _rows(tm, PW), _rows(tm, PW), _rows(tm, D), _rows(tm, D), _rows(tm, D)],
        out_shape=[_sds((S, D), F32), _sds((S, PW), BF16), _sds((S, PW), BF16), _sds((S, D), BF16),
                   _sds((S, D), BF16), _sds((S, D), BF16)],
        scratch_shapes=[pltpu.VMEM((HALO, PW), F32)] + _vmem_like(w_group, w_bp, w_ba, w_out),
        compiler_params=_params(("arbitrary",), 48), free=(5, 6), exchange=exchange)


def _mix_bwd_out(dh, gp, gs, yp, ys, pm, w_group, scale, w_bp, w_ba, w_out, exchange=None):
    tm = 512
    nt = S // tm

    def body(dh_hbm, gp_ref, gs_ref, yp_ref, ys_ref, pm_ref, wg_hbm, sc_ref, wbp_hbm, wba_hbm, wo_hbm,
             dlg_ref, dyp_ref, dys_ref, do_ref, dyg_ref, dxp_ref, dsc_ref, halo, wg_ref, wbp_ref, wba_ref, wo_ref,
             ring, ring_sem):
        step = pl.program_id(0)

        def fetch(s):
            rows = pl.ds(pl.multiple_of((nt - 1 - s) * tm, tm), tm)
            return pltpu.make_async_copy(dh_hbm.at[rows], ring.at[s % 3], ring_sem.at[s % 3])

        @pl.when(step == 0)
        def _():
            fetch(step).start()
            fetch(step + 1).start()
            halo[...] = jnp.zeros_like(halo)
            dsc_ref[...] = jnp.zeros_like(dsc_ref)

        @pl.when(step + 2 < nt)
        def _():
            fetch(step + 2).start()

        _stage([(wg_hbm, wg_ref), (wbp_hbm, wbp_ref), (wba_hbm, wba_ref), (wo_hbm, wo_ref)])
        fetch(step).wait()
        dm = _nt(ring[step % 3].astype(BF16), wo_ref[...])
        gp = gp_ref[...].astype(F32)
        gs = gs_ref[...].astype(F32)
        yp = yp_ref[...].astype(F32)
        ys = ys_ref[...].astype(F32)
        dlg_ref[:, :D] = (dm * yp * gp * (1.0 - gp)).astype(BF16)
        dlg_ref[:, D:] = (dm * ys * gs * (1.0 - gs)).astype(BF16)
        dyp_ref[...] = (dm * gp).astype(BF16)
        dys_ref[...] = (dm * gs).astype(BF16)
        dp = jnp.zeros((tm, PW), F32)
        do = jnp.zeros((tm, SBW), F32)
        for j in range(NSH):
            cols = slice(j * (D // NSH), (j + 1) * (D // NSH))
            dp = dp + _nt(dyp_ref[:, cols], wbp_ref[j])
            do = do + _nt(dys_ref[:, cols], wba_ref[j])
        do_ref[...] = do.astype(BF16)
        counts = _pool_counts((nt - 1 - step) * tm, tm)
        dscale = []
        for gi in range(len(POOL_WINDOWS)):
            lanes = slice(gi * PG, (gi + 1) * PG)
            dpg = dp[:, lanes]
            dscale.append(jnp.sum(dpg * _nn(pm_ref[:, lanes], wg_ref[gi]), axis=0, keepdims=True))
            dyg = (dpg * sc_ref[:, lanes]).astype(BF16)
            dyg_ref[:, lanes] = dyg
            dpm = _nt(dyg, wg_ref[gi])
            per = dpm / counts[gi]
            win = jnp.concatenate([per, halo[:, lanes]], axis=0)
            halo[:, lanes] = per[:HALO, :]
            for s in range(gi + 1):
                win = win + pltpu.roll(win, tm + HALO - (1 << s), 0)
            dxp_ref[:, lanes] = (win[:tm, :] - dpm).astype(BF16)
        dsc_ref[...] += jnp.concatenate(dscale, axis=1)

    rev = lambda width: pl.BlockSpec((tm, width), lambda i: (nt - 1 - i, 0))
    return _call(
        body, (dh, gp, gs, yp, ys, pm, w_group, scale, w_bp, w_ba, w_out), name="mix_bwd_out", grid=(nt,),
        in_specs=[_ANY, rev(D), rev(D), rev(D), rev(D), rev(PW), _ANY, _fixed((1, PW)), _ANY, _ANY, _ANY],
        out_specs=[rev(2 * D), rev(D), rev(D), rev(SBW), rev(PW), rev(PW), _fixed((1, PW))],
        out_shape=[_sds((S, 2 * D), BF16), _sds((S, D), BF16), _sds((S, D), BF16), _sds((S, SBW), BF16),
                   _sds((S, PW), BF16), _sds((S, PW), BF16), _sds((1, PW), F32)],
        scratch_shapes=[pltpu.VMEM((HALO, PW), F32)] + _vmem_like(w_group, w_bp, w_ba, w_out)
        + [pltpu.VMEM((3, tm, D), F32), pltpu.SemaphoreType.DMA((3,))],
        compiler_params=_params(("arbitrary",), 48), exchange=exchange)


def _mix_bwd_in(dh, h, gain, pieces, w_in, exchange=None):
    tm = 512
    widths = [p.shape[1] for p in pieces]

    def body(dh_ref, h_ref, g_ref, *rest):
        piece_refs, (w_hbm, dx_ref, dg_ref, w_ref, dp_ref) = rest[:len(pieces)], rest[len(pieces):]
        @pl.when(pl.program_id(0) == 0)
        def _():
            dg_ref[...] = jnp.zeros_like(dg_ref)

        def compute(ready):
            at = 0
            for ref, width in zip(piece_refs, widths):
                dp_ref[:, at:at + width] = ref[...]
                at += width
            du = jnp.zeros((tm, D), F32)
            for j in range(NSH):
                ready(j)
                du = du + _nt(dp_ref[:, j * D:(j + 1) * D], w_ref[j])
            r, hr = _rms(h_ref[...])
            dx, dgain = _rms_bwd(du, hr, r, g_ref[...])
            dx_ref[...] = dh_ref[...] + dx
            dg_ref[...] += dgain

        _staged([[(w_hbm.at[j], w_ref.at[j])] for j in range(NSH)], compute)

    return _call(
        body, (dh, h, gain, *pieces, w_in), name="mix_bwd_in", grid=(S // tm,),
        in_specs=[_rows(tm, D), _rows(tm, D), _fixed((1, D))] + [_rows(tm, w) for w in widths] + [_ANY],
        out_specs=[_rows(tm, D), _fixed((1, D))],
        out_shape=[_sds((S, D), F32), _sds((1, D), F32)],
        scratch_shapes=_vmem_like(w_in) + [pltpu.VMEM((tm, 4 * D), BF16)],
        compiler_params=_params(("arbitrary",), 48), exchange=exchange)


def _wgrad_in(u, pieces):
    dxp, dq, dk, dv, dlg = pieces

    def body(u_ref, dxp_ref, dq_ref, dk_ref, dv_ref, dlg_ref, o_ref):
        j = pl.program_id(0)
        u = u_ref[...]

        def two(left_ref, right_ref):
            o_ref[:, :PW] = _tn(u, left_ref[...]).astype(BF16)
            o_ref[:, PW:] = _tn(u, right_ref[...]).astype(BF16)

        pl.when(j == 0)(lambda: two(dxp_ref, dq_ref))
        pl.when(j == 1)(lambda: two(dk_ref, dv_ref))

        @pl.when(j >= 2)
        def _():
            o_ref[...] = _tn(u, dlg_ref[...]).astype(BF16)

    whole = lambda width: pl.BlockSpec((S, width), lambda j: (0, 0))
    return _call(
        body, (u, dxp, dq, dk, dv, dlg), name="wgrad_in", grid=(NSH,),
        in_specs=[whole(D), whole(PW), whole(SBW), whole(SBW), whole(SBW),
                  pl.BlockSpec((S, D), lambda j: (0, jnp.maximum(j - 2, 0)))],
        out_specs=[pl.BlockSpec((None, D, D), lambda j: (j, 0, 0))], out_shape=[_sds((NSH, D, D), BF16)],
        compiler_params=_params(("arbitrary",), 56))[0]


def _wgrad(a, b, nblk, ti, name, out_dtype=BF16, exchange=None, after=()):
    ka, n = a.shape[1], b.shape[1]
    ns = n // nblk

    def body(a_ref, b_ref, o_ref):
        o_ref[...] = _tn(a_ref[...].astype(BF16), b_ref[...].astype(BF16)).astype(out_dtype)

    res = _call(
        body, (a, b), name=name, grid=(nblk, ka // ti),
        in_specs=[pl.BlockSpec((S, ti), lambda j, i: (0, i)), pl.BlockSpec((S, ns), lambda j, i: (0, j))],
        out_specs=[pl.BlockSpec((None, ti, ns), lambda j, i: (j, i, 0))],
        out_shape=[_sds((nblk, ka, ns), out_dtype)],
        compiler_params=_params(("arbitrary", "arbitrary"), 56), exchange=exchange, after=after)
    return res[0] if exchange is None else (res[0][0], res[1])


def _wgrad_ffn(n, dgu, a, dh, name):
    ti, n_gu = D // 2, 2 * NSH

    def body(n_ref, dgu_ref, a_ref, dh_ref, ggu_ref, gd_ref):
        s = pl.program_id(0)

        @pl.when(s < n_gu)
        def _():
            ggu_ref[...] = _tn(n_ref[...], dgu_ref[...]).astype(BF16)

        @pl.when(s >= n_gu)
        def _():
            gd_ref[...] = _tn(a_ref[...], dh_ref[...]).astype(BF16)

    rows = lambda s: jnp.where(s < n_gu, s % 2, 1)
    block = lambda s: jnp.minimum(s // 2, NSH - 1)
    down = lambda s: jnp.maximum(s - n_gu, 0)
    return _call(
        body, (n, dgu, a, dh), name=name, grid=(n_gu + 2,),
        in_specs=[pl.BlockSpec((S, ti), lambda s: (0, rows(s))), pl.BlockSpec((S, FFS), lambda s: (0, block(s))),
                  pl.BlockSpec((S, FFS), lambda s: (0, down(s))), pl.BlockSpec((S, D), lambda s: (0, 0))],
        out_specs=[pl.BlockSpec((None, ti, FFS), lambda s: (block(s), rows(s), 0)),
                   pl.BlockSpec((FFS, D), lambda s: (down(s), 0))],
        out_shape=[_sds((NSH, D, FFS), BF16), _sds((DFF, D), BF16)],
        compiler_params=_params(("arbitrary",), 56))


def _wgrad_branches(p, dyp, o_sb, dys, pm, dyg, mm, dh):
    cols = D // NSH

    def body(p_ref, dyp_ref, o_ref, dys_ref, pm_ref, dyg_ref, mm_ref, dh_ref, gbp_ref, gba_ref, gg_ref, go_ref):
        gbp_ref[...] = _tn(p_ref[...], dyp_ref[...]).astype(BF16)
        gba_ref[...] = _tn(o_ref[...], dys_ref[...]).astype(BF16)
        gg_ref[...] = _tn(pm_ref[...], dyg_ref[...])
        go_ref[...] = _tn(mm_ref[...], dh_ref[...].astype(BF16)).astype(BF16)

    whole = lambda width: pl.BlockSpec((S, width), lambda j: (0, 0))
    col = lambda width: pl.BlockSpec((S, width), lambda j: (0, j))
    return _call(
        body, (p, dyp, o_sb, dys, pm, dyg, mm, dh), name="wgrad_branches", grid=(NSH,),
        in_specs=[whole(PW), col(cols), whole(SBW), col(cols), col(PG), col(PG), whole(D), col(cols)],
        out_specs=[pl.BlockSpec((None, PW, cols), lambda j: (j, 0, 0)),
                   pl.BlockSpec((None, SBW, cols), lambda j: (j, 0, 0)),
                   pl.BlockSpec((None, PG, PG), lambda j: (j, 0, 0)),
                   pl.BlockSpec((D, cols), lambda j: (0, j))],
        out_shape=[_sds((NSH, PW, cols), BF16), _sds((NSH, SBW, cols), BF16), _sds((NSH, PG, PG), F32),
                   _sds((D, D), BF16)],
        compiler_params=_params(("arbitrary",), 40))


def _place():
    x, y, c = lax.axis_index("x"), lax.axis_index("y"), lax.axis_index("c")
    chips = [(1 - x, y), (x, 1 - y), (1 - x, 1 - y)]
    return x, y, c, chips


def _remote(src, dst, ssem, rsem, dev):
    return pltpu.make_async_remote_copy(src_ref=src, dst_ref=dst, send_sem=ssem, recv_sem=rsem,
                                        device_id=dev, device_id_type=MESH)


def _cast_into_block(ws, me_idx, name):
    steps = 4
    shapes = [(w.shape[0] // steps, w.shape[1]) for w in ws]

    def body(me_ref, *refs):
        for w_ref, o_ref in zip(refs[:len(ws)], refs[len(ws):]):
            o_ref[...] = w_ref[...].astype(BF16)

    return pl.pallas_call(
        body, name=name, out_shape=[_sds((NSH,) + w.shape, BF16) for w in ws],
        grid_spec=pltpu.PrefetchScalarGridSpec(
            num_scalar_prefetch=1, grid=(steps,),
            in_specs=[pl.BlockSpec((r, c), lambda s, me: (s, 0)) for r, c in shapes],
            out_specs=[pl.BlockSpec((None, r, c), lambda s, me: (me[0], s, 0)) for r, c in shapes]),
        compiler_params=_params(("arbitrary",), 32),
    )(me_idx, *ws)


def _ex_gather(bufs):
    n = len(bufs)
    per = 8

    def plan(outs, ssem, rsem, w):
        x, y, c, _ = _place()
        sib, nbr_x, nbr_y = (x, y, 1 - c), (1 - x, y, c), (x, 1 - y, c)
        half = outs[w].shape[1] // 2
        quarter = half // 2
        sem = lambda k: (ssem.at[per * w + k], rsem.at[per * w + k])
        rows = lambda blk, start, size: outs[w].at[blk, pl.ds(start, size)]
        mine = rows(2 * x + y, c * half, half)
        from_x = rows(2 * (1 - x) + y, c * half, half)
        from_y = rows(2 * x + (1 - y), c * half, half)
        diag = 2 * (1 - x) + (1 - y)
        pass_y = rows(2 * (1 - x) + y, c * half, quarter)
        pass_x = rows(2 * x + (1 - y), c * half + quarter, quarter)
        diag_0, diag_1 = rows(diag, c * half, quarter), rows(diag, c * half + quarter, quarter)
        first = [_remote(mine, mine, *sem(0), nbr_x), _remote(mine, mine, *sem(1), nbr_y)]
        arrivals = [
            (_remote(from_x, from_x, *sem(0), nbr_x),
             [_remote(pass_y, pass_y, *sem(2), nbr_y), _remote(from_x, from_x, *sem(4), sib)]),
            (_remote(from_y, from_y, *sem(1), nbr_y),
             [_remote(pass_x, pass_x, *sem(3), nbr_x), _remote(from_y, from_y, *sem(5), sib)]),
            (_remote(diag_0, diag_0, *sem(2), nbr_y), [_remote(diag_0, diag_0, *sem(6), sib)]),
            (_remote(diag_1, diag_1, *sem(3), nbr_x), [_remote(diag_1, diag_1, *sem(7), sib)]),
        ]
        other = (1 - c) * half
        from_sibling = [
            _remote(rows(2 * (1 - x) + y, other, half), rows(2 * (1 - x) + y, other, half), *sem(4), sib),
            _remote(rows(2 * x + (1 - y), other, half), rows(2 * x + (1 - y), other, half), *sem(5), sib),
            _remote(rows(diag, other, quarter), rows(diag, other, quarter), *sem(6), sib),
            _remote(rows(diag, other + quarter, quarter), rows(diag, other + quarter, quarter), *sem(7), sib),
        ]
        return first, arrivals, from_sibling

    def start(ins, outs, ssem, rsem):
        x, y, c, _ = _place()
        for w in range(n):
            half = outs[w].shape[1] // 2
            mine = outs[w].at[2 * x + y, pl.ds(c * half, half)]
            _remote(mine, mine, ssem.at[per * w], rsem.at[per * w], (1 - x, y, c)).start()
            _remote(mine, mine, ssem.at[per * w + 1], rsem.at[per * w + 1], (x, 1 - y, c)).start()

    def finish(ins, outs, ssem, rsem):
        plans = [plan(outs, ssem, rsem, w) for w in range(n)]
        started = []
        for direct in (True, False):
            for first, arrivals, _ in plans:
                for arrived, onward in (arrivals[:2] if direct else arrivals[2:]):
                    arrived.wait_recv()
                    for cp in onward:
                        cp.start()
                    started += onward
        for first, _, from_sibling in plans:
            for cp in from_sibling:
                cp.wait_recv()
            started += first
        for cp in started:
            cp.wait_send()

    return Exchange(bufs, [_sds(b.shape, b.dtype) for b in bufs], {w: w for w in range(n)}, per * n, start, finish)


def _ex_gather_direct(bufs):
    n = len(bufs)

    def copies(outs, ssem, rsem, only_first=False):
        x, y, c, chips = _place()
        me, sib = 2 * x + y, (x, y, 1 - c)
        first, relay, last = [], [], []
        for w in range(n):
            half = outs[w].shape[1] // 2
            mine = outs[w].at[me, pl.ds(c * half, half)]
            for k, (px, py) in enumerate(chips):
                sems = (ssem.at[6 * w + k], rsem.at[6 * w + k])
                sib_sems = (ssem.at[6 * w + 3 + k], rsem.at[6 * w + 3 + k])
                first.append(_remote(mine, mine, *sems, (px, py, c)))
                if only_first:
                    continue
                got = outs[w].at[2 * px + py, pl.ds(c * half, half)]
                relay.append((_remote(got, got, *sems, (px, py, c)), _remote(got, got, *sib_sems, sib)))
                theirs = outs[w].at[2 * px + py, pl.ds((1 - c) * half, half)]
                last.append(_remote(theirs, theirs, *sib_sems, sib))
        return first, relay, last

    def start(ins, outs, ssem, rsem):
        for cp in copies(outs, ssem, rsem, only_first=True)[0]:
            cp.start()

    def finish(ins, outs, ssem, rsem):
        first, relay, last = copies(outs, ssem, rsem)
        for arrived, onward in relay:
            arrived.wait_recv()
            onward.start()
        for cp in last:
            cp.wait_recv()
        for cp in first:
            cp.wait_send()
        for _, onward in relay:
            onward.wait_send()

    return Exchange(bufs, [_sds(b.shape, b.dtype) for b in bufs], {w: w for w in range(n)}, 6 * n, start, finish)


def _simple_exchange(arrays, landing, aliases, make_copies, sibling_only=False):
    def start(ins, outs, ssem, rsem):
        for cp, _ in make_copies(ins, outs, ssem, rsem, False):
            cp.start()

    def finish(ins, outs, ssem, rsem):
        cps = make_copies(ins, outs, ssem, rsem, True)
        for _, landed in cps:
            landed.wait_recv()
        for cp, _ in cps:
            cp.wait_send()

    return Exchange(arrays, landing, aliases, len(arrays) * 3, start, finish, sibling_only)


def _ex_pair_swap(grads):
    def make(ins, outs, ssem, rsem, landing):
        x, y, c, _ = _place()
        cps = [_remote(ins[w].at[:, 1 - c], outs[w], ssem.at[w], rsem.at[w], (x, y, 1 - c))
               for w in range(len(grads))]
        return [(cp, cp) for cp in cps]

    return _simple_exchange(grads, [_sds((NSH,) + g.shape[2:], g.dtype) for g in grads], {}, make, True)


def _ex_relay(bufs):
    def make(ins, outs, ssem, rsem, landing):
        x, y, c, chips = _place()
        sib = (x, y, 1 - c)
        out = []
        for w in range(len(bufs)):
            half = outs[w].shape[1] // 2
            for k, (px, py) in enumerate(chips):
                sems = (ssem.at[3 * w + k], rsem.at[3 * w + k])
                have = outs[w].at[2 * px + py, pl.ds(c * half, half)]
                miss = outs[w].at[2 * px + py, pl.ds((1 - c) * half, half)]
                out.append((_remote(have, have, *sems, sib), _remote(miss, miss, *sems, sib) if landing else None))
        return out

    return _simple_exchange(bufs, [_sds(b.shape, b.dtype) for b in bufs], {w: w for w in range(len(bufs))}, make, True)


def _ex_share(bufs):
    def make(ins, outs, ssem, rsem, landing):
        x, y, c, _ = _place()
        sib = (x, y, 1 - c)
        return [(_remote(outs[w].at[c], outs[w].at[c], ssem.at[w], rsem.at[w], sib),
                 _remote(outs[w].at[1 - c], outs[w].at[1 - c], ssem.at[w], rsem.at[w], sib) if landing else None)
                for w in range(len(bufs))]

    return _simple_exchange(bufs, [_sds(b.shape, b.dtype) for b in bufs], {w: w for w in range(len(bufs))}, make, True)


def _small_copies(slots, ssems, rsems, sending):
    x, y, c, _ = _place()
    out = []
    for m in range(1, 8):
        px, py, pc = x ^ (m >> 2), y ^ ((m >> 1) & 1), c ^ (m & 1)
        slot = slots.at[4 * x + 2 * y + c if sending else 4 * px + 2 * py + pc]
        out.append(_remote(slot, slot, ssems[m - 1], rsems[m - 1], (px, py, pc)))
    return out


def _small_gather_start(slots, name, after=()):
    at = 1 + len(after)

    def body(*refs):
        for cp in _small_copies(refs[0], refs[at:at + 7], refs[at + 7:at + 14], True):
            cp.start()
        refs[-1][...] = jnp.zeros_like(refs[-1])

    outs = pl.pallas_call(
        body, name=name,
        out_shape=([pltpu.SemaphoreType.DMA(())] * 14 + [pltpu.HBM(slots.shape, slots.dtype)]
                   + [jax.ShapeDtypeStruct((8, 128), F32)]),
        in_specs=[_HBM] + [_ANY] * len(after), out_specs=[_SEM] * 14 + [_HBM, _VM], input_output_aliases={0: 14},
        compiler_params=pltpu.CompilerParams(has_side_effects=_EFFECT),
    )(*_in_hbm([slots]), *after)
    return outs[:14], outs[14], outs[15]


def _small_gather_wait(sems, slots, after, name):
    def body(*refs):
        for cp in _small_copies(refs[0], refs[1:8], refs[8:15], True):
            cp.wait_send()
        for cp in _small_copies(refs[0], refs[1:8], refs[8:15], False):
            cp.wait_recv()

    return pl.pallas_call(
        body, name=name, out_shape=pltpu.HBM(slots.shape, slots.dtype),
        in_specs=[_HBM] + [_SEM] * 14 + [_ANY] * len(after), out_specs=_HBM, input_output_aliases={0: 0},
        compiler_params=pltpu.CompilerParams(has_side_effects=_EFFECT),
    )(slots, *sems, *after)


def _pair_sum(grads, gots, c_idx, name):
    n = len(grads)

    def body(c_ref, *refs):
        for a_ref, b_ref, o_ref in zip(refs[:n], refs[n:2 * n], refs[2 * n:]):
            o_ref[...] = (a_ref[...].astype(F32) + b_ref[...].astype(F32)).astype(BF16)

    halves = [g.shape[2:] for g in grads]
    return list(pl.pallas_call(
        body, name=name, out_shape=[_sds((NSH,) + h, BF16) for h in halves],
        grid_spec=pltpu.PrefetchScalarGridSpec(
            num_scalar_prefetch=1, grid=(NSH,),
            in_specs=[pl.BlockSpec((None, None) + h, lambda j, c: (j, c[0], 0, 0)) for h in halves]
            + [pl.BlockSpec((None,) + h, lambda j, c: (j, 0, 0)) for h in halves],
            out_specs=[pl.BlockSpec((None,) + h, lambda j, c: (j, 0, 0)) for h in halves]),
        compiler_params=_params(("arbitrary",), 40),
    )(c_idx, *_in_hbm(list(grads) + list(gots))))


def _chip_sum(owns, gots, place, name):
    n = len(owns)

    def body(place_ref, *refs):
        for own_ref, got_ref, o_ref in zip(refs[:n], refs[n:2 * n], refs[2 * n:]):
            acc = own_ref[...].astype(F32)
            for k in range(3):
                acc = acc + got_ref[k].astype(F32)
            o_ref[...] = acc

    shapes = [(o.shape[1] // 2, o.shape[2]) for o in owns]
    return list(pl.pallas_call(
        body, name=name, out_shape=[_sds((2, 2 * r, c), F32) for r, c in shapes],
        grid_spec=pltpu.PrefetchScalarGridSpec(
            num_scalar_prefetch=1, grid=(2,),
            in_specs=[pl.BlockSpec((None, r, c), lambda s, p: (p[0], s, 0)) for r, c in shapes]
            + [pl.BlockSpec((3, r, c), lambda s, p: (0, s, 0)) for r, c in shapes],
            out_specs=[pl.BlockSpec((None, r, c), lambda s, p: (p[1], s, 0)) for r, c in shapes]),
        compiler_params=_params(("arbitrary",), 40),
    )(place, *_in_hbm(list(owns) + list(gots))))


def _adamw_math(w, g, m, v):
    m = B1 * m + (1.0 - B1) * g
    v = B2 * v + (1.0 - B2) * (g * g)
    m_hat = m / (1.0 - B1 ** STEP)
    v_hat = v / (1.0 - B2 ** STEP)
    return -LR * (m_hat / (jnp.sqrt(v_hat) + AEPS) + WD * w), m, v


def _adamw(ws, gs, ms, vs, name, after=()):
    n, steps = len(ws), 4

    def body(*refs):
        ins, outs = refs[:4 * n], refs[4 * n:]
        for i in range(n):
            w_ref, g_ref, m_ref, v_ref = ins[4 * i:4 * i + 4]
            go_ref, d_ref, nm_ref, nv_ref = outs[4 * i:4 * i + 4]
            g = g_ref[...]
            go_ref[...] = g
            d_ref[...], nm_ref[...], nv_ref[...] = _adamw_math(w_ref[...], g, m_ref[...], v_ref[...])

    args, specs, shapes, free = [], [], [], []
    for i, (w, g, m, v) in enumerate(zip(ws, gs, ms, vs)):
        args += [w, g, m, v]
        specs += [pl.BlockSpec((w.shape[0] // steps, w.shape[1]), lambda r: (r, 0))] * 4
        shapes += [_sds(w.shape, F32)] * 4
        free += [4 * i, 4 * i + 2, 4 * i + 3]
    outs = _call(body, args, name=name, grid=(steps,), out_shape=shapes, in_specs=specs, out_specs=specs,
                 compiler_params=_params(("arbitrary",), 48), free=tuple(free), after=after)
    return [outs[4 * i:4 * i + 4] for i in range(n)]


def _small_update(gathered, w, m, v, entries):
    rows = w.shape[0]

    def body(ga_ref, w_ref, m_ref, v_ref, *out_refs):
        for j, (first, n) in enumerate(entries):
            mine = slice(first, first + n)
            g = ga_ref[mine, :]
            for dev in range(1, 8):
                g = g + ga_ref[dev * rows + first:dev * rows + first + n, :]
            results = (g,) + _adamw_math(w_ref[mine, :], g, m_ref[mine, :], v_ref[mine, :])
            for i, res in enumerate(results):
                out_refs[i * len(entries) + j][...] = res

    outs = pl.pallas_call(
        body, name="small_update",
        out_shape=[jax.ShapeDtypeStruct((n, 128), F32) for _ in range(4) for _, n in entries],
        in_specs=[_VM] * 4, out_specs=[_VM] * (4 * len(entries)),
    )(gathered, w, m, v)
    return [outs[i * len(entries):(i + 1) * len(entries)] for i in range(4)]


SMALL = ("ffn1_norm", "mix_norm", "ffn2_norm", "final_norm", "pool_scale", "loss", "pool_w_group")
BIG = ("ffn1_w_gate_up", "ffn1_w_down", "w_in", "w_branch_pool", "w_branch_attn", "w_out",
       "ffn2_w_gate_up", "ffn2_w_down")
ORDER = ("ffn1_norm", "ffn1_w_gate_up", "ffn1_w_down", "mix_norm", "w_in", "pool_w_group", "pool_scale",
         "w_branch_pool", "w_branch_attn", "w_out", "ffn2_norm", "ffn2_w_gate_up", "ffn2_w_down", "final_norm")
SMALL_ROWS = 560


def _pack_small(t):
    parts = []
    for k in SMALL:
        rows = t[k].reshape(-1, 128) if k in t else jnp.zeros((1, 128), F32)
        parts.append(jnp.pad(rows, ((0, -rows.shape[0] % 8), (0, 0))))
    packed = jnp.concatenate(parts, axis=0)
    assert packed.shape == (SMALL_ROWS, 128), packed.shape
    return packed


def _small_entries(like):
    out, at = [], 0
    for k in SMALL:
        n = like[k].size // 128 if k in like else 1
        out.append((at, n))
        at += n + (-n % 8)
    return out


def _halves(g):
    return g.reshape(NSH, 2, g.shape[1] // 2, g.shape[2])


def kernel(x, ffn1_norm, ffn1_w_gate_up, ffn1_w_down, mix_norm, w_in, pool_w_group, pool_scale, w_branch_pool, w_branch_attn, w_out, ffn2_norm, ffn2_w_gate_up, ffn2_w_down, final_norm, loss_target, m_ffn1_norm, m_ffn1_w_gate_up, m_ffn1_w_down, m_mix_norm, m_w_in, m_pool_w_group, m_pool_scale, m_w_branch_pool, m_w_branch_attn, m_w_out, m_ffn2_norm, m_ffn2_w_gate_up, m_ffn2_w_down, m_final_norm, v_ffn1_norm, v_ffn1_w_gate_up, v_ffn1_w_down, v_mix_norm, v_w_in, v_pool_w_group, v_pool_scale, v_w_branch_pool, v_w_branch_attn, v_w_out, v_ffn2_norm, v_ffn2_w_gate_up, v_ffn2_w_down, v_final_norm):
    wts = dict(ffn1_norm=ffn1_norm, ffn1_w_gate_up=ffn1_w_gate_up, ffn1_w_down=ffn1_w_down, mix_norm=mix_norm,
               w_in=w_in, pool_w_group=pool_w_group, pool_scale=pool_scale, w_branch_pool=w_branch_pool,
               w_branch_attn=w_branch_attn, w_out=w_out, ffn2_norm=ffn2_norm, ffn2_w_gate_up=ffn2_w_gate_up,
               ffn2_w_down=ffn2_w_down, final_norm=final_norm)
    mom = dict(ffn1_norm=m_ffn1_norm, ffn1_w_gate_up=m_ffn1_w_gate_up, ffn1_w_down=m_ffn1_w_down,
               mix_norm=m_mix_norm, w_in=m_w_in, pool_w_group=m_pool_w_group, pool_scale=m_pool_scale,
               w_branch_pool=m_w_branch_pool, w_branch_attn=m_w_branch_attn, w_out=m_w_out,
               ffn2_norm=m_ffn2_norm, ffn2_w_gate_up=m_ffn2_w_gate_up, ffn2_w_down=m_ffn2_w_down,
               final_norm=m_final_norm)
    var = dict(ffn1_norm=v_ffn1_norm, ffn1_w_gate_up=v_ffn1_w_gate_up, ffn1_w_down=v_ffn1_w_down,
               mix_norm=v_mix_norm, w_in=v_w_in, pool_w_group=v_pool_w_group, pool_scale=v_pool_scale,
               w_branch_pool=v_w_branch_pool, w_branch_attn=v_w_branch_attn, w_out=v_w_out,
               ffn2_norm=v_ffn2_norm, ffn2_w_gate_up=v_ffn2_w_gate_up, ffn2_w_down=v_ffn2_w_down,
               final_norm=v_final_norm)

    c_idx = lax.axis_index("c").astype(jnp.int32).reshape(1)
    me_idx = (2 * lax.axis_index("x") + lax.axis_index("y")).astype(jnp.int32).reshape(1)
    place = jnp.concatenate([me_idx, c_idx])
    x0, tgt = x[0], loss_target[0]
    wgrp = pool_w_group[0].astype(BF16)
    g1, gm, g2, gf = ffn1_norm, mix_norm, ffn2_norm, final_norm.reshape(1, D)
    grad, delta, new_m, new_v = {}, {}, {}, {}

    def pair_sums(keys, parts, got):
        return _pair_sum(parts, got, c_idx, "pair_sum_" + keys[0])

    def chip_sums(keys, chip_parts, owned):
        return _chip_sum(chip_parts, owned, place, "chip_sum_" + keys[0])

    def adamw(keys, after=()):
        outs = _adamw([wts[k][0] for k in keys], [grad[k][0] for k in keys], [mom[k][0] for k in keys],
                      [var[k][0] for k in keys], "adamw_" + keys[0], after=after)
        for k, res in zip(keys, outs):
            grad[k], delta[k], new_m[k], new_v[k] = (o.reshape(wts[k].shape) for o in res)

    first, late = ("ffn1_w_gate_up", "ffn1_w_down"), ("w_branch_pool", "w_branch_attn", "w_out",
                                                       "ffn2_w_gate_up", "ffn2_w_down")
    own = {}
    for group in (first, ("w_in",), late):
        own.update(zip(group, _cast_into_block([wts[k][0] for k in group], me_idx, "cast_" + group[0])))
    full = dict(zip(first, _exchange_alone(_ex_gather([own[k] for k in first]), "gather_ffn1")))
    wgu1, wd1 = full["ffn1_w_gate_up"], full["ffn1_w_down"].reshape(DFF, D)
    (h1, n1, gu1, a1), (win,) = _ffn_fwd(x0, g1, wgu1, wd1, "ffn1_fwd", exchange=_ex_gather_direct([own["w_in"]]))
    sems_l, thru_l, token_l = _gather_start([own[k_] for k_ in late], [h1], "gather_late_start")
    u, xp, q, k, v, gp, gs = _mix_in(h1, gm, win, after=(token_l,))
    o_sb, ctot = _attn_fwd(q, k, v)
    arrived = _gather_wait(sems_l, thru_l, [o_sb], "gather_late_wait")
    wbp, wba, wout = _exchange_alone(_ex_relay(arrived[:3]), "relay_mix")
    wout = wout.reshape(D, D)
    (h2, pm, p, yp, ys, mm), (wgu2, wd2) = _mix_out(h1, xp, o_sb, gp, gs, wgrp, pool_scale, wbp, wba, wout,
                                                    exchange=_ex_relay(arrived[3:]))
    wd2 = wd2.reshape(DFF, D)
    dh2, dgu3, d_g2, loss_row, d_gf, n3, a3, dh3 = _ffn_last(h2, g2, wgu2, wd2, tgt, gf, "ffn2")

    def grad_down(a, dh, name, exchange=None):
        res = _wgrad(a, dh, 1, FFS, name, exchange=exchange)
        halves = lambda g: [_halves(g.reshape(NSH, DFF // NSH, D))]
        return halves(res) if exchange is None else (halves(res[0]), res[1])

    k_gu2, k_d2, k_gu1, k_d1, k_in = (("ffn2_w_gate_up",), ("ffn2_w_down",), ("ffn1_w_gate_up",),
                                      ("ffn1_w_down",), ("w_in",))
    g_gu2, g_d2 = _wgrad_ffn(n3, dgu3, a3, dh3, "wgrad_ffn2")
    pa = [_halves(g_gu2), _halves(g_d2.reshape(NSH, DFF // NSH, D))]
    (dlg, dyp, dys, do_sb, dyg, dxp, d_scale), got_a = _mix_bwd_out(
        dh2, gp, gs, yp, ys, pm, wgrp, pool_scale, wbp, wba, wout, exchange=_ex_pair_swap(pa))
    chip_a = pair_sums(k_gu2 + k_d2, pa, got_a)
    kb = ("w_out", "w_branch_pool", "w_branch_attn")
    g_bp, g_ba, d_group, g_out = _wgrad_branches(p, dyp, o_sb, dys, pm, dyg, mm, dh2)
    pb = [_halves(g_out.reshape(NSH, D // NSH, D)), _halves(g_bp), _halves(g_ba)]
    k_a, k_in = k_gu2 + k_d2, k_in + kb
    sems_a, thru_a, token_a = _scatter_start(chip_a, "scatter_a_start")
    dq, dk, dv = _attn_bwd(q, k, v, do_sb, ctot, after=(token_a,))
    chip_a, owned_a = _scatter_wait(sems_a, thru_a, [dq], "scatter_a_wait")
    halves_a = chip_sums(k_a, chip_a, owned_a)
    dproj = (dxp, dq, dk, dv, dlg)
    (dh1, d_gm), both_a = _mix_bwd_in(dh2, h1, gm, dproj, win, exchange=_ex_share(halves_a))
    for i, k_ in enumerate(k_a):
        grad[k_] = both_a[i].reshape(wts[k_].shape)

    p_in = [_halves(_wgrad_in(u, dproj))] + pb
    p_d1, got_in = grad_down(a1, dh1, "wgrad_d1", exchange=_ex_pair_swap(p_in))
    sems_in, thru_in, token_in = _scatter_start(pair_sums(k_in, p_in, got_in), "scatter_in_start")
    dgu1, got_d1 = _ffn_bwd_act(dh1, gu1, wd1, "ffn1_bwd_act", exchange=_ex_pair_swap(p_d1), after=(token_in,))
    sems_d1, thru_d1, token_d1 = _scatter_start(pair_sums(k_d1, p_d1, got_d1), "scatter_d1_start")
    p_gu1 = [_halves(_wgrad(n1, dgu1, NSH, D, "wgrad_gu1", after=(token_in, token_d1)))]
    sems_w, thru_w, token_w = _swap_start(p_gu1, "swap_gu1_start")
    chip_in, owned_in = _scatter_wait(sems_in, thru_in, [token_w], "scatter_in_wait")
    chip_d1, owned_d1 = _scatter_wait(sems_d1, thru_d1, [token_w], "scatter_d1_wait")
    halves_in = chip_sums(k_in, chip_in, owned_in)
    p_gu1, got_gu1 = _swap_wait(sems_w, thru_w, halves_in, "swap_gu1_wait")
    sems, thru, token = _scatter_start(pair_sums(k_gu1, p_gu1, got_gu1), "scatter_gu1_start")
    sems_h, thru_h, token_h = _share_start(halves_in, [token], "share_in_start")
    adamw(k_a, after=(token_h,))
    landed = _share_wait(sems_h, thru_h, [delta[k_a[0]]], "share_in_wait")
    for i, k_ in enumerate(k_in):
        grad[k_] = landed[i].reshape(wts[k_].shape)
    adamw(k_in)
    dx, d_g1 = _ffn_bwd_in(dh1, x0, g1, dgu1, wgu1, "ffn1_bwd_in", after=(token,))
    small_g = dict(ffn1_norm=d_g1, mix_norm=d_gm, ffn2_norm=d_g2, final_norm=d_gf, pool_scale=d_scale,
                   pool_w_group=d_group, loss=loss_row)
    dev = 4 * lax.axis_index("x") + 2 * lax.axis_index("y") + lax.axis_index("c")
    slots = lax.dynamic_update_slice(jnp.zeros((8, SMALL_ROWS, 128), F32), _pack_small(small_g)[None], (dev, 0, 0))
    chip_gu1, owned_gu1 = _scatter_wait(sems, thru, [dx] + [delta[k_] for k_ in k_a + k_in], "scatter_gu1_wait")
    halves_last = chip_sums(k_d1 + k_gu1, chip_d1 + chip_gu1, owned_d1 + owned_gu1)
    sems_l, thru_l, token_l = _share_start(halves_last, [], "share_last_start")
    sems_s, slots, token_s = _small_gather_start(slots, "small_gather_start", after=(token_l,))
    both = _share_wait(sems_l, thru_l, [token_s], "share_last_wait")
    grad["ffn1_w_down"] = both[0].reshape(ffn1_w_down.shape)
    grad["ffn1_w_gate_up"] = both[1].reshape(ffn1_w_gate_up.shape)
    adamw(k_d1 + k_gu1, after=(token_s,))
    gathered = _small_gather_wait(sems_s, slots, [delta[k_] for k_ in k_d1 + k_gu1], "small_gather_wait")
    gathered = gathered.reshape(8 * SMALL_ROWS, 128)
    results = _small_update(gathered, _pack_small(wts), _pack_small(mom), _pack_small(var), _small_entries(wts))
    for dst, entries in zip((grad, delta, new_m, new_v), results):
        for k_, rows in zip(SMALL, entries):
            if k_ in wts:
                dst[k_] = rows.reshape(wts[k_].shape)
            elif dst is grad:
                loss = rows[0, 0]
    return (loss, dx[None], *[grad[k_] for k_ in ORDER], *[delta[k_] for k_ in ORDER],
            *[new_m[k_] for k_ in ORDER], *[new_v[k_] for k_ in ORDER])
```

```python
import dataclasses
import functools

import jax
import jax.numpy as jnp
from jax import lax
from jax.experimental import pallas as pl
from jax.experimental.pallas import tpu as pltpu

F32 = jnp.float32
BF16 = jnp.bfloat16

S = 2048
D = 1024
DFF = 2816
FFS = 2 * DFF // 4
NSH = 4
PW = 512
PG = 128
POOL_WINDOWS = (2, 4, 8, 16)
HALO = 16
SBW = 512
DH = 64
EPS = 1e-6
SCALE = 0.125
LOG2E = 1.4426950408889634
TA = 256
QB = 2
MIB = 1024 * 1024

LR, B1, B2, AEPS, WD, STEP = 0.001, 0.9, 0.999, 1e-08, 0.01, 10

_VM = pl.BlockSpec(memory_space=pltpu.VMEM)
_ANY = pl.BlockSpec(memory_space=pl.ANY)
MESH = pl.DeviceIdType.MESH
SIBLING_PAIR_ID = 1


def _nn(a, b):
    return jnp.dot(a, b, preferred_element_type=F32)


def _nt(a, b):
    return lax.dot_general(a, b, (((1,), (1,)), ((), ())), preferred_element_type=F32)


def _tn(a, b):
    return lax.dot_general(a, b, (((0,), (0,)), ((), ())), preferred_element_type=F32)


def _params(sem, vmem_mib):
    return pltpu.CompilerParams(dimension_semantics=sem, vmem_limit_bytes=vmem_mib * MIB)


def _rows(tm, width):
    return pl.BlockSpec((tm, width), lambda i: (i, 0))


def _fixed(shape):
    return pl.BlockSpec(shape, lambda *_: (0,) * len(shape))


def _sds(shape, dtype):
    return pltpu.HBM(shape, dtype)


def _in_hbm(args):
    return [pltpu.with_memory_space_constraint(a, pltpu.HBM) for a in args]


def _stage(pairs):
    pieces = 4

    def copy_all(sems):
        copies = []
        for src, dst in pairs:
            step = src.shape[0] // pieces
            for p in range(pieces):
                part = pl.ds(p * step, step)
                if len(dst.shape) == len(src.shape):
                    piece = (src.at[part], dst.at[part])
                else:
                    piece = (src.at[p], dst.at[:, pl.ds(p * src.shape[2], src.shape[2])])
                copies.append(pltpu.make_async_copy(*piece, sems.at[len(copies)]))
        for c in copies:
            c.start()
        for c in copies:
            c.wait()

    @pl.when(pl.program_id(0) == 0)
    def _():
        pl.run_scoped(copy_all, pltpu.SemaphoreType.DMA((pieces * len(pairs),)))


def _vmem_like(*arrays):
    return [pltpu.VMEM(a.shape, a.dtype) for a in arrays]


def _vmem_wide(w):
    return pltpu.VMEM((w.shape[1], w.shape[0] * w.shape[2]), w.dtype)


FF_CHUNKS = ((0, 1536), (1536, DFF - 1536))


def _wide_columns(src, dst, c0, cn):
    width, out = src.shape[2], []
    for p in range(src.shape[0]):
        lo, hi = max(c0, p * width), min(c0 + cn, (p + 1) * width)
        if lo < hi:
            out.append((src.at[p, :, pl.ds(lo - p * width, hi - lo)], dst.at[:, pl.ds(lo, hi - lo)]))
    return out


def _staged(groups, compute):
    first = pl.program_id(0) == 0

    def with_copies(sems):
        copies = []
        for group in groups:
            base = sum(len(g) for g in copies)
            copies.append([pltpu.make_async_copy(s, d, sems.at[base + i]) for i, (s, d) in enumerate(group)])
        for group in copies:
            for c in group:
                c.start()

        def ready(k):
            for c in copies[k]:
                c.wait()

        compute(ready)

    @pl.when(first)
    def _():
        pl.run_scoped(with_copies, pltpu.SemaphoreType.DMA((sum(len(g) for g in groups),)))

    @pl.when(jnp.logical_not(first))
    def _():
        compute(lambda k: None)


class Exchange:
    def __init__(self, arrays, landing, aliases, n_sems, start, finish, sibling_only=False):
        self.arrays, self.landing, self.aliases, self.n_sems = list(arrays), list(landing), dict(aliases), n_sems
        self.start, self.finish = start, finish
        self.sibling_only = sibling_only

    def enter(self):
        if self.sibling_only:
            barrier = pltpu.get_barrier_semaphore()
            sibling = (lax.axis_index("x"), lax.axis_index("y"), 1 - lax.axis_index("c"))
            pl.semaphore_signal(barrier, inc=1, device_id=sibling, device_id_type=MESH)
            pl.semaphore_wait(barrier, 1)

    def params(self, compiler_params=None):
        kw = dict(collective_id=SIBLING_PAIR_ID) if self.sibling_only else {}
        if compiler_params is None:
            return pltpu.CompilerParams(**kw)
        return dataclasses.replace(compiler_params, **kw)


def _call(body, args, *, name, grid, in_specs, out_specs, out_shape, scratch_shapes=(), compiler_params=None,
          exchange=None, free=(), after=()):
    args = [a if i in free else pltpu.with_memory_space_constraint(a, pltpu.HBM) for i, a in enumerate(args)]
    if exchange is None:
        n_in = len(in_specs)

        def plain(*refs):
            body(*refs[:n_in], *refs[n_in + len(after):])

        return pl.pallas_call(plain, name=name, grid=grid, in_specs=list(in_specs) + [_ANY] * len(after),
                              out_specs=out_specs, out_shape=out_shape, scratch_shapes=list(scratch_shapes),
                              compiler_params=compiler_params)(*args, *after)
    ex = exchange
    n_in, n_out, n_scr = len(in_specs), len(out_specs), len(scratch_shapes)
    na, nl = len(ex.arrays), len(ex.landing)

    def hosted(*refs):
        at = [0]

        def take(n):
            at[0] += n
            return refs[at[0] - n:at[0]]

        k_in, _, e_in, k_out, e_out, k_scr = take(n_in), take(len(after)), take(na), take(n_out), take(nl), take(n_scr)
        ssem, rsem = take(2)
        ids = [pl.program_id(a) for a in range(len(grid))]
        first = functools.reduce(jnp.logical_and, [i == 0 for i in ids])
        last = functools.reduce(jnp.logical_and, [i == g - 1 for i, g in zip(ids, grid)])

        @pl.when(first)
        def _():
            ex.enter()
            ex.start(e_in, e_out, ssem, rsem)

        body(*k_in, *k_out, *k_scr)

        @pl.when(last)
        def _():
            ex.finish(e_in, e_out, ssem, rsem)

    outs = pl.pallas_call(
        hosted, name=name, grid=grid,
        in_specs=list(in_specs) + [_ANY] * (len(after) + na), out_specs=list(out_specs) + [_ANY] * nl,
        out_shape=list(out_shape) + ex.landing,
        scratch_shapes=list(scratch_shapes) + [pltpu.SemaphoreType.DMA((ex.n_sems,))] * 2,
        input_output_aliases={n_in + len(after) + i: n_out + j for i, j in ex.aliases.items()},
        compiler_params=ex.params(compiler_params),
    )(*args, *after, *_in_hbm(ex.arrays))
    return outs[:n_out], outs[n_out:]


def _exchange_alone(ex, name, after=()):
    na, nl = len(ex.arrays), len(ex.landing)

    def body(*refs):
        outs = refs[na + len(after):na + len(after) + nl]
        ex.enter()
        ex.start(refs[:na], outs, refs[-2], refs[-1])
        ex.finish(refs[:na], outs, refs[-2], refs[-1])

    return pl.pallas_call(
        body, name=name, in_specs=[_ANY] * (na + len(after)), out_specs=[_ANY] * nl,
        out_shape=ex.landing, scratch_shapes=[pltpu.SemaphoreType.DMA((ex.n_sems,))] * 2,
        input_output_aliases=ex.aliases, compiler_params=ex.params(),
    )(*_in_hbm(ex.arrays), *after)


_HBM = pl.BlockSpec(memory_space=pltpu.HBM)
_SEM = pl.BlockSpec(memory_space=pltpu.SEMAPHORE)
_EFFECT = pltpu.SideEffectType.DATAFLOW_SIDE_EFFECTING


def _scatter_copies(srcs, lands, ssems, rsems):
    x, y, c, chips = _place()
    return [_remote(srcs[w].at[2 * px + py], lands[w].at[k], ssems[3 * w + k], rsems[3 * w + k], (px, py, c))
            for w in range(len(srcs)) for k, (px, py) in enumerate(chips)]


def _scatter_start(parts, name):
    parts = list(parts)
    n, ncp = len(parts), 3 * len(parts)
    lands = [lax.empty((3,) + p.shape[1:], p.dtype) for p in parts]

    def body(*refs):
        srcs, land_refs = refs[:n], refs[n:2 * n]
        ssems, rsems = refs[2 * n:2 * n + ncp], refs[2 * n + ncp:2 * n + 2 * ncp]
        for cp in _scatter_copies(srcs, land_refs, ssems, rsems):
            cp.start()
        token = refs[-1]
        token[...] = jnp.zeros_like(token)

    outs = pl.pallas_call(
        body, name=name,
        out_shape=([pltpu.SemaphoreType.DMA(())] * (2 * ncp) + [pltpu.HBM(a.shape, a.dtype) for a in parts + lands]
                   + [jax.ShapeDtypeStruct((8, 128), F32)]),
        in_specs=[_HBM] * (2 * n), out_specs=[_SEM] * (2 * ncp) + [_HBM] * (2 * n) + [_VM],
        input_output_aliases={i: 2 * ncp + i for i in range(2 * n)},
        compiler_params=pltpu.CompilerParams(has_side_effects=_EFFECT),
    )(*_in_hbm(parts), *_in_hbm(lands))
    sems, thru, token = outs[:2 * ncp], outs[2 * ncp:2 * ncp + 2 * n], outs[-1]
    return sems, thru, token


def _scatter_wait(sems, thru, after, name):
    n = len(thru) // 2
    ncp = 3 * n

    def body(*refs):
        srcs, land_refs = refs[:n], refs[n:2 * n]
        ssems, rsems = refs[2 * n:2 * n + ncp], refs[2 * n + ncp:2 * n + 2 * ncp]
        for cp in _scatter_copies(srcs, land_refs, ssems, rsems):
            cp.wait_send()
            cp.wait_recv()

    outs = pl.pallas_call(
        body, name=name, out_shape=[pltpu.HBM(a.shape, a.dtype) for a in thru],
        in_specs=[_HBM] * (2 * n) + [_SEM] * (2 * ncp) + [_ANY] * len(after), out_specs=[_HBM] * (2 * n),
        input_output_aliases={i: i for i in range(2 * n)},
        compiler_params=pltpu.CompilerParams(has_side_effects=_EFFECT),
    )(*thru, *sems, *after)
    return outs[:n], outs[n:]


def _swap_copies(srcs, lands, ssems, rsems):
    x, y, c, _ = _place()
    return [_remote(srcs[w].at[:, 1 - c], lands[w], ssems[w], rsems[w], (x, y, 1 - c)) for w in range(len(srcs))]


def _swap_start(grads, name):
    grads = list(grads)
    n = len(grads)
    lands = [lax.empty((NSH,) + g.shape[2:], g.dtype) for g in grads]

    def body(*refs):
        barrier = pltpu.get_barrier_semaphore()
        sibling = (lax.axis_index("x"), lax.axis_index("y"), 1 - lax.axis_index("c"))
        pl.semaphore_signal(barrier, inc=1, device_id=sibling, device_id_type=MESH)
        pl.semaphore_wait(barrier, 1)
        for cp in _swap_copies(refs[:n], refs[n:2 * n], refs[2 * n:3 * n], refs[3 * n:4 * n]):
            cp.start()
        refs[-1][...] = jnp.zeros_like(refs[-1])

    outs = pl.pallas_call(
        body, name=name,
        out_shape=([pltpu.SemaphoreType.DMA(())] * (2 * n) + [pltpu.HBM(a.shape, a.dtype) for a in grads + lands]
                   + [jax.ShapeDtypeStruct((8, 128), F32)]),
        in_specs=[_HBM] * (2 * n), out_specs=[_SEM] * (2 * n) + [_HBM] * (2 * n) + [_VM],
        input_output_aliases={i: 2 * n + i for i in range(2 * n)},
        compiler_params=pltpu.CompilerParams(has_side_effects=_EFFECT, collective_id=SIBLING_PAIR_ID),
    )(*_in_hbm(grads), *_in_hbm(lands))
    return outs[:2 * n], outs[2 * n:4 * n], outs[-1]


def _swap_wait(sems, thru, after, name):
    n = len(thru) // 2

    def body(*refs):
        for cp in _swap_copies(refs[:n], refs[n:2 * n], refs[2 * n:3 * n], refs[3 * n:4 * n]):
            cp.wait_send()
            cp.wait_recv()

    outs = pl.pallas_call(
        body, name=name, out_shape=[pltpu.HBM(a.shape, a.dtype) for a in thru],
        in_specs=[_HBM] * (2 * n) + [_SEM] * (2 * n) + [_ANY] * len(after), out_specs=[_HBM] * (2 * n),
        input_output_aliases={i: i for i in range(2 * n)},
        compiler_params=pltpu.CompilerParams(has_side_effects=_EFFECT),
    )(*thru, *sems, *after)
    return outs[:n], outs[n:]


def _share_copies(bufs, ssems, rsems, sending):
    x, y, c, _ = _place()
    out = []
    for w, ref in enumerate(bufs):
        slot = ref.at[c if sending else 1 - c]
        out.append(_remote(slot, slot, ssems[w], rsems[w], (x, y, 1 - c)))
    return out


def _share_start(bufs, after, name):
    bufs = list(bufs)
    n = len(bufs)

    def body(*refs):
        barrier = pltpu.get_barrier_semaphore()
        sibling = (lax.axis_index("x"), lax.axis_index("y"), 1 - lax.axis_index("c"))
        pl.semaphore_signal(barrier, inc=1, device_id=sibling, device_id_type=MESH)
        pl.semaphore_wait(barrier, 1)
        at = n + len(after)
        for cp in _share_copies(refs[:n], refs[at:at + n], refs[at + n:at + 2 * n], True):
            cp.start()
        refs[-1][...] = jnp.zeros_like(refs[-1])

    outs = pl.pallas_call(
        body, name=name,
        out_shape=([pltpu.SemaphoreType.DMA(())] * (2 * n) + [pltpu.HBM(a.shape, a.dtype) for a in bufs]
                   + [jax.ShapeDtypeStruct((8, 128), F32)]),
        in_specs=[_HBM] * n + [_ANY] * len(after), out_specs=[_SEM] * (2 * n) + [_HBM] * n + [_VM],
        input_output_aliases={i: 2 * n + i for i in range(n)},
        compiler_params=pltpu.CompilerParams(has_side_effects=_EFFECT, collective_id=SIBLING_PAIR_ID),
    )(*_in_hbm(bufs), *after)
    return outs[:2 * n], outs[2 * n:3 * n], outs[-1]


def _share_wait(sems, thru, after, name):
    n = len(thru)

    def body(*refs):
        for cp in _share_copies(refs[:n], refs[n:2 * n], refs[2 * n:3 * n], True):
            cp.wait_send()
        for cp in _share_copies(refs[:n], refs[n:2 * n], refs[2 * n:3 * n], False):
            cp.wait_recv()

    return pl.pallas_call(
        body, name=name, out_shape=[pltpu.HBM(a.shape, a.dtype) for a in thru],
        in_specs=[_HBM] * n + [_SEM] * (2 * n) + [_ANY] * len(after), out_specs=[_HBM] * n,
        input_output_aliases={i: i for i in range(n)},
        compiler_params=pltpu.CompilerParams(has_side_effects=_EFFECT),
    )(*thru, *sems, *after)


def _gather_copies(bufs, ssems, rsems, sending):
    x, y, c, chips = _place()
    out = []
    for w, ref in enumerate(bufs):
        half = ref.shape[1] // 2
        for k, (px, py) in enumerate(chips):
            rows = ref.at[2 * x + y if sending else 2 * px + py, pl.ds(c * half, half)]
            out.append(_remote(rows, rows, ssems[3 * w + k], rsems[3 * w + k], (px, py, c)))
    return out


def _gather_start(bufs, after, name):
    n, ncp = len(bufs), 3 * len(bufs)

    def body(*refs):
        ssems, rsems = refs[n + len(after):n + len(after) + ncp], refs[n + len(after) + ncp:n + len(after) + 2 * ncp]
        for cp in _gather_copies(refs[:n], ssems, rsems, True):
            cp.start()
        token = refs[-1]
        token[...] = jnp.zeros_like(token)

    outs = pl.pallas_call(
        body, name=name,
        out_shape=([pltpu.SemaphoreType.DMA(())] * (2 * ncp) + [pltpu.HBM(a.shape, a.dtype) for a in bufs]
                   + [jax.ShapeDtypeStruct((8, 128), F32)]),
        in_specs=[_HBM] * n + [_ANY] * len(after), out_specs=[_SEM] * (2 * ncp) + [_HBM] * n + [_VM],
        input_output_aliases={i: 2 * ncp + i for i in range(n)},
        compiler_params=pltpu.CompilerParams(has_side_effects=_EFFECT),
    )(*_in_hbm(bufs), *after)
    return outs[:2 * ncp], outs[2 * ncp:2 * ncp + n], outs[-1]


def _gather_wait(sems, thru, after, name):
    n = len(thru)
    ncp = 3 * n

    def body(*refs):
        ssems, rsems = refs[n:n + ncp], refs[n + ncp:n + 2 * ncp]
        for cp in _gather_copies(refs[:n], ssems, rsems, True):
            cp.wait_send()
        for cp in _gather_copies(refs[:n], ssems, rsems, False):
            cp.wait_recv()

    return pl.pallas_call(
        body, name=name, out_shape=[pltpu.HBM(a.shape, a.dtype) for a in thru],
        in_specs=[_HBM] * n + [_SEM] * (2 * ncp) + [_ANY] * len(after), out_specs=[_HBM] * n,
        input_output_aliases={i: i for i in range(n)},
        compiler_params=pltpu.CompilerParams(has_side_effects=_EFFECT),
    )(*thru, *sems, *after)


def _rms(x):
    r = lax.rsqrt(jnp.mean(x * x, axis=-1, keepdims=True) + EPS)
    return r, x * r


def _rms_bwd(dn, xr, r, gain):
    dng = dn * gain
    dx = r * (dng - xr * jnp.mean(dng * xr, axis=-1, keepdims=True))
    return dx, jnp.sum(dn * xr, axis=0, keepdims=True)


def _ffn_weight_groups(wgu_hbm, wgu_ref, wd_hbm, wd_ref):
    groups = []
    for c0, cn in FF_CHUNKS:
        groups += [_wide_columns(wgu_hbm, wgu_ref, c0, cn), _wide_columns(wgu_hbm, wgu_ref, DFF + c0, cn),
                   [(wd_hbm.at[pl.ds(c0, cn)], wd_ref.at[pl.ds(c0, cn)])]]
    return groups


def _ffn_fwd(x, gain, wgu, wd, name, exchange=None):
    tm = 256

    def body(x_ref, g_ref, wgu_hbm, wd_hbm, h_ref, n_ref, gu_ref, a_ref, wgu_ref, wd_ref):
        def compute(ready):
            x = x_ref[...]
            _, xr = _rms(x)
            n = (xr * g_ref[...]).astype(BF16)
            n_ref[...] = n
            acc = jnp.zeros((tm, D), F32)
            for i, (c0, cn) in enumerate(FF_CHUNKS):
                ready(3 * i)
                g = _nn(n, wgu_ref[:, c0:c0 + cn])
                ready(3 * i + 1)
                u = _nn(n, wgu_ref[:, DFF + c0:DFF + c0 + cn])
                gu_ref[:, c0:c0 + cn] = g.astype(BF16)
                gu_ref[:, DFF + c0:DFF + c0 + cn] = u.astype(BF16)
                half_act = (0.5 * (g * jax.nn.sigmoid(g) * u)).astype(BF16)
                a_ref[:, c0:c0 + cn] = half_act
                ready(3 * i + 2)
                acc = acc + _nn(half_act, wd_ref[c0:c0 + cn, :])
            h_ref[...] = x + acc

        _staged(_ffn_weight_groups(wgu_hbm, wgu_ref, wd_hbm, wd_ref), compute)

    return _call(
        body, (x, gain, wgu, wd), name=name, grid=(S // tm,),
        in_specs=[_rows(tm, D), _fixed((1, D)), _ANY, _ANY],
        out_specs=[_rows(tm, D), _rows(tm, D), _rows(tm, 4 * FFS), _rows(tm, DFF)],
        out_shape=[_sds((S, D), F32), _sds((S, D), BF16), _sds((S, 4 * FFS), BF16), _sds((S, DFF), BF16)],
        scratch_shapes=[_vmem_wide(wgu)] + _vmem_like(wd),
        compiler_params=_params(("arbitrary",), 56), exchange=exchange)


def _ffn_last(x, gain, wgu, wd, target, gf, name):
    tm = 256

    def body(x_ref, g_ref, wgu_hbm, wd_hbm, t_ref, gf_ref, dx_ref, dgu_ref, dg_ref, loss_ref, dgf_ref, n_ref,
             a_ref, dh_ref, wgu_ref, wd_ref):
        @pl.when(pl.program_id(0) == 0)
        def _():
            dg_ref[...] = jnp.zeros_like(dg_ref)
            dgf_ref[...] = jnp.zeros_like(dgf_ref)
            loss_ref[...] = jnp.zeros_like(loss_ref)

        def compute():
            x = x_ref[...]
            r0, xr = _rms(x)
            n = (xr * g_ref[...]).astype(BF16)
            n_ref[...] = n
            acc = jnp.zeros((tm, D), F32)
            kept = []
            for c0, cn in FF_CHUNKS:
                g = _nn(n, wgu_ref[:, c0:c0 + cn])
                u = _nn(n, wgu_ref[:, DFF + c0:DFF + c0 + cn])
                kept.append((g.astype(BF16), u.astype(BF16)))
                half_act = (0.5 * (g * jax.nn.sigmoid(g) * u)).astype(BF16)
                a_ref[:, c0:c0 + cn] = half_act
                acc = acc + _nn(half_act, wd_ref[c0:c0 + cn, :])
            h = x + acc
            gf = gf_ref[...]
            r, hr = _rms(h)
            err = hr * gf - t_ref[...]
            dh, dgain_f = _rms_bwd(err * (1.0 / D), hr, r, gf)
            dhb = dh.astype(BF16)
            dh_ref[...] = dhb
            dn = jnp.zeros((tm, D), F32)
            for (c0, cn), (gb, ub) in zip(FF_CHUNKS, kept):
                g, u = gb.astype(F32), ub.astype(F32)
                da = 0.5 * _nt(dhb, wd_ref[c0:c0 + cn, :])
                sg = jax.nn.sigmoid(g)
                dgb = (da * u * (sg * (1.0 + g * (1.0 - sg)))).astype(BF16)
                dub = (da * (g * sg)).astype(BF16)
                dgu_ref[:, c0:c0 + cn] = dgb
                dgu_ref[:, DFF + c0:DFF + c0 + cn] = dub
                dn = dn + _nt(dgb, wgu_ref[:, c0:c0 + cn]) + _nt(dub, wgu_ref[:, DFF + c0:DFF + c0 + cn])
            dx, dgain = _rms_bwd(dn, xr, r0, g_ref[...])
            dx_ref[...] = dh + dx
            dg_ref[...] += dgain
            dgf_ref[...] += dgain_f
            loss_ref[...] += jnp.full((1, 128), (0.5 / D) * jnp.sum(err * err), F32)

        _stage([(wgu_hbm, wgu_ref), (wd_hbm, wd_ref)])
        compute()

    return _call(
        body, (x, gain, wgu, wd, target, gf), name=name, grid=(S // tm,),
        in_specs=[_rows(tm, D), _fixed((1, D)), _ANY, _ANY, _rows(tm, D), _fixed((1, D))],
        out_specs=[_rows(tm, D), _rows(tm, 4 * FFS), _fixed((1, D)), _fixed((1, 128)), _fixed((1, D)),
                   _rows(tm, D), _rows(tm, DFF), _rows(tm, D)],
        out_shape=[_sds((S, D), F32), _sds((S, 4 * FFS), BF16), _sds((1, D), F32), _sds((1, 128), F32),
                   _sds((1, D), F32), _sds((S, D), BF16), _sds((S, DFF), BF16), _sds((S, D), BF16)],
        scratch_shapes=[_vmem_wide(wgu)] + _vmem_like(wd),
        compiler_params=_params(("arbitrary",), 58), free=(4, 5))


def _ffn_bwd_act(dh, gu, wd, name, exchange=None, after=()):
    tm = 512

    def body(dh_ref, gu_ref, wd_hbm, dgu_ref, wd_ref):
        def compute(ready):
            dhb = dh_ref[...].astype(BF16)
            for i, (c0, cn) in enumerate(FF_CHUNKS):
                g = gu_ref[:, c0:c0 + cn].astype(F32)
                u = gu_ref[:, DFF + c0:DFF + c0 + cn].astype(F32)
                ready(i)
                da = 0.5 * _nt(dhb, wd_ref[c0:c0 + cn, :])
                sg = jax.nn.sigmoid(g)
                dgu_ref[:, c0:c0 + cn] = (da * u * (sg * (1.0 + g * (1.0 - sg)))).astype(BF16)
                dgu_ref[:, DFF + c0:DFF + c0 + cn] = (da * (g * sg)).astype(BF16)

        _staged([[(wd_hbm.at[pl.ds(c0, cn)], wd_ref.at[pl.ds(c0, cn)])] for c0, cn in FF_CHUNKS], compute)

    res = _call(
        body, (dh, gu, wd), name=name, grid=(S // tm,),
        in_specs=[_rows(tm, D), _rows(tm, 4 * FFS), _ANY], out_specs=[_rows(tm, 4 * FFS)],
        out_shape=[_sds((S, 4 * FFS), BF16)], scratch_shapes=_vmem_like(wd),
        compiler_params=_params(("arbitrary",), 56), exchange=exchange, after=after)
    return res[0] if exchange is None else (res[0][0], res[1])


def _ffn_bwd_in(dh, x, gain, dgu, wgu, name, exchange=None, after=()):
    tm = 512

    def body(dh_ref, x_ref, g_ref, dgu_ref, wgu_hbm, dx_ref, dg_ref, wgu_ref):
        chunks = [(half + c0, cn) for half in (0, DFF) for c0, cn in FF_CHUNKS]

        @pl.when(pl.program_id(0) == 0)
        def _():
            dg_ref[...] = jnp.zeros_like(dg_ref)

        def compute(ready):
            dn = jnp.zeros((tm, D), F32)
            for k, (c0, cn) in enumerate(chunks):
                ready(k)
                dn = dn + _nt(dgu_ref[:, c0:c0 + cn], wgu_ref[:, c0:c0 + cn])
            r, xr = _rms(x_ref[...])
            dx, dgain = _rms_bwd(dn, xr, r, g_ref[...])
            dx_ref[...] = dh_ref[...] + dx
            dg_ref[...] += dgain

        _staged([_wide_columns(wgu_hbm, wgu_ref, c0, cn) for c0, cn in chunks], compute)

    return _call(
        body, (dh, x, gain, dgu, wgu), name=name, grid=(S // tm,),
        in_specs=[_rows(tm, D), _rows(tm, D), _fixed((1, D)), _rows(tm, 4 * FFS), _ANY],
        out_specs=[_rows(tm, D), _fixed((1, D))],
        out_shape=[_sds((S, D), F32), _sds((1, D), F32)],
        scratch_shapes=[_vmem_wide(wgu)],
        compiler_params=_params(("arbitrary",), 56), exchange=exchange, after=after)


def _mix_in(h, gain, w_in, after=()):
    tm = 512

    def body(h_ref, g_ref, w_hbm, u_ref, xp_ref, q_ref, k_ref, v_ref, gp_ref, gs_ref, w_ref):
        def compute(ready):
            _, hr = _rms(h_ref[...])
            u = (hr * g_ref[...]).astype(BF16)
            u_ref[...] = u
            ready(0)
            p0 = _nn(u, w_ref[0])
            xp_ref[...] = p0[:, :PW]
            q_ref[...] = p0[:, PW:].astype(BF16)
            ready(1)
            p1 = _nn(u, w_ref[1])
            k_ref[...] = p1[:, :SBW].astype(BF16)
            v_ref[...] = p1[:, SBW:].astype(BF16)
            ready(2)
            gp_ref[...] = jax.nn.sigmoid(_nn(u, w_ref[2])).astype(BF16)
            ready(3)
            gs_ref[...] = jax.nn.sigmoid(_nn(u, w_ref[3])).astype(BF16)

        _staged([[(w_hbm.at[j], w_ref.at[j])] for j in range(NSH)], compute)

    return _call(
        body, (h, gain, w_in), name="mix_in", grid=(S // tm,),
        in_specs=[_rows(tm, D), _fixed((1, D)), _ANY],
        out_specs=[_rows(tm, D), _rows(tm, PW), _rows(tm, SBW), _rows(tm, SBW), _rows(tm, SBW),
                   _rows(tm, D), _rows(tm, D)],
        out_shape=[_sds((S, D), BF16), _sds((S, PW), F32), _sds((S, SBW), BF16), _sds((S, SBW), BF16),
                   _sds((S, SBW), BF16), _sds((S, D), BF16), _sds((S, D), BF16)],
        scratch_shapes=_vmem_like(w_in),
        compiler_params=_params(("arbitrary",), 48), free=(1,), after=after)


def _hilo_dot(x, tri):
    hi = x.astype(BF16)
    lo = (x - hi.astype(F32)).astype(BF16)
    return _nn(hi, tri) + _nn(lo, tri)


def _log_terms(qk):
    z2 = qk * (SCALE * LOG2E)
    lb = jnp.minimum(z2, 0.0) - jnp.log2(1.0 + jnp.exp2(-jnp.abs(z2)))
    return lb, lb - z2


def _head_masks():
    lane = lax.broadcasted_iota(jnp.int32, (1, 2 * DH), 1)
    return (lane < DH, lane >= DH)


def _attn_fwd(q, k, v, exchange=None):
    T = TA

    def body(q_ref, k_ref, v_ref, o_ref, c_ref):
        i2 = 2 * pl.program_id(1)
        row = lax.broadcasted_iota(jnp.int32, (T, T), 0)
        col = lax.broadcasted_iota(jnp.int32, (T, T), 1)
        after = (row > col).astype(BF16)
        causal = col < row
        masks = _head_masks()
        qms = {}
        for b in range(QB):
            q2 = q_ref[b * T:(b + 1) * T, :]
            for h, hm in enumerate(masks):
                qms[b, h] = jnp.where(hm, q2, jnp.zeros_like(q2))

        def blocks(keys, pairs, carries, os):
            ks, vms = [], []
            for j in keys:
                rows = pl.ds(pl.multiple_of(j * T, T), T)
                vj = v_ref[rows, :]
                ks.append(k_ref[rows, :])
                vms.append([jnp.where(hm, vj, jnp.zeros_like(vj)) for hm in masks])
            units = [(n, h) for n in range(len(pairs)) for h in range(2)]
            qks = {(n, h): _nt(qms[pairs[n][0], h], ks[pairs[n][1]]) for n, h in units}
            lbs, l1ms = {}, {}
            for u in units:
                lbs[u], l1m = _log_terms(qks[u])
                l1ms[u] = jnp.where(causal, l1m, 0.0) if pairs[u[0]][2] else l1m
            cins = {u: _hilo_dot(l1ms[u], after) for u in units}
            carries, os = dict(carries), list(os)
            for n, h in units:
                b, key, diag = pairs[n]
                a = jnp.exp2(lbs[n, h] + cins[n, h] + carries[b, h])
                if diag:
                    a = jnp.where(causal, a, 0.0)
                os[b] = os[b] + _nn(a.astype(BF16), vms[key][h])
                carries[b, h] = carries[b, h] + jnp.sum(l1ms[n, h], axis=1, keepdims=True)
            return carries, tuple(os)

        carries = {(b, h): jnp.zeros((T, 1), F32) for b in range(QB) for h in range(2)}
        os = tuple(jnp.zeros((T, 2 * DH), F32) for _ in range(QB))
        carries, os = blocks([i2 + 1, i2], [(1, 0, True), (0, 1, True), (1, 1, False)], carries, os)
        carries, os = lax.fori_loop(
            0, i2 // 2,
            lambda t, c: blocks([i2 - 1 - 2 * t, i2 - 2 - 2 * t],
                                [(0, 0, False), (1, 0, False), (0, 1, False), (1, 1, False)], c[0], c[1]),
            (carries, os))
        for b in range(QB):
            o_ref[b * T:(b + 1) * T, :] = os[b].astype(BF16)
            c_ref[b * T:(b + 1) * T, :] = jnp.where(masks[0], carries[b, 0], carries[b, 1])

    blk = pl.BlockSpec((QB * T, 2 * DH), lambda p, i: (i, p))
    full = pl.BlockSpec((S, 2 * DH), lambda p, i: (0, p))
    return _call(
        body, (q, k, v), name="attn_fwd", grid=(SBW // (2 * DH), S // (QB * T)),
        in_specs=[blk, full, full], out_specs=[blk, blk],
        out_shape=[_sds((S, SBW), BF16), _sds((S, SBW), F32)],
        compiler_params=_params(("arbitrary", "arbitrary"), 40), exchange=exchange)


def _attn_bwd(q, k, v, do, ctot, after=()):
    T = TA
    nq = S // (QB * T)

    def body(q_ref, k_ref, v_ref, do_ref, c_ref, dq_ref, dk_ref, dv_ref, dk_acc, dv_acc):
        step = pl.program_id(1)
        i2 = 2 * step

        @pl.when(step == 0)
        def _():
            dk_acc[...] = jnp.zeros_like(dk_acc)
            dv_acc[...] = jnp.zeros_like(dv_acc)

        row = lax.broadcasted_iota(jnp.int32, (T, T), 0)
        col = lax.broadcasted_iota(jnp.int32, (T, T), 1)
        upto = (row <= col).astype(BF16)
        before = (row < col).astype(BF16)
        causal = col < row
        masks = _head_masks()
        qms, doms, ctots = {}, {}, {}
        for b in range(QB):
            q2, do2 = q_ref[b * T:(b + 1) * T, :], do_ref[b * T:(b + 1) * T, :]
            for h, hm in enumerate(masks):
                qms[b, h] = jnp.where(hm, q2, jnp.zeros_like(q2))
                doms[b, h] = jnp.where(hm, do2, jnp.zeros_like(do2))
                ctots[b, h] = c_ref[b * T:(b + 1) * T, h * DH:h * DH + 1]

        def blocks(keys, pairs, sums, dqs):
            rows = [pl.ds(pl.multiple_of(j * T, T), T) for j in keys]
            ks, vs = [k_ref[r, :] for r in rows], [v_ref[r, :] for r in rows]
            kms = [[jnp.where(hm, kj, jnp.zeros_like(kj)) for hm in masks] for kj in ks]
            units = [(n, h) for n in range(len(pairs)) for h in range(2)]
            qks = {(n, h): _nt(qms[pairs[n][0], h], ks[pairs[n][1]]) for n, h in units}
            das = {(n, h): _nt(doms[pairs[n][0], h], vs[pairs[n][1]]) for n, h in units}
            lbs, l1ms = {}, {}
            for u in units:
                lbs[u], l1m = _log_terms(qks[u])
                l1ms[u] = jnp.where(causal, l1m, 0.0) if pairs[u[0]][2] else l1m
            pins = {u: _hilo_dot(l1ms[u], upto) for u in units}
            sums = dict(sums)
            a_s, dls, cps = {}, {}, {}
            for n, h in units:
                b, _, diag = pairs[n]
                cl, cp = sums[b, h]
                a = jnp.exp2(lbs[n, h] + (ctots[b, h] - cl) - pins[n, h])
                if diag:
                    a = jnp.where(causal, a, 0.0)
                a_s[n, h] = a.astype(BF16)
                dls[n, h] = das[n, h] * a
                cps[n, h] = cp
                sums[b, h] = (cl + jnp.sum(l1ms[n, h], axis=1, keepdims=True),
                              cp + jnp.sum(dls[n, h], axis=1, keepdims=True))
            pexs = {u: _hilo_dot(dls[u], before) for u in units}
            dzbs = {}
            for u in units:
                dz = dls[u] - jnp.exp2(lbs[u]) * (dls[u] + pexs[u] + cps[u])
                if pairs[u[0]][2]:
                    dz = jnp.where(causal, dz, 0.0)
                dzbs[u] = dz.astype(BF16)
            dqs = list(dqs)
            for n, h in units:
                dqs[pairs[n][0]] = dqs[pairs[n][0]] + _nn(dzbs[n, h], kms[pairs[n][1]][h])
            for key, r in enumerate(rows):
                mine = [(n, h) for n, h in units if pairs[n][1] == key]
                dk_acc[r, :] += functools.reduce(jnp.add, [_tn(dzbs[u], qms[pairs[u[0]][0], u[1]]) for u in mine])
                dv_acc[r, :] += functools.reduce(jnp.add, [_tn(a_s[u], doms[pairs[u[0]][0], u[1]]) for u in mine])
            return sums, tuple(dqs)

        zero = jnp.zeros((T, 1), F32)
        sums = {(b, h): (zero, zero) for b in range(QB) for h in range(2)}
        dqs = tuple(jnp.zeros((T, 2 * DH), F32) for _ in range(QB))
        sums, dqs = lax.fori_loop(
            0, i2 // 2,
            lambda t, c: blocks([2 * t, 2 * t + 1],
                                [(0, 0, False), (1, 0, False), (0, 1, False), (1, 1, False)], c[0], c[1]),
            (sums, dqs))
        _, dqs = blocks([i2, i2 + 1], [(0, 0, True), (1, 0, False), (1, 1, True)], sums, dqs)
        for b in range(QB):
            dq_ref[b * T:(b + 1) * T, :] = (dqs[b] * SCALE).astype(BF16)

        @pl.when(step == nq - 1)
        def _():
            dk_ref[...] = (dk_acc[...] * SCALE).astype(BF16)
            dv_ref[...] = dv_acc[...].astype(BF16)

    blk = pl.BlockSpec((QB * T, 2 * DH), lambda p, i: (i, p))
    full = pl.BlockSpec((S, 2 * DH), lambda p, i: (0, p))
    return _call(
        body, (q, k, v, do, ctot), name="attn_bwd", grid=(SBW // (2 * DH), nq),
        in_specs=[blk, full, full, blk, blk], out_specs=[blk, full, full],
        out_shape=[_sds((S, SBW), BF16), _sds((S, SBW), BF16), _sds((S, SBW), BF16)],
        scratch_shapes=[pltpu.VMEM((S, 2 * DH), F32), pltpu.VMEM((S, 2 * DH), F32)],
        compiler_params=_params(("arbitrary", "arbitrary"), 40), after=after)


def _pool_counts(first_row, tm):
    pos = first_row + lax.broadcasted_iota(jnp.int32, (tm, 1), 0)
    return [jnp.minimum(pos + 1, w).astype(F32) for w in POOL_WINDOWS]


def _mix_out(h, xp, o_sb, gp, gs, w_group, scale, w_bp, w_ba, w_out, exchange=None):
    tm = 512

    def body(h_hbm, xp_ref, o_ref, gp_ref, gs_ref, wg_hbm, sc_ref, wbp_hbm, wba_hbm, wo_hbm,
             h2_ref, pm_ref, p_ref, yp_ref, ys_ref, m_ref, halo, wg_ref, wbp_ref, wba_ref, wo_ref, ring, ring_sem):
        i = pl.program_id(0)

        def fetch(s):
            rows = pl.ds(pl.multiple_of(s * tm, tm), tm)
            return pltpu.make_async_copy(h_hbm.at[rows], ring.at[s % 3], ring_sem.at[s % 3])

        @pl.when(i == 0)
        def _():
            fetch(i).start()
            fetch(i + 1).start()
            halo[...] = jnp.zeros_like(halo)

        @pl.when(i + 2 < S // tm)
        def _():
            fetch(i + 2).start()

        _stage([(wg_hbm, wg_ref), (wbp_hbm, wbp_ref), (wba_hbm, wba_ref), (wo_hbm, wo_ref)])

        xp = xp_ref[...]
        ext = jnp.concatenate([halo[...], xp], axis=0)
        halo[...] = xp[tm - HALO:, :]
        counts = _pool_counts(i * tm, tm)
        for gi in range(len(POOL_WINDOWS)):
            lanes = slice(gi * PG, (gi + 1) * PG)
            win = ext[:, lanes]
            for step in range(gi + 1):
                win = win + pltpu.roll(win, 1 << step, 0)
            pm = (win[HALO:, :] / counts[gi] - xp[:, lanes]).astype(BF16)
            pm_ref[:, lanes] = pm
            p_ref[:, lanes] = (_nn(pm, wg_ref[gi]) * sc_ref[:, lanes]).astype(BF16)
        pb = p_ref[...]
        ob = o_ref[...]
        for j in range(NSH):
            cols = slice(j * (D // NSH), (j + 1) * (D // NSH))
            yp = _nn(pb, wbp_ref[j])
            ys = _nn(ob, wba_ref[j])
            yp_ref[:, cols] = yp.astype(BF16)
            ys_ref[:, cols] = ys.astype(BF16)
            m_ref[:, cols] = (gp_ref[:, cols].astype(F32) * yp + gs_ref[:, cols].astype(F32) * ys).astype(BF16)
        fetch(i).wait()
        h2_ref[...] = ring[i % 3] + _nn(m_ref[...], wo_ref[...])

    return _call(
        body, (h, xp, o_sb, gp, gs, w_group, scale, w_bp, w_ba, w_out), name="mix_out", grid=(S // tm,),
        in_specs=[_ANY, _rows(tm, PW), _rows(tm, SBW), _rows(tm, D), _rows(tm, D),
                  _ANY, _fixed((1, PW)), _ANY, _ANY, _ANY],
        out_specs=[_rows(tm, D), _rows(tm, PW), _rows(tm, PW), _rows(tm, D), _rows(tm, D), _rows(tm, D)],
        out_shape=[_sds((S, D), F32), _sds((S, PW), BF16), _sds((S, PW), BF16), _sds((S, D), BF16),
                   _sds((S, D), BF16), _sds((S, D), BF16)],
        scratch_shapes=[pltpu.VMEM((HALO, PW), F32)] + _vmem_like(w_group, w_bp, w_ba, w_out)
        + [pltpu.VMEM((3, tm, D), F32), pltpu.SemaphoreType.DMA((3,))],
        compiler_params=_params(("arbitrary",), 48), free=(5, 6), exchange=exchange)


def _mix_bwd_out(dh, gp, gs, yp, ys, pm, w_group, scale, w_bp, w_ba, w_out, exchange=None):
    tm = 512
    nt = S // tm

    def body(dh_hbm, gp_ref, gs_ref, yp_ref, ys_ref, pm_ref, wg_hbm, sc_ref, wbp_hbm, wba_hbm, wo_hbm,
             dlg_ref, dyp_ref, dys_ref, do_ref, dyg_ref, dxp_ref, dsc_ref, halo, wg_ref, wbp_ref, wba_ref, wo_ref,
             ring, ring_sem):
        step = pl.program_id(0)

        def fetch(s):
            rows = pl.ds(pl.multiple_of((nt - 1 - s) * tm, tm), tm)
            return pltpu.make_async_copy(dh_hbm.at[rows], ring.at[s % 3], ring_sem.at[s % 3])

        @pl.when(step == 0)
        def _():
            fetch(step).start()
            fetch(step + 1).start()
            halo[...] = jnp.zeros_like(halo)
            dsc_ref[...] = jnp.zeros_like(dsc_ref)

        @pl.when(step + 2 < nt)
        def _():
            fetch(step + 2).start()

        _stage([(wg_hbm, wg_ref), (wbp_hbm, wbp_ref), (wba_hbm, wba_ref), (wo_hbm, wo_ref)])
        fetch(step).wait()
        dm = _nt(ring[step % 3].astype(BF16), wo_ref[...])
        gp = gp_ref[...].astype(F32)
        gs = gs_ref[...].astype(F32)
        yp = yp_ref[...].astype(F32)
        ys = ys_ref[...].astype(F32)
        dlg_ref[:, :D] = (dm * yp * gp * (1.0 - gp)).astype(BF16)
        dlg_ref[:, D:] = (dm * ys * gs * (1.0 - gs)).astype(BF16)
        dyp_ref[...] = (dm * gp).astype(BF16)
        dys_ref[...] = (dm * gs).astype(BF16)
        dp = jnp.zeros((tm, PW), F32)
        do = jnp.zeros((tm, SBW), F32)
        for j in range(NSH):
            cols = slice(j * (D // NSH), (j + 1) * (D // NSH))
            dp = dp + _nt(dyp_ref[:, cols], wbp_ref[j])
            do = do + _nt(dys_ref[:, cols], wba_ref[j])
        do_ref[...] = do.astype(BF16)
        counts = _pool_counts((nt - 1 - step) * tm, tm)
        dscale = []
        for gi in range(len(POOL_WINDOWS)):
            lanes = slice(gi * PG, (gi + 1) * PG)
            dpg = dp[:, lanes]
            dscale.append(jnp.sum(dpg * _nn(pm_ref[:, lanes], wg_ref[gi]), axis=0, keepdims=True))
            dyg = (dpg * sc_ref[:, lanes]).astype(BF16)
            dyg_ref[:, lanes] = dyg
            dpm = _nt(dyg, wg_ref[gi])
            per = dpm / counts[gi]
            win = jnp.concatenate([per, halo[:, lanes]], axis=0)
            halo[:, lanes] = per[:HALO, :]
            for s in range(gi + 1):
                win = win + pltpu.roll(win, tm + HALO - (1 << s), 0)
            dxp_ref[:, lanes] = (win[:tm, :] - dpm).astype(BF16)
        dsc_ref[...] += jnp.concatenate(dscale, axis=1)

    rev = lambda width: pl.BlockSpec((tm, width), lambda i: (nt - 1 - i, 0))
    return _call(
        body, (dh, gp, gs, yp, ys, pm, w_group, scale, w_bp, w_ba, w_out), name="mix_bwd_out", grid=(nt,),
        in_specs=[_ANY, rev(D), rev(D), rev(D), rev(D), rev(PW), _ANY, _fixed((1, PW)), _ANY, _ANY, _ANY],
        out_specs=[rev(2 * D), rev(D), rev(D), rev(SBW), rev(PW), rev(PW), _fixed((1, PW))],
        out_shape=[_sds((S, 2 * D), BF16), _sds((S, D), BF16), _sds((S, D), BF16), _sds((S, SBW), BF16),
                   _sds((S, PW), BF16), _sds((S, PW), BF16), _sds((1, PW), F32)],
        scratch_shapes=[pltpu.VMEM((HALO, PW), F32)] + _vmem_like(w_group, w_bp, w_ba, w_out)
        + [pltpu.VMEM((3, tm, D), F32), pltpu.SemaphoreType.DMA((3,))],
        compiler_params=_params(("arbitrary",), 48), exchange=exchange)


def _mix_bwd_in(dh, h, gain, pieces, w_in, exchange=None):
    tm = 512
    widths = [p.shape[1] for p in pieces]

    def body(dh_ref, h_ref, g_ref, *rest):
        piece_refs, (w_hbm, dx_ref, dg_ref, w_ref, dp_ref) = rest[:len(pieces)], rest[len(pieces):]
        @pl.when(pl.program_id(0) == 0)
        def _():
            dg_ref[...] = jnp.zeros_like(dg_ref)

        def compute(ready):
            at = 0
            for ref, width in zip(piece_refs, widths):
                dp_ref[:, at:at + width] = ref[...]
                at += width
            du = jnp.zeros((tm, D), F32)
            for j in range(NSH):
                ready(j)
                du = du + _nt(dp_ref[:, j * D:(j + 1) * D], w_ref[j])
            r, hr = _rms(h_ref[...])
            dx, dgain = _rms_bwd(du, hr, r, g_ref[...])
            dx_ref[...] = dh_ref[...] + dx
            dg_ref[...] += dgain

        _staged([[(w_hbm.at[j], w_ref.at[j])] for j in range(NSH)], compute)

    return _call(
        body, (dh, h, gain, *pieces, w_in), name="mix_bwd_in", grid=(S // tm,),
        in_specs=[_rows(tm, D), _rows(tm, D), _fixed((1, D))] + [_rows(tm, w) for w in widths] + [_ANY],
        out_specs=[_rows(tm, D), _fixed((1, D))],
        out_shape=[_sds((S, D), F32), _sds((1, D), F32)],
        scratch_shapes=_vmem_like(w_in) + [pltpu.VMEM((tm, 4 * D), BF16)],
        compiler_params=_params(("arbitrary",), 48), exchange=exchange)


def _wgrad_in(u, pieces):
    dxp, dq, dk, dv, dlg = pieces

    def body(u_ref, dxp_ref, dq_ref, dk_ref, dv_ref, dlg_ref, o_ref):
        j = pl.program_id(0)
        u = u_ref[...]

        def two(left_ref, right_ref):
            o_ref[:, :PW] = _tn(u, left_ref[...]).astype(BF16)
            o_ref[:, PW:] = _tn(u, right_ref[...]).astype(BF16)

        pl.when(j == 0)(lambda: two(dxp_ref, dq_ref))
        pl.when(j == 1)(lambda: two(dk_ref, dv_ref))

        @pl.when(j >= 2)
        def _():
            o_ref[...] = _tn(u, dlg_ref[...]).astype(BF16)

    whole = lambda width: pl.BlockSpec((S, width), lambda j: (0, 0))
    return _call(
        body, (u, dxp, dq, dk, dv, dlg), name="wgrad_in", grid=(NSH,),
        in_specs=[whole(D), whole(PW), whole(SBW), whole(SBW), whole(SBW),
                  pl.BlockSpec((S, D), lambda j: (0, jnp.maximum(j - 2, 0)))],
        out_specs=[pl.BlockSpec((None, D, D), lambda j: (j, 0, 0))], out_shape=[_sds((NSH, D, D), BF16)],
        compiler_params=_params(("arbitrary",), 56))[0]


def _wgrad(a, b, nblk, ti, name, out_dtype=BF16, exchange=None, after=()):
    ka, n = a.shape[1], b.shape[1]
    ns = n // nblk

    def body(a_ref, b_ref, o_ref):
        o_ref[...] = _tn(a_ref[...].astype(BF16), b_ref[...].astype(BF16)).astype(out_dtype)

    res = _call(
        body, (a, b), name=name, grid=(nblk, ka // ti),
        in_specs=[pl.BlockSpec((S, ti), lambda j, i: (0, i)), pl.BlockSpec((S, ns), lambda j, i: (0, j))],
        out_specs=[pl.BlockSpec((None, ti, ns), lambda j, i: (j, i, 0))],
        out_shape=[_sds((nblk, ka, ns), out_dtype)],
        compiler_params=_params(("arbitrary", "arbitrary"), 56), exchange=exchange, after=after)
    return res[0] if exchange is None else (res[0][0], res[1])


def _wgrad_ffn(n, dgu, a, dh, name):
    ti, n_gu = D // 2, 2 * NSH

    def body(n_ref, dgu_ref, a_ref, dh_ref, ggu_ref, gd_ref):
        s = pl.program_id(0)

        @pl.when(s < n_gu)
        def _():
            ggu_ref[...] = _tn(n_ref[...], dgu_ref[...]).astype(BF16)

        @pl.when(s >= n_gu)
        def _():
            gd_ref[...] = _tn(a_ref[...], dh_ref[...]).astype(BF16)

    rows = lambda s: jnp.where(s < n_gu, s % 2, 1)
    block = lambda s: jnp.minimum(s // 2, NSH - 1)
    down = lambda s: jnp.maximum(s - n_gu, 0)
    return _call(
        body, (n, dgu, a, dh), name=name, grid=(n_gu + 2,),
        in_specs=[pl.BlockSpec((S, ti), lambda s: (0, rows(s))), pl.BlockSpec((S, FFS), lambda s: (0, block(s))),
                  pl.BlockSpec((S, FFS), lambda s: (0, down(s))), pl.BlockSpec((S, D), lambda s: (0, 0))],
        out_specs=[pl.BlockSpec((None, ti, FFS), lambda s: (block(s), rows(s), 0)),
                   pl.BlockSpec((FFS, D), lambda s: (down(s), 0))],
        out_shape=[_sds((NSH, D, FFS), BF16), _sds((DFF, D), BF16)],
        compiler_params=_params(("arbitrary",), 56))


def _wgrad_branches(p, dyp, o_sb, dys, pm, dyg, mm, dh):
    cols = D // NSH

    def body(p_ref, dyp_ref, o_ref, dys_ref, pm_ref, dyg_ref, mm_ref, dh_ref, gbp_ref, gba_ref, gg_ref, go_ref):
        gbp_ref[...] = _tn(p_ref[...], dyp_ref[...]).astype(BF16)
        gba_ref[...] = _tn(o_ref[...], dys_ref[...]).astype(BF16)
        gg_ref[...] = _tn(pm_ref[...], dyg_ref[...])
        go_ref[...] = _tn(mm_ref[...], dh_ref[...].astype(BF16)).astype(BF16)

    whole = lambda width: pl.BlockSpec((S, width), lambda j: (0, 0))
    col = lambda width: pl.BlockSpec((S, width), lambda j: (0, j))
    return _call(
        body, (p, dyp, o_sb, dys, pm, dyg, mm, dh), name="wgrad_branches", grid=(NSH,),
        in_specs=[whole(PW), col(cols), whole(SBW), col(cols), col(PG), col(PG), whole(D), col(cols)],
        out_specs=[pl.BlockSpec((None, PW, cols), lambda j: (j, 0, 0)),
                   pl.BlockSpec((None, SBW, cols), lambda j: (j, 0, 0)),
                   pl.BlockSpec((None, PG, PG), lambda j: (j, 0, 0)),
                   pl.BlockSpec((D, cols), lambda j: (0, j))],
        out_shape=[_sds((NSH, PW, cols), BF16), _sds((NSH, SBW, cols), BF16), _sds((NSH, PG, PG), F32),
                   _sds((D, D), BF16)],
        compiler_params=_params(("arbitrary",), 40))


def _place():
    x, y, c = lax.axis_index("x"), lax.axis_index("y"), lax.axis_index("c")
    chips = [(1 - x, y), (x, 1 - y), (1 - x, 1 - y)]
    return x, y, c, chips


def _remote(src, dst, ssem, rsem, dev):
    return pltpu.make_async_remote_copy(src_ref=src, dst_ref=dst, send_sem=ssem, recv_sem=rsem,
                                        device_id=dev, device_id_type=MESH)


def _cast_into_block(ws, me_idx, name):
    steps = 4
    shapes = [(w.shape[0] // steps, w.shape[1]) for w in ws]

    def body(me_ref, *refs):
        for w_ref, o_ref in zip(refs[:len(ws)], refs[len(ws):]):
            o_ref[...] = w_ref[...].astype(BF16)

    return pl.pallas_call(
        body, name=name, out_shape=[_sds((NSH,) + w.shape, BF16) for w in ws],
        grid_spec=pltpu.PrefetchScalarGridSpec(
            num_scalar_prefetch=1, grid=(steps,),
            in_specs=[pl.BlockSpec((r, c), lambda s, me: (s, 0)) for r, c in shapes],
            out_specs=[pl.BlockSpec((None, r, c), lambda s, me: (me[0], s, 0)) for r, c in shapes]),
        compiler_params=_params(("arbitrary",), 32),
    )(me_idx, *ws)


def _ex_gather(bufs):
    n = len(bufs)
    per = 8

    def plan(outs, ssem, rsem, w):
        x, y, c, _ = _place()
        sib, nbr_x, nbr_y = (x, y, 1 - c), (1 - x, y, c), (x, 1 - y, c)
        half = outs[w].shape[1] // 2
        quarter = half // 2
        sem = lambda k: (ssem.at[per * w + k], rsem.at[per * w + k])
        rows = lambda blk, start, size: outs[w].at[blk, pl.ds(start, size)]
        mine = rows(2 * x + y, c * half, half)
        from_x = rows(2 * (1 - x) + y, c * half, half)
        from_y = rows(2 * x + (1 - y), c * half, half)
        diag = 2 * (1 - x) + (1 - y)
        pass_y = rows(2 * (1 - x) + y, c * half, quarter)
        pass_x = rows(2 * x + (1 - y), c * half + quarter, quarter)
        diag_0, diag_1 = rows(diag, c * half, quarter), rows(diag, c * half + quarter, quarter)
        first = [_remote(mine, mine, *sem(0), nbr_x), _remote(mine, mine, *sem(1), nbr_y)]
        arrivals = [
            (_remote(from_x, from_x, *sem(0), nbr_x),
             [_remote(pass_y, pass_y, *sem(2), nbr_y), _remote(from_x, from_x, *sem(4), sib)]),
            (_remote(from_y, from_y, *sem(1), nbr_y),
             [_remote(pass_x, pass_x, *sem(3), nbr_x), _remote(from_y, from_y, *sem(5), sib)]),
            (_remote(diag_0, diag_0, *sem(2), nbr_y), [_remote(diag_0, diag_0, *sem(6), sib)]),
            (_remote(diag_1, diag_1, *sem(3), nbr_x), [_remote(diag_1, diag_1, *sem(7), sib)]),
        ]
        other = (1 - c) * half
        from_sibling = [
            _remote(rows(2 * (1 - x) + y, other, half), rows(2 * (1 - x) + y, other, half), *sem(4), sib),
            _remote(rows(2 * x + (1 - y), other, half), rows(2 * x + (1 - y), other, half), *sem(5), sib),
            _remote(rows(diag, other, quarter), rows(diag, other, quarter), *sem(6), sib),
            _remote(rows(diag, other + quarter, quarter), rows(diag, other + quarter, quarter), *sem(7), sib),
        ]
        return first, arrivals, from_sibling

    def start(ins, outs, ssem, rsem):
        x, y, c, _ = _place()
        for w in range(n):
            half = outs[w].shape[1] // 2
            mine = outs[w].at[2 * x + y, pl.ds(c * half, half)]
            _remote(mine, mine, ssem.at[per * w], rsem.at[per * w], (1 - x, y, c)).start()
            _remote(mine, mine, ssem.at[per * w + 1], rsem.at[per * w + 1], (x, 1 - y, c)).start()

    def finish(ins, outs, ssem, rsem):
        plans = [plan(outs, ssem, rsem, w) for w in range(n)]
        started = []
        for direct in (True, False):
            for first, arrivals, _ in plans:
                for arrived, onward in (arrivals[:2] if direct else arrivals[2:]):
                    arrived.wait_recv()
                    for cp in onward:
                        cp.start()
                    started += onward
        for first, _, from_sibling in plans:
            for cp in from_sibling:
                cp.wait_recv()
            started += first
        for cp in started:
            cp.wait_send()

    return Exchange(bufs, [_sds(b.shape, b.dtype) for b in bufs], {w: w for w in range(n)}, per * n, start, finish)


def _ex_gather_direct(bufs):
    n = len(bufs)

    def copies(outs, ssem, rsem, only_first=False):
        x, y, c, chips = _place()
        me, sib = 2 * x + y, (x, y, 1 - c)
        first, relay, last = [], [], []
        for w in range(n):
            half = outs[w].shape[1] // 2
            mine = outs[w].at[me, pl.ds(c * half, half)]
            for k, (px, py) in enumerate(chips):
                sems = (ssem.at[6 * w + k], rsem.at[6 * w + k])
                sib_sems = (ssem.at[6 * w + 3 + k], rsem.at[6 * w + 3 + k])
                first.append(_remote(mine, mine, *sems, (px, py, c)))
                if only_first:
                    continue
                got = outs[w].at[2 * px + py, pl.ds(c * half, half)]
                relay.append((_remote(got, got, *sems, (px, py, c)), _remote(got, got, *sib_sems, sib)))
                theirs = outs[w].at[2 * px + py, pl.ds((1 - c) * half, half)]
                last.append(_remote(theirs, theirs, *sib_sems, sib))
        return first, relay, last

    def start(ins, outs, ssem, rsem):
        for cp in copies(outs, ssem, rsem, only_first=True)[0]:
            cp.start()

    def finish(ins, outs, ssem, rsem):
        first, relay, last = copies(outs, ssem, rsem)
        for arrived, onward in relay:
            arrived.wait_recv()
            onward.start()
        for cp in last:
            cp.wait_recv()
        for cp in first:
            cp.wait_send()
        for _, onward in relay:
            onward.wait_send()

    return Exchange(bufs, [_sds(b.shape, b.dtype) for b in bufs], {w: w for w in range(n)}, 6 * n, start, finish)


def _simple_exchange(arrays, landing, aliases, make_copies, sibling_only=False):
    def start(ins, outs, ssem, rsem):
        for cp, _ in make_copies(ins, outs, ssem, rsem, False):
            cp.start()

    def finish(ins, outs, ssem, rsem):
        cps = make_copies(ins, outs, ssem, rsem, True)
        for _, landed in cps:
            landed.wait_recv()
        for cp, _ in cps:
            cp.wait_send()

    return Exchange(arrays, landing, aliases, len(arrays) * 3, start, finish, sibling_only)


def _ex_pair_swap(grads):
    def make(ins, outs, ssem, rsem, landing):
        x, y, c, _ = _place()
        cps = [_remote(ins[w].at[:, 1 - c], outs[w], ssem.at[w], rsem.at[w], (x, y, 1 - c))
               for w in range(len(grads))]
        return [(cp, cp) for cp in cps]

    return _simple_exchange(grads, [_sds((NSH,) + g.shape[2:], g.dtype) for g in grads], {}, make, True)


def _ex_relay(bufs):
    def make(ins, outs, ssem, rsem, landing):
        x, y, c, chips = _place()
        sib = (x, y, 1 - c)
        out = []
        for w in range(len(bufs)):
            half = outs[w].shape[1] // 2
            for k, (px, py) in enumerate(chips):
                sems = (ssem.at[3 * w + k], rsem.at[3 * w + k])
                have = outs[w].at[2 * px + py, pl.ds(c * half, half)]
                miss = outs[w].at[2 * px + py, pl.ds((1 - c) * half, half)]
                out.append((_remote(have, have, *sems, sib), _remote(miss, miss, *sems, sib) if landing else None))
        return out

    return _simple_exchange(bufs, [_sds(b.shape, b.dtype) for b in bufs], {w: w for w in range(len(bufs))}, make, True)


def _ex_share(bufs):
    def make(ins, outs, ssem, rsem, landing):
        x, y, c, _ = _place()
        sib = (x, y, 1 - c)
        return [(_remote(outs[w].at[c], outs[w].at[c], ssem.at[w], rsem.at[w], sib),
                 _remote(outs[w].at[1 - c], outs[w].at[1 - c], ssem.at[w], rsem.at[w], sib) if landing else None)
                for w in range(len(bufs))]

    return _simple_exchange(bufs, [_sds(b.shape, b.dtype) for b in bufs], {w: w for w in range(len(bufs))}, make, True)


def _small_copies(slots, ssems, rsems, sending):
    x, y, c, _ = _place()
    out = []
    for m in range(1, 8):
        px, py, pc = x ^ (m >> 2), y ^ ((m >> 1) & 1), c ^ (m & 1)
        slot = slots.at[4 * x + 2 * y + c if sending else 4 * px + 2 * py + pc]
        out.append(_remote(slot, slot, ssems[m - 1], rsems[m - 1], (px, py, pc)))
    return out


def _small_gather_start(slots, name, after=()):
    at = 1 + len(after)

    def body(*refs):
        for cp in _small_copies(refs[0], refs[at:at + 7], refs[at + 7:at + 14], True):
            cp.start()
        refs[-1][...] = jnp.zeros_like(refs[-1])

    outs = pl.pallas_call(
        body, name=name,
        out_shape=([pltpu.SemaphoreType.DMA(())] * 14 + [pltpu.HBM(slots.shape, slots.dtype)]
                   + [jax.ShapeDtypeStruct((8, 128), F32)]),
        in_specs=[_HBM] + [_ANY] * len(after), out_specs=[_SEM] * 14 + [_HBM, _VM], input_output_aliases={0: 14},
        compiler_params=pltpu.CompilerParams(has_side_effects=_EFFECT),
    )(*_in_hbm([slots]), *after)
    return outs[:14], outs[14], outs[15]


def _small_gather_wait(sems, slots, after, name):
    def body(*refs):
        for cp in _small_copies(refs[0], refs[1:8], refs[8:15], True):
            cp.wait_send()
        for cp in _small_copies(refs[0], refs[1:8], refs[8:15], False):
            cp.wait_recv()

    return pl.pallas_call(
        body, name=name, out_shape=pltpu.HBM(slots.shape, slots.dtype),
        in_specs=[_HBM] + [_SEM] * 14 + [_ANY] * len(after), out_specs=_HBM, input_output_aliases={0: 0},
        compiler_params=pltpu.CompilerParams(has_side_effects=_EFFECT),
    )(slots, *sems, *after)


def _pair_sum(grads, gots, c_idx, name):
    n = len(grads)

    def body(c_ref, *refs):
        for a_ref, b_ref, o_ref in zip(refs[:n], refs[n:2 * n], refs[2 * n:]):
            o_ref[...] = (a_ref[...].astype(F32) + b_ref[...].astype(F32)).astype(BF16)

    halves = [g.shape[2:] for g in grads]
    return list(pl.pallas_call(
        body, name=name, out_shape=[_sds((NSH,) + h, BF16) for h in halves],
        grid_spec=pltpu.PrefetchScalarGridSpec(
            num_scalar_prefetch=1, grid=(NSH,),
            in_specs=[pl.BlockSpec((None, None) + h, lambda j, c: (j, c[0], 0, 0)) for h in halves]
            + [pl.BlockSpec((None,) + h, lambda j, c: (j, 0, 0)) for h in halves],
            out_specs=[pl.BlockSpec((None,) + h, lambda j, c: (j, 0, 0)) for h in halves]),
        compiler_params=_params(("arbitrary",), 40),
    )(c_idx, *_in_hbm(list(grads) + list(gots))))


def _chip_sum(owns, gots, place, name):
    n = len(owns)

    def body(place_ref, *refs):
        for own_ref, got_ref, o_ref in zip(refs[:n], refs[n:2 * n], refs[2 * n:]):
            acc = own_ref[...].astype(F32)
            for k in range(3):
                acc = acc + got_ref[k].astype(F32)
            o_ref[...] = acc

    shapes = [(o.shape[1] // 2, o.shape[2]) for o in owns]
    return list(pl.pallas_call(
        body, name=name, out_shape=[_sds((2, 2 * r, c), F32) for r, c in shapes],
        grid_spec=pltpu.PrefetchScalarGridSpec(
            num_scalar_prefetch=1, grid=(2,),
            in_specs=[pl.BlockSpec((None, r, c), lambda s, p: (p[0], s, 0)) for r, c in shapes]
            + [pl.BlockSpec((3, r, c), lambda s, p: (0, s, 0)) for r, c in shapes],
            out_specs=[pl.BlockSpec((None, r, c), lambda s, p: (p[1], s, 0)) for r, c in shapes]),
        compiler_params=_params(("arbitrary",), 40),
    )(place, *_in_hbm(list(owns) + list(gots))))


def _adamw_math(w, g, m, v):
    m = B1 * m + (1.0 - B1) * g
    v = B2 * v + (1.0 - B2) * (g * g)
    m_hat = m / (1.0 - B1 ** STEP)
    v_hat = v / (1.0 - B2 ** STEP)
    return -LR * (m_hat / (jnp.sqrt(v_hat) + AEPS) + WD * w), m, v


def _adamw(ws, gs, ms, vs, name, after=()):
    n, steps = len(ws), 4

    def body(*refs):
        ins, outs = refs[:4 * n], refs[4 * n:]
        for i in range(n):
            w_ref, g_ref, m_ref, v_ref = ins[4 * i:4 * i + 4]
            go_ref, d_ref, nm_ref, nv_ref = outs[4 * i:4 * i + 4]
            g = g_ref[...]
            go_ref[...] = g
            d_ref[...], nm_ref[...], nv_ref[...] = _adamw_math(w_ref[...], g, m_ref[...], v_ref[...])

    args, specs, shapes, free = [], [], [], []
    for i, (w, g, m, v) in enumerate(zip(ws, gs, ms, vs)):
        args += [w, g, m, v]
        specs += [pl.BlockSpec((w.shape[0] // steps, w.shape[1]), lambda r: (r, 0))] * 4
        shapes += [_sds(w.shape, F32)] * 4
        free += [4 * i, 4 * i + 2, 4 * i + 3]
    outs = _call(body, args, name=name, grid=(steps,), out_shape=shapes, in_specs=specs, out_specs=specs,
                 compiler_params=_params(("arbitrary",), 48), free=tuple(free), after=after)
    return [outs[4 * i:4 * i + 4] for i in range(n)]


def _small_update(gathered, w, m, v, entries):
    rows = w.shape[0]

    def body(ga_ref, w_ref, m_ref, v_ref, *out_refs):
        for j, (first, n) in enumerate(entries):
            mine = slice(first, first + n)
            g = ga_ref[mine, :]
            for dev in range(1, 8):
                g = g + ga_ref[dev * rows + first:dev * rows + first + n, :]
            results = (g,) + _adamw_math(w_ref[mine, :], g, m_ref[mine, :], v_ref[mine, :])
            for i, res in enumerate(results):
                out_refs[i * len(entries) + j][...] = res

    outs = pl.pallas_call(
        body, name="small_update",
        out_shape=[jax.ShapeDtypeStruct((n, 128), F32) for _ in range(4) for _, n in entries],
        in_specs=[_VM] * 4, out_specs=[_VM] * (4 * len(entries)),
    )(gathered, w, m, v)
    return [outs[i * len(entries):(i + 1) * len(entries)] for i in range(4)]


SMALL = ("ffn1_norm", "mix_norm", "ffn2_norm", "final_norm", "pool_scale", "loss", "pool_w_group")
BIG = ("ffn1_w_gate_up", "ffn1_w_down", "w_in", "w_branch_pool", "w_branch_attn", "w_out",
       "ffn2_w_gate_up", "ffn2_w_down")
ORDER = ("ffn1_norm", "ffn1_w_gate_up", "ffn1_w_down", "mix_norm", "w_in", "pool_w_group", "pool_scale",
         "w_branch_pool", "w_branch_attn", "w_out", "ffn2_norm", "ffn2_w_gate_up", "ffn2_w_down", "final_norm")
SMALL_ROWS = 560


def _pack_small(t):
    parts = []
    for k in SMALL:
        rows = t[k].reshape(-1, 128) if k in t else jnp.zeros((1, 128), F32)
        parts.append(jnp.pad(rows, ((0, -rows.shape[0] % 8), (0, 0))))
    packed = jnp.concatenate(parts, axis=0)
    assert packed.shape == (SMALL_ROWS, 128), packed.shape
    return packed


def _small_entries(like):
    out, at = [], 0
    for k in SMALL:
        n = like[k].size // 128 if k in like else 1
        out.append((at, n))
        at += n + (-n % 8)
    return out


def _halves(g):
    return g.reshape(NSH, 2, g.shape[1] // 2, g.shape[2])


def kernel(x, ffn1_norm, ffn1_w_gate_up, ffn1_w_down, mix_norm, w_in, pool_w_group, pool_scale, w_branch_pool, w_branch_attn, w_out, ffn2_norm, ffn2_w_gate_up, ffn2_w_down, final_norm, loss_target, m_ffn1_norm, m_ffn1_w_gate_up, m_ffn1_w_down, m_mix_norm, m_w_in, m_pool_w_group, m_pool_scale, m_w_branch_pool, m_w_branch_attn, m_w_out, m_ffn2_norm, m_ffn2_w_gate_up, m_ffn2_w_down, m_final_norm, v_ffn1_norm, v_ffn1_w_gate_up, v_ffn1_w_down, v_mix_norm, v_w_in, v_pool_w_group, v_pool_scale, v_w_branch_pool, v_w_branch_attn, v_w_out, v_ffn2_norm, v_ffn2_w_gate_up, v_ffn2_w_down, v_final_norm):
    wts = dict(ffn1_norm=ffn1_norm, ffn1_w_gate_up=ffn1_w_gate_up, ffn1_w_down=ffn1_w_down, mix_norm=mix_norm,
               w_in=w_in, pool_w_group=pool_w_group, pool_scale=pool_scale, w_branch_pool=w_branch_pool,
               w_branch_attn=w_branch_attn, w_out=w_out, ffn2_norm=ffn2_norm, ffn2_w_gate_up=ffn2_w_gate_up,
               ffn2_w_down=ffn2_w_down, final_norm=final_norm)
    mom = dict(ffn1_norm=m_ffn1_norm, ffn1_w_gate_up=m_ffn1_w_gate_up, ffn1_w_down=m_ffn1_w_down,
               mix_norm=m_mix_norm, w_in=m_w_in, pool_w_group=m_pool_w_group, pool_scale=m_pool_scale,
               w_branch_pool=m_w_branch_pool, w_branch_attn=m_w_branch_attn, w_out=m_w_out,
               ffn2_norm=m_ffn2_norm, ffn2_w_gate_up=m_ffn2_w_gate_up, ffn2_w_down=m_ffn2_w_down,
               final_norm=m_final_norm)
    var = dict(ffn1_norm=v_ffn1_norm, ffn1_w_gate_up=v_ffn1_w_gate_up, ffn1_w_down=v_ffn1_w_down,
               mix_norm=v_mix_norm, w_in=v_w_in, pool_w_group=v_pool_w_group, pool_scale=v_pool_scale,
               w_branch_pool=v_w_branch_pool, w_branch_attn=v_w_branch_attn, w_out=v_w_out,
               ffn2_norm=v_ffn2_norm, ffn2_w_gate_up=v_ffn2_w_gate_up, ffn2_w_down=v_ffn2_w_down,
               final_norm=v_final_norm)

    c_idx = lax.axis_index("c").astype(jnp.int32).reshape(1)
    me_idx = (2 * lax.axis_index("x") + lax.axis_index("y")).astype(jnp.int32).reshape(1)
    place = jnp.concatenate([me_idx, c_idx])
    x0, tgt = x[0], loss_target[0]
    wgrp = pool_w_group[0].astype(BF16)
    g1, gm, g2, gf = ffn1_norm, mix_norm, ffn2_norm, final_norm.reshape(1, D)
    grad, delta, new_m, new_v = {}, {}, {}, {}

    def pair_sums(keys, parts, got):
        return _pair_sum(parts, got, c_idx, "pair_sum_" + keys[0])

    def chip_sums(keys, chip_parts, owned):
        return _chip_sum(chip_parts, owned, place, "chip_sum_" + keys[0])

    def adamw(keys, after=()):
        outs = _adamw([wts[k][0] for k in keys], [grad[k][0] for k in keys], [mom[k][0] for k in keys],
                      [var[k][0] for k in keys], "adamw_" + keys[0], after=after)
        for k, res in zip(keys, outs):
            grad[k], delta[k], new_m[k], new_v[k] = (o.reshape(wts[k].shape) for o in res)

    first, late = ("ffn1_w_gate_up", "ffn1_w_down"), ("w_branch_pool", "w_branch_attn", "w_out",
                                                       "ffn2_w_gate_up", "ffn2_w_down")
    own = {}
    for group in (first, ("w_in",), late):
        own.update(zip(group, _cast_into_block([wts[k][0] for k in group], me_idx, "cast_" + group[0])))
    full = dict(zip(first, _exchange_alone(_ex_gather([own[k] for k in first]), "gather_ffn1")))
    wgu1, wd1 = full["ffn1_w_gate_up"], full["ffn1_w_down"].reshape(DFF, D)
    (h1, n1, gu1, a1), (win,) = _ffn_fwd(x0, g1, wgu1, wd1, "ffn1_fwd", exchange=_ex_gather_direct([own["w_in"]]))
    sems_l, thru_l, token_l = _gather_start([own[k_] for k_ in late], [h1], "gather_late_start")
    u, xp, q, k, v, gp, gs = _mix_in(h1, gm, win, after=(token_l,))
    o_sb, ctot = _attn_fwd(q, k, v)
    arrived = _gather_wait(sems_l, thru_l, [o_sb], "gather_late_wait")
    wbp, wba, wout = _exchange_alone(_ex_relay(arrived[:3]), "relay_mix")
    wout = wout.reshape(D, D)
    (h2, pm, p, yp, ys, mm), (wgu2, wd2) = _mix_out(h1, xp, o_sb, gp, gs, wgrp, pool_scale, wbp, wba, wout,
                                                    exchange=_ex_relay(arrived[3:]))
    wd2 = wd2.reshape(DFF, D)
    dh2, dgu3, d_g2, loss_row, d_gf, n3, a3, dh3 = _ffn_last(h2, g2, wgu2, wd2, tgt, gf, "ffn2")

    def grad_down(a, dh, name, exchange=None):
        res = _wgrad(a, dh, 1, FFS, name, exchange=exchange)
        halves = lambda g: [_halves(g.reshape(NSH, DFF // NSH, D))]
        return halves(res) if exchange is None else (halves(res[0]), res[1])

    k_gu2, k_d2, k_gu1, k_d1, k_in = (("ffn2_w_gate_up",), ("ffn2_w_down",), ("ffn1_w_gate_up",),
                                      ("ffn1_w_down",), ("w_in",))
    g_gu2, g_d2 = _wgrad_ffn(n3, dgu3, a3, dh3, "wgrad_ffn2")
    pa = [_halves(g_gu2), _halves(g_d2.reshape(NSH, DFF // NSH, D))]
    (dlg, dyp, dys, do_sb, dyg, dxp, d_scale), got_a = _mix_bwd_out(
        dh2, gp, gs, yp, ys, pm, wgrp, pool_scale, wbp, wba, wout, exchange=_ex_pair_swap(pa))
    chip_a = pair_sums(k_gu2 + k_d2, pa, got_a)
    kb = ("w_out", "w_branch_pool", "w_branch_attn")
    g_bp, g_ba, d_group, g_out = _wgrad_branches(p, dyp, o_sb, dys, pm, dyg, mm, dh2)
    pb = [_halves(g_out.reshape(NSH, D // NSH, D)), _halves(g_bp), _halves(g_ba)]
    k_a, k_in = k_gu2 + k_d2, k_in + kb
    sems_a, thru_a, token_a = _scatter_start(chip_a, "scatter_a_start")
    dq, dk, dv = _attn_bwd(q, k, v, do_sb, ctot, after=(token_a,))
    chip_a, owned_a = _scatter_wait(sems_a, thru_a, [dq], "scatter_a_wait")
    halves_a = chip_sums(k_a, chip_a, owned_a)
    dproj = (dxp, dq, dk, dv, dlg)
    (dh1, d_gm), both_a = _mix_bwd_in(dh2, h1, gm, dproj, win, exchange=_ex_share(halves_a))
    for i, k_ in enumerate(k_a):
        grad[k_] = both_a[i].reshape(wts[k_].shape)

    p_in = [_halves(_wgrad_in(u, dproj))] + pb
    p_d1, got_in = grad_down(a1, dh1, "wgrad_d1", exchange=_ex_pair_swap(p_in))
    sems_in, thru_in, token_in = _scatter_start(pair_sums(k_in, p_in, got_in), "scatter_in_start")
    dgu1, got_d1 = _ffn_bwd_act(dh1, gu1, wd1, "ffn1_bwd_act", exchange=_ex_pair_swap(p_d1), after=(token_in,))
    sems_d1, thru_d1, token_d1 = _scatter_start(pair_sums(k_d1, p_d1, got_d1), "scatter_d1_start")
    p_gu1 = [_halves(_wgrad(n1, dgu1, NSH, D, "wgrad_gu1", after=(token_in, token_d1)))]
    sems_w, thru_w, token_w = _swap_start(p_gu1, "swap_gu1_start")
    chip_in, owned_in = _scatter_wait(sems_in, thru_in, [token_w], "scatter_in_wait")
    chip_d1, owned_d1 = _scatter_wait(sems_d1, thru_d1, [token_w], "scatter_d1_wait")
    halves_in = chip_sums(k_in, chip_in, owned_in)
    p_gu1, got_gu1 = _swap_wait(sems_w, thru_w, halves_in, "swap_gu1_wait")
    sems, thru, token = _scatter_start(pair_sums(k_gu1, p_gu1, got_gu1), "scatter_gu1_start")
    sems_h, thru_h, token_h = _share_start(halves_in, [token], "share_in_start")
    adamw(k_a, after=(token_h,))
    landed = _share_wait(sems_h, thru_h, [delta[k_a[0]]], "share_in_wait")
    for i, k_ in enumerate(k_in):
        grad[k_] = landed[i].reshape(wts[k_].shape)
    adamw(k_in)
    dx, d_g1 = _ffn_bwd_in(dh1, x0, g1, dgu1, wgu1, "ffn1_bwd_in", after=(token,))
    small_g = dict(ffn1_norm=d_g1, mix_norm=d_gm, ffn2_norm=d_g2, final_norm=d_gf, pool_scale=d_scale,
                   pool_w_group=d_group, loss=loss_row)
    dev = 4 * lax.axis_index("x") + 2 * lax.axis_index("y") + lax.axis_index("c")
    slots = lax.dynamic_update_slice(jnp.zeros((8, SMALL_ROWS, 128), F32), _pack_small(small_g)[None], (dev, 0, 0))
    chip_gu1, owned_gu1 = _scatter_wait(sems, thru, [dx] + [delta[k_] for k_ in k_a + k_in], "scatter_gu1_wait")
    halves_last = chip_sums(k_d1 + k_gu1, chip_d1 + chip_gu1, owned_d1 + owned_gu1)
    sems_l, thru_l, token_l = _share_start(halves_last, [], "share_last_start")
    sems_s, slots, token_s = _small_gather_start(slots, "small_gather_start", after=(token_l,))
    both = _share_wait(sems_l, thru_l, [token_s], "share_last_wait")
    grad["ffn1_w_down"] = both[0].reshape(ffn1_w_down.shape)
    grad["ffn1_w_gate_up"] = both[1].reshape(ffn1_w_gate_up.shape)
    adamw(k_d1 + k_gu1, after=(token_s,))
    gathered = _small_gather_wait(sems_s, slots, [delta[k_] for k_ in k_d1 + k_gu1], "small_gather_wait")
    gathered = gathered.reshape(8 * SMALL_ROWS, 128)
    results = _small_update(gathered, _pack_small(wts), _pack_small(mom), _pack_small(var), _small_entries(wts))
    for dst, entries in zip((grad, delta, new_m, new_v), results):
        for k_, rows in zip(SMALL, entries):
            if k_ in wts:
                dst[k_] = rows.reshape(wts[k_].shape)
            elif dst is grad:
                loss = rows[0, 0]
    return (loss, dx[None], *[grad[k_] for k_ in ORDER], *[delta[k_] for k_ in ORDER],
            *[new_m[k_] for k_ in ORDER], *[new_v[k_] for k_ in ORDER])
```
